```python
import jax, jax.numpy as jnp
from jax import lax
import numpy as np

D_MODEL = 2048
BATCH = 8
SEQ = 8192
DEPTH = 1

CTX_LEN = 256
GRID_W = 64
D_MIX = D_MODEL
W_CONV = D_MIX // 2
W_LRU = D_MIX - W_CONV
N_CONV_GROUPS = 16
N_LRU_HEADS = 16
LRU_HEAD_DIM = W_LRU // N_LRU_HEADS
CONV_A_WIDTH = 3
CONV_A_LEFT = 1
CONV_B_WIDTH = 4
CONV_B_LEFT = 2
LRU_C = 8.0
N_DIR = 2
D_IN_PROJ = 4 * W_CONV + 2 * W_LRU
EPS = 1e-6

kernel_name = "hybrid_conv_rglru_parallel_heads_dit"


def _rmsnorm(x, g):
    xf = x.astype(jnp.float32)
    y = xf * lax.rsqrt(jnp.mean(xf * xf, axis=-1, keepdims=True) + EPS)
    return (y * g.astype(jnp.float32)).astype(x.dtype)


def _dwconv(x, w, axis, left):
    k = w.shape[0]
    n = x.shape[axis]
    pad = [(0, 0)] * x.ndim
    pad[axis] = (left, k - 1 - left)
    xp = jnp.pad(x, pad)
    out = lax.slice_in_dim(xp, 0, n, axis=axis) * w[0]
    for j in range(1, k):
        out = out + lax.slice_in_dim(xp, j, j + n, axis=axis) * w[j]
    return out


def _conv_latent(x, w, left):
    b, l, ch = x.shape
    rows = l // GRID_W
    return _dwconv(x.reshape(b, rows, GRID_W, ch), w, 2, left).reshape(b, l, ch)


def _conv_context(x, w, left):
    return _dwconv(x, w, 1, left)


def _blockdiag(x, w, b):
    bsz, l, _ = x.shape
    y = jnp.einsum('blhi,hij->blhj', x.reshape(bsz, l, N_LRU_HEADS, LRU_HEAD_DIM), w)
    return y.reshape(bsz, l, W_LRU) + b


def _lru_coeffs(xb, wa, ba, wx, bx, lam):
    xf = xb.astype(jnp.float32)
    r = jax.nn.sigmoid(_blockdiag(xf, wa.astype(jnp.float32), ba.astype(jnp.float32)))
    i = jax.nn.sigmoid(_blockdiag(xf, wx.astype(jnp.float32), bx.astype(jnp.float32)))
    log_a = -LRU_C * r * jax.nn.softplus(-lam.astype(jnp.float32))
    a = jnp.exp(log_a)
    bterm = jnp.sqrt(-jnp.expm1(2.0 * log_a)) * (i * xf)
    return a, bterm


def _combine(e1, e2):
    a1, b1 = e1
    a2, b2 = e2
    return a1 * a2, a2 * b1 + b2


def _linear_scan(a, b, h0, reverse):
    if h0 is not None:
        idx = -1 if reverse else 0
        b = b.at[:, idx].add(a[:, idx] * h0)
    _, h = lax.associative_scan(_combine, (a, b), reverse=reverse, axis=1)
    return h


def _split_proj(p):
    cuts = [W_CONV, 2 * W_CONV, 3 * W_CONV, 4 * W_CONV, 4 * W_CONV + W_LRU]
    return jnp.split(p, cuts, axis=-1)


def _fwd_setup_inputs(seed: int = 0) -> dict:
    key = jax.random.key(seed)
    ks = jax.random.split(key, 20)
    f = jnp.float32
    nrm = lambda k, s, sc: jax.random.normal(k, s, f) * sc
    a0 = jax.random.uniform(ks[17], (DEPTH, N_DIR, W_LRU), f, 0.9, 0.999)
    s = a0 ** (1.0 / LRU_C)
    lru_lambda = jnp.log(s) - jnp.log1p(-s)
    return {
        "x": nrm(ks[0], (BATCH, SEQ, D_MODEL), 1.0),
        "c": nrm(ks[1], (BATCH, D_MODEL), 1.0),
        "ctx": nrm(ks[2], (BATCH, CTX_LEN, D_MODEL), 1.0),
        "c_ctx": nrm(ks[3], (D_MODEL,), 1.0),
        "norm_g": 1.0 + nrm(ks[4], (DEPTH, D_MODEL), 0.02),
        "w_ada": nrm(ks[5], (DEPTH, D_MODEL, 3 * D_MODEL), 0.5 * D_MODEL ** -0.5),
        "b_ada": nrm(ks[6], (DEPTH, 3 * D_MODEL), 0.02),
        "w_in": nrm(ks[7], (DEPTH, D_MODEL, D_IN_PROJ), D_MODEL ** -0.5),
        "w_conv_a": nrm(ks[8], (DEPTH, CONV_A_WIDTH, W_CONV), CONV_A_WIDTH ** -0.5),
        "w_conv_b": nrm(ks[9], (DEPTH, CONV_B_WIDTH, W_LRU), CONV_B_WIDTH ** -0.5),
        "b_conv_b": nrm(ks[10], (DEPTH, W_LRU), 0.02),
        "lru_wa": nrm(ks[11], (DEPTH, N_DIR, N_LRU_HEADS, LRU_HEAD_DIM, LRU_HEAD_DIM), LRU_HEAD_DIM ** -0.5),
        "lru_ba": nrm(ks[12], (DEPTH, N_DIR, W_LRU), 0.02),
        "lru_wx": nrm(ks[13], (DEPTH, N_DIR, N_LRU_HEADS, LRU_HEAD_DIM, LRU_HEAD_DIM), LRU_HEAD_DIM ** -0.5),
        "lru_bx": nrm(ks[14], (DEPTH, N_DIR, W_LRU), 0.02),
        "lru_lambda": lru_lambda,
        "w_out": nrm(ks[15], (DEPTH, D_MIX, D_MODEL), D_MIX ** -0.5),
        "final_g": 1.0 + nrm(ks[16], (D_MODEL,), 0.02),
    }


def _fwd_reference(x, c, ctx, c_ctx, norm_g, w_ada, b_ada, w_in, w_conv_a, w_conv_b, b_conv_b,
              lru_wa, lru_ba, lru_wx, lru_bx, lru_lambda, w_out, final_g):
    h_lat = x
    h_ctx = ctx
    for l in range(DEPTH):
        last = l == DEPTH - 1
        mod_lat = jax.nn.silu(c) @ w_ada[l] + b_ada[l]
        sh_l, sc_l, gt_l = jnp.split(mod_lat, 3, axis=-1)
        mod_ctx = jax.nn.silu(c_ctx) @ w_ada[l] + b_ada[l]
        sh_c, sc_c, gt_c = jnp.split(mod_ctx, 3, axis=-1)

        hl = _rmsnorm(h_lat, norm_g[l]) * (1.0 + sc_l[:, None]) + sh_l[:, None]
        hc = _rmsnorm(h_ctx, norm_g[l]) * (1.0 + sc_c) + sh_c

        bl, cl, ul, gl, vl, ql = _split_proj(hl @ w_in[l])
        bc, cc_, uc, gc, vc, qc = _split_proj(hc @ w_in[l])

        ya = bl * _conv_latent(cl * ul, w_conv_a[l], CONV_A_LEFT) * jax.nn.silu(gl)

        xbl = _conv_latent(vl, w_conv_b[l], CONV_B_LEFT) + b_conv_b[l]
        xbc = _conv_context(vc, w_conv_b[l], CONV_B_LEFT) + b_conv_b[l]
        y_lru = None
        ctx_states = []
        for d, rev in enumerate((False, True)):
            a_c, b_c = _lru_coeffs(xbc, lru_wa[l, d], lru_ba[l, d], lru_wx[l, d], lru_bx[l, d], lru_lambda[l, d])
            hs_c = _linear_scan(a_c, b_c, None, rev)
            h0 = hs_c[:, 0] if rev else hs_c[:, -1]
            a_l, b_l = _lru_coeffs(xbl, lru_wa[l, d], lru_ba[l, d], lru_wx[l, d], lru_bx[l, d], lru_lambda[l, d])
            hs_l = _linear_scan(a_l, b_l, h0, rev)
            y_lru = hs_l if y_lru is None else y_lru + hs_l
            ctx_states.append(hs_c)
        yb = y_lru.astype(h_lat.dtype) * jax.nn.silu(ql)

        out_lat = jnp.concatenate([ya, yb], axis=-1) @ w_out[l]
        new_lat = h_lat + gt_l[:, None] * out_lat

        if not last:
            ya_c = bc * _conv_context(cc_ * uc, w_conv_a[l], CONV_A_LEFT) * jax.nn.silu(gc)
            yb_c = (ctx_states[0] + ctx_states[1]).astype(h_ctx.dtype) * jax.nn.silu(qc)
            out_ctx = jnp.concatenate([ya_c, yb_c], axis=-1) @ w_out[l]
            h_ctx = h_ctx + gt_c * out_ctx
        h_lat = new_lat
    return _rmsnorm(h_lat, final_g)


import jax as _jax
import jax.numpy as _jnp

TWIN_FORMAT = 'train_step'
FWD_PARAMS = ['x', 'c', 'ctx', 'c_ctx', 'norm_g', 'w_ada', 'b_ada', 'w_in', 'w_conv_a', 'w_conv_b', 'b_conv_b', 'lru_wa', 'lru_ba', 'lru_wx', 'lru_bx', 'lru_lambda', 'w_out', 'final_g']
TWIN_WEIGHTS = ['c_ctx', 'norm_g', 'w_ada', 'b_ada', 'w_in', 'w_conv_a', 'w_conv_b', 'b_conv_b', 'lru_wa', 'lru_ba', 'lru_wx', 'lru_bx', 'lru_lambda', 'w_out', 'final_g']
TWIN_DIFF_INPUT = 'x'
TWIN_INPUTS = ['x', 'c', 'ctx', 'c_ctx', 'norm_g', 'w_ada', 'b_ada', 'w_in', 'w_conv_a', 'w_conv_b', 'b_conv_b', 'lru_wa', 'lru_ba', 'lru_wx', 'lru_bx', 'lru_lambda', 'w_out', 'final_g', 'loss_target', 'm_c_ctx', 'm_norm_g', 'm_w_ada', 'm_b_ada', 'm_w_in', 'm_w_conv_a', 'm_w_conv_b', 'm_b_conv_b', 'm_lru_wa', 'm_lru_ba', 'm_lru_wx', 'm_lru_bx', 'm_lru_lambda', 'm_w_out', 'm_final_g', 'v_c_ctx', 'v_norm_g', 'v_w_ada', 'v_b_ada', 'v_w_in', 'v_w_conv_a', 'v_w_conv_b', 'v_b_conv_b', 'v_lru_wa', 'v_lru_ba', 'v_lru_wx', 'v_lru_bx', 'v_lru_lambda', 'v_w_out', 'v_final_g']
TWIN_OUTPUTS = ['loss', 'grad_x', 'grad_c_ctx', 'grad_norm_g', 'grad_w_ada', 'grad_b_ada', 'grad_w_in', 'grad_w_conv_a', 'grad_w_conv_b', 'grad_b_conv_b', 'grad_lru_wa', 'grad_lru_ba', 'grad_lru_wx', 'grad_lru_bx', 'grad_lru_lambda', 'grad_w_out', 'grad_final_g', 'delta_c_ctx', 'delta_norm_g', 'delta_w_ada', 'delta_b_ada', 'delta_w_in', 'delta_w_conv_a', 'delta_w_conv_b', 'delta_b_conv_b', 'delta_lru_wa', 'delta_lru_ba', 'delta_lru_wx', 'delta_lru_bx', 'delta_lru_lambda', 'delta_w_out', 'delta_final_g', 'new_m_c_ctx', 'new_m_norm_g', 'new_m_w_ada', 'new_m_b_ada', 'new_m_w_in', 'new_m_w_conv_a', 'new_m_w_conv_b', 'new_m_b_conv_b', 'new_m_lru_wa', 'new_m_lru_ba', 'new_m_lru_wx', 'new_m_lru_bx', 'new_m_lru_lambda', 'new_m_w_out', 'new_m_final_g', 'new_v_c_ctx', 'new_v_norm_g', 'new_v_w_ada', 'new_v_b_ada', 'new_v_w_in', 'new_v_w_conv_a', 'new_v_w_conv_b', 'new_v_b_conv_b', 'new_v_lru_wa', 'new_v_lru_ba', 'new_v_lru_wx', 'new_v_lru_bx', 'new_v_lru_lambda', 'new_v_w_out', 'new_v_final_g']
TWIN_LEAF_KINDS = {'loss': 'loss', 'grad_x': 'grad_x', 'grad_c_ctx': 'grad_w', 'grad_norm_g': 'grad_w', 'grad_w_ada': 'grad_w', 'grad_b_ada': 'grad_w', 'grad_w_in': 'grad_w', 'grad_w_conv_a': 'grad_w', 'grad_w_conv_b': 'grad_w', 'grad_b_conv_b': 'grad_w', 'grad_lru_wa': 'grad_w', 'grad_lru_ba': 'grad_w', 'grad_lru_wx': 'grad_w', 'grad_lru_bx': 'grad_w', 'grad_lru_lambda': 'grad_w', 'grad_w_out': 'grad_w', 'grad_final_g': 'grad_w', 'delta_c_ctx': 'delta_w', 'delta_norm_g': 'delta_w', 'delta_w_ada': 'delta_w', 'delta_b_ada': 'delta_w', 'delta_w_in': 'delta_w', 'delta_w_conv_a': 'delta_w', 'delta_w_conv_b': 'delta_w', 'delta_b_conv_b': 'delta_w', 'delta_lru_wa': 'delta_w', 'delta_lru_ba': 'delta_w', 'delta_lru_wx': 'delta_w', 'delta_lru_bx': 'delta_w', 'delta_lru_lambda': 'delta_w', 'delta_w_out': 'delta_w', 'delta_final_g': 'delta_w', 'new_m_c_ctx': 'new_m', 'new_m_norm_g': 'new_m', 'new_m_w_ada': 'new_m', 'new_m_b_ada': 'new_m', 'new_m_w_in': 'new_m', 'new_m_w_conv_a': 'new_m', 'new_m_w_conv_b': 'new_m', 'new_m_b_conv_b': 'new_m', 'new_m_lru_wa': 'new_m', 'new_m_lru_ba': 'new_m', 'new_m_lru_wx': 'new_m', 'new_m_lru_bx': 'new_m', 'new_m_lru_lambda': 'new_m', 'new_m_w_out': 'new_m', 'new_m_final_g': 'new_m', 'new_v_c_ctx': 'new_v', 'new_v_norm_g': 'new_v', 'new_v_w_ada': 'new_v', 'new_v_b_ada': 'new_v', 'new_v_w_in': 'new_v', 'new_v_w_conv_a': 'new_v', 'new_v_w_conv_b': 'new_v', 'new_v_b_conv_b': 'new_v', 'new_v_lru_wa': 'new_v', 'new_v_lru_ba': 'new_v', 'new_v_lru_wx': 'new_v', 'new_v_lru_bx': 'new_v', 'new_v_lru_lambda': 'new_v', 'new_v_w_out': 'new_v', 'new_v_final_g': 'new_v'}


def _forward(args):
    return _fwd_reference(*[args[k] for k in FWD_PARAMS])


def _output_shape():
    def fwd():
        inp = _fwd_setup_inputs(0)
        return _fwd_reference(*[inp[k] for k in FWD_PARAMS])
    out = _jax.eval_shape(fwd)
    return out.shape, out.dtype

N_MICROBATCH = 1
ADAM_LR = 0.001
ADAM_B1 = 0.9
ADAM_B2 = 0.999
ADAM_EPS = 1e-08
ADAM_WD = 0.01
ADAM_STEP = 10
PER_EXAMPLE_BATCH_AXIS = {'x': 0, 'c': 0, 'ctx': 0, 'loss_target': 0}
SHARED_INPUTS = []
_WEIGHT_DTYPES = {'c_ctx': _jnp.float32, 'norm_g': _jnp.float32, 'w_ada': _jnp.float32, 'b_ada': _jnp.float32, 'w_in': _jnp.float32, 'w_conv_a': _jnp.float32, 'w_conv_b': _jnp.float32, 'b_conv_b': _jnp.float32, 'lru_wa': _jnp.float32, 'lru_ba': _jnp.float32, 'lru_wx': _jnp.float32, 'lru_bx': _jnp.float32, 'lru_lambda': _jnp.float32, 'w_out': _jnp.float32, 'final_g': _jnp.float32}
MOMENT_SCALE = {'c_ctx': 1.901592e-02, 'norm_g': 7.264563e-02, 'w_ada': 1.363484e-01, 'b_ada': 2.423018e-01, 'w_in': 5.804251e-02, 'w_conv_a': 2.972041e-02, 'w_conv_b': 8.278230e-02, 'b_conv_b': 3.114226e-01, 'lru_wa': 8.027061e-03, 'lru_ba': 8.819899e-03, 'lru_wx': 1.565864e-02, 'lru_bx': 1.625554e-02, 'lru_lambda': 2.321093e-02, 'w_out': 6.917672e-02, 'final_g': 3.216029e+01}


def _to_microbatches(a, axis):
    t = _jnp.moveaxis(a, axis, 0)
    t = t.reshape((N_MICROBATCH, t.shape[0] // N_MICROBATCH) + t.shape[1:])
    return _jnp.moveaxis(t, 1, axis + 1)


def setup_inputs(seed: int = 0) -> dict:
    inp = _fwd_setup_inputs(seed)
    key = _jax.random.fold_in(_jax.random.key(seed), 7919)
    shape, _ = _output_shape()
    out = dict(inp)
    out["loss_target"] = _jax.random.normal(_jax.random.fold_in(key, 0), shape, _jnp.float32)
    for i, name in enumerate(TWIN_WEIGHTS):
        w = inp[name].astype(_jnp.float32)
        if MOMENT_SCALE is None:
            s = _jnp.sqrt(_jnp.mean(_jnp.square(w)) + 1e-30)
        else:
            s = MOMENT_SCALE[name]
        km, kv = _jax.random.split(_jax.random.fold_in(key, i + 1))
        out[name] = w
        out["m_" + name] = s * _jax.random.normal(km, w.shape, _jnp.float32)
        out["v_" + name] = (s * s) * _jax.random.uniform(kv, w.shape, _jnp.float32, 0.5, 1.5)
    if N_MICROBATCH > 1:
        for name, axis in PER_EXAMPLE_BATCH_AXIS.items():
            out[name] = _to_microbatches(out[name], axis)
    return {'x': out['x'], 'c': out['c'], 'ctx': out['ctx'], 'c_ctx': out['c_ctx'], 'norm_g': out['norm_g'], 'w_ada': out['w_ada'], 'b_ada': out['b_ada'], 'w_in': out['w_in'], 'w_conv_a': out['w_conv_a'], 'w_conv_b': out['w_conv_b'], 'b_conv_b': out['b_conv_b'], 'lru_wa': out['lru_wa'], 'lru_ba': out['lru_ba'], 'lru_wx': out['lru_wx'], 'lru_bx': out['lru_bx'], 'lru_lambda': out['lru_lambda'], 'w_out': out['w_out'], 'final_g': out['final_g'], 'loss_target': out['loss_target'], 'm_c_ctx': out['m_c_ctx'], 'm_norm_g': out['m_norm_g'], 'm_w_ada': out['m_w_ada'], 'm_b_ada': out['m_b_ada'], 'm_w_in': out['m_w_in'], 'm_w_conv_a': out['m_w_conv_a'], 'm_w_conv_b': out['m_w_conv_b'], 'm_b_conv_b': out['m_b_conv_b'], 'm_lru_wa': out['m_lru_wa'], 'm_lru_ba': out['m_lru_ba'], 'm_lru_wx': out['m_lru_wx'], 'm_lru_bx': out['m_lru_bx'], 'm_lru_lambda': out['m_lru_lambda'], 'm_w_out': out['m_w_out'], 'm_final_g': out['m_final_g'], 'v_c_ctx': out['v_c_ctx'], 'v_norm_g': out['v_norm_g'], 'v_w_ada': out['v_w_ada'], 'v_b_ada': out['v_b_ada'], 'v_w_in': out['v_w_in'], 'v_w_conv_a': out['v_w_conv_a'], 'v_w_conv_b': out['v_w_conv_b'], 'v_b_conv_b': out['v_b_conv_b'], 'v_lru_wa': out['v_lru_wa'], 'v_lru_ba': out['v_lru_ba'], 'v_lru_wx': out['v_lru_wx'], 'v_lru_bx': out['v_lru_bx'], 'v_lru_lambda': out['v_lru_lambda'], 'v_w_out': out['v_w_out'], 'v_final_g': out['v_final_g']}


def _loss(weights, diff, rest, loss_target):
    with _jax.named_scope("forward"):
        args = {**rest, TWIN_DIFF_INPUT: diff, **{k: w.astype(_WEIGHT_DTYPES[k]) for k, w in weights.items()}}
        y = _forward(args)
    with _jax.named_scope("loss_head"):
        err = _jnp.square(y.astype(_jnp.float32) - loss_target)
        return 0.5 * _jnp.sum(_jnp.mean(err, axis=-1)) if err.ndim else 0.5 * err


def _adamw(w, g, m, v):
    m = ADAM_B1 * m + (1.0 - ADAM_B1) * g
    v = ADAM_B2 * v + (1.0 - ADAM_B2) * _jnp.square(g)
    m_hat = m / (1.0 - ADAM_B1 ** ADAM_STEP)
    v_hat = v / (1.0 - ADAM_B2 ** ADAM_STEP)
    delta = -ADAM_LR * (m_hat / (_jnp.sqrt(v_hat) + ADAM_EPS) + ADAM_WD * w)
    return delta, m, v


def reference(x, c, ctx, c_ctx, norm_g, w_ada, b_ada, w_in, w_conv_a, w_conv_b, b_conv_b, lru_wa, lru_ba, lru_wx, lru_bx, lru_lambda, w_out, final_g, loss_target, m_c_ctx, m_norm_g, m_w_ada, m_b_ada, m_w_in, m_w_conv_a, m_w_conv_b, m_b_conv_b, m_lru_wa, m_lru_ba, m_lru_wx, m_lru_bx, m_lru_lambda, m_w_out, m_final_g, v_c_ctx, v_norm_g, v_w_ada, v_b_ada, v_w_in, v_w_conv_a, v_w_conv_b, v_b_conv_b, v_lru_wa, v_lru_ba, v_lru_wx, v_lru_bx, v_lru_lambda, v_w_out, v_final_g):
    given = dict(x=x, c=c, ctx=ctx, c_ctx=c_ctx, norm_g=norm_g, w_ada=w_ada, b_ada=b_ada, w_in=w_in, w_conv_a=w_conv_a, w_conv_b=w_conv_b, b_conv_b=b_conv_b, lru_wa=lru_wa, lru_ba=lru_ba, lru_wx=lru_wx, lru_bx=lru_bx, lru_lambda=lru_lambda, w_out=w_out, final_g=final_g, loss_target=loss_target, m_c_ctx=m_c_ctx, m_norm_g=m_norm_g, m_w_ada=m_w_ada, m_b_ada=m_b_ada, m_w_in=m_w_in, m_w_conv_a=m_w_conv_a, m_w_conv_b=m_w_conv_b, m_b_conv_b=m_b_conv_b, m_lru_wa=m_lru_wa, m_lru_ba=m_lru_ba, m_lru_wx=m_lru_wx, m_lru_bx=m_lru_bx, m_lru_lambda=m_lru_lambda, m_w_out=m_w_out, m_final_g=m_final_g, v_c_ctx=v_c_ctx, v_norm_g=v_norm_g, v_w_ada=v_w_ada, v_b_ada=v_b_ada, v_w_in=v_w_in, v_w_conv_a=v_w_conv_a, v_w_conv_b=v_w_conv_b, v_b_conv_b=v_b_conv_b, v_lru_wa=v_lru_wa, v_lru_ba=v_lru_ba, v_lru_wx=v_lru_wx, v_lru_bx=v_lru_bx, v_lru_lambda=v_lru_lambda, v_w_out=v_w_out, v_final_g=v_final_g)
    weights = {n: given[n] for n in TWIN_WEIGHTS}
    shared = {n: given[n] for n in SHARED_INPUTS}
    per_example = {n: given[n] for n in ['x', 'c', 'ctx']}
    grad_fn = _jax.value_and_grad(_loss, argnums=(0, 1))

    def one_microbatch(ex, loss_target):
        ex = dict(ex)
        diff = ex.pop(TWIN_DIFF_INPUT)
        return grad_fn(weights, diff, {**shared, **ex}, loss_target)

    if N_MICROBATCH == 1:
        loss, (grad_w, grad_x) = one_microbatch(per_example, given["loss_target"])
    else:
        def body(carry, xs):
            loss_sum, grad_sum = carry
            l_k, (gw_k, gx_k) = one_microbatch(xs[0], xs[1])
            with _jax.named_scope("update"):
                return (loss_sum + l_k, _jax.tree.map(_jnp.add, grad_sum, gw_k)), gx_k

        init = (_jnp.zeros((), _jnp.float32), _jax.tree.map(_jnp.zeros_like, weights))
        (loss, grad_w), grad_x = _jax.lax.scan(body, init, (per_example, given["loss_target"]))
    with _jax.named_scope("update"):
        delta_w, new_m, new_v = {}, {}, {}
        for n in TWIN_WEIGHTS:
            delta_w[n], new_m[n], new_v[n] = _adamw(weights[n], grad_w[n], given["m_" + n], given["v_" + n])
    return (loss, grad_x, *[grad_w[n] for n in TWIN_WEIGHTS], *[delta_w[n] for n in TWIN_WEIGHTS],
            *[new_m[n] for n in TWIN_WEIGHTS], *[new_v[n] for n in TWIN_WEIGHTS])
```

```python
import functools

import jax
import jax.numpy as jnp
from jax import lax
from jax.experimental import pallas as pl
from jax.experimental.pallas import tpu as pltpu

F32 = jnp.float32
BF16 = jnp.bfloat16
MESH = pl.DeviceIdType.MESH
NDEV = 8
GRID_W = 64
N_HEADS = 16
LRU_C = 8.0
EPS = 1e-6
MXU_WIDTH = 256
VMEM_LIMIT = 60 * 1024 * 1024

ADAM_LR = 0.001
ADAM_B1 = 0.9
ADAM_B2 = 0.999
ADAM_EPS = 1e-08
ADAM_WD = 0.01
ADAM_STEP = 10
ADAM_C1 = 1.0 - ADAM_B1 ** ADAM_STEP
ADAM_C2 = 1.0 - ADAM_B2 ** ADAM_STEP

HIGHEST = lax.Precision.HIGHEST
ANY = pl.BlockSpec(memory_space=pl.ANY)
VMEM = pl.BlockSpec(memory_space=pltpu.VMEM)


def _call(body, **kw):
    return pl.pallas_call(body, **kw)


def _params(sem=None, vmem=VMEM_LIMIT):
    return pltpu.CompilerParams(dimension_semantics=sem, vmem_limit_bytes=vmem)


def _my_pos():
    return lax.axis_index("x"), lax.axis_index("y"), lax.axis_index("c")


def _idx(pos):
    return 4 * pos[0] + 2 * pos[1] + pos[2]


def _peer(k):
    x, y, c = _my_pos()
    return ((1 - x) if (k >> 2) & 1 else x, (1 - y) if (k >> 1) & 1 else y, (1 - c) if k & 1 else c)


def _exchange_vmem(src_ref, dst_ref, send_sems, recv_sems, base):
    me = _idx(_my_pos())
    sends = []
    for k in range(1, NDEV):
        cp = pltpu.make_async_remote_copy(
            src_ref=src_ref, dst_ref=dst_ref.at[me], send_sem=send_sems.at[base + k - 1],
            recv_sem=recv_sems.at[base + k - 1], device_id=_peer(k), device_id_type=MESH)
        cp.start()
        sends.append(cp)
    dst_ref[me] = src_ref[...]
    for k in range(1, NDEV):
        peer = _peer(k)
        pltpu.make_async_remote_copy(
            src_ref=src_ref, dst_ref=dst_ref.at[_idx(peer)], send_sem=send_sems.at[base + k - 1],
            recv_sem=recv_sems.at[base + k - 1], device_id=peer, device_id_type=MESH).wait_recv()
    for cp in sends:
        cp.wait_send()


def _sigmoid(z):
    return 1.0 / (1.0 + jnp.exp(-z))


def _softplus(x):
    return jnp.maximum(x, 0.0) + jnp.log1p(jnp.exp(-jnp.abs(x)))


def _neg_expm1(u):
    series = -(u * (1.0 + u * (0.5 + u * (1.0 / 6.0 + u * (1.0 / 24.0)))))
    return jnp.where(u > -0.03, series, 1.0 - jnp.exp(u))


def _dot(a, b):
    return jnp.dot(a, b, preferred_element_type=F32)


def _dot_nt(a, b):
    return lax.dot_general(a, b, (((1,), (1,)), ((), ())), preferred_element_type=F32)


def _rows(shape):
    return lax.broadcasted_iota(jnp.int32, shape, 0)


def _down(x, k, pos):
    return jnp.where(pos >= k, pltpu.roll(x, k, 0), 0.0)


def _up(x, k, pos, rowlen):
    return jnp.where(pos + k < rowlen, pltpu.roll(x, x.shape[0] - k, 0), 0.0)


def _pos_rowlen(shape, is_ctx):
    t = _rows(shape)
    pos = jnp.where(is_ctx, t, t & (GRID_W - 1))
    rowlen = jnp.where(is_ctx, shape[0], GRID_W)
    return pos, rowlen


def _conv4(v, w_ref, pos, rowlen):
    return (w_ref[0:1, :] * _down(v, 2, pos) + w_ref[1:2, :] * _down(v, 1, pos)
            + w_ref[2:3, :] * v + w_ref[3:4, :] * _up(v, 1, pos, rowlen))


def _conv4_t(dy, w_ref, pos, rowlen):
    return (w_ref[0:1, :] * _up(dy, 2, pos, rowlen) + w_ref[1:2, :] * _up(dy, 1, pos, rowlen)
            + w_ref[2:3, :] * dy + w_ref[3:4, :] * _down(dy, 1, pos))


def _conv3(t, w_ref, pos, rowlen):
    return w_ref[0:1, :] * _down(t, 1, pos) + w_ref[1:2, :] * t + w_ref[2:3, :] * _up(t, 1, pos, rowlen)


def _conv3_t(dz, w_ref, pos, rowlen):
    return w_ref[0:1, :] * _up(dz, 1, pos, rowlen) + w_ref[1:2, :] * dz + w_ref[2:3, :] * _down(dz, 1, pos)


def _chunk_scan(a, b, reverse):
    row = _rows(a.shape)
    for s in (1, 2, 4):
        if reverse:
            m = row < 8 - s
            sh = 8 - s
        else:
            m = row >= s
            sh = s
        a_s = jnp.where(m, pltpu.roll(a, sh, 0), 1.0)
        b_s = jnp.where(m, pltpu.roll(b, sh, 0), 0.0)
        b = b + a * b_s
        a = a * a_s
    return a, b


def _lru_coef(xb, wg_ref, d, ba, bx, lam, gc):
    w = xb.shape[1]
    xb16 = xb.astype(BF16)
    zr, zi = [], []
    for g in range(w // gc):
        z = _dot(xb16[:, g * gc:(g + 1) * gc], wg_ref[d, g])
        zr.append(z[:, :gc])
        zi.append(z[:, gc:])
    zr = zr[0] if len(zr) == 1 else jnp.concatenate(zr, axis=-1)
    zi = zi[0] if len(zi) == 1 else jnp.concatenate(zi, axis=-1)
    r = _sigmoid(zr + ba)
    ig = _sigmoid(zi + bx)
    sp = _softplus(-lam)
    la = -LRU_C * r * sp
    a = jnp.exp(la)
    s = jnp.sqrt(_neg_expm1(2.0 * la))
    return a, s, r, ig, sp


def _adamw(w, g, m, v):
    m2 = ADAM_B1 * m + (1.0 - ADAM_B1) * g
    v2 = ADAM_B2 * v + (1.0 - ADAM_B2) * (g * g)
    m_hat = m2 / ADAM_C1
    v_hat = v2 / ADAM_C2
    delta = -ADAM_LR * (m_hat / (jnp.sqrt(v_hat) + ADAM_EPS) + ADAM_WD * w)
    return delta, m2, v2


def _mod_forward(c8, cctx8, w_ada, small):
    d = c8.shape[1]
    cols = w_ada.shape[1]

    def body(c_ref, cctx_ref, w_ref, sm_ref, mod_ref, s_ref, sm_all, cbuf, mod_my, send_sems, recv_sems):
        _exchange_vmem(sm_ref, sm_all, send_sems, recv_sems, 2 * (NDEV - 1))
        _exchange_vmem(c_ref, cbuf, send_sems, recv_sems, 0)
        row = _rows((8, d))
        c_all = jnp.zeros((8, d), F32)
        for b in range(NDEV):
            c_all = jnp.where(row == b, cbuf[b], c_all)
        cc = cctx_ref[...]
        s_top = c_all * _sigmoid(c_all)
        s_bot = jnp.where(row == 0, cc * _sigmoid(cc), 0.0)
        s = jnp.concatenate([s_top, s_bot], axis=0)
        s_ref[...] = s
        mod_my[...] = jnp.dot(s, w_ref[...], precision=HIGHEST, preferred_element_type=F32)
        _exchange_vmem(mod_my, mod_ref, send_sems, recv_sems, NDEV - 1)

    return _call(
        body, name="mod_forward",
        out_shape=(jax.ShapeDtypeStruct((NDEV, 16, cols), F32), jax.ShapeDtypeStruct((16, d), F32),
                   jax.ShapeDtypeStruct((NDEV,) + small.shape, F32)),
        in_specs=[VMEM] * 4, out_specs=(VMEM,) * 3,
        scratch_shapes=[pltpu.VMEM((NDEV, 8, d), F32), pltpu.VMEM((16, cols), F32),
                        pltpu.SemaphoreType.DMA((3 * (NDEV - 1),)), pltpu.SemaphoreType.DMA((3 * (NDEV - 1),))],
        compiler_params=_params(),
    )(c8, cctx8, w_ada, small)


def _gather_weights(shards):
    n = len(shards)

    def body(*refs):
        ins, outs = refs[:n], refs[n:2 * n]
        send_sems, recv_sems, local_sems = refs[2 * n:]
        x, y, c = _my_pos()
        me, sibling = (x, y, c), (x, y, 1 - c)
        chips = [(1 - x, y), (x, 1 - y), (1 - x, 1 - y)]
        waits = []
        for a in range(n):
            src, out = ins[a], outs[a]

            def copy(k, block, to, from_src=False, src=src, out=out, a=a):
                return pltpu.make_async_remote_copy(
                    src_ref=src if from_src else out.at[_idx(block)], dst_ref=out.at[_idx(block)],
                    send_sem=send_sems.at[7 * a + k], recv_sem=recv_sems.at[7 * a + k],
                    device_id=to, device_id_type=MESH)

            mine = pltpu.make_async_copy(src, out.at[_idx(me)], local_sems.at[a])
            mine.start()
            first = [copy(0, me, sibling, True)]
            first += [copy(1 + j, me, (*chip, c), True) for j, chip in enumerate(chips)]
            for cp in first:
                cp.start()
            waits.append((copy, mine, first))
        for a in range(n):
            copy, mine, first = waits[a]
            passed = [copy(4 + j, (*chip, c), sibling) for j, chip in enumerate(chips)]
            for j, chip in enumerate(chips):
                copy(1 + j, (*chip, c), me).wait_recv()
                passed[j].start()
            copy(0, sibling, me).wait_recv()
            for j, chip in enumerate(chips):
                copy(4 + j, (*chip, 1 - c), me).wait_recv()
            for cp in first + passed:
                cp.wait_send()
            mine.wait()

    return _call(
        body, name="gather_weights",
        out_shape=tuple(jax.ShapeDtypeStruct((NDEV,) + s.shape, s.dtype) for s in shards),
        in_specs=[ANY] * n, out_specs=tuple([ANY] * n),
        scratch_shapes=[pltpu.SemaphoreType.DMA((7 * n,)), pltpu.SemaphoreType.DMA((7 * n,)),
                        pltpu.SemaphoreType.DMA((n,))],
        compiler_params=_params(),
    )(*shards)


def _scatter_partials(parts):
    n = len(parts)

    def body(*refs):
        ins, outs = refs[:n], refs[n:2 * n]
        send_sems, recv_sems, local_sems = refs[2 * n:]
        me = _idx(_my_pos())
        sends = []
        for a in range(n):
            mine = pltpu.make_async_copy(ins[a].at[me], outs[a].at[0], local_sems.at[a])
            mine.start()
            sends.append(mine)
        for k in range(1, NDEV):
            peer = _peer(k)
            for a in range(n):
                cp = pltpu.make_async_remote_copy(
                    src_ref=ins[a].at[_idx(peer)], dst_ref=outs[a].at[k],
                    send_sem=send_sems.at[7 * a + k - 1], recv_sem=recv_sems.at[7 * a + k - 1],
                    device_id=peer, device_id_type=MESH)
                cp.start()
                sends.append(cp)
        for k in range(1, NDEV):
            for a in range(n):
                pltpu.make_async_remote_copy(
                    src_ref=ins[a].at[0], dst_ref=outs[a].at[k],
                    send_sem=send_sems.at[7 * a + k - 1], recv_sem=recv_sems.at[7 * a + k - 1],
                    device_id=_peer(k), device_id_type=MESH).wait_recv()
        for cp in sends[:n]:
            cp.wait()
        for cp in sends[n:]:
            cp.wait_send()

    return _call(
        body, name="scatter_partials",
        out_shape=tuple(jax.ShapeDtypeStruct(p.shape, p.dtype) for p in parts),
        in_specs=[ANY] * n, out_specs=tuple([ANY] * n),
        scratch_shapes=[pltpu.SemaphoreType.DMA((7 * n,)), pltpu.SemaphoreType.DMA((7 * n,)),
                        pltpu.SemaphoreType.DMA((n,))],
        compiler_params=_params(),
    )(*parts)


def _reduce_small(packed, lru_parts, w_ada, dsilu_cctx):
    rp = packed.shape[0]
    rl = lru_parts.shape[1]
    d, cols = w_ada.shape
    assert cols % 128 == 0
    cb = cols // 128

    def body(p_ref, l_ref, w_ref, ds_ref, sum_ref, all_ref, lru_ref, cctx_ref,
             lsum, cpart, call, send_sems, recv_sems):
        me = _idx(_my_pos())
        _exchange_vmem(p_ref, all_ref, send_sems, recv_sems, 0)
        acc = all_ref[0]
        for j in range(1, NDEV):
            acc = acc + all_ref[j]
        sum_ref[...] = acc
        red = l_ref[0]
        for k in range(1, NDEV):
            red = red + l_ref[k]
        lsum[...] = red
        _exchange_vmem(lsum, lru_ref, send_sems, recv_sems, NDEV - 1)
        part = jnp.zeros((8, d), F32)
        for q in range(cb):
            dm = jnp.broadcast_to(sum_ref[pl.ds((NDEV + me) * cb + q, 1), :], (8, 128))
            part = part + lax.dot_general(dm, w_ref[:, q * 128:(q + 1) * 128],
                                          (((1,), (1,)), ((), ())), precision=HIGHEST,
                                          preferred_element_type=F32)
        cpart[...] = part
        _exchange_vmem(cpart, call, send_sems, recv_sems, 2 * (NDEV - 1))
        tot = call[0]
        for j in range(1, NDEV):
            tot = tot + call[j]
        cctx_ref[...] = tot * ds_ref[...]

    return _call(
        body, name="reduce_small",
        out_shape=(jax.ShapeDtypeStruct((rp, 128), F32), jax.ShapeDtypeStruct((NDEV, rp, 128), F32),
                   jax.ShapeDtypeStruct((NDEV, rl, 128), F32), jax.ShapeDtypeStruct((8, d), F32)),
        in_specs=[VMEM] * 4, out_specs=(VMEM,) * 4,
        scratch_shapes=[pltpu.VMEM((rl, 128), F32), pltpu.VMEM((8, d), F32), pltpu.VMEM((NDEV, 8, d), F32),
                        pltpu.SemaphoreType.DMA((3 * (NDEV - 1),)), pltpu.SemaphoreType.DMA((3 * (NDEV - 1),))],
        compiler_params=_params(),
    )(packed, lru_parts, w_ada, dsilu_cctx)


def _in_projection(x, ctx, mv, w_all, t):
    l, d = x.shape
    nt = l // t
    nb, _, bw = w_all.shape
    la = l + ctx.shape[0]

    def body(x_ref, c_ref, mv_ref, w_ref, p_ref, ht_ref, h_s):
        i, j = pl.program_id(0), pl.program_id(1)

        def norm(src):
            xf = src[...]
            r = lax.rsqrt(jnp.mean(xf * xf, axis=-1, keepdims=True) + EPS)
            h = xf * r * (mv_ref[0:1, :] * (1.0 + mv_ref[1:2, :])) + mv_ref[2:3, :]
            h_s[...] = h.astype(BF16)
            ht_ref[...] = h.T.astype(BF16)

        @pl.when((j == 0) & (i < nt))
        def _():
            norm(x_ref)

        @pl.when((j == 0) & (i >= nt))
        def _():
            norm(c_ref)

        p_ref[...] = _dot(h_s[...], w_ref[...]).astype(BF16)

    return _call(
        body, name="in_projection",
        grid=(nt + 1, nb),
        out_shape=(jax.ShapeDtypeStruct((la, nb * bw), BF16), jax.ShapeDtypeStruct((d, la), BF16)),
        in_specs=[pl.BlockSpec((t, d), lambda i, j: (jnp.minimum(i, nt - 1), 0)),
                  pl.BlockSpec((t, d), lambda i, j: (0, 0)),
                  pl.BlockSpec((None, 8, d), lambda i, j: (i // nt, 0, 0)),
                  pl.BlockSpec((None, d, bw), lambda i, j: (j, 0, 0))],
        out_specs=(pl.BlockSpec((t, bw), lambda i, j: (i, j)),
                   pl.BlockSpec((d, t), lambda i, j: (0, i))),
        scratch_shapes=[pltpu.VMEM((t, d), BF16)],
        compiler_params=_params(("arbitrary", "arbitrary")),
    )(x, ctx, mv, w_all)


def _lru_forward(p, wg, lv, wcb, l, t):
    la = p.shape[0]
    w = lv.shape[1]
    gc = wg.shape[2]
    nt = l // t
    nchunk = t // 8

    def body(vf_ref, vr_ref, wg_ref, lv_ref, wcb_ref, hf_ref, hr_ref, a_s, b_s, carry):
        i = pl.program_id(0)
        is_ctx = i == 0

        @pl.when(is_ctx)
        def _():
            carry[...] = jnp.zeros_like(carry)

        pos, rowlen = _pos_rowlen((t, w), is_ctx)
        for dr, (v_ref, h_ref) in enumerate(((vf_ref, hf_ref), (vr_ref, hr_ref))):
            v = v_ref[...].astype(F32)
            xb = _conv4(v, wcb_ref, pos, rowlen) + lv_ref[6:7, :]
            a, s, _, ig, _ = _lru_coef(xb, wg_ref, dr, lv_ref[3 * dr:3 * dr + 1, :],
                                       lv_ref[3 * dr + 1:3 * dr + 2, :], lv_ref[3 * dr + 2:3 * dr + 3, :], gc)
            a_s[...] = a
            b_s[...] = s * (ig * xb)
            reverse = dr == 1

            def step(k, h, reverse=reverse, h_ref=h_ref):
                r0 = pl.multiple_of((nchunk - 1 - k if reverse else k) * 8, 8)
                ca, cb = _chunk_scan(a_s[pl.ds(r0, 8), :], b_s[pl.ds(r0, 8), :], reverse)
                hh = ca * h + cb
                h_ref[pl.ds(r0, 8), :] = hh
                last = hh[0:1, :] if reverse else hh[7:8, :]
                return jnp.broadcast_to(last, (8, w))

            carry[dr] = lax.fori_loop(0, nchunk, step, carry[dr])

    full = lambda shape: pl.BlockSpec(shape, lambda i: (0,) * len(shape))
    fmap = lambda i: (jnp.where(i == 0, nt, i - 1), 0)
    rmap = lambda i: (jnp.where(i == 0, nt, nt - i), 0)
    vcol = 4
    return _call(
        body, name="lru_forward",
        grid=(nt + 1,),
        out_shape=(jax.ShapeDtypeStruct((la, w), F32), jax.ShapeDtypeStruct((la, w), F32)),
        in_specs=[pl.BlockSpec((t, w), lambda i: (jnp.where(i == 0, nt, i - 1), vcol)),
                  pl.BlockSpec((t, w), lambda i: (jnp.where(i == 0, nt, nt - i), vcol)),
                  full(wg.shape), full(lv.shape), full(wcb.shape)],
        out_specs=(pl.BlockSpec((t, w), fmap), pl.BlockSpec((t, w), rmap)),
        scratch_shapes=[pltpu.VMEM((t, w), F32), pltpu.VMEM((t, w), F32), pltpu.VMEM((2, 8, w), F32)],
        compiler_params=_params(("arbitrary",)),
    )(p, p, wg, lv, wcb)


def _mix_gates(p_refs, hf_ref, hr_ref, wca_ref, t, w):
    bl, cl, ul, gl, ql = [r[...].astype(F32) for r in p_refs]
    pos, rowlen = _pos_rowlen((t, w), False)
    tt = cl * ul
    z = _conv3(tt, wca_ref, pos, rowlen)
    sig_g = _sigmoid(gl)
    sig_q = _sigmoid(ql)
    ylru = hf_ref[...] + hr_ref[...]
    return bl, cl, ul, gl, ql, tt, z, sig_g, sig_q, ylru, pos, rowlen


def _p_specs(t, w, nt):
    return [pl.BlockSpec((t, w), functools.partial(lambda i, s: (jnp.minimum(i, nt - 1), s), s=s))
            for s in (0, 1, 2, 3, 5)]


def _mix_forward(x, tgt, p, hf, hr, wo, ov, wca, t):
    l, d = x.shape
    w = d // 2
    nt = l // t

    def body(x_ref, tg_ref, b_ref, c_ref, u_ref, g_ref, q_ref, hf_ref, hr_ref, wo_ref, ov_ref, wca_ref,
             dn_ref, ct_ref, do_ref, part_ref):
        i = pl.program_id(0)
        bl, _, _, gl, ql, _, z, sig_g, sig_q, ylru, _, _ = _mix_gates(
            (b_ref, c_ref, u_ref, g_ref, q_ref), hf_ref, hr_ref, wca_ref, t, w)
        ya = bl * z * (gl * sig_g)
        yb = ylru * (ql * sig_q)
        ct_ref[0:w, :] = ya.T.astype(BF16)
        ct_ref[w:, :] = yb.T.astype(BF16)
        out = _dot(ya.astype(BF16), wo_ref[0:w, :]) + _dot(yb.astype(BF16), wo_ref[w:, :])
        gate, fg = ov_ref[0:1, :], ov_ref[1:2, :]
        n = x_ref[...] + gate * out
        rr = lax.rsqrt(jnp.mean(n * n, axis=-1, keepdims=True) + EPS)
        nh = n * rr
        e = nh * fg - tg_ref[...]
        loss = 0.5 * jnp.sum(jnp.mean(e * e, axis=-1, keepdims=True), axis=0, keepdims=True)
        dy = e * (1.0 / d)
        dnh = dy * fg
        dn = rr * (dnh - nh * jnp.mean(dnh * nh, axis=-1, keepdims=True))
        dn_ref[...] = dn
        do_ref[...] = (dn * gate).astype(BF16)

        @pl.when(i == 0)
        def _():
            part_ref[...] = jnp.zeros_like(part_ref)

        part_ref[0:1, :] += jnp.sum(dy * nh, axis=0, keepdims=True)
        part_ref[1:2, :] += jnp.sum(dn * out, axis=0, keepdims=True)
        part_ref[2:3, :] += jnp.broadcast_to(loss, (1, d))

    tile = lambda cols: pl.BlockSpec((t, cols), lambda i: (i, 0))
    full = lambda shape: pl.BlockSpec(shape, lambda i: (0,) * len(shape))
    return _call(
        body, name="mix_forward",
        grid=(nt,),
        out_shape=(jax.ShapeDtypeStruct((l, d), F32), jax.ShapeDtypeStruct((d, l), BF16),
                   jax.ShapeDtypeStruct((l, d), BF16), jax.ShapeDtypeStruct((8, d), F32)),
        in_specs=[tile(d), tile(d)] + _p_specs(t, w, nt) + [tile(w), tile(w),
                  pl.BlockSpec((d, d), lambda i: (0, 0), pipeline_mode=pl.Buffered(1)),
                  full(ov.shape), full(wca.shape)],
        out_specs=(tile(d), pl.BlockSpec((d, t), lambda i: (0, i)), tile(d), full((8, d))),
        compiler_params=_params(("arbitrary",)),
    )(x, tgt, p, p, p, p, p, hf, hr, wo, ov, wca)


def _mix_backward(dout, p, hf, hr, wo, wca, l, t):
    d = dout.shape[1]
    w = d // 2
    nt = l // t
    la = p.shape[0]

    def body(do_ref, b_ref, c_ref, u_ref, g_ref, q_ref, hf_ref, hr_ref, wo_ref, wca_ref,
             dp_ref, dh_ref, part_ref):
        i = pl.program_id(0)

        @pl.when(i == 0)
        def _():
            part_ref[...] = jnp.zeros_like(part_ref)

        @pl.when(i == nt)
        def _():
            dp_ref[...] = jnp.zeros_like(dp_ref)

        @pl.when(i < nt)
        def _():
            bl, cl, ul, gl, ql, tt, z, sig_g, sig_q, ylru, pos, rowlen = _mix_gates(
                (b_ref, c_ref, u_ref, g_ref, q_ref), hf_ref, hr_ref, wca_ref, t, w)
            do = do_ref[...]
            dya = _dot_nt(do, wo_ref[0:w, :])
            dyb = _dot_nt(do, wo_ref[w:, :])
            sg = gl * sig_g
            dz = dya * bl * sg
            dt = _conv3_t(dz, wca_ref, pos, rowlen)
            dp_ref[:, 0:w] = (dya * z * sg).astype(BF16)
            dp_ref[:, w:2 * w] = (dt * ul).astype(BF16)
            dp_ref[:, 2 * w:3 * w] = (dt * cl).astype(BF16)
            dp_ref[:, 3 * w:4 * w] = (dya * bl * z * (sig_g * (1.0 + gl * (1.0 - sig_g)))).astype(BF16)
            dp_ref[:, 4 * w:5 * w] = jnp.zeros((t, w), BF16)
            dp_ref[:, 5 * w:6 * w] = (dyb * ylru * (sig_q * (1.0 + ql * (1.0 - sig_q)))).astype(BF16)
            dh_ref[...] = dyb * (ql * sig_q)
            part_ref[0:1, :] += jnp.sum(dz * _down(tt, 1, pos), axis=0, keepdims=True)
            part_ref[1:2, :] += jnp.sum(dz * tt, axis=0, keepdims=True)
            part_ref[2:3, :] += jnp.sum(dz * _up(tt, 1, pos, rowlen), axis=0, keepdims=True)

    clamp = lambda cols: pl.BlockSpec((t, cols), lambda i: (jnp.minimum(i, nt - 1), 0))
    full = lambda shape: pl.BlockSpec(shape, lambda i: (0,) * len(shape))
    return _call(
        body, name="mix_backward",
        grid=(nt + 1,),
        out_shape=(jax.ShapeDtypeStruct((la, 6 * w), BF16), jax.ShapeDtypeStruct((l, w), F32),
                   jax.ShapeDtypeStruct((8, w), F32)),
        in_specs=[clamp(d)] + _p_specs(t, w, nt) + [clamp(w), clamp(w),
                  pl.BlockSpec((d, d), lambda i: (0, 0), pipeline_mode=pl.Buffered(1)), full(wca.shape)],
        out_specs=(pl.BlockSpec((t, 6 * w), lambda i: (i, 0)), clamp(w), full((8, w))),
        compiler_params=_params(("arbitrary",)),
    )(dout, p, p, p, p, p, hf, hr, wo, wca)


def _lru_backward(direction, p, dhs, hs, wg, lv, wcb, l, t, dxb_other=None, dp=None):
    la, w = hs.shape
    gc = wg.shape[2]
    ng = w // gc
    nt = l // t
    nchunk = t // 8
    nblk8 = la // 8
    last = direction == 1

    if direction == 0:
        tile = lambda i: jnp.where(i == nt, nt, nt - 1 - i)
        halo = lambda i: jnp.where(tile(i) == 0, nblk8 - 1, tile(i) * (t // 8) - 1)
    else:
        tile = lambda i: i
        halo = lambda i: jnp.minimum((i + 1) * (t // 8), nblk8 - 1)

    def body(*refs):
        if last:
            (v_ref, dh_ref, hs_ref, halo_ref, wg_ref, lv_ref, wcb_ref, dxo_ref, _,
             out_ref, dwg_ref, part_ref, a_s, b_s, g_s, carry) = refs
        else:
            (v_ref, dh_ref, hs_ref, halo_ref, wg_ref, lv_ref, wcb_ref,
             out_ref, dwg_ref, part_ref, a_s, b_s, g_s, carry) = refs
        i = pl.program_id(0)
        is_ctx = i == nt

        @pl.when(i == 0)
        def _():
            carry[...] = jnp.zeros_like(carry)
            dwg_ref[...] = jnp.zeros_like(dwg_ref)
            part_ref[...] = jnp.zeros_like(part_ref)

        pos, rowlen = _pos_rowlen((t, w), is_ctx)
        row = _rows((t, w))
        v = v_ref[...].astype(F32)
        xb = _conv4(v, wcb_ref, pos, rowlen) + lv_ref[6:7, :]
        lam = lv_ref[3 * direction + 2:3 * direction + 3, :]
        a, s, r, ig, sp = _lru_coef(xb, wg_ref, direction, lv_ref[3 * direction:3 * direction + 1, :],
                                    lv_ref[3 * direction + 1:3 * direction + 2, :], lam, gc)
        dh = jnp.where(is_ctx, 0.0, dh_ref[...])
        hs_t = hs_ref[...]
        if direction == 0:
            edge = jnp.where(is_ctx, 0.0, halo_ref[7:8, :])
            hprev = jnp.where(row == 0, edge, pltpu.roll(hs_t, 1, 0))
        else:
            edge = jnp.where(is_ctx, 0.0, halo_ref[0:1, :])
            hprev = jnp.where(row == t - 1, edge, pltpu.roll(hs_t, t - 1, 0))
        a_s[...] = a
        b_s[...] = a * dh
        g_s[...] = dh

        def step(k, u_next):
            if direction == 0:
                r0 = pl.multiple_of((nchunk - 1 - k) * 8, 8)
            else:
                r0 = pl.multiple_of(k * 8, 8)
            ca, cb = _chunk_scan(a_s[pl.ds(r0, 8), :], b_s[pl.ds(r0, 8), :], direction == 0)
            u = ca * u_next + cb
            r8 = _rows((8, w))
            if direction == 0:
                shifted = jnp.where(r8 < 7, pltpu.roll(u, 7, 0), u_next)
                nxt = u[0:1, :]
            else:
                shifted = jnp.where(r8 >= 1, pltpu.roll(u, 1, 0), u_next)
                nxt = u[7:8, :]
            g_s[pl.ds(r0, 8), :] = g_s[pl.ds(r0, 8), :] + shifted
            return jnp.broadcast_to(nxt, (8, w))

        carry[...] = lax.fori_loop(0, nchunk, step, carry[...])

        g = g_s[...]
        ix = ig * xb
        dla = g * hprev * a - g * ix * (a * a / s)
        di = g * s * xb
        dxb = g * s * ig
        dzr = dla * (-LRU_C * sp) * r * (1.0 - r)
        dzi = di * ig * (1.0 - ig)
        part_ref[0:1, :] += jnp.sum(dzr, axis=0, keepdims=True)
        part_ref[1:2, :] += jnp.sum(dzi, axis=0, keepdims=True)
        part_ref[2:3, :] += jnp.sum(dla * r, axis=0, keepdims=True) * (LRU_C * _sigmoid(-lam))
        xb16 = xb.astype(BF16)
        pieces = []
        for gi in range(ng):
            sl = slice(gi * gc, (gi + 1) * gc)
            dz = jnp.concatenate([dzr[:, sl], dzi[:, sl]], axis=-1).astype(BF16)
            pieces.append(_dot_nt(dz, wg_ref[direction, gi]))
            dwg_ref[gi] += _dot(xb[:, sl].T.astype(BF16), dz)
        dxb = dxb + (pieces[0] if ng == 1 else jnp.concatenate(pieces, axis=-1))
        if not last:
            out_ref[...] = dxb
        else:
            dxb = dxb + dxo_ref[...]
            out_ref[...] = _conv4_t(dxb, wcb_ref, pos, rowlen).astype(BF16)
            part_ref[3:4, :] += jnp.sum(dxb, axis=0, keepdims=True)
            part_ref[4:5, :] += jnp.sum(dxb * _down(v, 2, pos), axis=0, keepdims=True)
            part_ref[5:6, :] += jnp.sum(dxb * _down(v, 1, pos), axis=0, keepdims=True)
            part_ref[6:7, :] += jnp.sum(dxb * v, axis=0, keepdims=True)
            part_ref[7:8, :] += jnp.sum(dxb * _up(v, 1, pos, rowlen), axis=0, keepdims=True)

    full = lambda shape: pl.BlockSpec(shape, lambda i: (0,) * len(shape))
    in_specs = [pl.BlockSpec((t, w), lambda i: (tile(i), 4)),
                pl.BlockSpec((t, w), lambda i: (jnp.minimum(tile(i), nt - 1), 0)),
                pl.BlockSpec((t, w), lambda i: (tile(i), 0)),
                pl.BlockSpec((8, w), lambda i: (halo(i), 0)),
                full(wg.shape), full(lv.shape), full(wcb.shape)]
    args = [p, dhs, hs, hs, wg, lv, wcb]
    if last:
        in_specs += [pl.BlockSpec((t, w), lambda i: (tile(i), 0)), ANY]
        args += [dxb_other, dp]
        out0 = jax.ShapeDtypeStruct(dp.shape, dp.dtype)
        spec0 = pl.BlockSpec((t, w), lambda i: (tile(i), 4))
        aliases = {8: 0}
    else:
        out0 = jax.ShapeDtypeStruct((la, w), F32)
        spec0 = pl.BlockSpec((t, w), lambda i: (tile(i), 0))
        aliases = {}
    return _call(
        body, name="lru_backward_%d" % direction,
        grid=(nt + 1,),
        out_shape=(out0, jax.ShapeDtypeStruct((ng, gc, 2 * gc), F32), jax.ShapeDtypeStruct((8, w), F32)),
        in_specs=in_specs,
        out_specs=(spec0, full((ng, gc, 2 * gc)), full((8, w))),
        scratch_shapes=[pltpu.VMEM((t, w), F32), pltpu.VMEM((t, w), F32), pltpu.VMEM((t, w), F32),
                        pltpu.VMEM((8, w), F32)],
        input_output_aliases=aliases,
        compiler_params=_params(("arbitrary",)),
    )(*args)


def _weight_grad_t(at, b, nblk_m, nblk_n, tk, name):
    m, k = at.shape
    n = b.shape[1]
    bm, bn = m // nblk_m, n // nblk_n
    nk = k // tk

    def body(a_ref, b_ref, o_ref, acc):
        kk = pl.program_id(2)

        @pl.when(kk == 0)
        def _():
            acc[...] = jnp.zeros_like(acc)

        acc[...] += _dot(a_ref[...], b_ref[...])

        @pl.when(kk == nk - 1)
        def _():
            o_ref[...] = acc[...].astype(BF16)

    return _call(
        body, name=name,
        grid=(nblk_m, nblk_n, nk),
        out_shape=jax.ShapeDtypeStruct((nblk_m * nblk_n, bm, bn), BF16),
        in_specs=[pl.BlockSpec((bm, tk), lambda i, j, kk: (i, kk)),
                  pl.BlockSpec((tk, bn), lambda i, j, kk: (kk, j))],
        out_specs=pl.BlockSpec((None, bm, bn), lambda i, j, kk: (i * nblk_n + j, 0, 0)),
        scratch_shapes=[pltpu.VMEM((bm, bn), F32)],
        compiler_params=_params(("arbitrary", "arbitrary", "arbitrary")),
    )(at, b)


def _input_backward(dp, w_all, x, ctx, dn, mv, t):
    l, d = x.shape
    nt = l // t
    nb, _, bw = w_all.shape

    def body(dp_ref, w_ref, x_ref, c_ref, dn_ref, mv_ref, gx_ref, part_ref, acc):
        i, k = pl.program_id(0), pl.program_id(1)

        @pl.when((i == 0) & (k == 0))
        def _():
            part_ref[...] = jnp.zeros_like(part_ref)

        @pl.when(k == 0)
        def _():
            acc[...] = jnp.zeros_like(acc)

        acc[...] += _dot_nt(dp_ref[...], w_ref[...])

        def finish(src, sc, row0):
            xf = src[...]
            r = lax.rsqrt(jnp.mean(xf * xf, axis=-1, keepdims=True) + EPS)
            xn = xf * r
            dhl = acc[...]
            gain = mv_ref[0:1, :]
            part_ref[row0:row0 + 1, :] += jnp.sum(dhl, axis=0, keepdims=True)
            part_ref[row0 + 1:row0 + 2, :] += jnp.sum(dhl * xn, axis=0, keepdims=True) * gain
            part_ref[4:5, :] += jnp.sum(dhl * xn, axis=0, keepdims=True) * (1.0 + sc)
            dxn = dhl * (gain * (1.0 + sc))
            return r * (dxn - xn * jnp.mean(dxn * xn, axis=-1, keepdims=True))

        @pl.when((k == nb - 1) & (i < nt))
        def _():
            gx_ref[...] = dn_ref[...] + finish(x_ref, mv_ref[1:2, :], 0)

        @pl.when((k == nb - 1) & (i >= nt))
        def _():
            finish(c_ref, mv_ref[2:3, :], 2)

    clamp = pl.BlockSpec((t, d), lambda i, k: (jnp.minimum(i, nt - 1), 0))
    return _call(
        body, name="input_backward",
        grid=(nt + 1, nb),
        out_shape=(jax.ShapeDtypeStruct((l, d), F32), jax.ShapeDtypeStruct((8, d), F32)),
        in_specs=[pl.BlockSpec((t, bw), lambda i, k: (i, k)),
                  pl.BlockSpec((None, d, bw), lambda i, k: (k, 0, 0)),
                  clamp, pl.BlockSpec((t, d), lambda i, k: (0, 0)), clamp,
                  pl.BlockSpec((8, d), lambda i, k: (0, 0))],
        out_specs=(clamp, pl.BlockSpec((8, d), lambda i, k: (0, 0))),
        scratch_shapes=[pltpu.VMEM((t, d), F32)],
        compiler_params=_params(("arbitrary", "arbitrary")),
    )(dp, w_all, x, ctx, dn, mv)


def _adamw_scattered(parts, w, m, v, tr):
    r, c = w.shape

    def body(p_ref, w_ref, m_ref, v_ref, g_ref, d_ref, m2_ref, v2_ref):
        g = p_ref[0].astype(F32)
        for k in range(1, NDEV):
            g = g + p_ref[k].astype(F32)
        g_ref[...] = g
        d_ref[...], m2_ref[...], v2_ref[...] = _adamw(w_ref[...], g, m_ref[...], v_ref[...])

    tile = pl.BlockSpec((tr, c), lambda i: (i, 0))
    return _call(
        body, name="adamw_scattered_%dx%d" % (r, c),
        grid=(r // tr,),
        out_shape=tuple(jax.ShapeDtypeStruct((r, c), F32) for _ in range(4)),
        in_specs=[pl.BlockSpec((NDEV, tr, c), lambda i: (0, i, 0)), tile, tile, tile],
        out_specs=(tile,) * 4,
        compiler_params=_params(("arbitrary",)),
    )(parts, w, m, v)


def _adamw_ada(st, dmod, w, m, v, tr):
    r, c = w.shape

    def body(s_ref, dm_ref, w_ref, m_ref, v_ref, g_ref, d_ref, m2_ref, v2_ref):
        g = jnp.dot(s_ref[...], dm_ref[...], precision=HIGHEST, preferred_element_type=F32)
        g_ref[...] = g
        d_ref[...], m2_ref[...], v2_ref[...] = _adamw(w_ref[...], g, m_ref[...], v_ref[...])

    tile = pl.BlockSpec((tr, c), lambda i: (i, 0))
    return _call(
        body, name="adamw_ada",
        grid=(r // tr,),
        out_shape=tuple(jax.ShapeDtypeStruct((r, c), F32) for _ in range(4)),
        in_specs=[pl.BlockSpec((tr, 16), lambda i: (i, 0)), pl.BlockSpec((16, c), lambda i: (0, 0)),
                  tile, tile, tile],
        out_specs=(tile,) * 4,
        compiler_params=_params(("arbitrary",)),
    )(st, dmod, w, m, v)


def _adamw_packed(g, w, m, v):
    def body(g_ref, w_ref, m_ref, v_ref, d_ref, m2_ref, v2_ref):
        d_ref[...], m2_ref[...], v2_ref[...] = _adamw(w_ref[...], g_ref[...], m_ref[...], v_ref[...])

    return _call(
        body, name="adamw_packed",
        out_shape=tuple(jax.ShapeDtypeStruct(w.shape, F32) for _ in range(3)),
        in_specs=[VMEM] * 4, out_specs=(VMEM,) * 3,
        compiler_params=_params(),
    )(g, w, m, v)


def _blockdiag_groups(wh, gc):
    h, dh, _ = wh.shape
    g = gc // dh
    w4 = wh.reshape(h // g, g, dh, dh)
    bd = jnp.einsum("ngij,gh->ngihj", w4, jnp.eye(g, dtype=wh.dtype))
    return bd.reshape(h // g, gc, gc)


def _blockdiag_extract(bd, dh):
    ng, gc, _ = bd.shape
    g = gc // dh
    x = bd.reshape(ng, g, dh, g, dh)
    return jnp.einsum("ngihj,gh->ngij", x, jnp.eye(g, dtype=bd.dtype)).reshape(ng * g, dh, dh)


def _rows8(*vecs):
    rows = [jnp.reshape(v, (1, -1)).astype(F32) for v in vecs]
    n = rows[0].shape[1]
    return jnp.concatenate(rows + [jnp.zeros((8 - len(rows), n), F32)], axis=0)


def _pack(pieces):
    flat = jnp.concatenate([jnp.reshape(a, (-1,)).astype(F32) for a in pieces])
    total = -(-flat.shape[0] // 1024) * 1024
    return jnp.pad(flat, (0, total - flat.shape[0])).reshape(total // 128, 128)


def _unpack(packed, shapes):
    flat = packed.reshape(-1)
    out, off = [], 0
    for s in shapes:
        n = 1
        for q in s:
            n *= q
        out.append(flat[off:off + n].reshape(s))
        off += n
    return out


def kernel(x, c, ctx, c_ctx, norm_g, w_ada, b_ada, w_in, w_conv_a, w_conv_b, b_conv_b, lru_wa, lru_ba, lru_wx, lru_bx, lru_lambda, w_out, final_g, loss_target, m_c_ctx, m_norm_g, m_w_ada, m_b_ada, m_w_in, m_w_conv_a, m_w_conv_b, m_b_conv_b, m_lru_wa, m_lru_ba, m_lru_wx, m_lru_bx, m_lru_lambda, m_w_out, m_final_g, v_c_ctx, v_norm_g, v_w_ada, v_b_ada, v_w_in, v_w_conv_a, v_w_conv_b, v_b_conv_b, v_lru_wa, v_lru_ba, v_lru_wx, v_lru_bx, v_lru_lambda, v_w_out, v_final_g):
    _, l, d = x.shape
    lc = ctx.shape[1]
    w = d // 2
    t = lc
    assert l % t == 0 and t % GRID_W == 0 and t % 128 == 0
    dh = w // N_HEADS
    gc = min(w, MXU_WIDTH)
    cols = w_ada.shape[2]
    wo_rows = w_out.shape[1]
    me = _idx(_my_pos())
    x2, ctx2, tgt2 = x[0], ctx[0], loss_target[0]
    w_ada2, w_in2, w_out2 = w_ada[0], w_in[0], w_out[0]

    small_mine = jnp.concatenate([w_conv_a[0], w_conv_b[0], lru_ba[0], lru_bx[0], lru_lambda[0],
                                  jnp.zeros((3, w // NDEV), F32)], axis=0)
    mod_all, s_mat, small_all = _mod_forward(
        jnp.broadcast_to(c, (8, d)), jnp.broadcast_to(c_ctx[None], (8, d)), w_ada2, small_mine)
    mod = jnp.transpose(mod_all, (1, 0, 2)).reshape(16, NDEV * cols) + b_ada
    mod_lat = lax.dynamic_slice_in_dim(mod, me, 1, axis=0)
    sh_l, sc_l, gt_l = jnp.split(mod_lat, 3, axis=-1)
    sh_c, sc_c, _ = jnp.split(mod[8:9], 3, axis=-1)
    small = jnp.transpose(small_all, (1, 0, 2)).reshape(16, w)
    wca = _rows8(*[small[j] for j in range(0, 3)])
    wcb = _rows8(*[small[j] for j in range(3, 7)])
    lv = _rows8(small[7], small[9], small[11], small[8], small[10], small[12], b_conv_b)
    w_all, wo_all = _gather_weights([w_in2.astype(BF16), w_out2.astype(BF16)])
    wo = wo_all.reshape(d, d)
    wg = jnp.stack([
        jnp.concatenate([_blockdiag_groups(lru_wa[0, dr], gc), _blockdiag_groups(lru_wx[0, dr], gc)], axis=-1)
        for dr in range(2)]).astype(BF16)

    mv_f = jnp.stack([_rows8(norm_g, sc_l, sh_l), _rows8(norm_g, sc_c, sh_c)])
    p, hlt = _in_projection(x2, ctx2, mv_f, w_all, t)
    hf, hr = _lru_forward(p, wg, lv, wcb, l, t)
    dn, catt, dout, part_mix = _mix_forward(x2, tgt2, p, hf, hr, wo, _rows8(gt_l, final_g), wca, t)
    dp, dhs, part_ca = _mix_backward(dout, p, hf, hr, wo, wca, l, t)
    dxb0, dwg0, part_l0 = _lru_backward(0, p, dhs, hf, wg, lv, wcb, l, t)
    dp, dwg1, part_l1 = _lru_backward(1, p, dhs, hr, wg, lv, wcb, l, t, dxb_other=dxb0, dp=dp)
    tk = 3 * t if (l + lc) % (3 * t) == 0 else t
    g_wout = _weight_grad_t(catt, dout, NDEV, 1, 4 * t if l % (4 * t) == 0 else t, "grad_w_out")
    g_win = _weight_grad_t(hlt, dp, 1, NDEV, tk, "grad_w_in")
    grad_x, part_in = _input_backward(dp, w_all, x2, ctx2, dn, _rows8(norm_g, sc_l, sc_c), t)

    dwa = jnp.stack([_blockdiag_extract(dwg0[:, :, :gc], dh), _blockdiag_extract(dwg1[:, :, :gc], dh)])
    dwx = jnp.stack([_blockdiag_extract(dwg0[:, :, gc:], dh), _blockdiag_extract(dwg1[:, :, gc:], dh)])
    lru_part = jnp.stack([dwa, dwx]).reshape(NDEV, -1, 128)
    sc_win, sc_wout, sc_lru = _scatter_partials([g_win, g_wout, lru_part])
    zeros_d = jnp.zeros((d,), F32)
    pieces = [
        jnp.concatenate([part_in[0], part_in[1], part_mix[1]]),
        jnp.concatenate([part_in[2], part_in[3], zeros_d]),
        part_in[4], part_mix[0], part_ca[0:3], part_l1[4:8], part_l1[3],
        jnp.stack([part_l0[0], part_l1[0]]), jnp.stack([part_l0[1], part_l1[1]]),
        jnp.stack([part_l0[2], part_l1[2]]), part_mix[2, 0:1],
    ]
    shapes = [(3 * d,), (3 * d,), (d,), (d,), (3, w), (4, w), (w,), (2, w), (2, w), (2, w), (1,)]
    sig_cc = jax.nn.sigmoid(c_ctx)
    dsilu_cc = jnp.broadcast_to((sig_cc * (1.0 + c_ctx * (1.0 - sig_cc)))[None], (8, d))
    psum, pall, lru_sum, g_cctx8 = _reduce_small(_pack(pieces), sc_lru, w_ada2, dsilu_cc)
    (g_modl, g_modc, g_norm, g_final, g_ca, g_cb, g_bcb, g_ba, g_bx, g_lam, loss1) = _unpack(psum, shapes)
    loss = loss1[0]
    g_cctx = g_cctx8[0]
    g_bada = (g_modl + g_modc)[None]
    g_lru = lru_sum.reshape(2, 2, N_HEADS, dh, dh)
    g_wa, g_wx = g_lru[0][None], g_lru[1][None]
    wsl = w // NDEV
    mine = lambda a: lax.dynamic_slice_in_dim(a, me * wsl, wsl, axis=-1)
    g_ca_m, g_cb_m, g_ba_m, g_bx_m, g_lam_m = (mine(g_ca)[None], mine(g_cb)[None], mine(g_ba)[None],
                                               mine(g_bx)[None], mine(g_lam)[None])
    g_norm, g_bcb = g_norm[None], g_bcb[None]

    cb = cols // 128
    per_dev = pall[:, :3 * d // 128].reshape(NDEV, NDEV, cols)
    dmod_lat = lax.dynamic_slice_in_dim(per_dev, me, 1, axis=1)[:, 0]
    dmod_ctx = lax.dynamic_slice_in_dim(g_modc.reshape(NDEV, cols), me, 1, axis=0)
    dmod16 = jnp.concatenate([dmod_lat, dmod_ctx, jnp.zeros((7, cols), F32)], axis=0)
    tr_ada = 256 if d % 256 == 0 else d
    g_wada, d_wada, m_wada, v_wada = _adamw_ada(s_mat.T, dmod16, w_ada2, m_w_ada[0], v_w_ada[0], tr_ada)
    g_win2, d_win, m_win, v_win = _adamw_scattered(sc_win, w_in2, m_w_in[0], v_w_in[0], tr_ada)
    tr_out = 64 if wo_rows % 64 == 0 else wo_rows
    g_wout2, d_wout, m_wout, v_wout = _adamw_scattered(sc_wout, w_out2, m_w_out[0], v_w_out[0], tr_out)

    small_w = [c_ctx, norm_g, b_ada, w_conv_a, w_conv_b, b_conv_b, lru_wa, lru_ba, lru_wx, lru_bx, lru_lambda, final_g]
    small_m = [m_c_ctx, m_norm_g, m_b_ada, m_w_conv_a, m_w_conv_b, m_b_conv_b, m_lru_wa, m_lru_ba, m_lru_wx,
               m_lru_bx, m_lru_lambda, m_final_g]
    small_v = [v_c_ctx, v_norm_g, v_b_ada, v_w_conv_a, v_w_conv_b, v_b_conv_b, v_lru_wa, v_lru_ba, v_lru_wx,
               v_lru_bx, v_lru_lambda, v_final_g]
    small_g = [g_cctx, g_norm, g_bada, g_ca_m, g_cb_m, g_bcb, g_wa, g_ba_m, g_wx, g_bx_m, g_lam_m, g_final]
    sshapes = [a.shape for a in small_w]
    d_s, m_s, v_s = _adamw_packed(_pack(small_g), _pack(small_w), _pack(small_m), _pack(small_v))
    d_s, m_s, v_s = _unpack(d_s, sshapes), _unpack(m_s, sshapes), _unpack(v_s, sshapes)
    small_g = [jnp.reshape(a, s) for a, s in zip(small_g, sshapes)]

    def weights(small_list, ada, win, wout):
        (cctx_, norm_, bada_, ca_, cb_, bcb_, wa_, ba_, wx_, bx_, lam_, final_) = small_list
        return [cctx_, norm_, ada[None], bada_, win[None], ca_, cb_, bcb_, wa_, ba_, wx_, bx_, lam_, wout[None], final_]

    return (loss, grad_x[None],
            *weights(small_g, g_wada, g_win2, g_wout2), *weights(d_s, d_wada, d_win, d_wout),
            *weights(m_s, m_wada, m_win, m_wout), *weights(v_s, v_wada, v_win, v_wout))
```

```python
import functools

import jax
import jax.numpy as jnp
from jax import lax
from jax.experimental import pallas as pl
from jax.experimental.pallas import tpu as pltpu

F32 = jnp.float32
BF16 = jnp.bfloat16
MESH = pl.DeviceIdType.MESH
NDEV = 8
GRID_W = 64
N_HEADS = 16
LRU_C = 8.0
EPS = 1e-6
MXU_WIDTH = 256
VMEM_LIMIT = 60 * 1024 * 1024

ADAM_LR = 0.001
ADAM_B1 = 0.9
ADAM_B2 = 0.999
ADAM_EPS = 1e-08
ADAM_WD = 0.01
ADAM_STEP = 10
ADAM_C1 = 1.0 - ADAM_B1 ** ADAM_STEP
ADAM_C2 = 1.0 - ADAM_B2 ** ADAM_STEP

HIGHEST = lax.Precision.HIGHEST
ANY = pl.BlockSpec(memory_space=pl.ANY)
VMEM = pl.BlockSpec(memory_space=pltpu.VMEM)


def _call(body, **kw):
    return pl.pallas_call(body, **kw)


def _params(sem=None, vmem=VMEM_LIMIT):
    return pltpu.CompilerParams(dimension_semantics=sem, vmem_limit_bytes=vmem)


def _my_pos():
    return lax.axis_index("x"), lax.axis_index("y"), lax.axis_index("c")


def _idx(pos):
    return 4 * pos[0] + 2 * pos[1] + pos[2]


def _peer(k):
    x, y, c = _my_pos()
    return ((1 - x) if (k >> 2) & 1 else x, (1 - y) if (k >> 1) & 1 else y, (1 - c) if k & 1 else c)


def _exchange_vmem(src_ref, dst_ref, send_sems, recv_sems, base):
    me = _idx(_my_pos())
    sends = []
    for k in range(1, NDEV):
        cp = pltpu.make_async_remote_copy(
            src_ref=src_ref, dst_ref=dst_ref.at[me], send_sem=send_sems.at[base + k - 1],
            recv_sem=recv_sems.at[base + k - 1], device_id=_peer(k), device_id_type=MESH)
        cp.start()
        sends.append(cp)
    dst_ref[me] = src_ref[...]
    for k in range(1, NDEV):
        peer = _peer(k)
        pltpu.make_async_remote_copy(
            src_ref=src_ref, dst_ref=dst_ref.at[_idx(peer)], send_sem=send_sems.at[base + k - 1],
            recv_sem=recv_sems.at[base + k - 1], device_id=peer, device_id_type=MESH).wait_recv()
    for cp in sends:
        cp.wait_send()


def _sigmoid(z):
    return 1.0 / (1.0 + jnp.exp(-z))


def _softplus(x):
    return jnp.maximum(x, 0.0) + jnp.log1p(jnp.exp(-jnp.abs(x)))


def _one_minus_sq(a, la):
    u = 2.0 * la
    series = -(u * (1.0 + u * (0.5 + u * (1.0 / 6.0))))
    return jnp.where(u > -0.03, series, 1.0 - a * a)


def _dot(a, b):
    return jnp.dot(a, b, preferred_element_type=F32)


def _dot_nt(a, b):
    return lax.dot_general(a, b, (((1,), (1,)), ((), ())), preferred_element_type=F32)


def _rows(shape):
    return lax.broadcasted_iota(jnp.int32, shape, 0)


def _down(x, k, pos):
    return jnp.where(pos >= k, pltpu.roll(x, k, 0), 0.0)


def _up(x, k, pos, rowlen):
    return jnp.where(pos + k < rowlen, pltpu.roll(x, x.shape[0] - k, 0), 0.0)


def _pos_rowlen(shape, is_ctx):
    t = _rows(shape)
    pos = jnp.where(is_ctx, t, t & (GRID_W - 1))
    rowlen = jnp.where(is_ctx, shape[0], GRID_W)
    return pos, rowlen


def _shift_matrices(t):
    r = lax.broadcasted_iota(jnp.int32, (t, t), 0)
    c = lax.broadcasted_iota(jnp.int32, (t, t), 1)
    kinds = []
    for rowlen in (GRID_W, t):
        pos = r % rowlen
        kinds.append(jnp.stack([(c == r - 2) & (pos >= 2), (c == r - 1) & (pos >= 1),
                                (c == r + 1) & (pos + 1 < rowlen), (c == r + 2) & (pos + 2 < rowlen)]))
    return jnp.stack(kinds).astype(BF16)


def _conv4(v16, sm_ref, w_ref):
    taps = (_dot(sm_ref[0], v16), _dot(sm_ref[1], v16), v16.astype(F32), _dot(sm_ref[2], v16))
    out = w_ref[0:1, :] * taps[0] + w_ref[1:2, :] * taps[1] + w_ref[2:3, :] * taps[2] + w_ref[3:4, :] * taps[3]
    return out, taps


def _conv4_t(dy, sm_ref, w_ref):
    dy16 = dy.astype(BF16)
    return (w_ref[0:1, :] * _dot(sm_ref[3], dy16) + w_ref[1:2, :] * _dot(sm_ref[2], dy16)
            + w_ref[2:3, :] * dy + w_ref[3:4, :] * _dot(sm_ref[1], dy16))


def _conv3(t, w_ref, pos, rowlen):
    return w_ref[0:1, :] * _down(t, 1, pos) + w_ref[1:2, :] * t + w_ref[2:3, :] * _up(t, 1, pos, rowlen)


def _conv3_t(dz, w_ref, pos, rowlen):
    return w_ref[0:1, :] * _up(dz, 1, pos, rowlen) + w_ref[1:2, :] * dz + w_ref[2:3, :] * _down(dz, 1, pos)


def _chunk_scan(a, b, reverse):
    row = _rows(a.shape)
    for s in (1, 2, 4):
        if reverse:
            m = row < 8 - s
            sh = 8 - s
        else:
            m = row >= s
            sh = s
        a_s = jnp.where(m, pltpu.roll(a, sh, 0), 1.0)
        b_s = jnp.where(m, pltpu.roll(b, sh, 0), 0.0)
        b = b + a * b_s
        a = a * a_s
    return a, b


def _chain_segments(ptot, hend, carry, reverse):
    ca, cb = _chunk_scan(ptot, hend, reverse)
    incl = ca * carry + cb
    r8 = _rows(incl.shape)
    if reverse:
        start = jnp.where(r8 < 7, pltpu.roll(incl, 7, 0), carry)
        last = incl[0:1, :]
    else:
        start = jnp.where(r8 >= 1, pltpu.roll(incl, 1, 0), carry)
        last = incl[7:8, :]
    return start, jnp.broadcast_to(last, incl.shape)


def _seg_rows(j, seg):
    return pl.ds(j, 8, stride=seg)


def _to_lanes(ref, x):
    for c in range(ref.shape[0]):
        ref[c] = x[:, c * 128:(c + 1) * 128]


def _from_lanes(ref):
    return jnp.concatenate([ref[c] for c in range(ref.shape[0])], axis=-1)


def _scan_tile(a_ref, b_ref, out_ref, carry_ref, reverse):
    nl, t, _ = a_ref.shape
    seg = t // 8
    lanes = range(nl)

    def at(k):
        return _seg_rows(seg - 1 - k if reverse else k, seg)

    def local(k, hp):
        h, p = hp
        a = [a_ref[c, at(k), :] for c in lanes]
        return ([a[c] * h[c] + b_ref[c, at(k), :] for c in lanes], [a[c] * p[c] for c in lanes])

    zeros = [jnp.zeros((8, 128), F32) for _ in lanes]
    ones = [jnp.ones((8, 128), F32) for _ in lanes]
    hend, ptot = lax.fori_loop(0, seg, local, (zeros, ones), unroll=2)
    start = []
    for c in lanes:
        st, carry_ref[c] = _chain_segments(ptot[c], hend[c], carry_ref[c], reverse)
        start.append(st)

    def final(k, h):
        h = [a_ref[c, at(k), :] * h[c] + b_ref[c, at(k), :] for c in lanes]
        for c in lanes:
            out_ref[c, at(k), :] = h[c]
        return h

    lax.fori_loop(0, seg, final, start, unroll=2)


def _scan_tile_backward(a_ref, dh_ref, g_ref, carry_ref, reverse):
    nl, t, _ = a_ref.shape
    seg = t // 8
    lanes = range(nl)

    def at(k):
        return _seg_rows(seg - 1 - k if reverse else k, seg)

    def local(k, up):
        u, p = up
        a = [a_ref[c, at(k), :] for c in lanes]
        return ([a[c] * (dh_ref[c, at(k), :] + u[c]) for c in lanes], [a[c] * p[c] for c in lanes])

    zeros = [jnp.zeros((8, 128), F32) for _ in lanes]
    ones = [jnp.ones((8, 128), F32) for _ in lanes]
    uend, ptot = lax.fori_loop(0, seg, local, (zeros, ones), unroll=2)
    start = []
    for c in lanes:
        st, carry_ref[c] = _chain_segments(ptot[c], uend[c], carry_ref[c], reverse)
        start.append(st)

    def final(k, u):
        g = [dh_ref[c, at(k), :] + u[c] for c in lanes]
        for c in lanes:
            g_ref[c, at(k), :] = g[c]
        return [a_ref[c, at(k), :] * g[c] for c in lanes]

    lax.fori_loop(0, seg, final, start, unroll=2)


def _lru_coef(xb, wg_ref, d, ba, bx, lam, gc):
    w = xb.shape[1]
    xb16 = xb.astype(BF16)
    zr, zi = [], []
    for g in range(w // gc):
        z = _dot(xb16[:, g * gc:(g + 1) * gc], wg_ref[d, g])
        zr.append(z[:, :gc])
        zi.append(z[:, gc:])
    zr = zr[0] if len(zr) == 1 else jnp.concatenate(zr, axis=-1)
    zi = zi[0] if len(zi) == 1 else jnp.concatenate(zi, axis=-1)
    r = _sigmoid(zr + ba)
    ig = _sigmoid(zi + bx)
    sp = _softplus(-lam)
    la = r * (-LRU_C * sp)
    a = jnp.exp(la)
    s = jnp.sqrt(_one_minus_sq(a, la))
    return a, s, r, ig, sp


def _adamw(w, g, m, v):
    m2 = ADAM_B1 * m + (1.0 - ADAM_B1) * g
    v2 = ADAM_B2 * v + (1.0 - ADAM_B2) * (g * g)
    m_hat = m2 / ADAM_C1
    v_hat = v2 / ADAM_C2
    delta = -ADAM_LR * (m_hat / (jnp.sqrt(v_hat) + ADAM_EPS) + ADAM_WD * w)
    return delta, m2, v2


def _mod_forward(c8, cctx8, w_ada, small):
    d = c8.shape[1]
    cols = w_ada.shape[1]

    def body(c_ref, cctx_ref, w_ref, sm_ref, mod_ref, s_ref, sm_all, cbuf, mod_my, send_sems, recv_sems):
        _exchange_vmem(sm_ref, sm_all, send_sems, recv_sems, 2 * (NDEV - 1))
        _exchange_vmem(c_ref, cbuf, send_sems, recv_sems, 0)
        row = _rows((8, d))
        c_all = jnp.zeros((8, d), F32)
        for b in range(NDEV):
            c_all = jnp.where(row == b, cbuf[b], c_all)
        cc = cctx_ref[...]
        s_top = c_all * _sigmoid(c_all)
        s_bot = jnp.where(row == 0, cc * _sigmoid(cc), 0.0)
        s = jnp.concatenate([s_top, s_bot], axis=0)
        s_ref[...] = s
        mod_my[...] = jnp.dot(s, w_ref[...], precision=HIGHEST, preferred_element_type=F32)
        _exchange_vmem(mod_my, mod_ref, send_sems, recv_sems, NDEV - 1)

    return _call(
        body, name="mod_forward",
        out_shape=(jax.ShapeDtypeStruct((NDEV, 16, cols), F32), jax.ShapeDtypeStruct((16, d), F32),
                   jax.ShapeDtypeStruct((NDEV,) + small.shape, F32)),
        in_specs=[VMEM] * 4, out_specs=(VMEM,) * 3,
        scratch_shapes=[pltpu.VMEM((NDEV, 8, d), F32), pltpu.VMEM((16, cols), F32),
                        pltpu.SemaphoreType.DMA((3 * (NDEV - 1),)), pltpu.SemaphoreType.DMA((3 * (NDEV - 1),))],
        compiler_params=_params(),
    )(c8, cctx8, w_ada, small)


def _gather_weights(shards):
    n = len(shards)

    def body(*refs):
        ins, outs = refs[:n], refs[n:2 * n]
        send_sems, recv_sems, local_sems = refs[2 * n:]
        x, y, c = _my_pos()
        me, sibling = (x, y, c), (x, y, 1 - c)
        chips = [(1 - x, y), (x, 1 - y), (1 - x, 1 - y)]
        waits = []
        for a in range(n):
            src, out = ins[a], outs[a]

            def copy(k, block, to, from_src=False, src=src, out=out, a=a):
                return pltpu.make_async_remote_copy(
                    src_ref=src if from_src else out.at[_idx(block)], dst_ref=out.at[_idx(block)],
                    send_sem=send_sems.at[7 * a + k], recv_sem=recv_sems.at[7 * a + k],
                    device_id=to, device_id_type=MESH)

            mine = pltpu.make_async_copy(src, out.at[_idx(me)], local_sems.at[a])
            mine.start()
            first = [copy(0, me, sibling, True)]
            first += [copy(1 + j, me, (*chip, c), True) for j, chip in enumerate(chips)]
            for cp in first:
                cp.start()
            waits.append((copy, mine, first))
        for a in range(n):
            copy, mine, first = waits[a]
            passed = [copy(4 + j, (*chip, c), sibling) for j, chip in enumerate(chips)]
            for j, chip in enumerate(chips):
                copy(1 + j, (*chip, c), me).wait_recv()
                passed[j].start()
            copy(0, sibling, me).wait_recv()
            for j, chip in enumerate(chips):
                copy(4 + j, (*chip, 1 - c), me).wait_recv()
            for cp in first + passed:
                cp.wait_send()
            mine.wait()

    return _call(
        body, name="gather_weights",
        out_shape=tuple(jax.ShapeDtypeStruct((NDEV,) + s.shape, s.dtype) for s in shards),
        in_specs=[ANY] * n, out_specs=tuple([ANY] * n),
        scratch_shapes=[pltpu.SemaphoreType.DMA((7 * n,)), pltpu.SemaphoreType.DMA((7 * n,)),
                        pltpu.SemaphoreType.DMA((n,))],
        compiler_params=_params(),
    )(*shards)


def _scatter_partials(parts):
    n = len(parts)

    def body(*refs):
        ins, outs = refs[:n], refs[n:2 * n]
        send_sems, recv_sems, local_sems = refs[2 * n:]
        me = _idx(_my_pos())
        sends = []
        for a in range(n):
            mine = pltpu.make_async_copy(ins[a].at[me], outs[a].at[0], local_sems.at[a])
            mine.start()
            sends.append(mine)
        for k in range(1, NDEV):
            peer = _peer(k)
            for a in range(n):
                cp = pltpu.make_async_remote_copy(
                    src_ref=ins[a].at[_idx(peer)], dst_ref=outs[a].at[k],
                    send_sem=send_sems.at[7 * a + k - 1], recv_sem=recv_sems.at[7 * a + k - 1],
                    device_id=peer, device_id_type=MESH)
                cp.start()
                sends.append(cp)
        for k in range(1, NDEV):
            for a in range(n):
                pltpu.make_async_remote_copy(
                    src_ref=ins[a].at[0], dst_ref=outs[a].at[k],
                    send_sem=send_sems.at[7 * a + k - 1], recv_sem=recv_sems.at[7 * a + k - 1],
                    device_id=_peer(k), device_id_type=MESH).wait_recv()
        for cp in sends[:n]:
            cp.wait()
        for cp in sends[n:]:
            cp.wait_send()

    return _call(
        body, name="scatter_partials",
        out_shape=tuple(jax.ShapeDtypeStruct(p.shape, p.dtype) for p in parts),
        in_specs=[ANY] * n, out_specs=tuple([ANY] * n),
        scratch_shapes=[pltpu.SemaphoreType.DMA((7 * n,)), pltpu.SemaphoreType.DMA((7 * n,)),
                        pltpu.SemaphoreType.DMA((n,))],
        compiler_params=_params(),
    )(*parts)


def _reduce_small(packed, lru_parts, w_ada, dsilu_cctx):
    rp = packed.shape[0]
    rl = lru_parts.shape[1]
    d, cols = w_ada.shape
    assert cols % 128 == 0
    cb = cols // 128

    def body(p_ref, l_ref, w_ref, ds_ref, sum_ref, all_ref, lru_ref, cctx_ref,
             lsum, cpart, call, send_sems, recv_sems):
        me = _idx(_my_pos())
        _exchange_vmem(p_ref, all_ref, send_sems, recv_sems, 0)
        acc = all_ref[0]
        for j in range(1, NDEV):
            acc = acc + all_ref[j]
        sum_ref[...] = acc
        red = l_ref[0]
        for k in range(1, NDEV):
            red = red + l_ref[k]
        lsum[...] = red
        _exchange_vmem(lsum, lru_ref, send_sems, recv_sems, NDEV - 1)
        part = jnp.zeros((8, d), F32)
        for q in range(cb):
            dm = jnp.broadcast_to(sum_ref[pl.ds((NDEV + me) * cb + q, 1), :], (8, 128))
            part = part + lax.dot_general(dm, w_ref[:, q * 128:(q + 1) * 128],
                                          (((1,), (1,)), ((), ())), precision=HIGHEST,
                                          preferred_element_type=F32)
        cpart[...] = part
        _exchange_vmem(cpart, call, send_sems, recv_sems, 2 * (NDEV - 1))
        tot = call[0]
        for j in range(1, NDEV):
            tot = tot + call[j]
        cctx_ref[...] = tot * ds_ref[...]

    return _call(
        body, name="reduce_small",
        out_shape=(jax.ShapeDtypeStruct((rp, 128), F32), jax.ShapeDtypeStruct((NDEV, rp, 128), F32),
                   jax.ShapeDtypeStruct((NDEV, rl, 128), F32), jax.ShapeDtypeStruct((8, d), F32)),
        in_specs=[VMEM] * 4, out_specs=(VMEM,) * 4,
        scratch_shapes=[pltpu.VMEM((rl, 128), F32), pltpu.VMEM((8, d), F32), pltpu.VMEM((NDEV, 8, d), F32),
                        pltpu.SemaphoreType.DMA((3 * (NDEV - 1),)), pltpu.SemaphoreType.DMA((3 * (NDEV - 1),))],
        compiler_params=_params(),
    )(packed, lru_parts, w_ada, dsilu_cctx)


def _in_projection(src, mv, w_all, la, row0, tm, nbj, name, prev=None):
    rows, d = src.shape
    nb, _, bw = w_all.shape
    blk0 = row0 // tm

    def body(*refs):
        x_ref, mv_ref, w_ref = refs[:3]
        p_ref, ht_ref, h_s = refs[-3:]

        @pl.when(pl.program_id(1) == 0)
        def _():
            xf = x_ref[...]
            r = lax.rsqrt(jnp.mean(xf * xf, axis=-1, keepdims=True) + EPS)
            h = xf * r * (mv_ref[0:1, :] * (1.0 + mv_ref[1:2, :])) + mv_ref[2:3, :]
            h_s[...] = h.astype(BF16)
            ht_ref[...] = h.T.astype(BF16)

        h = h_s[...]
        for q in range(nbj):
            p_ref[:, q * bw:(q + 1) * bw] = _dot(h, w_ref[q]).astype(BF16)

    in_specs = [pl.BlockSpec((tm, d), lambda i, j: (i, 0)),
                pl.BlockSpec((8, d), lambda i, j: (0, 0)),
                pl.BlockSpec((nbj, d, bw), lambda i, j: (j, 0, 0))]
    args = [src, mv, w_all]
    aliases = {}
    if prev is not None:
        in_specs += [ANY, ANY]
        args += list(prev)
        aliases = {3: 0, 4: 1}
    return _call(
        body, name=name,
        grid=(rows // tm, nb // nbj),
        out_shape=(jax.ShapeDtypeStruct((la, nb * bw), BF16), jax.ShapeDtypeStruct((d, la), BF16)),
        in_specs=in_specs,
        out_specs=(pl.BlockSpec((tm, nbj * bw), lambda i, j: (blk0 + i, j)),
                   pl.BlockSpec((d, tm), lambda i, j: (0, blk0 + i))),
        scratch_shapes=[pltpu.VMEM((tm, d), BF16)],
        input_output_aliases=aliases,
        compiler_params=_params(("arbitrary", "arbitrary")),
    )(*args)


def _lru_forward(p, wg, lv, wcb, sm, l, t):
    la = p.shape[0]
    w = lv.shape[1]
    gc = wg.shape[2]
    nt = l // t

    def body(vf_ref, vr_ref, wg_ref, lv_ref, wcb_ref, sm_ref, hf_ref, hr_ref, a_s, b_s, h_s, carry):
        @pl.when(pl.program_id(0) == 0)
        def _():
            carry[...] = jnp.zeros_like(carry)

        for dr, (v_ref, h_ref) in enumerate(((vf_ref, hf_ref), (vr_ref, hr_ref))):
            xb, _ = _conv4(v_ref[...], sm_ref, wcb_ref)
            xb = xb + lv_ref[6:7, :]
            a, s, _, ig, _ = _lru_coef(xb, wg_ref, dr, lv_ref[3 * dr:3 * dr + 1, :],
                                       lv_ref[3 * dr + 1:3 * dr + 2, :], lv_ref[3 * dr + 2:3 * dr + 3, :], gc)
            _to_lanes(a_s, a)
            _to_lanes(b_s, s * (ig * xb))
            _scan_tile(a_s, b_s, h_s, carry.at[dr], dr == 1)
            h_ref[...] = _from_lanes(h_s)

    full = lambda shape: pl.BlockSpec(shape, lambda i: (0,) * len(shape))
    fmap = lambda i: (jnp.where(i == 0, nt, i - 1), 0)
    rmap = lambda i: (jnp.where(i == 0, nt, nt - i), 0)
    vcol = 4
    return _call(
        body, name="lru_forward",
        grid=(nt + 1,),
        out_shape=(jax.ShapeDtypeStruct((la, w), F32), jax.ShapeDtypeStruct((la, w), F32)),
        in_specs=[pl.BlockSpec((t, w), lambda i: (jnp.where(i == 0, nt, i - 1), vcol)),
                  pl.BlockSpec((t, w), lambda i: (jnp.where(i == 0, nt, nt - i), vcol)),
                  full(wg.shape), full(lv.shape), full(wcb.shape),
                  pl.BlockSpec((None, 4, t, t), lambda i: (jnp.where(i == 0, 1, 0), 0, 0, 0))],
        out_specs=(pl.BlockSpec((t, w), fmap), pl.BlockSpec((t, w), rmap)),
        scratch_shapes=[pltpu.VMEM((w // 128, t, 128), F32)] * 3 + [pltpu.VMEM((2, w // 128, 8, 128), F32)],
        compiler_params=_params(("arbitrary",)),
    )(p, p, wg, lv, wcb, sm)


def _mix_gates(p_refs, hf_ref, hr_ref, wca_ref, t, w):
    bl, cl, ul, gl, ql = [r[...].astype(F32) for r in p_refs]
    pos, rowlen = _pos_rowlen((t, w), False)
    tt = cl * ul
    z = _conv3(tt, wca_ref, pos, rowlen)
    sig_g = _sigmoid(gl)
    sig_q = _sigmoid(ql)
    ylru = hf_ref[...] + hr_ref[...]
    return bl, cl, ul, gl, ql, tt, z, sig_g, sig_q, ylru, pos, rowlen


def _p_specs(t, w, nt):
    return [pl.BlockSpec((t, w), functools.partial(lambda i, s: (jnp.minimum(i, nt - 1), s), s=s))
            for s in (0, 1, 2, 3, 5)]


def _mix_forward(x, tgt, p, hf, hr, wo, ov, wca, t):
    l, d = x.shape
    w = d // 2
    nt = l // t

    def body(x_ref, tg_ref, b_ref, c_ref, u_ref, g_ref, q_ref, hf_ref, hr_ref, wo_ref, ov_ref, wca_ref,
             dn_ref, ct_ref, do_ref, part_ref):
        i = pl.program_id(0)
        bl, _, _, gl, ql, _, z, sig_g, sig_q, ylru, _, _ = _mix_gates(
            (b_ref, c_ref, u_ref, g_ref, q_ref), hf_ref, hr_ref, wca_ref, t, w)
        ya = bl * z * (gl * sig_g)
        yb = ylru * (ql * sig_q)
        ct_ref[0:w, :] = ya.T.astype(BF16)
        ct_ref[w:, :] = yb.T.astype(BF16)
        out = _dot(ya.astype(BF16), wo_ref[0:w, :]) + _dot(yb.astype(BF16), wo_ref[w:, :])
        gate, fg = ov_ref[0:1, :], ov_ref[1:2, :]
        n = x_ref[...] + gate * out
        rr = lax.rsqrt(jnp.mean(n * n, axis=-1, keepdims=True) + EPS)
        nh = n * rr
        e = nh * fg - tg_ref[...]
        loss = 0.5 * jnp.sum(jnp.mean(e * e, axis=-1, keepdims=True), axis=0, keepdims=True)
        dy = e * (1.0 / d)
        dnh = dy * fg
        dn = rr * (dnh - nh * jnp.mean(dnh * nh, axis=-1, keepdims=True))
        dn_ref[...] = dn
        do_ref[...] = (dn * gate).astype(BF16)

        @pl.when(i == 0)
        def _():
            part_ref[...] = jnp.zeros_like(part_ref)

        part_ref[0:1, :] += jnp.sum(dy * nh, axis=0, keepdims=True)
        part_ref[1:2, :] += jnp.sum(dn * out, axis=0, keepdims=True)
        part_ref[2:3, :] += jnp.broadcast_to(loss, (1, d))

    tile = lambda cols: pl.BlockSpec((t, cols), lambda i: (i, 0))
    full = lambda shape: pl.BlockSpec(shape, lambda i: (0,) * len(shape))
    return _call(
        body, name="mix_forward",
        grid=(nt,),
        out_shape=(jax.ShapeDtypeStruct((l, d), F32), jax.ShapeDtypeStruct((d, l), BF16),
                   jax.ShapeDtypeStruct((l, d), BF16), jax.ShapeDtypeStruct((8, d), F32)),
        in_specs=[tile(d), tile(d)] + _p_specs(t, w, nt) + [tile(w), tile(w),
                  pl.BlockSpec((d, d), lambda i: (0, 0), pipeline_mode=pl.Buffered(1)),
                  full(ov.shape), full(wca.shape)],
        out_specs=(tile(d), pl.BlockSpec((d, t), lambda i: (0, i)), tile(d), full((8, d))),
        compiler_params=_params(("arbitrary",)),
    )(x, tgt, p, p, p, p, p, hf, hr, wo, ov, wca)


def _mix_backward(dout, p, hf, hr, wo, wca, l, t):
    d = dout.shape[1]
    w = d // 2
    nt = l // t
    la = p.shape[0]

    def body(do_ref, b_ref, c_ref, u_ref, g_ref, q_ref, hf_ref, hr_ref, wo_ref, wca_ref,
             dp_ref, dh_ref, part_ref):
        i = pl.program_id(0)

        @pl.when(i == 0)
        def _():
            part_ref[...] = jnp.zeros_like(part_ref)

        @pl.when(i == nt)
        def _():
            dp_ref[...] = jnp.zeros_like(dp_ref)

        @pl.when(i < nt)
        def _():
            bl, cl, ul, gl, ql, tt, z, sig_g, sig_q, ylru, pos, rowlen = _mix_gates(
                (b_ref, c_ref, u_ref, g_ref, q_ref), hf_ref, hr_ref, wca_ref, t, w)
            do = do_ref[...]
            dya = _dot_nt(do, wo_ref[0:w, :])
            dyb = _dot_nt(do, wo_ref[w:, :])
            sg = gl * sig_g
            dz = dya * bl * sg
            dt = _conv3_t(dz, wca_ref, pos, rowlen)
            dp_ref[:, 0:w] = (dya * z * sg).astype(BF16)
            dp_ref[:, w:2 * w] = (dt * ul).astype(BF16)
            dp_ref[:, 2 * w:3 * w] = (dt * cl).astype(BF16)
            dp_ref[:, 3 * w:4 * w] = (dya * bl * z * (sig_g * (1.0 + gl * (1.0 - sig_g)))).astype(BF16)
            dp_ref[:, 4 * w:5 * w] = jnp.zeros((t, w), BF16)
            dp_ref[:, 5 * w:6 * w] = (dyb * ylru * (sig_q * (1.0 + ql * (1.0 - sig_q)))).astype(BF16)
            dh_ref[...] = dyb * (ql * sig_q)
            part_ref[0:1, :] += jnp.sum(dz * _down(tt, 1, pos), axis=0, keepdims=True)
            part_ref[1:2, :] += jnp.sum(dz * tt, axis=0, keepdims=True)
            part_ref[2:3, :] += jnp.sum(dz * _up(tt, 1, pos, rowlen), axis=0, keepdims=True)

    clamp = lambda cols: pl.BlockSpec((t, cols), lambda i: (jnp.minimum(i, nt - 1), 0))
    full = lambda shape: pl.BlockSpec(shape, lambda i: (0,) * len(shape))
    return _call(
        body, name="mix_backward",
        grid=(nt + 1,),
        out_shape=(jax.ShapeDtypeStruct((la, 6 * w), BF16), jax.ShapeDtypeStruct((l, w), F32),
                   jax.ShapeDtypeStruct((8, w), F32)),
        in_specs=[clamp(d)] + _p_specs(t, w, nt) + [clamp(w), clamp(w),
                  pl.BlockSpec((d, d), lambda i: (0, 0), pipeline_mode=pl.Buffered(1)), full(wca.shape)],
        out_specs=(pl.BlockSpec((t, 6 * w), lambda i: (i, 0)), clamp(w), full((8, w))),
        compiler_params=_params(("arbitrary",)),
    )(dout, p, p, p, p, p, hf, hr, wo, wca)


def _lru_backward(direction, p, dhs, hs, wg, lv, wcb, sm, l, t, dxb_other=None, dp=None):
    la, w = hs.shape
    gc = wg.shape[2]
    ng = w // gc
    nt = l // t
    nblk8 = la // 8
    last = direction == 1

    if direction == 0:
        tile = lambda i: jnp.where(i == nt, nt, nt - 1 - i)
        halo = lambda i: jnp.where(tile(i) == 0, nblk8 - 1, tile(i) * (t // 8) - 1)
    else:
        tile = lambda i: i
        halo = lambda i: jnp.minimum((i + 1) * (t // 8), nblk8 - 1)

    def body(*refs):
        if last:
            (v_ref, dh_ref, hs_ref, halo_ref, wg_ref, lv_ref, wcb_ref, sm_ref, dxo_ref, _,
             out_ref, dwg_ref, part_ref, a_s, dh_s, g_s, carry) = refs
        else:
            (v_ref, dh_ref, hs_ref, halo_ref, wg_ref, lv_ref, wcb_ref, sm_ref,
             out_ref, dwg_ref, part_ref, a_s, dh_s, g_s, carry) = refs
        i = pl.program_id(0)
        is_ctx = i == nt

        @pl.when(i == 0)
        def _():
            carry[...] = jnp.zeros_like(carry)
            dwg_ref[...] = jnp.zeros_like(dwg_ref)
            part_ref[...] = jnp.zeros_like(part_ref)

        row = _rows((t, w))
        xb, taps = _conv4(v_ref[...], sm_ref, wcb_ref)
        xb = xb + lv_ref[6:7, :]
        lam = lv_ref[3 * direction + 2:3 * direction + 3, :]
        a, s, r, ig, sp = _lru_coef(xb, wg_ref, direction, lv_ref[3 * direction:3 * direction + 1, :],
                                    lv_ref[3 * direction + 1:3 * direction + 2, :], lam, gc)
        hs_t = hs_ref[...]
        if direction == 0:
            edge = jnp.where(is_ctx, 0.0, halo_ref[7:8, :])
            hprev = jnp.where(row == 0, edge, pltpu.roll(hs_t, 1, 0))
        else:
            edge = jnp.where(is_ctx, 0.0, halo_ref[0:1, :])
            hprev = jnp.where(row == t - 1, edge, pltpu.roll(hs_t, t - 1, 0))
        _to_lanes(a_s, a)
        _to_lanes(dh_s, jnp.where(is_ctx, 0.0, dh_ref[...]))
        _scan_tile_backward(a_s, dh_s, g_s, carry, direction == 0)

        g = _from_lanes(g_s)
        ix = ig * xb
        gs = g * s
        dla = (g * a) * (hprev - ix * (a / s))
        dxb = gs * ig
        dzr = dla * (r * (1.0 - r)) * (-LRU_C * sp)
        dzi = gs * ix * (1.0 - ig)
        part_ref[0:1, :] += jnp.sum(dzr, axis=0, keepdims=True)
        part_ref[1:2, :] += jnp.sum(dzi, axis=0, keepdims=True)
        part_ref[2:3, :] += jnp.sum(dla * r, axis=0, keepdims=True) * (LRU_C * _sigmoid(-lam))
        pieces = []
        for gi in range(ng):
            sl = slice(gi * gc, (gi + 1) * gc)
            dz = jnp.concatenate([dzr[:, sl], dzi[:, sl]], axis=-1).astype(BF16)
            pieces.append(_dot_nt(dz, wg_ref[direction, gi]))
            dwg_ref[gi] += _dot(xb[:, sl].T.astype(BF16), dz)
        dxb = dxb + (pieces[0] if ng == 1 else jnp.concatenate(pieces, axis=-1))
        if not last:
            out_ref[...] = dxb
        else:
            dxb = dxb + dxo_ref[...]
            out_ref[...] = _conv4_t(dxb, sm_ref, wcb_ref).astype(BF16)
            part_ref[3:4, :] += jnp.sum(dxb, axis=0, keepdims=True)
            for j in range(4):
                part_ref[4 + j:5 + j, :] += jnp.sum(dxb * taps[j], axis=0, keepdims=True)

    full = lambda shape: pl.BlockSpec(shape, lambda i: (0,) * len(shape))
    in_specs = [pl.BlockSpec((t, w), lambda i: (tile(i), 4)),
                pl.BlockSpec((t, w), lambda i: (jnp.minimum(tile(i), nt - 1), 0)),
                pl.BlockSpec((t, w), lambda i: (tile(i), 0)),
                pl.BlockSpec((8, w), lambda i: (halo(i), 0)),
                full(wg.shape), full(lv.shape), full(wcb.shape),
                pl.BlockSpec((None, 4, t, t), lambda i: (jnp.where(i == nt, 1, 0), 0, 0, 0))]
    args = [p, dhs, hs, hs, wg, lv, wcb, sm]
    if last:
        in_specs += [pl.BlockSpec((t, w), lambda i: (tile(i), 0)), ANY]
        args += [dxb_other, dp]
        out0 = jax.ShapeDtypeStruct(dp.shape, dp.dtype)
        spec0 = pl.BlockSpec((t, w), lambda i: (tile(i), 4))
        aliases = {9: 0}
    else:
        out0 = jax.ShapeDtypeStruct((la, w), F32)
        spec0 = pl.BlockSpec((t, w), lambda i: (tile(i), 0))
        aliases = {}
    return _call(
        body, name="lru_backward_%d" % direction,
        grid=(nt + 1,),
        out_shape=(out0, jax.ShapeDtypeStruct((ng, gc, 2 * gc), F32), jax.ShapeDtypeStruct((8, w), F32)),
        in_specs=in_specs,
        out_specs=(spec0, full((ng, gc, 2 * gc)), full((8, w))),
        scratch_shapes=[pltpu.VMEM((w // 128, t, 128), F32)] * 3 + [pltpu.VMEM((w // 128, 8, 128), F32)],
        input_output_aliases=aliases,
        compiler_params=_params(("arbitrary",)),
    )(*args)


def _weight_grad_t(at, b, nblk_m, nblk_n, tk, name):
    m, k = at.shape
    n = b.shape[1]
    bm, bn = m // nblk_m, n // nblk_n
    nk = k // tk

    def body(a_ref, b_ref, o_ref, acc):
        kk = pl.program_id(2)

        @pl.when(kk == 0)
        def _():
            acc[...] = jnp.zeros_like(acc)

        acc[...] += _dot(a_ref[...], b_ref[...])

        @pl.when(kk == nk - 1)
        def _():
            o_ref[...] = acc[...].astype(BF16)

    return _call(
        body, name=name,
        grid=(nblk_m, nblk_n, nk),
        out_shape=jax.ShapeDtypeStruct((nblk_m * nblk_n, bm, bn), BF16),
        in_specs=[pl.BlockSpec((bm, tk), lambda i, j, kk: (i, kk)),
                  pl.BlockSpec((tk, bn), lambda i, j, kk: (kk, j))],
        out_specs=pl.BlockSpec((None, bm, bn), lambda i, j, kk: (i * nblk_n + j, 0, 0)),
        scratch_shapes=[pltpu.VMEM((bm, bn), F32)],
        compiler_params=_params(("arbitrary", "arbitrary", "arbitrary")),
    )(at, b)


def _input_backward(dp, w_all, src, mv, row0, tm, nbk, name, dn=None):
    rows, d = src.shape
    nb, _, bw = w_all.shape
    nk = nb // nbk
    blk0 = row0 // tm
    latent = dn is not None

    def body(*refs):
        dp_ref, w_ref, x_ref, mv_ref = refs[:4]
        part_ref, acc = refs[-2:]
        i, k = pl.program_id(0), pl.program_id(1)

        @pl.when((i == 0) & (k == 0))
        def _():
            part_ref[...] = jnp.zeros_like(part_ref)

        step = _dot_nt(dp_ref[:, 0:bw], w_ref[0])
        for q in range(1, nbk):
            step = step + _dot_nt(dp_ref[:, q * bw:(q + 1) * bw], w_ref[q])

        @pl.when(k == 0)
        def _():
            acc[...] = step

        @pl.when(k > 0)
        def _():
            acc[...] += step

        @pl.when(k == nk - 1)
        def _():
            xf = x_ref[...]
            r = lax.rsqrt(jnp.mean(xf * xf, axis=-1, keepdims=True) + EPS)
            xn = xf * r
            dhl = acc[...]
            gain, sc = mv_ref[0:1, :], mv_ref[1:2, :]
            dhx = jnp.sum(dhl * xn, axis=0, keepdims=True)
            part_ref[0:1, :] += jnp.sum(dhl, axis=0, keepdims=True)
            part_ref[1:2, :] += dhx * gain
            part_ref[2:3, :] += dhx * (1.0 + sc)
            if latent:
                dxn = dhl * (gain * (1.0 + sc))
                refs[5][...] = refs[4][...] + r * (dxn - xn * jnp.mean(dxn * xn, axis=-1, keepdims=True))

    tile = pl.BlockSpec((tm, d), lambda i, k: (i, 0))
    vec = pl.BlockSpec((8, d), lambda i, k: (0, 0))
    return _call(
        body, name=name,
        grid=(rows // tm, nk),
        out_shape=((jax.ShapeDtypeStruct((rows, d), F32),) if latent else ()) + (jax.ShapeDtypeStruct((8, d), F32),),
        in_specs=[pl.BlockSpec((tm, nbk * bw), lambda i, k: (blk0 + i, k)),
                  pl.BlockSpec((nbk, d, bw), lambda i, k: (k, 0, 0)), tile, vec] + ([tile] if latent else []),
        out_specs=((tile,) if latent else ()) + (vec,),
        scratch_shapes=[pltpu.VMEM((tm, d), F32)],
        compiler_params=_params(("arbitrary", "arbitrary")),
    )(*([dp, w_all, src, mv] + ([dn] if latent else [])))


def _adamw_scattered(parts, w, m, v, tr):
    r, c = w.shape

    def body(p_ref, w_ref, m_ref, v_ref, g_ref, d_ref, m2_ref, v2_ref):
        g = p_ref[0].astype(F32)
        for k in range(1, NDEV):
            g = g + p_ref[k].astype(F32)
        g_ref[...] = g
        d_ref[...], m2_ref[...], v2_ref[...] = _adamw(w_ref[...], g, m_ref[...], v_ref[...])

    tile = pl.BlockSpec((tr, c), lambda i: (i, 0))
    return _call(
        body, name="adamw_scattered_%dx%d" % (r, c),
        grid=(r // tr,),
        out_shape=tuple(jax.ShapeDtypeStruct((r, c), F32) for _ in range(4)),
        in_specs=[pl.BlockSpec((NDEV, tr, c), lambda i: (0, i, 0)), tile, tile, tile],
        out_specs=(tile,) * 4,
        compiler_params=_params(("arbitrary",)),
    )(parts, w, m, v)


def _adamw_ada(st, dmod, w, m, v, tr):
    r, c = w.shape

    def body(s_ref, dm_ref, w_ref, m_ref, v_ref, g_ref, d_ref, m2_ref, v2_ref):
        g = jnp.dot(s_ref[...], dm_ref[...], precision=HIGHEST, preferred_element_type=F32)
        g_ref[...] = g
        d_ref[...], m2_ref[...], v2_ref[...] = _adamw(w_ref[...], g, m_ref[...], v_ref[...])

    tile = pl.BlockSpec((tr, c), lambda i: (i, 0))
    return _call(
        body, name="adamw_ada",
        grid=(r // tr,),
        out_shape=tuple(jax.ShapeDtypeStruct((r, c), F32) for _ in range(4)),
        in_specs=[pl.BlockSpec((tr, 16), lambda i: (i, 0)), pl.BlockSpec((16, c), lambda i: (0, 0)),
                  tile, tile, tile],
        out_specs=(tile,) * 4,
        compiler_params=_params(("arbitrary",)),
    )(st, dmod, w, m, v)


def _adamw_packed(g, w, m, v):
    def body(g_ref, w_ref, m_ref, v_ref, d_ref, m2_ref, v2_ref):
        d_ref[...], m2_ref[...], v2_ref[...] = _adamw(w_ref[...], g_ref[...], m_ref[...], v_ref[...])

    return _call(
        body, name="adamw_packed",
        out_shape=tuple(jax.ShapeDtypeStruct(w.shape, F32) for _ in range(3)),
        in_specs=[VMEM] * 4, out_specs=(VMEM,) * 3,
        compiler_params=_params(),
    )(g, w, m, v)


def _blockdiag_groups(wh, gc):
    h, dh, _ = wh.shape
    g = gc // dh
    w4 = wh.reshape(h // g, g, dh, dh)
    bd = jnp.einsum("ngij,gh->ngihj", w4, jnp.eye(g, dtype=wh.dtype))
    return bd.reshape(h // g, gc, gc)


def _blockdiag_extract(bd, dh):
    ng, gc, _ = bd.shape
    g = gc // dh
    x = bd.reshape(ng, g, dh, g, dh)
    return jnp.einsum("ngihj,gh->ngij", x, jnp.eye(g, dtype=bd.dtype)).reshape(ng * g, dh, dh)


def _rows8(*vecs):
    rows = [jnp.reshape(v, (1, -1)).astype(F32) for v in vecs]
    n = rows[0].shape[1]
    return jnp.concatenate(rows + [jnp.zeros((8 - len(rows), n), F32)], axis=0)


def _pack(pieces):
    flat = jnp.concatenate([jnp.reshape(a, (-1,)).astype(F32) for a in pieces])
    total = -(-flat.shape[0] // 1024) * 1024
    return jnp.pad(flat, (0, total - flat.shape[0])).reshape(total // 128, 128)


def _unpack(packed, shapes):
    flat = packed.reshape(-1)
    out, off = [], 0
    for s in shapes:
        n = 1
        for q in s:
            n *= q
        out.append(flat[off:off + n].reshape(s))
        off += n
    return out


def kernel(x, c, ctx, c_ctx, norm_g, w_ada, b_ada, w_in, w_conv_a, w_conv_b, b_conv_b, lru_wa, lru_ba, lru_wx, lru_bx, lru_lambda, w_out, final_g, loss_target, m_c_ctx, m_norm_g, m_w_ada, m_b_ada, m_w_in, m_w_conv_a, m_w_conv_b, m_b_conv_b, m_lru_wa, m_lru_ba, m_lru_wx, m_lru_bx, m_lru_lambda, m_w_out, m_final_g, v_c_ctx, v_norm_g, v_w_ada, v_b_ada, v_w_in, v_w_conv_a, v_w_conv_b, v_b_conv_b, v_lru_wa, v_lru_ba, v_lru_wx, v_lru_bx, v_lru_lambda, v_w_out, v_final_g):
    _, l, d = x.shape
    lc = ctx.shape[1]
    w = d // 2
    t = lc
    assert l % t == 0 and t % GRID_W == 0 and t % 128 == 0
    dh = w // N_HEADS
    gc = min(w, MXU_WIDTH)
    cols = w_ada.shape[2]
    wo_rows = w_out.shape[1]
    me = _idx(_my_pos())
    x2, ctx2, tgt2 = x[0], ctx[0], loss_target[0]
    w_ada2, w_in2, w_out2 = w_ada[0], w_in[0], w_out[0]

    small_mine = jnp.concatenate([w_conv_a[0], w_conv_b[0], lru_ba[0], lru_bx[0], lru_lambda[0],
                                  jnp.zeros((3, w // NDEV), F32)], axis=0)
    mod_all, s_mat, small_all = _mod_forward(
        jnp.broadcast_to(c, (8, d)), jnp.broadcast_to(c_ctx[None], (8, d)), w_ada2, small_mine)
    mod = jnp.transpose(mod_all, (1, 0, 2)).reshape(16, NDEV * cols) + b_ada
    mod_lat = lax.dynamic_slice_in_dim(mod, me, 1, axis=0)
    sh_l, sc_l, gt_l = jnp.split(mod_lat, 3, axis=-1)
    sh_c, sc_c, _ = jnp.split(mod[8:9], 3, axis=-1)
    small = jnp.transpose(small_all, (1, 0, 2)).reshape(16, w)
    wca = _rows8(*[small[j] for j in range(0, 3)])
    wcb = _rows8(*[small[j] for j in range(3, 7)])
    lv = _rows8(small[7], small[9], small[11], small[8], small[10], small[12], b_conv_b)
    w_all, wo_all = _gather_weights([w_in2.astype(BF16), w_out2.astype(BF16)])
    wo = wo_all.reshape(d, d)
    wg = jnp.stack([
        jnp.concatenate([_blockdiag_groups(lru_wa[0, dr], gc), _blockdiag_groups(lru_wx[0, dr], gc)], axis=-1)
        for dr in range(2)]).astype(BF16)

    la = l + lc
    tm = 2 * t if l % (2 * t) == 0 else t
    p, hlt = _in_projection(x2, _rows8(norm_g, sc_l, sh_l), w_all, la, 0, tm, 2, "in_projection")
    p, hlt = _in_projection(ctx2, _rows8(norm_g, sc_c, sh_c), w_all, la, l, t, 2, "in_projection_ctx", prev=(p, hlt))
    sm = _shift_matrices(t)
    hf, hr = _lru_forward(p, wg, lv, wcb, sm, l, t)
    dn, catt, dout, part_mix = _mix_forward(x2, tgt2, p, hf, hr, wo, _rows8(gt_l, final_g), wca, t)
    dp, dhs, part_ca = _mix_backward(dout, p, hf, hr, wo, wca, l, t)
    dxb0, dwg0, part_l0 = _lru_backward(0, p, dhs, hf, wg, lv, wcb, sm, l, t)
    dp, dwg1, part_l1 = _lru_backward(1, p, dhs, hr, wg, lv, wcb, sm, l, t, dxb_other=dxb0, dp=dp)
    tk = 3 * t if la % (3 * t) == 0 else t
    g_wout = _weight_grad_t(catt, dout, 2, 1, 4 * t if l % (4 * t) == 0 else t, "grad_w_out")
    g_wout = g_wout.reshape(NDEV, wo_rows, d)
    g_win = _weight_grad_t(hlt, dp, 1, NDEV, tk, "grad_w_in")
    grad_x, part_lat = _input_backward(dp, w_all, x2, _rows8(norm_g, sc_l), 0, tm, 2, "input_backward", dn=dn)
    (part_ctx,) = _input_backward(dp, w_all, ctx2, _rows8(norm_g, sc_c), l, t, 2, "input_backward_ctx")
    part_in = jnp.concatenate([part_lat[0:2], part_ctx[0:2], (part_lat[2] + part_ctx[2])[None]], axis=0)

    dwa = jnp.stack([_blockdiag_extract(dwg0[:, :, :gc], dh), _blockdiag_extract(dwg1[:, :, :gc], dh)])
    dwx = jnp.stack([_blockdiag_extract(dwg0[:, :, gc:], dh), _blockdiag_extract(dwg1[:, :, gc:], dh)])
    lru_part = jnp.stack([dwa, dwx]).reshape(NDEV, -1, 128)
    sc_win, sc_wout, sc_lru = _scatter_partials([g_win, g_wout, lru_part])
    zeros_d = jnp.zeros((d,), F32)
    pieces = [
        jnp.concatenate([part_in[0], part_in[1], part_mix[1]]),
        jnp.concatenate([part_in[2], part_in[3], zeros_d]),
        part_in[4], part_mix[0], part_ca[0:3], part_l1[4:8], part_l1[3],
        jnp.stack([part_l0[0], part_l1[0]]), jnp.stack([part_l0[1], part_l1[1]]),
        jnp.stack([part_l0[2], part_l1[2]]), part_mix[2, 0:1],
    ]
    shapes = [(3 * d,), (3 * d,), (d,), (d,), (3, w), (4, w), (w,), (2, w), (2, w), (2, w), (1,)]
    sig_cc = jax.nn.sigmoid(c_ctx)
    dsilu_cc = jnp.broadcast_to((sig_cc * (1.0 + c_ctx * (1.0 - sig_cc)))[None], (8, d))
    psum, pall, lru_sum, g_cctx8 = _reduce_small(_pack(pieces), sc_lru, w_ada2, dsilu_cc)
    (g_modl, g_modc, g_norm, g_final, g_ca, g_cb, g_bcb, g_ba, g_bx, g_lam, loss1) = _unpack(psum, shapes)
    loss = loss1[0]
    g_cctx = g_cctx8[0]
    g_bada = (g_modl + g_modc)[None]
    g_lru = lru_sum.reshape(2, 2, N_HEADS, dh, dh)
    g_wa, g_wx = g_lru[0][None], g_lru[1][None]
    wsl = w // NDEV
    mine = lambda a: lax.dynamic_slice_in_dim(a, me * wsl, wsl, axis=-1)
    g_ca_m, g_cb_m, g_ba_m, g_bx_m, g_lam_m = (mine(g_ca)[None], mine(g_cb)[None], mine(g_ba)[None],
                                               mine(g_bx)[None], mine(g_lam)[None])
    g_norm, g_bcb = g_norm[None], g_bcb[None]

    cb = cols // 128
    per_dev = pall[:, :3 * d // 128].reshape(NDEV, NDEV, cols)
    dmod_lat = lax.dynamic_slice_in_dim(per_dev, me, 1, axis=1)[:, 0]
    dmod_ctx = lax.dynamic_slice_in_dim(g_modc.reshape(NDEV, cols), me, 1, axis=0)
    dmod16 = jnp.concatenate([dmod_lat, dmod_ctx, jnp.zeros((7, cols), F32)], axis=0)
    tr_ada = 256 if d % 256 == 0 else d
    g_wada, d_wada, m_wada, v_wada = _adamw_ada(s_mat.T, dmod16, w_ada2, m_w_ada[0], v_w_ada[0], tr_ada)
    g_win2, d_win, m_win, v_win = _adamw_scattered(sc_win, w_in2, m_w_in[0], v_w_in[0], tr_ada)
    tr_out = 64 if wo_rows % 64 == 0 else wo_rows
    g_wout2, d_wout, m_wout, v_wout = _adamw_scattered(sc_wout, w_out2, m_w_out[0], v_w_out[0], tr_out)

    small_w = [c_ctx, norm_g, b_ada, w_conv_a, w_conv_b, b_conv_b, lru_wa, lru_ba, lru_wx, lru_bx, lru_lambda, final_g]
    small_m = [m_c_ctx, m_norm_g, m_b_ada, m_w_conv_a, m_w_conv_b, m_b_conv_b, m_lru_wa, m_lru_ba, m_lru_wx,
               m_lru_bx, m_lru_lambda, m_final_g]
    small_v = [v_c_ctx, v_norm_g, v_b_ada, v_w_conv_a, v_w_conv_b, v_b_conv_b, v_lru_wa, v_lru_ba, v_lru_wx,
               v_lru_bx, v_lru_lambda, v_final_g]
    small_g = [g_cctx, g_norm, g_bada, g_ca_m, g_cb_m, g_bcb, g_wa, g_ba_m, g_wx, g_bx_m, g_lam_m, g_final]
    sshapes = [a.shape for a in small_w]
    d_s, m_s, v_s = _adamw_packed(_pack(small_g), _pack(small_w), _pack(small_m), _pack(small_v))
    d_s, m_s, v_s = _unpack(d_s, sshapes), _unpack(m_s, sshapes), _unpack(v_s, sshapes)
    small_g = [jnp.reshape(a, s) for a, s in zip(small_g, sshapes)]

    def weights(small_list, ada, win, wout):
        (cctx_, norm_, bada_, ca_, cb_, bcb_, wa_, ba_, wx_, bx_, lam_, final_) = small_list
        return [cctx_, norm_, ada[None], bada_, win[None], ca_, cb_, bcb_, wa_, ba_, wx_, bx_, lam_, wout[None], final_]

    return (loss, grad_x[None],
            *weights(small_g, g_wada, g_win2, g_wout2), *weights(d_s, d_wada, d_win, d_wout),
            *weights(m_s, m_wada, m_win, m_wout), *weights(v_s, v_wada, v_win, v_wout))
```

```python
import functools

import jax
import jax.numpy as jnp
from jax import lax
from jax.experimental import pallas as pl
from jax.experimental.pallas import tpu as pltpu

F32 = jnp.float32
BF16 = jnp.bfloat16
MESH = pl.DeviceIdType.MESH
NDEV = 8
GRID_W = 64
N_HEADS = 16
LRU_C = 8.0
EPS = 1e-6
MXU_WIDTH = 256
VMEM_LIMIT = 60 * 1024 * 1024

ADAM_LR = 0.001
ADAM_B1 = 0.9
ADAM_B2 = 0.999
ADAM_EPS = 1e-08
ADAM_WD = 0.01
ADAM_STEP = 10
ADAM_C1 = 1.0 - ADAM_B1 ** ADAM_STEP
ADAM_C2 = 1.0 - ADAM_B2 ** ADAM_STEP

HIGHEST = lax.Precision.HIGHEST
ANY = pl.BlockSpec(memory_space=pl.ANY)
VMEM = pl.BlockSpec(memory_space=pltpu.VMEM)


def _call(body, **kw):
    return pl.pallas_call(body, **kw)


def _params(sem=None, vmem=VMEM_LIMIT):
    return pltpu.CompilerParams(dimension_semantics=sem, vmem_limit_bytes=vmem)


def _my_pos():
    return lax.axis_index("x"), lax.axis_index("y"), lax.axis_index("c")


def _idx(pos):
    return 4 * pos[0] + 2 * pos[1] + pos[2]


def _peer(k):
    x, y, c = _my_pos()
    return ((1 - x) if (k >> 2) & 1 else x, (1 - y) if (k >> 1) & 1 else y, (1 - c) if k & 1 else c)


def _exchange_vmem(src_ref, dst_ref, send_sems, recv_sems, base):
    me = _idx(_my_pos())
    sends = []
    for k in range(1, NDEV):
        cp = pltpu.make_async_remote_copy(
            src_ref=src_ref, dst_ref=dst_ref.at[me], send_sem=send_sems.at[base + k - 1],
            recv_sem=recv_sems.at[base + k - 1], device_id=_peer(k), device_id_type=MESH)
        cp.start()
        sends.append(cp)
    dst_ref[me] = src_ref[...]
    for k in range(1, NDEV):
        peer = _peer(k)
        pltpu.make_async_remote_copy(
            src_ref=src_ref, dst_ref=dst_ref.at[_idx(peer)], send_sem=send_sems.at[base + k - 1],
            recv_sem=recv_sems.at[base + k - 1], device_id=peer, device_id_type=MESH).wait_recv()
    for cp in sends:
        cp.wait_send()


def _sigmoid(z):
    return 1.0 / (1.0 + jnp.exp(-z))


def _softplus(x):
    return jnp.maximum(x, 0.0) + jnp.log1p(jnp.exp(-jnp.abs(x)))


def _one_minus_sq(a, la):
    u = 2.0 * la
    series = -(u * (1.0 + u * (0.5 + u * (1.0 / 6.0))))
    return jnp.where(u > -0.03, series, 1.0 - a * a)


def _dot(a, b):
    return jnp.dot(a, b, preferred_element_type=F32)


def _dot_nt(a, b):
    return lax.dot_general(a, b, (((1,), (1,)), ((), ())), preferred_element_type=F32)


def _rows(shape):
    return lax.broadcasted_iota(jnp.int32, shape, 0)


def _down(x, k, pos):
    return jnp.where(pos >= k, pltpu.roll(x, k, 0), 0.0)


def _up(x, k, pos, rowlen):
    return jnp.where(pos + k < rowlen, pltpu.roll(x, x.shape[0] - k, 0), 0.0)


def _pos_rowlen(shape, is_ctx):
    t = _rows(shape)
    pos = jnp.where(is_ctx, t, t & (GRID_W - 1))
    rowlen = jnp.where(is_ctx, shape[0], GRID_W)
    return pos, rowlen


def _shift_matrices(t):
    r = lax.broadcasted_iota(jnp.int32, (t, t), 0)
    c = lax.broadcasted_iota(jnp.int32, (t, t), 1)
    kinds = []
    for rowlen in (GRID_W, t):
        pos = r % rowlen
        kinds.append(jnp.stack([(c == r - 2) & (pos >= 2), (c == r - 1) & (pos >= 1),
                                (c == r + 1) & (pos + 1 < rowlen), (c == r + 2) & (pos + 2 < rowlen)]))
    return jnp.stack(kinds).astype(BF16)


def _conv4(v16, sm_ref, w_ref):
    taps = (_dot(sm_ref[0], v16), _dot(sm_ref[1], v16), v16.astype(F32), _dot(sm_ref[2], v16))
    out = w_ref[0:1, :] * taps[0] + w_ref[1:2, :] * taps[1] + w_ref[2:3, :] * taps[2] + w_ref[3:4, :] * taps[3]
    return out, taps


def _conv4_t(dy, sm_ref, w_ref):
    dy16 = dy.astype(BF16)
    return (w_ref[0:1, :] * _dot(sm_ref[3], dy16) + w_ref[1:2, :] * _dot(sm_ref[2], dy16)
            + w_ref[2:3, :] * dy + w_ref[3:4, :] * _dot(sm_ref[1], dy16))


def _conv3(t, w_ref, pos, rowlen):
    return w_ref[0:1, :] * _down(t, 1, pos) + w_ref[1:2, :] * t + w_ref[2:3, :] * _up(t, 1, pos, rowlen)


def _conv3_t(dz, w_ref, pos, rowlen):
    return w_ref[0:1, :] * _up(dz, 1, pos, rowlen) + w_ref[1:2, :] * dz + w_ref[2:3, :] * _down(dz, 1, pos)


def _chunk_scan(a, b, reverse):
    row = _rows(a.shape)
    for s in (1, 2, 4):
        if reverse:
            m = row < 8 - s
            sh = 8 - s
        else:
            m = row >= s
            sh = s
        a_s = jnp.where(m, pltpu.roll(a, sh, 0), 1.0)
        b_s = jnp.where(m, pltpu.roll(b, sh, 0), 0.0)
        b = b + a * b_s
        a = a * a_s
    return a, b


def _scan_tile(a_ref, b_ref, out_ref, carry, reverse):
    t, w = a_ref.shape
    nchunk = t // 8

    def step(k, h):
        r0 = pl.multiple_of((nchunk - 1 - k if reverse else k) * 8, 8)
        ca, cb = _chunk_scan(a_ref[pl.ds(r0, 8), :], b_ref[pl.ds(r0, 8), :], reverse)
        hh = ca * h + cb
        out_ref[pl.ds(r0, 8), :] = hh
        return jnp.broadcast_to(hh[0:1, :] if reverse else hh[7:8, :], (8, w))

    return lax.fori_loop(0, nchunk, step, carry)


def _scan_tile_backward(a_ref, b_ref, g_ref, carry, reverse):
    t, w = a_ref.shape
    nchunk = t // 8

    def step(k, u_next):
        r0 = pl.multiple_of((nchunk - 1 - k if reverse else k) * 8, 8)
        ca, cb = _chunk_scan(a_ref[pl.ds(r0, 8), :], b_ref[pl.ds(r0, 8), :], reverse)
        u = ca * u_next + cb
        r8 = _rows((8, w))
        if reverse:
            shifted = jnp.where(r8 < 7, pltpu.roll(u, 7, 0), u_next)
        else:
            shifted = jnp.where(r8 >= 1, pltpu.roll(u, 1, 0), u_next)
        g_ref[pl.ds(r0, 8), :] = g_ref[pl.ds(r0, 8), :] + shifted
        return jnp.broadcast_to(u[0:1, :] if reverse else u[7:8, :], (8, w))

    return lax.fori_loop(0, nchunk, step, carry)


def _lru_coef(xb, wg_ref, d, ba, bx, lam, gc):
    w = xb.shape[1]
    xb16 = xb.astype(BF16)
    zr, zi = [], []
    for g in range(w // gc):
        z = _dot(xb16[:, g * gc:(g + 1) * gc], wg_ref[d, g])
        zr.append(z[:, :gc])
        zi.append(z[:, gc:])
    zr = zr[0] if len(zr) == 1 else jnp.concatenate(zr, axis=-1)
    zi = zi[0] if len(zi) == 1 else jnp.concatenate(zi, axis=-1)
    r = _sigmoid(zr + ba)
    ig = _sigmoid(zi + bx)
    sp = _softplus(-lam)
    la = r * (-LRU_C * sp)
    a = jnp.exp(la)
    s = jnp.sqrt(_one_minus_sq(a, la))
    return a, s, r, ig, sp


def _adamw(w, g, m, v):
    m2 = ADAM_B1 * m + (1.0 - ADAM_B1) * g
    v2 = ADAM_B2 * v + (1.0 - ADAM_B2) * (g * g)
    m_hat = m2 / ADAM_C1
    v_hat = v2 / ADAM_C2
    delta = -ADAM_LR * (m_hat / (jnp.sqrt(v_hat) + ADAM_EPS) + ADAM_WD * w)
    return delta, m2, v2


def _mod_forward(c8, cctx8, w_ada, small):
    d = c8.shape[1]
    cols = w_ada.shape[1]

    def body(c_ref, cctx_ref, w_ref, sm_ref, mod_ref, s_ref, sm_all, cbuf, mod_my, send_sems, recv_sems):
        _exchange_vmem(sm_ref, sm_all, send_sems, recv_sems, 2 * (NDEV - 1))
        _exchange_vmem(c_ref, cbuf, send_sems, recv_sems, 0)
        row = _rows((8, d))
        c_all = jnp.zeros((8, d), F32)
        for b in range(NDEV):
            c_all = jnp.where(row == b, cbuf[b], c_all)
        cc = cctx_ref[...]
        s_top = c_all * _sigmoid(c_all)
        s_bot = jnp.where(row == 0, cc * _sigmoid(cc), 0.0)
        s = jnp.concatenate([s_top, s_bot], axis=0)
        s_ref[...] = s
        mod_my[...] = jnp.dot(s, w_ref[...], precision=HIGHEST, preferred_element_type=F32)
        _exchange_vmem(mod_my, mod_ref, send_sems, recv_sems, NDEV - 1)

    return _call(
        body, name="mod_forward",
        out_shape=(jax.ShapeDtypeStruct((NDEV, 16, cols), F32), jax.ShapeDtypeStruct((16, d), F32),
                   jax.ShapeDtypeStruct((NDEV,) + small.shape, F32)),
        in_specs=[VMEM] * 4, out_specs=(VMEM,) * 3,
        scratch_shapes=[pltpu.VMEM((NDEV, 8, d), F32), pltpu.VMEM((16, cols), F32),
                        pltpu.SemaphoreType.DMA((3 * (NDEV - 1),)), pltpu.SemaphoreType.DMA((3 * (NDEV - 1),))],
        compiler_params=_params(),
    )(c8, cctx8, w_ada, small)


def _gather_weights(shards):
    n = len(shards)

    def body(*refs):
        ins, outs = refs[:n], refs[n:2 * n]
        send_sems, recv_sems, local_sems = refs[2 * n:]
        x, y, c = _my_pos()
        me, sibling = (x, y, c), (x, y, 1 - c)
        chips = [(1 - x, y), (x, 1 - y), (1 - x, 1 - y)]
        waits = []
        for a in range(n):
            src, out = ins[a], outs[a]

            def copy(k, block, to, from_src=False, src=src, out=out, a=a):
                return pltpu.make_async_remote_copy(
                    src_ref=src if from_src else out.at[_idx(block)], dst_ref=out.at[_idx(block)],
                    send_sem=send_sems.at[7 * a + k], recv_sem=recv_sems.at[7 * a + k],
                    device_id=to, device_id_type=MESH)

            mine = pltpu.make_async_copy(src, out.at[_idx(me)], local_sems.at[a])
            mine.start()
            first = [copy(0, me, sibling, True)]
            first += [copy(1 + j, me, (*chip, c), True) for j, chip in enumerate(chips)]
            for cp in first:
                cp.start()
            waits.append((copy, mine, first))
        for a in range(n):
            copy, mine, first = waits[a]
            passed = [copy(4 + j, (*chip, c), sibling) for j, chip in enumerate(chips)]
            for j, chip in enumerate(chips):
                copy(1 + j, (*chip, c), me).wait_recv()
                passed[j].start()
            copy(0, sibling, me).wait_recv()
            for j, chip in enumerate(chips):
                copy(4 + j, (*chip, 1 - c), me).wait_recv()
            for cp in first + passed:
                cp.wait_send()
            mine.wait()

    return _call(
        body, name="gather_weights",
        out_shape=tuple(jax.ShapeDtypeStruct((NDEV,) + s.shape, s.dtype) for s in shards),
        in_specs=[ANY] * n, out_specs=tuple([ANY] * n),
        scratch_shapes=[pltpu.SemaphoreType.DMA((7 * n,)), pltpu.SemaphoreType.DMA((7 * n,)),
                        pltpu.SemaphoreType.DMA((n,))],
        compiler_params=_params(),
    )(*shards)


def _scatter_partials(parts):
    n = len(parts)

    def body(*refs):
        ins, outs = refs[:n], refs[n:2 * n]
        send_sems, recv_sems, local_sems = refs[2 * n:]
        me = _idx(_my_pos())
        sends = []
        for a in range(n):
            mine = pltpu.make_async_copy(ins[a].at[me], outs[a].at[0], local_sems.at[a])
            mine.start()
            sends.append(mine)
        for k in range(1, NDEV):
            peer = _peer(k)
            for a in range(n):
                cp = pltpu.make_async_remote_copy(
                    src_ref=ins[a].at[_idx(peer)], dst_ref=outs[a].at[k],
                    send_sem=send_sems.at[7 * a + k - 1], recv_sem=recv_sems.at[7 * a + k - 1],
                    device_id=peer, device_id_type=MESH)
                cp.start()
                sends.append(cp)
        for k in range(1, NDEV):
            for a in range(n):
                pltpu.make_async_remote_copy(
                    src_ref=ins[a].at[0], dst_ref=outs[a].at[k],
                    send_sem=send_sems.at[7 * a + k - 1], recv_sem=recv_sems.at[7 * a + k - 1],
                    device_id=_peer(k), device_id_type=MESH).wait_recv()
        for cp in sends[:n]:
            cp.wait()
        for cp in sends[n:]:
            cp.wait_send()

    return _call(
        body, name="scatter_partials",
        out_shape=tuple(jax.ShapeDtypeStruct(p.shape, p.dtype) for p in parts),
        in_specs=[ANY] * n, out_specs=tuple([ANY] * n),
        scratch_shapes=[pltpu.SemaphoreType.DMA((7 * n,)), pltpu.SemaphoreType.DMA((7 * n,)),
                        pltpu.SemaphoreType.DMA((n,))],
        compiler_params=_params(),
    )(*parts)


def _reduce_small(packed, lru_parts, w_ada, dsilu_cctx):
    rp = packed.shape[0]
    rl = lru_parts.shape[1]
    d, cols = w_ada.shape
    assert cols % 128 == 0
    cb = cols // 128

    def body(p_ref, l_ref, w_ref, ds_ref, sum_ref, all_ref, lru_ref, cctx_ref,
             lsum, cpart, call, send_sems, recv_sems):
        me = _idx(_my_pos())
        _exchange_vmem(p_ref, all_ref, send_sems, recv_sems, 0)
        acc = all_ref[0]
        for j in range(1, NDEV):
            acc = acc + all_ref[j]
        sum_ref[...] = acc
        red = l_ref[0]
        for k in range(1, NDEV):
            red = red + l_ref[k]
        lsum[...] = red
        _exchange_vmem(lsum, lru_ref, send_sems, recv_sems, NDEV - 1)
        part = jnp.zeros((8, d), F32)
        for q in range(cb):
            dm = jnp.broadcast_to(sum_ref[pl.ds((NDEV + me) * cb + q, 1), :], (8, 128))
            part = part + lax.dot_general(dm, w_ref[:, q * 128:(q + 1) * 128],
                                          (((1,), (1,)), ((), ())), precision=HIGHEST,
                                          preferred_element_type=F32)
        cpart[...] = part
        _exchange_vmem(cpart, call, send_sems, recv_sems, 2 * (NDEV - 1))
        tot = call[0]
        for j in range(1, NDEV):
            tot = tot + call[j]
        cctx_ref[...] = tot * ds_ref[...]

    return _call(
        body, name="reduce_small",
        out_shape=(jax.ShapeDtypeStruct((rp, 128), F32), jax.ShapeDtypeStruct((NDEV, rp, 128), F32),
                   jax.ShapeDtypeStruct((NDEV, rl, 128), F32), jax.ShapeDtypeStruct((8, d), F32)),
        in_specs=[VMEM] * 4, out_specs=(VMEM,) * 4,
        scratch_shapes=[pltpu.VMEM((rl, 128), F32), pltpu.VMEM((8, d), F32), pltpu.VMEM((NDEV, 8, d), F32),
                        pltpu.SemaphoreType.DMA((3 * (NDEV - 1),)), pltpu.SemaphoreType.DMA((3 * (NDEV - 1),))],
        compiler_params=_params(),
    )(packed, lru_parts, w_ada, dsilu_cctx)


def _in_projection(src, mv, w_all, la, row0, tm, nbj, name, prev=None):
    rows, d = src.shape
    nb, _, bw = w_all.shape
    blk0 = row0 // tm

    def body(*refs):
        x_ref, mv_ref, w_ref = refs[:3]
        p_ref, ht_ref, h_s = refs[-3:]

        @pl.when(pl.program_id(1) == 0)
        def _():
            xf = x_ref[...]
            r = lax.rsqrt(jnp.mean(xf * xf, axis=-1, keepdims=True) + EPS)
            h = xf * r * (mv_ref[0:1, :] * (1.0 + mv_ref[1:2, :])) + mv_ref[2:3, :]
            h_s[...] = h.astype(BF16)
            ht_ref[...] = h.T.astype(BF16)

        h = h_s[...]
        for q in range(nbj):
            p_ref[:, q * bw:(q + 1) * bw] = _dot(h, w_ref[q]).astype(BF16)

    in_specs = [pl.BlockSpec((tm, d), lambda i, j: (i, 0)),
                pl.BlockSpec((8, d), lambda i, j: (0, 0)),
                pl.BlockSpec((nbj, d, bw), lambda i, j: (j, 0, 0))]
    args = [src, mv, w_all]
    aliases = {}
    if prev is not None:
        in_specs += [ANY, ANY]
        args += list(prev)
        aliases = {3: 0, 4: 1}
    return _call(
        body, name=name,
        grid=(rows // tm, nb // nbj),
        out_shape=(jax.ShapeDtypeStruct((la, nb * bw), BF16), jax.ShapeDtypeStruct((d, la), BF16)),
        in_specs=in_specs,
        out_specs=(pl.BlockSpec((tm, nbj * bw), lambda i, j: (blk0 + i, j)),
                   pl.BlockSpec((d, tm), lambda i, j: (0, blk0 + i))),
        scratch_shapes=[pltpu.VMEM((tm, d), BF16)],
        input_output_aliases=aliases,
        compiler_params=_params(("arbitrary", "arbitrary")),
    )(*args)


def _lru_forward(p, wg, lv, wcb, sm, l, t):
    la = p.shape[0]
    w = lv.shape[1]
    gc = wg.shape[2]
    nt = l // t

    def body(vf_ref, vr_ref, wg_ref, lv_ref, wcb_ref, sm_ref, hf_ref, hr_ref, a_s, b_s, carry):
        @pl.when(pl.program_id(0) == 0)
        def _():
            carry[...] = jnp.zeros_like(carry)

        for dr, (v_ref, h_ref) in enumerate(((vf_ref, hf_ref), (vr_ref, hr_ref))):
            xb, _ = _conv4(v_ref[...], sm_ref, wcb_ref)
            xb = xb + lv_ref[6:7, :]
            a, s, _, ig, _ = _lru_coef(xb, wg_ref, dr, lv_ref[3 * dr:3 * dr + 1, :],
                                       lv_ref[3 * dr + 1:3 * dr + 2, :], lv_ref[3 * dr + 2:3 * dr + 3, :], gc)
            a_s[...] = a
            b_s[...] = s * (ig * xb)
            carry[dr] = _scan_tile(a_s, b_s, h_ref, carry[dr], dr == 1)

    full = lambda shape: pl.BlockSpec(shape, lambda i: (0,) * len(shape))
    fmap = lambda i: (jnp.where(i == 0, nt, i - 1), 0)
    rmap = lambda i: (jnp.where(i == 0, nt, nt - i), 0)
    vcol = 4
    return _call(
        body, name="lru_forward",
        grid=(nt + 1,),
        out_shape=(jax.ShapeDtypeStruct((la, w), F32), jax.ShapeDtypeStruct((la, w), F32)),
        in_specs=[pl.BlockSpec((t, w), lambda i: (jnp.where(i == 0, nt, i - 1), vcol)),
                  pl.BlockSpec((t, w), lambda i: (jnp.where(i == 0, nt, nt - i), vcol)),
                  full(wg.shape), full(lv.shape), full(wcb.shape),
                  pl.BlockSpec((None, 4, t, t), lambda i: (jnp.where(i == 0, 1, 0), 0, 0, 0))],
        out_specs=(pl.BlockSpec((t, w), fmap), pl.BlockSpec((t, w), rmap)),
        scratch_shapes=[pltpu.VMEM((t, w), F32), pltpu.VMEM((t, w), F32), pltpu.VMEM((2, 8, w), F32)],
        compiler_params=_params(("arbitrary",)),
    )(p, p, wg, lv, wcb, sm)


def _mix_gates(p_refs, hf_ref, hr_ref, wca_ref, t, w):
    bl, cl, ul, gl, ql = [r[...].astype(F32) for r in p_refs]
    pos, rowlen = _pos_rowlen((t, w), False)
    tt = cl * ul
    z = _conv3(tt, wca_ref, pos, rowlen)
    sig_g = _sigmoid(gl)
    sig_q = _sigmoid(ql)
    ylru = hf_ref[...] + hr_ref[...]
    return bl, cl, ul, gl, ql, tt, z, sig_g, sig_q, ylru, pos, rowlen


def _p_specs(t, w, nt):
    return [pl.BlockSpec((t, w), functools.partial(lambda i, s: (jnp.minimum(i, nt - 1), s), s=s))
            for s in (0, 1, 2, 3, 5)]


def _mix_forward(x, tgt, p, hf, hr, wo, ov, wca, t):
    l, d = x.shape
    w = d // 2
    nt = l // t

    def body(x_ref, tg_ref, b_ref, c_ref, u_ref, g_ref, q_ref, hf_ref, hr_ref, wo_ref, ov_ref, wca_ref,
             dn_ref, ct_ref, do_ref, part_ref):
        i = pl.program_id(0)
        bl, _, _, gl, ql, _, z, sig_g, sig_q, ylru, _, _ = _mix_gates(
            (b_ref, c_ref, u_ref, g_ref, q_ref), hf_ref, hr_ref, wca_ref, t, w)
        ya = bl * z * (gl * sig_g)
        yb = ylru * (ql * sig_q)
        ct_ref[0:w, :] = ya.T.astype(BF16)
        ct_ref[w:, :] = yb.T.astype(BF16)
        out = _dot(ya.astype(BF16), wo_ref[0:w, :]) + _dot(yb.astype(BF16), wo_ref[w:, :])
        gate, fg = ov_ref[0:1, :], ov_ref[1:2, :]
        n = x_ref[...] + gate * out
        rr = lax.rsqrt(jnp.mean(n * n, axis=-1, keepdims=True) + EPS)
        nh = n * rr
        e = nh * fg - tg_ref[...]
        loss = 0.5 * jnp.sum(jnp.mean(e * e, axis=-1, keepdims=True), axis=0, keepdims=True)
        dy = e * (1.0 / d)
        dnh = dy * fg
        dn = rr * (dnh - nh * jnp.mean(dnh * nh, axis=-1, keepdims=True))
        dn_ref[...] = dn
        do_ref[...] = (dn * gate).astype(BF16)

        @pl.when(i == 0)
        def _():
            part_ref[...] = jnp.zeros_like(part_ref)

        part_ref[0:1, :] += jnp.sum(dy * nh, axis=0, keepdims=True)
        part_ref[1:2, :] += jnp.sum(dn * out, axis=0, keepdims=True)
        part_ref[2:3, :] += jnp.broadcast_to(loss, (1, d))

    tile = lambda cols: pl.BlockSpec((t, cols), lambda i: (i, 0))
    full = lambda shape: pl.BlockSpec(shape, lambda i: (0,) * len(shape))
    return _call(
        body, name="mix_forward",
        grid=(nt,),
        out_shape=(jax.ShapeDtypeStruct((l, d), F32), jax.ShapeDtypeStruct((d, l), BF16),
                   jax.ShapeDtypeStruct((l, d), BF16), jax.ShapeDtypeStruct((8, d), F32)),
        in_specs=[tile(d), tile(d)] + _p_specs(t, w, nt) + [tile(w), tile(w),
                  pl.BlockSpec((d, d), lambda i: (0, 0), pipeline_mode=pl.Buffered(1)),
                  full(ov.shape), full(wca.shape)],
        out_specs=(tile(d), pl.BlockSpec((d, t), lambda i: (0, i)), tile(d), full((8, d))),
        compiler_params=_params(("arbitrary",)),
    )(x, tgt, p, p, p, p, p, hf, hr, wo, ov, wca)


def _mix_backward(dout, p, hf, hr, wo, wca, l, t):
    d = dout.shape[1]
    w = d // 2
    nt = l // t
    la = p.shape[0]

    def body(do_ref, b_ref, c_ref, u_ref, g_ref, q_ref, hf_ref, hr_ref, wo_ref, wca_ref,
             dp_ref, dh_ref, part_ref):
        i = pl.program_id(0)

        @pl.when(i == 0)
        def _():
            part_ref[...] = jnp.zeros_like(part_ref)

        @pl.when(i == nt)
        def _():
            dp_ref[...] = jnp.zeros_like(dp_ref)

        @pl.when(i < nt)
        def _():
            bl, cl, ul, gl, ql, tt, z, sig_g, sig_q, ylru, pos, rowlen = _mix_gates(
                (b_ref, c_ref, u_ref, g_ref, q_ref), hf_ref, hr_ref, wca_ref, t, w)
            do = do_ref[...]
            dya = _dot_nt(do, wo_ref[0:w, :])
            dyb = _dot_nt(do, wo_ref[w:, :])
            sg = gl * sig_g
            dz = dya * bl * sg
            dt = _conv3_t(dz, wca_ref, pos, rowlen)
            dp_ref[:, 0:w] = (dya * z * sg).astype(BF16)
            dp_ref[:, w:2 * w] = (dt * ul).astype(BF16)
            dp_ref[:, 2 * w:3 * w] = (dt * cl).astype(BF16)
            dp_ref[:, 3 * w:4 * w] = (dya * bl * z * (sig_g * (1.0 + gl * (1.0 - sig_g)))).astype(BF16)
            dp_ref[:, 4 * w:5 * w] = jnp.zeros((t, w), BF16)
            dp_ref[:, 5 * w:6 * w] = (dyb * ylru * (sig_q * (1.0 + ql * (1.0 - sig_q)))).astype(BF16)
            dh_ref[...] = dyb * (ql * sig_q)
            part_ref[0:1, :] += jnp.sum(dz * _down(tt, 1, pos), axis=0, keepdims=True)
            part_ref[1:2, :] += jnp.sum(dz * tt, axis=0, keepdims=True)
            part_ref[2:3, :] += jnp.sum(dz * _up(tt, 1, pos, rowlen), axis=0, keepdims=True)

    clamp = lambda cols: pl.BlockSpec((t, cols), lambda i: (jnp.minimum(i, nt - 1), 0))
    full = lambda shape: pl.BlockSpec(shape, lambda i: (0,) * len(shape))
    return _call(
        body, name="mix_backward",
        grid=(nt + 1,),
        out_shape=(jax.ShapeDtypeStruct((la, 6 * w), BF16), jax.ShapeDtypeStruct((l, w), F32),
                   jax.ShapeDtypeStruct((8, w), F32)),
        in_specs=[clamp(d)] + _p_specs(t, w, nt) + [clamp(w), clamp(w),
                  pl.BlockSpec((d, d), lambda i: (0, 0), pipeline_mode=pl.Buffered(1)), full(wca.shape)],
        out_specs=(pl.BlockSpec((t, 6 * w), lambda i: (i, 0)), clamp(w), full((8, w))),
        compiler_params=_params(("arbitrary",)),
    )(dout, p, p, p, p, p, hf, hr, wo, wca)


def _lru_backward(direction, p, dhs, hs, wg, lv, wcb, sm, l, t, dxb_other=None, dp=None):
    la, w = hs.shape
    gc = wg.shape[2]
    ng = w // gc
    nt = l // t
    nblk8 = la // 8
    last = direction == 1

    if direction == 0:
        tile = lambda i: jnp.where(i == nt, nt, nt - 1 - i)
        halo = lambda i: jnp.where(tile(i) == 0, nblk8 - 1, tile(i) * (t // 8) - 1)
    else:
        tile = lambda i: i
        halo = lambda i: jnp.minimum((i + 1) * (t // 8), nblk8 - 1)

    def body(*refs):
        if last:
            (v_ref, dh_ref, hs_ref, halo_ref, wg_ref, lv_ref, wcb_ref, sm_ref, dxo_ref, _,
             out_ref, dwg_ref, part_ref, a_s, b_s, g_s, carry) = refs
        else:
            (v_ref, dh_ref, hs_ref, halo_ref, wg_ref, lv_ref, wcb_ref, sm_ref,
             out_ref, dwg_ref, part_ref, a_s, b_s, g_s, carry) = refs
        i = pl.program_id(0)
        is_ctx = i == nt

        @pl.when(i == 0)
        def _():
            carry[...] = jnp.zeros_like(carry)
            dwg_ref[...] = jnp.zeros_like(dwg_ref)
            part_ref[...] = jnp.zeros_like(part_ref)

        row = _rows((t, w))
        xb, taps = _conv4(v_ref[...], sm_ref, wcb_ref)
        xb = xb + lv_ref[6:7, :]
        lam = lv_ref[3 * direction + 2:3 * direction + 3, :]
        a, s, r, ig, sp = _lru_coef(xb, wg_ref, direction, lv_ref[3 * direction:3 * direction + 1, :],
                                    lv_ref[3 * direction + 1:3 * direction + 2, :], lam, gc)
        hs_t = hs_ref[...]
        if direction == 0:
            edge = jnp.where(is_ctx, 0.0, halo_ref[7:8, :])
            hprev = jnp.where(row == 0, edge, pltpu.roll(hs_t, 1, 0))
        else:
            edge = jnp.where(is_ctx, 0.0, halo_ref[0:1, :])
            hprev = jnp.where(row == t - 1, edge, pltpu.roll(hs_t, t - 1, 0))
        dh = jnp.where(is_ctx, 0.0, dh_ref[...])
        a_s[...] = a
        b_s[...] = a * dh
        g_s[...] = dh
        carry[...] = _scan_tile_backward(a_s, b_s, g_s, carry[...], direction == 0)

        g = g_s[...]
        ix = ig * xb
        gs = g * s
        dla = (g * a) * (hprev - ix * (a / s))
        dxb = gs * ig
        dzr = dla * (r * (1.0 - r)) * (-LRU_C * sp)
        dzi = gs * ix * (1.0 - ig)
        part_ref[0:1, :] += jnp.sum(dzr, axis=0, keepdims=True)
        part_ref[1:2, :] += jnp.sum(dzi, axis=0, keepdims=True)
        part_ref[2:3, :] += jnp.sum(dla * r, axis=0, keepdims=True) * (LRU_C * _sigmoid(-lam))
        pieces = []
        for gi in range(ng):
            sl = slice(gi * gc, (gi + 1) * gc)
            dz = jnp.concatenate([dzr[:, sl], dzi[:, sl]], axis=-1).astype(BF16)
            pieces.append(_dot_nt(dz, wg_ref[direction, gi]))
            dwg_ref[gi] += _dot(xb[:, sl].T.astype(BF16), dz)
        dxb = dxb + (pieces[0] if ng == 1 else jnp.concatenate(pieces, axis=-1))
        if not last:
            out_ref[...] = dxb
        else:
            dxb = dxb + dxo_ref[...]
            out_ref[...] = _conv4_t(dxb, sm_ref, wcb_ref).astype(BF16)
            part_ref[3:4, :] += jnp.sum(dxb, axis=0, keepdims=True)
            for j in range(4):
                part_ref[4 + j:5 + j, :] += jnp.sum(dxb * taps[j], axis=0, keepdims=True)

    full = lambda shape: pl.BlockSpec(shape, lambda i: (0,) * len(shape))
    in_specs = [pl.BlockSpec((t, w), lambda i: (tile(i), 4)),
                pl.BlockSpec((t, w), lambda i: (jnp.minimum(tile(i), nt - 1), 0)),
                pl.BlockSpec((t, w), lambda i: (tile(i), 0)),
                pl.BlockSpec((8, w), lambda i: (halo(i), 0)),
                full(wg.shape), full(lv.shape), full(wcb.shape),
                pl.BlockSpec((None, 4, t, t), lambda i: (jnp.where(i == nt, 1, 0), 0, 0, 0))]
    args = [p, dhs, hs, hs, wg, lv, wcb, sm]
    if last:
        in_specs += [pl.BlockSpec((t, w), lambda i: (tile(i), 0)), ANY]
        args += [dxb_other, dp]
        out0 = jax.ShapeDtypeStruct(dp.shape, dp.dtype)
        spec0 = pl.BlockSpec((t, w), lambda i: (tile(i), 4))
        aliases = {9: 0}
    else:
        out0 = jax.ShapeDtypeStruct((la, w), F32)
        spec0 = pl.BlockSpec((t, w), lambda i: (tile(i), 0))
        aliases = {}
    return _call(
        body, name="lru_backward_%d" % direction,
        grid=(nt + 1,),
        out_shape=(out0, jax.ShapeDtypeStruct((ng, gc, 2 * gc), F32), jax.ShapeDtypeStruct((8, w), F32)),
        in_specs=in_specs,
        out_specs=(spec0, full((ng, gc, 2 * gc)), full((8, w))),
        scratch_shapes=[pltpu.VMEM((t, w), F32), pltpu.VMEM((t, w), F32), pltpu.VMEM((t, w), F32),
                        pltpu.VMEM((8, w), F32)],
        input_output_aliases=aliases,
        compiler_params=_params(("arbitrary",)),
    )(*args)


def _weight_grad_t(at, b, nblk_m, nblk_n, tk, name):
    m, k = at.shape
    n = b.shape[1]
    bm, bn = m // nblk_m, n // nblk_n
    nk = k // tk

    def body(a_ref, b_ref, o_ref, acc):
        kk = pl.program_id(2)

        @pl.when(kk == 0)
        def _():
            acc[...] = jnp.zeros_like(acc)

        acc[...] += _dot(a_ref[...], b_ref[...])

        @pl.when(kk == nk - 1)
        def _():
            o_ref[...] = acc[...].astype(BF16)

    return _call(
        body, name=name,
        grid=(nblk_m, nblk_n, nk),
        out_shape=jax.ShapeDtypeStruct((nblk_m * nblk_n, bm, bn), BF16),
        in_specs=[pl.BlockSpec((bm, tk), lambda i, j, kk: (i, kk)),
                  pl.BlockSpec((tk, bn), lambda i, j, kk: (kk, j))],
        out_specs=pl.BlockSpec((None, bm, bn), lambda i, j, kk: (i * nblk_n + j, 0, 0)),
        scratch_shapes=[pltpu.VMEM((bm, bn), F32)],
        compiler_params=_params(("arbitrary", "arbitrary", "arbitrary")),
    )(at, b)


def _input_backward(dp, w_all, src, mv, row0, tm, nbk, name, dn=None):
    rows, d = src.shape
    nb, _, bw = w_all.shape
    nk = nb // nbk
    blk0 = row0 // tm
    latent = dn is not None

    def body(*refs):
        dp_ref, w_ref, x_ref, mv_ref = refs[:4]
        part_ref, acc = refs[-2:]
        i, k = pl.program_id(0), pl.program_id(1)

        @pl.when((i == 0) & (k == 0))
        def _():
            part_ref[...] = jnp.zeros_like(part_ref)

        step = _dot_nt(dp_ref[:, 0:bw], w_ref[0])
        for q in range(1, nbk):
            step = step + _dot_nt(dp_ref[:, q * bw:(q + 1) * bw], w_ref[q])

        @pl.when(k == 0)
        def _():
            acc[...] = step

        @pl.when(k > 0)
        def _():
            acc[...] += step

        @pl.when(k == nk - 1)
        def _():
            xf = x_ref[...]
            r = lax.rsqrt(jnp.mean(xf * xf, axis=-1, keepdims=True) + EPS)
            xn = xf * r
            dhl = acc[...]
            gain, sc = mv_ref[0:1, :], mv_ref[1:2, :]
            dhx = jnp.sum(dhl * xn, axis=0, keepdims=True)
            part_ref[0:1, :] += jnp.sum(dhl, axis=0, keepdims=True)
            part_ref[1:2, :] += dhx * gain
            part_ref[2:3, :] += dhx * (1.0 + sc)
            if latent:
                dxn = dhl * (gain * (1.0 + sc))
                refs[5][...] = refs[4][...] + r * (dxn - xn * jnp.mean(dxn * xn, axis=-1, keepdims=True))

    tile = pl.BlockSpec((tm, d), lambda i, k: (i, 0))
    vec = pl.BlockSpec((8, d), lambda i, k: (0, 0))
    return _call(
        body, name=name,
        grid=(rows // tm, nk),
        out_shape=((jax.ShapeDtypeStruct((rows, d), F32),) if latent else ()) + (jax.ShapeDtypeStruct((8, d), F32),),
        in_specs=[pl.BlockSpec((tm, nbk * bw), lambda i, k: (blk0 + i, k)),
                  pl.BlockSpec((nbk, d, bw), lambda i, k: (k, 0, 0)), tile, vec] + ([tile] if latent else []),
        out_specs=((tile,) if latent else ()) + (vec,),
        scratch_shapes=[pltpu.VMEM((tm, d), F32)],
        compiler_params=_params(("arbitrary", "arbitrary")),
    )(*([dp, w_all, src, mv] + ([dn] if latent else [])))


def _adamw_scattered(parts, w, m, v, tr):
    r, c = w.shape

    def body(p_ref, w_ref, m_ref, v_ref, g_ref, d_ref, m2_ref, v2_ref):
        g = p_ref[0].astype(F32)
        for k in range(1, NDEV):
            g = g + p_ref[k].astype(F32)
        g_ref[...] = g
        d_ref[...], m2_ref[...], v2_ref[...] = _adamw(w_ref[...], g, m_ref[...], v_ref[...])

    tile = pl.BlockSpec((tr, c), lambda i: (i, 0))
    return _call(
        body, name="adamw_scattered_%dx%d" % (r, c),
        grid=(r // tr,),
        out_shape=tuple(jax.ShapeDtypeStruct((r, c), F32) for _ in range(4)),
        in_specs=[pl.BlockSpec((NDEV, tr, c), lambda i: (0, i, 0)), tile, tile, tile],
        out_specs=(tile,) * 4,
        compiler_params=_params(("arbitrary",)),
    )(parts, w, m, v)


def _adamw_ada(st, dmod, w, m, v, tr):
    r, c = w.shape

    def body(s_ref, dm_ref, w_ref, m_ref, v_ref, g_ref, d_ref, m2_ref, v2_ref):
        g = jnp.dot(s_ref[...], dm_ref[...], precision=HIGHEST, preferred_element_type=F32)
        g_ref[...] = g
        d_ref[...], m2_ref[...], v2_ref[...] = _adamw(w_ref[...], g, m_ref[...], v_ref[...])

    tile = pl.BlockSpec((tr, c), lambda i: (i, 0))
    return _call(
        body, name="adamw_ada",
        grid=(r // tr,),
        out_shape=tuple(jax.ShapeDtypeStruct((r, c), F32) for _ in range(4)),
        in_specs=[pl.BlockSpec((tr, 16), lambda i: (i, 0)), pl.BlockSpec((16, c), lambda i: (0, 0)),
                  tile, tile, tile],
        out_specs=(tile,) * 4,
        compiler_params=_params(("arbitrary",)),
    )(st, dmod, w, m, v)


def _adamw_packed(g, w, m, v):
    def body(g_ref, w_ref, m_ref, v_ref, d_ref, m2_ref, v2_ref):
        d_ref[...], m2_ref[...], v2_ref[...] = _adamw(w_ref[...], g_ref[...], m_ref[...], v_ref[...])

    return _call(
        body, name="adamw_packed",
        out_shape=tuple(jax.ShapeDtypeStruct(w.shape, F32) for _ in range(3)),
        in_specs=[VMEM] * 4, out_specs=(VMEM,) * 3,
        compiler_params=_params(),
    )(g, w, m, v)


def _blockdiag_groups(wh, gc):
    h, dh, _ = wh.shape
    g = gc // dh
    w4 = wh.reshape(h // g, g, dh, dh)
    bd = jnp.einsum("ngij,gh->ngihj", w4, jnp.eye(g, dtype=wh.dtype))
    return bd.reshape(h // g, gc, gc)


def _blockdiag_extract(bd, dh):
    ng, gc, _ = bd.shape
    g = gc // dh
    x = bd.reshape(ng, g, dh, g, dh)
    return jnp.einsum("ngihj,gh->ngij", x, jnp.eye(g, dtype=bd.dtype)).reshape(ng * g, dh, dh)


def _rows8(*vecs):
    rows = [jnp.reshape(v, (1, -1)).astype(F32) for v in vecs]
    n = rows[0].shape[1]
    return jnp.concatenate(rows + [jnp.zeros((8 - len(rows), n), F32)], axis=0)


def _pack(pieces):
    flat = jnp.concatenate([jnp.reshape(a, (-1,)).astype(F32) for a in pieces])
    total = -(-flat.shape[0] // 1024) * 1024
    return jnp.pad(flat, (0, total - flat.shape[0])).reshape(total // 128, 128)


def _unpack(packed, shapes):
    flat = packed.reshape(-1)
    out, off = [], 0
    for s in shapes:
        n = 1
        for q in s:
            n *= q
        out.append(flat[off:off + n].reshape(s))
        off += n
    return out


def kernel(x, c, ctx, c_ctx, norm_g, w_ada, b_ada, w_in, w_conv_a, w_conv_b, b_conv_b, lru_wa, lru_ba, lru_wx, lru_bx, lru_lambda, w_out, final_g, loss_target, m_c_ctx, m_norm_g, m_w_ada, m_b_ada, m_w_in, m_w_conv_a, m_w_conv_b, m_b_conv_b, m_lru_wa, m_lru_ba, m_lru_wx, m_lru_bx, m_lru_lambda, m_w_out, m_final_g, v_c_ctx, v_norm_g, v_w_ada, v_b_ada, v_w_in, v_w_conv_a, v_w_conv_b, v_b_conv_b, v_lru_wa, v_lru_ba, v_lru_wx, v_lru_bx, v_lru_lambda, v_w_out, v_final_g):
    _, l, d = x.shape
    lc = ctx.shape[1]
    w = d // 2
    t = lc
    assert l % t == 0 and t % GRID_W == 0 and t % 128 == 0
    dh = w // N_HEADS
    gc = min(w, MXU_WIDTH)
    cols = w_ada.shape[2]
    wo_rows = w_out.shape[1]
    me = _idx(_my_pos())
    x2, ctx2, tgt2 = x[0], ctx[0], loss_target[0]
    w_ada2, w_in2, w_out2 = w_ada[0], w_in[0], w_out[0]

    small_mine = jnp.concatenate([w_conv_a[0], w_conv_b[0], lru_ba[0], lru_bx[0], lru_lambda[0],
                                  jnp.zeros((3, w // NDEV), F32)], axis=0)
    mod_all, s_mat, small_all = _mod_forward(
        jnp.broadcast_to(c, (8, d)), jnp.broadcast_to(c_ctx[None], (8, d)), w_ada2, small_mine)
    mod = jnp.transpose(mod_all, (1, 0, 2)).reshape(16, NDEV * cols) + b_ada
    mod_lat = lax.dynamic_slice_in_dim(mod, me, 1, axis=0)
    sh_l, sc_l, gt_l = jnp.split(mod_lat, 3, axis=-1)
    sh_c, sc_c, _ = jnp.split(mod[8:9], 3, axis=-1)
    small = jnp.transpose(small_all, (1, 0, 2)).reshape(16, w)
    wca = _rows8(*[small[j] for j in range(0, 3)])
    wcb = _rows8(*[small[j] for j in range(3, 7)])
    lv = _rows8(small[7], small[9], small[11], small[8], small[10], small[12], b_conv_b)
    w_all, wo_all = _gather_weights([w_in2.astype(BF16), w_out2.astype(BF16)])
    wo = wo_all.reshape(d, d)
    wg = jnp.stack([
        jnp.concatenate([_blockdiag_groups(lru_wa[0, dr], gc), _blockdiag_groups(lru_wx[0, dr], gc)], axis=-1)
        for dr in range(2)]).astype(BF16)

    la = l + lc
    tm = 2 * t if l % (2 * t) == 0 else t
    p, hlt = _in_projection(x2, _rows8(norm_g, sc_l, sh_l), w_all, la, 0, tm, 2, "in_projection")
    p, hlt = _in_projection(ctx2, _rows8(norm_g, sc_c, sh_c), w_all, la, l, t, 2, "in_projection_ctx", prev=(p, hlt))
    sm = _shift_matrices(t)
    hf, hr = _lru_forward(p, wg, lv, wcb, sm, l, t)
    dn, catt, dout, part_mix = _mix_forward(x2, tgt2, p, hf, hr, wo, _rows8(gt_l, final_g), wca, t)
    dp, dhs, part_ca = _mix_backward(dout, p, hf, hr, wo, wca, l, t)
    dxb0, dwg0, part_l0 = _lru_backward(0, p, dhs, hf, wg, lv, wcb, sm, l, t)
    dp, dwg1, part_l1 = _lru_backward(1, p, dhs, hr, wg, lv, wcb, sm, l, t, dxb_other=dxb0, dp=dp)
    tk = 3 * t if la % (3 * t) == 0 else t
    g_wout = _weight_grad_t(catt, dout, 2, 1, 4 * t if l % (4 * t) == 0 else t, "grad_w_out")
    g_wout = g_wout.reshape(NDEV, wo_rows, d)
    g_win = _weight_grad_t(hlt, dp, 1, NDEV, tk, "grad_w_in")
    grad_x, part_lat = _input_backward(dp, w_all, x2, _rows8(norm_g, sc_l), 0, tm, 2, "input_backward", dn=dn)
    (part_ctx,) = _input_backward(dp, w_all, ctx2, _rows8(norm_g, sc_c), l, t, 2, "input_backward_ctx")
    part_in = jnp.concatenate([part_lat[0:2], part_ctx[0:2], (part_lat[2] + part_ctx[2])[None]], axis=0)

    dwa = jnp.stack([_blockdiag_extract(dwg0[:, :, :gc], dh), _blockdiag_extract(dwg1[:, :, :gc], dh)])
    dwx = jnp.stack([_blockdiag_extract(dwg0[:, :, gc:], dh), _blockdiag_extract(dwg1[:, :, gc:], dh)])
    lru_part = jnp.stack([dwa, dwx]).reshape(NDEV, -1, 128)
    sc_win, sc_wout, sc_lru = _scatter_partials([g_win, g_wout, lru_part])
    zeros_d = jnp.zeros((d,), F32)
    pieces = [
        jnp.concatenate([part_in[0], part_in[1], part_mix[1]]),
        jnp.concatenate([part_in[2], part_in[3], zeros_d]),
        part_in[4], part_mix[0], part_ca[0:3], part_l1[4:8], part_l1[3],
        jnp.stack([part_l0[0], part_l1[0]]), jnp.stack([part_l0[1], part_l1[1]]),
        jnp.stack([part_l0[2], part_l1[2]]), part_mix[2, 0:1],
    ]
    shapes = [(3 * d,), (3 * d,), (d,), (d,), (3, w), (4, w), (w,), (2, w), (2, w), (2, w), (1,)]
    sig_cc = jax.nn.sigmoid(c_ctx)
    dsilu_cc = jnp.broadcast_to((sig_cc * (1.0 + c_ctx * (1.0 - sig_cc)))[None], (8, d))
    psum, pall, lru_sum, g_cctx8 = _reduce_small(_pack(pieces), sc_lru, w_ada2, dsilu_cc)
    (g_modl, g_modc, g_norm, g_final, g_ca, g_cb, g_bcb, g_ba, g_bx, g_lam, loss1) = _unpack(psum, shapes)
    loss = loss1[0]
    g_cctx = g_cctx8[0]
    g_bada = (g_modl + g_modc)[None]
    g_lru = lru_sum.reshape(2, 2, N_HEADS, dh, dh)
    g_wa, g_wx = g_lru[0][None], g_lru[1][None]
    wsl = w // NDEV
    mine = lambda a: lax.dynamic_slice_in_dim(a, me * wsl, wsl, axis=-1)
    g_ca_m, g_cb_m, g_ba_m, g_bx_m, g_lam_m = (mine(g_ca)[None], mine(g_cb)[None], mine(g_ba)[None],
                                               mine(g_bx)[None], mine(g_lam)[None])
    g_norm, g_bcb = g_norm[None], g_bcb[None]

    cb = cols // 128
    per_dev = pall[:, :3 * d // 128].reshape(NDEV, NDEV, cols)
    dmod_lat = lax.dynamic_slice_in_dim(per_dev, me, 1, axis=1)[:, 0]
    dmod_ctx = lax.dynamic_slice_in_dim(g_modc.reshape(NDEV, cols), me, 1, axis=0)
    dmod16 = jnp.concatenate([dmod_lat, dmod_ctx, jnp.zeros((7, cols), F32)], axis=0)
    tr_ada = 256 if d % 256 == 0 else d
    g_wada, d_wada, m_wada, v_wada = _adamw_ada(s_mat.T, dmod16, w_ada2, m_w_ada[0], v_w_ada[0], tr_ada)
    g_win2, d_win, m_win, v_win = _adamw_scattered(sc_win, w_in2, m_w_in[0], v_w_in[0], tr_ada)
    tr_out = 64 if wo_rows % 64 == 0 else wo_rows
    g_wout2, d_wout, m_wout, v_wout = _adamw_scattered(sc_wout, w_out2, m_w_out[0], v_w_out[0], tr_out)

    small_w = [c_ctx, norm_g, b_ada, w_conv_a, w_conv_b, b_conv_b, lru_wa, lru_ba, lru_wx, lru_bx, lru_lambda, final_g]
    small_m = [m_c_ctx, m_norm_g, m_b_ada, m_w_conv_a, m_w_conv_b, m_b_conv_b, m_lru_wa, m_lru_ba, m_lru_wx,
               m_lru_bx, m_lru_lambda, m_final_g]
    small_v = [v_c_ctx, v_norm_g, v_b_ada, v_w_conv_a, v_w_conv_b, v_b_conv_b, v_lru_wa, v_lru_ba, v_lru_wx,
               v_lru_bx, v_lru_lambda, v_final_g]
    small_g = [g_cctx, g_norm, g_bada, g_ca_m, g_cb_m, g_bcb, g_wa, g_ba_m, g_wx, g_bx_m, g_lam_m, g_final]
    sshapes = [a.shape for a in small_w]
    d_s, m_s, v_s = _adamw_packed(_pack(small_g), _pack(small_w), _pack(small_m), _pack(small_v))
    d_s, m_s, v_s = _unpack(d_s, sshapes), _unpack(m_s, sshapes), _unpack(v_s, sshapes)
    small_g = [jnp.reshape(a, s) for a, s in zip(small_g, sshapes)]

    def weights(small_list, ada, win, wout):
        (cctx_, norm_, bada_, ca_, cb_, bcb_, wa_, ba_, wx_, bx_, lam_, final_) = small_list
        return [cctx_, norm_, ada[None], bada_, win[None], ca_, cb_, bcb_, wa_, ba_, wx_, bx_, lam_, wout[None], final_]

    return (loss, grad_x[None],
            *weights(small_g, g_wada, g_win2, g_wout2), *weights(d_s, d_wada, d_win, d_wout),
            *weights(m_s, m_wada, m_win, m_wout), *weights(v_s, v_wada, v_win, v_wout))
```

```python
import functools

import jax
import jax.numpy as jnp
from jax import lax
from jax.experimental import pallas as pl
from jax.experimental.pallas import tpu as pltpu

F32 = jnp.float32
BF16 = jnp.bfloat16
MESH = pl.DeviceIdType.MESH
NDEV = 8
GRID_W = 64
N_HEADS = 16
LRU_C = 8.0
EPS = 1e-6
MXU_WIDTH = 256
VMEM_LIMIT = 60 * 1024 * 1024

ADAM_LR = 0.001
ADAM_B1 = 0.9
ADAM_B2 = 0.999
ADAM_EPS = 1e-08
ADAM_WD = 0.01
ADAM_STEP = 10
ADAM_C1 = 1.0 - ADAM_B1 ** ADAM_STEP
ADAM_C2 = 1.0 - ADAM_B2 ** ADAM_STEP

HIGHEST = lax.Precision.HIGHEST
ANY = pl.BlockSpec(memory_space=pl.ANY)
VMEM = pl.BlockSpec(memory_space=pltpu.VMEM)


def _call(body, **kw):
    return pl.pallas_call(body, **kw)


def _params(sem=None, vmem=VMEM_LIMIT):
    return pltpu.CompilerParams(dimension_semantics=sem, vmem_limit_bytes=vmem)


def _my_pos():
    return lax.axis_index("x"), lax.axis_index("y"), lax.axis_index("c")


def _idx(pos):
    return 4 * pos[0] + 2 * pos[1] + pos[2]


def _peer(k):
    x, y, c = _my_pos()
    return ((1 - x) if (k >> 2) & 1 else x, (1 - y) if (k >> 1) & 1 else y, (1 - c) if k & 1 else c)


def _exchange_vmem(src_ref, dst_ref, send_sems, recv_sems, base):
    me = _idx(_my_pos())
    sends = []
    for k in range(1, NDEV):
        cp = pltpu.make_async_remote_copy(
            src_ref=src_ref, dst_ref=dst_ref.at[me], send_sem=send_sems.at[base + k - 1],
            recv_sem=recv_sems.at[base + k - 1], device_id=_peer(k), device_id_type=MESH)
        cp.start()
        sends.append(cp)
    dst_ref[me] = src_ref[...]
    for k in range(1, NDEV):
        peer = _peer(k)
        pltpu.make_async_remote_copy(
            src_ref=src_ref, dst_ref=dst_ref.at[_idx(peer)], send_sem=send_sems.at[base + k - 1],
            recv_sem=recv_sems.at[base + k - 1], device_id=peer, device_id_type=MESH).wait_recv()
    for cp in sends:
        cp.wait_send()


def _sigmoid(z):
    return 1.0 / (1.0 + jnp.exp(-z))


def _softplus(x):
    return jnp.maximum(x, 0.0) + jnp.log1p(jnp.exp(-jnp.abs(x)))


def _one_minus_sq(a, la):
    u = 2.0 * la
    series = -(u * (1.0 + u * (0.5 + u * (1.0 / 6.0))))
    return jnp.where(u > -0.03, series, 1.0 - a * a)


def _dot(a, b):
    return jnp.dot(a, b, preferred_element_type=F32)


def _dot_nt(a, b):
    return lax.dot_general(a, b, (((1,), (1,)), ((), ())), preferred_element_type=F32)


def _rows(shape):
    return lax.broadcasted_iota(jnp.int32, shape, 0)


def _down(x, k, pos):
    return jnp.where(pos >= k, pltpu.roll(x, k, 0), 0.0)


def _up(x, k, pos, rowlen):
    return jnp.where(pos + k < rowlen, pltpu.roll(x, x.shape[0] - k, 0), 0.0)


def _pos_rowlen(shape, is_ctx):
    t = _rows(shape)
    pos = jnp.where(is_ctx, t, t & (GRID_W - 1))
    rowlen = jnp.where(is_ctx, shape[0], GRID_W)
    return pos, rowlen


def _shift_matrices(t):
    r = lax.broadcasted_iota(jnp.int32, (t, t), 0)
    c = lax.broadcasted_iota(jnp.int32, (t, t), 1)
    kinds = []
    for rowlen in (GRID_W, t):
        pos = r % rowlen
        kinds.append(jnp.stack([(c == r - 2) & (pos >= 2), (c == r - 1) & (pos >= 1),
                                (c == r + 1) & (pos + 1 < rowlen), (c == r + 2) & (pos + 2 < rowlen)]))
    return jnp.stack(kinds).astype(BF16)


def _conv4(v16, sm_ref, w_ref):
    taps = (_dot(sm_ref[0], v16), _dot(sm_ref[1], v16), v16.astype(F32), _dot(sm_ref[2], v16))
    out = w_ref[0:1, :] * taps[0] + w_ref[1:2, :] * taps[1] + w_ref[2:3, :] * taps[2] + w_ref[3:4, :] * taps[3]
    return out, taps


def _conv4_t(dy, sm_ref, w_ref):
    dy16 = dy.astype(BF16)
    return (w_ref[0:1, :] * _dot(sm_ref[3], dy16) + w_ref[1:2, :] * _dot(sm_ref[2], dy16)
            + w_ref[2:3, :] * dy + w_ref[3:4, :] * _dot(sm_ref[1], dy16))


def _conv3(t, w_ref, pos, rowlen):
    return w_ref[0:1, :] * _down(t, 1, pos) + w_ref[1:2, :] * t + w_ref[2:3, :] * _up(t, 1, pos, rowlen)


def _conv3_t(dz, w_ref, pos, rowlen):
    return w_ref[0:1, :] * _up(dz, 1, pos, rowlen) + w_ref[1:2, :] * dz + w_ref[2:3, :] * _down(dz, 1, pos)


def _chunk_scan(a, b, reverse):
    row = _rows(a.shape)
    for s in (1, 2, 4):
        if reverse:
            m = row < 8 - s
            sh = 8 - s
        else:
            m = row >= s
            sh = s
        a_s = jnp.where(m, pltpu.roll(a, sh, 0), 1.0)
        b_s = jnp.where(m, pltpu.roll(b, sh, 0), 0.0)
        b = b + a * b_s
        a = a * a_s
    return a, b


def _scan_tile(a_ref, b_ref, out_ref, carry, reverse):
    t, w = a_ref.shape
    nchunk = t // 8

    def step(k, h):
        r0 = pl.multiple_of((nchunk - 1 - k if reverse else k) * 8, 8)
        ca, cb = _chunk_scan(a_ref[pl.ds(r0, 8), :], b_ref[pl.ds(r0, 8), :], reverse)
        hh = ca * h + cb
        out_ref[pl.ds(r0, 8), :] = hh
        return jnp.broadcast_to(hh[0:1, :] if reverse else hh[7:8, :], (8, w))

    return lax.fori_loop(0, nchunk, step, carry)


def _scan_tile_backward(a_ref, b_ref, g_ref, carry, reverse):
    t, w = a_ref.shape
    nchunk = t // 8

    def step(k, u_next):
        r0 = pl.multiple_of((nchunk - 1 - k if reverse else k) * 8, 8)
        ca, cb = _chunk_scan(a_ref[pl.ds(r0, 8), :], b_ref[pl.ds(r0, 8), :], reverse)
        u = ca * u_next + cb
        r8 = _rows((8, w))
        if reverse:
            shifted = jnp.where(r8 < 7, pltpu.roll(u, 7, 0), u_next)
        else:
            shifted = jnp.where(r8 >= 1, pltpu.roll(u, 1, 0), u_next)
        g_ref[pl.ds(r0, 8), :] = g_ref[pl.ds(r0, 8), :] + shifted
        return jnp.broadcast_to(u[0:1, :] if reverse else u[7:8, :], (8, w))

    return lax.fori_loop(0, nchunk, step, carry)


def _lru_coef(xb, wg_ref, d, ba, bx, lam, gc):
    w = xb.shape[1]
    xb16 = xb.astype(BF16)
    zr, zi = [], []
    for g in range(w // gc):
        z = _dot(xb16[:, g * gc:(g + 1) * gc], wg_ref[d, g])
        zr.append(z[:, :gc])
        zi.append(z[:, gc:])
    zr = zr[0] if len(zr) == 1 else jnp.concatenate(zr, axis=-1)
    zi = zi[0] if len(zi) == 1 else jnp.concatenate(zi, axis=-1)
    r = _sigmoid(zr + ba)
    ig = _sigmoid(zi + bx)
    sp = _softplus(-lam)
    la = r * (-LRU_C * sp)
    a = jnp.exp(la)
    s = jnp.sqrt(_one_minus_sq(a, la))
    return a, s, r, ig, sp


def _adamw(w, g, m, v):
    m2 = ADAM_B1 * m + (1.0 - ADAM_B1) * g
    v2 = ADAM_B2 * v + (1.0 - ADAM_B2) * (g * g)
    m_hat = m2 / ADAM_C1
    v_hat = v2 / ADAM_C2
    delta = -ADAM_LR * (m_hat / (jnp.sqrt(v_hat) + ADAM_EPS) + ADAM_WD * w)
    return delta, m2, v2


def _mod_forward(c8, cctx8, w_ada, small):
    d = c8.shape[1]
    cols = w_ada.shape[1]

    def body(c_ref, cctx_ref, w_ref, sm_ref, mod_ref, s_ref, sm_all, cbuf, mod_my, send_sems, recv_sems):
        _exchange_vmem(sm_ref, sm_all, send_sems, recv_sems, 2 * (NDEV - 1))
        _exchange_vmem(c_ref, cbuf, send_sems, recv_sems, 0)
        row = _rows((8, d))
        c_all = jnp.zeros((8, d), F32)
        for b in range(NDEV):
            c_all = jnp.where(row == b, cbuf[b], c_all)
        cc = cctx_ref[...]
        s_top = c_all * _sigmoid(c_all)
        s_bot = jnp.where(row == 0, cc * _sigmoid(cc), 0.0)
        s = jnp.concatenate([s_top, s_bot], axis=0)
        s_ref[...] = s
        mod_my[...] = jnp.dot(s, w_ref[...], precision=HIGHEST, preferred_element_type=F32)
        _exchange_vmem(mod_my, mod_ref, send_sems, recv_sems, NDEV - 1)

    return _call(
        body, name="mod_forward",
        out_shape=(jax.ShapeDtypeStruct((NDEV, 16, cols), F32), jax.ShapeDtypeStruct((16, d), F32),
                   jax.ShapeDtypeStruct((NDEV,) + small.shape, F32)),
        in_specs=[VMEM] * 4, out_specs=(VMEM,) * 3,
        scratch_shapes=[pltpu.VMEM((NDEV, 8, d), F32), pltpu.VMEM((16, cols), F32),
                        pltpu.SemaphoreType.DMA((3 * (NDEV - 1),)), pltpu.SemaphoreType.DMA((3 * (NDEV - 1),))],
        compiler_params=_params(),
    )(c8, cctx8, w_ada, small)


def _gather_weights(shards):
    n = len(shards)

    def body(*refs):
        ins, outs = refs[:n], refs[n:2 * n]
        send_sems, recv_sems, local_sems = refs[2 * n:]
        x, y, c = _my_pos()
        me, sibling = (x, y, c), (x, y, 1 - c)
        chips = [(1 - x, y), (x, 1 - y), (1 - x, 1 - y)]
        waits = []
        for a in range(n):
            src, out = ins[a], outs[a]

            def copy(k, block, to, from_src=False, src=src, out=out, a=a):
                return pltpu.make_async_remote_copy(
                    src_ref=src if from_src else out.at[_idx(block)], dst_ref=out.at[_idx(block)],
                    send_sem=send_sems.at[7 * a + k], recv_sem=recv_sems.at[7 * a + k],
                    device_id=to, device_id_type=MESH)

            mine = pltpu.make_async_copy(src, out.at[_idx(me)], local_sems.at[a])
            mine.start()
            first = [copy(0, me, sibling, True)]
            first += [copy(1 + j, me, (*chip, c), True) for j, chip in enumerate(chips)]
            for cp in first:
                cp.start()
            waits.append((copy, mine, first))
        for a in range(n):
            copy, mine, first = waits[a]
            passed = [copy(4 + j, (*chip, c), sibling) for j, chip in enumerate(chips)]
            for j, chip in enumerate(chips):
                copy(1 + j, (*chip, c), me).wait_recv()
                passed[j].start()
            copy(0, sibling, me).wait_recv()
            for j, chip in enumerate(chips):
                copy(4 + j, (*chip, 1 - c), me).wait_recv()
            for cp in first + passed:
                cp.wait_send()
            mine.wait()

    return _call(
        body, name="gather_weights",
        out_shape=tuple(jax.ShapeDtypeStruct((NDEV,) + s.shape, s.dtype) for s in shards),
        in_specs=[ANY] * n, out_specs=tuple([ANY] * n),
        scratch_shapes=[pltpu.SemaphoreType.DMA((7 * n,)), pltpu.SemaphoreType.DMA((7 * n,)),
                        pltpu.SemaphoreType.DMA((n,))],
        compiler_params=_params(),
    )(*shards)


def _scatter_copies(src_ref, dst_ref, send_sems, recv_sems):
    me = _idx(_my_pos())
    copies = [pltpu.make_async_copy(src_ref.at[me], dst_ref.at[0], send_sems.at[0])]
    for k in range(1, NDEV):
        peer = _peer(k)
        copies.append(pltpu.make_async_remote_copy(
            src_ref=src_ref.at[_idx(peer)], dst_ref=dst_ref.at[k], send_sem=send_sems.at[k],
            recv_sem=recv_sems.at[k], device_id=peer, device_id_type=MESH))
    for cp in copies:
        cp.start()
    return copies


def _scatter_wait(copies):
    copies[0].wait()
    for cp in copies[1:]:
        cp.wait_recv()
    for cp in copies[1:]:
        cp.wait_send()


def _xor_distance(step, c):
    b0, b1, b2 = step & 1, (step >> 1) & 1, (step >> 2) & 1
    return jnp.where(c == 1, b0 | (b2 << 1) | (b1 << 2), step)


def _peer_at(dist):
    x, y, c = _my_pos()
    return (x ^ ((dist >> 2) & 1), y ^ ((dist >> 1) & 1), c ^ (dist & 1))


def _reduce_small(packed, lru_parts, w_ada, dsilu_cctx):
    rp = packed.shape[0]
    rl = lru_parts.shape[1]
    d, cols = w_ada.shape
    assert cols % 128 == 0
    cb = cols // 128

    def body(p_ref, l_ref, w_ref, ds_ref, sum_ref, all_ref, lru_ref, cctx_ref,
             lbuf, lsum, cpart, call, send_sems, recv_sems, lsend, lrecv):
        me = _idx(_my_pos())
        scattered = _scatter_copies(l_ref, lbuf, lsend, lrecv)
        _exchange_vmem(p_ref, all_ref, send_sems, recv_sems, 0)
        acc = all_ref[0]
        for j in range(1, NDEV):
            acc = acc + all_ref[j]
        sum_ref[...] = acc
        _scatter_wait(scattered)
        red = lbuf[0]
        for k in range(1, NDEV):
            red = red + lbuf[k]
        lsum[...] = red
        _exchange_vmem(lsum, lru_ref, send_sems, recv_sems, NDEV - 1)
        part = jnp.zeros((8, d), F32)
        for q in range(cb):
            dm = jnp.broadcast_to(sum_ref[pl.ds((NDEV + me) * cb + q, 1), :], (8, 128))
            part = part + lax.dot_general(dm, w_ref[:, q * 128:(q + 1) * 128],
                                          (((1,), (1,)), ((), ())), precision=HIGHEST,
                                          preferred_element_type=F32)
        cpart[...] = part
        _exchange_vmem(cpart, call, send_sems, recv_sems, 2 * (NDEV - 1))
        tot = call[0]
        for j in range(1, NDEV):
            tot = tot + call[j]
        cctx_ref[...] = tot * ds_ref[...]

    return _call(
        body, name="reduce_small",
        out_shape=(jax.ShapeDtypeStruct((rp, 128), F32), jax.ShapeDtypeStruct((NDEV, rp, 128), F32),
                   jax.ShapeDtypeStruct((NDEV, rl, 128), F32), jax.ShapeDtypeStruct((8, d), F32)),
        in_specs=[VMEM] * 4, out_specs=(VMEM,) * 4,
        scratch_shapes=[pltpu.VMEM((NDEV, rl, 128), F32), pltpu.VMEM((rl, 128), F32), pltpu.VMEM((8, d), F32),
                        pltpu.VMEM((NDEV, 8, d), F32),
                        pltpu.SemaphoreType.DMA((3 * (NDEV - 1),)), pltpu.SemaphoreType.DMA((3 * (NDEV - 1),)),
                        pltpu.SemaphoreType.DMA((NDEV,)), pltpu.SemaphoreType.DMA((NDEV,))],
        compiler_params=_params(),
    )(packed, lru_parts, w_ada, dsilu_cctx)


def _in_projection(src, mv, w_all, la, row0, tm, nbj, name, prev=None):
    rows, d = src.shape
    nb, _, bw = w_all.shape
    blk0 = row0 // tm

    def body(*refs):
        x_ref, mv_ref, w_ref = refs[:3]
        p_ref, ht_ref, h_s = refs[-3:]

        @pl.when(pl.program_id(1) == 0)
        def _():
            xf = x_ref[...]
            r = lax.rsqrt(jnp.mean(xf * xf, axis=-1, keepdims=True) + EPS)
            h = xf * r * (mv_ref[0:1, :] * (1.0 + mv_ref[1:2, :])) + mv_ref[2:3, :]
            h_s[...] = h.astype(BF16)
            ht_ref[...] = h.T.astype(BF16)

        h = h_s[...]
        for q in range(nbj):
            p_ref[:, q * bw:(q + 1) * bw] = _dot(h, w_ref[q]).astype(BF16)

    in_specs = [pl.BlockSpec((tm, d), lambda i, j: (i, 0)),
                pl.BlockSpec((8, d), lambda i, j: (0, 0)),
                pl.BlockSpec((nbj, d, bw), lambda i, j: (j, 0, 0))]
    args = [src, mv, w_all]
    aliases = {}
    if prev is not None:
        in_specs += [ANY, ANY]
        args += list(prev)
        aliases = {3: 0, 4: 1}
    return _call(
        body, name=name,
        grid=(rows // tm, nb // nbj),
        out_shape=(jax.ShapeDtypeStruct((la, nb * bw), BF16), jax.ShapeDtypeStruct((d, la), BF16)),
        in_specs=in_specs,
        out_specs=(pl.BlockSpec((tm, nbj * bw), lambda i, j: (blk0 + i, j)),
                   pl.BlockSpec((d, tm), lambda i, j: (0, blk0 + i))),
        scratch_shapes=[pltpu.VMEM((tm, d), BF16)],
        input_output_aliases=aliases,
        compiler_params=_params(("arbitrary", "arbitrary")),
    )(*args)


def _lru_forward(p, wg, lv, wcb, sm, l, t):
    la = p.shape[0]
    w = lv.shape[1]
    gc = wg.shape[2]
    nt = l // t

    def body(vf_ref, vr_ref, wg_ref, lv_ref, wcb_ref, sm_ref, hf_ref, hr_ref, a_s, b_s, carry):
        @pl.when(pl.program_id(0) == 0)
        def _():
            carry[...] = jnp.zeros_like(carry)

        for dr, (v_ref, h_ref) in enumerate(((vf_ref, hf_ref), (vr_ref, hr_ref))):
            xb, _ = _conv4(v_ref[...], sm_ref, wcb_ref)
            xb = xb + lv_ref[6:7, :]
            a, s, _, ig, _ = _lru_coef(xb, wg_ref, dr, lv_ref[3 * dr:3 * dr + 1, :],
                                       lv_ref[3 * dr + 1:3 * dr + 2, :], lv_ref[3 * dr + 2:3 * dr + 3, :], gc)
            a_s[...] = a
            b_s[...] = s * (ig * xb)
            carry[dr] = _scan_tile(a_s, b_s, h_ref, carry[dr], dr == 1)

    full = lambda shape: pl.BlockSpec(shape, lambda i: (0,) * len(shape))
    fmap = lambda i: (jnp.where(i == 0, nt, i - 1), 0)
    rmap = lambda i: (jnp.where(i == 0, nt, nt - i), 0)
    vcol = 4
    return _call(
        body, name="lru_forward",
        grid=(nt + 1,),
        out_shape=(jax.ShapeDtypeStruct((la, w), F32), jax.ShapeDtypeStruct((la, w), F32)),
        in_specs=[pl.BlockSpec((t, w), lambda i: (jnp.where(i == 0, nt, i - 1), vcol)),
                  pl.BlockSpec((t, w), lambda i: (jnp.where(i == 0, nt, nt - i), vcol)),
                  full(wg.shape), full(lv.shape), full(wcb.shape),
                  pl.BlockSpec((None, 4, t, t), lambda i: (jnp.where(i == 0, 1, 0), 0, 0, 0))],
        out_specs=(pl.BlockSpec((t, w), fmap), pl.BlockSpec((t, w), rmap)),
        scratch_shapes=[pltpu.VMEM((t, w), F32), pltpu.VMEM((t, w), F32), pltpu.VMEM((2, 8, w), F32)],
        compiler_params=_params(("arbitrary",)),
    )(p, p, wg, lv, wcb, sm)


def _mix_gates(p_refs, hf_ref, hr_ref, wca_ref, t, w):
    bl, cl, ul, gl, ql = [r[...].astype(F32) for r in p_refs]
    pos, rowlen = _pos_rowlen((t, w), False)
    tt = cl * ul
    z = _conv3(tt, wca_ref, pos, rowlen)
    sig_g = _sigmoid(gl)
    sig_q = _sigmoid(ql)
    ylru = hf_ref[...] + hr_ref[...]
    return bl, cl, ul, gl, ql, tt, z, sig_g, sig_q, ylru, pos, rowlen


def _p_specs(t, w, nt):
    return [pl.BlockSpec((t, w), functools.partial(lambda i, s: (jnp.minimum(i, nt - 1), s), s=s))
            for s in (0, 1, 2, 3, 5)]


def _mix_forward(x, tgt, p, hf, hr, wo, ov, wca, t):
    l, d = x.shape
    w = d // 2
    nt = l // t

    def body(x_ref, tg_ref, b_ref, c_ref, u_ref, g_ref, q_ref, hf_ref, hr_ref, wo_ref, ov_ref, wca_ref,
             dn_ref, ct_ref, do_ref, part_ref):
        i = pl.program_id(0)
        bl, _, _, gl, ql, _, z, sig_g, sig_q, ylru, _, _ = _mix_gates(
            (b_ref, c_ref, u_ref, g_ref, q_ref), hf_ref, hr_ref, wca_ref, t, w)
        ya = bl * z * (gl * sig_g)
        yb = ylru * (ql * sig_q)
        ct_ref[0:w, :] = ya.T.astype(BF16)
        ct_ref[w:, :] = yb.T.astype(BF16)
        out = _dot(ya.astype(BF16), wo_ref[0:w, :]) + _dot(yb.astype(BF16), wo_ref[w:, :])
        gate, fg = ov_ref[0:1, :], ov_ref[1:2, :]
        n = x_ref[...] + gate * out
        rr = lax.rsqrt(jnp.mean(n * n, axis=-1, keepdims=True) + EPS)
        nh = n * rr
        e = nh * fg - tg_ref[...]
        loss = 0.5 * jnp.sum(jnp.mean(e * e, axis=-1, keepdims=True), axis=0, keepdims=True)
        dy = e * (1.0 / d)
        dnh = dy * fg
        dn = rr * (dnh - nh * jnp.mean(dnh * nh, axis=-1, keepdims=True))
        dn_ref[...] = dn
        do_ref[...] = (dn * gate).astype(BF16)

        @pl.when(i == 0)
        def _():
            part_ref[...] = jnp.zeros_like(part_ref)

        part_ref[0:1, :] += jnp.sum(dy * nh, axis=0, keepdims=True)
        part_ref[1:2, :] += jnp.sum(dn * out, axis=0, keepdims=True)
        part_ref[2:3, :] += jnp.broadcast_to(loss, (1, d))

    tile = lambda cols: pl.BlockSpec((t, cols), lambda i: (i, 0))
    full = lambda shape: pl.BlockSpec(shape, lambda i: (0,) * len(shape))
    return _call(
        body, name="mix_forward",
        grid=(nt,),
        out_shape=(jax.ShapeDtypeStruct((l, d), F32), jax.ShapeDtypeStruct((d, l), BF16),
                   jax.ShapeDtypeStruct((l, d), BF16), jax.ShapeDtypeStruct((8, d), F32)),
        in_specs=[tile(d), tile(d)] + _p_specs(t, w, nt) + [tile(w), tile(w),
                  pl.BlockSpec((d, d), lambda i: (0, 0), pipeline_mode=pl.Buffered(1)),
                  full(ov.shape), full(wca.shape)],
        out_specs=(tile(d), pl.BlockSpec((d, t), lambda i: (0, i)), tile(d), full((8, d))),
        compiler_params=_params(("arbitrary",)),
    )(x, tgt, p, p, p, p, p, hf, hr, wo, ov, wca)


def _mix_backward(dout, p, hf, hr, wo, wca, l, t):
    d = dout.shape[1]
    w = d // 2
    nt = l // t
    la = p.shape[0]

    def body(do_ref, b_ref, c_ref, u_ref, g_ref, q_ref, hf_ref, hr_ref, wo_ref, wca_ref,
             dp_ref, dh_ref, part_ref):
        i = pl.program_id(0)

        @pl.when(i == 0)
        def _():
            part_ref[...] = jnp.zeros_like(part_ref)

        @pl.when(i == nt)
        def _():
            dp_ref[...] = jnp.zeros_like(dp_ref)

        @pl.when(i < nt)
        def _():
            bl, cl, ul, gl, ql, tt, z, sig_g, sig_q, ylru, pos, rowlen = _mix_gates(
                (b_ref, c_ref, u_ref, g_ref, q_ref), hf_ref, hr_ref, wca_ref, t, w)
            do = do_ref[...]
            dya = _dot_nt(do, wo_ref[0:w, :])
            dyb = _dot_nt(do, wo_ref[w:, :])
            sg = gl * sig_g
            dz = dya * bl * sg
            dt = _conv3_t(dz, wca_ref, pos, rowlen)
            dp_ref[:, 0:w] = (dya * z * sg).astype(BF16)
            dp_ref[:, w:2 * w] = (dt * ul).astype(BF16)
            dp_ref[:, 2 * w:3 * w] = (dt * cl).astype(BF16)
            dp_ref[:, 3 * w:4 * w] = (dya * bl * z * (sig_g * (1.0 + gl * (1.0 - sig_g)))).astype(BF16)
            dp_ref[:, 4 * w:5 * w] = jnp.zeros((t, w), BF16)
            dp_ref[:, 5 * w:6 * w] = (dyb * ylru * (sig_q * (1.0 + ql * (1.0 - sig_q)))).astype(BF16)
            dh_ref[...] = dyb * (ql * sig_q)
            part_ref[0:1, :] += jnp.sum(dz * _down(tt, 1, pos), axis=0, keepdims=True)
            part_ref[1:2, :] += jnp.sum(dz * tt, axis=0, keepdims=True)
            part_ref[2:3, :] += jnp.sum(dz * _up(tt, 1, pos, rowlen), axis=0, keepdims=True)

    clamp = lambda cols: pl.BlockSpec((t, cols), lambda i: (jnp.minimum(i, nt - 1), 0))
    full = lambda shape: pl.BlockSpec(shape, lambda i: (0,) * len(shape))
    return _call(
        body, name="mix_backward",
        grid=(nt + 1,),
        out_shape=(jax.ShapeDtypeStruct((la, 6 * w), BF16), jax.ShapeDtypeStruct((l, w), F32),
                   jax.ShapeDtypeStruct((8, w), F32)),
        in_specs=[clamp(d)] + _p_specs(t, w, nt) + [clamp(w), clamp(w),
                  pl.BlockSpec((d, d), lambda i: (0, 0), pipeline_mode=pl.Buffered(1)), full(wca.shape)],
        out_specs=(pl.BlockSpec((t, 6 * w), lambda i: (i, 0)), clamp(w), full((8, w))),
        compiler_params=_params(("arbitrary",)),
    )(dout, p, p, p, p, p, hf, hr, wo, wca)


def _lru_backward(direction, p, dhs, hs, wg, lv, wcb, sm, l, t, dxb_other=None, dp=None):
    la, w = hs.shape
    gc = wg.shape[2]
    ng = w // gc
    nt = l // t
    nblk8 = la // 8
    last = direction == 1

    if direction == 0:
        tile = lambda i: jnp.where(i == nt, nt, nt - 1 - i)
        halo = lambda i: jnp.where(tile(i) == 0, nblk8 - 1, tile(i) * (t // 8) - 1)
    else:
        tile = lambda i: i
        halo = lambda i: jnp.minimum((i + 1) * (t // 8), nblk8 - 1)

    def body(*refs):
        if last:
            (v_ref, dh_ref, hs_ref, halo_ref, wg_ref, lv_ref, wcb_ref, sm_ref, dxo_ref, _,
             out_ref, dwg_ref, part_ref, a_s, b_s, g_s, carry) = refs
        else:
            (v_ref, dh_ref, hs_ref, halo_ref, wg_ref, lv_ref, wcb_ref, sm_ref,
             out_ref, dwg_ref, part_ref, a_s, b_s, g_s, carry) = refs
        i = pl.program_id(0)
        is_ctx = i == nt

        @pl.when(i == 0)
        def _():
            carry[...] = jnp.zeros_like(carry)
            dwg_ref[...] = jnp.zeros_like(dwg_ref)
            part_ref[...] = jnp.zeros_like(part_ref)

        row = _rows((t, w))
        xb, taps = _conv4(v_ref[...], sm_ref, wcb_ref)
        xb = xb + lv_ref[6:7, :]
        lam = lv_ref[3 * direction + 2:3 * direction + 3, :]
        a, s, r, ig, sp = _lru_coef(xb, wg_ref, direction, lv_ref[3 * direction:3 * direction + 1, :],
                                    lv_ref[3 * direction + 1:3 * direction + 2, :], lam, gc)
        hs_t = hs_ref[...]
        if direction == 0:
            edge = jnp.where(is_ctx, 0.0, halo_ref[7:8, :])
            hprev = jnp.where(row == 0, edge, pltpu.roll(hs_t, 1, 0))
        else:
            edge = jnp.where(is_ctx, 0.0, halo_ref[0:1, :])
            hprev = jnp.where(row == t - 1, edge, pltpu.roll(hs_t, t - 1, 0))
        dh = jnp.where(is_ctx, 0.0, dh_ref[...])
        a_s[...] = a
        b_s[...] = a * dh
        g_s[...] = dh
        carry[...] = _scan_tile_backward(a_s, b_s, g_s, carry[...], direction == 0)

        g = g_s[...]
        ix = ig * xb
        gs = g * s
        dla = (g * a) * (hprev - ix * (a / s))
        dxb = gs * ig
        dzr = dla * (r * (1.0 - r)) * (-LRU_C * sp)
        dzi = gs * ix * (1.0 - ig)
        part_ref[0:1, :] += jnp.sum(dzr, axis=0, keepdims=True)
        part_ref[1:2, :] += jnp.sum(dzi, axis=0, keepdims=True)
        part_ref[2:3, :] += jnp.sum(dla * r, axis=0, keepdims=True) * (LRU_C * _sigmoid(-lam))
        pieces = []
        for gi in range(ng):
            sl = slice(gi * gc, (gi + 1) * gc)
            dz = jnp.concatenate([dzr[:, sl], dzi[:, sl]], axis=-1).astype(BF16)
            pieces.append(_dot_nt(dz, wg_ref[direction, gi]))
            dwg_ref[gi] += _dot(xb[:, sl].T.astype(BF16), dz)
        dxb = dxb + (pieces[0] if ng == 1 else jnp.concatenate(pieces, axis=-1))
        if not last:
            out_ref[...] = dxb
        else:
            dxb = dxb + dxo_ref[...]
            out_ref[...] = _conv4_t(dxb, sm_ref, wcb_ref).astype(BF16)
            part_ref[3:4, :] += jnp.sum(dxb, axis=0, keepdims=True)
            for j in range(4):
                part_ref[4 + j:5 + j, :] += jnp.sum(dxb * taps[j], axis=0, keepdims=True)

    full = lambda shape: pl.BlockSpec(shape, lambda i: (0,) * len(shape))
    in_specs = [pl.BlockSpec((t, w), lambda i: (tile(i), 4)),
                pl.BlockSpec((t, w), lambda i: (jnp.minimum(tile(i), nt - 1), 0)),
                pl.BlockSpec((t, w), lambda i: (tile(i), 0)),
                pl.BlockSpec((8, w), lambda i: (halo(i), 0)),
                full(wg.shape), full(lv.shape), full(wcb.shape),
                pl.BlockSpec((None, 4, t, t), lambda i: (jnp.where(i == nt, 1, 0), 0, 0, 0))]
    args = [p, dhs, hs, hs, wg, lv, wcb, sm]
    if last:
        in_specs += [pl.BlockSpec((t, w), lambda i: (tile(i), 0)), ANY]
        args += [dxb_other, dp]
        out0 = jax.ShapeDtypeStruct(dp.shape, dp.dtype)
        spec0 = pl.BlockSpec((t, w), lambda i: (tile(i), 4))
        aliases = {9: 0}
    else:
        out0 = jax.ShapeDtypeStruct((la, w), F32)
        spec0 = pl.BlockSpec((t, w), lambda i: (tile(i), 0))
        aliases = {}
    return _call(
        body, name="lru_backward_%d" % direction,
        grid=(nt + 1,),
        out_shape=(out0, jax.ShapeDtypeStruct((ng, gc, 2 * gc), F32), jax.ShapeDtypeStruct((8, w), F32)),
        in_specs=in_specs,
        out_specs=(spec0, full((ng, gc, 2 * gc)), full((8, w))),
        scratch_shapes=[pltpu.VMEM((t, w), F32), pltpu.VMEM((t, w), F32), pltpu.VMEM((t, w), F32),
                        pltpu.VMEM((8, w), F32)],
        input_output_aliases=aliases,
        compiler_params=_params(("arbitrary",)),
    )(*args)


def _weight_grad_t(at, b, nblk_m, nblk_n, tk, name):
    m, k = at.shape
    n = b.shape[1]
    bm, bn = m // nblk_m, n // nblk_n
    nk = k // tk

    def body(a_ref, b_ref, o_ref, acc):
        kk = pl.program_id(2)

        @pl.when(kk == 0)
        def _():
            acc[...] = jnp.zeros_like(acc)

        acc[...] += _dot(a_ref[...], b_ref[...])

        @pl.when(kk == nk - 1)
        def _():
            o_ref[...] = acc[...].astype(BF16)

    return _call(
        body, name=name,
        grid=(nblk_m, nblk_n, nk),
        out_shape=jax.ShapeDtypeStruct((nblk_m * nblk_n, bm, bn), BF16),
        in_specs=[pl.BlockSpec((bm, tk), lambda i, j, kk: (i, kk)),
                  pl.BlockSpec((tk, bn), lambda i, j, kk: (kk, j))],
        out_specs=pl.BlockSpec((None, bm, bn), lambda i, j, kk: (i * nblk_n + j, 0, 0)),
        scratch_shapes=[pltpu.VMEM((bm, bn), F32)],
        compiler_params=_params(("arbitrary", "arbitrary", "arbitrary")),
    )(at, b)


def _weight_grad_scatter(at, b, tk, name):
    m, k = at.shape
    n = b.shape[1]
    bn = n // NDEV
    nk = k // tk
    where = jnp.stack([_idx(_my_pos()), lax.axis_index("c")]).astype(jnp.int32)

    def body(w_ref, a_ref, b_ref, recv_ref, acc, sbuf, send_sems, recv_sems):
        s, kk = pl.program_id(0), pl.program_id(1)

        @pl.when(kk == 0)
        def _():
            acc[...] = _dot(a_ref[...], b_ref[...])

        @pl.when(kk > 0)
        def _():
            acc[...] += _dot(a_ref[...], b_ref[...])

        def push(step):
            dist = _xor_distance(step, w_ref[1])
            return pltpu.make_async_remote_copy(
                src_ref=sbuf.at[step % 2], dst_ref=recv_ref.at[dist], send_sem=send_sems.at[step],
                recv_sem=recv_sems.at[dist], device_id=_peer_at(dist), device_id_type=MESH)

        keep = pltpu.make_async_copy(sbuf.at[0], recv_ref.at[0], send_sems.at[0])

        @pl.when(kk == nk - 1)
        def _():
            @pl.when(s == 2)
            def _():
                keep.wait()

            @pl.when(s > 2)
            def _():
                push(s - 2).wait_send()

            sbuf[s % 2] = acc[...].astype(BF16)

            @pl.when(s == 0)
            def _():
                keep.start()

            @pl.when(s > 0)
            def _():
                push(s).start()

            @pl.when(s == NDEV - 1)
            def _():
                push(NDEV - 2).wait_send()
                push(NDEV - 1).wait_send()
                for dist in range(1, NDEV):
                    pltpu.make_async_remote_copy(
                        src_ref=sbuf.at[0], dst_ref=recv_ref.at[dist], send_sem=send_sems.at[0],
                        recv_sem=recv_sems.at[dist], device_id=_peer_at(dist), device_id_type=MESH).wait_recv()

    blk = lambda s, w_ref: w_ref[0] ^ _xor_distance(s, w_ref[1])
    return _call(
        body, name=name,
        grid_spec=pltpu.PrefetchScalarGridSpec(
            num_scalar_prefetch=1, grid=(NDEV, nk),
            in_specs=[pl.BlockSpec((m, tk), lambda s, kk, w_ref: (0, kk)),
                      pl.BlockSpec((tk, bn), lambda s, kk, w_ref: (kk, blk(s, w_ref)))],
            out_specs=ANY,
            scratch_shapes=[pltpu.VMEM((m, bn), F32), pltpu.VMEM((2, m, bn), BF16),
                            pltpu.SemaphoreType.DMA((NDEV,)), pltpu.SemaphoreType.DMA((NDEV,))]),
        out_shape=jax.ShapeDtypeStruct((NDEV, m, bn), BF16),
        compiler_params=_params(("arbitrary", "arbitrary")),
    )(where, at, b)


def _input_backward(dp, w_all, src, mv, row0, tm, nbk, name, dn=None, scatter=None):
    rows, d = src.shape
    nb, _, bw = w_all.shape
    nk = nb // nbk
    ni = rows // tm
    blk0 = row0 // tm
    latent = dn is not None
    exchange = scatter is not None
    n_in = 4 + latent + exchange

    def body(*refs):
        dp_ref, w_ref, x_ref, mv_ref = refs[:4]
        outs = refs[n_in:]
        part_ref = outs[latent]
        acc = outs[1 + latent + exchange]
        i, k = pl.program_id(0), pl.program_id(1)

        @pl.when((i == 0) & (k == 0))
        def _():
            part_ref[...] = jnp.zeros_like(part_ref)

        if exchange:
            src_ref, dst_ref = refs[n_in - 1], outs[1 + latent]
            send_sems, recv_sems = outs[-2:]
            me = _idx(_my_pos())
            copies = [pltpu.make_async_copy(src_ref.at[me], dst_ref.at[0], send_sems.at[0])]
            for q in range(1, NDEV):
                peer = _peer(q)
                copies.append(pltpu.make_async_remote_copy(
                    src_ref=src_ref.at[_idx(peer)], dst_ref=dst_ref.at[q], send_sem=send_sems.at[q],
                    recv_sem=recv_sems.at[q], device_id=peer, device_id_type=MESH))

            @pl.when((i == 0) & (k == 0))
            def _():
                for cp in copies:
                    cp.start()

        step = _dot_nt(dp_ref[:, 0:bw], w_ref[0])
        for q in range(1, nbk):
            step = step + _dot_nt(dp_ref[:, q * bw:(q + 1) * bw], w_ref[q])

        @pl.when(k == 0)
        def _():
            acc[...] = step

        @pl.when(k > 0)
        def _():
            acc[...] += step

        @pl.when(k == nk - 1)
        def _():
            xf = x_ref[...]
            r = lax.rsqrt(jnp.mean(xf * xf, axis=-1, keepdims=True) + EPS)
            xn = xf * r
            dhl = acc[...]
            gain, sc = mv_ref[0:1, :], mv_ref[1:2, :]
            dhx = jnp.sum(dhl * xn, axis=0, keepdims=True)
            part_ref[0:1, :] += jnp.sum(dhl, axis=0, keepdims=True)
            part_ref[1:2, :] += dhx * gain
            part_ref[2:3, :] += dhx * (1.0 + sc)
            if latent:
                dxn = dhl * (gain * (1.0 + sc))
                outs[0][...] = refs[4][...] + r * (dxn - xn * jnp.mean(dxn * xn, axis=-1, keepdims=True))

        if exchange:
            @pl.when((i == ni - 1) & (k == nk - 1))
            def _():
                _scatter_wait(copies)

    tile = pl.BlockSpec((tm, d), lambda i, k: (i, 0))
    vec = pl.BlockSpec((8, d), lambda i, k: (0, 0))
    sems = [pltpu.SemaphoreType.DMA((NDEV,)), pltpu.SemaphoreType.DMA((NDEV,))] if exchange else []
    return _call(
        body, name=name,
        grid=(ni, nk),
        out_shape=(((jax.ShapeDtypeStruct((rows, d), F32),) if latent else ()) + (jax.ShapeDtypeStruct((8, d), F32),)
                   + ((jax.ShapeDtypeStruct(scatter.shape, scatter.dtype),) if exchange else ())),
        in_specs=([pl.BlockSpec((tm, nbk * bw), lambda i, k: (blk0 + i, k)),
                   pl.BlockSpec((nbk, d, bw), lambda i, k: (k, 0, 0)), tile, vec]
                  + ([tile] if latent else []) + ([ANY] if exchange else [])),
        out_specs=((tile,) if latent else ()) + (vec,) + ((ANY,) if exchange else ()),
        scratch_shapes=[pltpu.VMEM((tm, d), F32)] + sems,
        compiler_params=_params(("arbitrary", "arbitrary")),
    )(*([dp, w_all, src, mv] + ([dn] if latent else []) + ([scatter] if exchange else [])))


def _adamw_scattered(parts, w, m, v, tr):
    r, c = w.shape

    def body(p_ref, w_ref, m_ref, v_ref, g_ref, d_ref, m2_ref, v2_ref):
        g = p_ref[0].astype(F32)
        for k in range(1, NDEV):
            g = g + p_ref[k].astype(F32)
        g_ref[...] = g
        d_ref[...], m2_ref[...], v2_ref[...] = _adamw(w_ref[...], g, m_ref[...], v_ref[...])

    tile = pl.BlockSpec((tr, c), lambda i: (i, 0))
    return _call(
        body, name="adamw_scattered_%dx%d" % (r, c),
        grid=(r // tr,),
        out_shape=tuple(jax.ShapeDtypeStruct((r, c), F32) for _ in range(4)),
        in_specs=[pl.BlockSpec((NDEV, tr, c), lambda i: (0, i, 0)), tile, tile, tile],
        out_specs=(tile,) * 4,
        compiler_params=_params(("arbitrary",)),
    )(parts, w, m, v)


def _adamw_ada(st, dmod, w, m, v, tr):
    r, c = w.shape

    def body(s_ref, dm_ref, w_ref, m_ref, v_ref, g_ref, d_ref, m2_ref, v2_ref):
        g = jnp.dot(s_ref[...], dm_ref[...], precision=HIGHEST, preferred_element_type=F32)
        g_ref[...] = g
        d_ref[...], m2_ref[...], v2_ref[...] = _adamw(w_ref[...], g, m_ref[...], v_ref[...])

    tile = pl.BlockSpec((tr, c), lambda i: (i, 0))
    return _call(
        body, name="adamw_ada",
        grid=(r // tr,),
        out_shape=tuple(jax.ShapeDtypeStruct((r, c), F32) for _ in range(4)),
        in_specs=[pl.BlockSpec((tr, 16), lambda i: (i, 0)), pl.BlockSpec((16, c), lambda i: (0, 0)),
                  tile, tile, tile],
        out_specs=(tile,) * 4,
        compiler_params=_params(("arbitrary",)),
    )(st, dmod, w, m, v)


def _adamw_packed(g, w, m, v):
    def body(g_ref, w_ref, m_ref, v_ref, d_ref, m2_ref, v2_ref):
        d_ref[...], m2_ref[...], v2_ref[...] = _adamw(w_ref[...], g_ref[...], m_ref[...], v_ref[...])

    return _call(
        body, name="adamw_packed",
        out_shape=tuple(jax.ShapeDtypeStruct(w.shape, F32) for _ in range(3)),
        in_specs=[VMEM] * 4, out_specs=(VMEM,) * 3,
        compiler_params=_params(),
    )(g, w, m, v)


def _blockdiag_groups(wh, gc):
    h, dh, _ = wh.shape
    g = gc // dh
    w4 = wh.reshape(h // g, g, dh, dh)
    bd = jnp.einsum("ngij,gh->ngihj", w4, jnp.eye(g, dtype=wh.dtype))
    return bd.reshape(h // g, gc, gc)


def _blockdiag_extract(bd, dh):
    ng, gc, _ = bd.shape
    g = gc // dh
    x = bd.reshape(ng, g, dh, g, dh)
    return jnp.einsum("ngihj,gh->ngij", x, jnp.eye(g, dtype=bd.dtype)).reshape(ng * g, dh, dh)


def _rows8(*vecs):
    rows = [jnp.reshape(v, (1, -1)).astype(F32) for v in vecs]
    n = rows[0].shape[1]
    return jnp.concatenate(rows + [jnp.zeros((8 - len(rows), n), F32)], axis=0)


def _pack(pieces):
    flat = jnp.concatenate([jnp.reshape(a, (-1,)).astype(F32) for a in pieces])
    total = -(-flat.shape[0] // 1024) * 1024
    return jnp.pad(flat, (0, total - flat.shape[0])).reshape(total // 128, 128)


def _unpack(packed, shapes):
    flat = packed.reshape(-1)
    out, off = [], 0
    for s in shapes:
        n = 1
        for q in s:
            n *= q
        out.append(flat[off:off + n].reshape(s))
        off += n
    return out


def kernel(x, c, ctx, c_ctx, norm_g, w_ada, b_ada, w_in, w_conv_a, w_conv_b, b_conv_b, lru_wa, lru_ba, lru_wx, lru_bx, lru_lambda, w_out, final_g, loss_target, m_c_ctx, m_norm_g, m_w_ada, m_b_ada, m_w_in, m_w_conv_a, m_w_conv_b, m_b_conv_b, m_lru_wa, m_lru_ba, m_lru_wx, m_lru_bx, m_lru_lambda, m_w_out, m_final_g, v_c_ctx, v_norm_g, v_w_ada, v_b_ada, v_w_in, v_w_conv_a, v_w_conv_b, v_b_conv_b, v_lru_wa, v_lru_ba, v_lru_wx, v_lru_bx, v_lru_lambda, v_w_out, v_final_g):
    _, l, d = x.shape
    lc = ctx.shape[1]
    w = d // 2
    t = lc
    assert l % t == 0 and t % GRID_W == 0 and t % 128 == 0
    dh = w // N_HEADS
    gc = min(w, MXU_WIDTH)
    cols = w_ada.shape[2]
    wo_rows = w_out.shape[1]
    me = _idx(_my_pos())
    x2, ctx2, tgt2 = x[0], ctx[0], loss_target[0]
    w_ada2, w_in2, w_out2 = w_ada[0], w_in[0], w_out[0]

    small_mine = jnp.concatenate([w_conv_a[0], w_conv_b[0], lru_ba[0], lru_bx[0], lru_lambda[0],
                                  jnp.zeros((3, w // NDEV), F32)], axis=0)
    mod_all, s_mat, small_all = _mod_forward(
        jnp.broadcast_to(c, (8, d)), jnp.broadcast_to(c_ctx[None], (8, d)), w_ada2, small_mine)
    mod = jnp.transpose(mod_all, (1, 0, 2)).reshape(16, NDEV * cols) + b_ada
    mod_lat = lax.dynamic_slice_in_dim(mod, me, 1, axis=0)
    sh_l, sc_l, gt_l = jnp.split(mod_lat, 3, axis=-1)
    sh_c, sc_c, _ = jnp.split(mod[8:9], 3, axis=-1)
    small = jnp.transpose(small_all, (1, 0, 2)).reshape(16, w)
    wca = _rows8(*[small[j] for j in range(0, 3)])
    wcb = _rows8(*[small[j] for j in range(3, 7)])
    lv = _rows8(small[7], small[9], small[11], small[8], small[10], small[12], b_conv_b)
    w_all, wo_all = _gather_weights([w_in2.astype(BF16), w_out2.astype(BF16)])
    wo = wo_all.reshape(d, d)
    wg = jnp.stack([
        jnp.concatenate([_blockdiag_groups(lru_wa[0, dr], gc), _blockdiag_groups(lru_wx[0, dr], gc)], axis=-1)
        for dr in range(2)]).astype(BF16)

    la = l + lc
    tm = 2 * t if l % (2 * t) == 0 else t
    p, hlt = _in_projection(x2, _rows8(norm_g, sc_l, sh_l), w_all, la, 0, tm, 2, "in_projection")
    p, hlt = _in_projection(ctx2, _rows8(norm_g, sc_c, sh_c), w_all, la, l, t, 2, "in_projection_ctx", prev=(p, hlt))
    sm = _shift_matrices(t)
    hf, hr = _lru_forward(p, wg, lv, wcb, sm, l, t)
    dn, catt, dout, part_mix = _mix_forward(x2, tgt2, p, hf, hr, wo, _rows8(gt_l, final_g), wca, t)
    dp, dhs, part_ca = _mix_backward(dout, p, hf, hr, wo, wca, l, t)
    dxb0, dwg0, part_l0 = _lru_backward(0, p, dhs, hf, wg, lv, wcb, sm, l, t)
    dp, dwg1, part_l1 = _lru_backward(1, p, dhs, hr, wg, lv, wcb, sm, l, t, dxb_other=dxb0, dp=dp)
    tk = 3 * t if la % (3 * t) == 0 else t
    g_wout = _weight_grad_t(catt, dout, 2, 1, 4 * t if l % (4 * t) == 0 else t, "grad_w_out")
    g_wout = g_wout.reshape(NDEV, wo_rows, d)
    sc_win = _weight_grad_scatter(hlt, dp, tk, "grad_w_in")
    grad_x, part_lat, sc_wout = _input_backward(dp, w_all, x2, _rows8(norm_g, sc_l), 0, tm, 2, "input_backward",
                                                dn=dn, scatter=g_wout)
    (part_ctx,) = _input_backward(dp, w_all, ctx2, _rows8(norm_g, sc_c), l, t, 2, "input_backward_ctx")
    part_in = jnp.concatenate([part_lat[0:2], part_ctx[0:2], (part_lat[2] + part_ctx[2])[None]], axis=0)

    dwa = jnp.stack([_blockdiag_extract(dwg0[:, :, :gc], dh), _blockdiag_extract(dwg1[:, :, :gc], dh)])
    dwx = jnp.stack([_blockdiag_extract(dwg0[:, :, gc:], dh), _blockdiag_extract(dwg1[:, :, gc:], dh)])
    lru_part = jnp.stack([dwa, dwx]).reshape(NDEV, -1, 128)
    zeros_d = jnp.zeros((d,), F32)
    pieces = [
        jnp.concatenate([part_in[0], part_in[1], part_mix[1]]),
        jnp.concatenate([part_in[2], part_in[3], zeros_d]),
        part_in[4], part_mix[0], part_ca[0:3], part_l1[4:8], part_l1[3],
        jnp.stack([part_l0[0], part_l1[0]]), jnp.stack([part_l0[1], part_l1[1]]),
        jnp.stack([part_l0[2], part_l1[2]]), part_mix[2, 0:1],
    ]
    shapes = [(3 * d,), (3 * d,), (d,), (d,), (3, w), (4, w), (w,), (2, w), (2, w), (2, w), (1,)]
    sig_cc = jax.nn.sigmoid(c_ctx)
    dsilu_cc = jnp.broadcast_to((sig_cc * (1.0 + c_ctx * (1.0 - sig_cc)))[None], (8, d))
    psum, pall, lru_sum, g_cctx8 = _reduce_small(_pack(pieces), lru_part, w_ada2, dsilu_cc)
    (g_modl, g_modc, g_norm, g_final, g_ca, g_cb, g_bcb, g_ba, g_bx, g_lam, loss1) = _unpack(psum, shapes)
    loss = loss1[0]
    g_cctx = g_cctx8[0]
    g_bada = (g_modl + g_modc)[None]
    g_lru = lru_sum.reshape(2, 2, N_HEADS, dh, dh)
    g_wa, g_wx = g_lru[0][None], g_lru[1][None]
    wsl = w // NDEV
    mine = lambda a: lax.dynamic_slice_in_dim(a, me * wsl, wsl, axis=-1)
    g_ca_m, g_cb_m, g_ba_m, g_bx_m, g_lam_m = (mine(g_ca)[None], mine(g_cb)[None], mine(g_ba)[None],
                                               mine(g_bx)[None], mine(g_lam)[None])
    g_norm, g_bcb = g_norm[None], g_bcb[None]

    cb = cols // 128
    per_dev = pall[:, :3 * d // 128].reshape(NDEV, NDEV, cols)
    dmod_lat = lax.dynamic_slice_in_dim(per_dev, me, 1, axis=1)[:, 0]
    dmod_ctx = lax.dynamic_slice_in_dim(g_modc.reshape(NDEV, cols), me, 1, axis=0)
    dmod16 = jnp.concatenate([dmod_lat, dmod_ctx, jnp.zeros((7, cols), F32)], axis=0)
    tr_ada = 256 if d % 256 == 0 else d
    g_wada, d_wada, m_wada, v_wada = _adamw_ada(s_mat.T, dmod16, w_ada2, m_w_ada[0], v_w_ada[0], tr_ada)
    g_win2, d_win, m_win, v_win = _adamw_scattered(sc_win, w_in2, m_w_in[0], v_w_in[0], tr_ada)
    tr_out = 64 if wo_rows % 64 == 0 else wo_rows
    g_wout2, d_wout, m_wout, v_wout = _adamw_scattered(sc_wout, w_out2, m_w_out[0], v_w_out[0], tr_out)

    small_w = [c_ctx, norm_g, b_ada, w_conv_a, w_conv_b, b_conv_b, lru_wa, lru_ba, lru_wx, lru_bx, lru_lambda, final_g]
    small_m = [m_c_ctx, m_norm_g, m_b_ada, m_w_conv_a, m_w_conv_b, m_b_conv_b, m_lru_wa, m_lru_ba, m_lru_wx,
               m_lru_bx, m_lru_lambda, m_final_g]
    small_v = [v_c_ctx, v_norm_g, v_b_ada, v_w_conv_a, v_w_conv_b, v_b_conv_b, v_lru_wa, v_lru_ba, v_lru_wx,
               v_lru_bx, v_lru_lambda, v_final_g]
    small_g = [g_cctx, g_norm, g_bada, g_ca_m, g_cb_m, g_bcb, g_wa, g_ba_m, g_wx, g_bx_m, g_lam_m, g_final]
    sshapes = [a.shape for a in small_w]
    d_s, m_s, v_s = _adamw_packed(_pack(small_g), _pack(small_w), _pack(small_m), _pack(small_v))
    d_s, m_s, v_s = _unpack(d_s, sshapes), _unpack(m_s, sshapes), _unpack(v_s, sshapes)
    small_g = [jnp.reshape(a, s) for a, s in zip(small_g, sshapes)]

    def weights(small_list, ada, win, wout):
        (cctx_, norm_, bada_, ca_, cb_, bcb_, wa_, ba_, wx_, bx_, lam_, final_) = small_list
        return [cctx_, norm_, ada[None], bada_, win[None], ca_, cb_, bcb_, wa_, ba_, wx_, bx_, lam_, wout[None], final_]

    return (loss, grad_x[None],
            *weights(small_g, g_wada, g_win2, g_wout2), *weights(d_s, d_wada, d_win, d_wout),
            *weights(m_s, m_wada, m_win, m_wout), *weights(v_s, v_wada, v_win, v_wout))
```

```python
import functools

import jax
import jax.numpy as jnp
from jax import lax
from jax.experimental import pallas as pl
from jax.experimental.pallas import tpu as pltpu

F32 = jnp.float32
BF16 = jnp.bfloat16
MESH = pl.DeviceIdType.MESH
NDEV = 8
GRID_W = 64
N_HEADS = 16
LRU_C = 8.0
EPS = 1e-6
MXU_WIDTH = 256
VMEM_LIMIT = 60 * 1024 * 1024

ADAM_LR = 0.001
ADAM_B1 = 0.9
ADAM_B2 = 0.999
ADAM_EPS = 1e-08
ADAM_WD = 0.01
ADAM_STEP = 10
ADAM_C1 = 1.0 - ADAM_B1 ** ADAM_STEP
ADAM_C2 = 1.0 - ADAM_B2 ** ADAM_STEP

HIGHEST = lax.Precision.HIGHEST
ANY = pl.BlockSpec(memory_space=pl.ANY)
VMEM = pl.BlockSpec(memory_space=pltpu.VMEM)


def _call(body, **kw):
    return pl.pallas_call(body, **kw)


def _params(sem=None, vmem=VMEM_LIMIT):
    return pltpu.CompilerParams(dimension_semantics=sem, vmem_limit_bytes=vmem)


def _my_pos():
    return lax.axis_index("x"), lax.axis_index("y"), lax.axis_index("c")


def _idx(pos):
    return 4 * pos[0] + 2 * pos[1] + pos[2]


def _peer(k):
    x, y, c = _my_pos()
    return ((1 - x) if (k >> 2) & 1 else x, (1 - y) if (k >> 1) & 1 else y, (1 - c) if k & 1 else c)


def _exchange_vmem(src_ref, dst_ref, send_sems, recv_sems, base):
    me = _idx(_my_pos())
    sends = []
    for k in range(1, NDEV):
        cp = pltpu.make_async_remote_copy(
            src_ref=src_ref, dst_ref=dst_ref.at[me], send_sem=send_sems.at[base + k - 1],
            recv_sem=recv_sems.at[base + k - 1], device_id=_peer(k), device_id_type=MESH)
        cp.start()
        sends.append(cp)
    dst_ref[me] = src_ref[...]
    for k in range(1, NDEV):
        peer = _peer(k)
        pltpu.make_async_remote_copy(
            src_ref=src_ref, dst_ref=dst_ref.at[_idx(peer)], send_sem=send_sems.at[base + k - 1],
            recv_sem=recv_sems.at[base + k - 1], device_id=peer, device_id_type=MESH).wait_recv()
    for cp in sends:
        cp.wait_send()


def _sigmoid(z):
    return 1.0 / (1.0 + jnp.exp(-z))


def _softplus(x):
    return jnp.maximum(x, 0.0) + jnp.log1p(jnp.exp(-jnp.abs(x)))


def _one_minus_sq(a, la):
    u = 2.0 * la
    series = -(u * (1.0 + u * (0.5 + u * (1.0 / 6.0))))
    return jnp.where(u > -0.03, series, 1.0 - a * a)


def _dot(a, b):
    return jnp.dot(a, b, preferred_element_type=F32)


def _dot_nt(a, b):
    return lax.dot_general(a, b, (((1,), (1,)), ((), ())), preferred_element_type=F32)


def _rows(shape):
    return lax.broadcasted_iota(jnp.int32, shape, 0)


def _down(x, k, pos):
    return jnp.where(pos >= k, pltpu.roll(x, k, 0), 0.0)


def _up(x, k, pos, rowlen):
    return jnp.where(pos + k < rowlen, pltpu.roll(x, x.shape[0] - k, 0), 0.0)


def _pos_rowlen(shape, is_ctx):
    t = _rows(shape)
    pos = jnp.where(is_ctx, t, t & (GRID_W - 1))
    rowlen = jnp.where(is_ctx, shape[0], GRID_W)
    return pos, rowlen


def _shift_matrices(t):
    r = lax.broadcasted_iota(jnp.int32, (t, t), 0)
    c = lax.broadcasted_iota(jnp.int32, (t, t), 1)
    kinds = []
    for rowlen in (GRID_W, t):
        pos = r % rowlen
        kinds.append(jnp.stack([(c == r - 2) & (pos >= 2), (c == r - 1) & (pos >= 1),
                                (c == r + 1) & (pos + 1 < rowlen), (c == r + 2) & (pos + 2 < rowlen)]))
    return jnp.stack(kinds).astype(BF16)


def _conv4(v16, sm_ref, w_ref):
    taps = (_dot(sm_ref[0], v16), _dot(sm_ref[1], v16), v16.astype(F32), _dot(sm_ref[2], v16))
    out = w_ref[0:1, :] * taps[0] + w_ref[1:2, :] * taps[1] + w_ref[2:3, :] * taps[2] + w_ref[3:4, :] * taps[3]
    return out, taps


def _conv4_t(dy, sm_ref, w_ref):
    dy16 = dy.astype(BF16)
    return (w_ref[0:1, :] * _dot(sm_ref[3], dy16) + w_ref[1:2, :] * _dot(sm_ref[2], dy16)
            + w_ref[2:3, :] * dy + w_ref[3:4, :] * _dot(sm_ref[1], dy16))


def _conv3(t, w_ref, pos, rowlen):
    return w_ref[0:1, :] * _down(t, 1, pos) + w_ref[1:2, :] * t + w_ref[2:3, :] * _up(t, 1, pos, rowlen)


def _conv3_t(dz, w_ref, pos, rowlen):
    return w_ref[0:1, :] * _up(dz, 1, pos, rowlen) + w_ref[1:2, :] * dz + w_ref[2:3, :] * _down(dz, 1, pos)


def _chunk_scan(a, b, reverse):
    row = _rows(a.shape)
    for s in (1, 2, 4):
        if reverse:
            m = row < 8 - s
            sh = 8 - s
        else:
            m = row >= s
            sh = s
        a_s = jnp.where(m, pltpu.roll(a, sh, 0), 1.0)
        b_s = jnp.where(m, pltpu.roll(b, sh, 0), 0.0)
        b = b + a * b_s
        a = a * a_s
    return a, b


def _scan_tile(a_ref, b_ref, out_ref, carry, reverse):
    t, w = a_ref.shape
    nchunk = t // 8

    def step(k, h):
        r0 = pl.multiple_of((nchunk - 1 - k if reverse else k) * 8, 8)
        ca, cb = _chunk_scan(a_ref[pl.ds(r0, 8), :], b_ref[pl.ds(r0, 8), :], reverse)
        hh = ca * h + cb
        out_ref[pl.ds(r0, 8), :] = hh
        return jnp.broadcast_to(hh[0:1, :] if reverse else hh[7:8, :], (8, w))

    return lax.fori_loop(0, nchunk, step, carry)


def _scan_tile_backward(a_ref, b_ref, g_ref, carry, reverse):
    t, w = a_ref.shape
    nchunk = t // 8

    def step(k, u_next):
        r0 = pl.multiple_of((nchunk - 1 - k if reverse else k) * 8, 8)
        ca, cb = _chunk_scan(a_ref[pl.ds(r0, 8), :], b_ref[pl.ds(r0, 8), :], reverse)
        u = ca * u_next + cb
        r8 = _rows((8, w))
        if reverse:
            shifted = jnp.where(r8 < 7, pltpu.roll(u, 7, 0), u_next)
        else:
            shifted = jnp.where(r8 >= 1, pltpu.roll(u, 1, 0), u_next)
        g_ref[pl.ds(r0, 8), :] = g_ref[pl.ds(r0, 8), :] + shifted
        return jnp.broadcast_to(u[0:1, :] if reverse else u[7:8, :], (8, w))

    return lax.fori_loop(0, nchunk, step, carry)


def _lru_coef(xb, wg_ref, d, ba, bx, lam, gc):
    w = xb.shape[1]
    xb16 = xb.astype(BF16)
    zr, zi = [], []
    for g in range(w // gc):
        z = _dot(xb16[:, g * gc:(g + 1) * gc], wg_ref[d, g])
        zr.append(z[:, :gc])
        zi.append(z[:, gc:])
    zr = zr[0] if len(zr) == 1 else jnp.concatenate(zr, axis=-1)
    zi = zi[0] if len(zi) == 1 else jnp.concatenate(zi, axis=-1)
    r = _sigmoid(zr + ba)
    ig = _sigmoid(zi + bx)
    sp = _softplus(-lam)
    la = r * (-LRU_C * sp)
    a = jnp.exp(la)
    s = jnp.sqrt(_one_minus_sq(a, la))
    return a, s, r, ig, sp


def _adamw(w, g, m, v):
    m2 = ADAM_B1 * m + (1.0 - ADAM_B1) * g
    v2 = ADAM_B2 * v + (1.0 - ADAM_B2) * (g * g)
    m_hat = m2 / ADAM_C1
    v_hat = v2 / ADAM_C2
    delta = -ADAM_LR * (m_hat / (jnp.sqrt(v_hat) + ADAM_EPS) + ADAM_WD * w)
    return delta, m2, v2


def _mod_forward(c8, cctx8, w_ada, small):
    d = c8.shape[1]
    cols = w_ada.shape[1]

    def body(c_ref, cctx_ref, w_ref, sm_ref, mod_ref, s_ref, sm_all, cbuf, mod_my, send_sems, recv_sems):
        _exchange_vmem(sm_ref, sm_all, send_sems, recv_sems, 2 * (NDEV - 1))
        _exchange_vmem(c_ref, cbuf, send_sems, recv_sems, 0)
        row = _rows((8, d))
        c_all = jnp.zeros((8, d), F32)
        for b in range(NDEV):
            c_all = jnp.where(row == b, cbuf[b], c_all)
        cc = cctx_ref[...]
        s_top = c_all * _sigmoid(c_all)
        s_bot = jnp.where(row == 0, cc * _sigmoid(cc), 0.0)
        s = jnp.concatenate([s_top, s_bot], axis=0)
        s_ref[...] = s
        mod_my[...] = jnp.dot(s, w_ref[...], precision=HIGHEST, preferred_element_type=F32)
        _exchange_vmem(mod_my, mod_ref, send_sems, recv_sems, NDEV - 1)

    return _call(
        body, name="mod_forward",
        out_shape=(jax.ShapeDtypeStruct((NDEV, 16, cols), F32), jax.ShapeDtypeStruct((16, d), F32),
                   jax.ShapeDtypeStruct((NDEV,) + small.shape, F32)),
        in_specs=[VMEM] * 4, out_specs=(VMEM,) * 3,
        scratch_shapes=[pltpu.VMEM((NDEV, 8, d), F32), pltpu.VMEM((16, cols), F32),
                        pltpu.SemaphoreType.DMA((3 * (NDEV - 1),)), pltpu.SemaphoreType.DMA((3 * (NDEV - 1),))],
        compiler_params=_params(),
    )(c8, cctx8, w_ada, small)


def _scatter_copies(src_ref, dst_ref, send_sems, recv_sems):
    me = _idx(_my_pos())
    copies = [pltpu.make_async_copy(src_ref.at[me], dst_ref.at[0], send_sems.at[0])]
    for k in range(1, NDEV):
        peer = _peer(k)
        copies.append(pltpu.make_async_remote_copy(
            src_ref=src_ref.at[_idx(peer)], dst_ref=dst_ref.at[k], send_sem=send_sems.at[k],
            recv_sem=recv_sems.at[k], device_id=peer, device_id_type=MESH))
    return copies


def _gather_copies(src_ref, dst_ref, send_sems, recv_sems):
    me = _idx(_my_pos())
    sends = [pltpu.make_async_copy(src_ref, dst_ref.at[me], send_sems.at[0])]
    arrivals = []
    for k in range(1, NDEV):
        peer = _peer(k)
        sends.append(pltpu.make_async_remote_copy(
            src_ref=src_ref, dst_ref=dst_ref.at[me], send_sem=send_sems.at[k],
            recv_sem=recv_sems.at[k], device_id=peer, device_id_type=MESH))
        arrivals.append(pltpu.make_async_remote_copy(
            src_ref=src_ref, dst_ref=dst_ref.at[_idx(peer)], send_sem=send_sems.at[k],
            recv_sem=recv_sems.at[k], device_id=peer, device_id_type=MESH))
    return sends, arrivals


def _exchange_wait(sends, arrivals):
    sends[0].wait()
    for cp in arrivals:
        cp.wait_recv()
    for cp in sends[1:]:
        cp.wait_send()


def _xor_distance(step, c):
    b0, b1, b2 = step & 1, (step >> 1) & 1, (step >> 2) & 1
    return jnp.where(c == 1, b0 | (b2 << 1) | (b1 << 2), step)


def _peer_at(dist):
    x, y, c = _my_pos()
    return (x ^ ((dist >> 2) & 1), y ^ ((dist >> 1) & 1), c ^ (dist & 1))


def _reduce_small(packed, lru_parts, w_ada, dsilu_cctx):
    rp = packed.shape[0]
    rl = lru_parts.shape[1]
    d, cols = w_ada.shape
    assert cols % 128 == 0
    cb = cols // 128

    def body(p_ref, l_ref, w_ref, ds_ref, sum_ref, all_ref, lru_ref, cctx_ref,
             lbuf, lsum, cpart, call, send_sems, recv_sems, lsend, lrecv):
        me = _idx(_my_pos())
        scattered = _scatter_copies(l_ref, lbuf, lsend, lrecv)
        for cp in scattered:
            cp.start()
        _exchange_vmem(p_ref, all_ref, send_sems, recv_sems, 0)
        acc = all_ref[0]
        for j in range(1, NDEV):
            acc = acc + all_ref[j]
        sum_ref[...] = acc
        _exchange_wait(scattered, scattered[1:])
        red = lbuf[0]
        for k in range(1, NDEV):
            red = red + lbuf[k]
        lsum[...] = red
        _exchange_vmem(lsum, lru_ref, send_sems, recv_sems, NDEV - 1)
        part = jnp.zeros((8, d), F32)
        for q in range(cb):
            dm = jnp.broadcast_to(sum_ref[pl.ds((NDEV + me) * cb + q, 1), :], (8, 128))
            part = part + lax.dot_general(dm, w_ref[:, q * 128:(q + 1) * 128],
                                          (((1,), (1,)), ((), ())), precision=HIGHEST,
                                          preferred_element_type=F32)
        cpart[...] = part
        _exchange_vmem(cpart, call, send_sems, recv_sems, 2 * (NDEV - 1))
        tot = call[0]
        for j in range(1, NDEV):
            tot = tot + call[j]
        cctx_ref[...] = tot * ds_ref[...]

    return _call(
        body, name="reduce_small",
        out_shape=(jax.ShapeDtypeStruct((rp, 128), F32), jax.ShapeDtypeStruct((NDEV, rp, 128), F32),
                   jax.ShapeDtypeStruct((NDEV, rl, 128), F32), jax.ShapeDtypeStruct((8, d), F32)),
        in_specs=[VMEM] * 4, out_specs=(VMEM,) * 4,
        scratch_shapes=[pltpu.VMEM((NDEV, rl, 128), F32), pltpu.VMEM((rl, 128), F32), pltpu.VMEM((8, d), F32),
                        pltpu.VMEM((NDEV, 8, d), F32),
                        pltpu.SemaphoreType.DMA((3 * (NDEV - 1),)), pltpu.SemaphoreType.DMA((3 * (NDEV - 1),)),
                        pltpu.SemaphoreType.DMA((NDEV,)), pltpu.SemaphoreType.DMA((NDEV,))],
        compiler_params=_params(),
    )(packed, lru_parts, w_ada, dsilu_cctx)


def _normalize(src, mv, la, row0, tm, name, prev=None):
    rows, d = src.shape
    blk0 = row0 // tm

    def body(*refs):
        x_ref, mv_ref = refs[:2]
        h_ref, ht_ref = refs[-2:]
        xf = x_ref[...]
        r = lax.rsqrt(jnp.mean(xf * xf, axis=-1, keepdims=True) + EPS)
        h = xf * r * (mv_ref[0:1, :] * (1.0 + mv_ref[1:2, :])) + mv_ref[2:3, :]
        h_ref[...] = h.astype(BF16)
        ht_ref[...] = h.T.astype(BF16)

    in_specs = [pl.BlockSpec((tm, d), lambda i: (i, 0)), pl.BlockSpec((8, d), lambda i: (0, 0))]
    args = [src, mv]
    aliases = {}
    if prev is not None:
        in_specs += [ANY, ANY]
        args += list(prev)
        aliases = {2: 0, 3: 1}
    return _call(
        body, name=name,
        grid=(rows // tm,),
        out_shape=(jax.ShapeDtypeStruct((la, d), BF16), jax.ShapeDtypeStruct((d, la), BF16)),
        in_specs=in_specs,
        out_specs=(pl.BlockSpec((tm, d), lambda i: (blk0 + i, 0)), pl.BlockSpec((d, tm), lambda i: (0, blk0 + i))),
        input_output_aliases=aliases,
        compiler_params=_params(("arbitrary",)),
    )(*args)


def _gather_order(step):
    return (step & 1) | (((step >> 2) & 1) << 1) | (((step >> 1) & 1) << 2)


def _in_projection(h, w_shard, tm):
    la, d = h.shape
    bw = w_shard.shape[1]
    ni = la // tm
    where = jnp.reshape(_idx(_my_pos()), (1,)).astype(jnp.int32)

    def body(me_ref, h_ref, w_ref, p_ref, all_ref, wbuf, send_sems, recv_sems, local_sems):
        s, i = pl.program_id(0), pl.program_id(1)
        x, y, c = _my_pos()
        me, sibling = (x, y, c), (x, y, 1 - c)
        chips = [(1 - x, y), (x, 1 - y), (1 - x, 1 - y)]

        def copy(k, block, to, from_shard=False):
            return pltpu.make_async_remote_copy(
                src_ref=w_ref if from_shard else all_ref.at[_idx(block)], dst_ref=all_ref.at[_idx(block)],
                send_sem=send_sems.at[k], recv_sem=recv_sems.at[k], device_id=to, device_id_type=MESH)

        def load(block, slot):
            return pltpu.make_async_copy(all_ref.at[_idx(block)], wbuf.at[slot], local_sems.at[1])

        keep = pltpu.make_async_copy(w_ref, all_ref.at[_idx(me)], local_sems.at[0])
        first = [copy(0, me, sibling, True)] + [copy(1 + j, me, (*chip, c), True) for j, chip in enumerate(chips)]
        passed = [copy(4 + j, (*chip, c), sibling) for j, chip in enumerate(chips)]
        steps = [(copy(0, sibling, me), None, sibling)]
        for j, chip in enumerate(chips):
            steps.append((copy(1 + j, (*chip, c), me), passed[j], (*chip, c)))
            steps.append((copy(4 + j, (*chip, 1 - c), me), None, (*chip, 1 - c)))

        @pl.when((s == 0) & (i == 0))
        def _():
            keep.start()
            mine = pltpu.make_async_copy(w_ref, wbuf.at[0], local_sems.at[1])
            mine.start()
            for cp in first:
                cp.start()
            mine.wait()

        for n, (arrival, forward, block) in enumerate(steps, start=1):
            @pl.when((s == n - 1) & (i == ni - 1))
            def _(arrival=arrival, forward=forward, block=block, n=n):
                arrival.wait_recv()
                if forward is not None:
                    forward.start()
                load(block, n % 2).start()

        @pl.when((s > 0) & (i == 0))
        def _():
            load(me, s % 2).wait()

        p_ref[...] = _dot(h_ref[...], wbuf[s % 2]).astype(BF16)

        @pl.when((s == NDEV - 1) & (i == ni - 1))
        def _():
            for cp in first + passed:
                cp.wait_send()
            keep.wait()

    return _call(
        body, name="in_projection",
        grid_spec=pltpu.PrefetchScalarGridSpec(
            num_scalar_prefetch=1, grid=(NDEV, ni),
            in_specs=[pl.BlockSpec((tm, d), lambda s, i, me_ref: (i, 0)), ANY],
            out_specs=(pl.BlockSpec((tm, bw), lambda s, i, me_ref: (i, me_ref[0] ^ _gather_order(s))), ANY),
            scratch_shapes=[pltpu.VMEM((2, d, bw), BF16), pltpu.SemaphoreType.DMA((7,)),
                            pltpu.SemaphoreType.DMA((7,)), pltpu.SemaphoreType.DMA((2,))]),
        out_shape=(jax.ShapeDtypeStruct((la, NDEV * bw), BF16), jax.ShapeDtypeStruct((NDEV, d, bw), BF16)),
        compiler_params=_params(("arbitrary", "arbitrary")),
    )(where, h, w_shard)


def _lru_forward(p, wg, lv, wcb, sm, wo_shard, l, t):
    la = p.shape[0]
    w = lv.shape[1]
    gc = wg.shape[2]
    nt = l // t

    def body(vf_ref, vr_ref, wg_ref, lv_ref, wcb_ref, sm_ref, wo_ref, hf_ref, hr_ref, wo_all,
             a_s, b_s, carry, send_sems, recv_sems):
        sends, arrivals = _gather_copies(wo_ref, wo_all, send_sems, recv_sems)

        @pl.when(pl.program_id(0) == 0)
        def _():
            carry[...] = jnp.zeros_like(carry)
            for cp in sends:
                cp.start()

        @pl.when(pl.program_id(0) == nt)
        def _():
            _exchange_wait(sends, arrivals)

        for dr, (v_ref, h_ref) in enumerate(((vf_ref, hf_ref), (vr_ref, hr_ref))):
            xb, _ = _conv4(v_ref[...], sm_ref, wcb_ref)
            xb = xb + lv_ref[6:7, :]
            a, s, _, ig, _ = _lru_coef(xb, wg_ref, dr, lv_ref[3 * dr:3 * dr + 1, :],
                                       lv_ref[3 * dr + 1:3 * dr + 2, :], lv_ref[3 * dr + 2:3 * dr + 3, :], gc)
            a_s[...] = a
            b_s[...] = s * (ig * xb)
            carry[dr] = _scan_tile(a_s, b_s, h_ref, carry[dr], dr == 1)

    full = lambda shape: pl.BlockSpec(shape, lambda i: (0,) * len(shape))
    fmap = lambda i: (jnp.where(i == 0, nt, i - 1), 0)
    rmap = lambda i: (jnp.where(i == 0, nt, nt - i), 0)
    vcol = 4
    return _call(
        body, name="lru_forward",
        grid=(nt + 1,),
        out_shape=(jax.ShapeDtypeStruct((la, w), F32), jax.ShapeDtypeStruct((la, w), F32),
                   jax.ShapeDtypeStruct((NDEV,) + wo_shard.shape, wo_shard.dtype)),
        in_specs=[pl.BlockSpec((t, w), lambda i: (jnp.where(i == 0, nt, i - 1), vcol)),
                  pl.BlockSpec((t, w), lambda i: (jnp.where(i == 0, nt, nt - i), vcol)),
                  full(wg.shape), full(lv.shape), full(wcb.shape),
                  pl.BlockSpec((None, 4, t, t), lambda i: (jnp.where(i == 0, 1, 0), 0, 0, 0)), ANY],
        out_specs=(pl.BlockSpec((t, w), fmap), pl.BlockSpec((t, w), rmap), ANY),
        scratch_shapes=[pltpu.VMEM((t, w), F32), pltpu.VMEM((t, w), F32), pltpu.VMEM((2, 8, w), F32),
                        pltpu.SemaphoreType.DMA((NDEV,)), pltpu.SemaphoreType.DMA((NDEV,))],
        compiler_params=_params(("arbitrary",)),
    )(p, p, wg, lv, wcb, sm, wo_shard)


def _mix_gates(p_refs, hf_ref, hr_ref, wca_ref, t, w):
    bl, cl, ul, gl, ql = [r[...].astype(F32) for r in p_refs]
    pos, rowlen = _pos_rowlen((t, w), False)
    tt = cl * ul
    z = _conv3(tt, wca_ref, pos, rowlen)
    sig_g = _sigmoid(gl)
    sig_q = _sigmoid(ql)
    ylru = hf_ref[...] + hr_ref[...]
    return bl, cl, ul, gl, ql, tt, z, sig_g, sig_q, ylru, pos, rowlen


def _p_specs(t, w, nt):
    return [pl.BlockSpec((t, w), functools.partial(lambda i, s: (jnp.minimum(i, nt - 1), s), s=s))
            for s in (0, 1, 2, 3, 5)]


def _mix_forward(x, tgt, p, hf, hr, wo, ov, wca, t):
    l, d = x.shape
    w = d // 2
    nt = l // t

    def body(x_ref, tg_ref, b_ref, c_ref, u_ref, g_ref, q_ref, hf_ref, hr_ref, wo_ref, ov_ref, wca_ref,
             dn_ref, ct_ref, do_ref, part_ref):
        i = pl.program_id(0)
        bl, _, _, gl, ql, _, z, sig_g, sig_q, ylru, _, _ = _mix_gates(
            (b_ref, c_ref, u_ref, g_ref, q_ref), hf_ref, hr_ref, wca_ref, t, w)
        ya = bl * z * (gl * sig_g)
        yb = ylru * (ql * sig_q)
        ct_ref[0:w, :] = ya.T.astype(BF16)
        ct_ref[w:, :] = yb.T.astype(BF16)
        out = _dot(ya.astype(BF16), wo_ref[0:w, :]) + _dot(yb.astype(BF16), wo_ref[w:, :])
        gate, fg = ov_ref[0:1, :], ov_ref[1:2, :]
        n = x_ref[...] + gate * out
        rr = lax.rsqrt(jnp.mean(n * n, axis=-1, keepdims=True) + EPS)
        nh = n * rr
        e = nh * fg - tg_ref[...]
        loss = 0.5 * jnp.sum(jnp.mean(e * e, axis=-1, keepdims=True), axis=0, keepdims=True)
        dy = e * (1.0 / d)
        dnh = dy * fg
        dn = rr * (dnh - nh * jnp.mean(dnh * nh, axis=-1, keepdims=True))
        dn_ref[...] = dn
        do_ref[...] = (dn * gate).astype(BF16)

        @pl.when(i == 0)
        def _():
            part_ref[...] = jnp.zeros_like(part_ref)

        part_ref[0:1, :] += jnp.sum(dy * nh, axis=0, keepdims=True)
        part_ref[1:2, :] += jnp.sum(dn * out, axis=0, keepdims=True)
        part_ref[2:3, :] += jnp.broadcast_to(loss, (1, d))

    tile = lambda cols: pl.BlockSpec((t, cols), lambda i: (i, 0))
    full = lambda shape: pl.BlockSpec(shape, lambda i: (0,) * len(shape))
    return _call(
        body, name="mix_forward",
        grid=(nt,),
        out_shape=(jax.ShapeDtypeStruct((l, d), F32), jax.ShapeDtypeStruct((d, l), BF16),
                   jax.ShapeDtypeStruct((l, d), BF16), jax.ShapeDtypeStruct((8, d), F32)),
        in_specs=[tile(d), tile(d)] + _p_specs(t, w, nt) + [tile(w), tile(w),
                  pl.BlockSpec((d, d), lambda i: (0, 0), pipeline_mode=pl.Buffered(1)),
                  full(ov.shape), full(wca.shape)],
        out_specs=(tile(d), pl.BlockSpec((d, t), lambda i: (0, i)), tile(d), full((8, d))),
        compiler_params=_params(("arbitrary",)),
    )(x, tgt, p, p, p, p, p, hf, hr, wo, ov, wca)


def _mix_backward(dout, p, hf, hr, wo, wca, g_wout, l, t):
    d = dout.shape[1]
    w = d // 2
    nt = l // t
    la = p.shape[0]

    def body(do_ref, b_ref, c_ref, u_ref, g_ref, q_ref, hf_ref, hr_ref, wo_ref, wca_ref, gw_ref,
             dp_ref, dh_ref, part_ref, sc_ref, send_sems, recv_sems):
        i = pl.program_id(0)
        copies = _scatter_copies(gw_ref, sc_ref, send_sems, recv_sems)

        @pl.when(i == 0)
        def _():
            part_ref[...] = jnp.zeros_like(part_ref)
            for cp in copies:
                cp.start()

        @pl.when(i == nt)
        def _():
            dp_ref[...] = jnp.zeros_like(dp_ref)
            _exchange_wait(copies, copies[1:])

        @pl.when(i < nt)
        def _():
            bl, cl, ul, gl, ql, tt, z, sig_g, sig_q, ylru, pos, rowlen = _mix_gates(
                (b_ref, c_ref, u_ref, g_ref, q_ref), hf_ref, hr_ref, wca_ref, t, w)
            do = do_ref[...]
            dya = _dot_nt(do, wo_ref[0:w, :])
            dyb = _dot_nt(do, wo_ref[w:, :])
            sg = gl * sig_g
            dz = dya * bl * sg
            dt = _conv3_t(dz, wca_ref, pos, rowlen)
            dp_ref[:, 0:w] = (dya * z * sg).astype(BF16)
            dp_ref[:, w:2 * w] = (dt * ul).astype(BF16)
            dp_ref[:, 2 * w:3 * w] = (dt * cl).astype(BF16)
            dp_ref[:, 3 * w:4 * w] = (dya * bl * z * (sig_g * (1.0 + gl * (1.0 - sig_g)))).astype(BF16)
            dp_ref[:, 4 * w:5 * w] = jnp.zeros((t, w), BF16)
            dp_ref[:, 5 * w:6 * w] = (dyb * ylru * (sig_q * (1.0 + ql * (1.0 - sig_q)))).astype(BF16)
            dh_ref[...] = dyb * (ql * sig_q)
            part_ref[0:1, :] += jnp.sum(dz * _down(tt, 1, pos), axis=0, keepdims=True)
            part_ref[1:2, :] += jnp.sum(dz * tt, axis=0, keepdims=True)
            part_ref[2:3, :] += jnp.sum(dz * _up(tt, 1, pos, rowlen), axis=0, keepdims=True)

    clamp = lambda cols: pl.BlockSpec((t, cols), lambda i: (jnp.minimum(i, nt - 1), 0))
    full = lambda shape: pl.BlockSpec(shape, lambda i: (0,) * len(shape))
    return _call(
        body, name="mix_backward",
        grid=(nt + 1,),
        out_shape=(jax.ShapeDtypeStruct((la, 6 * w), BF16), jax.ShapeDtypeStruct((l, w), F32),
                   jax.ShapeDtypeStruct((8, w), F32), jax.ShapeDtypeStruct(g_wout.shape, g_wout.dtype)),
        in_specs=[clamp(d)] + _p_specs(t, w, nt) + [clamp(w), clamp(w),
                  pl.BlockSpec((d, d), lambda i: (0, 0), pipeline_mode=pl.Buffered(1)), full(wca.shape), ANY],
        out_specs=(pl.BlockSpec((t, 6 * w), lambda i: (i, 0)), clamp(w), full((8, w)), ANY),
        scratch_shapes=[pltpu.SemaphoreType.DMA((NDEV,)), pltpu.SemaphoreType.DMA((NDEV,))],
        compiler_params=_params(("arbitrary",)),
    )(dout, p, p, p, p, p, hf, hr, wo, wca, g_wout)


def _lru_backward(direction, p, dhs, hs, wg, lv, wcb, sm, l, t, dxb_other=None, dp=None):
    la, w = hs.shape
    gc = wg.shape[2]
    ng = w // gc
    nt = l // t
    nblk8 = la // 8
    last = direction == 1

    if direction == 0:
        tile = lambda i: jnp.where(i == nt, nt, nt - 1 - i)
        halo = lambda i: jnp.where(tile(i) == 0, nblk8 - 1, tile(i) * (t // 8) - 1)
    else:
        tile = lambda i: i
        halo = lambda i: jnp.minimum((i + 1) * (t // 8), nblk8 - 1)

    def body(*refs):
        if last:
            (v_ref, dh_ref, hs_ref, halo_ref, wg_ref, lv_ref, wcb_ref, sm_ref, dxo_ref, _,
             out_ref, dwg_ref, part_ref, a_s, b_s, g_s, carry) = refs
        else:
            (v_ref, dh_ref, hs_ref, halo_ref, wg_ref, lv_ref, wcb_ref, sm_ref,
             out_ref, dwg_ref, part_ref, a_s, b_s, g_s, carry) = refs
        i = pl.program_id(0)
        is_ctx = i == nt

        @pl.when(i == 0)
        def _():
            carry[...] = jnp.zeros_like(carry)
            dwg_ref[...] = jnp.zeros_like(dwg_ref)
            part_ref[...] = jnp.zeros_like(part_ref)

        row = _rows((t, w))
        xb, taps = _conv4(v_ref[...], sm_ref, wcb_ref)
        xb = xb + lv_ref[6:7, :]
        lam = lv_ref[3 * direction + 2:3 * direction + 3, :]
        a, s, r, ig, sp = _lru_coef(xb, wg_ref, direction, lv_ref[3 * direction:3 * direction + 1, :],
                                    lv_ref[3 * direction + 1:3 * direction + 2, :], lam, gc)
        hs_t = hs_ref[...]
        if direction == 0:
            edge = jnp.where(is_ctx, 0.0, halo_ref[7:8, :])
            hprev = jnp.where(row == 0, edge, pltpu.roll(hs_t, 1, 0))
        else:
            edge = jnp.where(is_ctx, 0.0, halo_ref[0:1, :])
            hprev = jnp.where(row == t - 1, edge, pltpu.roll(hs_t, t - 1, 0))
        dh = jnp.where(is_ctx, 0.0, dh_ref[...])
        a_s[...] = a
        b_s[...] = a * dh
        g_s[...] = dh
        carry[...] = _scan_tile_backward(a_s, b_s, g_s, carry[...], direction == 0)

        g = g_s[...]
        ix = ig * xb
        gs = g * s
        dla = (g * a) * (hprev - ix * (a / s))
        dxb = gs * ig
        dzr = dla * (r * (1.0 - r)) * (-LRU_C * sp)
        dzi = gs * ix * (1.0 - ig)
        part_ref[0:1, :] += jnp.sum(dzr, axis=0, keepdims=True)
        part_ref[1:2, :] += jnp.sum(dzi, axis=0, keepdims=True)
        part_ref[2:3, :] += jnp.sum(dla * r, axis=0, keepdims=True) * (LRU_C * _sigmoid(-lam))
        pieces = []
        for gi in range(ng):
            sl = slice(gi * gc, (gi + 1) * gc)
            dz = jnp.concatenate([dzr[:, sl], dzi[:, sl]], axis=-1).astype(BF16)
            pieces.append(_dot_nt(dz, wg_ref[direction, gi]))
            dwg_ref[gi] += _dot(xb[:, sl].T.astype(BF16), dz)
        dxb = dxb + (pieces[0] if ng == 1 else jnp.concatenate(pieces, axis=-1))
        if not last:
            out_ref[...] = dxb
        else:
            dxb = dxb + dxo_ref[...]
            out_ref[...] = _conv4_t(dxb, sm_ref, wcb_ref).astype(BF16)
            part_ref[3:4, :] += jnp.sum(dxb, axis=0, keepdims=True)
            for j in range(4):
                part_ref[4 + j:5 + j, :] += jnp.sum(dxb * taps[j], axis=0, keepdims=True)

    full = lambda shape: pl.BlockSpec(shape, lambda i: (0,) * len(shape))
    in_specs = [pl.BlockSpec((t, w), lambda i: (tile(i), 4)),
                pl.BlockSpec((t, w), lambda i: (jnp.minimum(tile(i), nt - 1), 0)),
                pl.BlockSpec((t, w), lambda i: (tile(i), 0)),
                pl.BlockSpec((8, w), lambda i: (halo(i), 0)),
                full(wg.shape), full(lv.shape), full(wcb.shape),
                pl.BlockSpec((None, 4, t, t), lambda i: (jnp.where(i == nt, 1, 0), 0, 0, 0))]
    args = [p, dhs, hs, hs, wg, lv, wcb, sm]
    if last:
        in_specs += [pl.BlockSpec((t, w), lambda i: (tile(i), 0)), ANY]
        args += [dxb_other, dp]
        out0 = jax.ShapeDtypeStruct(dp.shape, dp.dtype)
        spec0 = pl.BlockSpec((t, w), lambda i: (tile(i), 4))
        aliases = {9: 0}
    else:
        out0 = jax.ShapeDtypeStruct((la, w), F32)
        spec0 = pl.BlockSpec((t, w), lambda i: (tile(i), 0))
        aliases = {}
    return _call(
        body, name="lru_backward_%d" % direction,
        grid=(nt + 1,),
        out_shape=(out0, jax.ShapeDtypeStruct((ng, gc, 2 * gc), F32), jax.ShapeDtypeStruct((8, w), F32)),
        in_specs=in_specs,
        out_specs=(spec0, full((ng, gc, 2 * gc)), full((8, w))),
        scratch_shapes=[pltpu.VMEM((t, w), F32), pltpu.VMEM((t, w), F32), pltpu.VMEM((t, w), F32),
                        pltpu.VMEM((8, w), F32)],
        input_output_aliases=aliases,
        compiler_params=_params(("arbitrary",)),
    )(*args)


def _weight_grad_t(at, b, nblk_m, nblk_n, tk, name):
    m, k = at.shape
    n = b.shape[1]
    bm, bn = m // nblk_m, n // nblk_n
    nk = k // tk

    def body(a_ref, b_ref, o_ref, acc):
        kk = pl.program_id(2)

        @pl.when(kk == 0)
        def _():
            acc[...] = jnp.zeros_like(acc)

        acc[...] += _dot(a_ref[...], b_ref[...])

        @pl.when(kk == nk - 1)
        def _():
            o_ref[...] = acc[...].astype(BF16)

    return _call(
        body, name=name,
        grid=(nblk_m, nblk_n, nk),
        out_shape=jax.ShapeDtypeStruct((nblk_m * nblk_n, bm, bn), BF16),
        in_specs=[pl.BlockSpec((bm, tk), lambda i, j, kk: (i, kk)),
                  pl.BlockSpec((tk, bn), lambda i, j, kk: (kk, j))],
        out_specs=pl.BlockSpec((None, bm, bn), lambda i, j, kk: (i * nblk_n + j, 0, 0)),
        scratch_shapes=[pltpu.VMEM((bm, bn), F32)],
        compiler_params=_params(("arbitrary", "arbitrary", "arbitrary")),
    )(at, b)


def _weight_grad_scatter(at, b, tk, name):
    m, k = at.shape
    n = b.shape[1]
    bn = n // NDEV
    nk = k // tk
    where = jnp.stack([_idx(_my_pos()), lax.axis_index("c")]).astype(jnp.int32)

    def body(w_ref, a_ref, b_ref, recv_ref, acc, sbuf, send_sems, recv_sems):
        s, kk = pl.program_id(0), pl.program_id(1)

        @pl.when(kk == 0)
        def _():
            acc[...] = _dot(a_ref[...], b_ref[...])

        @pl.when(kk > 0)
        def _():
            acc[...] += _dot(a_ref[...], b_ref[...])

        def push(step):
            dist = _xor_distance(step, w_ref[1])
            return pltpu.make_async_remote_copy(
                src_ref=sbuf.at[step % 2], dst_ref=recv_ref.at[dist], send_sem=send_sems.at[step],
                recv_sem=recv_sems.at[dist], device_id=_peer_at(dist), device_id_type=MESH)

        keep = pltpu.make_async_copy(sbuf.at[0], recv_ref.at[0], send_sems.at[0])

        @pl.when(kk == nk - 1)
        def _():
            @pl.when(s == 2)
            def _():
                keep.wait()

            @pl.when(s > 2)
            def _():
                push(s - 2).wait_send()

            sbuf[s % 2] = acc[...].astype(BF16)

            @pl.when(s == 0)
            def _():
                keep.start()

            @pl.when(s > 0)
            def _():
                push(s).start()

            @pl.when(s == NDEV - 1)
            def _():
                push(NDEV - 2).wait_send()
                push(NDEV - 1).wait_send()
                for dist in range(1, NDEV):
                    pltpu.make_async_remote_copy(
                        src_ref=sbuf.at[0], dst_ref=recv_ref.at[dist], send_sem=send_sems.at[0],
                        recv_sem=recv_sems.at[dist], device_id=_peer_at(dist), device_id_type=MESH).wait_recv()

    blk = lambda s, w_ref: w_ref[0] ^ _xor_distance(s, w_ref[1])
    return _call(
        body, name=name,
        grid_spec=pltpu.PrefetchScalarGridSpec(
            num_scalar_prefetch=1, grid=(NDEV, nk),
            in_specs=[pl.BlockSpec((m, tk), lambda s, kk, w_ref: (0, kk)),
                      pl.BlockSpec((tk, bn), lambda s, kk, w_ref: (kk, blk(s, w_ref)))],
            out_specs=ANY,
            scratch_shapes=[pltpu.VMEM((m, bn), F32), pltpu.VMEM((2, m, bn), BF16),
                            pltpu.SemaphoreType.DMA((NDEV,)), pltpu.SemaphoreType.DMA((NDEV,))]),
        out_shape=jax.ShapeDtypeStruct((NDEV, m, bn), BF16),
        compiler_params=_params(("arbitrary", "arbitrary")),
    )(where, at, b)


def _input_backward(dp, w_all, src, mv, row0, tm, nbk, name, dn=None):
    rows, d = src.shape
    nb, _, bw = w_all.shape
    nk = nb // nbk
    blk0 = row0 // tm
    latent = dn is not None

    def body(*refs):
        dp_ref, w_ref, x_ref, mv_ref = refs[:4]
        outs = refs[4 + latent:]
        part_ref, acc = outs[latent], outs[latent + 1]
        i, k = pl.program_id(0), pl.program_id(1)

        @pl.when((i == 0) & (k == 0))
        def _():
            part_ref[...] = jnp.zeros_like(part_ref)

        step = _dot_nt(dp_ref[:, 0:bw], w_ref[0])
        for q in range(1, nbk):
            step = step + _dot_nt(dp_ref[:, q * bw:(q + 1) * bw], w_ref[q])

        @pl.when(k == 0)
        def _():
            acc[...] = step

        @pl.when(k > 0)
        def _():
            acc[...] += step

        @pl.when(k == nk - 1)
        def _():
            xf = x_ref[...]
            r = lax.rsqrt(jnp.mean(xf * xf, axis=-1, keepdims=True) + EPS)
            xn = xf * r
            dhl = acc[...]
            gain, sc = mv_ref[0:1, :], mv_ref[1:2, :]
            dhx = jnp.sum(dhl * xn, axis=0, keepdims=True)
            part_ref[0:1, :] += jnp.sum(dhl, axis=0, keepdims=True)
            part_ref[1:2, :] += dhx * gain
            part_ref[2:3, :] += dhx * (1.0 + sc)
            if latent:
                dxn = dhl * (gain * (1.0 + sc))
                outs[0][...] = refs[4][...] + r * (dxn - xn * jnp.mean(dxn * xn, axis=-1, keepdims=True))

    tile = pl.BlockSpec((tm, d), lambda i, k: (i, 0))
    vec = pl.BlockSpec((8, d), lambda i, k: (0, 0))
    return _call(
        body, name=name,
        grid=(rows // tm, nk),
        out_shape=((jax.ShapeDtypeStruct((rows, d), F32),) if latent else ()) + (jax.ShapeDtypeStruct((8, d), F32),),
        in_specs=[pl.BlockSpec((tm, nbk * bw), lambda i, k: (blk0 + i, k)),
                  pl.BlockSpec((nbk, d, bw), lambda i, k: (k, 0, 0)), tile, vec] + ([tile] if latent else []),
        out_specs=((tile,) if latent else ()) + (vec,),
        scratch_shapes=[pltpu.VMEM((tm, d), F32)],
        compiler_params=_params(("arbitrary", "arbitrary")),
    )(*([dp, w_all, src, mv] + ([dn] if latent else [])))


def _adamw_scattered(parts, w, m, v, tr):
    r, c = w.shape

    def body(p_ref, w_ref, m_ref, v_ref, g_ref, d_ref, m2_ref, v2_ref):
        g = p_ref[0].astype(F32)
        for k in range(1, NDEV):
            g = g + p_ref[k].astype(F32)
        g_ref[...] = g
        d_ref[...], m2_ref[...], v2_ref[...] = _adamw(w_ref[...], g, m_ref[...], v_ref[...])

    tile = pl.BlockSpec((tr, c), lambda i: (i, 0))
    return _call(
        body, name="adamw_scattered_%dx%d" % (r, c),
        grid=(r // tr,),
        out_shape=tuple(jax.ShapeDtypeStruct((r, c), F32) for _ in range(4)),
        in_specs=[pl.BlockSpec((NDEV, tr, c), lambda i: (0, i, 0)), tile, tile, tile],
        out_specs=(tile,) * 4,
        compiler_params=_params(("arbitrary",)),
    )(parts, w, m, v)


def _adamw_ada(st, dmod, w, m, v, tr):
    r, c = w.shape

    def body(s_ref, dm_ref, w_ref, m_ref, v_ref, g_ref, d_ref, m2_ref, v2_ref):
        g = jnp.dot(s_ref[...], dm_ref[...], precision=HIGHEST, preferred_element_type=F32)
        g_ref[...] = g
        d_ref[...], m2_ref[...], v2_ref[...] = _adamw(w_ref[...], g, m_ref[...], v_ref[...])

    tile = pl.BlockSpec((tr, c), lambda i: (i, 0))
    return _call(
        body, name="adamw_ada",
        grid=(r // tr,),
        out_shape=tuple(jax.ShapeDtypeStruct((r, c), F32) for _ in range(4)),
        in_specs=[pl.BlockSpec((tr, 16), lambda i: (i, 0)), pl.BlockSpec((16, c), lambda i: (0, 0)),
                  tile, tile, tile],
        out_specs=(tile,) * 4,
        compiler_params=_params(("arbitrary",)),
    )(st, dmod, w, m, v)


def _adamw_packed(g, w, m, v):
    def body(g_ref, w_ref, m_ref, v_ref, d_ref, m2_ref, v2_ref):
        d_ref[...], m2_ref[...], v2_ref[...] = _adamw(w_ref[...], g_ref[...], m_ref[...], v_ref[...])

    return _call(
        body, name="adamw_packed",
        out_shape=tuple(jax.ShapeDtypeStruct(w.shape, F32) for _ in range(3)),
        in_specs=[VMEM] * 4, out_specs=(VMEM,) * 3,
        compiler_params=_params(),
    )(g, w, m, v)


def _blockdiag_groups(wh, gc):
    h, dh, _ = wh.shape
    g = gc // dh
    w4 = wh.reshape(h // g, g, dh, dh)
    bd = jnp.einsum("ngij,gh->ngihj", w4, jnp.eye(g, dtype=wh.dtype))
    return bd.reshape(h // g, gc, gc)


def _blockdiag_extract(bd, dh):
    ng, gc, _ = bd.shape
    g = gc // dh
    x = bd.reshape(ng, g, dh, g, dh)
    return jnp.einsum("ngihj,gh->ngij", x, jnp.eye(g, dtype=bd.dtype)).reshape(ng * g, dh, dh)


def _rows8(*vecs):
    rows = [jnp.reshape(v, (1, -1)).astype(F32) for v in vecs]
    n = rows[0].shape[1]
    return jnp.concatenate(rows + [jnp.zeros((8 - len(rows), n), F32)], axis=0)


def _pack(pieces):
    flat = jnp.concatenate([jnp.reshape(a, (-1,)).astype(F32) for a in pieces])
    total = -(-flat.shape[0] // 1024) * 1024
    return jnp.pad(flat, (0, total - flat.shape[0])).reshape(total // 128, 128)


def _unpack(packed, shapes):
    flat = packed.reshape(-1)
    out, off = [], 0
    for s in shapes:
        n = 1
        for q in s:
            n *= q
        out.append(flat[off:off + n].reshape(s))
        off += n
    return out


def kernel(x, c, ctx, c_ctx, norm_g, w_ada, b_ada, w_in, w_conv_a, w_conv_b, b_conv_b, lru_wa, lru_ba, lru_wx, lru_bx, lru_lambda, w_out, final_g, loss_target, m_c_ctx, m_norm_g, m_w_ada, m_b_ada, m_w_in, m_w_conv_a, m_w_conv_b, m_b_conv_b, m_lru_wa, m_lru_ba, m_lru_wx, m_lru_bx, m_lru_lambda, m_w_out, m_final_g, v_c_ctx, v_norm_g, v_w_ada, v_b_ada, v_w_in, v_w_conv_a, v_w_conv_b, v_b_conv_b, v_lru_wa, v_lru_ba, v_lru_wx, v_lru_bx, v_lru_lambda, v_w_out, v_final_g):
    _, l, d = x.shape
    lc = ctx.shape[1]
    w = d // 2
    t = lc
    assert l % t == 0 and t % GRID_W == 0 and t % 128 == 0
    dh = w // N_HEADS
    gc = min(w, MXU_WIDTH)
    cols = w_ada.shape[2]
    wo_rows = w_out.shape[1]
    me = _idx(_my_pos())
    x2, ctx2, tgt2 = x[0], ctx[0], loss_target[0]
    w_ada2, w_in2, w_out2 = w_ada[0], w_in[0], w_out[0]

    small_mine = jnp.concatenate([w_conv_a[0], w_conv_b[0], lru_ba[0], lru_bx[0], lru_lambda[0],
                                  jnp.zeros((3, w // NDEV), F32)], axis=0)
    mod_all, s_mat, small_all = _mod_forward(
        jnp.broadcast_to(c, (8, d)), jnp.broadcast_to(c_ctx[None], (8, d)), w_ada2, small_mine)
    mod = jnp.transpose(mod_all, (1, 0, 2)).reshape(16, NDEV * cols) + b_ada
    mod_lat = lax.dynamic_slice_in_dim(mod, me, 1, axis=0)
    sh_l, sc_l, gt_l = jnp.split(mod_lat, 3, axis=-1)
    sh_c, sc_c, _ = jnp.split(mod[8:9], 3, axis=-1)
    small = jnp.transpose(small_all, (1, 0, 2)).reshape(16, w)
    wca = _rows8(*[small[j] for j in range(0, 3)])
    wcb = _rows8(*[small[j] for j in range(3, 7)])
    lv = _rows8(small[7], small[9], small[11], small[8], small[10], small[12], b_conv_b)
    wg = jnp.stack([
        jnp.concatenate([_blockdiag_groups(lru_wa[0, dr], gc), _blockdiag_groups(lru_wx[0, dr], gc)], axis=-1)
        for dr in range(2)]).astype(BF16)

    la = l + lc
    tm = 2 * t if l % (2 * t) == 0 else t
    tk = 3 * t if la % (3 * t) == 0 else t
    h, hlt = _normalize(x2, _rows8(norm_g, sc_l, sh_l), la, 0, tm, "normalize")
    h, hlt = _normalize(ctx2, _rows8(norm_g, sc_c, sh_c), la, l, t, "normalize_ctx", prev=(h, hlt))
    p, w_all = _in_projection(h, w_in2.astype(BF16), tk)
    sm = _shift_matrices(t)
    hf, hr, wo_all = _lru_forward(p, wg, lv, wcb, sm, w_out2.astype(BF16), l, t)
    wo = wo_all.reshape(d, d)
    dn, catt, dout, part_mix = _mix_forward(x2, tgt2, p, hf, hr, wo, _rows8(gt_l, final_g), wca, t)
    g_wout = _weight_grad_t(catt, dout, 2, 1, 4 * t if l % (4 * t) == 0 else t, "grad_w_out")
    dp, dhs, part_ca, sc_wout = _mix_backward(dout, p, hf, hr, wo, wca, g_wout.reshape(NDEV, wo_rows, d), l, t)
    dxb0, dwg0, part_l0 = _lru_backward(0, p, dhs, hf, wg, lv, wcb, sm, l, t)
    dp, dwg1, part_l1 = _lru_backward(1, p, dhs, hr, wg, lv, wcb, sm, l, t, dxb_other=dxb0, dp=dp)
    sc_win = _weight_grad_scatter(hlt, dp, tk, "grad_w_in")
    grad_x, part_lat = _input_backward(dp, w_all, x2, _rows8(norm_g, sc_l), 0, tm, 2, "input_backward", dn=dn)
    (part_ctx,) = _input_backward(dp, w_all, ctx2, _rows8(norm_g, sc_c), l, t, 2, "input_backward_ctx")
    part_in = jnp.concatenate([part_lat[0:2], part_ctx[0:2], (part_lat[2] + part_ctx[2])[None]], axis=0)

    dwa = jnp.stack([_blockdiag_extract(dwg0[:, :, :gc], dh), _blockdiag_extract(dwg1[:, :, :gc], dh)])
    dwx = jnp.stack([_blockdiag_extract(dwg0[:, :, gc:], dh), _blockdiag_extract(dwg1[:, :, gc:], dh)])
    lru_part = jnp.stack([dwa, dwx]).reshape(NDEV, -1, 128)
    zeros_d = jnp.zeros((d,), F32)
    pieces = [
        jnp.concatenate([part_in[0], part_in[1], part_mix[1]]),
        jnp.concatenate([part_in[2], part_in[3], zeros_d]),
        part_in[4], part_mix[0], part_ca[0:3], part_l1[4:8], part_l1[3],
        jnp.stack([part_l0[0], part_l1[0]]), jnp.stack([part_l0[1], part_l1[1]]),
        jnp.stack([part_l0[2], part_l1[2]]), part_mix[2, 0:1],
    ]
    shapes = [(3 * d,), (3 * d,), (d,), (d,), (3, w), (4, w), (w,), (2, w), (2, w), (2, w), (1,)]
    sig_cc = jax.nn.sigmoid(c_ctx)
    dsilu_cc = jnp.broadcast_to((sig_cc * (1.0 + c_ctx * (1.0 - sig_cc)))[None], (8, d))
    psum, pall, lru_sum, g_cctx8 = _reduce_small(_pack(pieces), lru_part, w_ada2, dsilu_cc)
    (g_modl, g_modc, g_norm, g_final, g_ca, g_cb, g_bcb, g_ba, g_bx, g_lam, loss1) = _unpack(psum, shapes)
    loss = loss1[0]
    g_cctx = g_cctx8[0]
    g_bada = (g_modl + g_modc)[None]
    g_lru = lru_sum.reshape(2, 2, N_HEADS, dh, dh)
    g_wa, g_wx = g_lru[0][None], g_lru[1][None]
    wsl = w // NDEV
    mine = lambda a: lax.dynamic_slice_in_dim(a, me * wsl, wsl, axis=-1)
    g_ca_m, g_cb_m, g_ba_m, g_bx_m, g_lam_m = (mine(g_ca)[None], mine(g_cb)[None], mine(g_ba)[None],
                                               mine(g_bx)[None], mine(g_lam)[None])
    g_norm, g_bcb = g_norm[None], g_bcb[None]

    cb = cols // 128
    per_dev = pall[:, :3 * d // 128].reshape(NDEV, NDEV, cols)
    dmod_lat = lax.dynamic_slice_in_dim(per_dev, me, 1, axis=1)[:, 0]
    dmod_ctx = lax.dynamic_slice_in_dim(g_modc.reshape(NDEV, cols), me, 1, axis=0)
    dmod16 = jnp.concatenate([dmod_lat, dmod_ctx, jnp.zeros((7, cols), F32)], axis=0)
    tr_ada = 256 if d % 256 == 0 else d
    g_wada, d_wada, m_wada, v_wada = _adamw_ada(s_mat.T, dmod16, w_ada2, m_w_ada[0], v_w_ada[0], tr_ada)
    g_win2, d_win, m_win, v_win = _adamw_scattered(sc_win, w_in2, m_w_in[0], v_w_in[0], tr_ada)
    tr_out = 64 if wo_rows % 64 == 0 else wo_rows
    g_wout2, d_wout, m_wout, v_wout = _adamw_scattered(sc_wout, w_out2, m_w_out[0], v_w_out[0], tr_out)

    small_w = [c_ctx, norm_g, b_ada, w_conv_a, w_conv_b, b_conv_b, lru_wa, lru_ba, lru_wx, lru_bx, lru_lambda, final_g]
    small_m = [m_c_ctx, m_norm_g, m_b_ada, m_w_conv_a, m_w_conv_b, m_b_conv_b, m_lru_wa, m_lru_ba, m_lru_wx,
               m_lru_bx, m_lru_lambda, m_final_g]
    small_v = [v_c_ctx, v_norm_g, v_b_ada, v_w_conv_a, v_w_conv_b, v_b_conv_b, v_lru_wa, v_lru_ba, v_lru_wx,
               v_lru_bx, v_lru_lambda, v_final_g]
    small_g = [g_cctx, g_norm, g_bada, g_ca_m, g_cb_m, g_bcb, g_wa, g_ba_m, g_wx, g_bx_m, g_lam_m, g_final]
    sshapes = [a.shape for a in small_w]
    d_s, m_s, v_s = _adamw_packed(_pack(small_g), _pack(small_w), _pack(small_m), _pack(small_v))
    d_s, m_s, v_s = _unpack(d_s, sshapes), _unpack(m_s, sshapes), _unpack(v_s, sshapes)
    small_g = [jnp.reshape(a, s) for a, s in zip(small_g, sshapes)]

    def weights(small_list, ada, win, wout):
        (cctx_, norm_, bada_, ca_, cb_, bcb_, wa_, ba_, wx_, bx_, lam_, final_) = small_list
        return [cctx_, norm_, ada[None], bada_, win[None], ca_, cb_, bcb_, wa_, ba_, wx_, bx_, lam_, wout[None], final_]

    return (loss, grad_x[None],
            *weights(small_g, g_wada, g_win2, g_wout2), *weights(d_s, d_wada, d_win, d_wout),
            *weights(m_s, m_wada, m_win, m_wout), *weights(v_s, v_wada, v_win, v_wout))
```

```python
import functools

import jax
import jax.numpy as jnp
from jax import lax
from jax.experimental import pallas as pl
from jax.experimental.pallas import tpu as pltpu

F32 = jnp.float32
BF16 = jnp.bfloat16
MESH = pl.DeviceIdType.MESH
NDEV = 8
GRID_W = 64
N_HEADS = 16
LRU_C = 8.0
EPS = 1e-6
MXU_WIDTH = 256
VMEM_LIMIT = 60 * 1024 * 1024

ADAM_LR = 0.001
ADAM_B1 = 0.9
ADAM_B2 = 0.999
ADAM_EPS = 1e-08
ADAM_WD = 0.01
ADAM_STEP = 10
ADAM_C1 = 1.0 - ADAM_B1 ** ADAM_STEP
ADAM_C2 = 1.0 - ADAM_B2 ** ADAM_STEP

HIGHEST = lax.Precision.HIGHEST
ANY = pl.BlockSpec(memory_space=pl.ANY)
VMEM = pl.BlockSpec(memory_space=pltpu.VMEM)


def _call(body, **kw):
    return pl.pallas_call(body, **kw)


def _params(sem=None, vmem=VMEM_LIMIT):
    return pltpu.CompilerParams(dimension_semantics=sem, vmem_limit_bytes=vmem)


def _my_pos():
    return lax.axis_index("x"), lax.axis_index("y"), lax.axis_index("c")


def _idx(pos):
    return 4 * pos[0] + 2 * pos[1] + pos[2]


def _peer(k):
    x, y, c = _my_pos()
    return ((1 - x) if (k >> 2) & 1 else x, (1 - y) if (k >> 1) & 1 else y, (1 - c) if k & 1 else c)


def _exchange_vmem(src_ref, dst_ref, send_sems, recv_sems, base):
    me = _idx(_my_pos())
    sends = []
    for k in range(1, NDEV):
        cp = pltpu.make_async_remote_copy(
            src_ref=src_ref, dst_ref=dst_ref.at[me], send_sem=send_sems.at[base + k - 1],
            recv_sem=recv_sems.at[base + k - 1], device_id=_peer(k), device_id_type=MESH)
        cp.start()
        sends.append(cp)
    dst_ref[me] = src_ref[...]
    for k in range(1, NDEV):
        peer = _peer(k)
        pltpu.make_async_remote_copy(
            src_ref=src_ref, dst_ref=dst_ref.at[_idx(peer)], send_sem=send_sems.at[base + k - 1],
            recv_sem=recv_sems.at[base + k - 1], device_id=peer, device_id_type=MESH).wait_recv()
    for cp in sends:
        cp.wait_send()


def _sigmoid(z):
    return 1.0 / (1.0 + jnp.exp(-z))


def _softplus(x):
    return jnp.maximum(x, 0.0) + jnp.log1p(jnp.exp(-jnp.abs(x)))


def _one_minus_sq(a, la):
    u = 2.0 * la
    series = -(u * (1.0 + u * (0.5 + u * (1.0 / 6.0))))
    return jnp.where(u > -0.03, series, 1.0 - a * a)


def _dot(a, b):
    return jnp.dot(a, b, preferred_element_type=F32)


def _dot_nt(a, b):
    return lax.dot_general(a, b, (((1,), (1,)), ((), ())), preferred_element_type=F32)


def _rows(shape):
    return lax.broadcasted_iota(jnp.int32, shape, 0)


def _down(x, k, pos):
    return jnp.where(pos >= k, pltpu.roll(x, k, 0), 0.0)


def _up(x, k, pos, rowlen):
    return jnp.where(pos + k < rowlen, pltpu.roll(x, x.shape[0] - k, 0), 0.0)


def _pos_rowlen(shape, is_ctx):
    t = _rows(shape)
    pos = jnp.where(is_ctx, t, t & (GRID_W - 1))
    rowlen = jnp.where(is_ctx, shape[0], GRID_W)
    return pos, rowlen


def _shift_matrices(t):
    r = lax.broadcasted_iota(jnp.int32, (t, t), 0)
    c = lax.broadcasted_iota(jnp.int32, (t, t), 1)
    kinds = []
    for rowlen in (GRID_W, t):
        pos = r % rowlen
        kinds.append(jnp.stack([(c == r - 2) & (pos >= 2), (c == r - 1) & (pos >= 1),
                                (c == r + 1) & (pos + 1 < rowlen), (c == r + 2) & (pos + 2 < rowlen)]))
    return jnp.stack(kinds).astype(BF16)


def _conv4(v16, sm_ref, w_ref):
    taps = (_dot(sm_ref[0], v16), _dot(sm_ref[1], v16), v16.astype(F32), _dot(sm_ref[2], v16))
    out = w_ref[0:1, :] * taps[0] + w_ref[1:2, :] * taps[1] + w_ref[2:3, :] * taps[2] + w_ref[3:4, :] * taps[3]
    return out, taps


def _conv4_t(dy, sm_ref, w_ref):
    dy16 = dy.astype(BF16)
    return (w_ref[0:1, :] * _dot(sm_ref[3], dy16) + w_ref[1:2, :] * _dot(sm_ref[2], dy16)
            + w_ref[2:3, :] * dy + w_ref[3:4, :] * _dot(sm_ref[1], dy16))


def _conv3(t, w_ref, pos, rowlen):
    return w_ref[0:1, :] * _down(t, 1, pos) + w_ref[1:2, :] * t + w_ref[2:3, :] * _up(t, 1, pos, rowlen)


def _conv3_t(dz, w_ref, pos, rowlen):
    return w_ref[0:1, :] * _up(dz, 1, pos, rowlen) + w_ref[1:2, :] * dz + w_ref[2:3, :] * _down(dz, 1, pos)


def _chunk_scan(a, b, reverse):
    row = _rows(a.shape)
    for s in (1, 2, 4):
        if reverse:
            m = row < 8 - s
            sh = 8 - s
        else:
            m = row >= s
            sh = s
        a_s = jnp.where(m, pltpu.roll(a, sh, 0), 1.0)
        b_s = jnp.where(m, pltpu.roll(b, sh, 0), 0.0)
        b = b + a * b_s
        a = a * a_s
    return a, b


def _scan_tile(a_ref, b_ref, out_ref, carry, reverse):
    t, w = a_ref.shape
    nchunk = t // 8

    def step(k, h):
        r0 = pl.multiple_of((nchunk - 1 - k if reverse else k) * 8, 8)
        ca, cb = _chunk_scan(a_ref[pl.ds(r0, 8), :], b_ref[pl.ds(r0, 8), :], reverse)
        hh = ca * h + cb
        out_ref[pl.ds(r0, 8), :] = hh
        return jnp.broadcast_to(hh[0:1, :] if reverse else hh[7:8, :], (8, w))

    return lax.fori_loop(0, nchunk, step, carry)


def _scan_tile_backward(a_ref, b_ref, g_ref, carry, reverse):
    t, w = a_ref.shape
    nchunk = t // 8

    def step(k, u_next):
        r0 = pl.multiple_of((nchunk - 1 - k if reverse else k) * 8, 8)
        ca, cb = _chunk_scan(a_ref[pl.ds(r0, 8), :], b_ref[pl.ds(r0, 8), :], reverse)
        u = ca * u_next + cb
        r8 = _rows((8, w))
        if reverse:
            shifted = jnp.where(r8 < 7, pltpu.roll(u, 7, 0), u_next)
        else:
            shifted = jnp.where(r8 >= 1, pltpu.roll(u, 1, 0), u_next)
        g_ref[pl.ds(r0, 8), :] = g_ref[pl.ds(r0, 8), :] + shifted
        return jnp.broadcast_to(u[0:1, :] if reverse else u[7:8, :], (8, w))

    return lax.fori_loop(0, nchunk, step, carry)


def _lru_coef(xb, wg_ref, d, ba, bx, lam, gc):
    w = xb.shape[1]
    xb16 = xb.astype(BF16)
    zr, zi = [], []
    for g in range(w // gc):
        z = _dot(xb16[:, g * gc:(g + 1) * gc], wg_ref[d, g])
        zr.append(z[:, :gc])
        zi.append(z[:, gc:])
    zr = zr[0] if len(zr) == 1 else jnp.concatenate(zr, axis=-1)
    zi = zi[0] if len(zi) == 1 else jnp.concatenate(zi, axis=-1)
    r = _sigmoid(zr + ba)
    ig = _sigmoid(zi + bx)
    sp = _softplus(-lam)
    la = r * (-LRU_C * sp)
    a = jnp.exp(la)
    s = jnp.sqrt(_one_minus_sq(a, la))
    return a, s, r, ig, sp


def _adamw(w, g, m, v):
    m2 = ADAM_B1 * m + (1.0 - ADAM_B1) * g
    v2 = ADAM_B2 * v + (1.0 - ADAM_B2) * (g * g)
    m_hat = m2 / ADAM_C1
    v_hat = v2 / ADAM_C2
    delta = -ADAM_LR * (m_hat / (jnp.sqrt(v_hat) + ADAM_EPS) + ADAM_WD * w)
    return delta, m2, v2


def _mod_forward(c8, cctx8, w_ada, small):
    d = c8.shape[1]
    cols = w_ada.shape[1]

    def body(c_ref, cctx_ref, w_ref, sm_ref, mod_ref, s_ref, sm_all, cbuf, mod_my, send_sems, recv_sems):
        _exchange_vmem(sm_ref, sm_all, send_sems, recv_sems, 2 * (NDEV - 1))
        _exchange_vmem(c_ref, cbuf, send_sems, recv_sems, 0)
        row = _rows((8, d))
        c_all = jnp.zeros((8, d), F32)
        for b in range(NDEV):
            c_all = jnp.where(row == b, cbuf[b], c_all)
        cc = cctx_ref[...]
        s_top = c_all * _sigmoid(c_all)
        s_bot = jnp.where(row == 0, cc * _sigmoid(cc), 0.0)
        s = jnp.concatenate([s_top, s_bot], axis=0)
        s_ref[...] = s
        mod_my[...] = jnp.dot(s, w_ref[...], precision=HIGHEST, preferred_element_type=F32)
        _exchange_vmem(mod_my, mod_ref, send_sems, recv_sems, NDEV - 1)

    return _call(
        body, name="mod_forward",
        out_shape=(jax.ShapeDtypeStruct((NDEV, 16, cols), F32), jax.ShapeDtypeStruct((16, d), F32),
                   jax.ShapeDtypeStruct((NDEV,) + small.shape, F32)),
        in_specs=[VMEM] * 4, out_specs=(VMEM,) * 3,
        scratch_shapes=[pltpu.VMEM((NDEV, 8, d), F32), pltpu.VMEM((16, cols), F32),
                        pltpu.SemaphoreType.DMA((3 * (NDEV - 1),)), pltpu.SemaphoreType.DMA((3 * (NDEV - 1),))],
        compiler_params=_params(),
    )(c8, cctx8, w_ada, small)


def _scatter_copies(src_ref, dst_ref, send_sems, recv_sems):
    me = _idx(_my_pos())
    copies = [pltpu.make_async_copy(src_ref.at[me], dst_ref.at[0], send_sems.at[0])]
    for k in range(1, NDEV):
        peer = _peer(k)
        copies.append(pltpu.make_async_remote_copy(
            src_ref=src_ref.at[_idx(peer)], dst_ref=dst_ref.at[k], send_sem=send_sems.at[k],
            recv_sem=recv_sems.at[k], device_id=peer, device_id_type=MESH))
    return copies


def _gather_copies(src_ref, dst_ref, send_sems, recv_sems):
    me = _idx(_my_pos())
    sends = [pltpu.make_async_copy(src_ref, dst_ref.at[me], send_sems.at[0])]
    arrivals = []
    for k in range(1, NDEV):
        peer = _peer(k)
        sends.append(pltpu.make_async_remote_copy(
            src_ref=src_ref, dst_ref=dst_ref.at[me], send_sem=send_sems.at[k],
            recv_sem=recv_sems.at[k], device_id=peer, device_id_type=MESH))
        arrivals.append(pltpu.make_async_remote_copy(
            src_ref=src_ref, dst_ref=dst_ref.at[_idx(peer)], send_sem=send_sems.at[k],
            recv_sem=recv_sems.at[k], device_id=peer, device_id_type=MESH))
    return sends, arrivals


def _exchange_wait(sends, arrivals):
    sends[0].wait()
    for cp in arrivals:
        cp.wait_recv()
    for cp in sends[1:]:
        cp.wait_send()


def _chip_order(k, c):
    return (6, 4 - 2 * c, 2 + 2 * c, 0)[k]


def _scatter_order(s, c):
    k = s & 3
    mine = jnp.where(k == 0, 6, jnp.where(k == 1, 4 - 2 * c, jnp.where(k == 2, 2 + 2 * c, 0)))
    theirs = jnp.where(k == 0, 6, jnp.where(k == 1, 2 + 2 * c, jnp.where(k == 2, 4 - 2 * c, 0))) ^ 1
    return jnp.where(s < 4, theirs, mine)


def _peer_at(dist):
    x, y, c = _my_pos()
    return (x ^ ((dist >> 2) & 1), y ^ ((dist >> 1) & 1), c ^ (dist & 1))


def _reduce_small(packed, lru_parts, w_ada, dsilu_cctx):
    rp = packed.shape[0]
    rl = lru_parts.shape[1]
    d, cols = w_ada.shape
    assert cols % 128 == 0
    cb = cols // 128

    def body(p_ref, l_ref, w_ref, ds_ref, sum_ref, all_ref, lru_ref, cctx_ref,
             lbuf, lsum, cpart, call, send_sems, recv_sems, lsend, lrecv):
        me = _idx(_my_pos())
        scattered = _scatter_copies(l_ref, lbuf, lsend, lrecv)
        for cp in scattered:
            cp.start()
        _exchange_vmem(p_ref, all_ref, send_sems, recv_sems, 0)
        acc = all_ref[0]
        for j in range(1, NDEV):
            acc = acc + all_ref[j]
        sum_ref[...] = acc
        _exchange_wait(scattered, scattered[1:])
        red = lbuf[0]
        for k in range(1, NDEV):
            red = red + lbuf[k]
        lsum[...] = red
        _exchange_vmem(lsum, lru_ref, send_sems, recv_sems, NDEV - 1)
        part = jnp.zeros((8, d), F32)
        for q in range(cb):
            dm = jnp.broadcast_to(sum_ref[pl.ds((NDEV + me) * cb + q, 1), :], (8, 128))
            part = part + lax.dot_general(dm, w_ref[:, q * 128:(q + 1) * 128],
                                          (((1,), (1,)), ((), ())), precision=HIGHEST,
                                          preferred_element_type=F32)
        cpart[...] = part
        _exchange_vmem(cpart, call, send_sems, recv_sems, 2 * (NDEV - 1))
        tot = call[0]
        for j in range(1, NDEV):
            tot = tot + call[j]
        cctx_ref[...] = tot * ds_ref[...]

    return _call(
        body, name="reduce_small",
        out_shape=(jax.ShapeDtypeStruct((rp, 128), F32), jax.ShapeDtypeStruct((NDEV, rp, 128), F32),
                   jax.ShapeDtypeStruct((NDEV, rl, 128), F32), jax.ShapeDtypeStruct((8, d), F32)),
        in_specs=[VMEM] * 4, out_specs=(VMEM,) * 4,
        scratch_shapes=[pltpu.VMEM((NDEV, rl, 128), F32), pltpu.VMEM((rl, 128), F32), pltpu.VMEM((8, d), F32),
                        pltpu.VMEM((NDEV, 8, d), F32),
                        pltpu.SemaphoreType.DMA((3 * (NDEV - 1),)), pltpu.SemaphoreType.DMA((3 * (NDEV - 1),)),
                        pltpu.SemaphoreType.DMA((NDEV,)), pltpu.SemaphoreType.DMA((NDEV,))],
        compiler_params=_params(),
    )(packed, lru_parts, w_ada, dsilu_cctx)


def _normalize(src, mv, la, row0, tm, name, prev=None):
    rows, d = src.shape
    blk0 = row0 // tm

    def body(*refs):
        x_ref, mv_ref = refs[:2]
        h_ref, ht_ref = refs[-2:]
        xf = x_ref[...]
        r = lax.rsqrt(jnp.mean(xf * xf, axis=-1, keepdims=True) + EPS)
        h = xf * r * (mv_ref[0:1, :] * (1.0 + mv_ref[1:2, :])) + mv_ref[2:3, :]
        h_ref[...] = h.astype(BF16)
        ht_ref[...] = h.T.astype(BF16)

    in_specs = [pl.BlockSpec((tm, d), lambda i: (i, 0)), pl.BlockSpec((8, d), lambda i: (0, 0))]
    args = [src, mv]
    aliases = {}
    if prev is not None:
        in_specs += [ANY, ANY]
        args += list(prev)
        aliases = {2: 0, 3: 1}
    return _call(
        body, name=name,
        grid=(rows // tm,),
        out_shape=(jax.ShapeDtypeStruct((la, d), BF16), jax.ShapeDtypeStruct((d, la), BF16)),
        in_specs=in_specs,
        out_specs=(pl.BlockSpec((tm, d), lambda i: (blk0 + i, 0)), pl.BlockSpec((d, tm), lambda i: (0, blk0 + i))),
        input_output_aliases=aliases,
        compiler_params=_params(("arbitrary",)),
    )(*args)


def _gather_order(step):
    return (step & 1) | (((step >> 2) & 1) << 1) | (((step >> 1) & 1) << 2)


def _in_projection(h, w_shard, tm):
    la, d = h.shape
    bw = w_shard.shape[1]
    ni = la // tm
    where = jnp.reshape(_idx(_my_pos()), (1,)).astype(jnp.int32)

    def body(me_ref, h_ref, w_ref, p_ref, all_ref, wbuf, send_sems, recv_sems, local_sems):
        s, i = pl.program_id(0), pl.program_id(1)
        x, y, c = _my_pos()
        me, sibling = (x, y, c), (x, y, 1 - c)
        chips = [(1 - x, y), (x, 1 - y), (1 - x, 1 - y)]

        def copy(k, block, to, from_shard=False):
            return pltpu.make_async_remote_copy(
                src_ref=w_ref if from_shard else all_ref.at[_idx(block)], dst_ref=all_ref.at[_idx(block)],
                send_sem=send_sems.at[k], recv_sem=recv_sems.at[k], device_id=to, device_id_type=MESH)

        def load(block, slot):
            return pltpu.make_async_copy(all_ref.at[_idx(block)], wbuf.at[slot], local_sems.at[1])

        keep = pltpu.make_async_copy(w_ref, all_ref.at[_idx(me)], local_sems.at[0])
        first = [copy(0, me, sibling, True)] + [copy(1 + j, me, (*chip, c), True) for j, chip in enumerate(chips)]
        passed = [copy(4 + j, (*chip, c), sibling) for j, chip in enumerate(chips)]
        steps = [(copy(0, sibling, me), None, sibling)]
        for j, chip in enumerate(chips):
            steps.append((copy(1 + j, (*chip, c), me), passed[j], (*chip, c)))
            steps.append((copy(4 + j, (*chip, 1 - c), me), None, (*chip, 1 - c)))

        @pl.when((s == 0) & (i == 0))
        def _():
            keep.start()
            mine = pltpu.make_async_copy(w_ref, wbuf.at[0], local_sems.at[1])
            mine.start()
            for cp in first:
                cp.start()
            mine.wait()

        for n, (arrival, forward, block) in enumerate(steps, start=1):
            @pl.when((s == n - 1) & (i == ni - 1))
            def _(arrival=arrival, forward=forward, block=block, n=n):
                arrival.wait_recv()
                if forward is not None:
                    forward.start()
                load(block, n % 2).start()

        @pl.when((s > 0) & (i == 0))
        def _():
            load(me, s % 2).wait()

        p_ref[...] = _dot(h_ref[...], wbuf[s % 2]).astype(BF16)

        @pl.when((s == NDEV - 1) & (i == ni - 1))
        def _():
            for cp in first + passed:
                cp.wait_send()
            keep.wait()

    return _call(
        body, name="in_projection",
        grid_spec=pltpu.PrefetchScalarGridSpec(
            num_scalar_prefetch=1, grid=(NDEV, ni),
            in_specs=[pl.BlockSpec((tm, d), lambda s, i, me_ref: (i, 0)), ANY],
            out_specs=(pl.BlockSpec((tm, bw), lambda s, i, me_ref: (i, me_ref[0] ^ _gather_order(s))), ANY),
            scratch_shapes=[pltpu.VMEM((2, d, bw), BF16), pltpu.SemaphoreType.DMA((7,)),
                            pltpu.SemaphoreType.DMA((7,)), pltpu.SemaphoreType.DMA((2,))]),
        out_shape=(jax.ShapeDtypeStruct((la, NDEV * bw), BF16), jax.ShapeDtypeStruct((NDEV, d, bw), BF16)),
        compiler_params=_params(("arbitrary", "arbitrary")),
    )(where, h, w_shard)


def _lru_forward(p, wg, lv, wcb, sm, wo_shard, l, t):
    la = p.shape[0]
    w = lv.shape[1]
    gc = wg.shape[2]
    nt = l // t

    def body(vf_ref, vr_ref, wg_ref, lv_ref, wcb_ref, sm_ref, wo_ref, hf_ref, hr_ref, wo_all,
             a_s, b_s, carry, send_sems, recv_sems):
        sends, arrivals = _gather_copies(wo_ref, wo_all, send_sems, recv_sems)

        @pl.when(pl.program_id(0) == 0)
        def _():
            carry[...] = jnp.zeros_like(carry)
            for cp in sends:
                cp.start()

        @pl.when(pl.program_id(0) == nt)
        def _():
            _exchange_wait(sends, arrivals)

        for dr, (v_ref, h_ref) in enumerate(((vf_ref, hf_ref), (vr_ref, hr_ref))):
            xb, _ = _conv4(v_ref[...], sm_ref, wcb_ref)
            xb = xb + lv_ref[6:7, :]
            a, s, _, ig, _ = _lru_coef(xb, wg_ref, dr, lv_ref[3 * dr:3 * dr + 1, :],
                                       lv_ref[3 * dr + 1:3 * dr + 2, :], lv_ref[3 * dr + 2:3 * dr + 3, :], gc)
            a_s[...] = a
            b_s[...] = s * (ig * xb)
            carry[dr] = _scan_tile(a_s, b_s, h_ref, carry[dr], dr == 1)

    full = lambda shape: pl.BlockSpec(shape, lambda i: (0,) * len(shape))
    fmap = lambda i: (jnp.where(i == 0, nt, i - 1), 0)
    rmap = lambda i: (jnp.where(i == 0, nt, nt - i), 0)
    vcol = 4
    return _call(
        body, name="lru_forward",
        grid=(nt + 1,),
        out_shape=(jax.ShapeDtypeStruct((la, w), F32), jax.ShapeDtypeStruct((la, w), F32),
                   jax.ShapeDtypeStruct((NDEV,) + wo_shard.shape, wo_shard.dtype)),
        in_specs=[pl.BlockSpec((t, w), lambda i: (jnp.where(i == 0, nt, i - 1), vcol)),
                  pl.BlockSpec((t, w), lambda i: (jnp.where(i == 0, nt, nt - i), vcol)),
                  full(wg.shape), full(lv.shape), full(wcb.shape),
                  pl.BlockSpec((None, 4, t, t), lambda i: (jnp.where(i == 0, 1, 0), 0, 0, 0)), ANY],
        out_specs=(pl.BlockSpec((t, w), fmap), pl.BlockSpec((t, w), rmap), ANY),
        scratch_shapes=[pltpu.VMEM((t, w), F32), pltpu.VMEM((t, w), F32), pltpu.VMEM((2, 8, w), F32),
                        pltpu.SemaphoreType.DMA((NDEV,)), pltpu.SemaphoreType.DMA((NDEV,))],
        compiler_params=_params(("arbitrary",)),
    )(p, p, wg, lv, wcb, sm, wo_shard)


def _mix_gates(p_refs, hf_ref, hr_ref, wca_ref, t, w):
    bl, cl, ul, gl, ql = [r[...].astype(F32) for r in p_refs]
    pos, rowlen = _pos_rowlen((t, w), False)
    tt = cl * ul
    z = _conv3(tt, wca_ref, pos, rowlen)
    sig_g = _sigmoid(gl)
    sig_q = _sigmoid(ql)
    ylru = hf_ref[...] + hr_ref[...]
    return bl, cl, ul, gl, ql, tt, z, sig_g, sig_q, ylru, pos, rowlen


def _p_specs(t, w, nt):
    return [pl.BlockSpec((t, w), functools.partial(lambda i, s: (jnp.minimum(i, nt - 1), s), s=s))
            for s in (0, 1, 2, 3, 5)]


def _mix_forward(x, tgt, p, hf, hr, wo, ov, wca, t):
    l, d = x.shape
    w = d // 2
    nt = l // t

    def body(x_ref, tg_ref, b_ref, c_ref, u_ref, g_ref, q_ref, hf_ref, hr_ref, wo_ref, ov_ref, wca_ref,
             dn_ref, ct_ref, do_ref, part_ref):
        i = pl.program_id(0)
        bl, _, _, gl, ql, _, z, sig_g, sig_q, ylru, _, _ = _mix_gates(
            (b_ref, c_ref, u_ref, g_ref, q_ref), hf_ref, hr_ref, wca_ref, t, w)
        ya = bl * z * (gl * sig_g)
        yb = ylru * (ql * sig_q)
        ct_ref[0:w, :] = ya.T.astype(BF16)
        ct_ref[w:, :] = yb.T.astype(BF16)
        out = _dot(ya.astype(BF16), wo_ref[0:w, :]) + _dot(yb.astype(BF16), wo_ref[w:, :])
        gate, fg = ov_ref[0:1, :], ov_ref[1:2, :]
        n = x_ref[...] + gate * out
        rr = lax.rsqrt(jnp.mean(n * n, axis=-1, keepdims=True) + EPS)
        nh = n * rr
        e = nh * fg - tg_ref[...]
        loss = 0.5 * jnp.sum(jnp.mean(e * e, axis=-1, keepdims=True), axis=0, keepdims=True)
        dy = e * (1.0 / d)
        dnh = dy * fg
        dn = rr * (dnh - nh * jnp.mean(dnh * nh, axis=-1, keepdims=True))
        dn_ref[...] = dn
        do_ref[...] = (dn * gate).astype(BF16)

        @pl.when(i == 0)
        def _():
            part_ref[...] = jnp.zeros_like(part_ref)

        part_ref[0:1, :] += jnp.sum(dy * nh, axis=0, keepdims=True)
        part_ref[1:2, :] += jnp.sum(dn * out, axis=0, keepdims=True)
        part_ref[2:3, :] += jnp.broadcast_to(loss, (1, d))

    tile = lambda cols: pl.BlockSpec((t, cols), lambda i: (i, 0))
    full = lambda shape: pl.BlockSpec(shape, lambda i: (0,) * len(shape))
    return _call(
        body, name="mix_forward",
        grid=(nt,),
        out_shape=(jax.ShapeDtypeStruct((l, d), F32), jax.ShapeDtypeStruct((d, l), BF16),
                   jax.ShapeDtypeStruct((l, d), BF16), jax.ShapeDtypeStruct((8, d), F32)),
        in_specs=[tile(d), tile(d)] + _p_specs(t, w, nt) + [tile(w), tile(w),
                  pl.BlockSpec((d, d), lambda i: (0, 0), pipeline_mode=pl.Buffered(1)),
                  full(ov.shape), full(wca.shape)],
        out_specs=(tile(d), pl.BlockSpec((d, t), lambda i: (0, i)), tile(d), full((8, d))),
        compiler_params=_params(("arbitrary",)),
    )(x, tgt, p, p, p, p, p, hf, hr, wo, ov, wca)


def _mix_backward(dout, p, hf, hr, wo, wca, g_wout, l, t):
    d = dout.shape[1]
    w = d // 2
    nt = l // t
    la = p.shape[0]

    def body(do_ref, b_ref, c_ref, u_ref, g_ref, q_ref, hf_ref, hr_ref, wo_ref, wca_ref, gw_ref,
             dp_ref, dh_ref, part_ref, sc_ref, send_sems, recv_sems):
        i = pl.program_id(0)
        copies = _scatter_copies(gw_ref, sc_ref, send_sems, recv_sems)

        @pl.when(i == 0)
        def _():
            part_ref[...] = jnp.zeros_like(part_ref)
            for cp in copies:
                cp.start()

        @pl.when(i == nt)
        def _():
            dp_ref[...] = jnp.zeros_like(dp_ref)
            _exchange_wait(copies, copies[1:])

        @pl.when(i < nt)
        def _():
            bl, cl, ul, gl, ql, tt, z, sig_g, sig_q, ylru, pos, rowlen = _mix_gates(
                (b_ref, c_ref, u_ref, g_ref, q_ref), hf_ref, hr_ref, wca_ref, t, w)
            do = do_ref[...]
            dya = _dot_nt(do, wo_ref[0:w, :])
            dyb = _dot_nt(do, wo_ref[w:, :])
            sg = gl * sig_g
            dz = dya * bl * sg
            dt = _conv3_t(dz, wca_ref, pos, rowlen)
            dp_ref[:, 0:w] = (dya * z * sg).astype(BF16)
            dp_ref[:, w:2 * w] = (dt * ul).astype(BF16)
            dp_ref[:, 2 * w:3 * w] = (dt * cl).astype(BF16)
            dp_ref[:, 3 * w:4 * w] = (dya * bl * z * (sig_g * (1.0 + gl * (1.0 - sig_g)))).astype(BF16)
            dp_ref[:, 4 * w:5 * w] = jnp.zeros((t, w), BF16)
            dp_ref[:, 5 * w:6 * w] = (dyb * ylru * (sig_q * (1.0 + ql * (1.0 - sig_q)))).astype(BF16)
            dh_ref[...] = dyb * (ql * sig_q)
            part_ref[0:1, :] += jnp.sum(dz * _down(tt, 1, pos), axis=0, keepdims=True)
            part_ref[1:2, :] += jnp.sum(dz * tt, axis=0, keepdims=True)
            part_ref[2:3, :] += jnp.sum(dz * _up(tt, 1, pos, rowlen), axis=0, keepdims=True)

    clamp = lambda cols: pl.BlockSpec((t, cols), lambda i: (jnp.minimum(i, nt - 1), 0))
    full = lambda shape: pl.BlockSpec(shape, lambda i: (0,) * len(shape))
    return _call(
        body, name="mix_backward",
        grid=(nt + 1,),
        out_shape=(jax.ShapeDtypeStruct((la, 6 * w), BF16), jax.ShapeDtypeStruct((l, w), F32),
                   jax.ShapeDtypeStruct((8, w), F32), jax.ShapeDtypeStruct(g_wout.shape, g_wout.dtype)),
        in_specs=[clamp(d)] + _p_specs(t, w, nt) + [clamp(w), clamp(w),
                  pl.BlockSpec((d, d), lambda i: (0, 0), pipeline_mode=pl.Buffered(1)), full(wca.shape), ANY],
        out_specs=(pl.BlockSpec((t, 6 * w), lambda i: (i, 0)), clamp(w), full((8, w)), ANY),
        scratch_shapes=[pltpu.SemaphoreType.DMA((NDEV,)), pltpu.SemaphoreType.DMA((NDEV,))],
        compiler_params=_params(("arbitrary",)),
    )(dout, p, p, p, p, p, hf, hr, wo, wca, g_wout)


def _lru_backward(direction, p, dhs, hs, wg, lv, wcb, sm, l, t, dxb_other=None, dp=None):
    la, w = hs.shape
    gc = wg.shape[2]
    ng = w // gc
    nt = l // t
    nblk8 = la // 8
    last = direction == 1

    if direction == 0:
        tile = lambda i: jnp.where(i == nt, nt, nt - 1 - i)
        halo = lambda i: jnp.where(tile(i) == 0, nblk8 - 1, tile(i) * (t // 8) - 1)
    else:
        tile = lambda i: i
        halo = lambda i: jnp.minimum((i + 1) * (t // 8), nblk8 - 1)

    def body(*refs):
        if last:
            (v_ref, dh_ref, hs_ref, halo_ref, wg_ref, lv_ref, wcb_ref, sm_ref, dxo_ref, _,
             out_ref, dwg_ref, part_ref, a_s, b_s, g_s, carry) = refs
        else:
            (v_ref, dh_ref, hs_ref, halo_ref, wg_ref, lv_ref, wcb_ref, sm_ref,
             out_ref, dwg_ref, part_ref, a_s, b_s, g_s, carry) = refs
        i = pl.program_id(0)
        is_ctx = i == nt

        @pl.when(i == 0)
        def _():
            carry[...] = jnp.zeros_like(carry)
            dwg_ref[...] = jnp.zeros_like(dwg_ref)
            part_ref[...] = jnp.zeros_like(part_ref)

        row = _rows((t, w))
        xb, taps = _conv4(v_ref[...], sm_ref, wcb_ref)
        xb = xb + lv_ref[6:7, :]
        lam = lv_ref[3 * direction + 2:3 * direction + 3, :]
        a, s, r, ig, sp = _lru_coef(xb, wg_ref, direction, lv_ref[3 * direction:3 * direction + 1, :],
                                    lv_ref[3 * direction + 1:3 * direction + 2, :], lam, gc)
        hs_t = hs_ref[...]
        if direction == 0:
            edge = jnp.where(is_ctx, 0.0, halo_ref[7:8, :])
            hprev = jnp.where(row == 0, edge, pltpu.roll(hs_t, 1, 0))
        else:
            edge = jnp.where(is_ctx, 0.0, halo_ref[0:1, :])
            hprev = jnp.where(row == t - 1, edge, pltpu.roll(hs_t, t - 1, 0))
        dh = jnp.where(is_ctx, 0.0, dh_ref[...])
        a_s[...] = a
        b_s[...] = a * dh
        g_s[...] = dh
        carry[...] = _scan_tile_backward(a_s, b_s, g_s, carry[...], direction == 0)

        g = g_s[...]
        ix = ig * xb
        gs = g * s
        dla = (g * a) * (hprev - ix * (a / s))
        dxb = gs * ig
        dzr = dla * (r * (1.0 - r)) * (-LRU_C * sp)
        dzi = gs * ix * (1.0 - ig)
        part_ref[0:1, :] += jnp.sum(dzr, axis=0, keepdims=True)
        part_ref[1:2, :] += jnp.sum(dzi, axis=0, keepdims=True)
        part_ref[2:3, :] += jnp.sum(dla * r, axis=0, keepdims=True) * (LRU_C * _sigmoid(-lam))
        pieces = []
        for gi in range(ng):
            sl = slice(gi * gc, (gi + 1) * gc)
            dz = jnp.concatenate([dzr[:, sl], dzi[:, sl]], axis=-1).astype(BF16)
            pieces.append(_dot_nt(dz, wg_ref[direction, gi]))
            dwg_ref[gi] += _dot(xb[:, sl].T.astype(BF16), dz)
        dxb = dxb + (pieces[0] if ng == 1 else jnp.concatenate(pieces, axis=-1))
        if not last:
            out_ref[...] = dxb
        else:
            dxb = dxb + dxo_ref[...]
            out_ref[...] = _conv4_t(dxb, sm_ref, wcb_ref).astype(BF16)
            part_ref[3:4, :] += jnp.sum(dxb, axis=0, keepdims=True)
            for j in range(4):
                part_ref[4 + j:5 + j, :] += jnp.sum(dxb * taps[j], axis=0, keepdims=True)

    full = lambda shape: pl.BlockSpec(shape, lambda i: (0,) * len(shape))
    in_specs = [pl.BlockSpec((t, w), lambda i: (tile(i), 4)),
                pl.BlockSpec((t, w), lambda i: (jnp.minimum(tile(i), nt - 1), 0)),
                pl.BlockSpec((t, w), lambda i: (tile(i), 0)),
                pl.BlockSpec((8, w), lambda i: (halo(i), 0)),
                full(wg.shape), full(lv.shape), full(wcb.shape),
                pl.BlockSpec((None, 4, t, t), lambda i: (jnp.where(i == nt, 1, 0), 0, 0, 0))]
    args = [p, dhs, hs, hs, wg, lv, wcb, sm]
    if last:
        in_specs += [pl.BlockSpec((t, w), lambda i: (tile(i), 0)), ANY]
        args += [dxb_other, dp]
        out0 = jax.ShapeDtypeStruct(dp.shape, dp.dtype)
        spec0 = pl.BlockSpec((t, w), lambda i: (tile(i), 4))
        aliases = {9: 0}
    else:
        out0 = jax.ShapeDtypeStruct((la, w), F32)
        spec0 = pl.BlockSpec((t, w), lambda i: (tile(i), 0))
        aliases = {}
    return _call(
        body, name="lru_backward_%d" % direction,
        grid=(nt + 1,),
        out_shape=(out0, jax.ShapeDtypeStruct((ng, gc, 2 * gc), F32), jax.ShapeDtypeStruct((8, w), F32)),
        in_specs=in_specs,
        out_specs=(spec0, full((ng, gc, 2 * gc)), full((8, w))),
        scratch_shapes=[pltpu.VMEM((t, w), F32), pltpu.VMEM((t, w), F32), pltpu.VMEM((t, w), F32),
                        pltpu.VMEM((8, w), F32)],
        input_output_aliases=aliases,
        compiler_params=_params(("arbitrary",)),
    )(*args)


def _weight_grad_t(at, b, nblk_m, nblk_n, tk, name):
    m, k = at.shape
    n = b.shape[1]
    bm, bn = m // nblk_m, n // nblk_n
    nk = k // tk

    def body(a_ref, b_ref, o_ref, acc):
        kk = pl.program_id(2)

        @pl.when(kk == 0)
        def _():
            acc[...] = jnp.zeros_like(acc)

        acc[...] += _dot(a_ref[...], b_ref[...])

        @pl.when(kk == nk - 1)
        def _():
            o_ref[...] = acc[...].astype(BF16)

    return _call(
        body, name=name,
        grid=(nblk_m, nblk_n, nk),
        out_shape=jax.ShapeDtypeStruct((nblk_m * nblk_n, bm, bn), BF16),
        in_specs=[pl.BlockSpec((bm, tk), lambda i, j, kk: (i, kk)),
                  pl.BlockSpec((tk, bn), lambda i, j, kk: (kk, j))],
        out_specs=pl.BlockSpec((None, bm, bn), lambda i, j, kk: (i * nblk_n + j, 0, 0)),
        scratch_shapes=[pltpu.VMEM((bm, bn), F32)],
        compiler_params=_params(("arbitrary", "arbitrary", "arbitrary")),
    )(at, b)


def _weight_grad_scatter(at, b, tk, name):
    m, k = at.shape
    n = b.shape[1]
    bn = n // NDEV
    nk = k // tk
    where = jnp.stack([_idx(_my_pos()), lax.axis_index("c")]).astype(jnp.int32)

    def body(w_ref, a_ref, b_ref, recv_ref, acc, sbuf, sib, sib_send, sib_recv, chip_send, chip_recv, keep_sem):
        s, kk = pl.program_id(0), pl.program_id(1)
        x, y, c = _my_pos()

        @pl.when(kk == 0)
        def _():
            acc[...] = _dot(a_ref[...], b_ref[...])

        @pl.when(kk > 0)
        def _():
            acc[...] += _dot(a_ref[...], b_ref[...])

        def to_sibling(j):
            return pltpu.make_async_remote_copy(
                src_ref=sbuf.at[j % 2], dst_ref=sib.at[j], send_sem=sib_send.at[j], recv_sem=sib_recv.at[j],
                device_id=(x, y, 1 - c), device_id_type=MESH)

        def to_chip(j):
            dist = _chip_order(j, c)
            return pltpu.make_async_remote_copy(
                src_ref=sbuf.at[j % 2], dst_ref=recv_ref.at[dist // 2], send_sem=chip_send.at[j],
                recv_sem=chip_recv.at[dist // 2], device_id=_peer_at(dist), device_id_type=MESH)

        sends = [to_sibling(j) for j in range(4)] + [to_chip(j) for j in range(3)]
        sends.append(pltpu.make_async_copy(sbuf.at[1], recv_ref.at[0], keep_sem))

        for st in range(NDEV):
            @pl.when((kk == nk - 1) & (s == st))
            def _(st=st):
                if st >= 2:
                    sends[st - 2].wait_send()
                part = acc[...]
                if st >= 4:
                    to_sibling(st - 4).wait_recv()
                    part = part + sib[st - 4].astype(F32)
                sbuf[st % 2] = part.astype(BF16)
                sends[st].start()
                if st == NDEV - 1:
                    sends[st - 1].wait_send()
                    sends[st].wait()
                    for j in range(1, 4):
                        pltpu.make_async_remote_copy(
                            src_ref=sbuf.at[0], dst_ref=recv_ref.at[j], send_sem=chip_send.at[0],
                            recv_sem=chip_recv.at[j], device_id=_peer_at(2 * j), device_id_type=MESH).wait_recv()

    blk = lambda s, w_ref: w_ref[0] ^ _scatter_order(s, w_ref[1])
    return _call(
        body, name=name,
        grid_spec=pltpu.PrefetchScalarGridSpec(
            num_scalar_prefetch=1, grid=(NDEV, nk),
            in_specs=[pl.BlockSpec((m, tk), lambda s, kk, w_ref: (0, kk)),
                      pl.BlockSpec((tk, bn), lambda s, kk, w_ref: (kk, blk(s, w_ref)))],
            out_specs=ANY,
            scratch_shapes=[pltpu.VMEM((m, bn), F32), pltpu.VMEM((2, m, bn), BF16), pltpu.VMEM((4, m, bn), BF16),
                            pltpu.SemaphoreType.DMA((4,)), pltpu.SemaphoreType.DMA((4,)),
                            pltpu.SemaphoreType.DMA((4,)), pltpu.SemaphoreType.DMA((4,)),
                            pltpu.SemaphoreType.DMA]),
        out_shape=jax.ShapeDtypeStruct((4, m, bn), BF16),
        compiler_params=_params(("arbitrary", "arbitrary")),
    )(where, at, b)


def _input_backward(dp, w_all, src, mv, row0, tm, nbk, name, dn=None):
    rows, d = src.shape
    nb, _, bw = w_all.shape
    nk = nb // nbk
    blk0 = row0 // tm
    latent = dn is not None

    def body(*refs):
        dp_ref, w_ref, x_ref, mv_ref = refs[:4]
        outs = refs[4 + latent:]
        part_ref, acc = outs[latent], outs[latent + 1]
        i, k = pl.program_id(0), pl.program_id(1)

        @pl.when((i == 0) & (k == 0))
        def _():
            part_ref[...] = jnp.zeros_like(part_ref)

        step = _dot_nt(dp_ref[:, 0:bw], w_ref[0])
        for q in range(1, nbk):
            step = step + _dot_nt(dp_ref[:, q * bw:(q + 1) * bw], w_ref[q])

        @pl.when(k == 0)
        def _():
            acc[...] = step

        @pl.when(k > 0)
        def _():
            acc[...] += step

        @pl.when(k == nk - 1)
        def _():
            xf = x_ref[...]
            r = lax.rsqrt(jnp.mean(xf * xf, axis=-1, keepdims=True) + EPS)
            xn = xf * r
            dhl = acc[...]
            gain, sc = mv_ref[0:1, :], mv_ref[1:2, :]
            dhx = jnp.sum(dhl * xn, axis=0, keepdims=True)
            part_ref[0:1, :] += jnp.sum(dhl, axis=0, keepdims=True)
            part_ref[1:2, :] += dhx * gain
            part_ref[2:3, :] += dhx * (1.0 + sc)
            if latent:
                dxn = dhl * (gain * (1.0 + sc))
                outs[0][...] = refs[4][...] + r * (dxn - xn * jnp.mean(dxn * xn, axis=-1, keepdims=True))

    tile = pl.BlockSpec((tm, d), lambda i, k: (i, 0))
    vec = pl.BlockSpec((8, d), lambda i, k: (0, 0))
    return _call(
        body, name=name,
        grid=(rows // tm, nk),
        out_shape=((jax.ShapeDtypeStruct((rows, d), F32),) if latent else ()) + (jax.ShapeDtypeStruct((8, d), F32),),
        in_specs=[pl.BlockSpec((tm, nbk * bw), lambda i, k: (blk0 + i, k)),
                  pl.BlockSpec((nbk, d, bw), lambda i, k: (k, 0, 0)), tile, vec] + ([tile] if latent else []),
        out_specs=((tile,) if latent else ()) + (vec,),
        scratch_shapes=[pltpu.VMEM((tm, d), F32)],
        compiler_params=_params(("arbitrary", "arbitrary")),
    )(*([dp, w_all, src, mv] + ([dn] if latent else [])))


def _adamw_scattered(parts, w, m, v, tr):
    r, c = w.shape
    nslot = parts.shape[0]

    def body(p_ref, w_ref, m_ref, v_ref, g_ref, d_ref, m2_ref, v2_ref):
        g = p_ref[0].astype(F32)
        for k in range(1, nslot):
            g = g + p_ref[k].astype(F32)
        g_ref[...] = g
        d_ref[...], m2_ref[...], v2_ref[...] = _adamw(w_ref[...], g, m_ref[...], v_ref[...])

    tile = pl.BlockSpec((tr, c), lambda i: (i, 0))
    return _call(
        body, name="adamw_scattered_%dx%d" % (r, c),
        grid=(r // tr,),
        out_shape=tuple(jax.ShapeDtypeStruct((r, c), F32) for _ in range(4)),
        in_specs=[pl.BlockSpec((nslot, tr, c), lambda i: (0, i, 0)), tile, tile, tile],
        out_specs=(tile,) * 4,
        compiler_params=_params(("arbitrary",)),
    )(parts, w, m, v)


def _adamw_ada(st, dmod, w, m, v, tr):
    r, c = w.shape

    def body(s_ref, dm_ref, w_ref, m_ref, v_ref, g_ref, d_ref, m2_ref, v2_ref):
        g = jnp.dot(s_ref[...], dm_ref[...], precision=HIGHEST, preferred_element_type=F32)
        g_ref[...] = g
        d_ref[...], m2_ref[...], v2_ref[...] = _adamw(w_ref[...], g, m_ref[...], v_ref[...])

    tile = pl.BlockSpec((tr, c), lambda i: (i, 0))
    return _call(
        body, name="adamw_ada",
        grid=(r // tr,),
        out_shape=tuple(jax.ShapeDtypeStruct((r, c), F32) for _ in range(4)),
        in_specs=[pl.BlockSpec((tr, 16), lambda i: (i, 0)), pl.BlockSpec((16, c), lambda i: (0, 0)),
                  tile, tile, tile],
        out_specs=(tile,) * 4,
        compiler_params=_params(("arbitrary",)),
    )(st, dmod, w, m, v)


def _adamw_packed(g, w, m, v):
    def body(g_ref, w_ref, m_ref, v_ref, d_ref, m2_ref, v2_ref):
        d_ref[...], m2_ref[...], v2_ref[...] = _adamw(w_ref[...], g_ref[...], m_ref[...], v_ref[...])

    return _call(
        body, name="adamw_packed",
        out_shape=tuple(jax.ShapeDtypeStruct(w.shape, F32) for _ in range(3)),
        in_specs=[VMEM] * 4, out_specs=(VMEM,) * 3,
        compiler_params=_params(),
    )(g, w, m, v)


def _blockdiag_groups(wh, gc):
    h, dh, _ = wh.shape
    g = gc // dh
    w4 = wh.reshape(h // g, g, dh, dh)
    bd = jnp.einsum("ngij,gh->ngihj", w4, jnp.eye(g, dtype=wh.dtype))
    return bd.reshape(h // g, gc, gc)


def _blockdiag_extract(bd, dh):
    ng, gc, _ = bd.shape
    g = gc // dh
    x = bd.reshape(ng, g, dh, g, dh)
    return jnp.einsum("ngihj,gh->ngij", x, jnp.eye(g, dtype=bd.dtype)).reshape(ng * g, dh, dh)


def _rows8(*vecs):
    rows = [jnp.reshape(v, (1, -1)).astype(F32) for v in vecs]
    n = rows[0].shape[1]
    return jnp.concatenate(rows + [jnp.zeros((8 - len(rows), n), F32)], axis=0)


def _pack(pieces):
    flat = jnp.concatenate([jnp.reshape(a, (-1,)).astype(F32) for a in pieces])
    total = -(-flat.shape[0] // 1024) * 1024
    return jnp.pad(flat, (0, total - flat.shape[0])).reshape(total // 128, 128)


def _unpack(packed, shapes):
    flat = packed.reshape(-1)
    out, off = [], 0
    for s in shapes:
        n = 1
        for q in s:
            n *= q
        out.append(flat[off:off + n].reshape(s))
        off += n
    return out


def kernel(x, c, ctx, c_ctx, norm_g, w_ada, b_ada, w_in, w_conv_a, w_conv_b, b_conv_b, lru_wa, lru_ba, lru_wx, lru_bx, lru_lambda, w_out, final_g, loss_target, m_c_ctx, m_norm_g, m_w_ada, m_b_ada, m_w_in, m_w_conv_a, m_w_conv_b, m_b_conv_b, m_lru_wa, m_lru_ba, m_lru_wx, m_lru_bx, m_lru_lambda, m_w_out, m_final_g, v_c_ctx, v_norm_g, v_w_ada, v_b_ada, v_w_in, v_w_conv_a, v_w_conv_b, v_b_conv_b, v_lru_wa, v_lru_ba, v_lru_wx, v_lru_bx, v_lru_lambda, v_w_out, v_final_g):
    _, l, d = x.shape
    lc = ctx.shape[1]
    w = d // 2
    t = lc
    assert l % t == 0 and t % GRID_W == 0 and t % 128 == 0
    dh = w // N_HEADS
    gc = min(w, MXU_WIDTH)
    cols = w_ada.shape[2]
    wo_rows = w_out.shape[1]
    me = _idx(_my_pos())
    x2, ctx2, tgt2 = x[0], ctx[0], loss_target[0]
    w_ada2, w_in2, w_out2 = w_ada[0], w_in[0], w_out[0]

    small_mine = jnp.concatenate([w_conv_a[0], w_conv_b[0], lru_ba[0], lru_bx[0], lru_lambda[0],
                                  jnp.zeros((3, w // NDEV), F32)], axis=0)
    mod_all, s_mat, small_all = _mod_forward(
        jnp.broadcast_to(c, (8, d)), jnp.broadcast_to(c_ctx[None], (8, d)), w_ada2, small_mine)
    mod = jnp.transpose(mod_all, (1, 0, 2)).reshape(16, NDEV * cols) + b_ada
    mod_lat = lax.dynamic_slice_in_dim(mod, me, 1, axis=0)
    sh_l, sc_l, gt_l = jnp.split(mod_lat, 3, axis=-1)
    sh_c, sc_c, _ = jnp.split(mod[8:9], 3, axis=-1)
    small = jnp.transpose(small_all, (1, 0, 2)).reshape(16, w)
    wca = _rows8(*[small[j] for j in range(0, 3)])
    wcb = _rows8(*[small[j] for j in range(3, 7)])
    lv = _rows8(small[7], small[9], small[11], small[8], small[10], small[12], b_conv_b)
    wg = jnp.stack([
        jnp.concatenate([_blockdiag_groups(lru_wa[0, dr], gc), _blockdiag_groups(lru_wx[0, dr], gc)], axis=-1)
        for dr in range(2)]).astype(BF16)

    la = l + lc
    tm = 2 * t if l % (2 * t) == 0 else t
    tk = 3 * t if la % (3 * t) == 0 else t
    h, hlt = _normalize(x2, _rows8(norm_g, sc_l, sh_l), la, 0, tm, "normalize")
    h, hlt = _normalize(ctx2, _rows8(norm_g, sc_c, sh_c), la, l, t, "normalize_ctx", prev=(h, hlt))
    p, w_all = _in_projection(h, w_in2.astype(BF16), tk)
    sm = _shift_matrices(t)
    hf, hr, wo_all = _lru_forward(p, wg, lv, wcb, sm, w_out2.astype(BF16), l, t)
    wo = wo_all.reshape(d, d)
    dn, catt, dout, part_mix = _mix_forward(x2, tgt2, p, hf, hr, wo, _rows8(gt_l, final_g), wca, t)
    g_wout = _weight_grad_t(catt, dout, 2, 1, 4 * t if l % (4 * t) == 0 else t, "grad_w_out")
    dp, dhs, part_ca, sc_wout = _mix_backward(dout, p, hf, hr, wo, wca, g_wout.reshape(NDEV, wo_rows, d), l, t)
    dxb0, dwg0, part_l0 = _lru_backward(0, p, dhs, hf, wg, lv, wcb, sm, l, t)
    dp, dwg1, part_l1 = _lru_backward(1, p, dhs, hr, wg, lv, wcb, sm, l, t, dxb_other=dxb0, dp=dp)
    sc_win = _weight_grad_scatter(hlt, dp, tk, "grad_w_in")
    grad_x, part_lat = _input_backward(dp, w_all, x2, _rows8(norm_g, sc_l), 0, tm, 2, "input_backward", dn=dn)
    (part_ctx,) = _input_backward(dp, w_all, ctx2, _rows8(norm_g, sc_c), l, t, 2, "input_backward_ctx")
    part_in = jnp.concatenate([part_lat[0:2], part_ctx[0:2], (part_lat[2] + part_ctx[2])[None]], axis=0)

    dwa = jnp.stack([_blockdiag_extract(dwg0[:, :, :gc], dh), _blockdiag_extract(dwg1[:, :, :gc], dh)])
    dwx = jnp.stack([_blockdiag_extract(dwg0[:, :, gc:], dh), _blockdiag_extract(dwg1[:, :, gc:], dh)])
    lru_part = jnp.stack([dwa, dwx]).reshape(NDEV, -1, 128)
    zeros_d = jnp.zeros((d,), F32)
    pieces = [
        jnp.concatenate([part_in[0], part_in[1], part_mix[1]]),
        jnp.concatenate([part_in[2], part_in[3], zeros_d]),
        part_in[4], part_mix[0], part_ca[0:3], part_l1[4:8], part_l1[3],
        jnp.stack([part_l0[0], part_l1[0]]), jnp.stack([part_l0[1], part_l1[1]]),
        jnp.stack([part_l0[2], part_l1[2]]), part_mix[2, 0:1],
    ]
    shapes = [(3 * d,), (3 * d,), (d,), (d,), (3, w), (4, w), (w,), (2, w), (2, w), (2, w), (1,)]
    sig_cc = jax.nn.sigmoid(c_ctx)
    dsilu_cc = jnp.broadcast_to((sig_cc * (1.0 + c_ctx * (1.0 - sig_cc)))[None], (8, d))
    psum, pall, lru_sum, g_cctx8 = _reduce_small(_pack(pieces), lru_part, w_ada2, dsilu_cc)
    (g_modl, g_modc, g_norm, g_final, g_ca, g_cb, g_bcb, g_ba, g_bx, g_lam, loss1) = _unpack(psum, shapes)
    loss = loss1[0]
    g_cctx = g_cctx8[0]
    g_bada = (g_modl + g_modc)[None]
    g_lru = lru_sum.reshape(2, 2, N_HEADS, dh, dh)
    g_wa, g_wx = g_lru[0][None], g_lru[1][None]
    wsl = w // NDEV
    mine = lambda a: lax.dynamic_slice_in_dim(a, me * wsl, wsl, axis=-1)
    g_ca_m, g_cb_m, g_ba_m, g_bx_m, g_lam_m = (mine(g_ca)[None], mine(g_cb)[None], mine(g_ba)[None],
                                               mine(g_bx)[None], mine(g_lam)[None])
    g_norm, g_bcb = g_norm[None], g_bcb[None]

    cb = cols // 128
    per_dev = pall[:, :3 * d // 128].reshape(NDEV, NDEV, cols)
    dmod_lat = lax.dynamic_slice_in_dim(per_dev, me, 1, axis=1)[:, 0]
    dmod_ctx = lax.dynamic_slice_in_dim(g_modc.reshape(NDEV, cols), me, 1, axis=0)
    dmod16 = jnp.concatenate([dmod_lat, dmod_ctx, jnp.zeros((7, cols), F32)], axis=0)
    tr_ada = 256 if d % 256 == 0 else d
    g_wada, d_wada, m_wada, v_wada = _adamw_ada(s_mat.T, dmod16, w_ada2, m_w_ada[0], v_w_ada[0], tr_ada)
    g_win2, d_win, m_win, v_win = _adamw_scattered(sc_win, w_in2, m_w_in[0], v_w_in[0], tr_ada)
    tr_out = 64 if wo_rows % 64 == 0 else wo_rows
    g_wout2, d_wout, m_wout, v_wout = _adamw_scattered(sc_wout, w_out2, m_w_out[0], v_w_out[0], tr_out)

    small_w = [c_ctx, norm_g, b_ada, w_conv_a, w_conv_b, b_conv_b, lru_wa, lru_ba, lru_wx, lru_bx, lru_lambda, final_g]
    small_m = [m_c_ctx, m_norm_g, m_b_ada, m_w_conv_a, m_w_conv_b, m_b_conv_b, m_lru_wa, m_lru_ba, m_lru_wx,
               m_lru_bx, m_lru_lambda, m_final_g]
    small_v = [v_c_ctx, v_norm_g, v_b_ada, v_w_conv_a, v_w_conv_b, v_b_conv_b, v_lru_wa, v_lru_ba, v_lru_wx,
               v_lru_bx, v_lru_lambda, v_final_g]
    small_g = [g_cctx, g_norm, g_bada, g_ca_m, g_cb_m, g_bcb, g_wa, g_ba_m, g_wx, g_bx_m, g_lam_m, g_final]
    sshapes = [a.shape for a in small_w]
    d_s, m_s, v_s = _adamw_packed(_pack(small_g), _pack(small_w), _pack(small_m), _pack(small_v))
    d_s, m_s, v_s = _unpack(d_s, sshapes), _unpack(m_s, sshapes), _unpack(v_s, sshapes)
    small_g = [jnp.reshape(a, s) for a, s in zip(small_g, sshapes)]

    def weights(small_list, ada, win, wout):
        (cctx_, norm_, bada_, ca_, cb_, bcb_, wa_, ba_, wx_, bx_, lam_, final_) = small_list
        return [cctx_, norm_, ada[None], bada_, win[None], ca_, cb_, bcb_, wa_, ba_, wx_, bx_, lam_, wout[None], final_]

    return (loss, grad_x[None],
            *weights(small_g, g_wada, g_win2, g_wout2), *weights(d_s, d_wada, d_win, d_wout),
            *weights(m_s, m_wada, m_win, m_wout), *weights(v_s, v_wada, v_win, v_wout))
```

```python
import functools

import jax
import jax.numpy as jnp
import numpy as np
from jax import lax
from jax.experimental import pallas as pl
from jax.experimental.pallas import tpu as pltpu

F32 = jnp.float32
BF16 = jnp.bfloat16
MESH = pl.DeviceIdType.MESH
NDEV = 8
GRID_W = 64
N_HEADS = 16
LRU_C = 8.0
EPS = 1e-6
MXU_WIDTH = 256
VMEM_LIMIT = 60 * 1024 * 1024

ADAM_LR = 0.001
ADAM_B1 = 0.9
ADAM_B2 = 0.999
ADAM_EPS = 1e-08
ADAM_WD = 0.01
ADAM_STEP = 10
ADAM_C1 = 1.0 - ADAM_B1 ** ADAM_STEP
ADAM_C2 = 1.0 - ADAM_B2 ** ADAM_STEP

HIGHEST = lax.Precision.HIGHEST
ANY = pl.BlockSpec(memory_space=pl.ANY)
VMEM = pl.BlockSpec(memory_space=pltpu.VMEM)


def _call(body, **kw):
    return pl.pallas_call(body, **kw)


def _params(sem=None, vmem=VMEM_LIMIT):
    return pltpu.CompilerParams(dimension_semantics=sem, vmem_limit_bytes=vmem)


def _my_pos():
    return lax.axis_index("x"), lax.axis_index("y"), lax.axis_index("c")


def _idx(pos):
    return 4 * pos[0] + 2 * pos[1] + pos[2]


def _peer(k):
    x, y, c = _my_pos()
    return ((1 - x) if (k >> 2) & 1 else x, (1 - y) if (k >> 1) & 1 else y, (1 - c) if k & 1 else c)


def _exchange_vmem(src_ref, dst_ref, send_sems, recv_sems, base):
    me = _idx(_my_pos())
    sends = []
    for k in range(1, NDEV):
        cp = pltpu.make_async_remote_copy(
            src_ref=src_ref, dst_ref=dst_ref.at[me], send_sem=send_sems.at[base + k - 1],
            recv_sem=recv_sems.at[base + k - 1], device_id=_peer(k), device_id_type=MESH)
        cp.start()
        sends.append(cp)
    dst_ref[me] = src_ref[...]
    for k in range(1, NDEV):
        peer = _peer(k)
        pltpu.make_async_remote_copy(
            src_ref=src_ref, dst_ref=dst_ref.at[_idx(peer)], send_sem=send_sems.at[base + k - 1],
            recv_sem=recv_sems.at[base + k - 1], device_id=peer, device_id_type=MESH).wait_recv()
    for cp in sends:
        cp.wait_send()


def _sigmoid(z):
    return 1.0 / (1.0 + jnp.exp(-z))


def _softplus(x):
    return jnp.maximum(x, 0.0) + jnp.log1p(jnp.exp(-jnp.abs(x)))


def _one_minus_sq(a, la):
    u = 2.0 * la
    series = -(u * (1.0 + u * (0.5 + u * (1.0 / 6.0))))
    return jnp.where(u > -0.03, series, 1.0 - a * a)


def _dot(a, b):
    return jnp.dot(a, b, preferred_element_type=F32)


def _dot_nt(a, b):
    return lax.dot_general(a, b, (((1,), (1,)), ((), ())), preferred_element_type=F32)


def _rows(shape):
    return lax.broadcasted_iota(jnp.int32, shape, 0)


def _down(x, k, pos):
    return jnp.where(pos >= k, pltpu.roll(x, k, 0), 0.0)


def _up(x, k, pos, rowlen):
    return jnp.where(pos + k < rowlen, pltpu.roll(x, x.shape[0] - k, 0), 0.0)


def _pos_rowlen(shape, is_ctx):
    t = _rows(shape)
    pos = jnp.where(is_ctx, t, t & (GRID_W - 1))
    rowlen = jnp.where(is_ctx, shape[0], GRID_W)
    return pos, rowlen


def _scan_matrices(t):
    seg = t // 8
    r = np.arange(t)
    perm = (np.arange(t)[None, :] == ((r % 8) * seg + r // 8)[:, None]).astype(np.float32)
    rows, cols = r[:, None], r[None, :]
    taps, back = [], []
    for rowlen in (GRID_W, t):
        pos = rows % rowlen
        shift = {-2: (cols == rows - 2) & (pos >= 2), -1: (cols == rows - 1) & (pos >= 1),
                 0: cols == rows, 1: (cols == rows + 1) & (pos + 1 < rowlen),
                 2: (cols == rows + 2) & (pos + 2 < rowlen)}
        taps.append(np.stack([perm @ shift[k].astype(np.float32) for k in (-2, -1, 0, 1)]))
        back.append(np.stack([shift[k].astype(np.float32) @ perm.T for k in (2, 1, 0, -1)]))
    as_bf16 = lambda a: jnp.asarray(a, dtype=BF16)
    return as_bf16(np.stack(taps)), as_bf16(np.stack(back)), as_bf16(np.stack([perm, perm.T]))


def _conv3(t, w_ref, pos, rowlen):
    return w_ref[0:1, :] * _down(t, 1, pos) + w_ref[1:2, :] * t + w_ref[2:3, :] * _up(t, 1, pos, rowlen)


def _conv3_t(dz, w_ref, pos, rowlen):
    return w_ref[0:1, :] * _up(dz, 1, pos, rowlen) + w_ref[1:2, :] * dz + w_ref[2:3, :] * _down(dz, 1, pos)


def _chunk_scan(a, b, reverse):
    row = _rows(a.shape)
    for s in (1, 2, 4):
        if reverse:
            m = row < 8 - s
            sh = 8 - s
        else:
            m = row >= s
            sh = s
        a_s = jnp.where(m, pltpu.roll(a, sh, 0), 1.0)
        b_s = jnp.where(m, pltpu.roll(b, sh, 0), 0.0)
        b = b + a * b_s
        a = a * a_s
    return a, b


def _chain_segments(ptot, hend, carry, reverse):
    ca, cb = _chunk_scan(ptot, hend, reverse)
    incl = ca * carry + cb
    r8 = _rows(incl.shape)
    if reverse:
        start = jnp.where(r8 < 7, pltpu.roll(incl, 7, 0), carry)
        last = incl[0:1, :]
    else:
        start = jnp.where(r8 >= 1, pltpu.roll(incl, 1, 0), carry)
        last = incl[7:8, :]
    return start, jnp.broadcast_to(last, incl.shape)


def _block(k, nblock, reverse):
    return pl.ds(pl.multiple_of((nblock - 1 - k if reverse else k) * 8, 8), 8)


def _scan_tile(a_ref, b_ref, out_ref, carry, reverse):
    t, w = a_ref.shape
    seg = t // 8

    def local(k, hp):
        a = a_ref[_block(k, seg, reverse), :]
        return a * hp[0] + b_ref[_block(k, seg, reverse), :], a * hp[1]

    hend, ptot = lax.fori_loop(0, seg, local, (jnp.zeros((8, w), F32), jnp.ones((8, w), F32)), unroll=4)
    start, new_carry = _chain_segments(ptot, hend, carry, reverse)

    def final(k, h):
        h = a_ref[_block(k, seg, reverse), :] * h + b_ref[_block(k, seg, reverse), :]
        out_ref[_block(k, seg, reverse), :] = h
        return h

    lax.fori_loop(0, seg, final, start, unroll=4)
    return new_carry


def _scan_tile_backward(a_ref, dh_ref, g_ref, carry, reverse):
    t, w = a_ref.shape
    seg = t // 8

    def local(k, up):
        a = a_ref[_block(k, seg, reverse), :]
        return a * (dh_ref[_block(k, seg, reverse), :] + up[0]), a * up[1]

    uend, ptot = lax.fori_loop(0, seg, local, (jnp.zeros((8, w), F32), jnp.ones((8, w), F32)), unroll=4)
    start, new_carry = _chain_segments(ptot, uend, carry, reverse)

    def final(k, u):
        g = dh_ref[_block(k, seg, reverse), :] + u
        g_ref[_block(k, seg, reverse), :] = g
        return a_ref[_block(k, seg, reverse), :] * g

    lax.fori_loop(0, seg, final, start, unroll=4)
    return new_carry


def _lru_coef(xb, wg_ref, d, ba, bx, lam, gc):
    w = xb.shape[1]
    xb16 = xb.astype(BF16)
    zr, zi = [], []
    for g in range(w // gc):
        z = _dot(xb16[:, g * gc:(g + 1) * gc], wg_ref[d, g])
        zr.append(z[:, :gc])
        zi.append(z[:, gc:])
    zr = zr[0] if len(zr) == 1 else jnp.concatenate(zr, axis=-1)
    zi = zi[0] if len(zi) == 1 else jnp.concatenate(zi, axis=-1)
    r = _sigmoid(zr + ba)
    ig = _sigmoid(zi + bx)
    sp = _softplus(-lam)
    la = r * (-LRU_C * sp)
    a = jnp.exp(la)
    s = jnp.sqrt(_one_minus_sq(a, la))
    return a, s, r, ig, sp


def _adamw(w, g, m, v):
    m2 = ADAM_B1 * m + (1.0 - ADAM_B1) * g
    v2 = ADAM_B2 * v + (1.0 - ADAM_B2) * (g * g)
    m_hat = m2 / ADAM_C1
    v_hat = v2 / ADAM_C2
    delta = -ADAM_LR * (m_hat / (jnp.sqrt(v_hat) + ADAM_EPS) + ADAM_WD * w)
    return delta, m2, v2


def _mod_forward(c8, cctx8, w_ada, small):
    d = c8.shape[1]
    cols = w_ada.shape[1]

    def body(c_ref, cctx_ref, w_ref, sm_ref, mod_ref, s_ref, sm_all, cbuf, mod_my, send_sems, recv_sems):
        _exchange_vmem(sm_ref, sm_all, send_sems, recv_sems, 2 * (NDEV - 1))
        _exchange_vmem(c_ref, cbuf, send_sems, recv_sems, 0)
        row = _rows((8, d))
        c_all = jnp.zeros((8, d), F32)
        for b in range(NDEV):
            c_all = jnp.where(row == b, cbuf[b], c_all)
        cc = cctx_ref[...]
        s_top = c_all * _sigmoid(c_all)
        s_bot = jnp.where(row == 0, cc * _sigmoid(cc), 0.0)
        s = jnp.concatenate([s_top, s_bot], axis=0)
        s_ref[...] = s
        mod_my[...] = jnp.dot(s, w_ref[...], precision=HIGHEST, preferred_element_type=F32)
        _exchange_vmem(mod_my, mod_ref, send_sems, recv_sems, NDEV - 1)

    return _call(
        body, name="mod_forward",
        out_shape=(jax.ShapeDtypeStruct((NDEV, 16, cols), F32), jax.ShapeDtypeStruct((16, d), F32),
                   jax.ShapeDtypeStruct((NDEV,) + small.shape, F32)),
        in_specs=[VMEM] * 4, out_specs=(VMEM,) * 3,
        scratch_shapes=[pltpu.VMEM((NDEV, 8, d), F32), pltpu.VMEM((16, cols), F32),
                        pltpu.SemaphoreType.DMA((3 * (NDEV - 1),)), pltpu.SemaphoreType.DMA((3 * (NDEV - 1),))],
        compiler_params=_params(),
    )(c8, cctx8, w_ada, small)


def _scatter_copies(src_ref, dst_ref, send_sems, recv_sems):
    me = _idx(_my_pos())
    copies = [pltpu.make_async_copy(src_ref.at[me], dst_ref.at[0], send_sems.at[0])]
    for k in range(1, NDEV):
        peer = _peer(k)
        copies.append(pltpu.make_async_remote_copy(
            src_ref=src_ref.at[_idx(peer)], dst_ref=dst_ref.at[k], send_sem=send_sems.at[k],
            recv_sem=recv_sems.at[k], device_id=peer, device_id_type=MESH))
    return copies


def _gather_copies(src_ref, dst_ref, send_sems, recv_sems):
    me = _idx(_my_pos())
    sends = [pltpu.make_async_copy(src_ref, dst_ref.at[me], send_sems.at[0])]
    arrivals = []
    for k in range(1, NDEV):
        peer = _peer(k)
        sends.append(pltpu.make_async_remote_copy(
            src_ref=src_ref, dst_ref=dst_ref.at[me], send_sem=send_sems.at[k],
            recv_sem=recv_sems.at[k], device_id=peer, device_id_type=MESH))
        arrivals.append(pltpu.make_async_remote_copy(
            src_ref=src_ref, dst_ref=dst_ref.at[_idx(peer)], send_sem=send_sems.at[k],
            recv_sem=recv_sems.at[k], device_id=peer, device_id_type=MESH))
    return sends, arrivals


def _exchange_wait(sends, arrivals):
    sends[0].wait()
    for cp in arrivals:
        cp.wait_recv()
    for cp in sends[1:]:
        cp.wait_send()


def _chip_order(k, c):
    return (6, 4 - 2 * c, 2 + 2 * c, 0)[k]


def _scatter_order(s, c):
    k = s & 3
    mine = jnp.where(k == 0, 6, jnp.where(k == 1, 4 - 2 * c, jnp.where(k == 2, 2 + 2 * c, 0)))
    theirs = jnp.where(k == 0, 6, jnp.where(k == 1, 2 + 2 * c, jnp.where(k == 2, 4 - 2 * c, 0))) ^ 1
    return jnp.where(s < 4, theirs, mine)


def _peer_at(dist):
    x, y, c = _my_pos()
    return (x ^ ((dist >> 2) & 1), y ^ ((dist >> 1) & 1), c ^ (dist & 1))


def _reduce_small(packed, lru_parts, w_ada, dsilu_cctx):
    rp = packed.shape[0]
    rl = lru_parts.shape[1]
    d, cols = w_ada.shape
    assert cols % 128 == 0
    cb = cols // 128

    def body(p_ref, l_ref, w_ref, ds_ref, sum_ref, all_ref, lru_ref, cctx_ref,
             lbuf, lsum, cpart, call, send_sems, recv_sems, lsend, lrecv):
        me = _idx(_my_pos())
        scattered = _scatter_copies(l_ref, lbuf, lsend, lrecv)
        for cp in scattered:
            cp.start()
        _exchange_vmem(p_ref, all_ref, send_sems, recv_sems, 0)
        acc = all_ref[0]
        for j in range(1, NDEV):
            acc = acc + all_ref[j]
        sum_ref[...] = acc
        _exchange_wait(scattered, scattered[1:])
        red = lbuf[0]
        for k in range(1, NDEV):
            red = red + lbuf[k]
        lsum[...] = red
        _exchange_vmem(lsum, lru_ref, send_sems, recv_sems, NDEV - 1)
        part = jnp.zeros((8, d), F32)
        for q in range(cb):
            dm = jnp.broadcast_to(sum_ref[pl.ds((NDEV + me) * cb + q, 1), :], (8, 128))
            part = part + lax.dot_general(dm, w_ref[:, q * 128:(q + 1) * 128],
                                          (((1,), (1,)), ((), ())), precision=HIGHEST,
                                          preferred_element_type=F32)
        cpart[...] = part
        _exchange_vmem(cpart, call, send_sems, recv_sems, 2 * (NDEV - 1))
        tot = call[0]
        for j in range(1, NDEV):
            tot = tot + call[j]
        cctx_ref[...] = tot * ds_ref[...]

    return _call(
        body, name="reduce_small",
        out_shape=(jax.ShapeDtypeStruct((rp, 128), F32), jax.ShapeDtypeStruct((NDEV, rp, 128), F32),
                   jax.ShapeDtypeStruct((NDEV, rl, 128), F32), jax.ShapeDtypeStruct((8, d), F32)),
        in_specs=[VMEM] * 4, out_specs=(VMEM,) * 4,
        scratch_shapes=[pltpu.VMEM((NDEV, rl, 128), F32), pltpu.VMEM((rl, 128), F32), pltpu.VMEM((8, d), F32),
                        pltpu.VMEM((NDEV, 8, d), F32),
                        pltpu.SemaphoreType.DMA((3 * (NDEV - 1),)), pltpu.SemaphoreType.DMA((3 * (NDEV - 1),)),
                        pltpu.SemaphoreType.DMA((NDEV,)), pltpu.SemaphoreType.DMA((NDEV,))],
        compiler_params=_params(),
    )(packed, lru_parts, w_ada, dsilu_cctx)


def _normalize(src, mv, la, row0, tm, name, prev=None):
    rows, d = src.shape
    blk0 = row0 // tm

    def body(*refs):
        x_ref, mv_ref = refs[:2]
        h_ref, ht_ref = refs[-2:]
        xf = x_ref[...]
        r = lax.rsqrt(jnp.mean(xf * xf, axis=-1, keepdims=True) + EPS)
        h = xf * r * (mv_ref[0:1, :] * (1.0 + mv_ref[1:2, :])) + mv_ref[2:3, :]
        h_ref[...] = h.astype(BF16)
        ht_ref[...] = h.T.astype(BF16)

    in_specs = [pl.BlockSpec((tm, d), lambda i: (i, 0)), pl.BlockSpec((8, d), lambda i: (0, 0))]
    args = [src, mv]
    aliases = {}
    if prev is not None:
        in_specs += [ANY, ANY]
        args += list(prev)
        aliases = {2: 0, 3: 1}
    return _call(
        body, name=name,
        grid=(rows // tm,),
        out_shape=(jax.ShapeDtypeStruct((la, d), BF16), jax.ShapeDtypeStruct((d, la), BF16)),
        in_specs=in_specs,
        out_specs=(pl.BlockSpec((tm, d), lambda i: (blk0 + i, 0)), pl.BlockSpec((d, tm), lambda i: (0, blk0 + i))),
        input_output_aliases=aliases,
        compiler_params=_params(("arbitrary",)),
    )(*args)


def _gather_order(step):
    return (step & 1) | (((step >> 2) & 1) << 1) | (((step >> 1) & 1) << 2)


def _in_projection(h, w_shard, tm):
    la, d = h.shape
    bw = w_shard.shape[1]
    ni = la // tm
    where = jnp.reshape(_idx(_my_pos()), (1,)).astype(jnp.int32)

    def body(me_ref, h_ref, w_ref, p_ref, all_ref, wbuf, send_sems, recv_sems, local_sems):
        s, i = pl.program_id(0), pl.program_id(1)
        x, y, c = _my_pos()
        me, sibling = (x, y, c), (x, y, 1 - c)
        chips = [(1 - x, y), (x, 1 - y), (1 - x, 1 - y)]

        def copy(k, block, to, from_shard=False):
            return pltpu.make_async_remote_copy(
                src_ref=w_ref if from_shard else all_ref.at[_idx(block)], dst_ref=all_ref.at[_idx(block)],
                send_sem=send_sems.at[k], recv_sem=recv_sems.at[k], device_id=to, device_id_type=MESH)

        def load(block, slot):
            return pltpu.make_async_copy(all_ref.at[_idx(block)], wbuf.at[slot], local_sems.at[1])

        keep = pltpu.make_async_copy(w_ref, all_ref.at[_idx(me)], local_sems.at[0])
        first = [copy(0, me, sibling, True)] + [copy(1 + j, me, (*chip, c), True) for j, chip in enumerate(chips)]
        passed = [copy(4 + j, (*chip, c), sibling) for j, chip in enumerate(chips)]
        steps = [(copy(0, sibling, me), None, sibling)]
        for j, chip in enumerate(chips):
            steps.append((copy(1 + j, (*chip, c), me), passed[j], (*chip, c)))
            steps.append((copy(4 + j, (*chip, 1 - c), me), None, (*chip, 1 - c)))

        @pl.when((s == 0) & (i == 0))
        def _():
            keep.start()
            mine = pltpu.make_async_copy(w_ref, wbuf.at[0], local_sems.at[1])
            mine.start()
            for cp in first:
                cp.start()
            mine.wait()

        for n, (arrival, forward, block) in enumerate(steps, start=1):
            @pl.when((s == n - 1) & (i == ni - 1))
            def _(arrival=arrival, forward=forward, block=block, n=n):
                arrival.wait_recv()
                if forward is not None:
                    forward.start()
                load(block, n % 2).start()

        @pl.when((s > 0) & (i == 0))
        def _():
            load(me, s % 2).wait()

        p_ref[...] = _dot(h_ref[...], wbuf[s % 2]).astype(BF16)

        @pl.when((s == NDEV - 1) & (i == ni - 1))
        def _():
            for cp in first + passed:
                cp.wait_send()
            keep.wait()

    return _call(
        body, name="in_projection",
        grid_spec=pltpu.PrefetchScalarGridSpec(
            num_scalar_prefetch=1, grid=(NDEV, ni),
            in_specs=[pl.BlockSpec((tm, d), lambda s, i, me_ref: (i, 0)), ANY],
            out_specs=(pl.BlockSpec((tm, bw), lambda s, i, me_ref: (i, me_ref[0] ^ _gather_order(s))), ANY),
            scratch_shapes=[pltpu.VMEM((2, d, bw), BF16), pltpu.SemaphoreType.DMA((7,)),
                            pltpu.SemaphoreType.DMA((7,)), pltpu.SemaphoreType.DMA((2,))]),
        out_shape=(jax.ShapeDtypeStruct((la, NDEV * bw), BF16), jax.ShapeDtypeStruct((NDEV, d, bw), BF16)),
        compiler_params=_params(("arbitrary", "arbitrary")),
    )(where, h, w_shard)


def _conv_input(p, wcb, taps_m, l, t):
    la = p.shape[0]
    w = wcb.shape[1]
    nt = l // t

    def body(v_ref, wcb_ref, tm_ref, xb_ref):
        v16 = v_ref[...]
        xb = wcb_ref[4:5, :] + wcb_ref[0:1, :] * _dot(tm_ref[0], v16)
        for j in range(1, 4):
            xb = xb + wcb_ref[j:j + 1, :] * _dot(tm_ref[j], v16)
        xb_ref[...] = xb

    return _call(
        body, name="conv_input",
        grid=(nt + 1,),
        out_shape=jax.ShapeDtypeStruct((la, w), F32),
        in_specs=[pl.BlockSpec((t, w), lambda i: (i, 4)), pl.BlockSpec((8, w), lambda i: (0, 0)),
                  pl.BlockSpec((None, 4, t, t), lambda i: (i // nt, 0, 0, 0))],
        out_specs=pl.BlockSpec((t, w), lambda i: (i, 0)),
        compiler_params=_params(("arbitrary",)),
    )(p, wcb, taps_m)


def _lru_forward(xb, wg, lv, wo_shard, l, t):
    la, w = xb.shape
    gc = wg.shape[2]
    nt = l // t

    def body(xf_ref, xr_ref, wg_ref, lv_ref, wo_ref, hf_ref, hr_ref, wo_all,
             a_s, b_s, carry, send_sems, recv_sems):
        sends, arrivals = _gather_copies(wo_ref, wo_all, send_sems, recv_sems)

        @pl.when(pl.program_id(0) == 0)
        def _():
            carry[...] = jnp.zeros_like(carry)
            for cp in sends:
                cp.start()

        @pl.when(pl.program_id(0) == nt)
        def _():
            _exchange_wait(sends, arrivals)

        for dr, (x_ref, h_ref) in enumerate(((xf_ref, hf_ref), (xr_ref, hr_ref))):
            x = x_ref[...]
            a, s, _, ig, _ = _lru_coef(x, wg_ref, dr, lv_ref[3 * dr:3 * dr + 1, :],
                                       lv_ref[3 * dr + 1:3 * dr + 2, :], lv_ref[3 * dr + 2:3 * dr + 3, :], gc)
            a_s[...] = a
            b_s[...] = s * (ig * x)
            carry[dr] = _scan_tile(a_s, b_s, h_ref, carry[dr], dr == 1)

    full = lambda shape: pl.BlockSpec(shape, lambda i: (0,) * len(shape))
    fmap = lambda i: (jnp.where(i == 0, nt, i - 1), 0)
    rmap = lambda i: (jnp.where(i == 0, nt, nt - i), 0)
    return _call(
        body, name="lru_forward",
        grid=(nt + 1,),
        out_shape=(jax.ShapeDtypeStruct((la, w), F32), jax.ShapeDtypeStruct((la, w), F32),
                   jax.ShapeDtypeStruct((NDEV,) + wo_shard.shape, wo_shard.dtype)),
        in_specs=[pl.BlockSpec((t, w), fmap), pl.BlockSpec((t, w), rmap), full(wg.shape), full(lv.shape), ANY],
        out_specs=(pl.BlockSpec((t, w), fmap), pl.BlockSpec((t, w), rmap), ANY),
        scratch_shapes=[pltpu.VMEM((t, w), F32), pltpu.VMEM((t, w), F32), pltpu.VMEM((2, 8, w), F32),
                        pltpu.SemaphoreType.DMA((NDEV,)), pltpu.SemaphoreType.DMA((NDEV,))],
        compiler_params=_params(("arbitrary",)),
    )(xb, xb, wg, lv, wo_shard)


def _mix_gates(p_refs, hf_ref, hr_ref, wca_ref, perm_ref, t, w):
    bl, cl, ul, gl, ql = [r[...].astype(F32) for r in p_refs]
    pos, rowlen = _pos_rowlen((t, w), False)
    tt = cl * ul
    z = _conv3(tt, wca_ref, pos, rowlen)
    sig_g = _sigmoid(gl)
    sig_q = _sigmoid(ql)
    ylru = _dot(perm_ref[1], (hf_ref[...] + hr_ref[...]).astype(BF16))
    return bl, cl, ul, gl, ql, tt, z, sig_g, sig_q, ylru, pos, rowlen


def _p_specs(t, w, nt):
    return [pl.BlockSpec((t, w), functools.partial(lambda i, s: (jnp.minimum(i, nt - 1), s), s=s))
            for s in (0, 1, 2, 3, 5)]


def _mix_forward(x, tgt, p, hf, hr, wo, ov, wca, perm, t):
    l, d = x.shape
    w = d // 2
    nt = l // t

    def body(x_ref, tg_ref, b_ref, c_ref, u_ref, g_ref, q_ref, hf_ref, hr_ref, wo_ref, ov_ref, wca_ref, perm_ref,
             dn_ref, ct_ref, do_ref, part_ref):
        i = pl.program_id(0)
        bl, _, _, gl, ql, _, z, sig_g, sig_q, ylru, _, _ = _mix_gates(
            (b_ref, c_ref, u_ref, g_ref, q_ref), hf_ref, hr_ref, wca_ref, perm_ref, t, w)
        ya = bl * z * (gl * sig_g)
        yb = ylru * (ql * sig_q)
        ct_ref[0:w, :] = ya.T.astype(BF16)
        ct_ref[w:, :] = yb.T.astype(BF16)
        out = _dot(ya.astype(BF16), wo_ref[0:w, :]) + _dot(yb.astype(BF16), wo_ref[w:, :])
        gate, fg = ov_ref[0:1, :], ov_ref[1:2, :]
        n = x_ref[...] + gate * out
        rr = lax.rsqrt(jnp.mean(n * n, axis=-1, keepdims=True) + EPS)
        nh = n * rr
        e = nh * fg - tg_ref[...]
        loss = 0.5 * jnp.sum(jnp.mean(e * e, axis=-1, keepdims=True), axis=0, keepdims=True)
        dy = e * (1.0 / d)
        dnh = dy * fg
        dn = rr * (dnh - nh * jnp.mean(dnh * nh, axis=-1, keepdims=True))
        dn_ref[...] = dn
        do_ref[...] = (dn * gate).astype(BF16)

        @pl.when(i == 0)
        def _():
            part_ref[...] = jnp.zeros_like(part_ref)

        part_ref[0:1, :] += jnp.sum(dy * nh, axis=0, keepdims=True)
        part_ref[1:2, :] += jnp.sum(dn * out, axis=0, keepdims=True)
        part_ref[2:3, :] += jnp.broadcast_to(loss, (1, d))

    tile = lambda cols: pl.BlockSpec((t, cols), lambda i: (i, 0))
    full = lambda shape: pl.BlockSpec(shape, lambda i: (0,) * len(shape))
    return _call(
        body, name="mix_forward",
        grid=(nt,),
        out_shape=(jax.ShapeDtypeStruct((l, d), F32), jax.ShapeDtypeStruct((d, l), BF16),
                   jax.ShapeDtypeStruct((l, d), BF16), jax.ShapeDtypeStruct((8, d), F32)),
        in_specs=[tile(d), tile(d)] + _p_specs(t, w, nt) + [tile(w), tile(w),
                  pl.BlockSpec((d, d), lambda i: (0, 0), pipeline_mode=pl.Buffered(1)),
                  full(ov.shape), full(wca.shape), full(perm.shape)],
        out_specs=(tile(d), pl.BlockSpec((d, t), lambda i: (0, i)), tile(d), full((8, d))),
        compiler_params=_params(("arbitrary",)),
    )(x, tgt, p, p, p, p, p, hf, hr, wo, ov, wca, perm)


def _mix_backward(dout, p, hf, hr, wo, wca, perm, g_wout, l, t):
    d = dout.shape[1]
    w = d // 2
    nt = l // t
    la = p.shape[0]

    def body(do_ref, b_ref, c_ref, u_ref, g_ref, q_ref, hf_ref, hr_ref, wo_ref, wca_ref, perm_ref, gw_ref,
             dp_ref, dh_ref, part_ref, sc_ref, send_sems, recv_sems):
        i = pl.program_id(0)
        copies = _scatter_copies(gw_ref, sc_ref, send_sems, recv_sems)

        @pl.when(i == 0)
        def _():
            part_ref[...] = jnp.zeros_like(part_ref)
            for cp in copies:
                cp.start()

        @pl.when(i == nt)
        def _():
            dp_ref[...] = jnp.zeros_like(dp_ref)
            _exchange_wait(copies, copies[1:])

        @pl.when(i < nt)
        def _():
            bl, cl, ul, gl, ql, tt, z, sig_g, sig_q, ylru, pos, rowlen = _mix_gates(
                (b_ref, c_ref, u_ref, g_ref, q_ref), hf_ref, hr_ref, wca_ref, perm_ref, t, w)
            do = do_ref[...]
            dya = _dot_nt(do, wo_ref[0:w, :])
            dyb = _dot_nt(do, wo_ref[w:, :])
            sg = gl * sig_g
            dz = dya * bl * sg
            dt = _conv3_t(dz, wca_ref, pos, rowlen)
            dp_ref[:, 0:w] = (dya * z * sg).astype(BF16)
            dp_ref[:, w:2 * w] = (dt * ul).astype(BF16)
            dp_ref[:, 2 * w:3 * w] = (dt * cl).astype(BF16)
            dp_ref[:, 3 * w:4 * w] = (dya * bl * z * (sig_g * (1.0 + gl * (1.0 - sig_g)))).astype(BF16)
            dp_ref[:, 4 * w:5 * w] = jnp.zeros((t, w), BF16)
            dp_ref[:, 5 * w:6 * w] = (dyb * ylru * (sig_q * (1.0 + ql * (1.0 - sig_q)))).astype(BF16)
            dh_ref[...] = _dot(perm_ref[0], (dyb * (ql * sig_q)).astype(BF16)).astype(BF16)
            part_ref[0:1, :] += jnp.sum(dz * _down(tt, 1, pos), axis=0, keepdims=True)
            part_ref[1:2, :] += jnp.sum(dz * tt, axis=0, keepdims=True)
            part_ref[2:3, :] += jnp.sum(dz * _up(tt, 1, pos, rowlen), axis=0, keepdims=True)

    clamp = lambda cols: pl.BlockSpec((t, cols), lambda i: (jnp.minimum(i, nt - 1), 0))
    full = lambda shape: pl.BlockSpec(shape, lambda i: (0,) * len(shape))
    return _call(
        body, name="mix_backward",
        grid=(nt + 1,),
        out_shape=(jax.ShapeDtypeStruct((la, 6 * w), BF16), jax.ShapeDtypeStruct((l, w), BF16),
                   jax.ShapeDtypeStruct((8, w), F32), jax.ShapeDtypeStruct(g_wout.shape, g_wout.dtype)),
        in_specs=[clamp(d)] + _p_specs(t, w, nt) + [clamp(w), clamp(w),
                  pl.BlockSpec((d, d), lambda i: (0, 0), pipeline_mode=pl.Buffered(1)), full(wca.shape),
                  full(perm.shape), ANY],
        out_specs=(pl.BlockSpec((t, 6 * w), lambda i: (i, 0)), clamp(w), full((8, w)), ANY),
        scratch_shapes=[pltpu.SemaphoreType.DMA((NDEV,)), pltpu.SemaphoreType.DMA((NDEV,))],
        compiler_params=_params(("arbitrary",)),
    )(dout, p, p, p, p, p, hf, hr, wo, wca, perm, g_wout)


def _lru_backward(direction, xb, dhs, hs, wg, lv, l, t, conv=None):
    la, w = hs.shape
    gc = wg.shape[2]
    ng = w // gc
    nt = l // t
    nblk8 = la // 8
    last = conv is not None
    assert last == (direction == 1)

    if direction == 0:
        tile = lambda i: jnp.where(i == nt, nt, nt - 1 - i)
        halo = lambda i: jnp.where(tile(i) == 0, nblk8 - 1, tile(i) * (t // 8) - 1)
    else:
        tile = lambda i: i
        halo = lambda i: jnp.minimum((i + 1) * (t // 8), nblk8 - 1)

    def body(*refs):
        x_ref, dh_ref, hs_ref, halo_ref, wg_ref, lv_ref = refs[:6]
        if last:
            v_ref, wcb_ref, tm_ref, bm_ref, dxo_ref = refs[6:11]
        out_ref, dwg_ref, part_ref, a_s, dh_s, g_s, carry = refs[-7:]
        i = pl.program_id(0)
        is_ctx = i == nt

        @pl.when(i == 0)
        def _():
            carry[...] = jnp.zeros_like(carry)
            dwg_ref[...] = jnp.zeros_like(dwg_ref)
            part_ref[...] = jnp.zeros_like(part_ref)

        xb = x_ref[...]
        lam = lv_ref[3 * direction + 2:3 * direction + 3, :]
        a, s, r, ig, sp = _lru_coef(xb, wg_ref, direction, lv_ref[3 * direction:3 * direction + 1, :],
                                    lv_ref[3 * direction + 1:3 * direction + 2, :], lam, gc)
        hs_t = hs_ref[...]
        r8 = _rows((8, w))
        if direction == 0:
            edge = jnp.where(is_ctx, 0.0, halo_ref[7:8, :])
            first = jnp.where(r8 == 0, edge, pltpu.roll(hs_t[t - 8:, :], 1, 0))
            hprev = jnp.concatenate([first, hs_t[:t - 8, :]], axis=0)
        else:
            edge = jnp.where(is_ctx, 0.0, halo_ref[0:1, :])
            final = jnp.where(r8 == 7, edge, pltpu.roll(hs_t[:8, :], 7, 0))
            hprev = jnp.concatenate([hs_t[8:, :], final], axis=0)
        a_s[...] = a
        dh_s[...] = jnp.where(is_ctx, 0.0, dh_ref[...].astype(F32))
        carry[...] = _scan_tile_backward(a_s, dh_s, g_s, carry[...], direction == 0)

        g = g_s[...]
        ix = ig * xb
        gs = g * s
        dla = (g * a) * (hprev - ix * (a / s))
        dxb = gs * ig
        dzr = dla * (r * (1.0 - r)) * (-LRU_C * sp)
        dzi = gs * ix * (1.0 - ig)
        part_ref[0:1, :] += jnp.sum(dzr, axis=0, keepdims=True)
        part_ref[1:2, :] += jnp.sum(dzi, axis=0, keepdims=True)
        part_ref[2:3, :] += jnp.sum(dla * r, axis=0, keepdims=True) * (LRU_C * _sigmoid(-lam))
        pieces = []
        for gi in range(ng):
            sl = slice(gi * gc, (gi + 1) * gc)
            dz = jnp.concatenate([dzr[:, sl], dzi[:, sl]], axis=-1).astype(BF16)
            pieces.append(_dot_nt(dz, wg_ref[direction, gi]))
            dwg_ref[gi] += _dot(xb[:, sl].T.astype(BF16), dz)
        dxb = dxb + (pieces[0] if ng == 1 else jnp.concatenate(pieces, axis=-1))
        if not last:
            out_ref[...] = dxb
        else:
            dxb = dxb + dxo_ref[...]
            dxb16, v16 = dxb.astype(BF16), v_ref[...]
            dv = wcb_ref[0:1, :] * _dot(bm_ref[0], dxb16)
            for j in range(1, 4):
                dv = dv + wcb_ref[j:j + 1, :] * _dot(bm_ref[j], dxb16)
            out_ref[...] = dv.astype(BF16)
            part_ref[3:4, :] += jnp.sum(dxb, axis=0, keepdims=True)
            for j in range(4):
                part_ref[4 + j:5 + j, :] += jnp.sum(dxb * _dot(tm_ref[j], v16), axis=0, keepdims=True)

    full = lambda shape: pl.BlockSpec(shape, lambda i: (0,) * len(shape))
    kind = lambda i: (jnp.where(i == nt, 1, 0), 0, 0, 0)
    in_specs = [pl.BlockSpec((t, w), lambda i: (tile(i), 0)),
                pl.BlockSpec((t, w), lambda i: (jnp.minimum(tile(i), nt - 1), 0)),
                pl.BlockSpec((t, w), lambda i: (tile(i), 0)),
                pl.BlockSpec((8, w), lambda i: (halo(i), 0)),
                full(wg.shape), full(lv.shape)]
    args = [xb, dhs, hs, hs, wg, lv]
    if last:
        p, wcb, taps_m, back_m, dxb_other, dp = conv
        in_specs += [pl.BlockSpec((t, w), lambda i: (tile(i), 4)), full(wcb.shape),
                     pl.BlockSpec((None, 4, t, t), kind), pl.BlockSpec((None, 4, t, t), kind),
                     pl.BlockSpec((t, w), lambda i: (tile(i), 0)), ANY]
        args += [p, wcb, taps_m, back_m, dxb_other, dp]
        out0 = jax.ShapeDtypeStruct(dp.shape, dp.dtype)
        spec0 = pl.BlockSpec((t, w), lambda i: (tile(i), 4))
        aliases = {11: 0}
    else:
        out0 = jax.ShapeDtypeStruct((la, w), F32)
        spec0 = pl.BlockSpec((t, w), lambda i: (tile(i), 0))
        aliases = {}
    return _call(
        body, name="lru_backward_%d" % direction,
        grid=(nt + 1,),
        out_shape=(out0, jax.ShapeDtypeStruct((ng, gc, 2 * gc), F32), jax.ShapeDtypeStruct((8, w), F32)),
        in_specs=in_specs,
        out_specs=(spec0, full((ng, gc, 2 * gc)), full((8, w))),
        scratch_shapes=[pltpu.VMEM((t, w), F32), pltpu.VMEM((t, w), F32), pltpu.VMEM((t, w), F32),
                        pltpu.VMEM((8, w), F32)],
        input_output_aliases=aliases,
        compiler_params=_params(("arbitrary",)),
    )(*args)


def _weight_grad_t(at, b, nblk_m, nblk_n, tk, name):
    m, k = at.shape
    n = b.shape[1]
    bm, bn = m // nblk_m, n // nblk_n
    nk = k // tk

    def body(a_ref, b_ref, o_ref, acc):
        kk = pl.program_id(2)

        @pl.when(kk == 0)
        def _():
            acc[...] = jnp.zeros_like(acc)

        acc[...] += _dot(a_ref[...], b_ref[...])

        @pl.when(kk == nk - 1)
        def _():
            o_ref[...] = acc[...].astype(BF16)

    return _call(
        body, name=name,
        grid=(nblk_m, nblk_n, nk),
        out_shape=jax.ShapeDtypeStruct((nblk_m * nblk_n, bm, bn), BF16),
        in_specs=[pl.BlockSpec((bm, tk), lambda i, j, kk: (i, kk)),
                  pl.BlockSpec((tk, bn), lambda i, j, kk: (kk, j))],
        out_specs=pl.BlockSpec((None, bm, bn), lambda i, j, kk: (i * nblk_n + j, 0, 0)),
        scratch_shapes=[pltpu.VMEM((bm, bn), F32)],
        compiler_params=_params(("arbitrary", "arbitrary", "arbitrary")),
    )(at, b)


def _weight_grad_scatter(at, b, tk, name):
    m, k = at.shape
    n = b.shape[1]
    bn = n // NDEV
    nk = k // tk
    where = jnp.stack([_idx(_my_pos()), lax.axis_index("c")]).astype(jnp.int32)

    def body(w_ref, a_ref, b_ref, recv_ref, acc, sbuf, sib, sib_send, sib_recv, chip_send, chip_recv, keep_sem):
        s, kk = pl.program_id(0), pl.program_id(1)
        x, y, c = _my_pos()

        @pl.when(kk == 0)
        def _():
            acc[...] = _dot(a_ref[...], b_ref[...])

        @pl.when(kk > 0)
        def _():
            acc[...] += _dot(a_ref[...], b_ref[...])

        def to_sibling(j):
            return pltpu.make_async_remote_copy(
                src_ref=sbuf.at[j % 2], dst_ref=sib.at[j], send_sem=sib_send.at[j], recv_sem=sib_recv.at[j],
                device_id=(x, y, 1 - c), device_id_type=MESH)

        def to_chip(j):
            dist = _chip_order(j, c)
            return pltpu.make_async_remote_copy(
                src_ref=sbuf.at[j % 2], dst_ref=recv_ref.at[dist // 2], send_sem=chip_send.at[j],
                recv_sem=chip_recv.at[dist // 2], device_id=_peer_at(dist), device_id_type=MESH)

        sends = [to_sibling(j) for j in range(4)] + [to_chip(j) for j in range(3)]
        sends.append(pltpu.make_async_copy(sbuf.at[1], recv_ref.at[0], keep_sem))

        for st in range(NDEV):
            @pl.when((kk == nk - 1) & (s == st))
            def _(st=st):
                if st >= 2:
                    sends[st - 2].wait_send()
                part = acc[...]
                if st >= 4:
                    to_sibling(st - 4).wait_recv()
                    part = part + sib[st - 4].astype(F32)
                sbuf[st % 2] = part.astype(BF16)
                sends[st].start()
                if st == NDEV - 1:
                    sends[st - 1].wait_send()
                    sends[st].wait()
                    for j in range(1, 4):
                        pltpu.make_async_remote_copy(
                            src_ref=sbuf.at[0], dst_ref=recv_ref.at[j], send_sem=chip_send.at[0],
                            recv_sem=chip_recv.at[j], device_id=_peer_at(2 * j), device_id_type=MESH).wait_recv()

    blk = lambda s, w_ref: w_ref[0] ^ _scatter_order(s, w_ref[1])
    return _call(
        body, name=name,
        grid_spec=pltpu.PrefetchScalarGridSpec(
            num_scalar_prefetch=1, grid=(NDEV, nk),
            in_specs=[pl.BlockSpec((m, tk), lambda s, kk, w_ref: (0, kk)),
                      pl.BlockSpec((tk, bn), lambda s, kk, w_ref: (kk, blk(s, w_ref)))],
            out_specs=ANY,
            scratch_shapes=[pltpu.VMEM((m, bn), F32), pltpu.VMEM((2, m, bn), BF16), pltpu.VMEM((4, m, bn), BF16),
                            pltpu.SemaphoreType.DMA((4,)), pltpu.SemaphoreType.DMA((4,)),
                            pltpu.SemaphoreType.DMA((4,)), pltpu.SemaphoreType.DMA((4,)),
                            pltpu.SemaphoreType.DMA]),
        out_shape=jax.ShapeDtypeStruct((4, m, bn), BF16),
        compiler_params=_params(("arbitrary", "arbitrary")),
    )(where, at, b)


def _input_backward(dp, w_all, src, mv, row0, tm, nbk, name, dn=None):
    rows, d = src.shape
    nb, _, bw = w_all.shape
    nk = nb // nbk
    blk0 = row0 // tm
    latent = dn is not None

    def body(*refs):
        dp_ref, w_ref, x_ref, mv_ref = refs[:4]
        outs = refs[4 + latent:]
        part_ref, acc = outs[latent], outs[latent + 1]
        i, k = pl.program_id(0), pl.program_id(1)

        @pl.when((i == 0) & (k == 0))
        def _():
            part_ref[...] = jnp.zeros_like(part_ref)

        step = _dot_nt(dp_ref[:, 0:bw], w_ref[0])
        for q in range(1, nbk):
            step = step + _dot_nt(dp_ref[:, q * bw:(q + 1) * bw], w_ref[q])

        @pl.when(k == 0)
        def _():
            acc[...] = step

        @pl.when(k > 0)
        def _():
            acc[...] += step

        @pl.when(k == nk - 1)
        def _():
            xf = x_ref[...]
            r = lax.rsqrt(jnp.mean(xf * xf, axis=-1, keepdims=True) + EPS)
            xn = xf * r
            dhl = acc[...]
            gain, sc = mv_ref[0:1, :], mv_ref[1:2, :]
            dhx = jnp.sum(dhl * xn, axis=0, keepdims=True)
            part_ref[0:1, :] += jnp.sum(dhl, axis=0, keepdims=True)
            part_ref[1:2, :] += dhx * gain
            part_ref[2:3, :] += dhx * (1.0 + sc)
            if latent:
                dxn = dhl * (gain * (1.0 + sc))
                outs[0][...] = refs[4][...] + r * (dxn - xn * jnp.mean(dxn * xn, axis=-1, keepdims=True))

    tile = pl.BlockSpec((tm, d), lambda i, k: (i, 0))
    vec = pl.BlockSpec((8, d), lambda i, k: (0, 0))
    return _call(
        body, name=name,
        grid=(rows // tm, nk),
        out_shape=((jax.ShapeDtypeStruct((rows, d), F32),) if latent else ()) + (jax.ShapeDtypeStruct((8, d), F32),),
        in_specs=[pl.BlockSpec((tm, nbk * bw), lambda i, k: (blk0 + i, k)),
                  pl.BlockSpec((nbk, d, bw), lambda i, k: (k, 0, 0)), tile, vec] + ([tile] if latent else []),
        out_specs=((tile,) if latent else ()) + (vec,),
        scratch_shapes=[pltpu.VMEM((tm, d), F32)],
        compiler_params=_params(("arbitrary", "arbitrary")),
    )(*([dp, w_all, src, mv] + ([dn] if latent else [])))


def _adamw_scattered(parts, w, m, v, tr):
    r, c = w.shape
    nslot = parts.shape[0]

    def body(p_ref, w_ref, m_ref, v_ref, g_ref, d_ref, m2_ref, v2_ref):
        g = p_ref[0].astype(F32)
        for k in range(1, nslot):
            g = g + p_ref[k].astype(F32)
        g_ref[...] = g
        d_ref[...], m2_ref[...], v2_ref[...] = _adamw(w_ref[...], g, m_ref[...], v_ref[...])

    tile = pl.BlockSpec((tr, c), lambda i: (i, 0))
    return _call(
        body, name="adamw_scattered_%dx%d" % (r, c),
        grid=(r // tr,),
        out_shape=tuple(jax.ShapeDtypeStruct((r, c), F32) for _ in range(4)),
        in_specs=[pl.BlockSpec((nslot, tr, c), lambda i: (0, i, 0)), tile, tile, tile],
        out_specs=(tile,) * 4,
        compiler_params=_params(("arbitrary",)),
    )(parts, w, m, v)


def _adamw_ada(st, dmod, w, m, v, tr):
    r, c = w.shape

    def body(s_ref, dm_ref, w_ref, m_ref, v_ref, g_ref, d_ref, m2_ref, v2_ref):
        g = jnp.dot(s_ref[...], dm_ref[...], precision=HIGHEST, preferred_element_type=F32)
        g_ref[...] = g
        d_ref[...], m2_ref[...], v2_ref[...] = _adamw(w_ref[...], g, m_ref[...], v_ref[...])

    tile = pl.BlockSpec((tr, c), lambda i: (i, 0))
    return _call(
        body, name="adamw_ada",
        grid=(r // tr,),
        out_shape=tuple(jax.ShapeDtypeStruct((r, c), F32) for _ in range(4)),
        in_specs=[pl.BlockSpec((tr, 16), lambda i: (i, 0)), pl.BlockSpec((16, c), lambda i: (0, 0)),
                  tile, tile, tile],
        out_specs=(tile,) * 4,
        compiler_params=_params(("arbitrary",)),
    )(st, dmod, w, m, v)


def _adamw_packed(g, w, m, v):
    def body(g_ref, w_ref, m_ref, v_ref, d_ref, m2_ref, v2_ref):
        d_ref[...], m2_ref[...], v2_ref[...] = _adamw(w_ref[...], g_ref[...], m_ref[...], v_ref[...])

    return _call(
        body, name="adamw_packed",
        out_shape=tuple(jax.ShapeDtypeStruct(w.shape, F32) for _ in range(3)),
        in_specs=[VMEM] * 4, out_specs=(VMEM,) * 3,
        compiler_params=_params(),
    )(g, w, m, v)


def _blockdiag_groups(wh, gc):
    h, dh, _ = wh.shape
    g = gc // dh
    w4 = wh.reshape(h // g, g, dh, dh)
    bd = jnp.einsum("ngij,gh->ngihj", w4, jnp.eye(g, dtype=wh.dtype))
    return bd.reshape(h // g, gc, gc)


def _blockdiag_extract(bd, dh):
    ng, gc, _ = bd.shape
    g = gc // dh
    x = bd.reshape(ng, g, dh, g, dh)
    return jnp.einsum("ngihj,gh->ngij", x, jnp.eye(g, dtype=bd.dtype)).reshape(ng * g, dh, dh)


def _rows8(*vecs):
    rows = [jnp.reshape(v, (1, -1)).astype(F32) for v in vecs]
    n = rows[0].shape[1]
    return jnp.concatenate(rows + [jnp.zeros((8 - len(rows), n), F32)], axis=0)


def _pack(pieces):
    flat = jnp.concatenate([jnp.reshape(a, (-1,)).astype(F32) for a in pieces])
    total = -(-flat.shape[0] // 1024) * 1024
    return jnp.pad(flat, (0, total - flat.shape[0])).reshape(total // 128, 128)


def _unpack(packed, shapes):
    flat = packed.reshape(-1)
    out, off = [], 0
    for s in shapes:
        n = 1
        for q in s:
            n *= q
        out.append(flat[off:off + n].reshape(s))
        off += n
    return out


def kernel(x, c, ctx, c_ctx, norm_g, w_ada, b_ada, w_in, w_conv_a, w_conv_b, b_conv_b, lru_wa, lru_ba, lru_wx, lru_bx, lru_lambda, w_out, final_g, loss_target, m_c_ctx, m_norm_g, m_w_ada, m_b_ada, m_w_in, m_w_conv_a, m_w_conv_b, m_b_conv_b, m_lru_wa, m_lru_ba, m_lru_wx, m_lru_bx, m_lru_lambda, m_w_out, m_final_g, v_c_ctx, v_norm_g, v_w_ada, v_b_ada, v_w_in, v_w_conv_a, v_w_conv_b, v_b_conv_b, v_lru_wa, v_lru_ba, v_lru_wx, v_lru_bx, v_lru_lambda, v_w_out, v_final_g):
    _, l, d = x.shape
    lc = ctx.shape[1]
    w = d // 2
    t = lc
    assert l % t == 0 and t % GRID_W == 0 and t % 128 == 0
    dh = w // N_HEADS
    gc = min(w, MXU_WIDTH)
    cols = w_ada.shape[2]
    wo_rows = w_out.shape[1]
    me = _idx(_my_pos())
    x2, ctx2, tgt2 = x[0], ctx[0], loss_target[0]
    w_ada2, w_in2, w_out2 = w_ada[0], w_in[0], w_out[0]

    small_mine = jnp.concatenate([w_conv_a[0], w_conv_b[0], lru_ba[0], lru_bx[0], lru_lambda[0],
                                  jnp.zeros((3, w // NDEV), F32)], axis=0)
    mod_all, s_mat, small_all = _mod_forward(
        jnp.broadcast_to(c, (8, d)), jnp.broadcast_to(c_ctx[None], (8, d)), w_ada2, small_mine)
    mod = jnp.transpose(mod_all, (1, 0, 2)).reshape(16, NDEV * cols) + b_ada
    mod_lat = lax.dynamic_slice_in_dim(mod, me, 1, axis=0)
    sh_l, sc_l, gt_l = jnp.split(mod_lat, 3, axis=-1)
    sh_c, sc_c, _ = jnp.split(mod[8:9], 3, axis=-1)
    small = jnp.transpose(small_all, (1, 0, 2)).reshape(16, w)
    wca = _rows8(*[small[j] for j in range(0, 3)])
    wcb = _rows8(*[small[j] for j in range(3, 7)], b_conv_b)
    lv = _rows8(small[7], small[9], small[11], small[8], small[10], small[12])
    wg = jnp.stack([
        jnp.concatenate([_blockdiag_groups(lru_wa[0, dr], gc), _blockdiag_groups(lru_wx[0, dr], gc)], axis=-1)
        for dr in range(2)]).astype(BF16)

    la = l + lc
    tm = 2 * t if l % (2 * t) == 0 else t
    tk = 3 * t if la % (3 * t) == 0 else t
    h, hlt = _normalize(x2, _rows8(norm_g, sc_l, sh_l), la, 0, tm, "normalize")
    h, hlt = _normalize(ctx2, _rows8(norm_g, sc_c, sh_c), la, l, t, "normalize_ctx", prev=(h, hlt))
    p, w_all = _in_projection(h, w_in2.astype(BF16), tk)
    taps_m, back_m, perm = _scan_matrices(t)
    xb = _conv_input(p, wcb, taps_m, l, t)
    hf, hr, wo_all = _lru_forward(xb, wg, lv, w_out2.astype(BF16), l, t)
    wo = wo_all.reshape(d, d)
    dn, catt, dout, part_mix = _mix_forward(x2, tgt2, p, hf, hr, wo, _rows8(gt_l, final_g), wca, perm, t)
    g_wout = _weight_grad_t(catt, dout, 2, 1, 4 * t if l % (4 * t) == 0 else t, "grad_w_out")
    dp, dhs, part_ca, sc_wout = _mix_backward(dout, p, hf, hr, wo, wca, perm, g_wout.reshape(NDEV, wo_rows, d), l, t)
    dxb0, dwg0, part_l0 = _lru_backward(0, xb, dhs, hf, wg, lv, l, t)
    dp, dwg1, part_l1 = _lru_backward(1, xb, dhs, hr, wg, lv, l, t, conv=(p, wcb, taps_m, back_m, dxb0, dp))
    sc_win = _weight_grad_scatter(hlt, dp, tk, "grad_w_in")
    grad_x, part_lat = _input_backward(dp, w_all, x2, _rows8(norm_g, sc_l), 0, tm, 2, "input_backward", dn=dn)
    (part_ctx,) = _input_backward(dp, w_all, ctx2, _rows8(norm_g, sc_c), l, t, 2, "input_backward_ctx")
    part_in = jnp.concatenate([part_lat[0:2], part_ctx[0:2], (part_lat[2] + part_ctx[2])[None]], axis=0)

    dwa = jnp.stack([_blockdiag_extract(dwg0[:, :, :gc], dh), _blockdiag_extract(dwg1[:, :, :gc], dh)])
    dwx = jnp.stack([_blockdiag_extract(dwg0[:, :, gc:], dh), _blockdiag_extract(dwg1[:, :, gc:], dh)])
    lru_part = jnp.stack([dwa, dwx]).reshape(NDEV, -1, 128)
    zeros_d = jnp.zeros((d,), F32)
    pieces = [
        jnp.concatenate([part_in[0], part_in[1], part_mix[1]]),
        jnp.concatenate([part_in[2], part_in[3], zeros_d]),
        part_in[4], part_mix[0], part_ca[0:3], part_l1[4:8], part_l1[3],
        jnp.stack([part_l0[0], part_l1[0]]), jnp.stack([part_l0[1], part_l1[1]]),
        jnp.stack([part_l0[2], part_l1[2]]), part_mix[2, 0:1],
    ]
    shapes = [(3 * d,), (3 * d,), (d,), (d,), (3, w), (4, w), (w,), (2, w), (2, w), (2, w), (1,)]
    sig_cc = jax.nn.sigmoid(c_ctx)
    dsilu_cc = jnp.broadcast_to((sig_cc * (1.0 + c_ctx * (1.0 - sig_cc)))[None], (8, d))
    psum, pall, lru_sum, g_cctx8 = _reduce_small(_pack(pieces), lru_part, w_ada2, dsilu_cc)
    (g_modl, g_modc, g_norm, g_final, g_ca, g_cb, g_bcb, g_ba, g_bx, g_lam, loss1) = _unpack(psum, shapes)
    loss = loss1[0]
    g_cctx = g_cctx8[0]
    g_bada = (g_modl + g_modc)[None]
    g_lru = lru_sum.reshape(2, 2, N_HEADS, dh, dh)
    g_wa, g_wx = g_lru[0][None], g_lru[1][None]
    wsl = w // NDEV
    mine = lambda a: lax.dynamic_slice_in_dim(a, me * wsl, wsl, axis=-1)
    g_ca_m, g_cb_m, g_ba_m, g_bx_m, g_lam_m = (mine(g_ca)[None], mine(g_cb)[None], mine(g_ba)[None],
                                               mine(g_bx)[None], mine(g_lam)[None])
    g_norm, g_bcb = g_norm[None], g_bcb[None]

    cb = cols // 128
    per_dev = pall[:, :3 * d // 128].reshape(NDEV, NDEV, cols)
    dmod_lat = lax.dynamic_slice_in_dim(per_dev, me, 1, axis=1)[:, 0]
    dmod_ctx = lax.dynamic_slice_in_dim(g_modc.reshape(NDEV, cols), me, 1, axis=0)
    dmod16 = jnp.concatenate([dmod_lat, dmod_ctx, jnp.zeros((7, cols), F32)], axis=0)
    tr_ada = 256 if d % 256 == 0 else d
    g_wada, d_wada, m_wada, v_wada = _adamw_ada(s_mat.T, dmod16, w_ada2, m_w_ada[0], v_w_ada[0], tr_ada)
    g_win2, d_win, m_win, v_win = _adamw_scattered(sc_win, w_in2, m_w_in[0], v_w_in[0], tr_ada)
    tr_out = 64 if wo_rows % 64 == 0 else wo_rows
    g_wout2, d_wout, m_wout, v_wout = _adamw_scattered(sc_wout, w_out2, m_w_out[0], v_w_out[0], tr_out)

    small_w = [c_ctx, norm_g, b_ada, w_conv_a, w_conv_b, b_conv_b, lru_wa, lru_ba, lru_wx, lru_bx, lru_lambda, final_g]
    small_m = [m_c_ctx, m_norm_g, m_b_ada, m_w_conv_a, m_w_conv_b, m_b_conv_b, m_lru_wa, m_lru_ba, m_lru_wx,
               m_lru_bx, m_lru_lambda, m_final_g]
    small_v = [v_c_ctx, v_norm_g, v_b_ada, v_w_conv_a, v_w_conv_b, v_b_conv_b, v_lru_wa, v_lru_ba, v_lru_wx,
               v_lru_bx, v_lru_lambda, v_final_g]
    small_g = [g_cctx, g_norm, g_bada, g_ca_m, g_cb_m, g_bcb, g_wa, g_ba_m, g_wx, g_bx_m, g_lam_m, g_final]
    sshapes = [a.shape for a in small_w]
    d_s, m_s, v_s = _adamw_packed(_pack(small_g), _pack(small_w), _pack(small_m), _pack(small_v))
    d_s, m_s, v_s = _unpack(d_s, sshapes), _unpack(m_s, sshapes), _unpack(v_s, sshapes)
    small_g = [jnp.reshape(a, s) for a, s in zip(small_g, sshapes)]

    def weights(small_list, ada, win, wout):
        (cctx_, norm_, bada_, ca_, cb_, bcb_, wa_, ba_, wx_, bx_, lam_, final_) = small_list
        return [cctx_, norm_, ada[None], bada_, win[None], ca_, cb_, bcb_, wa_, ba_, wx_, bx_, lam_, wout[None], final_]

    return (loss, grad_x[None],
            *weights(small_g, g_wada, g_win2, g_wout2), *weights(d_s, d_wada, d_win, d_wout),
            *weights(m_s, m_wada, m_win, m_wout), *weights(v_s, v_wada, v_win, v_wout))
```

```python
import functools

import jax
import jax.numpy as jnp
import numpy as np
from jax import lax
from jax.experimental import pallas as pl
from jax.experimental.pallas import tpu as pltpu

F32 = jnp.float32
BF16 = jnp.bfloat16
MESH = pl.DeviceIdType.MESH
NDEV = 8
GRID_W = 64
N_HEADS = 16
LRU_C = 8.0
EPS = 1e-6
MXU_WIDTH = 256
VMEM_LIMIT = 60 * 1024 * 1024

ADAM_LR = 0.001
ADAM_B1 = 0.9
ADAM_B2 = 0.999
ADAM_EPS = 1e-08
ADAM_WD = 0.01
ADAM_STEP = 10
ADAM_C1 = 1.0 - ADAM_B1 ** ADAM_STEP
ADAM_C2 = 1.0 - ADAM_B2 ** ADAM_STEP

HIGHEST = lax.Precision.HIGHEST
ANY = pl.BlockSpec(memory_space=pl.ANY)
VMEM = pl.BlockSpec(memory_space=pltpu.VMEM)


def _call(body, **kw):
    return pl.pallas_call(body, **kw)


def _params(sem=None, vmem=VMEM_LIMIT):
    return pltpu.CompilerParams(dimension_semantics=sem, vmem_limit_bytes=vmem)


def _my_pos():
    return lax.axis_index("x"), lax.axis_index("y"), lax.axis_index("c")


def _idx(pos):
    return 4 * pos[0] + 2 * pos[1] + pos[2]


def _peer(k):
    x, y, c = _my_pos()
    return ((1 - x) if (k >> 2) & 1 else x, (1 - y) if (k >> 1) & 1 else y, (1 - c) if k & 1 else c)


def _exchange_vmem(src_ref, dst_ref, send_sems, recv_sems, base):
    me = _idx(_my_pos())
    sends = []
    for k in range(1, NDEV):
        cp = pltpu.make_async_remote_copy(
            src_ref=src_ref, dst_ref=dst_ref.at[me], send_sem=send_sems.at[base + k - 1],
            recv_sem=recv_sems.at[base + k - 1], device_id=_peer(k), device_id_type=MESH)
        cp.start()
        sends.append(cp)
    dst_ref[me] = src_ref[...]
    for k in range(1, NDEV):
        peer = _peer(k)
        pltpu.make_async_remote_copy(
            src_ref=src_ref, dst_ref=dst_ref.at[_idx(peer)], send_sem=send_sems.at[base + k - 1],
            recv_sem=recv_sems.at[base + k - 1], device_id=peer, device_id_type=MESH).wait_recv()
    for cp in sends:
        cp.wait_send()


def _sigmoid(z):
    return 0.5 * jnp.tanh(0.5 * z) + 0.5


def _softplus(x):
    return jnp.maximum(x, 0.0) + jnp.log1p(jnp.exp(-jnp.abs(x)))


def _one_minus_sq(a, la):
    series = (-2.0 * la) * (1.0 + la)
    return jnp.where(la > -0.0015, series, 1.0 - a * a)


def _dot(a, b):
    return jnp.dot(a, b, preferred_element_type=F32)


def _dot_nt(a, b):
    return lax.dot_general(a, b, (((1,), (1,)), ((), ())), preferred_element_type=F32)


def _rows(shape):
    return lax.broadcasted_iota(jnp.int32, shape, 0)


def _down(x, k, pos):
    return jnp.where(pos >= k, pltpu.roll(x, k, 0), 0.0)


def _up(x, k, pos, rowlen):
    return jnp.where(pos + k < rowlen, pltpu.roll(x, x.shape[0] - k, 0), 0.0)


def _pos_rowlen(shape, is_ctx):
    t = _rows(shape)
    pos = jnp.where(is_ctx, t, t & (GRID_W - 1))
    rowlen = jnp.where(is_ctx, shape[0], GRID_W)
    return pos, rowlen


def _scan_matrices(t):
    seg = t // 8
    r = np.arange(t)
    perm = (np.arange(t)[None, :] == ((r % 8) * seg + r // 8)[:, None]).astype(np.float32)
    rows, cols = r[:, None], r[None, :]
    taps, back = [], []
    for rowlen in (GRID_W, t):
        pos = rows % rowlen
        shift = {-2: (cols == rows - 2) & (pos >= 2), -1: (cols == rows - 1) & (pos >= 1),
                 0: cols == rows, 1: (cols == rows + 1) & (pos + 1 < rowlen),
                 2: (cols == rows + 2) & (pos + 2 < rowlen)}
        taps.append(np.stack([perm @ shift[k].astype(np.float32) for k in (-2, -1, 0, 1)]))
        back.append(np.stack([shift[k].astype(np.float32) @ perm.T for k in (2, 1, 0, -1)]))
    as_bf16 = lambda a: jnp.asarray(a, dtype=BF16)
    return as_bf16(np.stack(taps)), as_bf16(np.stack(back)), as_bf16(np.stack([perm, perm.T]))


def _conv3(t, w_ref, pos, rowlen):
    return w_ref[0:1, :] * _down(t, 1, pos) + w_ref[1:2, :] * t + w_ref[2:3, :] * _up(t, 1, pos, rowlen)


def _conv3_t(dz, w_ref, pos, rowlen):
    return w_ref[0:1, :] * _up(dz, 1, pos, rowlen) + w_ref[1:2, :] * dz + w_ref[2:3, :] * _down(dz, 1, pos)


def _chunk_scan(a, b, reverse):
    row = _rows(a.shape)
    for s in (1, 2, 4):
        if reverse:
            m = row < 8 - s
            sh = 8 - s
        else:
            m = row >= s
            sh = s
        a_s = jnp.where(m, pltpu.roll(a, sh, 0), 1.0)
        b_s = jnp.where(m, pltpu.roll(b, sh, 0), 0.0)
        b = b + a * b_s
        a = a * a_s
    return a, b


def _chain_segments(ptot, hend, carry, reverse):
    ca, cb = _chunk_scan(ptot, hend, reverse)
    incl = ca * carry + cb
    r8 = _rows(incl.shape)
    if reverse:
        start = jnp.where(r8 < 7, pltpu.roll(incl, 7, 0), carry)
        last = incl[0:1, :]
    else:
        start = jnp.where(r8 >= 1, pltpu.roll(incl, 1, 0), carry)
        last = incl[7:8, :]
    return start, jnp.broadcast_to(last, incl.shape)


def _blocks(nblock, reverse):
    order = range(nblock - 1, -1, -1) if reverse else range(nblock)
    return [slice(8 * k, 8 * k + 8) for k in order]


def _scan_tile(a_ref, b_ref, out_ref, carry, reverse):
    t, w = a_ref.shape
    seg = t // 8

    hend, ptot = jnp.zeros((8, w), F32), jnp.ones((8, w), F32)
    for rows in _blocks(seg, reverse):
        a = a_ref[rows, :]
        hend, ptot = a * hend + b_ref[rows, :], a * ptot
    h, new_carry = _chain_segments(ptot, hend, carry, reverse)
    for rows in _blocks(seg, reverse):
        h = a_ref[rows, :] * h + b_ref[rows, :]
        out_ref[rows, :] = h
    return new_carry


def _scan_tile_backward(a_ref, dh_ref, g_ref, carry, reverse):
    t, w = a_ref.shape
    seg = t // 8

    uend, ptot = jnp.zeros((8, w), F32), jnp.ones((8, w), F32)
    for rows in _blocks(seg, reverse):
        a = a_ref[rows, :]
        uend, ptot = a * (dh_ref[rows, :] + uend), a * ptot
    u, new_carry = _chain_segments(ptot, uend, carry, reverse)
    for rows in _blocks(seg, reverse):
        g = dh_ref[rows, :] + u
        g_ref[rows, :] = g
        u = a_ref[rows, :] * g
    return new_carry


def _lru_coef(xb, wg_ref, d, ba, bx, lam, gc):
    w = xb.shape[1]
    xb16 = xb.astype(BF16)
    zr, zi = [], []
    for g in range(w // gc):
        z = _dot(xb16[:, g * gc:(g + 1) * gc], wg_ref[d, g])
        zr.append(z[:, :gc])
        zi.append(z[:, gc:])
    zr = zr[0] if len(zr) == 1 else jnp.concatenate(zr, axis=-1)
    zi = zi[0] if len(zi) == 1 else jnp.concatenate(zi, axis=-1)
    r = _sigmoid(zr + ba)
    ig = _sigmoid(zi + bx)
    sp = _softplus(-lam)
    la = r * (-LRU_C * sp)
    a = jnp.exp(la)
    s = jnp.sqrt(_one_minus_sq(a, la))
    return a, s, r, ig, sp


def _adamw(w, g, m, v):
    m2 = ADAM_B1 * m + (1.0 - ADAM_B1) * g
    v2 = ADAM_B2 * v + (1.0 - ADAM_B2) * (g * g)
    m_hat = m2 / ADAM_C1
    v_hat = v2 / ADAM_C2
    delta = -ADAM_LR * (m_hat / (jnp.sqrt(v_hat) + ADAM_EPS) + ADAM_WD * w)
    return delta, m2, v2


def _mod_forward(c8, cctx8, w_ada, small):
    d = c8.shape[1]
    cols = w_ada.shape[1]

    def body(c_ref, cctx_ref, w_ref, sm_ref, mod_ref, s_ref, sm_all, cbuf, mod_my, send_sems, recv_sems):
        _exchange_vmem(sm_ref, sm_all, send_sems, recv_sems, 2 * (NDEV - 1))
        _exchange_vmem(c_ref, cbuf, send_sems, recv_sems, 0)
        row = _rows((8, d))
        c_all = jnp.zeros((8, d), F32)
        for b in range(NDEV):
            c_all = jnp.where(row == b, cbuf[b], c_all)
        cc = cctx_ref[...]
        s_top = c_all * _sigmoid(c_all)
        s_bot = jnp.where(row == 0, cc * _sigmoid(cc), 0.0)
        s = jnp.concatenate([s_top, s_bot], axis=0)
        s_ref[...] = s
        mod_my[...] = jnp.dot(s, w_ref[...], precision=HIGHEST, preferred_element_type=F32)
        _exchange_vmem(mod_my, mod_ref, send_sems, recv_sems, NDEV - 1)

    return _call(
        body, name="mod_forward",
        out_shape=(jax.ShapeDtypeStruct((NDEV, 16, cols), F32), jax.ShapeDtypeStruct((16, d), F32),
                   jax.ShapeDtypeStruct((NDEV,) + small.shape, F32)),
        in_specs=[VMEM] * 4, out_specs=(VMEM,) * 3,
        scratch_shapes=[pltpu.VMEM((NDEV, 8, d), F32), pltpu.VMEM((16, cols), F32),
                        pltpu.SemaphoreType.DMA((3 * (NDEV - 1),)), pltpu.SemaphoreType.DMA((3 * (NDEV - 1),))],
        compiler_params=_params(),
    )(c8, cctx8, w_ada, small)


def _scatter_copies(src_ref, dst_ref, send_sems, recv_sems):
    me = _idx(_my_pos())
    copies = [pltpu.make_async_copy(src_ref.at[me], dst_ref.at[0], send_sems.at[0])]
    for k in range(1, NDEV):
        peer = _peer(k)
        copies.append(pltpu.make_async_remote_copy(
            src_ref=src_ref.at[_idx(peer)], dst_ref=dst_ref.at[k], send_sem=send_sems.at[k],
            recv_sem=recv_sems.at[k], device_id=peer, device_id_type=MESH))
    return copies


def _gather_copies(src_ref, dst_ref, send_sems, recv_sems):
    me = _idx(_my_pos())
    sends = [pltpu.make_async_copy(src_ref, dst_ref.at[me], send_sems.at[0])]
    arrivals = []
    for k in range(1, NDEV):
        peer = _peer(k)
        sends.append(pltpu.make_async_remote_copy(
            src_ref=src_ref, dst_ref=dst_ref.at[me], send_sem=send_sems.at[k],
            recv_sem=recv_sems.at[k], device_id=peer, device_id_type=MESH))
        arrivals.append(pltpu.make_async_remote_copy(
            src_ref=src_ref, dst_ref=dst_ref.at[_idx(peer)], send_sem=send_sems.at[k],
            recv_sem=recv_sems.at[k], device_id=peer, device_id_type=MESH))
    return sends, arrivals


def _exchange_wait(sends, arrivals):
    sends[0].wait()
    for cp in arrivals:
        cp.wait_recv()
    for cp in sends[1:]:
        cp.wait_send()


def _chip_order(k, c):
    return (6, 4 - 2 * c, 2 + 2 * c, 0)[k]


def _scatter_order(s, c):
    k = s >> 1
    mine = jnp.where(k == 0, 6, jnp.where(k == 1, 4 - 2 * c, jnp.where(k == 2, 2 + 2 * c, 0)))
    theirs = jnp.where(k == 0, 6, jnp.where(k == 1, 2 + 2 * c, jnp.where(k == 2, 4 - 2 * c, 0))) ^ 1
    return jnp.where((s & 1) == 0, theirs, mine)


def _peer_at(dist):
    x, y, c = _my_pos()
    return (x ^ ((dist >> 2) & 1), y ^ ((dist >> 1) & 1), c ^ (dist & 1))


def _reduce_small(packed, lru_parts, w_ada, dsilu_cctx):
    rp = packed.shape[0]
    rl = lru_parts.shape[1]
    d, cols = w_ada.shape
    assert cols % 128 == 0
    cb = cols // 128

    def body(p_ref, l_ref, w_ref, ds_ref, sum_ref, all_ref, lru_ref, cctx_ref,
             lbuf, lsum, cpart, call, send_sems, recv_sems, lsend, lrecv):
        me = _idx(_my_pos())
        scattered = _scatter_copies(l_ref, lbuf, lsend, lrecv)
        for cp in scattered:
            cp.start()
        _exchange_vmem(p_ref, all_ref, send_sems, recv_sems, 0)
        acc = all_ref[0]
        for j in range(1, NDEV):
            acc = acc + all_ref[j]
        sum_ref[...] = acc
        _exchange_wait(scattered, scattered[1:])
        red = lbuf[0]
        for k in range(1, NDEV):
            red = red + lbuf[k]
        lsum[...] = red
        _exchange_vmem(lsum, lru_ref, send_sems, recv_sems, NDEV - 1)
        part = jnp.zeros((8, d), F32)
        for q in range(cb):
            dm = jnp.broadcast_to(sum_ref[pl.ds((NDEV + me) * cb + q, 1), :], (8, 128))
            part = part + lax.dot_general(dm, w_ref[:, q * 128:(q + 1) * 128],
                                          (((1,), (1,)), ((), ())), precision=HIGHEST,
                                          preferred_element_type=F32)
        cpart[...] = part
        _exchange_vmem(cpart, call, send_sems, recv_sems, 2 * (NDEV - 1))
        tot = call[0]
        for j in range(1, NDEV):
            tot = tot + call[j]
        cctx_ref[...] = tot * ds_ref[...]

    return _call(
        body, name="reduce_small",
        out_shape=(jax.ShapeDtypeStruct((rp, 128), F32), jax.ShapeDtypeStruct((NDEV, rp, 128), F32),
                   jax.ShapeDtypeStruct((NDEV, rl, 128), F32), jax.ShapeDtypeStruct((8, d), F32)),
        in_specs=[VMEM] * 4, out_specs=(VMEM,) * 4,
        scratch_shapes=[pltpu.VMEM((NDEV, rl, 128), F32), pltpu.VMEM((rl, 128), F32), pltpu.VMEM((8, d), F32),
                        pltpu.VMEM((NDEV, 8, d), F32),
                        pltpu.SemaphoreType.DMA((3 * (NDEV - 1),)), pltpu.SemaphoreType.DMA((3 * (NDEV - 1),)),
                        pltpu.SemaphoreType.DMA((NDEV,)), pltpu.SemaphoreType.DMA((NDEV,))],
        compiler_params=_params(),
    )(packed, lru_parts, w_ada, dsilu_cctx)


def _normalize(src, mv, la, row0, tm, name, prev=None):
    rows, d = src.shape
    blk0 = row0 // tm

    def body(*refs):
        x_ref, mv_ref = refs[:2]
        h_ref, ht_ref = refs[-2:]
        xf = x_ref[...]
        r = lax.rsqrt(jnp.mean(xf * xf, axis=-1, keepdims=True) + EPS)
        h = xf * r * (mv_ref[0:1, :] * (1.0 + mv_ref[1:2, :])) + mv_ref[2:3, :]
        h_ref[...] = h.astype(BF16)
        ht_ref[...] = h.T.astype(BF16)

    in_specs = [pl.BlockSpec((tm, d), lambda i: (i, 0)), pl.BlockSpec((8, d), lambda i: (0, 0))]
    args = [src, mv]
    aliases = {}
    if prev is not None:
        in_specs += [ANY, ANY]
        args += list(prev)
        aliases = {2: 0, 3: 1}
    return _call(
        body, name=name,
        grid=(rows // tm,),
        out_shape=(jax.ShapeDtypeStruct((la, d), BF16), jax.ShapeDtypeStruct((d, la), BF16)),
        in_specs=in_specs,
        out_specs=(pl.BlockSpec((tm, d), lambda i: (blk0 + i, 0)), pl.BlockSpec((d, tm), lambda i: (0, blk0 + i))),
        input_output_aliases=aliases,
        compiler_params=_params(("arbitrary",)),
    )(*args)


def _gather_order(step):
    return (step & 1) | (((step >> 2) & 1) << 1) | (((step >> 1) & 1) << 2)


def _in_projection(h, w_shard, tm):
    la, d = h.shape
    bw = w_shard.shape[1]
    ni = la // tm
    where = jnp.reshape(_idx(_my_pos()), (1,)).astype(jnp.int32)

    def body(me_ref, h_ref, w_ref, p_ref, all_ref, wbuf, send_sems, recv_sems, local_sems):
        s, i = pl.program_id(0), pl.program_id(1)
        x, y, c = _my_pos()
        me, sibling = (x, y, c), (x, y, 1 - c)
        chips = [(1 - x, y), (x, 1 - y), (1 - x, 1 - y)]

        def copy(k, block, to, from_shard=False):
            return pltpu.make_async_remote_copy(
                src_ref=w_ref if from_shard else all_ref.at[_idx(block)], dst_ref=all_ref.at[_idx(block)],
                send_sem=send_sems.at[k], recv_sem=recv_sems.at[k], device_id=to, device_id_type=MESH)

        def load(block, slot):
            return pltpu.make_async_copy(all_ref.at[_idx(block)], wbuf.at[slot], local_sems.at[1])

        keep = pltpu.make_async_copy(w_ref, all_ref.at[_idx(me)], local_sems.at[0])
        first = [copy(0, me, sibling, True)] + [copy(1 + j, me, (*chip, c), True) for j, chip in enumerate(chips)]
        passed = [copy(4 + j, (*chip, c), sibling) for j, chip in enumerate(chips)]
        steps = [(copy(0, sibling, me), None, sibling)]
        for j, chip in enumerate(chips):
            steps.append((copy(1 + j, (*chip, c), me), passed[j], (*chip, c)))
            steps.append((copy(4 + j, (*chip, 1 - c), me), None, (*chip, 1 - c)))

        @pl.when((s == 0) & (i == 0))
        def _():
            keep.start()
            mine = pltpu.make_async_copy(w_ref, wbuf.at[0], local_sems.at[1])
            mine.start()
            for cp in first:
                cp.start()
            mine.wait()

        for n, (arrival, forward, block) in enumerate(steps, start=1):
            @pl.when((s == n - 1) & (i == ni - 1))
            def _(arrival=arrival, forward=forward, block=block, n=n):
                arrival.wait_recv()
                if forward is not None:
                    forward.start()
                load(block, n % 2).start()

        @pl.when((s > 0) & (i == 0))
        def _():
            load(me, s % 2).wait()

        p_ref[...] = _dot(h_ref[...], wbuf[s % 2]).astype(BF16)

        @pl.when((s == NDEV - 1) & (i == ni - 1))
        def _():
            for cp in first + passed:
                cp.wait_send()
            keep.wait()

    return _call(
        body, name="in_projection",
        grid_spec=pltpu.PrefetchScalarGridSpec(
            num_scalar_prefetch=1, grid=(NDEV, ni),
            in_specs=[pl.BlockSpec((tm, d), lambda s, i, me_ref: (i, 0)), ANY],
            out_specs=(pl.BlockSpec((tm, bw), lambda s, i, me_ref: (i, me_ref[0] ^ _gather_order(s))), ANY),
            scratch_shapes=[pltpu.VMEM((2, d, bw), BF16), pltpu.SemaphoreType.DMA((7,)),
                            pltpu.SemaphoreType.DMA((7,)), pltpu.SemaphoreType.DMA((2,))]),
        out_shape=(jax.ShapeDtypeStruct((la, NDEV * bw), BF16), jax.ShapeDtypeStruct((NDEV, d, bw), BF16)),
        compiler_params=_params(("arbitrary", "arbitrary")),
    )(where, h, w_shard)


def _conv_input(p, wcb, taps_m, l, t):
    la = p.shape[0]
    w = wcb.shape[1]
    nt = l // t

    def body(v_ref, wcb_ref, tm_ref, xb_ref):
        v16 = v_ref[...]
        xb = wcb_ref[4:5, :] + wcb_ref[0:1, :] * _dot(tm_ref[0], v16)
        for j in range(1, 4):
            xb = xb + wcb_ref[j:j + 1, :] * _dot(tm_ref[j], v16)
        xb_ref[...] = xb

    return _call(
        body, name="conv_input",
        grid=(nt + 1,),
        out_shape=jax.ShapeDtypeStruct((la, w), F32),
        in_specs=[pl.BlockSpec((t, w), lambda i: (i, 4)), pl.BlockSpec((8, w), lambda i: (0, 0)),
                  pl.BlockSpec((None, 4, t, t), lambda i: (i // nt, 0, 0, 0))],
        out_specs=pl.BlockSpec((t, w), lambda i: (i, 0)),
        compiler_params=_params(("arbitrary",)),
    )(p, wcb, taps_m)


def _lru_forward(xb, wg, lv, wo_shard, l, t):
    la, w = xb.shape
    gc = wg.shape[2]
    nt = l // t

    def body(xf_ref, xr_ref, wg_ref, lv_ref, wo_ref, hf_ref, hr_ref, wo_all,
             a_s, b_s, carry, send_sems, recv_sems):
        sends, arrivals = _gather_copies(wo_ref, wo_all, send_sems, recv_sems)

        @pl.when(pl.program_id(0) == 0)
        def _():
            carry[...] = jnp.zeros_like(carry)
            for cp in sends:
                cp.start()

        @pl.when(pl.program_id(0) == nt)
        def _():
            _exchange_wait(sends, arrivals)

        for dr, (x_ref, h_ref) in enumerate(((xf_ref, hf_ref), (xr_ref, hr_ref))):
            x = x_ref[...]
            a, s, _, ig, _ = _lru_coef(x, wg_ref, dr, lv_ref[3 * dr:3 * dr + 1, :],
                                       lv_ref[3 * dr + 1:3 * dr + 2, :], lv_ref[3 * dr + 2:3 * dr + 3, :], gc)
            a_s[...] = a
            b_s[...] = s * (ig * x)
            carry[dr] = _scan_tile(a_s, b_s, h_ref, carry[dr], dr == 1)

    full = lambda shape: pl.BlockSpec(shape, lambda i: (0,) * len(shape))
    fmap = lambda i: (jnp.where(i == 0, nt, i - 1), 0)
    rmap = lambda i: (jnp.where(i == 0, nt, nt - i), 0)
    return _call(
        body, name="lru_forward",
        grid=(nt + 1,),
        out_shape=(jax.ShapeDtypeStruct((la, w), F32), jax.ShapeDtypeStruct((la, w), F32),
                   jax.ShapeDtypeStruct((NDEV,) + wo_shard.shape, wo_shard.dtype)),
        in_specs=[pl.BlockSpec((t, w), fmap), pl.BlockSpec((t, w), rmap), full(wg.shape), full(lv.shape), ANY],
        out_specs=(pl.BlockSpec((t, w), fmap), pl.BlockSpec((t, w), rmap), ANY),
        scratch_shapes=[pltpu.VMEM((t, w), F32), pltpu.VMEM((t, w), F32), pltpu.VMEM((2, 8, w), F32),
                        pltpu.SemaphoreType.DMA((NDEV,)), pltpu.SemaphoreType.DMA((NDEV,))],
        compiler_params=_params(("arbitrary",)),
    )(xb, xb, wg, lv, wo_shard)


def _mix_gates(p_refs, hf_ref, hr_ref, wca_ref, perm_ref, t, w):
    bl, cl, ul, gl, ql = [r[...].astype(F32) for r in p_refs]
    pos, rowlen = _pos_rowlen((t, w), False)
    tt = cl * ul
    z = _conv3(tt, wca_ref, pos, rowlen)
    sig_g = _sigmoid(gl)
    sig_q = _sigmoid(ql)
    ylru = _dot(perm_ref[1], (hf_ref[...] + hr_ref[...]).astype(BF16))
    return bl, cl, ul, gl, ql, tt, z, sig_g, sig_q, ylru, pos, rowlen


def _p_specs(t, w, nt):
    return [pl.BlockSpec((t, w), functools.partial(lambda i, s: (jnp.minimum(i, nt - 1), s), s=s))
            for s in (0, 1, 2, 3, 5)]


def _mix_forward(x, tgt, p, hf, hr, wo, ov, wca, perm, t):
    l, d = x.shape
    w = d // 2
    nt = l // t

    def body(x_ref, tg_ref, b_ref, c_ref, u_ref, g_ref, q_ref, hf_ref, hr_ref, wo_ref, ov_ref, wca_ref, perm_ref,
             dn_ref, ct_ref, do_ref, part_ref):
        i = pl.program_id(0)
        bl, _, _, gl, ql, _, z, sig_g, sig_q, ylru, _, _ = _mix_gates(
            (b_ref, c_ref, u_ref, g_ref, q_ref), hf_ref, hr_ref, wca_ref, perm_ref, t, w)
        ya = bl * z * (gl * sig_g)
        yb = ylru * (ql * sig_q)
        ct_ref[0:w, :] = ya.T.astype(BF16)
        ct_ref[w:, :] = yb.T.astype(BF16)
        out = _dot(ya.astype(BF16), wo_ref[0:w, :]) + _dot(yb.astype(BF16), wo_ref[w:, :])
        gate, fg = ov_ref[0:1, :], ov_ref[1:2, :]
        n = x_ref[...] + gate * out
        rr = lax.rsqrt(jnp.mean(n * n, axis=-1, keepdims=True) + EPS)
        nh = n * rr
        e = nh * fg - tg_ref[...]
        loss = 0.5 * jnp.sum(jnp.mean(e * e, axis=-1, keepdims=True), axis=0, keepdims=True)
        dy = e * (1.0 / d)
        dnh = dy * fg
        dn = rr * (dnh - nh * jnp.mean(dnh * nh, axis=-1, keepdims=True))
        dn_ref[...] = dn
        do_ref[...] = (dn * gate).astype(BF16)

        @pl.when(i == 0)
        def _():
            part_ref[...] = jnp.zeros_like(part_ref)

        part_ref[0:1, :] += jnp.sum(dy * nh, axis=0, keepdims=True)
        part_ref[1:2, :] += jnp.sum(dn * out, axis=0, keepdims=True)
        part_ref[2:3, :] += jnp.broadcast_to(loss, (1, d))

    tile = lambda cols: pl.BlockSpec((t, cols), lambda i: (i, 0))
    full = lambda shape: pl.BlockSpec(shape, lambda i: (0,) * len(shape))
    return _call(
        body, name="mix_forward",
        grid=(nt,),
        out_shape=(jax.ShapeDtypeStruct((l, d), F32), jax.ShapeDtypeStruct((d, l), BF16),
                   jax.ShapeDtypeStruct((l, d), BF16), jax.ShapeDtypeStruct((8, d), F32)),
        in_specs=[tile(d), tile(d)] + _p_specs(t, w, nt) + [tile(w), tile(w),
                  pl.BlockSpec((d, d), lambda i: (0, 0), pipeline_mode=pl.Buffered(1)),
                  full(ov.shape), full(wca.shape), full(perm.shape)],
        out_specs=(tile(d), pl.BlockSpec((d, t), lambda i: (0, i)), tile(d), full((8, d))),
        compiler_params=_params(("arbitrary",)),
    )(x, tgt, p, p, p, p, p, hf, hr, wo, ov, wca, perm)


def _mix_backward(dout, p, hf, hr, wo, wca, perm, g_wout, l, t):
    d = dout.shape[1]
    w = d // 2
    nt = l // t
    la = p.shape[0]

    def body(do_ref, b_ref, c_ref, u_ref, g_ref, q_ref, hf_ref, hr_ref, wo_ref, wca_ref, perm_ref, gw_ref,
             dp_ref, dh_ref, part_ref, sc_ref, send_sems, recv_sems):
        i = pl.program_id(0)
        copies = _scatter_copies(gw_ref, sc_ref, send_sems, recv_sems)

        @pl.when(i == 0)
        def _():
            part_ref[...] = jnp.zeros_like(part_ref)
            for cp in copies:
                cp.start()

        @pl.when(i == nt)
        def _():
            dp_ref[...] = jnp.zeros_like(dp_ref)
            _exchange_wait(copies, copies[1:])

        @pl.when(i < nt)
        def _():
            bl, cl, ul, gl, ql, tt, z, sig_g, sig_q, ylru, pos, rowlen = _mix_gates(
                (b_ref, c_ref, u_ref, g_ref, q_ref), hf_ref, hr_ref, wca_ref, perm_ref, t, w)
            do = do_ref[...]
            dya = _dot_nt(do, wo_ref[0:w, :])
            dyb = _dot_nt(do, wo_ref[w:, :])
            sg = gl * sig_g
            dz = dya * bl * sg
            dt = _conv3_t(dz, wca_ref, pos, rowlen)
            dp_ref[:, 0:w] = (dya * z * sg).astype(BF16)
            dp_ref[:, w:2 * w] = (dt * ul).astype(BF16)
            dp_ref[:, 2 * w:3 * w] = (dt * cl).astype(BF16)
            dp_ref[:, 3 * w:4 * w] = (dya * bl * z * (sig_g * (1.0 + gl * (1.0 - sig_g)))).astype(BF16)
            dp_ref[:, 4 * w:5 * w] = jnp.zeros((t, w), BF16)
            dp_ref[:, 5 * w:6 * w] = (dyb * ylru * (sig_q * (1.0 + ql * (1.0 - sig_q)))).astype(BF16)
            dh_ref[...] = _dot(perm_ref[0], (dyb * (ql * sig_q)).astype(BF16)).astype(BF16)
            part_ref[0:1, :] += jnp.sum(dz * _down(tt, 1, pos), axis=0, keepdims=True)
            part_ref[1:2, :] += jnp.sum(dz * tt, axis=0, keepdims=True)
            part_ref[2:3, :] += jnp.sum(dz * _up(tt, 1, pos, rowlen), axis=0, keepdims=True)

    clamp = lambda cols: pl.BlockSpec((t, cols), lambda i: (jnp.minimum(i, nt - 1), 0))
    full = lambda shape: pl.BlockSpec(shape, lambda i: (0,) * len(shape))
    return _call(
        body, name="mix_backward",
        grid=(nt + 1,),
        out_shape=(jax.ShapeDtypeStruct((la, 6 * w), BF16), jax.ShapeDtypeStruct((l, w), BF16),
                   jax.ShapeDtypeStruct((8, w), F32), jax.ShapeDtypeStruct(g_wout.shape, g_wout.dtype)),
        in_specs=[clamp(d)] + _p_specs(t, w, nt) + [clamp(w), clamp(w),
                  pl.BlockSpec((d, d), lambda i: (0, 0), pipeline_mode=pl.Buffered(1)), full(wca.shape),
                  full(perm.shape), ANY],
        out_specs=(pl.BlockSpec((t, 6 * w), lambda i: (i, 0)), clamp(w), full((8, w)), ANY),
        scratch_shapes=[pltpu.SemaphoreType.DMA((NDEV,)), pltpu.SemaphoreType.DMA((NDEV,))],
        compiler_params=_params(("arbitrary",)),
    )(dout, p, p, p, p, p, hf, hr, wo, wca, perm, g_wout)


def _lru_backward(direction, xb, dhs, hs, wg, lv, l, t, conv=None):
    la, w = hs.shape
    gc = wg.shape[2]
    ng = w // gc
    nt = l // t
    nblk8 = la // 8
    last = conv is not None
    assert last == (direction == 1)

    if direction == 0:
        tile = lambda i: jnp.where(i == nt, nt, nt - 1 - i)
        halo = lambda i: jnp.where(tile(i) == 0, nblk8 - 1, tile(i) * (t // 8) - 1)
    else:
        tile = lambda i: i
        halo = lambda i: jnp.minimum((i + 1) * (t // 8), nblk8 - 1)

    def body(*refs):
        x_ref, dh_ref, hs_ref, halo_ref, wg_ref, lv_ref = refs[:6]
        if last:
            v_ref, wcb_ref, tm_ref, bm_ref, dxo_ref = refs[6:11]
        out_ref, dwg_ref, part_ref, a_s, dh_s, g_s, carry = refs[-7:]
        i = pl.program_id(0)
        is_ctx = i == nt

        @pl.when(i == 0)
        def _():
            carry[...] = jnp.zeros_like(carry)
            dwg_ref[...] = jnp.zeros_like(dwg_ref)
            part_ref[...] = jnp.zeros_like(part_ref)

        xb = x_ref[...]
        lam = lv_ref[3 * direction + 2:3 * direction + 3, :]
        a, s, r, ig, sp = _lru_coef(xb, wg_ref, direction, lv_ref[3 * direction:3 * direction + 1, :],
                                    lv_ref[3 * direction + 1:3 * direction + 2, :], lam, gc)
        hs_t = hs_ref[...]
        r8 = _rows((8, w))
        if direction == 0:
            edge = jnp.where(is_ctx, 0.0, halo_ref[7:8, :])
            first = jnp.where(r8 == 0, edge, pltpu.roll(hs_t[t - 8:, :], 1, 0))
            hprev = jnp.concatenate([first, hs_t[:t - 8, :]], axis=0)
        else:
            edge = jnp.where(is_ctx, 0.0, halo_ref[0:1, :])
            final = jnp.where(r8 == 7, edge, pltpu.roll(hs_t[:8, :], 7, 0))
            hprev = jnp.concatenate([hs_t[8:, :], final], axis=0)
        a_s[...] = a
        dh_s[...] = jnp.where(is_ctx, 0.0, dh_ref[...].astype(F32))
        carry[...] = _scan_tile_backward(a_s, dh_s, g_s, carry[...], direction == 0)

        g = g_s[...]
        ix = ig * xb
        gs = g * s
        dla = (g * a) * (hprev - ix * (a / s))
        dxb = gs * ig
        dzr = dla * (r * (1.0 - r)) * (-LRU_C * sp)
        dzi = gs * ix * (1.0 - ig)
        part_ref[0:1, :] += jnp.sum(dzr, axis=0, keepdims=True)
        part_ref[1:2, :] += jnp.sum(dzi, axis=0, keepdims=True)
        part_ref[2:3, :] += jnp.sum(dla * r, axis=0, keepdims=True) * (LRU_C * _sigmoid(-lam))
        pieces = []
        for gi in range(ng):
            sl = slice(gi * gc, (gi + 1) * gc)
            dz = jnp.concatenate([dzr[:, sl], dzi[:, sl]], axis=-1).astype(BF16)
            pieces.append(_dot_nt(dz, wg_ref[direction, gi]))
            dwg_ref[gi] += _dot(xb[:, sl].T.astype(BF16), dz)
        dxb = dxb + (pieces[0] if ng == 1 else jnp.concatenate(pieces, axis=-1))
        if not last:
            out_ref[...] = dxb
        else:
            dxb = dxb + dxo_ref[...]
            dxb16, v16 = dxb.astype(BF16), v_ref[...]
            dv = wcb_ref[0:1, :] * _dot(bm_ref[0], dxb16)
            for j in range(1, 4):
                dv = dv + wcb_ref[j:j + 1, :] * _dot(bm_ref[j], dxb16)
            out_ref[...] = dv.astype(BF16)
            part_ref[3:4, :] += jnp.sum(dxb, axis=0, keepdims=True)
            for j in range(4):
                part_ref[4 + j:5 + j, :] += jnp.sum(dxb * _dot(tm_ref[j], v16), axis=0, keepdims=True)

    full = lambda shape: pl.BlockSpec(shape, lambda i: (0,) * len(shape))
    kind = lambda i: (jnp.where(i == nt, 1, 0), 0, 0, 0)
    in_specs = [pl.BlockSpec((t, w), lambda i: (tile(i), 0)),
                pl.BlockSpec((t, w), lambda i: (jnp.minimum(tile(i), nt - 1), 0)),
                pl.BlockSpec((t, w), lambda i: (tile(i), 0)),
                pl.BlockSpec((8, w), lambda i: (halo(i), 0)),
                full(wg.shape), full(lv.shape)]
    args = [xb, dhs, hs, hs, wg, lv]
    if last:
        p, wcb, taps_m, back_m, dxb_other, dp = conv
        in_specs += [pl.BlockSpec((t, w), lambda i: (tile(i), 4)), full(wcb.shape),
                     pl.BlockSpec((None, 4, t, t), kind), pl.BlockSpec((None, 4, t, t), kind),
                     pl.BlockSpec((t, w), lambda i: (tile(i), 0)), ANY]
        args += [p, wcb, taps_m, back_m, dxb_other, dp]
        out0 = jax.ShapeDtypeStruct(dp.shape, dp.dtype)
        spec0 = pl.BlockSpec((t, w), lambda i: (tile(i), 4))
        aliases = {11: 0}
    else:
        out0 = jax.ShapeDtypeStruct((la, w), F32)
        spec0 = pl.BlockSpec((t, w), lambda i: (tile(i), 0))
        aliases = {}
    return _call(
        body, name="lru_backward_%d" % direction,
        grid=(nt + 1,),
        out_shape=(out0, jax.ShapeDtypeStruct((ng, gc, 2 * gc), F32), jax.ShapeDtypeStruct((8, w), F32)),
        in_specs=in_specs,
        out_specs=(spec0, full((ng, gc, 2 * gc)), full((8, w))),
        scratch_shapes=[pltpu.VMEM((t, w), F32), pltpu.VMEM((t, w), F32), pltpu.VMEM((t, w), F32),
                        pltpu.VMEM((8, w), F32)],
        input_output_aliases=aliases,
        compiler_params=_params(("arbitrary",)),
    )(*args)


def _weight_grad_t(at, b, nblk_m, nblk_n, tk, name):
    m, k = at.shape
    n = b.shape[1]
    bm, bn = m // nblk_m, n // nblk_n
    nk = k // tk

    def body(a_ref, b_ref, o_ref, acc):
        kk = pl.program_id(2)

        @pl.when(kk == 0)
        def _():
            acc[...] = jnp.zeros_like(acc)

        acc[...] += _dot(a_ref[...], b_ref[...])

        @pl.when(kk == nk - 1)
        def _():
            o_ref[...] = acc[...].astype(BF16)

    return _call(
        body, name=name,
        grid=(nblk_m, nblk_n, nk),
        out_shape=jax.ShapeDtypeStruct((nblk_m * nblk_n, bm, bn), BF16),
        in_specs=[pl.BlockSpec((bm, tk), lambda i, j, kk: (i, kk)),
                  pl.BlockSpec((tk, bn), lambda i, j, kk: (kk, j))],
        out_specs=pl.BlockSpec((None, bm, bn), lambda i, j, kk: (i * nblk_n + j, 0, 0)),
        scratch_shapes=[pltpu.VMEM((bm, bn), F32)],
        compiler_params=_params(("arbitrary", "arbitrary", "arbitrary")),
    )(at, b)


def _weight_grad_scatter(at, b, tk, name):
    m, k = at.shape
    n = b.shape[1]
    bn = n // NDEV
    nk = k // tk
    where = jnp.stack([_idx(_my_pos()), lax.axis_index("c")]).astype(jnp.int32)

    def body(w_ref, a_ref, b_ref, recv_ref, acc, sbuf, sib, sib_send, sib_recv, chip_send, chip_recv, keep_sem):
        s, kk = pl.program_id(0), pl.program_id(1)
        x, y, c = _my_pos()

        @pl.when(kk == 0)
        def _():
            acc[...] = _dot(a_ref[...], b_ref[...])

        @pl.when(kk > 0)
        def _():
            acc[...] += _dot(a_ref[...], b_ref[...])

        def to_sibling(j):
            return pltpu.make_async_remote_copy(
                src_ref=sbuf.at[0], dst_ref=sib.at[j], send_sem=sib_send.at[j], recv_sem=sib_recv.at[j],
                device_id=(x, y, 1 - c), device_id_type=MESH)

        def to_chip(j):
            dist = _chip_order(j, c)
            return pltpu.make_async_remote_copy(
                src_ref=sbuf.at[1], dst_ref=recv_ref.at[dist // 2], send_sem=chip_send.at[j],
                recv_sem=chip_recv.at[dist // 2], device_id=_peer_at(dist), device_id_type=MESH)

        keep = pltpu.make_async_copy(sbuf.at[1], recv_ref.at[0], keep_sem)
        sends = []
        for j in range(4):
            sends += [to_sibling(j), to_chip(j) if j < 3 else keep]

        for st in range(NDEV):
            @pl.when((kk == nk - 1) & (s == st))
            def _(st=st):
                if st >= 2:
                    sends[st - 2].wait_send()
                part = acc[...]
                if st % 2 == 1:
                    to_sibling(st // 2).wait_recv()
                    part = part + sib[st // 2].astype(F32)
                sbuf[st % 2] = part.astype(BF16)
                sends[st].start()
                if st == NDEV - 1:
                    sends[st - 1].wait_send()
                    sends[st].wait()
                    for j in range(1, 4):
                        pltpu.make_async_remote_copy(
                            src_ref=sbuf.at[0], dst_ref=recv_ref.at[j], send_sem=chip_send.at[0],
                            recv_sem=chip_recv.at[j], device_id=_peer_at(2 * j), device_id_type=MESH).wait_recv()

    blk = lambda s, w_ref: w_ref[0] ^ _scatter_order(s, w_ref[1])
    return _call(
        body, name=name,
        grid_spec=pltpu.PrefetchScalarGridSpec(
            num_scalar_prefetch=1, grid=(NDEV, nk),
            in_specs=[pl.BlockSpec((m, tk), lambda s, kk, w_ref: (0, kk)),
                      pl.BlockSpec((tk, bn), lambda s, kk, w_ref: (kk, blk(s, w_ref)))],
            out_specs=ANY,
            scratch_shapes=[pltpu.VMEM((m, bn), F32), pltpu.VMEM((2, m, bn), BF16), pltpu.VMEM((4, m, bn), BF16),
                            pltpu.SemaphoreType.DMA((4,)), pltpu.SemaphoreType.DMA((4,)),
                            pltpu.SemaphoreType.DMA((4,)), pltpu.SemaphoreType.DMA((4,)),
                            pltpu.SemaphoreType.DMA]),
        out_shape=jax.ShapeDtypeStruct((4, m, bn), BF16),
        compiler_params=_params(("arbitrary", "arbitrary")),
    )(where, at, b)


def _input_backward(dp, w_all, src, mv, row0, tm, nbk, name, dn=None):
    rows, d = src.shape
    nb, _, bw = w_all.shape
    nk = nb // nbk
    blk0 = row0 // tm
    latent = dn is not None

    def body(*refs):
        dp_ref, w_ref, x_ref, mv_ref = refs[:4]
        outs = refs[4 + latent:]
        part_ref, acc = outs[latent], outs[latent + 1]
        i, k = pl.program_id(0), pl.program_id(1)

        @pl.when((i == 0) & (k == 0))
        def _():
            part_ref[...] = jnp.zeros_like(part_ref)

        step = _dot_nt(dp_ref[:, 0:bw], w_ref[0])
        for q in range(1, nbk):
            step = step + _dot_nt(dp_ref[:, q * bw:(q + 1) * bw], w_ref[q])

        @pl.when(k == 0)
        def _():
            acc[...] = step

        @pl.when(k > 0)
        def _():
            acc[...] += step

        @pl.when(k == nk - 1)
        def _():
            xf = x_ref[...]
            r = lax.rsqrt(jnp.mean(xf * xf, axis=-1, keepdims=True) + EPS)
            xn = xf * r
            dhl = acc[...]
            gain, sc = mv_ref[0:1, :], mv_ref[1:2, :]
            dhx = jnp.sum(dhl * xn, axis=0, keepdims=True)
            part_ref[0:1, :] += jnp.sum(dhl, axis=0, keepdims=True)
            part_ref[1:2, :] += dhx * gain
            part_ref[2:3, :] += dhx * (1.0 + sc)
            if latent:
                dxn = dhl * (gain * (1.0 + sc))
                outs[0][...] = refs[4][...] + r * (dxn - xn * jnp.mean(dxn * xn, axis=-1, keepdims=True))

    tile = pl.BlockSpec((tm, d), lambda i, k: (i, 0))
    vec = pl.BlockSpec((8, d), lambda i, k: (0, 0))
    return _call(
        body, name=name,
        grid=(rows // tm, nk),
        out_shape=((jax.ShapeDtypeStruct((rows, d), F32),) if latent else ()) + (jax.ShapeDtypeStruct((8, d), F32),),
        in_specs=[pl.BlockSpec((tm, nbk * bw), lambda i, k: (blk0 + i, k)),
                  pl.BlockSpec((nbk, d, bw), lambda i, k: (k, 0, 0)), tile, vec] + ([tile] if latent else []),
        out_specs=((tile,) if latent else ()) + (vec,),
        scratch_shapes=[pltpu.VMEM((tm, d), F32)],
        compiler_params=_params(("arbitrary", "arbitrary")),
    )(*([dp, w_all, src, mv] + ([dn] if latent else [])))


def _adamw_scattered(parts, w, m, v, tr):
    r, c = w.shape
    nslot = parts.shape[0]

    def body(p_ref, w_ref, m_ref, v_ref, g_ref, d_ref, m2_ref, v2_ref):
        g = p_ref[0].astype(F32)
        for k in range(1, nslot):
            g = g + p_ref[k].astype(F32)
        g_ref[...] = g
        d_ref[...], m2_ref[...], v2_ref[...] = _adamw(w_ref[...], g, m_ref[...], v_ref[...])

    tile = pl.BlockSpec((tr, c), lambda i: (i, 0))
    return _call(
        body, name="adamw_scattered_%dx%d" % (r, c),
        grid=(r // tr,),
        out_shape=tuple(jax.ShapeDtypeStruct((r, c), F32) for _ in range(4)),
        in_specs=[pl.BlockSpec((nslot, tr, c), lambda i: (0, i, 0)), tile, tile, tile],
        out_specs=(tile,) * 4,
        compiler_params=_params(("arbitrary",)),
    )(parts, w, m, v)


def _adamw_ada(st, dmod, w, m, v, tr):
    r, c = w.shape

    def body(s_ref, dm_ref, w_ref, m_ref, v_ref, g_ref, d_ref, m2_ref, v2_ref):
        g = jnp.dot(s_ref[...], dm_ref[...], precision=HIGHEST, preferred_element_type=F32)
        g_ref[...] = g
        d_ref[...], m2_ref[...], v2_ref[...] = _adamw(w_ref[...], g, m_ref[...], v_ref[...])

    tile = pl.BlockSpec((tr, c), lambda i: (i, 0))
    return _call(
        body, name="adamw_ada",
        grid=(r // tr,),
        out_shape=tuple(jax.ShapeDtypeStruct((r, c), F32) for _ in range(4)),
        in_specs=[pl.BlockSpec((tr, 16), lambda i: (i, 0)), pl.BlockSpec((16, c), lambda i: (0, 0)),
                  tile, tile, tile],
        out_specs=(tile,) * 4,
        compiler_params=_params(("arbitrary",)),
    )(st, dmod, w, m, v)


def _adamw_packed(g, w, m, v):
    def body(g_ref, w_ref, m_ref, v_ref, d_ref, m2_ref, v2_ref):
        d_ref[...], m2_ref[...], v2_ref[...] = _adamw(w_ref[...], g_ref[...], m_ref[...], v_ref[...])

    return _call(
        body, name="adamw_packed",
        out_shape=tuple(jax.ShapeDtypeStruct(w.shape, F32) for _ in range(3)),
        in_specs=[VMEM] * 4, out_specs=(VMEM,) * 3,
        compiler_params=_params(),
    )(g, w, m, v)


def _blockdiag_groups(wh, gc):
    h, dh, _ = wh.shape
    g = gc // dh
    w4 = wh.reshape(h // g, g, dh, dh)
    bd = jnp.einsum("ngij,gh->ngihj", w4, jnp.eye(g, dtype=wh.dtype))
    return bd.reshape(h // g, gc, gc)


def _blockdiag_extract(bd, dh):
    ng, gc, _ = bd.shape
    g = gc // dh
    x = bd.reshape(ng, g, dh, g, dh)
    return jnp.einsum("ngihj,gh->ngij", x, jnp.eye(g, dtype=bd.dtype)).reshape(ng * g, dh, dh)


def _rows8(*vecs):
    rows = [jnp.reshape(v, (1, -1)).astype(F32) for v in vecs]
    n = rows[0].shape[1]
    return jnp.concatenate(rows + [jnp.zeros((8 - len(rows), n), F32)], axis=0)


def _pack(pieces):
    flat = jnp.concatenate([jnp.reshape(a, (-1,)).astype(F32) for a in pieces])
    total = -(-flat.shape[0] // 1024) * 1024
    return jnp.pad(flat, (0, total - flat.shape[0])).reshape(total // 128, 128)


def _unpack(packed, shapes):
    flat = packed.reshape(-1)
    out, off = [], 0
    for s in shapes:
        n = 1
        for q in s:
            n *= q
        out.append(flat[off:off + n].reshape(s))
        off += n
    return out


def kernel(x, c, ctx, c_ctx, norm_g, w_ada, b_ada, w_in, w_conv_a, w_conv_b, b_conv_b, lru_wa, lru_ba, lru_wx, lru_bx, lru_lambda, w_out, final_g, loss_target, m_c_ctx, m_norm_g, m_w_ada, m_b_ada, m_w_in, m_w_conv_a, m_w_conv_b, m_b_conv_b, m_lru_wa, m_lru_ba, m_lru_wx, m_lru_bx, m_lru_lambda, m_w_out, m_final_g, v_c_ctx, v_norm_g, v_w_ada, v_b_ada, v_w_in, v_w_conv_a, v_w_conv_b, v_b_conv_b, v_lru_wa, v_lru_ba, v_lru_wx, v_lru_bx, v_lru_lambda, v_w_out, v_final_g):
    _, l, d = x.shape
    lc = ctx.shape[1]
    w = d // 2
    t = lc
    assert l % t == 0 and t % GRID_W == 0 and t % 128 == 0
    dh = w // N_HEADS
    gc = min(w, MXU_WIDTH)
    cols = w_ada.shape[2]
    wo_rows = w_out.shape[1]
    me = _idx(_my_pos())
    x2, ctx2, tgt2 = x[0], ctx[0], loss_target[0]
    w_ada2, w_in2, w_out2 = w_ada[0], w_in[0], w_out[0]

    small_mine = jnp.concatenate([w_conv_a[0], w_conv_b[0], lru_ba[0], lru_bx[0], lru_lambda[0],
                                  jnp.zeros((3, w // NDEV), F32)], axis=0)
    mod_all, s_mat, small_all = _mod_forward(
        jnp.broadcast_to(c, (8, d)), jnp.broadcast_to(c_ctx[None], (8, d)), w_ada2, small_mine)
    mod = jnp.transpose(mod_all, (1, 0, 2)).reshape(16, NDEV * cols) + b_ada
    mod_lat = lax.dynamic_slice_in_dim(mod, me, 1, axis=0)
    sh_l, sc_l, gt_l = jnp.split(mod_lat, 3, axis=-1)
    sh_c, sc_c, _ = jnp.split(mod[8:9], 3, axis=-1)
    small = jnp.transpose(small_all, (1, 0, 2)).reshape(16, w)
    wca = _rows8(*[small[j] for j in range(0, 3)])
    wcb = _rows8(*[small[j] for j in range(3, 7)], b_conv_b)
    lv = _rows8(small[7], small[9], small[11], small[8], small[10], small[12])
    wg = jnp.stack([
        jnp.concatenate([_blockdiag_groups(lru_wa[0, dr], gc), _blockdiag_groups(lru_wx[0, dr], gc)], axis=-1)
        for dr in range(2)]).astype(BF16)

    la = l + lc
    tm = 2 * t if l % (2 * t) == 0 else t
    tk = 3 * t if la % (3 * t) == 0 else t
    h, hlt = _normalize(x2, _rows8(norm_g, sc_l, sh_l), la, 0, tm, "normalize")
    h, hlt = _normalize(ctx2, _rows8(norm_g, sc_c, sh_c), la, l, t, "normalize_ctx", prev=(h, hlt))
    p, w_all = _in_projection(h, w_in2.astype(BF16), tk)
    taps_m, back_m, perm = _scan_matrices(t)
    xb = _conv_input(p, wcb, taps_m, l, t)
    hf, hr, wo_all = _lru_forward(xb, wg, lv, w_out2.astype(BF16), l, t)
    wo = wo_all.reshape(d, d)
    dn, catt, dout, part_mix = _mix_forward(x2, tgt2, p, hf, hr, wo, _rows8(gt_l, final_g), wca, perm, t)
    g_wout = _weight_grad_t(catt, dout, 2, 1, 4 * t if l % (4 * t) == 0 else t, "grad_w_out")
    dp, dhs, part_ca, sc_wout = _mix_backward(dout, p, hf, hr, wo, wca, perm, g_wout.reshape(NDEV, wo_rows, d), l, t)
    dxb0, dwg0, part_l0 = _lru_backward(0, xb, dhs, hf, wg, lv, l, t)
    dp, dwg1, part_l1 = _lru_backward(1, xb, dhs, hr, wg, lv, l, t, conv=(p, wcb, taps_m, back_m, dxb0, dp))
    sc_win = _weight_grad_scatter(hlt, dp, tk, "grad_w_in")
    grad_x, part_lat = _input_backward(dp, w_all, x2, _rows8(norm_g, sc_l), 0, tm, 2, "input_backward", dn=dn)
    (part_ctx,) = _input_backward(dp, w_all, ctx2, _rows8(norm_g, sc_c), l, t, 2, "input_backward_ctx")
    part_in = jnp.concatenate([part_lat[0:2], part_ctx[0:2], (part_lat[2] + part_ctx[2])[None]], axis=0)

    dwa = jnp.stack([_blockdiag_extract(dwg0[:, :, :gc], dh), _blockdiag_extract(dwg1[:, :, :gc], dh)])
    dwx = jnp.stack([_blockdiag_extract(dwg0[:, :, gc:], dh), _blockdiag_extract(dwg1[:, :, gc:], dh)])
    lru_part = jnp.stack([dwa, dwx]).reshape(NDEV, -1, 128)
    zeros_d = jnp.zeros((d,), F32)
    pieces = [
        jnp.concatenate([part_in[0], part_in[1], part_mix[1]]),
        jnp.concatenate([part_in[2], part_in[3], zeros_d]),
        part_in[4], part_mix[0], part_ca[0:3], part_l1[4:8], part_l1[3],
        jnp.stack([part_l0[0], part_l1[0]]), jnp.stack([part_l0[1], part_l1[1]]),
        jnp.stack([part_l0[2], part_l1[2]]), part_mix[2, 0:1],
    ]
    shapes = [(3 * d,), (3 * d,), (d,), (d,), (3, w), (4, w), (w,), (2, w), (2, w), (2, w), (1,)]
    sig_cc = jax.nn.sigmoid(c_ctx)
    dsilu_cc = jnp.broadcast_to((sig_cc * (1.0 + c_ctx * (1.0 - sig_cc)))[None], (8, d))
    psum, pall, lru_sum, g_cctx8 = _reduce_small(_pack(pieces), lru_part, w_ada2, dsilu_cc)
    (g_modl, g_modc, g_norm, g_final, g_ca, g_cb, g_bcb, g_ba, g_bx, g_lam, loss1) = _unpack(psum, shapes)
    loss = loss1[0]
    g_cctx = g_cctx8[0]
    g_bada = (g_modl + g_modc)[None]
    g_lru = lru_sum.reshape(2, 2, N_HEADS, dh, dh)
    g_wa, g_wx = g_lru[0][None], g_lru[1][None]
    wsl = w // NDEV
    mine = lambda a: lax.dynamic_slice_in_dim(a, me * wsl, wsl, axis=-1)
    g_ca_m, g_cb_m, g_ba_m, g_bx_m, g_lam_m = (mine(g_ca)[None], mine(g_cb)[None], mine(g_ba)[None],
                                               mine(g_bx)[None], mine(g_lam)[None])
    g_norm, g_bcb = g_norm[None], g_bcb[None]

    cb = cols // 128
    per_dev = pall[:, :3 * d // 128].reshape(NDEV, NDEV, cols)
    dmod_lat = lax.dynamic_slice_in_dim(per_dev, me, 1, axis=1)[:, 0]
    dmod_ctx = lax.dynamic_slice_in_dim(g_modc.reshape(NDEV, cols), me, 1, axis=0)
    dmod16 = jnp.concatenate([dmod_lat, dmod_ctx, jnp.zeros((7, cols), F32)], axis=0)
    tr_ada = 256 if d % 256 == 0 else d
    g_wada, d_wada, m_wada, v_wada = _adamw_ada(s_mat.T, dmod16, w_ada2, m_w_ada[0], v_w_ada[0], tr_ada)
    g_win2, d_win, m_win, v_win = _adamw_scattered(sc_win, w_in2, m_w_in[0], v_w_in[0], tr_ada)
    tr_out = 64 if wo_rows % 64 == 0 else wo_rows
    g_wout2, d_wout, m_wout, v_wout = _adamw_scattered(sc_wout, w_out2, m_w_out[0], v_w_out[0], tr_out)

    small_w = [c_ctx, norm_g, b_ada, w_conv_a, w_conv_b, b_conv_b, lru_wa, lru_ba, lru_wx, lru_bx, lru_lambda, final_g]
    small_m = [m_c_ctx, m_norm_g, m_b_ada, m_w_conv_a, m_w_conv_b, m_b_conv_b, m_lru_wa, m_lru_ba, m_lru_wx,
               m_lru_bx, m_lru_lambda, m_final_g]
    small_v = [v_c_ctx, v_norm_g, v_b_ada, v_w_conv_a, v_w_conv_b, v_b_conv_b, v_lru_wa, v_lru_ba, v_lru_wx,
               v_lru_bx, v_lru_lambda, v_final_g]
    small_g = [g_cctx, g_norm, g_bada, g_ca_m, g_cb_m, g_bcb, g_wa, g_ba_m, g_wx, g_bx_m, g_lam_m, g_final]
    sshapes = [a.shape for a in small_w]
    d_s, m_s, v_s = _adamw_packed(_pack(small_g), _pack(small_w), _pack(small_m), _pack(small_v))
    d_s, m_s, v_s = _unpack(d_s, sshapes), _unpack(m_s, sshapes), _unpack(v_s, sshapes)
    small_g = [jnp.reshape(a, s) for a, s in zip(small_g, sshapes)]

    def weights(small_list, ada, win, wout):
        (cctx_, norm_, bada_, ca_, cb_, bcb_, wa_, ba_, wx_, bx_, lam_, final_) = small_list
        return [cctx_, norm_, ada[None], bada_, win[None], ca_, cb_, bcb_, wa_, ba_, wx_, bx_, lam_, wout[None], final_]

    return (loss, grad_x[None],
            *weights(small_g, g_wada, g_win2, g_wout2), *weights(d_s, d_wada, d_win, d_wout),
            *weights(m_s, m_wada, m_win, m_wout), *weights(v_s, v_wada, v_win, v_wout))
```

```python
import functools

import jax
import jax.numpy as jnp
import numpy as np
from jax import lax
from jax.experimental import pallas as pl
from jax.experimental.pallas import tpu as pltpu

F32 = jnp.float32
BF16 = jnp.bfloat16
MESH = pl.DeviceIdType.MESH
NDEV = 8
GRID_W = 64
N_HEADS = 16
LRU_C = 8.0
EPS = 1e-6
MXU_WIDTH = 256
VMEM_LIMIT = 60 * 1024 * 1024

ADAM_LR = 0.001
ADAM_B1 = 0.9
ADAM_B2 = 0.999
ADAM_EPS = 1e-08
ADAM_WD = 0.01
ADAM_STEP = 10
ADAM_C1 = 1.0 - ADAM_B1 ** ADAM_STEP
ADAM_C2 = 1.0 - ADAM_B2 ** ADAM_STEP

HIGHEST = lax.Precision.HIGHEST
ANY = pl.BlockSpec(memory_space=pl.ANY)
VMEM = pl.BlockSpec(memory_space=pltpu.VMEM)


def _call(body, **kw):
    return pl.pallas_call(body, **kw)


def _params(sem=None, vmem=VMEM_LIMIT):
    return pltpu.CompilerParams(dimension_semantics=sem, vmem_limit_bytes=vmem)


def _my_pos():
    return lax.axis_index("x"), lax.axis_index("y"), lax.axis_index("c")


def _idx(pos):
    return 4 * pos[0] + 2 * pos[1] + pos[2]


def _peer(k):
    x, y, c = _my_pos()
    return ((1 - x) if (k >> 2) & 1 else x, (1 - y) if (k >> 1) & 1 else y, (1 - c) if k & 1 else c)


def _exchange_vmem(src_ref, dst_ref, send_sems, recv_sems, base):
    me = _idx(_my_pos())
    sends = []
    for k in range(1, NDEV):
        cp = pltpu.make_async_remote_copy(
            src_ref=src_ref, dst_ref=dst_ref.at[me], send_sem=send_sems.at[base + k - 1],
            recv_sem=recv_sems.at[base + k - 1], device_id=_peer(k), device_id_type=MESH)
        cp.start()
        sends.append(cp)
    dst_ref[me] = src_ref[...]
    for k in range(1, NDEV):
        peer = _peer(k)
        pltpu.make_async_remote_copy(
            src_ref=src_ref, dst_ref=dst_ref.at[_idx(peer)], send_sem=send_sems.at[base + k - 1],
            recv_sem=recv_sems.at[base + k - 1], device_id=peer, device_id_type=MESH).wait_recv()
    for cp in sends:
        cp.wait_send()


def _sigmoid(z):
    return 0.5 * jnp.tanh(0.5 * z) + 0.5


def _softplus(x):
    return jnp.maximum(x, 0.0) + jnp.log1p(jnp.exp(-jnp.abs(x)))


def _one_minus_sq(a, la):
    series = (-2.0 * la) * (1.0 + la)
    return jnp.where(la > -0.0015, series, 1.0 - a * a)


def _dot(a, b):
    return jnp.dot(a, b, preferred_element_type=F32)


def _dot_nt(a, b):
    return lax.dot_general(a, b, (((1,), (1,)), ((), ())), preferred_element_type=F32)


def _rows(shape):
    return lax.broadcasted_iota(jnp.int32, shape, 0)


def _down(x, k, pos):
    return jnp.where(pos >= k, pltpu.roll(x, k, 0), 0.0)


def _up(x, k, pos, rowlen):
    return jnp.where(pos + k < rowlen, pltpu.roll(x, x.shape[0] - k, 0), 0.0)


def _pos_rowlen(shape, is_ctx):
    t = _rows(shape)
    pos = jnp.where(is_ctx, t, t & (GRID_W - 1))
    rowlen = jnp.where(is_ctx, shape[0], GRID_W)
    return pos, rowlen


def _scan_matrices(t):
    seg = t // 8
    r = np.arange(t)
    perm = (np.arange(t)[None, :] == ((r % 8) * seg + r // 8)[:, None]).astype(np.float32)
    rows, cols = r[:, None], r[None, :]
    taps, back = [], []
    for rowlen in (GRID_W, t):
        pos = rows % rowlen
        shift = {-2: (cols == rows - 2) & (pos >= 2), -1: (cols == rows - 1) & (pos >= 1),
                 0: cols == rows, 1: (cols == rows + 1) & (pos + 1 < rowlen),
                 2: (cols == rows + 2) & (pos + 2 < rowlen)}
        taps.append(np.stack([perm @ shift[k].astype(np.float32) for k in (-2, -1, 0, 1)]))
        back.append(np.stack([shift[k].astype(np.float32) @ perm.T for k in (2, 1, 0, -1)]))
    as_bf16 = lambda a: jnp.asarray(a, dtype=BF16)
    return as_bf16(np.stack(taps)), as_bf16(np.stack(back)), as_bf16(np.stack([perm, perm.T]))


def _conv3(t, w_ref, pos, rowlen):
    return w_ref[0:1, :] * _down(t, 1, pos) + w_ref[1:2, :] * t + w_ref[2:3, :] * _up(t, 1, pos, rowlen)


def _conv3_t(dz, w_ref, pos, rowlen):
    return w_ref[0:1, :] * _up(dz, 1, pos, rowlen) + w_ref[1:2, :] * dz + w_ref[2:3, :] * _down(dz, 1, pos)


def _chunk_scan(a, b, reverse):
    row = _rows(a.shape)
    for s in (1, 2, 4):
        if reverse:
            m = row < 8 - s
            sh = 8 - s
        else:
            m = row >= s
            sh = s
        a_s = jnp.where(m, pltpu.roll(a, sh, 0), 1.0)
        b_s = jnp.where(m, pltpu.roll(b, sh, 0), 0.0)
        b = b + a * b_s
        a = a * a_s
    return a, b


def _chain_segments(ptot, hend, carry, reverse):
    ca, cb = _chunk_scan(ptot, hend, reverse)
    incl = ca * carry + cb
    r8 = _rows(incl.shape)
    if reverse:
        start = jnp.where(r8 < 7, pltpu.roll(incl, 7, 0), carry)
        last = incl[0:1, :]
    else:
        start = jnp.where(r8 >= 1, pltpu.roll(incl, 1, 0), carry)
        last = incl[7:8, :]
    return start, jnp.broadcast_to(last, incl.shape)


def _blocks(nblock, reverse):
    order = range(nblock - 1, -1, -1) if reverse else range(nblock)
    return [slice(8 * k, 8 * k + 8) for k in order]


def _scan_tile(a_ref, b_ref, out_ref, carry, reverse):
    t, w = a_ref.shape
    seg = t // 8

    hend, ptot = jnp.zeros((8, w), F32), jnp.ones((8, w), F32)
    for rows in _blocks(seg, reverse):
        a = a_ref[rows, :]
        hend, ptot = a * hend + b_ref[rows, :], a * ptot
    h, new_carry = _chain_segments(ptot, hend, carry, reverse)
    for rows in _blocks(seg, reverse):
        h = a_ref[rows, :] * h + b_ref[rows, :]
        out_ref[rows, :] = h
    return new_carry


def _scan_tile_backward(a_ref, dh_ref, g_ref, carry, reverse):
    t, w = a_ref.shape
    seg = t // 8

    uend, ptot = jnp.zeros((8, w), F32), jnp.ones((8, w), F32)
    for rows in _blocks(seg, reverse):
        a = a_ref[rows, :]
        uend, ptot = a * (dh_ref[rows, :] + uend), a * ptot
    u, new_carry = _chain_segments(ptot, uend, carry, reverse)
    for rows in _blocks(seg, reverse):
        g = dh_ref[rows, :] + u
        g_ref[rows, :] = g
        u = a_ref[rows, :] * g
    return new_carry


def _lru_coef(xb, wg_ref, d, ba, bx, lam, gc):
    w = xb.shape[1]
    xb16 = xb.astype(BF16)
    zr, zi = [], []
    for g in range(w // gc):
        z = _dot(xb16[:, g * gc:(g + 1) * gc], wg_ref[d, g])
        zr.append(z[:, :gc])
        zi.append(z[:, gc:])
    zr = zr[0] if len(zr) == 1 else jnp.concatenate(zr, axis=-1)
    zi = zi[0] if len(zi) == 1 else jnp.concatenate(zi, axis=-1)
    r = _sigmoid(zr + ba)
    ig = _sigmoid(zi + bx)
    sp = _softplus(-lam)
    la = r * (-LRU_C * sp)
    a = jnp.exp(la)
    s = jnp.sqrt(_one_minus_sq(a, la))
    return a, s, r, ig, sp


def _adamw(w, g, m, v):
    m2 = ADAM_B1 * m + (1.0 - ADAM_B1) * g
    v2 = ADAM_B2 * v + (1.0 - ADAM_B2) * (g * g)
    m_hat = m2 / ADAM_C1
    v_hat = v2 / ADAM_C2
    delta = -ADAM_LR * (m_hat / (jnp.sqrt(v_hat) + ADAM_EPS) + ADAM_WD * w)
    return delta, m2, v2


def _mod_forward(c8, cctx8, w_ada, small):
    d = c8.shape[1]
    cols = w_ada.shape[1]

    def body(c_ref, cctx_ref, w_ref, sm_ref, mod_ref, s_ref, sm_all, cbuf, mod_my, send_sems, recv_sems):
        _exchange_vmem(sm_ref, sm_all, send_sems, recv_sems, 2 * (NDEV - 1))
        _exchange_vmem(c_ref, cbuf, send_sems, recv_sems, 0)
        row = _rows((8, d))
        c_all = jnp.zeros((8, d), F32)
        for b in range(NDEV):
            c_all = jnp.where(row == b, cbuf[b], c_all)
        cc = cctx_ref[...]
        s_top = c_all * _sigmoid(c_all)
        s_bot = jnp.where(row == 0, cc * _sigmoid(cc), 0.0)
        s = jnp.concatenate([s_top, s_bot], axis=0)
        s_ref[...] = s
        mod_my[...] = jnp.dot(s, w_ref[...], precision=HIGHEST, preferred_element_type=F32)
        _exchange_vmem(mod_my, mod_ref, send_sems, recv_sems, NDEV - 1)

    return _call(
        body, name="mod_forward",
        out_shape=(jax.ShapeDtypeStruct((NDEV, 16, cols), F32), jax.ShapeDtypeStruct((16, d), F32),
                   jax.ShapeDtypeStruct((NDEV,) + small.shape, F32)),
        in_specs=[VMEM] * 4, out_specs=(VMEM,) * 3,
        scratch_shapes=[pltpu.VMEM((NDEV, 8, d), F32), pltpu.VMEM((16, cols), F32),
                        pltpu.SemaphoreType.DMA((3 * (NDEV - 1),)), pltpu.SemaphoreType.DMA((3 * (NDEV - 1),))],
        compiler_params=_params(),
    )(c8, cctx8, w_ada, small)


def _scatter_copies(src_ref, dst_ref, send_sems, recv_sems):
    me = _idx(_my_pos())
    copies = [pltpu.make_async_copy(src_ref.at[me], dst_ref.at[0], send_sems.at[0])]
    for k in range(1, NDEV):
        peer = _peer(k)
        copies.append(pltpu.make_async_remote_copy(
            src_ref=src_ref.at[_idx(peer)], dst_ref=dst_ref.at[k], send_sem=send_sems.at[k],
            recv_sem=recv_sems.at[k], device_id=peer, device_id_type=MESH))
    return copies


def _gather_copies(src_ref, dst_ref, send_sems, recv_sems):
    me = _idx(_my_pos())
    sends = [pltpu.make_async_copy(src_ref, dst_ref.at[me], send_sems.at[0])]
    arrivals = []
    for k in range(1, NDEV):
        peer = _peer(k)
        sends.append(pltpu.make_async_remote_copy(
            src_ref=src_ref, dst_ref=dst_ref.at[me], send_sem=send_sems.at[k],
            recv_sem=recv_sems.at[k], device_id=peer, device_id_type=MESH))
        arrivals.append(pltpu.make_async_remote_copy(
            src_ref=src_ref, dst_ref=dst_ref.at[_idx(peer)], send_sem=send_sems.at[k],
            recv_sem=recv_sems.at[k], device_id=peer, device_id_type=MESH))
    return sends, arrivals


def _exchange_wait(sends, arrivals):
    sends[0].wait()
    for cp in arrivals:
        cp.wait_recv()
    for cp in sends[1:]:
        cp.wait_send()


def _chip_order(k, c):
    return (6, 4 - 2 * c, 2 + 2 * c, 0)[k]


def _scatter_order(s, c):
    k = s >> 1
    mine = jnp.where(k == 0, 6, jnp.where(k == 1, 4 - 2 * c, jnp.where(k == 2, 2 + 2 * c, 0)))
    theirs = jnp.where(k == 0, 6, jnp.where(k == 1, 2 + 2 * c, jnp.where(k == 2, 4 - 2 * c, 0))) ^ 1
    return jnp.where((s & 1) == 0, theirs, mine)


def _peer_at(dist):
    x, y, c = _my_pos()
    return (x ^ ((dist >> 2) & 1), y ^ ((dist >> 1) & 1), c ^ (dist & 1))


def _reduce_small(packed, lru_parts, w_ada, dsilu_cctx):
    rp = packed.shape[0]
    rl = lru_parts.shape[1]
    d, cols = w_ada.shape
    assert cols % 128 == 0
    cb = cols // 128

    def body(p_ref, l_ref, w_ref, ds_ref, sum_ref, all_ref, lru_ref, cctx_ref,
             lbuf, lsum, cpart, call, send_sems, recv_sems, lsend, lrecv):
        me = _idx(_my_pos())
        scattered = _scatter_copies(l_ref, lbuf, lsend, lrecv)
        for cp in scattered:
            cp.start()
        _exchange_vmem(p_ref, all_ref, send_sems, recv_sems, 0)
        acc = all_ref[0]
        for j in range(1, NDEV):
            acc = acc + all_ref[j]
        sum_ref[...] = acc
        _exchange_wait(scattered, scattered[1:])
        red = lbuf[0]
        for k in range(1, NDEV):
            red = red + lbuf[k]
        lsum[...] = red
        _exchange_vmem(lsum, lru_ref, send_sems, recv_sems, NDEV - 1)
        part = jnp.zeros((8, d), F32)
        for q in range(cb):
            dm = jnp.broadcast_to(sum_ref[pl.ds((NDEV + me) * cb + q, 1), :], (8, 128))
            part = part + lax.dot_general(dm, w_ref[:, q * 128:(q + 1) * 128],
                                          (((1,), (1,)), ((), ())), precision=HIGHEST,
                                          preferred_element_type=F32)
        cpart[...] = part
        _exchange_vmem(cpart, call, send_sems, recv_sems, 2 * (NDEV - 1))
        tot = call[0]
        for j in range(1, NDEV):
            tot = tot + call[j]
        cctx_ref[...] = tot * ds_ref[...]

    return _call(
        body, name="reduce_small",
        out_shape=(jax.ShapeDtypeStruct((rp, 128), F32), jax.ShapeDtypeStruct((NDEV, rp, 128), F32),
                   jax.ShapeDtypeStruct((NDEV, rl, 128), F32), jax.ShapeDtypeStruct((8, d), F32)),
        in_specs=[VMEM] * 4, out_specs=(VMEM,) * 4,
        scratch_shapes=[pltpu.VMEM((NDEV, rl, 128), F32), pltpu.VMEM((rl, 128), F32), pltpu.VMEM((8, d), F32),
                        pltpu.VMEM((NDEV, 8, d), F32),
                        pltpu.SemaphoreType.DMA((3 * (NDEV - 1),)), pltpu.SemaphoreType.DMA((3 * (NDEV - 1),)),
                        pltpu.SemaphoreType.DMA((NDEV,)), pltpu.SemaphoreType.DMA((NDEV,))],
        compiler_params=_params(),
    )(packed, lru_parts, w_ada, dsilu_cctx)


def _normalize(src, mv, la, row0, tm, name, prev=None):
    rows, d = src.shape
    blk0 = row0 // tm

    def body(*refs):
        x_ref, mv_ref = refs[:2]
        h_ref, ht_ref = refs[-2:]
        xf = x_ref[...]
        r = lax.rsqrt(jnp.mean(xf * xf, axis=-1, keepdims=True) + EPS)
        h = xf * r * (mv_ref[0:1, :] * (1.0 + mv_ref[1:2, :])) + mv_ref[2:3, :]
        h_ref[...] = h.astype(BF16)
        ht_ref[...] = h.T.astype(BF16)

    in_specs = [pl.BlockSpec((tm, d), lambda i: (i, 0)), pl.BlockSpec((8, d), lambda i: (0, 0))]
    args = [src, mv]
    aliases = {}
    if prev is not None:
        in_specs += [ANY, ANY]
        args += list(prev)
        aliases = {2: 0, 3: 1}
    return _call(
        body, name=name,
        grid=(rows // tm,),
        out_shape=(jax.ShapeDtypeStruct((la, d), BF16), jax.ShapeDtypeStruct((d, la), BF16)),
        in_specs=in_specs,
        out_specs=(pl.BlockSpec((tm, d), lambda i: (blk0 + i, 0)), pl.BlockSpec((d, tm), lambda i: (0, blk0 + i))),
        input_output_aliases=aliases,
        compiler_params=_params(("arbitrary",)),
    )(*args)


def _gather_order(step):
    return (step & 1) | (((step >> 2) & 1) << 1) | (((step >> 1) & 1) << 2)


def _in_projection(h, w_shard, tm):
    la, d = h.shape
    bw = w_shard.shape[1]
    ni = la // tm
    where = jnp.reshape(_idx(_my_pos()), (1,)).astype(jnp.int32)

    def body(me_ref, h_ref, w_ref, p_ref, all_ref, wbuf, send_sems, recv_sems, local_sems):
        s, i = pl.program_id(0), pl.program_id(1)
        x, y, c = _my_pos()
        me, sibling = (x, y, c), (x, y, 1 - c)
        chips = [(1 - x, y), (x, 1 - y), (1 - x, 1 - y)]

        def copy(k, block, to, from_shard=False):
            return pltpu.make_async_remote_copy(
                src_ref=w_ref if from_shard else all_ref.at[_idx(block)], dst_ref=all_ref.at[_idx(block)],
                send_sem=send_sems.at[k], recv_sem=recv_sems.at[k], device_id=to, device_id_type=MESH)

        def load(block, slot):
            return pltpu.make_async_copy(all_ref.at[_idx(block)], wbuf.at[slot], local_sems.at[1])

        keep = pltpu.make_async_copy(w_ref, all_ref.at[_idx(me)], local_sems.at[0])
        first = [copy(0, me, sibling, True)] + [copy(1 + j, me, (*chip, c), True) for j, chip in enumerate(chips)]
        passed = [copy(4 + j, (*chip, c), sibling) for j, chip in enumerate(chips)]
        steps = [(copy(0, sibling, me), None, sibling)]
        for j, chip in enumerate(chips):
            steps.append((copy(1 + j, (*chip, c), me), passed[j], (*chip, c)))
            steps.append((copy(4 + j, (*chip, 1 - c), me), None, (*chip, 1 - c)))

        @pl.when((s == 0) & (i == 0))
        def _():
            keep.start()
            mine = pltpu.make_async_copy(w_ref, wbuf.at[0], local_sems.at[1])
            mine.start()
            for cp in first:
                cp.start()
            mine.wait()

        for n, (arrival, forward, block) in enumerate(steps, start=1):
            @pl.when((s == n - 1) & (i == ni - 1))
            def _(arrival=arrival, forward=forward, block=block, n=n):
                arrival.wait_recv()
                if forward is not None:
                    forward.start()
                load(block, n % 2).start()

        @pl.when((s > 0) & (i == 0))
        def _():
            load(me, s % 2).wait()

        p_ref[...] = _dot(h_ref[...], wbuf[s % 2]).astype(BF16)

        @pl.when((s == NDEV - 1) & (i == ni - 1))
        def _():
            for cp in first + passed:
                cp.wait_send()
            keep.wait()

    return _call(
        body, name="in_projection",
        grid_spec=pltpu.PrefetchScalarGridSpec(
            num_scalar_prefetch=1, grid=(NDEV, ni),
            in_specs=[pl.BlockSpec((tm, d), lambda s, i, me_ref: (i, 0)), ANY],
            out_specs=(pl.BlockSpec((tm, bw), lambda s, i, me_ref: (i, me_ref[0] ^ _gather_order(s))), ANY),
            scratch_shapes=[pltpu.VMEM((2, d, bw), BF16), pltpu.SemaphoreType.DMA((7,)),
                            pltpu.SemaphoreType.DMA((7,)), pltpu.SemaphoreType.DMA((2,))]),
        out_shape=(jax.ShapeDtypeStruct((la, NDEV * bw), BF16), jax.ShapeDtypeStruct((NDEV, d, bw), BF16)),
        compiler_params=_params(("arbitrary", "arbitrary")),
    )(where, h, w_shard)


def _conv_input(p, wcb, taps_m, l, t):
    la = p.shape[0]
    w = wcb.shape[1]
    nt = l // t

    def body(v_ref, wcb_ref, tm_ref, xb_ref):
        v16 = v_ref[...]
        xb = wcb_ref[4:5, :] + wcb_ref[0:1, :] * _dot(tm_ref[0], v16)
        for j in range(1, 4):
            xb = xb + wcb_ref[j:j + 1, :] * _dot(tm_ref[j], v16)
        xb_ref[...] = xb

    return _call(
        body, name="conv_input",
        grid=(nt + 1,),
        out_shape=jax.ShapeDtypeStruct((la, w), F32),
        in_specs=[pl.BlockSpec((t, w), lambda i: (i, 4)), pl.BlockSpec((8, w), lambda i: (0, 0)),
                  pl.BlockSpec((None, 4, t, t), lambda i: (i // nt, 0, 0, 0))],
        out_specs=pl.BlockSpec((t, w), lambda i: (i, 0)),
        compiler_params=_params(("arbitrary",)),
    )(p, wcb, taps_m)


def _lru_forward(xb, wg, lv, wo_shard, l, t):
    la, w = xb.shape
    gc = wg.shape[2]
    nt = l // t

    def body(xf_ref, xr_ref, wg_ref, lv_ref, wo_ref, hf_ref, hr_ref, wo_all,
             a_s, b_s, carry, send_sems, recv_sems):
        sends, arrivals = _gather_copies(wo_ref, wo_all, send_sems, recv_sems)

        @pl.when(pl.program_id(0) == 0)
        def _():
            carry[...] = jnp.zeros_like(carry)
            for cp in sends:
                cp.start()

        @pl.when(pl.program_id(0) == nt)
        def _():
            _exchange_wait(sends, arrivals)

        for dr, (x_ref, h_ref) in enumerate(((xf_ref, hf_ref), (xr_ref, hr_ref))):
            x = x_ref[...]
            a, s, _, ig, _ = _lru_coef(x, wg_ref, dr, lv_ref[3 * dr:3 * dr + 1, :],
                                       lv_ref[3 * dr + 1:3 * dr + 2, :], lv_ref[3 * dr + 2:3 * dr + 3, :], gc)
            a_s[...] = a
            b_s[...] = s * (ig * x)
            carry[dr] = _scan_tile(a_s, b_s, h_ref, carry[dr], dr == 1)

    full = lambda shape: pl.BlockSpec(shape, lambda i: (0,) * len(shape))
    fmap = lambda i: (jnp.where(i == 0, nt, i - 1), 0)
    rmap = lambda i: (jnp.where(i == 0, nt, nt - i), 0)
    return _call(
        body, name="lru_forward",
        grid=(nt + 1,),
        out_shape=(jax.ShapeDtypeStruct((la, w), F32), jax.ShapeDtypeStruct((la, w), F32),
                   jax.ShapeDtypeStruct((NDEV,) + wo_shard.shape, wo_shard.dtype)),
        in_specs=[pl.BlockSpec((t, w), fmap), pl.BlockSpec((t, w), rmap), full(wg.shape), full(lv.shape), ANY],
        out_specs=(pl.BlockSpec((t, w), fmap), pl.BlockSpec((t, w), rmap), ANY),
        scratch_shapes=[pltpu.VMEM((t, w), F32), pltpu.VMEM((t, w), F32), pltpu.VMEM((2, 8, w), F32),
                        pltpu.SemaphoreType.DMA((NDEV,)), pltpu.SemaphoreType.DMA((NDEV,))],
        compiler_params=_params(("arbitrary",)),
    )(xb, xb, wg, lv, wo_shard)


def _mix_gates(p_refs, hf_ref, hr_ref, wca_ref, perm_ref, t, w):
    bl, cl, ul, gl, ql = [r[...].astype(F32) for r in p_refs]
    pos, rowlen = _pos_rowlen((t, w), False)
    tt = cl * ul
    z = _conv3(tt, wca_ref, pos, rowlen)
    sig_g = _sigmoid(gl)
    sig_q = _sigmoid(ql)
    ylru = _dot(perm_ref[1], (hf_ref[...] + hr_ref[...]).astype(BF16))
    return bl, cl, ul, gl, ql, tt, z, sig_g, sig_q, ylru, pos, rowlen


def _p_specs(t, w, nt):
    return [pl.BlockSpec((t, w), functools.partial(lambda i, s: (jnp.minimum(i, nt - 1), s), s=s))
            for s in (0, 1, 2, 3, 5)]


def _mix_forward(x, tgt, p, hf, hr, wo, ov, wca, perm, t):
    l, d = x.shape
    w = d // 2
    nt = l // t

    def body(x_ref, tg_ref, b_ref, c_ref, u_ref, g_ref, q_ref, hf_ref, hr_ref, wo_ref, ov_ref, wca_ref, perm_ref,
             dn_ref, ct_ref, do_ref, part_ref):
        i = pl.program_id(0)
        bl, _, _, gl, ql, _, z, sig_g, sig_q, ylru, _, _ = _mix_gates(
            (b_ref, c_ref, u_ref, g_ref, q_ref), hf_ref, hr_ref, wca_ref, perm_ref, t, w)
        ya = bl * z * (gl * sig_g)
        yb = ylru * (ql * sig_q)
        ct_ref[0:w, :] = ya.T.astype(BF16)
        ct_ref[w:, :] = yb.T.astype(BF16)
        out = _dot(ya.astype(BF16), wo_ref[0:w, :]) + _dot(yb.astype(BF16), wo_ref[w:, :])
        gate, fg = ov_ref[0:1, :], ov_ref[1:2, :]
        n = x_ref[...] + gate * out
        rr = lax.rsqrt(jnp.mean(n * n, axis=-1, keepdims=True) + EPS)
        nh = n * rr
        e = nh * fg - tg_ref[...]
        loss = 0.5 * jnp.sum(jnp.mean(e * e, axis=-1, keepdims=True), axis=0, keepdims=True)
        dy = e * (1.0 / d)
        dnh = dy * fg
        dn = rr * (dnh - nh * jnp.mean(dnh * nh, axis=-1, keepdims=True))
        dn_ref[...] = dn
        do_ref[...] = (dn * gate).astype(BF16)

        @pl.when(i == 0)
        def _():
            part_ref[...] = jnp.zeros_like(part_ref)

        part_ref[0:1, :] += jnp.sum(dy * nh, axis=0, keepdims=True)
        part_ref[1:2, :] += jnp.sum(dn * out, axis=0, keepdims=True)
        part_ref[2:3, :] += jnp.broadcast_to(loss, (1, d))

    tile = lambda cols: pl.BlockSpec((t, cols), lambda i: (i, 0))
    full = lambda shape: pl.BlockSpec(shape, lambda i: (0,) * len(shape))
    return _call(
        body, name="mix_forward",
        grid=(nt,),
        out_shape=(jax.ShapeDtypeStruct((l, d), F32), jax.ShapeDtypeStruct((d, l), BF16),
                   jax.ShapeDtypeStruct((l, d), BF16), jax.ShapeDtypeStruct((8, d), F32)),
        in_specs=[tile(d), tile(d)] + _p_specs(t, w, nt) + [tile(w), tile(w),
                  pl.BlockSpec((d, d), lambda i: (0, 0), pipeline_mode=pl.Buffered(1)),
                  full(ov.shape), full(wca.shape), full(perm.shape)],
        out_specs=(tile(d), pl.BlockSpec((d, t), lambda i: (0, i)), tile(d), full((8, d))),
        compiler_params=_params(("arbitrary",)),
    )(x, tgt, p, p, p, p, p, hf, hr, wo, ov, wca, perm)


def _mix_backward(dout, p, hf, hr, wo, wca, perm, g_wout, l, t):
    d = dout.shape[1]
    w = d // 2
    nt = l // t
    la = p.shape[0]

    def body(do_ref, b_ref, c_ref, u_ref, g_ref, q_ref, hf_ref, hr_ref, wo_ref, wca_ref, perm_ref, gw_ref,
             dp_ref, dh_ref, part_ref, sc_ref, send_sems, recv_sems):
        i = pl.program_id(0)
        copies = _scatter_copies(gw_ref, sc_ref, send_sems, recv_sems)

        @pl.when(i == 0)
        def _():
            part_ref[...] = jnp.zeros_like(part_ref)
            for cp in copies:
                cp.start()

        @pl.when(i == nt)
        def _():
            dp_ref[...] = jnp.zeros_like(dp_ref)
            _exchange_wait(copies, copies[1:])

        @pl.when(i < nt)
        def _():
            bl, cl, ul, gl, ql, tt, z, sig_g, sig_q, ylru, pos, rowlen = _mix_gates(
                (b_ref, c_ref, u_ref, g_ref, q_ref), hf_ref, hr_ref, wca_ref, perm_ref, t, w)
            do = do_ref[...]
            dya = _dot_nt(do, wo_ref[0:w, :])
            dyb = _dot_nt(do, wo_ref[w:, :])
            sg = gl * sig_g
            dz = dya * bl * sg
            dt = _conv3_t(dz, wca_ref, pos, rowlen)
            dp_ref[:, 0:w] = (dya * z * sg).astype(BF16)
            dp_ref[:, w:2 * w] = (dt * ul).astype(BF16)
            dp_ref[:, 2 * w:3 * w] = (dt * cl).astype(BF16)
            dp_ref[:, 3 * w:4 * w] = (dya * bl * z * (sig_g * (1.0 + gl * (1.0 - sig_g)))).astype(BF16)
            dp_ref[:, 4 * w:5 * w] = jnp.zeros((t, w), BF16)
            dp_ref[:, 5 * w:6 * w] = (dyb * ylru * (sig_q * (1.0 + ql * (1.0 - sig_q)))).astype(BF16)
            dh_ref[...] = _dot(perm_ref[0], (dyb * (ql * sig_q)).astype(BF16)).astype(BF16)
            part_ref[0:1, :] += jnp.sum(dz * _down(tt, 1, pos), axis=0, keepdims=True)
            part_ref[1:2, :] += jnp.sum(dz * tt, axis=0, keepdims=True)
            part_ref[2:3, :] += jnp.sum(dz * _up(tt, 1, pos, rowlen), axis=0, keepdims=True)

    clamp = lambda cols: pl.BlockSpec((t, cols), lambda i: (jnp.minimum(i, nt - 1), 0))
    full = lambda shape: pl.BlockSpec(shape, lambda i: (0,) * len(shape))
    return _call(
        body, name="mix_backward",
        grid=(nt + 1,),
        out_shape=(jax.ShapeDtypeStruct((la, 6 * w), BF16), jax.ShapeDtypeStruct((l, w), BF16),
                   jax.ShapeDtypeStruct((8, w), F32), jax.ShapeDtypeStruct(g_wout.shape, g_wout.dtype)),
        in_specs=[clamp(d)] + _p_specs(t, w, nt) + [clamp(w), clamp(w),
                  pl.BlockSpec((d, d), lambda i: (0, 0), pipeline_mode=pl.Buffered(1)), full(wca.shape),
                  full(perm.shape), ANY],
        out_specs=(pl.BlockSpec((t, 6 * w), lambda i: (i, 0)), clamp(w), full((8, w)), ANY),
        scratch_shapes=[pltpu.SemaphoreType.DMA((NDEV,)), pltpu.SemaphoreType.DMA((NDEV,))],
        compiler_params=_params(("arbitrary",)),
    )(dout, p, p, p, p, p, hf, hr, wo, wca, perm, g_wout)


def _lru_backward(direction, xb, dhs, hs, wg, lv, l, t, conv=None):
    la, w = hs.shape
    gc = wg.shape[2]
    ng = w // gc
    nt = l // t
    nblk8 = la // 8
    last = conv is not None
    assert last == (direction == 1)

    if direction == 0:
        tile = lambda i: jnp.where(i == nt, nt, nt - 1 - i)
        halo = lambda i: jnp.where(tile(i) == 0, nblk8 - 1, tile(i) * (t // 8) - 1)
    else:
        tile = lambda i: i
        halo = lambda i: jnp.minimum((i + 1) * (t // 8), nblk8 - 1)

    def body(*refs):
        x_ref, dh_ref, hs_ref, halo_ref, wg_ref, lv_ref = refs[:6]
        if last:
            v_ref, wcb_ref, tm_ref, bm_ref, dxo_ref = refs[6:11]
        out_ref, dwg_ref, part_ref, a_s, dh_s, g_s, carry = refs[-7:]
        i = pl.program_id(0)
        is_ctx = i == nt

        @pl.when(i == 0)
        def _():
            carry[...] = jnp.zeros_like(carry)
            dwg_ref[...] = jnp.zeros_like(dwg_ref)
            part_ref[...] = jnp.zeros_like(part_ref)

        xb = x_ref[...]
        lam = lv_ref[3 * direction + 2:3 * direction + 3, :]
        a, s, r, ig, sp = _lru_coef(xb, wg_ref, direction, lv_ref[3 * direction:3 * direction + 1, :],
                                    lv_ref[3 * direction + 1:3 * direction + 2, :], lam, gc)
        hs_t = hs_ref[...]
        r8 = _rows((8, w))
        if direction == 0:
            edge = jnp.where(is_ctx, 0.0, halo_ref[7:8, :])
            first = jnp.where(r8 == 0, edge, pltpu.roll(hs_t[t - 8:, :], 1, 0))
            hprev = jnp.concatenate([first, hs_t[:t - 8, :]], axis=0)
        else:
            edge = jnp.where(is_ctx, 0.0, halo_ref[0:1, :])
            final = jnp.where(r8 == 7, edge, pltpu.roll(hs_t[:8, :], 7, 0))
            hprev = jnp.concatenate([hs_t[8:, :], final], axis=0)
        a_s[...] = a
        dh_s[...] = jnp.where(is_ctx, 0.0, dh_ref[...].astype(F32))
        carry[...] = _scan_tile_backward(a_s, dh_s, g_s, carry[...], direction == 0)

        g = g_s[...]
        ix = ig * xb
        gs = g * s
        dla = (g * a) * (hprev - ix * (a / s))
        dxb = gs * ig
        dzr = dla * (r * (1.0 - r)) * (-LRU_C * sp)
        dzi = gs * ix * (1.0 - ig)
        part_ref[0:1, :] += jnp.sum(dzr, axis=0, keepdims=True)
        part_ref[1:2, :] += jnp.sum(dzi, axis=0, keepdims=True)
        part_ref[2:3, :] += jnp.sum(dla * r, axis=0, keepdims=True) * (LRU_C * _sigmoid(-lam))
        pieces = []
        for gi in range(ng):
            sl = slice(gi * gc, (gi + 1) * gc)
            dz = jnp.concatenate([dzr[:, sl], dzi[:, sl]], axis=-1).astype(BF16)
            pieces.append(_dot_nt(dz, wg_ref[direction, gi]))
            dwg_ref[gi] += _dot(xb[:, sl].T.astype(BF16), dz)
        dxb = dxb + (pieces[0] if ng == 1 else jnp.concatenate(pieces, axis=-1))
        if not last:
            out_ref[...] = dxb
        else:
            dxb = dxb + dxo_ref[...]
            dxb16, v16 = dxb.astype(BF16), v_ref[...]
            dv = wcb_ref[0:1, :] * _dot(bm_ref[0], dxb16)
            for j in range(1, 4):
                dv = dv + wcb_ref[j:j + 1, :] * _dot(bm_ref[j], dxb16)
            out_ref[...] = dv.astype(BF16)
            part_ref[3:4, :] += jnp.sum(dxb, axis=0, keepdims=True)
            for j in range(4):
                part_ref[4 + j:5 + j, :] += jnp.sum(dxb * _dot(tm_ref[j], v16), axis=0, keepdims=True)

    full = lambda shape: pl.BlockSpec(shape, lambda i: (0,) * len(shape))
    kind = lambda i: (jnp.where(i == nt, 1, 0), 0, 0, 0)
    in_specs = [pl.BlockSpec((t, w), lambda i: (tile(i), 0)),
                pl.BlockSpec((t, w), lambda i: (jnp.minimum(tile(i), nt - 1), 0)),
                pl.BlockSpec((t, w), lambda i: (tile(i), 0)),
                pl.BlockSpec((8, w), lambda i: (halo(i), 0)),
                full(wg.shape), full(lv.shape)]
    args = [xb, dhs, hs, hs, wg, lv]
    if last:
        p, wcb, taps_m, back_m, dxb_other, dp = conv
        in_specs += [pl.BlockSpec((t, w), lambda i: (tile(i), 4)), full(wcb.shape),
                     pl.BlockSpec((None, 4, t, t), kind), pl.BlockSpec((None, 4, t, t), kind),
                     pl.BlockSpec((t, w), lambda i: (tile(i), 0)), ANY]
        args += [p, wcb, taps_m, back_m, dxb_other, dp]
        out0 = jax.ShapeDtypeStruct(dp.shape, dp.dtype)
        spec0 = pl.BlockSpec((t, w), lambda i: (tile(i), 4))
        aliases = {11: 0}
    else:
        out0 = jax.ShapeDtypeStruct((la, w), F32)
        spec0 = pl.BlockSpec((t, w), lambda i: (tile(i), 0))
        aliases = {}
    return _call(
        body, name="lru_backward_%d" % direction,
        grid=(nt + 1,),
        out_shape=(out0, jax.ShapeDtypeStruct((ng, gc, 2 * gc), F32), jax.ShapeDtypeStruct((8, w), F32)),
        in_specs=in_specs,
        out_specs=(spec0, full((ng, gc, 2 * gc)), full((8, w))),
        scratch_shapes=[pltpu.VMEM((t, w), F32), pltpu.VMEM((t, w), F32), pltpu.VMEM((t, w), F32),
                        pltpu.VMEM((8, w), F32)],
        input_output_aliases=aliases,
        compiler_params=_params(("arbitrary",)),
    )(*args)


def _weight_grad_t(at, b, nblk_m, nblk_n, tk, name):
    m, k = at.shape
    n = b.shape[1]
    bm, bn = m // nblk_m, n // nblk_n
    nk = k // tk

    def body(a_ref, b_ref, o_ref, acc):
        kk = pl.program_id(2)

        @pl.when(kk == 0)
        def _():
            acc[...] = jnp.zeros_like(acc)

        acc[...] += _dot(a_ref[...], b_ref[...])

        @pl.when(kk == nk - 1)
        def _():
            o_ref[...] = acc[...].astype(BF16)

    return _call(
        body, name=name,
        grid=(nblk_m, nblk_n, nk),
        out_shape=jax.ShapeDtypeStruct((nblk_m * nblk_n, bm, bn), BF16),
        in_specs=[pl.BlockSpec((bm, tk), lambda i, j, kk: (i, kk)),
                  pl.BlockSpec((tk, bn), lambda i, j, kk: (kk, j))],
        out_specs=pl.BlockSpec((None, bm, bn), lambda i, j, kk: (i * nblk_n + j, 0, 0)),
        scratch_shapes=[pltpu.VMEM((bm, bn), F32)],
        compiler_params=_params(("arbitrary", "arbitrary", "arbitrary")),
    )(at, b)


def _weight_grad_scatter(at, b, tk, name):
    m, k = at.shape
    n = b.shape[1]
    bn = n // NDEV
    nk = k // tk
    where = jnp.stack([_idx(_my_pos()), lax.axis_index("c")]).astype(jnp.int32)

    def body(w_ref, a_ref, b_ref, recv_ref, acc, sbuf, sib, sib_send, sib_recv, chip_send, chip_recv, keep_sem):
        s, kk = pl.program_id(0), pl.program_id(1)
        x, y, c = _my_pos()

        @pl.when(kk == 0)
        def _():
            acc[...] = _dot(a_ref[...], b_ref[...])

        @pl.when(kk > 0)
        def _():
            acc[...] += _dot(a_ref[...], b_ref[...])

        def to_sibling(j):
            return pltpu.make_async_remote_copy(
                src_ref=sbuf.at[0], dst_ref=sib.at[j], send_sem=sib_send.at[j], recv_sem=sib_recv.at[j],
                device_id=(x, y, 1 - c), device_id_type=MESH)

        def to_chip(j):
            dist = _chip_order(j, c)
            return pltpu.make_async_remote_copy(
                src_ref=sbuf.at[1], dst_ref=recv_ref.at[dist // 2], send_sem=chip_send.at[j],
                recv_sem=chip_recv.at[dist // 2], device_id=_peer_at(dist), device_id_type=MESH)

        keep = pltpu.make_async_copy(sbuf.at[1], recv_ref.at[0], keep_sem)
        sends = []
        for j in range(4):
            sends += [to_sibling(j), to_chip(j) if j < 3 else keep]

        for st in range(NDEV):
            @pl.when((kk == nk - 1) & (s == st))
            def _(st=st):
                if st >= 2:
                    sends[st - 2].wait_send()
                part = acc[...]
                if st % 2 == 1:
                    to_sibling(st // 2).wait_recv()
                    part = part + sib[st // 2].astype(F32)
                sbuf[st % 2] = part.astype(BF16)
                sends[st].start()
                if st == NDEV - 1:
                    sends[st - 1].wait_send()
                    sends[st].wait()
                    for j in range(1, 4):
                        pltpu.make_async_remote_copy(
                            src_ref=sbuf.at[0], dst_ref=recv_ref.at[j], send_sem=chip_send.at[0],
                            recv_sem=chip_recv.at[j], device_id=_peer_at(2 * j), device_id_type=MESH).wait_recv()

    blk = lambda s, w_ref: w_ref[0] ^ _scatter_order(s, w_ref[1])
    return _call(
        body, name=name,
        grid_spec=pltpu.PrefetchScalarGridSpec(
            num_scalar_prefetch=1, grid=(NDEV, nk),
            in_specs=[pl.BlockSpec((m, tk), lambda s, kk, w_ref: (0, kk)),
                      pl.BlockSpec((tk, bn), lambda s, kk, w_ref: (kk, blk(s, w_ref)))],
            out_specs=ANY,
            scratch_shapes=[pltpu.VMEM((m, bn), F32), pltpu.VMEM((2, m, bn), BF16), pltpu.VMEM((4, m, bn), BF16),
                            pltpu.SemaphoreType.DMA((4,)), pltpu.SemaphoreType.DMA((4,)),
                            pltpu.SemaphoreType.DMA((4,)), pltpu.SemaphoreType.DMA((4,)),
                            pltpu.SemaphoreType.DMA]),
        out_shape=jax.ShapeDtypeStruct((4, m, bn), BF16),
        compiler_params=_params(("arbitrary", "arbitrary")),
    )(where, at, b)


def _input_backward(dp, w_all, src, mv, row0, tm, nbk, name, dn=None):
    rows, d = src.shape
    nb, _, bw = w_all.shape
    nk = nb // nbk
    ni = rows // tm
    blk0 = row0 // tm
    latent = dn is not None

    def body(*refs):
        dp_ref, w_ref, x_ref, mv_ref = refs[:4]
        outs = refs[4 + latent:]
        part_ref, acc = outs[latent], outs[latent + 1]
        i, k = pl.program_id(0), pl.program_id(1)

        def product():
            step = _dot_nt(dp_ref[:, 0:bw], w_ref[0])
            for q in range(1, nbk):
                step = step + _dot_nt(dp_ref[:, q * bw:(q + 1) * bw], w_ref[q])
            return step

        def finish(slot):
            xf = x_ref[...]
            r = lax.rsqrt(jnp.mean(xf * xf, axis=-1, keepdims=True) + EPS)
            xn = xf * r
            dhl = acc[slot]
            gain, sc = mv_ref[0:1, :], mv_ref[1:2, :]
            dhx = jnp.sum(dhl * xn, axis=0, keepdims=True)
            part_ref[0:1, :] += jnp.sum(dhl, axis=0, keepdims=True)
            part_ref[1:2, :] += dhx * gain
            part_ref[2:3, :] += dhx * (1.0 + sc)
            if latent:
                dxn = dhl * (gain * (1.0 + sc))
                outs[0][...] = refs[4][...] + r * (dxn - xn * jnp.mean(dxn * xn, axis=-1, keepdims=True))

        @pl.when((i == 0) & (k == 0))
        def _():
            part_ref[...] = jnp.zeros_like(part_ref)
            acc[0] = product()

        @pl.when((i > 0) & (i < ni) & (k == 0))
        def _():
            acc[i % 2] = product()
            finish((i - 1) % 2)

        @pl.when((i == ni) & (k == 0))
        def _():
            finish((ni - 1) % 2)

        @pl.when((i < ni) & (k > 0))
        def _():
            acc[i % 2] += product()

    tile = pl.BlockSpec((tm, d), lambda i, k: (jnp.maximum(i - 1, 0), 0))
    vec = pl.BlockSpec((8, d), lambda i, k: (0, 0))
    kblock = lambda i, k: jnp.where(i == ni, nk - 1, k)
    return _call(
        body, name=name,
        grid=(ni + 1, nk),
        out_shape=((jax.ShapeDtypeStruct((rows, d), F32),) if latent else ()) + (jax.ShapeDtypeStruct((8, d), F32),),
        in_specs=[pl.BlockSpec((tm, nbk * bw), lambda i, k: (blk0 + jnp.minimum(i, ni - 1), kblock(i, k))),
                  pl.BlockSpec((nbk, d, bw), lambda i, k: (kblock(i, k), 0, 0)), tile, vec]
                 + ([tile] if latent else []),
        out_specs=((tile,) if latent else ()) + (vec,),
        scratch_shapes=[pltpu.VMEM((2, tm, d), F32)],
        compiler_params=_params(("arbitrary", "arbitrary")),
    )(*([dp, w_all, src, mv] + ([dn] if latent else [])))


def _adamw_scattered(parts, w, m, v, tr):
    r, c = w.shape
    nslot = parts.shape[0]

    def body(p_ref, w_ref, m_ref, v_ref, g_ref, d_ref, m2_ref, v2_ref):
        g = p_ref[0].astype(F32)
        for k in range(1, nslot):
            g = g + p_ref[k].astype(F32)
        g_ref[...] = g
        d_ref[...], m2_ref[...], v2_ref[...] = _adamw(w_ref[...], g, m_ref[...], v_ref[...])

    tile = pl.BlockSpec((tr, c), lambda i: (i, 0))
    return _call(
        body, name="adamw_scattered_%dx%d" % (r, c),
        grid=(r // tr,),
        out_shape=tuple(jax.ShapeDtypeStruct((r, c), F32) for _ in range(4)),
        in_specs=[pl.BlockSpec((nslot, tr, c), lambda i: (0, i, 0)), tile, tile, tile],
        out_specs=(tile,) * 4,
        compiler_params=_params(("arbitrary",)),
    )(parts, w, m, v)


def _adamw_ada(st, dmod, w, m, v, tr):
    r, c = w.shape

    def body(s_ref, dm_ref, w_ref, m_ref, v_ref, g_ref, d_ref, m2_ref, v2_ref):
        g = jnp.dot(s_ref[...], dm_ref[...], precision=HIGHEST, preferred_element_type=F32)
        g_ref[...] = g
        d_ref[...], m2_ref[...], v2_ref[...] = _adamw(w_ref[...], g, m_ref[...], v_ref[...])

    tile = pl.BlockSpec((tr, c), lambda i: (i, 0))
    return _call(
        body, name="adamw_ada",
        grid=(r // tr,),
        out_shape=tuple(jax.ShapeDtypeStruct((r, c), F32) for _ in range(4)),
        in_specs=[pl.BlockSpec((tr, 16), lambda i: (i, 0)), pl.BlockSpec((16, c), lambda i: (0, 0)),
                  tile, tile, tile],
        out_specs=(tile,) * 4,
        compiler_params=_params(("arbitrary",)),
    )(st, dmod, w, m, v)


def _adamw_packed(g, w, m, v):
    def body(g_ref, w_ref, m_ref, v_ref, d_ref, m2_ref, v2_ref):
        d_ref[...], m2_ref[...], v2_ref[...] = _adamw(w_ref[...], g_ref[...], m_ref[...], v_ref[...])

    return _call(
        body, name="adamw_packed",
        out_shape=tuple(jax.ShapeDtypeStruct(w.shape, F32) for _ in range(3)),
        in_specs=[VMEM] * 4, out_specs=(VMEM,) * 3,
        compiler_params=_params(),
    )(g, w, m, v)


def _blockdiag_groups(wh, gc):
    h, dh, _ = wh.shape
    g = gc // dh
    w4 = wh.reshape(h // g, g, dh, dh)
    bd = jnp.einsum("ngij,gh->ngihj", w4, jnp.eye(g, dtype=wh.dtype))
    return bd.reshape(h // g, gc, gc)


def _blockdiag_extract(bd, dh):
    ng, gc, _ = bd.shape
    g = gc // dh
    x = bd.reshape(ng, g, dh, g, dh)
    return jnp.einsum("ngihj,gh->ngij", x, jnp.eye(g, dtype=bd.dtype)).reshape(ng * g, dh, dh)


def _rows8(*vecs):
    rows = [jnp.reshape(v, (1, -1)).astype(F32) for v in vecs]
    n = rows[0].shape[1]
    return jnp.concatenate(rows + [jnp.zeros((8 - len(rows), n), F32)], axis=0)


def _pack(pieces):
    flat = jnp.concatenate([jnp.reshape(a, (-1,)).astype(F32) for a in pieces])
    total = -(-flat.shape[0] // 1024) * 1024
    return jnp.pad(flat, (0, total - flat.shape[0])).reshape(total // 128, 128)


def _unpack(packed, shapes):
    flat = packed.reshape(-1)
    out, off = [], 0
    for s in shapes:
        n = 1
        for q in s:
            n *= q
        out.append(flat[off:off + n].reshape(s))
        off += n
    return out


def kernel(x, c, ctx, c_ctx, norm_g, w_ada, b_ada, w_in, w_conv_a, w_conv_b, b_conv_b, lru_wa, lru_ba, lru_wx, lru_bx, lru_lambda, w_out, final_g, loss_target, m_c_ctx, m_norm_g, m_w_ada, m_b_ada, m_w_in, m_w_conv_a, m_w_conv_b, m_b_conv_b, m_lru_wa, m_lru_ba, m_lru_wx, m_lru_bx, m_lru_lambda, m_w_out, m_final_g, v_c_ctx, v_norm_g, v_w_ada, v_b_ada, v_w_in, v_w_conv_a, v_w_conv_b, v_b_conv_b, v_lru_wa, v_lru_ba, v_lru_wx, v_lru_bx, v_lru_lambda, v_w_out, v_final_g):
    _, l, d = x.shape
    lc = ctx.shape[1]
    w = d // 2
    t = lc
    assert l % t == 0 and t % GRID_W == 0 and t % 128 == 0
    dh = w // N_HEADS
    gc = min(w, MXU_WIDTH)
    cols = w_ada.shape[2]
    wo_rows = w_out.shape[1]
    me = _idx(_my_pos())
    x2, ctx2, tgt2 = x[0], ctx[0], loss_target[0]
    w_ada2, w_in2, w_out2 = w_ada[0], w_in[0], w_out[0]

    small_mine = jnp.concatenate([w_conv_a[0], w_conv_b[0], lru_ba[0], lru_bx[0], lru_lambda[0],
                                  jnp.zeros((3, w // NDEV), F32)], axis=0)
    mod_all, s_mat, small_all = _mod_forward(
        jnp.broadcast_to(c, (8, d)), jnp.broadcast_to(c_ctx[None], (8, d)), w_ada2, small_mine)
    mod = jnp.transpose(mod_all, (1, 0, 2)).reshape(16, NDEV * cols) + b_ada
    mod_lat = lax.dynamic_slice_in_dim(mod, me, 1, axis=0)
    sh_l, sc_l, gt_l = jnp.split(mod_lat, 3, axis=-1)
    sh_c, sc_c, _ = jnp.split(mod[8:9], 3, axis=-1)
    small = jnp.transpose(small_all, (1, 0, 2)).reshape(16, w)
    wca = _rows8(*[small[j] for j in range(0, 3)])
    wcb = _rows8(*[small[j] for j in range(3, 7)], b_conv_b)
    lv = _rows8(small[7], small[9], small[11], small[8], small[10], small[12])
    wg = jnp.stack([
        jnp.concatenate([_blockdiag_groups(lru_wa[0, dr], gc), _blockdiag_groups(lru_wx[0, dr], gc)], axis=-1)
        for dr in range(2)]).astype(BF16)

    la = l + lc
    tm = 2 * t if l % (2 * t) == 0 else t
    tk = 3 * t if la % (3 * t) == 0 else t
    h, hlt = _normalize(x2, _rows8(norm_g, sc_l, sh_l), la, 0, tm, "normalize")
    h, hlt = _normalize(ctx2, _rows8(norm_g, sc_c, sh_c), la, l, t, "normalize_ctx", prev=(h, hlt))
    p, w_all = _in_projection(h, w_in2.astype(BF16), la // 8 if la % 128 == 0 else tk)
    taps_m, back_m, perm = _scan_matrices(t)
    xb = _conv_input(p, wcb, taps_m, l, t)
    hf, hr, wo_all = _lru_forward(xb, wg, lv, w_out2.astype(BF16), l, t)
    wo = wo_all.reshape(d, d)
    dn, catt, dout, part_mix = _mix_forward(x2, tgt2, p, hf, hr, wo, _rows8(gt_l, final_g), wca, perm, t)
    g_wout = _weight_grad_t(catt, dout, 2, 1, 4 * t if l % (4 * t) == 0 else t, "grad_w_out")
    dp, dhs, part_ca, sc_wout = _mix_backward(dout, p, hf, hr, wo, wca, perm, g_wout.reshape(NDEV, wo_rows, d), l, t)
    dxb0, dwg0, part_l0 = _lru_backward(0, xb, dhs, hf, wg, lv, l, t)
    dp, dwg1, part_l1 = _lru_backward(1, xb, dhs, hr, wg, lv, l, t, conv=(p, wcb, taps_m, back_m, dxb0, dp))
    sc_win = _weight_grad_scatter(hlt, dp, tk, "grad_w_in")
    grad_x, part_lat = _input_backward(dp, w_all, x2, _rows8(norm_g, sc_l), 0, tm, 2, "input_backward", dn=dn)
    (part_ctx,) = _input_backward(dp, w_all, ctx2, _rows8(norm_g, sc_c), l, t, 2, "input_backward_ctx")
    part_in = jnp.concatenate([part_lat[0:2], part_ctx[0:2], (part_lat[2] + part_ctx[2])[None]], axis=0)

    dwa = jnp.stack([_blockdiag_extract(dwg0[:, :, :gc], dh), _blockdiag_extract(dwg1[:, :, :gc], dh)])
    dwx = jnp.stack([_blockdiag_extract(dwg0[:, :, gc:], dh), _blockdiag_extract(dwg1[:, :, gc:], dh)])
    lru_part = jnp.stack([dwa, dwx]).reshape(NDEV, -1, 128)
    zeros_d = jnp.zeros((d,), F32)
    pieces = [
        jnp.concatenate([part_in[0], part_in[1], part_mix[1]]),
        jnp.concatenate([part_in[2], part_in[3], zeros_d]),
        part_in[4], part_mix[0], part_ca[0:3], part_l1[4:8], part_l1[3],
        jnp.stack([part_l0[0], part_l1[0]]), jnp.stack([part_l0[1], part_l1[1]]),
        jnp.stack([part_l0[2], part_l1[2]]), part_mix[2, 0:1],
    ]
    shapes = [(3 * d,), (3 * d,), (d,), (d,), (3, w), (4, w), (w,), (2, w), (2, w), (2, w), (1,)]
    sig_cc = jax.nn.sigmoid(c_ctx)
    dsilu_cc = jnp.broadcast_to((sig_cc * (1.0 + c_ctx * (1.0 - sig_cc)))[None], (8, d))
    psum, pall, lru_sum, g_cctx8 = _reduce_small(_pack(pieces), lru_part, w_ada2, dsilu_cc)
    (g_modl, g_modc, g_norm, g_final, g_ca, g_cb, g_bcb, g_ba, g_bx, g_lam, loss1) = _unpack(psum, shapes)
    loss = loss1[0]
    g_cctx = g_cctx8[0]
    g_bada = (g_modl + g_modc)[None]
    g_lru = lru_sum.reshape(2, 2, N_HEADS, dh, dh)
    g_wa, g_wx = g_lru[0][None], g_lru[1][None]
    wsl = w // NDEV
    mine = lambda a: lax.dynamic_slice_in_dim(a, me * wsl, wsl, axis=-1)
    g_ca_m, g_cb_m, g_ba_m, g_bx_m, g_lam_m = (mine(g_ca)[None], mine(g_cb)[None], mine(g_ba)[None],
                                               mine(g_bx)[None], mine(g_lam)[None])
    g_norm, g_bcb = g_norm[None], g_bcb[None]

    cb = cols // 128
    per_dev = pall[:, :3 * d // 128].reshape(NDEV, NDEV, cols)
    dmod_lat = lax.dynamic_slice_in_dim(per_dev, me, 1, axis=1)[:, 0]
    dmod_ctx = lax.dynamic_slice_in_dim(g_modc.reshape(NDEV, cols), me, 1, axis=0)
    dmod16 = jnp.concatenate([dmod_lat, dmod_ctx, jnp.zeros((7, cols), F32)], axis=0)
    tr_ada = 256 if d % 256 == 0 else d
    g_wada, d_wada, m_wada, v_wada = _adamw_ada(s_mat.T, dmod16, w_ada2, m_w_ada[0], v_w_ada[0], tr_ada)
    g_win2, d_win, m_win, v_win = _adamw_scattered(sc_win, w_in2, m_w_in[0], v_w_in[0], tr_ada)
    tr_out = 64 if wo_rows % 64 == 0 else wo_rows
    g_wout2, d_wout, m_wout, v_wout = _adamw_scattered(sc_wout, w_out2, m_w_out[0], v_w_out[0], tr_out)

    small_w = [c_ctx, norm_g, b_ada, w_conv_a, w_conv_b, b_conv_b, lru_wa, lru_ba, lru_wx, lru_bx, lru_lambda, final_g]
    small_m = [m_c_ctx, m_norm_g, m_b_ada, m_w_conv_a, m_w_conv_b, m_b_conv_b, m_lru_wa, m_lru_ba, m_lru_wx,
               m_lru_bx, m_lru_lambda, m_final_g]
    small_v = [v_c_ctx, v_norm_g, v_b_ada, v_w_conv_a, v_w_conv_b, v_b_conv_b, v_lru_wa, v_lru_ba, v_lru_wx,
               v_lru_bx, v_lru_lambda, v_final_g]
    small_g = [g_cctx, g_norm, g_bada, g_ca_m, g_cb_m, g_bcb, g_wa, g_ba_m, g_wx, g_bx_m, g_lam_m, g_final]
    sshapes = [a.shape for a in small_w]
    d_s, m_s, v_s = _adamw_packed(_pack(small_g), _pack(small_w), _pack(small_m), _pack(small_v))
    d_s, m_s, v_s = _unpack(d_s, sshapes), _unpack(m_s, sshapes), _unpack(v_s, sshapes)
    small_g = [jnp.reshape(a, s) for a, s in zip(small_g, sshapes)]

    def weights(small_list, ada, win, wout):
        (cctx_, norm_, bada_, ca_, cb_, bcb_, wa_, ba_, wx_, bx_, lam_, final_) = small_list
        return [cctx_, norm_, ada[None], bada_, win[None], ca_, cb_, bcb_, wa_, ba_, wx_, bx_, lam_, wout[None], final_]

    return (loss, grad_x[None],
            *weights(small_g, g_wada, g_win2, g_wout2), *weights(d_s, d_wada, d_win, d_wout),
            *weights(m_s, m_wada, m_win, m_wout), *weights(v_s, v_wada, v_win, v_wout))
```

```python
import functools

import jax
import jax.numpy as jnp
import numpy as np
from jax import lax
from jax.experimental import pallas as pl
from jax.experimental.pallas import tpu as pltpu

F32 = jnp.float32
BF16 = jnp.bfloat16
MESH = pl.DeviceIdType.MESH
NDEV = 8
GRID_W = 64
N_HEADS = 16
LRU_C = 8.0
EPS = 1e-6
MXU_WIDTH = 256
VMEM_LIMIT = 60 * 1024 * 1024

ADAM_LR = 0.001
ADAM_B1 = 0.9
ADAM_B2 = 0.999
ADAM_EPS = 1e-08
ADAM_WD = 0.01
ADAM_STEP = 10
ADAM_C1 = 1.0 - ADAM_B1 ** ADAM_STEP
ADAM_C2 = 1.0 - ADAM_B2 ** ADAM_STEP

HIGHEST = lax.Precision.HIGHEST
ANY = pl.BlockSpec(memory_space=pl.ANY)
VMEM = pl.BlockSpec(memory_space=pltpu.VMEM)


def _call(body, **kw):
    return pl.pallas_call(body, **kw)


def _params(sem=None, vmem=VMEM_LIMIT):
    return pltpu.CompilerParams(dimension_semantics=sem, vmem_limit_bytes=vmem)


def _my_pos():
    return lax.axis_index("x"), lax.axis_index("y"), lax.axis_index("c")


def _idx(pos):
    return 4 * pos[0] + 2 * pos[1] + pos[2]


def _peer(k):
    x, y, c = _my_pos()
    return ((1 - x) if (k >> 2) & 1 else x, (1 - y) if (k >> 1) & 1 else y, (1 - c) if k & 1 else c)


def _exchange_vmem(src_ref, dst_ref, send_sems, recv_sems, base):
    me = _idx(_my_pos())
    sends = []
    for k in range(1, NDEV):
        cp = pltpu.make_async_remote_copy(
            src_ref=src_ref, dst_ref=dst_ref.at[me], send_sem=send_sems.at[base + k - 1],
            recv_sem=recv_sems.at[base + k - 1], device_id=_peer(k), device_id_type=MESH)
        cp.start()
        sends.append(cp)
    dst_ref[me] = src_ref[...]
    for k in range(1, NDEV):
        peer = _peer(k)
        pltpu.make_async_remote_copy(
            src_ref=src_ref, dst_ref=dst_ref.at[_idx(peer)], send_sem=send_sems.at[base + k - 1],
            recv_sem=recv_sems.at[base + k - 1], device_id=peer, device_id_type=MESH).wait_recv()
    for cp in sends:
        cp.wait_send()


def _sigmoid(z):
    return 0.5 * jnp.tanh(0.5 * z) + 0.5


def _softplus(x):
    return jnp.maximum(x, 0.0) + jnp.log1p(jnp.exp(-jnp.abs(x)))


def _one_minus_sq(a, la):
    series = (-2.0 * la) * (1.0 + la)
    return jnp.where(la > -0.0015, series, 1.0 - a * a)


def _dot(a, b):
    return jnp.dot(a, b, preferred_element_type=F32)


def _dot_nt(a, b):
    return lax.dot_general(a, b, (((1,), (1,)), ((), ())), preferred_element_type=F32)


def _rows(shape):
    return lax.broadcasted_iota(jnp.int32, shape, 0)


def _down(x, k, pos):
    return jnp.where(pos >= k, pltpu.roll(x, k, 0), 0.0)


def _up(x, k, pos, rowlen):
    return jnp.where(pos + k < rowlen, pltpu.roll(x, x.shape[0] - k, 0), 0.0)


def _pos_rowlen(shape, is_ctx):
    t = _rows(shape)
    pos = jnp.where(is_ctx, t, t & (GRID_W - 1))
    rowlen = jnp.where(is_ctx, shape[0], GRID_W)
    return pos, rowlen


def _scan_matrices(t):
    seg = t // 8
    r = np.arange(t)
    perm = (np.arange(t)[None, :] == ((r % 8) * seg + r // 8)[:, None]).astype(np.float32)
    rows, cols = r[:, None], r[None, :]
    taps, back = [], []
    for rowlen in (GRID_W, t):
        pos = rows % rowlen
        shift = {-2: (cols == rows - 2) & (pos >= 2), -1: (cols == rows - 1) & (pos >= 1),
                 0: cols == rows, 1: (cols == rows + 1) & (pos + 1 < rowlen),
                 2: (cols == rows + 2) & (pos + 2 < rowlen)}
        taps.append(np.stack([perm @ shift[k].astype(np.float32) for k in (-2, -1, 0, 1)]))
        back.append(np.stack([shift[k].astype(np.float32) @ perm.T for k in (2, 1, 0, -1)]))
    as_bf16 = lambda a: jnp.asarray(a, dtype=BF16)
    return as_bf16(np.stack(taps)), as_bf16(np.stack(back)), as_bf16(np.stack([perm, perm.T]))


def _conv3(t, w_ref, pos, rowlen):
    return w_ref[0:1, :] * _down(t, 1, pos) + w_ref[1:2, :] * t + w_ref[2:3, :] * _up(t, 1, pos, rowlen)


def _conv3_t(dz, w_ref, pos, rowlen):
    return w_ref[0:1, :] * _up(dz, 1, pos, rowlen) + w_ref[1:2, :] * dz + w_ref[2:3, :] * _down(dz, 1, pos)


def _chunk_scan(a, b, reverse):
    row = _rows(a.shape)
    for s in (1, 2, 4):
        if reverse:
            m = row < 8 - s
            sh = 8 - s
        else:
            m = row >= s
            sh = s
        a_s = jnp.where(m, pltpu.roll(a, sh, 0), 1.0)
        b_s = jnp.where(m, pltpu.roll(b, sh, 0), 0.0)
        b = b + a * b_s
        a = a * a_s
    return a, b


def _chain_segments(ptot, hend, carry, reverse):
    ca, cb = _chunk_scan(ptot, hend, reverse)
    incl = ca * carry + cb
    r8 = _rows(incl.shape)
    if reverse:
        start = jnp.where(r8 < 7, pltpu.roll(incl, 7, 0), carry)
        last = incl[0:1, :]
    else:
        start = jnp.where(r8 >= 1, pltpu.roll(incl, 1, 0), carry)
        last = incl[7:8, :]
    return start, jnp.broadcast_to(last, incl.shape)


def _blocks(nblock, reverse):
    order = range(nblock - 1, -1, -1) if reverse else range(nblock)
    return [slice(8 * k, 8 * k + 8) for k in order]


def _scan_tile(a_ref, b_ref, out_ref, carry, reverse):
    t, w = a_ref.shape
    seg = t // 8

    hend, ptot = jnp.zeros((8, w), F32), jnp.ones((8, w), F32)
    for rows in _blocks(seg, reverse):
        a = a_ref[rows, :]
        hend, ptot = a * hend + b_ref[rows, :], a * ptot
    h, new_carry = _chain_segments(ptot, hend, carry, reverse)
    for rows in _blocks(seg, reverse):
        h = a_ref[rows, :] * h + b_ref[rows, :]
        out_ref[rows, :] = h
    return new_carry


def _scan_tile_backward(a_ref, dh_ref, g_ref, carry, reverse):
    t, w = a_ref.shape
    seg = t // 8

    uend, ptot = jnp.zeros((8, w), F32), jnp.ones((8, w), F32)
    for rows in _blocks(seg, reverse):
        a = a_ref[rows, :]
        uend, ptot = a * (dh_ref[rows, :] + uend), a * ptot
    u, new_carry = _chain_segments(ptot, uend, carry, reverse)
    for rows in _blocks(seg, reverse):
        g = dh_ref[rows, :] + u
        g_ref[rows, :] = g
        u = a_ref[rows, :] * g
    return new_carry


def _lru_coef(xb, wg_ref, d, ba, bx, lam, gc):
    w = xb.shape[1]
    xb16 = xb.astype(BF16)
    zr, zi = [], []
    for g in range(w // gc):
        z = _dot(xb16[:, g * gc:(g + 1) * gc], wg_ref[d, g])
        zr.append(z[:, :gc])
        zi.append(z[:, gc:])
    zr = zr[0] if len(zr) == 1 else jnp.concatenate(zr, axis=-1)
    zi = zi[0] if len(zi) == 1 else jnp.concatenate(zi, axis=-1)
    r = _sigmoid(zr + ba)
    ig = _sigmoid(zi + bx)
    sp = _softplus(-lam)
    la = r * (-LRU_C * sp)
    a = jnp.exp(la)
    s = jnp.sqrt(_one_minus_sq(a, la))
    return a, s, r, ig, sp


def _adamw(w, g, m, v):
    m2 = ADAM_B1 * m + (1.0 - ADAM_B1) * g
    v2 = ADAM_B2 * v + (1.0 - ADAM_B2) * (g * g)
    m_hat = m2 / ADAM_C1
    v_hat = v2 / ADAM_C2
    delta = -ADAM_LR * (m_hat / (jnp.sqrt(v_hat) + ADAM_EPS) + ADAM_WD * w)
    return delta, m2, v2


def _mod_forward(c8, cctx8, w_ada, small):
    d = c8.shape[1]
    cols = w_ada.shape[1]

    def body(c_ref, cctx_ref, w_ref, sm_ref, mod_ref, s_ref, sm_all, cbuf, mod_my, send_sems, recv_sems):
        _exchange_vmem(sm_ref, sm_all, send_sems, recv_sems, 2 * (NDEV - 1))
        _exchange_vmem(c_ref, cbuf, send_sems, recv_sems, 0)
        row = _rows((8, d))
        c_all = jnp.zeros((8, d), F32)
        for b in range(NDEV):
            c_all = jnp.where(row == b, cbuf[b], c_all)
        cc = cctx_ref[...]
        s_top = c_all * _sigmoid(c_all)
        s_bot = jnp.where(row == 0, cc * _sigmoid(cc), 0.0)
        s = jnp.concatenate([s_top, s_bot], axis=0)
        s_ref[...] = s
        mod_my[...] = jnp.dot(s, w_ref[...], precision=HIGHEST, preferred_element_type=F32)
        _exchange_vmem(mod_my, mod_ref, send_sems, recv_sems, NDEV - 1)

    return _call(
        body, name="mod_forward",
        out_shape=(jax.ShapeDtypeStruct((NDEV, 16, cols), F32), jax.ShapeDtypeStruct((16, d), F32),
                   jax.ShapeDtypeStruct((NDEV,) + small.shape, F32)),
        in_specs=[VMEM] * 4, out_specs=(VMEM,) * 3,
        scratch_shapes=[pltpu.VMEM((NDEV, 8, d), F32), pltpu.VMEM((16, cols), F32),
                        pltpu.SemaphoreType.DMA((3 * (NDEV - 1),)), pltpu.SemaphoreType.DMA((3 * (NDEV - 1),))],
        compiler_params=_params(),
    )(c8, cctx8, w_ada, small)


def _scatter_copies(src_ref, dst_ref, send_sems, recv_sems):
    me = _idx(_my_pos())
    copies = [pltpu.make_async_copy(src_ref.at[me], dst_ref.at[0], send_sems.at[0])]
    for k in range(1, NDEV):
        peer = _peer(k)
        copies.append(pltpu.make_async_remote_copy(
            src_ref=src_ref.at[_idx(peer)], dst_ref=dst_ref.at[k], send_sem=send_sems.at[k],
            recv_sem=recv_sems.at[k], device_id=peer, device_id_type=MESH))
    return copies


def _gather_copies(src_ref, dst_ref, send_sems, recv_sems):
    me = _idx(_my_pos())
    sends = [pltpu.make_async_copy(src_ref, dst_ref.at[me], send_sems.at[0])]
    arrivals = []
    for k in range(1, NDEV):
        peer = _peer(k)
        sends.append(pltpu.make_async_remote_copy(
            src_ref=src_ref, dst_ref=dst_ref.at[me], send_sem=send_sems.at[k],
            recv_sem=recv_sems.at[k], device_id=peer, device_id_type=MESH))
        arrivals.append(pltpu.make_async_remote_copy(
            src_ref=src_ref, dst_ref=dst_ref.at[_idx(peer)], send_sem=send_sems.at[k],
            recv_sem=recv_sems.at[k], device_id=peer, device_id_type=MESH))
    return sends, arrivals


def _exchange_wait(sends, arrivals):
    sends[0].wait()
    for cp in arrivals:
        cp.wait_recv()
    for cp in sends[1:]:
        cp.wait_send()


def _chip_order(k, c):
    return (6, 4 - 2 * c, 2 + 2 * c, 0)[k]


def _scatter_order(s, c):
    k = s >> 1
    mine = jnp.where(k == 0, 6, jnp.where(k == 1, 4 - 2 * c, jnp.where(k == 2, 2 + 2 * c, 0)))
    theirs = jnp.where(k == 0, 6, jnp.where(k == 1, 2 + 2 * c, jnp.where(k == 2, 4 - 2 * c, 0))) ^ 1
    return jnp.where((s & 1) == 0, theirs, mine)


def _peer_at(dist):
    x, y, c = _my_pos()
    return (x ^ ((dist >> 2) & 1), y ^ ((dist >> 1) & 1), c ^ (dist & 1))


def _reduce_small(packed, lru_parts, w_ada, dsilu_cctx):
    rp = packed.shape[0]
    rl = lru_parts.shape[1]
    d, cols = w_ada.shape
    assert cols % 128 == 0
    cb = cols // 128

    def body(p_ref, l_ref, w_ref, ds_ref, sum_ref, all_ref, lru_ref, cctx_ref,
             lbuf, lsum, cpart, call, send_sems, recv_sems, lsend, lrecv):
        me = _idx(_my_pos())
        scattered = _scatter_copies(l_ref, lbuf, lsend, lrecv)
        for cp in scattered:
            cp.start()
        _exchange_vmem(p_ref, all_ref, send_sems, recv_sems, 0)
        acc = all_ref[0]
        for j in range(1, NDEV):
            acc = acc + all_ref[j]
        sum_ref[...] = acc
        _exchange_wait(scattered, scattered[1:])
        red = lbuf[0]
        for k in range(1, NDEV):
            red = red + lbuf[k]
        lsum[...] = red
        _exchange_vmem(lsum, lru_ref, send_sems, recv_sems, NDEV - 1)
        part = jnp.zeros((8, d), F32)
        for q in range(cb):
            dm = jnp.broadcast_to(sum_ref[pl.ds((NDEV + me) * cb + q, 1), :], (8, 128))
            part = part + lax.dot_general(dm, w_ref[:, q * 128:(q + 1) * 128],
                                          (((1,), (1,)), ((), ())), precision=HIGHEST,
                                          preferred_element_type=F32)
        cpart[...] = part
        _exchange_vmem(cpart, call, send_sems, recv_sems, 2 * (NDEV - 1))
        tot = call[0]
        for j in range(1, NDEV):
            tot = tot + call[j]
        cctx_ref[...] = tot * ds_ref[...]

    return _call(
        body, name="reduce_small",
        out_shape=(jax.ShapeDtypeStruct((rp, 128), F32), jax.ShapeDtypeStruct((NDEV, rp, 128), F32),
                   jax.ShapeDtypeStruct((NDEV, rl, 128), F32), jax.ShapeDtypeStruct((8, d), F32)),
        in_specs=[VMEM] * 4, out_specs=(VMEM,) * 4,
        scratch_shapes=[pltpu.VMEM((NDEV, rl, 128), F32), pltpu.VMEM((rl, 128), F32), pltpu.VMEM((8, d), F32),
                        pltpu.VMEM((NDEV, 8, d), F32),
                        pltpu.SemaphoreType.DMA((3 * (NDEV - 1),)), pltpu.SemaphoreType.DMA((3 * (NDEV - 1),)),
                        pltpu.SemaphoreType.DMA((NDEV,)), pltpu.SemaphoreType.DMA((NDEV,))],
        compiler_params=_params(),
    )(packed, lru_parts, w_ada, dsilu_cctx)


def _normalize(src, mv, la, row0, tm, name, prev=None):
    rows, d = src.shape
    blk0 = row0 // tm

    def body(*refs):
        x_ref, mv_ref = refs[:2]
        h_ref, ht_ref = refs[-2:]
        xf = x_ref[...]
        r = lax.rsqrt(jnp.mean(xf * xf, axis=-1, keepdims=True) + EPS)
        h = xf * r * (mv_ref[0:1, :] * (1.0 + mv_ref[1:2, :])) + mv_ref[2:3, :]
        h_ref[...] = h.astype(BF16)
        ht_ref[...] = h.T.astype(BF16)

    in_specs = [pl.BlockSpec((tm, d), lambda i: (i, 0)), pl.BlockSpec((8, d), lambda i: (0, 0))]
    args = [src, mv]
    aliases = {}
    if prev is not None:
        in_specs += [ANY, ANY]
        args += list(prev)
        aliases = {2: 0, 3: 1}
    return _call(
        body, name=name,
        grid=(rows // tm,),
        out_shape=(jax.ShapeDtypeStruct((la, d), BF16), jax.ShapeDtypeStruct((d, la), BF16)),
        in_specs=in_specs,
        out_specs=(pl.BlockSpec((tm, d), lambda i: (blk0 + i, 0)), pl.BlockSpec((d, tm), lambda i: (0, blk0 + i))),
        input_output_aliases=aliases,
        compiler_params=_params(("arbitrary",)),
    )(*args)


def _gather_order(step):
    return (step & 1) | (((step >> 2) & 1) << 1) | (((step >> 1) & 1) << 2)


def _in_projection(h, w_shard, tm):
    la, d = h.shape
    bw = w_shard.shape[1]
    ni = la // tm
    where = jnp.reshape(_idx(_my_pos()), (1,)).astype(jnp.int32)

    def body(me_ref, h_ref, w_ref, p_ref, all_ref, wbuf, send_sems, recv_sems, local_sems):
        s, i = pl.program_id(0), pl.program_id(1)
        x, y, c = _my_pos()
        me, sibling = (x, y, c), (x, y, 1 - c)
        chips = [(1 - x, y), (x, 1 - y), (1 - x, 1 - y)]

        def copy(k, block, to, from_shard=False):
            return pltpu.make_async_remote_copy(
                src_ref=w_ref if from_shard else all_ref.at[_idx(block)], dst_ref=all_ref.at[_idx(block)],
                send_sem=send_sems.at[k], recv_sem=recv_sems.at[k], device_id=to, device_id_type=MESH)

        def load(block, slot):
            return pltpu.make_async_copy(all_ref.at[_idx(block)], wbuf.at[slot], local_sems.at[1])

        keep = pltpu.make_async_copy(w_ref, all_ref.at[_idx(me)], local_sems.at[0])
        first = [copy(0, me, sibling, True)] + [copy(1 + j, me, (*chip, c), True) for j, chip in enumerate(chips)]
        passed = [copy(4 + j, (*chip, c), sibling) for j, chip in enumerate(chips)]
        steps = [(copy(0, sibling, me), None, sibling)]
        for j, chip in enumerate(chips):
            steps.append((copy(1 + j, (*chip, c), me), passed[j], (*chip, c)))
            steps.append((copy(4 + j, (*chip, 1 - c), me), None, (*chip, 1 - c)))

        @pl.when((s == 0) & (i == 0))
        def _():
            keep.start()
            mine = pltpu.make_async_copy(w_ref, wbuf.at[0], local_sems.at[1])
            mine.start()
            for cp in first:
                cp.start()
            mine.wait()

        for n, (arrival, forward, block) in enumerate(steps, start=1):
            @pl.when((s == n - 1) & (i == ni - 1))
            def _(arrival=arrival, forward=forward, block=block, n=n):
                arrival.wait_recv()
                if forward is not None:
                    forward.start()
                load(block, n % 2).start()

        @pl.when((s > 0) & (i == 0))
        def _():
            load(me, s % 2).wait()

        p_ref[...] = _dot(h_ref[...], wbuf[s % 2]).astype(BF16)

        @pl.when((s == NDEV - 1) & (i == ni - 1))
        def _():
            for cp in first + passed:
                cp.wait_send()
            keep.wait()

    return _call(
        body, name="in_projection",
        grid_spec=pltpu.PrefetchScalarGridSpec(
            num_scalar_prefetch=1, grid=(NDEV, ni),
            in_specs=[pl.BlockSpec((tm, d), lambda s, i, me_ref: (i, 0)), ANY],
            out_specs=(pl.BlockSpec((tm, bw), lambda s, i, me_ref: (i, me_ref[0] ^ _gather_order(s))), ANY),
            scratch_shapes=[pltpu.VMEM((2, d, bw), BF16), pltpu.SemaphoreType.DMA((7,)),
                            pltpu.SemaphoreType.DMA((7,)), pltpu.SemaphoreType.DMA((2,))]),
        out_shape=(jax.ShapeDtypeStruct((la, NDEV * bw), BF16), jax.ShapeDtypeStruct((NDEV, d, bw), BF16)),
        compiler_params=_params(("arbitrary", "arbitrary")),
    )(where, h, w_shard)


def _conv_input(p, wcb, taps_m, l, t):
    la = p.shape[0]
    w = wcb.shape[1]
    nt = l // t

    def body(v_ref, wcb_ref, tm_ref, xb_ref):
        v16 = v_ref[...]
        xb = wcb_ref[4:5, :] + wcb_ref[0:1, :] * _dot(tm_ref[0], v16)
        for j in range(1, 4):
            xb = xb + wcb_ref[j:j + 1, :] * _dot(tm_ref[j], v16)
        xb_ref[...] = xb

    return _call(
        body, name="conv_input",
        grid=(nt + 1,),
        out_shape=jax.ShapeDtypeStruct((la, w), F32),
        in_specs=[pl.BlockSpec((t, w), lambda i: (i, 4)), pl.BlockSpec((8, w), lambda i: (0, 0)),
                  pl.BlockSpec((None, 4, t, t), lambda i: (i // nt, 0, 0, 0))],
        out_specs=pl.BlockSpec((t, w), lambda i: (i, 0)),
        compiler_params=_params(("arbitrary",)),
    )(p, wcb, taps_m)


def _lru_forward(xb, wg, lv, wo_shard, l, t):
    la, w = xb.shape
    gc = wg.shape[2]
    nt = l // t

    def body(xf_ref, xr_ref, wg_ref, lv_ref, wo_ref, hf_ref, hr_ref, wo_all,
             a_s, b_s, carry, send_sems, recv_sems):
        sends, arrivals = _gather_copies(wo_ref, wo_all, send_sems, recv_sems)

        @pl.when(pl.program_id(0) == 0)
        def _():
            carry[...] = jnp.zeros_like(carry)
            for cp in sends:
                cp.start()

        @pl.when(pl.program_id(0) == nt)
        def _():
            _exchange_wait(sends, arrivals)

        for dr, (x_ref, h_ref) in enumerate(((xf_ref, hf_ref), (xr_ref, hr_ref))):
            x = x_ref[...]
            a, s, _, ig, _ = _lru_coef(x, wg_ref, dr, lv_ref[3 * dr:3 * dr + 1, :],
                                       lv_ref[3 * dr + 1:3 * dr + 2, :], lv_ref[3 * dr + 2:3 * dr + 3, :], gc)
            a_s[...] = a
            b_s[...] = s * (ig * x)
            carry[dr] = _scan_tile(a_s, b_s, h_ref, carry[dr], dr == 1)

    full = lambda shape: pl.BlockSpec(shape, lambda i: (0,) * len(shape))
    fmap = lambda i: (jnp.where(i == 0, nt, i - 1), 0)
    rmap = lambda i: (jnp.where(i == 0, nt, nt - i), 0)
    return _call(
        body, name="lru_forward",
        grid=(nt + 1,),
        out_shape=(jax.ShapeDtypeStruct((la, w), F32), jax.ShapeDtypeStruct((la, w), F32),
                   jax.ShapeDtypeStruct((NDEV,) + wo_shard.shape, wo_shard.dtype)),
        in_specs=[pl.BlockSpec((t, w), fmap), pl.BlockSpec((t, w), rmap), full(wg.shape), full(lv.shape), ANY],
        out_specs=(pl.BlockSpec((t, w), fmap), pl.BlockSpec((t, w), rmap), ANY),
        scratch_shapes=[pltpu.VMEM((t, w), F32), pltpu.VMEM((t, w), F32), pltpu.VMEM((2, 8, w), F32),
                        pltpu.SemaphoreType.DMA((NDEV,)), pltpu.SemaphoreType.DMA((NDEV,))],
        compiler_params=_params(("arbitrary",)),
    )(xb, xb, wg, lv, wo_shard)


def _mix_gates(p_refs, hf_ref, hr_ref, wca_ref, perm_ref, t, w):
    bl, cl, ul, gl, ql = [r[...].astype(F32) for r in p_refs]
    pos, rowlen = _pos_rowlen((t, w), False)
    tt = cl * ul
    z = _conv3(tt, wca_ref, pos, rowlen)
    sig_g = _sigmoid(gl)
    sig_q = _sigmoid(ql)
    ylru = _dot(perm_ref[1], (hf_ref[...] + hr_ref[...]).astype(BF16))
    return bl, cl, ul, gl, ql, tt, z, sig_g, sig_q, ylru, pos, rowlen


def _p_specs(t, w, nt):
    return [pl.BlockSpec((t, w), functools.partial(lambda i, s: (jnp.minimum(i, nt - 1), s), s=s))
            for s in (0, 1, 2, 3, 5)]


def _mix_forward(x, tgt, p, hf, hr, wo, ov, wca, perm, t):
    l, d = x.shape
    w = d // 2
    nt = l // t

    def body(x_ref, tg_ref, b_ref, c_ref, u_ref, g_ref, q_ref, hf_ref, hr_ref, wo_ref, ov_ref, wca_ref, perm_ref,
             dn_ref, ct_ref, do_ref, part_ref):
        i = pl.program_id(0)
        bl, _, _, gl, ql, _, z, sig_g, sig_q, ylru, _, _ = _mix_gates(
            (b_ref, c_ref, u_ref, g_ref, q_ref), hf_ref, hr_ref, wca_ref, perm_ref, t, w)
        ya = bl * z * (gl * sig_g)
        yb = ylru * (ql * sig_q)
        ct_ref[0:w, :] = ya.T.astype(BF16)
        ct_ref[w:, :] = yb.T.astype(BF16)
        out = _dot(ya.astype(BF16), wo_ref[0:w, :]) + _dot(yb.astype(BF16), wo_ref[w:, :])
        gate, fg = ov_ref[0:1, :], ov_ref[1:2, :]
        n = x_ref[...] + gate * out
        rr = lax.rsqrt(jnp.mean(n * n, axis=-1, keepdims=True) + EPS)
        nh = n * rr
        e = nh * fg - tg_ref[...]
        loss = 0.5 * jnp.sum(jnp.mean(e * e, axis=-1, keepdims=True), axis=0, keepdims=True)
        dy = e * (1.0 / d)
        dnh = dy * fg
        dn = rr * (dnh - nh * jnp.mean(dnh * nh, axis=-1, keepdims=True))
        dn_ref[...] = dn
        do_ref[...] = (dn * gate).astype(BF16)

        @pl.when(i == 0)
        def _():
            part_ref[...] = jnp.zeros_like(part_ref)

        part_ref[0:1, :] += jnp.sum(dy * nh, axis=0, keepdims=True)
        part_ref[1:2, :] += jnp.sum(dn * out, axis=0, keepdims=True)
        part_ref[2:3, :] += jnp.broadcast_to(loss, (1, d))

    tile = lambda cols: pl.BlockSpec((t, cols), lambda i: (i, 0))
    full = lambda shape: pl.BlockSpec(shape, lambda i: (0,) * len(shape))
    return _call(
        body, name="mix_forward",
        grid=(nt,),
        out_shape=(jax.ShapeDtypeStruct((l, d), F32), jax.ShapeDtypeStruct((d, l), BF16),
                   jax.ShapeDtypeStruct((l, d), BF16), jax.ShapeDtypeStruct((8, d), F32)),
        in_specs=[tile(d), tile(d)] + _p_specs(t, w, nt) + [tile(w), tile(w),
                  pl.BlockSpec((d, d), lambda i: (0, 0), pipeline_mode=pl.Buffered(1)),
                  full(ov.shape), full(wca.shape), full(perm.shape)],
        out_specs=(tile(d), pl.BlockSpec((d, t), lambda i: (0, i)), tile(d), full((8, d))),
        compiler_params=_params(("arbitrary",)),
    )(x, tgt, p, p, p, p, p, hf, hr, wo, ov, wca, perm)


def _mix_backward(dout, p, hf, hr, wo, wca, perm, g_wout, l, t):
    d = dout.shape[1]
    w = d // 2
    nt = l // t
    la = p.shape[0]

    def body(do_ref, b_ref, c_ref, u_ref, g_ref, q_ref, hf_ref, hr_ref, wo_ref, wca_ref, perm_ref, gw_ref,
             dp_ref, dh_ref, part_ref, sc_ref, send_sems, recv_sems):
        i = pl.program_id(0)
        copies = _scatter_copies(gw_ref, sc_ref, send_sems, recv_sems)

        @pl.when(i == 0)
        def _():
            part_ref[...] = jnp.zeros_like(part_ref)
            for cp in copies:
                cp.start()

        @pl.when(i == nt)
        def _():
            dp_ref[...] = jnp.zeros_like(dp_ref)
            _exchange_wait(copies, copies[1:])

        @pl.when(i < nt)
        def _():
            bl, cl, ul, gl, ql, tt, z, sig_g, sig_q, ylru, pos, rowlen = _mix_gates(
                (b_ref, c_ref, u_ref, g_ref, q_ref), hf_ref, hr_ref, wca_ref, perm_ref, t, w)
            do = do_ref[...]
            dya = _dot_nt(do, wo_ref[0:w, :])
            dyb = _dot_nt(do, wo_ref[w:, :])
            sg = gl * sig_g
            dz = dya * bl * sg
            dt = _conv3_t(dz, wca_ref, pos, rowlen)
            dp_ref[:, 0:w] = (dya * z * sg).astype(BF16)
            dp_ref[:, w:2 * w] = (dt * ul).astype(BF16)
            dp_ref[:, 2 * w:3 * w] = (dt * cl).astype(BF16)
            dp_ref[:, 3 * w:4 * w] = (dya * bl * z * (sig_g * (1.0 + gl * (1.0 - sig_g)))).astype(BF16)
            dp_ref[:, 4 * w:5 * w] = jnp.zeros((t, w), BF16)
            dp_ref[:, 5 * w:6 * w] = (dyb * ylru * (sig_q * (1.0 + ql * (1.0 - sig_q)))).astype(BF16)
            dh_ref[...] = _dot(perm_ref[0], (dyb * (ql * sig_q)).astype(BF16)).astype(BF16)
            part_ref[0:1, :] += jnp.sum(dz * _down(tt, 1, pos), axis=0, keepdims=True)
            part_ref[1:2, :] += jnp.sum(dz * tt, axis=0, keepdims=True)
            part_ref[2:3, :] += jnp.sum(dz * _up(tt, 1, pos, rowlen), axis=0, keepdims=True)

    clamp = lambda cols: pl.BlockSpec((t, cols), lambda i: (jnp.minimum(i, nt - 1), 0))
    full = lambda shape: pl.BlockSpec(shape, lambda i: (0,) * len(shape))
    return _call(
        body, name="mix_backward",
        grid=(nt + 1,),
        out_shape=(jax.ShapeDtypeStruct((la, 6 * w), BF16), jax.ShapeDtypeStruct((l, w), BF16),
                   jax.ShapeDtypeStruct((8, w), F32), jax.ShapeDtypeStruct(g_wout.shape, g_wout.dtype)),
        in_specs=[clamp(d)] + _p_specs(t, w, nt) + [clamp(w), clamp(w),
                  pl.BlockSpec((d, d), lambda i: (0, 0), pipeline_mode=pl.Buffered(1)), full(wca.shape),
                  full(perm.shape), ANY],
        out_specs=(pl.BlockSpec((t, 6 * w), lambda i: (i, 0)), clamp(w), full((8, w)), ANY),
        scratch_shapes=[pltpu.SemaphoreType.DMA((NDEV,)), pltpu.SemaphoreType.DMA((NDEV,))],
        compiler_params=_params(("arbitrary",)),
    )(dout, p, p, p, p, p, hf, hr, wo, wca, perm, g_wout)


def _lru_backward(direction, xb, dhs, hs, wg, lv, l, t, conv=None):
    la, w = hs.shape
    gc = wg.shape[2]
    ng = w // gc
    nt = l // t
    nblk8 = la // 8
    last = conv is not None
    assert last == (direction == 1)

    if direction == 0:
        tile = lambda i: jnp.where(i == nt, nt, nt - 1 - i)
        halo = lambda i: jnp.where(tile(i) == 0, nblk8 - 1, tile(i) * (t // 8) - 1)
    else:
        tile = lambda i: i
        halo = lambda i: jnp.minimum((i + 1) * (t // 8), nblk8 - 1)

    def body(*refs):
        x_ref, dh_ref, hs_ref, halo_ref, wg_ref, lv_ref = refs[:6]
        if last:
            v_ref, wcb_ref, tm_ref, bm_ref, dxo_ref = refs[6:11]
        out_ref, dwg_ref, part_ref, a_s, dh_s, g_s, carry = refs[-7:]
        i = pl.program_id(0)
        is_ctx = i == nt

        @pl.when(i == 0)
        def _():
            carry[...] = jnp.zeros_like(carry)
            dwg_ref[...] = jnp.zeros_like(dwg_ref)
            part_ref[...] = jnp.zeros_like(part_ref)

        xb = x_ref[...]
        lam = lv_ref[3 * direction + 2:3 * direction + 3, :]
        a, s, r, ig, sp = _lru_coef(xb, wg_ref, direction, lv_ref[3 * direction:3 * direction + 1, :],
                                    lv_ref[3 * direction + 1:3 * direction + 2, :], lam, gc)
        hs_t = hs_ref[...]
        r8 = _rows((8, w))
        if direction == 0:
            edge = jnp.where(is_ctx, 0.0, halo_ref[7:8, :])
            first = jnp.where(r8 == 0, edge, pltpu.roll(hs_t[t - 8:, :], 1, 0))
            hprev = jnp.concatenate([first, hs_t[:t - 8, :]], axis=0)
        else:
            edge = jnp.where(is_ctx, 0.0, halo_ref[0:1, :])
            final = jnp.where(r8 == 7, edge, pltpu.roll(hs_t[:8, :], 7, 0))
            hprev = jnp.concatenate([hs_t[8:, :], final], axis=0)
        a_s[...] = a
        dh_s[...] = jnp.where(is_ctx, 0.0, dh_ref[...].astype(F32))
        carry[...] = _scan_tile_backward(a_s, dh_s, g_s, carry[...], direction == 0)

        g = g_s[...]
        ix = ig * xb
        gs = g * s
        dla = (g * a) * (hprev - ix * (a / s))
        dxb = gs * ig
        dzr = dla * (r * (1.0 - r)) * (-LRU_C * sp)
        dzi = gs * ix * (1.0 - ig)
        part_ref[0:1, :] += jnp.sum(dzr, axis=0, keepdims=True)
        part_ref[1:2, :] += jnp.sum(dzi, axis=0, keepdims=True)
        part_ref[2:3, :] += jnp.sum(dla * r, axis=0, keepdims=True) * (LRU_C * _sigmoid(-lam))
        pieces = []
        for gi in range(ng):
            sl = slice(gi * gc, (gi + 1) * gc)
            dz = jnp.concatenate([dzr[:, sl], dzi[:, sl]], axis=-1).astype(BF16)
            pieces.append(_dot_nt(dz, wg_ref[direction, gi]))
            dwg_ref[gi] += _dot(xb[:, sl].T.astype(BF16), dz)
        dxb = dxb + (pieces[0] if ng == 1 else jnp.concatenate(pieces, axis=-1))
        if not last:
            out_ref[...] = dxb
        else:
            dxb = dxb + dxo_ref[...]
            dxb16, v16 = dxb.astype(BF16), v_ref[...]
            dv = wcb_ref[0:1, :] * _dot(bm_ref[0], dxb16)
            for j in range(1, 4):
                dv = dv + wcb_ref[j:j + 1, :] * _dot(bm_ref[j], dxb16)
            out_ref[...] = dv.astype(BF16)
            part_ref[3:4, :] += jnp.sum(dxb, axis=0, keepdims=True)
            for j in range(4):
                part_ref[4 + j:5 + j, :] += jnp.sum(dxb * _dot(tm_ref[j], v16), axis=0, keepdims=True)

    full = lambda shape: pl.BlockSpec(shape, lambda i: (0,) * len(shape))
    kind = lambda i: (jnp.where(i == nt, 1, 0), 0, 0, 0)
    in_specs = [pl.BlockSpec((t, w), lambda i: (tile(i), 0)),
                pl.BlockSpec((t, w), lambda i: (jnp.minimum(tile(i), nt - 1), 0)),
                pl.BlockSpec((t, w), lambda i: (tile(i), 0)),
                pl.BlockSpec((8, w), lambda i: (halo(i), 0)),
                full(wg.shape), full(lv.shape)]
    args = [xb, dhs, hs, hs, wg, lv]
    if last:
        p, wcb, taps_m, back_m, dxb_other, dp = conv
        in_specs += [pl.BlockSpec((t, w), lambda i: (tile(i), 4)), full(wcb.shape),
                     pl.BlockSpec((None, 4, t, t), kind), pl.BlockSpec((None, 4, t, t), kind),
                     pl.BlockSpec((t, w), lambda i: (tile(i), 0)), ANY]
        args += [p, wcb, taps_m, back_m, dxb_other, dp]
        out0 = jax.ShapeDtypeStruct(dp.shape, dp.dtype)
        spec0 = pl.BlockSpec((t, w), lambda i: (tile(i), 4))
        aliases = {11: 0}
    else:
        out0 = jax.ShapeDtypeStruct((la, w), F32)
        spec0 = pl.BlockSpec((t, w), lambda i: (tile(i), 0))
        aliases = {}
    return _call(
        body, name="lru_backward_%d" % direction,
        grid=(nt + 1,),
        out_shape=(out0, jax.ShapeDtypeStruct((ng, gc, 2 * gc), F32), jax.ShapeDtypeStruct((8, w), F32)),
        in_specs=in_specs,
        out_specs=(spec0, full((ng, gc, 2 * gc)), full((8, w))),
        scratch_shapes=[pltpu.VMEM((t, w), F32), pltpu.VMEM((t, w), F32), pltpu.VMEM((t, w), F32),
                        pltpu.VMEM((8, w), F32)],
        input_output_aliases=aliases,
        compiler_params=_params(("arbitrary",)),
    )(*args)


def _weight_grad_t(at, b, nblk_m, nblk_n, tk, name):
    m, k = at.shape
    n = b.shape[1]
    bm, bn = m // nblk_m, n // nblk_n
    nk = k // tk

    def body(a_ref, b_ref, o_ref, acc):
        kk = pl.program_id(2)

        @pl.when(kk == 0)
        def _():
            acc[...] = jnp.zeros_like(acc)

        acc[...] += _dot(a_ref[...], b_ref[...])

        @pl.when(kk == nk - 1)
        def _():
            o_ref[...] = acc[...].astype(BF16)

    return _call(
        body, name=name,
        grid=(nblk_m, nblk_n, nk),
        out_shape=jax.ShapeDtypeStruct((nblk_m * nblk_n, bm, bn), BF16),
        in_specs=[pl.BlockSpec((bm, tk), lambda i, j, kk: (i, kk)),
                  pl.BlockSpec((tk, bn), lambda i, j, kk: (kk, j))],
        out_specs=pl.BlockSpec((None, bm, bn), lambda i, j, kk: (i * nblk_n + j, 0, 0)),
        scratch_shapes=[pltpu.VMEM((bm, bn), F32)],
        compiler_params=_params(("arbitrary", "arbitrary", "arbitrary")),
    )(at, b)


def _weight_grad_scatter(at, b, tk, name):
    m, k = at.shape
    n = b.shape[1]
    bn = n // NDEV
    nk = k // tk
    where = jnp.stack([_idx(_my_pos()), lax.axis_index("c")]).astype(jnp.int32)

    def body(w_ref, a_ref, b_ref, recv_ref, acc, sbuf, sib, sib_send, sib_recv, chip_send, chip_recv, keep_sem):
        s, kk = pl.program_id(0), pl.program_id(1)
        x, y, c = _my_pos()

        @pl.when(kk == 0)
        def _():
            acc[...] = _dot(a_ref[...], b_ref[...])

        @pl.when(kk > 0)
        def _():
            acc[...] += _dot(a_ref[...], b_ref[...])

        def to_sibling(j):
            return pltpu.make_async_remote_copy(
                src_ref=sbuf.at[0], dst_ref=sib.at[j], send_sem=sib_send.at[j], recv_sem=sib_recv.at[j],
                device_id=(x, y, 1 - c), device_id_type=MESH)

        def to_chip(j):
            dist = _chip_order(j, c)
            return pltpu.make_async_remote_copy(
                src_ref=sbuf.at[1], dst_ref=recv_ref.at[dist // 2], send_sem=chip_send.at[j],
                recv_sem=chip_recv.at[dist // 2], device_id=_peer_at(dist), device_id_type=MESH)

        keep = pltpu.make_async_copy(sbuf.at[1], recv_ref.at[0], keep_sem)
        sends = []
        for j in range(4):
            sends += [to_sibling(j), to_chip(j) if j < 3 else keep]

        for st in range(NDEV):
            @pl.when((kk == nk - 1) & (s == st))
            def _(st=st):
                if st >= 2:
                    sends[st - 2].wait_send()
                part = acc[...]
                if st % 2 == 1:
                    to_sibling(st // 2).wait_recv()
                    part = part + sib[st // 2].astype(F32)
                sbuf[st % 2] = part.astype(BF16)
                sends[st].start()
                if st == NDEV - 1:
                    sends[st - 1].wait_send()
                    sends[st].wait()
                    for j in range(1, 4):
                        pltpu.make_async_remote_copy(
                            src_ref=sbuf.at[0], dst_ref=recv_ref.at[j], send_sem=chip_send.at[0],
                            recv_sem=chip_recv.at[j], device_id=_peer_at(2 * j), device_id_type=MESH).wait_recv()

    blk = lambda s, w_ref: w_ref[0] ^ _scatter_order(s, w_ref[1])
    return _call(
        body, name=name,
        grid_spec=pltpu.PrefetchScalarGridSpec(
            num_scalar_prefetch=1, grid=(NDEV, nk),
            in_specs=[pl.BlockSpec((m, tk), lambda s, kk, w_ref: (0, kk)),
                      pl.BlockSpec((tk, bn), lambda s, kk, w_ref: (kk, blk(s, w_ref)))],
            out_specs=ANY,
            scratch_shapes=[pltpu.VMEM((m, bn), F32), pltpu.VMEM((2, m, bn), BF16), pltpu.VMEM((4, m, bn), BF16),
                            pltpu.SemaphoreType.DMA((4,)), pltpu.SemaphoreType.DMA((4,)),
                            pltpu.SemaphoreType.DMA((4,)), pltpu.SemaphoreType.DMA((4,)),
                            pltpu.SemaphoreType.DMA]),
        out_shape=jax.ShapeDtypeStruct((4, m, bn), BF16),
        compiler_params=_params(("arbitrary", "arbitrary")),
    )(where, at, b)


def _input_backward(dp, w_all, src, mv, row0, tm, nbk, name, dn=None):
    rows, d = src.shape
    nb, _, bw = w_all.shape
    nk = nb // nbk
    ni = rows // tm
    blk0 = row0 // tm
    latent = dn is not None

    def body(*refs):
        dp_ref, w_ref, x_ref, mv_ref = refs[:4]
        outs = refs[4 + latent:]
        part_ref, acc = outs[latent], outs[latent + 1]
        i, k = pl.program_id(0), pl.program_id(1)

        def product():
            step = _dot_nt(dp_ref[:, 0:bw], w_ref[0])
            for q in range(1, nbk):
                step = step + _dot_nt(dp_ref[:, q * bw:(q + 1) * bw], w_ref[q])
            return step

        def finish(slot):
            xf = x_ref[...]
            r = lax.rsqrt(jnp.mean(xf * xf, axis=-1, keepdims=True) + EPS)
            xn = xf * r
            dhl = acc[slot]
            gain, sc = mv_ref[0:1, :], mv_ref[1:2, :]
            dhx = jnp.sum(dhl * xn, axis=0, keepdims=True)
            part_ref[0:1, :] += jnp.sum(dhl, axis=0, keepdims=True)
            part_ref[1:2, :] += dhx * gain
            part_ref[2:3, :] += dhx * (1.0 + sc)
            if latent:
                dxn = dhl * (gain * (1.0 + sc))
                outs[0][...] = refs[4][...] + r * (dxn - xn * jnp.mean(dxn * xn, axis=-1, keepdims=True))

        @pl.when((i == 0) & (k == 0))
        def _():
            part_ref[...] = jnp.zeros_like(part_ref)
            acc[0] = product()

        @pl.when((i > 0) & (i < ni) & (k == 0))
        def _():
            acc[i % 2] = product()
            finish((i - 1) % 2)

        @pl.when((i == ni) & (k == 0))
        def _():
            finish((ni - 1) % 2)

        @pl.when((i < ni) & (k > 0))
        def _():
            acc[i % 2] += product()

    tile = pl.BlockSpec((tm, d), lambda i, k: (jnp.maximum(i - 1, 0), 0))
    vec = pl.BlockSpec((8, d), lambda i, k: (0, 0))
    kblock = lambda i, k: jnp.where(i == ni, nk - 1, k)
    return _call(
        body, name=name,
        grid=(ni + 1, nk),
        out_shape=((jax.ShapeDtypeStruct((rows, d), F32),) if latent else ()) + (jax.ShapeDtypeStruct((8, d), F32),),
        in_specs=[pl.BlockSpec((tm, nbk * bw), lambda i, k: (blk0 + jnp.minimum(i, ni - 1), kblock(i, k))),
                  pl.BlockSpec((nbk, d, bw), lambda i, k: (kblock(i, k), 0, 0)), tile, vec]
                 + ([tile] if latent else []),
        out_specs=((tile,) if latent else ()) + (vec,),
        scratch_shapes=[pltpu.VMEM((2, tm, d), F32)],
        compiler_params=_params(("arbitrary", "arbitrary")),
    )(*([dp, w_all, src, mv] + ([dn] if latent else [])))


def _adamw_scattered(parts, w, m, v, tr):
    r, c = w.shape
    nslot = parts.shape[0]

    def body(p_ref, w_ref, m_ref, v_ref, g_ref, d_ref, m2_ref, v2_ref):
        g = p_ref[0].astype(F32)
        for k in range(1, nslot):
            g = g + p_ref[k].astype(F32)
        g_ref[...] = g
        d_ref[...], m2_ref[...], v2_ref[...] = _adamw(w_ref[...], g, m_ref[...], v_ref[...])

    tile = pl.BlockSpec((tr, c), lambda i: (i, 0))
    return _call(
        body, name="adamw_scattered_%dx%d" % (r, c),
        grid=(r // tr,),
        out_shape=tuple(jax.ShapeDtypeStruct((r, c), F32) for _ in range(4)),
        in_specs=[pl.BlockSpec((nslot, tr, c), lambda i: (0, i, 0)), tile, tile, tile],
        out_specs=(tile,) * 4,
        compiler_params=_params(("arbitrary",)),
    )(parts, w, m, v)


def _adamw_ada(st, dmod, w, m, v, tr):
    r, c = w.shape

    def body(s_ref, dm_ref, w_ref, m_ref, v_ref, g_ref, d_ref, m2_ref, v2_ref):
        g = jnp.dot(s_ref[...], dm_ref[...], precision=HIGHEST, preferred_element_type=F32)
        g_ref[...] = g
        d_ref[...], m2_ref[...], v2_ref[...] = _adamw(w_ref[...], g, m_ref[...], v_ref[...])

    tile = pl.BlockSpec((tr, c), lambda i: (i, 0))
    return _call(
        body, name="adamw_ada",
        grid=(r // tr,),
        out_shape=tuple(jax.ShapeDtypeStruct((r, c), F32) for _ in range(4)),
        in_specs=[pl.BlockSpec((tr, 16), lambda i: (i, 0)), pl.BlockSpec((16, c), lambda i: (0, 0)),
                  tile, tile, tile],
        out_specs=(tile,) * 4,
        compiler_params=_params(("arbitrary",)),
    )(st, dmod, w, m, v)


def _adamw_small(gs, ws, ms, vs):
    n = len(ws)

    def body(*refs):
        for j in range(n):
            g_ref, w_ref, m_ref, v_ref = refs[j], refs[n + j], refs[2 * n + j], refs[3 * n + j]
            d_ref, m2_ref, v2_ref = refs[4 * n + j], refs[5 * n + j], refs[6 * n + j]
            d_ref[...], m2_ref[...], v2_ref[...] = _adamw(w_ref[...], g_ref[...], m_ref[...], v_ref[...])

    shapes = tuple(jax.ShapeDtypeStruct(a.shape, F32) for a in ws)
    out = _call(
        body, name="adamw_small",
        out_shape=shapes * 3,
        in_specs=[VMEM] * (4 * n), out_specs=(VMEM,) * (3 * n),
        compiler_params=_params(),
    )(*gs, *ws, *ms, *vs)
    return list(out[:n]), list(out[n:2 * n]), list(out[2 * n:])


def _blockdiag_groups(wh, gc):
    h, dh, _ = wh.shape
    g = gc // dh
    w4 = wh.reshape(h // g, g, dh, dh)
    bd = jnp.einsum("ngij,gh->ngihj", w4, jnp.eye(g, dtype=wh.dtype))
    return bd.reshape(h // g, gc, gc)


def _blockdiag_extract(bd, dh):
    ng, gc, _ = bd.shape
    g = gc // dh
    x = bd.reshape(ng, g, dh, g, dh)
    return jnp.einsum("ngihj,gh->ngij", x, jnp.eye(g, dtype=bd.dtype)).reshape(ng * g, dh, dh)


def _rows8(*vecs):
    rows = [jnp.reshape(v, (1, -1)).astype(F32) for v in vecs]
    n = rows[0].shape[1]
    return jnp.concatenate(rows + [jnp.zeros((8 - len(rows), n), F32)], axis=0)


def _pack(pieces):
    flat = jnp.concatenate([jnp.reshape(a, (-1,)).astype(F32) for a in pieces])
    total = -(-flat.shape[0] // 1024) * 1024
    return jnp.pad(flat, (0, total - flat.shape[0])).reshape(total // 128, 128)


def _unpack(packed, shapes):
    flat = packed.reshape(-1)
    out, off = [], 0
    for s in shapes:
        n = 1
        for q in s:
            n *= q
        out.append(flat[off:off + n].reshape(s))
        off += n
    return out


def kernel(x, c, ctx, c_ctx, norm_g, w_ada, b_ada, w_in, w_conv_a, w_conv_b, b_conv_b, lru_wa, lru_ba, lru_wx, lru_bx, lru_lambda, w_out, final_g, loss_target, m_c_ctx, m_norm_g, m_w_ada, m_b_ada, m_w_in, m_w_conv_a, m_w_conv_b, m_b_conv_b, m_lru_wa, m_lru_ba, m_lru_wx, m_lru_bx, m_lru_lambda, m_w_out, m_final_g, v_c_ctx, v_norm_g, v_w_ada, v_b_ada, v_w_in, v_w_conv_a, v_w_conv_b, v_b_conv_b, v_lru_wa, v_lru_ba, v_lru_wx, v_lru_bx, v_lru_lambda, v_w_out, v_final_g):
    _, l, d = x.shape
    lc = ctx.shape[1]
    w = d // 2
    t = lc
    assert l % t == 0 and t % GRID_W == 0 and t % 128 == 0
    dh = w // N_HEADS
    gc = min(w, MXU_WIDTH)
    cols = w_ada.shape[2]
    wo_rows = w_out.shape[1]
    me = _idx(_my_pos())
    x2, ctx2, tgt2 = x[0], ctx[0], loss_target[0]
    w_ada2, w_in2, w_out2 = w_ada[0], w_in[0], w_out[0]

    small_mine = jnp.concatenate([a.reshape(-1) for a in (w_conv_a, w_conv_b, lru_ba, lru_bx, lru_lambda)]
                                 + [jnp.zeros((3 * (w // NDEV),), F32)]).reshape(16, w // NDEV)
    mod_all, s_mat, small_all = _mod_forward(
        jnp.broadcast_to(c, (8, d)), jnp.broadcast_to(c_ctx[None], (8, d)), w_ada2, small_mine)
    mod = jnp.transpose(mod_all, (1, 0, 2)).reshape(16, NDEV * cols) + b_ada
    mod_lat = lax.dynamic_slice_in_dim(mod, me, 1, axis=0)
    sh_l, sc_l, gt_l = jnp.split(mod_lat, 3, axis=-1)
    sh_c, sc_c, _ = jnp.split(mod[8:9], 3, axis=-1)
    small = jnp.transpose(small_all, (1, 0, 2)).reshape(16, w)
    wca = _rows8(*[small[j] for j in range(0, 3)])
    wcb = _rows8(*[small[j] for j in range(3, 7)], b_conv_b)
    lv = _rows8(small[7], small[9], small[11], small[8], small[10], small[12])
    wg = jnp.stack([
        jnp.concatenate([_blockdiag_groups(lru_wa[0, dr], gc), _blockdiag_groups(lru_wx[0, dr], gc)], axis=-1)
        for dr in range(2)]).astype(BF16)

    la = l + lc
    tm = 2 * t if l % (2 * t) == 0 else t
    tk = 3 * t if la % (3 * t) == 0 else t
    h, hlt = _normalize(x2, _rows8(norm_g, sc_l, sh_l), la, 0, tm, "normalize")
    h, hlt = _normalize(ctx2, _rows8(norm_g, sc_c, sh_c), la, l, t, "normalize_ctx", prev=(h, hlt))
    p, w_all = _in_projection(h, w_in2.astype(BF16), la // 8 if la % 128 == 0 else tk)
    taps_m, back_m, perm = _scan_matrices(t)
    xb = _conv_input(p, wcb, taps_m, l, t)
    hf, hr, wo_all = _lru_forward(xb, wg, lv, w_out2.astype(BF16), l, t)
    wo = wo_all.reshape(d, d)
    dn, catt, dout, part_mix = _mix_forward(x2, tgt2, p, hf, hr, wo, _rows8(gt_l, final_g), wca, perm, t)
    g_wout = _weight_grad_t(catt, dout, 2, 1, 4 * t if l % (4 * t) == 0 else t, "grad_w_out")
    dp, dhs, part_ca, sc_wout = _mix_backward(dout, p, hf, hr, wo, wca, perm, g_wout.reshape(NDEV, wo_rows, d), l, t)
    dxb0, dwg0, part_l0 = _lru_backward(0, xb, dhs, hf, wg, lv, l, t)
    dp, dwg1, part_l1 = _lru_backward(1, xb, dhs, hr, wg, lv, l, t, conv=(p, wcb, taps_m, back_m, dxb0, dp))
    sc_win = _weight_grad_scatter(hlt, dp, tk, "grad_w_in")
    grad_x, part_lat = _input_backward(dp, w_all, x2, _rows8(norm_g, sc_l), 0, tm, 2, "input_backward", dn=dn)
    (part_ctx,) = _input_backward(dp, w_all, ctx2, _rows8(norm_g, sc_c), l, t, 2, "input_backward_ctx")
    part_in = jnp.concatenate([part_lat[0:2], part_ctx[0:2], (part_lat[2] + part_ctx[2])[None]], axis=0)

    dwa = jnp.stack([_blockdiag_extract(dwg0[:, :, :gc], dh), _blockdiag_extract(dwg1[:, :, :gc], dh)])
    dwx = jnp.stack([_blockdiag_extract(dwg0[:, :, gc:], dh), _blockdiag_extract(dwg1[:, :, gc:], dh)])
    lru_part = jnp.stack([dwa, dwx]).reshape(NDEV, -1, 128)
    zeros_d = jnp.zeros((d,), F32)
    pieces = [
        jnp.concatenate([part_in[0], part_in[1], part_mix[1]]),
        jnp.concatenate([part_in[2], part_in[3], zeros_d]),
        part_in[4], part_mix[0], part_ca[0:3], part_l1[4:8], part_l1[3],
        jnp.stack([part_l0[0], part_l1[0]]), jnp.stack([part_l0[1], part_l1[1]]),
        jnp.stack([part_l0[2], part_l1[2]]), part_mix[2, 0:1],
    ]
    shapes = [(3 * d,), (3 * d,), (d,), (d,), (3, w), (4, w), (w,), (2, w), (2, w), (2, w), (1,)]
    sig_cc = jax.nn.sigmoid(c_ctx)
    dsilu_cc = jnp.broadcast_to((sig_cc * (1.0 + c_ctx * (1.0 - sig_cc)))[None], (8, d))
    psum, pall, lru_sum, g_cctx8 = _reduce_small(_pack(pieces), lru_part, w_ada2, dsilu_cc)
    (g_modl, g_modc, g_norm, g_final, g_ca, g_cb, g_bcb, g_ba, g_bx, g_lam, loss1) = _unpack(psum, shapes)
    loss = loss1[0]
    g_cctx = g_cctx8[0]
    g_bada = (g_modl + g_modc)[None]
    g_lru = lru_sum.reshape(2, 2, N_HEADS, dh, dh)
    g_wa, g_wx = g_lru[0][None], g_lru[1][None]
    wsl = w // NDEV
    mine = lambda a: lax.dynamic_slice_in_dim(a, me * wsl, wsl, axis=-1)
    g_ca_m, g_cb_m, g_ba_m, g_bx_m, g_lam_m = (mine(g_ca)[None], mine(g_cb)[None], mine(g_ba)[None],
                                               mine(g_bx)[None], mine(g_lam)[None])
    g_norm, g_bcb = g_norm[None], g_bcb[None]

    cb = cols // 128
    per_dev = pall[:, :3 * d // 128].reshape(NDEV, NDEV, cols)
    dmod_lat = lax.dynamic_slice_in_dim(per_dev, me, 1, axis=1)[:, 0]
    dmod_ctx = lax.dynamic_slice_in_dim(g_modc.reshape(NDEV, cols), me, 1, axis=0)
    dmod16 = jnp.concatenate([dmod_lat, dmod_ctx, jnp.zeros((7, cols), F32)], axis=0)
    tr_ada = 256 if d % 256 == 0 else d
    g_wada, d_wada, m_wada, v_wada = _adamw_ada(s_mat.T, dmod16, w_ada2, m_w_ada[0], v_w_ada[0], tr_ada)
    g_win2, d_win, m_win, v_win = _adamw_scattered(sc_win, w_in2, m_w_in[0], v_w_in[0], tr_ada)
    tr_out = 64 if wo_rows % 64 == 0 else wo_rows
    g_wout2, d_wout, m_wout, v_wout = _adamw_scattered(sc_wout, w_out2, m_w_out[0], v_w_out[0], tr_out)

    small_w = [c_ctx, norm_g, b_ada, w_conv_a, w_conv_b, b_conv_b, lru_wa, lru_ba, lru_wx, lru_bx, lru_lambda, final_g]
    small_m = [m_c_ctx, m_norm_g, m_b_ada, m_w_conv_a, m_w_conv_b, m_b_conv_b, m_lru_wa, m_lru_ba, m_lru_wx,
               m_lru_bx, m_lru_lambda, m_final_g]
    small_v = [v_c_ctx, v_norm_g, v_b_ada, v_w_conv_a, v_w_conv_b, v_b_conv_b, v_lru_wa, v_lru_ba, v_lru_wx,
               v_lru_bx, v_lru_lambda, v_final_g]
    small_g = [g_cctx, g_norm, g_bada, g_ca_m, g_cb_m, g_bcb, g_wa, g_ba_m, g_wx, g_bx_m, g_lam_m, g_final]
    small_g = [jnp.reshape(a, b.shape) for a, b in zip(small_g, small_w)]
    d_s, m_s, v_s = _adamw_small(small_g, small_w, small_m, small_v)

    def weights(small_list, ada, win, wout):
        (cctx_, norm_, bada_, ca_, cb_, bcb_, wa_, ba_, wx_, bx_, lam_, final_) = small_list
        return [cctx_, norm_, ada[None], bada_, win[None], ca_, cb_, bcb_, wa_, ba_, wx_, bx_, lam_, wout[None], final_]

    return (loss, grad_x[None],
            *weights(small_g, g_wada, g_win2, g_wout2), *weights(d_s, d_wada, d_win, d_wout),
            *weights(m_s, m_wada, m_win, m_wout), *weights(v_s, v_wada, v_win, v_wout))
```

```python
import functools

import jax
import jax.numpy as jnp
import numpy as np
from jax import lax
from jax.experimental import pallas as pl
from jax.experimental.pallas import tpu as pltpu

F32 = jnp.float32
BF16 = jnp.bfloat16
MESH = pl.DeviceIdType.MESH
NDEV = 8
GRID_W = 64
N_HEADS = 16
LRU_C = 8.0
EPS = 1e-6
MXU_WIDTH = 256
VMEM_LIMIT = 60 * 1024 * 1024

ADAM_LR = 0.001
ADAM_B1 = 0.9
ADAM_B2 = 0.999
ADAM_EPS = 1e-08
ADAM_WD = 0.01
ADAM_STEP = 10
ADAM_C1 = 1.0 - ADAM_B1 ** ADAM_STEP
ADAM_C2 = 1.0 - ADAM_B2 ** ADAM_STEP

HIGHEST = lax.Precision.HIGHEST
ANY = pl.BlockSpec(memory_space=pl.ANY)
VMEM = pl.BlockSpec(memory_space=pltpu.VMEM)


def _call(body, **kw):
    return pl.pallas_call(body, **kw)


def _params(sem=None, vmem=VMEM_LIMIT):
    return pltpu.CompilerParams(dimension_semantics=sem, vmem_limit_bytes=vmem)


def _my_pos():
    return lax.axis_index("x"), lax.axis_index("y"), lax.axis_index("c")


def _idx(pos):
    return 4 * pos[0] + 2 * pos[1] + pos[2]


def _peer(k):
    x, y, c = _my_pos()
    return ((1 - x) if (k >> 2) & 1 else x, (1 - y) if (k >> 1) & 1 else y, (1 - c) if k & 1 else c)


def _exchange_start(src_ref, dst_ref, send_sems, recv_sems, base):
    me = _idx(_my_pos())
    sends = []
    for k in range(1, NDEV):
        cp = pltpu.make_async_remote_copy(
            src_ref=src_ref, dst_ref=dst_ref.at[me], send_sem=send_sems.at[base + k - 1],
            recv_sem=recv_sems.at[base + k - 1], device_id=_peer(k), device_id_type=MESH)
        cp.start()
        sends.append(cp)
    dst_ref[me] = src_ref[...]
    return sends, (src_ref, dst_ref, send_sems, recv_sems, base)


def _exchange_finish(started):
    sends, (src_ref, dst_ref, send_sems, recv_sems, base) = started
    for k in range(1, NDEV):
        peer = _peer(k)
        pltpu.make_async_remote_copy(
            src_ref=src_ref, dst_ref=dst_ref.at[_idx(peer)], send_sem=send_sems.at[base + k - 1],
            recv_sem=recv_sems.at[base + k - 1], device_id=peer, device_id_type=MESH).wait_recv()
    for cp in sends:
        cp.wait_send()


def _exchange_vmem(src_ref, dst_ref, send_sems, recv_sems, base):
    _exchange_finish(_exchange_start(src_ref, dst_ref, send_sems, recv_sems, base))


def _sigmoid(z):
    return 0.5 * jnp.tanh(0.5 * z) + 0.5


def _softplus(x):
    return jnp.maximum(x, 0.0) + jnp.log1p(jnp.exp(-jnp.abs(x)))


def _one_minus_sq(a, la):
    series = (-2.0 * la) * (1.0 + la)
    return jnp.where(la > -0.0015, series, 1.0 - a * a)


def _dot(a, b):
    return jnp.dot(a, b, preferred_element_type=F32)


def _dot_nt(a, b):
    return lax.dot_general(a, b, (((1,), (1,)), ((), ())), preferred_element_type=F32)


def _rows(shape):
    return lax.broadcasted_iota(jnp.int32, shape, 0)


def _down(x, k, pos):
    return jnp.where(pos >= k, pltpu.roll(x, k, 0), 0.0)


def _up(x, k, pos, rowlen):
    return jnp.where(pos + k < rowlen, pltpu.roll(x, x.shape[0] - k, 0), 0.0)


def _pos_rowlen(shape, is_ctx):
    t = _rows(shape)
    pos = jnp.where(is_ctx, t, t & (GRID_W - 1))
    rowlen = jnp.where(is_ctx, shape[0], GRID_W)
    return pos, rowlen


def _scan_matrices(t):
    seg = t // 8
    r = np.arange(t)
    perm = (np.arange(t)[None, :] == ((r % 8) * seg + r // 8)[:, None]).astype(np.float32)
    rows, cols = r[:, None], r[None, :]
    taps, back = [], []
    for rowlen in (GRID_W, t):
        pos = rows % rowlen
        shift = {-2: (cols == rows - 2) & (pos >= 2), -1: (cols == rows - 1) & (pos >= 1),
                 0: cols == rows, 1: (cols == rows + 1) & (pos + 1 < rowlen),
                 2: (cols == rows + 2) & (pos + 2 < rowlen)}
        taps.append(np.stack([perm @ shift[k].astype(np.float32) for k in (-2, -1, 0, 1)]))
        back.append(np.stack([shift[k].astype(np.float32) @ perm.T for k in (2, 1, 0, -1)]))
    as_bf16 = lambda a: jnp.asarray(a, dtype=BF16)
    return as_bf16(np.stack(taps)), as_bf16(np.stack(back)), as_bf16(np.stack([perm, perm.T]))


def _conv3(t, w_ref, pos, rowlen):
    return w_ref[0:1, :] * _down(t, 1, pos) + w_ref[1:2, :] * t + w_ref[2:3, :] * _up(t, 1, pos, rowlen)


def _conv3_t(dz, w_ref, pos, rowlen):
    return w_ref[0:1, :] * _up(dz, 1, pos, rowlen) + w_ref[1:2, :] * dz + w_ref[2:3, :] * _down(dz, 1, pos)


def _chunk_scan(a, b, reverse):
    row = _rows(a.shape)
    for s in (1, 2, 4):
        if reverse:
            m = row < 8 - s
            sh = 8 - s
        else:
            m = row >= s
            sh = s
        a_s = jnp.where(m, pltpu.roll(a, sh, 0), 1.0)
        b_s = jnp.where(m, pltpu.roll(b, sh, 0), 0.0)
        b = b + a * b_s
        a = a * a_s
    return a, b


def _chain_segments(ptot, hend, carry, reverse):
    ca, cb = _chunk_scan(ptot, hend, reverse)
    incl = ca * carry + cb
    r8 = _rows(incl.shape)
    if reverse:
        start = jnp.where(r8 < 7, pltpu.roll(incl, 7, 0), carry)
        last = incl[0:1, :]
    else:
        start = jnp.where(r8 >= 1, pltpu.roll(incl, 1, 0), carry)
        last = incl[7:8, :]
    return start, jnp.broadcast_to(last, incl.shape)


def _blocks(nblock, reverse):
    order = range(nblock - 1, -1, -1) if reverse else range(nblock)
    return [slice(8 * k, 8 * k + 8) for k in order]


def _scan_tile(a_ref, b_ref, out_ref, carry, reverse):
    t, w = a_ref.shape
    seg = t // 8

    hend, ptot = jnp.zeros((8, w), F32), jnp.ones((8, w), F32)
    for rows in _blocks(seg, reverse):
        a = a_ref[rows, :]
        hend, ptot = a * hend + b_ref[rows, :], a * ptot
    h, new_carry = _chain_segments(ptot, hend, carry, reverse)
    for rows in _blocks(seg, reverse):
        h = a_ref[rows, :] * h + b_ref[rows, :]
        out_ref[rows, :] = h
    return new_carry


def _scan_tile_backward(a_ref, dh_ref, g_ref, carry, reverse):
    t, w = a_ref.shape
    seg = t // 8

    uend, ptot = jnp.zeros((8, w), F32), jnp.ones((8, w), F32)
    for rows in _blocks(seg, reverse):
        a = a_ref[rows, :]
        uend, ptot = a * (dh_ref[rows, :] + uend), a * ptot
    u, new_carry = _chain_segments(ptot, uend, carry, reverse)
    for rows in _blocks(seg, reverse):
        g = dh_ref[rows, :] + u
        g_ref[rows, :] = g
        u = a_ref[rows, :] * g
    return new_carry


def _lru_coef(xb, wg_ref, d, ba, bx, lam, gc):
    w = xb.shape[1]
    xb16 = xb.astype(BF16)
    zr, zi = [], []
    for g in range(w // gc):
        z = _dot(xb16[:, g * gc:(g + 1) * gc], wg_ref[d, g])
        zr.append(z[:, :gc])
        zi.append(z[:, gc:])
    zr = zr[0] if len(zr) == 1 else jnp.concatenate(zr, axis=-1)
    zi = zi[0] if len(zi) == 1 else jnp.concatenate(zi, axis=-1)
    r = _sigmoid(zr + ba)
    ig = _sigmoid(zi + bx)
    sp = _softplus(-lam)
    la = r * (-LRU_C * sp)
    a = jnp.exp(la)
    s = jnp.sqrt(_one_minus_sq(a, la))
    return a, s, r, ig, sp


def _adamw(w, g, m, v):
    m2 = ADAM_B1 * m + (1.0 - ADAM_B1) * g
    v2 = ADAM_B2 * v + (1.0 - ADAM_B2) * (g * g)
    m_hat = m2 / ADAM_C1
    v_hat = v2 / ADAM_C2
    delta = -ADAM_LR * (m_hat / (jnp.sqrt(v_hat) + ADAM_EPS) + ADAM_WD * w)
    return delta, m2, v2


def _mod_forward(c8, cctx8, w_ada, small):
    d = c8.shape[1]
    cols = w_ada.shape[1]

    def body(c_ref, cctx_ref, w_ref, sm_ref, mod_ref, s_ref, sm_all, cbuf, mod_my, send_sems, recv_sems):
        _exchange_vmem(sm_ref, sm_all, send_sems, recv_sems, 2 * (NDEV - 1))
        _exchange_vmem(c_ref, cbuf, send_sems, recv_sems, 0)
        row = _rows((8, d))
        c_all = jnp.zeros((8, d), F32)
        for b in range(NDEV):
            c_all = jnp.where(row == b, cbuf[b], c_all)
        cc = cctx_ref[...]
        s_top = c_all * _sigmoid(c_all)
        s_bot = jnp.where(row == 0, cc * _sigmoid(cc), 0.0)
        s = jnp.concatenate([s_top, s_bot], axis=0)
        s_ref[...] = s
        mod_my[...] = jnp.dot(s, w_ref[...], precision=HIGHEST, preferred_element_type=F32)
        _exchange_vmem(mod_my, mod_ref, send_sems, recv_sems, NDEV - 1)

    return _call(
        body, name="mod_forward",
        out_shape=(jax.ShapeDtypeStruct((NDEV, 16, cols), F32), jax.ShapeDtypeStruct((16, d), F32),
                   jax.ShapeDtypeStruct((NDEV,) + small.shape, F32)),
        in_specs=[VMEM] * 4, out_specs=(VMEM,) * 3,
        scratch_shapes=[pltpu.VMEM((NDEV, 8, d), F32), pltpu.VMEM((16, cols), F32),
                        pltpu.SemaphoreType.DMA((3 * (NDEV - 1),)), pltpu.SemaphoreType.DMA((3 * (NDEV - 1),))],
        compiler_params=_params(),
    )(c8, cctx8, w_ada, small)


def _scatter_copies(src_ref, dst_ref, send_sems, recv_sems):
    me = _idx(_my_pos())
    copies = [pltpu.make_async_copy(src_ref.at[me], dst_ref.at[0], send_sems.at[0])]
    for k in range(1, NDEV):
        peer = _peer(k)
        copies.append(pltpu.make_async_remote_copy(
            src_ref=src_ref.at[_idx(peer)], dst_ref=dst_ref.at[k], send_sem=send_sems.at[k],
            recv_sem=recv_sems.at[k], device_id=peer, device_id_type=MESH))
    return copies


def _gather_copies(src_ref, dst_ref, send_sems, recv_sems):
    me = _idx(_my_pos())
    sends = [pltpu.make_async_copy(src_ref, dst_ref.at[me], send_sems.at[0])]
    arrivals = []
    for k in range(1, NDEV):
        peer = _peer(k)
        sends.append(pltpu.make_async_remote_copy(
            src_ref=src_ref, dst_ref=dst_ref.at[me], send_sem=send_sems.at[k],
            recv_sem=recv_sems.at[k], device_id=peer, device_id_type=MESH))
        arrivals.append(pltpu.make_async_remote_copy(
            src_ref=src_ref, dst_ref=dst_ref.at[_idx(peer)], send_sem=send_sems.at[k],
            recv_sem=recv_sems.at[k], device_id=peer, device_id_type=MESH))
    return sends, arrivals


def _exchange_wait(sends, arrivals):
    sends[0].wait()
    for cp in arrivals:
        cp.wait_recv()
    for cp in sends[1:]:
        cp.wait_send()


def _chip_order(k, c):
    return (6, 4 - 2 * c, 2 + 2 * c, 0)[k]


def _scatter_order(s, c):
    k = s >> 1
    mine = jnp.where(k == 0, 6, jnp.where(k == 1, 4 - 2 * c, jnp.where(k == 2, 2 + 2 * c, 0)))
    theirs = jnp.where(k == 0, 6, jnp.where(k == 1, 2 + 2 * c, jnp.where(k == 2, 4 - 2 * c, 0))) ^ 1
    return jnp.where((s & 1) == 0, theirs, mine)


def _peer_at(dist):
    x, y, c = _my_pos()
    return (x ^ ((dist >> 2) & 1), y ^ ((dist >> 1) & 1), c ^ (dist & 1))


def _reduce_small(packed, lru_parts, w_ada, dsilu_cctx):
    rp = packed.shape[0]
    rl = lru_parts.shape[1]
    d, cols = w_ada.shape
    assert cols % 128 == 0
    cb = cols // 128

    def body(p_ref, l_ref, w_ref, ds_ref, sum_ref, all_ref, lru_ref, cctx_ref,
             lbuf, lsum, cpart, call, send_sems, recv_sems, lsend, lrecv):
        me = _idx(_my_pos())
        scattered = _scatter_copies(l_ref, lbuf, lsend, lrecv)
        for cp in scattered:
            cp.start()
        _exchange_vmem(p_ref, all_ref, send_sems, recv_sems, 0)
        acc = all_ref[0]
        for j in range(1, NDEV):
            acc = acc + all_ref[j]
        sum_ref[...] = acc
        _exchange_wait(scattered, scattered[1:])
        red = lbuf[0]
        for k in range(1, NDEV):
            red = red + lbuf[k]
        lsum[...] = red
        lru_gather = _exchange_start(lsum, lru_ref, send_sems, recv_sems, NDEV - 1)
        part = jnp.zeros((8, d), F32)
        for q in range(cb):
            dm = jnp.broadcast_to(sum_ref[pl.ds((NDEV + me) * cb + q, 1), :], (8, 128))
            part = part + lax.dot_general(dm, w_ref[:, q * 128:(q + 1) * 128],
                                          (((1,), (1,)), ((), ())), precision=HIGHEST,
                                          preferred_element_type=F32)
        cpart[...] = part
        _exchange_vmem(cpart, call, send_sems, recv_sems, 2 * (NDEV - 1))
        _exchange_finish(lru_gather)
        tot = call[0]
        for j in range(1, NDEV):
            tot = tot + call[j]
        cctx_ref[...] = tot * ds_ref[...]

    return _call(
        body, name="reduce_small",
        out_shape=(jax.ShapeDtypeStruct((rp, 128), F32), jax.ShapeDtypeStruct((NDEV, rp, 128), F32),
                   jax.ShapeDtypeStruct((NDEV, rl, 128), F32), jax.ShapeDtypeStruct((8, d), F32)),
        in_specs=[VMEM] * 4, out_specs=(VMEM,) * 4,
        scratch_shapes=[pltpu.VMEM((NDEV, rl, 128), F32), pltpu.VMEM((rl, 128), F32), pltpu.VMEM((8, d), F32),
                        pltpu.VMEM((NDEV, 8, d), F32),
                        pltpu.SemaphoreType.DMA((3 * (NDEV - 1),)), pltpu.SemaphoreType.DMA((3 * (NDEV - 1),)),
                        pltpu.SemaphoreType.DMA((NDEV,)), pltpu.SemaphoreType.DMA((NDEV,))],
        compiler_params=_params(),
    )(packed, lru_parts, w_ada, dsilu_cctx)


def _normalize(src, mv, la, row0, tm, name, prev=None):
    rows, d = src.shape
    blk0 = row0 // tm

    def body(*refs):
        x_ref, mv_ref = refs[:2]
        h_ref, ht_ref = refs[-2:]
        xf = x_ref[...]
        r = lax.rsqrt(jnp.mean(xf * xf, axis=-1, keepdims=True) + EPS)
        h = xf * r * (mv_ref[0:1, :] * (1.0 + mv_ref[1:2, :])) + mv_ref[2:3, :]
        h_ref[...] = h.astype(BF16)
        ht_ref[...] = h.T.astype(BF16)

    in_specs = [pl.BlockSpec((tm, d), lambda i: (i, 0)), pl.BlockSpec((8, d), lambda i: (0, 0))]
    args = [src, mv]
    aliases = {}
    if prev is not None:
        in_specs += [ANY, ANY]
        args += list(prev)
        aliases = {2: 0, 3: 1}
    return _call(
        body, name=name,
        grid=(rows // tm,),
        out_shape=(jax.ShapeDtypeStruct((la, d), BF16), jax.ShapeDtypeStruct((d, la), BF16)),
        in_specs=in_specs,
        out_specs=(pl.BlockSpec((tm, d), lambda i: (blk0 + i, 0)), pl.BlockSpec((d, tm), lambda i: (0, blk0 + i))),
        input_output_aliases=aliases,
        compiler_params=_params(("arbitrary",)),
    )(*args)


def _gather_order(step):
    return (step & 1) | (((step >> 2) & 1) << 1) | (((step >> 1) & 1) << 2)


def _in_projection(h, w_shard, tm):
    la, d = h.shape
    bw = w_shard.shape[1]
    ni = la // tm
    where = jnp.reshape(_idx(_my_pos()), (1,)).astype(jnp.int32)

    def body(me_ref, h_ref, w_ref, p_ref, all_ref, wbuf, send_sems, recv_sems, local_sems):
        s, i = pl.program_id(0), pl.program_id(1)
        x, y, c = _my_pos()
        me, sibling = (x, y, c), (x, y, 1 - c)
        chips = [(1 - x, y), (x, 1 - y), (1 - x, 1 - y)]

        def copy(k, block, to, from_shard=False):
            return pltpu.make_async_remote_copy(
                src_ref=w_ref if from_shard else all_ref.at[_idx(block)], dst_ref=all_ref.at[_idx(block)],
                send_sem=send_sems.at[k], recv_sem=recv_sems.at[k], device_id=to, device_id_type=MESH)

        def load(block, slot):
            return pltpu.make_async_copy(all_ref.at[_idx(block)], wbuf.at[slot], local_sems.at[1])

        keep = pltpu.make_async_copy(w_ref, all_ref.at[_idx(me)], local_sems.at[0])
        first = [copy(0, me, sibling, True)] + [copy(1 + j, me, (*chip, c), True) for j, chip in enumerate(chips)]
        passed = [copy(4 + j, (*chip, c), sibling) for j, chip in enumerate(chips)]
        steps = [(copy(0, sibling, me), None, sibling)]
        for j, chip in enumerate(chips):
            steps.append((copy(1 + j, (*chip, c), me), passed[j], (*chip, c)))
            steps.append((copy(4 + j, (*chip, 1 - c), me), None, (*chip, 1 - c)))

        @pl.when((s == 0) & (i == 0))
        def _():
            keep.start()
            mine = pltpu.make_async_copy(w_ref, wbuf.at[0], local_sems.at[1])
            mine.start()
            for cp in first:
                cp.start()
            mine.wait()

        for n, (arrival, forward, block) in enumerate(steps, start=1):
            @pl.when((s == n - 1) & (i == ni - 1))
            def _(arrival=arrival, forward=forward, block=block, n=n):
                arrival.wait_recv()
                if forward is not None:
                    forward.start()
                load(block, n % 2).start()

        @pl.when((s > 0) & (i == 0))
        def _():
            load(me, s % 2).wait()

        p_ref[...] = _dot(h_ref[...], wbuf[s % 2]).astype(BF16)

        @pl.when((s == NDEV - 1) & (i == ni - 1))
        def _():
            for cp in first + passed:
                cp.wait_send()
            keep.wait()

    return _call(
        body, name="in_projection",
        grid_spec=pltpu.PrefetchScalarGridSpec(
            num_scalar_prefetch=1, grid=(NDEV, ni),
            in_specs=[pl.BlockSpec((tm, d), lambda s, i, me_ref: (i, 0)), ANY],
            out_specs=(pl.BlockSpec((tm, bw), lambda s, i, me_ref: (i, me_ref[0] ^ _gather_order(s))), ANY),
            scratch_shapes=[pltpu.VMEM((2, d, bw), BF16), pltpu.SemaphoreType.DMA((7,)),
                            pltpu.SemaphoreType.DMA((7,)), pltpu.SemaphoreType.DMA((2,))]),
        out_shape=(jax.ShapeDtypeStruct((la, NDEV * bw), BF16), jax.ShapeDtypeStruct((NDEV, d, bw), BF16)),
        compiler_params=_params(("arbitrary", "arbitrary")),
    )(where, h, w_shard)


def _conv_input(p, wcb, taps_m, l, t):
    la = p.shape[0]
    w = wcb.shape[1]
    nt = l // t

    def body(v_ref, wcb_ref, tm_ref, xb_ref):
        v16 = v_ref[...]
        xb = wcb_ref[4:5, :] + wcb_ref[0:1, :] * _dot(tm_ref[0], v16)
        for j in range(1, 4):
            xb = xb + wcb_ref[j:j + 1, :] * _dot(tm_ref[j], v16)
        xb_ref[...] = xb

    return _call(
        body, name="conv_input",
        grid=(nt + 1,),
        out_shape=jax.ShapeDtypeStruct((la, w), F32),
        in_specs=[pl.BlockSpec((t, w), lambda i: (i, 4)), pl.BlockSpec((8, w), lambda i: (0, 0)),
                  pl.BlockSpec((None, 4, t, t), lambda i: (i // nt, 0, 0, 0))],
        out_specs=pl.BlockSpec((t, w), lambda i: (i, 0)),
        compiler_params=_params(("arbitrary",)),
    )(p, wcb, taps_m)


def _lru_forward(xb, wg, lv, wo_shard, l, t):
    la, w = xb.shape
    gc = wg.shape[2]
    nt = l // t

    def body(xf_ref, xr_ref, wg_ref, lv_ref, wo_ref, hf_ref, hr_ref, wo_all,
             a_s, b_s, carry, send_sems, recv_sems):
        sends, arrivals = _gather_copies(wo_ref, wo_all, send_sems, recv_sems)

        @pl.when(pl.program_id(0) == 0)
        def _():
            carry[...] = jnp.zeros_like(carry)
            for cp in sends:
                cp.start()

        @pl.when(pl.program_id(0) == nt)
        def _():
            _exchange_wait(sends, arrivals)

        for dr, (x_ref, h_ref) in enumerate(((xf_ref, hf_ref), (xr_ref, hr_ref))):
            x = x_ref[...]
            a, s, _, ig, _ = _lru_coef(x, wg_ref, dr, lv_ref[3 * dr:3 * dr + 1, :],
                                       lv_ref[3 * dr + 1:3 * dr + 2, :], lv_ref[3 * dr + 2:3 * dr + 3, :], gc)
            a_s[...] = a
            b_s[...] = s * (ig * x)
            carry[dr] = _scan_tile(a_s, b_s, h_ref, carry[dr], dr == 1)

    full = lambda shape: pl.BlockSpec(shape, lambda i: (0,) * len(shape))
    fmap = lambda i: (jnp.where(i == 0, nt, i - 1), 0)
    rmap = lambda i: (jnp.where(i == 0, nt, nt - i), 0)
    return _call(
        body, name="lru_forward",
        grid=(nt + 1,),
        out_shape=(jax.ShapeDtypeStruct((la, w), F32), jax.ShapeDtypeStruct((la, w), F32),
                   jax.ShapeDtypeStruct((NDEV,) + wo_shard.shape, wo_shard.dtype)),
        in_specs=[pl.BlockSpec((t, w), fmap), pl.BlockSpec((t, w), rmap), full(wg.shape), full(lv.shape), ANY],
        out_specs=(pl.BlockSpec((t, w), fmap), pl.BlockSpec((t, w), rmap), ANY),
        scratch_shapes=[pltpu.VMEM((t, w), F32), pltpu.VMEM((t, w), F32), pltpu.VMEM((2, 8, w), F32),
                        pltpu.SemaphoreType.DMA((NDEV,)), pltpu.SemaphoreType.DMA((NDEV,))],
        compiler_params=_params(("arbitrary",)),
    )(xb, xb, wg, lv, wo_shard)


def _mix_gates(p_refs, hf_ref, hr_ref, wca_ref, perm_ref, t, w):
    bl, cl, ul, gl, ql = [r[...].astype(F32) for r in p_refs]
    pos, rowlen = _pos_rowlen((t, w), False)
    tt = cl * ul
    z = _conv3(tt, wca_ref, pos, rowlen)
    sig_g = _sigmoid(gl)
    sig_q = _sigmoid(ql)
    ylru = _dot(perm_ref[1], (hf_ref[...] + hr_ref[...]).astype(BF16))
    return bl, cl, ul, gl, ql, tt, z, sig_g, sig_q, ylru, pos, rowlen


def _p_specs(t, w, nt):
    return [pl.BlockSpec((t, w), functools.partial(lambda i, s: (jnp.minimum(i, nt - 1), s), s=s))
            for s in (0, 1, 2, 3, 5)]


def _mix_forward(x, tgt, p, hf, hr, wo, ov, wca, perm, t):
    l, d = x.shape
    w = d // 2
    nt = l // t

    def body(x_ref, tg_ref, b_ref, c_ref, u_ref, g_ref, q_ref, hf_ref, hr_ref, wo_ref, ov_ref, wca_ref, perm_ref,
             dn_ref, ct_ref, do_ref, part_ref):
        i = pl.program_id(0)
        bl, _, _, gl, ql, _, z, sig_g, sig_q, ylru, _, _ = _mix_gates(
            (b_ref, c_ref, u_ref, g_ref, q_ref), hf_ref, hr_ref, wca_ref, perm_ref, t, w)
        ya = bl * z * (gl * sig_g)
        yb = ylru * (ql * sig_q)
        ct_ref[0:w, :] = ya.T.astype(BF16)
        ct_ref[w:, :] = yb.T.astype(BF16)
        out = _dot(ya.astype(BF16), wo_ref[0:w, :]) + _dot(yb.astype(BF16), wo_ref[w:, :])
        gate, fg = ov_ref[0:1, :], ov_ref[1:2, :]
        n = x_ref[...] + gate * out
        rr = lax.rsqrt(jnp.mean(n * n, axis=-1, keepdims=True) + EPS)
        nh = n * rr
        e = nh * fg - tg_ref[...]
        loss = 0.5 * jnp.sum(jnp.mean(e * e, axis=-1, keepdims=True), axis=0, keepdims=True)
        dy = e * (1.0 / d)
        dnh = dy * fg
        dn = rr * (dnh - nh * jnp.mean(dnh * nh, axis=-1, keepdims=True))
        dn_ref[...] = dn
        do_ref[...] = (dn * gate).astype(BF16)

        @pl.when(i == 0)
        def _():
            part_ref[...] = jnp.zeros_like(part_ref)

        part_ref[0:1, :] += jnp.sum(dy * nh, axis=0, keepdims=True)
        part_ref[1:2, :] += jnp.sum(dn * out, axis=0, keepdims=True)
        part_ref[2:3, :] += jnp.broadcast_to(loss, (1, d))

    tile = lambda cols: pl.BlockSpec((t, cols), lambda i: (i, 0))
    full = lambda shape: pl.BlockSpec(shape, lambda i: (0,) * len(shape))
    return _call(
        body, name="mix_forward",
        grid=(nt,),
        out_shape=(jax.ShapeDtypeStruct((l, d), F32), jax.ShapeDtypeStruct((d, l), BF16),
                   jax.ShapeDtypeStruct((l, d), BF16), jax.ShapeDtypeStruct((8, d), F32)),
        in_specs=[tile(d), tile(d)] + _p_specs(t, w, nt) + [tile(w), tile(w),
                  pl.BlockSpec((d, d), lambda i: (0, 0), pipeline_mode=pl.Buffered(1)),
                  full(ov.shape), full(wca.shape), full(perm.shape)],
        out_specs=(tile(d), pl.BlockSpec((d, t), lambda i: (0, i)), tile(d), full((8, d))),
        compiler_params=_params(("arbitrary",)),
    )(x, tgt, p, p, p, p, p, hf, hr, wo, ov, wca, perm)


def _mix_backward(dout, p, hf, hr, wo, wca, perm, g_wout, l, t):
    d = dout.shape[1]
    w = d // 2
    nt = l // t
    la = p.shape[0]

    def body(do_ref, b_ref, c_ref, u_ref, g_ref, q_ref, hf_ref, hr_ref, wo_ref, wca_ref, perm_ref, gw_ref,
             dp_ref, dh_ref, part_ref, sc_ref, send_sems, recv_sems):
        i = pl.program_id(0)
        copies = _scatter_copies(gw_ref, sc_ref, send_sems, recv_sems)

        @pl.when(i == 0)
        def _():
            part_ref[...] = jnp.zeros_like(part_ref)
            for cp in copies:
                cp.start()

        @pl.when(i == nt)
        def _():
            dp_ref[...] = jnp.zeros_like(dp_ref)
            _exchange_wait(copies, copies[1:])

        @pl.when(i < nt)
        def _():
            bl, cl, ul, gl, ql, tt, z, sig_g, sig_q, ylru, pos, rowlen = _mix_gates(
                (b_ref, c_ref, u_ref, g_ref, q_ref), hf_ref, hr_ref, wca_ref, perm_ref, t, w)
            do = do_ref[...]
            dya = _dot_nt(do, wo_ref[0:w, :])
            dyb = _dot_nt(do, wo_ref[w:, :])
            sg = gl * sig_g
            dz = dya * bl * sg
            dt = _conv3_t(dz, wca_ref, pos, rowlen)
            dp_ref[:, 0:w] = (dya * z * sg).astype(BF16)
            dp_ref[:, w:2 * w] = (dt * ul).astype(BF16)
            dp_ref[:, 2 * w:3 * w] = (dt * cl).astype(BF16)
            dp_ref[:, 3 * w:4 * w] = (dya * bl * z * (sig_g * (1.0 + gl * (1.0 - sig_g)))).astype(BF16)
            dp_ref[:, 4 * w:5 * w] = jnp.zeros((t, w), BF16)
            dp_ref[:, 5 * w:6 * w] = (dyb * ylru * (sig_q * (1.0 + ql * (1.0 - sig_q)))).astype(BF16)
            dh_ref[...] = _dot(perm_ref[0], (dyb * (ql * sig_q)).astype(BF16)).astype(BF16)
            part_ref[0:1, :] += jnp.sum(dz * _down(tt, 1, pos), axis=0, keepdims=True)
            part_ref[1:2, :] += jnp.sum(dz * tt, axis=0, keepdims=True)
            part_ref[2:3, :] += jnp.sum(dz * _up(tt, 1, pos, rowlen), axis=0, keepdims=True)

    clamp = lambda cols: pl.BlockSpec((t, cols), lambda i: (jnp.minimum(i, nt - 1), 0))
    full = lambda shape: pl.BlockSpec(shape, lambda i: (0,) * len(shape))
    return _call(
        body, name="mix_backward",
        grid=(nt + 1,),
        out_shape=(jax.ShapeDtypeStruct((la, 6 * w), BF16), jax.ShapeDtypeStruct((l, w), BF16),
                   jax.ShapeDtypeStruct((8, w), F32), jax.ShapeDtypeStruct(g_wout.shape, g_wout.dtype)),
        in_specs=[clamp(d)] + _p_specs(t, w, nt) + [clamp(w), clamp(w),
                  pl.BlockSpec((d, d), lambda i: (0, 0), pipeline_mode=pl.Buffered(1)), full(wca.shape),
                  full(perm.shape), ANY],
        out_specs=(pl.BlockSpec((t, 6 * w), lambda i: (i, 0)), clamp(w), full((8, w)), ANY),
        scratch_shapes=[pltpu.SemaphoreType.DMA((NDEV,)), pltpu.SemaphoreType.DMA((NDEV,))],
        compiler_params=_params(("arbitrary",)),
    )(dout, p, p, p, p, p, hf, hr, wo, wca, perm, g_wout)


def _lru_backward(direction, xb, dhs, hs, wg, lv, l, t, conv=None):
    la, w = hs.shape
    gc = wg.shape[2]
    ng = w // gc
    nt = l // t
    nblk8 = la // 8
    last = conv is not None
    assert last == (direction == 1)

    if direction == 0:
        tile = lambda i: jnp.where(i == nt, nt, nt - 1 - i)
        halo = lambda i: jnp.where(tile(i) == 0, nblk8 - 1, tile(i) * (t // 8) - 1)
    else:
        tile = lambda i: i
        halo = lambda i: jnp.minimum((i + 1) * (t // 8), nblk8 - 1)

    def body(*refs):
        x_ref, dh_ref, hs_ref, halo_ref, wg_ref, lv_ref = refs[:6]
        if last:
            v_ref, wcb_ref, bm_ref, dxo_ref = refs[6:10]
        out_ref, dwg_ref, part_ref, a_s, dh_s, g_s, carry = refs[-7:]
        i = pl.program_id(0)
        is_ctx = i == nt

        @pl.when(i == 0)
        def _():
            carry[...] = jnp.zeros_like(carry)
            dwg_ref[...] = jnp.zeros_like(dwg_ref)
            part_ref[...] = jnp.zeros_like(part_ref)

        xb = x_ref[...]
        lam = lv_ref[3 * direction + 2:3 * direction + 3, :]
        a, s, r, ig, sp = _lru_coef(xb, wg_ref, direction, lv_ref[3 * direction:3 * direction + 1, :],
                                    lv_ref[3 * direction + 1:3 * direction + 2, :], lam, gc)
        hs_t = hs_ref[...]
        r8 = _rows((8, w))
        if direction == 0:
            edge = jnp.where(is_ctx, 0.0, halo_ref[7:8, :])
            first = jnp.where(r8 == 0, edge, pltpu.roll(hs_t[t - 8:, :], 1, 0))
            hprev = jnp.concatenate([first, hs_t[:t - 8, :]], axis=0)
        else:
            edge = jnp.where(is_ctx, 0.0, halo_ref[0:1, :])
            final = jnp.where(r8 == 7, edge, pltpu.roll(hs_t[:8, :], 7, 0))
            hprev = jnp.concatenate([hs_t[8:, :], final], axis=0)
        a_s[...] = a
        dh_s[...] = jnp.where(is_ctx, 0.0, dh_ref[...].astype(F32))
        carry[...] = _scan_tile_backward(a_s, dh_s, g_s, carry[...], direction == 0)

        g = g_s[...]
        ix = ig * xb
        gs = g * s
        dla = (g * a) * (hprev - ix * (a / s))
        dxb = gs * ig
        dzr = dla * (r * (1.0 - r)) * (-LRU_C * sp)
        dzi = gs * ix * (1.0 - ig)
        part_ref[0:1, :] += jnp.sum(dzr, axis=0, keepdims=True)
        part_ref[1:2, :] += jnp.sum(dzi, axis=0, keepdims=True)
        part_ref[2:3, :] += jnp.sum(dla * r, axis=0, keepdims=True) * (LRU_C * _sigmoid(-lam))
        pieces = []
        for gi in range(ng):
            sl = slice(gi * gc, (gi + 1) * gc)
            dz = jnp.concatenate([dzr[:, sl], dzi[:, sl]], axis=-1).astype(BF16)
            pieces.append(_dot_nt(dz, wg_ref[direction, gi]))
            dwg_ref[gi] += _dot(xb[:, sl].T.astype(BF16), dz)
        dxb = dxb + (pieces[0] if ng == 1 else jnp.concatenate(pieces, axis=-1))
        if not last:
            out_ref[...] = dxb
        else:
            dxb = dxb + dxo_ref[...]
            dxb16, v = dxb.astype(BF16), v_ref[...].astype(F32)
            dv = jnp.zeros((t, w), F32)
            for j in range(4):
                back = _dot(bm_ref[j], dxb16)
                dv = dv + wcb_ref[j:j + 1, :] * back
                part_ref[4 + j:5 + j, :] += jnp.sum(back * v, axis=0, keepdims=True)
            out_ref[...] = dv.astype(BF16)
            part_ref[3:4, :] += jnp.sum(dxb, axis=0, keepdims=True)

    full = lambda shape: pl.BlockSpec(shape, lambda i: (0,) * len(shape))
    kind = lambda i: (jnp.where(i == nt, 1, 0), 0, 0, 0)
    in_specs = [pl.BlockSpec((t, w), lambda i: (tile(i), 0)),
                pl.BlockSpec((t, w), lambda i: (jnp.minimum(tile(i), nt - 1), 0)),
                pl.BlockSpec((t, w), lambda i: (tile(i), 0)),
                pl.BlockSpec((8, w), lambda i: (halo(i), 0)),
                full(wg.shape), full(lv.shape)]
    args = [xb, dhs, hs, hs, wg, lv]
    if last:
        p, wcb, back_m, dxb_other, dp = conv
        in_specs += [pl.BlockSpec((t, w), lambda i: (tile(i), 4)), full(wcb.shape),
                     pl.BlockSpec((None, 4, t, t), kind), pl.BlockSpec((t, w), lambda i: (tile(i), 0)), ANY]
        args += [p, wcb, back_m, dxb_other, dp]
        out0 = jax.ShapeDtypeStruct(dp.shape, dp.dtype)
        spec0 = pl.BlockSpec((t, w), lambda i: (tile(i), 4))
        aliases = {10: 0}
    else:
        out0 = jax.ShapeDtypeStruct((la, w), F32)
        spec0 = pl.BlockSpec((t, w), lambda i: (tile(i), 0))
        aliases = {}
    return _call(
        body, name="lru_backward_%d" % direction,
        grid=(nt + 1,),
        out_shape=(out0, jax.ShapeDtypeStruct((ng, gc, 2 * gc), F32), jax.ShapeDtypeStruct((8, w), F32)),
        in_specs=in_specs,
        out_specs=(spec0, full((ng, gc, 2 * gc)), full((8, w))),
        scratch_shapes=[pltpu.VMEM((t, w), F32), pltpu.VMEM((t, w), F32), pltpu.VMEM((t, w), F32),
                        pltpu.VMEM((8, w), F32)],
        input_output_aliases=aliases,
        compiler_params=_params(("arbitrary",)),
    )(*args)


def _weight_grad_t(at, b, nblk_m, nblk_n, tk, name):
    m, k = at.shape
    n = b.shape[1]
    bm, bn = m // nblk_m, n // nblk_n
    nk = k // tk

    def body(a_ref, b_ref, o_ref, acc):
        kk = pl.program_id(2)

        @pl.when(kk == 0)
        def _():
            acc[...] = jnp.zeros_like(acc)

        acc[...] += _dot(a_ref[...], b_ref[...])

        @pl.when(kk == nk - 1)
        def _():
            o_ref[...] = acc[...].astype(BF16)

    return _call(
        body, name=name,
        grid=(nblk_m, nblk_n, nk),
        out_shape=jax.ShapeDtypeStruct((nblk_m * nblk_n, bm, bn), BF16),
        in_specs=[pl.BlockSpec((bm, tk), lambda i, j, kk: (i, kk)),
                  pl.BlockSpec((tk, bn), lambda i, j, kk: (kk, j))],
        out_specs=pl.BlockSpec((None, bm, bn), lambda i, j, kk: (i * nblk_n + j, 0, 0)),
        scratch_shapes=[pltpu.VMEM((bm, bn), F32)],
        compiler_params=_params(("arbitrary", "arbitrary", "arbitrary")),
    )(at, b)


def _weight_grad_scatter(at, b, tk, name):
    m, k = at.shape
    n = b.shape[1]
    bn = n // NDEV
    nk = k // tk
    where = jnp.stack([_idx(_my_pos()), lax.axis_index("c")]).astype(jnp.int32)

    def body(w_ref, a_ref, b_ref, recv_ref, acc, sbuf, sib, sib_send, sib_recv, chip_send, chip_recv, keep_sem):
        s, kk = pl.program_id(0), pl.program_id(1)
        x, y, c = _my_pos()

        @pl.when(kk == 0)
        def _():
            acc[...] = _dot(a_ref[...], b_ref[...])

        @pl.when(kk > 0)
        def _():
            acc[...] += _dot(a_ref[...], b_ref[...])

        def to_sibling(j):
            return pltpu.make_async_remote_copy(
                src_ref=sbuf.at[0], dst_ref=sib.at[j], send_sem=sib_send.at[j], recv_sem=sib_recv.at[j],
                device_id=(x, y, 1 - c), device_id_type=MESH)

        def to_chip(j):
            dist = _chip_order(j, c)
            return pltpu.make_async_remote_copy(
                src_ref=sbuf.at[1], dst_ref=recv_ref.at[dist // 2], send_sem=chip_send.at[j],
                recv_sem=chip_recv.at[dist // 2], device_id=_peer_at(dist), device_id_type=MESH)

        keep = pltpu.make_async_copy(sbuf.at[1], recv_ref.at[0], keep_sem)
        sends = []
        for j in range(4):
            sends += [to_sibling(j), to_chip(j) if j < 3 else keep]

        for st in range(NDEV):
            @pl.when((kk == nk - 1) & (s == st))
            def _(st=st):
                if st >= 2:
                    sends[st - 2].wait_send()
                part = acc[...]
                if st % 2 == 1:
                    to_sibling(st // 2).wait_recv()
                    part = part + sib[st // 2].astype(F32)
                sbuf[st % 2] = part.astype(BF16)
                sends[st].start()
                if st == NDEV - 1:
                    sends[st - 1].wait_send()
                    sends[st].wait()
                    for j in range(1, 4):
                        pltpu.make_async_remote_copy(
                            src_ref=sbuf.at[0], dst_ref=recv_ref.at[j], send_sem=chip_send.at[0],
                            recv_sem=chip_recv.at[j], device_id=_peer_at(2 * j), device_id_type=MESH).wait_recv()

    blk = lambda s, w_ref: w_ref[0] ^ _scatter_order(s, w_ref[1])
    return _call(
        body, name=name,
        grid_spec=pltpu.PrefetchScalarGridSpec(
            num_scalar_prefetch=1, grid=(NDEV, nk),
            in_specs=[pl.BlockSpec((m, tk), lambda s, kk, w_ref: (0, kk)),
                      pl.BlockSpec((tk, bn), lambda s, kk, w_ref: (kk, blk(s, w_ref)))],
            out_specs=ANY,
            scratch_shapes=[pltpu.VMEM((m, bn), F32), pltpu.VMEM((2, m, bn), BF16), pltpu.VMEM((4, m, bn), BF16),
                            pltpu.SemaphoreType.DMA((4,)), pltpu.SemaphoreType.DMA((4,)),
                            pltpu.SemaphoreType.DMA((4,)), pltpu.SemaphoreType.DMA((4,)),
                            pltpu.SemaphoreType.DMA]),
        out_shape=jax.ShapeDtypeStruct((4, m, bn), BF16),
        compiler_params=_params(("arbitrary", "arbitrary")),
    )(where, at, b)


def _input_backward(dp, w_all, src, mv, row0, tm, nbk, name, dn=None):
    rows, d = src.shape
    nb, _, bw = w_all.shape
    nk = nb // nbk
    ni = rows // tm
    blk0 = row0 // tm
    latent = dn is not None

    def body(*refs):
        dp_ref, w_ref, x_ref, mv_ref = refs[:4]
        outs = refs[4 + latent:]
        part_ref, acc = outs[latent], outs[latent + 1]
        i, k = pl.program_id(0), pl.program_id(1)

        def product():
            step = _dot_nt(dp_ref[:, 0:bw], w_ref[0])
            for q in range(1, nbk):
                step = step + _dot_nt(dp_ref[:, q * bw:(q + 1) * bw], w_ref[q])
            return step

        def finish(slot):
            xf = x_ref[...]
            r = lax.rsqrt(jnp.mean(xf * xf, axis=-1, keepdims=True) + EPS)
            xn = xf * r
            dhl = acc[slot]
            gain, sc = mv_ref[0:1, :], mv_ref[1:2, :]
            dhx = jnp.sum(dhl * xn, axis=0, keepdims=True)
            part_ref[0:1, :] += jnp.sum(dhl, axis=0, keepdims=True)
            part_ref[1:2, :] += dhx * gain
            part_ref[2:3, :] += dhx * (1.0 + sc)
            if latent:
                dxn = dhl * (gain * (1.0 + sc))
                outs[0][...] = refs[4][...] + r * (dxn - xn * jnp.mean(dxn * xn, axis=-1, keepdims=True))

        @pl.when((i == 0) & (k == 0))
        def _():
            part_ref[...] = jnp.zeros_like(part_ref)
            acc[0] = product()

        @pl.when((i > 0) & (i < ni) & (k == 0))
        def _():
            acc[i % 2] = product()
            finish((i - 1) % 2)

        @pl.when((i == ni) & (k == 0))
        def _():
            finish((ni - 1) % 2)

        @pl.when((i < ni) & (k > 0))
        def _():
            acc[i % 2] += product()

    tile = pl.BlockSpec((tm, d), lambda i, k: (jnp.maximum(i - 1, 0), 0))
    vec = pl.BlockSpec((8, d), lambda i, k: (0, 0))
    kblock = lambda i, k: jnp.where(i == ni, nk - 1, k)
    return _call(
        body, name=name,
        grid=(ni + 1, nk),
        out_shape=((jax.ShapeDtypeStruct((rows, d), F32),) if latent else ()) + (jax.ShapeDtypeStruct((8, d), F32),),
        in_specs=[pl.BlockSpec((tm, nbk * bw), lambda i, k: (blk0 + jnp.minimum(i, ni - 1), kblock(i, k))),
                  pl.BlockSpec((nbk, d, bw), lambda i, k: (kblock(i, k), 0, 0)), tile, vec]
                 + ([tile] if latent else []),
        out_specs=((tile,) if latent else ()) + (vec,),
        scratch_shapes=[pltpu.VMEM((2, tm, d), F32)],
        compiler_params=_params(("arbitrary", "arbitrary")),
    )(*([dp, w_all, src, mv] + ([dn] if latent else [])))


def _adamw_scattered(parts, w, m, v, tr):
    r, c = w.shape
    nslot = parts.shape[0]

    def body(p_ref, w_ref, m_ref, v_ref, g_ref, d_ref, m2_ref, v2_ref):
        g = p_ref[0].astype(F32)
        for k in range(1, nslot):
            g = g + p_ref[k].astype(F32)
        g_ref[...] = g
        d_ref[...], m2_ref[...], v2_ref[...] = _adamw(w_ref[...], g, m_ref[...], v_ref[...])

    tile = pl.BlockSpec((tr, c), lambda i: (i, 0))
    return _call(
        body, name="adamw_scattered_%dx%d" % (r, c),
        grid=(r // tr,),
        out_shape=tuple(jax.ShapeDtypeStruct((r, c), F32) for _ in range(4)),
        in_specs=[pl.BlockSpec((nslot, tr, c), lambda i: (0, i, 0)), tile, tile, tile],
        out_specs=(tile,) * 4,
        compiler_params=_params(("arbitrary",)),
    )(parts, w, m, v)


def _adamw_ada(st, dmod, w, m, v, tr):
    r, c = w.shape

    def body(s_ref, dm_ref, w_ref, m_ref, v_ref, g_ref, d_ref, m2_ref, v2_ref):
        g = jnp.dot(s_ref[...], dm_ref[...], precision=HIGHEST, preferred_element_type=F32)
        g_ref[...] = g
        d_ref[...], m2_ref[...], v2_ref[...] = _adamw(w_ref[...], g, m_ref[...], v_ref[...])

    tile = pl.BlockSpec((tr, c), lambda i: (i, 0))
    return _call(
        body, name="adamw_ada",
        grid=(r // tr,),
        out_shape=tuple(jax.ShapeDtypeStruct((r, c), F32) for _ in range(4)),
        in_specs=[pl.BlockSpec((tr, 16), lambda i: (i, 0)), pl.BlockSpec((16, c), lambda i: (0, 0)),
                  tile, tile, tile],
        out_specs=(tile,) * 4,
        compiler_params=_params(("arbitrary",)),
    )(st, dmod, w, m, v)


def _adamw_small(gs, ws, ms, vs):
    n = len(ws)

    def body(*refs):
        for j in range(n):
            g_ref, w_ref, m_ref, v_ref = refs[j], refs[n + j], refs[2 * n + j], refs[3 * n + j]
            d_ref, m2_ref, v2_ref = refs[4 * n + j], refs[5 * n + j], refs[6 * n + j]
            d_ref[...], m2_ref[...], v2_ref[...] = _adamw(w_ref[...], g_ref[...], m_ref[...], v_ref[...])

    shapes = tuple(jax.ShapeDtypeStruct(a.shape, F32) for a in ws)
    out = _call(
        body, name="adamw_small",
        out_shape=shapes * 3,
        in_specs=[VMEM] * (4 * n), out_specs=(VMEM,) * (3 * n),
        compiler_params=_params(),
    )(*gs, *ws, *ms, *vs)
    return list(out[:n]), list(out[n:2 * n]), list(out[2 * n:])


def _blockdiag_groups(wh, gc):
    h, dh, _ = wh.shape
    g = gc // dh
    w4 = wh.reshape(h // g, g, dh, dh)
    bd = jnp.einsum("ngij,gh->ngihj", w4, jnp.eye(g, dtype=wh.dtype))
    return bd.reshape(h // g, gc, gc)


def _blockdiag_extract(bd, dh):
    ng, gc, _ = bd.shape
    g = gc // dh
    x = bd.reshape(ng, g, dh, g, dh)
    return jnp.einsum("ngihj,gh->ngij", x, jnp.eye(g, dtype=bd.dtype)).reshape(ng * g, dh, dh)


def _rows8(*vecs):
    rows = [jnp.reshape(v, (1, -1)).astype(F32) for v in vecs]
    n = rows[0].shape[1]
    return jnp.concatenate(rows + [jnp.zeros((8 - len(rows), n), F32)], axis=0)


def _pack(pieces):
    flat = jnp.concatenate([jnp.reshape(a, (-1,)).astype(F32) for a in pieces])
    total = -(-flat.shape[0] // 1024) * 1024
    return jnp.pad(flat, (0, total - flat.shape[0])).reshape(total // 128, 128)


def _unpack(packed, shapes):
    flat = packed.reshape(-1)
    out, off = [], 0
    for s in shapes:
        n = 1
        for q in s:
            n *= q
        out.append(flat[off:off + n].reshape(s))
        off += n
    return out


def kernel(x, c, ctx, c_ctx, norm_g, w_ada, b_ada, w_in, w_conv_a, w_conv_b, b_conv_b, lru_wa, lru_ba, lru_wx, lru_bx, lru_lambda, w_out, final_g, loss_target, m_c_ctx, m_norm_g, m_w_ada, m_b_ada, m_w_in, m_w_conv_a, m_w_conv_b, m_b_conv_b, m_lru_wa, m_lru_ba, m_lru_wx, m_lru_bx, m_lru_lambda, m_w_out, m_final_g, v_c_ctx, v_norm_g, v_w_ada, v_b_ada, v_w_in, v_w_conv_a, v_w_conv_b, v_b_conv_b, v_lru_wa, v_lru_ba, v_lru_wx, v_lru_bx, v_lru_lambda, v_w_out, v_final_g):
    _, l, d = x.shape
    lc = ctx.shape[1]
    w = d // 2
    t = lc
    assert l % t == 0 and t % GRID_W == 0 and t % 128 == 0
    dh = w // N_HEADS
    gc = min(w, MXU_WIDTH)
    cols = w_ada.shape[2]
    wo_rows = w_out.shape[1]
    me = _idx(_my_pos())
    x2, ctx2, tgt2 = x[0], ctx[0], loss_target[0]
    w_ada2, w_in2, w_out2 = w_ada[0], w_in[0], w_out[0]

    small_mine = jnp.concatenate([a.reshape(-1) for a in (w_conv_a, w_conv_b, lru_ba, lru_bx, lru_lambda)]
                                 + [jnp.zeros((3 * (w // NDEV),), F32)]).reshape(16, w // NDEV)
    mod_all, s_mat, small_all = _mod_forward(
        jnp.broadcast_to(c, (8, d)), jnp.broadcast_to(c_ctx[None], (8, d)), w_ada2, small_mine)
    mod = jnp.transpose(mod_all, (1, 0, 2)).reshape(16, NDEV * cols) + b_ada
    mod_lat = lax.dynamic_slice_in_dim(mod, me, 1, axis=0)
    sh_l, sc_l, gt_l = jnp.split(mod_lat, 3, axis=-1)
    sh_c, sc_c, _ = jnp.split(mod[8:9], 3, axis=-1)
    small = jnp.transpose(small_all, (1, 0, 2)).reshape(16, w)
    wca = _rows8(*[small[j] for j in range(0, 3)])
    wcb = _rows8(*[small[j] for j in range(3, 7)], b_conv_b)
    lv = _rows8(small[7], small[9], small[11], small[8], small[10], small[12])
    wg = jnp.stack([
        jnp.concatenate([_blockdiag_groups(lru_wa[0, dr], gc), _blockdiag_groups(lru_wx[0, dr], gc)], axis=-1)
        for dr in range(2)]).astype(BF16)

    la = l + lc
    tm = 2 * t if l % (2 * t) == 0 else t
    tk = 3 * t if la % (3 * t) == 0 else t
    h, hlt = _normalize(x2, _rows8(norm_g, sc_l, sh_l), la, 0, tm, "normalize")
    h, hlt = _normalize(ctx2, _rows8(norm_g, sc_c, sh_c), la, l, t, "normalize_ctx", prev=(h, hlt))
    p, w_all = _in_projection(h, w_in2.astype(BF16), la // 8 if la % 128 == 0 else tk)
    taps_m, back_m, perm = _scan_matrices(t)
    xb = _conv_input(p, wcb, taps_m, l, t)
    hf, hr, wo_all = _lru_forward(xb, wg, lv, w_out2.astype(BF16), l, t)
    wo = wo_all.reshape(d, d)
    dn, catt, dout, part_mix = _mix_forward(x2, tgt2, p, hf, hr, wo, _rows8(gt_l, final_g), wca, perm, t)
    g_wout = _weight_grad_t(catt, dout, 2, 1, 4 * t if l % (4 * t) == 0 else t, "grad_w_out")
    dp, dhs, part_ca, sc_wout = _mix_backward(dout, p, hf, hr, wo, wca, perm, g_wout.reshape(NDEV, wo_rows, d), l, t)
    dxb0, dwg0, part_l0 = _lru_backward(0, xb, dhs, hf, wg, lv, l, t)
    dp, dwg1, part_l1 = _lru_backward(1, xb, dhs, hr, wg, lv, l, t, conv=(p, wcb, back_m, dxb0, dp))
    sc_win = _weight_grad_scatter(hlt, dp, tk, "grad_w_in")
    grad_x, part_lat = _input_backward(dp, w_all, x2, _rows8(norm_g, sc_l), 0, tm, 2, "input_backward", dn=dn)
    (part_ctx,) = _input_backward(dp, w_all, ctx2, _rows8(norm_g, sc_c), l, t, 2, "input_backward_ctx")
    part_in = jnp.concatenate([part_lat[0:2], part_ctx[0:2], (part_lat[2] + part_ctx[2])[None]], axis=0)

    dwa = jnp.stack([_blockdiag_extract(dwg0[:, :, :gc], dh), _blockdiag_extract(dwg1[:, :, :gc], dh)])
    dwx = jnp.stack([_blockdiag_extract(dwg0[:, :, gc:], dh), _blockdiag_extract(dwg1[:, :, gc:], dh)])
    lru_part = jnp.stack([dwa, dwx]).reshape(NDEV, -1, 128)
    zeros_d = jnp.zeros((d,), F32)
    pieces = [
        jnp.concatenate([part_in[0], part_in[1], part_mix[1]]),
        jnp.concatenate([part_in[2], part_in[3], zeros_d]),
        part_in[4], part_mix[0], part_ca[0:3], part_l1[4:8], part_l1[3],
        jnp.stack([part_l0[0], part_l1[0]]), jnp.stack([part_l0[1], part_l1[1]]),
        jnp.stack([part_l0[2], part_l1[2]]), part_mix[2, 0:1],
    ]
    shapes = [(3 * d,), (3 * d,), (d,), (d,), (3, w), (4, w), (w,), (2, w), (2, w), (2, w), (1,)]
    sig_cc = jax.nn.sigmoid(c_ctx)
    dsilu_cc = jnp.broadcast_to((sig_cc * (1.0 + c_ctx * (1.0 - sig_cc)))[None], (8, d))
    psum, pall, lru_sum, g_cctx8 = _reduce_small(_pack(pieces), lru_part, w_ada2, dsilu_cc)
    (g_modl, g_modc, g_norm, g_final, g_ca, g_cb, g_bcb, g_ba, g_bx, g_lam, loss1) = _unpack(psum, shapes)
    loss = loss1[0]
    g_cctx = g_cctx8[0]
    g_bada = (g_modl + g_modc)[None]
    g_lru = lru_sum.reshape(2, 2, N_HEADS, dh, dh)
    g_wa, g_wx = g_lru[0][None], g_lru[1][None]
    wsl = w // NDEV
    mine = lambda a: lax.dynamic_slice_in_dim(a, me * wsl, wsl, axis=-1)
    g_ca_m, g_cb_m, g_ba_m, g_bx_m, g_lam_m = (mine(g_ca)[None], mine(g_cb)[None], mine(g_ba)[None],
                                               mine(g_bx)[None], mine(g_lam)[None])
    g_norm, g_bcb = g_norm[None], g_bcb[None]

    cb = cols // 128
    per_dev = pall[:, :3 * d // 128].reshape(NDEV, NDEV, cols)
    dmod_lat = lax.dynamic_slice_in_dim(per_dev, me, 1, axis=1)[:, 0]
    dmod_ctx = lax.dynamic_slice_in_dim(g_modc.reshape(NDEV, cols), me, 1, axis=0)
    dmod16 = jnp.concatenate([dmod_lat, dmod_ctx, jnp.zeros((7, cols), F32)], axis=0)
    tr_ada = 256 if d % 256 == 0 else d
    g_wada, d_wada, m_wada, v_wada = _adamw_ada(s_mat.T, dmod16, w_ada2, m_w_ada[0], v_w_ada[0], tr_ada)
    g_win2, d_win, m_win, v_win = _adamw_scattered(sc_win, w_in2, m_w_in[0], v_w_in[0], tr_ada)
    tr_out = 64 if wo_rows % 64 == 0 else wo_rows
    g_wout2, d_wout, m_wout, v_wout = _adamw_scattered(sc_wout, w_out2, m_w_out[0], v_w_out[0], tr_out)

    small_w = [c_ctx, norm_g, b_ada, w_conv_a, w_conv_b, b_conv_b, lru_wa, lru_ba, lru_wx, lru_bx, lru_lambda, final_g]
    small_m = [m_c_ctx, m_norm_g, m_b_ada, m_w_conv_a, m_w_conv_b, m_b_conv_b, m_lru_wa, m_lru_ba, m_lru_wx,
               m_lru_bx, m_lru_lambda, m_final_g]
    small_v = [v_c_ctx, v_norm_g, v_b_ada, v_w_conv_a, v_w_conv_b, v_b_conv_b, v_lru_wa, v_lru_ba, v_lru_wx,
               v_lru_bx, v_lru_lambda, v_final_g]
    small_g = [g_cctx, g_norm, g_bada, g_ca_m, g_cb_m, g_bcb, g_wa, g_ba_m, g_wx, g_bx_m, g_lam_m, g_final]
    small_g = [jnp.reshape(a, b.shape) for a, b in zip(small_g, small_w)]
    d_s, m_s, v_s = _adamw_small(small_g, small_w, small_m, small_v)

    def weights(small_list, ada, win, wout):
        (cctx_, norm_, bada_, ca_, cb_, bcb_, wa_, ba_, wx_, bx_, lam_, final_) = small_list
        return [cctx_, norm_, ada[None], bada_, win[None], ca_, cb_, bcb_, wa_, ba_, wx_, bx_, lam_, wout[None], final_]

    return (loss, grad_x[None],
            *weights(small_g, g_wada, g_win2, g_wout2), *weights(d_s, d_wada, d_win, d_wout),
            *weights(m_s, m_wada, m_win, m_wout), *weights(v_s, v_wada, v_win, v_wout))
```

```python
import functools

import jax
import jax.numpy as jnp
import numpy as np
from jax import lax
from jax.experimental import pallas as pl
from jax.experimental.pallas import tpu as pltpu

F32 = jnp.float32
BF16 = jnp.bfloat16
MESH = pl.DeviceIdType.MESH
NDEV = 8
GRID_W = 64
N_HEADS = 16
LRU_C = 8.0
EPS = 1e-6
MXU_WIDTH = 256
VMEM_LIMIT = 60 * 1024 * 1024

ADAM_LR = 0.001
ADAM_B1 = 0.9
ADAM_B2 = 0.999
ADAM_EPS = 1e-08
ADAM_WD = 0.01
ADAM_STEP = 10
ADAM_C1 = 1.0 - ADAM_B1 ** ADAM_STEP
ADAM_C2 = 1.0 - ADAM_B2 ** ADAM_STEP

HIGHEST = lax.Precision.HIGHEST
ANY = pl.BlockSpec(memory_space=pl.ANY)
VMEM = pl.BlockSpec(memory_space=pltpu.VMEM)


def _call(body, **kw):
    return pl.pallas_call(body, **kw)


def _params(sem=None, vmem=VMEM_LIMIT):
    return pltpu.CompilerParams(dimension_semantics=sem, vmem_limit_bytes=vmem)


def _my_pos():
    return lax.axis_index("x"), lax.axis_index("y"), lax.axis_index("c")


def _idx(pos):
    return 4 * pos[0] + 2 * pos[1] + pos[2]


def _peer(k):
    x, y, c = _my_pos()
    return ((1 - x) if (k >> 2) & 1 else x, (1 - y) if (k >> 1) & 1 else y, (1 - c) if k & 1 else c)


def _exchange_start(src_ref, dst_ref, send_sems, recv_sems, base):
    me = _idx(_my_pos())
    sends = []
    for k in range(1, NDEV):
        cp = pltpu.make_async_remote_copy(
            src_ref=src_ref, dst_ref=dst_ref.at[me], send_sem=send_sems.at[base + k - 1],
            recv_sem=recv_sems.at[base + k - 1], device_id=_peer(k), device_id_type=MESH)
        cp.start()
        sends.append(cp)
    dst_ref[me] = src_ref[...]
    return sends, (src_ref, dst_ref, send_sems, recv_sems, base)


def _exchange_finish(started):
    sends, (src_ref, dst_ref, send_sems, recv_sems, base) = started
    for k in range(1, NDEV):
        peer = _peer(k)
        pltpu.make_async_remote_copy(
            src_ref=src_ref, dst_ref=dst_ref.at[_idx(peer)], send_sem=send_sems.at[base + k - 1],
            recv_sem=recv_sems.at[base + k - 1], device_id=peer, device_id_type=MESH).wait_recv()
    for cp in sends:
        cp.wait_send()


def _exchange_vmem(src_ref, dst_ref, send_sems, recv_sems, base):
    _exchange_finish(_exchange_start(src_ref, dst_ref, send_sems, recv_sems, base))


def _sigmoid(z):
    return 0.5 * jnp.tanh(0.5 * z) + 0.5


def _softplus(x):
    return jnp.maximum(x, 0.0) + jnp.log1p(jnp.exp(-jnp.abs(x)))


def _one_minus_sq(a, la):
    series = (-2.0 * la) * (1.0 + la)
    return jnp.where(la > -0.0015, series, 1.0 - a * a)


def _dot(a, b):
    return jnp.dot(a, b, preferred_element_type=F32)


def _dot_nt(a, b):
    return lax.dot_general(a, b, (((1,), (1,)), ((), ())), preferred_element_type=F32)


def _rows(shape):
    return lax.broadcasted_iota(jnp.int32, shape, 0)


def _down(x, k, pos):
    return jnp.where(pos >= k, pltpu.roll(x, k, 0), 0.0)


def _up(x, k, pos, rowlen):
    return jnp.where(pos + k < rowlen, pltpu.roll(x, x.shape[0] - k, 0), 0.0)


def _pos_rowlen(shape, is_ctx):
    t = _rows(shape)
    pos = jnp.where(is_ctx, t, t & (GRID_W - 1))
    rowlen = jnp.where(is_ctx, shape[0], GRID_W)
    return pos, rowlen


def _scan_matrices(t):
    seg = t // 8
    r = np.arange(t)
    perm = (np.arange(t)[None, :] == ((r % 8) * seg + r // 8)[:, None]).astype(np.float32)
    rows, cols = r[:, None], r[None, :]
    taps, back = [], []
    for rowlen in (GRID_W, t):
        pos = rows % rowlen
        shift = {-2: (cols == rows - 2) & (pos >= 2), -1: (cols == rows - 1) & (pos >= 1),
                 0: cols == rows, 1: (cols == rows + 1) & (pos + 1 < rowlen),
                 2: (cols == rows + 2) & (pos + 2 < rowlen)}
        taps.append(np.stack([perm @ shift[k].astype(np.float32) for k in (-2, -1, 0, 1)]))
        back.append(np.stack([shift[k].astype(np.float32) @ perm.T for k in (2, 1, 0, -1)]))
    as_bf16 = lambda a: jnp.asarray(a, dtype=BF16)
    return as_bf16(np.stack(taps)), as_bf16(np.stack(back)), as_bf16(np.stack([perm, perm.T]))


def _conv3(t, w_ref, pos, rowlen):
    return w_ref[0:1, :] * _down(t, 1, pos) + w_ref[1:2, :] * t + w_ref[2:3, :] * _up(t, 1, pos, rowlen)


def _conv3_t(dz, w_ref, pos, rowlen):
    return w_ref[0:1, :] * _up(dz, 1, pos, rowlen) + w_ref[1:2, :] * dz + w_ref[2:3, :] * _down(dz, 1, pos)


def _chunk_scan(a, b, reverse):
    row = _rows(a.shape)
    for s in (1, 2, 4):
        if reverse:
            m = row < 8 - s
            sh = 8 - s
        else:
            m = row >= s
            sh = s
        a_s = jnp.where(m, pltpu.roll(a, sh, 0), 1.0)
        b_s = jnp.where(m, pltpu.roll(b, sh, 0), 0.0)
        b = b + a * b_s
        a = a * a_s
    return a, b


def _chain_segments(ptot, hend, carry, reverse):
    ca, cb = _chunk_scan(ptot, hend, reverse)
    incl = ca * carry + cb
    r8 = _rows(incl.shape)
    if reverse:
        start = jnp.where(r8 < 7, pltpu.roll(incl, 7, 0), carry)
        last = incl[0:1, :]
    else:
        start = jnp.where(r8 >= 1, pltpu.roll(incl, 1, 0), carry)
        last = incl[7:8, :]
    return start, jnp.broadcast_to(last, incl.shape)


def _blocks(nblock, reverse):
    order = range(nblock - 1, -1, -1) if reverse else range(nblock)
    return [slice(8 * k, 8 * k + 8) for k in order]


def _scan_tile(a_ref, b_ref, out_ref, carry, reverse):
    t, w = a_ref.shape
    seg = t // 8

    hend, ptot = jnp.zeros((8, w), F32), jnp.ones((8, w), F32)
    for rows in _blocks(seg, reverse):
        a = a_ref[rows, :]
        hend, ptot = a * hend + b_ref[rows, :], a * ptot
    h, new_carry = _chain_segments(ptot, hend, carry, reverse)
    for rows in _blocks(seg, reverse):
        h = a_ref[rows, :] * h + b_ref[rows, :]
        out_ref[rows, :] = h
    return new_carry


def _scan_tile_backward(a_ref, dh_ref, g_ref, carry, reverse):
    t, w = a_ref.shape
    seg = t // 8

    uend, ptot = jnp.zeros((8, w), F32), jnp.ones((8, w), F32)
    for rows in _blocks(seg, reverse):
        a = a_ref[rows, :]
        uend, ptot = a * (dh_ref[rows, :] + uend), a * ptot
    u, new_carry = _chain_segments(ptot, uend, carry, reverse)
    for rows in _blocks(seg, reverse):
        g = dh_ref[rows, :] + u
        g_ref[rows, :] = g
        u = a_ref[rows, :] * g
    return new_carry


def _lru_coef(xb, wg_ref, d, ba, bx, lam, gc):
    w = xb.shape[1]
    xb16 = xb.astype(BF16)
    zr, zi = [], []
    for g in range(w // gc):
        z = _dot(xb16[:, g * gc:(g + 1) * gc], wg_ref[d, g])
        zr.append(z[:, :gc])
        zi.append(z[:, gc:])
    zr = zr[0] if len(zr) == 1 else jnp.concatenate(zr, axis=-1)
    zi = zi[0] if len(zi) == 1 else jnp.concatenate(zi, axis=-1)
    tr = jnp.tanh(zr + ba)
    ti = jnp.tanh(zi + bx)
    sp = _softplus(-lam)
    half = -0.5 * LRU_C * sp
    la = tr * half + half
    a = jnp.exp(la)
    q = _one_minus_sq(a, la)
    rs = lax.rsqrt(jnp.maximum(q, 1e-30))
    return a, q * rs, rs, tr, ti, sp


def _adamw(w, g, m, v):
    m2 = ADAM_B1 * m + (1.0 - ADAM_B1) * g
    v2 = ADAM_B2 * v + (1.0 - ADAM_B2) * (g * g)
    m_hat = m2 / ADAM_C1
    v_hat = v2 / ADAM_C2
    delta = -ADAM_LR * (m_hat / (jnp.sqrt(v_hat) + ADAM_EPS) + ADAM_WD * w)
    return delta, m2, v2


def _mod_forward(c8, cctx8, w_ada, small):
    d = c8.shape[1]
    cols = w_ada.shape[1]

    def body(c_ref, cctx_ref, w_ref, sm_ref, mod_ref, s_ref, sm_all, cbuf, mod_my, send_sems, recv_sems):
        _exchange_vmem(sm_ref, sm_all, send_sems, recv_sems, 2 * (NDEV - 1))
        _exchange_vmem(c_ref, cbuf, send_sems, recv_sems, 0)
        row = _rows((8, d))
        c_all = jnp.zeros((8, d), F32)
        for b in range(NDEV):
            c_all = jnp.where(row == b, cbuf[b], c_all)
        cc = cctx_ref[...]
        s_top = c_all * _sigmoid(c_all)
        s_bot = jnp.where(row == 0, cc * _sigmoid(cc), 0.0)
        s = jnp.concatenate([s_top, s_bot], axis=0)
        s_ref[...] = s
        mod_my[...] = jnp.dot(s, w_ref[...], precision=HIGHEST, preferred_element_type=F32)
        _exchange_vmem(mod_my, mod_ref, send_sems, recv_sems, NDEV - 1)

    return _call(
        body, name="mod_forward",
        out_shape=(jax.ShapeDtypeStruct((NDEV, 16, cols), F32), jax.ShapeDtypeStruct((16, d), F32),
                   jax.ShapeDtypeStruct((NDEV,) + small.shape, F32)),
        in_specs=[VMEM] * 4, out_specs=(VMEM,) * 3,
        scratch_shapes=[pltpu.VMEM((NDEV, 8, d), F32), pltpu.VMEM((16, cols), F32),
                        pltpu.SemaphoreType.DMA((3 * (NDEV - 1),)), pltpu.SemaphoreType.DMA((3 * (NDEV - 1),))],
        compiler_params=_params(),
    )(c8, cctx8, w_ada, small)


def _scatter_copies(src_ref, dst_ref, send_sems, recv_sems):
    me = _idx(_my_pos())
    copies = [pltpu.make_async_copy(src_ref.at[me], dst_ref.at[0], send_sems.at[0])]
    for k in range(1, NDEV):
        peer = _peer(k)
        copies.append(pltpu.make_async_remote_copy(
            src_ref=src_ref.at[_idx(peer)], dst_ref=dst_ref.at[k], send_sem=send_sems.at[k],
            recv_sem=recv_sems.at[k], device_id=peer, device_id_type=MESH))
    return copies


def _gather_copies(src_ref, dst_ref, send_sems, recv_sems):
    me = _idx(_my_pos())
    sends = [pltpu.make_async_copy(src_ref, dst_ref.at[me], send_sems.at[0])]
    arrivals = []
    for k in range(1, NDEV):
        peer = _peer(k)
        sends.append(pltpu.make_async_remote_copy(
            src_ref=src_ref, dst_ref=dst_ref.at[me], send_sem=send_sems.at[k],
            recv_sem=recv_sems.at[k], device_id=peer, device_id_type=MESH))
        arrivals.append(pltpu.make_async_remote_copy(
            src_ref=src_ref, dst_ref=dst_ref.at[_idx(peer)], send_sem=send_sems.at[k],
            recv_sem=recv_sems.at[k], device_id=peer, device_id_type=MESH))
    return sends, arrivals


def _exchange_wait(sends, arrivals):
    sends[0].wait()
    for cp in arrivals:
        cp.wait_recv()
    for cp in sends[1:]:
        cp.wait_send()


def _chip_order(k, c):
    return (6, 4 - 2 * c, 2 + 2 * c, 0)[k]


def _scatter_order(s, c):
    k = s >> 1
    mine = jnp.where(k == 0, 6, jnp.where(k == 1, 4 - 2 * c, jnp.where(k == 2, 2 + 2 * c, 0)))
    theirs = jnp.where(k == 0, 6, jnp.where(k == 1, 2 + 2 * c, jnp.where(k == 2, 4 - 2 * c, 0))) ^ 1
    return jnp.where((s & 1) == 0, theirs, mine)


def _peer_at(dist):
    x, y, c = _my_pos()
    return (x ^ ((dist >> 2) & 1), y ^ ((dist >> 1) & 1), c ^ (dist & 1))


def _reduce_small(packed, lru_parts, w_ada, dsilu_cctx):
    rp = packed.shape[0]
    rl = lru_parts.shape[1]
    d, cols = w_ada.shape
    assert cols % 128 == 0
    cb = cols // 128

    def body(p_ref, l_ref, w_ref, ds_ref, sum_ref, all_ref, lru_ref, cctx_ref,
             lbuf, lsum, cpart, call, send_sems, recv_sems, lsend, lrecv):
        me = _idx(_my_pos())
        scattered = _scatter_copies(l_ref, lbuf, lsend, lrecv)
        for cp in scattered:
            cp.start()
        _exchange_vmem(p_ref, all_ref, send_sems, recv_sems, 0)
        acc = all_ref[0]
        for j in range(1, NDEV):
            acc = acc + all_ref[j]
        sum_ref[...] = acc
        _exchange_wait(scattered, scattered[1:])
        red = lbuf[0]
        for k in range(1, NDEV):
            red = red + lbuf[k]
        lsum[...] = red
        lru_gather = _exchange_start(lsum, lru_ref, send_sems, recv_sems, NDEV - 1)
        part = jnp.zeros((8, d), F32)
        for q in range(cb):
            dm = jnp.broadcast_to(sum_ref[pl.ds((NDEV + me) * cb + q, 1), :], (8, 128))
            part = part + lax.dot_general(dm, w_ref[:, q * 128:(q + 1) * 128],
                                          (((1,), (1,)), ((), ())), precision=HIGHEST,
                                          preferred_element_type=F32)
        cpart[...] = part
        _exchange_vmem(cpart, call, send_sems, recv_sems, 2 * (NDEV - 1))
        _exchange_finish(lru_gather)
        tot = call[0]
        for j in range(1, NDEV):
            tot = tot + call[j]
        cctx_ref[...] = tot * ds_ref[...]

    return _call(
        body, name="reduce_small",
        out_shape=(jax.ShapeDtypeStruct((rp, 128), F32), jax.ShapeDtypeStruct((NDEV, rp, 128), F32),
                   jax.ShapeDtypeStruct((NDEV, rl, 128), F32), jax.ShapeDtypeStruct((8, d), F32)),
        in_specs=[VMEM] * 4, out_specs=(VMEM,) * 4,
        scratch_shapes=[pltpu.VMEM((NDEV, rl, 128), F32), pltpu.VMEM((rl, 128), F32), pltpu.VMEM((8, d), F32),
                        pltpu.VMEM((NDEV, 8, d), F32),
                        pltpu.SemaphoreType.DMA((3 * (NDEV - 1),)), pltpu.SemaphoreType.DMA((3 * (NDEV - 1),)),
                        pltpu.SemaphoreType.DMA((NDEV,)), pltpu.SemaphoreType.DMA((NDEV,))],
        compiler_params=_params(),
    )(packed, lru_parts, w_ada, dsilu_cctx)


def _normalize(src, mv, la, row0, tm, name, prev=None):
    rows, d = src.shape
    blk0 = row0 // tm

    def body(*refs):
        x_ref, mv_ref = refs[:2]
        h_ref, ht_ref = refs[-2:]
        xf = x_ref[...]
        r = lax.rsqrt(jnp.mean(xf * xf, axis=-1, keepdims=True) + EPS)
        h = xf * r * (mv_ref[0:1, :] * (1.0 + mv_ref[1:2, :])) + mv_ref[2:3, :]
        h_ref[...] = h.astype(BF16)
        ht_ref[...] = h.T.astype(BF16)

    in_specs = [pl.BlockSpec((tm, d), lambda i: (i, 0)), pl.BlockSpec((8, d), lambda i: (0, 0))]
    args = [src, mv]
    aliases = {}
    if prev is not None:
        in_specs += [ANY, ANY]
        args += list(prev)
        aliases = {2: 0, 3: 1}
    return _call(
        body, name=name,
        grid=(rows // tm,),
        out_shape=(jax.ShapeDtypeStruct((la, d), BF16), jax.ShapeDtypeStruct((d, la), BF16)),
        in_specs=in_specs,
        out_specs=(pl.BlockSpec((tm, d), lambda i: (blk0 + i, 0)), pl.BlockSpec((d, tm), lambda i: (0, blk0 + i))),
        input_output_aliases=aliases,
        compiler_params=_params(("arbitrary",)),
    )(*args)


def _gather_order(step):
    return (step & 1) | (((step >> 2) & 1) << 1) | (((step >> 1) & 1) << 2)


def _in_projection(h, w_shard, tm):
    la, d = h.shape
    bw = w_shard.shape[1]
    ni = la // tm
    where = jnp.reshape(_idx(_my_pos()), (1,)).astype(jnp.int32)

    def body(me_ref, h_ref, w_ref, p_ref, all_ref, wbuf, send_sems, recv_sems, local_sems):
        s, i = pl.program_id(0), pl.program_id(1)
        x, y, c = _my_pos()
        me, sibling = (x, y, c), (x, y, 1 - c)
        chips = [(1 - x, y), (x, 1 - y), (1 - x, 1 - y)]

        def copy(k, block, to, from_shard=False):
            return pltpu.make_async_remote_copy(
                src_ref=w_ref if from_shard else all_ref.at[_idx(block)], dst_ref=all_ref.at[_idx(block)],
                send_sem=send_sems.at[k], recv_sem=recv_sems.at[k], device_id=to, device_id_type=MESH)

        def load(block, slot):
            return pltpu.make_async_copy(all_ref.at[_idx(block)], wbuf.at[slot], local_sems.at[1])

        keep = pltpu.make_async_copy(w_ref, all_ref.at[_idx(me)], local_sems.at[0])
        first = [copy(0, me, sibling, True)] + [copy(1 + j, me, (*chip, c), True) for j, chip in enumerate(chips)]
        passed = [copy(4 + j, (*chip, c), sibling) for j, chip in enumerate(chips)]
        steps = [(copy(0, sibling, me), None, sibling)]
        for j, chip in enumerate(chips):
            steps.append((copy(1 + j, (*chip, c), me), passed[j], (*chip, c)))
            steps.append((copy(4 + j, (*chip, 1 - c), me), None, (*chip, 1 - c)))

        @pl.when((s == 0) & (i == 0))
        def _():
            keep.start()
            mine = pltpu.make_async_copy(w_ref, wbuf.at[0], local_sems.at[1])
            mine.start()
            for cp in first:
                cp.start()
            mine.wait()

        for n, (arrival, forward, block) in enumerate(steps, start=1):
            @pl.when((s == n - 1) & (i == ni - 1))
            def _(arrival=arrival, forward=forward, block=block, n=n):
                arrival.wait_recv()
                if forward is not None:
                    forward.start()
                load(block, n % 2).start()

        @pl.when((s > 0) & (i == 0))
        def _():
            load(me, s % 2).wait()

        p_ref[...] = _dot(h_ref[...], wbuf[s % 2]).astype(BF16)

        @pl.when((s == NDEV - 1) & (i == ni - 1))
        def _():
            for cp in first + passed:
                cp.wait_send()
            keep.wait()

    return _call(
        body, name="in_projection",
        grid_spec=pltpu.PrefetchScalarGridSpec(
            num_scalar_prefetch=1, grid=(NDEV, ni),
            in_specs=[pl.BlockSpec((tm, d), lambda s, i, me_ref: (i, 0)), ANY],
            out_specs=(pl.BlockSpec((tm, bw), lambda s, i, me_ref: (i, me_ref[0] ^ _gather_order(s))), ANY),
            scratch_shapes=[pltpu.VMEM((2, d, bw), BF16), pltpu.SemaphoreType.DMA((7,)),
                            pltpu.SemaphoreType.DMA((7,)), pltpu.SemaphoreType.DMA((2,))]),
        out_shape=(jax.ShapeDtypeStruct((la, NDEV * bw), BF16), jax.ShapeDtypeStruct((NDEV, d, bw), BF16)),
        compiler_params=_params(("arbitrary", "arbitrary")),
    )(where, h, w_shard)


def _conv_input(p, wcb, taps_m, l, t):
    la = p.shape[0]
    w = wcb.shape[1]
    nt = l // t

    def body(v_ref, wcb_ref, tm_ref, xb_ref):
        v16 = v_ref[...]
        xb = wcb_ref[4:5, :] + wcb_ref[0:1, :] * _dot(tm_ref[0], v16)
        for j in range(1, 4):
            xb = xb + wcb_ref[j:j + 1, :] * _dot(tm_ref[j], v16)
        xb_ref[...] = xb

    return _call(
        body, name="conv_input",
        grid=(nt + 1,),
        out_shape=jax.ShapeDtypeStruct((la, w), F32),
        in_specs=[pl.BlockSpec((t, w), lambda i: (i, 4)), pl.BlockSpec((8, w), lambda i: (0, 0)),
                  pl.BlockSpec((None, 4, t, t), lambda i: (i // nt, 0, 0, 0))],
        out_specs=pl.BlockSpec((t, w), lambda i: (i, 0)),
        compiler_params=_params(("arbitrary",)),
    )(p, wcb, taps_m)


def _lru_forward(xb, wg, lv, wo_shard, l, t):
    la, w = xb.shape
    gc = wg.shape[2]
    nt = l // t

    def body(xf_ref, xr_ref, wg_ref, lv_ref, wo_ref, hf_ref, hr_ref, wo_all,
             a_s, b_s, carry, send_sems, recv_sems):
        sends, arrivals = _gather_copies(wo_ref, wo_all, send_sems, recv_sems)

        @pl.when(pl.program_id(0) == 0)
        def _():
            carry[...] = jnp.zeros_like(carry)
            for cp in sends:
                cp.start()

        @pl.when(pl.program_id(0) == nt)
        def _():
            _exchange_wait(sends, arrivals)

        for dr, (x_ref, h_ref) in enumerate(((xf_ref, hf_ref), (xr_ref, hr_ref))):
            x = x_ref[...]
            a, s, _, _, ti, _ = _lru_coef(x, wg_ref, dr, lv_ref[3 * dr:3 * dr + 1, :],
                                          lv_ref[3 * dr + 1:3 * dr + 2, :], lv_ref[3 * dr + 2:3 * dr + 3, :], gc)
            a_s[...] = a
            b_s[...] = (s * x) * (0.5 * ti + 0.5)
            carry[dr] = _scan_tile(a_s, b_s, h_ref, carry[dr], dr == 1)

    full = lambda shape: pl.BlockSpec(shape, lambda i: (0,) * len(shape))
    fmap = lambda i: (jnp.where(i == 0, nt, i - 1), 0)
    rmap = lambda i: (jnp.where(i == 0, nt, nt - i), 0)
    return _call(
        body, name="lru_forward",
        grid=(nt + 1,),
        out_shape=(jax.ShapeDtypeStruct((la, w), F32), jax.ShapeDtypeStruct((la, w), F32),
                   jax.ShapeDtypeStruct((NDEV,) + wo_shard.shape, wo_shard.dtype)),
        in_specs=[pl.BlockSpec((t, w), fmap), pl.BlockSpec((t, w), rmap), full(wg.shape), full(lv.shape), ANY],
        out_specs=(pl.BlockSpec((t, w), fmap), pl.BlockSpec((t, w), rmap), ANY),
        scratch_shapes=[pltpu.VMEM((t, w), F32), pltpu.VMEM((t, w), F32), pltpu.VMEM((2, 8, w), F32),
                        pltpu.SemaphoreType.DMA((NDEV,)), pltpu.SemaphoreType.DMA((NDEV,))],
        compiler_params=_params(("arbitrary",)),
    )(xb, xb, wg, lv, wo_shard)


def _mix_gates(p_refs, hf_ref, hr_ref, wca_ref, perm_ref, t, w):
    bl, cl, ul, gl, ql = [r[...].astype(F32) for r in p_refs]
    pos, rowlen = _pos_rowlen((t, w), False)
    tt = cl * ul
    z = _conv3(tt, wca_ref, pos, rowlen)
    sig_g = _sigmoid(gl)
    sig_q = _sigmoid(ql)
    ylru = _dot(perm_ref[1], (hf_ref[...] + hr_ref[...]).astype(BF16))
    return bl, cl, ul, gl, ql, tt, z, sig_g, sig_q, ylru, pos, rowlen


def _p_specs(t, w, nt):
    return [pl.BlockSpec((t, w), functools.partial(lambda i, s: (jnp.minimum(i, nt - 1), s), s=s))
            for s in (0, 1, 2, 3, 5)]


def _mix_forward(x, tgt, p, hf, hr, wo, ov, wca, perm, t):
    l, d = x.shape
    w = d // 2
    nt = l // t

    def body(x_ref, tg_ref, b_ref, c_ref, u_ref, g_ref, q_ref, hf_ref, hr_ref, wo_ref, ov_ref, wca_ref, perm_ref,
             dn_ref, ct_ref, do_ref, part_ref):
        i = pl.program_id(0)
        bl, _, _, gl, ql, _, z, sig_g, sig_q, ylru, _, _ = _mix_gates(
            (b_ref, c_ref, u_ref, g_ref, q_ref), hf_ref, hr_ref, wca_ref, perm_ref, t, w)
        ya = bl * z * (gl * sig_g)
        yb = ylru * (ql * sig_q)
        ct_ref[0:w, :] = ya.T.astype(BF16)
        ct_ref[w:, :] = yb.T.astype(BF16)
        out = _dot(ya.astype(BF16), wo_ref[0:w, :]) + _dot(yb.astype(BF16), wo_ref[w:, :])
        gate, fg = ov_ref[0:1, :], ov_ref[1:2, :]
        n = x_ref[...] + gate * out
        rr = lax.rsqrt(jnp.mean(n * n, axis=-1, keepdims=True) + EPS)
        nh = n * rr
        e = nh * fg - tg_ref[...]
        loss = 0.5 * jnp.sum(jnp.mean(e * e, axis=-1, keepdims=True), axis=0, keepdims=True)
        dy = e * (1.0 / d)
        dnh = dy * fg
        dn = rr * (dnh - nh * jnp.mean(dnh * nh, axis=-1, keepdims=True))
        dn_ref[...] = dn.astype(BF16)
        do_ref[...] = (dn * gate).astype(BF16)

        @pl.when(i == 0)
        def _():
            part_ref[...] = jnp.zeros_like(part_ref)

        part_ref[0:1, :] += jnp.sum(dy * nh, axis=0, keepdims=True)
        part_ref[1:2, :] += jnp.sum(dn * out, axis=0, keepdims=True)
        part_ref[2:3, :] += jnp.broadcast_to(loss, (1, d))

    tile = lambda cols: pl.BlockSpec((t, cols), lambda i: (i, 0))
    full = lambda shape: pl.BlockSpec(shape, lambda i: (0,) * len(shape))
    return _call(
        body, name="mix_forward",
        grid=(nt,),
        out_shape=(jax.ShapeDtypeStruct((l, d), BF16), jax.ShapeDtypeStruct((d, l), BF16),
                   jax.ShapeDtypeStruct((l, d), BF16), jax.ShapeDtypeStruct((8, d), F32)),
        in_specs=[tile(d), tile(d)] + _p_specs(t, w, nt) + [tile(w), tile(w),
                  pl.BlockSpec((d, d), lambda i: (0, 0), pipeline_mode=pl.Buffered(1)),
                  full(ov.shape), full(wca.shape), full(perm.shape)],
        out_specs=(tile(d), pl.BlockSpec((d, t), lambda i: (0, i)), tile(d), full((8, d))),
        compiler_params=_params(("arbitrary",)),
    )(x, tgt, p, p, p, p, p, hf, hr, wo, ov, wca, perm)


def _mix_backward(dout, p, hf, hr, wo, wca, perm, g_wout, l, t):
    d = dout.shape[1]
    w = d // 2
    nt = l // t
    la = p.shape[0]

    def body(do_ref, b_ref, c_ref, u_ref, g_ref, q_ref, hf_ref, hr_ref, wo_ref, wca_ref, perm_ref, gw_ref,
             dp_ref, dh_ref, part_ref, sc_ref, send_sems, recv_sems):
        i = pl.program_id(0)
        copies = _scatter_copies(gw_ref, sc_ref, send_sems, recv_sems)

        @pl.when(i == 0)
        def _():
            part_ref[...] = jnp.zeros_like(part_ref)
            for cp in copies:
                cp.start()

        @pl.when(i == nt)
        def _():
            dp_ref[...] = jnp.zeros_like(dp_ref)
            _exchange_wait(copies, copies[1:])

        @pl.when(i < nt)
        def _():
            bl, cl, ul, gl, ql, tt, z, sig_g, sig_q, ylru, pos, rowlen = _mix_gates(
                (b_ref, c_ref, u_ref, g_ref, q_ref), hf_ref, hr_ref, wca_ref, perm_ref, t, w)
            do = do_ref[...]
            dya = _dot_nt(do, wo_ref[0:w, :])
            dyb = _dot_nt(do, wo_ref[w:, :])
            sg = gl * sig_g
            dz = dya * bl * sg
            dt = _conv3_t(dz, wca_ref, pos, rowlen)
            dp_ref[:, 0:w] = (dya * z * sg).astype(BF16)
            dp_ref[:, w:2 * w] = (dt * ul).astype(BF16)
            dp_ref[:, 2 * w:3 * w] = (dt * cl).astype(BF16)
            dp_ref[:, 3 * w:4 * w] = (dya * bl * z * (sig_g * (1.0 + gl * (1.0 - sig_g)))).astype(BF16)
            dp_ref[:, 4 * w:5 * w] = jnp.zeros((t, w), BF16)
            dp_ref[:, 5 * w:6 * w] = (dyb * ylru * (sig_q * (1.0 + ql * (1.0 - sig_q)))).astype(BF16)
            dh_ref[...] = _dot(perm_ref[0], (dyb * (ql * sig_q)).astype(BF16)).astype(BF16)
            part_ref[0:1, :] += jnp.sum(dz * _down(tt, 1, pos), axis=0, keepdims=True)
            part_ref[1:2, :] += jnp.sum(dz * tt, axis=0, keepdims=True)
            part_ref[2:3, :] += jnp.sum(dz * _up(tt, 1, pos, rowlen), axis=0, keepdims=True)

    clamp = lambda cols: pl.BlockSpec((t, cols), lambda i: (jnp.minimum(i, nt - 1), 0))
    full = lambda shape: pl.BlockSpec(shape, lambda i: (0,) * len(shape))
    return _call(
        body, name="mix_backward",
        grid=(nt + 1,),
        out_shape=(jax.ShapeDtypeStruct((la, 6 * w), BF16), jax.ShapeDtypeStruct((l, w), BF16),
                   jax.ShapeDtypeStruct((8, w), F32), jax.ShapeDtypeStruct(g_wout.shape, g_wout.dtype)),
        in_specs=[clamp(d)] + _p_specs(t, w, nt) + [clamp(w), clamp(w),
                  pl.BlockSpec((d, d), lambda i: (0, 0), pipeline_mode=pl.Buffered(1)), full(wca.shape),
                  full(perm.shape), ANY],
        out_specs=(pl.BlockSpec((t, 6 * w), lambda i: (i, 0)), clamp(w), full((8, w)), ANY),
        scratch_shapes=[pltpu.SemaphoreType.DMA((NDEV,)), pltpu.SemaphoreType.DMA((NDEV,))],
        compiler_params=_params(("arbitrary",)),
    )(dout, p, p, p, p, p, hf, hr, wo, wca, perm, g_wout)


def _lru_backward(direction, xb, dhs, hs, wg, lv, l, t, conv=None):
    la, w = hs.shape
    gc = wg.shape[2]
    ng = w // gc
    nt = l // t
    nblk8 = la // 8
    last = conv is not None
    assert last == (direction == 1)

    if direction == 0:
        tile = lambda i: jnp.where(i == nt, nt, nt - 1 - i)
        halo = lambda i: jnp.where(tile(i) == 0, nblk8 - 1, tile(i) * (t // 8) - 1)
    else:
        tile = lambda i: i
        halo = lambda i: jnp.minimum((i + 1) * (t // 8), nblk8 - 1)

    def body(*refs):
        x_ref, dh_ref, hs_ref, halo_ref, wg_ref, lv_ref = refs[:6]
        if last:
            v_ref, wcb_ref, bm_ref, dxo_ref = refs[6:10]
        out_ref, dwg_ref, part_ref, a_s, dh_s, g_s, carry = refs[-7:]
        i = pl.program_id(0)
        is_ctx = i == nt

        @pl.when(i == 0)
        def _():
            carry[...] = jnp.zeros_like(carry)
            dwg_ref[...] = jnp.zeros_like(dwg_ref)
            part_ref[...] = jnp.zeros_like(part_ref)

        xb = x_ref[...]
        lam = lv_ref[3 * direction + 2:3 * direction + 3, :]
        a, s, rs, tr, ti, sp = _lru_coef(xb, wg_ref, direction, lv_ref[3 * direction:3 * direction + 1, :],
                                         lv_ref[3 * direction + 1:3 * direction + 2, :], lam, gc)
        hs_t = hs_ref[...]
        r8 = _rows((8, w))
        if direction == 0:
            edge = jnp.where(is_ctx, 0.0, halo_ref[7:8, :])
            first = jnp.where(r8 == 0, edge, pltpu.roll(hs_t[t - 8:, :], 1, 0))
            hprev = jnp.concatenate([first, hs_t[:t - 8, :]], axis=0)
        else:
            edge = jnp.where(is_ctx, 0.0, halo_ref[0:1, :])
            final = jnp.where(r8 == 7, edge, pltpu.roll(hs_t[:8, :], 7, 0))
            hprev = jnp.concatenate([hs_t[8:, :], final], axis=0)
        a_s[...] = a
        dh_s[...] = jnp.where(is_ctx, 0.0, dh_ref[...].astype(F32))
        carry[...] = _scan_tile_backward(a_s, dh_s, g_s, carry[...], direction == 0)

        g = g_s[...]
        r = 0.5 * tr + 0.5
        ig = 0.5 * ti + 0.5
        ix = ig * xb
        gs = g * s
        dla = (g * a) * (hprev - ix * (a * rs))
        dxb = gs * ig
        dzr = dla * (r * (1.0 - tr)) * (-LRU_C * sp)
        dzi = gs * ix * (1.0 - ti)
        part_ref[0:1, :] += jnp.sum(dzr, axis=0, keepdims=True)
        part_ref[1:2, :] += jnp.sum(dzi, axis=0, keepdims=True)
        part_ref[2:3, :] += jnp.sum(dla * r, axis=0, keepdims=True) * (LRU_C * _sigmoid(-lam))
        pieces = []
        for gi in range(ng):
            sl = slice(gi * gc, (gi + 1) * gc)
            dz = jnp.concatenate([dzr[:, sl], dzi[:, sl]], axis=-1).astype(BF16)
            pieces.append(_dot_nt(dz, wg_ref[direction, gi]))
            dwg_ref[gi] += _dot(xb[:, sl].T.astype(BF16), dz)
        dxb = dxb + (pieces[0] if ng == 1 else jnp.concatenate(pieces, axis=-1))
        if not last:
            out_ref[...] = dxb
        else:
            dxb = dxb + dxo_ref[...]
            dxb16, v = dxb.astype(BF16), v_ref[...].astype(F32)
            dv = jnp.zeros((t, w), F32)
            for j in range(4):
                back = _dot(bm_ref[j], dxb16)
                dv = dv + wcb_ref[j:j + 1, :] * back
                part_ref[4 + j:5 + j, :] += jnp.sum(back * v, axis=0, keepdims=True)
            out_ref[...] = dv.astype(BF16)
            part_ref[3:4, :] += jnp.sum(dxb, axis=0, keepdims=True)

    full = lambda shape: pl.BlockSpec(shape, lambda i: (0,) * len(shape))
    kind = lambda i: (jnp.where(i == nt, 1, 0), 0, 0, 0)
    in_specs = [pl.BlockSpec((t, w), lambda i: (tile(i), 0)),
                pl.BlockSpec((t, w), lambda i: (jnp.minimum(tile(i), nt - 1), 0)),
                pl.BlockSpec((t, w), lambda i: (tile(i), 0)),
                pl.BlockSpec((8, w), lambda i: (halo(i), 0)),
                full(wg.shape), full(lv.shape)]
    args = [xb, dhs, hs, hs, wg, lv]
    if last:
        p, wcb, back_m, dxb_other, dp = conv
        in_specs += [pl.BlockSpec((t, w), lambda i: (tile(i), 4)), full(wcb.shape),
                     pl.BlockSpec((None, 4, t, t), kind), pl.BlockSpec((t, w), lambda i: (tile(i), 0)), ANY]
        args += [p, wcb, back_m, dxb_other, dp]
        out0 = jax.ShapeDtypeStruct(dp.shape, dp.dtype)
        spec0 = pl.BlockSpec((t, w), lambda i: (tile(i), 4))
        aliases = {10: 0}
    else:
        out0 = jax.ShapeDtypeStruct((la, w), F32)
        spec0 = pl.BlockSpec((t, w), lambda i: (tile(i), 0))
        aliases = {}
    return _call(
        body, name="lru_backward_%d" % direction,
        grid=(nt + 1,),
        out_shape=(out0, jax.ShapeDtypeStruct((ng, gc, 2 * gc), F32), jax.ShapeDtypeStruct((8, w), F32)),
        in_specs=in_specs,
        out_specs=(spec0, full((ng, gc, 2 * gc)), full((8, w))),
        scratch_shapes=[pltpu.VMEM((t, w), F32), pltpu.VMEM((t, w), F32), pltpu.VMEM((t, w), F32),
                        pltpu.VMEM((8, w), F32)],
        input_output_aliases=aliases,
        compiler_params=_params(("arbitrary",)),
    )(*args)


def _weight_grad_t(at, b, nblk_m, nblk_n, tk, name):
    m, k = at.shape
    n = b.shape[1]
    bm, bn = m // nblk_m, n // nblk_n
    nk = k // tk

    def body(a_ref, b_ref, o_ref, acc):
        kk = pl.program_id(2)

        @pl.when(kk == 0)
        def _():
            acc[...] = jnp.zeros_like(acc)

        acc[...] += _dot(a_ref[...], b_ref[...])

        @pl.when(kk == nk - 1)
        def _():
            o_ref[...] = acc[...].astype(BF16)

    return _call(
        body, name=name,
        grid=(nblk_m, nblk_n, nk),
        out_shape=jax.ShapeDtypeStruct((nblk_m * nblk_n, bm, bn), BF16),
        in_specs=[pl.BlockSpec((bm, tk), lambda i, j, kk: (i, kk)),
                  pl.BlockSpec((tk, bn), lambda i, j, kk: (kk, j))],
        out_specs=pl.BlockSpec((None, bm, bn), lambda i, j, kk: (i * nblk_n + j, 0, 0)),
        scratch_shapes=[pltpu.VMEM((bm, bn), F32)],
        compiler_params=_params(("arbitrary", "arbitrary", "arbitrary")),
    )(at, b)


def _weight_grad_scatter(at, b, tk, name):
    m, k = at.shape
    n = b.shape[1]
    bn = n // NDEV
    nk = k // tk
    where = jnp.stack([_idx(_my_pos()), lax.axis_index("c")]).astype(jnp.int32)

    def body(w_ref, a_ref, b_ref, recv_ref, acc, sbuf, sib, sib_send, sib_recv, chip_send, chip_recv, keep_sem):
        s, kk = pl.program_id(0), pl.program_id(1)
        x, y, c = _my_pos()

        @pl.when(kk == 0)
        def _():
            acc[...] = _dot(a_ref[...], b_ref[...])

        @pl.when(kk > 0)
        def _():
            acc[...] += _dot(a_ref[...], b_ref[...])

        def to_sibling(j):
            return pltpu.make_async_remote_copy(
                src_ref=sbuf.at[0], dst_ref=sib.at[j], send_sem=sib_send.at[j], recv_sem=sib_recv.at[j],
                device_id=(x, y, 1 - c), device_id_type=MESH)

        def to_chip(j):
            dist = _chip_order(j, c)
            return pltpu.make_async_remote_copy(
                src_ref=sbuf.at[1], dst_ref=recv_ref.at[dist // 2], send_sem=chip_send.at[j],
                recv_sem=chip_recv.at[dist // 2], device_id=_peer_at(dist), device_id_type=MESH)

        keep = pltpu.make_async_copy(sbuf.at[1], recv_ref.at[0], keep_sem)
        sends = []
        for j in range(4):
            sends += [to_sibling(j), to_chip(j) if j < 3 else keep]

        for st in range(NDEV):
            @pl.when((kk == nk - 1) & (s == st))
            def _(st=st):
                if st >= 2:
                    sends[st - 2].wait_send()
                part = acc[...]
                if st % 2 == 1:
                    to_sibling(st // 2).wait_recv()
                    part = part + sib[st // 2].astype(F32)
                sbuf[st % 2] = part.astype(BF16)
                sends[st].start()
                if st == NDEV - 1:
                    sends[st - 1].wait_send()
                    sends[st].wait()
                    for j in range(1, 4):
                        pltpu.make_async_remote_copy(
                            src_ref=sbuf.at[0], dst_ref=recv_ref.at[j], send_sem=chip_send.at[0],
                            recv_sem=chip_recv.at[j], device_id=_peer_at(2 * j), device_id_type=MESH).wait_recv()

    blk = lambda s, w_ref: w_ref[0] ^ _scatter_order(s, w_ref[1])
    return _call(
        body, name=name,
        grid_spec=pltpu.PrefetchScalarGridSpec(
            num_scalar_prefetch=1, grid=(NDEV, nk),
            in_specs=[pl.BlockSpec((m, tk), lambda s, kk, w_ref: (0, kk)),
                      pl.BlockSpec((tk, bn), lambda s, kk, w_ref: (kk, blk(s, w_ref)))],
            out_specs=ANY,
            scratch_shapes=[pltpu.VMEM((m, bn), F32), pltpu.VMEM((2, m, bn), BF16), pltpu.VMEM((4, m, bn), BF16),
                            pltpu.SemaphoreType.DMA((4,)), pltpu.SemaphoreType.DMA((4,)),
                            pltpu.SemaphoreType.DMA((4,)), pltpu.SemaphoreType.DMA((4,)),
                            pltpu.SemaphoreType.DMA]),
        out_shape=jax.ShapeDtypeStruct((4, m, bn), BF16),
        compiler_params=_params(("arbitrary", "arbitrary")),
    )(where, at, b)


def _input_backward(dp, w_all, src, mv, row0, tm, nbk, name, dn=None):
    rows, d = src.shape
    nb, _, bw = w_all.shape
    nk = nb // nbk
    ni = rows // tm
    blk0 = row0 // tm
    latent = dn is not None

    def body(*refs):
        dp_ref, w_ref, x_ref, mv_ref = refs[:4]
        outs = refs[4 + latent:]
        part_ref, acc = outs[latent], outs[latent + 1]
        i, k = pl.program_id(0), pl.program_id(1)

        def product():
            step = _dot_nt(dp_ref[:, 0:bw], w_ref[0])
            for q in range(1, nbk):
                step = step + _dot_nt(dp_ref[:, q * bw:(q + 1) * bw], w_ref[q])
            return step

        def finish(slot):
            xf = x_ref[...]
            r = lax.rsqrt(jnp.mean(xf * xf, axis=-1, keepdims=True) + EPS)
            xn = xf * r
            dhl = acc[slot]
            gain, sc = mv_ref[0:1, :], mv_ref[1:2, :]
            dhx = jnp.sum(dhl * xn, axis=0, keepdims=True)
            part_ref[0:1, :] += jnp.sum(dhl, axis=0, keepdims=True)
            part_ref[1:2, :] += dhx * gain
            part_ref[2:3, :] += dhx * (1.0 + sc)
            if latent:
                dxn = dhl * (gain * (1.0 + sc))
                outs[0][...] = (refs[4][...].astype(F32)
                                + r * (dxn - xn * jnp.mean(dxn * xn, axis=-1, keepdims=True)))

        @pl.when((i == 0) & (k == 0))
        def _():
            part_ref[...] = jnp.zeros_like(part_ref)
            acc[0] = product()

        @pl.when((i > 0) & (i < ni) & (k == 0))
        def _():
            acc[i % 2] = product()
            finish((i - 1) % 2)

        @pl.when((i == ni) & (k == 0))
        def _():
            finish((ni - 1) % 2)

        @pl.when((i < ni) & (k > 0))
        def _():
            acc[i % 2] += product()

    tile = pl.BlockSpec((tm, d), lambda i, k: (jnp.maximum(i - 1, 0), 0))
    vec = pl.BlockSpec((8, d), lambda i, k: (0, 0))
    kblock = lambda i, k: jnp.where(i == ni, nk - 1, k)
    return _call(
        body, name=name,
        grid=(ni + 1, nk),
        out_shape=((jax.ShapeDtypeStruct((rows, d), F32),) if latent else ()) + (jax.ShapeDtypeStruct((8, d), F32),),
        in_specs=[pl.BlockSpec((tm, nbk * bw), lambda i, k: (blk0 + jnp.minimum(i, ni - 1), kblock(i, k))),
                  pl.BlockSpec((nbk, d, bw), lambda i, k: (kblock(i, k), 0, 0)), tile, vec]
                 + ([tile] if latent else []),
        out_specs=((tile,) if latent else ()) + (vec,),
        scratch_shapes=[pltpu.VMEM((2, tm, d), F32)],
        compiler_params=_params(("arbitrary", "arbitrary")),
    )(*([dp, w_all, src, mv] + ([dn] if latent else [])))


def _adamw_scattered(parts, w, m, v, tr):
    r, c = w.shape
    nslot = parts.shape[0]

    def body(p_ref, w_ref, m_ref, v_ref, g_ref, d_ref, m2_ref, v2_ref):
        g = p_ref[0].astype(F32)
        for k in range(1, nslot):
            g = g + p_ref[k].astype(F32)
        g_ref[...] = g
        d_ref[...], m2_ref[...], v2_ref[...] = _adamw(w_ref[...], g, m_ref[...], v_ref[...])

    tile = pl.BlockSpec((tr, c), lambda i: (i, 0))
    return _call(
        body, name="adamw_scattered_%dx%d" % (r, c),
        grid=(r // tr,),
        out_shape=tuple(jax.ShapeDtypeStruct((r, c), F32) for _ in range(4)),
        in_specs=[pl.BlockSpec((nslot, tr, c), lambda i: (0, i, 0)), tile, tile, tile],
        out_specs=(tile,) * 4,
        compiler_params=_params(("arbitrary",)),
    )(parts, w, m, v)


def _adamw_ada(st, dmod, w, m, v, tr):
    r, c = w.shape

    def body(s_ref, dm_ref, w_ref, m_ref, v_ref, g_ref, d_ref, m2_ref, v2_ref):
        g = jnp.dot(s_ref[...], dm_ref[...], precision=HIGHEST, preferred_element_type=F32)
        g_ref[...] = g
        d_ref[...], m2_ref[...], v2_ref[...] = _adamw(w_ref[...], g, m_ref[...], v_ref[...])

    tile = pl.BlockSpec((tr, c), lambda i: (i, 0))
    return _call(
        body, name="adamw_ada",
        grid=(r // tr,),
        out_shape=tuple(jax.ShapeDtypeStruct((r, c), F32) for _ in range(4)),
        in_specs=[pl.BlockSpec((tr, 16), lambda i: (i, 0)), pl.BlockSpec((16, c), lambda i: (0, 0)),
                  tile, tile, tile],
        out_specs=(tile,) * 4,
        compiler_params=_params(("arbitrary",)),
    )(st, dmod, w, m, v)


def _adamw_small(gs, ws, ms, vs):
    n = len(ws)

    def body(*refs):
        for j in range(n):
            g_ref, w_ref, m_ref, v_ref = refs[j], refs[n + j], refs[2 * n + j], refs[3 * n + j]
            d_ref, m2_ref, v2_ref = refs[4 * n + j], refs[5 * n + j], refs[6 * n + j]
            d_ref[...], m2_ref[...], v2_ref[...] = _adamw(w_ref[...], g_ref[...], m_ref[...], v_ref[...])

    shapes = tuple(jax.ShapeDtypeStruct(a.shape, F32) for a in ws)
    out = _call(
        body, name="adamw_small",
        out_shape=shapes * 3,
        in_specs=[VMEM] * (4 * n), out_specs=(VMEM,) * (3 * n),
        compiler_params=_params(),
    )(*gs, *ws, *ms, *vs)
    return list(out[:n]), list(out[n:2 * n]), list(out[2 * n:])


def _blockdiag_groups(wh, gc):
    h, dh, _ = wh.shape
    g = gc // dh
    w4 = wh.reshape(h // g, g, dh, dh)
    bd = jnp.einsum("ngij,gh->ngihj", w4, jnp.eye(g, dtype=wh.dtype))
    return bd.reshape(h // g, gc, gc)


def _blockdiag_extract(bd, dh):
    ng, gc, _ = bd.shape
    g = gc // dh
    x = bd.reshape(ng, g, dh, g, dh)
    return jnp.einsum("ngihj,gh->ngij", x, jnp.eye(g, dtype=bd.dtype)).reshape(ng * g, dh, dh)


def _rows8(*vecs):
    rows = [jnp.reshape(v, (1, -1)).astype(F32) for v in vecs]
    n = rows[0].shape[1]
    return jnp.concatenate(rows + [jnp.zeros((8 - len(rows), n), F32)], axis=0)


def _pack(pieces):
    flat = jnp.concatenate([jnp.reshape(a, (-1,)).astype(F32) for a in pieces])
    total = -(-flat.shape[0] // 1024) * 1024
    return jnp.pad(flat, (0, total - flat.shape[0])).reshape(total // 128, 128)


def _unpack(packed, shapes):
    flat = packed.reshape(-1)
    out, off = [], 0
    for s in shapes:
        n = 1
        for q in s:
            n *= q
        out.append(flat[off:off + n].reshape(s))
        off += n
    return out


def kernel(x, c, ctx, c_ctx, norm_g, w_ada, b_ada, w_in, w_conv_a, w_conv_b, b_conv_b, lru_wa, lru_ba, lru_wx, lru_bx, lru_lambda, w_out, final_g, loss_target, m_c_ctx, m_norm_g, m_w_ada, m_b_ada, m_w_in, m_w_conv_a, m_w_conv_b, m_b_conv_b, m_lru_wa, m_lru_ba, m_lru_wx, m_lru_bx, m_lru_lambda, m_w_out, m_final_g, v_c_ctx, v_norm_g, v_w_ada, v_b_ada, v_w_in, v_w_conv_a, v_w_conv_b, v_b_conv_b, v_lru_wa, v_lru_ba, v_lru_wx, v_lru_bx, v_lru_lambda, v_w_out, v_final_g):
    _, l, d = x.shape
    lc = ctx.shape[1]
    w = d // 2
    t = lc
    assert l % t == 0 and t % GRID_W == 0 and t % 128 == 0
    dh = w // N_HEADS
    gc = min(w, MXU_WIDTH)
    cols = w_ada.shape[2]
    wo_rows = w_out.shape[1]
    me = _idx(_my_pos())
    x2, ctx2, tgt2 = x[0], ctx[0], loss_target[0]
    w_ada2, w_in2, w_out2 = w_ada[0], w_in[0], w_out[0]

    small_mine = jnp.concatenate([a.reshape(-1) for a in (w_conv_a, w_conv_b, lru_ba, lru_bx, lru_lambda)]
                                 + [jnp.zeros((3 * (w // NDEV),), F32)]).reshape(16, w // NDEV)
    mod_all, s_mat, small_all = _mod_forward(
        jnp.broadcast_to(c, (8, d)), jnp.broadcast_to(c_ctx[None], (8, d)), w_ada2, small_mine)
    mod = jnp.transpose(mod_all, (1, 0, 2)).reshape(16, NDEV * cols) + b_ada
    mod_lat = lax.dynamic_slice_in_dim(mod, me, 1, axis=0)
    sh_l, sc_l, gt_l = jnp.split(mod_lat, 3, axis=-1)
    sh_c, sc_c, _ = jnp.split(mod[8:9], 3, axis=-1)
    small = jnp.transpose(small_all, (1, 0, 2)).reshape(16, w)
    wca = _rows8(*[small[j] for j in range(0, 3)])
    wcb = _rows8(*[small[j] for j in range(3, 7)], b_conv_b)
    lv = _rows8(0.5 * small[7], 0.5 * small[9], small[11], 0.5 * small[8], 0.5 * small[10], small[12])
    wg = jnp.stack([
        jnp.concatenate([_blockdiag_groups(lru_wa[0, dr], gc), _blockdiag_groups(lru_wx[0, dr], gc)], axis=-1)
        for dr in range(2)])
    wg = (0.5 * wg).astype(BF16)

    la = l + lc
    tm = 2 * t if l % (2 * t) == 0 else t
    tk = 3 * t if la % (3 * t) == 0 else t
    h, hlt = _normalize(x2, _rows8(norm_g, sc_l, sh_l), la, 0, tm, "normalize")
    h, hlt = _normalize(ctx2, _rows8(norm_g, sc_c, sh_c), la, l, t, "normalize_ctx", prev=(h, hlt))
    p, w_all = _in_projection(h, w_in2.astype(BF16), la // 8 if la % 128 == 0 else tk)
    taps_m, back_m, perm = _scan_matrices(t)
    xb = _conv_input(p, wcb, taps_m, l, t)
    hf, hr, wo_all = _lru_forward(xb, wg, lv, w_out2.astype(BF16), l, t)
    wo = wo_all.reshape(d, d)
    dn, catt, dout, part_mix = _mix_forward(x2, tgt2, p, hf, hr, wo, _rows8(gt_l, final_g), wca, perm, t)
    g_wout = _weight_grad_t(catt, dout, 2, 1, 4 * t if l % (4 * t) == 0 else t, "grad_w_out")
    dp, dhs, part_ca, sc_wout = _mix_backward(dout, p, hf, hr, wo, wca, perm, g_wout.reshape(NDEV, wo_rows, d), l, t)
    dxb0, dwg0, part_l0 = _lru_backward(0, xb, dhs, hf, wg, lv, l, t)
    dp, dwg1, part_l1 = _lru_backward(1, xb, dhs, hr, wg, lv, l, t, conv=(p, wcb, back_m, dxb0, dp))
    sc_win = _weight_grad_scatter(hlt, dp, tk, "grad_w_in")
    grad_x, part_lat = _input_backward(dp, w_all, x2, _rows8(norm_g, sc_l), 0, tm, 2, "input_backward", dn=dn)
    (part_ctx,) = _input_backward(dp, w_all, ctx2, _rows8(norm_g, sc_c), l, t, 2, "input_backward_ctx")
    part_in = jnp.concatenate([part_lat[0:2], part_ctx[0:2], (part_lat[2] + part_ctx[2])[None]], axis=0)

    dwa = jnp.stack([_blockdiag_extract(dwg0[:, :, :gc], dh), _blockdiag_extract(dwg1[:, :, :gc], dh)])
    dwx = jnp.stack([_blockdiag_extract(dwg0[:, :, gc:], dh), _blockdiag_extract(dwg1[:, :, gc:], dh)])
    lru_part = (0.5 * jnp.stack([dwa, dwx])).reshape(NDEV, -1, 128)
    zeros_d = jnp.zeros((d,), F32)
    pieces = [
        jnp.concatenate([part_in[0], part_in[1], part_mix[1]]),
        jnp.concatenate([part_in[2], part_in[3], zeros_d]),
        part_in[4], part_mix[0], part_ca[0:3], part_l1[4:8], part_l1[3],
        0.5 * jnp.stack([part_l0[0], part_l1[0]]), 0.5 * jnp.stack([part_l0[1], part_l1[1]]),
        jnp.stack([part_l0[2], part_l1[2]]), part_mix[2, 0:1],
    ]
    shapes = [(3 * d,), (3 * d,), (d,), (d,), (3, w), (4, w), (w,), (2, w), (2, w), (2, w), (1,)]
    sig_cc = jax.nn.sigmoid(c_ctx)
    dsilu_cc = jnp.broadcast_to((sig_cc * (1.0 + c_ctx * (1.0 - sig_cc)))[None], (8, d))
    psum, pall, lru_sum, g_cctx8 = _reduce_small(_pack(pieces), lru_part, w_ada2, dsilu_cc)
    (g_modl, g_modc, g_norm, g_final, g_ca, g_cb, g_bcb, g_ba, g_bx, g_lam, loss1) = _unpack(psum, shapes)
    loss = loss1[0]
    g_cctx = g_cctx8[0]
    g_bada = (g_modl + g_modc)[None]
    g_lru = lru_sum.reshape(2, 2, N_HEADS, dh, dh)
    g_wa, g_wx = g_lru[0][None], g_lru[1][None]
    wsl = w // NDEV
    mine = lambda a: lax.dynamic_slice_in_dim(a, me * wsl, wsl, axis=-1)
    g_ca_m, g_cb_m, g_ba_m, g_bx_m, g_lam_m = (mine(g_ca)[None], mine(g_cb)[None], mine(g_ba)[None],
                                               mine(g_bx)[None], mine(g_lam)[None])
    g_norm, g_bcb = g_norm[None], g_bcb[None]

    cb = cols // 128
    per_dev = pall[:, :3 * d // 128].reshape(NDEV, NDEV, cols)
    dmod_lat = lax.dynamic_slice_in_dim(per_dev, me, 1, axis=1)[:, 0]
    dmod_ctx = lax.dynamic_slice_in_dim(g_modc.reshape(NDEV, cols), me, 1, axis=0)
    dmod16 = jnp.concatenate([dmod_lat, dmod_ctx, jnp.zeros((7, cols), F32)], axis=0)
    tr_ada = 256 if d % 256 == 0 else d
    g_wada, d_wada, m_wada, v_wada = _adamw_ada(s_mat.T, dmod16, w_ada2, m_w_ada[0], v_w_ada[0], tr_ada)
    g_win2, d_win, m_win, v_win = _adamw_scattered(sc_win, w_in2, m_w_in[0], v_w_in[0], tr_ada)
    tr_out = 64 if wo_rows % 64 == 0 else wo_rows
    g_wout2, d_wout, m_wout, v_wout = _adamw_scattered(sc_wout, w_out2, m_w_out[0], v_w_out[0], tr_out)

    small_w = [c_ctx, norm_g, b_ada, w_conv_a, w_conv_b, b_conv_b, lru_wa, lru_ba, lru_wx, lru_bx, lru_lambda, final_g]
    small_m = [m_c_ctx, m_norm_g, m_b_ada, m_w_conv_a, m_w_conv_b, m_b_conv_b, m_lru_wa, m_lru_ba, m_lru_wx,
               m_lru_bx, m_lru_lambda, m_final_g]
    small_v = [v_c_ctx, v_norm_g, v_b_ada, v_w_conv_a, v_w_conv_b, v_b_conv_b, v_lru_wa, v_lru_ba, v_lru_wx,
               v_lru_bx, v_lru_lambda, v_final_g]
    small_g = [g_cctx, g_norm, g_bada, g_ca_m, g_cb_m, g_bcb, g_wa, g_ba_m, g_wx, g_bx_m, g_lam_m, g_final]
    small_g = [jnp.reshape(a, b.shape) for a, b in zip(small_g, small_w)]
    d_s, m_s, v_s = _adamw_small(small_g, small_w, small_m, small_v)

    def weights(small_list, ada, win, wout):
        (cctx_, norm_, bada_, ca_, cb_, bcb_, wa_, ba_, wx_, bx_, lam_, final_) = small_list
        return [cctx_, norm_, ada[None], bada_, win[None], ca_, cb_, bcb_, wa_, ba_, wx_, bx_, lam_, wout[None], final_]

    return (loss, grad_x[None],
            *weights(small_g, g_wada, g_win2, g_wout2), *weights(d_s, d_wada, d_win, d_wout),
            *weights(m_s, m_wada, m_win, m_wout), *weights(v_s, v_wada, v_win, v_wout))
```

```python
import functools

import jax
import jax.numpy as jnp
import numpy as np
from jax import lax
from jax.experimental import pallas as pl
from jax.experimental.pallas import tpu as pltpu

F32 = jnp.float32
BF16 = jnp.bfloat16
MESH = pl.DeviceIdType.MESH
NDEV = 8
GRID_W = 64
N_HEADS = 16
LRU_C = 8.0
EPS = 1e-6
MXU_WIDTH = 256
VMEM_LIMIT = 60 * 1024 * 1024

ADAM_LR = 0.001
ADAM_B1 = 0.9
ADAM_B2 = 0.999
ADAM_EPS = 1e-08
ADAM_WD = 0.01
ADAM_STEP = 10
ADAM_C1 = 1.0 - ADAM_B1 ** ADAM_STEP
ADAM_C2 = 1.0 - ADAM_B2 ** ADAM_STEP

HIGHEST = lax.Precision.HIGHEST
ANY = pl.BlockSpec(memory_space=pl.ANY)
VMEM = pl.BlockSpec(memory_space=pltpu.VMEM)


def _call(body, **kw):
    return pl.pallas_call(body, **kw)


def _params(sem=None, vmem=VMEM_LIMIT):
    return pltpu.CompilerParams(dimension_semantics=sem, vmem_limit_bytes=vmem)


def _my_pos():
    return lax.axis_index("x"), lax.axis_index("y"), lax.axis_index("c")


def _idx(pos):
    return 4 * pos[0] + 2 * pos[1] + pos[2]


def _peer(k):
    x, y, c = _my_pos()
    return ((1 - x) if (k >> 2) & 1 else x, (1 - y) if (k >> 1) & 1 else y, (1 - c) if k & 1 else c)


def _exchange_start(src_ref, dst_ref, send_sems, recv_sems, base):
    me = _idx(_my_pos())
    sends = []
    for k in range(1, NDEV):
        cp = pltpu.make_async_remote_copy(
            src_ref=src_ref, dst_ref=dst_ref.at[me], send_sem=send_sems.at[base + k - 1],
            recv_sem=recv_sems.at[base + k - 1], device_id=_peer(k), device_id_type=MESH)
        cp.start()
        sends.append(cp)
    dst_ref[me] = src_ref[...]
    return sends, (src_ref, dst_ref, send_sems, recv_sems, base)


def _exchange_finish(started):
    sends, (src_ref, dst_ref, send_sems, recv_sems, base) = started
    for k in range(1, NDEV):
        peer = _peer(k)
        pltpu.make_async_remote_copy(
            src_ref=src_ref, dst_ref=dst_ref.at[_idx(peer)], send_sem=send_sems.at[base + k - 1],
            recv_sem=recv_sems.at[base + k - 1], device_id=peer, device_id_type=MESH).wait_recv()
    for cp in sends:
        cp.wait_send()


def _exchange_vmem(src_ref, dst_ref, send_sems, recv_sems, base):
    _exchange_finish(_exchange_start(src_ref, dst_ref, send_sems, recv_sems, base))


def _sigmoid(z):
    return 0.5 * jnp.tanh(0.5 * z) + 0.5


def _softplus(x):
    return jnp.maximum(x, 0.0) + jnp.log1p(jnp.exp(-jnp.abs(x)))


def _one_minus_sq(a, la):
    series = (-2.0 * la) * (1.0 + la)
    return jnp.where(la > -0.0015, series, 1.0 - a * a)


def _dot(a, b):
    return jnp.dot(a, b, preferred_element_type=F32)


def _dot_nt(a, b):
    return lax.dot_general(a, b, (((1,), (1,)), ((), ())), preferred_element_type=F32)


def _rows(shape):
    return lax.broadcasted_iota(jnp.int32, shape, 0)


def _scan_matrices(t):
    seg = t // 8
    r = np.arange(t)
    perm = (np.arange(t)[None, :] == ((r % 8) * seg + r // 8)[:, None]).astype(np.float32)
    rows, cols = r[:, None], r[None, :]
    taps, back = [], []
    for rowlen in (GRID_W, t):
        pos = rows % rowlen
        shift = {-2: (cols == rows - 2) & (pos >= 2), -1: (cols == rows - 1) & (pos >= 1),
                 0: cols == rows, 1: (cols == rows + 1) & (pos + 1 < rowlen),
                 2: (cols == rows + 2) & (pos + 2 < rowlen)}
        if rowlen == GRID_W:
            beside = [shift[-1].astype(np.float32), shift[1].astype(np.float32)]
        taps.append(np.stack([perm @ shift[k].astype(np.float32) for k in (-2, -1, 0, 1)]))
        back.append(np.stack([shift[k].astype(np.float32) @ perm.T for k in (2, 1, 0, -1)]))
    as_bf16 = lambda a: jnp.asarray(a, dtype=BF16)
    return as_bf16(np.stack(taps)), as_bf16(np.stack(back)), as_bf16(np.stack([perm, perm.T] + beside))


def _chunk_scan(a, b, reverse):
    row = _rows(a.shape)
    for s in (1, 2, 4):
        if reverse:
            m = row < 8 - s
            sh = 8 - s
        else:
            m = row >= s
            sh = s
        a_s = jnp.where(m, pltpu.roll(a, sh, 0), 1.0)
        b_s = jnp.where(m, pltpu.roll(b, sh, 0), 0.0)
        b = b + a * b_s
        a = a * a_s
    return a, b


def _chain_segments(ptot, hend, carry, reverse):
    ca, cb = _chunk_scan(ptot, hend, reverse)
    incl = ca * carry + cb
    r8 = _rows(incl.shape)
    if reverse:
        start = jnp.where(r8 < 7, pltpu.roll(incl, 7, 0), carry)
        last = incl[0:1, :]
    else:
        start = jnp.where(r8 >= 1, pltpu.roll(incl, 1, 0), carry)
        last = incl[7:8, :]
    return start, jnp.broadcast_to(last, incl.shape)


def _blocks(nblock, reverse):
    order = range(nblock - 1, -1, -1) if reverse else range(nblock)
    return [slice(8 * k, 8 * k + 8) for k in order]


def _scan_tile(a_ref, b_ref, out_ref, carry, reverse):
    t, w = a_ref.shape
    seg = t // 8

    hend, ptot = jnp.zeros((8, w), F32), jnp.ones((8, w), F32)
    for rows in _blocks(seg, reverse):
        a = a_ref[rows, :]
        hend, ptot = a * hend + b_ref[rows, :], a * ptot
    h, new_carry = _chain_segments(ptot, hend, carry, reverse)
    for rows in _blocks(seg, reverse):
        h = a_ref[rows, :] * h + b_ref[rows, :]
        out_ref[rows, :] = h
    return new_carry


def _scan_tile_backward(a_ref, dh_ref, g_ref, carry, reverse):
    t, w = a_ref.shape
    seg = t // 8

    uend, ptot = jnp.zeros((8, w), F32), jnp.ones((8, w), F32)
    for rows in _blocks(seg, reverse):
        a = a_ref[rows, :]
        uend, ptot = a * (dh_ref[rows, :] + uend), a * ptot
    u, new_carry = _chain_segments(ptot, uend, carry, reverse)
    for rows in _blocks(seg, reverse):
        g = dh_ref[rows, :] + u
        g_ref[rows, :] = g
        u = a_ref[rows, :] * g
    return new_carry


def _lru_coef(xb, wg_ref, d, ba, bx, lam, gc):
    w = xb.shape[1]
    xb16 = xb.astype(BF16)
    zr, zi = [], []
    for g in range(w // gc):
        z = _dot(xb16[:, g * gc:(g + 1) * gc], wg_ref[d, g])
        zr.append(z[:, :gc])
        zi.append(z[:, gc:])
    zr = zr[0] if len(zr) == 1 else jnp.concatenate(zr, axis=-1)
    zi = zi[0] if len(zi) == 1 else jnp.concatenate(zi, axis=-1)
    tr = jnp.tanh(zr + ba)
    ti = jnp.tanh(zi + bx)
    sp = _softplus(-lam)
    half = -0.5 * LRU_C * sp
    la = tr * half + half
    a = jnp.exp(la)
    q = _one_minus_sq(a, la)
    rs = lax.rsqrt(jnp.maximum(q, 1e-30))
    return a, q * rs, rs, tr, ti, sp


def _adamw(w, g, m, v):
    m2 = ADAM_B1 * m + (1.0 - ADAM_B1) * g
    v2 = ADAM_B2 * v + (1.0 - ADAM_B2) * (g * g)
    m_hat = m2 / ADAM_C1
    v_hat = v2 / ADAM_C2
    delta = -ADAM_LR * (m_hat / (jnp.sqrt(v_hat) + ADAM_EPS) + ADAM_WD * w)
    return delta, m2, v2


def _mod_forward(c8, cctx8, w_ada, small):
    d = c8.shape[1]
    cols = w_ada.shape[1]

    def body(c_ref, cctx_ref, w_ref, sm_ref, mod_ref, s_ref, sm_all, cbuf, mod_my, send_sems, recv_sems):
        _exchange_vmem(sm_ref, sm_all, send_sems, recv_sems, 2 * (NDEV - 1))
        _exchange_vmem(c_ref, cbuf, send_sems, recv_sems, 0)
        row = _rows((8, d))
        c_all = jnp.zeros((8, d), F32)
        for b in range(NDEV):
            c_all = jnp.where(row == b, cbuf[b], c_all)
        cc = cctx_ref[...]
        s_top = c_all * _sigmoid(c_all)
        s_bot = jnp.where(row == 0, cc * _sigmoid(cc), 0.0)
        s = jnp.concatenate([s_top, s_bot], axis=0)
        s_ref[...] = s
        mod_my[...] = jnp.dot(s, w_ref[...], precision=HIGHEST, preferred_element_type=F32)
        _exchange_vmem(mod_my, mod_ref, send_sems, recv_sems, NDEV - 1)

    return _call(
        body, name="mod_forward",
        out_shape=(jax.ShapeDtypeStruct((NDEV, 16, cols), F32), jax.ShapeDtypeStruct((16, d), F32),
                   jax.ShapeDtypeStruct((NDEV,) + small.shape, F32)),
        in_specs=[VMEM] * 4, out_specs=(VMEM,) * 3,
        scratch_shapes=[pltpu.VMEM((NDEV, 8, d), F32), pltpu.VMEM((16, cols), F32),
                        pltpu.SemaphoreType.DMA((3 * (NDEV - 1),)), pltpu.SemaphoreType.DMA((3 * (NDEV - 1),))],
        compiler_params=_params(),
    )(c8, cctx8, w_ada, small)


def _scatter_copies(src_ref, dst_ref, send_sems, recv_sems):
    me = _idx(_my_pos())
    copies = [pltpu.make_async_copy(src_ref.at[me], dst_ref.at[0], send_sems.at[0])]
    for k in range(1, NDEV):
        peer = _peer(k)
        copies.append(pltpu.make_async_remote_copy(
            src_ref=src_ref.at[_idx(peer)], dst_ref=dst_ref.at[k], send_sem=send_sems.at[k],
            recv_sem=recv_sems.at[k], device_id=peer, device_id_type=MESH))
    return copies


def _gather_copies(src_ref, dst_ref, send_sems, recv_sems):
    me = _idx(_my_pos())
    sends = [pltpu.make_async_copy(src_ref, dst_ref.at[me], send_sems.at[0])]
    arrivals = []
    for k in range(1, NDEV):
        peer = _peer(k)
        sends.append(pltpu.make_async_remote_copy(
            src_ref=src_ref, dst_ref=dst_ref.at[me], send_sem=send_sems.at[k],
            recv_sem=recv_sems.at[k], device_id=peer, device_id_type=MESH))
        arrivals.append(pltpu.make_async_remote_copy(
            src_ref=src_ref, dst_ref=dst_ref.at[_idx(peer)], send_sem=send_sems.at[k],
            recv_sem=recv_sems.at[k], device_id=peer, device_id_type=MESH))
    return sends, arrivals


def _exchange_wait(sends, arrivals):
    sends[0].wait()
    for cp in arrivals:
        cp.wait_recv()
    for cp in sends[1:]:
        cp.wait_send()


def _chip_order(k, c):
    return (6, 4 - 2 * c, 2 + 2 * c, 0)[k]


def _scatter_order(s, c):
    k = s >> 1
    mine = jnp.where(k == 0, 6, jnp.where(k == 1, 4 - 2 * c, jnp.where(k == 2, 2 + 2 * c, 0)))
    theirs = jnp.where(k == 0, 6, jnp.where(k == 1, 2 + 2 * c, jnp.where(k == 2, 4 - 2 * c, 0))) ^ 1
    return jnp.where((s & 1) == 0, theirs, mine)


def _peer_at(dist):
    x, y, c = _my_pos()
    return (x ^ ((dist >> 2) & 1), y ^ ((dist >> 1) & 1), c ^ (dist & 1))


def _reduce_small(packed, lru_parts, w_ada, dsilu_cctx):
    rp = packed.shape[0]
    rl = lru_parts.shape[1]
    d, cols = w_ada.shape
    assert cols % 128 == 0
    cb = cols // 128

    def body(p_ref, l_ref, w_ref, ds_ref, sum_ref, all_ref, lru_ref, cctx_ref,
             lbuf, lsum, cpart, call, send_sems, recv_sems, lsend, lrecv):
        me = _idx(_my_pos())
        scattered = _scatter_copies(l_ref, lbuf, lsend, lrecv)
        for cp in scattered:
            cp.start()
        _exchange_vmem(p_ref, all_ref, send_sems, recv_sems, 0)
        acc = all_ref[0]
        for j in range(1, NDEV):
            acc = acc + all_ref[j]
        sum_ref[...] = acc
        _exchange_wait(scattered, scattered[1:])
        red = lbuf[0]
        for k in range(1, NDEV):
            red = red + lbuf[k]
        lsum[...] = red
        lru_gather = _exchange_start(lsum, lru_ref, send_sems, recv_sems, NDEV - 1)
        part = jnp.zeros((8, d), F32)
        for q in range(cb):
            dm = jnp.broadcast_to(sum_ref[pl.ds((NDEV + me) * cb + q, 1), :], (8, 128))
            part = part + lax.dot_general(dm, w_ref[:, q * 128:(q + 1) * 128],
                                          (((1,), (1,)), ((), ())), precision=HIGHEST,
                                          preferred_element_type=F32)
        cpart[...] = part
        _exchange_vmem(cpart, call, send_sems, recv_sems, 2 * (NDEV - 1))
        _exchange_finish(lru_gather)
        tot = call[0]
        for j in range(1, NDEV):
            tot = tot + call[j]
        cctx_ref[...] = tot * ds_ref[...]

    return _call(
        body, name="reduce_small",
        out_shape=(jax.ShapeDtypeStruct((rp, 128), F32), jax.ShapeDtypeStruct((NDEV, rp, 128), F32),
                   jax.ShapeDtypeStruct((NDEV, rl, 128), F32), jax.ShapeDtypeStruct((8, d), F32)),
        in_specs=[VMEM] * 4, out_specs=(VMEM,) * 4,
        scratch_shapes=[pltpu.VMEM((NDEV, rl, 128), F32), pltpu.VMEM((rl, 128), F32), pltpu.VMEM((8, d), F32),
                        pltpu.VMEM((NDEV, 8, d), F32),
                        pltpu.SemaphoreType.DMA((3 * (NDEV - 1),)), pltpu.SemaphoreType.DMA((3 * (NDEV - 1),)),
                        pltpu.SemaphoreType.DMA((NDEV,)), pltpu.SemaphoreType.DMA((NDEV,))],
        compiler_params=_params(),
    )(packed, lru_parts, w_ada, dsilu_cctx)


def _normalize(src, mv, la, row0, tm, name, prev=None):
    rows, d = src.shape
    blk0 = row0 // tm

    def body(*refs):
        x_ref, mv_ref = refs[:2]
        h_ref, ht_ref = refs[-2:]
        xf = x_ref[...]
        r = lax.rsqrt(jnp.mean(xf * xf, axis=-1, keepdims=True) + EPS)
        h = xf * r * (mv_ref[0:1, :] * (1.0 + mv_ref[1:2, :])) + mv_ref[2:3, :]
        h_ref[...] = h.astype(BF16)
        ht_ref[...] = h.T.astype(BF16)

    in_specs = [pl.BlockSpec((tm, d), lambda i: (i, 0)), pl.BlockSpec((8, d), lambda i: (0, 0))]
    args = [src, mv]
    aliases = {}
    if prev is not None:
        in_specs += [ANY, ANY]
        args += list(prev)
        aliases = {2: 0, 3: 1}
    return _call(
        body, name=name,
        grid=(rows // tm,),
        out_shape=(jax.ShapeDtypeStruct((la, d), BF16), jax.ShapeDtypeStruct((d, la), BF16)),
        in_specs=in_specs,
        out_specs=(pl.BlockSpec((tm, d), lambda i: (blk0 + i, 0)), pl.BlockSpec((d, tm), lambda i: (0, blk0 + i))),
        input_output_aliases=aliases,
        compiler_params=_params(("arbitrary",)),
    )(*args)


def _gather_order(step):
    return (step & 1) | (((step >> 2) & 1) << 1) | (((step >> 1) & 1) << 2)


def _in_projection(h, w_shard, tm):
    la, d = h.shape
    bw = w_shard.shape[1]
    ni = la // tm
    where = jnp.reshape(_idx(_my_pos()), (1,)).astype(jnp.int32)

    def body(me_ref, h_ref, w_ref, p_ref, all_ref, wbuf, send_sems, recv_sems, local_sems):
        s, i = pl.program_id(0), pl.program_id(1)
        x, y, c = _my_pos()
        me, sibling = (x, y, c), (x, y, 1 - c)
        chips = [(1 - x, y), (x, 1 - y), (1 - x, 1 - y)]

        def copy(k, block, to, from_shard=False):
            return pltpu.make_async_remote_copy(
                src_ref=w_ref if from_shard else all_ref.at[_idx(block)], dst_ref=all_ref.at[_idx(block)],
                send_sem=send_sems.at[k], recv_sem=recv_sems.at[k], device_id=to, device_id_type=MESH)

        def load(block, slot):
            return pltpu.make_async_copy(all_ref.at[_idx(block)], wbuf.at[slot], local_sems.at[1])

        keep = pltpu.make_async_copy(w_ref, all_ref.at[_idx(me)], local_sems.at[0])
        first = [copy(0, me, sibling, True)] + [copy(1 + j, me, (*chip, c), True) for j, chip in enumerate(chips)]
        passed = [copy(4 + j, (*chip, c), sibling) for j, chip in enumerate(chips)]
        steps = [(copy(0, sibling, me), None, sibling)]
        for j, chip in enumerate(chips):
            steps.append((copy(1 + j, (*chip, c), me), passed[j], (*chip, c)))
            steps.append((copy(4 + j, (*chip, 1 - c), me), None, (*chip, 1 - c)))

        @pl.when((s == 0) & (i == 0))
        def _():
            keep.start()
            mine = pltpu.make_async_copy(w_ref, wbuf.at[0], local_sems.at[1])
            mine.start()
            for cp in first:
                cp.start()
            mine.wait()

        for n, (arrival, forward, block) in enumerate(steps, start=1):
            @pl.when((s == n - 1) & (i == ni - 1))
            def _(arrival=arrival, forward=forward, block=block, n=n):
                arrival.wait_recv()
                if forward is not None:
                    forward.start()
                load(block, n % 2).start()

        @pl.when((s > 0) & (i == 0))
        def _():
            load(me, s % 2).wait()

        p_ref[...] = _dot(h_ref[...], wbuf[s % 2]).astype(BF16)

        @pl.when((s == NDEV - 1) & (i == ni - 1))
        def _():
            for cp in first + passed:
                cp.wait_send()
            keep.wait()

    return _call(
        body, name="in_projection",
        grid_spec=pltpu.PrefetchScalarGridSpec(
            num_scalar_prefetch=1, grid=(NDEV, ni),
            in_specs=[pl.BlockSpec((tm, d), lambda s, i, me_ref: (i, 0)), ANY],
            out_specs=(pl.BlockSpec((tm, bw), lambda s, i, me_ref: (i, me_ref[0] ^ _gather_order(s))), ANY),
            scratch_shapes=[pltpu.VMEM((2, d, bw), BF16), pltpu.SemaphoreType.DMA((7,)),
                            pltpu.SemaphoreType.DMA((7,)), pltpu.SemaphoreType.DMA((2,))]),
        out_shape=(jax.ShapeDtypeStruct((la, NDEV * bw), BF16), jax.ShapeDtypeStruct((NDEV, d, bw), BF16)),
        compiler_params=_params(("arbitrary", "arbitrary")),
    )(where, h, w_shard)


def _conv_input(p, wcb, taps_m, l, t):
    la = p.shape[0]
    w = wcb.shape[1]
    nt = l // t

    def body(v_ref, wcb_ref, tm_ref, xb_ref):
        v16 = v_ref[...]
        xb = wcb_ref[4:5, :] + wcb_ref[0:1, :] * _dot(tm_ref[0], v16)
        for j in range(1, 4):
            xb = xb + wcb_ref[j:j + 1, :] * _dot(tm_ref[j], v16)
        xb_ref[...] = xb

    return _call(
        body, name="conv_input",
        grid=(nt + 1,),
        out_shape=jax.ShapeDtypeStruct((la, w), F32),
        in_specs=[pl.BlockSpec((t, w), lambda i: (i, 4)), pl.BlockSpec((8, w), lambda i: (0, 0)),
                  pl.BlockSpec((None, 4, t, t), lambda i: (i // nt, 0, 0, 0))],
        out_specs=pl.BlockSpec((t, w), lambda i: (i, 0)),
        compiler_params=_params(("arbitrary",)),
    )(p, wcb, taps_m)


def _lru_forward(xb, wg, lv, wo_shard, l, t):
    la, w = xb.shape
    gc = wg.shape[2]
    nt = l // t

    def body(xf_ref, xr_ref, wg_ref, lv_ref, wo_ref, hf_ref, hr_ref, wo_all,
             a_s, b_s, carry, send_sems, recv_sems):
        sends, arrivals = _gather_copies(wo_ref, wo_all, send_sems, recv_sems)

        @pl.when(pl.program_id(0) == 0)
        def _():
            carry[...] = jnp.zeros_like(carry)
            for cp in sends:
                cp.start()

        @pl.when(pl.program_id(0) == nt)
        def _():
            _exchange_wait(sends, arrivals)

        for dr, (x_ref, h_ref) in enumerate(((xf_ref, hf_ref), (xr_ref, hr_ref))):
            x = x_ref[...]
            a, s, _, _, ti, _ = _lru_coef(x, wg_ref, dr, lv_ref[3 * dr:3 * dr + 1, :],
                                          lv_ref[3 * dr + 1:3 * dr + 2, :], lv_ref[3 * dr + 2:3 * dr + 3, :], gc)
            a_s[...] = a
            b_s[...] = (s * x) * (0.5 * ti + 0.5)
            carry[dr] = _scan_tile(a_s, b_s, h_ref, carry[dr], dr == 1)

    full = lambda shape: pl.BlockSpec(shape, lambda i: (0,) * len(shape))
    fmap = lambda i: (jnp.where(i == 0, nt, i - 1), 0)
    rmap = lambda i: (jnp.where(i == 0, nt, nt - i), 0)
    return _call(
        body, name="lru_forward",
        grid=(nt + 1,),
        out_shape=(jax.ShapeDtypeStruct((la, w), F32), jax.ShapeDtypeStruct((la, w), F32),
                   jax.ShapeDtypeStruct((NDEV,) + wo_shard.shape, wo_shard.dtype)),
        in_specs=[pl.BlockSpec((t, w), fmap), pl.BlockSpec((t, w), rmap), full(wg.shape), full(lv.shape), ANY],
        out_specs=(pl.BlockSpec((t, w), fmap), pl.BlockSpec((t, w), rmap), ANY),
        scratch_shapes=[pltpu.VMEM((t, w), F32), pltpu.VMEM((t, w), F32), pltpu.VMEM((2, 8, w), F32),
                        pltpu.SemaphoreType.DMA((NDEV,)), pltpu.SemaphoreType.DMA((NDEV,))],
        compiler_params=_params(("arbitrary",)),
    )(xb, xb, wg, lv, wo_shard)


def _mix_gates(p_refs, hf_ref, hr_ref, wca_ref, perm_ref, t, w):
    bl, cl, ul, gl, ql = [r[...].astype(F32) for r in p_refs]
    tt = cl * ul
    tt16 = tt.astype(BF16)
    before, after = _dot(perm_ref[2], tt16), _dot(perm_ref[3], tt16)
    z = wca_ref[0:1, :] * before + wca_ref[1:2, :] * tt + wca_ref[2:3, :] * after
    sig_g = _sigmoid(gl)
    sig_q = _sigmoid(ql)
    ylru = _dot(perm_ref[1], (hf_ref[...] + hr_ref[...]).astype(BF16))
    return bl, cl, ul, gl, ql, (before, tt, after), z, sig_g, sig_q, ylru


def _p_specs(t, w, nt):
    return [pl.BlockSpec((t, w), functools.partial(lambda i, s: (jnp.minimum(i, nt - 1), s), s=s))
            for s in (0, 1, 2, 3, 5)]


def _mix_forward(x, tgt, p, hf, hr, wo, ov, wca, perm, t):
    l, d = x.shape
    w = d // 2
    nt = l // t

    def body(x_ref, tg_ref, b_ref, c_ref, u_ref, g_ref, q_ref, hf_ref, hr_ref, wo_ref, ov_ref, wca_ref, perm_ref,
             dn_ref, ct_ref, do_ref, part_ref):
        i = pl.program_id(0)
        bl, _, _, gl, ql, _, z, sig_g, sig_q, ylru = _mix_gates(
            (b_ref, c_ref, u_ref, g_ref, q_ref), hf_ref, hr_ref, wca_ref, perm_ref, t, w)
        ya = bl * z * (gl * sig_g)
        yb = ylru * (ql * sig_q)
        ct_ref[0:w, :] = ya.T.astype(BF16)
        ct_ref[w:, :] = yb.T.astype(BF16)
        out = _dot(ya.astype(BF16), wo_ref[0:w, :]) + _dot(yb.astype(BF16), wo_ref[w:, :])
        gate, fg = ov_ref[0:1, :], ov_ref[1:2, :]
        n = x_ref[...] + gate * out
        rr = lax.rsqrt(jnp.mean(n * n, axis=-1, keepdims=True) + EPS)
        nh = n * rr
        e = nh * fg - tg_ref[...]
        loss = 0.5 * jnp.sum(jnp.mean(e * e, axis=-1, keepdims=True), axis=0, keepdims=True)
        dy = e * (1.0 / d)
        dnh = dy * fg
        dn = rr * (dnh - nh * jnp.mean(dnh * nh, axis=-1, keepdims=True))
        dn_ref[...] = dn.astype(BF16)
        do_ref[...] = (dn * gate).astype(BF16)

        @pl.when(i == 0)
        def _():
            part_ref[...] = jnp.zeros_like(part_ref)

        part_ref[0:1, :] += jnp.sum(dy * nh, axis=0, keepdims=True)
        part_ref[1:2, :] += jnp.sum(dn * out, axis=0, keepdims=True)
        part_ref[2:3, :] += jnp.broadcast_to(loss, (1, d))

    tile = lambda cols: pl.BlockSpec((t, cols), lambda i: (i, 0))
    full = lambda shape: pl.BlockSpec(shape, lambda i: (0,) * len(shape))
    return _call(
        body, name="mix_forward",
        grid=(nt,),
        out_shape=(jax.ShapeDtypeStruct((l, d), BF16), jax.ShapeDtypeStruct((d, l), BF16),
                   jax.ShapeDtypeStruct((l, d), BF16), jax.ShapeDtypeStruct((8, d), F32)),
        in_specs=[tile(d), tile(d)] + _p_specs(t, w, nt) + [tile(w), tile(w),
                  pl.BlockSpec((d, d), lambda i: (0, 0), pipeline_mode=pl.Buffered(1)),
                  full(ov.shape), full(wca.shape), full(perm.shape)],
        out_specs=(tile(d), pl.BlockSpec((d, t), lambda i: (0, i)), tile(d), full((8, d))),
        compiler_params=_params(("arbitrary",)),
    )(x, tgt, p, p, p, p, p, hf, hr, wo, ov, wca, perm)


def _mix_backward(dout, p, hf, hr, wo, wca, perm, g_wout, l, t):
    d = dout.shape[1]
    w = d // 2
    nt = l // t
    la = p.shape[0]

    def body(do_ref, b_ref, c_ref, u_ref, g_ref, q_ref, hf_ref, hr_ref, wo_ref, wca_ref, perm_ref, gw_ref,
             dp_ref, dh_ref, part_ref, sc_ref, send_sems, recv_sems):
        i = pl.program_id(0)
        copies = _scatter_copies(gw_ref, sc_ref, send_sems, recv_sems)

        @pl.when(i == 0)
        def _():
            part_ref[...] = jnp.zeros_like(part_ref)
            for cp in copies:
                cp.start()

        @pl.when(i == nt)
        def _():
            dp_ref[...] = jnp.zeros_like(dp_ref)
            _exchange_wait(copies, copies[1:])

        @pl.when(i < nt)
        def _():
            bl, cl, ul, gl, ql, taps, z, sig_g, sig_q, ylru = _mix_gates(
                (b_ref, c_ref, u_ref, g_ref, q_ref), hf_ref, hr_ref, wca_ref, perm_ref, t, w)
            do = do_ref[...]
            dya = _dot_nt(do, wo_ref[0:w, :])
            dyb = _dot_nt(do, wo_ref[w:, :])
            sg = gl * sig_g
            dz = dya * bl * sg
            dz16 = dz.astype(BF16)
            dt = (wca_ref[0:1, :] * _dot(perm_ref[3], dz16) + wca_ref[1:2, :] * dz
                  + wca_ref[2:3, :] * _dot(perm_ref[2], dz16))
            dp_ref[:, 0:w] = (dya * z * sg).astype(BF16)
            dp_ref[:, w:2 * w] = (dt * ul).astype(BF16)
            dp_ref[:, 2 * w:3 * w] = (dt * cl).astype(BF16)
            dp_ref[:, 3 * w:4 * w] = (dya * bl * z * (sig_g * (1.0 + gl * (1.0 - sig_g)))).astype(BF16)
            dp_ref[:, 4 * w:5 * w] = jnp.zeros((t, w), BF16)
            dp_ref[:, 5 * w:6 * w] = (dyb * ylru * (sig_q * (1.0 + ql * (1.0 - sig_q)))).astype(BF16)
            dh_ref[...] = _dot(perm_ref[0], (dyb * (ql * sig_q)).astype(BF16)).astype(BF16)
            for j in range(3):
                part_ref[j:j + 1, :] += jnp.sum(dz * taps[j], axis=0, keepdims=True)

    clamp = lambda cols: pl.BlockSpec((t, cols), lambda i: (jnp.minimum(i, nt - 1), 0))
    full = lambda shape: pl.BlockSpec(shape, lambda i: (0,) * len(shape))
    return _call(
        body, name="mix_backward",
        grid=(nt + 1,),
        out_shape=(jax.ShapeDtypeStruct((la, 6 * w), BF16), jax.ShapeDtypeStruct((l, w), BF16),
                   jax.ShapeDtypeStruct((8, w), F32), jax.ShapeDtypeStruct(g_wout.shape, g_wout.dtype)),
        in_specs=[clamp(d)] + _p_specs(t, w, nt) + [clamp(w), clamp(w),
                  pl.BlockSpec((d, d), lambda i: (0, 0), pipeline_mode=pl.Buffered(1)), full(wca.shape),
                  full(perm.shape), ANY],
        out_specs=(pl.BlockSpec((t, 6 * w), lambda i: (i, 0)), clamp(w), full((8, w)), ANY),
        scratch_shapes=[pltpu.SemaphoreType.DMA((NDEV,)), pltpu.SemaphoreType.DMA((NDEV,))],
        compiler_params=_params(("arbitrary",)),
    )(dout, p, p, p, p, p, hf, hr, wo, wca, perm, g_wout)


def _lru_backward(direction, xb, dhs, hs, wg, lv, l, t, conv=None):
    la, w = hs.shape
    gc = wg.shape[2]
    ng = w // gc
    nt = l // t
    nblk8 = la // 8
    last = conv is not None
    assert last == (direction == 1)

    if direction == 0:
        tile = lambda i: jnp.where(i == nt, nt, nt - 1 - i)
        halo = lambda i: jnp.where(tile(i) == 0, nblk8 - 1, tile(i) * (t // 8) - 1)
    else:
        tile = lambda i: i
        halo = lambda i: jnp.minimum((i + 1) * (t // 8), nblk8 - 1)

    def body(*refs):
        x_ref, dh_ref, hs_ref, halo_ref, wg_ref, lv_ref = refs[:6]
        if last:
            v_ref, wcb_ref, bm_ref, dxo_ref = refs[6:10]
        out_ref, dwg_ref, part_ref, a_s, dh_s, g_s, carry = refs[-7:]
        i = pl.program_id(0)
        is_ctx = i == nt

        @pl.when(i == 0)
        def _():
            carry[...] = jnp.zeros_like(carry)
            dwg_ref[...] = jnp.zeros_like(dwg_ref)
            part_ref[...] = jnp.zeros_like(part_ref)

        xb = x_ref[...]
        lam = lv_ref[3 * direction + 2:3 * direction + 3, :]
        a, s, rs, tr, ti, sp = _lru_coef(xb, wg_ref, direction, lv_ref[3 * direction:3 * direction + 1, :],
                                         lv_ref[3 * direction + 1:3 * direction + 2, :], lam, gc)
        hs_t = hs_ref[...]
        r8 = _rows((8, w))
        if direction == 0:
            edge = jnp.where(is_ctx, 0.0, halo_ref[7:8, :])
            first = jnp.where(r8 == 0, edge, pltpu.roll(hs_t[t - 8:, :], 1, 0))
            hprev = jnp.concatenate([first, hs_t[:t - 8, :]], axis=0)
        else:
            edge = jnp.where(is_ctx, 0.0, halo_ref[0:1, :])
            final = jnp.where(r8 == 7, edge, pltpu.roll(hs_t[:8, :], 7, 0))
            hprev = jnp.concatenate([hs_t[8:, :], final], axis=0)
        a_s[...] = a
        dh_s[...] = jnp.where(is_ctx, 0.0, dh_ref[...].astype(F32))
        carry[...] = _scan_tile_backward(a_s, dh_s, g_s, carry[...], direction == 0)

        g = g_s[...]
        r = 0.5 * tr + 0.5
        ig = 0.5 * ti + 0.5
        ix = ig * xb
        gs = g * s
        dla = (g * a) * (hprev - ix * (a * rs))
        dxb = gs * ig
        dzr = dla * (r * (1.0 - tr)) * (-LRU_C * sp)
        dzi = gs * ix * (1.0 - ti)
        part_ref[0:1, :] += jnp.sum(dzr, axis=0, keepdims=True)
        part_ref[1:2, :] += jnp.sum(dzi, axis=0, keepdims=True)
        part_ref[2:3, :] += jnp.sum(dla * r, axis=0, keepdims=True) * (LRU_C * _sigmoid(-lam))
        pieces = []
        for gi in range(ng):
            sl = slice(gi * gc, (gi + 1) * gc)
            dz = jnp.concatenate([dzr[:, sl], dzi[:, sl]], axis=-1).astype(BF16)
            pieces.append(_dot_nt(dz, wg_ref[direction, gi]))
            dwg_ref[gi] += _dot(xb[:, sl].T.astype(BF16), dz)
        dxb = dxb + (pieces[0] if ng == 1 else jnp.concatenate(pieces, axis=-1))
        if not last:
            out_ref[...] = dxb
        else:
            dxb = dxb + dxo_ref[...]
            dxb16, v = dxb.astype(BF16), v_ref[...].astype(F32)
            dv = jnp.zeros((t, w), F32)
            for j in range(4):
                back = _dot(bm_ref[j], dxb16)
                dv = dv + wcb_ref[j:j + 1, :] * back
                part_ref[4 + j:5 + j, :] += jnp.sum(back * v, axis=0, keepdims=True)
            out_ref[...] = dv.astype(BF16)
            part_ref[3:4, :] += jnp.sum(dxb, axis=0, keepdims=True)

    full = lambda shape: pl.BlockSpec(shape, lambda i: (0,) * len(shape))
    kind = lambda i: (jnp.where(i == nt, 1, 0), 0, 0, 0)
    in_specs = [pl.BlockSpec((t, w), lambda i: (tile(i), 0)),
                pl.BlockSpec((t, w), lambda i: (jnp.minimum(tile(i), nt - 1), 0)),
                pl.BlockSpec((t, w), lambda i: (tile(i), 0)),
                pl.BlockSpec((8, w), lambda i: (halo(i), 0)),
                full(wg.shape), full(lv.shape)]
    args = [xb, dhs, hs, hs, wg, lv]
    if last:
        p, wcb, back_m, dxb_other, dp = conv
        in_specs += [pl.BlockSpec((t, w), lambda i: (tile(i), 4)), full(wcb.shape),
                     pl.BlockSpec((None, 4, t, t), kind), pl.BlockSpec((t, w), lambda i: (tile(i), 0)), ANY]
        args += [p, wcb, back_m, dxb_other, dp]
        out0 = jax.ShapeDtypeStruct(dp.shape, dp.dtype)
        spec0 = pl.BlockSpec((t, w), lambda i: (tile(i), 4))
        aliases = {10: 0}
    else:
        out0 = jax.ShapeDtypeStruct((la, w), F32)
        spec0 = pl.BlockSpec((t, w), lambda i: (tile(i), 0))
        aliases = {}
    return _call(
        body, name="lru_backward_%d" % direction,
        grid=(nt + 1,),
        out_shape=(out0, jax.ShapeDtypeStruct((ng, gc, 2 * gc), F32), jax.ShapeDtypeStruct((8, w), F32)),
        in_specs=in_specs,
        out_specs=(spec0, full((ng, gc, 2 * gc)), full((8, w))),
        scratch_shapes=[pltpu.VMEM((t, w), F32), pltpu.VMEM((t, w), F32), pltpu.VMEM((t, w), F32),
                        pltpu.VMEM((8, w), F32)],
        input_output_aliases=aliases,
        compiler_params=_params(("arbitrary",)),
    )(*args)


def _weight_grad_t(at, b, nblk_m, nblk_n, tk, name):
    m, k = at.shape
    n = b.shape[1]
    bm, bn = m // nblk_m, n // nblk_n
    nk = k // tk

    def body(a_ref, b_ref, o_ref, acc):
        kk = pl.program_id(2)

        @pl.when(kk == 0)
        def _():
            acc[...] = jnp.zeros_like(acc)

        acc[...] += _dot(a_ref[...], b_ref[...])

        @pl.when(kk == nk - 1)
        def _():
            o_ref[...] = acc[...].astype(BF16)

    return _call(
        body, name=name,
        grid=(nblk_m, nblk_n, nk),
        out_shape=jax.ShapeDtypeStruct((nblk_m * nblk_n, bm, bn), BF16),
        in_specs=[pl.BlockSpec((bm, tk), lambda i, j, kk: (i, kk)),
                  pl.BlockSpec((tk, bn), lambda i, j, kk: (kk, j))],
        out_specs=pl.BlockSpec((None, bm, bn), lambda i, j, kk: (i * nblk_n + j, 0, 0)),
        scratch_shapes=[pltpu.VMEM((bm, bn), F32)],
        compiler_params=_params(("arbitrary", "arbitrary", "arbitrary")),
    )(at, b)


def _weight_grad_scatter(at, b, tk, name):
    m, k = at.shape
    n = b.shape[1]
    bn = n // NDEV
    nk = k // tk
    where = jnp.stack([_idx(_my_pos()), lax.axis_index("c")]).astype(jnp.int32)

    def body(w_ref, a_ref, b_ref, recv_ref, acc, sbuf, sib, sib_send, sib_recv, chip_send, chip_recv, keep_sem):
        s, kk = pl.program_id(0), pl.program_id(1)
        x, y, c = _my_pos()

        @pl.when(kk == 0)
        def _():
            acc[...] = _dot(a_ref[...], b_ref[...])

        @pl.when(kk > 0)
        def _():
            acc[...] += _dot(a_ref[...], b_ref[...])

        def to_sibling(j):
            return pltpu.make_async_remote_copy(
                src_ref=sbuf.at[0], dst_ref=sib.at[j], send_sem=sib_send.at[j], recv_sem=sib_recv.at[j],
                device_id=(x, y, 1 - c), device_id_type=MESH)

        def to_chip(j):
            dist = _chip_order(j, c)
            return pltpu.make_async_remote_copy(
                src_ref=sbuf.at[1], dst_ref=recv_ref.at[dist // 2], send_sem=chip_send.at[j],
                recv_sem=chip_recv.at[dist // 2], device_id=_peer_at(dist), device_id_type=MESH)

        keep = pltpu.make_async_copy(sbuf.at[1], recv_ref.at[0], keep_sem)
        sends = []
        for j in range(4):
            sends += [to_sibling(j), to_chip(j) if j < 3 else keep]

        for st in range(NDEV):
            @pl.when((kk == nk - 1) & (s == st))
            def _(st=st):
                if st >= 2:
                    sends[st - 2].wait_send()
                part = acc[...]
                if st % 2 == 1:
                    to_sibling(st // 2).wait_recv()
                    part = part + sib[st // 2].astype(F32)
                sbuf[st % 2] = part.astype(BF16)
                sends[st].start()
                if st == NDEV - 1:
                    sends[st - 1].wait_send()
                    sends[st].wait()
                    for j in range(1, 4):
                        pltpu.make_async_remote_copy(
                            src_ref=sbuf.at[0], dst_ref=recv_ref.at[j], send_sem=chip_send.at[0],
                            recv_sem=chip_recv.at[j], device_id=_peer_at(2 * j), device_id_type=MESH).wait_recv()

    blk = lambda s, w_ref: w_ref[0] ^ _scatter_order(s, w_ref[1])
    return _call(
        body, name=name,
        grid_spec=pltpu.PrefetchScalarGridSpec(
            num_scalar_prefetch=1, grid=(NDEV, nk),
            in_specs=[pl.BlockSpec((m, tk), lambda s, kk, w_ref: (0, kk)),
                      pl.BlockSpec((tk, bn), lambda s, kk, w_ref: (kk, blk(s, w_ref)))],
            out_specs=ANY,
            scratch_shapes=[pltpu.VMEM((m, bn), F32), pltpu.VMEM((2, m, bn), BF16), pltpu.VMEM((4, m, bn), BF16),
                            pltpu.SemaphoreType.DMA((4,)), pltpu.SemaphoreType.DMA((4,)),
                            pltpu.SemaphoreType.DMA((4,)), pltpu.SemaphoreType.DMA((4,)),
                            pltpu.SemaphoreType.DMA]),
        out_shape=jax.ShapeDtypeStruct((4, m, bn), BF16),
        compiler_params=_params(("arbitrary", "arbitrary")),
    )(where, at, b)


def _input_backward(dp, w_all, src, mv, row0, tm, nbk, name, dn=None):
    rows, d = src.shape
    nb, _, bw = w_all.shape
    nk = nb // nbk
    ni = rows // tm
    blk0 = row0 // tm
    latent = dn is not None

    def body(*refs):
        dp_ref, w_ref, x_ref, mv_ref = refs[:4]
        outs = refs[4 + latent:]
        part_ref, acc = outs[latent], outs[latent + 1]
        i, k = pl.program_id(0), pl.program_id(1)

        def product():
            step = _dot_nt(dp_ref[:, 0:bw], w_ref[0])
            for q in range(1, nbk):
                step = step + _dot_nt(dp_ref[:, q * bw:(q + 1) * bw], w_ref[q])
            return step

        def finish(slot):
            xf = x_ref[...]
            r = lax.rsqrt(jnp.mean(xf * xf, axis=-1, keepdims=True) + EPS)
            xn = xf * r
            dhl = acc[slot]
            gain, sc = mv_ref[0:1, :], mv_ref[1:2, :]
            dhx = jnp.sum(dhl * xn, axis=0, keepdims=True)
            part_ref[0:1, :] += jnp.sum(dhl, axis=0, keepdims=True)
            part_ref[1:2, :] += dhx * gain
            part_ref[2:3, :] += dhx * (1.0 + sc)
            if latent:
                dxn = dhl * (gain * (1.0 + sc))
                outs[0][...] = (refs[4][...].astype(F32)
                                + r * (dxn - xn * jnp.mean(dxn * xn, axis=-1, keepdims=True)))

        @pl.when((i == 0) & (k == 0))
        def _():
            part_ref[...] = jnp.zeros_like(part_ref)
            acc[0] = product()

        @pl.when((i > 0) & (i < ni) & (k == 0))
        def _():
            acc[i % 2] = product()
            finish((i - 1) % 2)

        @pl.when((i == ni) & (k == 0))
        def _():
            finish((ni - 1) % 2)

        @pl.when((i < ni) & (k > 0))
        def _():
            acc[i % 2] += product()

    tile = pl.BlockSpec((tm, d), lambda i, k: (jnp.maximum(i - 1, 0), 0))
    vec = pl.BlockSpec((8, d), lambda i, k: (0, 0))
    kblock = lambda i, k: jnp.where(i == ni, nk - 1, k)
    return _call(
        body, name=name,
        grid=(ni + 1, nk),
        out_shape=((jax.ShapeDtypeStruct((rows, d), F32),) if latent else ()) + (jax.ShapeDtypeStruct((8, d), F32),),
        in_specs=[pl.BlockSpec((tm, nbk * bw), lambda i, k: (blk0 + jnp.minimum(i, ni - 1), kblock(i, k))),
                  pl.BlockSpec((nbk, d, bw), lambda i, k: (kblock(i, k), 0, 0)), tile, vec]
                 + ([tile] if latent else []),
        out_specs=((tile,) if latent else ()) + (vec,),
        scratch_shapes=[pltpu.VMEM((2, tm, d), F32)],
        compiler_params=_params(("arbitrary", "arbitrary")),
    )(*([dp, w_all, src, mv] + ([dn] if latent else [])))


def _adamw_scattered(parts, w, m, v, tr):
    r, c = w.shape
    nslot = parts.shape[0]

    def body(p_ref, w_ref, m_ref, v_ref, g_ref, d_ref, m2_ref, v2_ref):
        g = p_ref[0].astype(F32)
        for k in range(1, nslot):
            g = g + p_ref[k].astype(F32)
        g_ref[...] = g
        d_ref[...], m2_ref[...], v2_ref[...] = _adamw(w_ref[...], g, m_ref[...], v_ref[...])

    tile = pl.BlockSpec((tr, c), lambda i: (i, 0))
    return _call(
        body, name="adamw_scattered_%dx%d" % (r, c),
        grid=(r // tr,),
        out_shape=tuple(jax.ShapeDtypeStruct((r, c), F32) for _ in range(4)),
        in_specs=[pl.BlockSpec((nslot, tr, c), lambda i: (0, i, 0)), tile, tile, tile],
        out_specs=(tile,) * 4,
        compiler_params=_params(("arbitrary",)),
    )(parts, w, m, v)


def _adamw_ada(st, dmod, w, m, v, tr):
    r, c = w.shape

    def body(s_ref, dm_ref, w_ref, m_ref, v_ref, g_ref, d_ref, m2_ref, v2_ref):
        g = jnp.dot(s_ref[...], dm_ref[...], precision=HIGHEST, preferred_element_type=F32)
        g_ref[...] = g
        d_ref[...], m2_ref[...], v2_ref[...] = _adamw(w_ref[...], g, m_ref[...], v_ref[...])

    tile = pl.BlockSpec((tr, c), lambda i: (i, 0))
    return _call(
        body, name="adamw_ada",
        grid=(r // tr,),
        out_shape=tuple(jax.ShapeDtypeStruct((r, c), F32) for _ in range(4)),
        in_specs=[pl.BlockSpec((tr, 16), lambda i: (i, 0)), pl.BlockSpec((16, c), lambda i: (0, 0)),
                  tile, tile, tile],
        out_specs=(tile,) * 4,
        compiler_params=_params(("arbitrary",)),
    )(st, dmod, w, m, v)


def _adamw_small(gs, ws, ms, vs):
    n = len(ws)

    def body(*refs):
        for j in range(n):
            g_ref, w_ref, m_ref, v_ref = refs[j], refs[n + j], refs[2 * n + j], refs[3 * n + j]
            d_ref, m2_ref, v2_ref = refs[4 * n + j], refs[5 * n + j], refs[6 * n + j]
            d_ref[...], m2_ref[...], v2_ref[...] = _adamw(w_ref[...], g_ref[...], m_ref[...], v_ref[...])

    shapes = tuple(jax.ShapeDtypeStruct(a.shape, F32) for a in ws)
    out = _call(
        body, name="adamw_small",
        out_shape=shapes * 3,
        in_specs=[VMEM] * (4 * n), out_specs=(VMEM,) * (3 * n),
        compiler_params=_params(),
    )(*gs, *ws, *ms, *vs)
    return list(out[:n]), list(out[n:2 * n]), list(out[2 * n:])


def _blockdiag_groups(wh, gc):
    h, dh, _ = wh.shape
    g = gc // dh
    w4 = wh.reshape(h // g, g, dh, dh)
    bd = jnp.einsum("ngij,gh->ngihj", w4, jnp.eye(g, dtype=wh.dtype))
    return bd.reshape(h // g, gc, gc)


def _blockdiag_extract(bd, dh):
    ng, gc, _ = bd.shape
    g = gc // dh
    x = bd.reshape(ng, g, dh, g, dh)
    return jnp.einsum("ngihj,gh->ngij", x, jnp.eye(g, dtype=bd.dtype)).reshape(ng * g, dh, dh)


def _largest_tile(n, cap):
    return max(q for q in range(128, min(n, cap) + 1, 128) if n % q == 0)


def _rows8(*vecs):
    rows = [jnp.reshape(v, (1, -1)).astype(F32) for v in vecs]
    n = rows[0].shape[1]
    return jnp.concatenate(rows + [jnp.zeros((8 - len(rows), n), F32)], axis=0)


def _pack(pieces):
    flat = jnp.concatenate([jnp.reshape(a, (-1,)).astype(F32) for a in pieces])
    total = -(-flat.shape[0] // 1024) * 1024
    return jnp.pad(flat, (0, total - flat.shape[0])).reshape(total // 128, 128)


def _unpack(packed, shapes):
    flat = packed.reshape(-1)
    out, off = [], 0
    for s in shapes:
        n = 1
        for q in s:
            n *= q
        out.append(flat[off:off + n].reshape(s))
        off += n
    return out


def kernel(x, c, ctx, c_ctx, norm_g, w_ada, b_ada, w_in, w_conv_a, w_conv_b, b_conv_b, lru_wa, lru_ba, lru_wx, lru_bx, lru_lambda, w_out, final_g, loss_target, m_c_ctx, m_norm_g, m_w_ada, m_b_ada, m_w_in, m_w_conv_a, m_w_conv_b, m_b_conv_b, m_lru_wa, m_lru_ba, m_lru_wx, m_lru_bx, m_lru_lambda, m_w_out, m_final_g, v_c_ctx, v_norm_g, v_w_ada, v_b_ada, v_w_in, v_w_conv_a, v_w_conv_b, v_b_conv_b, v_lru_wa, v_lru_ba, v_lru_wx, v_lru_bx, v_lru_lambda, v_w_out, v_final_g):
    _, l, d = x.shape
    lc = ctx.shape[1]
    w = d // 2
    t = lc
    assert l % t == 0 and t % GRID_W == 0 and t % 128 == 0
    dh = w // N_HEADS
    gc = min(w, MXU_WIDTH)
    cols = w_ada.shape[2]
    wo_rows = w_out.shape[1]
    me = _idx(_my_pos())
    x2, ctx2, tgt2 = x[0], ctx[0], loss_target[0]
    w_ada2, w_in2, w_out2 = w_ada[0], w_in[0], w_out[0]

    small_mine = jnp.concatenate([a.reshape(-1) for a in (w_conv_a, w_conv_b, lru_ba, lru_bx, lru_lambda)]
                                 + [jnp.zeros((3 * (w // NDEV),), F32)]).reshape(16, w // NDEV)
    mod_all, s_mat, small_all = _mod_forward(
        jnp.broadcast_to(c, (8, d)), jnp.broadcast_to(c_ctx[None], (8, d)), w_ada2, small_mine)
    mod = jnp.transpose(mod_all, (1, 0, 2)).reshape(16, NDEV * cols) + b_ada
    mod_lat = lax.dynamic_slice_in_dim(mod, me, 1, axis=0)
    sh_l, sc_l, gt_l = jnp.split(mod_lat, 3, axis=-1)
    sh_c, sc_c, _ = jnp.split(mod[8:9], 3, axis=-1)
    small = jnp.transpose(small_all, (1, 0, 2)).reshape(16, w)
    wca = _rows8(*[small[j] for j in range(0, 3)])
    wcb = _rows8(*[small[j] for j in range(3, 7)], b_conv_b)
    lv = _rows8(0.5 * small[7], 0.5 * small[9], small[11], 0.5 * small[8], 0.5 * small[10], small[12])
    wg = jnp.stack([
        jnp.concatenate([_blockdiag_groups(lru_wa[0, dr], gc), _blockdiag_groups(lru_wx[0, dr], gc)], axis=-1)
        for dr in range(2)])
    wg = (0.5 * wg).astype(BF16)

    la = l + lc
    tm = 2 * t if l % (2 * t) == 0 else t
    tk = 3 * t if la % (3 * t) == 0 else t
    h, hlt = _normalize(x2, _rows8(norm_g, sc_l, sh_l), la, 0, tm, "normalize")
    h, hlt = _normalize(ctx2, _rows8(norm_g, sc_c, sh_c), la, l, t, "normalize_ctx", prev=(h, hlt))
    p, w_all = _in_projection(h, w_in2.astype(BF16), la // 8 if la % 128 == 0 else tk)
    taps_m, back_m, perm = _scan_matrices(t)
    xb = _conv_input(p, wcb, taps_m, l, t)
    hf, hr, wo_all = _lru_forward(xb, wg, lv, w_out2.astype(BF16), l, t)
    wo = wo_all.reshape(d, d)
    dn, catt, dout, part_mix = _mix_forward(x2, tgt2, p, hf, hr, wo, _rows8(gt_l, final_g), wca, perm, t)
    g_wout = _weight_grad_t(catt, dout, 2, 1, _largest_tile(l, 2048), "grad_w_out")
    dp, dhs, part_ca, sc_wout = _mix_backward(dout, p, hf, hr, wo, wca, perm, g_wout.reshape(NDEV, wo_rows, d), l, t)
    dxb0, dwg0, part_l0 = _lru_backward(0, xb, dhs, hf, wg, lv, l, t)
    dp, dwg1, part_l1 = _lru_backward(1, xb, dhs, hr, wg, lv, l, t, conv=(p, wcb, back_m, dxb0, dp))
    sc_win = _weight_grad_scatter(hlt, dp, _largest_tile(la, 1408), "grad_w_in")
    grad_x, part_lat = _input_backward(dp, w_all, x2, _rows8(norm_g, sc_l), 0, tm, 2, "input_backward", dn=dn)
    (part_ctx,) = _input_backward(dp, w_all, ctx2, _rows8(norm_g, sc_c), l, t, 2, "input_backward_ctx")
    part_in = jnp.concatenate([part_lat[0:2], part_ctx[0:2], (part_lat[2] + part_ctx[2])[None]], axis=0)

    dwa = jnp.stack([_blockdiag_extract(dwg0[:, :, :gc], dh), _blockdiag_extract(dwg1[:, :, :gc], dh)])
    dwx = jnp.stack([_blockdiag_extract(dwg0[:, :, gc:], dh), _blockdiag_extract(dwg1[:, :, gc:], dh)])
    lru_part = (0.5 * jnp.stack([dwa, dwx])).reshape(NDEV, -1, 128)
    zeros_d = jnp.zeros((d,), F32)
    pieces = [
        jnp.concatenate([part_in[0], part_in[1], part_mix[1]]),
        jnp.concatenate([part_in[2], part_in[3], zeros_d]),
        part_in[4], part_mix[0], part_ca[0:3], part_l1[4:8], part_l1[3],
        0.5 * jnp.stack([part_l0[0], part_l1[0]]), 0.5 * jnp.stack([part_l0[1], part_l1[1]]),
        jnp.stack([part_l0[2], part_l1[2]]), part_mix[2, 0:1],
    ]
    shapes = [(3 * d,), (3 * d,), (d,), (d,), (3, w), (4, w), (w,), (2, w), (2, w), (2, w), (1,)]
    sig_cc = jax.nn.sigmoid(c_ctx)
    dsilu_cc = jnp.broadcast_to((sig_cc * (1.0 + c_ctx * (1.0 - sig_cc)))[None], (8, d))
    psum, pall, lru_sum, g_cctx8 = _reduce_small(_pack(pieces), lru_part, w_ada2, dsilu_cc)
    (g_modl, g_modc, g_norm, g_final, g_ca, g_cb, g_bcb, g_ba, g_bx, g_lam, loss1) = _unpack(psum, shapes)
    loss = loss1[0]
    g_cctx = g_cctx8[0]
    g_bada = (g_modl + g_modc)[None]
    g_lru = lru_sum.reshape(2, 2, N_HEADS, dh, dh)
    g_wa, g_wx = g_lru[0][None], g_lru[1][None]
    wsl = w // NDEV
    mine = lambda a: lax.dynamic_slice_in_dim(a, me * wsl, wsl, axis=-1)
    g_ca_m, g_cb_m, g_ba_m, g_bx_m, g_lam_m = (mine(g_ca)[None], mine(g_cb)[None], mine(g_ba)[None],
                                               mine(g_bx)[None], mine(g_lam)[None])
    g_norm, g_bcb = g_norm[None], g_bcb[None]

    cb = cols // 128
    per_dev = pall[:, :3 * d // 128].reshape(NDEV, NDEV, cols)
    dmod_lat = lax.dynamic_slice_in_dim(per_dev, me, 1, axis=1)[:, 0]
    dmod_ctx = lax.dynamic_slice_in_dim(g_modc.reshape(NDEV, cols), me, 1, axis=0)
    dmod16 = jnp.concatenate([dmod_lat, dmod_ctx, jnp.zeros((7, cols), F32)], axis=0)
    tr_ada = 256 if d % 256 == 0 else d
    g_wada, d_wada, m_wada, v_wada = _adamw_ada(s_mat.T, dmod16, w_ada2, m_w_ada[0], v_w_ada[0], tr_ada)
    g_win2, d_win, m_win, v_win = _adamw_scattered(sc_win, w_in2, m_w_in[0], v_w_in[0], tr_ada)
    tr_out = 64 if wo_rows % 64 == 0 else wo_rows
    g_wout2, d_wout, m_wout, v_wout = _adamw_scattered(sc_wout, w_out2, m_w_out[0], v_w_out[0], tr_out)

    small_w = [c_ctx, norm_g, b_ada, w_conv_a, w_conv_b, b_conv_b, lru_wa, lru_ba, lru_wx, lru_bx, lru_lambda, final_g]
    small_m = [m_c_ctx, m_norm_g, m_b_ada, m_w_conv_a, m_w_conv_b, m_b_conv_b, m_lru_wa, m_lru_ba, m_lru_wx,
               m_lru_bx, m_lru_lambda, m_final_g]
    small_v = [v_c_ctx, v_norm_g, v_b_ada, v_w_conv_a, v_w_conv_b, v_b_conv_b, v_lru_wa, v_lru_ba, v_lru_wx,
               v_lru_bx, v_lru_lambda, v_final_g]
    small_g = [g_cctx, g_norm, g_bada, g_ca_m, g_cb_m, g_bcb, g_wa, g_ba_m, g_wx, g_bx_m, g_lam_m, g_final]
    small_g = [jnp.reshape(a, b.shape) for a, b in zip(small_g, small_w)]
    d_s, m_s, v_s = _adamw_small(small_g, small_w, small_m, small_v)

    def weights(small_list, ada, win, wout):
        (cctx_, norm_, bada_, ca_, cb_, bcb_, wa_, ba_, wx_, bx_, lam_, final_) = small_list
        return [cctx_, norm_, ada[None], bada_, win[None], ca_, cb_, bcb_, wa_, ba_, wx_, bx_, lam_, wout[None], final_]

    return (loss, grad_x[None],
            *weights(small_g, g_wada, g_win2, g_wout2), *weights(d_s, d_wada, d_win, d_wout),
            *weights(m_s, m_wada, m_win, m_wout), *weights(v_s, v_wada, v_win, v_wout))
```

```python
import functools

import jax
import jax.numpy as jnp
import numpy as np
from jax import lax
from jax.experimental import pallas as pl
from jax.experimental.pallas import tpu as pltpu

F32 = jnp.float32
BF16 = jnp.bfloat16
MESH = pl.DeviceIdType.MESH
NDEV = 8
GRID_W = 64
N_HEADS = 16
LRU_C = 8.0
EPS = 1e-6
MXU_WIDTH = 256
VMEM_LIMIT = 60 * 1024 * 1024

ADAM_LR = 0.001
ADAM_B1 = 0.9
ADAM_B2 = 0.999
ADAM_EPS = 1e-08
ADAM_WD = 0.01
ADAM_STEP = 10
ADAM_C1 = 1.0 - ADAM_B1 ** ADAM_STEP
ADAM_C2 = 1.0 - ADAM_B2 ** ADAM_STEP

HIGHEST = lax.Precision.HIGHEST
ANY = pl.BlockSpec(memory_space=pl.ANY)
VMEM = pl.BlockSpec(memory_space=pltpu.VMEM)


def _call(body, **kw):
    return pl.pallas_call(body, **kw)


def _params(sem=None, vmem=VMEM_LIMIT):
    return pltpu.CompilerParams(dimension_semantics=sem, vmem_limit_bytes=vmem)


def _my_pos():
    return lax.axis_index("x"), lax.axis_index("y"), lax.axis_index("c")


def _idx(pos):
    return 4 * pos[0] + 2 * pos[1] + pos[2]


def _peer(k):
    x, y, c = _my_pos()
    return ((1 - x) if (k >> 2) & 1 else x, (1 - y) if (k >> 1) & 1 else y, (1 - c) if k & 1 else c)


def _exchange_start(src_ref, dst_ref, send_sems, recv_sems, base):
    me = _idx(_my_pos())
    sends = []
    for k in range(1, NDEV):
        cp = pltpu.make_async_remote_copy(
            src_ref=src_ref, dst_ref=dst_ref.at[me], send_sem=send_sems.at[base + k - 1],
            recv_sem=recv_sems.at[base + k - 1], device_id=_peer(k), device_id_type=MESH)
        cp.start()
        sends.append(cp)
    dst_ref[me] = src_ref[...]
    return sends, (src_ref, dst_ref, send_sems, recv_sems, base)


def _exchange_finish(started):
    sends, (src_ref, dst_ref, send_sems, recv_sems, base) = started
    for k in range(1, NDEV):
        peer = _peer(k)
        pltpu.make_async_remote_copy(
            src_ref=src_ref, dst_ref=dst_ref.at[_idx(peer)], send_sem=send_sems.at[base + k - 1],
            recv_sem=recv_sems.at[base + k - 1], device_id=peer, device_id_type=MESH).wait_recv()
    for cp in sends:
        cp.wait_send()


def _exchange_vmem(src_ref, dst_ref, send_sems, recv_sems, base):
    _exchange_finish(_exchange_start(src_ref, dst_ref, send_sems, recv_sems, base))


def _sigmoid(z):
    return 0.5 * jnp.tanh(0.5 * z) + 0.5


def _softplus(x):
    return jnp.maximum(x, 0.0) + jnp.log1p(jnp.exp(-jnp.abs(x)))


def _one_minus_sq(a, la):
    series = (-2.0 * la) * (1.0 + la)
    return jnp.where(la > -0.0015, series, 1.0 - a * a)


def _dot(a, b):
    return jnp.dot(a, b, preferred_element_type=F32)


def _dot_nt(a, b):
    return lax.dot_general(a, b, (((1,), (1,)), ((), ())), preferred_element_type=F32)


def _rows(shape):
    return lax.broadcasted_iota(jnp.int32, shape, 0)


def _scan_matrices(t):
    seg = t // 8
    r = np.arange(t)
    perm = (np.arange(t)[None, :] == ((r % 8) * seg + r // 8)[:, None]).astype(np.float32)
    rows, cols = r[:, None], r[None, :]
    taps, back = [], []
    for rowlen in (GRID_W, t):
        pos = rows % rowlen
        shift = {-2: (cols == rows - 2) & (pos >= 2), -1: (cols == rows - 1) & (pos >= 1),
                 0: cols == rows, 1: (cols == rows + 1) & (pos + 1 < rowlen),
                 2: (cols == rows + 2) & (pos + 2 < rowlen)}
        if rowlen == GRID_W:
            beside = [shift[-1].astype(np.float32), shift[1].astype(np.float32)]
        taps.append(np.stack([perm @ shift[k].astype(np.float32) for k in (-2, -1, 0, 1)]))
        back.append(np.stack([shift[k].astype(np.float32) @ perm.T for k in (2, 1, 0, -1)]))
    as_bf16 = lambda a: jnp.asarray(a, dtype=BF16)
    return as_bf16(np.stack(taps)), as_bf16(np.stack(back)), as_bf16(np.stack([perm, perm.T] + beside))


def _chunk_scan(a, b, reverse):
    row = _rows(a.shape)
    for s in (1, 2, 4):
        if reverse:
            m = row < 8 - s
            sh = 8 - s
        else:
            m = row >= s
            sh = s
        a_s = jnp.where(m, pltpu.roll(a, sh, 0), 1.0)
        b_s = jnp.where(m, pltpu.roll(b, sh, 0), 0.0)
        b = b + a * b_s
        a = a * a_s
    return a, b


def _chain_segments(ptot, hend, carry, reverse):
    ca, cb = _chunk_scan(ptot, hend, reverse)
    incl = ca * carry + cb
    r8 = _rows(incl.shape)
    if reverse:
        start = jnp.where(r8 < 7, pltpu.roll(incl, 7, 0), carry)
        last = incl[0:1, :]
    else:
        start = jnp.where(r8 >= 1, pltpu.roll(incl, 1, 0), carry)
        last = incl[7:8, :]
    return start, jnp.broadcast_to(last, incl.shape)


def _blocks(nblock, reverse):
    order = range(nblock - 1, -1, -1) if reverse else range(nblock)
    return [slice(8 * k, 8 * k + 8) for k in order]


def _scan_tile(a_ref, b_ref, out_ref, carry, reverse):
    t, w = a_ref.shape
    seg = t // 8

    hend, ptot = jnp.zeros((8, w), F32), jnp.ones((8, w), F32)
    for rows in _blocks(seg, reverse):
        a = a_ref[rows, :]
        hend, ptot = a * hend + b_ref[rows, :], a * ptot
    h, new_carry = _chain_segments(ptot, hend, carry, reverse)
    for rows in _blocks(seg, reverse):
        h = a_ref[rows, :] * h + b_ref[rows, :]
        out_ref[rows, :] = h
    return new_carry


def _scan_tile_backward(a_ref, dh_ref, g_ref, carry, reverse):
    t, w = a_ref.shape
    seg = t // 8

    uend, ptot = jnp.zeros((8, w), F32), jnp.ones((8, w), F32)
    for rows in _blocks(seg, reverse):
        a = a_ref[rows, :]
        uend, ptot = a * (dh_ref[rows, :] + uend), a * ptot
    u, new_carry = _chain_segments(ptot, uend, carry, reverse)
    for rows in _blocks(seg, reverse):
        g = dh_ref[rows, :] + u
        g_ref[rows, :] = g
        u = a_ref[rows, :] * g
    return new_carry


def _lru_coef(xb, wg_ref, d, ba, bx, lam, gc):
    w = xb.shape[1]
    xb16 = xb.astype(BF16)
    zr, zi = [], []
    for g in range(w // gc):
        z = _dot(xb16[:, g * gc:(g + 1) * gc], wg_ref[d, g])
        zr.append(z[:, :gc])
        zi.append(z[:, gc:])
    zr = zr[0] if len(zr) == 1 else jnp.concatenate(zr, axis=-1)
    zi = zi[0] if len(zi) == 1 else jnp.concatenate(zi, axis=-1)
    tr = jnp.tanh(zr + ba)
    ti = jnp.tanh(zi + bx)
    sp = _softplus(-lam)
    half = -0.5 * LRU_C * sp
    la = tr * half + half
    a = jnp.exp(la)
    q = _one_minus_sq(a, la)
    rs = lax.rsqrt(jnp.maximum(q, 1e-30))
    return a, q * rs, rs, tr, ti, sp


def _adamw(w, g, m, v):
    m2 = ADAM_B1 * m + (1.0 - ADAM_B1) * g
    v2 = ADAM_B2 * v + (1.0 - ADAM_B2) * (g * g)
    m_hat = m2 / ADAM_C1
    v_hat = v2 / ADAM_C2
    delta = -ADAM_LR * (m_hat / (jnp.sqrt(v_hat) + ADAM_EPS) + ADAM_WD * w)
    return delta, m2, v2


def _mod_forward(c8, cctx8, w_ada, small):
    d = c8.shape[1]
    cols = w_ada.shape[1]

    def body(c_ref, cctx_ref, w_ref, sm_ref, mod_ref, s_ref, sm_all, cbuf, mod_my, send_sems, recv_sems):
        _exchange_vmem(sm_ref, sm_all, send_sems, recv_sems, 2 * (NDEV - 1))
        _exchange_vmem(c_ref, cbuf, send_sems, recv_sems, 0)
        row = _rows((8, d))
        c_all = jnp.zeros((8, d), F32)
        for b in range(NDEV):
            c_all = jnp.where(row == b, cbuf[b], c_all)
        cc = cctx_ref[...]
        s_top = c_all * _sigmoid(c_all)
        s_bot = jnp.where(row == 0, cc * _sigmoid(cc), 0.0)
        s = jnp.concatenate([s_top, s_bot], axis=0)
        s_ref[...] = s
        mod_my[...] = jnp.dot(s, w_ref[...], precision=HIGHEST, preferred_element_type=F32)
        _exchange_vmem(mod_my, mod_ref, send_sems, recv_sems, NDEV - 1)

    return _call(
        body, name="mod_forward",
        out_shape=(jax.ShapeDtypeStruct((NDEV, 16, cols), F32), jax.ShapeDtypeStruct((16, d), F32),
                   jax.ShapeDtypeStruct((NDEV,) + small.shape, F32)),
        in_specs=[VMEM] * 4, out_specs=(VMEM,) * 3,
        scratch_shapes=[pltpu.VMEM((NDEV, 8, d), F32), pltpu.VMEM((16, cols), F32),
                        pltpu.SemaphoreType.DMA((3 * (NDEV - 1),)), pltpu.SemaphoreType.DMA((3 * (NDEV - 1),))],
        compiler_params=_params(),
    )(c8, cctx8, w_ada, small)


def _scatter_copies(src_ref, dst_ref, send_sems, recv_sems):
    me = _idx(_my_pos())
    copies = [pltpu.make_async_copy(src_ref.at[me], dst_ref.at[0], send_sems.at[0])]
    for k in range(1, NDEV):
        peer = _peer(k)
        copies.append(pltpu.make_async_remote_copy(
            src_ref=src_ref.at[_idx(peer)], dst_ref=dst_ref.at[k], send_sem=send_sems.at[k],
            recv_sem=recv_sems.at[k], device_id=peer, device_id_type=MESH))
    return copies


def _gather_copies(src_ref, dst_ref, send_sems, recv_sems):
    me = _idx(_my_pos())
    sends = [pltpu.make_async_copy(src_ref, dst_ref.at[me], send_sems.at[0])]
    arrivals = []
    for k in range(1, NDEV):
        peer = _peer(k)
        sends.append(pltpu.make_async_remote_copy(
            src_ref=src_ref, dst_ref=dst_ref.at[me], send_sem=send_sems.at[k],
            recv_sem=recv_sems.at[k], device_id=peer, device_id_type=MESH))
        arrivals.append(pltpu.make_async_remote_copy(
            src_ref=src_ref, dst_ref=dst_ref.at[_idx(peer)], send_sem=send_sems.at[k],
            recv_sem=recv_sems.at[k], device_id=peer, device_id_type=MESH))
    return sends, arrivals


def _exchange_wait(sends, arrivals):
    sends[0].wait()
    for cp in arrivals:
        cp.wait_recv()
    for cp in sends[1:]:
        cp.wait_send()


def _chip_order(k, c):
    return (6, 4 - 2 * c, 2 + 2 * c, 0)[k]


def _scatter_order(s, c):
    k = s >> 1
    mine = jnp.where(k == 0, 6, jnp.where(k == 1, 4 - 2 * c, jnp.where(k == 2, 2 + 2 * c, 0)))
    theirs = jnp.where(k == 0, 6, jnp.where(k == 1, 2 + 2 * c, jnp.where(k == 2, 4 - 2 * c, 0))) ^ 1
    return jnp.where((s & 1) == 0, theirs, mine)


def _peer_at(dist):
    x, y, c = _my_pos()
    return (x ^ ((dist >> 2) & 1), y ^ ((dist >> 1) & 1), c ^ (dist & 1))


def _reduce_small(packed, lru_parts, w_ada, dsilu_cctx):
    rp = packed.shape[0]
    rl = lru_parts.shape[1]
    d, cols = w_ada.shape
    assert cols % 128 == 0
    cb = cols // 128

    def body(p_ref, l_ref, w_ref, ds_ref, sum_ref, all_ref, lru_ref, cctx_ref,
             lbuf, lsum, cpart, call, send_sems, recv_sems, lsend, lrecv):
        me = _idx(_my_pos())
        scattered = _scatter_copies(l_ref, lbuf, lsend, lrecv)
        for cp in scattered:
            cp.start()
        _exchange_vmem(p_ref, all_ref, send_sems, recv_sems, 0)
        acc = all_ref[0]
        for j in range(1, NDEV):
            acc = acc + all_ref[j]
        sum_ref[...] = acc
        _exchange_wait(scattered, scattered[1:])
        red = lbuf[0]
        for k in range(1, NDEV):
            red = red + lbuf[k]
        lsum[...] = red
        lru_gather = _exchange_start(lsum, lru_ref, send_sems, recv_sems, NDEV - 1)
        part = jnp.zeros((8, d), F32)
        for q in range(cb):
            dm = jnp.broadcast_to(sum_ref[pl.ds((NDEV + me) * cb + q, 1), :], (8, 128))
            part = part + lax.dot_general(dm, w_ref[:, q * 128:(q + 1) * 128],
                                          (((1,), (1,)), ((), ())), precision=HIGHEST,
                                          preferred_element_type=F32)
        cpart[...] = part
        _exchange_vmem(cpart, call, send_sems, recv_sems, 2 * (NDEV - 1))
        _exchange_finish(lru_gather)
        tot = call[0]
        for j in range(1, NDEV):
            tot = tot + call[j]
        cctx_ref[...] = tot * ds_ref[...]

    return _call(
        body, name="reduce_small",
        out_shape=(jax.ShapeDtypeStruct((rp, 128), F32), jax.ShapeDtypeStruct((NDEV, rp, 128), F32),
                   jax.ShapeDtypeStruct((NDEV, rl, 128), F32), jax.ShapeDtypeStruct((8, d), F32)),
        in_specs=[VMEM] * 4, out_specs=(VMEM,) * 4,
        scratch_shapes=[pltpu.VMEM((NDEV, rl, 128), F32), pltpu.VMEM((rl, 128), F32), pltpu.VMEM((8, d), F32),
                        pltpu.VMEM((NDEV, 8, d), F32),
                        pltpu.SemaphoreType.DMA((3 * (NDEV - 1),)), pltpu.SemaphoreType.DMA((3 * (NDEV - 1),)),
                        pltpu.SemaphoreType.DMA((NDEV,)), pltpu.SemaphoreType.DMA((NDEV,))],
        compiler_params=_params(),
    )(packed, lru_parts, w_ada, dsilu_cctx)


def _normalize(src, mv, la, row0, tm, name, prev=None):
    rows, d = src.shape
    blk0 = row0 // tm

    def body(*refs):
        x_ref, mv_ref = refs[:2]
        h_ref, ht_ref = refs[-2:]
        xf = x_ref[...]
        r = lax.rsqrt(jnp.mean(xf * xf, axis=-1, keepdims=True) + EPS)
        h = xf * r * (mv_ref[0:1, :] * (1.0 + mv_ref[1:2, :])) + mv_ref[2:3, :]
        h_ref[...] = h.astype(BF16)
        ht_ref[...] = h.T.astype(BF16)

    in_specs = [pl.BlockSpec((tm, d), lambda i: (i, 0)), pl.BlockSpec((8, d), lambda i: (0, 0))]
    args = [src, mv]
    aliases = {}
    if prev is not None:
        in_specs += [ANY, ANY]
        args += list(prev)
        aliases = {2: 0, 3: 1}
    return _call(
        body, name=name,
        grid=(rows // tm,),
        out_shape=(jax.ShapeDtypeStruct((la, d), BF16), jax.ShapeDtypeStruct((d, la), BF16)),
        in_specs=in_specs,
        out_specs=(pl.BlockSpec((tm, d), lambda i: (blk0 + i, 0)), pl.BlockSpec((d, tm), lambda i: (0, blk0 + i))),
        input_output_aliases=aliases,
        compiler_params=_params(("arbitrary",)),
    )(*args)


def _gather_order(step):
    return (step & 1) | (((step >> 2) & 1) << 1) | (((step >> 1) & 1) << 2)


def _in_projection(h, w_shard, tm):
    la, d = h.shape
    bw = w_shard.shape[1]
    ni = la // tm
    where = jnp.reshape(_idx(_my_pos()), (1,)).astype(jnp.int32)

    def body(me_ref, h_ref, w_ref, p_ref, all_ref, wbuf, send_sems, recv_sems, local_sems):
        s, i = pl.program_id(0), pl.program_id(1)
        x, y, c = _my_pos()
        me, sibling = (x, y, c), (x, y, 1 - c)
        chips = [(1 - x, y), (x, 1 - y), (1 - x, 1 - y)]

        def copy(k, block, to, from_shard=False):
            return pltpu.make_async_remote_copy(
                src_ref=w_ref if from_shard else all_ref.at[_idx(block)], dst_ref=all_ref.at[_idx(block)],
                send_sem=send_sems.at[k], recv_sem=recv_sems.at[k], device_id=to, device_id_type=MESH)

        def load(block, slot):
            return pltpu.make_async_copy(all_ref.at[_idx(block)], wbuf.at[slot], local_sems.at[1])

        keep = pltpu.make_async_copy(w_ref, all_ref.at[_idx(me)], local_sems.at[0])
        first = [copy(0, me, sibling, True)] + [copy(1 + j, me, (*chip, c), True) for j, chip in enumerate(chips)]
        passed = [copy(4 + j, (*chip, c), sibling) for j, chip in enumerate(chips)]
        steps = [(copy(0, sibling, me), None, sibling)]
        for j, chip in enumerate(chips):
            steps.append((copy(1 + j, (*chip, c), me), passed[j], (*chip, c)))
            steps.append((copy(4 + j, (*chip, 1 - c), me), None, (*chip, 1 - c)))

        @pl.when((s == 0) & (i == 0))
        def _():
            keep.start()
            mine = pltpu.make_async_copy(w_ref, wbuf.at[0], local_sems.at[1])
            mine.start()
            for cp in first:
                cp.start()
            mine.wait()

        for n, (arrival, forward, block) in enumerate(steps, start=1):
            @pl.when((s == n - 1) & (i == ni - 1))
            def _(arrival=arrival, forward=forward, block=block, n=n):
                arrival.wait_recv()
                if forward is not None:
                    forward.start()
                load(block, n % 2).start()

        @pl.when((s > 0) & (i == 0))
        def _():
            load(me, s % 2).wait()

        p_ref[...] = _dot(h_ref[...], wbuf[s % 2]).astype(BF16)

        @pl.when((s == NDEV - 1) & (i == ni - 1))
        def _():
            for cp in first + passed:
                cp.wait_send()
            keep.wait()

    return _call(
        body, name="in_projection",
        grid_spec=pltpu.PrefetchScalarGridSpec(
            num_scalar_prefetch=1, grid=(NDEV, ni),
            in_specs=[pl.BlockSpec((tm, d), lambda s, i, me_ref: (i, 0)), ANY],
            out_specs=(pl.BlockSpec((tm, bw), lambda s, i, me_ref: (i, me_ref[0] ^ _gather_order(s))), ANY),
            scratch_shapes=[pltpu.VMEM((2, d, bw), BF16), pltpu.SemaphoreType.DMA((7,)),
                            pltpu.SemaphoreType.DMA((7,)), pltpu.SemaphoreType.DMA((2,))]),
        out_shape=(jax.ShapeDtypeStruct((la, NDEV * bw), BF16), jax.ShapeDtypeStruct((NDEV, d, bw), BF16)),
        compiler_params=_params(("arbitrary", "arbitrary")),
    )(where, h, w_shard)


def _conv_input(p, wcb, taps_m, l, t):
    la = p.shape[0]
    w = wcb.shape[1]
    nt = l // t

    def body(v_ref, wcb_ref, tm_ref, xb_ref):
        v16 = v_ref[...]
        xb = wcb_ref[4:5, :] + wcb_ref[0:1, :] * _dot(tm_ref[0], v16)
        for j in range(1, 4):
            xb = xb + wcb_ref[j:j + 1, :] * _dot(tm_ref[j], v16)
        xb_ref[...] = xb

    return _call(
        body, name="conv_input",
        grid=(nt + 1,),
        out_shape=jax.ShapeDtypeStruct((la, w), F32),
        in_specs=[pl.BlockSpec((t, w), lambda i: (i, 4)), pl.BlockSpec((8, w), lambda i: (0, 0)),
                  pl.BlockSpec((None, 4, t, t), lambda i: (i // nt, 0, 0, 0))],
        out_specs=pl.BlockSpec((t, w), lambda i: (i, 0)),
        compiler_params=_params(("arbitrary",)),
    )(p, wcb, taps_m)


def _lru_forward(xb, wg, lv, wo_shard, l, t):
    la, w = xb.shape
    gc = wg.shape[2]
    nt = l // t

    def body(xf_ref, xr_ref, wg_ref, lv_ref, wo_ref, hf_ref, hr_ref, wo_all,
             a_s, b_s, carry, send_sems, recv_sems):
        sends, arrivals = _gather_copies(wo_ref, wo_all, send_sems, recv_sems)

        @pl.when(pl.program_id(0) == 0)
        def _():
            carry[...] = jnp.zeros_like(carry)
            for cp in sends:
                cp.start()

        @pl.when(pl.program_id(0) == nt)
        def _():
            _exchange_wait(sends, arrivals)

        for dr, (x_ref, h_ref) in enumerate(((xf_ref, hf_ref), (xr_ref, hr_ref))):
            x = x_ref[...]
            a, s, _, _, ti, _ = _lru_coef(x, wg_ref, dr, lv_ref[3 * dr:3 * dr + 1, :],
                                          lv_ref[3 * dr + 1:3 * dr + 2, :], lv_ref[3 * dr + 2:3 * dr + 3, :], gc)
            a_s[...] = a
            b_s[...] = (s * x) * (0.5 * ti + 0.5)
            carry[dr] = _scan_tile(a_s, b_s, h_ref, carry[dr], dr == 1)

    full = lambda shape: pl.BlockSpec(shape, lambda i: (0,) * len(shape))
    fmap = lambda i: (jnp.where(i == 0, nt, i - 1), 0)
    rmap = lambda i: (jnp.where(i == 0, nt, nt - i), 0)
    return _call(
        body, name="lru_forward",
        grid=(nt + 1,),
        out_shape=(jax.ShapeDtypeStruct((la, w), F32), jax.ShapeDtypeStruct((la, w), F32),
                   jax.ShapeDtypeStruct((NDEV,) + wo_shard.shape, wo_shard.dtype)),
        in_specs=[pl.BlockSpec((t, w), fmap), pl.BlockSpec((t, w), rmap), full(wg.shape), full(lv.shape), ANY],
        out_specs=(pl.BlockSpec((t, w), fmap), pl.BlockSpec((t, w), rmap), ANY),
        scratch_shapes=[pltpu.VMEM((t, w), F32), pltpu.VMEM((t, w), F32), pltpu.VMEM((2, 8, w), F32),
                        pltpu.SemaphoreType.DMA((NDEV,)), pltpu.SemaphoreType.DMA((NDEV,))],
        compiler_params=_params(("arbitrary",)),
    )(xb, xb, wg, lv, wo_shard)


def _mix_gates(p_refs, hf_ref, hr_ref, wca_ref, perm_ref, t, w):
    bl, cl, ul, gl, ql = [r[...].astype(F32) for r in p_refs]
    tt = cl * ul
    tt16 = tt.astype(BF16)
    before, after = _dot(perm_ref[2], tt16), _dot(perm_ref[3], tt16)
    z = wca_ref[0:1, :] * before + wca_ref[1:2, :] * tt + wca_ref[2:3, :] * after
    sig_g = _sigmoid(gl)
    sig_q = _sigmoid(ql)
    ylru = _dot(perm_ref[1], (hf_ref[...] + hr_ref[...]).astype(BF16))
    return bl, cl, ul, gl, ql, (before, tt, after), z, sig_g, sig_q, ylru


def _p_specs(t, w, nt):
    return [pl.BlockSpec((t, w), functools.partial(lambda i, s: (jnp.minimum(i, nt - 1), s), s=s))
            for s in (0, 1, 2, 3, 5)]


def _mix_forward(x, tgt, p, hf, hr, wo, ov, wca, perm, t):
    l, d = x.shape
    w = d // 2
    nt = l // t

    def body(x_ref, tg_ref, b_ref, c_ref, u_ref, g_ref, q_ref, hf_ref, hr_ref, wo_ref, ov_ref, wca_ref, perm_ref,
             dn_ref, ct_ref, do_ref, part_ref):
        i = pl.program_id(0)
        bl, _, _, gl, ql, _, z, sig_g, sig_q, ylru = _mix_gates(
            (b_ref, c_ref, u_ref, g_ref, q_ref), hf_ref, hr_ref, wca_ref, perm_ref, t, w)
        ya = bl * z * (gl * sig_g)
        yb = ylru * (ql * sig_q)
        ct_ref[0:w, :] = ya.T.astype(BF16)
        ct_ref[w:, :] = yb.T.astype(BF16)
        out = _dot(ya.astype(BF16), wo_ref[0:w, :]) + _dot(yb.astype(BF16), wo_ref[w:, :])
        gate, fg = ov_ref[0:1, :], ov_ref[1:2, :]
        n = x_ref[...] + gate * out
        rr = lax.rsqrt(jnp.mean(n * n, axis=-1, keepdims=True) + EPS)
        nh = n * rr
        e = nh * fg - tg_ref[...]
        loss = 0.5 * jnp.sum(jnp.mean(e * e, axis=-1, keepdims=True), axis=0, keepdims=True)
        dy = e * (1.0 / d)
        dnh = dy * fg
        dn = rr * (dnh - nh * jnp.mean(dnh * nh, axis=-1, keepdims=True))
        dn_ref[...] = dn.astype(BF16)
        do_ref[...] = (dn * gate).astype(BF16)

        @pl.when(i == 0)
        def _():
            part_ref[...] = jnp.zeros_like(part_ref)

        part_ref[0:1, :] += jnp.sum(dy * nh, axis=0, keepdims=True)
        part_ref[1:2, :] += jnp.sum(dn * out, axis=0, keepdims=True)
        part_ref[2:3, :] += jnp.broadcast_to(loss, (1, d))

    tile = lambda cols: pl.BlockSpec((t, cols), lambda i: (i, 0))
    full = lambda shape: pl.BlockSpec(shape, lambda i: (0,) * len(shape))
    return _call(
        body, name="mix_forward",
        grid=(nt,),
        out_shape=(jax.ShapeDtypeStruct((l, d), BF16), jax.ShapeDtypeStruct((d, l), BF16),
                   jax.ShapeDtypeStruct((l, d), BF16), jax.ShapeDtypeStruct((8, d), F32)),
        in_specs=[tile(d), tile(d)] + _p_specs(t, w, nt) + [tile(w), tile(w),
                  pl.BlockSpec((d, d), lambda i: (0, 0), pipeline_mode=pl.Buffered(1)),
                  full(ov.shape), full(wca.shape), full(perm.shape)],
        out_specs=(tile(d), pl.BlockSpec((d, t), lambda i: (0, i)), tile(d), full((8, d))),
        compiler_params=_params(("arbitrary",)),
    )(x, tgt, p, p, p, p, p, hf, hr, wo, ov, wca, perm)


def _mix_backward(dout, p, hf, hr, wo, wca, perm, l, t):
    d = dout.shape[1]
    w = d // 2
    nt = l // t
    la = p.shape[0]

    def body(do_ref, b_ref, c_ref, u_ref, g_ref, q_ref, hf_ref, hr_ref, wo_ref, wca_ref, perm_ref,
             dp_ref, dh_ref, part_ref):
        i = pl.program_id(0)

        @pl.when(i == 0)
        def _():
            part_ref[...] = jnp.zeros_like(part_ref)

        @pl.when(i == nt)
        def _():
            dp_ref[...] = jnp.zeros_like(dp_ref)

        @pl.when(i < nt)
        def _():
            bl, cl, ul, gl, ql, taps, z, sig_g, sig_q, ylru = _mix_gates(
                (b_ref, c_ref, u_ref, g_ref, q_ref), hf_ref, hr_ref, wca_ref, perm_ref, t, w)
            do = do_ref[...]
            dya = _dot_nt(do, wo_ref[0:w, :])
            dyb = _dot_nt(do, wo_ref[w:, :])
            sg = gl * sig_g
            dz = dya * bl * sg
            dz16 = dz.astype(BF16)
            dt = (wca_ref[0:1, :] * _dot(perm_ref[3], dz16) + wca_ref[1:2, :] * dz
                  + wca_ref[2:3, :] * _dot(perm_ref[2], dz16))
            dp_ref[:, 0:w] = (dya * z * sg).astype(BF16)
            dp_ref[:, w:2 * w] = (dt * ul).astype(BF16)
            dp_ref[:, 2 * w:3 * w] = (dt * cl).astype(BF16)
            dp_ref[:, 3 * w:4 * w] = (dya * bl * z * (sig_g * (1.0 + gl * (1.0 - sig_g)))).astype(BF16)
            dp_ref[:, 4 * w:5 * w] = jnp.zeros((t, w), BF16)
            dp_ref[:, 5 * w:6 * w] = (dyb * ylru * (sig_q * (1.0 + ql * (1.0 - sig_q)))).astype(BF16)
            dh_ref[...] = _dot(perm_ref[0], (dyb * (ql * sig_q)).astype(BF16)).astype(BF16)
            for j in range(3):
                part_ref[j:j + 1, :] += jnp.sum(dz * taps[j], axis=0, keepdims=True)

    clamp = lambda cols: pl.BlockSpec((t, cols), lambda i: (jnp.minimum(i, nt - 1), 0))
    full = lambda shape: pl.BlockSpec(shape, lambda i: (0,) * len(shape))
    return _call(
        body, name="mix_backward",
        grid=(nt + 1,),
        out_shape=(jax.ShapeDtypeStruct((la, 6 * w), BF16), jax.ShapeDtypeStruct((l, w), BF16),
                   jax.ShapeDtypeStruct((8, w), F32)),
        in_specs=[clamp(d)] + _p_specs(t, w, nt) + [clamp(w), clamp(w),
                  pl.BlockSpec((d, d), lambda i: (0, 0), pipeline_mode=pl.Buffered(1)), full(wca.shape),
                  full(perm.shape)],
        out_specs=(pl.BlockSpec((t, 6 * w), lambda i: (i, 0)), clamp(w), full((8, w))),
        compiler_params=_params(("arbitrary",)),
    )(dout, p, p, p, p, p, hf, hr, wo, wca, perm)


def _lru_backward(direction, xb, dhs, hs, wg, lv, l, t, extra):
    la, w = hs.shape
    gc = wg.shape[2]
    ng = w // gc
    nt = l // t
    nblk8 = la // 8
    last = direction == 1
    n_in = 6 + len(extra)

    if direction == 0:
        tile = lambda i: jnp.where(i == nt, nt, nt - 1 - i)
        halo = lambda i: jnp.where(tile(i) == 0, nblk8 - 1, tile(i) * (t // 8) - 1)
    else:
        tile = lambda i: i
        halo = lambda i: jnp.minimum((i + 1) * (t // 8), nblk8 - 1)

    def body(*refs):
        x_ref, dh_ref, hs_ref, halo_ref, wg_ref, lv_ref = refs[:6]
        out_ref, dwg_ref, part_ref, more_ref, a_s, dh_s, g_s, carry = refs[n_in:n_in + 8]
        i = pl.program_id(0)
        is_ctx = i == nt
        if last:
            v_ref, wcb_ref, bm_ref, dxo_ref, _, gw_ref = refs[6:n_in]
            copies = _scatter_copies(gw_ref, more_ref, *refs[n_in + 8:])
        else:
            ct_ref, do_ref = refs[6:n_in]
            acc = refs[n_in + 8]

        @pl.when(i == 0)
        def _():
            carry[...] = jnp.zeros_like(carry)
            dwg_ref[...] = jnp.zeros_like(dwg_ref)
            part_ref[...] = jnp.zeros_like(part_ref)
            if last:
                for cp in copies:
                    cp.start()
            else:
                acc[...] = jnp.zeros_like(acc)

        if last:
            @pl.when(is_ctx)
            def _():
                _exchange_wait(copies, copies[1:])
        else:
            @pl.when(i < nt)
            def _():
                acc[...] += _dot(ct_ref[...], do_ref[...])

            @pl.when(is_ctx)
            def _():
                more_ref[...] = acc[...].astype(BF16)

        xb = x_ref[...]
        lam = lv_ref[3 * direction + 2:3 * direction + 3, :]
        a, s, rs, tr, ti, sp = _lru_coef(xb, wg_ref, direction, lv_ref[3 * direction:3 * direction + 1, :],
                                         lv_ref[3 * direction + 1:3 * direction + 2, :], lam, gc)
        hs_t = hs_ref[...]
        r8 = _rows((8, w))
        if direction == 0:
            edge = jnp.where(is_ctx, 0.0, halo_ref[7:8, :])
            first = jnp.where(r8 == 0, edge, pltpu.roll(hs_t[t - 8:, :], 1, 0))
            hprev = jnp.concatenate([first, hs_t[:t - 8, :]], axis=0)
        else:
            edge = jnp.where(is_ctx, 0.0, halo_ref[0:1, :])
            final = jnp.where(r8 == 7, edge, pltpu.roll(hs_t[:8, :], 7, 0))
            hprev = jnp.concatenate([hs_t[8:, :], final], axis=0)
        a_s[...] = a
        dh_s[...] = jnp.where(is_ctx, 0.0, dh_ref[...].astype(F32))
        carry[...] = _scan_tile_backward(a_s, dh_s, g_s, carry[...], direction == 0)

        g = g_s[...]
        r = 0.5 * tr + 0.5
        ig = 0.5 * ti + 0.5
        ix = ig * xb
        gs = g * s
        dla = (g * a) * (hprev - ix * (a * rs))
        dxb = gs * ig
        dzr = dla * (r * (1.0 - tr)) * (-LRU_C * sp)
        dzi = gs * ix * (1.0 - ti)
        part_ref[0:1, :] += jnp.sum(dzr, axis=0, keepdims=True)
        part_ref[1:2, :] += jnp.sum(dzi, axis=0, keepdims=True)
        part_ref[2:3, :] += jnp.sum(dla * r, axis=0, keepdims=True) * (LRU_C * _sigmoid(-lam))
        pieces = []
        for gi in range(ng):
            sl = slice(gi * gc, (gi + 1) * gc)
            dz = jnp.concatenate([dzr[:, sl], dzi[:, sl]], axis=-1).astype(BF16)
            pieces.append(_dot_nt(dz, wg_ref[direction, gi]))
            dwg_ref[gi] += _dot(xb[:, sl].T.astype(BF16), dz)
        dxb = dxb + (pieces[0] if ng == 1 else jnp.concatenate(pieces, axis=-1))
        if not last:
            out_ref[...] = dxb
        else:
            dxb = dxb + dxo_ref[...]
            dxb16, v = dxb.astype(BF16), v_ref[...].astype(F32)
            dv = jnp.zeros((t, w), F32)
            for j in range(4):
                back = _dot(bm_ref[j], dxb16)
                dv = dv + wcb_ref[j:j + 1, :] * back
                part_ref[4 + j:5 + j, :] += jnp.sum(back * v, axis=0, keepdims=True)
            out_ref[...] = dv.astype(BF16)
            part_ref[3:4, :] += jnp.sum(dxb, axis=0, keepdims=True)

    full = lambda shape: pl.BlockSpec(shape, lambda i: (0,) * len(shape))
    kind = lambda i: (jnp.where(i == nt, 1, 0), 0, 0, 0)
    in_specs = [pl.BlockSpec((t, w), lambda i: (tile(i), 0)),
                pl.BlockSpec((t, w), lambda i: (jnp.minimum(tile(i), nt - 1), 0)),
                pl.BlockSpec((t, w), lambda i: (tile(i), 0)),
                pl.BlockSpec((8, w), lambda i: (halo(i), 0)),
                full(wg.shape), full(lv.shape)]
    args = [xb, dhs, hs, hs, wg, lv] + list(extra)
    scratch = [pltpu.VMEM((t, w), F32), pltpu.VMEM((t, w), F32), pltpu.VMEM((t, w), F32), pltpu.VMEM((8, w), F32)]
    if last:
        p, wcb, back_m, dxb_other, dp, g_wout = extra
        in_specs += [pl.BlockSpec((t, w), lambda i: (tile(i), 4)), full(wcb.shape),
                     pl.BlockSpec((None, 4, t, t), kind), pl.BlockSpec((t, w), lambda i: (tile(i), 0)), ANY, ANY]
        out0 = jax.ShapeDtypeStruct(dp.shape, dp.dtype)
        spec0 = pl.BlockSpec((t, w), lambda i: (tile(i), 4))
        more, more_spec = jax.ShapeDtypeStruct(g_wout.shape, g_wout.dtype), ANY
        scratch += [pltpu.SemaphoreType.DMA((NDEV,)), pltpu.SemaphoreType.DMA((NDEV,))]
        aliases = {10: 0}
    else:
        catt, dout = extra
        d = dout.shape[1]
        latent = lambda i: jnp.minimum(tile(i), nt - 1)
        in_specs += [pl.BlockSpec((d, t), lambda i: (0, latent(i))), pl.BlockSpec((t, d), lambda i: (latent(i), 0))]
        out0 = jax.ShapeDtypeStruct((la, w), F32)
        spec0 = pl.BlockSpec((t, w), lambda i: (tile(i), 0))
        more, more_spec = jax.ShapeDtypeStruct((d, d), BF16), full((d, d))
        scratch += [pltpu.VMEM((d, d), F32)]
        aliases = {}
    return _call(
        body, name="lru_backward_%d" % direction,
        grid=(nt + 1,),
        out_shape=(out0, jax.ShapeDtypeStruct((ng, gc, 2 * gc), F32), jax.ShapeDtypeStruct((8, w), F32), more),
        in_specs=in_specs,
        out_specs=(spec0, full((ng, gc, 2 * gc)), full((8, w)), more_spec),
        scratch_shapes=scratch,
        input_output_aliases=aliases,
        compiler_params=_params(("arbitrary",)),
    )(*args)


def _weight_grad_scatter(at, b, tk, name):
    m, k = at.shape
    n = b.shape[1]
    bn = n // NDEV
    nk = k // tk
    where = jnp.stack([_idx(_my_pos()), lax.axis_index("c")]).astype(jnp.int32)

    def body(w_ref, a_ref, b_ref, recv_ref, acc, sbuf, sib, sib_send, sib_recv, chip_send, chip_recv, keep_sem):
        s, kk = pl.program_id(0), pl.program_id(1)
        x, y, c = _my_pos()

        @pl.when(kk == 0)
        def _():
            acc[...] = _dot(a_ref[...], b_ref[...])

        @pl.when(kk > 0)
        def _():
            acc[...] += _dot(a_ref[...], b_ref[...])

        def to_sibling(j):
            return pltpu.make_async_remote_copy(
                src_ref=sbuf.at[0], dst_ref=sib.at[j], send_sem=sib_send.at[j], recv_sem=sib_recv.at[j],
                device_id=(x, y, 1 - c), device_id_type=MESH)

        def to_chip(j):
            dist = _chip_order(j, c)
            return pltpu.make_async_remote_copy(
                src_ref=sbuf.at[1], dst_ref=recv_ref.at[dist // 2], send_sem=chip_send.at[j],
                recv_sem=chip_recv.at[dist // 2], device_id=_peer_at(dist), device_id_type=MESH)

        keep = pltpu.make_async_copy(sbuf.at[1], recv_ref.at[0], keep_sem)
        sends = []
        for j in range(4):
            sends += [to_sibling(j), to_chip(j) if j < 3 else keep]

        for st in range(NDEV):
            @pl.when((kk == nk - 1) & (s == st))
            def _(st=st):
                if st >= 2:
                    sends[st - 2].wait_send()
                part = acc[...]
                if st % 2 == 1:
                    to_sibling(st // 2).wait_recv()
                    part = part + sib[st // 2].astype(F32)
                sbuf[st % 2] = part.astype(BF16)
                sends[st].start()
                if st == NDEV - 1:
                    sends[st - 1].wait_send()
                    sends[st].wait()
                    for j in range(1, 4):
                        pltpu.make_async_remote_copy(
                            src_ref=sbuf.at[0], dst_ref=recv_ref.at[j], send_sem=chip_send.at[0],
                            recv_sem=chip_recv.at[j], device_id=_peer_at(2 * j), device_id_type=MESH).wait_recv()

    blk = lambda s, w_ref: w_ref[0] ^ _scatter_order(s, w_ref[1])
    return _call(
        body, name=name,
        grid_spec=pltpu.PrefetchScalarGridSpec(
            num_scalar_prefetch=1, grid=(NDEV, nk),
            in_specs=[pl.BlockSpec((m, tk), lambda s, kk, w_ref: (0, kk)),
                      pl.BlockSpec((tk, bn), lambda s, kk, w_ref: (kk, blk(s, w_ref)))],
            out_specs=ANY,
            scratch_shapes=[pltpu.VMEM((m, bn), F32), pltpu.VMEM((2, m, bn), BF16), pltpu.VMEM((4, m, bn), BF16),
                            pltpu.SemaphoreType.DMA((4,)), pltpu.SemaphoreType.DMA((4,)),
                            pltpu.SemaphoreType.DMA((4,)), pltpu.SemaphoreType.DMA((4,)),
                            pltpu.SemaphoreType.DMA]),
        out_shape=jax.ShapeDtypeStruct((4, m, bn), BF16),
        compiler_params=_params(("arbitrary", "arbitrary")),
    )(where, at, b)


def _input_backward(dp, w_all, src, mv, row0, tm, nbk, name, dn=None):
    rows, d = src.shape
    nb, _, bw = w_all.shape
    nk = nb // nbk
    ni = rows // tm
    blk0 = row0 // tm
    latent = dn is not None

    def body(*refs):
        dp_ref, w_ref, x_ref, mv_ref = refs[:4]
        outs = refs[4 + latent:]
        part_ref, acc = outs[latent], outs[latent + 1]
        i, k = pl.program_id(0), pl.program_id(1)

        def product():
            step = _dot_nt(dp_ref[:, 0:bw], w_ref[0])
            for q in range(1, nbk):
                step = step + _dot_nt(dp_ref[:, q * bw:(q + 1) * bw], w_ref[q])
            return step

        def finish(slot):
            xf = x_ref[...]
            r = lax.rsqrt(jnp.mean(xf * xf, axis=-1, keepdims=True) + EPS)
            xn = xf * r
            dhl = acc[slot]
            gain, sc = mv_ref[0:1, :], mv_ref[1:2, :]
            dhx = jnp.sum(dhl * xn, axis=0, keepdims=True)
            part_ref[0:1, :] += jnp.sum(dhl, axis=0, keepdims=True)
            part_ref[1:2, :] += dhx * gain
            part_ref[2:3, :] += dhx * (1.0 + sc)
            if latent:
                dxn = dhl * (gain * (1.0 + sc))
                outs[0][...] = (refs[4][...].astype(F32)
                                + r * (dxn - xn * jnp.mean(dxn * xn, axis=-1, keepdims=True)))

        @pl.when((i == 0) & (k == 0))
        def _():
            part_ref[...] = jnp.zeros_like(part_ref)
            acc[0] = product()

        @pl.when((i > 0) & (i < ni) & (k == 0))
        def _():
            acc[i % 2] = product()
            finish((i - 1) % 2)

        @pl.when((i == ni) & (k == 0))
        def _():
            finish((ni - 1) % 2)

        @pl.when((i < ni) & (k > 0))
        def _():
            acc[i % 2] += product()

    tile = pl.BlockSpec((tm, d), lambda i, k: (jnp.maximum(i - 1, 0), 0))
    vec = pl.BlockSpec((8, d), lambda i, k: (0, 0))
    kblock = lambda i, k: jnp.where(i == ni, nk - 1, k)
    return _call(
        body, name=name,
        grid=(ni + 1, nk),
        out_shape=((jax.ShapeDtypeStruct((rows, d), F32),) if latent else ()) + (jax.ShapeDtypeStruct((8, d), F32),),
        in_specs=[pl.BlockSpec((tm, nbk * bw), lambda i, k: (blk0 + jnp.minimum(i, ni - 1), kblock(i, k))),
                  pl.BlockSpec((nbk, d, bw), lambda i, k: (kblock(i, k), 0, 0)), tile, vec]
                 + ([tile] if latent else []),
        out_specs=((tile,) if latent else ()) + (vec,),
        scratch_shapes=[pltpu.VMEM((2, tm, d), F32)],
        compiler_params=_params(("arbitrary", "arbitrary")),
    )(*([dp, w_all, src, mv] + ([dn] if latent else [])))


def _adamw_scattered(parts, w, m, v, tr):
    r, c = w.shape
    nslot = parts.shape[0]

    def body(p_ref, w_ref, m_ref, v_ref, g_ref, d_ref, m2_ref, v2_ref):
        g = p_ref[0].astype(F32)
        for k in range(1, nslot):
            g = g + p_ref[k].astype(F32)
        g_ref[...] = g
        d_ref[...], m2_ref[...], v2_ref[...] = _adamw(w_ref[...], g, m_ref[...], v_ref[...])

    tile = pl.BlockSpec((tr, c), lambda i: (i, 0))
    return _call(
        body, name="adamw_scattered_%dx%d" % (r, c),
        grid=(r // tr,),
        out_shape=tuple(jax.ShapeDtypeStruct((r, c), F32) for _ in range(4)),
        in_specs=[pl.BlockSpec((nslot, tr, c), lambda i: (0, i, 0)), tile, tile, tile],
        out_specs=(tile,) * 4,
        compiler_params=_params(("arbitrary",)),
    )(parts, w, m, v)


def _adamw_ada(st, dmod, w, m, v, tr):
    r, c = w.shape

    def body(s_ref, dm_ref, w_ref, m_ref, v_ref, g_ref, d_ref, m2_ref, v2_ref):
        g = jnp.dot(s_ref[...], dm_ref[...], precision=HIGHEST, preferred_element_type=F32)
        g_ref[...] = g
        d_ref[...], m2_ref[...], v2_ref[...] = _adamw(w_ref[...], g, m_ref[...], v_ref[...])

    tile = pl.BlockSpec((tr, c), lambda i: (i, 0))
    return _call(
        body, name="adamw_ada",
        grid=(r // tr,),
        out_shape=tuple(jax.ShapeDtypeStruct((r, c), F32) for _ in range(4)),
        in_specs=[pl.BlockSpec((tr, 16), lambda i: (i, 0)), pl.BlockSpec((16, c), lambda i: (0, 0)),
                  tile, tile, tile],
        out_specs=(tile,) * 4,
        compiler_params=_params(("arbitrary",)),
    )(st, dmod, w, m, v)


def _adamw_small(gs, ws, ms, vs):
    n = len(ws)

    def body(*refs):
        for j in range(n):
            g_ref, w_ref, m_ref, v_ref = refs[j], refs[n + j], refs[2 * n + j], refs[3 * n + j]
            d_ref, m2_ref, v2_ref = refs[4 * n + j], refs[5 * n + j], refs[6 * n + j]
            d_ref[...], m2_ref[...], v2_ref[...] = _adamw(w_ref[...], g_ref[...], m_ref[...], v_ref[...])

    shapes = tuple(jax.ShapeDtypeStruct(a.shape, F32) for a in ws)
    out = _call(
        body, name="adamw_small",
        out_shape=shapes * 3,
        in_specs=[VMEM] * (4 * n), out_specs=(VMEM,) * (3 * n),
        compiler_params=_params(),
    )(*gs, *ws, *ms, *vs)
    return list(out[:n]), list(out[n:2 * n]), list(out[2 * n:])


def _blockdiag_groups(wh, gc):
    h, dh, _ = wh.shape
    g = gc // dh
    w4 = wh.reshape(h // g, g, dh, dh)
    bd = jnp.einsum("ngij,gh->ngihj", w4, jnp.eye(g, dtype=wh.dtype))
    return bd.reshape(h // g, gc, gc)


def _blockdiag_extract(bd, dh):
    ng, gc, _ = bd.shape
    g = gc // dh
    x = bd.reshape(ng, g, dh, g, dh)
    return jnp.einsum("ngihj,gh->ngij", x, jnp.eye(g, dtype=bd.dtype)).reshape(ng * g, dh, dh)


def _largest_tile(n, cap):
    return max(q for q in range(128, min(n, cap) + 1, 128) if n % q == 0)


def _rows8(*vecs):
    rows = [jnp.reshape(v, (1, -1)).astype(F32) for v in vecs]
    n = rows[0].shape[1]
    return jnp.concatenate(rows + [jnp.zeros((8 - len(rows), n), F32)], axis=0)


def _pack(pieces):
    flat = jnp.concatenate([jnp.reshape(a, (-1,)).astype(F32) for a in pieces])
    total = -(-flat.shape[0] // 1024) * 1024
    return jnp.pad(flat, (0, total - flat.shape[0])).reshape(total // 128, 128)


def _unpack(packed, shapes):
    flat = packed.reshape(-1)
    out, off = [], 0
    for s in shapes:
        n = 1
        for q in s:
            n *= q
        out.append(flat[off:off + n].reshape(s))
        off += n
    return out


def kernel(x, c, ctx, c_ctx, norm_g, w_ada, b_ada, w_in, w_conv_a, w_conv_b, b_conv_b, lru_wa, lru_ba, lru_wx, lru_bx, lru_lambda, w_out, final_g, loss_target, m_c_ctx, m_norm_g, m_w_ada, m_b_ada, m_w_in, m_w_conv_a, m_w_conv_b, m_b_conv_b, m_lru_wa, m_lru_ba, m_lru_wx, m_lru_bx, m_lru_lambda, m_w_out, m_final_g, v_c_ctx, v_norm_g, v_w_ada, v_b_ada, v_w_in, v_w_conv_a, v_w_conv_b, v_b_conv_b, v_lru_wa, v_lru_ba, v_lru_wx, v_lru_bx, v_lru_lambda, v_w_out, v_final_g):
    _, l, d = x.shape
    lc = ctx.shape[1]
    w = d // 2
    t = lc
    assert l % t == 0 and t % GRID_W == 0 and t % 128 == 0
    dh = w // N_HEADS
    gc = min(w, MXU_WIDTH)
    cols = w_ada.shape[2]
    wo_rows = w_out.shape[1]
    me = _idx(_my_pos())
    x2, ctx2, tgt2 = x[0], ctx[0], loss_target[0]
    w_ada2, w_in2, w_out2 = w_ada[0], w_in[0], w_out[0]

    small_mine = jnp.concatenate([a.reshape(-1) for a in (w_conv_a, w_conv_b, lru_ba, lru_bx, lru_lambda)]
                                 + [jnp.zeros((3 * (w // NDEV),), F32)]).reshape(16, w // NDEV)
    mod_all, s_mat, small_all = _mod_forward(
        jnp.broadcast_to(c, (8, d)), jnp.broadcast_to(c_ctx[None], (8, d)), w_ada2, small_mine)
    mod = jnp.transpose(mod_all, (1, 0, 2)).reshape(16, NDEV * cols) + b_ada
    mod_lat = lax.dynamic_slice_in_dim(mod, me, 1, axis=0)
    sh_l, sc_l, gt_l = jnp.split(mod_lat, 3, axis=-1)
    sh_c, sc_c, _ = jnp.split(mod[8:9], 3, axis=-1)
    small = jnp.transpose(small_all, (1, 0, 2)).reshape(16, w)
    wca = _rows8(*[small[j] for j in range(0, 3)])
    wcb = _rows8(*[small[j] for j in range(3, 7)], b_conv_b)
    lv = _rows8(0.5 * small[7], 0.5 * small[9], small[11], 0.5 * small[8], 0.5 * small[10], small[12])
    wg = jnp.stack([
        jnp.concatenate([_blockdiag_groups(lru_wa[0, dr], gc), _blockdiag_groups(lru_wx[0, dr], gc)], axis=-1)
        for dr in range(2)])
    wg = (0.5 * wg).astype(BF16)

    la = l + lc
    tm = 2 * t if l % (2 * t) == 0 else t
    tk = 3 * t if la % (3 * t) == 0 else t
    h, hlt = _normalize(x2, _rows8(norm_g, sc_l, sh_l), la, 0, tm, "normalize")
    h, hlt = _normalize(ctx2, _rows8(norm_g, sc_c, sh_c), la, l, t, "normalize_ctx", prev=(h, hlt))
    p, w_all = _in_projection(h, w_in2.astype(BF16), la // 8 if la % 128 == 0 else tk)
    taps_m, back_m, perm = _scan_matrices(t)
    xb = _conv_input(p, wcb, taps_m, l, t)
    hf, hr, wo_all = _lru_forward(xb, wg, lv, w_out2.astype(BF16), l, t)
    wo = wo_all.reshape(d, d)
    dn, catt, dout, part_mix = _mix_forward(x2, tgt2, p, hf, hr, wo, _rows8(gt_l, final_g), wca, perm, t)
    dp, dhs, part_ca = _mix_backward(dout, p, hf, hr, wo, wca, perm, l, t)
    dxb0, dwg0, part_l0, g_wout = _lru_backward(0, xb, dhs, hf, wg, lv, l, t, (catt, dout))
    dp, dwg1, part_l1, sc_wout = _lru_backward(
        1, xb, dhs, hr, wg, lv, l, t, (p, wcb, back_m, dxb0, dp, g_wout.reshape(NDEV, wo_rows, d)))
    sc_win = _weight_grad_scatter(hlt, dp, _largest_tile(la, 1408), "grad_w_in")
    grad_x, part_lat = _input_backward(dp, w_all, x2, _rows8(norm_g, sc_l), 0, tm, 2, "input_backward", dn=dn)
    (part_ctx,) = _input_backward(dp, w_all, ctx2, _rows8(norm_g, sc_c), l, t, 2, "input_backward_ctx")
    part_in = jnp.concatenate([part_lat[0:2], part_ctx[0:2], (part_lat[2] + part_ctx[2])[None]], axis=0)

    dwa = jnp.stack([_blockdiag_extract(dwg0[:, :, :gc], dh), _blockdiag_extract(dwg1[:, :, :gc], dh)])
    dwx = jnp.stack([_blockdiag_extract(dwg0[:, :, gc:], dh), _blockdiag_extract(dwg1[:, :, gc:], dh)])
    lru_part = (0.5 * jnp.stack([dwa, dwx])).reshape(NDEV, -1, 128)
    zeros_d = jnp.zeros((d,), F32)
    pieces = [
        jnp.concatenate([part_in[0], part_in[1], part_mix[1]]),
        jnp.concatenate([part_in[2], part_in[3], zeros_d]),
        part_in[4], part_mix[0], part_ca[0:3], part_l1[4:8], part_l1[3],
        0.5 * jnp.stack([part_l0[0], part_l1[0]]), 0.5 * jnp.stack([part_l0[1], part_l1[1]]),
        jnp.stack([part_l0[2], part_l1[2]]), part_mix[2, 0:1],
    ]
    shapes = [(3 * d,), (3 * d,), (d,), (d,), (3, w), (4, w), (w,), (2, w), (2, w), (2, w), (1,)]
    sig_cc = jax.nn.sigmoid(c_ctx)
    dsilu_cc = jnp.broadcast_to((sig_cc * (1.0 + c_ctx * (1.0 - sig_cc)))[None], (8, d))
    psum, pall, lru_sum, g_cctx8 = _reduce_small(_pack(pieces), lru_part, w_ada2, dsilu_cc)
    (g_modl, g_modc, g_norm, g_final, g_ca, g_cb, g_bcb, g_ba, g_bx, g_lam, loss1) = _unpack(psum, shapes)
    loss = loss1[0]
    g_cctx = g_cctx8[0]
    g_bada = (g_modl + g_modc)[None]
    g_lru = lru_sum.reshape(2, 2, N_HEADS, dh, dh)
    g_wa, g_wx = g_lru[0][None], g_lru[1][None]
    wsl = w // NDEV
    mine = lambda a: lax.dynamic_slice_in_dim(a, me * wsl, wsl, axis=-1)
    g_ca_m, g_cb_m, g_ba_m, g_bx_m, g_lam_m = (mine(g_ca)[None], mine(g_cb)[None], mine(g_ba)[None],
                                               mine(g_bx)[None], mine(g_lam)[None])
    g_norm, g_bcb = g_norm[None], g_bcb[None]

    cb = cols // 128
    per_dev = pall[:, :3 * d // 128].reshape(NDEV, NDEV, cols)
    dmod_lat = lax.dynamic_slice_in_dim(per_dev, me, 1, axis=1)[:, 0]
    dmod_ctx = lax.dynamic_slice_in_dim(g_modc.reshape(NDEV, cols), me, 1, axis=0)
    dmod16 = jnp.concatenate([dmod_lat, dmod_ctx, jnp.zeros((7, cols), F32)], axis=0)
    tr_ada = 256 if d % 256 == 0 else d
    g_wada, d_wada, m_wada, v_wada = _adamw_ada(s_mat.T, dmod16, w_ada2, m_w_ada[0], v_w_ada[0], tr_ada)
    g_win2, d_win, m_win, v_win = _adamw_scattered(sc_win, w_in2, m_w_in[0], v_w_in[0], tr_ada)
    tr_out = 64 if wo_rows % 64 == 0 else wo_rows
    g_wout2, d_wout, m_wout, v_wout = _adamw_scattered(sc_wout, w_out2, m_w_out[0], v_w_out[0], tr_out)

    small_w = [c_ctx, norm_g, b_ada, w_conv_a, w_conv_b, b_conv_b, lru_wa, lru_ba, lru_wx, lru_bx, lru_lambda, final_g]
    small_m = [m_c_ctx, m_norm_g, m_b_ada, m_w_conv_a, m_w_conv_b, m_b_conv_b, m_lru_wa, m_lru_ba, m_lru_wx,
               m_lru_bx, m_lru_lambda, m_final_g]
    small_v = [v_c_ctx, v_norm_g, v_b_ada, v_w_conv_a, v_w_conv_b, v_b_conv_b, v_lru_wa, v_lru_ba, v_lru_wx,
               v_lru_bx, v_lru_lambda, v_final_g]
    small_g = [g_cctx, g_norm, g_bada, g_ca_m, g_cb_m, g_bcb, g_wa, g_ba_m, g_wx, g_bx_m, g_lam_m, g_final]
    small_g = [jnp.reshape(a, b.shape) for a, b in zip(small_g, small_w)]
    d_s, m_s, v_s = _adamw_small(small_g, small_w, small_m, small_v)

    def weights(small_list, ada, win, wout):
        (cctx_, norm_, bada_, ca_, cb_, bcb_, wa_, ba_, wx_, bx_, lam_, final_) = small_list
        return [cctx_, norm_, ada[None], bada_, win[None], ca_, cb_, bcb_, wa_, ba_, wx_, bx_, lam_, wout[None], final_]

    return (loss, grad_x[None],
            *weights(small_g, g_wada, g_win2, g_wout2), *weights(d_s, d_wada, d_win, d_wout),
            *weights(m_s, m_wada, m_win, m_wout), *weights(v_s, v_wada, v_win, v_wout))
```

```python
import functools

import jax
import jax.numpy as jnp
import numpy as np
from jax import lax
from jax.experimental import pallas as pl
from jax.experimental.pallas import tpu as pltpu

F32 = jnp.float32
BF16 = jnp.bfloat16
MESH = pl.DeviceIdType.MESH
NDEV = 8
GRID_W = 64
N_HEADS = 16
LRU_C = 8.0
EPS = 1e-6
MXU_WIDTH = 256
VMEM_LIMIT = 60 * 1024 * 1024

ADAM_LR = 0.001
ADAM_B1 = 0.9
ADAM_B2 = 0.999
ADAM_EPS = 1e-08
ADAM_WD = 0.01
ADAM_STEP = 10
ADAM_C1 = 1.0 - ADAM_B1 ** ADAM_STEP
ADAM_C2 = 1.0 - ADAM_B2 ** ADAM_STEP

HIGHEST = lax.Precision.HIGHEST
ANY = pl.BlockSpec(memory_space=pl.ANY)
VMEM = pl.BlockSpec(memory_space=pltpu.VMEM)


def _call(body, **kw):
    return pl.pallas_call(body, **kw)


def _params(sem=None, vmem=VMEM_LIMIT):
    return pltpu.CompilerParams(dimension_semantics=sem, vmem_limit_bytes=vmem)


def _my_pos():
    return lax.axis_index("x"), lax.axis_index("y"), lax.axis_index("c")


def _idx(pos):
    return 4 * pos[0] + 2 * pos[1] + pos[2]


def _peer(k):
    x, y, c = _my_pos()
    return ((1 - x) if (k >> 2) & 1 else x, (1 - y) if (k >> 1) & 1 else y, (1 - c) if k & 1 else c)


def _exchange_start(src_ref, dst_ref, send_sems, recv_sems, base):
    me = _idx(_my_pos())
    sends = []
    for k in range(1, NDEV):
        cp = pltpu.make_async_remote_copy(
            src_ref=src_ref, dst_ref=dst_ref.at[me], send_sem=send_sems.at[base + k - 1],
            recv_sem=recv_sems.at[base + k - 1], device_id=_peer(k), device_id_type=MESH)
        cp.start()
        sends.append(cp)
    dst_ref[me] = src_ref[...]
    return sends, (src_ref, dst_ref, send_sems, recv_sems, base)


def _exchange_finish(started):
    sends, (src_ref, dst_ref, send_sems, recv_sems, base) = started
    for k in range(1, NDEV):
        peer = _peer(k)
        pltpu.make_async_remote_copy(
            src_ref=src_ref, dst_ref=dst_ref.at[_idx(peer)], send_sem=send_sems.at[base + k - 1],
            recv_sem=recv_sems.at[base + k - 1], device_id=peer, device_id_type=MESH).wait_recv()
    for cp in sends:
        cp.wait_send()


def _exchange_vmem(src_ref, dst_ref, send_sems, recv_sems, base):
    _exchange_finish(_exchange_start(src_ref, dst_ref, send_sems, recv_sems, base))


def _sigmoid(z):
    return 0.5 * jnp.tanh(0.5 * z) + 0.5


def _softplus(x):
    return jnp.maximum(x, 0.0) + jnp.log1p(jnp.exp(-jnp.abs(x)))


def _one_minus_sq(a, la):
    series = (-2.0 * la) * (1.0 + la)
    return jnp.where(la > -0.0015, series, 1.0 - a * a)


def _dot(a, b):
    return jnp.dot(a, b, preferred_element_type=F32)


def _dot_nt(a, b):
    return lax.dot_general(a, b, (((1,), (1,)), ((), ())), preferred_element_type=F32)


def _rows(shape):
    return lax.broadcasted_iota(jnp.int32, shape, 0)


def _scan_matrices(t):
    seg = t // 8
    r = np.arange(t)
    perm = (np.arange(t)[None, :] == ((r % 8) * seg + r // 8)[:, None]).astype(np.float32)
    rows, cols = r[:, None], r[None, :]
    taps, back = [], []
    for rowlen in (GRID_W, t):
        pos = rows % rowlen
        shift = {-2: (cols == rows - 2) & (pos >= 2), -1: (cols == rows - 1) & (pos >= 1),
                 0: cols == rows, 1: (cols == rows + 1) & (pos + 1 < rowlen),
                 2: (cols == rows + 2) & (pos + 2 < rowlen)}
        if rowlen == GRID_W:
            beside = [shift[-1].astype(np.float32), shift[1].astype(np.float32)]
        taps.append(np.stack([perm @ shift[k].astype(np.float32) for k in (-2, -1, 0, 1)]))
        back.append(np.stack([shift[k].astype(np.float32) @ perm.T for k in (2, 1, 0, -1)]))
    as_bf16 = lambda a: jnp.asarray(a, dtype=BF16)
    return as_bf16(np.stack(taps)), as_bf16(np.stack(back)), as_bf16(np.stack([perm, perm.T] + beside))


def _chunk_scan(a, b, reverse):
    row = _rows(a.shape)
    for s in (1, 2, 4):
        if reverse:
            m = row < 8 - s
            sh = 8 - s
        else:
            m = row >= s
            sh = s
        a_s = jnp.where(m, pltpu.roll(a, sh, 0), 1.0)
        b_s = jnp.where(m, pltpu.roll(b, sh, 0), 0.0)
        b = b + a * b_s
        a = a * a_s
    return a, b


def _chain_segments(ptot, hend, carry, reverse):
    ca, cb = _chunk_scan(ptot, hend, reverse)
    incl = ca * carry + cb
    r8 = _rows(incl.shape)
    if reverse:
        start = jnp.where(r8 < 7, pltpu.roll(incl, 7, 0), carry)
        last = incl[0:1, :]
    else:
        start = jnp.where(r8 >= 1, pltpu.roll(incl, 1, 0), carry)
        last = incl[7:8, :]
    return start, jnp.broadcast_to(last, incl.shape)


def _blocks(nblock, reverse):
    order = range(nblock - 1, -1, -1) if reverse else range(nblock)
    return [slice(8 * k, 8 * k + 8) for k in order]


def _scan_tile(a_ref, b_ref, out_ref, carry, reverse):
    t, w = a_ref.shape
    seg = t // 8

    hend, ptot = jnp.zeros((8, w), F32), jnp.ones((8, w), F32)
    for rows in _blocks(seg, reverse):
        a = a_ref[rows, :]
        hend, ptot = a * hend + b_ref[rows, :], a * ptot
    h, new_carry = _chain_segments(ptot, hend, carry, reverse)
    for rows in _blocks(seg, reverse):
        h = a_ref[rows, :] * h + b_ref[rows, :]
        out_ref[rows, :] = h
    return new_carry


def _scan_tile_backward(a_ref, dh_ref, g_ref, carry, reverse):
    t, w = a_ref.shape
    seg = t // 8

    uend, ptot = jnp.zeros((8, w), F32), jnp.ones((8, w), F32)
    for rows in _blocks(seg, reverse):
        a = a_ref[rows, :]
        uend, ptot = a * (dh_ref[rows, :] + uend), a * ptot
    u, new_carry = _chain_segments(ptot, uend, carry, reverse)
    for rows in _blocks(seg, reverse):
        g = dh_ref[rows, :] + u
        g_ref[rows, :] = g
        u = a_ref[rows, :] * g
    return new_carry


def _lru_coef(xb, wg_ref, d, ba, bx, lam, gc):
    w = xb.shape[1]
    xb16 = xb.astype(BF16)
    zr, zi = [], []
    for g in range(w // gc):
        z = _dot(xb16[:, g * gc:(g + 1) * gc], wg_ref[d, g])
        zr.append(z[:, :gc])
        zi.append(z[:, gc:])
    zr = zr[0] if len(zr) == 1 else jnp.concatenate(zr, axis=-1)
    zi = zi[0] if len(zi) == 1 else jnp.concatenate(zi, axis=-1)
    tr = jnp.tanh(zr + ba)
    ti = jnp.tanh(zi + bx)
    sp = _softplus(-lam)
    half = -0.5 * LRU_C * sp
    la = tr * half + half
    a = jnp.exp(la)
    q = _one_minus_sq(a, la)
    rs = lax.rsqrt(jnp.maximum(q, 1e-30))
    return a, q * rs, rs, tr, ti, sp


def _adamw(w, g, m, v):
    m2 = ADAM_B1 * m + (1.0 - ADAM_B1) * g
    v2 = ADAM_B2 * v + (1.0 - ADAM_B2) * (g * g)
    m_hat = m2 / ADAM_C1
    v_hat = v2 / ADAM_C2
    delta = -ADAM_LR * (m_hat / (jnp.sqrt(v_hat) + ADAM_EPS) + ADAM_WD * w)
    return delta, m2, v2


def _mod_forward(c8, cctx8, w_ada, small):
    d = c8.shape[1]
    cols = w_ada.shape[1]

    def body(c_ref, cctx_ref, w_ref, sm_ref, mod_ref, s_ref, sm_all, cbuf, mod_my, send_sems, recv_sems):
        _exchange_vmem(sm_ref, sm_all, send_sems, recv_sems, 2 * (NDEV - 1))
        _exchange_vmem(c_ref, cbuf, send_sems, recv_sems, 0)
        row = _rows((8, d))
        c_all = jnp.zeros((8, d), F32)
        for b in range(NDEV):
            c_all = jnp.where(row == b, cbuf[b], c_all)
        cc = cctx_ref[...]
        s_top = c_all * _sigmoid(c_all)
        s_bot = jnp.where(row == 0, cc * _sigmoid(cc), 0.0)
        s = jnp.concatenate([s_top, s_bot], axis=0)
        s_ref[...] = s
        mod_my[...] = jnp.dot(s, w_ref[...], precision=HIGHEST, preferred_element_type=F32)
        _exchange_vmem(mod_my, mod_ref, send_sems, recv_sems, NDEV - 1)

    return _call(
        body, name="mod_forward",
        out_shape=(jax.ShapeDtypeStruct((NDEV, 16, cols), F32), jax.ShapeDtypeStruct((16, d), F32),
                   jax.ShapeDtypeStruct((NDEV,) + small.shape, F32)),
        in_specs=[VMEM] * 4, out_specs=(VMEM,) * 3,
        scratch_shapes=[pltpu.VMEM((NDEV, 8, d), F32), pltpu.VMEM((16, cols), F32),
                        pltpu.SemaphoreType.DMA((3 * (NDEV - 1),)), pltpu.SemaphoreType.DMA((3 * (NDEV - 1),))],
        compiler_params=_params(),
    )(c8, cctx8, w_ada, small)


def _scatter_copies(src_ref, dst_ref, send_sems, recv_sems):
    me = _idx(_my_pos())
    copies = [pltpu.make_async_copy(src_ref.at[me], dst_ref.at[0], send_sems.at[0])]
    for k in range(1, NDEV):
        peer = _peer(k)
        copies.append(pltpu.make_async_remote_copy(
            src_ref=src_ref.at[_idx(peer)], dst_ref=dst_ref.at[k], send_sem=send_sems.at[k],
            recv_sem=recv_sems.at[k], device_id=peer, device_id_type=MESH))
    return copies


def _gather_copies(src_ref, dst_ref, send_sems, recv_sems):
    me = _idx(_my_pos())
    sends = [pltpu.make_async_copy(src_ref, dst_ref.at[me], send_sems.at[0])]
    arrivals = []
    for k in range(1, NDEV):
        peer = _peer(k)
        sends.append(pltpu.make_async_remote_copy(
            src_ref=src_ref, dst_ref=dst_ref.at[me], send_sem=send_sems.at[k],
            recv_sem=recv_sems.at[k], device_id=peer, device_id_type=MESH))
        arrivals.append(pltpu.make_async_remote_copy(
            src_ref=src_ref, dst_ref=dst_ref.at[_idx(peer)], send_sem=send_sems.at[k],
            recv_sem=recv_sems.at[k], device_id=peer, device_id_type=MESH))
    return sends, arrivals


def _exchange_wait(sends, arrivals):
    sends[0].wait()
    for cp in arrivals:
        cp.wait_recv()
    for cp in sends[1:]:
        cp.wait_send()


def _chip_order(k, c):
    return (6, 4 - 2 * c, 2 + 2 * c, 0)[k]


def _scatter_order(s, c):
    k = s >> 1
    mine = jnp.where(k == 0, 6, jnp.where(k == 1, 4 - 2 * c, jnp.where(k == 2, 2 + 2 * c, 0)))
    theirs = jnp.where(k == 0, 6, jnp.where(k == 1, 2 + 2 * c, jnp.where(k == 2, 4 - 2 * c, 0))) ^ 1
    return jnp.where((s & 1) == 0, theirs, mine)


def _peer_at(dist):
    x, y, c = _my_pos()
    return (x ^ ((dist >> 2) & 1), y ^ ((dist >> 1) & 1), c ^ (dist & 1))


def _reduce_small(packed, lru_parts, w_ada, dsilu_cctx):
    rp = packed.shape[0]
    rl = lru_parts.shape[1]
    d, cols = w_ada.shape
    assert cols % 128 == 0
    cb = cols // 128

    def body(p_ref, l_ref, w_ref, ds_ref, sum_ref, all_ref, lru_ref, cctx_ref,
             lbuf, lsum, cpart, call, send_sems, recv_sems, lsend, lrecv):
        me = _idx(_my_pos())
        scattered = _scatter_copies(l_ref, lbuf, lsend, lrecv)
        for cp in scattered:
            cp.start()
        _exchange_vmem(p_ref, all_ref, send_sems, recv_sems, 0)
        acc = all_ref[0]
        for j in range(1, NDEV):
            acc = acc + all_ref[j]
        sum_ref[...] = acc
        _exchange_wait(scattered, scattered[1:])
        red = lbuf[0]
        for k in range(1, NDEV):
            red = red + lbuf[k]
        lsum[...] = red
        lru_gather = _exchange_start(lsum, lru_ref, send_sems, recv_sems, NDEV - 1)
        part = jnp.zeros((8, d), F32)
        for q in range(cb):
            dm = jnp.broadcast_to(sum_ref[pl.ds((NDEV + me) * cb + q, 1), :], (8, 128))
            part = part + lax.dot_general(dm, w_ref[:, q * 128:(q + 1) * 128],
                                          (((1,), (1,)), ((), ())), precision=HIGHEST,
                                          preferred_element_type=F32)
        cpart[...] = part
        _exchange_vmem(cpart, call, send_sems, recv_sems, 2 * (NDEV - 1))
        _exchange_finish(lru_gather)
        tot = call[0]
        for j in range(1, NDEV):
            tot = tot + call[j]
        cctx_ref[...] = tot * ds_ref[...]

    return _call(
        body, name="reduce_small",
        out_shape=(jax.ShapeDtypeStruct((rp, 128), F32), jax.ShapeDtypeStruct((NDEV, rp, 128), F32),
                   jax.ShapeDtypeStruct((NDEV, rl, 128), F32), jax.ShapeDtypeStruct((8, d), F32)),
        in_specs=[VMEM] * 4, out_specs=(VMEM,) * 4,
        scratch_shapes=[pltpu.VMEM((NDEV, rl, 128), F32), pltpu.VMEM((rl, 128), F32), pltpu.VMEM((8, d), F32),
                        pltpu.VMEM((NDEV, 8, d), F32),
                        pltpu.SemaphoreType.DMA((3 * (NDEV - 1),)), pltpu.SemaphoreType.DMA((3 * (NDEV - 1),)),
                        pltpu.SemaphoreType.DMA((NDEV,)), pltpu.SemaphoreType.DMA((NDEV,))],
        compiler_params=_params(),
    )(packed, lru_parts, w_ada, dsilu_cctx)


def _normalize(src, mv, la, row0, tm, name, prev=None):
    rows, d = src.shape
    blk0 = row0 // tm

    def body(*refs):
        x_ref, mv_ref = refs[:2]
        h_ref, ht_ref = refs[-2:]
        xf = x_ref[...]
        r = lax.rsqrt(jnp.mean(xf * xf, axis=-1, keepdims=True) + EPS)
        h = xf * r * (mv_ref[0:1, :] * (1.0 + mv_ref[1:2, :])) + mv_ref[2:3, :]
        h_ref[...] = h.astype(BF16)
        ht_ref[...] = h.T.astype(BF16)

    in_specs = [pl.BlockSpec((tm, d), lambda i: (i, 0)), pl.BlockSpec((8, d), lambda i: (0, 0))]
    args = [src, mv]
    aliases = {}
    if prev is not None:
        in_specs += [ANY, ANY]
        args += list(prev)
        aliases = {2: 0, 3: 1}
    return _call(
        body, name=name,
        grid=(rows // tm,),
        out_shape=(jax.ShapeDtypeStruct((la, d), BF16), jax.ShapeDtypeStruct((d, la), BF16)),
        in_specs=in_specs,
        out_specs=(pl.BlockSpec((tm, d), lambda i: (blk0 + i, 0)), pl.BlockSpec((d, tm), lambda i: (0, blk0 + i))),
        input_output_aliases=aliases,
        compiler_params=_params(("arbitrary",)),
    )(*args)


def _gather_order(step):
    return (step & 1) | (((step >> 2) & 1) << 1) | (((step >> 1) & 1) << 2)


def _in_projection(h, w_shard, tm):
    la, d = h.shape
    bw = w_shard.shape[1]
    ni = la // tm
    where = jnp.reshape(_idx(_my_pos()), (1,)).astype(jnp.int32)

    def body(me_ref, h_ref, w_ref, p_ref, all_ref, wbuf, send_sems, recv_sems, local_sems):
        s, i = pl.program_id(0), pl.program_id(1)
        x, y, c = _my_pos()
        me, sibling = (x, y, c), (x, y, 1 - c)
        chips = [(1 - x, y), (x, 1 - y), (1 - x, 1 - y)]

        def copy(k, block, to, from_shard=False):
            return pltpu.make_async_remote_copy(
                src_ref=w_ref if from_shard else all_ref.at[_idx(block)], dst_ref=all_ref.at[_idx(block)],
                send_sem=send_sems.at[k], recv_sem=recv_sems.at[k], device_id=to, device_id_type=MESH)

        def load(block, slot):
            return pltpu.make_async_copy(all_ref.at[_idx(block)], wbuf.at[slot], local_sems.at[1])

        keep = pltpu.make_async_copy(w_ref, all_ref.at[_idx(me)], local_sems.at[0])
        first = [copy(0, me, sibling, True)] + [copy(1 + j, me, (*chip, c), True) for j, chip in enumerate(chips)]
        passed = [copy(4 + j, (*chip, c), sibling) for j, chip in enumerate(chips)]
        steps = [(copy(0, sibling, me), None, sibling)]
        for j, chip in enumerate(chips):
            steps.append((copy(1 + j, (*chip, c), me), passed[j], (*chip, c)))
            steps.append((copy(4 + j, (*chip, 1 - c), me), None, (*chip, 1 - c)))

        @pl.when((s == 0) & (i == 0))
        def _():
            keep.start()
            mine = pltpu.make_async_copy(w_ref, wbuf.at[0], local_sems.at[1])
            mine.start()
            for cp in first:
                cp.start()
            mine.wait()

        for n, (arrival, forward, block) in enumerate(steps, start=1):
            @pl.when((s == n - 1) & (i == ni - 1))
            def _(arrival=arrival, forward=forward, block=block, n=n):
                arrival.wait_recv()
                if forward is not None:
                    forward.start()
                load(block, n % 2).start()

        @pl.when((s > 0) & (i == 0))
        def _():
            load(me, s % 2).wait()

        p_ref[...] = _dot(h_ref[...], wbuf[s % 2]).astype(BF16)

        @pl.when((s == NDEV - 1) & (i == ni - 1))
        def _():
            for cp in first + passed:
                cp.wait_send()
            keep.wait()

    return _call(
        body, name="in_projection",
        grid_spec=pltpu.PrefetchScalarGridSpec(
            num_scalar_prefetch=1, grid=(NDEV, ni),
            in_specs=[pl.BlockSpec((tm, d), lambda s, i, me_ref: (i, 0)), ANY],
            out_specs=(pl.BlockSpec((tm, bw), lambda s, i, me_ref: (i, me_ref[0] ^ _gather_order(s))), ANY),
            scratch_shapes=[pltpu.VMEM((2, d, bw), BF16), pltpu.SemaphoreType.DMA((7,)),
                            pltpu.SemaphoreType.DMA((7,)), pltpu.SemaphoreType.DMA((2,))]),
        out_shape=(jax.ShapeDtypeStruct((la, NDEV * bw), BF16), jax.ShapeDtypeStruct((NDEV, d, bw), BF16)),
        compiler_params=_params(("arbitrary", "arbitrary")),
    )(where, h, w_shard)


def _conv_input(p, wcb, taps_m, l, t):
    la = p.shape[0]
    w = wcb.shape[1]
    nt = l // t

    def body(v_ref, wcb_ref, tm_ref, xb_ref):
        taps = _dot(tm_ref[...].reshape(4 * t, t), v_ref[...])
        xb = wcb_ref[4:5, :] + wcb_ref[0:1, :] * taps[0:t]
        for j in range(1, 4):
            xb = xb + wcb_ref[j:j + 1, :] * taps[j * t:(j + 1) * t]
        xb_ref[...] = xb

    return _call(
        body, name="conv_input",
        grid=(nt + 1,),
        out_shape=jax.ShapeDtypeStruct((la, w), F32),
        in_specs=[pl.BlockSpec((t, w), lambda i: (i, 4)), pl.BlockSpec((8, w), lambda i: (0, 0)),
                  pl.BlockSpec((None, 4, t, t), lambda i: (i // nt, 0, 0, 0))],
        out_specs=pl.BlockSpec((t, w), lambda i: (i, 0)),
        compiler_params=_params(("arbitrary",)),
    )(p, wcb, taps_m)


def _lru_forward(xb, wg, lv, wo_shard, l, t):
    la, w = xb.shape
    gc = wg.shape[2]
    nt = l // t

    def body(xf_ref, xr_ref, wg_ref, lv_ref, wo_ref, hf_ref, hr_ref, wo_all,
             a_s, b_s, carry, send_sems, recv_sems):
        sends, arrivals = _gather_copies(wo_ref, wo_all, send_sems, recv_sems)

        @pl.when(pl.program_id(0) == 0)
        def _():
            carry[...] = jnp.zeros_like(carry)
            for cp in sends:
                cp.start()

        @pl.when(pl.program_id(0) == nt)
        def _():
            _exchange_wait(sends, arrivals)

        for dr, (x_ref, h_ref) in enumerate(((xf_ref, hf_ref), (xr_ref, hr_ref))):
            x = x_ref[...]
            a, s, _, _, ti, _ = _lru_coef(x, wg_ref, dr, lv_ref[3 * dr:3 * dr + 1, :],
                                          lv_ref[3 * dr + 1:3 * dr + 2, :], lv_ref[3 * dr + 2:3 * dr + 3, :], gc)
            a_s[...] = a
            b_s[...] = (s * x) * (0.5 * ti + 0.5)
            carry[dr] = _scan_tile(a_s, b_s, h_ref, carry[dr], dr == 1)

    full = lambda shape: pl.BlockSpec(shape, lambda i: (0,) * len(shape))
    fmap = lambda i: (jnp.where(i == 0, nt, i - 1), 0)
    rmap = lambda i: (jnp.where(i == 0, nt, nt - i), 0)
    return _call(
        body, name="lru_forward",
        grid=(nt + 1,),
        out_shape=(jax.ShapeDtypeStruct((la, w), F32), jax.ShapeDtypeStruct((la, w), F32),
                   jax.ShapeDtypeStruct((NDEV,) + wo_shard.shape, wo_shard.dtype)),
        in_specs=[pl.BlockSpec((t, w), fmap), pl.BlockSpec((t, w), rmap), full(wg.shape), full(lv.shape), ANY],
        out_specs=(pl.BlockSpec((t, w), fmap), pl.BlockSpec((t, w), rmap), ANY),
        scratch_shapes=[pltpu.VMEM((t, w), F32), pltpu.VMEM((t, w), F32), pltpu.VMEM((2, 8, w), F32),
                        pltpu.SemaphoreType.DMA((NDEV,)), pltpu.SemaphoreType.DMA((NDEV,))],
        compiler_params=_params(("arbitrary",)),
    )(xb, xb, wg, lv, wo_shard)


def _mix_gates(p_refs, hf_ref, hr_ref, wca_ref, perm_ref, t, w):
    bl, cl, ul, gl, ql = [r[...].astype(F32) for r in p_refs]
    tt = cl * ul
    tt16 = tt.astype(BF16)
    beside = _dot(perm_ref[2:4].reshape(2 * t, t), tt16)
    before, after = beside[:t], beside[t:]
    z = wca_ref[0:1, :] * before + wca_ref[1:2, :] * tt + wca_ref[2:3, :] * after
    sig_g = _sigmoid(gl)
    sig_q = _sigmoid(ql)
    ylru = _dot(perm_ref[1], (hf_ref[...] + hr_ref[...]).astype(BF16))
    return bl, cl, ul, gl, ql, (before, tt, after), z, sig_g, sig_q, ylru


def _p_specs(t, w, nt):
    return [pl.BlockSpec((t, w), functools.partial(lambda i, s: (jnp.minimum(i, nt - 1), s), s=s))
            for s in (0, 1, 2, 3, 5)]


def _mix_forward(x, tgt, p, hf, hr, wo, ov, wca, perm, t):
    l, d = x.shape
    w = d // 2
    nt = l // t

    def body(x_ref, tg_ref, b_ref, c_ref, u_ref, g_ref, q_ref, hf_ref, hr_ref, wo_ref, ov_ref, wca_ref, perm_ref,
             dn_ref, ct_ref, do_ref, part_ref, out_s):
        i = pl.program_id(0)

        def project(slot):
            bl, _, _, gl, ql, _, z, sig_g, sig_q, ylru = _mix_gates(
                (b_ref, c_ref, u_ref, g_ref, q_ref), hf_ref, hr_ref, wca_ref, perm_ref, t, w)
            ya = bl * z * (gl * sig_g)
            yb = ylru * (ql * sig_q)
            ct_ref[0:w, :] = ya.T.astype(BF16)
            ct_ref[w:, :] = yb.T.astype(BF16)
            out_s[slot] = _dot(ya.astype(BF16), wo_ref[0:w, :]) + _dot(yb.astype(BF16), wo_ref[w:, :])

        def finish(slot):
            out = out_s[slot]
            gate, fg = ov_ref[0:1, :], ov_ref[1:2, :]
            n = x_ref[...] + gate * out
            rr = lax.rsqrt(jnp.mean(n * n, axis=-1, keepdims=True) + EPS)
            nh = n * rr
            e = nh * fg - tg_ref[...]
            loss = jnp.sum(jnp.sum(e * e, axis=-1, keepdims=True), axis=0, keepdims=True) * (0.5 / d)
            dnh = e * (fg * (1.0 / d))
            dn = rr * (dnh - nh * jnp.mean(dnh * nh, axis=-1, keepdims=True))
            dn_ref[...] = dn.astype(BF16)
            do_ref[...] = (dn * gate).astype(BF16)
            part_ref[0:1, :] += jnp.sum(e * nh, axis=0, keepdims=True) * (1.0 / d)
            part_ref[1:2, :] += jnp.sum(dn * out, axis=0, keepdims=True)
            part_ref[2:3, :] += jnp.broadcast_to(loss, (1, d))

        @pl.when(i == 0)
        def _():
            part_ref[...] = jnp.zeros_like(part_ref)
            project(0)

        for slot in (0, 1):
            @pl.when((i > 0) & (i < nt) & (i % 2 == slot))
            def _(slot=slot):
                project(slot)
                finish(1 - slot)

        @pl.when(i == nt)
        def _():
            finish((nt - 1) % 2)

    now = lambda cols: pl.BlockSpec((t, cols), lambda i: (jnp.minimum(i, nt - 1), 0))
    before = lambda cols: pl.BlockSpec((t, cols), lambda i: (jnp.maximum(i - 1, 0), 0))
    full = lambda shape: pl.BlockSpec(shape, lambda i: (0,) * len(shape))
    return _call(
        body, name="mix_forward",
        grid=(nt + 1,),
        out_shape=(jax.ShapeDtypeStruct((l, d), BF16), jax.ShapeDtypeStruct((d, l), BF16),
                   jax.ShapeDtypeStruct((l, d), BF16), jax.ShapeDtypeStruct((8, d), F32)),
        in_specs=[before(d), before(d)] + _p_specs(t, w, nt) + [now(w), now(w),
                  pl.BlockSpec((d, d), lambda i: (0, 0), pipeline_mode=pl.Buffered(1)),
                  full(ov.shape), full(wca.shape), full(perm.shape)],
        out_specs=(before(d), pl.BlockSpec((d, t), lambda i: (0, jnp.minimum(i, nt - 1))), before(d), full((8, d))),
        scratch_shapes=[pltpu.VMEM((2, t, d), F32)],
        compiler_params=_params(("arbitrary",)),
    )(x, tgt, p, p, p, p, p, hf, hr, wo, ov, wca, perm)


def _mix_backward(dout, p, hf, hr, wo, wca, perm, g_wout, l, t):
    d = dout.shape[1]
    w = d // 2
    nt = l // t
    la = p.shape[0]

    def body(do_ref, b_ref, c_ref, u_ref, g_ref, q_ref, hf_ref, hr_ref, wo_ref, wca_ref, perm_ref, gw_ref,
             dp_ref, dh_ref, part_ref, sc_ref, send_sems, recv_sems):
        i = pl.program_id(0)
        copies = _scatter_copies(gw_ref, sc_ref, send_sems, recv_sems)

        @pl.when(i == 0)
        def _():
            part_ref[...] = jnp.zeros_like(part_ref)
            for cp in copies:
                cp.start()

        @pl.when(i == nt)
        def _():
            dp_ref[...] = jnp.zeros_like(dp_ref)
            _exchange_wait(copies, copies[1:])

        @pl.when(i < nt)
        def _():
            bl, cl, ul, gl, ql, taps, z, sig_g, sig_q, ylru = _mix_gates(
                (b_ref, c_ref, u_ref, g_ref, q_ref), hf_ref, hr_ref, wca_ref, perm_ref, t, w)
            do = do_ref[...]
            dya = _dot_nt(do, wo_ref[0:w, :])
            dyb = _dot_nt(do, wo_ref[w:, :])
            sg = gl * sig_g
            dz = dya * bl * sg
            dz16 = dz.astype(BF16)
            beside = _dot(perm_ref[2:4].reshape(2 * t, t), dz16)
            dt = wca_ref[0:1, :] * beside[t:] + wca_ref[1:2, :] * dz + wca_ref[2:3, :] * beside[:t]
            dp_ref[:, 0:w] = (dya * z * sg).astype(BF16)
            dp_ref[:, w:2 * w] = (dt * ul).astype(BF16)
            dp_ref[:, 2 * w:3 * w] = (dt * cl).astype(BF16)
            dp_ref[:, 3 * w:4 * w] = (dya * bl * z * (sig_g * (1.0 + gl * (1.0 - sig_g)))).astype(BF16)
            dp_ref[:, 4 * w:5 * w] = jnp.zeros((t, w), BF16)
            dp_ref[:, 5 * w:6 * w] = (dyb * ylru * (sig_q * (1.0 + ql * (1.0 - sig_q)))).astype(BF16)
            dh_ref[...] = _dot(perm_ref[0], (dyb * (ql * sig_q)).astype(BF16)).astype(BF16)
            for j in range(3):
                part_ref[j:j + 1, :] += jnp.sum(dz * taps[j], axis=0, keepdims=True)

    clamp = lambda cols: pl.BlockSpec((t, cols), lambda i: (jnp.minimum(i, nt - 1), 0))
    full = lambda shape: pl.BlockSpec(shape, lambda i: (0,) * len(shape))
    return _call(
        body, name="mix_backward",
        grid=(nt + 1,),
        out_shape=(jax.ShapeDtypeStruct((la, 6 * w), BF16), jax.ShapeDtypeStruct((l, w), BF16),
                   jax.ShapeDtypeStruct((8, w), F32), jax.ShapeDtypeStruct(g_wout.shape, g_wout.dtype)),
        in_specs=[clamp(d)] + _p_specs(t, w, nt) + [clamp(w), clamp(w),
                  pl.BlockSpec((d, d), lambda i: (0, 0), pipeline_mode=pl.Buffered(1)), full(wca.shape),
                  full(perm.shape), ANY],
        out_specs=(pl.BlockSpec((t, 6 * w), lambda i: (i, 0)), clamp(w), full((8, w)), ANY),
        scratch_shapes=[pltpu.SemaphoreType.DMA((NDEV,)), pltpu.SemaphoreType.DMA((NDEV,))],
        compiler_params=_params(("arbitrary",)),
    )(dout, p, p, p, p, p, hf, hr, wo, wca, perm, g_wout)


def _lru_backward(direction, xb, dhs, hs, wg, lv, l, t, conv=None):
    la, w = hs.shape
    gc = wg.shape[2]
    ng = w // gc
    nt = l // t
    nblk8 = la // 8
    last = conv is not None
    assert last == (direction == 1)

    if direction == 0:
        tile = lambda i: jnp.where(i == nt, nt, nt - 1 - i)
        halo = lambda i: jnp.where(tile(i) == 0, nblk8 - 1, tile(i) * (t // 8) - 1)
    else:
        tile = lambda i: i
        halo = lambda i: jnp.minimum((i + 1) * (t // 8), nblk8 - 1)

    def body(*refs):
        x_ref, dh_ref, hs_ref, halo_ref, wg_ref, lv_ref = refs[:6]
        if last:
            v_ref, wcb_ref, bm_ref, dxo_ref = refs[6:10]
        out_ref, dwg_ref, part_ref, a_s, dh_s, g_s, carry = refs[-7:]
        i = pl.program_id(0)
        is_ctx = i == nt

        @pl.when(i == 0)
        def _():
            carry[...] = jnp.zeros_like(carry)
            dwg_ref[...] = jnp.zeros_like(dwg_ref)
            part_ref[...] = jnp.zeros_like(part_ref)

        xb = x_ref[...]
        lam = lv_ref[3 * direction + 2:3 * direction + 3, :]
        a, s, rs, tr, ti, sp = _lru_coef(xb, wg_ref, direction, lv_ref[3 * direction:3 * direction + 1, :],
                                         lv_ref[3 * direction + 1:3 * direction + 2, :], lam, gc)
        hs_t = hs_ref[...]
        r8 = _rows((8, w))
        if direction == 0:
            edge = jnp.where(is_ctx, 0.0, halo_ref[7:8, :])
            first = jnp.where(r8 == 0, edge, pltpu.roll(hs_t[t - 8:, :], 1, 0))
            hprev = jnp.concatenate([first, hs_t[:t - 8, :]], axis=0)
        else:
            edge = jnp.where(is_ctx, 0.0, halo_ref[0:1, :])
            final = jnp.where(r8 == 7, edge, pltpu.roll(hs_t[:8, :], 7, 0))
            hprev = jnp.concatenate([hs_t[8:, :], final], axis=0)
        a_s[...] = a
        dh_s[...] = jnp.where(is_ctx, 0.0, dh_ref[...].astype(F32))
        carry[...] = _scan_tile_backward(a_s, dh_s, g_s, carry[...], direction == 0)

        g = g_s[...]
        r = 0.5 * tr + 0.5
        ig = 0.5 * ti + 0.5
        ix = ig * xb
        gs = g * s
        dla = (g * a) * (hprev - ix * (a * rs))
        dxb = gs * ig
        dzr = dla * (r * (1.0 - tr)) * (-LRU_C * sp)
        dzi = gs * ix * (1.0 - ti)
        part_ref[0:1, :] += jnp.sum(dzr, axis=0, keepdims=True)
        part_ref[1:2, :] += jnp.sum(dzi, axis=0, keepdims=True)
        part_ref[2:3, :] += jnp.sum(dla * r, axis=0, keepdims=True) * (LRU_C * _sigmoid(-lam))
        pieces = []
        for gi in range(ng):
            sl = slice(gi * gc, (gi + 1) * gc)
            dz = jnp.concatenate([dzr[:, sl], dzi[:, sl]], axis=-1).astype(BF16)
            pieces.append(_dot_nt(dz, wg_ref[direction, gi]))
            dwg_ref[gi] += _dot(xb[:, sl].T.astype(BF16), dz)
        dxb = dxb + (pieces[0] if ng == 1 else jnp.concatenate(pieces, axis=-1))
        if not last:
            out_ref[...] = dxb
        else:
            dxb = dxb + dxo_ref[...]
            v = v_ref[...].astype(F32)
            backs = _dot(bm_ref[...].reshape(4 * t, t), dxb.astype(BF16))
            dv = jnp.zeros((t, w), F32)
            for j in range(4):
                back = backs[j * t:(j + 1) * t]
                dv = dv + wcb_ref[j:j + 1, :] * back
                part_ref[4 + j:5 + j, :] += jnp.sum(back * v, axis=0, keepdims=True)
            out_ref[...] = dv.astype(BF16)
            part_ref[3:4, :] += jnp.sum(dxb, axis=0, keepdims=True)

    full = lambda shape: pl.BlockSpec(shape, lambda i: (0,) * len(shape))
    kind = lambda i: (jnp.where(i == nt, 1, 0), 0, 0, 0)
    in_specs = [pl.BlockSpec((t, w), lambda i: (tile(i), 0)),
                pl.BlockSpec((t, w), lambda i: (jnp.minimum(tile(i), nt - 1), 0)),
                pl.BlockSpec((t, w), lambda i: (tile(i), 0)),
                pl.BlockSpec((8, w), lambda i: (halo(i), 0)),
                full(wg.shape), full(lv.shape)]
    args = [xb, dhs, hs, hs, wg, lv]
    if last:
        p, wcb, back_m, dxb_other, dp = conv
        in_specs += [pl.BlockSpec((t, w), lambda i: (tile(i), 4)), full(wcb.shape),
                     pl.BlockSpec((None, 4, t, t), kind), pl.BlockSpec((t, w), lambda i: (tile(i), 0)), ANY]
        args += [p, wcb, back_m, dxb_other, dp]
        out0 = jax.ShapeDtypeStruct(dp.shape, dp.dtype)
        spec0 = pl.BlockSpec((t, w), lambda i: (tile(i), 4))
        aliases = {10: 0}
    else:
        out0 = jax.ShapeDtypeStruct((la, w), F32)
        spec0 = pl.BlockSpec((t, w), lambda i: (tile(i), 0))
        aliases = {}
    return _call(
        body, name="lru_backward_%d" % direction,
        grid=(nt + 1,),
        out_shape=(out0, jax.ShapeDtypeStruct((ng, gc, 2 * gc), F32), jax.ShapeDtypeStruct((8, w), F32)),
        in_specs=in_specs,
        out_specs=(spec0, full((ng, gc, 2 * gc)), full((8, w))),
        scratch_shapes=[pltpu.VMEM((t, w), F32), pltpu.VMEM((t, w), F32), pltpu.VMEM((t, w), F32),
                        pltpu.VMEM((8, w), F32)],
        input_output_aliases=aliases,
        compiler_params=_params(("arbitrary",)),
    )(*args)


def _weight_grad_t(at, b, nblk_m, nblk_n, tk, name):
    m, k = at.shape
    n = b.shape[1]
    bm, bn = m // nblk_m, n // nblk_n
    nk = k // tk

    def body(a_ref, b_ref, o_ref, acc):
        kk = pl.program_id(2)

        @pl.when(kk == 0)
        def _():
            acc[...] = jnp.zeros_like(acc)

        acc[...] += _dot(a_ref[...], b_ref[...])

        @pl.when(kk == nk - 1)
        def _():
            o_ref[...] = acc[...].astype(BF16)

    return _call(
        body, name=name,
        grid=(nblk_m, nblk_n, nk),
        out_shape=jax.ShapeDtypeStruct((nblk_m * nblk_n, bm, bn), BF16),
        in_specs=[pl.BlockSpec((bm, tk), lambda i, j, kk: (i, kk)),
                  pl.BlockSpec((tk, bn), lambda i, j, kk: (kk, j))],
        out_specs=pl.BlockSpec((None, bm, bn), lambda i, j, kk: (i * nblk_n + j, 0, 0)),
        scratch_shapes=[pltpu.VMEM((bm, bn), F32)],
        compiler_params=_params(("arbitrary", "arbitrary", "arbitrary")),
    )(at, b)


def _weight_grad_scatter(at, b, tk, name):
    m, k = at.shape
    n = b.shape[1]
    bn = n // NDEV
    nk = k // tk
    where = jnp.stack([_idx(_my_pos()), lax.axis_index("c")]).astype(jnp.int32)

    def body(w_ref, a_ref, b_ref, recv_ref, acc, sbuf, sib, sib_send, sib_recv, chip_send, chip_recv, keep_sem):
        s, kk = pl.program_id(0), pl.program_id(1)
        x, y, c = _my_pos()

        @pl.when(kk == 0)
        def _():
            acc[...] = _dot(a_ref[...], b_ref[...])

        @pl.when(kk > 0)
        def _():
            acc[...] += _dot(a_ref[...], b_ref[...])

        def to_sibling(j):
            return pltpu.make_async_remote_copy(
                src_ref=sbuf.at[0], dst_ref=sib.at[j], send_sem=sib_send.at[j], recv_sem=sib_recv.at[j],
                device_id=(x, y, 1 - c), device_id_type=MESH)

        def to_chip(j):
            dist = _chip_order(j, c)
            return pltpu.make_async_remote_copy(
                src_ref=sbuf.at[1], dst_ref=recv_ref.at[dist // 2], send_sem=chip_send.at[j],
                recv_sem=chip_recv.at[dist // 2], device_id=_peer_at(dist), device_id_type=MESH)

        keep = pltpu.make_async_copy(sbuf.at[1], recv_ref.at[0], keep_sem)
        sends = []
        for j in range(4):
            sends += [to_sibling(j), to_chip(j) if j < 3 else keep]

        for st in range(NDEV):
            @pl.when((kk == nk - 1) & (s == st))
            def _(st=st):
                if st >= 2:
                    sends[st - 2].wait_send()
                part = acc[...]
                if st % 2 == 1:
                    to_sibling(st // 2).wait_recv()
                    part = part + sib[st // 2].astype(F32)
                sbuf[st % 2] = part.astype(BF16)
                sends[st].start()
                if st == NDEV - 1:
                    sends[st - 1].wait_send()
                    sends[st].wait()
                    for j in range(1, 4):
                        pltpu.make_async_remote_copy(
                            src_ref=sbuf.at[0], dst_ref=recv_ref.at[j], send_sem=chip_send.at[0],
                            recv_sem=chip_recv.at[j], device_id=_peer_at(2 * j), device_id_type=MESH).wait_recv()

    blk = lambda s, w_ref: w_ref[0] ^ _scatter_order(s, w_ref[1])
    return _call(
        body, name=name,
        grid_spec=pltpu.PrefetchScalarGridSpec(
            num_scalar_prefetch=1, grid=(NDEV, nk),
            in_specs=[pl.BlockSpec((m, tk), lambda s, kk, w_ref: (0, kk)),
                      pl.BlockSpec((tk, bn), lambda s, kk, w_ref: (kk, blk(s, w_ref)))],
            out_specs=ANY,
            scratch_shapes=[pltpu.VMEM((m, bn), F32), pltpu.VMEM((2, m, bn), BF16), pltpu.VMEM((4, m, bn), BF16),
                            pltpu.SemaphoreType.DMA((4,)), pltpu.SemaphoreType.DMA((4,)),
                            pltpu.SemaphoreType.DMA((4,)), pltpu.SemaphoreType.DMA((4,)),
                            pltpu.SemaphoreType.DMA]),
        out_shape=jax.ShapeDtypeStruct((4, m, bn), BF16),
        compiler_params=_params(("arbitrary", "arbitrary")),
    )(where, at, b)


def _input_backward(dp, w_all, src, mv, row0, tm, nbk, name, dn=None):
    rows, d = src.shape
    nb, _, bw = w_all.shape
    nk = nb // nbk
    ni = rows // tm
    blk0 = row0 // tm
    latent = dn is not None

    def body(*refs):
        dp_ref, w_ref, x_ref, mv_ref = refs[:4]
        outs = refs[4 + latent:]
        part_ref, acc = outs[latent], outs[latent + 1]
        i, k = pl.program_id(0), pl.program_id(1)

        def product():
            step = _dot_nt(dp_ref[:, 0:bw], w_ref[0])
            for q in range(1, nbk):
                step = step + _dot_nt(dp_ref[:, q * bw:(q + 1) * bw], w_ref[q])
            return step

        def finish(slot):
            xf = x_ref[...]
            r = lax.rsqrt(jnp.mean(xf * xf, axis=-1, keepdims=True) + EPS)
            xn = xf * r
            dhl = acc[slot]
            gain, sc = mv_ref[0:1, :], mv_ref[1:2, :]
            dhx = jnp.sum(dhl * xn, axis=0, keepdims=True)
            part_ref[0:1, :] += jnp.sum(dhl, axis=0, keepdims=True)
            part_ref[1:2, :] += dhx * gain
            part_ref[2:3, :] += dhx * (1.0 + sc)
            if latent:
                dxn = dhl * (gain * (1.0 + sc))
                outs[0][...] = (refs[4][...].astype(F32)
                                + r * (dxn - xn * jnp.mean(dxn * xn, axis=-1, keepdims=True)))

        @pl.when((i == 0) & (k == 0))
        def _():
            part_ref[...] = jnp.zeros_like(part_ref)
            acc[0] = product()

        for slot in (0, 1):
            @pl.when((i > 0) & (i < ni) & (k == 0) & (i % 2 == slot))
            def _(slot=slot):
                acc[slot] = product()
                finish(1 - slot)

            @pl.when((i < ni) & (k > 0) & (i % 2 == slot))
            def _(slot=slot):
                acc[slot] += product()

        @pl.when((i == ni) & (k == 0))
        def _():
            finish((ni - 1) % 2)

    tile = pl.BlockSpec((tm, d), lambda i, k: (jnp.maximum(i - 1, 0), 0))
    vec = pl.BlockSpec((8, d), lambda i, k: (0, 0))
    kblock = lambda i, k: jnp.where(i == ni, nk - 1, k)
    return _call(
        body, name=name,
        grid=(ni + 1, nk),
        out_shape=((jax.ShapeDtypeStruct((rows, d), F32),) if latent else ()) + (jax.ShapeDtypeStruct((8, d), F32),),
        in_specs=[pl.BlockSpec((tm, nbk * bw), lambda i, k: (blk0 + jnp.minimum(i, ni - 1), kblock(i, k))),
                  pl.BlockSpec((nbk, d, bw), lambda i, k: (kblock(i, k), 0, 0)), tile, vec]
                 + ([tile] if latent else []),
        out_specs=((tile,) if latent else ()) + (vec,),
        scratch_shapes=[pltpu.VMEM((2, tm, d), F32)],
        compiler_params=_params(("arbitrary", "arbitrary")),
    )(*([dp, w_all, src, mv] + ([dn] if latent else [])))


def _adamw_scattered(parts, w, m, v, tr):
    r, c = w.shape
    nslot = parts.shape[0]

    def body(p_ref, w_ref, m_ref, v_ref, g_ref, d_ref, m2_ref, v2_ref):
        g = p_ref[0].astype(F32)
        for k in range(1, nslot):
            g = g + p_ref[k].astype(F32)
        g_ref[...] = g
        d_ref[...], m2_ref[...], v2_ref[...] = _adamw(w_ref[...], g, m_ref[...], v_ref[...])

    tile = pl.BlockSpec((tr, c), lambda i: (i, 0))
    return _call(
        body, name="adamw_scattered_%dx%d" % (r, c),
        grid=(r // tr,),
        out_shape=tuple(jax.ShapeDtypeStruct((r, c), F32) for _ in range(4)),
        in_specs=[pl.BlockSpec((nslot, tr, c), lambda i: (0, i, 0)), tile, tile, tile],
        out_specs=(tile,) * 4,
        compiler_params=_params(("arbitrary",)),
    )(parts, w, m, v)


def _adamw_ada(st, dmod, w, m, v, tr):
    r, c = w.shape

    def body(s_ref, dm_ref, w_ref, m_ref, v_ref, g_ref, d_ref, m2_ref, v2_ref):
        g = jnp.dot(s_ref[...], dm_ref[...], precision=HIGHEST, preferred_element_type=F32)
        g_ref[...] = g
        d_ref[...], m2_ref[...], v2_ref[...] = _adamw(w_ref[...], g, m_ref[...], v_ref[...])

    tile = pl.BlockSpec((tr, c), lambda i: (i, 0))
    return _call(
        body, name="adamw_ada",
        grid=(r // tr,),
        out_shape=tuple(jax.ShapeDtypeStruct((r, c), F32) for _ in range(4)),
        in_specs=[pl.BlockSpec((tr, 16), lambda i: (i, 0)), pl.BlockSpec((16, c), lambda i: (0, 0)),
                  tile, tile, tile],
        out_specs=(tile,) * 4,
        compiler_params=_params(("arbitrary",)),
    )(st, dmod, w, m, v)


def _adamw_small(gs, ws, ms, vs):
    n = len(ws)

    def body(*refs):
        for j in range(n):
            g_ref, w_ref, m_ref, v_ref = refs[j], refs[n + j], refs[2 * n + j], refs[3 * n + j]
            d_ref, m2_ref, v2_ref = refs[4 * n + j], refs[5 * n + j], refs[6 * n + j]
            d_ref[...], m2_ref[...], v2_ref[...] = _adamw(w_ref[...], g_ref[...], m_ref[...], v_ref[...])

    shapes = tuple(jax.ShapeDtypeStruct(a.shape, F32) for a in ws)
    out = _call(
        body, name="adamw_small",
        out_shape=shapes * 3,
        in_specs=[VMEM] * (4 * n), out_specs=(VMEM,) * (3 * n),
        compiler_params=_params(),
    )(*gs, *ws, *ms, *vs)
    return list(out[:n]), list(out[n:2 * n]), list(out[2 * n:])


def _blockdiag_groups(wh, gc):
    h, dh, _ = wh.shape
    g = gc // dh
    w4 = wh.reshape(h // g, g, dh, dh)
    bd = jnp.einsum("ngij,gh->ngihj", w4, jnp.eye(g, dtype=wh.dtype))
    return bd.reshape(h // g, gc, gc)


def _blockdiag_extract(bd, dh):
    ng, gc, _ = bd.shape
    g = gc // dh
    x = bd.reshape(ng, g, dh, g, dh)
    return jnp.einsum("ngihj,gh->ngij", x, jnp.eye(g, dtype=bd.dtype)).reshape(ng * g, dh, dh)


def _largest_tile(n, cap):
    return max(q for q in range(128, min(n, cap) + 1, 128) if n % q == 0)


def _rows8(*vecs):
    rows = [jnp.reshape(v, (1, -1)).astype(F32) for v in vecs]
    n = rows[0].shape[1]
    return jnp.concatenate(rows + [jnp.zeros((8 - len(rows), n), F32)], axis=0)


def _pack(pieces):
    flat = jnp.concatenate([jnp.reshape(a, (-1,)).astype(F32) for a in pieces])
    total = -(-flat.shape[0] // 1024) * 1024
    return jnp.pad(flat, (0, total - flat.shape[0])).reshape(total // 128, 128)


def _unpack(packed, shapes):
    flat = packed.reshape(-1)
    out, off = [], 0
    for s in shapes:
        n = 1
        for q in s:
            n *= q
        out.append(flat[off:off + n].reshape(s))
        off += n
    return out


def kernel(x, c, ctx, c_ctx, norm_g, w_ada, b_ada, w_in, w_conv_a, w_conv_b, b_conv_b, lru_wa, lru_ba, lru_wx, lru_bx, lru_lambda, w_out, final_g, loss_target, m_c_ctx, m_norm_g, m_w_ada, m_b_ada, m_w_in, m_w_conv_a, m_w_conv_b, m_b_conv_b, m_lru_wa, m_lru_ba, m_lru_wx, m_lru_bx, m_lru_lambda, m_w_out, m_final_g, v_c_ctx, v_norm_g, v_w_ada, v_b_ada, v_w_in, v_w_conv_a, v_w_conv_b, v_b_conv_b, v_lru_wa, v_lru_ba, v_lru_wx, v_lru_bx, v_lru_lambda, v_w_out, v_final_g):
    _, l, d = x.shape
    lc = ctx.shape[1]
    w = d // 2
    t = lc
    assert l % t == 0 and t % GRID_W == 0 and t % 128 == 0
    dh = w // N_HEADS
    gc = min(w, MXU_WIDTH)
    cols = w_ada.shape[2]
    wo_rows = w_out.shape[1]
    me = _idx(_my_pos())
    x2, ctx2, tgt2 = x[0], ctx[0], loss_target[0]
    w_ada2, w_in2, w_out2 = w_ada[0], w_in[0], w_out[0]

    small_mine = jnp.concatenate([a.reshape(-1) for a in (w_conv_a, w_conv_b, lru_ba, lru_bx, lru_lambda)]
                                 + [jnp.zeros((3 * (w // NDEV),), F32)]).reshape(16, w // NDEV)
    mod_all, s_mat, small_all = _mod_forward(
        jnp.broadcast_to(c, (8, d)), jnp.broadcast_to(c_ctx[None], (8, d)), w_ada2, small_mine)
    mod = jnp.transpose(mod_all, (1, 0, 2)).reshape(16, NDEV * cols) + b_ada
    mod_lat = lax.dynamic_slice_in_dim(mod, me, 1, axis=0)
    sh_l, sc_l, gt_l = jnp.split(mod_lat, 3, axis=-1)
    sh_c, sc_c, _ = jnp.split(mod[8:9], 3, axis=-1)
    small = jnp.transpose(small_all, (1, 0, 2)).reshape(16, w)
    wca = _rows8(*[small[j] for j in range(0, 3)])
    wcb = _rows8(*[small[j] for j in range(3, 7)], b_conv_b)
    lv = _rows8(0.5 * small[7], 0.5 * small[9], small[11], 0.5 * small[8], 0.5 * small[10], small[12])
    wg = jnp.stack([
        jnp.concatenate([_blockdiag_groups(lru_wa[0, dr], gc), _blockdiag_groups(lru_wx[0, dr], gc)], axis=-1)
        for dr in range(2)])
    wg = (0.5 * wg).astype(BF16)

    la = l + lc
    tm = 2 * t if l % (2 * t) == 0 else t
    tk = 3 * t if la % (3 * t) == 0 else t
    h, hlt = _normalize(x2, _rows8(norm_g, sc_l, sh_l), la, 0, tm, "normalize")
    h, hlt = _normalize(ctx2, _rows8(norm_g, sc_c, sh_c), la, l, t, "normalize_ctx", prev=(h, hlt))
    p, w_all = _in_projection(h, w_in2.astype(BF16), la // 8 if la % 128 == 0 else tk)
    taps_m, back_m, perm = _scan_matrices(t)
    xb = _conv_input(p, wcb, taps_m, l, t)
    hf, hr, wo_all = _lru_forward(xb, wg, lv, w_out2.astype(BF16), l, t)
    wo = wo_all.reshape(d, d)
    dn, catt, dout, part_mix = _mix_forward(x2, tgt2, p, hf, hr, wo, _rows8(gt_l, final_g), wca, perm, t)
    g_wout = _weight_grad_t(catt, dout, 2, 1, _largest_tile(l, 2048), "grad_w_out")
    dp, dhs, part_ca, sc_wout = _mix_backward(dout, p, hf, hr, wo, wca, perm, g_wout.reshape(NDEV, wo_rows, d), l, t)
    dxb0, dwg0, part_l0 = _lru_backward(0, xb, dhs, hf, wg, lv, l, t)
    dp, dwg1, part_l1 = _lru_backward(1, xb, dhs, hr, wg, lv, l, t, conv=(p, wcb, back_m, dxb0, dp))
    sc_win = _weight_grad_scatter(hlt, dp, _largest_tile(la, 1408), "grad_w_in")
    grad_x, part_lat = _input_backward(dp, w_all, x2, _rows8(norm_g, sc_l), 0, tm, 2, "input_backward", dn=dn)
    (part_ctx,) = _input_backward(dp, w_all, ctx2, _rows8(norm_g, sc_c), l, t, 2, "input_backward_ctx")
    part_in = jnp.concatenate([part_lat[0:2], part_ctx[0:2], (part_lat[2] + part_ctx[2])[None]], axis=0)

    dwa = jnp.stack([_blockdiag_extract(dwg0[:, :, :gc], dh), _blockdiag_extract(dwg1[:, :, :gc], dh)])
    dwx = jnp.stack([_blockdiag_extract(dwg0[:, :, gc:], dh), _blockdiag_extract(dwg1[:, :, gc:], dh)])
    lru_part = (0.5 * jnp.stack([dwa, dwx])).reshape(NDEV, -1, 128)
    zeros_d = jnp.zeros((d,), F32)
    pieces = [
        jnp.concatenate([part_in[0], part_in[1], part_mix[1]]),
        jnp.concatenate([part_in[2], part_in[3], zeros_d]),
        part_in[4], part_mix[0], part_ca[0:3], part_l1[4:8], part_l1[3],
        0.5 * jnp.stack([part_l0[0], part_l1[0]]), 0.5 * jnp.stack([part_l0[1], part_l1[1]]),
        jnp.stack([part_l0[2], part_l1[2]]), part_mix[2, 0:1],
    ]
    shapes = [(3 * d,), (3 * d,), (d,), (d,), (3, w), (4, w), (w,), (2, w), (2, w), (2, w), (1,)]
    sig_cc = jax.nn.sigmoid(c_ctx)
    dsilu_cc = jnp.broadcast_to((sig_cc * (1.0 + c_ctx * (1.0 - sig_cc)))[None], (8, d))
    psum, pall, lru_sum, g_cctx8 = _reduce_small(_pack(pieces), lru_part, w_ada2, dsilu_cc)
    (g_modl, g_modc, g_norm, g_final, g_ca, g_cb, g_bcb, g_ba, g_bx, g_lam, loss1) = _unpack(psum, shapes)
    loss = loss1[0]
    g_cctx = g_cctx8[0]
    g_bada = (g_modl + g_modc)[None]
    g_lru = lru_sum.reshape(2, 2, N_HEADS, dh, dh)
    g_wa, g_wx = g_lru[0][None], g_lru[1][None]
    wsl = w // NDEV
    mine = lambda a: lax.dynamic_slice_in_dim(a, me * wsl, wsl, axis=-1)
    g_ca_m, g_cb_m, g_ba_m, g_bx_m, g_lam_m = (mine(g_ca)[None], mine(g_cb)[None], mine(g_ba)[None],
                                               mine(g_bx)[None], mine(g_lam)[None])
    g_norm, g_bcb = g_norm[None], g_bcb[None]

    cb = cols // 128
    per_dev = pall[:, :3 * d // 128].reshape(NDEV, NDEV, cols)
    dmod_lat = lax.dynamic_slice_in_dim(per_dev, me, 1, axis=1)[:, 0]
    dmod_ctx = lax.dynamic_slice_in_dim(g_modc.reshape(NDEV, cols), me, 1, axis=0)
    dmod16 = jnp.concatenate([dmod_lat, dmod_ctx, jnp.zeros((7, cols), F32)], axis=0)
    tr_ada = 256 if d % 256 == 0 else d
    g_wada, d_wada, m_wada, v_wada = _adamw_ada(s_mat.T, dmod16, w_ada2, m_w_ada[0], v_w_ada[0], tr_ada)
    g_win2, d_win, m_win, v_win = _adamw_scattered(sc_win, w_in2, m_w_in[0], v_w_in[0], tr_ada)
    tr_out = 64 if wo_rows % 64 == 0 else wo_rows
    g_wout2, d_wout, m_wout, v_wout = _adamw_scattered(sc_wout, w_out2, m_w_out[0], v_w_out[0], tr_out)

    small_w = [c_ctx, norm_g, b_ada, w_conv_a, w_conv_b, b_conv_b, lru_wa, lru_ba, lru_wx, lru_bx, lru_lambda, final_g]
    small_m = [m_c_ctx, m_norm_g, m_b_ada, m_w_conv_a, m_w_conv_b, m_b_conv_b, m_lru_wa, m_lru_ba, m_lru_wx,
               m_lru_bx, m_lru_lambda, m_final_g]
    small_v = [v_c_ctx, v_norm_g, v_b_ada, v_w_conv_a, v_w_conv_b, v_b_conv_b, v_lru_wa, v_lru_ba, v_lru_wx,
               v_lru_bx, v_lru_lambda, v_final_g]
    small_g = [g_cctx, g_norm, g_bada, g_ca_m, g_cb_m, g_bcb, g_wa, g_ba_m, g_wx, g_bx_m, g_lam_m, g_final]
    small_g = [jnp.reshape(a, b.shape) for a, b in zip(small_g, small_w)]
    d_s, m_s, v_s = _adamw_small(small_g, small_w, small_m, small_v)

    def weights(small_list, ada, win, wout):
        (cctx_, norm_, bada_, ca_, cb_, bcb_, wa_, ba_, wx_, bx_, lam_, final_) = small_list
        return [cctx_, norm_, ada[None], bada_, win[None], ca_, cb_, bcb_, wa_, ba_, wx_, bx_, lam_, wout[None], final_]

    return (loss, grad_x[None],
            *weights(small_g, g_wada, g_win2, g_wout2), *weights(d_s, d_wada, d_win, d_wout),
            *weights(m_s, m_wada, m_win, m_wout), *weights(v_s, v_wada, v_win, v_wout))
```

```python
import functools

import jax
import jax.numpy as jnp
import numpy as np
from jax import lax
from jax.experimental import pallas as pl
from jax.experimental.pallas import tpu as pltpu

F32 = jnp.float32
BF16 = jnp.bfloat16
MESH = pl.DeviceIdType.MESH
NDEV = 8
GRID_W = 64
N_HEADS = 16
LRU_C = 8.0
EPS = 1e-6
MXU_WIDTH = 256
VMEM_LIMIT = 60 * 1024 * 1024

ADAM_LR = 0.001
ADAM_B1 = 0.9
ADAM_B2 = 0.999
ADAM_EPS = 1e-08
ADAM_WD = 0.01
ADAM_STEP = 10
ADAM_C1 = 1.0 - ADAM_B1 ** ADAM_STEP
ADAM_C2 = 1.0 - ADAM_B2 ** ADAM_STEP

HIGHEST = lax.Precision.HIGHEST
ANY = pl.BlockSpec(memory_space=pl.ANY)
VMEM = pl.BlockSpec(memory_space=pltpu.VMEM)


def _call(body, **kw):
    return pl.pallas_call(body, **kw)


def _params(sem=None, vmem=VMEM_LIMIT):
    return pltpu.CompilerParams(dimension_semantics=sem, vmem_limit_bytes=vmem)


def _my_pos():
    return lax.axis_index("x"), lax.axis_index("y"), lax.axis_index("c")


def _idx(pos):
    return 4 * pos[0] + 2 * pos[1] + pos[2]


def _peer(k):
    x, y, c = _my_pos()
    return ((1 - x) if (k >> 2) & 1 else x, (1 - y) if (k >> 1) & 1 else y, (1 - c) if k & 1 else c)


def _exchange_start(src_ref, dst_ref, send_sems, recv_sems, base):
    me = _idx(_my_pos())
    sends = []
    for k in range(1, NDEV):
        cp = pltpu.make_async_remote_copy(
            src_ref=src_ref, dst_ref=dst_ref.at[me], send_sem=send_sems.at[base + k - 1],
            recv_sem=recv_sems.at[base + k - 1], device_id=_peer(k), device_id_type=MESH)
        cp.start()
        sends.append(cp)
    dst_ref[me] = src_ref[...]
    return sends, (src_ref, dst_ref, send_sems, recv_sems, base)


def _exchange_finish(started):
    sends, (src_ref, dst_ref, send_sems, recv_sems, base) = started
    for k in range(1, NDEV):
        peer = _peer(k)
        pltpu.make_async_remote_copy(
            src_ref=src_ref, dst_ref=dst_ref.at[_idx(peer)], send_sem=send_sems.at[base + k - 1],
            recv_sem=recv_sems.at[base + k - 1], device_id=peer, device_id_type=MESH).wait_recv()
    for cp in sends:
        cp.wait_send()


def _exchange_vmem(src_ref, dst_ref, send_sems, recv_sems, base):
    _exchange_finish(_exchange_start(src_ref, dst_ref, send_sems, recv_sems, base))


def _sigmoid(z):
    return 0.5 * jnp.tanh(0.5 * z) + 0.5


def _softplus(x):
    return jnp.maximum(x, 0.0) + jnp.log1p(jnp.exp(-jnp.abs(x)))


def _one_minus_sq(a, la):
    series = (-2.0 * la) * (1.0 + la)
    return jnp.where(la > -0.0015, series, 1.0 - a * a)


def _dot(a, b):
    return jnp.dot(a, b, preferred_element_type=F32)


def _dot_nt(a, b):
    return lax.dot_general(a, b, (((1,), (1,)), ((), ())), preferred_element_type=F32)


def _rows(shape):
    return lax.broadcasted_iota(jnp.int32, shape, 0)


def _scan_matrices(t):
    seg = t // 8
    r = np.arange(t)
    perm = (np.arange(t)[None, :] == ((r % 8) * seg + r // 8)[:, None]).astype(np.float32)
    rows, cols = r[:, None], r[None, :]
    taps, back = [], []
    for rowlen in (GRID_W, t):
        pos = rows % rowlen
        shift = {-2: (cols == rows - 2) & (pos >= 2), -1: (cols == rows - 1) & (pos >= 1),
                 0: cols == rows, 1: (cols == rows + 1) & (pos + 1 < rowlen),
                 2: (cols == rows + 2) & (pos + 2 < rowlen)}
        if rowlen == GRID_W:
            beside = [shift[-1].astype(np.float32), shift[1].astype(np.float32)]
        taps.append(np.stack([perm @ shift[k].astype(np.float32) for k in (-2, -1, 0, 1)]))
        back.append(np.stack([shift[k].astype(np.float32) @ perm.T for k in (2, 1, 0, -1)]))
    as_bf16 = lambda a: jnp.asarray(a, dtype=BF16)
    return as_bf16(np.stack(taps)), as_bf16(np.stack(back)), as_bf16(np.stack([perm, perm.T] + beside))


def _chunk_scan(a, b, reverse):
    row = _rows(a.shape)
    for s in (1, 2, 4):
        if reverse:
            m = row < 8 - s
            sh = 8 - s
        else:
            m = row >= s
            sh = s
        a_s = jnp.where(m, pltpu.roll(a, sh, 0), 1.0)
        b_s = jnp.where(m, pltpu.roll(b, sh, 0), 0.0)
        b = b + a * b_s
        a = a * a_s
    return a, b


def _chain_segments(ptot, hend, carry, reverse):
    ca, cb = _chunk_scan(ptot, hend, reverse)
    incl = ca * carry + cb
    r8 = _rows(incl.shape)
    if reverse:
        start = jnp.where(r8 < 7, pltpu.roll(incl, 7, 0), carry)
        last = incl[0:1, :]
    else:
        start = jnp.where(r8 >= 1, pltpu.roll(incl, 1, 0), carry)
        last = incl[7:8, :]
    return start, jnp.broadcast_to(last, incl.shape)


def _blocks(nblock, reverse):
    order = range(nblock - 1, -1, -1) if reverse else range(nblock)
    return [slice(8 * k, 8 * k + 8) for k in order]


def _scan_tile(a_ref, b_ref, out_ref, carry, reverse):
    t, w = a_ref.shape
    seg = t // 8

    hend, ptot = jnp.zeros((8, w), F32), jnp.ones((8, w), F32)
    for rows in _blocks(seg, reverse):
        a = a_ref[rows, :]
        hend, ptot = a * hend + b_ref[rows, :], a * ptot
    h, new_carry = _chain_segments(ptot, hend, carry, reverse)
    for rows in _blocks(seg, reverse):
        h = a_ref[rows, :] * h + b_ref[rows, :]
        out_ref[rows, :] = h
    return new_carry


def _scan_tile_backward(a_ref, dh_ref, g_ref, carry, reverse):
    t, w = a_ref.shape
    seg = t // 8

    uend, ptot = jnp.zeros((8, w), F32), jnp.ones((8, w), F32)
    for rows in _blocks(seg, reverse):
        a = a_ref[rows, :]
        uend, ptot = a * (dh_ref[rows, :] + uend), a * ptot
    u, new_carry = _chain_segments(ptot, uend, carry, reverse)
    for rows in _blocks(seg, reverse):
        g = dh_ref[rows, :] + u
        g_ref[rows, :] = g
        u = a_ref[rows, :] * g
    return new_carry


def _lru_coef(xb, wg_ref, d, ba, bx, lam, gc):
    w = xb.shape[1]
    xb16 = xb.astype(BF16)
    zr, zi = [], []
    for g in range(w // gc):
        z = _dot(xb16[:, g * gc:(g + 1) * gc], wg_ref[d, g])
        zr.append(z[:, :gc])
        zi.append(z[:, gc:])
    zr = zr[0] if len(zr) == 1 else jnp.concatenate(zr, axis=-1)
    zi = zi[0] if len(zi) == 1 else jnp.concatenate(zi, axis=-1)
    tr = jnp.tanh(zr + ba)
    ti = jnp.tanh(zi + bx)
    sp = _softplus(-lam)
    half = -0.5 * LRU_C * sp
    la = tr * half + half
    a = jnp.exp(la)
    q = _one_minus_sq(a, la)
    rs = lax.rsqrt(jnp.maximum(q, 1e-30))
    return a, q * rs, rs, tr, ti, sp


def _adamw(w, g, m, v):
    m2 = ADAM_B1 * m + (1.0 - ADAM_B1) * g
    v2 = ADAM_B2 * v + (1.0 - ADAM_B2) * (g * g)
    m_hat = m2 / ADAM_C1
    v_hat = v2 / ADAM_C2
    delta = -ADAM_LR * (m_hat / (jnp.sqrt(v_hat) + ADAM_EPS) + ADAM_WD * w)
    return delta, m2, v2


def _mod_forward(c8, cctx8, w_ada, small):
    d = c8.shape[1]
    cols = w_ada.shape[1]

    def body(c_ref, cctx_ref, w_ref, sm_ref, mod_ref, s_ref, sm_all, cbuf, mod_my, send_sems, recv_sems):
        _exchange_vmem(sm_ref, sm_all, send_sems, recv_sems, 2 * (NDEV - 1))
        _exchange_vmem(c_ref, cbuf, send_sems, recv_sems, 0)
        row = _rows((8, d))
        c_all = jnp.zeros((8, d), F32)
        for b in range(NDEV):
            c_all = jnp.where(row == b, cbuf[b], c_all)
        cc = cctx_ref[...]
        s_top = c_all * _sigmoid(c_all)
        s_bot = jnp.where(row == 0, cc * _sigmoid(cc), 0.0)
        s = jnp.concatenate([s_top, s_bot], axis=0)
        s_ref[...] = s
        mod_my[...] = jnp.dot(s, w_ref[...], precision=HIGHEST, preferred_element_type=F32)
        _exchange_vmem(mod_my, mod_ref, send_sems, recv_sems, NDEV - 1)

    return _call(
        body, name="mod_forward",
        out_shape=(jax.ShapeDtypeStruct((NDEV, 16, cols), F32), jax.ShapeDtypeStruct((16, d), F32),
                   jax.ShapeDtypeStruct((NDEV,) + small.shape, F32)),
        in_specs=[VMEM] * 4, out_specs=(VMEM,) * 3,
        scratch_shapes=[pltpu.VMEM((NDEV, 8, d), F32), pltpu.VMEM((16, cols), F32),
                        pltpu.SemaphoreType.DMA((3 * (NDEV - 1),)), pltpu.SemaphoreType.DMA((3 * (NDEV - 1),))],
        compiler_params=_params(),
    )(c8, cctx8, w_ada, small)


def _scatter_copies(src_ref, dst_ref, send_sems, recv_sems):
    me = _idx(_my_pos())
    copies = [pltpu.make_async_copy(src_ref.at[me], dst_ref.at[0], send_sems.at[0])]
    for k in range(1, NDEV):
        peer = _peer(k)
        copies.append(pltpu.make_async_remote_copy(
            src_ref=src_ref.at[_idx(peer)], dst_ref=dst_ref.at[k], send_sem=send_sems.at[k],
            recv_sem=recv_sems.at[k], device_id=peer, device_id_type=MESH))
    return copies


def _gather_copies(src_ref, dst_ref, send_sems, recv_sems):
    me = _idx(_my_pos())
    sends = [pltpu.make_async_copy(src_ref, dst_ref.at[me], send_sems.at[0])]
    arrivals = []
    for k in range(1, NDEV):
        peer = _peer(k)
        sends.append(pltpu.make_async_remote_copy(
            src_ref=src_ref, dst_ref=dst_ref.at[me], send_sem=send_sems.at[k],
            recv_sem=recv_sems.at[k], device_id=peer, device_id_type=MESH))
        arrivals.append(pltpu.make_async_remote_copy(
            src_ref=src_ref, dst_ref=dst_ref.at[_idx(peer)], send_sem=send_sems.at[k],
            recv_sem=recv_sems.at[k], device_id=peer, device_id_type=MESH))
    return sends, arrivals


def _exchange_wait(sends, arrivals):
    sends[0].wait()
    for cp in arrivals:
        cp.wait_recv()
    for cp in sends[1:]:
        cp.wait_send()


def _chip_order(k, c):
    return (6, 4 - 2 * c, 2 + 2 * c, 0)[k]


def _scatter_order(s, c):
    k = s >> 1
    mine = jnp.where(k == 0, 6, jnp.where(k == 1, 4 - 2 * c, jnp.where(k == 2, 2 + 2 * c, 0)))
    theirs = jnp.where(k == 0, 6, jnp.where(k == 1, 2 + 2 * c, jnp.where(k == 2, 4 - 2 * c, 0))) ^ 1
    return jnp.where((s & 1) == 0, theirs, mine)


def _peer_at(dist):
    x, y, c = _my_pos()
    return (x ^ ((dist >> 2) & 1), y ^ ((dist >> 1) & 1), c ^ (dist & 1))


def _reduce_small(packed, lru_parts, w_ada, dsilu_cctx):
    rp = packed.shape[0]
    rl = lru_parts.shape[1]
    d, cols = w_ada.shape
    assert cols % 128 == 0
    cb = cols // 128

    def body(p_ref, l_ref, w_ref, ds_ref, sum_ref, all_ref, lru_ref, cctx_ref,
             lbuf, lsum, cpart, call, send_sems, recv_sems, lsend, lrecv):
        me = _idx(_my_pos())
        scattered = _scatter_copies(l_ref, lbuf, lsend, lrecv)
        for cp in scattered:
            cp.start()
        _exchange_vmem(p_ref, all_ref, send_sems, recv_sems, 0)
        acc = all_ref[0]
        for j in range(1, NDEV):
            acc = acc + all_ref[j]
        sum_ref[...] = acc
        _exchange_wait(scattered, scattered[1:])
        red = lbuf[0]
        for k in range(1, NDEV):
            red = red + lbuf[k]
        lsum[...] = red
        lru_gather = _exchange_start(lsum, lru_ref, send_sems, recv_sems, NDEV - 1)
        part = jnp.zeros((8, d), F32)
        for q in range(cb):
            dm = jnp.broadcast_to(sum_ref[pl.ds((NDEV + me) * cb + q, 1), :], (8, 128))
            part = part + lax.dot_general(dm, w_ref[:, q * 128:(q + 1) * 128],
                                          (((1,), (1,)), ((), ())), precision=HIGHEST,
                                          preferred_element_type=F32)
        cpart[...] = part
        _exchange_vmem(cpart, call, send_sems, recv_sems, 2 * (NDEV - 1))
        _exchange_finish(lru_gather)
        tot = call[0]
        for j in range(1, NDEV):
            tot = tot + call[j]
        cctx_ref[...] = tot * ds_ref[...]

    return _call(
        body, name="reduce_small",
        out_shape=(jax.ShapeDtypeStruct((rp, 128), F32), jax.ShapeDtypeStruct((NDEV, rp, 128), F32),
                   jax.ShapeDtypeStruct((NDEV, rl, 128), F32), jax.ShapeDtypeStruct((8, d), F32)),
        in_specs=[VMEM] * 4, out_specs=(VMEM,) * 4,
        scratch_shapes=[pltpu.VMEM((NDEV, rl, 128), F32), pltpu.VMEM((rl, 128), F32), pltpu.VMEM((8, d), F32),
                        pltpu.VMEM((NDEV, 8, d), F32),
                        pltpu.SemaphoreType.DMA((3 * (NDEV - 1),)), pltpu.SemaphoreType.DMA((3 * (NDEV - 1),)),
                        pltpu.SemaphoreType.DMA((NDEV,)), pltpu.SemaphoreType.DMA((NDEV,))],
        compiler_params=_params(),
    )(packed, lru_parts, w_ada, dsilu_cctx)


def _normalize(src, mv, la, row0, tm, name, prev=None):
    rows, d = src.shape
    blk0 = row0 // tm

    def body(*refs):
        x_ref, mv_ref = refs[:2]
        h_ref, ht_ref = refs[-2:]
        xf = x_ref[...]
        r = lax.rsqrt(jnp.mean(xf * xf, axis=-1, keepdims=True) + EPS)
        h = xf * r * (mv_ref[0:1, :] * (1.0 + mv_ref[1:2, :])) + mv_ref[2:3, :]
        h_ref[...] = h.astype(BF16)
        ht_ref[...] = h.T.astype(BF16)

    in_specs = [pl.BlockSpec((tm, d), lambda i: (i, 0)), pl.BlockSpec((8, d), lambda i: (0, 0))]
    args = [src, mv]
    aliases = {}
    if prev is not None:
        in_specs += [ANY, ANY]
        args += list(prev)
        aliases = {2: 0, 3: 1}
    return _call(
        body, name=name,
        grid=(rows // tm,),
        out_shape=(jax.ShapeDtypeStruct((la, d), BF16), jax.ShapeDtypeStruct((d, la), BF16)),
        in_specs=in_specs,
        out_specs=(pl.BlockSpec((tm, d), lambda i: (blk0 + i, 0)), pl.BlockSpec((d, tm), lambda i: (0, blk0 + i))),
        input_output_aliases=aliases,
        compiler_params=_params(("arbitrary",)),
    )(*args)


def _gather_order(step):
    return (step & 1) | (((step >> 2) & 1) << 1) | (((step >> 1) & 1) << 2)


def _in_projection(h, w_shard, tm):
    la, d = h.shape
    bw = w_shard.shape[1]
    ni = la // tm
    where = jnp.reshape(_idx(_my_pos()), (1,)).astype(jnp.int32)

    def body(me_ref, h_ref, w_ref, p_ref, all_ref, wbuf, send_sems, recv_sems, local_sems):
        s, i = pl.program_id(0), pl.program_id(1)
        x, y, c = _my_pos()
        me, sibling = (x, y, c), (x, y, 1 - c)
        chips = [(1 - x, y), (x, 1 - y), (1 - x, 1 - y)]

        def copy(k, block, to, from_shard=False):
            return pltpu.make_async_remote_copy(
                src_ref=w_ref if from_shard else all_ref.at[_idx(block)], dst_ref=all_ref.at[_idx(block)],
                send_sem=send_sems.at[k], recv_sem=recv_sems.at[k], device_id=to, device_id_type=MESH)

        def load(block, slot):
            return pltpu.make_async_copy(all_ref.at[_idx(block)], wbuf.at[slot], local_sems.at[1])

        keep = pltpu.make_async_copy(w_ref, all_ref.at[_idx(me)], local_sems.at[0])
        first = [copy(0, me, sibling, True)] + [copy(1 + j, me, (*chip, c), True) for j, chip in enumerate(chips)]
        passed = [copy(4 + j, (*chip, c), sibling) for j, chip in enumerate(chips)]
        steps = [(copy(0, sibling, me), None, sibling)]
        for j, chip in enumerate(chips):
            steps.append((copy(1 + j, (*chip, c), me), passed[j], (*chip, c)))
            steps.append((copy(4 + j, (*chip, 1 - c), me), None, (*chip, 1 - c)))

        @pl.when((s == 0) & (i == 0))
        def _():
            keep.start()
            mine = pltpu.make_async_copy(w_ref, wbuf.at[0], local_sems.at[1])
            mine.start()
            for cp in first:
                cp.start()
            mine.wait()

        for n, (arrival, forward, block) in enumerate(steps, start=1):
            @pl.when((s == n - 1) & (i == ni - 1))
            def _(arrival=arrival, forward=forward, block=block, n=n):
                arrival.wait_recv()
                if forward is not None:
                    forward.start()
                load(block, n % 2).start()

        @pl.when((s > 0) & (i == 0))
        def _():
            load(me, s % 2).wait()

        p_ref[...] = _dot(h_ref[...], wbuf[s % 2]).astype(BF16)

        @pl.when((s == NDEV - 1) & (i == ni - 1))
        def _():
            for cp in first + passed:
                cp.wait_send()
            keep.wait()

    return _call(
        body, name="in_projection",
        grid_spec=pltpu.PrefetchScalarGridSpec(
            num_scalar_prefetch=1, grid=(NDEV, ni),
            in_specs=[pl.BlockSpec((tm, d), lambda s, i, me_ref: (i, 0)), ANY],
            out_specs=(pl.BlockSpec((tm, bw), lambda s, i, me_ref: (i, me_ref[0] ^ _gather_order(s))), ANY),
            scratch_shapes=[pltpu.VMEM((2, d, bw), BF16), pltpu.SemaphoreType.DMA((7,)),
                            pltpu.SemaphoreType.DMA((7,)), pltpu.SemaphoreType.DMA((2,))]),
        out_shape=(jax.ShapeDtypeStruct((la, NDEV * bw), BF16), jax.ShapeDtypeStruct((NDEV, d, bw), BF16)),
        compiler_params=_params(("arbitrary", "arbitrary")),
    )(where, h, w_shard)


def _conv_input(p, wcb, taps_m, l, t):
    la = p.shape[0]
    w = wcb.shape[1]
    nt = l // t

    def body(v_ref, wcb_ref, tm_ref, xb_ref):
        taps = _dot(tm_ref[...].reshape(4 * t, t), v_ref[...])
        xb = wcb_ref[4:5, :] + wcb_ref[0:1, :] * taps[0:t]
        for j in range(1, 4):
            xb = xb + wcb_ref[j:j + 1, :] * taps[j * t:(j + 1) * t]
        xb_ref[...] = xb

    return _call(
        body, name="conv_input",
        grid=(nt + 1,),
        out_shape=jax.ShapeDtypeStruct((la, w), F32),
        in_specs=[pl.BlockSpec((t, w), lambda i: (i, 4)), pl.BlockSpec((8, w), lambda i: (0, 0)),
                  pl.BlockSpec((None, 4, t, t), lambda i: (i // nt, 0, 0, 0))],
        out_specs=pl.BlockSpec((t, w), lambda i: (i, 0)),
        compiler_params=_params(("arbitrary",)),
    )(p, wcb, taps_m)


def _lru_forward(xb, wg, lv, wo_shard, l, t):
    la, w = xb.shape
    gc = wg.shape[2]
    nt = l // t

    def body(xf_ref, xr_ref, wg_ref, lv_ref, wo_ref, hf_ref, hr_ref, wo_all,
             a_s, b_s, carry, send_sems, recv_sems):
        sends, arrivals = _gather_copies(wo_ref, wo_all, send_sems, recv_sems)

        @pl.when(pl.program_id(0) == 0)
        def _():
            carry[...] = jnp.zeros_like(carry)
            for cp in sends:
                cp.start()

        @pl.when(pl.program_id(0) == nt)
        def _():
            _exchange_wait(sends, arrivals)

        for dr, (x_ref, h_ref) in enumerate(((xf_ref, hf_ref), (xr_ref, hr_ref))):
            x = x_ref[...]
            a, s, _, _, ti, _ = _lru_coef(x, wg_ref, dr, lv_ref[3 * dr:3 * dr + 1, :],
                                          lv_ref[3 * dr + 1:3 * dr + 2, :], lv_ref[3 * dr + 2:3 * dr + 3, :], gc)
            a_s[...] = a
            b_s[...] = (s * x) * (0.5 * ti + 0.5)
            carry[dr] = _scan_tile(a_s, b_s, h_ref, carry[dr], dr == 1)

    full = lambda shape: pl.BlockSpec(shape, lambda i: (0,) * len(shape))
    fmap = lambda i: (jnp.where(i == 0, nt, i - 1), 0)
    rmap = lambda i: (jnp.where(i == 0, nt, nt - i), 0)
    return _call(
        body, name="lru_forward",
        grid=(nt + 1,),
        out_shape=(jax.ShapeDtypeStruct((la, w), F32), jax.ShapeDtypeStruct((la, w), F32),
                   jax.ShapeDtypeStruct((NDEV,) + wo_shard.shape, wo_shard.dtype)),
        in_specs=[pl.BlockSpec((t, w), fmap), pl.BlockSpec((t, w), rmap), full(wg.shape), full(lv.shape), ANY],
        out_specs=(pl.BlockSpec((t, w), fmap), pl.BlockSpec((t, w), rmap), ANY),
        scratch_shapes=[pltpu.VMEM((t, w), F32), pltpu.VMEM((t, w), F32), pltpu.VMEM((2, 8, w), F32),
                        pltpu.SemaphoreType.DMA((NDEV,)), pltpu.SemaphoreType.DMA((NDEV,))],
        compiler_params=_params(("arbitrary",)),
    )(xb, xb, wg, lv, wo_shard)


def _mix_gates(p_refs, hf_ref, hr_ref, wca_ref, perm_ref, t, w):
    bl, cl, ul, gl, ql = [r[...].astype(F32) for r in p_refs]
    tt = cl * ul
    tt16 = tt.astype(BF16)
    beside = _dot(perm_ref[2:4].reshape(2 * t, t), tt16)
    before, after = beside[:t], beside[t:]
    z = wca_ref[0:1, :] * before + wca_ref[1:2, :] * tt + wca_ref[2:3, :] * after
    sig_g = _sigmoid(gl)
    sig_q = _sigmoid(ql)
    ylru = _dot(perm_ref[1], (hf_ref[...] + hr_ref[...]).astype(BF16))
    return bl, cl, ul, gl, ql, (before, tt, after), z, sig_g, sig_q, ylru


def _p_specs(t, w, nt):
    return [pl.BlockSpec((t, w), functools.partial(lambda i, s: (jnp.minimum(i, nt - 1), s), s=s))
            for s in (0, 1, 2, 3, 5)]


def _mix_forward(x, tgt, p, hf, hr, wo, ov, wca, perm, t):
    l, d = x.shape
    w = d // 2
    nt = l // t

    def body(x_ref, tg_ref, b_ref, c_ref, u_ref, g_ref, q_ref, hf_ref, hr_ref, wo_ref, ov_ref, wca_ref, perm_ref,
             dn_ref, ct_ref, do_ref, part_ref):
        i = pl.program_id(0)
        bl, _, _, gl, ql, _, z, sig_g, sig_q, ylru = _mix_gates(
            (b_ref, c_ref, u_ref, g_ref, q_ref), hf_ref, hr_ref, wca_ref, perm_ref, t, w)
        ya = bl * z * (gl * sig_g)
        yb = ylru * (ql * sig_q)
        ct_ref[0:w, :] = ya.T.astype(BF16)
        ct_ref[w:, :] = yb.T.astype(BF16)
        out = _dot(ya.astype(BF16), wo_ref[0:w, :]) + _dot(yb.astype(BF16), wo_ref[w:, :])
        gate, fg = ov_ref[0:1, :], ov_ref[1:2, :]
        n = x_ref[...] + gate * out
        rr = lax.rsqrt(jnp.mean(n * n, axis=-1, keepdims=True) + EPS)
        nh = n * rr
        e = nh * fg - tg_ref[...]
        loss = 0.5 * jnp.sum(jnp.mean(e * e, axis=-1, keepdims=True), axis=0, keepdims=True)
        dy = e * (1.0 / d)
        dnh = dy * fg
        dn = rr * (dnh - nh * jnp.mean(dnh * nh, axis=-1, keepdims=True))
        dn_ref[...] = dn.astype(BF16)
        do_ref[...] = (dn * gate).astype(BF16)

        @pl.when(i == 0)
        def _():
            part_ref[...] = jnp.zeros_like(part_ref)

        part_ref[0:1, :] += jnp.sum(dy * nh, axis=0, keepdims=True)
        part_ref[1:2, :] += jnp.sum(dn * out, axis=0, keepdims=True)
        part_ref[2:3, :] += jnp.broadcast_to(loss, (1, d))

    tile = lambda cols: pl.BlockSpec((t, cols), lambda i: (i, 0))
    full = lambda shape: pl.BlockSpec(shape, lambda i: (0,) * len(shape))
    return _call(
        body, name="mix_forward",
        grid=(nt,),
        out_shape=(jax.ShapeDtypeStruct((l, d), BF16), jax.ShapeDtypeStruct((d, l), BF16),
                   jax.ShapeDtypeStruct((l, d), BF16), jax.ShapeDtypeStruct((8, d), F32)),
        in_specs=[tile(d), tile(d)] + _p_specs(t, w, nt) + [tile(w), tile(w),
                  pl.BlockSpec((d, d), lambda i: (0, 0), pipeline_mode=pl.Buffered(1)),
                  full(ov.shape), full(wca.shape), full(perm.shape)],
        out_specs=(tile(d), pl.BlockSpec((d, t), lambda i: (0, i)), tile(d), full((8, d))),
        compiler_params=_params(("arbitrary",)),
    )(x, tgt, p, p, p, p, p, hf, hr, wo, ov, wca, perm)


def _mix_backward(dout, p, hf, hr, wo, wca, perm, g_wout, l, t):
    d = dout.shape[1]
    w = d // 2
    nt = l // t
    la = p.shape[0]

    def body(do_ref, b_ref, c_ref, u_ref, g_ref, q_ref, hf_ref, hr_ref, wo_ref, wca_ref, perm_ref, gw_ref,
             dp_ref, dh_ref, part_ref, sc_ref, send_sems, recv_sems):
        i = pl.program_id(0)
        copies = _scatter_copies(gw_ref, sc_ref, send_sems, recv_sems)

        @pl.when(i == 0)
        def _():
            part_ref[...] = jnp.zeros_like(part_ref)
            for cp in copies:
                cp.start()

        @pl.when(i == nt)
        def _():
            dp_ref[...] = jnp.zeros_like(dp_ref)
            _exchange_wait(copies, copies[1:])

        @pl.when(i < nt)
        def _():
            bl, cl, ul, gl, ql, taps, z, sig_g, sig_q, ylru = _mix_gates(
                (b_ref, c_ref, u_ref, g_ref, q_ref), hf_ref, hr_ref, wca_ref, perm_ref, t, w)
            do = do_ref[...]
            dya = _dot_nt(do, wo_ref[0:w, :])
            dyb = _dot_nt(do, wo_ref[w:, :])
            sg = gl * sig_g
            dz = dya * bl * sg
            dz16 = dz.astype(BF16)
            beside = _dot(perm_ref[2:4].reshape(2 * t, t), dz16)
            dt = wca_ref[0:1, :] * beside[t:] + wca_ref[1:2, :] * dz + wca_ref[2:3, :] * beside[:t]
            dp_ref[:, 0:w] = (dya * z * sg).astype(BF16)
            dp_ref[:, w:2 * w] = (dt * ul).astype(BF16)
            dp_ref[:, 2 * w:3 * w] = (dt * cl).astype(BF16)
            dp_ref[:, 3 * w:4 * w] = (dya * bl * z * (sig_g * (1.0 + gl * (1.0 - sig_g)))).astype(BF16)
            dp_ref[:, 4 * w:5 * w] = jnp.zeros((t, w), BF16)
            dp_ref[:, 5 * w:6 * w] = (dyb * ylru * (sig_q * (1.0 + ql * (1.0 - sig_q)))).astype(BF16)
            dh_ref[...] = _dot(perm_ref[0], (dyb * (ql * sig_q)).astype(BF16)).astype(BF16)
            for j in range(3):
                part_ref[j:j + 1, :] += jnp.sum(dz * taps[j], axis=0, keepdims=True)

    clamp = lambda cols: pl.BlockSpec((t, cols), lambda i: (jnp.minimum(i, nt - 1), 0))
    full = lambda shape: pl.BlockSpec(shape, lambda i: (0,) * len(shape))
    return _call(
        body, name="mix_backward",
        grid=(nt + 1,),
        out_shape=(jax.ShapeDtypeStruct((la, 6 * w), BF16), jax.ShapeDtypeStruct((l, w), BF16),
                   jax.ShapeDtypeStruct((8, w), F32), jax.ShapeDtypeStruct(g_wout.shape, g_wout.dtype)),
        in_specs=[clamp(d)] + _p_specs(t, w, nt) + [clamp(w), clamp(w),
                  pl.BlockSpec((d, d), lambda i: (0, 0), pipeline_mode=pl.Buffered(1)), full(wca.shape),
                  full(perm.shape), ANY],
        out_specs=(pl.BlockSpec((t, 6 * w), lambda i: (i, 0)), clamp(w), full((8, w)), ANY),
        scratch_shapes=[pltpu.SemaphoreType.DMA((NDEV,)), pltpu.SemaphoreType.DMA((NDEV,))],
        compiler_params=_params(("arbitrary",)),
    )(dout, p, p, p, p, p, hf, hr, wo, wca, perm, g_wout)


def _lru_backward(direction, xb, dhs, hs, wg, lv, l, t, conv=None):
    la, w = hs.shape
    gc = wg.shape[2]
    ng = w // gc
    nt = l // t
    nblk8 = la // 8
    last = conv is not None
    assert last == (direction == 1)

    if direction == 0:
        tile = lambda i: jnp.where(i == nt, nt, nt - 1 - i)
        halo = lambda i: jnp.where(tile(i) == 0, nblk8 - 1, tile(i) * (t // 8) - 1)
    else:
        tile = lambda i: i
        halo = lambda i: jnp.minimum((i + 1) * (t // 8), nblk8 - 1)

    def body(*refs):
        x_ref, dh_ref, hs_ref, halo_ref, wg_ref, lv_ref = refs[:6]
        if last:
            v_ref, wcb_ref, bm_ref, dxo_ref = refs[6:10]
        out_ref, dwg_ref, part_ref, a_s, dh_s, g_s, carry = refs[-7:]
        i = pl.program_id(0)
        is_ctx = i == nt

        @pl.when(i == 0)
        def _():
            carry[...] = jnp.zeros_like(carry)
            dwg_ref[...] = jnp.zeros_like(dwg_ref)
            part_ref[...] = jnp.zeros_like(part_ref)

        xb = x_ref[...]
        lam = lv_ref[3 * direction + 2:3 * direction + 3, :]
        a, s, rs, tr, ti, sp = _lru_coef(xb, wg_ref, direction, lv_ref[3 * direction:3 * direction + 1, :],
                                         lv_ref[3 * direction + 1:3 * direction + 2, :], lam, gc)
        hs_t = hs_ref[...]
        r8 = _rows((8, w))
        if direction == 0:
            edge = jnp.where(is_ctx, 0.0, halo_ref[7:8, :])
            first = jnp.where(r8 == 0, edge, pltpu.roll(hs_t[t - 8:, :], 1, 0))
            hprev = jnp.concatenate([first, hs_t[:t - 8, :]], axis=0)
        else:
            edge = jnp.where(is_ctx, 0.0, halo_ref[0:1, :])
            final = jnp.where(r8 == 7, edge, pltpu.roll(hs_t[:8, :], 7, 0))
            hprev = jnp.concatenate([hs_t[8:, :], final], axis=0)
        a_s[...] = a
        dh_s[...] = jnp.where(is_ctx, 0.0, dh_ref[...].astype(F32))
        carry[...] = _scan_tile_backward(a_s, dh_s, g_s, carry[...], direction == 0)

        g = g_s[...]
        r = 0.5 * tr + 0.5
        ig = 0.5 * ti + 0.5
        ix = ig * xb
        gs = g * s
        dla = (g * a) * (hprev - ix * (a * rs))
        dxb = gs * ig
        dzr = dla * (r * (1.0 - tr)) * (-LRU_C * sp)
        dzi = gs * ix * (1.0 - ti)
        part_ref[0:1, :] += jnp.sum(dzr, axis=0, keepdims=True)
        part_ref[1:2, :] += jnp.sum(dzi, axis=0, keepdims=True)
        part_ref[2:3, :] += jnp.sum(dla * r, axis=0, keepdims=True) * (LRU_C * _sigmoid(-lam))
        pieces = []
        for gi in range(ng):
            sl = slice(gi * gc, (gi + 1) * gc)
            dz = jnp.concatenate([dzr[:, sl], dzi[:, sl]], axis=-1).astype(BF16)
            pieces.append(_dot_nt(dz, wg_ref[direction, gi]))
            dwg_ref[gi] += _dot(xb[:, sl].T.astype(BF16), dz)
        dxb = dxb + (pieces[0] if ng == 1 else jnp.concatenate(pieces, axis=-1))
        if not last:
            out_ref[...] = dxb
        else:
            dxb = dxb + dxo_ref[...]
            v = v_ref[...].astype(F32)
            backs = _dot(bm_ref[...].reshape(4 * t, t), dxb.astype(BF16))
            dv = jnp.zeros((t, w), F32)
            for j in range(4):
                back = backs[j * t:(j + 1) * t]
                dv = dv + wcb_ref[j:j + 1, :] * back
                part_ref[4 + j:5 + j, :] += jnp.sum(back * v, axis=0, keepdims=True)
            out_ref[...] = dv.astype(BF16)
            part_ref[3:4, :] += jnp.sum(dxb, axis=0, keepdims=True)

    full = lambda shape: pl.BlockSpec(shape, lambda i: (0,) * len(shape))
    kind = lambda i: (jnp.where(i == nt, 1, 0), 0, 0, 0)
    in_specs = [pl.BlockSpec((t, w), lambda i: (tile(i), 0)),
                pl.BlockSpec((t, w), lambda i: (jnp.minimum(tile(i), nt - 1), 0)),
                pl.BlockSpec((t, w), lambda i: (tile(i), 0)),
                pl.BlockSpec((8, w), lambda i: (halo(i), 0)),
                full(wg.shape), full(lv.shape)]
    args = [xb, dhs, hs, hs, wg, lv]
    if last:
        p, wcb, back_m, dxb_other, dp = conv
        in_specs += [pl.BlockSpec((t, w), lambda i: (tile(i), 4)), full(wcb.shape),
                     pl.BlockSpec((None, 4, t, t), kind), pl.BlockSpec((t, w), lambda i: (tile(i), 0)), ANY]
        args += [p, wcb, back_m, dxb_other, dp]
        out0 = jax.ShapeDtypeStruct(dp.shape, dp.dtype)
        spec0 = pl.BlockSpec((t, w), lambda i: (tile(i), 4))
        aliases = {10: 0}
    else:
        out0 = jax.ShapeDtypeStruct((la, w), F32)
        spec0 = pl.BlockSpec((t, w), lambda i: (tile(i), 0))
        aliases = {}
    return _call(
        body, name="lru_backward_%d" % direction,
        grid=(nt + 1,),
        out_shape=(out0, jax.ShapeDtypeStruct((ng, gc, 2 * gc), F32), jax.ShapeDtypeStruct((8, w), F32)),
        in_specs=in_specs,
        out_specs=(spec0, full((ng, gc, 2 * gc)), full((8, w))),
        scratch_shapes=[pltpu.VMEM((t, w), F32), pltpu.VMEM((t, w), F32), pltpu.VMEM((t, w), F32),
                        pltpu.VMEM((8, w), F32)],
        input_output_aliases=aliases,
        compiler_params=_params(("arbitrary",)),
    )(*args)


def _weight_grad_t(at, b, nblk_m, nblk_n, tk, name):
    m, k = at.shape
    n = b.shape[1]
    bm, bn = m // nblk_m, n // nblk_n
    nk = k // tk

    def body(a_ref, b_ref, o_ref, acc):
        kk = pl.program_id(2)

        @pl.when(kk == 0)
        def _():
            acc[...] = jnp.zeros_like(acc)

        acc[...] += _dot(a_ref[...], b_ref[...])

        @pl.when(kk == nk - 1)
        def _():
            o_ref[...] = acc[...].astype(BF16)

    return _call(
        body, name=name,
        grid=(nblk_m, nblk_n, nk),
        out_shape=jax.ShapeDtypeStruct((nblk_m * nblk_n, bm, bn), BF16),
        in_specs=[pl.BlockSpec((bm, tk), lambda i, j, kk: (i, kk)),
                  pl.BlockSpec((tk, bn), lambda i, j, kk: (kk, j))],
        out_specs=pl.BlockSpec((None, bm, bn), lambda i, j, kk: (i * nblk_n + j, 0, 0)),
        scratch_shapes=[pltpu.VMEM((bm, bn), F32)],
        compiler_params=_params(("arbitrary", "arbitrary", "arbitrary")),
    )(at, b)


def _weight_grad_scatter(at, b, tk, name):
    m, k = at.shape
    n = b.shape[1]
    bn = n // NDEV
    nk = k // tk
    where = jnp.stack([_idx(_my_pos()), lax.axis_index("c")]).astype(jnp.int32)

    def body(w_ref, a_ref, b_ref, recv_ref, acc, sbuf, sib, sib_send, sib_recv, chip_send, chip_recv, keep_sem):
        s, kk = pl.program_id(0), pl.program_id(1)
        x, y, c = _my_pos()

        @pl.when(kk == 0)
        def _():
            acc[...] = _dot(a_ref[...], b_ref[...])

        @pl.when(kk > 0)
        def _():
            acc[...] += _dot(a_ref[...], b_ref[...])

        def to_sibling(j):
            return pltpu.make_async_remote_copy(
                src_ref=sbuf.at[0], dst_ref=sib.at[j], send_sem=sib_send.at[j], recv_sem=sib_recv.at[j],
                device_id=(x, y, 1 - c), device_id_type=MESH)

        def to_chip(j):
            dist = _chip_order(j, c)
            return pltpu.make_async_remote_copy(
                src_ref=sbuf.at[1], dst_ref=recv_ref.at[dist // 2], send_sem=chip_send.at[j],
                recv_sem=chip_recv.at[dist // 2], device_id=_peer_at(dist), device_id_type=MESH)

        keep = pltpu.make_async_copy(sbuf.at[1], recv_ref.at[0], keep_sem)
        sends = []
        for j in range(4):
            sends += [to_sibling(j), to_chip(j) if j < 3 else keep]

        for st in range(NDEV):
            @pl.when((kk == nk - 1) & (s == st))
            def _(st=st):
                if st >= 2:
                    sends[st - 2].wait_send()
                part = acc[...]
                if st % 2 == 1:
                    to_sibling(st // 2).wait_recv()
                    part = part + sib[st // 2].astype(F32)
                sbuf[st % 2] = part.astype(BF16)
                sends[st].start()
                if st == NDEV - 1:
                    sends[st - 1].wait_send()
                    sends[st].wait()
                    for j in range(1, 4):
                        pltpu.make_async_remote_copy(
                            src_ref=sbuf.at[0], dst_ref=recv_ref.at[j], send_sem=chip_send.at[0],
                            recv_sem=chip_recv.at[j], device_id=_peer_at(2 * j), device_id_type=MESH).wait_recv()

    blk = lambda s, w_ref: w_ref[0] ^ _scatter_order(s, w_ref[1])
    return _call(
        body, name=name,
        grid_spec=pltpu.PrefetchScalarGridSpec(
            num_scalar_prefetch=1, grid=(NDEV, nk),
            in_specs=[pl.BlockSpec((m, tk), lambda s, kk, w_ref: (0, kk)),
                      pl.BlockSpec((tk, bn), lambda s, kk, w_ref: (kk, blk(s, w_ref)))],
            out_specs=ANY,
            scratch_shapes=[pltpu.VMEM((m, bn), F32), pltpu.VMEM((2, m, bn), BF16), pltpu.VMEM((4, m, bn), BF16),
                            pltpu.SemaphoreType.DMA((4,)), pltpu.SemaphoreType.DMA((4,)),
                            pltpu.SemaphoreType.DMA((4,)), pltpu.SemaphoreType.DMA((4,)),
                            pltpu.SemaphoreType.DMA]),
        out_shape=jax.ShapeDtypeStruct((4, m, bn), BF16),
        compiler_params=_params(("arbitrary", "arbitrary")),
    )(where, at, b)


def _input_backward(dp, w_all, src, mv, row0, tm, nbk, name, dn=None):
    rows, d = src.shape
    nb, _, bw = w_all.shape
    nk = nb // nbk
    ni = rows // (2 * tm)
    blk0 = row0 // (2 * tm)
    latent = dn is not None

    def body(*refs):
        dp_ref, w_ref, x_ref, mv_ref = refs[:4]
        outs = refs[4 + latent:]
        part_ref, acc = outs[latent], outs[latent + 1]
        i, k = pl.program_id(0), pl.program_id(1)

        def product():
            step = _dot_nt(dp_ref[:, 0:bw], w_ref[0])
            for q in range(1, nbk):
                step = step + _dot_nt(dp_ref[:, q * bw:(q + 1) * bw], w_ref[q])
            return step

        def finish(half):
            xf = x_ref[...]
            r = lax.rsqrt(jnp.mean(xf * xf, axis=-1, keepdims=True) + EPS)
            xn = xf * r
            dhl = acc[half * tm:(half + 1) * tm, :]
            gain, sc = mv_ref[0:1, :], mv_ref[1:2, :]
            dhx = jnp.sum(dhl * xn, axis=0, keepdims=True)
            part_ref[0:1, :] += jnp.sum(dhl, axis=0, keepdims=True)
            part_ref[1:2, :] += dhx * gain
            part_ref[2:3, :] += dhx * (1.0 + sc)
            if latent:
                dxn = dhl * (gain * (1.0 + sc))
                outs[0][...] = (refs[4][...].astype(F32)
                                + r * (dxn - xn * jnp.mean(dxn * xn, axis=-1, keepdims=True)))

        @pl.when((i == 0) & (k == 0))
        def _():
            part_ref[...] = jnp.zeros_like(part_ref)

        @pl.when(k == 0)
        def _():
            acc[...] = product()

        @pl.when((k > 0) & (k < nk))
        def _():
            acc[...] += product()

        for half in (0, 1):
            @pl.when(k == nk + half)
            def _(half=half):
                finish(half)

    half_tile = pl.BlockSpec((tm, d), lambda i, k: (2 * i + jnp.where(k > nk, 1, 0), 0))
    vec = pl.BlockSpec((8, d), lambda i, k: (0, 0))
    kblock = lambda k: jnp.minimum(k, nk - 1)
    return _call(
        body, name=name,
        grid=(ni, nk + 2),
        out_shape=((jax.ShapeDtypeStruct((rows, d), F32),) if latent else ()) + (jax.ShapeDtypeStruct((8, d), F32),),
        in_specs=[pl.BlockSpec((2 * tm, nbk * bw), lambda i, k: (blk0 + i, kblock(k))),
                  pl.BlockSpec((nbk, d, bw), lambda i, k: (kblock(k), 0, 0)), half_tile, vec]
                 + ([half_tile] if latent else []),
        out_specs=((half_tile,) if latent else ()) + (vec,),
        scratch_shapes=[pltpu.VMEM((2 * tm, d), F32)],
        compiler_params=_params(("arbitrary", "arbitrary")),
    )(*([dp, w_all, src, mv] + ([dn] if latent else [])))


def _adamw_scattered(parts, w, m, v, tr):
    r, c = w.shape
    nslot = parts.shape[0]

    def body(p_ref, w_ref, m_ref, v_ref, g_ref, d_ref, m2_ref, v2_ref):
        g = p_ref[0].astype(F32)
        for k in range(1, nslot):
            g = g + p_ref[k].astype(F32)
        g_ref[...] = g
        d_ref[...], m2_ref[...], v2_ref[...] = _adamw(w_ref[...], g, m_ref[...], v_ref[...])

    tile = pl.BlockSpec((tr, c), lambda i: (i, 0))
    return _call(
        body, name="adamw_scattered_%dx%d" % (r, c),
        grid=(r // tr,),
        out_shape=tuple(jax.ShapeDtypeStruct((r, c), F32) for _ in range(4)),
        in_specs=[pl.BlockSpec((nslot, tr, c), lambda i: (0, i, 0)), tile, tile, tile],
        out_specs=(tile,) * 4,
        compiler_params=_params(("arbitrary",)),
    )(parts, w, m, v)


def _adamw_ada(st, dmod, w, m, v, tr):
    r, c = w.shape

    def body(s_ref, dm_ref, w_ref, m_ref, v_ref, g_ref, d_ref, m2_ref, v2_ref):
        g = jnp.dot(s_ref[...], dm_ref[...], precision=HIGHEST, preferred_element_type=F32)
        g_ref[...] = g
        d_ref[...], m2_ref[...], v2_ref[...] = _adamw(w_ref[...], g, m_ref[...], v_ref[...])

    tile = pl.BlockSpec((tr, c), lambda i: (i, 0))
    return _call(
        body, name="adamw_ada",
        grid=(r // tr,),
        out_shape=tuple(jax.ShapeDtypeStruct((r, c), F32) for _ in range(4)),
        in_specs=[pl.BlockSpec((tr, 16), lambda i: (i, 0)), pl.BlockSpec((16, c), lambda i: (0, 0)),
                  tile, tile, tile],
        out_specs=(tile,) * 4,
        compiler_params=_params(("arbitrary",)),
    )(st, dmod, w, m, v)


def _adamw_small(gs, ws, ms, vs):
    n = len(ws)

    def body(*refs):
        for j in range(n):
            g_ref, w_ref, m_ref, v_ref = refs[j], refs[n + j], refs[2 * n + j], refs[3 * n + j]
            d_ref, m2_ref, v2_ref = refs[4 * n + j], refs[5 * n + j], refs[6 * n + j]
            d_ref[...], m2_ref[...], v2_ref[...] = _adamw(w_ref[...], g_ref[...], m_ref[...], v_ref[...])

    shapes = tuple(jax.ShapeDtypeStruct(a.shape, F32) for a in ws)
    out = _call(
        body, name="adamw_small",
        out_shape=shapes * 3,
        in_specs=[VMEM] * (4 * n), out_specs=(VMEM,) * (3 * n),
        compiler_params=_params(),
    )(*gs, *ws, *ms, *vs)
    return list(out[:n]), list(out[n:2 * n]), list(out[2 * n:])


def _blockdiag_groups(wh, gc):
    h, dh, _ = wh.shape
    g = gc // dh
    w4 = wh.reshape(h // g, g, dh, dh)
    bd = jnp.einsum("ngij,gh->ngihj", w4, jnp.eye(g, dtype=wh.dtype))
    return bd.reshape(h // g, gc, gc)


def _blockdiag_extract(bd, dh):
    ng, gc, _ = bd.shape
    g = gc // dh
    x = bd.reshape(ng, g, dh, g, dh)
    return jnp.einsum("ngihj,gh->ngij", x, jnp.eye(g, dtype=bd.dtype)).reshape(ng * g, dh, dh)


def _largest_tile(n, cap):
    return max(q for q in range(128, min(n, cap) + 1, 128) if n % q == 0)


def _rows8(*vecs):
    rows = [jnp.reshape(v, (1, -1)).astype(F32) for v in vecs]
    n = rows[0].shape[1]
    return jnp.concatenate(rows + [jnp.zeros((8 - len(rows), n), F32)], axis=0)


def _pack(pieces):
    flat = jnp.concatenate([jnp.reshape(a, (-1,)).astype(F32) for a in pieces])
    total = -(-flat.shape[0] // 1024) * 1024
    return jnp.pad(flat, (0, total - flat.shape[0])).reshape(total // 128, 128)


def _unpack(packed, shapes):
    flat = packed.reshape(-1)
    out, off = [], 0
    for s in shapes:
        n = 1
        for q in s:
            n *= q
        out.append(flat[off:off + n].reshape(s))
        off += n
    return out


def kernel(x, c, ctx, c_ctx, norm_g, w_ada, b_ada, w_in, w_conv_a, w_conv_b, b_conv_b, lru_wa, lru_ba, lru_wx, lru_bx, lru_lambda, w_out, final_g, loss_target, m_c_ctx, m_norm_g, m_w_ada, m_b_ada, m_w_in, m_w_conv_a, m_w_conv_b, m_b_conv_b, m_lru_wa, m_lru_ba, m_lru_wx, m_lru_bx, m_lru_lambda, m_w_out, m_final_g, v_c_ctx, v_norm_g, v_w_ada, v_b_ada, v_w_in, v_w_conv_a, v_w_conv_b, v_b_conv_b, v_lru_wa, v_lru_ba, v_lru_wx, v_lru_bx, v_lru_lambda, v_w_out, v_final_g):
    _, l, d = x.shape
    lc = ctx.shape[1]
    w = d // 2
    t = lc
    assert l % t == 0 and t % GRID_W == 0 and t % 128 == 0
    dh = w // N_HEADS
    gc = min(w, MXU_WIDTH)
    cols = w_ada.shape[2]
    wo_rows = w_out.shape[1]
    me = _idx(_my_pos())
    x2, ctx2, tgt2 = x[0], ctx[0], loss_target[0]
    w_ada2, w_in2, w_out2 = w_ada[0], w_in[0], w_out[0]

    small_mine = jnp.concatenate([a.reshape(-1) for a in (w_conv_a, w_conv_b, lru_ba, lru_bx, lru_lambda)]
                                 + [jnp.zeros((3 * (w // NDEV),), F32)]).reshape(16, w // NDEV)
    mod_all, s_mat, small_all = _mod_forward(
        jnp.broadcast_to(c, (8, d)), jnp.broadcast_to(c_ctx[None], (8, d)), w_ada2, small_mine)
    mod = jnp.transpose(mod_all, (1, 0, 2)).reshape(16, NDEV * cols) + b_ada
    mod_lat = lax.dynamic_slice_in_dim(mod, me, 1, axis=0)
    sh_l, sc_l, gt_l = jnp.split(mod_lat, 3, axis=-1)
    sh_c, sc_c, _ = jnp.split(mod[8:9], 3, axis=-1)
    small = jnp.transpose(small_all, (1, 0, 2)).reshape(16, w)
    wca = _rows8(*[small[j] for j in range(0, 3)])
    wcb = _rows8(*[small[j] for j in range(3, 7)], b_conv_b)
    lv = _rows8(0.5 * small[7], 0.5 * small[9], small[11], 0.5 * small[8], 0.5 * small[10], small[12])
    wg = jnp.stack([
        jnp.concatenate([_blockdiag_groups(lru_wa[0, dr], gc), _blockdiag_groups(lru_wx[0, dr], gc)], axis=-1)
        for dr in range(2)])
    wg = (0.5 * wg).astype(BF16)

    la = l + lc
    tm = 2 * t if l % (2 * t) == 0 else t
    tk = 3 * t if la % (3 * t) == 0 else t
    h, hlt = _normalize(x2, _rows8(norm_g, sc_l, sh_l), la, 0, tm, "normalize")
    h, hlt = _normalize(ctx2, _rows8(norm_g, sc_c, sh_c), la, l, t, "normalize_ctx", prev=(h, hlt))
    p, w_all = _in_projection(h, w_in2.astype(BF16), la // 8 if la % 128 == 0 else tk)
    taps_m, back_m, perm = _scan_matrices(t)
    xb = _conv_input(p, wcb, taps_m, l, t)
    hf, hr, wo_all = _lru_forward(xb, wg, lv, w_out2.astype(BF16), l, t)
    wo = wo_all.reshape(d, d)
    dn, catt, dout, part_mix = _mix_forward(x2, tgt2, p, hf, hr, wo, _rows8(gt_l, final_g), wca, perm, t)
    g_wout = _weight_grad_t(catt, dout, 2, 1, _largest_tile(l, 2048), "grad_w_out")
    dp, dhs, part_ca, sc_wout = _mix_backward(dout, p, hf, hr, wo, wca, perm, g_wout.reshape(NDEV, wo_rows, d), l, t)
    dxb0, dwg0, part_l0 = _lru_backward(0, xb, dhs, hf, wg, lv, l, t)
    dp, dwg1, part_l1 = _lru_backward(1, xb, dhs, hr, wg, lv, l, t, conv=(p, wcb, back_m, dxb0, dp))
    sc_win = _weight_grad_scatter(hlt, dp, _largest_tile(la, 1408), "grad_w_in")
    grad_x, part_lat = _input_backward(dp, w_all, x2, _rows8(norm_g, sc_l), 0, tm, 2, "input_backward", dn=dn)
    (part_ctx,) = _input_backward(dp, w_all, ctx2, _rows8(norm_g, sc_c), l, t // 2, 2, "input_backward_ctx")
    part_in = jnp.concatenate([part_lat[0:2], part_ctx[0:2], (part_lat[2] + part_ctx[2])[None]], axis=0)

    dwa = jnp.stack([_blockdiag_extract(dwg0[:, :, :gc], dh), _blockdiag_extract(dwg1[:, :, :gc], dh)])
    dwx = jnp.stack([_blockdiag_extract(dwg0[:, :, gc:], dh), _blockdiag_extract(dwg1[:, :, gc:], dh)])
    lru_part = (0.5 * jnp.stack([dwa, dwx])).reshape(NDEV, -1, 128)
    zeros_d = jnp.zeros((d,), F32)
    pieces = [
        jnp.concatenate([part_in[0], part_in[1], part_mix[1]]),
        jnp.concatenate([part_in[2], part_in[3], zeros_d]),
        part_in[4], part_mix[0], part_ca[0:3], part_l1[4:8], part_l1[3],
        0.5 * jnp.stack([part_l0[0], part_l1[0]]), 0.5 * jnp.stack([part_l0[1], part_l1[1]]),
        jnp.stack([part_l0[2], part_l1[2]]), part_mix[2, 0:1],
    ]
    shapes = [(3 * d,), (3 * d,), (d,), (d,), (3, w), (4, w), (w,), (2, w), (2, w), (2, w), (1,)]
    sig_cc = jax.nn.sigmoid(c_ctx)
    dsilu_cc = jnp.broadcast_to((sig_cc * (1.0 + c_ctx * (1.0 - sig_cc)))[None], (8, d))
    psum, pall, lru_sum, g_cctx8 = _reduce_small(_pack(pieces), lru_part, w_ada2, dsilu_cc)
    (g_modl, g_modc, g_norm, g_final, g_ca, g_cb, g_bcb, g_ba, g_bx, g_lam, loss1) = _unpack(psum, shapes)
    loss = loss1[0]
    g_cctx = g_cctx8[0]
    g_bada = (g_modl + g_modc)[None]
    g_lru = lru_sum.reshape(2, 2, N_HEADS, dh, dh)
    g_wa, g_wx = g_lru[0][None], g_lru[1][None]
    wsl = w // NDEV
    mine = lambda a: lax.dynamic_slice_in_dim(a, me * wsl, wsl, axis=-1)
    g_ca_m, g_cb_m, g_ba_m, g_bx_m, g_lam_m = (mine(g_ca)[None], mine(g_cb)[None], mine(g_ba)[None],
                                               mine(g_bx)[None], mine(g_lam)[None])
    g_norm, g_bcb = g_norm[None], g_bcb[None]

    cb = cols // 128
    per_dev = pall[:, :3 * d // 128].reshape(NDEV, NDEV, cols)
    dmod_lat = lax.dynamic_slice_in_dim(per_dev, me, 1, axis=1)[:, 0]
    dmod_ctx = lax.dynamic_slice_in_dim(g_modc.reshape(NDEV, cols), me, 1, axis=0)
    dmod16 = jnp.concatenate([dmod_lat, dmod_ctx, jnp.zeros((7, cols), F32)], axis=0)
    tr_ada = 256 if d % 256 == 0 else d
    g_wada, d_wada, m_wada, v_wada = _adamw_ada(s_mat.T, dmod16, w_ada2, m_w_ada[0], v_w_ada[0], tr_ada)
    g_win2, d_win, m_win, v_win = _adamw_scattered(sc_win, w_in2, m_w_in[0], v_w_in[0], tr_ada)
    tr_out = 64 if wo_rows % 64 == 0 else wo_rows
    g_wout2, d_wout, m_wout, v_wout = _adamw_scattered(sc_wout, w_out2, m_w_out[0], v_w_out[0], tr_out)

    small_w = [c_ctx, norm_g, b_ada, w_conv_a, w_conv_b, b_conv_b, lru_wa, lru_ba, lru_wx, lru_bx, lru_lambda, final_g]
    small_m = [m_c_ctx, m_norm_g, m_b_ada, m_w_conv_a, m_w_conv_b, m_b_conv_b, m_lru_wa, m_lru_ba, m_lru_wx,
               m_lru_bx, m_lru_lambda, m_final_g]
    small_v = [v_c_ctx, v_norm_g, v_b_ada, v_w_conv_a, v_w_conv_b, v_b_conv_b, v_lru_wa, v_lru_ba, v_lru_wx,
               v_lru_bx, v_lru_lambda, v_final_g]
    small_g = [g_cctx, g_norm, g_bada, g_ca_m, g_cb_m, g_bcb, g_wa, g_ba_m, g_wx, g_bx_m, g_lam_m, g_final]
    small_g = [jnp.reshape(a, b.shape) for a, b in zip(small_g, small_w)]
    d_s, m_s, v_s = _adamw_small(small_g, small_w, small_m, small_v)

    def weights(small_list, ada, win, wout):
        (cctx_, norm_, bada_, ca_, cb_, bcb_, wa_, ba_, wx_, bx_, lam_, final_) = small_list
        return [cctx_, norm_, ada[None], bada_, win[None], ca_, cb_, bcb_, wa_, ba_, wx_, bx_, lam_, wout[None], final_]

    return (loss, grad_x[None],
            *weights(small_g, g_wada, g_win2, g_wout2), *weights(d_s, d_wada, d_win, d_wout),
            *weights(m_s, m_wada, m_win, m_wout), *weights(v_s, v_wada, v_win, v_wout))
```

```python
import functools

import jax
import jax.numpy as jnp
import numpy as np
from jax import lax
from jax.experimental import pallas as pl
from jax.experimental.pallas import tpu as pltpu

F32 = jnp.float32
BF16 = jnp.bfloat16
MESH = pl.DeviceIdType.MESH
NDEV = 8
GRID_W = 64
N_HEADS = 16
LRU_C = 8.0
EPS = 1e-6
MXU_WIDTH = 256
VMEM_LIMIT = 60 * 1024 * 1024

ADAM_LR = 0.001
ADAM_B1 = 0.9
ADAM_B2 = 0.999
ADAM_EPS = 1e-08
ADAM_WD = 0.01
ADAM_STEP = 10
ADAM_C1 = 1.0 - ADAM_B1 ** ADAM_STEP
ADAM_C2 = 1.0 - ADAM_B2 ** ADAM_STEP

HIGHEST = lax.Precision.HIGHEST
ANY = pl.BlockSpec(memory_space=pl.ANY)
VMEM = pl.BlockSpec(memory_space=pltpu.VMEM)


def _call(body, **kw):
    return pl.pallas_call(body, **kw)


def _params(sem=None, vmem=VMEM_LIMIT):
    return pltpu.CompilerParams(dimension_semantics=sem, vmem_limit_bytes=vmem)


def _my_pos():
    return lax.axis_index("x"), lax.axis_index("y"), lax.axis_index("c")


def _idx(pos):
    return 4 * pos[0] + 2 * pos[1] + pos[2]


def _peer(k):
    x, y, c = _my_pos()
    return ((1 - x) if (k >> 2) & 1 else x, (1 - y) if (k >> 1) & 1 else y, (1 - c) if k & 1 else c)


def _exchange_start(src_ref, dst_ref, send_sems, recv_sems, base):
    me = _idx(_my_pos())
    sends = []
    for k in range(1, NDEV):
        cp = pltpu.make_async_remote_copy(
            src_ref=src_ref, dst_ref=dst_ref.at[me], send_sem=send_sems.at[base + k - 1],
            recv_sem=recv_sems.at[base + k - 1], device_id=_peer(k), device_id_type=MESH)
        cp.start()
        sends.append(cp)
    dst_ref[me] = src_ref[...]
    return sends, (src_ref, dst_ref, send_sems, recv_sems, base)


def _exchange_finish(started):
    sends, (src_ref, dst_ref, send_sems, recv_sems, base) = started
    for k in range(1, NDEV):
        peer = _peer(k)
        pltpu.make_async_remote_copy(
            src_ref=src_ref, dst_ref=dst_ref.at[_idx(peer)], send_sem=send_sems.at[base + k - 1],
            recv_sem=recv_sems.at[base + k - 1], device_id=peer, device_id_type=MESH).wait_recv()
    for cp in sends:
        cp.wait_send()


def _exchange_vmem(src_ref, dst_ref, send_sems, recv_sems, base):
    _exchange_finish(_exchange_start(src_ref, dst_ref, send_sems, recv_sems, base))


def _sigmoid(z):
    return 0.5 * jnp.tanh(0.5 * z) + 0.5


def _softplus(x):
    return jnp.maximum(x, 0.0) + jnp.log1p(jnp.exp(-jnp.abs(x)))


def _one_minus_sq(a, la):
    series = (-2.0 * la) * (1.0 + la)
    return jnp.where(la > -0.0015, series, 1.0 - a * a)


def _dot(a, b):
    return jnp.dot(a, b, preferred_element_type=F32)


def _dot_nt(a, b):
    return lax.dot_general(a, b, (((1,), (1,)), ((), ())), preferred_element_type=F32)


def _rows(shape):
    return lax.broadcasted_iota(jnp.int32, shape, 0)


def _scan_matrices(t):
    seg = t // 8
    r = np.arange(t)
    perm = (np.arange(t)[None, :] == ((r % 8) * seg + r // 8)[:, None]).astype(np.float32)
    rows, cols = r[:, None], r[None, :]
    taps, back = [], []
    for rowlen in (GRID_W, t):
        pos = rows % rowlen
        shift = {-2: (cols == rows - 2) & (pos >= 2), -1: (cols == rows - 1) & (pos >= 1),
                 0: cols == rows, 1: (cols == rows + 1) & (pos + 1 < rowlen),
                 2: (cols == rows + 2) & (pos + 2 < rowlen)}
        if rowlen == GRID_W:
            beside = [shift[-1].astype(np.float32), shift[1].astype(np.float32)]
        taps.append(np.stack([perm @ shift[k].astype(np.float32) for k in (-2, -1, 0, 1)]))
        back.append(np.stack([shift[k].astype(np.float32) @ perm.T for k in (2, 1, 0, -1)]))
    as_bf16 = lambda a: jnp.asarray(a, dtype=BF16)
    return as_bf16(np.stack(taps)), as_bf16(np.stack(back)), as_bf16(np.stack([perm, perm.T] + beside))


def _chunk_scan(a, b, reverse):
    row = _rows(a.shape)
    for s in (1, 2, 4):
        if reverse:
            m = row < 8 - s
            sh = 8 - s
        else:
            m = row >= s
            sh = s
        a_s = jnp.where(m, pltpu.roll(a, sh, 0), 1.0)
        b_s = jnp.where(m, pltpu.roll(b, sh, 0), 0.0)
        b = b + a * b_s
        a = a * a_s
    return a, b


def _chain_segments(ptot, hend, carry, reverse):
    ca, cb = _chunk_scan(ptot, hend, reverse)
    incl = ca * carry + cb
    r8 = _rows(incl.shape)
    if reverse:
        start = jnp.where(r8 < 7, pltpu.roll(incl, 7, 0), carry)
        last = incl[0:1, :]
    else:
        start = jnp.where(r8 >= 1, pltpu.roll(incl, 1, 0), carry)
        last = incl[7:8, :]
    return start, jnp.broadcast_to(last, incl.shape)


def _blocks(nblock, reverse):
    order = range(nblock - 1, -1, -1) if reverse else range(nblock)
    return [slice(8 * k, 8 * k + 8) for k in order]


def _scan_tile(a_ref, b_ref, out_ref, carry, reverse):
    t, w = a_ref.shape
    seg = t // 8

    hend, ptot = jnp.zeros((8, w), F32), jnp.ones((8, w), F32)
    for rows in _blocks(seg, reverse):
        a = a_ref[rows, :]
        hend, ptot = a * hend + b_ref[rows, :], a * ptot
    h, new_carry = _chain_segments(ptot, hend, carry, reverse)
    for rows in _blocks(seg, reverse):
        h = a_ref[rows, :] * h + b_ref[rows, :]
        out_ref[rows, :] = h
    return new_carry


def _scan_tile_backward(a_ref, dh_ref, g_ref, carry, reverse):
    t, w = a_ref.shape
    seg = t // 8

    uend, ptot = jnp.zeros((8, w), F32), jnp.ones((8, w), F32)
    for rows in _blocks(seg, reverse):
        a = a_ref[rows, :]
        uend, ptot = a * (dh_ref[rows, :] + uend), a * ptot
    u, new_carry = _chain_segments(ptot, uend, carry, reverse)
    for rows in _blocks(seg, reverse):
        g = dh_ref[rows, :] + u
        g_ref[rows, :] = g
        u = a_ref[rows, :] * g
    return new_carry


def _lru_coef(xb, wg_ref, d, ba, bx, lam, gc):
    w = xb.shape[1]
    xb16 = xb.astype(BF16)
    zr, zi = [], []
    for g in range(w // gc):
        z = _dot(xb16[:, g * gc:(g + 1) * gc], wg_ref[d, g])
        zr.append(z[:, :gc])
        zi.append(z[:, gc:])
    zr = zr[0] if len(zr) == 1 else jnp.concatenate(zr, axis=-1)
    zi = zi[0] if len(zi) == 1 else jnp.concatenate(zi, axis=-1)
    tr = jnp.tanh(zr + ba)
    ti = jnp.tanh(zi + bx)
    sp = _softplus(-lam)
    half = -0.5 * LRU_C * sp
    la = tr * half + half
    a = jnp.exp(la)
    q = _one_minus_sq(a, la)
    rs = lax.rsqrt(jnp.maximum(q, 1e-30))
    return a, q * rs, rs, tr, ti, sp


def _adamw(w, g, m, v):
    m2 = ADAM_B1 * m + (1.0 - ADAM_B1) * g
    v2 = ADAM_B2 * v + (1.0 - ADAM_B2) * (g * g)
    m_hat = m2 / ADAM_C1
    v_hat = v2 / ADAM_C2
    delta = -ADAM_LR * (m_hat / (jnp.sqrt(v_hat) + ADAM_EPS) + ADAM_WD * w)
    return delta, m2, v2


def _mod_forward(c8, cctx8, w_ada, small):
    d = c8.shape[1]
    cols = w_ada.shape[1]

    def body(c_ref, cctx_ref, w_ref, sm_ref, mod_ref, s_ref, sm_all, cbuf, mod_my, send_sems, recv_sems):
        _exchange_vmem(sm_ref, sm_all, send_sems, recv_sems, 2 * (NDEV - 1))
        _exchange_vmem(c_ref, cbuf, send_sems, recv_sems, 0)
        row = _rows((8, d))
        c_all = jnp.zeros((8, d), F32)
        for b in range(NDEV):
            c_all = jnp.where(row == b, cbuf[b], c_all)
        cc = cctx_ref[...]
        s_top = c_all * _sigmoid(c_all)
        s_bot = jnp.where(row == 0, cc * _sigmoid(cc), 0.0)
        s = jnp.concatenate([s_top, s_bot], axis=0)
        s_ref[...] = s
        mod_my[...] = jnp.dot(s, w_ref[...], precision=HIGHEST, preferred_element_type=F32)
        _exchange_vmem(mod_my, mod_ref, send_sems, recv_sems, NDEV - 1)

    return _call(
        body, name="mod_forward",
        out_shape=(jax.ShapeDtypeStruct((NDEV, 16, cols), F32), jax.ShapeDtypeStruct((16, d), F32),
                   jax.ShapeDtypeStruct((NDEV,) + small.shape, F32)),
        in_specs=[VMEM] * 4, out_specs=(VMEM,) * 3,
        scratch_shapes=[pltpu.VMEM((NDEV, 8, d), F32), pltpu.VMEM((16, cols), F32),
                        pltpu.SemaphoreType.DMA((3 * (NDEV - 1),)), pltpu.SemaphoreType.DMA((3 * (NDEV - 1),))],
        compiler_params=_params(),
    )(c8, cctx8, w_ada, small)


def _scatter_copies(src_ref, dst_ref, send_sems, recv_sems):
    me = _idx(_my_pos())
    copies = [pltpu.make_async_copy(src_ref.at[me], dst_ref.at[0], send_sems.at[0])]
    for k in range(1, NDEV):
        peer = _peer(k)
        copies.append(pltpu.make_async_remote_copy(
            src_ref=src_ref.at[_idx(peer)], dst_ref=dst_ref.at[k], send_sem=send_sems.at[k],
            recv_sem=recv_sems.at[k], device_id=peer, device_id_type=MESH))
    return copies


def _gather_copies(src_ref, dst_ref, send_sems, recv_sems):
    me = _idx(_my_pos())
    sends = [pltpu.make_async_copy(src_ref, dst_ref.at[me], send_sems.at[0])]
    arrivals = []
    for k in range(1, NDEV):
        peer = _peer(k)
        sends.append(pltpu.make_async_remote_copy(
            src_ref=src_ref, dst_ref=dst_ref.at[me], send_sem=send_sems.at[k],
            recv_sem=recv_sems.at[k], device_id=peer, device_id_type=MESH))
        arrivals.append(pltpu.make_async_remote_copy(
            src_ref=src_ref, dst_ref=dst_ref.at[_idx(peer)], send_sem=send_sems.at[k],
            recv_sem=recv_sems.at[k], device_id=peer, device_id_type=MESH))
    return sends, arrivals


def _exchange_wait(sends, arrivals):
    sends[0].wait()
    for cp in arrivals:
        cp.wait_recv()
    for cp in sends[1:]:
        cp.wait_send()


def _chip_order(k, c):
    return (6, 4 - 2 * c, 2 + 2 * c, 0)[k]


def _scatter_order(s, c):
    k = s >> 1
    mine = jnp.where(k == 0, 6, jnp.where(k == 1, 4 - 2 * c, jnp.where(k == 2, 2 + 2 * c, 0)))
    theirs = jnp.where(k == 0, 6, jnp.where(k == 1, 2 + 2 * c, jnp.where(k == 2, 4 - 2 * c, 0))) ^ 1
    return jnp.where((s & 1) == 0, theirs, mine)


def _peer_at(dist):
    x, y, c = _my_pos()
    return (x ^ ((dist >> 2) & 1), y ^ ((dist >> 1) & 1), c ^ (dist & 1))


def _reduce_small(packed, lru_parts, w_ada, dsilu_cctx):
    rp = packed.shape[0]
    rl = lru_parts.shape[1]
    d, cols = w_ada.shape
    assert cols % 128 == 0
    cb = cols // 128

    def body(p_ref, l_ref, w_ref, ds_ref, sum_ref, all_ref, lru_ref, cctx_ref,
             lbuf, lsum, cpart, call, send_sems, recv_sems, lsend, lrecv):
        me = _idx(_my_pos())
        scattered = _scatter_copies(l_ref, lbuf, lsend, lrecv)
        for cp in scattered:
            cp.start()
        _exchange_vmem(p_ref, all_ref, send_sems, recv_sems, 0)
        acc = all_ref[0]
        for j in range(1, NDEV):
            acc = acc + all_ref[j]
        sum_ref[...] = acc
        _exchange_wait(scattered, scattered[1:])
        red = lbuf[0]
        for k in range(1, NDEV):
            red = red + lbuf[k]
        lsum[...] = red
        lru_gather = _exchange_start(lsum, lru_ref, send_sems, recv_sems, NDEV - 1)
        part = jnp.zeros((8, d), F32)
        for q in range(cb):
            dm = jnp.broadcast_to(sum_ref[pl.ds((NDEV + me) * cb + q, 1), :], (8, 128))
            part = part + lax.dot_general(dm, w_ref[:, q * 128:(q + 1) * 128],
                                          (((1,), (1,)), ((), ())), precision=HIGHEST,
                                          preferred_element_type=F32)
        cpart[...] = part
        _exchange_vmem(cpart, call, send_sems, recv_sems, 2 * (NDEV - 1))
        _exchange_finish(lru_gather)
        tot = call[0]
        for j in range(1, NDEV):
            tot = tot + call[j]
        cctx_ref[...] = tot * ds_ref[...]

    return _call(
        body, name="reduce_small",
        out_shape=(jax.ShapeDtypeStruct((rp, 128), F32), jax.ShapeDtypeStruct((NDEV, rp, 128), F32),
                   jax.ShapeDtypeStruct((NDEV, rl, 128), F32), jax.ShapeDtypeStruct((8, d), F32)),
        in_specs=[VMEM] * 4, out_specs=(VMEM,) * 4,
        scratch_shapes=[pltpu.VMEM((NDEV, rl, 128), F32), pltpu.VMEM((rl, 128), F32), pltpu.VMEM((8, d), F32),
                        pltpu.VMEM((NDEV, 8, d), F32),
                        pltpu.SemaphoreType.DMA((3 * (NDEV - 1),)), pltpu.SemaphoreType.DMA((3 * (NDEV - 1),)),
                        pltpu.SemaphoreType.DMA((NDEV,)), pltpu.SemaphoreType.DMA((NDEV,))],
        compiler_params=_params(),
    )(packed, lru_parts, w_ada, dsilu_cctx)


def _normalize(src, mv, la, row0, tm, name, prev=None):
    rows, d = src.shape
    blk0 = row0 // tm

    def body(*refs):
        x_ref, mv_ref, h_ref = refs[0], refs[1], refs[-1]
        xf = x_ref[...]
        r = lax.rsqrt(jnp.mean(xf * xf, axis=-1, keepdims=True) + EPS)
        h = xf * r * (mv_ref[0:1, :] * (1.0 + mv_ref[1:2, :])) + mv_ref[2:3, :]
        h_ref[...] = h.astype(BF16)

    in_specs = [pl.BlockSpec((tm, d), lambda i: (i, 0)), pl.BlockSpec((8, d), lambda i: (0, 0))]
    args = [src, mv]
    aliases = {}
    if prev is not None:
        in_specs += [ANY]
        args += [prev]
        aliases = {2: 0}
    return _call(
        body, name=name,
        grid=(rows // tm,),
        out_shape=jax.ShapeDtypeStruct((la, d), BF16),
        in_specs=in_specs,
        out_specs=pl.BlockSpec((tm, d), lambda i: (blk0 + i, 0)),
        input_output_aliases=aliases,
        compiler_params=_params(("arbitrary",)),
    )(*args)


def _gather_order(step):
    return (step & 1) | (((step >> 2) & 1) << 1) | (((step >> 1) & 1) << 2)


def _in_projection(h, w_shard, tm):
    la, d = h.shape
    bw = w_shard.shape[1]
    ni = la // tm
    where = jnp.reshape(_idx(_my_pos()), (1,)).astype(jnp.int32)

    def body(me_ref, h_ref, w_ref, p_ref, all_ref, wbuf, send_sems, recv_sems, local_sems):
        s, i = pl.program_id(0), pl.program_id(1)
        x, y, c = _my_pos()
        me, sibling = (x, y, c), (x, y, 1 - c)
        chips = [(1 - x, y), (x, 1 - y), (1 - x, 1 - y)]

        def copy(k, block, to, from_shard=False):
            return pltpu.make_async_remote_copy(
                src_ref=w_ref if from_shard else all_ref.at[_idx(block)], dst_ref=all_ref.at[_idx(block)],
                send_sem=send_sems.at[k], recv_sem=recv_sems.at[k], device_id=to, device_id_type=MESH)

        def load(block, slot):
            return pltpu.make_async_copy(all_ref.at[_idx(block)], wbuf.at[slot], local_sems.at[1])

        keep = pltpu.make_async_copy(w_ref, all_ref.at[_idx(me)], local_sems.at[0])
        first = [copy(0, me, sibling, True)] + [copy(1 + j, me, (*chip, c), True) for j, chip in enumerate(chips)]
        passed = [copy(4 + j, (*chip, c), sibling) for j, chip in enumerate(chips)]
        steps = [(copy(0, sibling, me), None, sibling)]
        for j, chip in enumerate(chips):
            steps.append((copy(1 + j, (*chip, c), me), passed[j], (*chip, c)))
            steps.append((copy(4 + j, (*chip, 1 - c), me), None, (*chip, 1 - c)))

        @pl.when((s == 0) & (i == 0))
        def _():
            keep.start()
            mine = pltpu.make_async_copy(w_ref, wbuf.at[0], local_sems.at[1])
            mine.start()
            for cp in first:
                cp.start()
            mine.wait()

        for n, (arrival, forward, block) in enumerate(steps, start=1):
            @pl.when((s == n - 1) & (i == ni - 1))
            def _(arrival=arrival, forward=forward, block=block, n=n):
                arrival.wait_recv()
                if forward is not None:
                    forward.start()
                load(block, n % 2).start()

        @pl.when((s > 0) & (i == 0))
        def _():
            load(me, s % 2).wait()

        p_ref[...] = _dot(h_ref[...], wbuf[s % 2]).astype(BF16)

        @pl.when((s == NDEV - 1) & (i == ni - 1))
        def _():
            for cp in first + passed:
                cp.wait_send()
            keep.wait()

    return _call(
        body, name="in_projection",
        grid_spec=pltpu.PrefetchScalarGridSpec(
            num_scalar_prefetch=1, grid=(NDEV, ni),
            in_specs=[pl.BlockSpec((tm, d), lambda s, i, me_ref: (i, 0)), ANY],
            out_specs=(pl.BlockSpec((tm, bw), lambda s, i, me_ref: (i, me_ref[0] ^ _gather_order(s))), ANY),
            scratch_shapes=[pltpu.VMEM((2, d, bw), BF16), pltpu.SemaphoreType.DMA((7,)),
                            pltpu.SemaphoreType.DMA((7,)), pltpu.SemaphoreType.DMA((2,))]),
        out_shape=(jax.ShapeDtypeStruct((la, NDEV * bw), BF16), jax.ShapeDtypeStruct((NDEV, d, bw), BF16)),
        compiler_params=_params(("arbitrary", "arbitrary")),
    )(where, h, w_shard)


def _conv_input(p, wcb, taps_m, l, t):
    la = p.shape[0]
    w = wcb.shape[1]
    nt = l // t

    def body(v_ref, wcb_ref, tm_ref, xb_ref):
        taps = _dot(tm_ref[...].reshape(4 * t, t), v_ref[...])
        xb = wcb_ref[4:5, :] + wcb_ref[0:1, :] * taps[0:t]
        for j in range(1, 4):
            xb = xb + wcb_ref[j:j + 1, :] * taps[j * t:(j + 1) * t]
        xb_ref[...] = xb

    return _call(
        body, name="conv_input",
        grid=(nt + 1,),
        out_shape=jax.ShapeDtypeStruct((la, w), F32),
        in_specs=[pl.BlockSpec((t, w), lambda i: (i, 4)), pl.BlockSpec((8, w), lambda i: (0, 0)),
                  pl.BlockSpec((None, 4, t, t), lambda i: (i // nt, 0, 0, 0))],
        out_specs=pl.BlockSpec((t, w), lambda i: (i, 0)),
        compiler_params=_params(("arbitrary",)),
    )(p, wcb, taps_m)


def _lru_forward(xb, wg, lv, wo_shard, l, t):
    la, w = xb.shape
    gc = wg.shape[2]
    nt = l // t

    def body(xf_ref, xr_ref, wg_ref, lv_ref, wo_ref, hf_ref, hr_ref, wo_all,
             a_s, b_s, carry, send_sems, recv_sems):
        sends, arrivals = _gather_copies(wo_ref, wo_all, send_sems, recv_sems)

        @pl.when(pl.program_id(0) == 0)
        def _():
            carry[...] = jnp.zeros_like(carry)
            for cp in sends:
                cp.start()

        @pl.when(pl.program_id(0) == nt)
        def _():
            _exchange_wait(sends, arrivals)

        for dr, (x_ref, h_ref) in enumerate(((xf_ref, hf_ref), (xr_ref, hr_ref))):
            x = x_ref[...]
            a, s, _, _, ti, _ = _lru_coef(x, wg_ref, dr, lv_ref[3 * dr:3 * dr + 1, :],
                                          lv_ref[3 * dr + 1:3 * dr + 2, :], lv_ref[3 * dr + 2:3 * dr + 3, :], gc)
            a_s[...] = a
            b_s[...] = (s * x) * (0.5 * ti + 0.5)
            carry[dr] = _scan_tile(a_s, b_s, h_ref, carry[dr], dr == 1)

    full = lambda shape: pl.BlockSpec(shape, lambda i: (0,) * len(shape))
    fmap = lambda i: (jnp.where(i == 0, nt, i - 1), 0)
    rmap = lambda i: (jnp.where(i == 0, nt, nt - i), 0)
    return _call(
        body, name="lru_forward",
        grid=(nt + 1,),
        out_shape=(jax.ShapeDtypeStruct((la, w), F32), jax.ShapeDtypeStruct((la, w), F32),
                   jax.ShapeDtypeStruct((NDEV,) + wo_shard.shape, wo_shard.dtype)),
        in_specs=[pl.BlockSpec((t, w), fmap), pl.BlockSpec((t, w), rmap), full(wg.shape), full(lv.shape), ANY],
        out_specs=(pl.BlockSpec((t, w), fmap), pl.BlockSpec((t, w), rmap), ANY),
        scratch_shapes=[pltpu.VMEM((t, w), F32), pltpu.VMEM((t, w), F32), pltpu.VMEM((2, 8, w), F32),
                        pltpu.SemaphoreType.DMA((NDEV,)), pltpu.SemaphoreType.DMA((NDEV,))],
        compiler_params=_params(("arbitrary",)),
    )(xb, xb, wg, lv, wo_shard)


def _mix_gates(p_refs, hf_ref, hr_ref, wca_ref, perm_ref, t, w):
    bl, cl, ul, gl, ql = [r[...].astype(F32) for r in p_refs]
    tt = cl * ul
    tt16 = tt.astype(BF16)
    beside = _dot(perm_ref[2:4].reshape(2 * t, t), tt16)
    before, after = beside[:t], beside[t:]
    z = wca_ref[0:1, :] * before + wca_ref[1:2, :] * tt + wca_ref[2:3, :] * after
    sig_g = _sigmoid(gl)
    sig_q = _sigmoid(ql)
    ylru = _dot(perm_ref[1], (hf_ref[...] + hr_ref[...]).astype(BF16))
    return bl, cl, ul, gl, ql, (before, tt, after), z, sig_g, sig_q, ylru


def _p_specs(t, w, nt):
    return [pl.BlockSpec((t, w), functools.partial(lambda i, s: (jnp.minimum(i, nt - 1), s), s=s))
            for s in (0, 1, 2, 3, 5)]


def _mix_forward(x, tgt, p, hf, hr, wo, ov, wca, perm, t):
    l, d = x.shape
    w = d // 2
    nt = l // t

    def body(x_ref, tg_ref, b_ref, c_ref, u_ref, g_ref, q_ref, hf_ref, hr_ref, wo_ref, ov_ref, wca_ref, perm_ref,
             dn_ref, ct_ref, do_ref, part_ref):
        i = pl.program_id(0)
        bl, _, _, gl, ql, _, z, sig_g, sig_q, ylru = _mix_gates(
            (b_ref, c_ref, u_ref, g_ref, q_ref), hf_ref, hr_ref, wca_ref, perm_ref, t, w)
        ya = bl * z * (gl * sig_g)
        yb = ylru * (ql * sig_q)
        ct_ref[:, 0:w] = ya.astype(BF16)
        ct_ref[:, w:] = yb.astype(BF16)
        out = _dot(ya.astype(BF16), wo_ref[0:w, :]) + _dot(yb.astype(BF16), wo_ref[w:, :])
        gate, fg = ov_ref[0:1, :], ov_ref[1:2, :]
        n = x_ref[...] + gate * out
        rr = lax.rsqrt(jnp.mean(n * n, axis=-1, keepdims=True) + EPS)
        nh = n * rr
        e = nh * fg - tg_ref[...]
        loss = 0.5 * jnp.sum(jnp.mean(e * e, axis=-1, keepdims=True), axis=0, keepdims=True)
        dy = e * (1.0 / d)
        dnh = dy * fg
        dn = rr * (dnh - nh * jnp.mean(dnh * nh, axis=-1, keepdims=True))
        dn_ref[...] = dn.astype(BF16)
        do_ref[...] = (dn * gate).astype(BF16)

        @pl.when(i == 0)
        def _():
            part_ref[...] = jnp.zeros_like(part_ref)

        part_ref[0:1, :] += jnp.sum(dy * nh, axis=0, keepdims=True)
        part_ref[1:2, :] += jnp.sum(dn * out, axis=0, keepdims=True)
        part_ref[2:3, :] += jnp.broadcast_to(loss, (1, d))

    tile = lambda cols: pl.BlockSpec((t, cols), lambda i: (i, 0))
    full = lambda shape: pl.BlockSpec(shape, lambda i: (0,) * len(shape))
    return _call(
        body, name="mix_forward",
        grid=(nt,),
        out_shape=(jax.ShapeDtypeStruct((l, d), BF16), jax.ShapeDtypeStruct((l, d), BF16),
                   jax.ShapeDtypeStruct((l, d), BF16), jax.ShapeDtypeStruct((8, d), F32)),
        in_specs=[tile(d), tile(d)] + _p_specs(t, w, nt) + [tile(w), tile(w),
                  pl.BlockSpec((d, d), lambda i: (0, 0), pipeline_mode=pl.Buffered(1)),
                  full(ov.shape), full(wca.shape), full(perm.shape)],
        out_specs=(tile(d), tile(d), tile(d), full((8, d))),
        compiler_params=_params(("arbitrary",)),
    )(x, tgt, p, p, p, p, p, hf, hr, wo, ov, wca, perm)


def _mix_backward(dout, p, hf, hr, wo, wca, perm, g_wout, l, t):
    d = dout.shape[1]
    w = d // 2
    nt = l // t
    la = p.shape[0]

    def body(do_ref, b_ref, c_ref, u_ref, g_ref, q_ref, hf_ref, hr_ref, wo_ref, wca_ref, perm_ref, gw_ref,
             dp_ref, dh_ref, part_ref, sc_ref, send_sems, recv_sems):
        i = pl.program_id(0)
        copies = _scatter_copies(gw_ref, sc_ref, send_sems, recv_sems)

        @pl.when(i == 0)
        def _():
            part_ref[...] = jnp.zeros_like(part_ref)
            for cp in copies:
                cp.start()

        @pl.when(i == nt)
        def _():
            dp_ref[...] = jnp.zeros_like(dp_ref)
            _exchange_wait(copies, copies[1:])

        @pl.when(i < nt)
        def _():
            bl, cl, ul, gl, ql, taps, z, sig_g, sig_q, ylru = _mix_gates(
                (b_ref, c_ref, u_ref, g_ref, q_ref), hf_ref, hr_ref, wca_ref, perm_ref, t, w)
            do = do_ref[...]
            dya = _dot_nt(do, wo_ref[0:w, :])
            dyb = _dot_nt(do, wo_ref[w:, :])
            sg = gl * sig_g
            dz = dya * bl * sg
            dz16 = dz.astype(BF16)
            beside = _dot(perm_ref[2:4].reshape(2 * t, t), dz16)
            dt = wca_ref[0:1, :] * beside[t:] + wca_ref[1:2, :] * dz + wca_ref[2:3, :] * beside[:t]
            dp_ref[:, 0:w] = (dya * z * sg).astype(BF16)
            dp_ref[:, w:2 * w] = (dt * ul).astype(BF16)
            dp_ref[:, 2 * w:3 * w] = (dt * cl).astype(BF16)
            dp_ref[:, 3 * w:4 * w] = (dya * bl * z * (sig_g * (1.0 + gl * (1.0 - sig_g)))).astype(BF16)
            dp_ref[:, 4 * w:5 * w] = jnp.zeros((t, w), BF16)
            dp_ref[:, 5 * w:6 * w] = (dyb * ylru * (sig_q * (1.0 + ql * (1.0 - sig_q)))).astype(BF16)
            dh_ref[...] = _dot(perm_ref[0], (dyb * (ql * sig_q)).astype(BF16)).astype(BF16)
            for j in range(3):
                part_ref[j:j + 1, :] += jnp.sum(dz * taps[j], axis=0, keepdims=True)

    clamp = lambda cols: pl.BlockSpec((t, cols), lambda i: (jnp.minimum(i, nt - 1), 0))
    full = lambda shape: pl.BlockSpec(shape, lambda i: (0,) * len(shape))
    return _call(
        body, name="mix_backward",
        grid=(nt + 1,),
        out_shape=(jax.ShapeDtypeStruct((la, 6 * w), BF16), jax.ShapeDtypeStruct((l, w), BF16),
                   jax.ShapeDtypeStruct((8, w), F32), jax.ShapeDtypeStruct(g_wout.shape, g_wout.dtype)),
        in_specs=[clamp(d)] + _p_specs(t, w, nt) + [clamp(w), clamp(w),
                  pl.BlockSpec((d, d), lambda i: (0, 0), pipeline_mode=pl.Buffered(1)), full(wca.shape),
                  full(perm.shape), ANY],
        out_specs=(pl.BlockSpec((t, 6 * w), lambda i: (i, 0)), clamp(w), full((8, w)), ANY),
        scratch_shapes=[pltpu.SemaphoreType.DMA((NDEV,)), pltpu.SemaphoreType.DMA((NDEV,))],
        compiler_params=_params(("arbitrary",)),
    )(dout, p, p, p, p, p, hf, hr, wo, wca, perm, g_wout)


def _lru_backward(direction, xb, dhs, hs, wg, lv, l, t, conv=None):
    la, w = hs.shape
    gc = wg.shape[2]
    ng = w // gc
    nt = l // t
    nblk8 = la // 8
    last = conv is not None
    assert last == (direction == 1)

    if direction == 0:
        tile = lambda i: jnp.where(i == nt, nt, nt - 1 - i)
        halo = lambda i: jnp.where(tile(i) == 0, nblk8 - 1, tile(i) * (t // 8) - 1)
    else:
        tile = lambda i: i
        halo = lambda i: jnp.minimum((i + 1) * (t // 8), nblk8 - 1)

    def body(*refs):
        x_ref, dh_ref, hs_ref, halo_ref, wg_ref, lv_ref = refs[:6]
        if last:
            v_ref, wcb_ref, bm_ref, dxo_ref = refs[6:10]
        out_ref, dwg_ref, part_ref, a_s, dh_s, g_s, carry = refs[-7:]
        i = pl.program_id(0)
        is_ctx = i == nt

        @pl.when(i == 0)
        def _():
            carry[...] = jnp.zeros_like(carry)
            dwg_ref[...] = jnp.zeros_like(dwg_ref)
            part_ref[...] = jnp.zeros_like(part_ref)

        xb = x_ref[...]
        lam = lv_ref[3 * direction + 2:3 * direction + 3, :]
        a, s, rs, tr, ti, sp = _lru_coef(xb, wg_ref, direction, lv_ref[3 * direction:3 * direction + 1, :],
                                         lv_ref[3 * direction + 1:3 * direction + 2, :], lam, gc)
        hs_t = hs_ref[...]
        r8 = _rows((8, w))
        if direction == 0:
            edge = jnp.where(is_ctx, 0.0, halo_ref[7:8, :])
            first = jnp.where(r8 == 0, edge, pltpu.roll(hs_t[t - 8:, :], 1, 0))
            hprev = jnp.concatenate([first, hs_t[:t - 8, :]], axis=0)
        else:
            edge = jnp.where(is_ctx, 0.0, halo_ref[0:1, :])
            final = jnp.where(r8 == 7, edge, pltpu.roll(hs_t[:8, :], 7, 0))
            hprev = jnp.concatenate([hs_t[8:, :], final], axis=0)
        a_s[...] = a
        dh_s[...] = jnp.where(is_ctx, 0.0, dh_ref[...].astype(F32))
        carry[...] = _scan_tile_backward(a_s, dh_s, g_s, carry[...], direction == 0)

        g = g_s[...]
        r = 0.5 * tr + 0.5
        ig = 0.5 * ti + 0.5
        ix = ig * xb
        gs = g * s
        dla = (g * a) * (hprev - ix * (a * rs))
        dxb = gs * ig
        dzr = dla * (r * (1.0 - tr)) * (-LRU_C * sp)
        dzi = gs * ix * (1.0 - ti)
        part_ref[0:1, :] += jnp.sum(dzr, axis=0, keepdims=True)
        part_ref[1:2, :] += jnp.sum(dzi, axis=0, keepdims=True)
        part_ref[2:3, :] += jnp.sum(dla * r, axis=0, keepdims=True) * (LRU_C * _sigmoid(-lam))
        pieces = []
        for gi in range(ng):
            sl = slice(gi * gc, (gi + 1) * gc)
            dz = jnp.concatenate([dzr[:, sl], dzi[:, sl]], axis=-1).astype(BF16)
            pieces.append(_dot_nt(dz, wg_ref[direction, gi]))
            dwg_ref[gi] += _dot(xb[:, sl].T.astype(BF16), dz)
        dxb = dxb + (pieces[0] if ng == 1 else jnp.concatenate(pieces, axis=-1))
        if not last:
            out_ref[...] = dxb
        else:
            dxb = dxb + dxo_ref[...]
            v = v_ref[...].astype(F32)
            backs = _dot(bm_ref[...].reshape(4 * t, t), dxb.astype(BF16))
            dv = jnp.zeros((t, w), F32)
            for j in range(4):
                back = backs[j * t:(j + 1) * t]
                dv = dv + wcb_ref[j:j + 1, :] * back
                part_ref[4 + j:5 + j, :] += jnp.sum(back * v, axis=0, keepdims=True)
            out_ref[...] = dv.astype(BF16)
            part_ref[3:4, :] += jnp.sum(dxb, axis=0, keepdims=True)

    full = lambda shape: pl.BlockSpec(shape, lambda i: (0,) * len(shape))
    kind = lambda i: (jnp.where(i == nt, 1, 0), 0, 0, 0)
    in_specs = [pl.BlockSpec((t, w), lambda i: (tile(i), 0)),
                pl.BlockSpec((t, w), lambda i: (jnp.minimum(tile(i), nt - 1), 0)),
                pl.BlockSpec((t, w), lambda i: (tile(i), 0)),
                pl.BlockSpec((8, w), lambda i: (halo(i), 0)),
                full(wg.shape), full(lv.shape)]
    args = [xb, dhs, hs, hs, wg, lv]
    if last:
        p, wcb, back_m, dxb_other, dp = conv
        in_specs += [pl.BlockSpec((t, w), lambda i: (tile(i), 4)), full(wcb.shape),
                     pl.BlockSpec((None, 4, t, t), kind), pl.BlockSpec((t, w), lambda i: (tile(i), 0)), ANY]
        args += [p, wcb, back_m, dxb_other, dp]
        out0 = jax.ShapeDtypeStruct(dp.shape, dp.dtype)
        spec0 = pl.BlockSpec((t, w), lambda i: (tile(i), 4))
        aliases = {10: 0}
    else:
        out0 = jax.ShapeDtypeStruct((la, w), F32)
        spec0 = pl.BlockSpec((t, w), lambda i: (tile(i), 0))
        aliases = {}
    return _call(
        body, name="lru_backward_%d" % direction,
        grid=(nt + 1,),
        out_shape=(out0, jax.ShapeDtypeStruct((ng, gc, 2 * gc), F32), jax.ShapeDtypeStruct((8, w), F32)),
        in_specs=in_specs,
        out_specs=(spec0, full((ng, gc, 2 * gc)), full((8, w))),
        scratch_shapes=[pltpu.VMEM((t, w), F32), pltpu.VMEM((t, w), F32), pltpu.VMEM((t, w), F32),
                        pltpu.VMEM((8, w), F32)],
        input_output_aliases=aliases,
        compiler_params=_params(("arbitrary",)),
    )(*args)


def _weight_grad_t(a, b, nblk_m, nblk_n, tk, name):
    k, m = a.shape
    n = b.shape[1]
    bm, bn = m // nblk_m, n // nblk_n
    nk = k // tk

    def body(a_ref, b_ref, o_ref, acc):
        kk = pl.program_id(2)

        @pl.when(kk == 0)
        def _():
            acc[...] = jnp.zeros_like(acc)

        acc[...] += lax.dot_general(a_ref[...], b_ref[...], (((0,), (0,)), ((), ())), preferred_element_type=F32)

        @pl.when(kk == nk - 1)
        def _():
            o_ref[...] = acc[...].astype(BF16)

    return _call(
        body, name=name,
        grid=(nblk_m, nblk_n, nk),
        out_shape=jax.ShapeDtypeStruct((nblk_m * nblk_n, bm, bn), BF16),
        in_specs=[pl.BlockSpec((tk, bm), lambda i, j, kk: (kk, i)),
                  pl.BlockSpec((tk, bn), lambda i, j, kk: (kk, j))],
        out_specs=pl.BlockSpec((None, bm, bn), lambda i, j, kk: (i * nblk_n + j, 0, 0)),
        scratch_shapes=[pltpu.VMEM((bm, bn), F32)],
        compiler_params=_params(("arbitrary", "arbitrary", "arbitrary")),
    )(a, b)


def _weight_grad_scatter(at, b, tk, name):
    k, m = at.shape
    n = b.shape[1]
    bn = n // NDEV
    nk = k // tk
    where = jnp.stack([_idx(_my_pos()), lax.axis_index("c")]).astype(jnp.int32)
    tn = (((0,), (0,)), ((), ()))

    def body(w_ref, a_ref, b_ref, recv_ref, acc, sbuf, sib, sib_send, sib_recv, chip_send, chip_recv, keep_sem):
        s, kk = pl.program_id(0), pl.program_id(1)
        x, y, c = _my_pos()

        @pl.when(kk == 0)
        def _():
            acc[...] = lax.dot_general(a_ref[...], b_ref[...], tn, preferred_element_type=F32)

        @pl.when(kk > 0)
        def _():
            acc[...] += lax.dot_general(a_ref[...], b_ref[...], tn, preferred_element_type=F32)

        def to_sibling(j):
            return pltpu.make_async_remote_copy(
                src_ref=sbuf.at[0], dst_ref=sib.at[j], send_sem=sib_send.at[j], recv_sem=sib_recv.at[j],
                device_id=(x, y, 1 - c), device_id_type=MESH)

        def to_chip(j):
            dist = _chip_order(j, c)
            return pltpu.make_async_remote_copy(
                src_ref=sbuf.at[1], dst_ref=recv_ref.at[dist // 2], send_sem=chip_send.at[j],
                recv_sem=chip_recv.at[dist // 2], device_id=_peer_at(dist), device_id_type=MESH)

        keep = pltpu.make_async_copy(sbuf.at[1], recv_ref.at[0], keep_sem)
        sends = []
        for j in range(4):
            sends += [to_sibling(j), to_chip(j) if j < 3 else keep]

        for st in range(NDEV):
            @pl.when((kk == nk - 1) & (s == st))
            def _(st=st):
                if st >= 2:
                    sends[st - 2].wait_send()
                part = acc[...]
                if st % 2 == 1:
                    to_sibling(st // 2).wait_recv()
                    part = part + sib[st // 2].astype(F32)
                sbuf[st % 2] = part.astype(BF16)
                sends[st].start()
                if st == NDEV - 1:
                    sends[st - 1].wait_send()
                    sends[st].wait()
                    for j in range(1, 4):
                        pltpu.make_async_remote_copy(
                            src_ref=sbuf.at[0], dst_ref=recv_ref.at[j], send_sem=chip_send.at[0],
                            recv_sem=chip_recv.at[j], device_id=_peer_at(2 * j), device_id_type=MESH).wait_recv()

    blk = lambda s, w_ref: w_ref[0] ^ _scatter_order(s, w_ref[1])
    return _call(
        body, name=name,
        grid_spec=pltpu.PrefetchScalarGridSpec(
            num_scalar_prefetch=1, grid=(NDEV, nk),
            in_specs=[pl.BlockSpec((tk, m), lambda s, kk, w_ref: (kk, 0)),
                      pl.BlockSpec((tk, bn), lambda s, kk, w_ref: (kk, blk(s, w_ref)))],
            out_specs=ANY,
            scratch_shapes=[pltpu.VMEM((m, bn), F32), pltpu.VMEM((2, m, bn), BF16), pltpu.VMEM((4, m, bn), BF16),
                            pltpu.SemaphoreType.DMA((4,)), pltpu.SemaphoreType.DMA((4,)),
                            pltpu.SemaphoreType.DMA((4,)), pltpu.SemaphoreType.DMA((4,)),
                            pltpu.SemaphoreType.DMA]),
        out_shape=jax.ShapeDtypeStruct((4, m, bn), BF16),
        compiler_params=_params(("arbitrary", "arbitrary")),
    )(where, at, b)


def _input_backward(dp, w_all, src, mv, row0, tm, nbk, name, dn=None):
    rows, d = src.shape
    nb, _, bw = w_all.shape
    nk = nb // nbk
    ni = rows // tm
    blk0 = row0 // tm
    latent = dn is not None

    def body(*refs):
        dp_ref, w_ref, x_ref, mv_ref = refs[:4]
        outs = refs[4 + latent:]
        part_ref, acc = outs[latent], outs[latent + 1]
        i, k = pl.program_id(0), pl.program_id(1)

        def product():
            step = _dot_nt(dp_ref[:, 0:bw], w_ref[0])
            for q in range(1, nbk):
                step = step + _dot_nt(dp_ref[:, q * bw:(q + 1) * bw], w_ref[q])
            return step

        def finish(slot):
            xf = x_ref[...]
            r = lax.rsqrt(jnp.mean(xf * xf, axis=-1, keepdims=True) + EPS)
            xn = xf * r
            dhl = acc[slot]
            gain, sc = mv_ref[0:1, :], mv_ref[1:2, :]
            dhx = jnp.sum(dhl * xn, axis=0, keepdims=True)
            part_ref[0:1, :] += jnp.sum(dhl, axis=0, keepdims=True)
            part_ref[1:2, :] += dhx * gain
            part_ref[2:3, :] += dhx * (1.0 + sc)
            if latent:
                dxn = dhl * (gain * (1.0 + sc))
                outs[0][...] = (refs[4][...].astype(F32)
                                + r * (dxn - xn * jnp.mean(dxn * xn, axis=-1, keepdims=True)))

        @pl.when((i == 0) & (k == 0))
        def _():
            part_ref[...] = jnp.zeros_like(part_ref)
            acc[0] = product()

        @pl.when((i > 0) & (i < ni) & (k == 0))
        def _():
            acc[i % 2] = product()
            finish((i - 1) % 2)

        @pl.when((i == ni) & (k == 0))
        def _():
            finish((ni - 1) % 2)

        @pl.when((i < ni) & (k > 0))
        def _():
            acc[i % 2] += product()

    tile = pl.BlockSpec((tm, d), lambda i, k: (jnp.maximum(i - 1, 0), 0))
    vec = pl.BlockSpec((8, d), lambda i, k: (0, 0))
    kblock = lambda i, k: jnp.where(i == ni, nk - 1, k)
    return _call(
        body, name=name,
        grid=(ni + 1, nk),
        out_shape=((jax.ShapeDtypeStruct((rows, d), F32),) if latent else ()) + (jax.ShapeDtypeStruct((8, d), F32),),
        in_specs=[pl.BlockSpec((tm, nbk * bw), lambda i, k: (blk0 + jnp.minimum(i, ni - 1), kblock(i, k))),
                  pl.BlockSpec((nbk, d, bw), lambda i, k: (kblock(i, k), 0, 0)), tile, vec]
                 + ([tile] if latent else []),
        out_specs=((tile,) if latent else ()) + (vec,),
        scratch_shapes=[pltpu.VMEM((2, tm, d), F32)],
        compiler_params=_params(("arbitrary", "arbitrary")),
    )(*([dp, w_all, src, mv] + ([dn] if latent else [])))


def _adamw_scattered(parts, w, m, v, tr):
    r, c = w.shape
    nslot = parts.shape[0]

    def body(p_ref, w_ref, m_ref, v_ref, g_ref, d_ref, m2_ref, v2_ref):
        g = p_ref[0].astype(F32)
        for k in range(1, nslot):
            g = g + p_ref[k].astype(F32)
        g_ref[...] = g
        d_ref[...], m2_ref[...], v2_ref[...] = _adamw(w_ref[...], g, m_ref[...], v_ref[...])

    tile = pl.BlockSpec((tr, c), lambda i: (i, 0))
    return _call(
        body, name="adamw_scattered_%dx%d" % (r, c),
        grid=(r // tr,),
        out_shape=tuple(jax.ShapeDtypeStruct((r, c), F32) for _ in range(4)),
        in_specs=[pl.BlockSpec((nslot, tr, c), lambda i: (0, i, 0)), tile, tile, tile],
        out_specs=(tile,) * 4,
        compiler_params=_params(("arbitrary",)),
    )(parts, w, m, v)


def _adamw_ada(st, dmod, w, m, v, tr):
    r, c = w.shape

    def body(s_ref, dm_ref, w_ref, m_ref, v_ref, g_ref, d_ref, m2_ref, v2_ref):
        g = jnp.dot(s_ref[...], dm_ref[...], precision=HIGHEST, preferred_element_type=F32)
        g_ref[...] = g
        d_ref[...], m2_ref[...], v2_ref[...] = _adamw(w_ref[...], g, m_ref[...], v_ref[...])

    tile = pl.BlockSpec((tr, c), lambda i: (i, 0))
    return _call(
        body, name="adamw_ada",
        grid=(r // tr,),
        out_shape=tuple(jax.ShapeDtypeStruct((r, c), F32) for _ in range(4)),
        in_specs=[pl.BlockSpec((tr, 16), lambda i: (i, 0)), pl.BlockSpec((16, c), lambda i: (0, 0)),
                  tile, tile, tile],
        out_specs=(tile,) * 4,
        compiler_params=_params(("arbitrary",)),
    )(st, dmod, w, m, v)


def _adamw_small(gs, ws, ms, vs):
    n = len(ws)

    def body(*refs):
        for j in range(n):
            g_ref, w_ref, m_ref, v_ref = refs[j], refs[n + j], refs[2 * n + j], refs[3 * n + j]
            d_ref, m2_ref, v2_ref = refs[4 * n + j], refs[5 * n + j], refs[6 * n + j]
            d_ref[...], m2_ref[...], v2_ref[...] = _adamw(w_ref[...], g_ref[...], m_ref[...], v_ref[...])

    shapes = tuple(jax.ShapeDtypeStruct(a.shape, F32) for a in ws)
    out = _call(
        body, name="adamw_small",
        out_shape=shapes * 3,
        in_specs=[VMEM] * (4 * n), out_specs=(VMEM,) * (3 * n),
        compiler_params=_params(),
    )(*gs, *ws, *ms, *vs)
    return list(out[:n]), list(out[n:2 * n]), list(out[2 * n:])


def _blockdiag_groups(wh, gc):
    h, dh, _ = wh.shape
    g = gc // dh
    w4 = wh.reshape(h // g, g, dh, dh)
    bd = jnp.einsum("ngij,gh->ngihj", w4, jnp.eye(g, dtype=wh.dtype))
    return bd.reshape(h // g, gc, gc)


def _blockdiag_extract(bd, dh):
    ng, gc, _ = bd.shape
    g = gc // dh
    x = bd.reshape(ng, g, dh, g, dh)
    return jnp.einsum("ngihj,gh->ngij", x, jnp.eye(g, dtype=bd.dtype)).reshape(ng * g, dh, dh)


def _largest_tile(n, cap):
    return max(q for q in range(128, min(n, cap) + 1, 128) if n % q == 0)


def _rows8(*vecs):
    rows = [jnp.reshape(v, (1, -1)).astype(F32) for v in vecs]
    n = rows[0].shape[1]
    return jnp.concatenate(rows + [jnp.zeros((8 - len(rows), n), F32)], axis=0)


def _pack(pieces):
    flat = jnp.concatenate([jnp.reshape(a, (-1,)).astype(F32) for a in pieces])
    total = -(-flat.shape[0] // 1024) * 1024
    return jnp.pad(flat, (0, total - flat.shape[0])).reshape(total // 128, 128)


def _unpack(packed, shapes):
    flat = packed.reshape(-1)
    out, off = [], 0
    for s in shapes:
        n = 1
        for q in s:
            n *= q
        out.append(flat[off:off + n].reshape(s))
        off += n
    return out


def kernel(x, c, ctx, c_ctx, norm_g, w_ada, b_ada, w_in, w_conv_a, w_conv_b, b_conv_b, lru_wa, lru_ba, lru_wx, lru_bx, lru_lambda, w_out, final_g, loss_target, m_c_ctx, m_norm_g, m_w_ada, m_b_ada, m_w_in, m_w_conv_a, m_w_conv_b, m_b_conv_b, m_lru_wa, m_lru_ba, m_lru_wx, m_lru_bx, m_lru_lambda, m_w_out, m_final_g, v_c_ctx, v_norm_g, v_w_ada, v_b_ada, v_w_in, v_w_conv_a, v_w_conv_b, v_b_conv_b, v_lru_wa, v_lru_ba, v_lru_wx, v_lru_bx, v_lru_lambda, v_w_out, v_final_g):
    _, l, d = x.shape
    lc = ctx.shape[1]
    w = d // 2
    t = lc
    assert l % t == 0 and t % GRID_W == 0 and t % 128 == 0
    dh = w // N_HEADS
    gc = min(w, MXU_WIDTH)
    cols = w_ada.shape[2]
    wo_rows = w_out.shape[1]
    me = _idx(_my_pos())
    x2, ctx2, tgt2 = x[0], ctx[0], loss_target[0]
    w_ada2, w_in2, w_out2 = w_ada[0], w_in[0], w_out[0]

    small_mine = jnp.concatenate([a.reshape(-1) for a in (w_conv_a, w_conv_b, lru_ba, lru_bx, lru_lambda)]
                                 + [jnp.zeros((3 * (w // NDEV),), F32)]).reshape(16, w // NDEV)
    mod_all, s_mat, small_all = _mod_forward(
        jnp.broadcast_to(c, (8, d)), jnp.broadcast_to(c_ctx[None], (8, d)), w_ada2, small_mine)
    mod = jnp.transpose(mod_all, (1, 0, 2)).reshape(16, NDEV * cols) + b_ada
    mod_lat = lax.dynamic_slice_in_dim(mod, me, 1, axis=0)
    sh_l, sc_l, gt_l = jnp.split(mod_lat, 3, axis=-1)
    sh_c, sc_c, _ = jnp.split(mod[8:9], 3, axis=-1)
    small = jnp.transpose(small_all, (1, 0, 2)).reshape(16, w)
    wca = _rows8(*[small[j] for j in range(0, 3)])
    wcb = _rows8(*[small[j] for j in range(3, 7)], b_conv_b)
    lv = _rows8(0.5 * small[7], 0.5 * small[9], small[11], 0.5 * small[8], 0.5 * small[10], small[12])
    wg = jnp.stack([
        jnp.concatenate([_blockdiag_groups(lru_wa[0, dr], gc), _blockdiag_groups(lru_wx[0, dr], gc)], axis=-1)
        for dr in range(2)])
    wg = (0.5 * wg).astype(BF16)

    la = l + lc
    tm = 2 * t if l % (2 * t) == 0 else t
    tk = 3 * t if la % (3 * t) == 0 else t
    h = _normalize(x2, _rows8(norm_g, sc_l, sh_l), la, 0, tm, "normalize")
    h = _normalize(ctx2, _rows8(norm_g, sc_c, sh_c), la, l, t, "normalize_ctx", prev=h)
    p, w_all = _in_projection(h, w_in2.astype(BF16), la // 8 if la % 128 == 0 else tk)
    taps_m, back_m, perm = _scan_matrices(t)
    xb = _conv_input(p, wcb, taps_m, l, t)
    hf, hr, wo_all = _lru_forward(xb, wg, lv, w_out2.astype(BF16), l, t)
    wo = wo_all.reshape(d, d)
    dn, cat, dout, part_mix = _mix_forward(x2, tgt2, p, hf, hr, wo, _rows8(gt_l, final_g), wca, perm, t)
    g_wout = _weight_grad_t(cat, dout, 2, 1, _largest_tile(l, 2048), "grad_w_out")
    dp, dhs, part_ca, sc_wout = _mix_backward(dout, p, hf, hr, wo, wca, perm, g_wout.reshape(NDEV, wo_rows, d), l, t)
    dxb0, dwg0, part_l0 = _lru_backward(0, xb, dhs, hf, wg, lv, l, t)
    dp, dwg1, part_l1 = _lru_backward(1, xb, dhs, hr, wg, lv, l, t, conv=(p, wcb, back_m, dxb0, dp))
    sc_win = _weight_grad_scatter(h, dp, _largest_tile(la, 1408), "grad_w_in")
    grad_x, part_lat = _input_backward(dp, w_all, x2, _rows8(norm_g, sc_l), 0, tm, 2, "input_backward", dn=dn)
    (part_ctx,) = _input_backward(dp, w_all, ctx2, _rows8(norm_g, sc_c), l, t, 2, "input_backward_ctx")
    part_in = jnp.concatenate([part_lat[0:2], part_ctx[0:2], (part_lat[2] + part_ctx[2])[None]], axis=0)

    dwa = jnp.stack([_blockdiag_extract(dwg0[:, :, :gc], dh), _blockdiag_extract(dwg1[:, :, :gc], dh)])
    dwx = jnp.stack([_blockdiag_extract(dwg0[:, :, gc:], dh), _blockdiag_extract(dwg1[:, :, gc:], dh)])
    lru_part = (0.5 * jnp.stack([dwa, dwx])).reshape(NDEV, -1, 128)
    zeros_d = jnp.zeros((d,), F32)
    pieces = [
        jnp.concatenate([part_in[0], part_in[1], part_mix[1]]),
        jnp.concatenate([part_in[2], part_in[3], zeros_d]),
        part_in[4], part_mix[0], part_ca[0:3], part_l1[4:8], part_l1[3],
        0.5 * jnp.stack([part_l0[0], part_l1[0]]), 0.5 * jnp.stack([part_l0[1], part_l1[1]]),
        jnp.stack([part_l0[2], part_l1[2]]), part_mix[2, 0:1],
    ]
    shapes = [(3 * d,), (3 * d,), (d,), (d,), (3, w), (4, w), (w,), (2, w), (2, w), (2, w), (1,)]
    sig_cc = jax.nn.sigmoid(c_ctx)
    dsilu_cc = jnp.broadcast_to((sig_cc * (1.0 + c_ctx * (1.0 - sig_cc)))[None], (8, d))
    psum, pall, lru_sum, g_cctx8 = _reduce_small(_pack(pieces), lru_part, w_ada2, dsilu_cc)
    (g_modl, g_modc, g_norm, g_final, g_ca, g_cb, g_bcb, g_ba, g_bx, g_lam, loss1) = _unpack(psum, shapes)
    loss = loss1[0]
    g_cctx = g_cctx8[0]
    g_bada = (g_modl + g_modc)[None]
    g_lru = lru_sum.reshape(2, 2, N_HEADS, dh, dh)
    g_wa, g_wx = g_lru[0][None], g_lru[1][None]
    wsl = w // NDEV
    mine = lambda a: lax.dynamic_slice_in_dim(a, me * wsl, wsl, axis=-1)
    g_ca_m, g_cb_m, g_ba_m, g_bx_m, g_lam_m = (mine(g_ca)[None], mine(g_cb)[None], mine(g_ba)[None],
                                               mine(g_bx)[None], mine(g_lam)[None])
    g_norm, g_bcb = g_norm[None], g_bcb[None]

    cb = cols // 128
    per_dev = pall[:, :3 * d // 128].reshape(NDEV, NDEV, cols)
    dmod_lat = lax.dynamic_slice_in_dim(per_dev, me, 1, axis=1)[:, 0]
    dmod_ctx = lax.dynamic_slice_in_dim(g_modc.reshape(NDEV, cols), me, 1, axis=0)
    dmod16 = jnp.concatenate([dmod_lat, dmod_ctx, jnp.zeros((7, cols), F32)], axis=0)
    tr_ada = 256 if d % 256 == 0 else d
    g_wada, d_wada, m_wada, v_wada = _adamw_ada(s_mat.T, dmod16, w_ada2, m_w_ada[0], v_w_ada[0], tr_ada)
    g_win2, d_win, m_win, v_win = _adamw_scattered(sc_win, w_in2, m_w_in[0], v_w_in[0], tr_ada)
    tr_out = 64 if wo_rows % 64 == 0 else wo_rows
    g_wout2, d_wout, m_wout, v_wout = _adamw_scattered(sc_wout, w_out2, m_w_out[0], v_w_out[0], tr_out)

    small_w = [c_ctx, norm_g, b_ada, w_conv_a, w_conv_b, b_conv_b, lru_wa, lru_ba, lru_wx, lru_bx, lru_lambda, final_g]
    small_m = [m_c_ctx, m_norm_g, m_b_ada, m_w_conv_a, m_w_conv_b, m_b_conv_b, m_lru_wa, m_lru_ba, m_lru_wx,
               m_lru_bx, m_lru_lambda, m_final_g]
    small_v = [v_c_ctx, v_norm_g, v_b_ada, v_w_conv_a, v_w_conv_b, v_b_conv_b, v_lru_wa, v_lru_ba, v_lru_wx,
               v_lru_bx, v_lru_lambda, v_final_g]
    small_g = [g_cctx, g_norm, g_bada, g_ca_m, g_cb_m, g_bcb, g_wa, g_ba_m, g_wx, g_bx_m, g_lam_m, g_final]
    small_g = [jnp.reshape(a, b.shape) for a, b in zip(small_g, small_w)]
    d_s, m_s, v_s = _adamw_small(small_g, small_w, small_m, small_v)

    def weights(small_list, ada, win, wout):
        (cctx_, norm_, bada_, ca_, cb_, bcb_, wa_, ba_, wx_, bx_, lam_, final_) = small_list
        return [cctx_, norm_, ada[None], bada_, win[None], ca_, cb_, bcb_, wa_, ba_, wx_, bx_, lam_, wout[None], final_]

    return (loss, grad_x[None],
            *weights(small_g, g_wada, g_win2, g_wout2), *weights(d_s, d_wada, d_win, d_wout),
            *weights(m_s, m_wada, m_win, m_wout), *weights(v_s, v_wada, v_win, v_wout))
```

```python
import functools

import jax
import jax.numpy as jnp
import numpy as np
from jax import lax
from jax.experimental import pallas as pl
from jax.experimental.pallas import tpu as pltpu

F32 = jnp.float32
BF16 = jnp.bfloat16
MESH = pl.DeviceIdType.MESH
NDEV = 8
GRID_W = 64
N_HEADS = 16
LRU_C = 8.0
EPS = 1e-6
MXU_WIDTH = 256
VMEM_LIMIT = 60 * 1024 * 1024

ADAM_LR = 0.001
ADAM_B1 = 0.9
ADAM_B2 = 0.999
ADAM_EPS = 1e-08
ADAM_WD = 0.01
ADAM_STEP = 10
ADAM_C1 = 1.0 - ADAM_B1 ** ADAM_STEP
ADAM_C2 = 1.0 - ADAM_B2 ** ADAM_STEP

HIGHEST = lax.Precision.HIGHEST
ANY = pl.BlockSpec(memory_space=pl.ANY)
VMEM = pl.BlockSpec(memory_space=pltpu.VMEM)


def _call(body, **kw):
    return pl.pallas_call(body, **kw)


def _params(sem=None, vmem=VMEM_LIMIT):
    return pltpu.CompilerParams(dimension_semantics=sem, vmem_limit_bytes=vmem)


def _my_pos():
    return lax.axis_index("x"), lax.axis_index("y"), lax.axis_index("c")


def _idx(pos):
    return 4 * pos[0] + 2 * pos[1] + pos[2]


def _peer(k):
    x, y, c = _my_pos()
    return ((1 - x) if (k >> 2) & 1 else x, (1 - y) if (k >> 1) & 1 else y, (1 - c) if k & 1 else c)


def _exchange_start(src_ref, dst_ref, send_sems, recv_sems, base):
    me = _idx(_my_pos())
    sends = []
    for k in range(1, NDEV):
        cp = pltpu.make_async_remote_copy(
            src_ref=src_ref, dst_ref=dst_ref.at[me], send_sem=send_sems.at[base + k - 1],
            recv_sem=recv_sems.at[base + k - 1], device_id=_peer(k), device_id_type=MESH)
        cp.start()
        sends.append(cp)
    dst_ref[me] = src_ref[...]
    return sends, (src_ref, dst_ref, send_sems, recv_sems, base)


def _exchange_finish(started):
    sends, (src_ref, dst_ref, send_sems, recv_sems, base) = started
    for k in range(1, NDEV):
        peer = _peer(k)
        pltpu.make_async_remote_copy(
            src_ref=src_ref, dst_ref=dst_ref.at[_idx(peer)], send_sem=send_sems.at[base + k - 1],
            recv_sem=recv_sems.at[base + k - 1], device_id=peer, device_id_type=MESH).wait_recv()
    for cp in sends:
        cp.wait_send()


def _exchange_vmem(src_ref, dst_ref, send_sems, recv_sems, base):
    _exchange_finish(_exchange_start(src_ref, dst_ref, send_sems, recv_sems, base))


def _sigmoid(z):
    return 0.5 * jnp.tanh(0.5 * z) + 0.5


def _softplus(x):
    return jnp.maximum(x, 0.0) + jnp.log1p(jnp.exp(-jnp.abs(x)))


def _one_minus_sq(a, la):
    series = (-2.0 * la) * (1.0 + la)
    return jnp.where(la > -0.0015, series, 1.0 - a * a)


def _dot(a, b):
    return jnp.dot(a, b, preferred_element_type=F32)


def _dot_nt(a, b):
    return lax.dot_general(a, b, (((1,), (1,)), ((), ())), preferred_element_type=F32)


def _rows(shape):
    return lax.broadcasted_iota(jnp.int32, shape, 0)


def _scan_matrices(t):
    seg = t // 8
    r = np.arange(t)
    perm = (np.arange(t)[None, :] == ((r % 8) * seg + r // 8)[:, None]).astype(np.float32)
    rows, cols = r[:, None], r[None, :]
    taps, back = [], []
    for rowlen in (GRID_W, t):
        pos = rows % rowlen
        shift = {-2: (cols == rows - 2) & (pos >= 2), -1: (cols == rows - 1) & (pos >= 1),
                 0: cols == rows, 1: (cols == rows + 1) & (pos + 1 < rowlen),
                 2: (cols == rows + 2) & (pos + 2 < rowlen)}
        if rowlen == GRID_W:
            beside = [shift[-1].astype(np.float32), shift[1].astype(np.float32)]
        taps.append(np.stack([perm @ shift[k].astype(np.float32) for k in (-2, -1, 0, 1)]))
        back.append(np.stack([shift[k].astype(np.float32) @ perm.T for k in (2, 1, 0, -1)]))
    as_bf16 = lambda a: jnp.asarray(a, dtype=BF16)
    return as_bf16(np.stack(taps)), as_bf16(np.stack(back)), as_bf16(np.stack([perm, perm.T] + beside))


def _chunk_scan(a, b, reverse):
    row = _rows(a.shape)
    for s in (1, 2, 4):
        if reverse:
            m = row < 8 - s
            sh = 8 - s
        else:
            m = row >= s
            sh = s
        a_s = jnp.where(m, pltpu.roll(a, sh, 0), 1.0)
        b_s = jnp.where(m, pltpu.roll(b, sh, 0), 0.0)
        b = b + a * b_s
        a = a * a_s
    return a, b


def _chain_segments(ptot, hend, carry, reverse):
    ca, cb = _chunk_scan(ptot, hend, reverse)
    incl = ca * carry + cb
    r8 = _rows(incl.shape)
    if reverse:
        start = jnp.where(r8 < 7, pltpu.roll(incl, 7, 0), carry)
        last = incl[0:1, :]
    else:
        start = jnp.where(r8 >= 1, pltpu.roll(incl, 1, 0), carry)
        last = incl[7:8, :]
    return start, jnp.broadcast_to(last, incl.shape)


def _blocks(nblock, reverse):
    order = range(nblock - 1, -1, -1) if reverse else range(nblock)
    return [slice(8 * k, 8 * k + 8) for k in order]


def _scan_tile(a_ref, b_ref, out_ref, carry, reverse):
    t, w = a_ref.shape
    seg = t // 8

    hend, ptot = jnp.zeros((8, w), F32), jnp.ones((8, w), F32)
    for rows in _blocks(seg, reverse):
        a = a_ref[rows, :]
        hend, ptot = a * hend + b_ref[rows, :], a * ptot
    h, new_carry = _chain_segments(ptot, hend, carry, reverse)
    for rows in _blocks(seg, reverse):
        h = a_ref[rows, :] * h + b_ref[rows, :]
        out_ref[rows, :] = h
    return new_carry


def _scan_tile_backward(a_ref, dh_ref, g_ref, carry, reverse):
    t, w = a_ref.shape
    seg = t // 8

    uend, ptot = jnp.zeros((8, w), F32), jnp.ones((8, w), F32)
    for rows in _blocks(seg, reverse):
        a = a_ref[rows, :]
        uend, ptot = a * (dh_ref[rows, :] + uend), a * ptot
    u, new_carry = _chain_segments(ptot, uend, carry, reverse)
    for rows in _blocks(seg, reverse):
        g = dh_ref[rows, :] + u
        g_ref[rows, :] = g
        u = a_ref[rows, :] * g
    return new_carry


def _lru_coef(xb, wg_ref, d, ba, bx, lam, gc):
    w = xb.shape[1]
    xb16 = xb.astype(BF16)
    zr, zi = [], []
    for g in range(w // gc):
        z = _dot(xb16[:, g * gc:(g + 1) * gc], wg_ref[d, g])
        zr.append(z[:, :gc])
        zi.append(z[:, gc:])
    zr = zr[0] if len(zr) == 1 else jnp.concatenate(zr, axis=-1)
    zi = zi[0] if len(zi) == 1 else jnp.concatenate(zi, axis=-1)
    tr = jnp.tanh(zr + ba)
    ti = jnp.tanh(zi + bx)
    sp = _softplus(-lam)
    half = -0.5 * LRU_C * sp
    la = tr * half + half
    a = jnp.exp(la)
    q = _one_minus_sq(a, la)
    rs = lax.rsqrt(jnp.maximum(q, 1e-30))
    return a, q * rs, rs, tr, ti, sp


def _adamw(w, g, m, v):
    m2 = ADAM_B1 * m + (1.0 - ADAM_B1) * g
    v2 = ADAM_B2 * v + (1.0 - ADAM_B2) * (g * g)
    m_hat = m2 / ADAM_C1
    v_hat = v2 / ADAM_C2
    delta = -ADAM_LR * (m_hat / (jnp.sqrt(v_hat) + ADAM_EPS) + ADAM_WD * w)
    return delta, m2, v2


def _mod_forward(c8, cctx8, w_ada, small):
    d = c8.shape[1]
    cols = w_ada.shape[1]

    def body(c_ref, cctx_ref, w_ref, sm_ref, mod_ref, s_ref, sm_all, cbuf, mod_my, send_sems, recv_sems):
        _exchange_vmem(sm_ref, sm_all, send_sems, recv_sems, 2 * (NDEV - 1))
        _exchange_vmem(c_ref, cbuf, send_sems, recv_sems, 0)
        row = _rows((8, d))
        c_all = jnp.zeros((8, d), F32)
        for b in range(NDEV):
            c_all = jnp.where(row == b, cbuf[b], c_all)
        cc = cctx_ref[...]
        s_top = c_all * _sigmoid(c_all)
        s_bot = jnp.where(row == 0, cc * _sigmoid(cc), 0.0)
        s = jnp.concatenate([s_top, s_bot], axis=0)
        s_ref[...] = s
        mod_my[...] = jnp.dot(s, w_ref[...], precision=HIGHEST, preferred_element_type=F32)
        _exchange_vmem(mod_my, mod_ref, send_sems, recv_sems, NDEV - 1)

    return _call(
        body, name="mod_forward",
        out_shape=(jax.ShapeDtypeStruct((NDEV, 16, cols), F32), jax.ShapeDtypeStruct((16, d), F32),
                   jax.ShapeDtypeStruct((NDEV,) + small.shape, F32)),
        in_specs=[VMEM] * 4, out_specs=(VMEM,) * 3,
        scratch_shapes=[pltpu.VMEM((NDEV, 8, d), F32), pltpu.VMEM((16, cols), F32),
                        pltpu.SemaphoreType.DMA((3 * (NDEV - 1),)), pltpu.SemaphoreType.DMA((3 * (NDEV - 1),))],
        compiler_params=_params(),
    )(c8, cctx8, w_ada, small)


def _scatter_copies(src_ref, dst_ref, send_sems, recv_sems):
    me = _idx(_my_pos())
    copies = [pltpu.make_async_copy(src_ref.at[me], dst_ref.at[0], send_sems.at[0])]
    for k in range(1, NDEV):
        peer = _peer(k)
        copies.append(pltpu.make_async_remote_copy(
            src_ref=src_ref.at[_idx(peer)], dst_ref=dst_ref.at[k], send_sem=send_sems.at[k],
            recv_sem=recv_sems.at[k], device_id=peer, device_id_type=MESH))
    return copies


def _gather_copies(src_ref, dst_ref, send_sems, recv_sems):
    me = _idx(_my_pos())
    sends = [pltpu.make_async_copy(src_ref, dst_ref.at[me], send_sems.at[0])]
    arrivals = []
    for k in range(1, NDEV):
        peer = _peer(k)
        sends.append(pltpu.make_async_remote_copy(
            src_ref=src_ref, dst_ref=dst_ref.at[me], send_sem=send_sems.at[k],
            recv_sem=recv_sems.at[k], device_id=peer, device_id_type=MESH))
        arrivals.append(pltpu.make_async_remote_copy(
            src_ref=src_ref, dst_ref=dst_ref.at[_idx(peer)], send_sem=send_sems.at[k],
            recv_sem=recv_sems.at[k], device_id=peer, device_id_type=MESH))
    return sends, arrivals


def _exchange_wait(sends, arrivals):
    sends[0].wait()
    for cp in arrivals:
        cp.wait_recv()
    for cp in sends[1:]:
        cp.wait_send()


def _chip_order(k, c):
    return (6, 4 - 2 * c, 2 + 2 * c, 0)[k]


def _scatter_order(s, c):
    k = s >> 1
    mine = jnp.where(k == 0, 6, jnp.where(k == 1, 4 - 2 * c, jnp.where(k == 2, 2 + 2 * c, 0)))
    theirs = jnp.where(k == 0, 6, jnp.where(k == 1, 2 + 2 * c, jnp.where(k == 2, 4 - 2 * c, 0))) ^ 1
    return jnp.where((s & 1) == 0, theirs, mine)


def _peer_at(dist):
    x, y, c = _my_pos()
    return (x ^ ((dist >> 2) & 1), y ^ ((dist >> 1) & 1), c ^ (dist & 1))


def _reduce_small(packed, lru_parts, w_ada, dsilu_cctx):
    rp = packed.shape[0]
    rl = lru_parts.shape[1]
    d, cols = w_ada.shape
    assert cols % 128 == 0
    cb = cols // 128

    def body(p_ref, l_ref, w_ref, ds_ref, sum_ref, all_ref, lru_ref, cctx_ref,
             lbuf, lsum, cpart, call, send_sems, recv_sems, lsend, lrecv):
        me = _idx(_my_pos())
        scattered = _scatter_copies(l_ref, lbuf, lsend, lrecv)
        for cp in scattered:
            cp.start()
        _exchange_vmem(p_ref, all_ref, send_sems, recv_sems, 0)
        acc = all_ref[0]
        for j in range(1, NDEV):
            acc = acc + all_ref[j]
        sum_ref[...] = acc
        _exchange_wait(scattered, scattered[1:])
        red = lbuf[0]
        for k in range(1, NDEV):
            red = red + lbuf[k]
        lsum[...] = red
        lru_gather = _exchange_start(lsum, lru_ref, send_sems, recv_sems, NDEV - 1)
        part = jnp.zeros((8, d), F32)
        for q in range(cb):
            dm = jnp.broadcast_to(sum_ref[pl.ds((NDEV + me) * cb + q, 1), :], (8, 128))
            part = part + lax.dot_general(dm, w_ref[:, q * 128:(q + 1) * 128],
                                          (((1,), (1,)), ((), ())), precision=HIGHEST,
                                          preferred_element_type=F32)
        cpart[...] = part
        _exchange_vmem(cpart, call, send_sems, recv_sems, 2 * (NDEV - 1))
        _exchange_finish(lru_gather)
        tot = call[0]
        for j in range(1, NDEV):
            tot = tot + call[j]
        cctx_ref[...] = tot * ds_ref[...]

    return _call(
        body, name="reduce_small",
        out_shape=(jax.ShapeDtypeStruct((rp, 128), F32), jax.ShapeDtypeStruct((NDEV, rp, 128), F32),
                   jax.ShapeDtypeStruct((NDEV, rl, 128), F32), jax.ShapeDtypeStruct((8, d), F32)),
        in_specs=[VMEM] * 4, out_specs=(VMEM,) * 4,
        scratch_shapes=[pltpu.VMEM((NDEV, rl, 128), F32), pltpu.VMEM((rl, 128), F32), pltpu.VMEM((8, d), F32),
                        pltpu.VMEM((NDEV, 8, d), F32),
                        pltpu.SemaphoreType.DMA((3 * (NDEV - 1),)), pltpu.SemaphoreType.DMA((3 * (NDEV - 1),)),
                        pltpu.SemaphoreType.DMA((NDEV,)), pltpu.SemaphoreType.DMA((NDEV,))],
        compiler_params=_params(),
    )(packed, lru_parts, w_ada, dsilu_cctx)


def _normalize(src, mv, la, row0, tm, name, prev=None):
    rows, d = src.shape
    blk0 = row0 // tm

    def body(*refs):
        x_ref, mv_ref, h_ref = refs[0], refs[1], refs[-1]
        xf = x_ref[...]
        r = lax.rsqrt(jnp.mean(xf * xf, axis=-1, keepdims=True) + EPS)
        h = xf * r * (mv_ref[0:1, :] * (1.0 + mv_ref[1:2, :])) + mv_ref[2:3, :]
        h_ref[...] = h.astype(BF16)

    in_specs = [pl.BlockSpec((tm, d), lambda i: (i, 0)), pl.BlockSpec((8, d), lambda i: (0, 0))]
    args = [src, mv]
    aliases = {}
    if prev is not None:
        in_specs += [ANY]
        args += [prev]
        aliases = {2: 0}
    return _call(
        body, name=name,
        grid=(rows // tm,),
        out_shape=jax.ShapeDtypeStruct((la, d), BF16),
        in_specs=in_specs,
        out_specs=pl.BlockSpec((tm, d), lambda i: (blk0 + i, 0)),
        input_output_aliases=aliases,
        compiler_params=_params(("arbitrary",)),
    )(*args)


def _gather_order(step):
    return (step & 1) | (((step >> 2) & 1) << 1) | (((step >> 1) & 1) << 2)


def _in_projection(h, w_shard, tm):
    la, d = h.shape
    bw = w_shard.shape[1]
    ni = la // tm
    where = jnp.reshape(_idx(_my_pos()), (1,)).astype(jnp.int32)

    def body(me_ref, h_ref, w_ref, p_ref, all_ref, wbuf, send_sems, recv_sems, local_sems):
        s, i = pl.program_id(0), pl.program_id(1)
        x, y, c = _my_pos()
        me, sibling = (x, y, c), (x, y, 1 - c)
        chips = [(1 - x, y), (x, 1 - y), (1 - x, 1 - y)]

        def copy(k, block, to, from_shard=False):
            return pltpu.make_async_remote_copy(
                src_ref=w_ref if from_shard else all_ref.at[_idx(block)], dst_ref=all_ref.at[_idx(block)],
                send_sem=send_sems.at[k], recv_sem=recv_sems.at[k], device_id=to, device_id_type=MESH)

        def load(block, slot):
            return pltpu.make_async_copy(all_ref.at[_idx(block)], wbuf.at[slot], local_sems.at[1])

        keep = pltpu.make_async_copy(w_ref, all_ref.at[_idx(me)], local_sems.at[0])
        first = [copy(0, me, sibling, True)] + [copy(1 + j, me, (*chip, c), True) for j, chip in enumerate(chips)]
        passed = [copy(4 + j, (*chip, c), sibling) for j, chip in enumerate(chips)]
        steps = [(copy(0, sibling, me), None, sibling)]
        for j, chip in enumerate(chips):
            steps.append((copy(1 + j, (*chip, c), me), passed[j], (*chip, c)))
            steps.append((copy(4 + j, (*chip, 1 - c), me), None, (*chip, 1 - c)))

        @pl.when((s == 0) & (i == 0))
        def _():
            keep.start()
            mine = pltpu.make_async_copy(w_ref, wbuf.at[0], local_sems.at[1])
            mine.start()
            for cp in first:
                cp.start()
            mine.wait()

        for n, (arrival, forward, block) in enumerate(steps, start=1):
            @pl.when((s == n - 1) & (i == ni - 1))
            def _(arrival=arrival, forward=forward, block=block, n=n):
                arrival.wait_recv()
                if forward is not None:
                    forward.start()
                load(block, n % 2).start()

        @pl.when((s > 0) & (i == 0))
        def _():
            load(me, s % 2).wait()

        p_ref[...] = _dot(h_ref[...], wbuf[s % 2]).astype(BF16)

        @pl.when((s == NDEV - 1) & (i == ni - 1))
        def _():
            for cp in first + passed:
                cp.wait_send()
            keep.wait()

    return _call(
        body, name="in_projection",
        grid_spec=pltpu.PrefetchScalarGridSpec(
            num_scalar_prefetch=1, grid=(NDEV, ni),
            in_specs=[pl.BlockSpec((tm, d), lambda s, i, me_ref: (i, 0)), ANY],
            out_specs=(pl.BlockSpec((tm, bw), lambda s, i, me_ref: (i, me_ref[0] ^ _gather_order(s))), ANY),
            scratch_shapes=[pltpu.VMEM((2, d, bw), BF16), pltpu.SemaphoreType.DMA((7,)),
                            pltpu.SemaphoreType.DMA((7,)), pltpu.SemaphoreType.DMA((2,))]),
        out_shape=(jax.ShapeDtypeStruct((la, NDEV * bw), BF16), jax.ShapeDtypeStruct((NDEV, d, bw), BF16)),
        compiler_params=_params(("arbitrary", "arbitrary")),
    )(where, h, w_shard)


def _conv_input(p, wcb, taps_m, l, t):
    la = p.shape[0]
    w = wcb.shape[1]
    nt = l // t

    def body(v_ref, wcb_ref, tm_ref, xb_ref):
        taps = _dot(tm_ref[...].reshape(4 * t, t), v_ref[...])
        xb = wcb_ref[4:5, :] + wcb_ref[0:1, :] * taps[0:t]
        for j in range(1, 4):
            xb = xb + wcb_ref[j:j + 1, :] * taps[j * t:(j + 1) * t]
        xb_ref[...] = xb

    return _call(
        body, name="conv_input",
        grid=(nt + 1,),
        out_shape=jax.ShapeDtypeStruct((la, w), F32),
        in_specs=[pl.BlockSpec((t, w), lambda i: (i, 4)), pl.BlockSpec((8, w), lambda i: (0, 0)),
                  pl.BlockSpec((None, 4, t, t), lambda i: (i // nt, 0, 0, 0))],
        out_specs=pl.BlockSpec((t, w), lambda i: (i, 0)),
        compiler_params=_params(("arbitrary",)),
    )(p, wcb, taps_m)


def _lru_forward(xb, wg, lv, wo_shard, l, t):
    la, w = xb.shape
    gc = wg.shape[2]
    nt = l // t

    def body(xf_ref, xr_ref, wg_ref, lv_ref, wo_ref, hf_ref, hr_ref, wo_all,
             a_s, b_s, carry, send_sems, recv_sems):
        sends, arrivals = _gather_copies(wo_ref, wo_all, send_sems, recv_sems)

        @pl.when(pl.program_id(0) == 0)
        def _():
            carry[...] = jnp.zeros_like(carry)
            for cp in sends:
                cp.start()

        @pl.when(pl.program_id(0) == nt)
        def _():
            _exchange_wait(sends, arrivals)

        for dr, (x_ref, h_ref) in enumerate(((xf_ref, hf_ref), (xr_ref, hr_ref))):
            x = x_ref[...]
            a, s, _, _, ti, _ = _lru_coef(x, wg_ref, dr, lv_ref[3 * dr:3 * dr + 1, :],
                                          lv_ref[3 * dr + 1:3 * dr + 2, :], lv_ref[3 * dr + 2:3 * dr + 3, :], gc)
            a_s[...] = a
            b_s[...] = (s * x) * (0.5 * ti + 0.5)
            carry[dr] = _scan_tile(a_s, b_s, h_ref, carry[dr], dr == 1)

    full = lambda shape: pl.BlockSpec(shape, lambda i: (0,) * len(shape))
    fmap = lambda i: (jnp.where(i == 0, nt, i - 1), 0)
    rmap = lambda i: (jnp.where(i == 0, nt, nt - i), 0)
    return _call(
        body, name="lru_forward",
        grid=(nt + 1,),
        out_shape=(jax.ShapeDtypeStruct((la, w), F32), jax.ShapeDtypeStruct((la, w), F32),
                   jax.ShapeDtypeStruct((NDEV,) + wo_shard.shape, wo_shard.dtype)),
        in_specs=[pl.BlockSpec((t, w), fmap), pl.BlockSpec((t, w), rmap), full(wg.shape), full(lv.shape), ANY],
        out_specs=(pl.BlockSpec((t, w), fmap), pl.BlockSpec((t, w), rmap), ANY),
        scratch_shapes=[pltpu.VMEM((t, w), F32), pltpu.VMEM((t, w), F32), pltpu.VMEM((2, 8, w), F32),
                        pltpu.SemaphoreType.DMA((NDEV,)), pltpu.SemaphoreType.DMA((NDEV,))],
        compiler_params=_params(("arbitrary",)),
    )(xb, xb, wg, lv, wo_shard)


def _mix_gates(p_refs, hf_ref, hr_ref, wca_ref, perm_ref, t, w):
    bl, cl, ul, gl, ql = [r[...].astype(F32) for r in p_refs]
    tt = cl * ul
    tt16 = tt.astype(BF16)
    beside = _dot(perm_ref[2:4].reshape(2 * t, t), tt16)
    before, after = beside[:t], beside[t:]
    z = wca_ref[0:1, :] * before + wca_ref[1:2, :] * tt + wca_ref[2:3, :] * after
    sig_g = _sigmoid(gl)
    sig_q = _sigmoid(ql)
    ylru = _dot(perm_ref[1], (hf_ref[...] + hr_ref[...]).astype(BF16))
    return bl, cl, ul, gl, ql, (before, tt, after), z, sig_g, sig_q, ylru


def _p_specs(t, w, nt):
    return [pl.BlockSpec((t, w), functools.partial(lambda i, s: (jnp.minimum(i, nt - 1), s), s=s))
            for s in (0, 1, 2, 3, 5)]


def _mix_forward(x, tgt, p, hf, hr, wo, ov, wca, perm, t):
    l, d = x.shape
    w = d // 2
    nt = l // t

    def body(x_ref, tg_ref, b_ref, c_ref, u_ref, g_ref, q_ref, hf_ref, hr_ref, wo_ref, ov_ref, wca_ref, perm_ref,
             dn_ref, ct_ref, do_ref, part_ref):
        i = pl.program_id(0)
        bl, _, _, gl, ql, _, z, sig_g, sig_q, ylru = _mix_gates(
            (b_ref, c_ref, u_ref, g_ref, q_ref), hf_ref, hr_ref, wca_ref, perm_ref, t, w)
        ya = bl * z * (gl * sig_g)
        yb = ylru * (ql * sig_q)
        ct_ref[:, 0:w] = ya.astype(BF16)
        ct_ref[:, w:] = yb.astype(BF16)
        out = _dot(ya.astype(BF16), wo_ref[0:w, :]) + _dot(yb.astype(BF16), wo_ref[w:, :])
        gate, fg = ov_ref[0:1, :], ov_ref[1:2, :]
        n = x_ref[...] + gate * out
        rr = lax.rsqrt(jnp.mean(n * n, axis=-1, keepdims=True) + EPS)
        nh = n * rr
        e = nh * fg - tg_ref[...]
        loss = 0.5 * jnp.sum(jnp.mean(e * e, axis=-1, keepdims=True), axis=0, keepdims=True)
        dy = e * (1.0 / d)
        dnh = dy * fg
        dn = rr * (dnh - nh * jnp.mean(dnh * nh, axis=-1, keepdims=True))
        dn_ref[...] = dn.astype(BF16)
        do_ref[...] = (dn * gate).astype(BF16)

        @pl.when(i == 0)
        def _():
            part_ref[...] = jnp.zeros_like(part_ref)

        part_ref[0:1, :] += jnp.sum(dy * nh, axis=0, keepdims=True)
        part_ref[1:2, :] += jnp.sum(dn * out, axis=0, keepdims=True)
        part_ref[2:3, :] += jnp.broadcast_to(loss, (1, d))

    tile = lambda cols: pl.BlockSpec((t, cols), lambda i: (i, 0))
    full = lambda shape: pl.BlockSpec(shape, lambda i: (0,) * len(shape))
    return _call(
        body, name="mix_forward",
        grid=(nt,),
        out_shape=(jax.ShapeDtypeStruct((l, d), BF16), jax.ShapeDtypeStruct((l, d), BF16),
                   jax.ShapeDtypeStruct((l, d), BF16), jax.ShapeDtypeStruct((8, d), F32)),
        in_specs=[tile(d), tile(d)] + _p_specs(t, w, nt) + [tile(w), tile(w),
                  pl.BlockSpec((d, d), lambda i: (0, 0), pipeline_mode=pl.Buffered(1)),
                  full(ov.shape), full(wca.shape), full(perm.shape)],
        out_specs=(tile(d), tile(d), tile(d), full((8, d))),
        compiler_params=_params(("arbitrary",)),
    )(x, tgt, p, p, p, p, p, hf, hr, wo, ov, wca, perm)


def _mix_backward(dout, p, hf, hr, wo, wca, perm, g_wout, l, t):
    d = dout.shape[1]
    w = d // 2
    nt = l // t
    la = p.shape[0]

    def body(do_ref, b_ref, c_ref, u_ref, g_ref, q_ref, hf_ref, hr_ref, wo_ref, wca_ref, perm_ref, gw_ref,
             dp_ref, dh_ref, part_ref, sc_ref, send_sems, recv_sems):
        i = pl.program_id(0)
        copies = _scatter_copies(gw_ref, sc_ref, send_sems, recv_sems)

        @pl.when(i == 0)
        def _():
            part_ref[...] = jnp.zeros_like(part_ref)
            for cp in copies:
                cp.start()

        @pl.when(i == nt)
        def _():
            dp_ref[...] = jnp.zeros_like(dp_ref)
            _exchange_wait(copies, copies[1:])

        @pl.when(i < nt)
        def _():
            bl, cl, ul, gl, ql, taps, z, sig_g, sig_q, ylru = _mix_gates(
                (b_ref, c_ref, u_ref, g_ref, q_ref), hf_ref, hr_ref, wca_ref, perm_ref, t, w)
            do = do_ref[...]
            dya = _dot_nt(do, wo_ref[0:w, :])
            dyb = _dot_nt(do, wo_ref[w:, :])
            sg = gl * sig_g
            dz = dya * bl * sg
            dz16 = dz.astype(BF16)
            beside = _dot(perm_ref[2:4].reshape(2 * t, t), dz16)
            dt = wca_ref[0:1, :] * beside[t:] + wca_ref[1:2, :] * dz + wca_ref[2:3, :] * beside[:t]
            dp_ref[:, 0:w] = (dya * z * sg).astype(BF16)
            dp_ref[:, w:2 * w] = (dt * ul).astype(BF16)
            dp_ref[:, 2 * w:3 * w] = (dt * cl).astype(BF16)
            dp_ref[:, 3 * w:4 * w] = (dya * bl * z * (sig_g * (1.0 + gl * (1.0 - sig_g)))).astype(BF16)
            dp_ref[:, 4 * w:5 * w] = jnp.zeros((t, w), BF16)
            dp_ref[:, 5 * w:6 * w] = (dyb * ylru * (sig_q * (1.0 + ql * (1.0 - sig_q)))).astype(BF16)
            dh_ref[...] = _dot(perm_ref[0], (dyb * (ql * sig_q)).astype(BF16)).astype(BF16)
            for j in range(3):
                part_ref[j:j + 1, :] += jnp.sum(dz * taps[j], axis=0, keepdims=True)

    clamp = lambda cols: pl.BlockSpec((t, cols), lambda i: (jnp.minimum(i, nt - 1), 0))
    full = lambda shape: pl.BlockSpec(shape, lambda i: (0,) * len(shape))
    return _call(
        body, name="mix_backward",
        grid=(nt + 1,),
        out_shape=(jax.ShapeDtypeStruct((la, 6 * w), BF16), jax.ShapeDtypeStruct((l, w), BF16),
                   jax.ShapeDtypeStruct((8, w), F32), jax.ShapeDtypeStruct(g_wout.shape, g_wout.dtype)),
        in_specs=[clamp(d)] + _p_specs(t, w, nt) + [clamp(w), clamp(w),
                  pl.BlockSpec((d, d), lambda i: (0, 0), pipeline_mode=pl.Buffered(1)), full(wca.shape),
                  full(perm.shape), ANY],
        out_specs=(pl.BlockSpec((t, 6 * w), lambda i: (i, 0)), clamp(w), full((8, w)), ANY),
        scratch_shapes=[pltpu.SemaphoreType.DMA((NDEV,)), pltpu.SemaphoreType.DMA((NDEV,))],
        compiler_params=_params(("arbitrary",)),
    )(dout, p, p, p, p, p, hf, hr, wo, wca, perm, g_wout)


def _lru_backward(direction, xb, dhs, hs, wg, lv, l, t, conv=None):
    la, w = hs.shape
    gc = wg.shape[2]
    ng = w // gc
    nt = l // t
    nblk8 = la // 8
    last = conv is not None
    assert last == (direction == 1)

    if direction == 0:
        tile = lambda i: jnp.where(i == nt, nt, nt - 1 - i)
        halo = lambda i: jnp.where(tile(i) == 0, nblk8 - 1, tile(i) * (t // 8) - 1)
    else:
        tile = lambda i: i
        halo = lambda i: jnp.minimum((i + 1) * (t // 8), nblk8 - 1)

    def body(*refs):
        x_ref, dh_ref, hs_ref, halo_ref, wg_ref, lv_ref = refs[:6]
        if last:
            v_ref, wcb_ref, bm_ref, dxo_ref = refs[6:10]
        out_ref, dwg_ref, part_ref, a_s, dh_s, g_s, carry = refs[-7:]
        i = pl.program_id(0)
        is_ctx = i == nt

        @pl.when(i == 0)
        def _():
            carry[...] = jnp.zeros_like(carry)
            dwg_ref[...] = jnp.zeros_like(dwg_ref)
            part_ref[...] = jnp.zeros_like(part_ref)

        xb = x_ref[...]
        lam = lv_ref[3 * direction + 2:3 * direction + 3, :]
        a, s, rs, tr, ti, sp = _lru_coef(xb, wg_ref, direction, lv_ref[3 * direction:3 * direction + 1, :],
                                         lv_ref[3 * direction + 1:3 * direction + 2, :], lam, gc)
        hs_t = hs_ref[...]
        r8 = _rows((8, w))
        if direction == 0:
            edge = jnp.where(is_ctx, 0.0, halo_ref[7:8, :])
            first = jnp.where(r8 == 0, edge, pltpu.roll(hs_t[t - 8:, :], 1, 0))
            hprev = jnp.concatenate([first, hs_t[:t - 8, :]], axis=0)
        else:
            edge = jnp.where(is_ctx, 0.0, halo_ref[0:1, :])
            final = jnp.where(r8 == 7, edge, pltpu.roll(hs_t[:8, :], 7, 0))
            hprev = jnp.concatenate([hs_t[8:, :], final], axis=0)
        a_s[...] = a
        dh_s[...] = jnp.where(is_ctx, 0.0, dh_ref[...].astype(F32))
        carry[...] = _scan_tile_backward(a_s, dh_s, g_s, carry[...], direction == 0)

        g = g_s[...]
        r = 0.5 * tr + 0.5
        ig = 0.5 * ti + 0.5
        ix = ig * xb
        gs = g * s
        dla = (g * a) * (hprev - ix * (a * rs))
        dxb = gs * ig
        dzr = dla * (r * (1.0 - tr)) * (-LRU_C * sp)
        dzi = gs * ix * (1.0 - ti)
        part_ref[0:1, :] += jnp.sum(dzr, axis=0, keepdims=True)
        part_ref[1:2, :] += jnp.sum(dzi, axis=0, keepdims=True)
        part_ref[2:3, :] += jnp.sum(dla * r, axis=0, keepdims=True) * (LRU_C * _sigmoid(-lam))
        pieces = []
        for gi in range(ng):
            sl = slice(gi * gc, (gi + 1) * gc)
            dz = jnp.concatenate([dzr[:, sl], dzi[:, sl]], axis=-1).astype(BF16)
            pieces.append(_dot_nt(dz, wg_ref[direction, gi]))
            dwg_ref[gi] += _dot(xb[:, sl].T.astype(BF16), dz)
        dxb = dxb + (pieces[0] if ng == 1 else jnp.concatenate(pieces, axis=-1))
        if not last:
            out_ref[...] = dxb
        else:
            dxb = dxb + dxo_ref[...]
            v = v_ref[...].astype(F32)
            backs = _dot(bm_ref[...].reshape(4 * t, t), dxb.astype(BF16))
            dv = jnp.zeros((t, w), F32)
            for j in range(4):
                back = backs[j * t:(j + 1) * t]
                dv = dv + wcb_ref[j:j + 1, :] * back
                part_ref[4 + j:5 + j, :] += jnp.sum(back * v, axis=0, keepdims=True)
            out_ref[...] = dv.astype(BF16)
            part_ref[3:4, :] += jnp.sum(dxb, axis=0, keepdims=True)

    full = lambda shape: pl.BlockSpec(shape, lambda i: (0,) * len(shape))
    kind = lambda i: (jnp.where(i == nt, 1, 0), 0, 0, 0)
    in_specs = [pl.BlockSpec((t, w), lambda i: (tile(i), 0)),
                pl.BlockSpec((t, w), lambda i: (jnp.minimum(tile(i), nt - 1), 0)),
                pl.BlockSpec((t, w), lambda i: (tile(i), 0)),
                pl.BlockSpec((8, w), lambda i: (halo(i), 0)),
                full(wg.shape), full(lv.shape)]
    args = [xb, dhs, hs, hs, wg, lv]
    if last:
        p, wcb, back_m, dxb_other, dp = conv
        in_specs += [pl.BlockSpec((t, w), lambda i: (tile(i), 4)), full(wcb.shape),
                     pl.BlockSpec((None, 4, t, t), kind), pl.BlockSpec((t, w), lambda i: (tile(i), 0)), ANY]
        args += [p, wcb, back_m, dxb_other, dp]
        out0 = jax.ShapeDtypeStruct(dp.shape, dp.dtype)
        spec0 = pl.BlockSpec((t, w), lambda i: (tile(i), 4))
        aliases = {10: 0}
    else:
        out0 = jax.ShapeDtypeStruct((la, w), F32)
        spec0 = pl.BlockSpec((t, w), lambda i: (tile(i), 0))
        aliases = {}
    return _call(
        body, name="lru_backward_%d" % direction,
        grid=(nt + 1,),
        out_shape=(out0, jax.ShapeDtypeStruct((ng, gc, 2 * gc), F32), jax.ShapeDtypeStruct((8, w), F32)),
        in_specs=in_specs,
        out_specs=(spec0, full((ng, gc, 2 * gc)), full((8, w))),
        scratch_shapes=[pltpu.VMEM((t, w), F32), pltpu.VMEM((t, w), F32), pltpu.VMEM((t, w), F32),
                        pltpu.VMEM((8, w), F32)],
        input_output_aliases=aliases,
        compiler_params=_params(("arbitrary",)),
    )(*args)


def _weight_grad_t(a, b, nblk_m, nblk_n, tk, name):
    k, m = a.shape
    n = b.shape[1]
    bm, bn = m // nblk_m, n // nblk_n
    nk = k // tk

    def body(a_ref, b_ref, o_ref, acc):
        kk = pl.program_id(2)

        @pl.when(kk == 0)
        def _():
            acc[...] = jnp.zeros_like(acc)

        acc[...] += lax.dot_general(a_ref[...], b_ref[...], (((0,), (0,)), ((), ())), preferred_element_type=F32)

        @pl.when(kk == nk - 1)
        def _():
            o_ref[...] = acc[...].astype(BF16)

    return _call(
        body, name=name,
        grid=(nblk_m, nblk_n, nk),
        out_shape=jax.ShapeDtypeStruct((nblk_m * nblk_n, bm, bn), BF16),
        in_specs=[pl.BlockSpec((tk, bm), lambda i, j, kk: (kk, i)),
                  pl.BlockSpec((tk, bn), lambda i, j, kk: (kk, j))],
        out_specs=pl.BlockSpec((None, bm, bn), lambda i, j, kk: (i * nblk_n + j, 0, 0)),
        scratch_shapes=[pltpu.VMEM((bm, bn), F32)],
        compiler_params=_params(("arbitrary", "arbitrary", "arbitrary")),
    )(a, b)


def _weight_grad_scatter(at, b, tk, name):
    k, m = at.shape
    n = b.shape[1]
    bn = n // NDEV
    nk = k // tk
    where = jnp.stack([_idx(_my_pos()), lax.axis_index("c")]).astype(jnp.int32)
    tn = (((0,), (0,)), ((), ()))

    def body(w_ref, a_ref, b_ref, recv_ref, acc, sbuf, sib, sib_send, sib_recv, chip_send, chip_recv, keep_sem):
        s, kk = pl.program_id(0), pl.program_id(1)
        x, y, c = _my_pos()

        @pl.when(kk == 0)
        def _():
            acc[...] = lax.dot_general(a_ref[...], b_ref[...], tn, preferred_element_type=F32)

        @pl.when(kk > 0)
        def _():
            acc[...] += lax.dot_general(a_ref[...], b_ref[...], tn, preferred_element_type=F32)

        def to_sibling(j):
            return pltpu.make_async_remote_copy(
                src_ref=sbuf.at[0], dst_ref=sib.at[j], send_sem=sib_send.at[j], recv_sem=sib_recv.at[j],
                device_id=(x, y, 1 - c), device_id_type=MESH)

        def to_chip(j):
            dist = _chip_order(j, c)
            return pltpu.make_async_remote_copy(
                src_ref=sbuf.at[1], dst_ref=recv_ref.at[dist // 2], send_sem=chip_send.at[j],
                recv_sem=chip_recv.at[dist // 2], device_id=_peer_at(dist), device_id_type=MESH)

        keep = pltpu.make_async_copy(sbuf.at[1], recv_ref.at[0], keep_sem)
        sends = []
        for j in range(4):
            sends += [to_sibling(j), to_chip(j) if j < 3 else keep]

        for st in range(NDEV):
            @pl.when((kk == nk - 1) & (s == st))
            def _(st=st):
                if st >= 2:
                    sends[st - 2].wait_send()
                part = acc[...]
                if st % 2 == 1:
                    to_sibling(st // 2).wait_recv()
                    part = part + sib[st // 2].astype(F32)
                sbuf[st % 2] = part.astype(BF16)
                sends[st].start()
                if st == NDEV - 1:
                    sends[st - 1].wait_send()
                    sends[st].wait()
                    for j in range(1, 4):
                        pltpu.make_async_remote_copy(
                            src_ref=sbuf.at[0], dst_ref=recv_ref.at[j], send_sem=chip_send.at[0],
                            recv_sem=chip_recv.at[j], device_id=_peer_at(2 * j), device_id_type=MESH).wait_recv()

    blk = lambda s, w_ref: w_ref[0] ^ _scatter_order(s, w_ref[1])
    return _call(
        body, name=name,
        grid_spec=pltpu.PrefetchScalarGridSpec(
            num_scalar_prefetch=1, grid=(NDEV, nk),
            in_specs=[pl.BlockSpec((tk, m), lambda s, kk, w_ref: (kk, 0)),
                      pl.BlockSpec((tk, bn), lambda s, kk, w_ref: (kk, blk(s, w_ref)))],
            out_specs=ANY,
            scratch_shapes=[pltpu.VMEM((m, bn), F32), pltpu.VMEM((2, m, bn), BF16), pltpu.VMEM((4, m, bn), BF16),
                            pltpu.SemaphoreType.DMA((4,)), pltpu.SemaphoreType.DMA((4,)),
                            pltpu.SemaphoreType.DMA((4,)), pltpu.SemaphoreType.DMA((4,)),
                            pltpu.SemaphoreType.DMA]),
        out_shape=jax.ShapeDtypeStruct((4, m, bn), BF16),
        compiler_params=_params(("arbitrary", "arbitrary")),
    )(where, at, b)


def _input_backward(dp, w_all, src, mv, row0, tm, nbk, name, dn=None):
    rows, d = src.shape
    nb, _, bw = w_all.shape
    nk = nb // nbk
    ni = rows // tm
    blk0 = row0 // tm
    latent = dn is not None

    def body(*refs):
        dp_ref, w_ref, x_ref, mv_ref = refs[:4]
        outs = refs[4 + latent:]
        part_ref, acc = outs[latent], outs[latent + 1]
        i, k = pl.program_id(0), pl.program_id(1)

        def product():
            step = _dot_nt(dp_ref[:, 0:bw], w_ref[0])
            for q in range(1, nbk):
                step = step + _dot_nt(dp_ref[:, q * bw:(q + 1) * bw], w_ref[q])
            return step

        def finish(slot):
            xf = x_ref[...]
            r = lax.rsqrt(jnp.mean(xf * xf, axis=-1, keepdims=True) + EPS)
            xn = xf * r
            dhl = acc[slot]
            gain, sc = mv_ref[0:1, :], mv_ref[1:2, :]
            dhx = jnp.sum(dhl * xn, axis=0, keepdims=True)
            part_ref[0:1, :] += jnp.sum(dhl, axis=0, keepdims=True)
            part_ref[1:2, :] += dhx * gain
            part_ref[2:3, :] += dhx * (1.0 + sc)
            if latent:
                dxn = dhl * (gain * (1.0 + sc))
                outs[0][...] = (refs[4][...].astype(F32)
                                + r * (dxn - xn * jnp.mean(dxn * xn, axis=-1, keepdims=True)))

        @pl.when((i == 0) & (k == 0))
        def _():
            part_ref[...] = jnp.zeros_like(part_ref)
            acc[0] = product()

        @pl.when((i > 0) & (i < ni) & (k == 0))
        def _():
            acc[i % 2] = product()
            finish((i - 1) % 2)

        @pl.when((i == ni) & (k == 0))
        def _():
            finish((ni - 1) % 2)

        @pl.when((i < ni) & (k > 0))
        def _():
            acc[i % 2] += product()

    tile = pl.BlockSpec((tm, d), lambda i, k: (jnp.maximum(i - 1, 0), 0))
    vec = pl.BlockSpec((8, d), lambda i, k: (0, 0))
    kblock = lambda i, k: jnp.where(i == ni, nk - 1, k)
    return _call(
        body, name=name,
        grid=(ni + 1, nk),
        out_shape=((jax.ShapeDtypeStruct((rows, d), F32),) if latent else ()) + (jax.ShapeDtypeStruct((8, d), F32),),
        in_specs=[pl.BlockSpec((tm, nbk * bw), lambda i, k: (blk0 + jnp.minimum(i, ni - 1), kblock(i, k))),
                  pl.BlockSpec((nbk, d, bw), lambda i, k: (kblock(i, k), 0, 0)), tile, vec]
                 + ([tile] if latent else []),
        out_specs=((tile,) if latent else ()) + (vec,),
        scratch_shapes=[pltpu.VMEM((2, tm, d), F32)],
        compiler_params=_params(("arbitrary", "arbitrary")),
    )(*([dp, w_all, src, mv] + ([dn] if latent else [])))


def _adamw_scattered(parts, w, m, v, tr):
    r, c = w.shape
    nslot = parts.shape[0]

    def body(p_ref, w_ref, m_ref, v_ref, g_ref, d_ref, m2_ref, v2_ref):
        g = p_ref[0].astype(F32)
        for k in range(1, nslot):
            g = g + p_ref[k].astype(F32)
        g_ref[...] = g
        d_ref[...], m2_ref[...], v2_ref[...] = _adamw(w_ref[...], g, m_ref[...], v_ref[...])

    tile = pl.BlockSpec((tr, c), lambda i: (i, 0))
    return _call(
        body, name="adamw_scattered_%dx%d" % (r, c),
        grid=(r // tr,),
        out_shape=tuple(jax.ShapeDtypeStruct((r, c), F32) for _ in range(4)),
        in_specs=[pl.BlockSpec((nslot, tr, c), lambda i: (0, i, 0)), tile, tile, tile],
        out_specs=(tile,) * 4,
        compiler_params=_params(("arbitrary",)),
    )(parts, w, m, v)


def _adamw_ada(st, dmod, w, m, v, tr):
    r, c = w.shape

    def body(s_ref, dm_ref, w_ref, m_ref, v_ref, g_ref, d_ref, m2_ref, v2_ref):
        g = jnp.dot(s_ref[...], dm_ref[...], precision=HIGHEST, preferred_element_type=F32)
        g_ref[...] = g
        d_ref[...], m2_ref[...], v2_ref[...] = _adamw(w_ref[...], g, m_ref[...], v_ref[...])

    tile = pl.BlockSpec((tr, c), lambda i: (i, 0))
    return _call(
        body, name="adamw_ada",
        grid=(r // tr,),
        out_shape=tuple(jax.ShapeDtypeStruct((r, c), F32) for _ in range(4)),
        in_specs=[pl.BlockSpec((tr, 16), lambda i: (i, 0)), pl.BlockSpec((16, c), lambda i: (0, 0)),
                  tile, tile, tile],
        out_specs=(tile,) * 4,
        compiler_params=_params(("arbitrary",)),
    )(st, dmod, w, m, v)


def _adamw_small(gs, ws, ms, vs):
    n = len(ws)

    def body(*refs):
        for j in range(n):
            g_ref, w_ref, m_ref, v_ref = refs[j], refs[n + j], refs[2 * n + j], refs[3 * n + j]
            d_ref, m2_ref, v2_ref = refs[4 * n + j], refs[5 * n + j], refs[6 * n + j]
            d_ref[...], m2_ref[...], v2_ref[...] = _adamw(w_ref[...], g_ref[...], m_ref[...], v_ref[...])

    shapes = tuple(jax.ShapeDtypeStruct(a.shape, F32) for a in ws)
    out = _call(
        body, name="adamw_small",
        out_shape=shapes * 3,
        in_specs=[VMEM] * (4 * n), out_specs=(VMEM,) * (3 * n),
        compiler_params=_params(),
    )(*gs, *ws, *ms, *vs)
    return list(out[:n]), list(out[n:2 * n]), list(out[2 * n:])


def _blockdiag_groups(wh, gc):
    h, dh, _ = wh.shape
    g = gc // dh
    w4 = wh.reshape(h // g, g, dh, dh)
    bd = jnp.einsum("ngij,gh->ngihj", w4, jnp.eye(g, dtype=wh.dtype))
    return bd.reshape(h // g, gc, gc)


def _blockdiag_extract(bd, dh):
    ng, gc, _ = bd.shape
    g = gc // dh
    x = bd.reshape(ng, g, dh, g, dh)
    return jnp.einsum("ngihj,gh->ngij", x, jnp.eye(g, dtype=bd.dtype)).reshape(ng * g, dh, dh)


def _largest_tile(n, cap):
    return max(q for q in range(128, min(n, cap) + 1, 128) if n % q == 0)


def _rows8(*vecs):
    rows = [jnp.reshape(v, (1, -1)).astype(F32) for v in vecs]
    n = rows[0].shape[1]
    return jnp.concatenate(rows + [jnp.zeros((8 - len(rows), n), F32)], axis=0)


def _pack(pieces):
    flat = jnp.concatenate([jnp.reshape(a, (-1,)).astype(F32) for a in pieces])
    total = -(-flat.shape[0] // 1024) * 1024
    return jnp.pad(flat, (0, total - flat.shape[0])).reshape(total // 128, 128)


def _unpack(packed, shapes):
    flat = packed.reshape(-1)
    out, off = [], 0
    for s in shapes:
        n = 1
        for q in s:
            n *= q
        out.append(flat[off:off + n].reshape(s))
        off += n
    return out


def kernel(x, c, ctx, c_ctx, norm_g, w_ada, b_ada, w_in, w_conv_a, w_conv_b, b_conv_b, lru_wa, lru_ba, lru_wx, lru_bx, lru_lambda, w_out, final_g, loss_target, m_c_ctx, m_norm_g, m_w_ada, m_b_ada, m_w_in, m_w_conv_a, m_w_conv_b, m_b_conv_b, m_lru_wa, m_lru_ba, m_lru_wx, m_lru_bx, m_lru_lambda, m_w_out, m_final_g, v_c_ctx, v_norm_g, v_w_ada, v_b_ada, v_w_in, v_w_conv_a, v_w_conv_b, v_b_conv_b, v_lru_wa, v_lru_ba, v_lru_wx, v_lru_bx, v_lru_lambda, v_w_out, v_final_g):
    _, l, d = x.shape
    lc = ctx.shape[1]
    w = d // 2
    t = lc
    assert l % t == 0 and t % GRID_W == 0 and t % 128 == 0
    dh = w // N_HEADS
    gc = min(w, MXU_WIDTH)
    cols = w_ada.shape[2]
    wo_rows = w_out.shape[1]
    me = _idx(_my_pos())
    x2, ctx2, tgt2 = x[0], ctx[0], loss_target[0]
    w_ada2, w_in2, w_out2 = w_ada[0], w_in[0], w_out[0]

    small_mine = jnp.concatenate([a.reshape(-1) for a in (w_conv_a, w_conv_b, lru_ba, lru_bx, lru_lambda)]
                                 + [jnp.zeros((3 * (w // NDEV),), F32)]).reshape(16, w // NDEV)
    mod_all, s_mat, small_all = _mod_forward(
        jnp.broadcast_to(c, (8, d)), jnp.broadcast_to(c_ctx[None], (8, d)), w_ada2, small_mine)
    mod = jnp.transpose(mod_all, (1, 0, 2)).reshape(16, NDEV * cols) + b_ada
    mod_lat = lax.dynamic_slice_in_dim(mod, me, 1, axis=0)
    sh_l, sc_l, gt_l = jnp.split(mod_lat, 3, axis=-1)
    sh_c, sc_c, _ = jnp.split(mod[8:9], 3, axis=-1)
    small = jnp.transpose(small_all, (1, 0, 2)).reshape(16, w)
    wca = _rows8(*[small[j] for j in range(0, 3)])
    wcb = _rows8(*[small[j] for j in range(3, 7)], b_conv_b)
    lv = _rows8(0.5 * small[7], 0.5 * small[9], small[11], 0.5 * small[8], 0.5 * small[10], small[12])
    wg = jnp.stack([
        jnp.concatenate([_blockdiag_groups(lru_wa[0, dr], gc), _blockdiag_groups(lru_wx[0, dr], gc)], axis=-1)
        for dr in range(2)])
    wg = (0.5 * wg).astype(BF16)

    la = l + lc
    tm = 2 * t if l % (2 * t) == 0 else t
    tk = 3 * t if la % (3 * t) == 0 else t
    h = _normalize(x2, _rows8(norm_g, sc_l, sh_l), la, 0, tm, "normalize")
    h = _normalize(ctx2, _rows8(norm_g, sc_c, sh_c), la, l, t, "normalize_ctx", prev=h)
    p, w_all = _in_projection(h, w_in2.astype(BF16), la // 4 if la % 64 == 0 else tk)
    taps_m, back_m, perm = _scan_matrices(t)
    xb = _conv_input(p, wcb, taps_m, l, t)
    hf, hr, wo_all = _lru_forward(xb, wg, lv, w_out2.astype(BF16), l, t)
    wo = wo_all.reshape(d, d)
    dn, cat, dout, part_mix = _mix_forward(x2, tgt2, p, hf, hr, wo, _rows8(gt_l, final_g), wca, perm, t)
    g_wout = _weight_grad_t(cat, dout, 2, 1, _largest_tile(l, 2048), "grad_w_out")
    dp, dhs, part_ca, sc_wout = _mix_backward(dout, p, hf, hr, wo, wca, perm, g_wout.reshape(NDEV, wo_rows, d), l, t)
    dxb0, dwg0, part_l0 = _lru_backward(0, xb, dhs, hf, wg, lv, l, t)
    dp, dwg1, part_l1 = _lru_backward(1, xb, dhs, hr, wg, lv, l, t, conv=(p, wcb, back_m, dxb0, dp))
    sc_win = _weight_grad_scatter(h, dp, tk, "grad_w_in")
    grad_x, part_lat = _input_backward(dp, w_all, x2, _rows8(norm_g, sc_l), 0, tm, 2, "input_backward", dn=dn)
    (part_ctx,) = _input_backward(dp, w_all, ctx2, _rows8(norm_g, sc_c), l, t, 2, "input_backward_ctx")
    part_in = jnp.concatenate([part_lat[0:2], part_ctx[0:2], (part_lat[2] + part_ctx[2])[None]], axis=0)

    dwa = jnp.stack([_blockdiag_extract(dwg0[:, :, :gc], dh), _blockdiag_extract(dwg1[:, :, :gc], dh)])
    dwx = jnp.stack([_blockdiag_extract(dwg0[:, :, gc:], dh), _blockdiag_extract(dwg1[:, :, gc:], dh)])
    lru_part = (0.5 * jnp.stack([dwa, dwx])).reshape(NDEV, -1, 128)
    zeros_d = jnp.zeros((d,), F32)
    pieces = [
        jnp.concatenate([part_in[0], part_in[1], part_mix[1]]),
        jnp.concatenate([part_in[2], part_in[3], zeros_d]),
        part_in[4], part_mix[0], part_ca[0:3], part_l1[4:8], part_l1[3],
        0.5 * jnp.stack([part_l0[0], part_l1[0]]), 0.5 * jnp.stack([part_l0[1], part_l1[1]]),
        jnp.stack([part_l0[2], part_l1[2]]), part_mix[2, 0:1],
    ]
    shapes = [(3 * d,), (3 * d,), (d,), (d,), (3, w), (4, w), (w,), (2, w), (2, w), (2, w), (1,)]
    sig_cc = jax.nn.sigmoid(c_ctx)
    dsilu_cc = jnp.broadcast_to((sig_cc * (1.0 + c_ctx * (1.0 - sig_cc)))[None], (8, d))
    psum, pall, lru_sum, g_cctx8 = _reduce_small(_pack(pieces), lru_part, w_ada2, dsilu_cc)
    (g_modl, g_modc, g_norm, g_final, g_ca, g_cb, g_bcb, g_ba, g_bx, g_lam, loss1) = _unpack(psum, shapes)
    loss = loss1[0]
    g_cctx = g_cctx8[0]
    g_bada = (g_modl + g_modc)[None]
    g_lru = lru_sum.reshape(2, 2, N_HEADS, dh, dh)
    g_wa, g_wx = g_lru[0][None], g_lru[1][None]
    wsl = w // NDEV
    mine = lambda a: lax.dynamic_slice_in_dim(a, me * wsl, wsl, axis=-1)
    g_ca_m, g_cb_m, g_ba_m, g_bx_m, g_lam_m = (mine(g_ca)[None], mine(g_cb)[None], mine(g_ba)[None],
                                               mine(g_bx)[None], mine(g_lam)[None])
    g_norm, g_bcb = g_norm[None], g_bcb[None]

    cb = cols // 128
    per_dev = pall[:, :3 * d // 128].reshape(NDEV, NDEV, cols)
    dmod_lat = lax.dynamic_slice_in_dim(per_dev, me, 1, axis=1)[:, 0]
    dmod_ctx = lax.dynamic_slice_in_dim(g_modc.reshape(NDEV, cols), me, 1, axis=0)
    dmod16 = jnp.concatenate([dmod_lat, dmod_ctx, jnp.zeros((7, cols), F32)], axis=0)
    tr_ada = 256 if d % 256 == 0 else d
    g_wada, d_wada, m_wada, v_wada = _adamw_ada(s_mat.T, dmod16, w_ada2, m_w_ada[0], v_w_ada[0], tr_ada)
    g_win2, d_win, m_win, v_win = _adamw_scattered(sc_win, w_in2, m_w_in[0], v_w_in[0], tr_ada)
    tr_out = 64 if wo_rows % 64 == 0 else wo_rows
    g_wout2, d_wout, m_wout, v_wout = _adamw_scattered(sc_wout, w_out2, m_w_out[0], v_w_out[0], tr_out)

    small_w = [c_ctx, norm_g, b_ada, w_conv_a, w_conv_b, b_conv_b, lru_wa, lru_ba, lru_wx, lru_bx, lru_lambda, final_g]
    small_m = [m_c_ctx, m_norm_g, m_b_ada, m_w_conv_a, m_w_conv_b, m_b_conv_b, m_lru_wa, m_lru_ba, m_lru_wx,
               m_lru_bx, m_lru_lambda, m_final_g]
    small_v = [v_c_ctx, v_norm_g, v_b_ada, v_w_conv_a, v_w_conv_b, v_b_conv_b, v_lru_wa, v_lru_ba, v_lru_wx,
               v_lru_bx, v_lru_lambda, v_final_g]
    small_g = [g_cctx, g_norm, g_bada, g_ca_m, g_cb_m, g_bcb, g_wa, g_ba_m, g_wx, g_bx_m, g_lam_m, g_final]
    small_g = [jnp.reshape(a, b.shape) for a, b in zip(small_g, small_w)]
    d_s, m_s, v_s = _adamw_small(small_g, small_w, small_m, small_v)

    def weights(small_list, ada, win, wout):
        (cctx_, norm_, bada_, ca_, cb_, bcb_, wa_, ba_, wx_, bx_, lam_, final_) = small_list
        return [cctx_, norm_, ada[None], bada_, win[None], ca_, cb_, bcb_, wa_, ba_, wx_, bx_, lam_, wout[None], final_]

    return (loss, grad_x[None],
            *weights(small_g, g_wada, g_win2, g_wout2), *weights(d_s, d_wada, d_win, d_wout),
            *weights(m_s, m_wada, m_win, m_wout), *weights(v_s, v_wada, v_win, v_wout))
```

```python
import functools

import jax
import jax.numpy as jnp
import numpy as np
from jax import lax
from jax.experimental import pallas as pl
from jax.experimental.pallas import tpu as pltpu

F32 = jnp.float32
BF16 = jnp.bfloat16
MESH = pl.DeviceIdType.MESH
NDEV = 8
GRID_W = 64
N_HEADS = 16
LRU_C = 8.0
EPS = 1e-6
MXU_WIDTH = 256
VMEM_LIMIT = 60 * 1024 * 1024

ADAM_LR = 0.001
ADAM_B1 = 0.9
ADAM_B2 = 0.999
ADAM_EPS = 1e-08
ADAM_WD = 0.01
ADAM_STEP = 10
ADAM_C1 = 1.0 - ADAM_B1 ** ADAM_STEP
ADAM_C2 = 1.0 - ADAM_B2 ** ADAM_STEP

HIGHEST = lax.Precision.HIGHEST
ANY = pl.BlockSpec(memory_space=pl.ANY)
VMEM = pl.BlockSpec(memory_space=pltpu.VMEM)


def _call(body, **kw):
    return pl.pallas_call(body, **kw)


def _params(sem=None, vmem=VMEM_LIMIT):
    return pltpu.CompilerParams(dimension_semantics=sem, vmem_limit_bytes=vmem)


def _my_pos():
    return lax.axis_index("x"), lax.axis_index("y"), lax.axis_index("c")


def _idx(pos):
    return 4 * pos[0] + 2 * pos[1] + pos[2]


def _peer(k):
    x, y, c = _my_pos()
    return ((1 - x) if (k >> 2) & 1 else x, (1 - y) if (k >> 1) & 1 else y, (1 - c) if k & 1 else c)


def _exchange_start(src_ref, dst_ref, send_sems, recv_sems, base):
    me = _idx(_my_pos())
    sends = []
    for k in range(1, NDEV):
        cp = pltpu.make_async_remote_copy(
            src_ref=src_ref, dst_ref=dst_ref.at[me], send_sem=send_sems.at[base + k - 1],
            recv_sem=recv_sems.at[base + k - 1], device_id=_peer(k), device_id_type=MESH)
        cp.start()
        sends.append(cp)
    dst_ref[me] = src_ref[...]
    return sends, (src_ref, dst_ref, send_sems, recv_sems, base)


def _exchange_finish(started):
    sends, (src_ref, dst_ref, send_sems, recv_sems, base) = started
    for k in range(1, NDEV):
        peer = _peer(k)
        pltpu.make_async_remote_copy(
            src_ref=src_ref, dst_ref=dst_ref.at[_idx(peer)], send_sem=send_sems.at[base + k - 1],
            recv_sem=recv_sems.at[base + k - 1], device_id=peer, device_id_type=MESH).wait_recv()
    for cp in sends:
        cp.wait_send()


def _exchange_vmem(src_ref, dst_ref, send_sems, recv_sems, base):
    _exchange_finish(_exchange_start(src_ref, dst_ref, send_sems, recv_sems, base))


def _sigmoid(z):
    return 0.5 * jnp.tanh(0.5 * z) + 0.5


def _softplus(x):
    return jnp.maximum(x, 0.0) + jnp.log1p(jnp.exp(-jnp.abs(x)))


def _one_minus_sq(a, la):
    series = (-2.0 * la) * (1.0 + la)
    return jnp.where(la > -0.0015, series, 1.0 - a * a)


def _dot(a, b):
    return jnp.dot(a, b, preferred_element_type=F32)


def _dot_nt(a, b):
    return lax.dot_general(a, b, (((1,), (1,)), ((), ())), preferred_element_type=F32)


def _rows(shape):
    return lax.broadcasted_iota(jnp.int32, shape, 0)


def _scan_matrices(t):
    seg = t // 8
    r = np.arange(t)
    perm = (np.arange(t)[None, :] == ((r % 8) * seg + r // 8)[:, None]).astype(np.float32)
    rows, cols = r[:, None], r[None, :]
    taps, back = [], []
    for rowlen in (GRID_W, t):
        pos = rows % rowlen
        shift = {-2: (cols == rows - 2) & (pos >= 2), -1: (cols == rows - 1) & (pos >= 1),
                 0: cols == rows, 1: (cols == rows + 1) & (pos + 1 < rowlen),
                 2: (cols == rows + 2) & (pos + 2 < rowlen)}
        if rowlen == GRID_W:
            beside = [shift[-1].astype(np.float32), shift[1].astype(np.float32)]
        taps.append(np.stack([perm @ shift[k].astype(np.float32) for k in (-2, -1, 0, 1)]))
        back.append(np.stack([shift[k].astype(np.float32) @ perm.T for k in (2, 1, 0, -1)]))
    as_bf16 = lambda a: jnp.asarray(a, dtype=BF16)
    return as_bf16(np.stack(taps)), as_bf16(np.stack(back)), as_bf16(np.stack([perm, perm.T] + beside))


def _chunk_scan(a, b, reverse):
    row = _rows(a.shape)
    for s in (1, 2, 4):
        if reverse:
            m = row < 8 - s
            sh = 8 - s
        else:
            m = row >= s
            sh = s
        a_s = jnp.where(m, pltpu.roll(a, sh, 0), 1.0)
        b_s = jnp.where(m, pltpu.roll(b, sh, 0), 0.0)
        b = b + a * b_s
        a = a * a_s
    return a, b


def _chain_segments(ptot, hend, carry, reverse):
    ca, cb = _chunk_scan(ptot, hend, reverse)
    incl = ca * carry + cb
    r8 = _rows(incl.shape)
    if reverse:
        start = jnp.where(r8 < 7, pltpu.roll(incl, 7, 0), carry)
        last = incl[0:1, :]
    else:
        start = jnp.where(r8 >= 1, pltpu.roll(incl, 1, 0), carry)
        last = incl[7:8, :]
    return start, jnp.broadcast_to(last, incl.shape)


def _blocks(nblock, reverse):
    order = range(nblock - 1, -1, -1) if reverse else range(nblock)
    return [slice(8 * k, 8 * k + 8) for k in order]


def _scan_tile(a_ref, b_ref, out_ref, carry, reverse):
    t, w = a_ref.shape
    seg = t // 8

    hend, ptot = jnp.zeros((8, w), F32), jnp.ones((8, w), F32)
    for rows in _blocks(seg, reverse):
        a = a_ref[rows, :]
        hend, ptot = a * hend + b_ref[rows, :], a * ptot
    h, new_carry = _chain_segments(ptot, hend, carry, reverse)
    for rows in _blocks(seg, reverse):
        h = a_ref[rows, :] * h + b_ref[rows, :]
        out_ref[rows, :] = h
    return new_carry


def _scan_tile_backward(a_ref, dh_ref, g_ref, carry, reverse):
    t, w = a_ref.shape
    seg = t // 8

    uend, ptot = jnp.zeros((8, w), F32), jnp.ones((8, w), F32)
    for rows in _blocks(seg, reverse):
        a = a_ref[rows, :]
        uend, ptot = a * (dh_ref[rows, :] + uend), a * ptot
    u, new_carry = _chain_segments(ptot, uend, carry, reverse)
    for rows in _blocks(seg, reverse):
        g = dh_ref[rows, :] + u
        g_ref[rows, :] = g
        u = a_ref[rows, :] * g
    return new_carry


def _lru_coef(xb, wg_ref, d, ba, bx, lam, gc):
    w = xb.shape[1]
    xb16 = xb.astype(BF16)
    zr, zi = [], []
    for g in range(w // gc):
        z = _dot(xb16[:, g * gc:(g + 1) * gc], wg_ref[d, g])
        zr.append(z[:, :gc])
        zi.append(z[:, gc:])
    zr = zr[0] if len(zr) == 1 else jnp.concatenate(zr, axis=-1)
    zi = zi[0] if len(zi) == 1 else jnp.concatenate(zi, axis=-1)
    tr = jnp.tanh(zr + ba)
    ti = jnp.tanh(zi + bx)
    sp = _softplus(-lam)
    half = -0.5 * LRU_C * sp
    la = tr * half + half
    a = jnp.exp(la)
    q = _one_minus_sq(a, la)
    rs = lax.rsqrt(jnp.maximum(q, 1e-30))
    return a, q * rs, rs, tr, ti, sp


def _adamw(w, g, m, v):
    m2 = ADAM_B1 * m + (1.0 - ADAM_B1) * g
    v2 = ADAM_B2 * v + (1.0 - ADAM_B2) * (g * g)
    m_hat = m2 / ADAM_C1
    v_hat = v2 / ADAM_C2
    delta = -ADAM_LR * (m_hat / (jnp.sqrt(v_hat) + ADAM_EPS) + ADAM_WD * w)
    return delta, m2, v2


def _mod_forward(c8, cctx8, w_ada, small):
    d = c8.shape[1]
    cols = w_ada.shape[1]

    def body(c_ref, cctx_ref, w_ref, sm_ref, mod_ref, s_ref, sm_all, cbuf, mod_my, send_sems, recv_sems):
        _exchange_vmem(sm_ref, sm_all, send_sems, recv_sems, 2 * (NDEV - 1))
        _exchange_vmem(c_ref, cbuf, send_sems, recv_sems, 0)
        row = _rows((8, d))
        c_all = jnp.zeros((8, d), F32)
        for b in range(NDEV):
            c_all = jnp.where(row == b, cbuf[b], c_all)
        cc = cctx_ref[...]
        s_top = c_all * _sigmoid(c_all)
        s_bot = jnp.where(row == 0, cc * _sigmoid(cc), 0.0)
        s = jnp.concatenate([s_top, s_bot], axis=0)
        s_ref[...] = s
        mod_my[...] = jnp.dot(s, w_ref[...], precision=HIGHEST, preferred_element_type=F32)
        _exchange_vmem(mod_my, mod_ref, send_sems, recv_sems, NDEV - 1)

    return _call(
        body, name="mod_forward",
        out_shape=(jax.ShapeDtypeStruct((NDEV, 16, cols), F32), jax.ShapeDtypeStruct((16, d), F32),
                   jax.ShapeDtypeStruct((NDEV,) + small.shape, F32)),
        in_specs=[VMEM] * 4, out_specs=(VMEM,) * 3,
        scratch_shapes=[pltpu.VMEM((NDEV, 8, d), F32), pltpu.VMEM((16, cols), F32),
                        pltpu.SemaphoreType.DMA((3 * (NDEV - 1),)), pltpu.SemaphoreType.DMA((3 * (NDEV - 1),))],
        compiler_params=_params(),
    )(c8, cctx8, w_ada, small)


def _scatter_copies(src_ref, dst_ref, send_sems, recv_sems):
    me = _idx(_my_pos())
    copies = [pltpu.make_async_copy(src_ref.at[me], dst_ref.at[0], send_sems.at[0])]
    for k in range(1, NDEV):
        peer = _peer(k)
        copies.append(pltpu.make_async_remote_copy(
            src_ref=src_ref.at[_idx(peer)], dst_ref=dst_ref.at[k], send_sem=send_sems.at[k],
            recv_sem=recv_sems.at[k], device_id=peer, device_id_type=MESH))
    return copies


def _gather_copies(src_ref, dst_ref, send_sems, recv_sems):
    me = _idx(_my_pos())
    sends = [pltpu.make_async_copy(src_ref, dst_ref.at[me], send_sems.at[0])]
    arrivals = []
    for k in range(1, NDEV):
        peer = _peer(k)
        sends.append(pltpu.make_async_remote_copy(
            src_ref=src_ref, dst_ref=dst_ref.at[me], send_sem=send_sems.at[k],
            recv_sem=recv_sems.at[k], device_id=peer, device_id_type=MESH))
        arrivals.append(pltpu.make_async_remote_copy(
            src_ref=src_ref, dst_ref=dst_ref.at[_idx(peer)], send_sem=send_sems.at[k],
            recv_sem=recv_sems.at[k], device_id=peer, device_id_type=MESH))
    return sends, arrivals


def _exchange_wait(sends, arrivals):
    sends[0].wait()
    for cp in arrivals:
        cp.wait_recv()
    for cp in sends[1:]:
        cp.wait_send()


def _chip_order(k, c):
    return (6, 4 - 2 * c, 2 + 2 * c, 0)[k]


def _scatter_order(s, c):
    k = s >> 1
    mine = jnp.where(k == 0, 6, jnp.where(k == 1, 4 - 2 * c, jnp.where(k == 2, 2 + 2 * c, 0)))
    theirs = jnp.where(k == 0, 6, jnp.where(k == 1, 2 + 2 * c, jnp.where(k == 2, 4 - 2 * c, 0))) ^ 1
    return jnp.where((s & 1) == 0, theirs, mine)


def _peer_at(dist):
    x, y, c = _my_pos()
    return (x ^ ((dist >> 2) & 1), y ^ ((dist >> 1) & 1), c ^ (dist & 1))


def _reduce_small(packed, lru_parts, w_ada, dsilu_cctx):
    rp = packed.shape[0]
    rl = lru_parts.shape[1]
    d, cols = w_ada.shape
    assert cols % 128 == 0
    cb = cols // 128

    def body(p_ref, l_ref, w_ref, ds_ref, sum_ref, all_ref, lru_ref, cctx_ref,
             lbuf, lsum, cpart, call, send_sems, recv_sems, lsend, lrecv):
        me = _idx(_my_pos())
        scattered = _scatter_copies(l_ref, lbuf, lsend, lrecv)
        for cp in scattered:
            cp.start()
        _exchange_vmem(p_ref, all_ref, send_sems, recv_sems, 0)
        acc = all_ref[0]
        for j in range(1, NDEV):
            acc = acc + all_ref[j]
        sum_ref[...] = acc
        _exchange_wait(scattered, scattered[1:])
        red = lbuf[0]
        for k in range(1, NDEV):
            red = red + lbuf[k]
        lsum[...] = red
        lru_gather = _exchange_start(lsum, lru_ref, send_sems, recv_sems, NDEV - 1)
        part = jnp.zeros((8, d), F32)
        for q in range(cb):
            dm = jnp.broadcast_to(sum_ref[pl.ds((NDEV + me) * cb + q, 1), :], (8, 128))
            part = part + lax.dot_general(dm, w_ref[:, q * 128:(q + 1) * 128],
                                          (((1,), (1,)), ((), ())), precision=HIGHEST,
                                          preferred_element_type=F32)
        cpart[...] = part
        _exchange_vmem(cpart, call, send_sems, recv_sems, 2 * (NDEV - 1))
        _exchange_finish(lru_gather)
        tot = call[0]
        for j in range(1, NDEV):
            tot = tot + call[j]
        cctx_ref[...] = tot * ds_ref[...]

    return _call(
        body, name="reduce_small",
        out_shape=(jax.ShapeDtypeStruct((rp, 128), F32), jax.ShapeDtypeStruct((NDEV, rp, 128), F32),
                   jax.ShapeDtypeStruct((NDEV, rl, 128), F32), jax.ShapeDtypeStruct((8, d), F32)),
        in_specs=[VMEM] * 4, out_specs=(VMEM,) * 4,
        scratch_shapes=[pltpu.VMEM((NDEV, rl, 128), F32), pltpu.VMEM((rl, 128), F32), pltpu.VMEM((8, d), F32),
                        pltpu.VMEM((NDEV, 8, d), F32),
                        pltpu.SemaphoreType.DMA((3 * (NDEV - 1),)), pltpu.SemaphoreType.DMA((3 * (NDEV - 1),)),
                        pltpu.SemaphoreType.DMA((NDEV,)), pltpu.SemaphoreType.DMA((NDEV,))],
        compiler_params=_params(),
    )(packed, lru_parts, w_ada, dsilu_cctx)


def _normalize(src, mv, la, row0, tm, name, prev=None):
    rows, d = src.shape
    blk0 = row0 // tm

    def body(*refs):
        x_ref, mv_ref, h_ref = refs[0], refs[1], refs[-1]
        xf = x_ref[...]
        r = lax.rsqrt(jnp.mean(xf * xf, axis=-1, keepdims=True) + EPS)
        h = xf * r * (mv_ref[0:1, :] * (1.0 + mv_ref[1:2, :])) + mv_ref[2:3, :]
        h_ref[...] = h.astype(BF16)

    in_specs = [pl.BlockSpec((tm, d), lambda i: (i, 0)), pl.BlockSpec((8, d), lambda i: (0, 0))]
    args = [src, mv]
    aliases = {}
    if prev is not None:
        in_specs += [ANY]
        args += [prev]
        aliases = {2: 0}
    return _call(
        body, name=name,
        grid=(rows // tm,),
        out_shape=jax.ShapeDtypeStruct((la, d), BF16),
        in_specs=in_specs,
        out_specs=pl.BlockSpec((tm, d), lambda i: (blk0 + i, 0)),
        input_output_aliases=aliases,
        compiler_params=_params(("arbitrary",)),
    )(*args)


def _gather_order(step):
    return (step & 1) | (((step >> 2) & 1) << 1) | (((step >> 1) & 1) << 2)


def _in_projection(h, w_shard, tm):
    la, d = h.shape
    bw = w_shard.shape[1]
    ni = la // tm
    where = jnp.reshape(_idx(_my_pos()), (1,)).astype(jnp.int32)

    def body(me_ref, h_ref, w_ref, p_ref, all_ref, wbuf, send_sems, recv_sems, local_sems):
        s, i = pl.program_id(0), pl.program_id(1)
        x, y, c = _my_pos()
        me, sibling = (x, y, c), (x, y, 1 - c)
        chips = [(1 - x, y), (x, 1 - y), (1 - x, 1 - y)]

        def copy(k, block, to, from_shard=False):
            return pltpu.make_async_remote_copy(
                src_ref=w_ref if from_shard else all_ref.at[_idx(block)], dst_ref=all_ref.at[_idx(block)],
                send_sem=send_sems.at[k], recv_sem=recv_sems.at[k], device_id=to, device_id_type=MESH)

        def load(block, slot):
            return pltpu.make_async_copy(all_ref.at[_idx(block)], wbuf.at[slot], local_sems.at[1])

        keep = pltpu.make_async_copy(w_ref, all_ref.at[_idx(me)], local_sems.at[0])
        first = [copy(0, me, sibling, True)] + [copy(1 + j, me, (*chip, c), True) for j, chip in enumerate(chips)]
        passed = [copy(4 + j, (*chip, c), sibling) for j, chip in enumerate(chips)]
        steps = [(copy(0, sibling, me), None, sibling)]
        for j, chip in enumerate(chips):
            steps.append((copy(1 + j, (*chip, c), me), passed[j], (*chip, c)))
            steps.append((copy(4 + j, (*chip, 1 - c), me), None, (*chip, 1 - c)))

        @pl.when((s == 0) & (i == 0))
        def _():
            keep.start()
            mine = pltpu.make_async_copy(w_ref, wbuf.at[0], local_sems.at[1])
            mine.start()
            for cp in first:
                cp.start()
            mine.wait()

        for n, (arrival, forward, block) in enumerate(steps, start=1):
            @pl.when((s == n - 1) & (i == ni - 1))
            def _(arrival=arrival, forward=forward, block=block, n=n):
                arrival.wait_recv()
                if forward is not None:
                    forward.start()
                load(block, n % 2).start()

        @pl.when((s > 0) & (i == 0))
        def _():
            load(me, s % 2).wait()

        p_ref[...] = _dot(h_ref[...], wbuf[s % 2]).astype(BF16)

        @pl.when((s == NDEV - 1) & (i == ni - 1))
        def _():
            for cp in first + passed:
                cp.wait_send()
            keep.wait()

    return _call(
        body, name="in_projection",
        grid_spec=pltpu.PrefetchScalarGridSpec(
            num_scalar_prefetch=1, grid=(NDEV, ni),
            in_specs=[pl.BlockSpec((tm, d), lambda s, i, me_ref: (i, 0)), ANY],
            out_specs=(pl.BlockSpec((tm, bw), lambda s, i, me_ref: (i, me_ref[0] ^ _gather_order(s))), ANY),
            scratch_shapes=[pltpu.VMEM((2, d, bw), BF16), pltpu.SemaphoreType.DMA((7,)),
                            pltpu.SemaphoreType.DMA((7,)), pltpu.SemaphoreType.DMA((2,))]),
        out_shape=(jax.ShapeDtypeStruct((la, NDEV * bw), BF16), jax.ShapeDtypeStruct((NDEV, d, bw), BF16)),
        compiler_params=_params(("arbitrary", "arbitrary")),
    )(where, h, w_shard)


def _conv_input(p, wcb, taps_m, l, t):
    la = p.shape[0]
    w = wcb.shape[1]
    nt = l // t

    def body(v_ref, wcb_ref, tm_ref, xb_ref):
        taps = _dot(tm_ref[...].reshape(4 * t, t), v_ref[...])
        xb = wcb_ref[4:5, :] + wcb_ref[0:1, :] * taps[0:t]
        for j in range(1, 4):
            xb = xb + wcb_ref[j:j + 1, :] * taps[j * t:(j + 1) * t]
        xb_ref[...] = xb

    return _call(
        body, name="conv_input",
        grid=(nt + 1,),
        out_shape=jax.ShapeDtypeStruct((la, w), F32),
        in_specs=[pl.BlockSpec((t, w), lambda i: (i, 4)), pl.BlockSpec((8, w), lambda i: (0, 0)),
                  pl.BlockSpec((None, 4, t, t), lambda i: (i // nt, 0, 0, 0))],
        out_specs=pl.BlockSpec((t, w), lambda i: (i, 0)),
        compiler_params=_params(("arbitrary",)),
    )(p, wcb, taps_m)


def _lru_forward(xb, wg, lv, wo_shard, l, t):
    la, w = xb.shape
    gc = wg.shape[2]
    nt = l // t

    def body(xf_ref, xr_ref, wg_ref, lv_ref, wo_ref, hf_ref, hr_ref, wo_all,
             a_s, b_s, carry, send_sems, recv_sems):
        sends, arrivals = _gather_copies(wo_ref, wo_all, send_sems, recv_sems)

        @pl.when(pl.program_id(0) == 0)
        def _():
            carry[...] = jnp.zeros_like(carry)
            for cp in sends:
                cp.start()

        @pl.when(pl.program_id(0) == nt)
        def _():
            _exchange_wait(sends, arrivals)

        for dr, (x_ref, h_ref) in enumerate(((xf_ref, hf_ref), (xr_ref, hr_ref))):
            x = x_ref[...]
            a, s, _, _, ti, _ = _lru_coef(x, wg_ref, dr, lv_ref[3 * dr:3 * dr + 1, :],
                                          lv_ref[3 * dr + 1:3 * dr + 2, :], lv_ref[3 * dr + 2:3 * dr + 3, :], gc)
            a_s[...] = a
            b_s[...] = (s * x) * (0.5 * ti + 0.5)
            carry[dr] = _scan_tile(a_s, b_s, h_ref, carry[dr], dr == 1)

    full = lambda shape: pl.BlockSpec(shape, lambda i: (0,) * len(shape))
    fmap = lambda i: (jnp.where(i == 0, nt, i - 1), 0)
    rmap = lambda i: (jnp.where(i == 0, nt, nt - i), 0)
    return _call(
        body, name="lru_forward",
        grid=(nt + 1,),
        out_shape=(jax.ShapeDtypeStruct((la, w), F32), jax.ShapeDtypeStruct((la, w), F32),
                   jax.ShapeDtypeStruct((NDEV,) + wo_shard.shape, wo_shard.dtype)),
        in_specs=[pl.BlockSpec((t, w), fmap), pl.BlockSpec((t, w), rmap), full(wg.shape), full(lv.shape), ANY],
        out_specs=(pl.BlockSpec((t, w), fmap), pl.BlockSpec((t, w), rmap), ANY),
        scratch_shapes=[pltpu.VMEM((t, w), F32), pltpu.VMEM((t, w), F32), pltpu.VMEM((2, 8, w), F32),
                        pltpu.SemaphoreType.DMA((NDEV,)), pltpu.SemaphoreType.DMA((NDEV,))],
        compiler_params=_params(("arbitrary",)),
    )(xb, xb, wg, lv, wo_shard)


def _mix_gates(p_refs, hf_ref, hr_ref, wca_ref, perm_ref, t, w):
    bl, cl, ul, gl, ql = [r[...].astype(F32) for r in p_refs]
    tt = cl * ul
    tt16 = tt.astype(BF16)
    beside = _dot(perm_ref[2:4].reshape(2 * t, t), tt16)
    before, after = beside[:t], beside[t:]
    z = wca_ref[0:1, :] * before + wca_ref[1:2, :] * tt + wca_ref[2:3, :] * after
    sig_g = _sigmoid(gl)
    sig_q = _sigmoid(ql)
    ylru = _dot(perm_ref[1], (hf_ref[...] + hr_ref[...]).astype(BF16))
    return bl, cl, ul, gl, ql, (before, tt, after), z, sig_g, sig_q, ylru


def _p_specs(t, w, nt):
    return [pl.BlockSpec((t, w), functools.partial(lambda i, s: (jnp.minimum(i, nt - 1), s), s=s))
            for s in (0, 1, 2, 3, 5)]


def _mix_forward(x, tgt, p, hf, hr, wo, ov, wca, perm, t):
    l, d = x.shape
    w = d // 2
    nt = l // t

    def body(x_ref, tg_ref, b_ref, c_ref, u_ref, g_ref, q_ref, hf_ref, hr_ref, wo_ref, ov_ref, wca_ref, perm_ref,
             dn_ref, ct_ref, do_ref, part_ref):
        i = pl.program_id(0)
        bl, _, _, gl, ql, _, z, sig_g, sig_q, ylru = _mix_gates(
            (b_ref, c_ref, u_ref, g_ref, q_ref), hf_ref, hr_ref, wca_ref, perm_ref, t, w)
        ya = bl * z * (gl * sig_g)
        yb = ylru * (ql * sig_q)
        ct_ref[:, 0:w] = ya.astype(BF16)
        ct_ref[:, w:] = yb.astype(BF16)
        out = _dot(ya.astype(BF16), wo_ref[0:w, :]) + _dot(yb.astype(BF16), wo_ref[w:, :])
        gate, fg = ov_ref[0:1, :], ov_ref[1:2, :]
        n = x_ref[...] + gate * out
        rr = lax.rsqrt(jnp.mean(n * n, axis=-1, keepdims=True) + EPS)
        nh = n * rr
        e = nh * fg - tg_ref[...]
        loss = 0.5 * jnp.sum(jnp.mean(e * e, axis=-1, keepdims=True), axis=0, keepdims=True)
        dy = e * (1.0 / d)
        dnh = dy * fg
        dn = rr * (dnh - nh * jnp.mean(dnh * nh, axis=-1, keepdims=True))
        dn_ref[...] = dn.astype(BF16)
        do_ref[...] = (dn * gate).astype(BF16)

        @pl.when(i == 0)
        def _():
            part_ref[...] = jnp.zeros_like(part_ref)

        part_ref[0:1, :] += jnp.sum(dy * nh, axis=0, keepdims=True)
        part_ref[1:2, :] += jnp.sum(dn * out, axis=0, keepdims=True)
        part_ref[2:3, :] += jnp.broadcast_to(loss, (1, d))

    tile = lambda cols: pl.BlockSpec((t, cols), lambda i: (i, 0))
    full = lambda shape: pl.BlockSpec(shape, lambda i: (0,) * len(shape))
    return _call(
        body, name="mix_forward",
        grid=(nt,),
        out_shape=(jax.ShapeDtypeStruct((l, d), BF16), jax.ShapeDtypeStruct((l, d), BF16),
                   jax.ShapeDtypeStruct((l, d), BF16), jax.ShapeDtypeStruct((8, d), F32)),
        in_specs=[tile(d), tile(d)] + _p_specs(t, w, nt) + [tile(w), tile(w),
                  pl.BlockSpec((d, d), lambda i: (0, 0), pipeline_mode=pl.Buffered(1)),
                  full(ov.shape), full(wca.shape), full(perm.shape)],
        out_specs=(tile(d), tile(d), tile(d), full((8, d))),
        compiler_params=_params(("arbitrary",)),
    )(x, tgt, p, p, p, p, p, hf, hr, wo, ov, wca, perm)


def _mix_backward(dout, p, hf, hr, wo, wca, perm, g_wout, l, t):
    d = dout.shape[1]
    w = d // 2
    nt = l // t
    la = p.shape[0]

    def body(do_ref, b_ref, c_ref, u_ref, g_ref, q_ref, hf_ref, hr_ref, wo_ref, wca_ref, perm_ref, gw_ref,
             dp_ref, dh_ref, part_ref, sc_ref, send_sems, recv_sems):
        i = pl.program_id(0)
        copies = _scatter_copies(gw_ref, sc_ref, send_sems, recv_sems)

        @pl.when(i == 0)
        def _():
            part_ref[...] = jnp.zeros_like(part_ref)
            for cp in copies:
                cp.start()

        @pl.when(i == nt)
        def _():
            dp_ref[...] = jnp.zeros_like(dp_ref)
            _exchange_wait(copies, copies[1:])

        @pl.when(i < nt)
        def _():
            bl, cl, ul, gl, ql, taps, z, sig_g, sig_q, ylru = _mix_gates(
                (b_ref, c_ref, u_ref, g_ref, q_ref), hf_ref, hr_ref, wca_ref, perm_ref, t, w)
            do = do_ref[...]
            dya = _dot_nt(do, wo_ref[0:w, :])
            dyb = _dot_nt(do, wo_ref[w:, :])
            sg = gl * sig_g
            dz = dya * bl * sg
            dz16 = dz.astype(BF16)
            beside = _dot(perm_ref[2:4].reshape(2 * t, t), dz16)
            dt = wca_ref[0:1, :] * beside[t:] + wca_ref[1:2, :] * dz + wca_ref[2:3, :] * beside[:t]
            dp_ref[:, 0:w] = (dya * z * sg).astype(BF16)
            dp_ref[:, w:2 * w] = (dt * ul).astype(BF16)
            dp_ref[:, 2 * w:3 * w] = (dt * cl).astype(BF16)
            dp_ref[:, 3 * w:4 * w] = (dya * bl * z * (sig_g * (1.0 + gl * (1.0 - sig_g)))).astype(BF16)
            dp_ref[:, 4 * w:5 * w] = jnp.zeros((t, w), BF16)
            dp_ref[:, 5 * w:6 * w] = (dyb * ylru * (sig_q * (1.0 + ql * (1.0 - sig_q)))).astype(BF16)
            dh_ref[...] = _dot(perm_ref[0], (dyb * (ql * sig_q)).astype(BF16)).astype(BF16)
            for j in range(3):
                part_ref[j:j + 1, :] += jnp.sum(dz * taps[j], axis=0, keepdims=True)

    clamp = lambda cols: pl.BlockSpec((t, cols), lambda i: (jnp.minimum(i, nt - 1), 0))
    full = lambda shape: pl.BlockSpec(shape, lambda i: (0,) * len(shape))
    return _call(
        body, name="mix_backward",
        grid=(nt + 1,),
        out_shape=(jax.ShapeDtypeStruct((la, 6 * w), BF16), jax.ShapeDtypeStruct((l, w), BF16),
                   jax.ShapeDtypeStruct((8, w), F32), jax.ShapeDtypeStruct(g_wout.shape, g_wout.dtype)),
        in_specs=[clamp(d)] + _p_specs(t, w, nt) + [clamp(w), clamp(w),
                  pl.BlockSpec((d, d), lambda i: (0, 0), pipeline_mode=pl.Buffered(1)), full(wca.shape),
                  full(perm.shape), ANY],
        out_specs=(pl.BlockSpec((t, 6 * w), lambda i: (i, 0)), clamp(w), full((8, w)), ANY),
        scratch_shapes=[pltpu.SemaphoreType.DMA((NDEV,)), pltpu.SemaphoreType.DMA((NDEV,))],
        compiler_params=_params(("arbitrary",)),
    )(dout, p, p, p, p, p, hf, hr, wo, wca, perm, g_wout)


def _lru_backward(direction, xb, dhs, hs, wg, lv, l, t, conv=None):
    la, w = hs.shape
    gc = wg.shape[2]
    ng = w // gc
    nt = l // t
    nblk8 = la // 8
    last = conv is not None
    assert last == (direction == 1)

    if direction == 0:
        tile = lambda i: jnp.where(i == nt, nt, nt - 1 - i)
        halo = lambda i: jnp.where(tile(i) == 0, nblk8 - 1, tile(i) * (t // 8) - 1)
    else:
        tile = lambda i: i
        halo = lambda i: jnp.minimum((i + 1) * (t // 8), nblk8 - 1)

    def body(*refs):
        x_ref, dh_ref, hs_ref, halo_ref, wg_ref, lv_ref = refs[:6]
        if last:
            v_ref, wcb_ref, bm_ref, dxo_ref = refs[6:10]
        out_ref, dwg_ref, part_ref, a_s, dh_s, g_s, carry = refs[-7:]
        i = pl.program_id(0)
        is_ctx = i == nt

        @pl.when(i == 0)
        def _():
            carry[...] = jnp.zeros_like(carry)
            dwg_ref[...] = jnp.zeros_like(dwg_ref)
            part_ref[...] = jnp.zeros_like(part_ref)

        xb = x_ref[...]
        lam = lv_ref[3 * direction + 2:3 * direction + 3, :]
        a, s, rs, tr, ti, sp = _lru_coef(xb, wg_ref, direction, lv_ref[3 * direction:3 * direction + 1, :],
                                         lv_ref[3 * direction + 1:3 * direction + 2, :], lam, gc)
        hs_t = hs_ref[...]
        r8 = _rows((8, w))
        if direction == 0:
            edge = jnp.where(is_ctx, 0.0, halo_ref[7:8, :])
            first = jnp.where(r8 == 0, edge, pltpu.roll(hs_t[t - 8:, :], 1, 0))
            hprev = jnp.concatenate([first, hs_t[:t - 8, :]], axis=0)
        else:
            edge = jnp.where(is_ctx, 0.0, halo_ref[0:1, :])
            final = jnp.where(r8 == 7, edge, pltpu.roll(hs_t[:8, :], 7, 0))
            hprev = jnp.concatenate([hs_t[8:, :], final], axis=0)
        a_s[...] = a
        dh_s[...] = jnp.where(is_ctx, 0.0, dh_ref[...].astype(F32))
        carry[...] = _scan_tile_backward(a_s, dh_s, g_s, carry[...], direction == 0)

        g = g_s[...]
        r = 0.5 * tr + 0.5
        ig = 0.5 * ti + 0.5
        ix = ig * xb
        gs = g * s
        dla = (g * a) * (hprev - ix * (a * rs))
        dxb = gs * ig
        dzr = dla * (r * (1.0 - tr)) * (-LRU_C * sp)
        dzi = gs * ix * (1.0 - ti)
        part_ref[0:1, :] += jnp.sum(dzr, axis=0, keepdims=True)
        part_ref[1:2, :] += jnp.sum(dzi, axis=0, keepdims=True)
        part_ref[2:3, :] += jnp.sum(dla * r, axis=0, keepdims=True) * (LRU_C * _sigmoid(-lam))
        pieces = []
        for gi in range(ng):
            sl = slice(gi * gc, (gi + 1) * gc)
            dz = jnp.concatenate([dzr[:, sl], dzi[:, sl]], axis=-1).astype(BF16)
            pieces.append(_dot_nt(dz, wg_ref[direction, gi]))
            dwg_ref[gi] += _dot(xb[:, sl].T.astype(BF16), dz)
        dxb = dxb + (pieces[0] if ng == 1 else jnp.concatenate(pieces, axis=-1))
        if not last:
            out_ref[...] = dxb
        else:
            dxb = dxb + dxo_ref[...]
            v = v_ref[...].astype(F32)
            backs = _dot(bm_ref[...].reshape(4 * t, t), dxb.astype(BF16))
            dv = jnp.zeros((t, w), F32)
            for j in range(4):
                back = backs[j * t:(j + 1) * t]
                dv = dv + wcb_ref[j:j + 1, :] * back
                part_ref[4 + j:5 + j, :] += jnp.sum(back * v, axis=0, keepdims=True)
            out_ref[...] = dv.astype(BF16)
            part_ref[3:4, :] += jnp.sum(dxb, axis=0, keepdims=True)

    full = lambda shape: pl.BlockSpec(shape, lambda i: (0,) * len(shape))
    kind = lambda i: (jnp.where(i == nt, 1, 0), 0, 0, 0)
    in_specs = [pl.BlockSpec((t, w), lambda i: (tile(i), 0)),
                pl.BlockSpec((t, w), lambda i: (jnp.minimum(tile(i), nt - 1), 0)),
                pl.BlockSpec((t, w), lambda i: (tile(i), 0)),
                pl.BlockSpec((8, w), lambda i: (halo(i), 0)),
                full(wg.shape), full(lv.shape)]
    args = [xb, dhs, hs, hs, wg, lv]
    if last:
        p, wcb, back_m, dxb_other, dp = conv
        in_specs += [pl.BlockSpec((t, w), lambda i: (tile(i), 4)), full(wcb.shape),
                     pl.BlockSpec((None, 4, t, t), kind), pl.BlockSpec((t, w), lambda i: (tile(i), 0)), ANY]
        args += [p, wcb, back_m, dxb_other, dp]
        out0 = jax.ShapeDtypeStruct(dp.shape, dp.dtype)
        spec0 = pl.BlockSpec((t, w), lambda i: (tile(i), 4))
        aliases = {10: 0}
    else:
        out0 = jax.ShapeDtypeStruct((la, w), F32)
        spec0 = pl.BlockSpec((t, w), lambda i: (tile(i), 0))
        aliases = {}
    return _call(
        body, name="lru_backward_%d" % direction,
        grid=(nt + 1,),
        out_shape=(out0, jax.ShapeDtypeStruct((ng, gc, 2 * gc), F32), jax.ShapeDtypeStruct((8, w), F32)),
        in_specs=in_specs,
        out_specs=(spec0, full((ng, gc, 2 * gc)), full((8, w))),
        scratch_shapes=[pltpu.VMEM((t, w), F32), pltpu.VMEM((t, w), F32), pltpu.VMEM((t, w), F32),
                        pltpu.VMEM((8, w), F32)],
        input_output_aliases=aliases,
        compiler_params=_params(("arbitrary",)),
    )(*args)


def _weight_grad_t(a, b, nblk_m, nblk_n, tk, name):
    k, m = a.shape
    n = b.shape[1]
    bm, bn = m // nblk_m, n // nblk_n
    nk = k // tk

    def body(a_ref, b_ref, o_ref, acc):
        kk = pl.program_id(2)

        @pl.when(kk == 0)
        def _():
            acc[...] = jnp.zeros_like(acc)

        acc[...] += lax.dot_general(a_ref[...], b_ref[...], (((0,), (0,)), ((), ())), preferred_element_type=F32)

        @pl.when(kk == nk - 1)
        def _():
            o_ref[...] = acc[...].astype(BF16)

    return _call(
        body, name=name,
        grid=(nblk_m, nblk_n, nk),
        out_shape=jax.ShapeDtypeStruct((nblk_m * nblk_n, bm, bn), BF16),
        in_specs=[pl.BlockSpec((tk, bm), lambda i, j, kk: (kk, i)),
                  pl.BlockSpec((tk, bn), lambda i, j, kk: (kk, j))],
        out_specs=pl.BlockSpec((None, bm, bn), lambda i, j, kk: (i * nblk_n + j, 0, 0)),
        scratch_shapes=[pltpu.VMEM((bm, bn), F32)],
        compiler_params=_params(("arbitrary", "arbitrary", "arbitrary")),
    )(a, b)


def _weight_grad_scatter(at, b, tk, name):
    k, m = at.shape
    n = b.shape[1]
    bn = n // NDEV
    nk = k // tk
    where = jnp.stack([_idx(_my_pos()), lax.axis_index("c")]).astype(jnp.int32)
    tn = (((0,), (0,)), ((), ()))

    def body(w_ref, a_ref, b_ref, recv_ref, acc, sbuf, sib, sib_send, sib_recv, chip_send, chip_recv, keep_sem):
        s, kk = pl.program_id(0), pl.program_id(1)
        x, y, c = _my_pos()

        @pl.when(kk == 0)
        def _():
            acc[...] = lax.dot_general(a_ref[...], b_ref[...], tn, preferred_element_type=F32)

        @pl.when(kk > 0)
        def _():
            acc[...] += lax.dot_general(a_ref[...], b_ref[...], tn, preferred_element_type=F32)

        def to_sibling(j):
            return pltpu.make_async_remote_copy(
                src_ref=sbuf.at[0], dst_ref=sib.at[j], send_sem=sib_send.at[j], recv_sem=sib_recv.at[j],
                device_id=(x, y, 1 - c), device_id_type=MESH)

        def to_chip(j):
            dist = _chip_order(j, c)
            return pltpu.make_async_remote_copy(
                src_ref=sbuf.at[1], dst_ref=recv_ref.at[dist // 2], send_sem=chip_send.at[j],
                recv_sem=chip_recv.at[dist // 2], device_id=_peer_at(dist), device_id_type=MESH)

        keep = pltpu.make_async_copy(sbuf.at[1], recv_ref.at[0], keep_sem)
        sends = []
        for j in range(4):
            sends += [to_sibling(j), to_chip(j) if j < 3 else keep]

        for st in range(NDEV):
            @pl.when((kk == nk - 1) & (s == st))
            def _(st=st):
                if st >= 2:
                    sends[st - 2].wait_send()
                part = acc[...]
                if st % 2 == 1:
                    to_sibling(st // 2).wait_recv()
                    part = part + sib[st // 2].astype(F32)
                sbuf[st % 2] = part.astype(BF16)
                sends[st].start()
                if st == NDEV - 1:
                    sends[st - 1].wait_send()
                    sends[st].wait()
                    for j in range(1, 4):
                        pltpu.make_async_remote_copy(
                            src_ref=sbuf.at[0], dst_ref=recv_ref.at[j], send_sem=chip_send.at[0],
                            recv_sem=chip_recv.at[j], device_id=_peer_at(2 * j), device_id_type=MESH).wait_recv()

    blk = lambda s, w_ref: w_ref[0] ^ _scatter_order(s, w_ref[1])
    return _call(
        body, name=name,
        grid_spec=pltpu.PrefetchScalarGridSpec(
            num_scalar_prefetch=1, grid=(NDEV, nk),
            in_specs=[pl.BlockSpec((tk, m), lambda s, kk, w_ref: (kk, 0)),
                      pl.BlockSpec((tk, bn), lambda s, kk, w_ref: (kk, blk(s, w_ref)))],
            out_specs=ANY,
            scratch_shapes=[pltpu.VMEM((m, bn), F32), pltpu.VMEM((2, m, bn), BF16), pltpu.VMEM((4, m, bn), BF16),
                            pltpu.SemaphoreType.DMA((4,)), pltpu.SemaphoreType.DMA((4,)),
                            pltpu.SemaphoreType.DMA((4,)), pltpu.SemaphoreType.DMA((4,)),
                            pltpu.SemaphoreType.DMA]),
        out_shape=jax.ShapeDtypeStruct((4, m, bn), BF16),
        compiler_params=_params(("arbitrary", "arbitrary")),
    )(where, at, b)


def _input_backward(dp, w_all, src, mv, row0, tm, nbk, name, dn=None):
    rows, d = src.shape
    nb, _, bw = w_all.shape
    nk = nb // nbk
    ni = rows // tm
    blk0 = row0 // tm
    latent = dn is not None

    def body(*refs):
        dp_ref, w_ref, x_ref, mv_ref = refs[:4]
        outs = refs[4 + latent:]
        part_ref, acc = outs[latent], outs[latent + 1]
        i, k = pl.program_id(0), pl.program_id(1)

        def product():
            step = _dot_nt(dp_ref[:, 0:bw], w_ref[0])
            for q in range(1, nbk):
                step = step + _dot_nt(dp_ref[:, q * bw:(q + 1) * bw], w_ref[q])
            return step

        def finish(slot):
            xf = x_ref[...]
            r = lax.rsqrt(jnp.mean(xf * xf, axis=-1, keepdims=True) + EPS)
            xn = xf * r
            dhl = acc[slot]
            gain, sc = mv_ref[0:1, :], mv_ref[1:2, :]
            dhx = jnp.sum(dhl * xn, axis=0, keepdims=True)
            part_ref[0:1, :] += jnp.sum(dhl, axis=0, keepdims=True)
            part_ref[1:2, :] += dhx * gain
            part_ref[2:3, :] += dhx * (1.0 + sc)
            if latent:
                dxn = dhl * (gain * (1.0 + sc))
                outs[0][...] = (refs[4][...].astype(F32)
                                + r * (dxn - xn * jnp.mean(dxn * xn, axis=-1, keepdims=True)))

        @pl.when((i == 0) & (k == 0))
        def _():
            part_ref[...] = jnp.zeros_like(part_ref)
            acc[0] = product()

        @pl.when((i > 0) & (i < ni) & (k == 0))
        def _():
            acc[i % 2] = product()
            finish((i - 1) % 2)

        @pl.when((i == ni) & (k == 0))
        def _():
            finish((ni - 1) % 2)

        @pl.when((i < ni) & (k > 0))
        def _():
            acc[i % 2] += product()

    tile = pl.BlockSpec((tm, d), lambda i, k: (jnp.maximum(i - 1, 0), 0))
    vec = pl.BlockSpec((8, d), lambda i, k: (0, 0))
    kblock = lambda i, k: jnp.where(i == ni, nk - 1, k)
    return _call(
        body, name=name,
        grid=(ni + 1, nk),
        out_shape=((jax.ShapeDtypeStruct((rows, d), F32),) if latent else ()) + (jax.ShapeDtypeStruct((8, d), F32),),
        in_specs=[pl.BlockSpec((tm, nbk * bw), lambda i, k: (blk0 + jnp.minimum(i, ni - 1), kblock(i, k))),
                  pl.BlockSpec((nbk, d, bw), lambda i, k: (kblock(i, k), 0, 0)), tile, vec]
                 + ([tile] if latent else []),
        out_specs=((tile,) if latent else ()) + (vec,),
        scratch_shapes=[pltpu.VMEM((2, tm, d), F32)],
        compiler_params=_params(("arbitrary", "arbitrary")),
    )(*([dp, w_all, src, mv] + ([dn] if latent else [])))


def _adamw_scattered(parts, w, m, v, tr):
    r, c = w.shape
    nslot = parts.shape[0]

    def body(p_ref, w_ref, m_ref, v_ref, g_ref, d_ref, m2_ref, v2_ref):
        g = p_ref[0].astype(F32)
        for k in range(1, nslot):
            g = g + p_ref[k].astype(F32)
        g_ref[...] = g
        d_ref[...], m2_ref[...], v2_ref[...] = _adamw(w_ref[...], g, m_ref[...], v_ref[...])

    tile = pl.BlockSpec((tr, c), lambda i: (i, 0))
    return _call(
        body, name="adamw_scattered_%dx%d" % (r, c),
        grid=(r // tr,),
        out_shape=tuple(jax.ShapeDtypeStruct((r, c), F32) for _ in range(4)),
        in_specs=[pl.BlockSpec((nslot, tr, c), lambda i: (0, i, 0)), tile, tile, tile],
        out_specs=(tile,) * 4,
        compiler_params=_params(("arbitrary",)),
    )(parts, w, m, v)


def _adamw_ada(st, dmod, w, m, v, tr):
    r, c = w.shape

    def body(s_ref, dm_ref, w_ref, m_ref, v_ref, g_ref, d_ref, m2_ref, v2_ref):
        g = jnp.dot(s_ref[...], dm_ref[...], precision=HIGHEST, preferred_element_type=F32)
        g_ref[...] = g
        d_ref[...], m2_ref[...], v2_ref[...] = _adamw(w_ref[...], g, m_ref[...], v_ref[...])

    tile = pl.BlockSpec((tr, c), lambda i: (i, 0))
    return _call(
        body, name="adamw_ada",
        grid=(r // tr,),
        out_shape=tuple(jax.ShapeDtypeStruct((r, c), F32) for _ in range(4)),
        in_specs=[pl.BlockSpec((tr, 16), lambda i: (i, 0)), pl.BlockSpec((16, c), lambda i: (0, 0)),
                  tile, tile, tile],
        out_specs=(tile,) * 4,
        compiler_params=_params(("arbitrary",)),
    )(st, dmod, w, m, v)


def _adamw_small(gs, ws, ms, vs):
    n = len(ws)

    def body(*refs):
        for j in range(n):
            g_ref, w_ref, m_ref, v_ref = refs[j], refs[n + j], refs[2 * n + j], refs[3 * n + j]
            d_ref, m2_ref, v2_ref = refs[4 * n + j], refs[5 * n + j], refs[6 * n + j]
            d_ref[...], m2_ref[...], v2_ref[...] = _adamw(w_ref[...], g_ref[...], m_ref[...], v_ref[...])

    shapes = tuple(jax.ShapeDtypeStruct(a.shape, F32) for a in ws)
    out = _call(
        body, name="adamw_small",
        out_shape=shapes * 3,
        in_specs=[VMEM] * (4 * n), out_specs=(VMEM,) * (3 * n),
        compiler_params=_params(),
    )(*gs, *ws, *ms, *vs)
    return list(out[:n]), list(out[n:2 * n]), list(out[2 * n:])


def _blockdiag_groups(wh, gc):
    h, dh, _ = wh.shape
    g = gc // dh
    w4 = wh.reshape(h // g, g, dh, dh)
    bd = jnp.einsum("ngij,gh->ngihj", w4, jnp.eye(g, dtype=wh.dtype))
    return bd.reshape(h // g, gc, gc)


def _blockdiag_extract(bd, dh):
    ng, gc, _ = bd.shape
    g = gc // dh
    x = bd.reshape(ng, g, dh, g, dh)
    return jnp.einsum("ngihj,gh->ngij", x, jnp.eye(g, dtype=bd.dtype)).reshape(ng * g, dh, dh)


def _largest_tile(n, cap):
    return max(q for q in range(128, min(n, cap) + 1, 128) if n % q == 0)


def _rows8(*vecs):
    rows = [jnp.reshape(v, (1, -1)).astype(F32) for v in vecs]
    n = rows[0].shape[1]
    return jnp.concatenate(rows + [jnp.zeros((8 - len(rows), n), F32)], axis=0)


def _pack(pieces):
    flat = jnp.concatenate([jnp.reshape(a, (-1,)).astype(F32) for a in pieces])
    total = -(-flat.shape[0] // 1024) * 1024
    return jnp.pad(flat, (0, total - flat.shape[0])).reshape(total // 128, 128)


def _unpack(packed, shapes):
    flat = packed.reshape(-1)
    out, off = [], 0
    for s in shapes:
        n = 1
        for q in s:
            n *= q
        out.append(flat[off:off + n].reshape(s))
        off += n
    return out


def kernel(x, c, ctx, c_ctx, norm_g, w_ada, b_ada, w_in, w_conv_a, w_conv_b, b_conv_b, lru_wa, lru_ba, lru_wx, lru_bx, lru_lambda, w_out, final_g, loss_target, m_c_ctx, m_norm_g, m_w_ada, m_b_ada, m_w_in, m_w_conv_a, m_w_conv_b, m_b_conv_b, m_lru_wa, m_lru_ba, m_lru_wx, m_lru_bx, m_lru_lambda, m_w_out, m_final_g, v_c_ctx, v_norm_g, v_w_ada, v_b_ada, v_w_in, v_w_conv_a, v_w_conv_b, v_b_conv_b, v_lru_wa, v_lru_ba, v_lru_wx, v_lru_bx, v_lru_lambda, v_w_out, v_final_g):
    _, l, d = x.shape
    lc = ctx.shape[1]
    w = d // 2
    t = lc
    assert l % t == 0 and t % GRID_W == 0 and t % 128 == 0
    dh = w // N_HEADS
    gc = min(w, MXU_WIDTH)
    cols = w_ada.shape[2]
    wo_rows = w_out.shape[1]
    me = _idx(_my_pos())
    x2, ctx2, tgt2 = x[0], ctx[0], loss_target[0]
    w_ada2, w_in2, w_out2 = w_ada[0], w_in[0], w_out[0]

    small_mine = jnp.concatenate([a.reshape(-1) for a in (w_conv_a, w_conv_b, lru_ba, lru_bx, lru_lambda)]
                                 + [jnp.zeros((3 * (w // NDEV),), F32)]).reshape(16, w // NDEV)
    mod_all, s_mat, small_all = _mod_forward(
        jnp.broadcast_to(c, (8, d)), jnp.broadcast_to(c_ctx[None], (8, d)), w_ada2, small_mine)
    mod = jnp.transpose(mod_all, (1, 0, 2)).reshape(16, NDEV * cols) + b_ada
    mod_lat = lax.dynamic_slice_in_dim(mod, me, 1, axis=0)
    sh_l, sc_l, gt_l = jnp.split(mod_lat, 3, axis=-1)
    sh_c, sc_c, _ = jnp.split(mod[8:9], 3, axis=-1)
    small = jnp.transpose(small_all, (1, 0, 2)).reshape(16, w)
    wca = _rows8(*[small[j] for j in range(0, 3)])
    wcb = _rows8(*[small[j] for j in range(3, 7)], b_conv_b)
    lv = _rows8(0.5 * small[7], 0.5 * small[9], small[11], 0.5 * small[8], 0.5 * small[10], small[12])
    wg = jnp.stack([
        jnp.concatenate([_blockdiag_groups(lru_wa[0, dr], gc), _blockdiag_groups(lru_wx[0, dr], gc)], axis=-1)
        for dr in range(2)])
    wg = (0.5 * wg).astype(BF16)

    la = l + lc
    tm = 2 * t if l % (2 * t) == 0 else t
    tk = 3 * t if la % (3 * t) == 0 else t
    h = _normalize(x2, _rows8(norm_g, sc_l, sh_l), la, 0, tm, "normalize")
    h = _normalize(ctx2, _rows8(norm_g, sc_c, sh_c), la, l, t, "normalize_ctx", prev=h)
    p, w_all = _in_projection(h, w_in2.astype(BF16), la // 4 if la % 64 == 0 else tk)
    taps_m, back_m, perm = _scan_matrices(t)
    xb = _conv_input(p, wcb, taps_m, l, t)
    hf, hr, wo_all = _lru_forward(xb, wg, lv, w_out2.astype(BF16), l, t)
    wo = wo_all.reshape(d, d)
    dn, cat, dout, part_mix = _mix_forward(x2, tgt2, p, hf, hr, wo, _rows8(gt_l, final_g), wca, perm, t)
    g_wout = _weight_grad_t(cat, dout, 1, 1, _largest_tile(l, 1024), "grad_w_out")
    dp, dhs, part_ca, sc_wout = _mix_backward(dout, p, hf, hr, wo, wca, perm, g_wout.reshape(NDEV, wo_rows, d), l, t)
    dxb0, dwg0, part_l0 = _lru_backward(0, xb, dhs, hf, wg, lv, l, t)
    dp, dwg1, part_l1 = _lru_backward(1, xb, dhs, hr, wg, lv, l, t, conv=(p, wcb, back_m, dxb0, dp))
    sc_win = _weight_grad_scatter(h, dp, tk, "grad_w_in")
    grad_x, part_lat = _input_backward(dp, w_all, x2, _rows8(norm_g, sc_l), 0, tm, 2, "input_backward", dn=dn)
    (part_ctx,) = _input_backward(dp, w_all, ctx2, _rows8(norm_g, sc_c), l, t, 2, "input_backward_ctx")
    part_in = jnp.concatenate([part_lat[0:2], part_ctx[0:2], (part_lat[2] + part_ctx[2])[None]], axis=0)

    dwa = jnp.stack([_blockdiag_extract(dwg0[:, :, :gc], dh), _blockdiag_extract(dwg1[:, :, :gc], dh)])
    dwx = jnp.stack([_blockdiag_extract(dwg0[:, :, gc:], dh), _blockdiag_extract(dwg1[:, :, gc:], dh)])
    lru_part = (0.5 * jnp.stack([dwa, dwx])).reshape(NDEV, -1, 128)
    zeros_d = jnp.zeros((d,), F32)
    pieces = [
        jnp.concatenate([part_in[0], part_in[1], part_mix[1]]),
        jnp.concatenate([part_in[2], part_in[3], zeros_d]),
        part_in[4], part_mix[0], part_ca[0:3], part_l1[4:8], part_l1[3],
        0.5 * jnp.stack([part_l0[0], part_l1[0]]), 0.5 * jnp.stack([part_l0[1], part_l1[1]]),
        jnp.stack([part_l0[2], part_l1[2]]), part_mix[2, 0:1],
    ]
    shapes = [(3 * d,), (3 * d,), (d,), (d,), (3, w), (4, w), (w,), (2, w), (2, w), (2, w), (1,)]
    sig_cc = jax.nn.sigmoid(c_ctx)
    dsilu_cc = jnp.broadcast_to((sig_cc * (1.0 + c_ctx * (1.0 - sig_cc)))[None], (8, d))
    psum, pall, lru_sum, g_cctx8 = _reduce_small(_pack(pieces), lru_part, w_ada2, dsilu_cc)
    (g_modl, g_modc, g_norm, g_final, g_ca, g_cb, g_bcb, g_ba, g_bx, g_lam, loss1) = _unpack(psum, shapes)
    loss = loss1[0]
    g_cctx = g_cctx8[0]
    g_bada = (g_modl + g_modc)[None]
    g_lru = lru_sum.reshape(2, 2, N_HEADS, dh, dh)
    g_wa, g_wx = g_lru[0][None], g_lru[1][None]
    wsl = w // NDEV
    mine = lambda a: lax.dynamic_slice_in_dim(a, me * wsl, wsl, axis=-1)
    g_ca_m, g_cb_m, g_ba_m, g_bx_m, g_lam_m = (mine(g_ca)[None], mine(g_cb)[None], mine(g_ba)[None],
                                               mine(g_bx)[None], mine(g_lam)[None])
    g_norm, g_bcb = g_norm[None], g_bcb[None]

    per_dev = pall[:, :3 * d // 128].reshape(NDEV, NDEV, cols)
    dmod_lat = lax.dynamic_slice_in_dim(per_dev, me, 1, axis=1)[:, 0]
    dmod_ctx = lax.dynamic_slice_in_dim(g_modc.reshape(NDEV, cols), me, 1, axis=0)
    dmod16 = jnp.concatenate([dmod_lat, dmod_ctx, jnp.zeros((7, cols), F32)], axis=0)
    tr_ada = 256 if d % 256 == 0 else d
    g_wada, d_wada, m_wada, v_wada = _adamw_ada(s_mat.T, dmod16, w_ada2, m_w_ada[0], v_w_ada[0], tr_ada)
    g_win2, d_win, m_win, v_win = _adamw_scattered(sc_win, w_in2, m_w_in[0], v_w_in[0], tr_ada)
    tr_out = 64 if wo_rows % 64 == 0 else wo_rows
    g_wout2, d_wout, m_wout, v_wout = _adamw_scattered(sc_wout, w_out2, m_w_out[0], v_w_out[0], tr_out)

    small_w = [c_ctx, norm_g, b_ada, w_conv_a, w_conv_b, b_conv_b, lru_wa, lru_ba, lru_wx, lru_bx, lru_lambda, final_g]
    small_m = [m_c_ctx, m_norm_g, m_b_ada, m_w_conv_a, m_w_conv_b, m_b_conv_b, m_lru_wa, m_lru_ba, m_lru_wx,
               m_lru_bx, m_lru_lambda, m_final_g]
    small_v = [v_c_ctx, v_norm_g, v_b_ada, v_w_conv_a, v_w_conv_b, v_b_conv_b, v_lru_wa, v_lru_ba, v_lru_wx,
               v_lru_bx, v_lru_lambda, v_final_g]
    small_g = [g_cctx, g_norm, g_bada, g_ca_m, g_cb_m, g_bcb, g_wa, g_ba_m, g_wx, g_bx_m, g_lam_m, g_final]
    small_g = [jnp.reshape(a, b.shape) for a, b in zip(small_g, small_w)]
    d_s, m_s, v_s = _adamw_small(small_g, small_w, small_m, small_v)

    def weights(small_list, ada, win, wout):
        (cctx_, norm_, bada_, ca_, cb_, bcb_, wa_, ba_, wx_, bx_, lam_, final_) = small_list
        return [cctx_, norm_, ada[None], bada_, win[None], ca_, cb_, bcb_, wa_, ba_, wx_, bx_, lam_, wout[None], final_]

    return (loss, grad_x[None],
            *weights(small_g, g_wada, g_win2, g_wout2), *weights(d_s, d_wada, d_win, d_wout),
            *weights(m_s, m_wada, m_win, m_wout), *weights(v_s, v_wada, v_win, v_wout))
```

```python
import functools

import jax
import jax.numpy as jnp
import numpy as np
from jax import lax
from jax.experimental import pallas as pl
from jax.experimental.pallas import tpu as pltpu

F32 = jnp.float32
BF16 = jnp.bfloat16
MESH = pl.DeviceIdType.MESH
NDEV = 8
GRID_W = 64
N_HEADS = 16
LRU_C = 8.0
EPS = 1e-6
MXU_WIDTH = 256
VMEM_LIMIT = 60 * 1024 * 1024

ADAM_LR = 0.001
ADAM_B1 = 0.9
ADAM_B2 = 0.999
ADAM_EPS = 1e-08
ADAM_WD = 0.01
ADAM_STEP = 10
ADAM_C1 = 1.0 - ADAM_B1 ** ADAM_STEP
ADAM_C2 = 1.0 - ADAM_B2 ** ADAM_STEP

HIGHEST = lax.Precision.HIGHEST
ANY = pl.BlockSpec(memory_space=pl.ANY)
VMEM = pl.BlockSpec(memory_space=pltpu.VMEM)


def _call(body, **kw):
    return pl.pallas_call(body, **kw)


def _params(sem=None, vmem=VMEM_LIMIT):
    return pltpu.CompilerParams(dimension_semantics=sem, vmem_limit_bytes=vmem)


def _my_pos():
    return lax.axis_index("x"), lax.axis_index("y"), lax.axis_index("c")


def _idx(pos):
    return 4 * pos[0] + 2 * pos[1] + pos[2]


def _peer(k):
    x, y, c = _my_pos()
    return ((1 - x) if (k >> 2) & 1 else x, (1 - y) if (k >> 1) & 1 else y, (1 - c) if k & 1 else c)


def _exchange_start(src_ref, dst_ref, send_sems, recv_sems, base):
    me = _idx(_my_pos())
    sends = []
    for k in range(1, NDEV):
        cp = pltpu.make_async_remote_copy(
            src_ref=src_ref, dst_ref=dst_ref.at[me], send_sem=send_sems.at[base + k - 1],
            recv_sem=recv_sems.at[base + k - 1], device_id=_peer(k), device_id_type=MESH)
        cp.start()
        sends.append(cp)
    dst_ref[me] = src_ref[...]
    return sends, (src_ref, dst_ref, send_sems, recv_sems, base)


def _exchange_finish(started):
    sends, (src_ref, dst_ref, send_sems, recv_sems, base) = started
    for k in range(1, NDEV):
        peer = _peer(k)
        pltpu.make_async_remote_copy(
            src_ref=src_ref, dst_ref=dst_ref.at[_idx(peer)], send_sem=send_sems.at[base + k - 1],
            recv_sem=recv_sems.at[base + k - 1], device_id=peer, device_id_type=MESH).wait_recv()
    for cp in sends:
        cp.wait_send()


def _exchange_vmem(src_ref, dst_ref, send_sems, recv_sems, base):
    _exchange_finish(_exchange_start(src_ref, dst_ref, send_sems, recv_sems, base))


def _sigmoid(z):
    return 0.5 * jnp.tanh(0.5 * z) + 0.5


def _softplus(x):
    return jnp.maximum(x, 0.0) + jnp.log1p(jnp.exp(-jnp.abs(x)))


def _one_minus_sq(a, la):
    series = (-2.0 * la) * (1.0 + la)
    return jnp.where(la > -0.0015, series, 1.0 - a * a)


def _dot(a, b):
    return jnp.dot(a, b, preferred_element_type=F32)


def _dot_nt(a, b):
    return lax.dot_general(a, b, (((1,), (1,)), ((), ())), preferred_element_type=F32)


def _rows(shape):
    return lax.broadcasted_iota(jnp.int32, shape, 0)


def _scan_matrices(t):
    seg = t // 8
    r = np.arange(t)
    perm = (np.arange(t)[None, :] == ((r % 8) * seg + r // 8)[:, None]).astype(np.float32)
    rows, cols = r[:, None], r[None, :]
    taps, back = [], []
    for rowlen in (GRID_W, t):
        pos = rows % rowlen
        shift = {-2: (cols == rows - 2) & (pos >= 2), -1: (cols == rows - 1) & (pos >= 1),
                 0: cols == rows, 1: (cols == rows + 1) & (pos + 1 < rowlen),
                 2: (cols == rows + 2) & (pos + 2 < rowlen)}
        if rowlen == GRID_W:
            beside = [shift[-1].astype(np.float32), shift[1].astype(np.float32)]
        taps.append(np.stack([perm @ shift[k].astype(np.float32) for k in (-2, -1, 0, 1)]))
        back.append(np.stack([shift[k].astype(np.float32) @ perm.T for k in (2, 1, 0, -1)]))
    as_bf16 = lambda a: jnp.asarray(a, dtype=BF16)
    return as_bf16(np.stack(taps)), as_bf16(np.stack(back)), as_bf16(np.stack([perm, perm.T] + beside))


def _chunk_scan(a, b, reverse):
    row = _rows(a.shape)
    for s in (1, 2, 4):
        if reverse:
            m = row < 8 - s
            sh = 8 - s
        else:
            m = row >= s
            sh = s
        a_s = jnp.where(m, pltpu.roll(a, sh, 0), 1.0)
        b_s = jnp.where(m, pltpu.roll(b, sh, 0), 0.0)
        b = b + a * b_s
        a = a * a_s
    return a, b


def _chain_segments(ptot, hend, carry, reverse):
    ca, cb = _chunk_scan(ptot, hend, reverse)
    incl = ca * carry + cb
    r8 = _rows(incl.shape)
    if reverse:
        start = jnp.where(r8 < 7, pltpu.roll(incl, 7, 0), carry)
        last = incl[0:1, :]
    else:
        start = jnp.where(r8 >= 1, pltpu.roll(incl, 1, 0), carry)
        last = incl[7:8, :]
    return start, jnp.broadcast_to(last, incl.shape)


def _blocks(nblock, reverse):
    order = range(nblock - 1, -1, -1) if reverse else range(nblock)
    return [slice(8 * k, 8 * k + 8) for k in order]


def _scan_tile(a_ref, b_ref, out_ref, carry, reverse):
    t, w = a_ref.shape
    seg = t // 8

    hend, ptot = jnp.zeros((8, w), F32), jnp.ones((8, w), F32)
    for rows in _blocks(seg, reverse):
        a = a_ref[rows, :]
        hend, ptot = a * hend + b_ref[rows, :], a * ptot
    h, new_carry = _chain_segments(ptot, hend, carry, reverse)
    for rows in _blocks(seg, reverse):
        h = a_ref[rows, :] * h + b_ref[rows, :]
        out_ref[rows, :] = h
    return new_carry


def _scan_tile_backward(a_ref, dh_ref, g_ref, carry, reverse):
    t, w = a_ref.shape
    seg = t // 8

    uend, ptot = jnp.zeros((8, w), F32), jnp.ones((8, w), F32)
    for rows in _blocks(seg, reverse):
        a = a_ref[rows, :]
        uend, ptot = a * (dh_ref[rows, :] + uend), a * ptot
    u, new_carry = _chain_segments(ptot, uend, carry, reverse)
    for rows in _blocks(seg, reverse):
        g = dh_ref[rows, :] + u
        g_ref[rows, :] = g
        u = a_ref[rows, :] * g
    return new_carry


def _lru_coef(xb, wg_ref, d, ba, bx, lam, gc):
    w = xb.shape[1]
    xb16 = xb.astype(BF16)
    zr, zi = [], []
    for g in range(w // gc):
        z = _dot(xb16[:, g * gc:(g + 1) * gc], wg_ref[d, g])
        zr.append(z[:, :gc])
        zi.append(z[:, gc:])
    zr = zr[0] if len(zr) == 1 else jnp.concatenate(zr, axis=-1)
    zi = zi[0] if len(zi) == 1 else jnp.concatenate(zi, axis=-1)
    tr = jnp.tanh(zr + ba)
    ti = jnp.tanh(zi + bx)
    sp = _softplus(-lam)
    half = -0.5 * LRU_C * sp
    la = tr * half + half
    a = jnp.exp(la)
    q = _one_minus_sq(a, la)
    rs = lax.rsqrt(jnp.maximum(q, 1e-30))
    return a, q * rs, rs, tr, ti, sp


def _adamw(w, g, m, v):
    m2 = ADAM_B1 * m + (1.0 - ADAM_B1) * g
    v2 = ADAM_B2 * v + (1.0 - ADAM_B2) * (g * g)
    m_hat = m2 / ADAM_C1
    v_hat = v2 / ADAM_C2
    delta = -ADAM_LR * (m_hat / (jnp.sqrt(v_hat) + ADAM_EPS) + ADAM_WD * w)
    return delta, m2, v2


def _mod_forward(c8, cctx8, w_ada, small):
    d = c8.shape[1]
    cols = w_ada.shape[1]

    def body(c_ref, cctx_ref, w_ref, sm_ref, mod_ref, s_ref, sm_all, cbuf, mod_my, send_sems, recv_sems):
        _exchange_vmem(sm_ref, sm_all, send_sems, recv_sems, 2 * (NDEV - 1))
        _exchange_vmem(c_ref, cbuf, send_sems, recv_sems, 0)
        row = _rows((8, d))
        c_all = jnp.zeros((8, d), F32)
        for b in range(NDEV):
            c_all = jnp.where(row == b, cbuf[b], c_all)
        cc = cctx_ref[...]
        s_top = c_all * _sigmoid(c_all)
        s_bot = jnp.where(row == 0, cc * _sigmoid(cc), 0.0)
        s = jnp.concatenate([s_top, s_bot], axis=0)
        s_ref[...] = s
        mod_my[...] = jnp.dot(s, w_ref[...], precision=HIGHEST, preferred_element_type=F32)
        _exchange_vmem(mod_my, mod_ref, send_sems, recv_sems, NDEV - 1)

    return _call(
        body, name="mod_forward",
        out_shape=(jax.ShapeDtypeStruct((NDEV, 16, cols), F32), jax.ShapeDtypeStruct((16, d), F32),
                   jax.ShapeDtypeStruct((NDEV,) + small.shape, F32)),
        in_specs=[VMEM] * 4, out_specs=(VMEM,) * 3,
        scratch_shapes=[pltpu.VMEM((NDEV, 8, d), F32), pltpu.VMEM((16, cols), F32),
                        pltpu.SemaphoreType.DMA((3 * (NDEV - 1),)), pltpu.SemaphoreType.DMA((3 * (NDEV - 1),))],
        compiler_params=_params(),
    )(c8, cctx8, w_ada, small)


def _scatter_copies(src_ref, dst_ref, send_sems, recv_sems):
    me = _idx(_my_pos())
    copies = [pltpu.make_async_copy(src_ref.at[me], dst_ref.at[0], send_sems.at[0])]
    for k in range(1, NDEV):
        peer = _peer(k)
        copies.append(pltpu.make_async_remote_copy(
            src_ref=src_ref.at[_idx(peer)], dst_ref=dst_ref.at[k], send_sem=send_sems.at[k],
            recv_sem=recv_sems.at[k], device_id=peer, device_id_type=MESH))
    return copies


def _gather_copies(src_ref, dst_ref, send_sems, recv_sems):
    me = _idx(_my_pos())
    sends = [pltpu.make_async_copy(src_ref, dst_ref.at[me], send_sems.at[0])]
    arrivals = []
    for k in range(1, NDEV):
        peer = _peer(k)
        sends.append(pltpu.make_async_remote_copy(
            src_ref=src_ref, dst_ref=dst_ref.at[me], send_sem=send_sems.at[k],
            recv_sem=recv_sems.at[k], device_id=peer, device_id_type=MESH))
        arrivals.append(pltpu.make_async_remote_copy(
            src_ref=src_ref, dst_ref=dst_ref.at[_idx(peer)], send_sem=send_sems.at[k],
            recv_sem=recv_sems.at[k], device_id=peer, device_id_type=MESH))
    return sends, arrivals


def _exchange_wait(sends, arrivals):
    sends[0].wait()
    for cp in arrivals:
        cp.wait_recv()
    for cp in sends[1:]:
        cp.wait_send()


def _chip_order(k, c):
    return (6, 4 - 2 * c, 2 + 2 * c, 0)[k]


def _scatter_order(s, c):
    k = s >> 1
    mine = jnp.where(k == 0, 6, jnp.where(k == 1, 4 - 2 * c, jnp.where(k == 2, 2 + 2 * c, 0)))
    theirs = jnp.where(k == 0, 6, jnp.where(k == 1, 2 + 2 * c, jnp.where(k == 2, 4 - 2 * c, 0))) ^ 1
    return jnp.where((s & 1) == 0, theirs, mine)


def _peer_at(dist):
    x, y, c = _my_pos()
    return (x ^ ((dist >> 2) & 1), y ^ ((dist >> 1) & 1), c ^ (dist & 1))


def _reduce_small(packed, lru_parts, w_ada, dsilu_cctx):
    rp = packed.shape[0]
    rl = lru_parts.shape[1]
    d, cols = w_ada.shape
    assert cols % 128 == 0
    cb = cols // 128

    def body(p_ref, l_ref, w_ref, ds_ref, sum_ref, all_ref, lru_ref, cctx_ref,
             lbuf, lsum, cpart, call, send_sems, recv_sems, lsend, lrecv):
        me = _idx(_my_pos())
        scattered = _scatter_copies(l_ref, lbuf, lsend, lrecv)
        for cp in scattered:
            cp.start()
        _exchange_vmem(p_ref, all_ref, send_sems, recv_sems, 0)
        acc = all_ref[0]
        for j in range(1, NDEV):
            acc = acc + all_ref[j]
        sum_ref[...] = acc
        _exchange_wait(scattered, scattered[1:])
        red = lbuf[0]
        for k in range(1, NDEV):
            red = red + lbuf[k]
        lsum[...] = red
        lru_gather = _exchange_start(lsum, lru_ref, send_sems, recv_sems, NDEV - 1)
        part = jnp.zeros((8, d), F32)
        for q in range(cb):
            dm = jnp.broadcast_to(sum_ref[pl.ds((NDEV + me) * cb + q, 1), :], (8, 128))
            part = part + lax.dot_general(dm, w_ref[:, q * 128:(q + 1) * 128],
                                          (((1,), (1,)), ((), ())), precision=HIGHEST,
                                          preferred_element_type=F32)
        cpart[...] = part
        _exchange_vmem(cpart, call, send_sems, recv_sems, 2 * (NDEV - 1))
        _exchange_finish(lru_gather)
        tot = call[0]
        for j in range(1, NDEV):
            tot = tot + call[j]
        cctx_ref[...] = tot * ds_ref[...]

    return _call(
        body, name="reduce_small",
        out_shape=(jax.ShapeDtypeStruct((rp, 128), F32), jax.ShapeDtypeStruct((NDEV, rp, 128), F32),
                   jax.ShapeDtypeStruct((NDEV, rl, 128), F32), jax.ShapeDtypeStruct((8, d), F32)),
        in_specs=[VMEM] * 4, out_specs=(VMEM,) * 4,
        scratch_shapes=[pltpu.VMEM((NDEV, rl, 128), F32), pltpu.VMEM((rl, 128), F32), pltpu.VMEM((8, d), F32),
                        pltpu.VMEM((NDEV, 8, d), F32),
                        pltpu.SemaphoreType.DMA((3 * (NDEV - 1),)), pltpu.SemaphoreType.DMA((3 * (NDEV - 1),)),
                        pltpu.SemaphoreType.DMA((NDEV,)), pltpu.SemaphoreType.DMA((NDEV,))],
        compiler_params=_params(),
    )(packed, lru_parts, w_ada, dsilu_cctx)


def _normalize(src, mv, la, row0, tm, name, prev=None):
    rows, d = src.shape
    blk0 = row0 // tm

    def body(*refs):
        x_ref, mv_ref, h_ref = refs[0], refs[1], refs[-1]
        xf = x_ref[...]
        r = lax.rsqrt(jnp.mean(xf * xf, axis=-1, keepdims=True) + EPS)
        h = xf * r * (mv_ref[0:1, :] * (1.0 + mv_ref[1:2, :])) + mv_ref[2:3, :]
        h_ref[...] = h.astype(BF16)

    in_specs = [pl.BlockSpec((tm, d), lambda i: (i, 0)), pl.BlockSpec((8, d), lambda i: (0, 0))]
    args = [src, mv]
    aliases = {}
    if prev is not None:
        in_specs += [ANY]
        args += [prev]
        aliases = {2: 0}
    return _call(
        body, name=name,
        grid=(rows // tm,),
        out_shape=jax.ShapeDtypeStruct((la, d), BF16),
        in_specs=in_specs,
        out_specs=pl.BlockSpec((tm, d), lambda i: (blk0 + i, 0)),
        input_output_aliases=aliases,
        compiler_params=_params(("arbitrary",)),
    )(*args)


def _gather_order(step, c):
    first = jnp.where(c == 1, 4, 2)
    other = 6 - first
    order = (0, 1, first, other + 1, other, first + 1, 6, 7)
    dist = order[7]
    for s in range(6, -1, -1):
        dist = jnp.where(step == s, order[s], dist)
    return dist


def _in_projection(h, w_shard, tm):
    la, d = h.shape
    bw = w_shard.shape[1]
    ni = la // tm
    where = jnp.stack([_idx(_my_pos()), lax.axis_index("c")]).astype(jnp.int32)

    def body(me_ref, h_ref, w_ref, p_ref, all_ref, wbuf, send_sems, recv_sems, local_sems):
        s, i = pl.program_id(0), pl.program_id(1)
        x, y, c = _my_pos()
        me, sibling = (x, y, c), (x, y, 1 - c)
        north = c == 1
        chips = [(jnp.where(north, 1 - x, x), jnp.where(north, y, 1 - y)),
                 (jnp.where(north, x, 1 - x), jnp.where(north, 1 - y, y)),
                 (1 - x, 1 - y)]
        theirs = (1, 0, 2)

        def copy(k, block, to, from_shard=False):
            return pltpu.make_async_remote_copy(
                src_ref=w_ref if from_shard else all_ref.at[_idx(block)], dst_ref=all_ref.at[_idx(block)],
                send_sem=send_sems.at[k], recv_sem=recv_sems.at[k], device_id=to, device_id_type=MESH)

        def load(block, slot):
            return pltpu.make_async_copy(all_ref.at[_idx(block)], wbuf.at[slot], local_sems.at[1])

        keep = pltpu.make_async_copy(w_ref, all_ref.at[_idx(me)], local_sems.at[0])
        first = [copy(0, me, sibling, True)] + [copy(1 + j, me, (*chip, c), True) for j, chip in enumerate(chips)]
        passed = [copy(4 + theirs[j], (*chip, c), sibling) for j, chip in enumerate(chips)]
        direct = [(copy(1 + j, (*chip, c), me), passed[j], (*chip, c)) for j, chip in enumerate(chips)]
        handed = [(copy(4 + j, (*chip, 1 - c), me), None, (*chip, 1 - c)) for j, chip in enumerate(chips)]
        steps = [(copy(0, sibling, me), None, sibling),
                 direct[0], handed[1], direct[1], handed[0], direct[2], handed[2]]

        @pl.when((s == 0) & (i == 0))
        def _():
            keep.start()
            mine = pltpu.make_async_copy(w_ref, wbuf.at[0], local_sems.at[1])
            mine.start()
            first[0].start()
            first[1].start()
            mine.wait()

        for later in (1, 2):
            @pl.when((s == later) & (i == 0))
            def _(later=later):
                first[1 + later].start()

        for n, (arrival, forward, block) in enumerate(steps, start=1):
            @pl.when((s == n - 1) & (i == ni - 1))
            def _(arrival=arrival, forward=forward, block=block, n=n):
                arrival.wait_recv()
                if forward is not None:
                    forward.start()
                load(block, n % 2).start()

        @pl.when((s > 0) & (i == 0))
        def _():
            load(me, s % 2).wait()

        p_ref[...] = _dot(h_ref[...], wbuf[s % 2]).astype(BF16)

        @pl.when((s == NDEV - 1) & (i == ni - 1))
        def _():
            for cp in first + passed:
                cp.wait_send()
            keep.wait()

    return _call(
        body, name="in_projection",
        grid_spec=pltpu.PrefetchScalarGridSpec(
            num_scalar_prefetch=1, grid=(NDEV, ni),
            in_specs=[pl.BlockSpec((tm, d), lambda s, i, me_ref: (i, 0)), ANY],
            out_specs=(pl.BlockSpec((tm, bw), lambda s, i, me_ref: (i, me_ref[0] ^ _gather_order(s, me_ref[1]))),
                       ANY),
            scratch_shapes=[pltpu.VMEM((2, d, bw), BF16), pltpu.SemaphoreType.DMA((7,)),
                            pltpu.SemaphoreType.DMA((7,)), pltpu.SemaphoreType.DMA((2,))]),
        out_shape=(jax.ShapeDtypeStruct((la, NDEV * bw), BF16), jax.ShapeDtypeStruct((NDEV, d, bw), BF16)),
        compiler_params=_params(("arbitrary", "arbitrary")),
    )(where, h, w_shard)


def _conv_input(p, wcb, taps_m, l, t):
    la = p.shape[0]
    w = wcb.shape[1]
    nt = l // t

    def body(v_ref, wcb_ref, tm_ref, xb_ref):
        taps = _dot(tm_ref[...].reshape(4 * t, t), v_ref[...])
        xb = wcb_ref[4:5, :] + wcb_ref[0:1, :] * taps[0:t]
        for j in range(1, 4):
            xb = xb + wcb_ref[j:j + 1, :] * taps[j * t:(j + 1) * t]
        xb_ref[...] = xb

    return _call(
        body, name="conv_input",
        grid=(nt + 1,),
        out_shape=jax.ShapeDtypeStruct((la, w), F32),
        in_specs=[pl.BlockSpec((t, w), lambda i: (i, 4)), pl.BlockSpec((8, w), lambda i: (0, 0)),
                  pl.BlockSpec((None, 4, t, t), lambda i: (i // nt, 0, 0, 0))],
        out_specs=pl.BlockSpec((t, w), lambda i: (i, 0)),
        compiler_params=_params(("arbitrary",)),
    )(p, wcb, taps_m)


def _lru_forward(xb, wg, lv, wo_shard, l, t):
    la, w = xb.shape
    gc = wg.shape[2]
    nt = l // t

    def body(xf_ref, xr_ref, wg_ref, lv_ref, wo_ref, hf_ref, hr_ref, wo_all,
             a_s, b_s, carry, send_sems, recv_sems):
        sends, arrivals = _gather_copies(wo_ref, wo_all, send_sems, recv_sems)

        @pl.when(pl.program_id(0) == 0)
        def _():
            carry[...] = jnp.zeros_like(carry)
            for cp in sends:
                cp.start()

        @pl.when(pl.program_id(0) == nt)
        def _():
            _exchange_wait(sends, arrivals)

        for dr, (x_ref, h_ref) in enumerate(((xf_ref, hf_ref), (xr_ref, hr_ref))):
            x = x_ref[...]
            a, s, _, _, ti, _ = _lru_coef(x, wg_ref, dr, lv_ref[3 * dr:3 * dr + 1, :],
                                          lv_ref[3 * dr + 1:3 * dr + 2, :], lv_ref[3 * dr + 2:3 * dr + 3, :], gc)
            a_s[...] = a
            b_s[...] = (s * x) * (0.5 * ti + 0.5)
            carry[dr] = _scan_tile(a_s, b_s, h_ref, carry[dr], dr == 1)

    full = lambda shape: pl.BlockSpec(shape, lambda i: (0,) * len(shape))
    fmap = lambda i: (jnp.where(i == 0, nt, i - 1), 0)
    rmap = lambda i: (jnp.where(i == 0, nt, nt - i), 0)
    return _call(
        body, name="lru_forward",
        grid=(nt + 1,),
        out_shape=(jax.ShapeDtypeStruct((la, w), F32), jax.ShapeDtypeStruct((la, w), F32),
                   jax.ShapeDtypeStruct((NDEV,) + wo_shard.shape, wo_shard.dtype)),
        in_specs=[pl.BlockSpec((t, w), fmap), pl.BlockSpec((t, w), rmap), full(wg.shape), full(lv.shape), ANY],
        out_specs=(pl.BlockSpec((t, w), fmap), pl.BlockSpec((t, w), rmap), ANY),
        scratch_shapes=[pltpu.VMEM((t, w), F32), pltpu.VMEM((t, w), F32), pltpu.VMEM((2, 8, w), F32),
                        pltpu.SemaphoreType.DMA((NDEV,)), pltpu.SemaphoreType.DMA((NDEV,))],
        compiler_params=_params(("arbitrary",)),
    )(xb, xb, wg, lv, wo_shard)


def _mix_gates(p_refs, hf_ref, hr_ref, wca_ref, perm_ref, t, w):
    bl, cl, ul, gl, ql = [r[...].astype(F32) for r in p_refs]
    tt = cl * ul
    tt16 = tt.astype(BF16)
    beside = _dot(perm_ref[2:4].reshape(2 * t, t), tt16)
    before, after = beside[:t], beside[t:]
    z = wca_ref[0:1, :] * before + wca_ref[1:2, :] * tt + wca_ref[2:3, :] * after
    sig_g = _sigmoid(gl)
    sig_q = _sigmoid(ql)
    ylru = _dot(perm_ref[1], (hf_ref[...] + hr_ref[...]).astype(BF16))
    return bl, cl, ul, gl, ql, (before, tt, after), z, sig_g, sig_q, ylru


def _p_specs(t, w, nt):
    return [pl.BlockSpec((t, w), functools.partial(lambda i, s: (jnp.minimum(i, nt - 1), s), s=s))
            for s in (0, 1, 2, 3, 5)]


def _mix_forward(x, tgt, p, hf, hr, wo, ov, wca, perm, t):
    l, d = x.shape
    w = d // 2
    nt = l // t

    def body(x_ref, tg_ref, b_ref, c_ref, u_ref, g_ref, q_ref, hf_ref, hr_ref, wo_ref, ov_ref, wca_ref, perm_ref,
             dn_ref, ct_ref, do_ref, part_ref):
        i = pl.program_id(0)
        bl, _, _, gl, ql, _, z, sig_g, sig_q, ylru = _mix_gates(
            (b_ref, c_ref, u_ref, g_ref, q_ref), hf_ref, hr_ref, wca_ref, perm_ref, t, w)
        ya = bl * z * (gl * sig_g)
        yb = ylru * (ql * sig_q)
        ct_ref[:, 0:w] = ya.astype(BF16)
        ct_ref[:, w:] = yb.astype(BF16)
        out = _dot(ya.astype(BF16), wo_ref[0:w, :]) + _dot(yb.astype(BF16), wo_ref[w:, :])
        gate, fg = ov_ref[0:1, :], ov_ref[1:2, :]
        n = x_ref[...] + gate * out
        rr = lax.rsqrt(jnp.mean(n * n, axis=-1, keepdims=True) + EPS)
        nh = n * rr
        e = nh * fg - tg_ref[...]
        loss = 0.5 * jnp.sum(jnp.mean(e * e, axis=-1, keepdims=True), axis=0, keepdims=True)
        dy = e * (1.0 / d)
        dnh = dy * fg
        dn = rr * (dnh - nh * jnp.mean(dnh * nh, axis=-1, keepdims=True))
        dn_ref[...] = dn.astype(BF16)
        do_ref[...] = (dn * gate).astype(BF16)

        @pl.when(i == 0)
        def _():
            part_ref[...] = jnp.zeros_like(part_ref)

        part_ref[0:1, :] += jnp.sum(dy * nh, axis=0, keepdims=True)
        part_ref[1:2, :] += jnp.sum(dn * out, axis=0, keepdims=True)
        part_ref[2:3, :] += jnp.broadcast_to(loss, (1, d))

    tile = lambda cols: pl.BlockSpec((t, cols), lambda i: (i, 0))
    full = lambda shape: pl.BlockSpec(shape, lambda i: (0,) * len(shape))
    return _call(
        body, name="mix_forward",
        grid=(nt,),
        out_shape=(jax.ShapeDtypeStruct((l, d), BF16), jax.ShapeDtypeStruct((l, d), BF16),
                   jax.ShapeDtypeStruct((l, d), BF16), jax.ShapeDtypeStruct((8, d), F32)),
        in_specs=[tile(d), tile(d)] + _p_specs(t, w, nt) + [tile(w), tile(w),
                  pl.BlockSpec((d, d), lambda i: (0, 0), pipeline_mode=pl.Buffered(1)),
                  full(ov.shape), full(wca.shape), full(perm.shape)],
        out_specs=(tile(d), tile(d), tile(d), full((8, d))),
        compiler_params=_params(("arbitrary",)),
    )(x, tgt, p, p, p, p, p, hf, hr, wo, ov, wca, perm)


def _mix_backward(dout, p, hf, hr, wo, wca, perm, g_wout, l, t):
    d = dout.shape[1]
    w = d // 2
    nt = l // t
    la = p.shape[0]

    def body(do_ref, b_ref, c_ref, u_ref, g_ref, q_ref, hf_ref, hr_ref, wo_ref, wca_ref, perm_ref, gw_ref,
             dp_ref, dh_ref, part_ref, sc_ref, send_sems, recv_sems):
        i = pl.program_id(0)
        copies = _scatter_copies(gw_ref, sc_ref, send_sems, recv_sems)

        @pl.when(i == 0)
        def _():
            part_ref[...] = jnp.zeros_like(part_ref)
            for cp in copies:
                cp.start()

        @pl.when(i == nt)
        def _():
            dp_ref[...] = jnp.zeros_like(dp_ref)
            _exchange_wait(copies, copies[1:])

        @pl.when(i < nt)
        def _():
            bl, cl, ul, gl, ql, taps, z, sig_g, sig_q, ylru = _mix_gates(
                (b_ref, c_ref, u_ref, g_ref, q_ref), hf_ref, hr_ref, wca_ref, perm_ref, t, w)
            do = do_ref[...]
            dya = _dot_nt(do, wo_ref[0:w, :])
            dyb = _dot_nt(do, wo_ref[w:, :])
            sg = gl * sig_g
            dz = dya * bl * sg
            dz16 = dz.astype(BF16)
            beside = _dot(perm_ref[2:4].reshape(2 * t, t), dz16)
            dt = wca_ref[0:1, :] * beside[t:] + wca_ref[1:2, :] * dz + wca_ref[2:3, :] * beside[:t]
            dp_ref[:, 0:w] = (dya * z * sg).astype(BF16)
            dp_ref[:, w:2 * w] = (dt * ul).astype(BF16)
            dp_ref[:, 2 * w:3 * w] = (dt * cl).astype(BF16)
            dp_ref[:, 3 * w:4 * w] = (dya * bl * z * (sig_g * (1.0 + gl * (1.0 - sig_g)))).astype(BF16)
            dp_ref[:, 4 * w:5 * w] = jnp.zeros((t, w), BF16)
            dp_ref[:, 5 * w:6 * w] = (dyb * ylru * (sig_q * (1.0 + ql * (1.0 - sig_q)))).astype(BF16)
            dh_ref[...] = _dot(perm_ref[0], (dyb * (ql * sig_q)).astype(BF16)).astype(BF16)
            for j in range(3):
                part_ref[j:j + 1, :] += jnp.sum(dz * taps[j], axis=0, keepdims=True)

    clamp = lambda cols: pl.BlockSpec((t, cols), lambda i: (jnp.minimum(i, nt - 1), 0))
    full = lambda shape: pl.BlockSpec(shape, lambda i: (0,) * len(shape))
    return _call(
        body, name="mix_backward",
        grid=(nt + 1,),
        out_shape=(jax.ShapeDtypeStruct((la, 6 * w), BF16), jax.ShapeDtypeStruct((l, w), BF16),
                   jax.ShapeDtypeStruct((8, w), F32), jax.ShapeDtypeStruct(g_wout.shape, g_wout.dtype)),
        in_specs=[clamp(d)] + _p_specs(t, w, nt) + [clamp(w), clamp(w),
                  pl.BlockSpec((d, d), lambda i: (0, 0), pipeline_mode=pl.Buffered(1)), full(wca.shape),
                  full(perm.shape), ANY],
        out_specs=(pl.BlockSpec((t, 6 * w), lambda i: (i, 0)), clamp(w), full((8, w)), ANY),
        scratch_shapes=[pltpu.SemaphoreType.DMA((NDEV,)), pltpu.SemaphoreType.DMA((NDEV,))],
        compiler_params=_params(("arbitrary",)),
    )(dout, p, p, p, p, p, hf, hr, wo, wca, perm, g_wout)


def _lru_backward(direction, xb, dhs, hs, wg, lv, l, t, conv=None):
    la, w = hs.shape
    gc = wg.shape[2]
    ng = w // gc
    nt = l // t
    nblk8 = la // 8
    last = conv is not None
    assert last == (direction == 1)

    if direction == 0:
        tile = lambda i: jnp.where(i == nt, nt, nt - 1 - i)
        halo = lambda i: jnp.where(tile(i) == 0, nblk8 - 1, tile(i) * (t // 8) - 1)
    else:
        tile = lambda i: i
        halo = lambda i: jnp.minimum((i + 1) * (t // 8), nblk8 - 1)

    def body(*refs):
        x_ref, dh_ref, hs_ref, halo_ref, wg_ref, lv_ref = refs[:6]
        if last:
            v_ref, wcb_ref, bm_ref, dxo_ref = refs[6:10]
        out_ref, dwg_ref, part_ref, a_s, dh_s, g_s, carry = refs[-7:]
        i = pl.program_id(0)
        is_ctx = i == nt

        @pl.when(i == 0)
        def _():
            carry[...] = jnp.zeros_like(carry)
            dwg_ref[...] = jnp.zeros_like(dwg_ref)
            part_ref[...] = jnp.zeros_like(part_ref)

        xb = x_ref[...]
        lam = lv_ref[3 * direction + 2:3 * direction + 3, :]
        a, s, rs, tr, ti, sp = _lru_coef(xb, wg_ref, direction, lv_ref[3 * direction:3 * direction + 1, :],
                                         lv_ref[3 * direction + 1:3 * direction + 2, :], lam, gc)
        hs_t = hs_ref[...]
        r8 = _rows((8, w))
        if direction == 0:
            edge = jnp.where(is_ctx, 0.0, halo_ref[7:8, :])
            first = jnp.where(r8 == 0, edge, pltpu.roll(hs_t[t - 8:, :], 1, 0))
            hprev = jnp.concatenate([first, hs_t[:t - 8, :]], axis=0)
        else:
            edge = jnp.where(is_ctx, 0.0, halo_ref[0:1, :])
            final = jnp.where(r8 == 7, edge, pltpu.roll(hs_t[:8, :], 7, 0))
            hprev = jnp.concatenate([hs_t[8:, :], final], axis=0)
        a_s[...] = a
        dh_s[...] = jnp.where(is_ctx, 0.0, dh_ref[...].astype(F32))
        carry[...] = _scan_tile_backward(a_s, dh_s, g_s, carry[...], direction == 0)

        g = g_s[...]
        r = 0.5 * tr + 0.5
        ig = 0.5 * ti + 0.5
        ix = ig * xb
        gs = g * s
        dla = (g * a) * (hprev - ix * (a * rs))
        dxb = gs * ig
        dzr = dla * (r * (1.0 - tr)) * (-LRU_C * sp)
        dzi = gs * ix * (1.0 - ti)
        part_ref[0:1, :] += jnp.sum(dzr, axis=0, keepdims=True)
        part_ref[1:2, :] += jnp.sum(dzi, axis=0, keepdims=True)
        part_ref[2:3, :] += jnp.sum(dla * r, axis=0, keepdims=True) * (LRU_C * _sigmoid(-lam))
        pieces = []
        for gi in range(ng):
            sl = slice(gi * gc, (gi + 1) * gc)
            dz = jnp.concatenate([dzr[:, sl], dzi[:, sl]], axis=-1).astype(BF16)
            pieces.append(_dot_nt(dz, wg_ref[direction, gi]))
            dwg_ref[gi] += _dot(xb[:, sl].T.astype(BF16), dz)
        dxb = dxb + (pieces[0] if ng == 1 else jnp.concatenate(pieces, axis=-1))
        if not last:
            out_ref[...] = dxb
        else:
            dxb = dxb + dxo_ref[...]
            v = v_ref[...].astype(F32)
            backs = _dot(bm_ref[...].reshape(4 * t, t), dxb.astype(BF16))
            dv = jnp.zeros((t, w), F32)
            for j in range(4):
                back = backs[j * t:(j + 1) * t]
                dv = dv + wcb_ref[j:j + 1, :] * back
                part_ref[4 + j:5 + j, :] += jnp.sum(back * v, axis=0, keepdims=True)
            out_ref[...] = dv.astype(BF16)
            part_ref[3:4, :] += jnp.sum(dxb, axis=0, keepdims=True)

    full = lambda shape: pl.BlockSpec(shape, lambda i: (0,) * len(shape))
    kind = lambda i: (jnp.where(i == nt, 1, 0), 0, 0, 0)
    in_specs = [pl.BlockSpec((t, w), lambda i: (tile(i), 0)),
                pl.BlockSpec((t, w), lambda i: (jnp.minimum(tile(i), nt - 1), 0)),
                pl.BlockSpec((t, w), lambda i: (tile(i), 0)),
                pl.BlockSpec((8, w), lambda i: (halo(i), 0)),
                full(wg.shape), full(lv.shape)]
    args = [xb, dhs, hs, hs, wg, lv]
    if last:
        p, wcb, back_m, dxb_other, dp = conv
        in_specs += [pl.BlockSpec((t, w), lambda i: (tile(i), 4)), full(wcb.shape),
                     pl.BlockSpec((None, 4, t, t), kind), pl.BlockSpec((t, w), lambda i: (tile(i), 0)), ANY]
        args += [p, wcb, back_m, dxb_other, dp]
        out0 = jax.ShapeDtypeStruct(dp.shape, dp.dtype)
        spec0 = pl.BlockSpec((t, w), lambda i: (tile(i), 4))
        aliases = {10: 0}
    else:
        out0 = jax.ShapeDtypeStruct((la, w), F32)
        spec0 = pl.BlockSpec((t, w), lambda i: (tile(i), 0))
        aliases = {}
    return _call(
        body, name="lru_backward_%d" % direction,
        grid=(nt + 1,),
        out_shape=(out0, jax.ShapeDtypeStruct((ng, gc, 2 * gc), F32), jax.ShapeDtypeStruct((8, w), F32)),
        in_specs=in_specs,
        out_specs=(spec0, full((ng, gc, 2 * gc)), full((8, w))),
        scratch_shapes=[pltpu.VMEM((t, w), F32), pltpu.VMEM((t, w), F32), pltpu.VMEM((t, w), F32),
                        pltpu.VMEM((8, w), F32)],
        input_output_aliases=aliases,
        compiler_params=_params(("arbitrary",)),
    )(*args)


def _weight_grad_t(a, b, nblk_m, nblk_n, tk, name):
    k, m = a.shape
    n = b.shape[1]
    bm, bn = m // nblk_m, n // nblk_n
    nk = k // tk

    def body(a_ref, b_ref, o_ref, acc):
        kk = pl.program_id(2)

        @pl.when(kk == 0)
        def _():
            acc[...] = jnp.zeros_like(acc)

        acc[...] += lax.dot_general(a_ref[...], b_ref[...], (((0,), (0,)), ((), ())), preferred_element_type=F32)

        @pl.when(kk == nk - 1)
        def _():
            o_ref[...] = acc[...].astype(BF16)

    return _call(
        body, name=name,
        grid=(nblk_m, nblk_n, nk),
        out_shape=jax.ShapeDtypeStruct((nblk_m * nblk_n, bm, bn), BF16),
        in_specs=[pl.BlockSpec((tk, bm), lambda i, j, kk: (kk, i)),
                  pl.BlockSpec((tk, bn), lambda i, j, kk: (kk, j))],
        out_specs=pl.BlockSpec((None, bm, bn), lambda i, j, kk: (i * nblk_n + j, 0, 0)),
        scratch_shapes=[pltpu.VMEM((bm, bn), F32)],
        compiler_params=_params(("arbitrary", "arbitrary", "arbitrary")),
    )(a, b)


def _weight_grad_scatter(at, b, tk, name):
    k, m = at.shape
    n = b.shape[1]
    bn = n // NDEV
    nk = k // tk
    where = jnp.stack([_idx(_my_pos()), lax.axis_index("c")]).astype(jnp.int32)
    tn = (((0,), (0,)), ((), ()))

    def body(w_ref, a_ref, b_ref, recv_ref, acc, sbuf, sib, sib_send, sib_recv, chip_send, chip_recv, keep_sem):
        s, kk = pl.program_id(0), pl.program_id(1)
        x, y, c = _my_pos()

        @pl.when(kk == 0)
        def _():
            acc[...] = lax.dot_general(a_ref[...], b_ref[...], tn, preferred_element_type=F32)

        @pl.when(kk > 0)
        def _():
            acc[...] += lax.dot_general(a_ref[...], b_ref[...], tn, preferred_element_type=F32)

        def to_sibling(j):
            return pltpu.make_async_remote_copy(
                src_ref=sbuf.at[0], dst_ref=sib.at[j], send_sem=sib_send.at[j], recv_sem=sib_recv.at[j],
                device_id=(x, y, 1 - c), device_id_type=MESH)

        def to_chip(j):
            dist = _chip_order(j, c)
            return pltpu.make_async_remote_copy(
                src_ref=sbuf.at[1], dst_ref=recv_ref.at[dist // 2], send_sem=chip_send.at[j],
                recv_sem=chip_recv.at[dist // 2], device_id=_peer_at(dist), device_id_type=MESH)

        keep = pltpu.make_async_copy(sbuf.at[1], recv_ref.at[0], keep_sem)
        sends = []
        for j in range(4):
            sends += [to_sibling(j), to_chip(j) if j < 3 else keep]

        for st in range(NDEV):
            @pl.when((kk == nk - 1) & (s == st))
            def _(st=st):
                if st >= 2:
                    sends[st - 2].wait_send()
                part = acc[...]
                if st % 2 == 1:
                    to_sibling(st // 2).wait_recv()
                    part = part + sib[st // 2].astype(F32)
                sbuf[st % 2] = part.astype(BF16)
                sends[st].start()
                if st == NDEV - 1:
                    sends[st - 1].wait_send()
                    sends[st].wait()
                    for j in range(1, 4):
                        pltpu.make_async_remote_copy(
                            src_ref=sbuf.at[0], dst_ref=recv_ref.at[j], send_sem=chip_send.at[0],
                            recv_sem=chip_recv.at[j], device_id=_peer_at(2 * j), device_id_type=MESH).wait_recv()

    blk = lambda s, w_ref: w_ref[0] ^ _scatter_order(s, w_ref[1])
    return _call(
        body, name=name,
        grid_spec=pltpu.PrefetchScalarGridSpec(
            num_scalar_prefetch=1, grid=(NDEV, nk),
            in_specs=[pl.BlockSpec((tk, m), lambda s, kk, w_ref: (kk, 0)),
                      pl.BlockSpec((tk, bn), lambda s, kk, w_ref: (kk, blk(s, w_ref)))],
            out_specs=ANY,
            scratch_shapes=[pltpu.VMEM((m, bn), F32), pltpu.VMEM((2, m, bn), BF16), pltpu.VMEM((4, m, bn), BF16),
                            pltpu.SemaphoreType.DMA((4,)), pltpu.SemaphoreType.DMA((4,)),
                            pltpu.SemaphoreType.DMA((4,)), pltpu.SemaphoreType.DMA((4,)),
                            pltpu.SemaphoreType.DMA]),
        out_shape=jax.ShapeDtypeStruct((4, m, bn), BF16),
        compiler_params=_params(("arbitrary", "arbitrary")),
    )(where, at, b)


def _input_backward(dp, w_all, src, mv, row0, tm, nbk, name, dn=None):
    rows, d = src.shape
    nb, _, bw = w_all.shape
    nk = nb // nbk
    ni = rows // tm
    blk0 = row0 // tm
    latent = dn is not None

    def body(*refs):
        dp_ref, w_ref, x_ref, mv_ref = refs[:4]
        outs = refs[4 + latent:]
        part_ref, acc = outs[latent], outs[latent + 1]
        i, k = pl.program_id(0), pl.program_id(1)

        def product():
            step = _dot_nt(dp_ref[:, 0:bw], w_ref[0])
            for q in range(1, nbk):
                step = step + _dot_nt(dp_ref[:, q * bw:(q + 1) * bw], w_ref[q])
            return step

        def finish(slot):
            xf = x_ref[...]
            r = lax.rsqrt(jnp.mean(xf * xf, axis=-1, keepdims=True) + EPS)
            xn = xf * r
            dhl = acc[slot]
            gain, sc = mv_ref[0:1, :], mv_ref[1:2, :]
            dhx = jnp.sum(dhl * xn, axis=0, keepdims=True)
            part_ref[0:1, :] += jnp.sum(dhl, axis=0, keepdims=True)
            part_ref[1:2, :] += dhx * gain
            part_ref[2:3, :] += dhx * (1.0 + sc)
            if latent:
                dxn = dhl * (gain * (1.0 + sc))
                outs[0][...] = (refs[4][...].astype(F32)
                                + r * (dxn - xn * jnp.mean(dxn * xn, axis=-1, keepdims=True)))

        @pl.when((i == 0) & (k == 0))
        def _():
            part_ref[...] = jnp.zeros_like(part_ref)
            acc[0] = product()

        @pl.when((i > 0) & (i < ni) & (k == 0))
        def _():
            acc[i % 2] = product()
            finish((i - 1) % 2)

        @pl.when((i == ni) & (k == 0))
        def _():
            finish((ni - 1) % 2)

        @pl.when((i < ni) & (k > 0))
        def _():
            acc[i % 2] += product()

    tile = pl.BlockSpec((tm, d), lambda i, k: (jnp.maximum(i - 1, 0), 0))
    vec = pl.BlockSpec((8, d), lambda i, k: (0, 0))
    kblock = lambda i, k: jnp.where(i == ni, nk - 1, k)
    return _call(
        body, name=name,
        grid=(ni + 1, nk),
        out_shape=((jax.ShapeDtypeStruct((rows, d), F32),) if latent else ()) + (jax.ShapeDtypeStruct((8, d), F32),),
        in_specs=[pl.BlockSpec((tm, nbk * bw), lambda i, k: (blk0 + jnp.minimum(i, ni - 1), kblock(i, k))),
                  pl.BlockSpec((nbk, d, bw), lambda i, k: (kblock(i, k), 0, 0)), tile, vec]
                 + ([tile] if latent else []),
        out_specs=((tile,) if latent else ()) + (vec,),
        scratch_shapes=[pltpu.VMEM((2, tm, d), F32)],
        compiler_params=_params(("arbitrary", "arbitrary")),
    )(*([dp, w_all, src, mv] + ([dn] if latent else [])))


def _adamw_scattered(parts, w, m, v, tr):
    r, c = w.shape
    nslot = parts.shape[0]

    def body(p_ref, w_ref, m_ref, v_ref, g_ref, d_ref, m2_ref, v2_ref):
        g = p_ref[0].astype(F32)
        for k in range(1, nslot):
            g = g + p_ref[k].astype(F32)
        g_ref[...] = g
        d_ref[...], m2_ref[...], v2_ref[...] = _adamw(w_ref[...], g, m_ref[...], v_ref[...])

    tile = pl.BlockSpec((tr, c), lambda i: (i, 0))
    return _call(
        body, name="adamw_scattered_%dx%d" % (r, c),
        grid=(r // tr,),
        out_shape=tuple(jax.ShapeDtypeStruct((r, c), F32) for _ in range(4)),
        in_specs=[pl.BlockSpec((nslot, tr, c), lambda i: (0, i, 0)), tile, tile, tile],
        out_specs=(tile,) * 4,
        compiler_params=_params(("arbitrary",)),
    )(parts, w, m, v)


def _adamw_ada(st, dmod, w, m, v, tr):
    r, c = w.shape

    def body(s_ref, dm_ref, w_ref, m_ref, v_ref, g_ref, d_ref, m2_ref, v2_ref):
        g = jnp.dot(s_ref[...], dm_ref[...], precision=HIGHEST, preferred_element_type=F32)
        g_ref[...] = g
        d_ref[...], m2_ref[...], v2_ref[...] = _adamw(w_ref[...], g, m_ref[...], v_ref[...])

    tile = pl.BlockSpec((tr, c), lambda i: (i, 0))
    return _call(
        body, name="adamw_ada",
        grid=(r // tr,),
        out_shape=tuple(jax.ShapeDtypeStruct((r, c), F32) for _ in range(4)),
        in_specs=[pl.BlockSpec((tr, 16), lambda i: (i, 0)), pl.BlockSpec((16, c), lambda i: (0, 0)),
                  tile, tile, tile],
        out_specs=(tile,) * 4,
        compiler_params=_params(("arbitrary",)),
    )(st, dmod, w, m, v)


def _adamw_small(gs, ws, ms, vs):
    n = len(ws)

    def body(*refs):
        for j in range(n):
            g_ref, w_ref, m_ref, v_ref = refs[j], refs[n + j], refs[2 * n + j], refs[3 * n + j]
            d_ref, m2_ref, v2_ref = refs[4 * n + j], refs[5 * n + j], refs[6 * n + j]
            d_ref[...], m2_ref[...], v2_ref[...] = _adamw(w_ref[...], g_ref[...], m_ref[...], v_ref[...])

    shapes = tuple(jax.ShapeDtypeStruct(a.shape, F32) for a in ws)
    out = _call(
        body, name="adamw_small",
        out_shape=shapes * 3,
        in_specs=[VMEM] * (4 * n), out_specs=(VMEM,) * (3 * n),
        compiler_params=_params(),
    )(*gs, *ws, *ms, *vs)
    return list(out[:n]), list(out[n:2 * n]), list(out[2 * n:])


def _blockdiag_groups(wh, gc):
    h, dh, _ = wh.shape
    g = gc // dh
    w4 = wh.reshape(h // g, g, dh, dh)
    bd = jnp.einsum("ngij,gh->ngihj", w4, jnp.eye(g, dtype=wh.dtype))
    return bd.reshape(h // g, gc, gc)


def _blockdiag_extract(bd, dh):
    ng, gc, _ = bd.shape
    g = gc // dh
    x = bd.reshape(ng, g, dh, g, dh)
    return jnp.einsum("ngihj,gh->ngij", x, jnp.eye(g, dtype=bd.dtype)).reshape(ng * g, dh, dh)


def _largest_tile(n, cap):
    return max(q for q in range(128, min(n, cap) + 1, 128) if n % q == 0)


def _rows8(*vecs):
    rows = [jnp.reshape(v, (1, -1)).astype(F32) for v in vecs]
    n = rows[0].shape[1]
    return jnp.concatenate(rows + [jnp.zeros((8 - len(rows), n), F32)], axis=0)


def _pack(pieces):
    flat = jnp.concatenate([jnp.reshape(a, (-1,)).astype(F32) for a in pieces])
    total = -(-flat.shape[0] // 1024) * 1024
    return jnp.pad(flat, (0, total - flat.shape[0])).reshape(total // 128, 128)


def _unpack(packed, shapes):
    flat = packed.reshape(-1)
    out, off = [], 0
    for s in shapes:
        n = 1
        for q in s:
            n *= q
        out.append(flat[off:off + n].reshape(s))
        off += n
    return out


def kernel(x, c, ctx, c_ctx, norm_g, w_ada, b_ada, w_in, w_conv_a, w_conv_b, b_conv_b, lru_wa, lru_ba, lru_wx, lru_bx, lru_lambda, w_out, final_g, loss_target, m_c_ctx, m_norm_g, m_w_ada, m_b_ada, m_w_in, m_w_conv_a, m_w_conv_b, m_b_conv_b, m_lru_wa, m_lru_ba, m_lru_wx, m_lru_bx, m_lru_lambda, m_w_out, m_final_g, v_c_ctx, v_norm_g, v_w_ada, v_b_ada, v_w_in, v_w_conv_a, v_w_conv_b, v_b_conv_b, v_lru_wa, v_lru_ba, v_lru_wx, v_lru_bx, v_lru_lambda, v_w_out, v_final_g):
    _, l, d = x.shape
    lc = ctx.shape[1]
    w = d // 2
    t = lc
    assert l % t == 0 and t % GRID_W == 0 and t % 128 == 0
    dh = w // N_HEADS
    gc = min(w, MXU_WIDTH)
    cols = w_ada.shape[2]
    wo_rows = w_out.shape[1]
    me = _idx(_my_pos())
    x2, ctx2, tgt2 = x[0], ctx[0], loss_target[0]
    w_ada2, w_in2, w_out2 = w_ada[0], w_in[0], w_out[0]

    small_mine = jnp.concatenate([a.reshape(-1) for a in (w_conv_a, w_conv_b, lru_ba, lru_bx, lru_lambda)]
                                 + [jnp.zeros((3 * (w // NDEV),), F32)]).reshape(16, w // NDEV)
    mod_all, s_mat, small_all = _mod_forward(
        jnp.broadcast_to(c, (8, d)), jnp.broadcast_to(c_ctx[None], (8, d)), w_ada2, small_mine)
    mod = jnp.transpose(mod_all, (1, 0, 2)).reshape(16, NDEV * cols) + b_ada
    mod_lat = lax.dynamic_slice_in_dim(mod, me, 1, axis=0)
    sh_l, sc_l, gt_l = jnp.split(mod_lat, 3, axis=-1)
    sh_c, sc_c, _ = jnp.split(mod[8:9], 3, axis=-1)
    small = jnp.transpose(small_all, (1, 0, 2)).reshape(16, w)
    wca = _rows8(*[small[j] for j in range(0, 3)])
    wcb = _rows8(*[small[j] for j in range(3, 7)], b_conv_b)
    lv = _rows8(0.5 * small[7], 0.5 * small[9], small[11], 0.5 * small[8], 0.5 * small[10], small[12])
    wg = jnp.stack([
        jnp.concatenate([_blockdiag_groups(lru_wa[0, dr], gc), _blockdiag_groups(lru_wx[0, dr], gc)], axis=-1)
        for dr in range(2)])
    wg = (0.5 * wg).astype(BF16)

    la = l + lc
    tm = 2 * t if l % (2 * t) == 0 else t
    tk = 3 * t if la % (3 * t) == 0 else t
    h = _normalize(x2, _rows8(norm_g, sc_l, sh_l), la, 0, tm, "normalize")
    h = _normalize(ctx2, _rows8(norm_g, sc_c, sh_c), la, l, t, "normalize_ctx", prev=h)
    p, w_all = _in_projection(h, w_in2.astype(BF16), la // 4 if la % 64 == 0 else tk)
    taps_m, back_m, perm = _scan_matrices(t)
    xb = _conv_input(p, wcb, taps_m, l, t)
    hf, hr, wo_all = _lru_forward(xb, wg, lv, w_out2.astype(BF16), l, t)
    wo = wo_all.reshape(d, d)
    dn, cat, dout, part_mix = _mix_forward(x2, tgt2, p, hf, hr, wo, _rows8(gt_l, final_g), wca, perm, t)
    g_wout = _weight_grad_t(cat, dout, 2, 1, _largest_tile(l, 2048), "grad_w_out")
    dp, dhs, part_ca, sc_wout = _mix_backward(dout, p, hf, hr, wo, wca, perm, g_wout.reshape(NDEV, wo_rows, d), l, t)
    dxb0, dwg0, part_l0 = _lru_backward(0, xb, dhs, hf, wg, lv, l, t)
    dp, dwg1, part_l1 = _lru_backward(1, xb, dhs, hr, wg, lv, l, t, conv=(p, wcb, back_m, dxb0, dp))
    sc_win = _weight_grad_scatter(h, dp, tk, "grad_w_in")
    grad_x, part_lat = _input_backward(dp, w_all, x2, _rows8(norm_g, sc_l), 0, tm, 2, "input_backward", dn=dn)
    (part_ctx,) = _input_backward(dp, w_all, ctx2, _rows8(norm_g, sc_c), l, t, 2, "input_backward_ctx")
    part_in = jnp.concatenate([part_lat[0:2], part_ctx[0:2], (part_lat[2] + part_ctx[2])[None]], axis=0)

    dwa = jnp.stack([_blockdiag_extract(dwg0[:, :, :gc], dh), _blockdiag_extract(dwg1[:, :, :gc], dh)])
    dwx = jnp.stack([_blockdiag_extract(dwg0[:, :, gc:], dh), _blockdiag_extract(dwg1[:, :, gc:], dh)])
    lru_part = (0.5 * jnp.stack([dwa, dwx])).reshape(NDEV, -1, 128)
    zeros_d = jnp.zeros((d,), F32)
    pieces = [
        jnp.concatenate([part_in[0], part_in[1], part_mix[1]]),
        jnp.concatenate([part_in[2], part_in[3], zeros_d]),
        part_in[4], part_mix[0], part_ca[0:3], part_l1[4:8], part_l1[3],
        0.5 * jnp.stack([part_l0[0], part_l1[0]]), 0.5 * jnp.stack([part_l0[1], part_l1[1]]),
        jnp.stack([part_l0[2], part_l1[2]]), part_mix[2, 0:1],
    ]
    shapes = [(3 * d,), (3 * d,), (d,), (d,), (3, w), (4, w), (w,), (2, w), (2, w), (2, w), (1,)]
    sig_cc = jax.nn.sigmoid(c_ctx)
    dsilu_cc = jnp.broadcast_to((sig_cc * (1.0 + c_ctx * (1.0 - sig_cc)))[None], (8, d))
    psum, pall, lru_sum, g_cctx8 = _reduce_small(_pack(pieces), lru_part, w_ada2, dsilu_cc)
    (g_modl, g_modc, g_norm, g_final, g_ca, g_cb, g_bcb, g_ba, g_bx, g_lam, loss1) = _unpack(psum, shapes)
    loss = loss1[0]
    g_cctx = g_cctx8[0]
    g_bada = (g_modl + g_modc)[None]
    g_lru = lru_sum.reshape(2, 2, N_HEADS, dh, dh)
    g_wa, g_wx = g_lru[0][None], g_lru[1][None]
    wsl = w // NDEV
    mine = lambda a: lax.dynamic_slice_in_dim(a, me * wsl, wsl, axis=-1)
    g_ca_m, g_cb_m, g_ba_m, g_bx_m, g_lam_m = (mine(g_ca)[None], mine(g_cb)[None], mine(g_ba)[None],
                                               mine(g_bx)[None], mine(g_lam)[None])
    g_norm, g_bcb = g_norm[None], g_bcb[None]

    per_dev = pall[:, :3 * d // 128].reshape(NDEV, NDEV, cols)
    dmod_lat = lax.dynamic_slice_in_dim(per_dev, me, 1, axis=1)[:, 0]
    dmod_ctx = lax.dynamic_slice_in_dim(g_modc.reshape(NDEV, cols), me, 1, axis=0)
    dmod16 = jnp.concatenate([dmod_lat, dmod_ctx, jnp.zeros((7, cols), F32)], axis=0)
    tr_ada = 256 if d % 256 == 0 else d
    g_wada, d_wada, m_wada, v_wada = _adamw_ada(s_mat.T, dmod16, w_ada2, m_w_ada[0], v_w_ada[0], tr_ada)
    g_win2, d_win, m_win, v_win = _adamw_scattered(sc_win, w_in2, m_w_in[0], v_w_in[0], tr_ada)
    tr_out = 64 if wo_rows % 64 == 0 else wo_rows
    g_wout2, d_wout, m_wout, v_wout = _adamw_scattered(sc_wout, w_out2, m_w_out[0], v_w_out[0], tr_out)

    small_w = [c_ctx, norm_g, b_ada, w_conv_a, w_conv_b, b_conv_b, lru_wa, lru_ba, lru_wx, lru_bx, lru_lambda, final_g]
    small_m = [m_c_ctx, m_norm_g, m_b_ada, m_w_conv_a, m_w_conv_b, m_b_conv_b, m_lru_wa, m_lru_ba, m_lru_wx,
               m_lru_bx, m_lru_lambda, m_final_g]
    small_v = [v_c_ctx, v_norm_g, v_b_ada, v_w_conv_a, v_w_conv_b, v_b_conv_b, v_lru_wa, v_lru_ba, v_lru_wx,
               v_lru_bx, v_lru_lambda, v_final_g]
    small_g = [g_cctx, g_norm, g_bada, g_ca_m, g_cb_m, g_bcb, g_wa, g_ba_m, g_wx, g_bx_m, g_lam_m, g_final]
    small_g = [jnp.reshape(a, b.shape) for a, b in zip(small_g, small_w)]
    d_s, m_s, v_s = _adamw_small(small_g, small_w, small_m, small_v)

    def weights(small_list, ada, win, wout):
        (cctx_, norm_, bada_, ca_, cb_, bcb_, wa_, ba_, wx_, bx_, lam_, final_) = small_list
        return [cctx_, norm_, ada[None], bada_, win[None], ca_, cb_, bcb_, wa_, ba_, wx_, bx_, lam_, wout[None], final_]

    return (loss, grad_x[None],
            *weights(small_g, g_wada, g_win2, g_wout2), *weights(d_s, d_wada, d_win, d_wout),
            *weights(m_s, m_wada, m_win, m_wout), *weights(v_s, v_wada, v_win, v_wout))
```

```python
import functools

import jax
import jax.numpy as jnp
import numpy as np
from jax import lax
from jax.experimental import pallas as pl
from jax.experimental.pallas import tpu as pltpu

F32 = jnp.float32
BF16 = jnp.bfloat16
MESH = pl.DeviceIdType.MESH
NDEV = 8
GRID_W = 64
N_HEADS = 16
LRU_C = 8.0
EPS = 1e-6
MXU_WIDTH = 256
VMEM_LIMIT = 60 * 1024 * 1024

ADAM_LR = 0.001
ADAM_B1 = 0.9
ADAM_B2 = 0.999
ADAM_EPS = 1e-08
ADAM_WD = 0.01
ADAM_STEP = 10
ADAM_C1 = 1.0 - ADAM_B1 ** ADAM_STEP
ADAM_C2 = 1.0 - ADAM_B2 ** ADAM_STEP

HIGHEST = lax.Precision.HIGHEST
ANY = pl.BlockSpec(memory_space=pl.ANY)
VMEM = pl.BlockSpec(memory_space=pltpu.VMEM)


def _call(body, **kw):
    return pl.pallas_call(body, **kw)


def _params(sem=None, vmem=VMEM_LIMIT):
    return pltpu.CompilerParams(dimension_semantics=sem, vmem_limit_bytes=vmem)


def _my_pos():
    return lax.axis_index("x"), lax.axis_index("y"), lax.axis_index("c")


def _idx(pos):
    return 4 * pos[0] + 2 * pos[1] + pos[2]


def _peer(k):
    x, y, c = _my_pos()
    return ((1 - x) if (k >> 2) & 1 else x, (1 - y) if (k >> 1) & 1 else y, (1 - c) if k & 1 else c)


def _exchange_start(src_ref, dst_ref, send_sems, recv_sems, base):
    me = _idx(_my_pos())
    sends = []
    for k in range(1, NDEV):
        cp = pltpu.make_async_remote_copy(
            src_ref=src_ref, dst_ref=dst_ref.at[me], send_sem=send_sems.at[base + k - 1],
            recv_sem=recv_sems.at[base + k - 1], device_id=_peer(k), device_id_type=MESH)
        cp.start()
        sends.append(cp)
    dst_ref[me] = src_ref[...]
    return sends, (src_ref, dst_ref, send_sems, recv_sems, base)


def _exchange_finish(started):
    sends, (src_ref, dst_ref, send_sems, recv_sems, base) = started
    for k in range(1, NDEV):
        peer = _peer(k)
        pltpu.make_async_remote_copy(
            src_ref=src_ref, dst_ref=dst_ref.at[_idx(peer)], send_sem=send_sems.at[base + k - 1],
            recv_sem=recv_sems.at[base + k - 1], device_id=peer, device_id_type=MESH).wait_recv()
    for cp in sends:
        cp.wait_send()


def _exchange_vmem(src_ref, dst_ref, send_sems, recv_sems, base):
    _exchange_finish(_exchange_start(src_ref, dst_ref, send_sems, recv_sems, base))


def _sigmoid(z):
    return 0.5 * jnp.tanh(0.5 * z) + 0.5


def _softplus(x):
    return jnp.maximum(x, 0.0) + jnp.log1p(jnp.exp(-jnp.abs(x)))


def _one_minus_sq(a, la):
    series = (-2.0 * la) * (1.0 + la)
    return jnp.where(la > -0.0015, series, 1.0 - a * a)


def _dot(a, b):
    return jnp.dot(a, b, preferred_element_type=F32)


def _dot_nt(a, b):
    return lax.dot_general(a, b, (((1,), (1,)), ((), ())), preferred_element_type=F32)


def _rows(shape):
    return lax.broadcasted_iota(jnp.int32, shape, 0)


def _scan_matrices(t):
    seg = t // 8
    r = np.arange(t)
    perm = (np.arange(t)[None, :] == ((r % 8) * seg + r // 8)[:, None]).astype(np.float32)
    rows, cols = r[:, None], r[None, :]
    taps, back = [], []
    for rowlen in (GRID_W, t):
        pos = rows % rowlen
        shift = {-2: (cols == rows - 2) & (pos >= 2), -1: (cols == rows - 1) & (pos >= 1),
                 0: cols == rows, 1: (cols == rows + 1) & (pos + 1 < rowlen),
                 2: (cols == rows + 2) & (pos + 2 < rowlen)}
        if rowlen == GRID_W:
            beside = [shift[-1].astype(np.float32), shift[1].astype(np.float32)]
        taps.append(np.stack([perm @ shift[k].astype(np.float32) for k in (-2, -1, 0, 1)]))
        back.append(np.stack([shift[k].astype(np.float32) @ perm.T for k in (2, 1, 0, -1)]))
    as_bf16 = lambda a: jnp.asarray(a, dtype=BF16)
    return as_bf16(np.stack(taps)), as_bf16(np.stack(back)), as_bf16(np.stack([perm, perm.T] + beside))


def _chunk_scan(a, b, reverse):
    row = _rows(a.shape)
    for s in (1, 2, 4):
        if reverse:
            m = row < 8 - s
            sh = 8 - s
        else:
            m = row >= s
            sh = s
        a_s = jnp.where(m, pltpu.roll(a, sh, 0), 1.0)
        b_s = jnp.where(m, pltpu.roll(b, sh, 0), 0.0)
        b = b + a * b_s
        a = a * a_s
    return a, b


def _chain_segments(ptot, hend, carry, reverse):
    ca, cb = _chunk_scan(ptot, hend, reverse)
    incl = ca * carry + cb
    r8 = _rows(incl.shape)
    if reverse:
        start = jnp.where(r8 < 7, pltpu.roll(incl, 7, 0), carry)
        last = incl[0:1, :]
    else:
        start = jnp.where(r8 >= 1, pltpu.roll(incl, 1, 0), carry)
        last = incl[7:8, :]
    return start, jnp.broadcast_to(last, incl.shape)


def _blocks(nblock, reverse):
    order = range(nblock - 1, -1, -1) if reverse else range(nblock)
    return [slice(8 * k, 8 * k + 8) for k in order]


def _scan_tile(a_ref, b_ref, out_ref, carry, reverse):
    t, w = a_ref.shape
    seg = t // 8

    hend, ptot = jnp.zeros((8, w), F32), jnp.ones((8, w), F32)
    for rows in _blocks(seg, reverse):
        a = a_ref[rows, :]
        hend, ptot = a * hend + b_ref[rows, :], a * ptot
    h, new_carry = _chain_segments(ptot, hend, carry, reverse)
    for rows in _blocks(seg, reverse):
        h = a_ref[rows, :] * h + b_ref[rows, :]
        out_ref[rows, :] = h
    return new_carry


def _scan_tile_backward(a_ref, dh_ref, g_ref, carry, reverse):
    t, w = a_ref.shape
    seg = t // 8

    uend, ptot = jnp.zeros((8, w), F32), jnp.ones((8, w), F32)
    for rows in _blocks(seg, reverse):
        a = a_ref[rows, :]
        uend, ptot = a * (dh_ref[rows, :] + uend), a * ptot
    u, new_carry = _chain_segments(ptot, uend, carry, reverse)
    for rows in _blocks(seg, reverse):
        g = dh_ref[rows, :] + u
        g_ref[rows, :] = g
        u = a_ref[rows, :] * g
    return new_carry


def _lru_coef(xb, wg_ref, d, ba, bx, lam, gc):
    w = xb.shape[1]
    xb16 = xb.astype(BF16)
    zr, zi = [], []
    for g in range(w // gc):
        z = _dot(xb16[:, g * gc:(g + 1) * gc], wg_ref[d, g])
        zr.append(z[:, :gc])
        zi.append(z[:, gc:])
    zr = zr[0] if len(zr) == 1 else jnp.concatenate(zr, axis=-1)
    zi = zi[0] if len(zi) == 1 else jnp.concatenate(zi, axis=-1)
    tr = jnp.tanh(zr + ba)
    ti = jnp.tanh(zi + bx)
    sp = _softplus(-lam)
    half = -0.5 * LRU_C * sp
    la = tr * half + half
    a = jnp.exp(la)
    q = _one_minus_sq(a, la)
    rs = lax.rsqrt(jnp.maximum(q, 1e-30))
    return a, q * rs, rs, tr, ti, sp


def _adamw(w, g, m, v):
    m2 = ADAM_B1 * m + (1.0 - ADAM_B1) * g
    v2 = ADAM_B2 * v + (1.0 - ADAM_B2) * (g * g)
    m_hat = m2 / ADAM_C1
    v_hat = v2 / ADAM_C2
    delta = -ADAM_LR * (m_hat / (jnp.sqrt(v_hat) + ADAM_EPS) + ADAM_WD * w)
    return delta, m2, v2


def _mod_forward(c8, cctx8, w_ada, small):
    d = c8.shape[1]
    cols = w_ada.shape[1]

    def body(c_ref, cctx_ref, w_ref, sm_ref, mod_ref, s_ref, sm_all, cbuf, mod_my, send_sems, recv_sems):
        _exchange_vmem(sm_ref, sm_all, send_sems, recv_sems, 2 * (NDEV - 1))
        _exchange_vmem(c_ref, cbuf, send_sems, recv_sems, 0)
        row = _rows((8, d))
        c_all = jnp.zeros((8, d), F32)
        for b in range(NDEV):
            c_all = jnp.where(row == b, cbuf[b], c_all)
        cc = cctx_ref[...]
        s_top = c_all * _sigmoid(c_all)
        s_bot = jnp.where(row == 0, cc * _sigmoid(cc), 0.0)
        s = jnp.concatenate([s_top, s_bot], axis=0)
        s_ref[...] = s
        mod_my[...] = jnp.dot(s, w_ref[...], precision=HIGHEST, preferred_element_type=F32)
        _exchange_vmem(mod_my, mod_ref, send_sems, recv_sems, NDEV - 1)

    return _call(
        body, name="mod_forward",
        out_shape=(jax.ShapeDtypeStruct((NDEV, 16, cols), F32), jax.ShapeDtypeStruct((16, d), F32),
                   jax.ShapeDtypeStruct((NDEV,) + small.shape, F32)),
        in_specs=[VMEM] * 4, out_specs=(VMEM,) * 3,
        scratch_shapes=[pltpu.VMEM((NDEV, 8, d), F32), pltpu.VMEM((16, cols), F32),
                        pltpu.SemaphoreType.DMA((3 * (NDEV - 1),)), pltpu.SemaphoreType.DMA((3 * (NDEV - 1),))],
        compiler_params=_params(),
    )(c8, cctx8, w_ada, small)


def _scatter_copies(src_ref, dst_ref, send_sems, recv_sems):
    me = _idx(_my_pos())
    copies = [pltpu.make_async_copy(src_ref.at[me], dst_ref.at[0], send_sems.at[0])]
    for k in range(1, NDEV):
        peer = _peer(k)
        copies.append(pltpu.make_async_remote_copy(
            src_ref=src_ref.at[_idx(peer)], dst_ref=dst_ref.at[k], send_sem=send_sems.at[k],
            recv_sem=recv_sems.at[k], device_id=peer, device_id_type=MESH))
    return copies


def _gather_copies(src_ref, dst_ref, send_sems, recv_sems):
    me = _idx(_my_pos())
    sends = [pltpu.make_async_copy(src_ref, dst_ref.at[me], send_sems.at[0])]
    arrivals = []
    for k in range(1, NDEV):
        peer = _peer(k)
        sends.append(pltpu.make_async_remote_copy(
            src_ref=src_ref, dst_ref=dst_ref.at[me], send_sem=send_sems.at[k],
            recv_sem=recv_sems.at[k], device_id=peer, device_id_type=MESH))
        arrivals.append(pltpu.make_async_remote_copy(
            src_ref=src_ref, dst_ref=dst_ref.at[_idx(peer)], send_sem=send_sems.at[k],
            recv_sem=recv_sems.at[k], device_id=peer, device_id_type=MESH))
    return sends, arrivals


def _exchange_wait(sends, arrivals):
    sends[0].wait()
    for cp in arrivals:
        cp.wait_recv()
    for cp in sends[1:]:
        cp.wait_send()


def _chip_order(k, c):
    return (6, 4 - 2 * c, 2 + 2 * c, 0)[k]


def _scatter_order(s, c):
    k = s >> 1
    mine = jnp.where(k == 0, 6, jnp.where(k == 1, 4 - 2 * c, jnp.where(k == 2, 2 + 2 * c, 0)))
    theirs = jnp.where(k == 0, 6, jnp.where(k == 1, 2 + 2 * c, jnp.where(k == 2, 4 - 2 * c, 0))) ^ 1
    return jnp.where((s & 1) == 0, theirs, mine)


def _peer_at(dist):
    x, y, c = _my_pos()
    return (x ^ ((dist >> 2) & 1), y ^ ((dist >> 1) & 1), c ^ (dist & 1))


def _reduce_small(packed, lru_parts, w_ada, dsilu_cctx):
    rp = packed.shape[0]
    rl = lru_parts.shape[1]
    d, cols = w_ada.shape
    assert cols % 128 == 0
    cb = cols // 128

    def body(p_ref, l_ref, w_ref, ds_ref, sum_ref, all_ref, lru_ref, cctx_ref,
             lbuf, lsum, cpart, call, send_sems, recv_sems, lsend, lrecv):
        me = _idx(_my_pos())
        scattered = _scatter_copies(l_ref, lbuf, lsend, lrecv)
        for cp in scattered:
            cp.start()
        _exchange_vmem(p_ref, all_ref, send_sems, recv_sems, 0)
        acc = all_ref[0]
        for j in range(1, NDEV):
            acc = acc + all_ref[j]
        sum_ref[...] = acc
        _exchange_wait(scattered, scattered[1:])
        red = lbuf[0]
        for k in range(1, NDEV):
            red = red + lbuf[k]
        lsum[...] = red
        lru_gather = _exchange_start(lsum, lru_ref, send_sems, recv_sems, NDEV - 1)
        part = jnp.zeros((8, d), F32)
        for q in range(cb):
            dm = jnp.broadcast_to(sum_ref[pl.ds((NDEV + me) * cb + q, 1), :], (8, 128))
            part = part + lax.dot_general(dm, w_ref[:, q * 128:(q + 1) * 128],
                                          (((1,), (1,)), ((), ())), precision=HIGHEST,
                                          preferred_element_type=F32)
        cpart[...] = part
        _exchange_vmem(cpart, call, send_sems, recv_sems, 2 * (NDEV - 1))
        _exchange_finish(lru_gather)
        tot = call[0]
        for j in range(1, NDEV):
            tot = tot + call[j]
        cctx_ref[...] = tot * ds_ref[...]

    return _call(
        body, name="reduce_small",
        out_shape=(jax.ShapeDtypeStruct((rp, 128), F32), jax.ShapeDtypeStruct((NDEV, rp, 128), F32),
                   jax.ShapeDtypeStruct((NDEV, rl, 128), F32), jax.ShapeDtypeStruct((8, d), F32)),
        in_specs=[VMEM] * 4, out_specs=(VMEM,) * 4,
        scratch_shapes=[pltpu.VMEM((NDEV, rl, 128), F32), pltpu.VMEM((rl, 128), F32), pltpu.VMEM((8, d), F32),
                        pltpu.VMEM((NDEV, 8, d), F32),
                        pltpu.SemaphoreType.DMA((3 * (NDEV - 1),)), pltpu.SemaphoreType.DMA((3 * (NDEV - 1),)),
                        pltpu.SemaphoreType.DMA((NDEV,)), pltpu.SemaphoreType.DMA((NDEV,))],
        compiler_params=_params(),
    )(packed, lru_parts, w_ada, dsilu_cctx)


def _normalize(src, mv, la, row0, tm, name, prev=None):
    rows, d = src.shape
    blk0 = row0 // tm

    def body(*refs):
        x_ref, mv_ref, h_ref = refs[0], refs[1], refs[-1]
        xf = x_ref[...]
        r = lax.rsqrt(jnp.mean(xf * xf, axis=-1, keepdims=True) + EPS)
        h = xf * r * (mv_ref[0:1, :] * (1.0 + mv_ref[1:2, :])) + mv_ref[2:3, :]
        h_ref[...] = h.astype(BF16)

    in_specs = [pl.BlockSpec((tm, d), lambda i: (i, 0)), pl.BlockSpec((8, d), lambda i: (0, 0))]
    args = [src, mv]
    aliases = {}
    if prev is not None:
        in_specs += [ANY]
        args += [prev]
        aliases = {2: 0}
    return _call(
        body, name=name,
        grid=(rows // tm,),
        out_shape=jax.ShapeDtypeStruct((la, d), BF16),
        in_specs=in_specs,
        out_specs=pl.BlockSpec((tm, d), lambda i: (blk0 + i, 0)),
        input_output_aliases=aliases,
        compiler_params=_params(("arbitrary",)),
    )(*args)


def _gather_order(step):
    return (step & 1) | (((step >> 2) & 1) << 1) | (((step >> 1) & 1) << 2)


def _in_projection(h, w_shard, wo_shard, tm):
    la, d = h.shape
    bw = w_shard.shape[1]
    ni = la // tm
    where = jnp.reshape(_idx(_my_pos()), (1,)).astype(jnp.int32)

    def body(me_ref, h_ref, w_ref, wo_ref, p_ref, all_ref, wo_all, wbuf, send_sems, recv_sems, local_sems,
             wo_send, wo_recv):
        s, i = pl.program_id(0), pl.program_id(1)
        x, y, c = _my_pos()
        wo_sends, wo_arrivals = _gather_copies(wo_ref, wo_all, wo_send, wo_recv)

        @pl.when((s == NDEV // 2) & (i == 0))
        def _():
            for cp in wo_sends:
                cp.start()

        me, sibling = (x, y, c), (x, y, 1 - c)
        chips = [(1 - x, y), (x, 1 - y), (1 - x, 1 - y)]

        def copy(k, block, to, from_shard=False):
            return pltpu.make_async_remote_copy(
                src_ref=w_ref if from_shard else all_ref.at[_idx(block)], dst_ref=all_ref.at[_idx(block)],
                send_sem=send_sems.at[k], recv_sem=recv_sems.at[k], device_id=to, device_id_type=MESH)

        def load(block, slot):
            return pltpu.make_async_copy(all_ref.at[_idx(block)], wbuf.at[slot], local_sems.at[1])

        keep = pltpu.make_async_copy(w_ref, all_ref.at[_idx(me)], local_sems.at[0])
        first = [copy(0, me, sibling, True)] + [copy(1 + j, me, (*chip, c), True) for j, chip in enumerate(chips)]
        passed = [copy(4 + j, (*chip, c), sibling) for j, chip in enumerate(chips)]
        steps = [(copy(0, sibling, me), None, sibling)]
        for j, chip in enumerate(chips):
            steps.append((copy(1 + j, (*chip, c), me), passed[j], (*chip, c)))
            steps.append((copy(4 + j, (*chip, 1 - c), me), None, (*chip, 1 - c)))

        @pl.when((s == 0) & (i == 0))
        def _():
            keep.start()
            mine = pltpu.make_async_copy(w_ref, wbuf.at[0], local_sems.at[1])
            mine.start()
            for cp in first:
                cp.start()
            mine.wait()

        for n, (arrival, forward, block) in enumerate(steps, start=1):
            @pl.when((s == n - 1) & (i == ni - 1))
            def _(arrival=arrival, forward=forward, block=block, n=n):
                arrival.wait_recv()
                if forward is not None:
                    forward.start()
                load(block, n % 2).start()

        @pl.when((s > 0) & (i == 0))
        def _():
            load(me, s % 2).wait()

        p_ref[...] = _dot(h_ref[...], wbuf[s % 2]).astype(BF16)

        @pl.when((s == NDEV - 1) & (i == ni - 1))
        def _():
            for cp in first + passed:
                cp.wait_send()
            keep.wait()
            _exchange_wait(wo_sends, wo_arrivals)

    return _call(
        body, name="in_projection",
        grid_spec=pltpu.PrefetchScalarGridSpec(
            num_scalar_prefetch=1, grid=(NDEV, ni),
            in_specs=[pl.BlockSpec((tm, d), lambda s, i, me_ref: (i, 0)), ANY, ANY],
            out_specs=(pl.BlockSpec((tm, bw), lambda s, i, me_ref: (i, me_ref[0] ^ _gather_order(s))), ANY, ANY),
            scratch_shapes=[pltpu.VMEM((2, d, bw), BF16), pltpu.SemaphoreType.DMA((7,)),
                            pltpu.SemaphoreType.DMA((7,)), pltpu.SemaphoreType.DMA((2,)),
                            pltpu.SemaphoreType.DMA((NDEV,)), pltpu.SemaphoreType.DMA((NDEV,))]),
        out_shape=(jax.ShapeDtypeStruct((la, NDEV * bw), BF16), jax.ShapeDtypeStruct((NDEV, d, bw), BF16),
                   jax.ShapeDtypeStruct((NDEV,) + wo_shard.shape, wo_shard.dtype)),
        compiler_params=_params(("arbitrary", "arbitrary")),
    )(where, h, w_shard, wo_shard)


def _conv_input(p, wcb, taps_m, l, t):
    la = p.shape[0]
    w = wcb.shape[1]
    nt = l // t

    def body(v_ref, wcb_ref, tm_ref, xb_ref):
        taps = _dot(tm_ref[...].reshape(4 * t, t), v_ref[...])
        xb = wcb_ref[4:5, :] + wcb_ref[0:1, :] * taps[0:t]
        for j in range(1, 4):
            xb = xb + wcb_ref[j:j + 1, :] * taps[j * t:(j + 1) * t]
        xb_ref[...] = xb

    return _call(
        body, name="conv_input",
        grid=(nt + 1,),
        out_shape=jax.ShapeDtypeStruct((la, w), F32),
        in_specs=[pl.BlockSpec((t, w), lambda i: (i, 4)), pl.BlockSpec((8, w), lambda i: (0, 0)),
                  pl.BlockSpec((None, 4, t, t), lambda i: (i // nt, 0, 0, 0))],
        out_specs=pl.BlockSpec((t, w), lambda i: (i, 0)),
        compiler_params=_params(("arbitrary",)),
    )(p, wcb, taps_m)


def _lru_forward(xb, wg, lv, l, t):
    la, w = xb.shape
    gc = wg.shape[2]
    nt = l // t

    def body(xf_ref, xr_ref, wg_ref, lv_ref, hf_ref, hr_ref, a_s, b_s, carry):
        @pl.when(pl.program_id(0) == 0)
        def _():
            carry[...] = jnp.zeros_like(carry)

        for dr, (x_ref, h_ref) in enumerate(((xf_ref, hf_ref), (xr_ref, hr_ref))):
            x = x_ref[...]
            a, s, _, _, ti, _ = _lru_coef(x, wg_ref, dr, lv_ref[3 * dr:3 * dr + 1, :],
                                          lv_ref[3 * dr + 1:3 * dr + 2, :], lv_ref[3 * dr + 2:3 * dr + 3, :], gc)
            a_s[...] = a
            b_s[...] = (s * x) * (0.5 * ti + 0.5)
            carry[dr] = _scan_tile(a_s, b_s, h_ref, carry[dr], dr == 1)

    full = lambda shape: pl.BlockSpec(shape, lambda i: (0,) * len(shape))
    fmap = lambda i: (jnp.where(i == 0, nt, i - 1), 0)
    rmap = lambda i: (jnp.where(i == 0, nt, nt - i), 0)
    return _call(
        body, name="lru_forward",
        grid=(nt + 1,),
        out_shape=(jax.ShapeDtypeStruct((la, w), F32), jax.ShapeDtypeStruct((la, w), F32)),
        in_specs=[pl.BlockSpec((t, w), fmap), pl.BlockSpec((t, w), rmap), full(wg.shape), full(lv.shape)],
        out_specs=(pl.BlockSpec((t, w), fmap), pl.BlockSpec((t, w), rmap)),
        scratch_shapes=[pltpu.VMEM((t, w), F32), pltpu.VMEM((t, w), F32), pltpu.VMEM((2, 8, w), F32)],
        compiler_params=_params(("arbitrary",)),
    )(xb, xb, wg, lv)


def _mix_gates(p_refs, hf_ref, hr_ref, wca_ref, perm_ref, t, w):
    bl, cl, ul, gl, ql = [r[...].astype(F32) for r in p_refs]
    tt = cl * ul
    tt16 = tt.astype(BF16)
    beside = _dot(perm_ref[2:4].reshape(2 * t, t), tt16)
    before, after = beside[:t], beside[t:]
    z = wca_ref[0:1, :] * before + wca_ref[1:2, :] * tt + wca_ref[2:3, :] * after
    sig_g = _sigmoid(gl)
    sig_q = _sigmoid(ql)
    ylru = _dot(perm_ref[1], (hf_ref[...] + hr_ref[...]).astype(BF16))
    return bl, cl, ul, gl, ql, (before, tt, after), z, sig_g, sig_q, ylru


def _p_specs(t, w, nt):
    return [pl.BlockSpec((t, w), functools.partial(lambda i, s: (jnp.minimum(i, nt - 1), s), s=s))
            for s in (0, 1, 2, 3, 5)]


def _mix_forward(x, tgt, p, hf, hr, wo, ov, wca, perm, t):
    l, d = x.shape
    w = d // 2
    nt = l // t

    def body(x_ref, tg_ref, b_ref, c_ref, u_ref, g_ref, q_ref, hf_ref, hr_ref, wo_ref, ov_ref, wca_ref, perm_ref,
             dn_ref, ct_ref, do_ref, part_ref):
        i = pl.program_id(0)
        bl, _, _, gl, ql, _, z, sig_g, sig_q, ylru = _mix_gates(
            (b_ref, c_ref, u_ref, g_ref, q_ref), hf_ref, hr_ref, wca_ref, perm_ref, t, w)
        ya = bl * z * (gl * sig_g)
        yb = ylru * (ql * sig_q)
        ct_ref[:, 0:w] = ya.astype(BF16)
        ct_ref[:, w:] = yb.astype(BF16)
        out = _dot(ya.astype(BF16), wo_ref[0:w, :]) + _dot(yb.astype(BF16), wo_ref[w:, :])
        gate, fg = ov_ref[0:1, :], ov_ref[1:2, :]
        n = x_ref[...] + gate * out
        rr = lax.rsqrt(jnp.mean(n * n, axis=-1, keepdims=True) + EPS)
        nh = n * rr
        e = nh * fg - tg_ref[...]
        loss = 0.5 * jnp.sum(jnp.mean(e * e, axis=-1, keepdims=True), axis=0, keepdims=True)
        dy = e * (1.0 / d)
        dnh = dy * fg
        dn = rr * (dnh - nh * jnp.mean(dnh * nh, axis=-1, keepdims=True))
        dn_ref[...] = dn.astype(BF16)
        do_ref[...] = (dn * gate).astype(BF16)

        @pl.when(i == 0)
        def _():
            part_ref[...] = jnp.zeros_like(part_ref)

        part_ref[0:1, :] += jnp.sum(dy * nh, axis=0, keepdims=True)
        part_ref[1:2, :] += jnp.sum(dn * out, axis=0, keepdims=True)
        part_ref[2:3, :] += jnp.broadcast_to(loss, (1, d))

    tile = lambda cols: pl.BlockSpec((t, cols), lambda i: (i, 0))
    full = lambda shape: pl.BlockSpec(shape, lambda i: (0,) * len(shape))
    return _call(
        body, name="mix_forward",
        grid=(nt,),
        out_shape=(jax.ShapeDtypeStruct((l, d), BF16), jax.ShapeDtypeStruct((l, d), BF16),
                   jax.ShapeDtypeStruct((l, d), BF16), jax.ShapeDtypeStruct((8, d), F32)),
        in_specs=[tile(d), tile(d)] + _p_specs(t, w, nt) + [tile(w), tile(w),
                  pl.BlockSpec((d, d), lambda i: (0, 0), pipeline_mode=pl.Buffered(1)),
                  full(ov.shape), full(wca.shape), full(perm.shape)],
        out_specs=(tile(d), tile(d), tile(d), full((8, d))),
        compiler_params=_params(("arbitrary",)),
    )(x, tgt, p, p, p, p, p, hf, hr, wo, ov, wca, perm)


def _mix_backward(dout, p, hf, hr, wo, wca, perm, g_wout, l, t):
    d = dout.shape[1]
    w = d // 2
    nt = l // t
    la = p.shape[0]

    def body(do_ref, b_ref, c_ref, u_ref, g_ref, q_ref, hf_ref, hr_ref, wo_ref, wca_ref, perm_ref, gw_ref,
             dp_ref, dh_ref, part_ref, sc_ref, send_sems, recv_sems):
        i = pl.program_id(0)
        copies = _scatter_copies(gw_ref, sc_ref, send_sems, recv_sems)

        @pl.when(i == 0)
        def _():
            part_ref[...] = jnp.zeros_like(part_ref)
            for cp in copies:
                cp.start()

        @pl.when(i == nt)
        def _():
            dp_ref[...] = jnp.zeros_like(dp_ref)
            _exchange_wait(copies, copies[1:])

        @pl.when(i < nt)
        def _():
            bl, cl, ul, gl, ql, taps, z, sig_g, sig_q, ylru = _mix_gates(
                (b_ref, c_ref, u_ref, g_ref, q_ref), hf_ref, hr_ref, wca_ref, perm_ref, t, w)
            do = do_ref[...]
            dya = _dot_nt(do, wo_ref[0:w, :])
            dyb = _dot_nt(do, wo_ref[w:, :])
            sg = gl * sig_g
            dz = dya * bl * sg
            dz16 = dz.astype(BF16)
            beside = _dot(perm_ref[2:4].reshape(2 * t, t), dz16)
            dt = wca_ref[0:1, :] * beside[t:] + wca_ref[1:2, :] * dz + wca_ref[2:3, :] * beside[:t]
            dp_ref[:, 0:w] = (dya * z * sg).astype(BF16)
            dp_ref[:, w:2 * w] = (dt * ul).astype(BF16)
            dp_ref[:, 2 * w:3 * w] = (dt * cl).astype(BF16)
            dp_ref[:, 3 * w:4 * w] = (dya * bl * z * (sig_g * (1.0 + gl * (1.0 - sig_g)))).astype(BF16)
            dp_ref[:, 4 * w:5 * w] = jnp.zeros((t, w), BF16)
            dp_ref[:, 5 * w:6 * w] = (dyb * ylru * (sig_q * (1.0 + ql * (1.0 - sig_q)))).astype(BF16)
            dh_ref[...] = _dot(perm_ref[0], (dyb * (ql * sig_q)).astype(BF16)).astype(BF16)
            for j in range(3):
                part_ref[j:j + 1, :] += jnp.sum(dz * taps[j], axis=0, keepdims=True)

    clamp = lambda cols: pl.BlockSpec((t, cols), lambda i: (jnp.minimum(i, nt - 1), 0))
    full = lambda shape: pl.BlockSpec(shape, lambda i: (0,) * len(shape))
    return _call(
        body, name="mix_backward",
        grid=(nt + 1,),
        out_shape=(jax.ShapeDtypeStruct((la, 6 * w), BF16), jax.ShapeDtypeStruct((l, w), BF16),
                   jax.ShapeDtypeStruct((8, w), F32), jax.ShapeDtypeStruct(g_wout.shape, g_wout.dtype)),
        in_specs=[clamp(d)] + _p_specs(t, w, nt) + [clamp(w), clamp(w),
                  pl.BlockSpec((d, d), lambda i: (0, 0), pipeline_mode=pl.Buffered(1)), full(wca.shape),
                  full(perm.shape), ANY],
        out_specs=(pl.BlockSpec((t, 6 * w), lambda i: (i, 0)), clamp(w), full((8, w)), ANY),
        scratch_shapes=[pltpu.SemaphoreType.DMA((NDEV,)), pltpu.SemaphoreType.DMA((NDEV,))],
        compiler_params=_params(("arbitrary",)),
    )(dout, p, p, p, p, p, hf, hr, wo, wca, perm, g_wout)


def _lru_backward(direction, xb, dhs, hs, wg, lv, l, t, conv=None):
    la, w = hs.shape
    gc = wg.shape[2]
    ng = w // gc
    nt = l // t
    nblk8 = la // 8
    last = conv is not None
    assert last == (direction == 1)

    if direction == 0:
        tile = lambda i: jnp.where(i == nt, nt, nt - 1 - i)
        halo = lambda i: jnp.where(tile(i) == 0, nblk8 - 1, tile(i) * (t // 8) - 1)
    else:
        tile = lambda i: i
        halo = lambda i: jnp.minimum((i + 1) * (t // 8), nblk8 - 1)

    def body(*refs):
        x_ref, dh_ref, hs_ref, halo_ref, wg_ref, lv_ref = refs[:6]
        if last:
            v_ref, wcb_ref, bm_ref, dxo_ref = refs[6:10]
        out_ref, dwg_ref, part_ref, a_s, dh_s, g_s, carry = refs[-7:]
        i = pl.program_id(0)
        is_ctx = i == nt

        @pl.when(i == 0)
        def _():
            carry[...] = jnp.zeros_like(carry)
            dwg_ref[...] = jnp.zeros_like(dwg_ref)
            part_ref[...] = jnp.zeros_like(part_ref)

        xb = x_ref[...]
        lam = lv_ref[3 * direction + 2:3 * direction + 3, :]
        a, s, rs, tr, ti, sp = _lru_coef(xb, wg_ref, direction, lv_ref[3 * direction:3 * direction + 1, :],
                                         lv_ref[3 * direction + 1:3 * direction + 2, :], lam, gc)
        hs_t = hs_ref[...]
        r8 = _rows((8, w))
        if direction == 0:
            edge = jnp.where(is_ctx, 0.0, halo_ref[7:8, :])
            first = jnp.where(r8 == 0, edge, pltpu.roll(hs_t[t - 8:, :], 1, 0))
            hprev = jnp.concatenate([first, hs_t[:t - 8, :]], axis=0)
        else:
            edge = jnp.where(is_ctx, 0.0, halo_ref[0:1, :])
            final = jnp.where(r8 == 7, edge, pltpu.roll(hs_t[:8, :], 7, 0))
            hprev = jnp.concatenate([hs_t[8:, :], final], axis=0)
        a_s[...] = a
        dh_s[...] = jnp.where(is_ctx, 0.0, dh_ref[...].astype(F32))
        carry[...] = _scan_tile_backward(a_s, dh_s, g_s, carry[...], direction == 0)

        g = g_s[...]
        r = 0.5 * tr + 0.5
        ig = 0.5 * ti + 0.5
        ix = ig * xb
        gs = g * s
        dla = (g * a) * (hprev - ix * (a * rs))
        dxb = gs * ig
        dzr = dla * (r * (1.0 - tr)) * (-LRU_C * sp)
        dzi = gs * ix * (1.0 - ti)
        part_ref[0:1, :] += jnp.sum(dzr, axis=0, keepdims=True)
        part_ref[1:2, :] += jnp.sum(dzi, axis=0, keepdims=True)
        part_ref[2:3, :] += jnp.sum(dla * r, axis=0, keepdims=True) * (LRU_C * _sigmoid(-lam))
        pieces = []
        for gi in range(ng):
            sl = slice(gi * gc, (gi + 1) * gc)
            dz = jnp.concatenate([dzr[:, sl], dzi[:, sl]], axis=-1).astype(BF16)
            pieces.append(_dot_nt(dz, wg_ref[direction, gi]))
            dwg_ref[gi] += _dot(xb[:, sl].T.astype(BF16), dz)
        dxb = dxb + (pieces[0] if ng == 1 else jnp.concatenate(pieces, axis=-1))
        if not last:
            out_ref[...] = dxb
        else:
            dxb = dxb + dxo_ref[...]
            v = v_ref[...].astype(F32)
            backs = _dot(bm_ref[...].reshape(4 * t, t), dxb.astype(BF16))
            dv = jnp.zeros((t, w), F32)
            for j in range(4):
                back = backs[j * t:(j + 1) * t]
                dv = dv + wcb_ref[j:j + 1, :] * back
                part_ref[4 + j:5 + j, :] += jnp.sum(back * v, axis=0, keepdims=True)
            out_ref[...] = dv.astype(BF16)
            part_ref[3:4, :] += jnp.sum(dxb, axis=0, keepdims=True)

    full = lambda shape: pl.BlockSpec(shape, lambda i: (0,) * len(shape))
    kind = lambda i: (jnp.where(i == nt, 1, 0), 0, 0, 0)
    in_specs = [pl.BlockSpec((t, w), lambda i: (tile(i), 0)),
                pl.BlockSpec((t, w), lambda i: (jnp.minimum(tile(i), nt - 1), 0)),
                pl.BlockSpec((t, w), lambda i: (tile(i), 0)),
                pl.BlockSpec((8, w), lambda i: (halo(i), 0)),
                full(wg.shape), full(lv.shape)]
    args = [xb, dhs, hs, hs, wg, lv]
    if last:
        p, wcb, back_m, dxb_other, dp = conv
        in_specs += [pl.BlockSpec((t, w), lambda i: (tile(i), 4)), full(wcb.shape),
                     pl.BlockSpec((None, 4, t, t), kind), pl.BlockSpec((t, w), lambda i: (tile(i), 0)), ANY]
        args += [p, wcb, back_m, dxb_other, dp]
        out0 = jax.ShapeDtypeStruct(dp.shape, dp.dtype)
        spec0 = pl.BlockSpec((t, w), lambda i: (tile(i), 4))
        aliases = {10: 0}
    else:
        out0 = jax.ShapeDtypeStruct((la, w), F32)
        spec0 = pl.BlockSpec((t, w), lambda i: (tile(i), 0))
        aliases = {}
    return _call(
        body, name="lru_backward_%d" % direction,
        grid=(nt + 1,),
        out_shape=(out0, jax.ShapeDtypeStruct((ng, gc, 2 * gc), F32), jax.ShapeDtypeStruct((8, w), F32)),
        in_specs=in_specs,
        out_specs=(spec0, full((ng, gc, 2 * gc)), full((8, w))),
        scratch_shapes=[pltpu.VMEM((t, w), F32), pltpu.VMEM((t, w), F32), pltpu.VMEM((t, w), F32),
                        pltpu.VMEM((8, w), F32)],
        input_output_aliases=aliases,
        compiler_params=_params(("arbitrary",)),
    )(*args)


def _weight_grad_t(a, b, nblk_m, nblk_n, tk, name):
    k, m = a.shape
    n = b.shape[1]
    bm, bn = m // nblk_m, n // nblk_n
    nk = k // tk

    def body(a_ref, b_ref, o_ref, acc):
        kk = pl.program_id(2)

        @pl.when(kk == 0)
        def _():
            acc[...] = jnp.zeros_like(acc)

        acc[...] += lax.dot_general(a_ref[...], b_ref[...], (((0,), (0,)), ((), ())), preferred_element_type=F32)

        @pl.when(kk == nk - 1)
        def _():
            o_ref[...] = acc[...].astype(BF16)

    return _call(
        body, name=name,
        grid=(nblk_m, nblk_n, nk),
        out_shape=jax.ShapeDtypeStruct((nblk_m * nblk_n, bm, bn), BF16),
        in_specs=[pl.BlockSpec((tk, bm), lambda i, j, kk: (kk, i)),
                  pl.BlockSpec((tk, bn), lambda i, j, kk: (kk, j))],
        out_specs=pl.BlockSpec((None, bm, bn), lambda i, j, kk: (i * nblk_n + j, 0, 0)),
        scratch_shapes=[pltpu.VMEM((bm, bn), F32)],
        compiler_params=_params(("arbitrary", "arbitrary", "arbitrary")),
    )(a, b)


def _weight_grad_scatter(at, b, tk, name):
    k, m = at.shape
    n = b.shape[1]
    bn = n // NDEV
    nk = k // tk
    where = jnp.stack([_idx(_my_pos()), lax.axis_index("c")]).astype(jnp.int32)
    tn = (((0,), (0,)), ((), ()))

    def body(w_ref, a_ref, b_ref, recv_ref, acc, sbuf, sib, sib_send, sib_recv, chip_send, chip_recv, keep_sem):
        s, kk = pl.program_id(0), pl.program_id(1)
        x, y, c = _my_pos()

        @pl.when(kk == 0)
        def _():
            acc[...] = lax.dot_general(a_ref[...], b_ref[...], tn, preferred_element_type=F32)

        @pl.when(kk > 0)
        def _():
            acc[...] += lax.dot_general(a_ref[...], b_ref[...], tn, preferred_element_type=F32)

        def to_sibling(j):
            return pltpu.make_async_remote_copy(
                src_ref=sbuf.at[0], dst_ref=sib.at[j], send_sem=sib_send.at[j], recv_sem=sib_recv.at[j],
                device_id=(x, y, 1 - c), device_id_type=MESH)

        def to_chip(j):
            dist = _chip_order(j, c)
            return pltpu.make_async_remote_copy(
                src_ref=sbuf.at[1], dst_ref=recv_ref.at[dist // 2], send_sem=chip_send.at[j],
                recv_sem=chip_recv.at[dist // 2], device_id=_peer_at(dist), device_id_type=MESH)

        keep = pltpu.make_async_copy(sbuf.at[1], recv_ref.at[0], keep_sem)
        sends = []
        for j in range(4):
            sends += [to_sibling(j), to_chip(j) if j < 3 else keep]

        for st in range(NDEV):
            @pl.when((kk == nk - 1) & (s == st))
            def _(st=st):
                if st >= 2:
                    sends[st - 2].wait_send()
                part = acc[...]
                if st % 2 == 1:
                    to_sibling(st // 2).wait_recv()
                    part = part + sib[st // 2].astype(F32)
                sbuf[st % 2] = part.astype(BF16)
                sends[st].start()
                if st == NDEV - 1:
                    sends[st - 1].wait_send()
                    sends[st].wait()
                    for j in range(1, 4):
                        pltpu.make_async_remote_copy(
                            src_ref=sbuf.at[0], dst_ref=recv_ref.at[j], send_sem=chip_send.at[0],
                            recv_sem=chip_recv.at[j], device_id=_peer_at(2 * j), device_id_type=MESH).wait_recv()

    blk = lambda s, w_ref: w_ref[0] ^ _scatter_order(s, w_ref[1])
    return _call(
        body, name=name,
        grid_spec=pltpu.PrefetchScalarGridSpec(
            num_scalar_prefetch=1, grid=(NDEV, nk),
            in_specs=[pl.BlockSpec((tk, m), lambda s, kk, w_ref: (kk, 0)),
                      pl.BlockSpec((tk, bn), lambda s, kk, w_ref: (kk, blk(s, w_ref)))],
            out_specs=ANY,
            scratch_shapes=[pltpu.VMEM((m, bn), F32), pltpu.VMEM((2, m, bn), BF16), pltpu.VMEM((4, m, bn), BF16),
                            pltpu.SemaphoreType.DMA((4,)), pltpu.SemaphoreType.DMA((4,)),
                            pltpu.SemaphoreType.DMA((4,)), pltpu.SemaphoreType.DMA((4,)),
                            pltpu.SemaphoreType.DMA]),
        out_shape=jax.ShapeDtypeStruct((4, m, bn), BF16),
        compiler_params=_params(("arbitrary", "arbitrary")),
    )(where, at, b)


def _input_backward(dp, w_all, src, mv, row0, tm, nbk, name, dn=None):
    rows, d = src.shape
    nb, _, bw = w_all.shape
    nk = nb // nbk
    ni = rows // tm
    blk0 = row0 // tm
    latent = dn is not None

    def body(*refs):
        dp_ref, w_ref, x_ref, mv_ref = refs[:4]
        outs = refs[4 + latent:]
        part_ref, acc = outs[latent], outs[latent + 1]
        i, k = pl.program_id(0), pl.program_id(1)

        def product():
            step = _dot_nt(dp_ref[:, 0:bw], w_ref[0])
            for q in range(1, nbk):
                step = step + _dot_nt(dp_ref[:, q * bw:(q + 1) * bw], w_ref[q])
            return step

        def finish(slot):
            xf = x_ref[...]
            r = lax.rsqrt(jnp.mean(xf * xf, axis=-1, keepdims=True) + EPS)
            xn = xf * r
            dhl = acc[slot]
            gain, sc = mv_ref[0:1, :], mv_ref[1:2, :]
            dhx = jnp.sum(dhl * xn, axis=0, keepdims=True)
            part_ref[0:1, :] += jnp.sum(dhl, axis=0, keepdims=True)
            part_ref[1:2, :] += dhx * gain
            part_ref[2:3, :] += dhx * (1.0 + sc)
            if latent:
                dxn = dhl * (gain * (1.0 + sc))
                outs[0][...] = (refs[4][...].astype(F32)
                                + r * (dxn - xn * jnp.mean(dxn * xn, axis=-1, keepdims=True)))

        @pl.when((i == 0) & (k == 0))
        def _():
            part_ref[...] = jnp.zeros_like(part_ref)
            acc[0] = product()

        @pl.when((i > 0) & (i < ni) & (k == 0))
        def _():
            acc[i % 2] = product()
            finish((i - 1) % 2)

        @pl.when((i == ni) & (k == 0))
        def _():
            finish((ni - 1) % 2)

        @pl.when((i < ni) & (k > 0))
        def _():
            acc[i % 2] += product()

    tile = pl.BlockSpec((tm, d), lambda i, k: (jnp.maximum(i - 1, 0), 0))
    vec = pl.BlockSpec((8, d), lambda i, k: (0, 0))
    kblock = lambda i, k: jnp.where(i == ni, nk - 1, k)
    return _call(
        body, name=name,
        grid=(ni + 1, nk),
        out_shape=((jax.ShapeDtypeStruct((rows, d), F32),) if latent else ()) + (jax.ShapeDtypeStruct((8, d), F32),),
        in_specs=[pl.BlockSpec((tm, nbk * bw), lambda i, k: (blk0 + jnp.minimum(i, ni - 1), kblock(i, k))),
                  pl.BlockSpec((nbk, d, bw), lambda i, k: (kblock(i, k), 0, 0)), tile, vec]
                 + ([tile] if latent else []),
        out_specs=((tile,) if latent else ()) + (vec,),
        scratch_shapes=[pltpu.VMEM((2, tm, d), F32)],
        compiler_params=_params(("arbitrary", "arbitrary")),
    )(*([dp, w_all, src, mv] + ([dn] if latent else [])))


def _adamw_scattered(parts, w, m, v, tr):
    r, c = w.shape
    nslot = parts.shape[0]

    def body(p_ref, w_ref, m_ref, v_ref, g_ref, d_ref, m2_ref, v2_ref):
        g = p_ref[0].astype(F32)
        for k in range(1, nslot):
            g = g + p_ref[k].astype(F32)
        g_ref[...] = g
        d_ref[...], m2_ref[...], v2_ref[...] = _adamw(w_ref[...], g, m_ref[...], v_ref[...])

    tile = pl.BlockSpec((tr, c), lambda i: (i, 0))
    return _call(
        body, name="adamw_scattered_%dx%d" % (r, c),
        grid=(r // tr,),
        out_shape=tuple(jax.ShapeDtypeStruct((r, c), F32) for _ in range(4)),
        in_specs=[pl.BlockSpec((nslot, tr, c), lambda i: (0, i, 0)), tile, tile, tile],
        out_specs=(tile,) * 4,
        compiler_params=_params(("arbitrary",)),
    )(parts, w, m, v)


def _adamw_ada(st, dmod, w, m, v, tr):
    r, c = w.shape

    def body(s_ref, dm_ref, w_ref, m_ref, v_ref, g_ref, d_ref, m2_ref, v2_ref):
        g = jnp.dot(s_ref[...], dm_ref[...], precision=HIGHEST, preferred_element_type=F32)
        g_ref[...] = g
        d_ref[...], m2_ref[...], v2_ref[...] = _adamw(w_ref[...], g, m_ref[...], v_ref[...])

    tile = pl.BlockSpec((tr, c), lambda i: (i, 0))
    return _call(
        body, name="adamw_ada",
        grid=(r // tr,),
        out_shape=tuple(jax.ShapeDtypeStruct((r, c), F32) for _ in range(4)),
        in_specs=[pl.BlockSpec((tr, 16), lambda i: (i, 0)), pl.BlockSpec((16, c), lambda i: (0, 0)),
                  tile, tile, tile],
        out_specs=(tile,) * 4,
        compiler_params=_params(("arbitrary",)),
    )(st, dmod, w, m, v)


def _adamw_small(gs, ws, ms, vs):
    n = len(ws)

    def body(*refs):
        for j in range(n):
            g_ref, w_ref, m_ref, v_ref = refs[j], refs[n + j], refs[2 * n + j], refs[3 * n + j]
            d_ref, m2_ref, v2_ref = refs[4 * n + j], refs[5 * n + j], refs[6 * n + j]
            d_ref[...], m2_ref[...], v2_ref[...] = _adamw(w_ref[...], g_ref[...], m_ref[...], v_ref[...])

    shapes = tuple(jax.ShapeDtypeStruct(a.shape, F32) for a in ws)
    out = _call(
        body, name="adamw_small",
        out_shape=shapes * 3,
        in_specs=[VMEM] * (4 * n), out_specs=(VMEM,) * (3 * n),
        compiler_params=_params(),
    )(*gs, *ws, *ms, *vs)
    return list(out[:n]), list(out[n:2 * n]), list(out[2 * n:])


def _blockdiag_groups(wh, gc):
    h, dh, _ = wh.shape
    g = gc // dh
    w4 = wh.reshape(h // g, g, dh, dh)
    bd = jnp.einsum("ngij,gh->ngihj", w4, jnp.eye(g, dtype=wh.dtype))
    return bd.reshape(h // g, gc, gc)


def _blockdiag_extract(bd, dh):
    ng, gc, _ = bd.shape
    g = gc // dh
    x = bd.reshape(ng, g, dh, g, dh)
    return jnp.einsum("ngihj,gh->ngij", x, jnp.eye(g, dtype=bd.dtype)).reshape(ng * g, dh, dh)


def _largest_tile(n, cap):
    return max(q for q in range(128, min(n, cap) + 1, 128) if n % q == 0)


def _rows8(*vecs):
    rows = [jnp.reshape(v, (1, -1)).astype(F32) for v in vecs]
    n = rows[0].shape[1]
    return jnp.concatenate(rows + [jnp.zeros((8 - len(rows), n), F32)], axis=0)


def _pack(pieces):
    flat = jnp.concatenate([jnp.reshape(a, (-1,)).astype(F32) for a in pieces])
    total = -(-flat.shape[0] // 1024) * 1024
    return jnp.pad(flat, (0, total - flat.shape[0])).reshape(total // 128, 128)


def _unpack(packed, shapes):
    flat = packed.reshape(-1)
    out, off = [], 0
    for s in shapes:
        n = 1
        for q in s:
            n *= q
        out.append(flat[off:off + n].reshape(s))
        off += n
    return out


def kernel(x, c, ctx, c_ctx, norm_g, w_ada, b_ada, w_in, w_conv_a, w_conv_b, b_conv_b, lru_wa, lru_ba, lru_wx, lru_bx, lru_lambda, w_out, final_g, loss_target, m_c_ctx, m_norm_g, m_w_ada, m_b_ada, m_w_in, m_w_conv_a, m_w_conv_b, m_b_conv_b, m_lru_wa, m_lru_ba, m_lru_wx, m_lru_bx, m_lru_lambda, m_w_out, m_final_g, v_c_ctx, v_norm_g, v_w_ada, v_b_ada, v_w_in, v_w_conv_a, v_w_conv_b, v_b_conv_b, v_lru_wa, v_lru_ba, v_lru_wx, v_lru_bx, v_lru_lambda, v_w_out, v_final_g):
    _, l, d = x.shape
    lc = ctx.shape[1]
    w = d // 2
    t = lc
    assert l % t == 0 and t % GRID_W == 0 and t % 128 == 0
    dh = w // N_HEADS
    gc = min(w, MXU_WIDTH)
    cols = w_ada.shape[2]
    wo_rows = w_out.shape[1]
    me = _idx(_my_pos())
    x2, ctx2, tgt2 = x[0], ctx[0], loss_target[0]
    w_ada2, w_in2, w_out2 = w_ada[0], w_in[0], w_out[0]

    small_mine = jnp.concatenate([a.reshape(-1) for a in (w_conv_a, w_conv_b, lru_ba, lru_bx, lru_lambda)]
                                 + [jnp.zeros((3 * (w // NDEV),), F32)]).reshape(16, w // NDEV)
    mod_all, s_mat, small_all = _mod_forward(
        jnp.broadcast_to(c, (8, d)), jnp.broadcast_to(c_ctx[None], (8, d)), w_ada2, small_mine)
    mod = jnp.transpose(mod_all, (1, 0, 2)).reshape(16, NDEV * cols) + b_ada
    mod_lat = lax.dynamic_slice_in_dim(mod, me, 1, axis=0)
    sh_l, sc_l, gt_l = jnp.split(mod_lat, 3, axis=-1)
    sh_c, sc_c, _ = jnp.split(mod[8:9], 3, axis=-1)
    small = jnp.transpose(small_all, (1, 0, 2)).reshape(16, w)
    wca = _rows8(*[small[j] for j in range(0, 3)])
    wcb = _rows8(*[small[j] for j in range(3, 7)], b_conv_b)
    lv = _rows8(0.5 * small[7], 0.5 * small[9], small[11], 0.5 * small[8], 0.5 * small[10], small[12])
    wg = jnp.stack([
        jnp.concatenate([_blockdiag_groups(lru_wa[0, dr], gc), _blockdiag_groups(lru_wx[0, dr], gc)], axis=-1)
        for dr in range(2)])
    wg = (0.5 * wg).astype(BF16)

    la = l + lc
    tm = 2 * t if l % (2 * t) == 0 else t
    tk = 3 * t if la % (3 * t) == 0 else t
    h = _normalize(x2, _rows8(norm_g, sc_l, sh_l), la, 0, tm, "normalize")
    h = _normalize(ctx2, _rows8(norm_g, sc_c, sh_c), la, l, t, "normalize_ctx", prev=h)
    p, w_all, wo_all = _in_projection(h, w_in2.astype(BF16), w_out2.astype(BF16), la // 4 if la % 64 == 0 else tk)
    taps_m, back_m, perm = _scan_matrices(t)
    xb = _conv_input(p, wcb, taps_m, l, t)
    hf, hr = _lru_forward(xb, wg, lv, l, t)
    wo = wo_all.reshape(d, d)
    dn, cat, dout, part_mix = _mix_forward(x2, tgt2, p, hf, hr, wo, _rows8(gt_l, final_g), wca, perm, t)
    g_wout = _weight_grad_t(cat, dout, 2, 1, _largest_tile(l, 2048), "grad_w_out")
    dp, dhs, part_ca, sc_wout = _mix_backward(dout, p, hf, hr, wo, wca, perm, g_wout.reshape(NDEV, wo_rows, d), l, t)
    dxb0, dwg0, part_l0 = _lru_backward(0, xb, dhs, hf, wg, lv, l, t)
    dp, dwg1, part_l1 = _lru_backward(1, xb, dhs, hr, wg, lv, l, t, conv=(p, wcb, back_m, dxb0, dp))
    sc_win = _weight_grad_scatter(h, dp, tk, "grad_w_in")
    grad_x, part_lat = _input_backward(dp, w_all, x2, _rows8(norm_g, sc_l), 0, tm, 2, "input_backward", dn=dn)
    (part_ctx,) = _input_backward(dp, w_all, ctx2, _rows8(norm_g, sc_c), l, t, 2, "input_backward_ctx")
    part_in = jnp.concatenate([part_lat[0:2], part_ctx[0:2], (part_lat[2] + part_ctx[2])[None]], axis=0)

    dwa = jnp.stack([_blockdiag_extract(dwg0[:, :, :gc], dh), _blockdiag_extract(dwg1[:, :, :gc], dh)])
    dwx = jnp.stack([_blockdiag_extract(dwg0[:, :, gc:], dh), _blockdiag_extract(dwg1[:, :, gc:], dh)])
    lru_part = (0.5 * jnp.stack([dwa, dwx])).reshape(NDEV, -1, 128)
    zeros_d = jnp.zeros((d,), F32)
    pieces = [
        jnp.concatenate([part_in[0], part_in[1], part_mix[1]]),
        jnp.concatenate([part_in[2], part_in[3], zeros_d]),
        part_in[4], part_mix[0], part_ca[0:3], part_l1[4:8], part_l1[3],
        0.5 * jnp.stack([part_l0[0], part_l1[0]]), 0.5 * jnp.stack([part_l0[1], part_l1[1]]),
        jnp.stack([part_l0[2], part_l1[2]]), part_mix[2, 0:1],
    ]
    shapes = [(3 * d,), (3 * d,), (d,), (d,), (3, w), (4, w), (w,), (2, w), (2, w), (2, w), (1,)]
    sig_cc = jax.nn.sigmoid(c_ctx)
    dsilu_cc = jnp.broadcast_to((sig_cc * (1.0 + c_ctx * (1.0 - sig_cc)))[None], (8, d))
    psum, pall, lru_sum, g_cctx8 = _reduce_small(_pack(pieces), lru_part, w_ada2, dsilu_cc)
    (g_modl, g_modc, g_norm, g_final, g_ca, g_cb, g_bcb, g_ba, g_bx, g_lam, loss1) = _unpack(psum, shapes)
    loss = loss1[0]
    g_cctx = g_cctx8[0]
    g_bada = (g_modl + g_modc)[None]
    g_lru = lru_sum.reshape(2, 2, N_HEADS, dh, dh)
    g_wa, g_wx = g_lru[0][None], g_lru[1][None]
    wsl = w // NDEV
    mine = lambda a: lax.dynamic_slice_in_dim(a, me * wsl, wsl, axis=-1)
    g_ca_m, g_cb_m, g_ba_m, g_bx_m, g_lam_m = (mine(g_ca)[None], mine(g_cb)[None], mine(g_ba)[None],
                                               mine(g_bx)[None], mine(g_lam)[None])
    g_norm, g_bcb = g_norm[None], g_bcb[None]

    per_dev = pall[:, :3 * d // 128].reshape(NDEV, NDEV, cols)
    dmod_lat = lax.dynamic_slice_in_dim(per_dev, me, 1, axis=1)[:, 0]
    dmod_ctx = lax.dynamic_slice_in_dim(g_modc.reshape(NDEV, cols), me, 1, axis=0)
    dmod16 = jnp.concatenate([dmod_lat, dmod_ctx, jnp.zeros((7, cols), F32)], axis=0)
    tr_ada = 256 if d % 256 == 0 else d
    g_wada, d_wada, m_wada, v_wada = _adamw_ada(s_mat.T, dmod16, w_ada2, m_w_ada[0], v_w_ada[0], tr_ada)
    g_win2, d_win, m_win, v_win = _adamw_scattered(sc_win, w_in2, m_w_in[0], v_w_in[0], tr_ada)
    tr_out = 64 if wo_rows % 64 == 0 else wo_rows
    g_wout2, d_wout, m_wout, v_wout = _adamw_scattered(sc_wout, w_out2, m_w_out[0], v_w_out[0], tr_out)

    small_w = [c_ctx, norm_g, b_ada, w_conv_a, w_conv_b, b_conv_b, lru_wa, lru_ba, lru_wx, lru_bx, lru_lambda, final_g]
    small_m = [m_c_ctx, m_norm_g, m_b_ada, m_w_conv_a, m_w_conv_b, m_b_conv_b, m_lru_wa, m_lru_ba, m_lru_wx,
               m_lru_bx, m_lru_lambda, m_final_g]
    small_v = [v_c_ctx, v_norm_g, v_b_ada, v_w_conv_a, v_w_conv_b, v_b_conv_b, v_lru_wa, v_lru_ba, v_lru_wx,
               v_lru_bx, v_lru_lambda, v_final_g]
    small_g = [g_cctx, g_norm, g_bada, g_ca_m, g_cb_m, g_bcb, g_wa, g_ba_m, g_wx, g_bx_m, g_lam_m, g_final]
    small_g = [jnp.reshape(a, b.shape) for a, b in zip(small_g, small_w)]
    d_s, m_s, v_s = _adamw_small(small_g, small_w, small_m, small_v)

    def weights(small_list, ada, win, wout):
        (cctx_, norm_, bada_, ca_, cb_, bcb_, wa_, ba_, wx_, bx_, lam_, final_) = small_list
        return [cctx_, norm_, ada[None], bada_, win[None], ca_, cb_, bcb_, wa_, ba_, wx_, bx_, lam_, wout[None], final_]

    return (loss, grad_x[None],
            *weights(small_g, g_wada, g_win2, g_wout2), *weights(d_s, d_wada, d_win, d_wout),
            *weights(m_s, m_wada, m_win, m_wout), *weights(v_s, v_wada, v_win, v_wout))
```

```python
import functools

import jax
import jax.numpy as jnp
import numpy as np
from jax import lax
from jax.experimental import pallas as pl
from jax.experimental.pallas import tpu as pltpu

F32 = jnp.float32
BF16 = jnp.bfloat16
MESH = pl.DeviceIdType.MESH
NDEV = 8
GRID_W = 64
N_HEADS = 16
LRU_C = 8.0
EPS = 1e-6
MXU_WIDTH = 256
VMEM_LIMIT = 60 * 1024 * 1024

ADAM_LR = 0.001
ADAM_B1 = 0.9
ADAM_B2 = 0.999
ADAM_EPS = 1e-08
ADAM_WD = 0.01
ADAM_STEP = 10
ADAM_C1 = 1.0 - ADAM_B1 ** ADAM_STEP
ADAM_C2 = 1.0 - ADAM_B2 ** ADAM_STEP

HIGHEST = lax.Precision.HIGHEST
ANY = pl.BlockSpec(memory_space=pl.ANY)
VMEM = pl.BlockSpec(memory_space=pltpu.VMEM)


def _call(body, **kw):
    return pl.pallas_call(body, **kw)


def _params(sem=None, vmem=VMEM_LIMIT):
    return pltpu.CompilerParams(dimension_semantics=sem, vmem_limit_bytes=vmem)


def _my_pos():
    return lax.axis_index("x"), lax.axis_index("y"), lax.axis_index("c")


def _idx(pos):
    return 4 * pos[0] + 2 * pos[1] + pos[2]


def _peer(k):
    x, y, c = _my_pos()
    return ((1 - x) if (k >> 2) & 1 else x, (1 - y) if (k >> 1) & 1 else y, (1 - c) if k & 1 else c)


def _exchange_start(src_ref, dst_ref, send_sems, recv_sems, base):
    me = _idx(_my_pos())
    sends = []
    for k in range(1, NDEV):
        cp = pltpu.make_async_remote_copy(
            src_ref=src_ref, dst_ref=dst_ref.at[me], send_sem=send_sems.at[base + k - 1],
            recv_sem=recv_sems.at[base + k - 1], device_id=_peer(k), device_id_type=MESH)
        cp.start()
        sends.append(cp)
    dst_ref[me] = src_ref[...]
    return sends, (src_ref, dst_ref, send_sems, recv_sems, base)


def _exchange_finish(started):
    sends, (src_ref, dst_ref, send_sems, recv_sems, base) = started
    for k in range(1, NDEV):
        peer = _peer(k)
        pltpu.make_async_remote_copy(
            src_ref=src_ref, dst_ref=dst_ref.at[_idx(peer)], send_sem=send_sems.at[base + k - 1],
            recv_sem=recv_sems.at[base + k - 1], device_id=peer, device_id_type=MESH).wait_recv()
    for cp in sends:
        cp.wait_send()


def _exchange_vmem(src_ref, dst_ref, send_sems, recv_sems, base):
    _exchange_finish(_exchange_start(src_ref, dst_ref, send_sems, recv_sems, base))


def _sigmoid(z):
    return 0.5 * jnp.tanh(0.5 * z) + 0.5


def _softplus(x):
    return jnp.maximum(x, 0.0) + jnp.log1p(jnp.exp(-jnp.abs(x)))


def _one_minus_sq(a, la):
    series = (-2.0 * la) * (1.0 + la)
    return jnp.where(la > -0.0015, series, 1.0 - a * a)


def _dot(a, b):
    return jnp.dot(a, b, preferred_element_type=F32)


def _dot_nt(a, b):
    return lax.dot_general(a, b, (((1,), (1,)), ((), ())), preferred_element_type=F32)


def _rows(shape):
    return lax.broadcasted_iota(jnp.int32, shape, 0)


def _scan_matrices(t):
    seg = t // 8
    r = np.arange(t)
    perm = (np.arange(t)[None, :] == ((r % 8) * seg + r // 8)[:, None]).astype(np.float32)
    rows, cols = r[:, None], r[None, :]
    taps, back = [], []
    for rowlen in (GRID_W, t):
        pos = rows % rowlen
        shift = {-2: (cols == rows - 2) & (pos >= 2), -1: (cols == rows - 1) & (pos >= 1),
                 0: cols == rows, 1: (cols == rows + 1) & (pos + 1 < rowlen),
                 2: (cols == rows + 2) & (pos + 2 < rowlen)}
        if rowlen == GRID_W:
            beside = [shift[-1].astype(np.float32), shift[1].astype(np.float32)]
        taps.append(np.stack([perm @ shift[k].astype(np.float32) for k in (-2, -1, 0, 1)]))
        back.append(np.stack([shift[k].astype(np.float32) @ perm.T for k in (2, 1, 0, -1)]))
    as_bf16 = lambda a: jnp.asarray(a, dtype=BF16)
    return as_bf16(np.stack(taps)), as_bf16(np.stack(back)), as_bf16(np.stack([perm, perm.T] + beside))


def _chunk_scan(a, b, reverse):
    row = _rows(a.shape)
    for s in (1, 2, 4):
        if reverse:
            m = row < 8 - s
            sh = 8 - s
        else:
            m = row >= s
            sh = s
        a_s = jnp.where(m, pltpu.roll(a, sh, 0), 1.0)
        b_s = jnp.where(m, pltpu.roll(b, sh, 0), 0.0)
        b = b + a * b_s
        a = a * a_s
    return a, b


def _chain_segments(ptot, hend, carry, reverse):
    ca, cb = _chunk_scan(ptot, hend, reverse)
    incl = ca * carry + cb
    r8 = _rows(incl.shape)
    if reverse:
        start = jnp.where(r8 < 7, pltpu.roll(incl, 7, 0), carry)
        last = incl[0:1, :]
    else:
        start = jnp.where(r8 >= 1, pltpu.roll(incl, 1, 0), carry)
        last = incl[7:8, :]
    return start, jnp.broadcast_to(last, incl.shape)


def _blocks(nblock, reverse):
    order = range(nblock - 1, -1, -1) if reverse else range(nblock)
    return [slice(8 * k, 8 * k + 8) for k in order]


def _scan_tile(a_ref, b_ref, out_ref, carry, reverse):
    t, w = a_ref.shape
    seg = t // 8

    hend, ptot = jnp.zeros((8, w), F32), jnp.ones((8, w), F32)
    for rows in _blocks(seg, reverse):
        a = a_ref[rows, :]
        hend, ptot = a * hend + b_ref[rows, :], a * ptot
    h, new_carry = _chain_segments(ptot, hend, carry, reverse)
    for rows in _blocks(seg, reverse):
        h = a_ref[rows, :] * h + b_ref[rows, :]
        out_ref[rows, :] = h
    return new_carry


def _scan_tile_backward(a_ref, dh_ref, g_ref, carry, reverse):
    t, w = a_ref.shape
    seg = t // 8

    uend, ptot = jnp.zeros((8, w), F32), jnp.ones((8, w), F32)
    for rows in _blocks(seg, reverse):
        a = a_ref[rows, :]
        uend, ptot = a * (dh_ref[rows, :] + uend), a * ptot
    u, new_carry = _chain_segments(ptot, uend, carry, reverse)
    for rows in _blocks(seg, reverse):
        g = dh_ref[rows, :] + u
        g_ref[rows, :] = g
        u = a_ref[rows, :] * g
    return new_carry


def _lru_coef(xb, wg_ref, d, ba, bx, lam, gc):
    w = xb.shape[1]
    xb16 = xb.astype(BF16)
    zr, zi = [], []
    for g in range(w // gc):
        z = _dot(xb16[:, g * gc:(g + 1) * gc], wg_ref[d, g])
        zr.append(z[:, :gc])
        zi.append(z[:, gc:])
    zr = zr[0] if len(zr) == 1 else jnp.concatenate(zr, axis=-1)
    zi = zi[0] if len(zi) == 1 else jnp.concatenate(zi, axis=-1)
    tr = jnp.tanh(zr + ba)
    ti = jnp.tanh(zi + bx)
    sp = _softplus(-lam)
    half = -0.5 * LRU_C * sp
    la = tr * half + half
    a = jnp.exp(la)
    q = _one_minus_sq(a, la)
    rs = lax.rsqrt(jnp.maximum(q, 1e-30))
    return a, q * rs, rs, tr, ti, sp


def _adamw(w, g, m, v):
    m2 = ADAM_B1 * m + (1.0 - ADAM_B1) * g
    v2 = ADAM_B2 * v + (1.0 - ADAM_B2) * (g * g)
    m_hat = m2 / ADAM_C1
    v_hat = v2 / ADAM_C2
    delta = -ADAM_LR * (m_hat / (jnp.sqrt(v_hat) + ADAM_EPS) + ADAM_WD * w)
    return delta, m2, v2


def _mod_forward(c8, cctx8, w_ada, small):
    d = c8.shape[1]
    cols = w_ada.shape[1]

    def body(c_ref, cctx_ref, w_ref, sm_ref, mod_ref, s_ref, sm_all, cbuf, mod_my, send_sems, recv_sems):
        _exchange_vmem(sm_ref, sm_all, send_sems, recv_sems, 2 * (NDEV - 1))
        _exchange_vmem(c_ref, cbuf, send_sems, recv_sems, 0)
        row = _rows((8, d))
        c_all = jnp.zeros((8, d), F32)
        for b in range(NDEV):
            c_all = jnp.where(row == b, cbuf[b], c_all)
        cc = cctx_ref[...]
        s_top = c_all * _sigmoid(c_all)
        s_bot = jnp.where(row == 0, cc * _sigmoid(cc), 0.0)
        s = jnp.concatenate([s_top, s_bot], axis=0)
        s_ref[...] = s
        mod_my[...] = jnp.dot(s, w_ref[...], precision=HIGHEST, preferred_element_type=F32)
        _exchange_vmem(mod_my, mod_ref, send_sems, recv_sems, NDEV - 1)

    return _call(
        body, name="mod_forward",
        out_shape=(jax.ShapeDtypeStruct((NDEV, 16, cols), F32), jax.ShapeDtypeStruct((16, d), F32),
                   jax.ShapeDtypeStruct((NDEV,) + small.shape, F32)),
        in_specs=[VMEM] * 4, out_specs=(VMEM,) * 3,
        scratch_shapes=[pltpu.VMEM((NDEV, 8, d), F32), pltpu.VMEM((16, cols), F32),
                        pltpu.SemaphoreType.DMA((3 * (NDEV - 1),)), pltpu.SemaphoreType.DMA((3 * (NDEV - 1),))],
        compiler_params=_params(),
    )(c8, cctx8, w_ada, small)


def _scatter_copies(src_ref, dst_ref, send_sems, recv_sems):
    me = _idx(_my_pos())
    copies = [pltpu.make_async_copy(src_ref.at[me], dst_ref.at[0], send_sems.at[0])]
    for k in range(1, NDEV):
        peer = _peer(k)
        copies.append(pltpu.make_async_remote_copy(
            src_ref=src_ref.at[_idx(peer)], dst_ref=dst_ref.at[k], send_sem=send_sems.at[k],
            recv_sem=recv_sems.at[k], device_id=peer, device_id_type=MESH))
    return copies


def _gather_copies(src_ref, dst_ref, send_sems, recv_sems):
    me = _idx(_my_pos())
    sends = [pltpu.make_async_copy(src_ref, dst_ref.at[me], send_sems.at[0])]
    arrivals = []
    for k in range(1, NDEV):
        peer = _peer(k)
        sends.append(pltpu.make_async_remote_copy(
            src_ref=src_ref, dst_ref=dst_ref.at[me], send_sem=send_sems.at[k],
            recv_sem=recv_sems.at[k], device_id=peer, device_id_type=MESH))
        arrivals.append(pltpu.make_async_remote_copy(
            src_ref=src_ref, dst_ref=dst_ref.at[_idx(peer)], send_sem=send_sems.at[k],
            recv_sem=recv_sems.at[k], device_id=peer, device_id_type=MESH))
    return sends, arrivals


def _exchange_wait(sends, arrivals):
    sends[0].wait()
    for cp in arrivals:
        cp.wait_recv()
    for cp in sends[1:]:
        cp.wait_send()


def _chip_order(k, c):
    return (6, 4 - 2 * c, 2 + 2 * c, 0)[k]


def _scatter_order(s, c):
    k = s >> 1
    mine = jnp.where(k == 0, 6, jnp.where(k == 1, 4 - 2 * c, jnp.where(k == 2, 2 + 2 * c, 0)))
    theirs = jnp.where(k == 0, 6, jnp.where(k == 1, 2 + 2 * c, jnp.where(k == 2, 4 - 2 * c, 0))) ^ 1
    return jnp.where((s & 1) == 0, theirs, mine)


def _peer_at(dist):
    x, y, c = _my_pos()
    return (x ^ ((dist >> 2) & 1), y ^ ((dist >> 1) & 1), c ^ (dist & 1))


def _reduce_small(packed, lru_parts, w_ada, dsilu_cctx):
    rp = packed.shape[0]
    rl = lru_parts.shape[1]
    d, cols = w_ada.shape
    assert cols % 128 == 0
    cb = cols // 128

    def body(p_ref, l_ref, w_ref, ds_ref, sum_ref, all_ref, lru_ref, cctx_ref,
             lbuf, lsum, cpart, call, send_sems, recv_sems, lsend, lrecv):
        me = _idx(_my_pos())
        scattered = _scatter_copies(l_ref, lbuf, lsend, lrecv)
        for cp in scattered:
            cp.start()
        _exchange_vmem(p_ref, all_ref, send_sems, recv_sems, 0)
        acc = all_ref[0]
        for j in range(1, NDEV):
            acc = acc + all_ref[j]
        sum_ref[...] = acc
        _exchange_wait(scattered, scattered[1:])
        red = lbuf[0]
        for k in range(1, NDEV):
            red = red + lbuf[k]
        lsum[...] = red
        lru_gather = _exchange_start(lsum, lru_ref, send_sems, recv_sems, NDEV - 1)
        part = jnp.zeros((8, d), F32)
        for q in range(cb):
            dm = jnp.broadcast_to(sum_ref[pl.ds((NDEV + me) * cb + q, 1), :], (8, 128))
            part = part + lax.dot_general(dm, w_ref[:, q * 128:(q + 1) * 128],
                                          (((1,), (1,)), ((), ())), precision=HIGHEST,
                                          preferred_element_type=F32)
        cpart[...] = part
        _exchange_vmem(cpart, call, send_sems, recv_sems, 2 * (NDEV - 1))
        _exchange_finish(lru_gather)
        tot = call[0]
        for j in range(1, NDEV):
            tot = tot + call[j]
        cctx_ref[...] = tot * ds_ref[...]

    return _call(
        body, name="reduce_small",
        out_shape=(jax.ShapeDtypeStruct((rp, 128), F32), jax.ShapeDtypeStruct((NDEV, rp, 128), F32),
                   jax.ShapeDtypeStruct((NDEV, rl, 128), F32), jax.ShapeDtypeStruct((8, d), F32)),
        in_specs=[VMEM] * 4, out_specs=(VMEM,) * 4,
        scratch_shapes=[pltpu.VMEM((NDEV, rl, 128), F32), pltpu.VMEM((rl, 128), F32), pltpu.VMEM((8, d), F32),
                        pltpu.VMEM((NDEV, 8, d), F32),
                        pltpu.SemaphoreType.DMA((3 * (NDEV - 1),)), pltpu.SemaphoreType.DMA((3 * (NDEV - 1),)),
                        pltpu.SemaphoreType.DMA((NDEV,)), pltpu.SemaphoreType.DMA((NDEV,))],
        compiler_params=_params(),
    )(packed, lru_parts, w_ada, dsilu_cctx)


def _normalize(src, mv, la, row0, tm, name, prev=None):
    rows, d = src.shape
    blk0 = row0 // tm

    def body(*refs):
        x_ref, mv_ref, h_ref = refs[0], refs[1], refs[-1]
        xf = x_ref[...]
        r = lax.rsqrt(jnp.mean(xf * xf, axis=-1, keepdims=True) + EPS)
        h = xf * r * (mv_ref[0:1, :] * (1.0 + mv_ref[1:2, :])) + mv_ref[2:3, :]
        h_ref[...] = h.astype(BF16)

    in_specs = [pl.BlockSpec((tm, d), lambda i: (i, 0)), pl.BlockSpec((8, d), lambda i: (0, 0))]
    args = [src, mv]
    aliases = {}
    if prev is not None:
        in_specs += [ANY]
        args += [prev]
        aliases = {2: 0}
    return _call(
        body, name=name,
        grid=(rows // tm,),
        out_shape=jax.ShapeDtypeStruct((la, d), BF16),
        in_specs=in_specs,
        out_specs=pl.BlockSpec((tm, d), lambda i: (blk0 + i, 0)),
        input_output_aliases=aliases,
        compiler_params=_params(("arbitrary",)),
    )(*args)


def _gather_order(step):
    return (step & 1) | (((step >> 2) & 1) << 1) | (((step >> 1) & 1) << 2)


def _in_projection(h, w_shard, wo_shard, tm):
    la, d = h.shape
    bw = w_shard.shape[1]
    ni = la // tm
    where = jnp.reshape(_idx(_my_pos()), (1,)).astype(jnp.int32)

    def body(me_ref, h_ref, w_ref, wo_ref, p_ref, all_ref, wo_all, wbuf, send_sems, recv_sems, local_sems,
             wo_send, wo_recv):
        s, i = pl.program_id(0), pl.program_id(1)
        x, y, c = _my_pos()
        wo_sends, wo_arrivals = _gather_copies(wo_ref, wo_all, wo_send, wo_recv)

        @pl.when((s == NDEV // 2) & (i == 0))
        def _():
            for cp in wo_sends:
                cp.start()

        me, sibling = (x, y, c), (x, y, 1 - c)
        chips = [(1 - x, y), (x, 1 - y), (1 - x, 1 - y)]

        def copy(k, block, to, from_shard=False):
            return pltpu.make_async_remote_copy(
                src_ref=w_ref if from_shard else all_ref.at[_idx(block)], dst_ref=all_ref.at[_idx(block)],
                send_sem=send_sems.at[k], recv_sem=recv_sems.at[k], device_id=to, device_id_type=MESH)

        def load(block, slot):
            return pltpu.make_async_copy(all_ref.at[_idx(block)], wbuf.at[slot], local_sems.at[1])

        keep = pltpu.make_async_copy(w_ref, all_ref.at[_idx(me)], local_sems.at[0])
        first = [copy(0, me, sibling, True)] + [copy(1 + j, me, (*chip, c), True) for j, chip in enumerate(chips)]
        passed = [copy(4 + j, (*chip, c), sibling) for j, chip in enumerate(chips)]
        steps = [(copy(0, sibling, me), None, sibling)]
        for j, chip in enumerate(chips):
            steps.append((copy(1 + j, (*chip, c), me), passed[j], (*chip, c)))
            steps.append((copy(4 + j, (*chip, 1 - c), me), None, (*chip, 1 - c)))

        @pl.when((s == 0) & (i == 0))
        def _():
            keep.start()
            mine = pltpu.make_async_copy(w_ref, wbuf.at[0], local_sems.at[1])
            mine.start()
            for cp in first:
                cp.start()
            mine.wait()

        for n, (arrival, forward, block) in enumerate(steps, start=1):
            @pl.when((s == n - 1) & (i == ni - 1))
            def _(arrival=arrival, forward=forward, block=block, n=n):
                arrival.wait_recv()
                if forward is not None:
                    forward.start()
                load(block, n % 2).start()

        @pl.when((s > 0) & (i == 0))
        def _():
            load(me, s % 2).wait()

        p_ref[...] = _dot(h_ref[...], wbuf[s % 2]).astype(BF16)

        @pl.when((s == NDEV - 1) & (i == ni - 1))
        def _():
            for cp in first + passed:
                cp.wait_send()
            keep.wait()
            _exchange_wait(wo_sends, wo_arrivals)

    return _call(
        body, name="in_projection",
        grid_spec=pltpu.PrefetchScalarGridSpec(
            num_scalar_prefetch=1, grid=(NDEV, ni),
            in_specs=[pl.BlockSpec((tm, d), lambda s, i, me_ref: (i, 0)), ANY, ANY],
            out_specs=(pl.BlockSpec((tm, bw), lambda s, i, me_ref: (i, me_ref[0] ^ _gather_order(s))), ANY, ANY),
            scratch_shapes=[pltpu.VMEM((2, d, bw), BF16), pltpu.SemaphoreType.DMA((7,)),
                            pltpu.SemaphoreType.DMA((7,)), pltpu.SemaphoreType.DMA((2,)),
                            pltpu.SemaphoreType.DMA((NDEV,)), pltpu.SemaphoreType.DMA((NDEV,))]),
        out_shape=(jax.ShapeDtypeStruct((la, NDEV * bw), BF16), jax.ShapeDtypeStruct((NDEV, d, bw), BF16),
                   jax.ShapeDtypeStruct((NDEV,) + wo_shard.shape, wo_shard.dtype)),
        compiler_params=_params(("arbitrary", "arbitrary")),
    )(where, h, w_shard, wo_shard)


def _conv_input(p, wcb, taps_m, l, t):
    la = p.shape[0]
    w = wcb.shape[1]
    nt = l // t

    def body(v_ref, wcb_ref, tm_ref, xb_ref):
        taps = _dot(tm_ref[...].reshape(4 * t, t), v_ref[...])
        xb = wcb_ref[4:5, :] + wcb_ref[0:1, :] * taps[0:t]
        for j in range(1, 4):
            xb = xb + wcb_ref[j:j + 1, :] * taps[j * t:(j + 1) * t]
        xb_ref[...] = xb

    return _call(
        body, name="conv_input",
        grid=(nt + 1,),
        out_shape=jax.ShapeDtypeStruct((la, w), F32),
        in_specs=[pl.BlockSpec((t, w), lambda i: (i, 4)), pl.BlockSpec((8, w), lambda i: (0, 0)),
                  pl.BlockSpec((None, 4, t, t), lambda i: (i // nt, 0, 0, 0))],
        out_specs=pl.BlockSpec((t, w), lambda i: (i, 0)),
        compiler_params=_params(("arbitrary",)),
    )(p, wcb, taps_m)


def _lru_forward(xb, wg, lv, l, t):
    la, w = xb.shape
    gc = wg.shape[2]
    nt = l // t

    def body(xf_ref, xr_ref, wg_ref, lv_ref, hf_ref, hr_ref, a_s, b_s, carry):
        @pl.when(pl.program_id(0) == 0)
        def _():
            carry[...] = jnp.zeros_like(carry)

        for dr, (x_ref, h_ref) in enumerate(((xf_ref, hf_ref), (xr_ref, hr_ref))):
            x = x_ref[...]
            a, s, _, _, ti, _ = _lru_coef(x, wg_ref, dr, lv_ref[3 * dr:3 * dr + 1, :],
                                          lv_ref[3 * dr + 1:3 * dr + 2, :], lv_ref[3 * dr + 2:3 * dr + 3, :], gc)
            a_s[...] = a
            b_s[...] = (s * x) * (0.5 * ti + 0.5)
            carry[dr] = _scan_tile(a_s, b_s, h_ref, carry[dr], dr == 1)

    full = lambda shape: pl.BlockSpec(shape, lambda i: (0,) * len(shape))
    fmap = lambda i: (jnp.where(i == 0, nt, i - 1), 0)
    rmap = lambda i: (jnp.where(i == 0, nt, nt - i), 0)
    return _call(
        body, name="lru_forward",
        grid=(nt + 1,),
        out_shape=(jax.ShapeDtypeStruct((la, w), F32), jax.ShapeDtypeStruct((la, w), F32)),
        in_specs=[pl.BlockSpec((t, w), fmap), pl.BlockSpec((t, w), rmap), full(wg.shape), full(lv.shape)],
        out_specs=(pl.BlockSpec((t, w), fmap), pl.BlockSpec((t, w), rmap)),
        scratch_shapes=[pltpu.VMEM((t, w), F32), pltpu.VMEM((t, w), F32), pltpu.VMEM((2, 8, w), F32)],
        compiler_params=_params(("arbitrary",)),
    )(xb, xb, wg, lv)


def _mix_gates(p_refs, hf_ref, hr_ref, wca_ref, perm_ref, t, w):
    bl, cl, ul, gl, ql = [r[...].astype(F32) for r in p_refs]
    tt = cl * ul
    tt16 = tt.astype(BF16)
    beside = _dot(perm_ref[2:4].reshape(2 * t, t), tt16)
    before, after = beside[:t], beside[t:]
    z = wca_ref[0:1, :] * before + wca_ref[1:2, :] * tt + wca_ref[2:3, :] * after
    sig_g = _sigmoid(gl)
    sig_q = _sigmoid(ql)
    ylru = _dot(perm_ref[1], (hf_ref[...] + hr_ref[...]).astype(BF16))
    return bl, cl, ul, gl, ql, (before, tt, after), z, sig_g, sig_q, ylru


def _p_specs(t, w, nt):
    return [pl.BlockSpec((t, w), functools.partial(lambda i, s: (jnp.minimum(i, nt - 1), s), s=s))
            for s in (0, 1, 2, 3, 5)]


def _mix_forward(x, tgt, p, hf, hr, wo, ov, wca, perm, t):
    l, d = x.shape
    w = d // 2
    nt = l // t

    def body(x_ref, tg_ref, b_ref, c_ref, u_ref, g_ref, q_ref, hf_ref, hr_ref, wo_ref, ov_ref, wca_ref, perm_ref,
             dn_ref, ct_ref, do_ref, part_ref):
        i = pl.program_id(0)
        bl, _, _, gl, ql, _, z, sig_g, sig_q, ylru = _mix_gates(
            (b_ref, c_ref, u_ref, g_ref, q_ref), hf_ref, hr_ref, wca_ref, perm_ref, t, w)
        ya = bl * z * (gl * sig_g)
        yb = ylru * (ql * sig_q)
        ct_ref[:, 0:w] = ya.astype(BF16)
        ct_ref[:, w:] = yb.astype(BF16)
        out = _dot(ya.astype(BF16), wo_ref[0:w, :]) + _dot(yb.astype(BF16), wo_ref[w:, :])
        gate, fg = ov_ref[0:1, :], ov_ref[1:2, :]
        n = x_ref[...] + gate * out
        rr = lax.rsqrt(jnp.mean(n * n, axis=-1, keepdims=True) + EPS)
        nh = n * rr
        e = nh * fg - tg_ref[...]
        loss = 0.5 * jnp.sum(jnp.mean(e * e, axis=-1, keepdims=True), axis=0, keepdims=True)
        dy = e * (1.0 / d)
        dnh = dy * fg
        dn = rr * (dnh - nh * jnp.mean(dnh * nh, axis=-1, keepdims=True))
        dn_ref[...] = dn.astype(BF16)
        do_ref[...] = (dn * gate).astype(BF16)

        @pl.when(i == 0)
        def _():
            part_ref[...] = jnp.zeros_like(part_ref)

        part_ref[0:1, :] += jnp.sum(dy * nh, axis=0, keepdims=True)
        part_ref[1:2, :] += jnp.sum(dn * out, axis=0, keepdims=True)
        part_ref[2:3, :] += jnp.broadcast_to(loss, (1, d))

    tile = lambda cols: pl.BlockSpec((t, cols), lambda i: (i, 0))
    full = lambda shape: pl.BlockSpec(shape, lambda i: (0,) * len(shape))
    return _call(
        body, name="mix_forward",
        grid=(nt,),
        out_shape=(jax.ShapeDtypeStruct((l, d), BF16), jax.ShapeDtypeStruct((l, d), BF16),
                   jax.ShapeDtypeStruct((l, d), BF16), jax.ShapeDtypeStruct((8, d), F32)),
        in_specs=[tile(d), tile(d)] + _p_specs(t, w, nt) + [tile(w), tile(w),
                  pl.BlockSpec((d, d), lambda i: (0, 0), pipeline_mode=pl.Buffered(1)),
                  full(ov.shape), full(wca.shape), full(perm.shape)],
        out_specs=(tile(d), tile(d), tile(d), full((8, d))),
        compiler_params=_params(("arbitrary",)),
    )(x, tgt, p, p, p, p, p, hf, hr, wo, ov, wca, perm)


def _mix_backward(dout, p, hf, hr, wo, wca, perm, l, t):
    d = dout.shape[1]
    w = d // 2
    nt = l // t
    la = p.shape[0]

    def body(do_ref, b_ref, c_ref, u_ref, g_ref, q_ref, hf_ref, hr_ref, wo_ref, wca_ref, perm_ref,
             dp_ref, dh_ref, part_ref):
        i = pl.program_id(0)

        @pl.when(i == 0)
        def _():
            part_ref[...] = jnp.zeros_like(part_ref)

        @pl.when(i == nt)
        def _():
            dp_ref[...] = jnp.zeros_like(dp_ref)

        @pl.when(i < nt)
        def _():
            bl, cl, ul, gl, ql, taps, z, sig_g, sig_q, ylru = _mix_gates(
                (b_ref, c_ref, u_ref, g_ref, q_ref), hf_ref, hr_ref, wca_ref, perm_ref, t, w)
            do = do_ref[...]
            dya = _dot_nt(do, wo_ref[0:w, :])
            dyb = _dot_nt(do, wo_ref[w:, :])
            sg = gl * sig_g
            dz = dya * bl * sg
            dz16 = dz.astype(BF16)
            beside = _dot(perm_ref[2:4].reshape(2 * t, t), dz16)
            dt = wca_ref[0:1, :] * beside[t:] + wca_ref[1:2, :] * dz + wca_ref[2:3, :] * beside[:t]
            dp_ref[:, 0:w] = (dya * z * sg).astype(BF16)
            dp_ref[:, w:2 * w] = (dt * ul).astype(BF16)
            dp_ref[:, 2 * w:3 * w] = (dt * cl).astype(BF16)
            dp_ref[:, 3 * w:4 * w] = (dya * bl * z * (sig_g * (1.0 + gl * (1.0 - sig_g)))).astype(BF16)
            dp_ref[:, 4 * w:5 * w] = jnp.zeros((t, w), BF16)
            dp_ref[:, 5 * w:6 * w] = (dyb * ylru * (sig_q * (1.0 + ql * (1.0 - sig_q)))).astype(BF16)
            dh_ref[...] = _dot(perm_ref[0], (dyb * (ql * sig_q)).astype(BF16)).astype(BF16)
            for j in range(3):
                part_ref[j:j + 1, :] += jnp.sum(dz * taps[j], axis=0, keepdims=True)

    clamp = lambda cols: pl.BlockSpec((t, cols), lambda i: (jnp.minimum(i, nt - 1), 0))
    full = lambda shape: pl.BlockSpec(shape, lambda i: (0,) * len(shape))
    return _call(
        body, name="mix_backward",
        grid=(nt + 1,),
        out_shape=(jax.ShapeDtypeStruct((la, 6 * w), BF16), jax.ShapeDtypeStruct((l, w), BF16),
                   jax.ShapeDtypeStruct((8, w), F32)),
        in_specs=[clamp(d)] + _p_specs(t, w, nt) + [clamp(w), clamp(w),
                  pl.BlockSpec((d, d), lambda i: (0, 0), pipeline_mode=pl.Buffered(1)), full(wca.shape),
                  full(perm.shape)],
        out_specs=(pl.BlockSpec((t, 6 * w), lambda i: (i, 0)), clamp(w), full((8, w))),
        compiler_params=_params(("arbitrary",)),
    )(dout, p, p, p, p, p, hf, hr, wo, wca, perm)


def _lru_backward(direction, xb, dhs, hs, wg, lv, l, t, conv=None):
    la, w = hs.shape
    gc = wg.shape[2]
    ng = w // gc
    nt = l // t
    nblk8 = la // 8
    last = conv is not None
    assert last == (direction == 1)

    if direction == 0:
        tile = lambda i: jnp.where(i == nt, nt, nt - 1 - i)
        halo = lambda i: jnp.where(tile(i) == 0, nblk8 - 1, tile(i) * (t // 8) - 1)
    else:
        tile = lambda i: i
        halo = lambda i: jnp.minimum((i + 1) * (t // 8), nblk8 - 1)

    def body(*refs):
        x_ref, dh_ref, hs_ref, halo_ref, wg_ref, lv_ref = refs[:6]
        if last:
            v_ref, wcb_ref, bm_ref, dxo_ref = refs[6:10]
        out_ref, dwg_ref, part_ref, a_s, dh_s, g_s, carry = refs[-7:]
        i = pl.program_id(0)
        is_ctx = i == nt

        @pl.when(i == 0)
        def _():
            carry[...] = jnp.zeros_like(carry)
            dwg_ref[...] = jnp.zeros_like(dwg_ref)
            part_ref[...] = jnp.zeros_like(part_ref)

        xb = x_ref[...]
        lam = lv_ref[3 * direction + 2:3 * direction + 3, :]
        a, s, rs, tr, ti, sp = _lru_coef(xb, wg_ref, direction, lv_ref[3 * direction:3 * direction + 1, :],
                                         lv_ref[3 * direction + 1:3 * direction + 2, :], lam, gc)
        hs_t = hs_ref[...]
        r8 = _rows((8, w))
        if direction == 0:
            edge = jnp.where(is_ctx, 0.0, halo_ref[7:8, :])
            first = jnp.where(r8 == 0, edge, pltpu.roll(hs_t[t - 8:, :], 1, 0))
            hprev = jnp.concatenate([first, hs_t[:t - 8, :]], axis=0)
        else:
            edge = jnp.where(is_ctx, 0.0, halo_ref[0:1, :])
            final = jnp.where(r8 == 7, edge, pltpu.roll(hs_t[:8, :], 7, 0))
            hprev = jnp.concatenate([hs_t[8:, :], final], axis=0)
        a_s[...] = a
        dh_s[...] = jnp.where(is_ctx, 0.0, dh_ref[...].astype(F32))
        carry[...] = _scan_tile_backward(a_s, dh_s, g_s, carry[...], direction == 0)

        g = g_s[...]
        r = 0.5 * tr + 0.5
        ig = 0.5 * ti + 0.5
        ix = ig * xb
        gs = g * s
        dla = (g * a) * (hprev - ix * (a * rs))
        dxb = gs * ig
        dzr = dla * (r * (1.0 - tr)) * (-LRU_C * sp)
        dzi = gs * ix * (1.0 - ti)
        part_ref[0:1, :] += jnp.sum(dzr, axis=0, keepdims=True)
        part_ref[1:2, :] += jnp.sum(dzi, axis=0, keepdims=True)
        part_ref[2:3, :] += jnp.sum(dla * r, axis=0, keepdims=True) * (LRU_C * _sigmoid(-lam))
        pieces = []
        for gi in range(ng):
            sl = slice(gi * gc, (gi + 1) * gc)
            dz = jnp.concatenate([dzr[:, sl], dzi[:, sl]], axis=-1).astype(BF16)
            pieces.append(_dot_nt(dz, wg_ref[direction, gi]))
            dwg_ref[gi] += _dot(xb[:, sl].T.astype(BF16), dz)
        dxb = dxb + (pieces[0] if ng == 1 else jnp.concatenate(pieces, axis=-1))
        if not last:
            out_ref[...] = dxb
        else:
            dxb = dxb + dxo_ref[...]
            v = v_ref[...].astype(F32)
            backs = _dot(bm_ref[...].reshape(4 * t, t), dxb.astype(BF16))
            dv = jnp.zeros((t, w), F32)
            for j in range(4):
                back = backs[j * t:(j + 1) * t]
                dv = dv + wcb_ref[j:j + 1, :] * back
                part_ref[4 + j:5 + j, :] += jnp.sum(back * v, axis=0, keepdims=True)
            out_ref[...] = dv.astype(BF16)
            part_ref[3:4, :] += jnp.sum(dxb, axis=0, keepdims=True)

    full = lambda shape: pl.BlockSpec(shape, lambda i: (0,) * len(shape))
    kind = lambda i: (jnp.where(i == nt, 1, 0), 0, 0, 0)
    in_specs = [pl.BlockSpec((t, w), lambda i: (tile(i), 0)),
                pl.BlockSpec((t, w), lambda i: (jnp.minimum(tile(i), nt - 1), 0)),
                pl.BlockSpec((t, w), lambda i: (tile(i), 0)),
                pl.BlockSpec((8, w), lambda i: (halo(i), 0)),
                full(wg.shape), full(lv.shape)]
    args = [xb, dhs, hs, hs, wg, lv]
    if last:
        p, wcb, back_m, dxb_other, dp = conv
        in_specs += [pl.BlockSpec((t, w), lambda i: (tile(i), 4)), full(wcb.shape),
                     pl.BlockSpec((None, 4, t, t), kind), pl.BlockSpec((t, w), lambda i: (tile(i), 0)), ANY]
        args += [p, wcb, back_m, dxb_other, dp]
        out0 = jax.ShapeDtypeStruct(dp.shape, dp.dtype)
        spec0 = pl.BlockSpec((t, w), lambda i: (tile(i), 4))
        aliases = {10: 0}
    else:
        out0 = jax.ShapeDtypeStruct((la, w), F32)
        spec0 = pl.BlockSpec((t, w), lambda i: (tile(i), 0))
        aliases = {}
    return _call(
        body, name="lru_backward_%d" % direction,
        grid=(nt + 1,),
        out_shape=(out0, jax.ShapeDtypeStruct((ng, gc, 2 * gc), F32), jax.ShapeDtypeStruct((8, w), F32)),
        in_specs=in_specs,
        out_specs=(spec0, full((ng, gc, 2 * gc)), full((8, w))),
        scratch_shapes=[pltpu.VMEM((t, w), F32), pltpu.VMEM((t, w), F32), pltpu.VMEM((t, w), F32),
                        pltpu.VMEM((8, w), F32)],
        input_output_aliases=aliases,
        compiler_params=_params(("arbitrary",)),
    )(*args)


def _weight_grad_t(a, b, nblk_m, nblk_n, tk, name):
    k, m = a.shape
    n = b.shape[1]
    bm, bn = m // nblk_m, n // nblk_n
    nk = k // tk

    def body(a_ref, b_ref, o_ref, acc):
        kk = pl.program_id(2)

        @pl.when(kk == 0)
        def _():
            acc[...] = jnp.zeros_like(acc)

        acc[...] += lax.dot_general(a_ref[...], b_ref[...], (((0,), (0,)), ((), ())), preferred_element_type=F32)

        @pl.when(kk == nk - 1)
        def _():
            o_ref[...] = acc[...].astype(BF16)

    return _call(
        body, name=name,
        grid=(nblk_m, nblk_n, nk),
        out_shape=jax.ShapeDtypeStruct((nblk_m * nblk_n, bm, bn), BF16),
        in_specs=[pl.BlockSpec((tk, bm), lambda i, j, kk: (kk, i)),
                  pl.BlockSpec((tk, bn), lambda i, j, kk: (kk, j))],
        out_specs=pl.BlockSpec((None, bm, bn), lambda i, j, kk: (i * nblk_n + j, 0, 0)),
        scratch_shapes=[pltpu.VMEM((bm, bn), F32)],
        compiler_params=_params(("arbitrary", "arbitrary", "arbitrary")),
    )(a, b)


def _weight_grad_scatter(at, b, other, tk, name):
    k, m = at.shape
    n = b.shape[1]
    bn = n // NDEV
    nk = k // tk
    where = jnp.stack([_idx(_my_pos()), lax.axis_index("c")]).astype(jnp.int32)
    tn = (((0,), (0,)), ((), ()))

    def body(w_ref, a_ref, b_ref, o_ref, recv_ref, osc_ref, acc, sbuf, sib, sib_send, sib_recv, chip_send,
             chip_recv, keep_sem, o_send, o_recv):
        s, kk = pl.program_id(0), pl.program_id(1)
        x, y, c = _my_pos()
        others = _scatter_copies(o_ref, osc_ref, o_send, o_recv)

        @pl.when((s == 0) & (kk == 0))
        def _():
            for cp in others:
                cp.start()

        @pl.when(kk == 0)
        def _():
            acc[...] = lax.dot_general(a_ref[...], b_ref[...], tn, preferred_element_type=F32)

        @pl.when(kk > 0)
        def _():
            acc[...] += lax.dot_general(a_ref[...], b_ref[...], tn, preferred_element_type=F32)

        def to_sibling(j):
            return pltpu.make_async_remote_copy(
                src_ref=sbuf.at[0], dst_ref=sib.at[j], send_sem=sib_send.at[j], recv_sem=sib_recv.at[j],
                device_id=(x, y, 1 - c), device_id_type=MESH)

        def to_chip(j):
            dist = _chip_order(j, c)
            return pltpu.make_async_remote_copy(
                src_ref=sbuf.at[1], dst_ref=recv_ref.at[dist // 2], send_sem=chip_send.at[j],
                recv_sem=chip_recv.at[dist // 2], device_id=_peer_at(dist), device_id_type=MESH)

        keep = pltpu.make_async_copy(sbuf.at[1], recv_ref.at[0], keep_sem)
        sends = []
        for j in range(4):
            sends += [to_sibling(j), to_chip(j) if j < 3 else keep]

        for st in range(NDEV):
            @pl.when((kk == nk - 1) & (s == st))
            def _(st=st):
                if st >= 2:
                    sends[st - 2].wait_send()
                part = acc[...]
                if st % 2 == 1:
                    to_sibling(st // 2).wait_recv()
                    part = part + sib[st // 2].astype(F32)
                sbuf[st % 2] = part.astype(BF16)
                sends[st].start()
                if st == NDEV - 1:
                    sends[st - 1].wait_send()
                    sends[st].wait()
                    for j in range(1, 4):
                        pltpu.make_async_remote_copy(
                            src_ref=sbuf.at[0], dst_ref=recv_ref.at[j], send_sem=chip_send.at[0],
                            recv_sem=chip_recv.at[j], device_id=_peer_at(2 * j), device_id_type=MESH).wait_recv()
                    _exchange_wait(others, others[1:])

    blk = lambda s, w_ref: w_ref[0] ^ _scatter_order(s, w_ref[1])
    return _call(
        body, name=name,
        grid_spec=pltpu.PrefetchScalarGridSpec(
            num_scalar_prefetch=1, grid=(NDEV, nk),
            in_specs=[pl.BlockSpec((tk, m), lambda s, kk, w_ref: (kk, 0)),
                      pl.BlockSpec((tk, bn), lambda s, kk, w_ref: (kk, blk(s, w_ref))), ANY],
            out_specs=(ANY, ANY),
            scratch_shapes=[pltpu.VMEM((m, bn), F32), pltpu.VMEM((2, m, bn), BF16), pltpu.VMEM((4, m, bn), BF16),
                            pltpu.SemaphoreType.DMA((4,)), pltpu.SemaphoreType.DMA((4,)),
                            pltpu.SemaphoreType.DMA((4,)), pltpu.SemaphoreType.DMA((4,)),
                            pltpu.SemaphoreType.DMA, pltpu.SemaphoreType.DMA((NDEV,)),
                            pltpu.SemaphoreType.DMA((NDEV,))]),
        out_shape=(jax.ShapeDtypeStruct((4, m, bn), BF16), jax.ShapeDtypeStruct(other.shape, other.dtype)),
        compiler_params=_params(("arbitrary", "arbitrary")),
    )(where, at, b, other)


def _input_backward(dp, w_all, src, mv, row0, tm, nbk, name, dn=None):
    rows, d = src.shape
    nb, _, bw = w_all.shape
    nk = nb // nbk
    ni = rows // tm
    blk0 = row0 // tm
    latent = dn is not None

    def body(*refs):
        dp_ref, w_ref, x_ref, mv_ref = refs[:4]
        outs = refs[4 + latent:]
        part_ref, acc = outs[latent], outs[latent + 1]
        i, k = pl.program_id(0), pl.program_id(1)

        def product():
            step = _dot_nt(dp_ref[:, 0:bw], w_ref[0])
            for q in range(1, nbk):
                step = step + _dot_nt(dp_ref[:, q * bw:(q + 1) * bw], w_ref[q])
            return step

        def finish(slot):
            xf = x_ref[...]
            r = lax.rsqrt(jnp.mean(xf * xf, axis=-1, keepdims=True) + EPS)
            xn = xf * r
            dhl = acc[slot]
            gain, sc = mv_ref[0:1, :], mv_ref[1:2, :]
            dhx = jnp.sum(dhl * xn, axis=0, keepdims=True)
            part_ref[0:1, :] += jnp.sum(dhl, axis=0, keepdims=True)
            part_ref[1:2, :] += dhx * gain
            part_ref[2:3, :] += dhx * (1.0 + sc)
            if latent:
                dxn = dhl * (gain * (1.0 + sc))
                outs[0][...] = (refs[4][...].astype(F32)
                                + r * (dxn - xn * jnp.mean(dxn * xn, axis=-1, keepdims=True)))

        @pl.when((i == 0) & (k == 0))
        def _():
            part_ref[...] = jnp.zeros_like(part_ref)
            acc[0] = product()

        @pl.when((i > 0) & (i < ni) & (k == 0))
        def _():
            acc[i % 2] = product()
            finish((i - 1) % 2)

        @pl.when((i == ni) & (k == 0))
        def _():
            finish((ni - 1) % 2)

        @pl.when((i < ni) & (k > 0))
        def _():
            acc[i % 2] += product()

    tile = pl.BlockSpec((tm, d), lambda i, k: (jnp.maximum(i - 1, 0), 0))
    vec = pl.BlockSpec((8, d), lambda i, k: (0, 0))
    kblock = lambda i, k: jnp.where(i == ni, nk - 1, k)
    return _call(
        body, name=name,
        grid=(ni + 1, nk),
        out_shape=((jax.ShapeDtypeStruct((rows, d), F32),) if latent else ()) + (jax.ShapeDtypeStruct((8, d), F32),),
        in_specs=[pl.BlockSpec((tm, nbk * bw), lambda i, k: (blk0 + jnp.minimum(i, ni - 1), kblock(i, k))),
                  pl.BlockSpec((nbk, d, bw), lambda i, k: (kblock(i, k), 0, 0)), tile, vec]
                 + ([tile] if latent else []),
        out_specs=((tile,) if latent else ()) + (vec,),
        scratch_shapes=[pltpu.VMEM((2, tm, d), F32)],
        compiler_params=_params(("arbitrary", "arbitrary")),
    )(*([dp, w_all, src, mv] + ([dn] if latent else [])))


def _adamw_scattered(parts, w, m, v, tr):
    r, c = w.shape
    nslot = parts.shape[0]

    def body(p_ref, w_ref, m_ref, v_ref, g_ref, d_ref, m2_ref, v2_ref):
        g = p_ref[0].astype(F32)
        for k in range(1, nslot):
            g = g + p_ref[k].astype(F32)
        g_ref[...] = g
        d_ref[...], m2_ref[...], v2_ref[...] = _adamw(w_ref[...], g, m_ref[...], v_ref[...])

    tile = pl.BlockSpec((tr, c), lambda i: (i, 0))
    return _call(
        body, name="adamw_scattered_%dx%d" % (r, c),
        grid=(r // tr,),
        out_shape=tuple(jax.ShapeDtypeStruct((r, c), F32) for _ in range(4)),
        in_specs=[pl.BlockSpec((nslot, tr, c), lambda i: (0, i, 0)), tile, tile, tile],
        out_specs=(tile,) * 4,
        compiler_params=_params(("arbitrary",)),
    )(parts, w, m, v)


def _adamw_ada(st, dmod, w, m, v, tr):
    r, c = w.shape

    def body(s_ref, dm_ref, w_ref, m_ref, v_ref, g_ref, d_ref, m2_ref, v2_ref):
        g = jnp.dot(s_ref[...], dm_ref[...], precision=HIGHEST, preferred_element_type=F32)
        g_ref[...] = g
        d_ref[...], m2_ref[...], v2_ref[...] = _adamw(w_ref[...], g, m_ref[...], v_ref[...])

    tile = pl.BlockSpec((tr, c), lambda i: (i, 0))
    return _call(
        body, name="adamw_ada",
        grid=(r // tr,),
        out_shape=tuple(jax.ShapeDtypeStruct((r, c), F32) for _ in range(4)),
        in_specs=[pl.BlockSpec((tr, 16), lambda i: (i, 0)), pl.BlockSpec((16, c), lambda i: (0, 0)),
                  tile, tile, tile],
        out_specs=(tile,) * 4,
        compiler_params=_params(("arbitrary",)),
    )(st, dmod, w, m, v)


def _adamw_small(gs, ws, ms, vs):
    n = len(ws)

    def body(*refs):
        for j in range(n):
            g_ref, w_ref, m_ref, v_ref = refs[j], refs[n + j], refs[2 * n + j], refs[3 * n + j]
            d_ref, m2_ref, v2_ref = refs[4 * n + j], refs[5 * n + j], refs[6 * n + j]
            d_ref[...], m2_ref[...], v2_ref[...] = _adamw(w_ref[...], g_ref[...], m_ref[...], v_ref[...])

    shapes = tuple(jax.ShapeDtypeStruct(a.shape, F32) for a in ws)
    out = _call(
        body, name="adamw_small",
        out_shape=shapes * 3,
        in_specs=[VMEM] * (4 * n), out_specs=(VMEM,) * (3 * n),
        compiler_params=_params(),
    )(*gs, *ws, *ms, *vs)
    return list(out[:n]), list(out[n:2 * n]), list(out[2 * n:])


def _blockdiag_groups(wh, gc):
    h, dh, _ = wh.shape
    g = gc // dh
    w4 = wh.reshape(h // g, g, dh, dh)
    bd = jnp.einsum("ngij,gh->ngihj", w4, jnp.eye(g, dtype=wh.dtype))
    return bd.reshape(h // g, gc, gc)


def _blockdiag_extract(bd, dh):
    ng, gc, _ = bd.shape
    g = gc // dh
    x = bd.reshape(ng, g, dh, g, dh)
    return jnp.einsum("ngihj,gh->ngij", x, jnp.eye(g, dtype=bd.dtype)).reshape(ng * g, dh, dh)


def _largest_tile(n, cap):
    return max(q for q in range(128, min(n, cap) + 1, 128) if n % q == 0)


def _rows8(*vecs):
    rows = [jnp.reshape(v, (1, -1)).astype(F32) for v in vecs]
    n = rows[0].shape[1]
    return jnp.concatenate(rows + [jnp.zeros((8 - len(rows), n), F32)], axis=0)


def _pack(pieces):
    flat = jnp.concatenate([jnp.reshape(a, (-1,)).astype(F32) for a in pieces])
    total = -(-flat.shape[0] // 1024) * 1024
    return jnp.pad(flat, (0, total - flat.shape[0])).reshape(total // 128, 128)


def _unpack(packed, shapes):
    flat = packed.reshape(-1)
    out, off = [], 0
    for s in shapes:
        n = 1
        for q in s:
            n *= q
        out.append(flat[off:off + n].reshape(s))
        off += n
    return out


def kernel(x, c, ctx, c_ctx, norm_g, w_ada, b_ada, w_in, w_conv_a, w_conv_b, b_conv_b, lru_wa, lru_ba, lru_wx, lru_bx, lru_lambda, w_out, final_g, loss_target, m_c_ctx, m_norm_g, m_w_ada, m_b_ada, m_w_in, m_w_conv_a, m_w_conv_b, m_b_conv_b, m_lru_wa, m_lru_ba, m_lru_wx, m_lru_bx, m_lru_lambda, m_w_out, m_final_g, v_c_ctx, v_norm_g, v_w_ada, v_b_ada, v_w_in, v_w_conv_a, v_w_conv_b, v_b_conv_b, v_lru_wa, v_lru_ba, v_lru_wx, v_lru_bx, v_lru_lambda, v_w_out, v_final_g):
    _, l, d = x.shape
    lc = ctx.shape[1]
    w = d // 2
    t = lc
    assert l % t == 0 and t % GRID_W == 0 and t % 128 == 0
    dh = w // N_HEADS
    gc = min(w, MXU_WIDTH)
    cols = w_ada.shape[2]
    wo_rows = w_out.shape[1]
    me = _idx(_my_pos())
    x2, ctx2, tgt2 = x[0], ctx[0], loss_target[0]
    w_ada2, w_in2, w_out2 = w_ada[0], w_in[0], w_out[0]

    small_mine = jnp.concatenate([a.reshape(-1) for a in (w_conv_a, w_conv_b, lru_ba, lru_bx, lru_lambda)]
                                 + [jnp.zeros((3 * (w // NDEV),), F32)]).reshape(16, w // NDEV)
    mod_all, s_mat, small_all = _mod_forward(
        jnp.broadcast_to(c, (8, d)), jnp.broadcast_to(c_ctx[None], (8, d)), w_ada2, small_mine)
    mod = jnp.transpose(mod_all, (1, 0, 2)).reshape(16, NDEV * cols) + b_ada
    mod_lat = lax.dynamic_slice_in_dim(mod, me, 1, axis=0)
    sh_l, sc_l, gt_l = jnp.split(mod_lat, 3, axis=-1)
    sh_c, sc_c, _ = jnp.split(mod[8:9], 3, axis=-1)
    small = jnp.transpose(small_all, (1, 0, 2)).reshape(16, w)
    wca = _rows8(*[small[j] for j in range(0, 3)])
    wcb = _rows8(*[small[j] for j in range(3, 7)], b_conv_b)
    lv = _rows8(0.5 * small[7], 0.5 * small[9], small[11], 0.5 * small[8], 0.5 * small[10], small[12])
    wg = jnp.stack([
        jnp.concatenate([_blockdiag_groups(lru_wa[0, dr], gc), _blockdiag_groups(lru_wx[0, dr], gc)], axis=-1)
        for dr in range(2)])
    wg = (0.5 * wg).astype(BF16)

    la = l + lc
    tm = 2 * t if l % (2 * t) == 0 else t
    tk = 3 * t if la % (3 * t) == 0 else t
    h = _normalize(x2, _rows8(norm_g, sc_l, sh_l), la, 0, tm, "normalize")
    h = _normalize(ctx2, _rows8(norm_g, sc_c, sh_c), la, l, t, "normalize_ctx", prev=h)
    p, w_all, wo_all = _in_projection(h, w_in2.astype(BF16), w_out2.astype(BF16), la // 4 if la % 64 == 0 else tk)
    taps_m, back_m, perm = _scan_matrices(t)
    xb = _conv_input(p, wcb, taps_m, l, t)
    hf, hr = _lru_forward(xb, wg, lv, l, t)
    wo = wo_all.reshape(d, d)
    dn, cat, dout, part_mix = _mix_forward(x2, tgt2, p, hf, hr, wo, _rows8(gt_l, final_g), wca, perm, t)
    g_wout = _weight_grad_t(cat, dout, 2, 1, _largest_tile(l, 2048), "grad_w_out")
    dp, dhs, part_ca = _mix_backward(dout, p, hf, hr, wo, wca, perm, l, t)
    dxb0, dwg0, part_l0 = _lru_backward(0, xb, dhs, hf, wg, lv, l, t)
    dp, dwg1, part_l1 = _lru_backward(1, xb, dhs, hr, wg, lv, l, t, conv=(p, wcb, back_m, dxb0, dp))
    sc_win, sc_wout = _weight_grad_scatter(h, dp, g_wout.reshape(NDEV, wo_rows, d), tk, "grad_w_in")
    grad_x, part_lat = _input_backward(dp, w_all, x2, _rows8(norm_g, sc_l), 0, tm, 2, "input_backward", dn=dn)
    (part_ctx,) = _input_backward(dp, w_all, ctx2, _rows8(norm_g, sc_c), l, t, 2, "input_backward_ctx")
    part_in = jnp.concatenate([part_lat[0:2], part_ctx[0:2], (part_lat[2] + part_ctx[2])[None]], axis=0)

    dwa = jnp.stack([_blockdiag_extract(dwg0[:, :, :gc], dh), _blockdiag_extract(dwg1[:, :, :gc], dh)])
    dwx = jnp.stack([_blockdiag_extract(dwg0[:, :, gc:], dh), _blockdiag_extract(dwg1[:, :, gc:], dh)])
    lru_part = (0.5 * jnp.stack([dwa, dwx])).reshape(NDEV, -1, 128)
    zeros_d = jnp.zeros((d,), F32)
    pieces = [
        jnp.concatenate([part_in[0], part_in[1], part_mix[1]]),
        jnp.concatenate([part_in[2], part_in[3], zeros_d]),
        part_in[4], part_mix[0], part_ca[0:3], part_l1[4:8], part_l1[3],
        0.5 * jnp.stack([part_l0[0], part_l1[0]]), 0.5 * jnp.stack([part_l0[1], part_l1[1]]),
        jnp.stack([part_l0[2], part_l1[2]]), part_mix[2, 0:1],
    ]
    shapes = [(3 * d,), (3 * d,), (d,), (d,), (3, w), (4, w), (w,), (2, w), (2, w), (2, w), (1,)]
    sig_cc = jax.nn.sigmoid(c_ctx)
    dsilu_cc = jnp.broadcast_to((sig_cc * (1.0 + c_ctx * (1.0 - sig_cc)))[None], (8, d))
    psum, pall, lru_sum, g_cctx8 = _reduce_small(_pack(pieces), lru_part, w_ada2, dsilu_cc)
    (g_modl, g_modc, g_norm, g_final, g_ca, g_cb, g_bcb, g_ba, g_bx, g_lam, loss1) = _unpack(psum, shapes)
    loss = loss1[0]
    g_cctx = g_cctx8[0]
    g_bada = (g_modl + g_modc)[None]
    g_lru = lru_sum.reshape(2, 2, N_HEADS, dh, dh)
    g_wa, g_wx = g_lru[0][None], g_lru[1][None]
    wsl = w // NDEV
    mine = lambda a: lax.dynamic_slice_in_dim(a, me * wsl, wsl, axis=-1)
    g_ca_m, g_cb_m, g_ba_m, g_bx_m, g_lam_m = (mine(g_ca)[None], mine(g_cb)[None], mine(g_ba)[None],
                                               mine(g_bx)[None], mine(g_lam)[None])
    g_norm, g_bcb = g_norm[None], g_bcb[None]

    per_dev = pall[:, :3 * d // 128].reshape(NDEV, NDEV, cols)
    dmod_lat = lax.dynamic_slice_in_dim(per_dev, me, 1, axis=1)[:, 0]
    dmod_ctx = lax.dynamic_slice_in_dim(g_modc.reshape(NDEV, cols), me, 1, axis=0)
    dmod16 = jnp.concatenate([dmod_lat, dmod_ctx, jnp.zeros((7, cols), F32)], axis=0)
    tr_ada = 256 if d % 256 == 0 else d
    g_wada, d_wada, m_wada, v_wada = _adamw_ada(s_mat.T, dmod16, w_ada2, m_w_ada[0], v_w_ada[0], tr_ada)
    g_win2, d_win, m_win, v_win = _adamw_scattered(sc_win, w_in2, m_w_in[0], v_w_in[0], tr_ada)
    tr_out = 64 if wo_rows % 64 == 0 else wo_rows
    g_wout2, d_wout, m_wout, v_wout = _adamw_scattered(sc_wout, w_out2, m_w_out[0], v_w_out[0], tr_out)

    small_w = [c_ctx, norm_g, b_ada, w_conv_a, w_conv_b, b_conv_b, lru_wa, lru_ba, lru_wx, lru_bx, lru_lambda, final_g]
    small_m = [m_c_ctx, m_norm_g, m_b_ada, m_w_conv_a, m_w_conv_b, m_b_conv_b, m_lru_wa, m_lru_ba, m_lru_wx,
               m_lru_bx, m_lru_lambda, m_final_g]
    small_v = [v_c_ctx, v_norm_g, v_b_ada, v_w_conv_a, v_w_conv_b, v_b_conv_b, v_lru_wa, v_lru_ba, v_lru_wx,
               v_lru_bx, v_lru_lambda, v_final_g]
    small_g = [g_cctx, g_norm, g_bada, g_ca_m, g_cb_m, g_bcb, g_wa, g_ba_m, g_wx, g_bx_m, g_lam_m, g_final]
    small_g = [jnp.reshape(a, b.shape) for a, b in zip(small_g, small_w)]
    d_s, m_s, v_s = _adamw_small(small_g, small_w, small_m, small_v)

    def weights(small_list, ada, win, wout):
        (cctx_, norm_, bada_, ca_, cb_, bcb_, wa_, ba_, wx_, bx_, lam_, final_) = small_list
        return [cctx_, norm_, ada[None], bada_, win[None], ca_, cb_, bcb_, wa_, ba_, wx_, bx_, lam_, wout[None], final_]

    return (loss, grad_x[None],
            *weights(small_g, g_wada, g_win2, g_wout2), *weights(d_s, d_wada, d_win, d_wout),
            *weights(m_s, m_wada, m_win, m_wout), *weights(v_s, v_wada, v_win, v_wout))
```

```python
import functools

import jax
import jax.numpy as jnp
import numpy as np
from jax import lax
from jax.experimental import pallas as pl
from jax.experimental.pallas import tpu as pltpu

F32 = jnp.float32
BF16 = jnp.bfloat16
MESH = pl.DeviceIdType.MESH
NDEV = 8
GRID_W = 64
N_HEADS = 16
LRU_C = 8.0
EPS = 1e-6
MXU_WIDTH = 256
VMEM_LIMIT = 60 * 1024 * 1024

ADAM_LR = 0.001
ADAM_B1 = 0.9
ADAM_B2 = 0.999
ADAM_EPS = 1e-08
ADAM_WD = 0.01
ADAM_STEP = 10
ADAM_C1 = 1.0 - ADAM_B1 ** ADAM_STEP
ADAM_C2 = 1.0 - ADAM_B2 ** ADAM_STEP

HIGHEST = lax.Precision.HIGHEST
ANY = pl.BlockSpec(memory_space=pl.ANY)
VMEM = pl.BlockSpec(memory_space=pltpu.VMEM)


def _call(body, **kw):
    return pl.pallas_call(body, **kw)


def _params(sem=None, vmem=VMEM_LIMIT):
    return pltpu.CompilerParams(dimension_semantics=sem, vmem_limit_bytes=vmem)


def _my_pos():
    return lax.axis_index("x"), lax.axis_index("y"), lax.axis_index("c")


def _idx(pos):
    return 4 * pos[0] + 2 * pos[1] + pos[2]


def _peer(k):
    x, y, c = _my_pos()
    return ((1 - x) if (k >> 2) & 1 else x, (1 - y) if (k >> 1) & 1 else y, (1 - c) if k & 1 else c)


def _exchange_start(src_ref, dst_ref, send_sems, recv_sems, base):
    me = _idx(_my_pos())
    sends = []
    for k in range(1, NDEV):
        cp = pltpu.make_async_remote_copy(
            src_ref=src_ref, dst_ref=dst_ref.at[me], send_sem=send_sems.at[base + k - 1],
            recv_sem=recv_sems.at[base + k - 1], device_id=_peer(k), device_id_type=MESH)
        cp.start()
        sends.append(cp)
    dst_ref[me] = src_ref[...]
    return sends, (src_ref, dst_ref, send_sems, recv_sems, base)


def _exchange_finish(started):
    sends, (src_ref, dst_ref, send_sems, recv_sems, base) = started
    for k in range(1, NDEV):
        peer = _peer(k)
        pltpu.make_async_remote_copy(
            src_ref=src_ref, dst_ref=dst_ref.at[_idx(peer)], send_sem=send_sems.at[base + k - 1],
            recv_sem=recv_sems.at[base + k - 1], device_id=peer, device_id_type=MESH).wait_recv()
    for cp in sends:
        cp.wait_send()


def _exchange_vmem(src_ref, dst_ref, send_sems, recv_sems, base):
    _exchange_finish(_exchange_start(src_ref, dst_ref, send_sems, recv_sems, base))


def _sigmoid(z):
    return 0.5 * jnp.tanh(0.5 * z) + 0.5


def _softplus(x):
    return jnp.maximum(x, 0.0) + jnp.log1p(jnp.exp(-jnp.abs(x)))


def _one_minus_sq(a, la):
    series = (-2.0 * la) * (1.0 + la)
    return jnp.where(la > -0.0015, series, 1.0 - a * a)


def _dot(a, b):
    return jnp.dot(a, b, preferred_element_type=F32)


def _dot_nt(a, b):
    return lax.dot_general(a, b, (((1,), (1,)), ((), ())), preferred_element_type=F32)


def _rows(shape):
    return lax.broadcasted_iota(jnp.int32, shape, 0)


def _scan_matrices(t):
    seg = t // 8
    r = np.arange(t)
    perm = (np.arange(t)[None, :] == ((r % 8) * seg + r // 8)[:, None]).astype(np.float32)
    rows, cols = r[:, None], r[None, :]
    taps, back = [], []
    for rowlen in (GRID_W, t):
        pos = rows % rowlen
        shift = {-2: (cols == rows - 2) & (pos >= 2), -1: (cols == rows - 1) & (pos >= 1),
                 0: cols == rows, 1: (cols == rows + 1) & (pos + 1 < rowlen),
                 2: (cols == rows + 2) & (pos + 2 < rowlen)}
        if rowlen == GRID_W:
            beside = [shift[-1].astype(np.float32), shift[1].astype(np.float32)]
        taps.append(np.stack([perm @ shift[k].astype(np.float32) for k in (-2, -1, 0, 1)]))
        back.append(np.stack([shift[k].astype(np.float32) @ perm.T for k in (2, 1, 0, -1)]))
    as_bf16 = lambda a: jnp.asarray(a, dtype=BF16)
    return as_bf16(np.stack(taps)), as_bf16(np.stack(back)), as_bf16(np.stack([perm, perm.T] + beside))


def _chunk_scan(a, b, reverse):
    row = _rows(a.shape)
    for s in (1, 2, 4):
        if reverse:
            m = row < 8 - s
            sh = 8 - s
        else:
            m = row >= s
            sh = s
        a_s = jnp.where(m, pltpu.roll(a, sh, 0), 1.0)
        b_s = jnp.where(m, pltpu.roll(b, sh, 0), 0.0)
        b = b + a * b_s
        a = a * a_s
    return a, b


def _chain_segments(ptot, hend, carry, reverse):
    ca, cb = _chunk_scan(ptot, hend, reverse)
    incl = ca * carry + cb
    r8 = _rows(incl.shape)
    if reverse:
        start = jnp.where(r8 < 7, pltpu.roll(incl, 7, 0), carry)
        last = incl[0:1, :]
    else:
        start = jnp.where(r8 >= 1, pltpu.roll(incl, 1, 0), carry)
        last = incl[7:8, :]
    return start, jnp.broadcast_to(last, incl.shape)


def _blocks(nblock, reverse):
    order = range(nblock - 1, -1, -1) if reverse else range(nblock)
    return [slice(8 * k, 8 * k + 8) for k in order]


def _scan_tile(a_ref, b_ref, out_ref, carry, reverse):
    t, w = a_ref.shape
    seg = t // 8

    hend, ptot = jnp.zeros((8, w), F32), jnp.ones((8, w), F32)
    for rows in _blocks(seg, reverse):
        a = a_ref[rows, :]
        hend, ptot = a * hend + b_ref[rows, :], a * ptot
    h, new_carry = _chain_segments(ptot, hend, carry, reverse)
    for rows in _blocks(seg, reverse):
        h = a_ref[rows, :] * h + b_ref[rows, :]
        out_ref[rows, :] = h
    return new_carry


def _scan_tile_backward(a_ref, dh_ref, g_ref, carry, reverse):
    t, w = a_ref.shape
    seg = t // 8

    uend, ptot = jnp.zeros((8, w), F32), jnp.ones((8, w), F32)
    for rows in _blocks(seg, reverse):
        a = a_ref[rows, :]
        uend, ptot = a * (dh_ref[rows, :] + uend), a * ptot
    u, new_carry = _chain_segments(ptot, uend, carry, reverse)
    for rows in _blocks(seg, reverse):
        g = dh_ref[rows, :] + u
        g_ref[rows, :] = g
        u = a_ref[rows, :] * g
    return new_carry


def _lru_coef(xb, wg_ref, d, ba, bx, lam, gc):
    w = xb.shape[1]
    xb16 = xb.astype(BF16)
    zr, zi = [], []
    for g in range(w // gc):
        z = _dot(xb16[:, g * gc:(g + 1) * gc], wg_ref[d, g])
        zr.append(z[:, :gc])
        zi.append(z[:, gc:])
    zr = zr[0] if len(zr) == 1 else jnp.concatenate(zr, axis=-1)
    zi = zi[0] if len(zi) == 1 else jnp.concatenate(zi, axis=-1)
    tr = jnp.tanh(zr + ba)
    ti = jnp.tanh(zi + bx)
    sp = _softplus(-lam)
    half = -0.5 * LRU_C * sp
    la = tr * half + half
    a = jnp.exp(la)
    q = _one_minus_sq(a, la)
    rs = lax.rsqrt(jnp.maximum(q, 1e-30))
    return a, q * rs, rs, tr, ti, sp


def _adamw(w, g, m, v):
    m2 = ADAM_B1 * m + (1.0 - ADAM_B1) * g
    v2 = ADAM_B2 * v + (1.0 - ADAM_B2) * (g * g)
    m_hat = m2 / ADAM_C1
    v_hat = v2 / ADAM_C2
    delta = -ADAM_LR * (m_hat / (jnp.sqrt(v_hat) + ADAM_EPS) + ADAM_WD * w)
    return delta, m2, v2


def _mod_forward(c8, cctx8, w_ada, small):
    d = c8.shape[1]
    cols = w_ada.shape[1]

    def body(c_ref, cctx_ref, w_ref, sm_ref, mod_ref, s_ref, sm_all, cbuf, mod_my, send_sems, recv_sems):
        _exchange_vmem(sm_ref, sm_all, send_sems, recv_sems, 2 * (NDEV - 1))
        _exchange_vmem(c_ref, cbuf, send_sems, recv_sems, 0)
        row = _rows((8, d))
        c_all = jnp.zeros((8, d), F32)
        for b in range(NDEV):
            c_all = jnp.where(row == b, cbuf[b], c_all)
        cc = cctx_ref[...]
        s_top = c_all * _sigmoid(c_all)
        s_bot = jnp.where(row == 0, cc * _sigmoid(cc), 0.0)
        s = jnp.concatenate([s_top, s_bot], axis=0)
        s_ref[...] = s
        mod_my[...] = jnp.dot(s, w_ref[...], precision=HIGHEST, preferred_element_type=F32)
        _exchange_vmem(mod_my, mod_ref, send_sems, recv_sems, NDEV - 1)

    return _call(
        body, name="mod_forward",
        out_shape=(jax.ShapeDtypeStruct((NDEV, 16, cols), F32), jax.ShapeDtypeStruct((16, d), F32),
                   jax.ShapeDtypeStruct((NDEV,) + small.shape, F32)),
        in_specs=[VMEM] * 4, out_specs=(VMEM,) * 3,
        scratch_shapes=[pltpu.VMEM((NDEV, 8, d), F32), pltpu.VMEM((16, cols), F32),
                        pltpu.SemaphoreType.DMA((3 * (NDEV - 1),)), pltpu.SemaphoreType.DMA((3 * (NDEV - 1),))],
        compiler_params=_params(),
    )(c8, cctx8, w_ada, small)


def _scatter_copies(src_ref, dst_ref, send_sems, recv_sems):
    me = _idx(_my_pos())
    copies = [pltpu.make_async_copy(src_ref.at[me], dst_ref.at[0], send_sems.at[0])]
    for k in range(1, NDEV):
        peer = _peer(k)
        copies.append(pltpu.make_async_remote_copy(
            src_ref=src_ref.at[_idx(peer)], dst_ref=dst_ref.at[k], send_sem=send_sems.at[k],
            recv_sem=recv_sems.at[k], device_id=peer, device_id_type=MESH))
    return copies


def _gather_copies(src_ref, dst_ref, send_sems, recv_sems):
    me = _idx(_my_pos())
    sends = [pltpu.make_async_copy(src_ref, dst_ref.at[me], send_sems.at[0])]
    arrivals = []
    for k in range(1, NDEV):
        peer = _peer(k)
        sends.append(pltpu.make_async_remote_copy(
            src_ref=src_ref, dst_ref=dst_ref.at[me], send_sem=send_sems.at[k],
            recv_sem=recv_sems.at[k], device_id=peer, device_id_type=MESH))
        arrivals.append(pltpu.make_async_remote_copy(
            src_ref=src_ref, dst_ref=dst_ref.at[_idx(peer)], send_sem=send_sems.at[k],
            recv_sem=recv_sems.at[k], device_id=peer, device_id_type=MESH))
    return sends, arrivals


def _exchange_wait(sends, arrivals):
    sends[0].wait()
    for cp in arrivals:
        cp.wait_recv()
    for cp in sends[1:]:
        cp.wait_send()


def _chip_order(k, c):
    return (6, 4 - 2 * c, 2 + 2 * c, 0)[k]


def _scatter_order(s, c):
    k = s >> 1
    mine = jnp.where(k == 0, 6, jnp.where(k == 1, 4 - 2 * c, jnp.where(k == 2, 2 + 2 * c, 0)))
    theirs = jnp.where(k == 0, 6, jnp.where(k == 1, 2 + 2 * c, jnp.where(k == 2, 4 - 2 * c, 0))) ^ 1
    return jnp.where((s & 1) == 0, theirs, mine)


def _peer_at(dist):
    x, y, c = _my_pos()
    return (x ^ ((dist >> 2) & 1), y ^ ((dist >> 1) & 1), c ^ (dist & 1))


def _reduce_small(packed, lru_parts, w_ada, dsilu_cctx):
    rp = packed.shape[0]
    rl = lru_parts.shape[1]
    d, cols = w_ada.shape
    assert cols % 128 == 0
    cb = cols // 128

    def body(p_ref, l_ref, w_ref, ds_ref, sum_ref, all_ref, lru_ref, cctx_ref,
             lbuf, lsum, cpart, call, send_sems, recv_sems, lsend, lrecv):
        me = _idx(_my_pos())
        scattered = _scatter_copies(l_ref, lbuf, lsend, lrecv)
        for cp in scattered:
            cp.start()
        _exchange_vmem(p_ref, all_ref, send_sems, recv_sems, 0)
        acc = all_ref[0]
        for j in range(1, NDEV):
            acc = acc + all_ref[j]
        sum_ref[...] = acc
        _exchange_wait(scattered, scattered[1:])
        red = lbuf[0]
        for k in range(1, NDEV):
            red = red + lbuf[k]
        lsum[...] = red
        lru_gather = _exchange_start(lsum, lru_ref, send_sems, recv_sems, NDEV - 1)
        part = jnp.zeros((8, d), F32)
        for q in range(cb):
            dm = jnp.broadcast_to(sum_ref[pl.ds((NDEV + me) * cb + q, 1), :], (8, 128))
            part = part + lax.dot_general(dm, w_ref[:, q * 128:(q + 1) * 128],
                                          (((1,), (1,)), ((), ())), precision=HIGHEST,
                                          preferred_element_type=F32)
        cpart[...] = part
        _exchange_vmem(cpart, call, send_sems, recv_sems, 2 * (NDEV - 1))
        _exchange_finish(lru_gather)
        tot = call[0]
        for j in range(1, NDEV):
            tot = tot + call[j]
        cctx_ref[...] = tot * ds_ref[...]

    return _call(
        body, name="reduce_small",
        out_shape=(jax.ShapeDtypeStruct((rp, 128), F32), jax.ShapeDtypeStruct((NDEV, rp, 128), F32),
                   jax.ShapeDtypeStruct((NDEV, rl, 128), F32), jax.ShapeDtypeStruct((8, d), F32)),
        in_specs=[VMEM] * 4, out_specs=(VMEM,) * 4,
        scratch_shapes=[pltpu.VMEM((NDEV, rl, 128), F32), pltpu.VMEM((rl, 128), F32), pltpu.VMEM((8, d), F32),
                        pltpu.VMEM((NDEV, 8, d), F32),
                        pltpu.SemaphoreType.DMA((3 * (NDEV - 1),)), pltpu.SemaphoreType.DMA((3 * (NDEV - 1),)),
                        pltpu.SemaphoreType.DMA((NDEV,)), pltpu.SemaphoreType.DMA((NDEV,))],
        compiler_params=_params(),
    )(packed, lru_parts, w_ada, dsilu_cctx)


def _normalize(src, mv, la, row0, tm, name, prev=None):
    rows, d = src.shape
    blk0 = row0 // tm

    def body(*refs):
        x_ref, mv_ref, h_ref = refs[0], refs[1], refs[-1]
        xf = x_ref[...]
        r = lax.rsqrt(jnp.mean(xf * xf, axis=-1, keepdims=True) + EPS)
        h = xf * r * (mv_ref[0:1, :] * (1.0 + mv_ref[1:2, :])) + mv_ref[2:3, :]
        h_ref[...] = h.astype(BF16)

    in_specs = [pl.BlockSpec((tm, d), lambda i: (i, 0)), pl.BlockSpec((8, d), lambda i: (0, 0))]
    args = [src, mv]
    aliases = {}
    if prev is not None:
        in_specs += [ANY]
        args += [prev]
        aliases = {2: 0}
    return _call(
        body, name=name,
        grid=(rows // tm,),
        out_shape=jax.ShapeDtypeStruct((la, d), BF16),
        in_specs=in_specs,
        out_specs=pl.BlockSpec((tm, d), lambda i: (blk0 + i, 0)),
        input_output_aliases=aliases,
        compiler_params=_params(("arbitrary",)),
    )(*args)


def _gather_order(step):
    return (step & 1) | (((step >> 2) & 1) << 1) | (((step >> 1) & 1) << 2)


def _in_projection(h, w_shard, wo_shard, tm):
    la, d = h.shape
    bw = w_shard.shape[1]
    ni = la // tm
    where = jnp.reshape(_idx(_my_pos()), (1,)).astype(jnp.int32)

    def body(me_ref, h_ref, w_ref, wo_ref, p_ref, all_ref, wo_all, wbuf, send_sems, recv_sems, local_sems,
             wo_send, wo_recv):
        s, i = pl.program_id(0), pl.program_id(1)
        x, y, c = _my_pos()
        wo_sends, wo_arrivals = _gather_copies(wo_ref, wo_all, wo_send, wo_recv)

        @pl.when((s == NDEV // 2) & (i == 0))
        def _():
            for cp in wo_sends:
                cp.start()

        me, sibling = (x, y, c), (x, y, 1 - c)
        chips = [(1 - x, y), (x, 1 - y), (1 - x, 1 - y)]

        def copy(k, block, to, from_shard=False):
            return pltpu.make_async_remote_copy(
                src_ref=w_ref if from_shard else all_ref.at[_idx(block)], dst_ref=all_ref.at[_idx(block)],
                send_sem=send_sems.at[k], recv_sem=recv_sems.at[k], device_id=to, device_id_type=MESH)

        def load(block, slot):
            return pltpu.make_async_copy(all_ref.at[_idx(block)], wbuf.at[slot], local_sems.at[1])

        keep = pltpu.make_async_copy(w_ref, all_ref.at[_idx(me)], local_sems.at[0])
        first = [copy(0, me, sibling, True)] + [copy(1 + j, me, (*chip, c), True) for j, chip in enumerate(chips)]
        passed = [copy(4 + j, (*chip, c), sibling) for j, chip in enumerate(chips)]
        steps = [(copy(0, sibling, me), None, sibling)]
        for j, chip in enumerate(chips):
            steps.append((copy(1 + j, (*chip, c), me), passed[j], (*chip, c)))
            steps.append((copy(4 + j, (*chip, 1 - c), me), None, (*chip, 1 - c)))

        @pl.when((s == 0) & (i == 0))
        def _():
            keep.start()
            mine = pltpu.make_async_copy(w_ref, wbuf.at[0], local_sems.at[1])
            mine.start()
            for cp in first:
                cp.start()
            mine.wait()

        for n, (arrival, forward, block) in enumerate(steps, start=1):
            @pl.when((s == n - 1) & (i == ni - 1))
            def _(arrival=arrival, forward=forward, block=block, n=n):
                arrival.wait_recv()
                if forward is not None:
                    forward.start()
                load(block, n % 2).start()

        @pl.when((s > 0) & (i == 0))
        def _():
            load(me, s % 2).wait()

        p_ref[...] = _dot(h_ref[...], wbuf[s % 2]).astype(BF16)

        @pl.when((s == NDEV - 1) & (i == ni - 1))
        def _():
            for cp in first + passed:
                cp.wait_send()
            keep.wait()
            _exchange_wait(wo_sends, wo_arrivals)

    return _call(
        body, name="in_projection",
        grid_spec=pltpu.PrefetchScalarGridSpec(
            num_scalar_prefetch=1, grid=(NDEV, ni),
            in_specs=[pl.BlockSpec((tm, d), lambda s, i, me_ref: (i, 0)), ANY, ANY],
            out_specs=(pl.BlockSpec((tm, bw), lambda s, i, me_ref: (i, me_ref[0] ^ _gather_order(s))), ANY, ANY),
            scratch_shapes=[pltpu.VMEM((2, d, bw), BF16), pltpu.SemaphoreType.DMA((7,)),
                            pltpu.SemaphoreType.DMA((7,)), pltpu.SemaphoreType.DMA((2,)),
                            pltpu.SemaphoreType.DMA((NDEV,)), pltpu.SemaphoreType.DMA((NDEV,))]),
        out_shape=(jax.ShapeDtypeStruct((la, NDEV * bw), BF16), jax.ShapeDtypeStruct((NDEV, d, bw), BF16),
                   jax.ShapeDtypeStruct((NDEV,) + wo_shard.shape, wo_shard.dtype)),
        compiler_params=_params(("arbitrary", "arbitrary")),
    )(where, h, w_shard, wo_shard)


def _conv_input(p, wcb, taps_m, l, t):
    la = p.shape[0]
    w = wcb.shape[1]
    nt = l // t

    def body(v_ref, wcb_ref, tm_ref, xb_ref):
        taps = _dot(tm_ref[...].reshape(4 * t, t), v_ref[...])
        xb = wcb_ref[4:5, :] + wcb_ref[0:1, :] * taps[0:t]
        for j in range(1, 4):
            xb = xb + wcb_ref[j:j + 1, :] * taps[j * t:(j + 1) * t]
        xb_ref[...] = xb

    return _call(
        body, name="conv_input",
        grid=(nt + 1,),
        out_shape=jax.ShapeDtypeStruct((la, w), F32),
        in_specs=[pl.BlockSpec((t, w), lambda i: (i, 4)), pl.BlockSpec((8, w), lambda i: (0, 0)),
                  pl.BlockSpec((None, 4, t, t), lambda i: (i // nt, 0, 0, 0))],
        out_specs=pl.BlockSpec((t, w), lambda i: (i, 0)),
        compiler_params=_params(("arbitrary",)),
    )(p, wcb, taps_m)


def _lru_forward(xb, wg, lv, l, t):
    la, w = xb.shape
    gc = wg.shape[2]
    nt = l // t

    def body(xf_ref, xr_ref, wg_ref, lv_ref, hf_ref, hr_ref, a_s, b_s, carry):
        @pl.when(pl.program_id(0) == 0)
        def _():
            carry[...] = jnp.zeros_like(carry)

        for dr, (x_ref, h_ref) in enumerate(((xf_ref, hf_ref), (xr_ref, hr_ref))):
            x = x_ref[...]
            a, s, _, _, ti, _ = _lru_coef(x, wg_ref, dr, lv_ref[3 * dr:3 * dr + 1, :],
                                          lv_ref[3 * dr + 1:3 * dr + 2, :], lv_ref[3 * dr + 2:3 * dr + 3, :], gc)
            a_s[...] = a
            b_s[...] = (s * x) * (0.5 * ti + 0.5)
            carry[dr] = _scan_tile(a_s, b_s, h_ref, carry[dr], dr == 1)

    full = lambda shape: pl.BlockSpec(shape, lambda i: (0,) * len(shape))
    fmap = lambda i: (jnp.where(i == 0, nt, i - 1), 0)
    rmap = lambda i: (jnp.where(i == 0, nt, nt - i), 0)
    return _call(
        body, name="lru_forward",
        grid=(nt + 1,),
        out_shape=(jax.ShapeDtypeStruct((la, w), F32), jax.ShapeDtypeStruct((la, w), F32)),
        in_specs=[pl.BlockSpec((t, w), fmap), pl.BlockSpec((t, w), rmap), full(wg.shape), full(lv.shape)],
        out_specs=(pl.BlockSpec((t, w), fmap), pl.BlockSpec((t, w), rmap)),
        scratch_shapes=[pltpu.VMEM((t, w), F32), pltpu.VMEM((t, w), F32), pltpu.VMEM((2, 8, w), F32)],
        compiler_params=_params(("arbitrary",)),
    )(xb, xb, wg, lv)


def _mix_gates(p_refs, hf_ref, hr_ref, wca_ref, perm_ref, t, w):
    bl, cl, ul, gl, ql = [r[...].astype(F32) for r in p_refs]
    tt = cl * ul
    tt16 = tt.astype(BF16)
    beside = _dot(perm_ref[2:4].reshape(2 * t, t), tt16)
    before, after = beside[:t], beside[t:]
    z = wca_ref[0:1, :] * before + wca_ref[1:2, :] * tt + wca_ref[2:3, :] * after
    sig_g = _sigmoid(gl)
    sig_q = _sigmoid(ql)
    ylru = _dot(perm_ref[1], (hf_ref[...] + hr_ref[...]).astype(BF16))
    return bl, cl, ul, gl, ql, (before, tt, after), z, sig_g, sig_q, ylru


def _p_specs(t, w, nt):
    return [pl.BlockSpec((t, w), functools.partial(lambda i, s: (jnp.minimum(i, nt - 1), s), s=s))
            for s in (0, 1, 2, 3, 5)]


def _mix_forward(x, tgt, p, hf, hr, wo, ov, wca, perm, t):
    l, d = x.shape
    w = d // 2
    nt = l // t

    def body(x_ref, tg_ref, b_ref, c_ref, u_ref, g_ref, q_ref, hf_ref, hr_ref, wo_ref, ov_ref, wca_ref, perm_ref,
             dn_ref, ct_ref, do_ref, part_ref):
        i = pl.program_id(0)
        bl, _, _, gl, ql, _, z, sig_g, sig_q, ylru = _mix_gates(
            (b_ref, c_ref, u_ref, g_ref, q_ref), hf_ref, hr_ref, wca_ref, perm_ref, t, w)
        ya = bl * z * (gl * sig_g)
        yb = ylru * (ql * sig_q)
        ct_ref[:, 0:w] = ya.astype(BF16)
        ct_ref[:, w:] = yb.astype(BF16)
        out = _dot(ya.astype(BF16), wo_ref[0:w, :]) + _dot(yb.astype(BF16), wo_ref[w:, :])
        gate, fg = ov_ref[0:1, :], ov_ref[1:2, :]
        n = x_ref[...] + gate * out
        rr = lax.rsqrt(jnp.mean(n * n, axis=-1, keepdims=True) + EPS)
        nh = n * rr
        e = nh * fg - tg_ref[...]
        loss = 0.5 * jnp.sum(jnp.mean(e * e, axis=-1, keepdims=True), axis=0, keepdims=True)
        dy = e * (1.0 / d)
        dnh = dy * fg
        dn = rr * (dnh - nh * jnp.mean(dnh * nh, axis=-1, keepdims=True))
        dn_ref[...] = dn.astype(BF16)
        do_ref[...] = (dn * gate).astype(BF16)

        @pl.when(i == 0)
        def _():
            part_ref[...] = jnp.zeros_like(part_ref)

        part_ref[0:1, :] += jnp.sum(dy * nh, axis=0, keepdims=True)
        part_ref[1:2, :] += jnp.sum(dn * out, axis=0, keepdims=True)
        part_ref[2:3, :] += jnp.broadcast_to(loss, (1, d))

    tile = lambda cols: pl.BlockSpec((t, cols), lambda i: (i, 0))
    full = lambda shape: pl.BlockSpec(shape, lambda i: (0,) * len(shape))
    return _call(
        body, name="mix_forward",
        grid=(nt,),
        out_shape=(jax.ShapeDtypeStruct((l, d), BF16), jax.ShapeDtypeStruct((l, d), BF16),
                   jax.ShapeDtypeStruct((l, d), BF16), jax.ShapeDtypeStruct((8, d), F32)),
        in_specs=[tile(d), tile(d)] + _p_specs(t, w, nt) + [tile(w), tile(w),
                  pl.BlockSpec((d, d), lambda i: (0, 0), pipeline_mode=pl.Buffered(1)),
                  full(ov.shape), full(wca.shape), full(perm.shape)],
        out_specs=(tile(d), tile(d), tile(d), full((8, d))),
        compiler_params=_params(("arbitrary",)),
    )(x, tgt, p, p, p, p, p, hf, hr, wo, ov, wca, perm)


def _mix_backward(dout, p, hf, hr, wo, wca, perm, l, t):
    d = dout.shape[1]
    w = d // 2
    nt = l // t
    la = p.shape[0]

    def body(do_ref, b_ref, c_ref, u_ref, g_ref, q_ref, hf_ref, hr_ref, wo_ref, wca_ref, perm_ref,
             dp_ref, dh_ref, part_ref):
        i = pl.program_id(0)

        @pl.when(i == 0)
        def _():
            part_ref[...] = jnp.zeros_like(part_ref)

        @pl.when(i == nt)
        def _():
            dp_ref[...] = jnp.zeros_like(dp_ref)

        @pl.when(i < nt)
        def _():
            bl, cl, ul, gl, ql, taps, z, sig_g, sig_q, ylru = _mix_gates(
                (b_ref, c_ref, u_ref, g_ref, q_ref), hf_ref, hr_ref, wca_ref, perm_ref, t, w)
            do = do_ref[...]
            dya = _dot_nt(do, wo_ref[0:w, :])
            dyb = _dot_nt(do, wo_ref[w:, :])
            sg = gl * sig_g
            dz = dya * bl * sg
            dz16 = dz.astype(BF16)
            beside = _dot(perm_ref[2:4].reshape(2 * t, t), dz16)
            dt = wca_ref[0:1, :] * beside[t:] + wca_ref[1:2, :] * dz + wca_ref[2:3, :] * beside[:t]
            dp_ref[:, 0:w] = (dya * z * sg).astype(BF16)
            dp_ref[:, w:2 * w] = (dt * ul).astype(BF16)
            dp_ref[:, 2 * w:3 * w] = (dt * cl).astype(BF16)
            dp_ref[:, 3 * w:4 * w] = (dya * bl * z * (sig_g * (1.0 + gl * (1.0 - sig_g)))).astype(BF16)
            dp_ref[:, 4 * w:5 * w] = jnp.zeros((t, w), BF16)
            dp_ref[:, 5 * w:6 * w] = (dyb * ylru * (sig_q * (1.0 + ql * (1.0 - sig_q)))).astype(BF16)
            dh_ref[...] = _dot(perm_ref[0], (dyb * (ql * sig_q)).astype(BF16)).astype(BF16)
            for j in range(3):
                part_ref[j:j + 1, :] += jnp.sum(dz * taps[j], axis=0, keepdims=True)

    clamp = lambda cols: pl.BlockSpec((t, cols), lambda i: (jnp.minimum(i, nt - 1), 0))
    full = lambda shape: pl.BlockSpec(shape, lambda i: (0,) * len(shape))
    return _call(
        body, name="mix_backward",
        grid=(nt + 1,),
        out_shape=(jax.ShapeDtypeStruct((la, 6 * w), BF16), jax.ShapeDtypeStruct((l, w), BF16),
                   jax.ShapeDtypeStruct((8, w), F32)),
        in_specs=[clamp(d)] + _p_specs(t, w, nt) + [clamp(w), clamp(w),
                  pl.BlockSpec((d, d), lambda i: (0, 0), pipeline_mode=pl.Buffered(1)), full(wca.shape),
                  full(perm.shape)],
        out_specs=(pl.BlockSpec((t, 6 * w), lambda i: (i, 0)), clamp(w), full((8, w))),
        compiler_params=_params(("arbitrary",)),
    )(dout, p, p, p, p, p, hf, hr, wo, wca, perm)


def _lru_backward(direction, xb, dhs, hs, wg, lv, l, t, conv=None):
    la, w = hs.shape
    gc = wg.shape[2]
    ng = w // gc
    nt = l // t
    nblk8 = la // 8
    last = conv is not None
    assert last == (direction == 1)

    if direction == 0:
        tile = lambda i: jnp.where(i == nt, nt, nt - 1 - i)
        halo = lambda i: jnp.where(tile(i) == 0, nblk8 - 1, tile(i) * (t // 8) - 1)
    else:
        tile = lambda i: i
        halo = lambda i: jnp.minimum((i + 1) * (t // 8), nblk8 - 1)

    def body(*refs):
        x_ref, dh_ref, hs_ref, halo_ref, wg_ref, lv_ref = refs[:6]
        i = pl.program_id(0)
        is_ctx = i == nt
        if last:
            v_ref, wcb_ref, bm_ref, dxo_ref, _, gw_ref = refs[6:12]
            out_ref, dwg_ref, part_ref, sc_ref, a_s, dh_s, g_s, carry, send_sems, recv_sems = refs[12:]
            copies = _scatter_copies(gw_ref, sc_ref, send_sems, recv_sems)
        else:
            out_ref, dwg_ref, part_ref, a_s, dh_s, g_s, carry = refs[-7:]
            copies = []

        @pl.when(i == 0)
        def _():
            carry[...] = jnp.zeros_like(carry)
            dwg_ref[...] = jnp.zeros_like(dwg_ref)
            part_ref[...] = jnp.zeros_like(part_ref)
            for cp in copies:
                cp.start()

        if last:
            @pl.when(is_ctx)
            def _():
                _exchange_wait(copies, copies[1:])

        xb = x_ref[...]
        lam = lv_ref[3 * direction + 2:3 * direction + 3, :]
        a, s, rs, tr, ti, sp = _lru_coef(xb, wg_ref, direction, lv_ref[3 * direction:3 * direction + 1, :],
                                         lv_ref[3 * direction + 1:3 * direction + 2, :], lam, gc)
        hs_t = hs_ref[...]
        r8 = _rows((8, w))
        if direction == 0:
            edge = jnp.where(is_ctx, 0.0, halo_ref[7:8, :])
            first = jnp.where(r8 == 0, edge, pltpu.roll(hs_t[t - 8:, :], 1, 0))
            hprev = jnp.concatenate([first, hs_t[:t - 8, :]], axis=0)
        else:
            edge = jnp.where(is_ctx, 0.0, halo_ref[0:1, :])
            final = jnp.where(r8 == 7, edge, pltpu.roll(hs_t[:8, :], 7, 0))
            hprev = jnp.concatenate([hs_t[8:, :], final], axis=0)
        a_s[...] = a
        dh_s[...] = jnp.where(is_ctx, 0.0, dh_ref[...].astype(F32))
        carry[...] = _scan_tile_backward(a_s, dh_s, g_s, carry[...], direction == 0)

        g = g_s[...]
        r = 0.5 * tr + 0.5
        ig = 0.5 * ti + 0.5
        ix = ig * xb
        gs = g * s
        dla = (g * a) * (hprev - ix * (a * rs))
        dxb = gs * ig
        dzr = dla * (r * (1.0 - tr)) * (-LRU_C * sp)
        dzi = gs * ix * (1.0 - ti)
        part_ref[0:1, :] += jnp.sum(dzr, axis=0, keepdims=True)
        part_ref[1:2, :] += jnp.sum(dzi, axis=0, keepdims=True)
        part_ref[2:3, :] += jnp.sum(dla * r, axis=0, keepdims=True) * (LRU_C * _sigmoid(-lam))
        pieces = []
        for gi in range(ng):
            sl = slice(gi * gc, (gi + 1) * gc)
            dz = jnp.concatenate([dzr[:, sl], dzi[:, sl]], axis=-1).astype(BF16)
            pieces.append(_dot_nt(dz, wg_ref[direction, gi]))
            dwg_ref[gi] += _dot(xb[:, sl].T.astype(BF16), dz)
        dxb = dxb + (pieces[0] if ng == 1 else jnp.concatenate(pieces, axis=-1))
        if not last:
            out_ref[...] = dxb
        else:
            dxb = dxb + dxo_ref[...]
            v = v_ref[...].astype(F32)
            backs = _dot(bm_ref[...].reshape(4 * t, t), dxb.astype(BF16))
            dv = jnp.zeros((t, w), F32)
            for j in range(4):
                back = backs[j * t:(j + 1) * t]
                dv = dv + wcb_ref[j:j + 1, :] * back
                part_ref[4 + j:5 + j, :] += jnp.sum(back * v, axis=0, keepdims=True)
            out_ref[...] = dv.astype(BF16)
            part_ref[3:4, :] += jnp.sum(dxb, axis=0, keepdims=True)

    full = lambda shape: pl.BlockSpec(shape, lambda i: (0,) * len(shape))
    kind = lambda i: (jnp.where(i == nt, 1, 0), 0, 0, 0)
    in_specs = [pl.BlockSpec((t, w), lambda i: (tile(i), 0)),
                pl.BlockSpec((t, w), lambda i: (jnp.minimum(tile(i), nt - 1), 0)),
                pl.BlockSpec((t, w), lambda i: (tile(i), 0)),
                pl.BlockSpec((8, w), lambda i: (halo(i), 0)),
                full(wg.shape), full(lv.shape)]
    args = [xb, dhs, hs, hs, wg, lv]
    more_out, more_spec, more_scratch = (), (), []
    if last:
        p, wcb, back_m, dxb_other, dp, g_wout = conv
        in_specs += [pl.BlockSpec((t, w), lambda i: (tile(i), 4)), full(wcb.shape),
                     pl.BlockSpec((None, 4, t, t), kind), pl.BlockSpec((t, w), lambda i: (tile(i), 0)), ANY, ANY]
        args += [p, wcb, back_m, dxb_other, dp, g_wout]
        out0 = jax.ShapeDtypeStruct(dp.shape, dp.dtype)
        spec0 = pl.BlockSpec((t, w), lambda i: (tile(i), 4))
        more_out, more_spec = (jax.ShapeDtypeStruct(g_wout.shape, g_wout.dtype),), (ANY,)
        more_scratch = [pltpu.SemaphoreType.DMA((NDEV,)), pltpu.SemaphoreType.DMA((NDEV,))]
        aliases = {10: 0}
    else:
        out0 = jax.ShapeDtypeStruct((la, w), F32)
        spec0 = pl.BlockSpec((t, w), lambda i: (tile(i), 0))
        aliases = {}
    return _call(
        body, name="lru_backward_%d" % direction,
        grid=(nt + 1,),
        out_shape=(out0, jax.ShapeDtypeStruct((ng, gc, 2 * gc), F32), jax.ShapeDtypeStruct((8, w), F32)) + more_out,
        in_specs=in_specs,
        out_specs=(spec0, full((ng, gc, 2 * gc)), full((8, w))) + more_spec,
        scratch_shapes=[pltpu.VMEM((t, w), F32), pltpu.VMEM((t, w), F32), pltpu.VMEM((t, w), F32),
                        pltpu.VMEM((8, w), F32)] + more_scratch,
        input_output_aliases=aliases,
        compiler_params=_params(("arbitrary",)),
    )(*args)


def _weight_grad_t(a, b, nblk_m, nblk_n, tk, name):
    k, m = a.shape
    n = b.shape[1]
    bm, bn = m // nblk_m, n // nblk_n
    nk = k // tk

    def body(a_ref, b_ref, o_ref, acc):
        kk = pl.program_id(2)

        @pl.when(kk == 0)
        def _():
            acc[...] = jnp.zeros_like(acc)

        acc[...] += lax.dot_general(a_ref[...], b_ref[...], (((0,), (0,)), ((), ())), preferred_element_type=F32)

        @pl.when(kk == nk - 1)
        def _():
            o_ref[...] = acc[...].astype(BF16)

    return _call(
        body, name=name,
        grid=(nblk_m, nblk_n, nk),
        out_shape=jax.ShapeDtypeStruct((nblk_m * nblk_n, bm, bn), BF16),
        in_specs=[pl.BlockSpec((tk, bm), lambda i, j, kk: (kk, i)),
                  pl.BlockSpec((tk, bn), lambda i, j, kk: (kk, j))],
        out_specs=pl.BlockSpec((None, bm, bn), lambda i, j, kk: (i * nblk_n + j, 0, 0)),
        scratch_shapes=[pltpu.VMEM((bm, bn), F32)],
        compiler_params=_params(("arbitrary", "arbitrary", "arbitrary")),
    )(a, b)


def _weight_grad_scatter(at, b, tk, name):
    k, m = at.shape
    n = b.shape[1]
    bn = n // NDEV
    nk = k // tk
    where = jnp.stack([_idx(_my_pos()), lax.axis_index("c")]).astype(jnp.int32)
    tn = (((0,), (0,)), ((), ()))

    def body(w_ref, a_ref, b_ref, recv_ref, acc, sbuf, sib, sib_send, sib_recv, chip_send, chip_recv, keep_sem):
        s, kk = pl.program_id(0), pl.program_id(1)
        x, y, c = _my_pos()

        @pl.when(kk == 0)
        def _():
            acc[...] = lax.dot_general(a_ref[...], b_ref[...], tn, preferred_element_type=F32)

        @pl.when(kk > 0)
        def _():
            acc[...] += lax.dot_general(a_ref[...], b_ref[...], tn, preferred_element_type=F32)

        def to_sibling(j):
            return pltpu.make_async_remote_copy(
                src_ref=sbuf.at[0], dst_ref=sib.at[j], send_sem=sib_send.at[j], recv_sem=sib_recv.at[j],
                device_id=(x, y, 1 - c), device_id_type=MESH)

        def to_chip(j):
            dist = _chip_order(j, c)
            return pltpu.make_async_remote_copy(
                src_ref=sbuf.at[1], dst_ref=recv_ref.at[dist // 2], send_sem=chip_send.at[j],
                recv_sem=chip_recv.at[dist // 2], device_id=_peer_at(dist), device_id_type=MESH)

        keep = pltpu.make_async_copy(sbuf.at[1], recv_ref.at[0], keep_sem)
        sends = []
        for j in range(4):
            sends += [to_sibling(j), to_chip(j) if j < 3 else keep]

        for st in range(NDEV):
            @pl.when((kk == nk - 1) & (s == st))
            def _(st=st):
                if st >= 2:
                    sends[st - 2].wait_send()
                part = acc[...]
                if st % 2 == 1:
                    to_sibling(st // 2).wait_recv()
                    part = part + sib[st // 2].astype(F32)
                sbuf[st % 2] = part.astype(BF16)
                sends[st].start()
                if st == NDEV - 1:
                    sends[st - 1].wait_send()
                    sends[st].wait()
                    for j in range(1, 4):
                        pltpu.make_async_remote_copy(
                            src_ref=sbuf.at[0], dst_ref=recv_ref.at[j], send_sem=chip_send.at[0],
                            recv_sem=chip_recv.at[j], device_id=_peer_at(2 * j), device_id_type=MESH).wait_recv()

    blk = lambda s, w_ref: w_ref[0] ^ _scatter_order(s, w_ref[1])
    return _call(
        body, name=name,
        grid_spec=pltpu.PrefetchScalarGridSpec(
            num_scalar_prefetch=1, grid=(NDEV, nk),
            in_specs=[pl.BlockSpec((tk, m), lambda s, kk, w_ref: (kk, 0)),
                      pl.BlockSpec((tk, bn), lambda s, kk, w_ref: (kk, blk(s, w_ref)))],
            out_specs=ANY,
            scratch_shapes=[pltpu.VMEM((m, bn), F32), pltpu.VMEM((2, m, bn), BF16), pltpu.VMEM((4, m, bn), BF16),
                            pltpu.SemaphoreType.DMA((4,)), pltpu.SemaphoreType.DMA((4,)),
                            pltpu.SemaphoreType.DMA((4,)), pltpu.SemaphoreType.DMA((4,)),
                            pltpu.SemaphoreType.DMA]),
        out_shape=jax.ShapeDtypeStruct((4, m, bn), BF16),
        compiler_params=_params(("arbitrary", "arbitrary")),
    )(where, at, b)


def _input_backward(dp, w_all, src, mv, row0, tm, nbk, name, dn=None):
    rows, d = src.shape
    nb, _, bw = w_all.shape
    nk = nb // nbk
    ni = rows // tm
    blk0 = row0 // tm
    latent = dn is not None

    def body(*refs):
        dp_ref, w_ref, x_ref, mv_ref = refs[:4]
        outs = refs[4 + latent:]
        part_ref, acc = outs[latent], outs[latent + 1]
        i, k = pl.program_id(0), pl.program_id(1)

        def product():
            step = _dot_nt(dp_ref[:, 0:bw], w_ref[0])
            for q in range(1, nbk):
                step = step + _dot_nt(dp_ref[:, q * bw:(q + 1) * bw], w_ref[q])
            return step

        def finish(slot):
            xf = x_ref[...]
            r = lax.rsqrt(jnp.mean(xf * xf, axis=-1, keepdims=True) + EPS)
            xn = xf * r
            dhl = acc[slot]
            gain, sc = mv_ref[0:1, :], mv_ref[1:2, :]
            dhx = jnp.sum(dhl * xn, axis=0, keepdims=True)
            part_ref[0:1, :] += jnp.sum(dhl, axis=0, keepdims=True)
            part_ref[1:2, :] += dhx * gain
            part_ref[2:3, :] += dhx * (1.0 + sc)
            if latent:
                dxn = dhl * (gain * (1.0 + sc))
                outs[0][...] = (refs[4][...].astype(F32)
                                + r * (dxn - xn * jnp.mean(dxn * xn, axis=-1, keepdims=True)))

        @pl.when((i == 0) & (k == 0))
        def _():
            part_ref[...] = jnp.zeros_like(part_ref)
            acc[0] = product()

        @pl.when((i > 0) & (i < ni) & (k == 0))
        def _():
            acc[i % 2] = product()
            finish((i - 1) % 2)

        @pl.when((i == ni) & (k == 0))
        def _():
            finish((ni - 1) % 2)

        @pl.when((i < ni) & (k > 0))
        def _():
            acc[i % 2] += product()

    tile = pl.BlockSpec((tm, d), lambda i, k: (jnp.maximum(i - 1, 0), 0))
    vec = pl.BlockSpec((8, d), lambda i, k: (0, 0))
    kblock = lambda i, k: jnp.where(i == ni, nk - 1, k)
    return _call(
        body, name=name,
        grid=(ni + 1, nk),
        out_shape=((jax.ShapeDtypeStruct((rows, d), F32),) if latent else ()) + (jax.ShapeDtypeStruct((8, d), F32),),
        in_specs=[pl.BlockSpec((tm, nbk * bw), lambda i, k: (blk0 + jnp.minimum(i, ni - 1), kblock(i, k))),
                  pl.BlockSpec((nbk, d, bw), lambda i, k: (kblock(i, k), 0, 0)), tile, vec]
                 + ([tile] if latent else []),
        out_specs=((tile,) if latent else ()) + (vec,),
        scratch_shapes=[pltpu.VMEM((2, tm, d), F32)],
        compiler_params=_params(("arbitrary", "arbitrary")),
    )(*([dp, w_all, src, mv] + ([dn] if latent else [])))


def _adamw_scattered(parts, w, m, v, tr):
    r, c = w.shape
    nslot = parts.shape[0]

    def body(p_ref, w_ref, m_ref, v_ref, g_ref, d_ref, m2_ref, v2_ref):
        g = p_ref[0].astype(F32)
        for k in range(1, nslot):
            g = g + p_ref[k].astype(F32)
        g_ref[...] = g
        d_ref[...], m2_ref[...], v2_ref[...] = _adamw(w_ref[...], g, m_ref[...], v_ref[...])

    tile = pl.BlockSpec((tr, c), lambda i: (i, 0))
    return _call(
        body, name="adamw_scattered_%dx%d" % (r, c),
        grid=(r // tr,),
        out_shape=tuple(jax.ShapeDtypeStruct((r, c), F32) for _ in range(4)),
        in_specs=[pl.BlockSpec((nslot, tr, c), lambda i: (0, i, 0)), tile, tile, tile],
        out_specs=(tile,) * 4,
        compiler_params=_params(("arbitrary",)),
    )(parts, w, m, v)


def _adamw_ada(st, dmod, w, m, v, tr):
    r, c = w.shape

    def body(s_ref, dm_ref, w_ref, m_ref, v_ref, g_ref, d_ref, m2_ref, v2_ref):
        g = jnp.dot(s_ref[...], dm_ref[...], precision=HIGHEST, preferred_element_type=F32)
        g_ref[...] = g
        d_ref[...], m2_ref[...], v2_ref[...] = _adamw(w_ref[...], g, m_ref[...], v_ref[...])

    tile = pl.BlockSpec((tr, c), lambda i: (i, 0))
    return _call(
        body, name="adamw_ada",
        grid=(r // tr,),
        out_shape=tuple(jax.ShapeDtypeStruct((r, c), F32) for _ in range(4)),
        in_specs=[pl.BlockSpec((tr, 16), lambda i: (i, 0)), pl.BlockSpec((16, c), lambda i: (0, 0)),
                  tile, tile, tile],
        out_specs=(tile,) * 4,
        compiler_params=_params(("arbitrary",)),
    )(st, dmod, w, m, v)


def _adamw_small(gs, ws, ms, vs):
    n = len(ws)

    def body(*refs):
        for j in range(n):
            g_ref, w_ref, m_ref, v_ref = refs[j], refs[n + j], refs[2 * n + j], refs[3 * n + j]
            d_ref, m2_ref, v2_ref = refs[4 * n + j], refs[5 * n + j], refs[6 * n + j]
            d_ref[...], m2_ref[...], v2_ref[...] = _adamw(w_ref[...], g_ref[...], m_ref[...], v_ref[...])

    shapes = tuple(jax.ShapeDtypeStruct(a.shape, F32) for a in ws)
    out = _call(
        body, name="adamw_small",
        out_shape=shapes * 3,
        in_specs=[VMEM] * (4 * n), out_specs=(VMEM,) * (3 * n),
        compiler_params=_params(),
    )(*gs, *ws, *ms, *vs)
    return list(out[:n]), list(out[n:2 * n]), list(out[2 * n:])


def _blockdiag_groups(wh, gc):
    h, dh, _ = wh.shape
    g = gc // dh
    w4 = wh.reshape(h // g, g, dh, dh)
    bd = jnp.einsum("ngij,gh->ngihj", w4, jnp.eye(g, dtype=wh.dtype))
    return bd.reshape(h // g, gc, gc)


def _blockdiag_extract(bd, dh):
    ng, gc, _ = bd.shape
    g = gc // dh
    x = bd.reshape(ng, g, dh, g, dh)
    return jnp.einsum("ngihj,gh->ngij", x, jnp.eye(g, dtype=bd.dtype)).reshape(ng * g, dh, dh)


def _largest_tile(n, cap):
    return max(q for q in range(128, min(n, cap) + 1, 128) if n % q == 0)


def _rows8(*vecs):
    rows = [jnp.reshape(v, (1, -1)).astype(F32) for v in vecs]
    n = rows[0].shape[1]
    return jnp.concatenate(rows + [jnp.zeros((8 - len(rows), n), F32)], axis=0)


def _pack(pieces):
    flat = jnp.concatenate([jnp.reshape(a, (-1,)).astype(F32) for a in pieces])
    total = -(-flat.shape[0] // 1024) * 1024
    return jnp.pad(flat, (0, total - flat.shape[0])).reshape(total // 128, 128)


def _unpack(packed, shapes):
    flat = packed.reshape(-1)
    out, off = [], 0
    for s in shapes:
        n = 1
        for q in s:
            n *= q
        out.append(flat[off:off + n].reshape(s))
        off += n
    return out


def kernel(x, c, ctx, c_ctx, norm_g, w_ada, b_ada, w_in, w_conv_a, w_conv_b, b_conv_b, lru_wa, lru_ba, lru_wx, lru_bx, lru_lambda, w_out, final_g, loss_target, m_c_ctx, m_norm_g, m_w_ada, m_b_ada, m_w_in, m_w_conv_a, m_w_conv_b, m_b_conv_b, m_lru_wa, m_lru_ba, m_lru_wx, m_lru_bx, m_lru_lambda, m_w_out, m_final_g, v_c_ctx, v_norm_g, v_w_ada, v_b_ada, v_w_in, v_w_conv_a, v_w_conv_b, v_b_conv_b, v_lru_wa, v_lru_ba, v_lru_wx, v_lru_bx, v_lru_lambda, v_w_out, v_final_g):
    _, l, d = x.shape
    lc = ctx.shape[1]
    w = d // 2
    t = lc
    assert l % t == 0 and t % GRID_W == 0 and t % 128 == 0
    dh = w // N_HEADS
    gc = min(w, MXU_WIDTH)
    cols = w_ada.shape[2]
    wo_rows = w_out.shape[1]
    me = _idx(_my_pos())
    x2, ctx2, tgt2 = x[0], ctx[0], loss_target[0]
    w_ada2, w_in2, w_out2 = w_ada[0], w_in[0], w_out[0]

    small_mine = jnp.concatenate([a.reshape(-1) for a in (w_conv_a, w_conv_b, lru_ba, lru_bx, lru_lambda)]
                                 + [jnp.zeros((3 * (w // NDEV),), F32)]).reshape(16, w // NDEV)
    mod_all, s_mat, small_all = _mod_forward(
        jnp.broadcast_to(c, (8, d)), jnp.broadcast_to(c_ctx[None], (8, d)), w_ada2, small_mine)
    mod = jnp.transpose(mod_all, (1, 0, 2)).reshape(16, NDEV * cols) + b_ada
    mod_lat = lax.dynamic_slice_in_dim(mod, me, 1, axis=0)
    sh_l, sc_l, gt_l = jnp.split(mod_lat, 3, axis=-1)
    sh_c, sc_c, _ = jnp.split(mod[8:9], 3, axis=-1)
    small = jnp.transpose(small_all, (1, 0, 2)).reshape(16, w)
    wca = _rows8(*[small[j] for j in range(0, 3)])
    wcb = _rows8(*[small[j] for j in range(3, 7)], b_conv_b)
    lv = _rows8(0.5 * small[7], 0.5 * small[9], small[11], 0.5 * small[8], 0.5 * small[10], small[12])
    wg = jnp.stack([
        jnp.concatenate([_blockdiag_groups(lru_wa[0, dr], gc), _blockdiag_groups(lru_wx[0, dr], gc)], axis=-1)
        for dr in range(2)])
    wg = (0.5 * wg).astype(BF16)

    la = l + lc
    tm = 2 * t if l % (2 * t) == 0 else t
    tk = 3 * t if la % (3 * t) == 0 else t
    h = _normalize(x2, _rows8(norm_g, sc_l, sh_l), la, 0, tm, "normalize")
    h = _normalize(ctx2, _rows8(norm_g, sc_c, sh_c), la, l, t, "normalize_ctx", prev=h)
    p, w_all, wo_all = _in_projection(h, w_in2.astype(BF16), w_out2.astype(BF16), la // 4 if la % 64 == 0 else tk)
    taps_m, back_m, perm = _scan_matrices(t)
    xb = _conv_input(p, wcb, taps_m, l, t)
    hf, hr = _lru_forward(xb, wg, lv, l, t)
    wo = wo_all.reshape(d, d)
    dn, cat, dout, part_mix = _mix_forward(x2, tgt2, p, hf, hr, wo, _rows8(gt_l, final_g), wca, perm, t)
    g_wout = _weight_grad_t(cat, dout, 2, 1, _largest_tile(l, 2048), "grad_w_out")
    dp, dhs, part_ca = _mix_backward(dout, p, hf, hr, wo, wca, perm, l, t)
    dxb0, dwg0, part_l0 = _lru_backward(0, xb, dhs, hf, wg, lv, l, t)
    dp, dwg1, part_l1, sc_wout = _lru_backward(
        1, xb, dhs, hr, wg, lv, l, t, conv=(p, wcb, back_m, dxb0, dp, g_wout.reshape(NDEV, wo_rows, d)))
    sc_win = _weight_grad_scatter(h, dp, tk, "grad_w_in")
    grad_x, part_lat = _input_backward(dp, w_all, x2, _rows8(norm_g, sc_l), 0, tm, 2, "input_backward", dn=dn)
    (part_ctx,) = _input_backward(dp, w_all, ctx2, _rows8(norm_g, sc_c), l, t, 2, "input_backward_ctx")
    part_in = jnp.concatenate([part_lat[0:2], part_ctx[0:2], (part_lat[2] + part_ctx[2])[None]], axis=0)

    dwa = jnp.stack([_blockdiag_extract(dwg0[:, :, :gc], dh), _blockdiag_extract(dwg1[:, :, :gc], dh)])
    dwx = jnp.stack([_blockdiag_extract(dwg0[:, :, gc:], dh), _blockdiag_extract(dwg1[:, :, gc:], dh)])
    lru_part = (0.5 * jnp.stack([dwa, dwx])).reshape(NDEV, -1, 128)
    zeros_d = jnp.zeros((d,), F32)
    pieces = [
        jnp.concatenate([part_in[0], part_in[1], part_mix[1]]),
        jnp.concatenate([part_in[2], part_in[3], zeros_d]),
        part_in[4], part_mix[0], part_ca[0:3], part_l1[4:8], part_l1[3],
        0.5 * jnp.stack([part_l0[0], part_l1[0]]), 0.5 * jnp.stack([part_l0[1], part_l1[1]]),
        jnp.stack([part_l0[2], part_l1[2]]), part_mix[2, 0:1],
    ]
    shapes = [(3 * d,), (3 * d,), (d,), (d,), (3, w), (4, w), (w,), (2, w), (2, w), (2, w), (1,)]
    sig_cc = jax.nn.sigmoid(c_ctx)
    dsilu_cc = jnp.broadcast_to((sig_cc * (1.0 + c_ctx * (1.0 - sig_cc)))[None], (8, d))
    psum, pall, lru_sum, g_cctx8 = _reduce_small(_pack(pieces), lru_part, w_ada2, dsilu_cc)
    (g_modl, g_modc, g_norm, g_final, g_ca, g_cb, g_bcb, g_ba, g_bx, g_lam, loss1) = _unpack(psum, shapes)
    loss = loss1[0]
    g_cctx = g_cctx8[0]
    g_bada = (g_modl + g_modc)[None]
    g_lru = lru_sum.reshape(2, 2, N_HEADS, dh, dh)
    g_wa, g_wx = g_lru[0][None], g_lru[1][None]
    wsl = w // NDEV
    mine = lambda a: lax.dynamic_slice_in_dim(a, me * wsl, wsl, axis=-1)
    g_ca_m, g_cb_m, g_ba_m, g_bx_m, g_lam_m = (mine(g_ca)[None], mine(g_cb)[None], mine(g_ba)[None],
                                               mine(g_bx)[None], mine(g_lam)[None])
    g_norm, g_bcb = g_norm[None], g_bcb[None]

    per_dev = pall[:, :3 * d // 128].reshape(NDEV, NDEV, cols)
    dmod_lat = lax.dynamic_slice_in_dim(per_dev, me, 1, axis=1)[:, 0]
    dmod_ctx = lax.dynamic_slice_in_dim(g_modc.reshape(NDEV, cols), me, 1, axis=0)
    dmod16 = jnp.concatenate([dmod_lat, dmod_ctx, jnp.zeros((7, cols), F32)], axis=0)
    tr_ada = 256 if d % 256 == 0 else d
    g_wada, d_wada, m_wada, v_wada = _adamw_ada(s_mat.T, dmod16, w_ada2, m_w_ada[0], v_w_ada[0], tr_ada)
    g_win2, d_win, m_win, v_win = _adamw_scattered(sc_win, w_in2, m_w_in[0], v_w_in[0], tr_ada)
    tr_out = 64 if wo_rows % 64 == 0 else wo_rows
    g_wout2, d_wout, m_wout, v_wout = _adamw_scattered(sc_wout, w_out2, m_w_out[0], v_w_out[0], tr_out)

    small_w = [c_ctx, norm_g, b_ada, w_conv_a, w_conv_b, b_conv_b, lru_wa, lru_ba, lru_wx, lru_bx, lru_lambda, final_g]
    small_m = [m_c_ctx, m_norm_g, m_b_ada, m_w_conv_a, m_w_conv_b, m_b_conv_b, m_lru_wa, m_lru_ba, m_lru_wx,
               m_lru_bx, m_lru_lambda, m_final_g]
    small_v = [v_c_ctx, v_norm_g, v_b_ada, v_w_conv_a, v_w_conv_b, v_b_conv_b, v_lru_wa, v_lru_ba, v_lru_wx,
               v_lru_bx, v_lru_lambda, v_final_g]
    small_g = [g_cctx, g_norm, g_bada, g_ca_m, g_cb_m, g_bcb, g_wa, g_ba_m, g_wx, g_bx_m, g_lam_m, g_final]
    small_g = [jnp.reshape(a, b.shape) for a, b in zip(small_g, small_w)]
    d_s, m_s, v_s = _adamw_small(small_g, small_w, small_m, small_v)

    def weights(small_list, ada, win, wout):
        (cctx_, norm_, bada_, ca_, cb_, bcb_, wa_, ba_, wx_, bx_, lam_, final_) = small_list
        return [cctx_, norm_, ada[None], bada_, win[None], ca_, cb_, bcb_, wa_, ba_, wx_, bx_, lam_, wout[None], final_]

    return (loss, grad_x[None],
            *weights(small_g, g_wada, g_win2, g_wout2), *weights(d_s, d_wada, d_win, d_wout),
            *weights(m_s, m_wada, m_win, m_wout), *weights(v_s, v_wada, v_win, v_wout))
```

```python
import functools

import jax
import jax.numpy as jnp
import numpy as np
from jax import lax
from jax.experimental import pallas as pl
from jax.experimental.pallas import tpu as pltpu

F32 = jnp.float32
BF16 = jnp.bfloat16
MESH = pl.DeviceIdType.MESH
NDEV = 8
GRID_W = 64
N_HEADS = 16
LRU_C = 8.0
EPS = 1e-6
MXU_WIDTH = 256
VMEM_LIMIT = 60 * 1024 * 1024

ADAM_LR = 0.001
ADAM_B1 = 0.9
ADAM_B2 = 0.999
ADAM_EPS = 1e-08
ADAM_WD = 0.01
ADAM_STEP = 10
ADAM_C1 = 1.0 - ADAM_B1 ** ADAM_STEP
ADAM_C2 = 1.0 - ADAM_B2 ** ADAM_STEP

HIGHEST = lax.Precision.HIGHEST
ANY = pl.BlockSpec(memory_space=pl.ANY)
VMEM = pl.BlockSpec(memory_space=pltpu.VMEM)


def _call(body, **kw):
    return pl.pallas_call(body, **kw)


def _params(sem=None, vmem=VMEM_LIMIT):
    return pltpu.CompilerParams(dimension_semantics=sem, vmem_limit_bytes=vmem)


def _my_pos():
    return lax.axis_index("x"), lax.axis_index("y"), lax.axis_index("c")


def _idx(pos):
    return 4 * pos[0] + 2 * pos[1] + pos[2]


def _peer(k):
    x, y, c = _my_pos()
    return ((1 - x) if (k >> 2) & 1 else x, (1 - y) if (k >> 1) & 1 else y, (1 - c) if k & 1 else c)


def _exchange_start(src_ref, dst_ref, send_sems, recv_sems, base):
    me = _idx(_my_pos())
    sends = []
    for k in range(1, NDEV):
        cp = pltpu.make_async_remote_copy(
            src_ref=src_ref, dst_ref=dst_ref.at[me], send_sem=send_sems.at[base + k - 1],
            recv_sem=recv_sems.at[base + k - 1], device_id=_peer(k), device_id_type=MESH)
        cp.start()
        sends.append(cp)
    dst_ref[me] = src_ref[...]
    return sends, (src_ref, dst_ref, send_sems, recv_sems, base)


def _exchange_finish(started):
    sends, (src_ref, dst_ref, send_sems, recv_sems, base) = started
    for k in range(1, NDEV):
        peer = _peer(k)
        pltpu.make_async_remote_copy(
            src_ref=src_ref, dst_ref=dst_ref.at[_idx(peer)], send_sem=send_sems.at[base + k - 1],
            recv_sem=recv_sems.at[base + k - 1], device_id=peer, device_id_type=MESH).wait_recv()
    for cp in sends:
        cp.wait_send()


def _exchange_vmem(src_ref, dst_ref, send_sems, recv_sems, base):
    _exchange_finish(_exchange_start(src_ref, dst_ref, send_sems, recv_sems, base))


def _sigmoid(z):
    return 0.5 * jnp.tanh(0.5 * z) + 0.5


def _softplus(x):
    return jnp.maximum(x, 0.0) + jnp.log1p(jnp.exp(-jnp.abs(x)))


def _one_minus_sq(a, la):
    series = (-2.0 * la) * (1.0 + la)
    return jnp.where(la > -0.0015, series, 1.0 - a * a)


def _dot(a, b):
    return jnp.dot(a, b, preferred_element_type=F32)


def _dot_nt(a, b):
    return lax.dot_general(a, b, (((1,), (1,)), ((), ())), preferred_element_type=F32)


def _rows(shape):
    return lax.broadcasted_iota(jnp.int32, shape, 0)


def _scan_matrices(t):
    seg = t // 8
    r = np.arange(t)
    perm = (np.arange(t)[None, :] == ((r % 8) * seg + r // 8)[:, None]).astype(np.float32)
    rows, cols = r[:, None], r[None, :]
    taps, back = [], []
    for rowlen in (GRID_W, t):
        pos = rows % rowlen
        shift = {-2: (cols == rows - 2) & (pos >= 2), -1: (cols == rows - 1) & (pos >= 1),
                 0: cols == rows, 1: (cols == rows + 1) & (pos + 1 < rowlen),
                 2: (cols == rows + 2) & (pos + 2 < rowlen)}
        if rowlen == GRID_W:
            beside = [shift[-1].astype(np.float32), shift[1].astype(np.float32)]
        taps.append(np.stack([perm @ shift[k].astype(np.float32) for k in (-2, -1, 0, 1)]))
        back.append(np.stack([shift[k].astype(np.float32) @ perm.T for k in (2, 1, 0, -1)]))
    as_bf16 = lambda a: jnp.asarray(a, dtype=BF16)
    return as_bf16(np.stack(taps)), as_bf16(np.stack(back)), as_bf16(np.stack([perm, perm.T] + beside))


def _chunk_scan(a, b, reverse):
    row = _rows(a.shape)
    for s in (1, 2, 4):
        if reverse:
            m = row < 8 - s
            sh = 8 - s
        else:
            m = row >= s
            sh = s
        a_s = jnp.where(m, pltpu.roll(a, sh, 0), 1.0)
        b_s = jnp.where(m, pltpu.roll(b, sh, 0), 0.0)
        b = b + a * b_s
        a = a * a_s
    return a, b


def _chain_segments(ptot, hend, carry, reverse):
    ca, cb = _chunk_scan(ptot, hend, reverse)
    incl = ca * carry + cb
    r8 = _rows(incl.shape)
    if reverse:
        start = jnp.where(r8 < 7, pltpu.roll(incl, 7, 0), carry)
        last = incl[0:1, :]
    else:
        start = jnp.where(r8 >= 1, pltpu.roll(incl, 1, 0), carry)
        last = incl[7:8, :]
    return start, jnp.broadcast_to(last, incl.shape)


def _blocks(nblock, reverse):
    order = range(nblock - 1, -1, -1) if reverse else range(nblock)
    return [slice(8 * k, 8 * k + 8) for k in order]


def _scan_tile(a_ref, b_ref, out_ref, carry, reverse):
    t, w = a_ref.shape
    seg = t // 8

    hend, ptot = jnp.zeros((8, w), F32), jnp.ones((8, w), F32)
    for rows in _blocks(seg, reverse):
        a = a_ref[rows, :]
        hend, ptot = a * hend + b_ref[rows, :], a * ptot
    h, new_carry = _chain_segments(ptot, hend, carry, reverse)
    for rows in _blocks(seg, reverse):
        h = a_ref[rows, :] * h + b_ref[rows, :]
        out_ref[rows, :] = h
    return new_carry


def _scan_tile_backward(a_ref, dh_ref, g_ref, carry, reverse):
    t, w = a_ref.shape
    seg = t // 8

    uend, ptot = jnp.zeros((8, w), F32), jnp.ones((8, w), F32)
    for rows in _blocks(seg, reverse):
        a = a_ref[rows, :]
        uend, ptot = a * (dh_ref[rows, :] + uend), a * ptot
    u, new_carry = _chain_segments(ptot, uend, carry, reverse)
    for rows in _blocks(seg, reverse):
        g = dh_ref[rows, :] + u
        g_ref[rows, :] = g
        u = a_ref[rows, :] * g
    return new_carry


def _lru_coef(xb, wg_ref, d, ba, bx, lam, gc):
    w = xb.shape[1]
    xb16 = xb.astype(BF16)
    zr, zi = [], []
    for g in range(w // gc):
        z = _dot(xb16[:, g * gc:(g + 1) * gc], wg_ref[d, g])
        zr.append(z[:, :gc])
        zi.append(z[:, gc:])
    zr = zr[0] if len(zr) == 1 else jnp.concatenate(zr, axis=-1)
    zi = zi[0] if len(zi) == 1 else jnp.concatenate(zi, axis=-1)
    tr = jnp.tanh(zr + ba)
    ti = jnp.tanh(zi + bx)
    sp = _softplus(-lam)
    half = -0.5 * LRU_C * sp
    la = tr * half + half
    a = jnp.exp(la)
    q = _one_minus_sq(a, la)
    rs = lax.rsqrt(jnp.maximum(q, 1e-30))
    return a, q * rs, rs, tr, ti, sp


def _adamw(w, g, m, v):
    m2 = ADAM_B1 * m + (1.0 - ADAM_B1) * g
    v2 = ADAM_B2 * v + (1.0 - ADAM_B2) * (g * g)
    m_hat = m2 / ADAM_C1
    v_hat = v2 / ADAM_C2
    delta = -ADAM_LR * (m_hat / (jnp.sqrt(v_hat) + ADAM_EPS) + ADAM_WD * w)
    return delta, m2, v2


def _mod_forward(c8, cctx8, w_ada, small):
    d = c8.shape[1]
    cols = w_ada.shape[1]

    def body(c_ref, cctx_ref, w_ref, sm_ref, mod_ref, s_ref, sm_all, cbuf, mod_my, send_sems, recv_sems):
        _exchange_vmem(sm_ref, sm_all, send_sems, recv_sems, 2 * (NDEV - 1))
        _exchange_vmem(c_ref, cbuf, send_sems, recv_sems, 0)
        row = _rows((8, d))
        c_all = jnp.zeros((8, d), F32)
        for b in range(NDEV):
            c_all = jnp.where(row == b, cbuf[b], c_all)
        cc = cctx_ref[...]
        s_top = c_all * _sigmoid(c_all)
        s_bot = jnp.where(row == 0, cc * _sigmoid(cc), 0.0)
        s = jnp.concatenate([s_top, s_bot], axis=0)
        s_ref[...] = s
        mod_my[...] = jnp.dot(s, w_ref[...], precision=HIGHEST, preferred_element_type=F32)
        _exchange_vmem(mod_my, mod_ref, send_sems, recv_sems, NDEV - 1)

    return _call(
        body, name="mod_forward",
        out_shape=(jax.ShapeDtypeStruct((NDEV, 16, cols), F32), jax.ShapeDtypeStruct((16, d), F32),
                   jax.ShapeDtypeStruct((NDEV,) + small.shape, F32)),
        in_specs=[VMEM] * 4, out_specs=(VMEM,) * 3,
        scratch_shapes=[pltpu.VMEM((NDEV, 8, d), F32), pltpu.VMEM((16, cols), F32),
                        pltpu.SemaphoreType.DMA((3 * (NDEV - 1),)), pltpu.SemaphoreType.DMA((3 * (NDEV - 1),))],
        compiler_params=_params(),
    )(c8, cctx8, w_ada, small)


def _scatter_copies(src_ref, dst_ref, send_sems, recv_sems):
    me = _idx(_my_pos())
    copies = [pltpu.make_async_copy(src_ref.at[me], dst_ref.at[0], send_sems.at[0])]
    for k in range(1, NDEV):
        peer = _peer(k)
        copies.append(pltpu.make_async_remote_copy(
            src_ref=src_ref.at[_idx(peer)], dst_ref=dst_ref.at[k], send_sem=send_sems.at[k],
            recv_sem=recv_sems.at[k], device_id=peer, device_id_type=MESH))
    return copies


def _gather_copies(src_ref, dst_ref, send_sems, recv_sems):
    me = _idx(_my_pos())
    sends = [pltpu.make_async_copy(src_ref, dst_ref.at[me], send_sems.at[0])]
    arrivals = []
    for k in range(1, NDEV):
        peer = _peer(k)
        sends.append(pltpu.make_async_remote_copy(
            src_ref=src_ref, dst_ref=dst_ref.at[me], send_sem=send_sems.at[k],
            recv_sem=recv_sems.at[k], device_id=peer, device_id_type=MESH))
        arrivals.append(pltpu.make_async_remote_copy(
            src_ref=src_ref, dst_ref=dst_ref.at[_idx(peer)], send_sem=send_sems.at[k],
            recv_sem=recv_sems.at[k], device_id=peer, device_id_type=MESH))
    return sends, arrivals


def _exchange_wait(sends, arrivals):
    sends[0].wait()
    for cp in arrivals:
        cp.wait_recv()
    for cp in sends[1:]:
        cp.wait_send()


def _chip_order(k, c):
    return (6, 4 - 2 * c, 2 + 2 * c, 0)[k]


def _scatter_order(s, c):
    k = s >> 1
    mine = jnp.where(k == 0, 6, jnp.where(k == 1, 4 - 2 * c, jnp.where(k == 2, 2 + 2 * c, 0)))
    theirs = jnp.where(k == 0, 6, jnp.where(k == 1, 2 + 2 * c, jnp.where(k == 2, 4 - 2 * c, 0))) ^ 1
    return jnp.where((s & 1) == 0, theirs, mine)


def _peer_at(dist):
    x, y, c = _my_pos()
    return (x ^ ((dist >> 2) & 1), y ^ ((dist >> 1) & 1), c ^ (dist & 1))


def _reduce_small(packed, lru_parts, w_ada, dsilu_cctx):
    rp = packed.shape[0]
    rl = lru_parts.shape[1]
    d, cols = w_ada.shape
    assert cols % 128 == 0
    cb = cols // 128

    def body(p_ref, l_ref, w_ref, ds_ref, sum_ref, all_ref, lru_ref, cctx_ref,
             lbuf, lsum, cpart, call, send_sems, recv_sems, lsend, lrecv):
        me = _idx(_my_pos())
        scattered = _scatter_copies(l_ref, lbuf, lsend, lrecv)
        for cp in scattered:
            cp.start()
        _exchange_vmem(p_ref, all_ref, send_sems, recv_sems, 0)
        acc = all_ref[0]
        for j in range(1, NDEV):
            acc = acc + all_ref[j]
        sum_ref[...] = acc
        _exchange_wait(scattered, scattered[1:])
        red = lbuf[0]
        for k in range(1, NDEV):
            red = red + lbuf[k]
        lsum[...] = red
        lru_gather = _exchange_start(lsum, lru_ref, send_sems, recv_sems, NDEV - 1)
        part = jnp.zeros((8, d), F32)
        for q in range(cb):
            dm = jnp.broadcast_to(sum_ref[pl.ds((NDEV + me) * cb + q, 1), :], (8, 128))
            part = part + lax.dot_general(dm, w_ref[:, q * 128:(q + 1) * 128],
                                          (((1,), (1,)), ((), ())), precision=HIGHEST,
                                          preferred_element_type=F32)
        cpart[...] = part
        _exchange_vmem(cpart, call, send_sems, recv_sems, 2 * (NDEV - 1))
        _exchange_finish(lru_gather)
        tot = call[0]
        for j in range(1, NDEV):
            tot = tot + call[j]
        cctx_ref[...] = tot * ds_ref[...]

    return _call(
        body, name="reduce_small",
        out_shape=(jax.ShapeDtypeStruct((rp, 128), F32), jax.ShapeDtypeStruct((NDEV, rp, 128), F32),
                   jax.ShapeDtypeStruct((NDEV, rl, 128), F32), jax.ShapeDtypeStruct((8, d), F32)),
        in_specs=[VMEM] * 4, out_specs=(VMEM,) * 4,
        scratch_shapes=[pltpu.VMEM((NDEV, rl, 128), F32), pltpu.VMEM((rl, 128), F32), pltpu.VMEM((8, d), F32),
                        pltpu.VMEM((NDEV, 8, d), F32),
                        pltpu.SemaphoreType.DMA((3 * (NDEV - 1),)), pltpu.SemaphoreType.DMA((3 * (NDEV - 1),)),
                        pltpu.SemaphoreType.DMA((NDEV,)), pltpu.SemaphoreType.DMA((NDEV,))],
        compiler_params=_params(),
    )(packed, lru_parts, w_ada, dsilu_cctx)


def _normalize(src, mv, la, row0, tm, name, prev=None):
    rows, d = src.shape
    blk0 = row0 // tm

    def body(*refs):
        x_ref, mv_ref, h_ref = refs[0], refs[1], refs[-1]
        xf = x_ref[...]
        r = lax.rsqrt(jnp.mean(xf * xf, axis=-1, keepdims=True) + EPS)
        h = xf * r * (mv_ref[0:1, :] * (1.0 + mv_ref[1:2, :])) + mv_ref[2:3, :]
        h_ref[...] = h.astype(BF16)

    in_specs = [pl.BlockSpec((tm, d), lambda i: (i, 0)), pl.BlockSpec((8, d), lambda i: (0, 0))]
    args = [src, mv]
    aliases = {}
    if prev is not None:
        in_specs += [ANY]
        args += [prev]
        aliases = {2: 0}
    return _call(
        body, name=name,
        grid=(rows // tm,),
        out_shape=jax.ShapeDtypeStruct((la, d), BF16),
        in_specs=in_specs,
        out_specs=pl.BlockSpec((tm, d), lambda i: (blk0 + i, 0)),
        input_output_aliases=aliases,
        compiler_params=_params(("arbitrary",)),
    )(*args)


def _gather_order(step):
    return (step & 1) | (((step >> 2) & 1) << 1) | (((step >> 1) & 1) << 2)


def _in_projection(h, w_shard, wo_shard, tm):
    la, d = h.shape
    bw = w_shard.shape[1]
    ni = la // tm
    where = jnp.reshape(_idx(_my_pos()), (1,)).astype(jnp.int32)

    def body(me_ref, h_ref, w_ref, wo_ref, p_ref, all_ref, wo_all, wbuf, send_sems, recv_sems, local_sems,
             wo_send, wo_recv):
        s, i = pl.program_id(0), pl.program_id(1)
        x, y, c = _my_pos()
        wo_sends, wo_arrivals = _gather_copies(wo_ref, wo_all, wo_send, wo_recv)

        @pl.when((s == NDEV // 2) & (i == 0))
        def _():
            for cp in wo_sends:
                cp.start()

        me, sibling = (x, y, c), (x, y, 1 - c)
        chips = [(1 - x, y), (x, 1 - y), (1 - x, 1 - y)]

        def copy(k, block, to, from_shard=False):
            return pltpu.make_async_remote_copy(
                src_ref=w_ref if from_shard else all_ref.at[_idx(block)], dst_ref=all_ref.at[_idx(block)],
                send_sem=send_sems.at[k], recv_sem=recv_sems.at[k], device_id=to, device_id_type=MESH)

        def load(block, slot):
            return pltpu.make_async_copy(all_ref.at[_idx(block)], wbuf.at[slot], local_sems.at[1])

        keep = pltpu.make_async_copy(w_ref, all_ref.at[_idx(me)], local_sems.at[0])
        first = [copy(0, me, sibling, True)] + [copy(1 + j, me, (*chip, c), True) for j, chip in enumerate(chips)]
        passed = [copy(4 + j, (*chip, c), sibling) for j, chip in enumerate(chips)]
        steps = [(copy(0, sibling, me), None, sibling)]
        for j, chip in enumerate(chips):
            steps.append((copy(1 + j, (*chip, c), me), passed[j], (*chip, c)))
            steps.append((copy(4 + j, (*chip, 1 - c), me), None, (*chip, 1 - c)))

        @pl.when((s == 0) & (i == 0))
        def _():
            keep.start()
            mine = pltpu.make_async_copy(w_ref, wbuf.at[0], local_sems.at[1])
            mine.start()
            for cp in first:
                cp.start()
            mine.wait()

        for n, (arrival, forward, block) in enumerate(steps, start=1):
            @pl.when((s == n - 1) & (i == ni - 1))
            def _(arrival=arrival, forward=forward, block=block, n=n):
                arrival.wait_recv()
                if forward is not None:
                    forward.start()
                load(block, n % 2).start()

        @pl.when((s > 0) & (i == 0))
        def _():
            load(me, s % 2).wait()

        p_ref[...] = _dot(h_ref[...], wbuf[s % 2]).astype(BF16)

        @pl.when((s == NDEV - 1) & (i == ni - 1))
        def _():
            for cp in first + passed:
                cp.wait_send()
            keep.wait()
            _exchange_wait(wo_sends, wo_arrivals)

    return _call(
        body, name="in_projection",
        grid_spec=pltpu.PrefetchScalarGridSpec(
            num_scalar_prefetch=1, grid=(NDEV, ni),
            in_specs=[pl.BlockSpec((tm, d), lambda s, i, me_ref: (i, 0)), ANY, ANY],
            out_specs=(pl.BlockSpec((tm, bw), lambda s, i, me_ref: (i, me_ref[0] ^ _gather_order(s))), ANY, ANY),
            scratch_shapes=[pltpu.VMEM((2, d, bw), BF16), pltpu.SemaphoreType.DMA((7,)),
                            pltpu.SemaphoreType.DMA((7,)), pltpu.SemaphoreType.DMA((2,)),
                            pltpu.SemaphoreType.DMA((NDEV,)), pltpu.SemaphoreType.DMA((NDEV,))]),
        out_shape=(jax.ShapeDtypeStruct((la, NDEV * bw), BF16), jax.ShapeDtypeStruct((NDEV, d, bw), BF16),
                   jax.ShapeDtypeStruct((NDEV,) + wo_shard.shape, wo_shard.dtype)),
        compiler_params=_params(("arbitrary", "arbitrary")),
    )(where, h, w_shard, wo_shard)


def _conv_input(p, wcb, taps_m, l, t):
    la = p.shape[0]
    w = wcb.shape[1]
    nt = l // t

    def body(v_ref, wcb_ref, tm_ref, xb_ref):
        taps = _dot(tm_ref[...].reshape(4 * t, t), v_ref[...])
        xb = wcb_ref[4:5, :] + wcb_ref[0:1, :] * taps[0:t]
        for j in range(1, 4):
            xb = xb + wcb_ref[j:j + 1, :] * taps[j * t:(j + 1) * t]
        xb_ref[...] = xb

    return _call(
        body, name="conv_input",
        grid=(nt + 1,),
        out_shape=jax.ShapeDtypeStruct((la, w), F32),
        in_specs=[pl.BlockSpec((t, w), lambda i: (i, 4)), pl.BlockSpec((8, w), lambda i: (0, 0)),
                  pl.BlockSpec((None, 4, t, t), lambda i: (i // nt, 0, 0, 0))],
        out_specs=pl.BlockSpec((t, w), lambda i: (i, 0)),
        compiler_params=_params(("arbitrary",)),
    )(p, wcb, taps_m)


def _lru_forward(xb, wg, lv, l, t):
    la, w = xb.shape
    gc = wg.shape[2]
    nt = l // t

    def body(xf_ref, xr_ref, wg_ref, lv_ref, hf_ref, hr_ref, a_s, b_s, carry):
        @pl.when(pl.program_id(0) == 0)
        def _():
            carry[...] = jnp.zeros_like(carry)

        for dr, (x_ref, h_ref) in enumerate(((xf_ref, hf_ref), (xr_ref, hr_ref))):
            x = x_ref[...]
            a, s, _, _, ti, _ = _lru_coef(x, wg_ref, dr, lv_ref[3 * dr:3 * dr + 1, :],
                                          lv_ref[3 * dr + 1:3 * dr + 2, :], lv_ref[3 * dr + 2:3 * dr + 3, :], gc)
            a_s[...] = a
            b_s[...] = (s * x) * (0.5 * ti + 0.5)
            carry[dr] = _scan_tile(a_s, b_s, h_ref, carry[dr], dr == 1)

    full = lambda shape: pl.BlockSpec(shape, lambda i: (0,) * len(shape))
    fmap = lambda i: (jnp.where(i == 0, nt, i - 1), 0)
    rmap = lambda i: (jnp.where(i == 0, nt, nt - i), 0)
    return _call(
        body, name="lru_forward",
        grid=(nt + 1,),
        out_shape=(jax.ShapeDtypeStruct((la, w), F32), jax.ShapeDtypeStruct((la, w), F32)),
        in_specs=[pl.BlockSpec((t, w), fmap), pl.BlockSpec((t, w), rmap), full(wg.shape), full(lv.shape)],
        out_specs=(pl.BlockSpec((t, w), fmap), pl.BlockSpec((t, w), rmap)),
        scratch_shapes=[pltpu.VMEM((t, w), F32), pltpu.VMEM((t, w), F32), pltpu.VMEM((2, 8, w), F32)],
        compiler_params=_params(("arbitrary",)),
    )(xb, xb, wg, lv)


def _mix_gates(p_refs, hf_ref, hr_ref, wca_ref, perm_ref, t, w):
    bl, cl, ul, gl, ql = [r[...].astype(F32) for r in p_refs]
    tt = cl * ul
    tt16 = tt.astype(BF16)
    beside = _dot(perm_ref[2:4].reshape(2 * t, t), tt16)
    before, after = beside[:t], beside[t:]
    z = wca_ref[0:1, :] * before + wca_ref[1:2, :] * tt + wca_ref[2:3, :] * after
    sig_g = _sigmoid(gl)
    sig_q = _sigmoid(ql)
    ylru = _dot(perm_ref[1], (hf_ref[...] + hr_ref[...]).astype(BF16))
    return bl, cl, ul, gl, ql, (before, tt, after), z, sig_g, sig_q, ylru


def _p_specs(t, w, nt):
    return [pl.BlockSpec((t, w), functools.partial(lambda i, s: (jnp.minimum(i, nt - 1), s), s=s))
            for s in (0, 1, 2, 3, 5)]


def _mix_forward(x, tgt, p, hf, hr, wo, ov, wca, perm, t):
    l, d = x.shape
    w = d // 2
    nt = l // t

    def body(x_ref, tg_ref, b_ref, c_ref, u_ref, g_ref, q_ref, hf_ref, hr_ref, wo_ref, ov_ref, wca_ref, perm_ref,
             dn_ref, ct_ref, do_ref, part_ref):
        i = pl.program_id(0)
        bl, _, _, gl, ql, _, z, sig_g, sig_q, ylru = _mix_gates(
            (b_ref, c_ref, u_ref, g_ref, q_ref), hf_ref, hr_ref, wca_ref, perm_ref, t, w)
        ya = bl * z * (gl * sig_g)
        yb = ylru * (ql * sig_q)
        ct_ref[:, 0:w] = ya.astype(BF16)
        ct_ref[:, w:] = yb.astype(BF16)
        out = _dot(ya.astype(BF16), wo_ref[0:w, :]) + _dot(yb.astype(BF16), wo_ref[w:, :])
        gate, fg = ov_ref[0:1, :], ov_ref[1:2, :]
        n = x_ref[...] + gate * out
        rr = lax.rsqrt(jnp.mean(n * n, axis=-1, keepdims=True) + EPS)
        nh = n * rr
        e = nh * fg - tg_ref[...]
        loss = 0.5 * jnp.sum(jnp.mean(e * e, axis=-1, keepdims=True), axis=0, keepdims=True)
        dy = e * (1.0 / d)
        dnh = dy * fg
        dn = rr * (dnh - nh * jnp.mean(dnh * nh, axis=-1, keepdims=True))
        dn_ref[...] = dn.astype(BF16)
        do_ref[...] = (dn * gate).astype(BF16)

        @pl.when(i == 0)
        def _():
            part_ref[...] = jnp.zeros_like(part_ref)

        part_ref[0:1, :] += jnp.sum(dy * nh, axis=0, keepdims=True)
        part_ref[1:2, :] += jnp.sum(dn * out, axis=0, keepdims=True)
        part_ref[2:3, :] += jnp.broadcast_to(loss, (1, d))

    tile = lambda cols: pl.BlockSpec((t, cols), lambda i: (i, 0))
    full = lambda shape: pl.BlockSpec(shape, lambda i: (0,) * len(shape))
    return _call(
        body, name="mix_forward",
        grid=(nt,),
        out_shape=(jax.ShapeDtypeStruct((l, d), BF16), jax.ShapeDtypeStruct((l, d), BF16),
                   jax.ShapeDtypeStruct((l, d), BF16), jax.ShapeDtypeStruct((8, d), F32)),
        in_specs=[tile(d), tile(d)] + _p_specs(t, w, nt) + [tile(w), tile(w),
                  pl.BlockSpec((d, d), lambda i: (0, 0), pipeline_mode=pl.Buffered(1)),
                  full(ov.shape), full(wca.shape), full(perm.shape)],
        out_specs=(tile(d), tile(d), tile(d), full((8, d))),
        compiler_params=_params(("arbitrary",)),
    )(x, tgt, p, p, p, p, p, hf, hr, wo, ov, wca, perm)


def _mix_backward(dout, p, hf, hr, wo, wca, perm, l, t):
    d = dout.shape[1]
    w = d // 2
    nt = l // t
    la = p.shape[0]

    def body(do_ref, b_ref, c_ref, u_ref, g_ref, q_ref, hf_ref, hr_ref, wo_ref, wca_ref, perm_ref,
             dp_ref, dh_ref, part_ref):
        i = pl.program_id(0)

        @pl.when(i == 0)
        def _():
            part_ref[...] = jnp.zeros_like(part_ref)

        @pl.when(i == nt)
        def _():
            dp_ref[...] = jnp.zeros_like(dp_ref)

        @pl.when(i < nt)
        def _():
            bl, cl, ul, gl, ql, taps, z, sig_g, sig_q, ylru = _mix_gates(
                (b_ref, c_ref, u_ref, g_ref, q_ref), hf_ref, hr_ref, wca_ref, perm_ref, t, w)
            do = do_ref[...]
            dya = _dot_nt(do, wo_ref[0:w, :])
            dyb = _dot_nt(do, wo_ref[w:, :])
            sg = gl * sig_g
            dz = dya * bl * sg
            dz16 = dz.astype(BF16)
            beside = _dot(perm_ref[2:4].reshape(2 * t, t), dz16)
            dt = wca_ref[0:1, :] * beside[t:] + wca_ref[1:2, :] * dz + wca_ref[2:3, :] * beside[:t]
            dp_ref[:, 0:w] = (dya * z * sg).astype(BF16)
            dp_ref[:, w:2 * w] = (dt * ul).astype(BF16)
            dp_ref[:, 2 * w:3 * w] = (dt * cl).astype(BF16)
            dp_ref[:, 3 * w:4 * w] = (dya * bl * z * (sig_g * (1.0 + gl * (1.0 - sig_g)))).astype(BF16)
            dp_ref[:, 4 * w:5 * w] = jnp.zeros((t, w), BF16)
            dp_ref[:, 5 * w:6 * w] = (dyb * ylru * (sig_q * (1.0 + ql * (1.0 - sig_q)))).astype(BF16)
            dh_ref[...] = _dot(perm_ref[0], (dyb * (ql * sig_q)).astype(BF16)).astype(BF16)
            for j in range(3):
                part_ref[j:j + 1, :] += jnp.sum(dz * taps[j], axis=0, keepdims=True)

    clamp = lambda cols: pl.BlockSpec((t, cols), lambda i: (jnp.minimum(i, nt - 1), 0))
    full = lambda shape: pl.BlockSpec(shape, lambda i: (0,) * len(shape))
    return _call(
        body, name="mix_backward",
        grid=(nt + 1,),
        out_shape=(jax.ShapeDtypeStruct((la, 6 * w), BF16), jax.ShapeDtypeStruct((l, w), BF16),
                   jax.ShapeDtypeStruct((8, w), F32)),
        in_specs=[clamp(d)] + _p_specs(t, w, nt) + [clamp(w), clamp(w),
                  pl.BlockSpec((d, d), lambda i: (0, 0), pipeline_mode=pl.Buffered(1)), full(wca.shape),
                  full(perm.shape)],
        out_specs=(pl.BlockSpec((t, 6 * w), lambda i: (i, 0)), clamp(w), full((8, w))),
        compiler_params=_params(("arbitrary",)),
    )(dout, p, p, p, p, p, hf, hr, wo, wca, perm)


def _lru_backward(direction, xb, dhs, hs, wg, lv, l, t, conv=None):
    la, w = hs.shape
    gc = wg.shape[2]
    ng = w // gc
    nt = l // t
    nblk8 = la // 8
    last = conv is not None
    assert last == (direction == 1)

    if direction == 0:
        tile = lambda i: jnp.where(i == nt, nt, nt - 1 - i)
        halo = lambda i: jnp.where(tile(i) == 0, nblk8 - 1, tile(i) * (t // 8) - 1)
    else:
        tile = lambda i: i
        halo = lambda i: jnp.minimum((i + 1) * (t // 8), nblk8 - 1)

    def body(*refs):
        x_ref, dh_ref, hs_ref, halo_ref, wg_ref, lv_ref = refs[:6]
        i = pl.program_id(0)
        is_ctx = i == nt
        if last:
            v_ref, wcb_ref, bm_ref, dxo_ref, _, gw_ref = refs[6:12]
            out_ref, dwg_ref, part_ref, sc_ref, a_s, dh_s, g_s, carry, send_sems, recv_sems = refs[12:]
            copies = _scatter_copies(gw_ref, sc_ref, send_sems, recv_sems)
        else:
            out_ref, dwg_ref, part_ref, a_s, dh_s, g_s, carry = refs[-7:]
            copies = []

        @pl.when(i == 0)
        def _():
            carry[...] = jnp.zeros_like(carry)
            dwg_ref[...] = jnp.zeros_like(dwg_ref)
            part_ref[...] = jnp.zeros_like(part_ref)
            for cp in copies:
                cp.start()

        if last:
            @pl.when(is_ctx)
            def _():
                _exchange_wait(copies, copies[1:])

        xb = x_ref[...]
        lam = lv_ref[3 * direction + 2:3 * direction + 3, :]
        a, s, rs, tr, ti, sp = _lru_coef(xb, wg_ref, direction, lv_ref[3 * direction:3 * direction + 1, :],
                                         lv_ref[3 * direction + 1:3 * direction + 2, :], lam, gc)
        hs_t = hs_ref[...]
        r8 = _rows((8, w))
        if direction == 0:
            edge = jnp.where(is_ctx, 0.0, halo_ref[7:8, :])
            first = jnp.where(r8 == 0, edge, pltpu.roll(hs_t[t - 8:, :], 1, 0))
            hprev = jnp.concatenate([first, hs_t[:t - 8, :]], axis=0)
        else:
            edge = jnp.where(is_ctx, 0.0, halo_ref[0:1, :])
            final = jnp.where(r8 == 7, edge, pltpu.roll(hs_t[:8, :], 7, 0))
            hprev = jnp.concatenate([hs_t[8:, :], final], axis=0)
        a_s[...] = a
        dh_s[...] = jnp.where(is_ctx, 0.0, dh_ref[...].astype(F32))
        carry[...] = _scan_tile_backward(a_s, dh_s, g_s, carry[...], direction == 0)

        g = g_s[...]
        r = 0.5 * tr + 0.5
        ig = 0.5 * ti + 0.5
        ix = ig * xb
        gs = g * s
        dla = (g * a) * (hprev - ix * (a * rs))
        dxb = gs * ig
        dzr = dla * (r * (1.0 - tr)) * (-LRU_C * sp)
        dzi = gs * ix * (1.0 - ti)
        part_ref[0:1, :] += jnp.sum(dzr, axis=0, keepdims=True)
        part_ref[1:2, :] += jnp.sum(dzi, axis=0, keepdims=True)
        part_ref[2:3, :] += jnp.sum(dla * r, axis=0, keepdims=True) * (LRU_C * _sigmoid(-lam))
        pieces = []
        for gi in range(ng):
            sl = slice(gi * gc, (gi + 1) * gc)
            dz = jnp.concatenate([dzr[:, sl], dzi[:, sl]], axis=-1).astype(BF16)
            pieces.append(_dot_nt(dz, wg_ref[direction, gi]))
            dwg_ref[gi] += _dot(xb[:, sl].T.astype(BF16), dz)
        dxb = dxb + (pieces[0] if ng == 1 else jnp.concatenate(pieces, axis=-1))
        if not last:
            out_ref[...] = dxb
        else:
            dxb = dxb + dxo_ref[...]
            v = v_ref[...].astype(F32)
            backs = _dot(bm_ref[...].reshape(4 * t, t), dxb.astype(BF16))
            dv = jnp.zeros((t, w), F32)
            for j in range(4):
                back = backs[j * t:(j + 1) * t]
                dv = dv + wcb_ref[j:j + 1, :] * back
                part_ref[4 + j:5 + j, :] += jnp.sum(back * v, axis=0, keepdims=True)
            out_ref[...] = dv.astype(BF16)
            part_ref[3:4, :] += jnp.sum(dxb, axis=0, keepdims=True)

    full = lambda shape: pl.BlockSpec(shape, lambda i: (0,) * len(shape))
    kind = lambda i: (jnp.where(i == nt, 1, 0), 0, 0, 0)
    in_specs = [pl.BlockSpec((t, w), lambda i: (tile(i), 0)),
                pl.BlockSpec((t, w), lambda i: (jnp.minimum(tile(i), nt - 1), 0)),
                pl.BlockSpec((t, w), lambda i: (tile(i), 0)),
                pl.BlockSpec((8, w), lambda i: (halo(i), 0)),
                full(wg.shape), full(lv.shape)]
    args = [xb, dhs, hs, hs, wg, lv]
    more_out, more_spec, more_scratch = (), (), []
    if last:
        p, wcb, back_m, dxb_other, dp, g_wout = conv
        in_specs += [pl.BlockSpec((t, w), lambda i: (tile(i), 4)), full(wcb.shape),
                     pl.BlockSpec((None, 4, t, t), kind), pl.BlockSpec((t, w), lambda i: (tile(i), 0)), ANY, ANY]
        args += [p, wcb, back_m, dxb_other, dp, g_wout]
        out0 = jax.ShapeDtypeStruct(dp.shape, dp.dtype)
        spec0 = pl.BlockSpec((t, w), lambda i: (tile(i), 4))
        more_out, more_spec = (jax.ShapeDtypeStruct(g_wout.shape, g_wout.dtype),), (ANY,)
        more_scratch = [pltpu.SemaphoreType.DMA((NDEV,)), pltpu.SemaphoreType.DMA((NDEV,))]
        aliases = {10: 0}
    else:
        out0 = jax.ShapeDtypeStruct((la, w), F32)
        spec0 = pl.BlockSpec((t, w), lambda i: (tile(i), 0))
        aliases = {}
    return _call(
        body, name="lru_backward_%d" % direction,
        grid=(nt + 1,),
        out_shape=(out0, jax.ShapeDtypeStruct((ng, gc, 2 * gc), F32), jax.ShapeDtypeStruct((8, w), F32)) + more_out,
        in_specs=in_specs,
        out_specs=(spec0, full((ng, gc, 2 * gc)), full((8, w))) + more_spec,
        scratch_shapes=[pltpu.VMEM((t, w), F32), pltpu.VMEM((t, w), F32), pltpu.VMEM((t, w), F32),
                        pltpu.VMEM((8, w), F32)] + more_scratch,
        input_output_aliases=aliases,
        compiler_params=_params(("arbitrary",)),
    )(*args)


def _weight_grad_t(a, b, nblk_m, nblk_n, tk, name):
    k, m = a.shape
    n = b.shape[1]
    bm, bn = m // nblk_m, n // nblk_n
    nk = k // tk

    def body(a_ref, b_ref, o_ref, acc):
        kk = pl.program_id(2)

        @pl.when(kk == 0)
        def _():
            acc[...] = jnp.zeros_like(acc)

        acc[...] += lax.dot_general(a_ref[...], b_ref[...], (((0,), (0,)), ((), ())), preferred_element_type=F32)

        @pl.when(kk == nk - 1)
        def _():
            o_ref[...] = acc[...].astype(BF16)

    return _call(
        body, name=name,
        grid=(nblk_m, nblk_n, nk),
        out_shape=jax.ShapeDtypeStruct((nblk_m * nblk_n, bm, bn), BF16),
        in_specs=[pl.BlockSpec((tk, bm), lambda i, j, kk: (kk, i)),
                  pl.BlockSpec((tk, bn), lambda i, j, kk: (kk, j))],
        out_specs=pl.BlockSpec((None, bm, bn), lambda i, j, kk: (i * nblk_n + j, 0, 0)),
        scratch_shapes=[pltpu.VMEM((bm, bn), F32)],
        compiler_params=_params(("arbitrary", "arbitrary", "arbitrary")),
    )(a, b)


def _weight_grad_scatter(at, b, tk, name):
    k, m = at.shape
    n = b.shape[1]
    bn = n // NDEV
    nk = k // tk
    where = jnp.stack([_idx(_my_pos()), lax.axis_index("c")]).astype(jnp.int32)
    tn = (((0,), (0,)), ((), ()))

    def body(w_ref, a_ref, b_ref, recv_ref, acc, sbuf, sib, sib_send, sib_recv, chip_send, chip_recv, keep_sem):
        s, kk = pl.program_id(0), pl.program_id(1)
        x, y, c = _my_pos()

        @pl.when(kk == 0)
        def _():
            acc[...] = lax.dot_general(a_ref[...], b_ref[...], tn, preferred_element_type=F32)

        @pl.when(kk > 0)
        def _():
            acc[...] += lax.dot_general(a_ref[...], b_ref[...], tn, preferred_element_type=F32)

        def to_sibling(j):
            return pltpu.make_async_remote_copy(
                src_ref=sbuf.at[0], dst_ref=sib.at[j], send_sem=sib_send.at[j], recv_sem=sib_recv.at[j],
                device_id=(x, y, 1 - c), device_id_type=MESH)

        def to_chip(j):
            dist = _chip_order(j, c)
            return pltpu.make_async_remote_copy(
                src_ref=sbuf.at[1], dst_ref=recv_ref.at[dist // 2], send_sem=chip_send.at[j],
                recv_sem=chip_recv.at[dist // 2], device_id=_peer_at(dist), device_id_type=MESH)

        keep = pltpu.make_async_copy(sbuf.at[1], recv_ref.at[0], keep_sem)
        sends = []
        for j in range(4):
            sends += [to_sibling(j), to_chip(j) if j < 3 else keep]

        for st in range(NDEV):
            @pl.when((kk == nk - 1) & (s == st))
            def _(st=st):
                if st >= 2:
                    sends[st - 2].wait_send()
                part = acc[...]
                if st % 2 == 1:
                    to_sibling(st // 2).wait_recv()
                    part = part + sib[st // 2].astype(F32)
                sbuf[st % 2] = part.astype(BF16)
                sends[st].start()
                if st == NDEV - 1:
                    sends[st - 1].wait_send()
                    sends[st].wait()
                    for j in range(1, 4):
                        pltpu.make_async_remote_copy(
                            src_ref=sbuf.at[0], dst_ref=recv_ref.at[j], send_sem=chip_send.at[0],
                            recv_sem=chip_recv.at[j], device_id=_peer_at(2 * j), device_id_type=MESH).wait_recv()

    blk = lambda s, w_ref: w_ref[0] ^ _scatter_order(s, w_ref[1])
    return _call(
        body, name=name,
        grid_spec=pltpu.PrefetchScalarGridSpec(
            num_scalar_prefetch=1, grid=(NDEV, nk),
            in_specs=[pl.BlockSpec((tk, m), lambda s, kk, w_ref: (kk, 0)),
                      pl.BlockSpec((tk, bn), lambda s, kk, w_ref: (kk, blk(s, w_ref)))],
            out_specs=ANY,
            scratch_shapes=[pltpu.VMEM((m, bn), F32), pltpu.VMEM((2, m, bn), BF16), pltpu.VMEM((4, m, bn), BF16),
                            pltpu.SemaphoreType.DMA((4,)), pltpu.SemaphoreType.DMA((4,)),
                            pltpu.SemaphoreType.DMA((4,)), pltpu.SemaphoreType.DMA((4,)),
                            pltpu.SemaphoreType.DMA]),
        out_shape=jax.ShapeDtypeStruct((4, m, bn), BF16),
        compiler_params=_params(("arbitrary", "arbitrary")),
    )(where, at, b)


def _input_backward(dp, w_all, src, mv, row0, tm, nbk, name, dn=None, cols=None):
    rows, d = src.shape
    nb, _, bw = w_all.shape
    first, last = (0, nb * bw - 1) if cols is None else cols
    k0 = first // (nbk * bw)
    nk = last // (nbk * bw) - k0 + 1
    ni = rows // tm
    blk0 = row0 // tm
    latent = dn is not None

    def body(*refs):
        dp_ref, w_ref, x_ref, mv_ref = refs[:4]
        outs = refs[4 + latent:]
        part_ref, acc = outs[latent], outs[latent + 1]
        i, k = pl.program_id(0), pl.program_id(1)

        def product():
            step = _dot_nt(dp_ref[:, 0:bw], w_ref[0])
            for q in range(1, nbk):
                step = step + _dot_nt(dp_ref[:, q * bw:(q + 1) * bw], w_ref[q])
            return step

        def finish(slot):
            xf = x_ref[...]
            r = lax.rsqrt(jnp.mean(xf * xf, axis=-1, keepdims=True) + EPS)
            xn = xf * r
            dhl = acc[slot]
            gain, sc = mv_ref[0:1, :], mv_ref[1:2, :]
            dhx = jnp.sum(dhl * xn, axis=0, keepdims=True)
            part_ref[0:1, :] += jnp.sum(dhl, axis=0, keepdims=True)
            part_ref[1:2, :] += dhx * gain
            part_ref[2:3, :] += dhx * (1.0 + sc)
            if latent:
                dxn = dhl * (gain * (1.0 + sc))
                outs[0][...] = (refs[4][...].astype(F32)
                                + r * (dxn - xn * jnp.mean(dxn * xn, axis=-1, keepdims=True)))

        @pl.when((i == 0) & (k == 0))
        def _():
            part_ref[...] = jnp.zeros_like(part_ref)
            acc[0] = product()

        @pl.when((i > 0) & (i < ni) & (k == 0))
        def _():
            acc[i % 2] = product()
            finish((i - 1) % 2)

        @pl.when((i == ni) & (k == 0))
        def _():
            finish((ni - 1) % 2)

        @pl.when((i < ni) & (k > 0))
        def _():
            acc[i % 2] += product()

    tile = pl.BlockSpec((tm, d), lambda i, k: (jnp.maximum(i - 1, 0), 0))
    vec = pl.BlockSpec((8, d), lambda i, k: (0, 0))
    kblock = lambda i, k: k0 + jnp.where(i == ni, nk - 1, k)
    return _call(
        body, name=name,
        grid=(ni + 1, nk),
        out_shape=((jax.ShapeDtypeStruct((rows, d), F32),) if latent else ()) + (jax.ShapeDtypeStruct((8, d), F32),),
        in_specs=[pl.BlockSpec((tm, nbk * bw), lambda i, k: (blk0 + jnp.minimum(i, ni - 1), kblock(i, k))),
                  pl.BlockSpec((nbk, d, bw), lambda i, k: (kblock(i, k), 0, 0)), tile, vec]
                 + ([tile] if latent else []),
        out_specs=((tile,) if latent else ()) + (vec,),
        scratch_shapes=[pltpu.VMEM((2, tm, d), F32)],
        compiler_params=_params(("arbitrary", "arbitrary")),
    )(*([dp, w_all, src, mv] + ([dn] if latent else [])))


def _adamw_scattered(parts, w, m, v, tr):
    r, c = w.shape
    nslot = parts.shape[0]

    def body(p_ref, w_ref, m_ref, v_ref, g_ref, d_ref, m2_ref, v2_ref):
        g = p_ref[0].astype(F32)
        for k in range(1, nslot):
            g = g + p_ref[k].astype(F32)
        g_ref[...] = g
        d_ref[...], m2_ref[...], v2_ref[...] = _adamw(w_ref[...], g, m_ref[...], v_ref[...])

    tile = pl.BlockSpec((tr, c), lambda i: (i, 0))
    return _call(
        body, name="adamw_scattered_%dx%d" % (r, c),
        grid=(r // tr,),
        out_shape=tuple(jax.ShapeDtypeStruct((r, c), F32) for _ in range(4)),
        in_specs=[pl.BlockSpec((nslot, tr, c), lambda i: (0, i, 0)), tile, tile, tile],
        out_specs=(tile,) * 4,
        compiler_params=_params(("arbitrary",)),
    )(parts, w, m, v)


def _adamw_ada(st, dmod, w, m, v, tr):
    r, c = w.shape

    def body(s_ref, dm_ref, w_ref, m_ref, v_ref, g_ref, d_ref, m2_ref, v2_ref):
        g = jnp.dot(s_ref[...], dm_ref[...], precision=HIGHEST, preferred_element_type=F32)
        g_ref[...] = g
        d_ref[...], m2_ref[...], v2_ref[...] = _adamw(w_ref[...], g, m_ref[...], v_ref[...])

    tile = pl.BlockSpec((tr, c), lambda i: (i, 0))
    return _call(
        body, name="adamw_ada",
        grid=(r // tr,),
        out_shape=tuple(jax.ShapeDtypeStruct((r, c), F32) for _ in range(4)),
        in_specs=[pl.BlockSpec((tr, 16), lambda i: (i, 0)), pl.BlockSpec((16, c), lambda i: (0, 0)),
                  tile, tile, tile],
        out_specs=(tile,) * 4,
        compiler_params=_params(("arbitrary",)),
    )(st, dmod, w, m, v)


def _adamw_small(gs, ws, ms, vs):
    n = len(ws)

    def body(*refs):
        for j in range(n):
            g_ref, w_ref, m_ref, v_ref = refs[j], refs[n + j], refs[2 * n + j], refs[3 * n + j]
            d_ref, m2_ref, v2_ref = refs[4 * n + j], refs[5 * n + j], refs[6 * n + j]
            d_ref[...], m2_ref[...], v2_ref[...] = _adamw(w_ref[...], g_ref[...], m_ref[...], v_ref[...])

    shapes = tuple(jax.ShapeDtypeStruct(a.shape, F32) for a in ws)
    out = _call(
        body, name="adamw_small",
        out_shape=shapes * 3,
        in_specs=[VMEM] * (4 * n), out_specs=(VMEM,) * (3 * n),
        compiler_params=_params(),
    )(*gs, *ws, *ms, *vs)
    return list(out[:n]), list(out[n:2 * n]), list(out[2 * n:])


def _blockdiag_groups(wh, gc):
    h, dh, _ = wh.shape
    g = gc // dh
    w4 = wh.reshape(h // g, g, dh, dh)
    bd = jnp.einsum("ngij,gh->ngihj", w4, jnp.eye(g, dtype=wh.dtype))
    return bd.reshape(h // g, gc, gc)


def _blockdiag_extract(bd, dh):
    ng, gc, _ = bd.shape
    g = gc // dh
    x = bd.reshape(ng, g, dh, g, dh)
    return jnp.einsum("ngihj,gh->ngij", x, jnp.eye(g, dtype=bd.dtype)).reshape(ng * g, dh, dh)


def _largest_tile(n, cap):
    return max(q for q in range(128, min(n, cap) + 1, 128) if n % q == 0)


def _rows8(*vecs):
    rows = [jnp.reshape(v, (1, -1)).astype(F32) for v in vecs]
    n = rows[0].shape[1]
    return jnp.concatenate(rows + [jnp.zeros((8 - len(rows), n), F32)], axis=0)


def _pack(pieces):
    flat = jnp.concatenate([jnp.reshape(a, (-1,)).astype(F32) for a in pieces])
    total = -(-flat.shape[0] // 1024) * 1024
    return jnp.pad(flat, (0, total - flat.shape[0])).reshape(total // 128, 128)


def _unpack(packed, shapes):
    flat = packed.reshape(-1)
    out, off = [], 0
    for s in shapes:
        n = 1
        for q in s:
            n *= q
        out.append(flat[off:off + n].reshape(s))
        off += n
    return out


def kernel(x, c, ctx, c_ctx, norm_g, w_ada, b_ada, w_in, w_conv_a, w_conv_b, b_conv_b, lru_wa, lru_ba, lru_wx, lru_bx, lru_lambda, w_out, final_g, loss_target, m_c_ctx, m_norm_g, m_w_ada, m_b_ada, m_w_in, m_w_conv_a, m_w_conv_b, m_b_conv_b, m_lru_wa, m_lru_ba, m_lru_wx, m_lru_bx, m_lru_lambda, m_w_out, m_final_g, v_c_ctx, v_norm_g, v_w_ada, v_b_ada, v_w_in, v_w_conv_a, v_w_conv_b, v_b_conv_b, v_lru_wa, v_lru_ba, v_lru_wx, v_lru_bx, v_lru_lambda, v_w_out, v_final_g):
    _, l, d = x.shape
    lc = ctx.shape[1]
    w = d // 2
    t = lc
    assert l % t == 0 and t % GRID_W == 0 and t % 128 == 0
    dh = w // N_HEADS
    gc = min(w, MXU_WIDTH)
    cols = w_ada.shape[2]
    wo_rows = w_out.shape[1]
    me = _idx(_my_pos())
    x2, ctx2, tgt2 = x[0], ctx[0], loss_target[0]
    w_ada2, w_in2, w_out2 = w_ada[0], w_in[0], w_out[0]

    small_mine = jnp.concatenate([a.reshape(-1) for a in (w_conv_a, w_conv_b, lru_ba, lru_bx, lru_lambda)]
                                 + [jnp.zeros((3 * (w // NDEV),), F32)]).reshape(16, w // NDEV)
    mod_all, s_mat, small_all = _mod_forward(
        jnp.broadcast_to(c, (8, d)), jnp.broadcast_to(c_ctx[None], (8, d)), w_ada2, small_mine)
    mod = jnp.transpose(mod_all, (1, 0, 2)).reshape(16, NDEV * cols) + b_ada
    mod_lat = lax.dynamic_slice_in_dim(mod, me, 1, axis=0)
    sh_l, sc_l, gt_l = jnp.split(mod_lat, 3, axis=-1)
    sh_c, sc_c, _ = jnp.split(mod[8:9], 3, axis=-1)
    small = jnp.transpose(small_all, (1, 0, 2)).reshape(16, w)
    wca = _rows8(*[small[j] for j in range(0, 3)])
    wcb = _rows8(*[small[j] for j in range(3, 7)], b_conv_b)
    lv = _rows8(0.5 * small[7], 0.5 * small[9], small[11], 0.5 * small[8], 0.5 * small[10], small[12])
    wg = jnp.stack([
        jnp.concatenate([_blockdiag_groups(lru_wa[0, dr], gc), _blockdiag_groups(lru_wx[0, dr], gc)], axis=-1)
        for dr in range(2)])
    wg = (0.5 * wg).astype(BF16)

    la = l + lc
    tm = 2 * t if l % (2 * t) == 0 else t
    tk = 3 * t if la % (3 * t) == 0 else t
    h = _normalize(x2, _rows8(norm_g, sc_l, sh_l), la, 0, tm, "normalize")
    h = _normalize(ctx2, _rows8(norm_g, sc_c, sh_c), la, l, t, "normalize_ctx", prev=h)
    p, w_all, wo_all = _in_projection(h, w_in2.astype(BF16), w_out2.astype(BF16), la // 4 if la % 64 == 0 else tk)
    taps_m, back_m, perm = _scan_matrices(t)
    xb = _conv_input(p, wcb, taps_m, l, t)
    hf, hr = _lru_forward(xb, wg, lv, l, t)
    wo = wo_all.reshape(d, d)
    dn, cat, dout, part_mix = _mix_forward(x2, tgt2, p, hf, hr, wo, _rows8(gt_l, final_g), wca, perm, t)
    g_wout = _weight_grad_t(cat, dout, 2, 1, _largest_tile(l, 2048), "grad_w_out")
    dp, dhs, part_ca = _mix_backward(dout, p, hf, hr, wo, wca, perm, l, t)
    dxb0, dwg0, part_l0 = _lru_backward(0, xb, dhs, hf, wg, lv, l, t)
    dp, dwg1, part_l1, sc_wout = _lru_backward(
        1, xb, dhs, hr, wg, lv, l, t, conv=(p, wcb, back_m, dxb0, dp, g_wout.reshape(NDEV, wo_rows, d)))
    sc_win = _weight_grad_scatter(h, dp, tk, "grad_w_in")
    grad_x, part_lat = _input_backward(dp, w_all, x2, _rows8(norm_g, sc_l), 0, tm, 2, "input_backward", dn=dn)
    (part_ctx,) = _input_backward(dp, w_all, ctx2, _rows8(norm_g, sc_c), l, t, 2, "input_backward_ctx",
                                  cols=(4 * w, 5 * w - 1))
    part_in = jnp.concatenate([part_lat[0:2], part_ctx[0:2], (part_lat[2] + part_ctx[2])[None]], axis=0)

    dwa = jnp.stack([_blockdiag_extract(dwg0[:, :, :gc], dh), _blockdiag_extract(dwg1[:, :, :gc], dh)])
    dwx = jnp.stack([_blockdiag_extract(dwg0[:, :, gc:], dh), _blockdiag_extract(dwg1[:, :, gc:], dh)])
    lru_part = (0.5 * jnp.stack([dwa, dwx])).reshape(NDEV, -1, 128)
    zeros_d = jnp.zeros((d,), F32)
    pieces = [
        jnp.concatenate([part_in[0], part_in[1], part_mix[1]]),
        jnp.concatenate([part_in[2], part_in[3], zeros_d]),
        part_in[4], part_mix[0], part_ca[0:3], part_l1[4:8], part_l1[3],
        0.5 * jnp.stack([part_l0[0], part_l1[0]]), 0.5 * jnp.stack([part_l0[1], part_l1[1]]),
        jnp.stack([part_l0[2], part_l1[2]]), part_mix[2, 0:1],
    ]
    shapes = [(3 * d,), (3 * d,), (d,), (d,), (3, w), (4, w), (w,), (2, w), (2, w), (2, w), (1,)]
    sig_cc = jax.nn.sigmoid(c_ctx)
    dsilu_cc = jnp.broadcast_to((sig_cc * (1.0 + c_ctx * (1.0 - sig_cc)))[None], (8, d))
    psum, pall, lru_sum, g_cctx8 = _reduce_small(_pack(pieces), lru_part, w_ada2, dsilu_cc)
    (g_modl, g_modc, g_norm, g_final, g_ca, g_cb, g_bcb, g_ba, g_bx, g_lam, loss1) = _unpack(psum, shapes)
    loss = loss1[0]
    g_cctx = g_cctx8[0]
    g_bada = (g_modl + g_modc)[None]
    g_lru = lru_sum.reshape(2, 2, N_HEADS, dh, dh)
    g_wa, g_wx = g_lru[0][None], g_lru[1][None]
    wsl = w // NDEV
    mine = lambda a: lax.dynamic_slice_in_dim(a, me * wsl, wsl, axis=-1)
    g_ca_m, g_cb_m, g_ba_m, g_bx_m, g_lam_m = (mine(g_ca)[None], mine(g_cb)[None], mine(g_ba)[None],
                                               mine(g_bx)[None], mine(g_lam)[None])
    g_norm, g_bcb = g_norm[None], g_bcb[None]

    per_dev = pall[:, :3 * d // 128].reshape(NDEV, NDEV, cols)
    dmod_lat = lax.dynamic_slice_in_dim(per_dev, me, 1, axis=1)[:, 0]
    dmod_ctx = lax.dynamic_slice_in_dim(g_modc.reshape(NDEV, cols), me, 1, axis=0)
    dmod16 = jnp.concatenate([dmod_lat, dmod_ctx, jnp.zeros((7, cols), F32)], axis=0)
    tr_ada = 256 if d % 256 == 0 else d
    g_wada, d_wada, m_wada, v_wada = _adamw_ada(s_mat.T, dmod16, w_ada2, m_w_ada[0], v_w_ada[0], tr_ada)
    g_win2, d_win, m_win, v_win = _adamw_scattered(sc_win, w_in2, m_w_in[0], v_w_in[0], tr_ada)
    tr_out = 64 if wo_rows % 64 == 0 else wo_rows
    g_wout2, d_wout, m_wout, v_wout = _adamw_scattered(sc_wout, w_out2, m_w_out[0], v_w_out[0], tr_out)

    small_w = [c_ctx, norm_g, b_ada, w_conv_a, w_conv_b, b_conv_b, lru_wa, lru_ba, lru_wx, lru_bx, lru_lambda, final_g]
    small_m = [m_c_ctx, m_norm_g, m_b_ada, m_w_conv_a, m_w_conv_b, m_b_conv_b, m_lru_wa, m_lru_ba, m_lru_wx,
               m_lru_bx, m_lru_lambda, m_final_g]
    small_v = [v_c_ctx, v_norm_g, v_b_ada, v_w_conv_a, v_w_conv_b, v_b_conv_b, v_lru_wa, v_lru_ba, v_lru_wx,
               v_lru_bx, v_lru_lambda, v_final_g]
    small_g = [g_cctx, g_norm, g_bada, g_ca_m, g_cb_m, g_bcb, g_wa, g_ba_m, g_wx, g_bx_m, g_lam_m, g_final]
    small_g = [jnp.reshape(a, b.shape) for a, b in zip(small_g, small_w)]
    d_s, m_s, v_s = _adamw_small(small_g, small_w, small_m, small_v)

    def weights(small_list, ada, win, wout):
        (cctx_, norm_, bada_, ca_, cb_, bcb_, wa_, ba_, wx_, bx_, lam_, final_) = small_list
        return [cctx_, norm_, ada[None], bada_, win[None], ca_, cb_, bcb_, wa_, ba_, wx_, bx_, lam_, wout[None], final_]

    return (loss, grad_x[None],
            *weights(small_g, g_wada, g_win2, g_wout2), *weights(d_s, d_wada, d_win, d_wout),
            *weights(m_s, m_wada, m_win, m_wout), *weights(v_s, v_wada, v_win, v_wout))
```

```python
import functools

import jax
import jax.numpy as jnp
import numpy as np
from jax import lax
from jax.experimental import pallas as pl
from jax.experimental.pallas import tpu as pltpu

F32 = jnp.float32
BF16 = jnp.bfloat16
MESH = pl.DeviceIdType.MESH
NDEV = 8
GRID_W = 64
N_HEADS = 16
LRU_C = 8.0
EPS = 1e-6
MXU_WIDTH = 256
VMEM_LIMIT = 60 * 1024 * 1024

ADAM_LR = 0.001
ADAM_B1 = 0.9
ADAM_B2 = 0.999
ADAM_EPS = 1e-08
ADAM_WD = 0.01
ADAM_STEP = 10
ADAM_C1 = 1.0 - ADAM_B1 ** ADAM_STEP
ADAM_C2 = 1.0 - ADAM_B2 ** ADAM_STEP

HIGHEST = lax.Precision.HIGHEST
ANY = pl.BlockSpec(memory_space=pl.ANY)
VMEM = pl.BlockSpec(memory_space=pltpu.VMEM)


def _call(body, **kw):
    return pl.pallas_call(body, **kw)


def _params(sem=None, vmem=VMEM_LIMIT):
    return pltpu.CompilerParams(dimension_semantics=sem, vmem_limit_bytes=vmem)


def _my_pos():
    return lax.axis_index("x"), lax.axis_index("y"), lax.axis_index("c")


def _idx(pos):
    return 4 * pos[0] + 2 * pos[1] + pos[2]


def _peer(k):
    x, y, c = _my_pos()
    return ((1 - x) if (k >> 2) & 1 else x, (1 - y) if (k >> 1) & 1 else y, (1 - c) if k & 1 else c)


def _exchange_start(src_ref, dst_ref, send_sems, recv_sems, base):
    me = _idx(_my_pos())
    sends = []
    for k in range(1, NDEV):
        cp = pltpu.make_async_remote_copy(
            src_ref=src_ref, dst_ref=dst_ref.at[me], send_sem=send_sems.at[base + k - 1],
            recv_sem=recv_sems.at[base + k - 1], device_id=_peer(k), device_id_type=MESH)
        cp.start()
        sends.append(cp)
    dst_ref[me] = src_ref[...]
    return sends, (src_ref, dst_ref, send_sems, recv_sems, base)


def _exchange_finish(started):
    sends, (src_ref, dst_ref, send_sems, recv_sems, base) = started
    for k in range(1, NDEV):
        peer = _peer(k)
        pltpu.make_async_remote_copy(
            src_ref=src_ref, dst_ref=dst_ref.at[_idx(peer)], send_sem=send_sems.at[base + k - 1],
            recv_sem=recv_sems.at[base + k - 1], device_id=peer, device_id_type=MESH).wait_recv()
    for cp in sends:
        cp.wait_send()


def _exchange_vmem(src_ref, dst_ref, send_sems, recv_sems, base):
    _exchange_finish(_exchange_start(src_ref, dst_ref, send_sems, recv_sems, base))


def _sigmoid(z):
    return 0.5 * jnp.tanh(0.5 * z) + 0.5


def _softplus(x):
    return jnp.maximum(x, 0.0) + jnp.log1p(jnp.exp(-jnp.abs(x)))


def _one_minus_sq(a, la):
    series = (-2.0 * la) * (1.0 + la)
    return jnp.where(la > -0.0015, series, 1.0 - a * a)


def _dot(a, b):
    return jnp.dot(a, b, preferred_element_type=F32)


def _dot_nt(a, b):
    return lax.dot_general(a, b, (((1,), (1,)), ((), ())), preferred_element_type=F32)


def _rows(shape):
    return lax.broadcasted_iota(jnp.int32, shape, 0)


def _scan_matrices(t):
    seg = t // 8
    r = np.arange(t)
    perm = (np.arange(t)[None, :] == ((r % 8) * seg + r // 8)[:, None]).astype(np.float32)
    rows, cols = r[:, None], r[None, :]
    taps, back = [], []
    for rowlen in (GRID_W, t):
        pos = rows % rowlen
        shift = {-2: (cols == rows - 2) & (pos >= 2), -1: (cols == rows - 1) & (pos >= 1),
                 0: cols == rows, 1: (cols == rows + 1) & (pos + 1 < rowlen),
                 2: (cols == rows + 2) & (pos + 2 < rowlen)}
        if rowlen == GRID_W:
            beside = [shift[-1].astype(np.float32), shift[1].astype(np.float32)]
        taps.append(np.stack([perm @ shift[k].astype(np.float32) for k in (-2, -1, 0, 1)]))
        back.append(np.stack([shift[k].astype(np.float32) @ perm.T for k in (2, 1, 0, -1)]))
    as_bf16 = lambda a: jnp.asarray(a, dtype=BF16)
    return as_bf16(np.stack(taps)), as_bf16(np.stack(back)), as_bf16(np.stack([perm, perm.T] + beside))


def _chunk_scan(a, b, reverse):
    row = _rows(a.shape)
    for s in (1, 2, 4):
        if reverse:
            m = row < 8 - s
            sh = 8 - s
        else:
            m = row >= s
            sh = s
        a_s = jnp.where(m, pltpu.roll(a, sh, 0), 1.0)
        b_s = jnp.where(m, pltpu.roll(b, sh, 0), 0.0)
        b = b + a * b_s
        a = a * a_s
    return a, b


def _chain_segments(ptot, hend, carry, reverse):
    ca, cb = _chunk_scan(ptot, hend, reverse)
    incl = ca * carry + cb
    r8 = _rows(incl.shape)
    if reverse:
        start = jnp.where(r8 < 7, pltpu.roll(incl, 7, 0), carry)
        last = incl[0:1, :]
    else:
        start = jnp.where(r8 >= 1, pltpu.roll(incl, 1, 0), carry)
        last = incl[7:8, :]
    return start, jnp.broadcast_to(last, incl.shape)


def _blocks(nblock, reverse):
    order = range(nblock - 1, -1, -1) if reverse else range(nblock)
    return [slice(8 * k, 8 * k + 8) for k in order]


def _scan_tile(a_ref, b_ref, out_ref, carry, reverse):
    t, w = a_ref.shape
    seg = t // 8

    hend, ptot = jnp.zeros((8, w), F32), jnp.ones((8, w), F32)
    for rows in _blocks(seg, reverse):
        a = a_ref[rows, :]
        hend, ptot = a * hend + b_ref[rows, :], a * ptot
    h, new_carry = _chain_segments(ptot, hend, carry, reverse)
    for rows in _blocks(seg, reverse):
        h = a_ref[rows, :] * h + b_ref[rows, :]
        out_ref[rows, :] = h
    return new_carry


def _scan_tile_backward(a_ref, dh_ref, g_ref, carry, reverse):
    t, w = a_ref.shape
    seg = t // 8

    uend, ptot = jnp.zeros((8, w), F32), jnp.ones((8, w), F32)
    for rows in _blocks(seg, reverse):
        a = a_ref[rows, :]
        uend, ptot = a * (dh_ref[rows, :] + uend), a * ptot
    u, new_carry = _chain_segments(ptot, uend, carry, reverse)
    for rows in _blocks(seg, reverse):
        g = dh_ref[rows, :] + u
        g_ref[rows, :] = g
        u = a_ref[rows, :] * g
    return new_carry


def _lru_coef(xb, wg_ref, d, ba, bx, lam, gc):
    w = xb.shape[1]
    xb16 = xb.astype(BF16)
    zr, zi = [], []
    for g in range(w // gc):
        z = _dot(xb16[:, g * gc:(g + 1) * gc], wg_ref[d, g])
        zr.append(z[:, :gc])
        zi.append(z[:, gc:])
    zr = zr[0] if len(zr) == 1 else jnp.concatenate(zr, axis=-1)
    zi = zi[0] if len(zi) == 1 else jnp.concatenate(zi, axis=-1)
    tr = jnp.tanh(zr + ba)
    ti = jnp.tanh(zi + bx)
    sp = _softplus(-lam)
    half = -0.5 * LRU_C * sp
    la = tr * half + half
    a = jnp.exp(la)
    q = _one_minus_sq(a, la)
    rs = lax.rsqrt(jnp.maximum(q, 1e-30))
    return a, q * rs, rs, tr, ti, sp


def _adamw(w, g, m, v):
    m2 = ADAM_B1 * m + (1.0 - ADAM_B1) * g
    v2 = ADAM_B2 * v + (1.0 - ADAM_B2) * (g * g)
    m_hat = m2 / ADAM_C1
    v_hat = v2 / ADAM_C2
    delta = -ADAM_LR * (m_hat / (jnp.sqrt(v_hat) + ADAM_EPS) + ADAM_WD * w)
    return delta, m2, v2


def _mod_forward(c8, cctx8, w_ada, small):
    d = c8.shape[1]
    cols = w_ada.shape[1]

    def body(c_ref, cctx_ref, w_ref, sm_ref, mod_ref, s_ref, sm_all, cbuf, mod_my, send_sems, recv_sems):
        _exchange_vmem(sm_ref, sm_all, send_sems, recv_sems, 2 * (NDEV - 1))
        _exchange_vmem(c_ref, cbuf, send_sems, recv_sems, 0)
        row = _rows((8, d))
        c_all = jnp.zeros((8, d), F32)
        for b in range(NDEV):
            c_all = jnp.where(row == b, cbuf[b], c_all)
        cc = cctx_ref[...]
        s_top = c_all * _sigmoid(c_all)
        s_bot = jnp.where(row == 0, cc * _sigmoid(cc), 0.0)
        s = jnp.concatenate([s_top, s_bot], axis=0)
        s_ref[...] = s
        mod_my[...] = jnp.dot(s, w_ref[...], precision=HIGHEST, preferred_element_type=F32)
        _exchange_vmem(mod_my, mod_ref, send_sems, recv_sems, NDEV - 1)

    return _call(
        body, name="mod_forward",
        out_shape=(jax.ShapeDtypeStruct((NDEV, 16, cols), F32), jax.ShapeDtypeStruct((16, d), F32),
                   jax.ShapeDtypeStruct((NDEV,) + small.shape, F32)),
        in_specs=[VMEM] * 4, out_specs=(VMEM,) * 3,
        scratch_shapes=[pltpu.VMEM((NDEV, 8, d), F32), pltpu.VMEM((16, cols), F32),
                        pltpu.SemaphoreType.DMA((3 * (NDEV - 1),)), pltpu.SemaphoreType.DMA((3 * (NDEV - 1),))],
        compiler_params=_params(),
    )(c8, cctx8, w_ada, small)


def _scatter_copies(src_ref, dst_ref, send_sems, recv_sems):
    me = _idx(_my_pos())
    copies = [pltpu.make_async_copy(src_ref.at[me], dst_ref.at[0], send_sems.at[0])]
    for k in range(1, NDEV):
        peer = _peer(k)
        copies.append(pltpu.make_async_remote_copy(
            src_ref=src_ref.at[_idx(peer)], dst_ref=dst_ref.at[k], send_sem=send_sems.at[k],
            recv_sem=recv_sems.at[k], device_id=peer, device_id_type=MESH))
    return copies


def _gather_copies(src_ref, dst_ref, send_sems, recv_sems):
    me = _idx(_my_pos())
    sends = [pltpu.make_async_copy(src_ref, dst_ref.at[me], send_sems.at[0])]
    arrivals = []
    for k in range(1, NDEV):
        peer = _peer(k)
        sends.append(pltpu.make_async_remote_copy(
            src_ref=src_ref, dst_ref=dst_ref.at[me], send_sem=send_sems.at[k],
            recv_sem=recv_sems.at[k], device_id=peer, device_id_type=MESH))
        arrivals.append(pltpu.make_async_remote_copy(
            src_ref=src_ref, dst_ref=dst_ref.at[_idx(peer)], send_sem=send_sems.at[k],
            recv_sem=recv_sems.at[k], device_id=peer, device_id_type=MESH))
    return sends, arrivals


def _exchange_wait(sends, arrivals):
    sends[0].wait()
    for cp in arrivals:
        cp.wait_recv()
    for cp in sends[1:]:
        cp.wait_send()


def _chip_order(k, c):
    return (6, 4 - 2 * c, 2 + 2 * c, 0)[k]


def _scatter_order(s, c):
    k = s >> 1
    mine = jnp.where(k == 0, 6, jnp.where(k == 1, 4 - 2 * c, jnp.where(k == 2, 2 + 2 * c, 0)))
    theirs = jnp.where(k == 0, 6, jnp.where(k == 1, 2 + 2 * c, jnp.where(k == 2, 4 - 2 * c, 0))) ^ 1
    return jnp.where((s & 1) == 0, theirs, mine)


def _peer_at(dist):
    x, y, c = _my_pos()
    return (x ^ ((dist >> 2) & 1), y ^ ((dist >> 1) & 1), c ^ (dist & 1))


def _reduce_small(packed, w_ada, dsilu_cctx):
    rp = packed.shape[0]
    d, cols = w_ada.shape
    assert cols % 128 == 0
    cb = cols // 128

    def body(p_ref, w_ref, ds_ref, sum_ref, all_ref, cctx_ref, cpart, call, send_sems, recv_sems):
        me = _idx(_my_pos())
        _exchange_vmem(p_ref, all_ref, send_sems, recv_sems, 0)
        acc = all_ref[0]
        for j in range(1, NDEV):
            acc = acc + all_ref[j]
        sum_ref[...] = acc
        part = jnp.zeros((8, d), F32)
        for q in range(cb):
            dm = jnp.broadcast_to(sum_ref[pl.ds((NDEV + me) * cb + q, 1), :], (8, 128))
            part = part + lax.dot_general(dm, w_ref[:, q * 128:(q + 1) * 128],
                                          (((1,), (1,)), ((), ())), precision=HIGHEST,
                                          preferred_element_type=F32)
        cpart[...] = part
        _exchange_vmem(cpart, call, send_sems, recv_sems, NDEV - 1)
        tot = call[0]
        for j in range(1, NDEV):
            tot = tot + call[j]
        cctx_ref[...] = tot * ds_ref[...]

    return _call(
        body, name="reduce_small",
        out_shape=(jax.ShapeDtypeStruct((rp, 128), F32), jax.ShapeDtypeStruct((NDEV, rp, 128), F32),
                   jax.ShapeDtypeStruct((8, d), F32)),
        in_specs=[VMEM] * 3, out_specs=(VMEM,) * 3,
        scratch_shapes=[pltpu.VMEM((8, d), F32), pltpu.VMEM((NDEV, 8, d), F32),
                        pltpu.SemaphoreType.DMA((2 * (NDEV - 1),)), pltpu.SemaphoreType.DMA((2 * (NDEV - 1),))],
        compiler_params=_params(),
    )(packed, w_ada, dsilu_cctx)


def _normalize(src, mv, la, row0, tm, name, prev=None):
    rows, d = src.shape
    blk0 = row0 // tm

    def body(*refs):
        x_ref, mv_ref, h_ref = refs[0], refs[1], refs[-1]
        xf = x_ref[...]
        r = lax.rsqrt(jnp.mean(xf * xf, axis=-1, keepdims=True) + EPS)
        h = xf * r * (mv_ref[0:1, :] * (1.0 + mv_ref[1:2, :])) + mv_ref[2:3, :]
        h_ref[...] = h.astype(BF16)

    in_specs = [pl.BlockSpec((tm, d), lambda i: (i, 0)), pl.BlockSpec((8, d), lambda i: (0, 0))]
    args = [src, mv]
    aliases = {}
    if prev is not None:
        in_specs += [ANY]
        args += [prev]
        aliases = {2: 0}
    return _call(
        body, name=name,
        grid=(rows // tm,),
        out_shape=jax.ShapeDtypeStruct((la, d), BF16),
        in_specs=in_specs,
        out_specs=pl.BlockSpec((tm, d), lambda i: (blk0 + i, 0)),
        input_output_aliases=aliases,
        compiler_params=_params(("arbitrary",)),
    )(*args)


def _gather_order(step):
    return (step & 1) | (((step >> 2) & 1) << 1) | (((step >> 1) & 1) << 2)


def _in_projection(h, w_shard, wo_shard, tm):
    la, d = h.shape
    bw = w_shard.shape[1]
    ni = la // tm
    where = jnp.reshape(_idx(_my_pos()), (1,)).astype(jnp.int32)

    def body(me_ref, h_ref, w_ref, wo_ref, p_ref, all_ref, wo_all, wbuf, send_sems, recv_sems, local_sems,
             wo_send, wo_recv):
        s, i = pl.program_id(0), pl.program_id(1)
        x, y, c = _my_pos()
        wo_sends, wo_arrivals = _gather_copies(wo_ref, wo_all, wo_send, wo_recv)

        @pl.when((s == NDEV // 2) & (i == 0))
        def _():
            for cp in wo_sends:
                cp.start()

        me, sibling = (x, y, c), (x, y, 1 - c)
        chips = [(1 - x, y), (x, 1 - y), (1 - x, 1 - y)]

        def copy(k, block, to, from_shard=False):
            return pltpu.make_async_remote_copy(
                src_ref=w_ref if from_shard else all_ref.at[_idx(block)], dst_ref=all_ref.at[_idx(block)],
                send_sem=send_sems.at[k], recv_sem=recv_sems.at[k], device_id=to, device_id_type=MESH)

        def load(block, slot):
            return pltpu.make_async_copy(all_ref.at[_idx(block)], wbuf.at[slot], local_sems.at[1])

        keep = pltpu.make_async_copy(w_ref, all_ref.at[_idx(me)], local_sems.at[0])
        first = [copy(0, me, sibling, True)] + [copy(1 + j, me, (*chip, c), True) for j, chip in enumerate(chips)]
        passed = [copy(4 + j, (*chip, c), sibling) for j, chip in enumerate(chips)]
        steps = [(copy(0, sibling, me), None, sibling)]
        for j, chip in enumerate(chips):
            steps.append((copy(1 + j, (*chip, c), me), passed[j], (*chip, c)))
            steps.append((copy(4 + j, (*chip, 1 - c), me), None, (*chip, 1 - c)))

        @pl.when((s == 0) & (i == 0))
        def _():
            keep.start()
            mine = pltpu.make_async_copy(w_ref, wbuf.at[0], local_sems.at[1])
            mine.start()
            for cp in first:
                cp.start()
            mine.wait()

        for n, (arrival, forward, block) in enumerate(steps, start=1):
            @pl.when((s == n - 1) & (i == ni - 1))
            def _(arrival=arrival, forward=forward, block=block, n=n):
                arrival.wait_recv()
                if forward is not None:
                    forward.start()
                load(block, n % 2).start()

        @pl.when((s > 0) & (i == 0))
        def _():
            load(me, s % 2).wait()

        p_ref[...] = _dot(h_ref[...], wbuf[s % 2]).astype(BF16)

        @pl.when((s == NDEV - 1) & (i == ni - 1))
        def _():
            for cp in first + passed:
                cp.wait_send()
            keep.wait()
            _exchange_wait(wo_sends, wo_arrivals)

    return _call(
        body, name="in_projection",
        grid_spec=pltpu.PrefetchScalarGridSpec(
            num_scalar_prefetch=1, grid=(NDEV, ni),
            in_specs=[pl.BlockSpec((tm, d), lambda s, i, me_ref: (i, 0)), ANY, ANY],
            out_specs=(pl.BlockSpec((tm, bw), lambda s, i, me_ref: (i, me_ref[0] ^ _gather_order(s))), ANY, ANY),
            scratch_shapes=[pltpu.VMEM((2, d, bw), BF16), pltpu.SemaphoreType.DMA((7,)),
                            pltpu.SemaphoreType.DMA((7,)), pltpu.SemaphoreType.DMA((2,)),
                            pltpu.SemaphoreType.DMA((NDEV,)), pltpu.SemaphoreType.DMA((NDEV,))]),
        out_shape=(jax.ShapeDtypeStruct((la, NDEV * bw), BF16), jax.ShapeDtypeStruct((NDEV, d, bw), BF16),
                   jax.ShapeDtypeStruct((NDEV,) + wo_shard.shape, wo_shard.dtype)),
        compiler_params=_params(("arbitrary", "arbitrary")),
    )(where, h, w_shard, wo_shard)


def _conv_input(p, wcb, taps_m, l, t):
    la = p.shape[0]
    w = wcb.shape[1]
    nt = l // t

    def body(v_ref, wcb_ref, tm_ref, xb_ref):
        taps = _dot(tm_ref[...].reshape(4 * t, t), v_ref[...])
        xb = wcb_ref[4:5, :] + wcb_ref[0:1, :] * taps[0:t]
        for j in range(1, 4):
            xb = xb + wcb_ref[j:j + 1, :] * taps[j * t:(j + 1) * t]
        xb_ref[...] = xb

    return _call(
        body, name="conv_input",
        grid=(nt + 1,),
        out_shape=jax.ShapeDtypeStruct((la, w), F32),
        in_specs=[pl.BlockSpec((t, w), lambda i: (i, 4)), pl.BlockSpec((8, w), lambda i: (0, 0)),
                  pl.BlockSpec((None, 4, t, t), lambda i: (i // nt, 0, 0, 0))],
        out_specs=pl.BlockSpec((t, w), lambda i: (i, 0)),
        compiler_params=_params(("arbitrary",)),
    )(p, wcb, taps_m)


def _lru_forward(xb, wg, lv, l, t):
    la, w = xb.shape
    gc = wg.shape[2]
    nt = l // t

    def body(xf_ref, xr_ref, wg_ref, lv_ref, hf_ref, hr_ref, a_s, b_s, carry):
        @pl.when(pl.program_id(0) == 0)
        def _():
            carry[...] = jnp.zeros_like(carry)

        for dr, (x_ref, h_ref) in enumerate(((xf_ref, hf_ref), (xr_ref, hr_ref))):
            x = x_ref[...]
            a, s, _, _, ti, _ = _lru_coef(x, wg_ref, dr, lv_ref[3 * dr:3 * dr + 1, :],
                                          lv_ref[3 * dr + 1:3 * dr + 2, :], lv_ref[3 * dr + 2:3 * dr + 3, :], gc)
            a_s[...] = a
            b_s[...] = (s * x) * (0.5 * ti + 0.5)
            carry[dr] = _scan_tile(a_s, b_s, h_ref, carry[dr], dr == 1)

    full = lambda shape: pl.BlockSpec(shape, lambda i: (0,) * len(shape))
    fmap = lambda i: (jnp.where(i == 0, nt, i - 1), 0)
    rmap = lambda i: (jnp.where(i == 0, nt, nt - i), 0)
    return _call(
        body, name="lru_forward",
        grid=(nt + 1,),
        out_shape=(jax.ShapeDtypeStruct((la, w), F32), jax.ShapeDtypeStruct((la, w), F32)),
        in_specs=[pl.BlockSpec((t, w), fmap), pl.BlockSpec((t, w), rmap), full(wg.shape), full(lv.shape)],
        out_specs=(pl.BlockSpec((t, w), fmap), pl.BlockSpec((t, w), rmap)),
        scratch_shapes=[pltpu.VMEM((t, w), F32), pltpu.VMEM((t, w), F32), pltpu.VMEM((2, 8, w), F32)],
        compiler_params=_params(("arbitrary",)),
    )(xb, xb, wg, lv)


def _mix_gates(p_refs, hf_ref, hr_ref, wca_ref, perm_ref, t, w):
    bl, cl, ul, gl, ql = [r[...].astype(F32) for r in p_refs]
    tt = cl * ul
    tt16 = tt.astype(BF16)
    beside = _dot(perm_ref[2:4].reshape(2 * t, t), tt16)
    before, after = beside[:t], beside[t:]
    z = wca_ref[0:1, :] * before + wca_ref[1:2, :] * tt + wca_ref[2:3, :] * after
    sig_g = _sigmoid(gl)
    sig_q = _sigmoid(ql)
    ylru = _dot(perm_ref[1], (hf_ref[...] + hr_ref[...]).astype(BF16))
    return bl, cl, ul, gl, ql, (before, tt, after), z, sig_g, sig_q, ylru


def _p_specs(t, w, nt):
    return [pl.BlockSpec((t, w), functools.partial(lambda i, s: (jnp.minimum(i, nt - 1), s), s=s))
            for s in (0, 1, 2, 3, 5)]


def _mix_forward(x, tgt, p, hf, hr, wo, ov, wca, perm, t):
    l, d = x.shape
    w = d // 2
    nt = l // t

    def body(x_ref, tg_ref, b_ref, c_ref, u_ref, g_ref, q_ref, hf_ref, hr_ref, wo_ref, ov_ref, wca_ref, perm_ref,
             dn_ref, ct_ref, do_ref, part_ref):
        i = pl.program_id(0)
        bl, _, _, gl, ql, _, z, sig_g, sig_q, ylru = _mix_gates(
            (b_ref, c_ref, u_ref, g_ref, q_ref), hf_ref, hr_ref, wca_ref, perm_ref, t, w)
        ya = bl * z * (gl * sig_g)
        yb = ylru * (ql * sig_q)
        ct_ref[:, 0:w] = ya.astype(BF16)
        ct_ref[:, w:] = yb.astype(BF16)
        out = _dot(ya.astype(BF16), wo_ref[0:w, :]) + _dot(yb.astype(BF16), wo_ref[w:, :])
        gate, fg = ov_ref[0:1, :], ov_ref[1:2, :]
        n = x_ref[...] + gate * out
        rr = lax.rsqrt(jnp.mean(n * n, axis=-1, keepdims=True) + EPS)
        nh = n * rr
        e = nh * fg - tg_ref[...]
        loss = 0.5 * jnp.sum(jnp.mean(e * e, axis=-1, keepdims=True), axis=0, keepdims=True)
        dy = e * (1.0 / d)
        dnh = dy * fg
        dn = rr * (dnh - nh * jnp.mean(dnh * nh, axis=-1, keepdims=True))
        dn_ref[...] = dn.astype(BF16)
        do_ref[...] = (dn * gate).astype(BF16)

        @pl.when(i == 0)
        def _():
            part_ref[...] = jnp.zeros_like(part_ref)

        part_ref[0:1, :] += jnp.sum(dy * nh, axis=0, keepdims=True)
        part_ref[1:2, :] += jnp.sum(dn * out, axis=0, keepdims=True)
        part_ref[2:3, :] += jnp.broadcast_to(loss, (1, d))

    tile = lambda cols: pl.BlockSpec((t, cols), lambda i: (i, 0))
    full = lambda shape: pl.BlockSpec(shape, lambda i: (0,) * len(shape))
    return _call(
        body, name="mix_forward",
        grid=(nt,),
        out_shape=(jax.ShapeDtypeStruct((l, d), BF16), jax.ShapeDtypeStruct((l, d), BF16),
                   jax.ShapeDtypeStruct((l, d), BF16), jax.ShapeDtypeStruct((8, d), F32)),
        in_specs=[tile(d), tile(d)] + _p_specs(t, w, nt) + [tile(w), tile(w),
                  pl.BlockSpec((d, d), lambda i: (0, 0), pipeline_mode=pl.Buffered(1)),
                  full(ov.shape), full(wca.shape), full(perm.shape)],
        out_specs=(tile(d), tile(d), tile(d), full((8, d))),
        compiler_params=_params(("arbitrary",)),
    )(x, tgt, p, p, p, p, p, hf, hr, wo, ov, wca, perm)


def _mix_backward(dout, p, hf, hr, wo, wca, perm, l, t):
    d = dout.shape[1]
    w = d // 2
    nt = l // t
    la = p.shape[0]

    def body(do_ref, b_ref, c_ref, u_ref, g_ref, q_ref, hf_ref, hr_ref, wo_ref, wca_ref, perm_ref,
             dp_ref, dh_ref, part_ref):
        i = pl.program_id(0)

        @pl.when(i == 0)
        def _():
            part_ref[...] = jnp.zeros_like(part_ref)

        @pl.when(i == nt)
        def _():
            dp_ref[...] = jnp.zeros_like(dp_ref)

        @pl.when(i < nt)
        def _():
            bl, cl, ul, gl, ql, taps, z, sig_g, sig_q, ylru = _mix_gates(
                (b_ref, c_ref, u_ref, g_ref, q_ref), hf_ref, hr_ref, wca_ref, perm_ref, t, w)
            do = do_ref[...]
            dya = _dot_nt(do, wo_ref[0:w, :])
            dyb = _dot_nt(do, wo_ref[w:, :])
            sg = gl * sig_g
            dz = dya * bl * sg
            dz16 = dz.astype(BF16)
            beside = _dot(perm_ref[2:4].reshape(2 * t, t), dz16)
            dt = wca_ref[0:1, :] * beside[t:] + wca_ref[1:2, :] * dz + wca_ref[2:3, :] * beside[:t]
            dp_ref[:, 0:w] = (dya * z * sg).astype(BF16)
            dp_ref[:, w:2 * w] = (dt * ul).astype(BF16)
            dp_ref[:, 2 * w:3 * w] = (dt * cl).astype(BF16)
            dp_ref[:, 3 * w:4 * w] = (dya * bl * z * (sig_g * (1.0 + gl * (1.0 - sig_g)))).astype(BF16)
            dp_ref[:, 4 * w:5 * w] = jnp.zeros((t, w), BF16)
            dp_ref[:, 5 * w:6 * w] = (dyb * ylru * (sig_q * (1.0 + ql * (1.0 - sig_q)))).astype(BF16)
            dh_ref[...] = _dot(perm_ref[0], (dyb * (ql * sig_q)).astype(BF16)).astype(BF16)
            for j in range(3):
                part_ref[j:j + 1, :] += jnp.sum(dz * taps[j], axis=0, keepdims=True)

    clamp = lambda cols: pl.BlockSpec((t, cols), lambda i: (jnp.minimum(i, nt - 1), 0))
    full = lambda shape: pl.BlockSpec(shape, lambda i: (0,) * len(shape))
    return _call(
        body, name="mix_backward",
        grid=(nt + 1,),
        out_shape=(jax.ShapeDtypeStruct((la, 6 * w), BF16), jax.ShapeDtypeStruct((l, w), BF16),
                   jax.ShapeDtypeStruct((8, w), F32)),
        in_specs=[clamp(d)] + _p_specs(t, w, nt) + [clamp(w), clamp(w),
                  pl.BlockSpec((d, d), lambda i: (0, 0), pipeline_mode=pl.Buffered(1)), full(wca.shape),
                  full(perm.shape)],
        out_specs=(pl.BlockSpec((t, 6 * w), lambda i: (i, 0)), clamp(w), full((8, w))),
        compiler_params=_params(("arbitrary",)),
    )(dout, p, p, p, p, p, hf, hr, wo, wca, perm)


def _lru_backward(direction, xb, dhs, hs, wg, lv, l, t, conv=None):
    la, w = hs.shape
    gc = wg.shape[2]
    ng = w // gc
    nt = l // t
    nblk8 = la // 8
    last = conv is not None
    assert last == (direction == 1)

    if direction == 0:
        tile = lambda i: jnp.where(i == nt, nt, nt - 1 - i)
        halo = lambda i: jnp.where(tile(i) == 0, nblk8 - 1, tile(i) * (t // 8) - 1)
    else:
        tile = lambda i: i
        halo = lambda i: jnp.minimum((i + 1) * (t // 8), nblk8 - 1)

    def body(*refs):
        x_ref, dh_ref, hs_ref, halo_ref, wg_ref, lv_ref = refs[:6]
        i = pl.program_id(0)
        is_ctx = i == nt
        if last:
            v_ref, wcb_ref, bm_ref, dxo_ref, _, gw_ref = refs[6:12]
            out_ref, dwg_ref, part_ref, sc_ref, a_s, dh_s, g_s, carry, send_sems, recv_sems = refs[12:]
            copies = _scatter_copies(gw_ref, sc_ref, send_sems, recv_sems)
        else:
            out_ref, dwg_ref, part_ref, a_s, dh_s, g_s, carry = refs[-7:]
            copies = []

        @pl.when(i == 0)
        def _():
            carry[...] = jnp.zeros_like(carry)
            dwg_ref[...] = jnp.zeros_like(dwg_ref)
            part_ref[...] = jnp.zeros_like(part_ref)
            for cp in copies:
                cp.start()

        if last:
            @pl.when(is_ctx)
            def _():
                _exchange_wait(copies, copies[1:])

        xb = x_ref[...]
        lam = lv_ref[3 * direction + 2:3 * direction + 3, :]
        a, s, rs, tr, ti, sp = _lru_coef(xb, wg_ref, direction, lv_ref[3 * direction:3 * direction + 1, :],
                                         lv_ref[3 * direction + 1:3 * direction + 2, :], lam, gc)
        hs_t = hs_ref[...]
        r8 = _rows((8, w))
        if direction == 0:
            edge = jnp.where(is_ctx, 0.0, halo_ref[7:8, :])
            first = jnp.where(r8 == 0, edge, pltpu.roll(hs_t[t - 8:, :], 1, 0))
            hprev = jnp.concatenate([first, hs_t[:t - 8, :]], axis=0)
        else:
            edge = jnp.where(is_ctx, 0.0, halo_ref[0:1, :])
            final = jnp.where(r8 == 7, edge, pltpu.roll(hs_t[:8, :], 7, 0))
            hprev = jnp.concatenate([hs_t[8:, :], final], axis=0)
        a_s[...] = a
        dh_s[...] = jnp.where(is_ctx, 0.0, dh_ref[...].astype(F32))
        carry[...] = _scan_tile_backward(a_s, dh_s, g_s, carry[...], direction == 0)

        g = g_s[...]
        r = 0.5 * tr + 0.5
        ig = 0.5 * ti + 0.5
        ix = ig * xb
        gs = g * s
        dla = (g * a) * (hprev - ix * (a * rs))
        dxb = gs * ig
        dzr = dla * (r * (1.0 - tr)) * (-LRU_C * sp)
        dzi = gs * ix * (1.0 - ti)
        part_ref[0:1, :] += jnp.sum(dzr, axis=0, keepdims=True)
        part_ref[1:2, :] += jnp.sum(dzi, axis=0, keepdims=True)
        part_ref[2:3, :] += jnp.sum(dla * r, axis=0, keepdims=True) * (LRU_C * _sigmoid(-lam))
        pieces = []
        for gi in range(ng):
            sl = slice(gi * gc, (gi + 1) * gc)
            dz = jnp.concatenate([dzr[:, sl], dzi[:, sl]], axis=-1).astype(BF16)
            pieces.append(_dot_nt(dz, wg_ref[direction, gi]))
            dwg_ref[gi] += _dot(xb[:, sl].T.astype(BF16), dz)
        dxb = dxb + (pieces[0] if ng == 1 else jnp.concatenate(pieces, axis=-1))
        if not last:
            out_ref[...] = dxb
        else:
            dxb = dxb + dxo_ref[...]
            v = v_ref[...].astype(F32)
            backs = _dot(bm_ref[...].reshape(4 * t, t), dxb.astype(BF16))
            dv = jnp.zeros((t, w), F32)
            for j in range(4):
                back = backs[j * t:(j + 1) * t]
                dv = dv + wcb_ref[j:j + 1, :] * back
                part_ref[4 + j:5 + j, :] += jnp.sum(back * v, axis=0, keepdims=True)
            out_ref[...] = dv.astype(BF16)
            part_ref[3:4, :] += jnp.sum(dxb, axis=0, keepdims=True)

    full = lambda shape: pl.BlockSpec(shape, lambda i: (0,) * len(shape))
    kind = lambda i: (jnp.where(i == nt, 1, 0), 0, 0, 0)
    in_specs = [pl.BlockSpec((t, w), lambda i: (tile(i), 0)),
                pl.BlockSpec((t, w), lambda i: (jnp.minimum(tile(i), nt - 1), 0)),
                pl.BlockSpec((t, w), lambda i: (tile(i), 0)),
                pl.BlockSpec((8, w), lambda i: (halo(i), 0)),
                full(wg.shape), full(lv.shape)]
    args = [xb, dhs, hs, hs, wg, lv]
    more_out, more_spec, more_scratch = (), (), []
    if last:
        p, wcb, back_m, dxb_other, dp, g_wout = conv
        in_specs += [pl.BlockSpec((t, w), lambda i: (tile(i), 4)), full(wcb.shape),
                     pl.BlockSpec((None, 4, t, t), kind), pl.BlockSpec((t, w), lambda i: (tile(i), 0)), ANY, ANY]
        args += [p, wcb, back_m, dxb_other, dp, g_wout]
        out0 = jax.ShapeDtypeStruct(dp.shape, dp.dtype)
        spec0 = pl.BlockSpec((t, w), lambda i: (tile(i), 4))
        more_out, more_spec = (jax.ShapeDtypeStruct(g_wout.shape, g_wout.dtype),), (ANY,)
        more_scratch = [pltpu.SemaphoreType.DMA((NDEV,)), pltpu.SemaphoreType.DMA((NDEV,))]
        aliases = {10: 0}
    else:
        out0 = jax.ShapeDtypeStruct((la, w), F32)
        spec0 = pl.BlockSpec((t, w), lambda i: (tile(i), 0))
        aliases = {}
    return _call(
        body, name="lru_backward_%d" % direction,
        grid=(nt + 1,),
        out_shape=(out0, jax.ShapeDtypeStruct((ng, gc, 2 * gc), F32), jax.ShapeDtypeStruct((8, w), F32)) + more_out,
        in_specs=in_specs,
        out_specs=(spec0, full((ng, gc, 2 * gc)), full((8, w))) + more_spec,
        scratch_shapes=[pltpu.VMEM((t, w), F32), pltpu.VMEM((t, w), F32), pltpu.VMEM((t, w), F32),
                        pltpu.VMEM((8, w), F32)] + more_scratch,
        input_output_aliases=aliases,
        compiler_params=_params(("arbitrary",)),
    )(*args)


def _weight_grad_t(a, b, nblk_m, nblk_n, tk, name):
    k, m = a.shape
    n = b.shape[1]
    bm, bn = m // nblk_m, n // nblk_n
    nk = k // tk

    def body(a_ref, b_ref, o_ref, acc):
        kk = pl.program_id(2)

        @pl.when(kk == 0)
        def _():
            acc[...] = jnp.zeros_like(acc)

        acc[...] += lax.dot_general(a_ref[...], b_ref[...], (((0,), (0,)), ((), ())), preferred_element_type=F32)

        @pl.when(kk == nk - 1)
        def _():
            o_ref[...] = acc[...].astype(BF16)

    return _call(
        body, name=name,
        grid=(nblk_m, nblk_n, nk),
        out_shape=jax.ShapeDtypeStruct((nblk_m * nblk_n, bm, bn), BF16),
        in_specs=[pl.BlockSpec((tk, bm), lambda i, j, kk: (kk, i)),
                  pl.BlockSpec((tk, bn), lambda i, j, kk: (kk, j))],
        out_specs=pl.BlockSpec((None, bm, bn), lambda i, j, kk: (i * nblk_n + j, 0, 0)),
        scratch_shapes=[pltpu.VMEM((bm, bn), F32)],
        compiler_params=_params(("arbitrary", "arbitrary", "arbitrary")),
    )(a, b)


def _weight_grad_scatter(at, b, lru_parts, tk, name):
    k, m = at.shape
    n = b.shape[1]
    bn = n // NDEV
    nk = k // tk
    rl = lru_parts.shape[1]
    where = jnp.stack([_idx(_my_pos()), lax.axis_index("c")]).astype(jnp.int32)
    tn = (((0,), (0,)), ((), ()))

    def body(w_ref, a_ref, b_ref, l_ref, recv_ref, lru_ref, acc, sbuf, sib, lbuf, lsum, lall,
             sib_send, sib_recv, chip_send, chip_recv, keep_sem, l_send, l_recv, g_send, g_recv):
        s, kk = pl.program_id(0), pl.program_id(1)
        x, y, c = _my_pos()
        scattered = _scatter_copies(l_ref, lbuf, l_send, l_recv)
        gathered, arrivals = _gather_copies(lsum, lall, g_send, g_recv)

        @pl.when((s == 0) & (kk == 0))
        def _():
            for cp in scattered:
                cp.start()

        @pl.when((s == NDEV // 2) & (kk == 0))
        def _():
            _exchange_wait(scattered, scattered[1:])
            red = lbuf[0]
            for j in range(1, NDEV):
                red = red + lbuf[j]
            lsum[...] = red
            for cp in gathered:
                cp.start()

        @pl.when(kk == 0)
        def _():
            acc[...] = lax.dot_general(a_ref[...], b_ref[...], tn, preferred_element_type=F32)

        @pl.when(kk > 0)
        def _():
            acc[...] += lax.dot_general(a_ref[...], b_ref[...], tn, preferred_element_type=F32)

        def to_sibling(j):
            return pltpu.make_async_remote_copy(
                src_ref=sbuf.at[0], dst_ref=sib.at[j], send_sem=sib_send.at[j], recv_sem=sib_recv.at[j],
                device_id=(x, y, 1 - c), device_id_type=MESH)

        def to_chip(j):
            dist = _chip_order(j, c)
            return pltpu.make_async_remote_copy(
                src_ref=sbuf.at[1], dst_ref=recv_ref.at[dist // 2], send_sem=chip_send.at[j],
                recv_sem=chip_recv.at[dist // 2], device_id=_peer_at(dist), device_id_type=MESH)

        keep = pltpu.make_async_copy(sbuf.at[1], recv_ref.at[0], keep_sem)
        sends = []
        for j in range(4):
            sends += [to_sibling(j), to_chip(j) if j < 3 else keep]

        for st in range(NDEV):
            @pl.when((kk == nk - 1) & (s == st))
            def _(st=st):
                if st >= 2:
                    sends[st - 2].wait_send()
                part = acc[...]
                if st % 2 == 1:
                    to_sibling(st // 2).wait_recv()
                    part = part + sib[st // 2].astype(F32)
                sbuf[st % 2] = part.astype(BF16)
                sends[st].start()
                if st == NDEV - 1:
                    sends[st - 1].wait_send()
                    sends[st].wait()
                    for j in range(1, 4):
                        pltpu.make_async_remote_copy(
                            src_ref=sbuf.at[0], dst_ref=recv_ref.at[j], send_sem=chip_send.at[0],
                            recv_sem=chip_recv.at[j], device_id=_peer_at(2 * j), device_id_type=MESH).wait_recv()
                    _exchange_wait(gathered, arrivals)
                    lru_ref[...] = lall[...]

    blk = lambda s, w_ref: w_ref[0] ^ _scatter_order(s, w_ref[1])
    return _call(
        body, name=name,
        grid_spec=pltpu.PrefetchScalarGridSpec(
            num_scalar_prefetch=1, grid=(NDEV, nk),
            in_specs=[pl.BlockSpec((tk, m), lambda s, kk, w_ref: (kk, 0)),
                      pl.BlockSpec((tk, bn), lambda s, kk, w_ref: (kk, blk(s, w_ref))), ANY],
            out_specs=(ANY, pl.BlockSpec((NDEV, rl, 128), lambda s, kk, w_ref: (0, 0, 0))),
            scratch_shapes=[pltpu.VMEM((m, bn), F32), pltpu.VMEM((2, m, bn), BF16), pltpu.VMEM((4, m, bn), BF16),
                            pltpu.VMEM((NDEV, rl, 128), F32), pltpu.VMEM((rl, 128), F32),
                            pltpu.VMEM((NDEV, rl, 128), F32),
                            pltpu.SemaphoreType.DMA((4,)), pltpu.SemaphoreType.DMA((4,)),
                            pltpu.SemaphoreType.DMA((4,)), pltpu.SemaphoreType.DMA((4,)),
                            pltpu.SemaphoreType.DMA,
                            pltpu.SemaphoreType.DMA((NDEV,)), pltpu.SemaphoreType.DMA((NDEV,)),
                            pltpu.SemaphoreType.DMA((NDEV,)), pltpu.SemaphoreType.DMA((NDEV,))]),
        out_shape=(jax.ShapeDtypeStruct((4, m, bn), BF16), jax.ShapeDtypeStruct((NDEV, rl, 128), F32)),
        compiler_params=_params(("arbitrary", "arbitrary")),
    )(where, at, b, lru_parts)


def _input_backward(dp, w_all, src, mv, row0, tm, nbk, name, dn=None, cols=None):
    rows, d = src.shape
    nb, _, bw = w_all.shape
    first, last = (0, nb * bw - 1) if cols is None else cols
    k0 = first // (nbk * bw)
    nk = last // (nbk * bw) - k0 + 1
    ni = rows // tm
    blk0 = row0 // tm
    latent = dn is not None

    def body(*refs):
        dp_ref, w_ref, x_ref, mv_ref = refs[:4]
        outs = refs[4 + latent:]
        part_ref, acc = outs[latent], outs[latent + 1]
        i, k = pl.program_id(0), pl.program_id(1)

        def product():
            step = _dot_nt(dp_ref[:, 0:bw], w_ref[0])
            for q in range(1, nbk):
                step = step + _dot_nt(dp_ref[:, q * bw:(q + 1) * bw], w_ref[q])
            return step

        def finish(slot):
            xf = x_ref[...]
            r = lax.rsqrt(jnp.mean(xf * xf, axis=-1, keepdims=True) + EPS)
            xn = xf * r
            dhl = acc[slot]
            gain, sc = mv_ref[0:1, :], mv_ref[1:2, :]
            dhx = jnp.sum(dhl * xn, axis=0, keepdims=True)
            part_ref[0:1, :] += jnp.sum(dhl, axis=0, keepdims=True)
            part_ref[1:2, :] += dhx * gain
            part_ref[2:3, :] += dhx * (1.0 + sc)
            if latent:
                dxn = dhl * (gain * (1.0 + sc))
                outs[0][...] = (refs[4][...].astype(F32)
                                + r * (dxn - xn * jnp.mean(dxn * xn, axis=-1, keepdims=True)))

        @pl.when((i == 0) & (k == 0))
        def _():
            part_ref[...] = jnp.zeros_like(part_ref)
            acc[0] = product()

        @pl.when((i > 0) & (i < ni) & (k == 0))
        def _():
            acc[i % 2] = product()
            finish((i - 1) % 2)

        @pl.when((i == ni) & (k == 0))
        def _():
            finish((ni - 1) % 2)

        @pl.when((i < ni) & (k > 0))
        def _():
            acc[i % 2] += product()

    tile = pl.BlockSpec((tm, d), lambda i, k: (jnp.maximum(i - 1, 0), 0))
    vec = pl.BlockSpec((8, d), lambda i, k: (0, 0))
    kblock = lambda i, k: k0 + jnp.where(i == ni, nk - 1, k)
    return _call(
        body, name=name,
        grid=(ni + 1, nk),
        out_shape=((jax.ShapeDtypeStruct((rows, d), F32),) if latent else ()) + (jax.ShapeDtypeStruct((8, d), F32),),
        in_specs=[pl.BlockSpec((tm, nbk * bw), lambda i, k: (blk0 + jnp.minimum(i, ni - 1), kblock(i, k))),
                  pl.BlockSpec((nbk, d, bw), lambda i, k: (kblock(i, k), 0, 0)), tile, vec]
                 + ([tile] if latent else []),
        out_specs=((tile,) if latent else ()) + (vec,),
        scratch_shapes=[pltpu.VMEM((2, tm, d), F32)],
        compiler_params=_params(("arbitrary", "arbitrary")),
    )(*([dp, w_all, src, mv] + ([dn] if latent else [])))


def _adamw_scattered(parts, w, m, v, tr):
    r, c = w.shape
    nslot = parts.shape[0]

    def body(p_ref, w_ref, m_ref, v_ref, g_ref, d_ref, m2_ref, v2_ref):
        g = p_ref[0].astype(F32)
        for k in range(1, nslot):
            g = g + p_ref[k].astype(F32)
        g_ref[...] = g
        d_ref[...], m2_ref[...], v2_ref[...] = _adamw(w_ref[...], g, m_ref[...], v_ref[...])

    tile = pl.BlockSpec((tr, c), lambda i: (i, 0))
    return _call(
        body, name="adamw_scattered_%dx%d" % (r, c),
        grid=(r // tr,),
        out_shape=tuple(jax.ShapeDtypeStruct((r, c), F32) for _ in range(4)),
        in_specs=[pl.BlockSpec((nslot, tr, c), lambda i: (0, i, 0)), tile, tile, tile],
        out_specs=(tile,) * 4,
        compiler_params=_params(("arbitrary",)),
    )(parts, w, m, v)


def _adamw_ada(st, dmod, w, m, v, tr):
    r, c = w.shape

    def body(s_ref, dm_ref, w_ref, m_ref, v_ref, g_ref, d_ref, m2_ref, v2_ref):
        g = jnp.dot(s_ref[...], dm_ref[...], precision=HIGHEST, preferred_element_type=F32)
        g_ref[...] = g
        d_ref[...], m2_ref[...], v2_ref[...] = _adamw(w_ref[...], g, m_ref[...], v_ref[...])

    tile = pl.BlockSpec((tr, c), lambda i: (i, 0))
    return _call(
        body, name="adamw_ada",
        grid=(r // tr,),
        out_shape=tuple(jax.ShapeDtypeStruct((r, c), F32) for _ in range(4)),
        in_specs=[pl.BlockSpec((tr, 16), lambda i: (i, 0)), pl.BlockSpec((16, c), lambda i: (0, 0)),
                  tile, tile, tile],
        out_specs=(tile,) * 4,
        compiler_params=_params(("arbitrary",)),
    )(st, dmod, w, m, v)


def _adamw_small(gs, ws, ms, vs):
    n = len(ws)

    def body(*refs):
        for j in range(n):
            g_ref, w_ref, m_ref, v_ref = refs[j], refs[n + j], refs[2 * n + j], refs[3 * n + j]
            d_ref, m2_ref, v2_ref = refs[4 * n + j], refs[5 * n + j], refs[6 * n + j]
            d_ref[...], m2_ref[...], v2_ref[...] = _adamw(w_ref[...], g_ref[...], m_ref[...], v_ref[...])

    shapes = tuple(jax.ShapeDtypeStruct(a.shape, F32) for a in ws)
    out = _call(
        body, name="adamw_small",
        out_shape=shapes * 3,
        in_specs=[VMEM] * (4 * n), out_specs=(VMEM,) * (3 * n),
        compiler_params=_params(),
    )(*gs, *ws, *ms, *vs)
    return list(out[:n]), list(out[n:2 * n]), list(out[2 * n:])


def _blockdiag_groups(wh, gc):
    h, dh, _ = wh.shape
    g = gc // dh
    w4 = wh.reshape(h // g, g, dh, dh)
    bd = jnp.einsum("ngij,gh->ngihj", w4, jnp.eye(g, dtype=wh.dtype))
    return bd.reshape(h // g, gc, gc)


def _blockdiag_extract(bd, dh):
    ng, gc, _ = bd.shape
    g = gc // dh
    x = bd.reshape(ng, g, dh, g, dh)
    return jnp.einsum("ngihj,gh->ngij", x, jnp.eye(g, dtype=bd.dtype)).reshape(ng * g, dh, dh)


def _largest_tile(n, cap):
    return max(q for q in range(128, min(n, cap) + 1, 128) if n % q == 0)


def _rows8(*vecs):
    rows = [jnp.reshape(v, (1, -1)).astype(F32) for v in vecs]
    n = rows[0].shape[1]
    return jnp.concatenate(rows + [jnp.zeros((8 - len(rows), n), F32)], axis=0)


def _pack(pieces):
    flat = jnp.concatenate([jnp.reshape(a, (-1,)).astype(F32) for a in pieces])
    total = -(-flat.shape[0] // 1024) * 1024
    return jnp.pad(flat, (0, total - flat.shape[0])).reshape(total // 128, 128)


def _unpack(packed, shapes):
    flat = packed.reshape(-1)
    out, off = [], 0
    for s in shapes:
        n = 1
        for q in s:
            n *= q
        out.append(flat[off:off + n].reshape(s))
        off += n
    return out


def kernel(x, c, ctx, c_ctx, norm_g, w_ada, b_ada, w_in, w_conv_a, w_conv_b, b_conv_b, lru_wa, lru_ba, lru_wx, lru_bx, lru_lambda, w_out, final_g, loss_target, m_c_ctx, m_norm_g, m_w_ada, m_b_ada, m_w_in, m_w_conv_a, m_w_conv_b, m_b_conv_b, m_lru_wa, m_lru_ba, m_lru_wx, m_lru_bx, m_lru_lambda, m_w_out, m_final_g, v_c_ctx, v_norm_g, v_w_ada, v_b_ada, v_w_in, v_w_conv_a, v_w_conv_b, v_b_conv_b, v_lru_wa, v_lru_ba, v_lru_wx, v_lru_bx, v_lru_lambda, v_w_out, v_final_g):
    _, l, d = x.shape
    lc = ctx.shape[1]
    w = d // 2
    t = lc
    assert l % t == 0 and t % GRID_W == 0 and t % 128 == 0
    dh = w // N_HEADS
    gc = min(w, MXU_WIDTH)
    cols = w_ada.shape[2]
    wo_rows = w_out.shape[1]
    me = _idx(_my_pos())
    x2, ctx2, tgt2 = x[0], ctx[0], loss_target[0]
    w_ada2, w_in2, w_out2 = w_ada[0], w_in[0], w_out[0]

    small_mine = jnp.concatenate([a.reshape(-1) for a in (w_conv_a, w_conv_b, lru_ba, lru_bx, lru_lambda)]
                                 + [jnp.zeros((3 * (w // NDEV),), F32)]).reshape(16, w // NDEV)
    mod_all, s_mat, small_all = _mod_forward(
        jnp.broadcast_to(c, (8, d)), jnp.broadcast_to(c_ctx[None], (8, d)), w_ada2, small_mine)
    mod = jnp.transpose(mod_all, (1, 0, 2)).reshape(16, NDEV * cols) + b_ada
    mod_lat = lax.dynamic_slice_in_dim(mod, me, 1, axis=0)
    sh_l, sc_l, gt_l = jnp.split(mod_lat, 3, axis=-1)
    sh_c, sc_c, _ = jnp.split(mod[8:9], 3, axis=-1)
    small = jnp.transpose(small_all, (1, 0, 2)).reshape(16, w)
    wca = _rows8(*[small[j] for j in range(0, 3)])
    wcb = _rows8(*[small[j] for j in range(3, 7)], b_conv_b)
    lv = _rows8(0.5 * small[7], 0.5 * small[9], small[11], 0.5 * small[8], 0.5 * small[10], small[12])
    wg = jnp.stack([
        jnp.concatenate([_blockdiag_groups(lru_wa[0, dr], gc), _blockdiag_groups(lru_wx[0, dr], gc)], axis=-1)
        for dr in range(2)])
    wg = (0.5 * wg).astype(BF16)

    la = l + lc
    tm = 2 * t if l % (2 * t) == 0 else t
    tk = 3 * t if la % (3 * t) == 0 else t
    h = _normalize(x2, _rows8(norm_g, sc_l, sh_l), la, 0, tm, "normalize")
    h = _normalize(ctx2, _rows8(norm_g, sc_c, sh_c), la, l, t, "normalize_ctx", prev=h)
    p, w_all, wo_all = _in_projection(h, w_in2.astype(BF16), w_out2.astype(BF16), la // 4 if la % 64 == 0 else tk)
    taps_m, back_m, perm = _scan_matrices(t)
    xb = _conv_input(p, wcb, taps_m, l, t)
    hf, hr = _lru_forward(xb, wg, lv, l, t)
    wo = wo_all.reshape(d, d)
    dn, cat, dout, part_mix = _mix_forward(x2, tgt2, p, hf, hr, wo, _rows8(gt_l, final_g), wca, perm, t)
    g_wout = _weight_grad_t(cat, dout, 2, 1, _largest_tile(l, 2048), "grad_w_out")
    dp, dhs, part_ca = _mix_backward(dout, p, hf, hr, wo, wca, perm, l, t)
    dxb0, dwg0, part_l0 = _lru_backward(0, xb, dhs, hf, wg, lv, l, t)
    dp, dwg1, part_l1, sc_wout = _lru_backward(
        1, xb, dhs, hr, wg, lv, l, t, conv=(p, wcb, back_m, dxb0, dp, g_wout.reshape(NDEV, wo_rows, d)))
    dwa = jnp.stack([_blockdiag_extract(dwg0[:, :, :gc], dh), _blockdiag_extract(dwg1[:, :, :gc], dh)])
    dwx = jnp.stack([_blockdiag_extract(dwg0[:, :, gc:], dh), _blockdiag_extract(dwg1[:, :, gc:], dh)])
    lru_part = (0.5 * jnp.stack([dwa, dwx])).reshape(NDEV, -1, 128)
    sc_win, lru_sum = _weight_grad_scatter(h, dp, lru_part, tk, "grad_w_in")
    grad_x, part_lat = _input_backward(dp, w_all, x2, _rows8(norm_g, sc_l), 0, tm, 2, "input_backward", dn=dn)
    (part_ctx,) = _input_backward(dp, w_all, ctx2, _rows8(norm_g, sc_c), l, t, 2, "input_backward_ctx",
                                  cols=(4 * w, 5 * w - 1))
    part_in = jnp.concatenate([part_lat[0:2], part_ctx[0:2], (part_lat[2] + part_ctx[2])[None]], axis=0)

    zeros_d = jnp.zeros((d,), F32)
    pieces = [
        jnp.concatenate([part_in[0], part_in[1], part_mix[1]]),
        jnp.concatenate([part_in[2], part_in[3], zeros_d]),
        part_in[4], part_mix[0], part_ca[0:3], part_l1[4:8], part_l1[3],
        0.5 * jnp.stack([part_l0[0], part_l1[0]]), 0.5 * jnp.stack([part_l0[1], part_l1[1]]),
        jnp.stack([part_l0[2], part_l1[2]]), part_mix[2, 0:1],
    ]
    shapes = [(3 * d,), (3 * d,), (d,), (d,), (3, w), (4, w), (w,), (2, w), (2, w), (2, w), (1,)]
    sig_cc = jax.nn.sigmoid(c_ctx)
    dsilu_cc = jnp.broadcast_to((sig_cc * (1.0 + c_ctx * (1.0 - sig_cc)))[None], (8, d))
    psum, pall, g_cctx8 = _reduce_small(_pack(pieces), w_ada2, dsilu_cc)
    (g_modl, g_modc, g_norm, g_final, g_ca, g_cb, g_bcb, g_ba, g_bx, g_lam, loss1) = _unpack(psum, shapes)
    loss = loss1[0]
    g_cctx = g_cctx8[0]
    g_bada = (g_modl + g_modc)[None]
    g_lru = lru_sum.reshape(2, 2, N_HEADS, dh, dh)
    g_wa, g_wx = g_lru[0][None], g_lru[1][None]
    wsl = w // NDEV
    mine = lambda a: lax.dynamic_slice_in_dim(a, me * wsl, wsl, axis=-1)
    g_ca_m, g_cb_m, g_ba_m, g_bx_m, g_lam_m = (mine(g_ca)[None], mine(g_cb)[None], mine(g_ba)[None],
                                               mine(g_bx)[None], mine(g_lam)[None])
    g_norm, g_bcb = g_norm[None], g_bcb[None]

    per_dev = pall[:, :3 * d // 128].reshape(NDEV, NDEV, cols)
    dmod_lat = lax.dynamic_slice_in_dim(per_dev, me, 1, axis=1)[:, 0]
    dmod_ctx = lax.dynamic_slice_in_dim(g_modc.reshape(NDEV, cols), me, 1, axis=0)
    dmod16 = jnp.concatenate([dmod_lat, dmod_ctx, jnp.zeros((7, cols), F32)], axis=0)
    tr_ada = 256 if d % 256 == 0 else d
    g_wada, d_wada, m_wada, v_wada = _adamw_ada(s_mat.T, dmod16, w_ada2, m_w_ada[0], v_w_ada[0], tr_ada)
    g_win2, d_win, m_win, v_win = _adamw_scattered(sc_win, w_in2, m_w_in[0], v_w_in[0], tr_ada)
    tr_out = 64 if wo_rows % 64 == 0 else wo_rows
    g_wout2, d_wout, m_wout, v_wout = _adamw_scattered(sc_wout, w_out2, m_w_out[0], v_w_out[0], tr_out)

    small_w = [c_ctx, norm_g, b_ada, w_conv_a, w_conv_b, b_conv_b, lru_wa, lru_ba, lru_wx, lru_bx, lru_lambda, final_g]
    small_m = [m_c_ctx, m_norm_g, m_b_ada, m_w_conv_a, m_w_conv_b, m_b_conv_b, m_lru_wa, m_lru_ba, m_lru_wx,
               m_lru_bx, m_lru_lambda, m_final_g]
    small_v = [v_c_ctx, v_norm_g, v_b_ada, v_w_conv_a, v_w_conv_b, v_b_conv_b, v_lru_wa, v_lru_ba, v_lru_wx,
               v_lru_bx, v_lru_lambda, v_final_g]
    small_g = [g_cctx, g_norm, g_bada, g_ca_m, g_cb_m, g_bcb, g_wa, g_ba_m, g_wx, g_bx_m, g_lam_m, g_final]
    small_g = [jnp.reshape(a, b.shape) for a, b in zip(small_g, small_w)]
    d_s, m_s, v_s = _adamw_small(small_g, small_w, small_m, small_v)

    def weights(small_list, ada, win, wout):
        (cctx_, norm_, bada_, ca_, cb_, bcb_, wa_, ba_, wx_, bx_, lam_, final_) = small_list
        return [cctx_, norm_, ada[None], bada_, win[None], ca_, cb_, bcb_, wa_, ba_, wx_, bx_, lam_, wout[None], final_]

    return (loss, grad_x[None],
            *weights(small_g, g_wada, g_win2, g_wout2), *weights(d_s, d_wada, d_win, d_wout),
            *weights(m_s, m_wada, m_win, m_wout), *weights(v_s, v_wada, v_win, v_wout))
```

```python
import functools

import jax
import jax.numpy as jnp
import numpy as np
from jax import lax
from jax.experimental import pallas as pl
from jax.experimental.pallas import tpu as pltpu

F32 = jnp.float32
BF16 = jnp.bfloat16
MESH = pl.DeviceIdType.MESH
NDEV = 8
GRID_W = 64
N_HEADS = 16
LRU_C = 8.0
EPS = 1e-6
MXU_WIDTH = 256
VMEM_LIMIT = 60 * 1024 * 1024

ADAM_LR = 0.001
ADAM_B1 = 0.9
ADAM_B2 = 0.999
ADAM_EPS = 1e-08
ADAM_WD = 0.01
ADAM_STEP = 10
ADAM_C1 = 1.0 - ADAM_B1 ** ADAM_STEP
ADAM_C2 = 1.0 - ADAM_B2 ** ADAM_STEP

HIGHEST = lax.Precision.HIGHEST
ANY = pl.BlockSpec(memory_space=pl.ANY)
VMEM = pl.BlockSpec(memory_space=pltpu.VMEM)


def _call(body, **kw):
    return pl.pallas_call(body, **kw)


def _params(sem=None, vmem=VMEM_LIMIT):
    return pltpu.CompilerParams(dimension_semantics=sem, vmem_limit_bytes=vmem)


def _my_pos():
    return lax.axis_index("x"), lax.axis_index("y"), lax.axis_index("c")


def _idx(pos):
    return 4 * pos[0] + 2 * pos[1] + pos[2]


def _peer(k):
    x, y, c = _my_pos()
    return ((1 - x) if (k >> 2) & 1 else x, (1 - y) if (k >> 1) & 1 else y, (1 - c) if k & 1 else c)


def _exchange_start(src_ref, dst_ref, send_sems, recv_sems, base):
    me = _idx(_my_pos())
    sends = []
    for k in range(1, NDEV):
        cp = pltpu.make_async_remote_copy(
            src_ref=src_ref, dst_ref=dst_ref.at[me], send_sem=send_sems.at[base + k - 1],
            recv_sem=recv_sems.at[base + k - 1], device_id=_peer(k), device_id_type=MESH)
        cp.start()
        sends.append(cp)
    dst_ref[me] = src_ref[...]
    return sends, (src_ref, dst_ref, send_sems, recv_sems, base)


def _exchange_finish(started):
    sends, (src_ref, dst_ref, send_sems, recv_sems, base) = started
    for k in range(1, NDEV):
        peer = _peer(k)
        pltpu.make_async_remote_copy(
            src_ref=src_ref, dst_ref=dst_ref.at[_idx(peer)], send_sem=send_sems.at[base + k - 1],
            recv_sem=recv_sems.at[base + k - 1], device_id=peer, device_id_type=MESH).wait_recv()
    for cp in sends:
        cp.wait_send()


def _exchange_vmem(src_ref, dst_ref, send_sems, recv_sems, base):
    _exchange_finish(_exchange_start(src_ref, dst_ref, send_sems, recv_sems, base))


def _sigmoid(z):
    return 0.5 * jnp.tanh(0.5 * z) + 0.5


def _softplus(x):
    return jnp.maximum(x, 0.0) + jnp.log1p(jnp.exp(-jnp.abs(x)))


def _one_minus_sq(a, la):
    series = (-2.0 * la) * (1.0 + la)
    return jnp.where(la > -0.0015, series, 1.0 - a * a)


def _dot(a, b):
    return jnp.dot(a, b, preferred_element_type=F32)


def _dot_nt(a, b):
    return lax.dot_general(a, b, (((1,), (1,)), ((), ())), preferred_element_type=F32)


def _rows(shape):
    return lax.broadcasted_iota(jnp.int32, shape, 0)


def _scan_matrices(t):
    seg = t // 8
    r = np.arange(t)
    perm = (np.arange(t)[None, :] == ((r % 8) * seg + r // 8)[:, None]).astype(np.float32)
    rows, cols = r[:, None], r[None, :]
    taps, back = [], []
    for rowlen in (GRID_W, t):
        pos = rows % rowlen
        shift = {-2: (cols == rows - 2) & (pos >= 2), -1: (cols == rows - 1) & (pos >= 1),
                 0: cols == rows, 1: (cols == rows + 1) & (pos + 1 < rowlen),
                 2: (cols == rows + 2) & (pos + 2 < rowlen)}
        if rowlen == GRID_W:
            beside = [shift[-1].astype(np.float32), shift[1].astype(np.float32)]
        taps.append(np.stack([perm @ shift[k].astype(np.float32) for k in (-2, -1, 0, 1)]))
        back.append(np.stack([shift[k].astype(np.float32) @ perm.T for k in (2, 1, 0, -1)]))
    as_bf16 = lambda a: jnp.asarray(a, dtype=BF16)
    return as_bf16(np.stack(taps)), as_bf16(np.stack(back)), as_bf16(np.stack([perm, perm.T] + beside))


def _chunk_scan(a, b, reverse):
    row = _rows(a.shape)
    for s in (1, 2, 4):
        if reverse:
            m = row < 8 - s
            sh = 8 - s
        else:
            m = row >= s
            sh = s
        a_s = jnp.where(m, pltpu.roll(a, sh, 0), 1.0)
        b_s = jnp.where(m, pltpu.roll(b, sh, 0), 0.0)
        b = b + a * b_s
        a = a * a_s
    return a, b


def _chain_segments(ptot, hend, carry, reverse):
    ca, cb = _chunk_scan(ptot, hend, reverse)
    incl = ca * carry + cb
    r8 = _rows(incl.shape)
    if reverse:
        start = jnp.where(r8 < 7, pltpu.roll(incl, 7, 0), carry)
        last = incl[0:1, :]
    else:
        start = jnp.where(r8 >= 1, pltpu.roll(incl, 1, 0), carry)
        last = incl[7:8, :]
    return start, jnp.broadcast_to(last, incl.shape)


def _blocks(nblock, reverse):
    order = range(nblock - 1, -1, -1) if reverse else range(nblock)
    return [slice(8 * k, 8 * k + 8) for k in order]


def _scan_tile(a_ref, b_ref, out_ref, carry, reverse):
    t, w = a_ref.shape
    seg = t // 8

    hend, ptot = jnp.zeros((8, w), F32), jnp.ones((8, w), F32)
    for rows in _blocks(seg, reverse):
        a = a_ref[rows, :]
        hend, ptot = a * hend + b_ref[rows, :], a * ptot
    h, new_carry = _chain_segments(ptot, hend, carry, reverse)
    for rows in _blocks(seg, reverse):
        h = a_ref[rows, :] * h + b_ref[rows, :]
        out_ref[rows, :] = h
    return new_carry


def _scan_tile_backward(a_ref, dh_ref, g_ref, carry, reverse):
    t, w = a_ref.shape
    seg = t // 8

    uend, ptot = jnp.zeros((8, w), F32), jnp.ones((8, w), F32)
    for rows in _blocks(seg, reverse):
        a = a_ref[rows, :]
        uend, ptot = a * (dh_ref[rows, :] + uend), a * ptot
    u, new_carry = _chain_segments(ptot, uend, carry, reverse)
    for rows in _blocks(seg, reverse):
        g = dh_ref[rows, :] + u
        g_ref[rows, :] = g
        u = a_ref[rows, :] * g
    return new_carry


def _lru_coef(xb, wg_ref, d, ba, bx, lam, gc):
    w = xb.shape[1]
    xb16 = xb.astype(BF16)
    zr, zi = [], []
    for g in range(w // gc):
        z = _dot(xb16[:, g * gc:(g + 1) * gc], wg_ref[d, g])
        zr.append(z[:, :gc])
        zi.append(z[:, gc:])
    zr = zr[0] if len(zr) == 1 else jnp.concatenate(zr, axis=-1)
    zi = zi[0] if len(zi) == 1 else jnp.concatenate(zi, axis=-1)
    tr = jnp.tanh(zr + ba)
    ti = jnp.tanh(zi + bx)
    sp = _softplus(-lam)
    half = -0.5 * LRU_C * sp
    la = tr * half + half
    a = jnp.exp(la)
    q = _one_minus_sq(a, la)
    rs = lax.rsqrt(jnp.maximum(q, 1e-30))
    return a, q * rs, rs, tr, ti, sp


def _adamw(w, g, m, v):
    m2 = ADAM_B1 * m + (1.0 - ADAM_B1) * g
    v2 = ADAM_B2 * v + (1.0 - ADAM_B2) * (g * g)
    m_hat = m2 / ADAM_C1
    v_hat = v2 / ADAM_C2
    delta = -ADAM_LR * (m_hat / (jnp.sqrt(v_hat) + ADAM_EPS) + ADAM_WD * w)
    return delta, m2, v2


def _mod_forward(c8, cctx8, w_ada, small):
    d = c8.shape[1]
    cols = w_ada.shape[1]

    def body(c_ref, cctx_ref, w_ref, sm_ref, mod_ref, s_ref, sm_all, cbuf, mod_my, send_sems, recv_sems):
        _exchange_vmem(sm_ref, sm_all, send_sems, recv_sems, 2 * (NDEV - 1))
        _exchange_vmem(c_ref, cbuf, send_sems, recv_sems, 0)
        row = _rows((8, d))
        c_all = jnp.zeros((8, d), F32)
        for b in range(NDEV):
            c_all = jnp.where(row == b, cbuf[b], c_all)
        cc = cctx_ref[...]
        s_top = c_all * _sigmoid(c_all)
        s_bot = jnp.where(row == 0, cc * _sigmoid(cc), 0.0)
        s = jnp.concatenate([s_top, s_bot], axis=0)
        s_ref[...] = s
        mod_my[...] = jnp.dot(s, w_ref[...], precision=HIGHEST, preferred_element_type=F32)
        _exchange_vmem(mod_my, mod_ref, send_sems, recv_sems, NDEV - 1)

    return _call(
        body, name="mod_forward",
        out_shape=(jax.ShapeDtypeStruct((NDEV, 16, cols), F32), jax.ShapeDtypeStruct((16, d), F32),
                   jax.ShapeDtypeStruct((NDEV,) + small.shape, F32)),
        in_specs=[VMEM] * 4, out_specs=(VMEM,) * 3,
        scratch_shapes=[pltpu.VMEM((NDEV, 8, d), F32), pltpu.VMEM((16, cols), F32),
                        pltpu.SemaphoreType.DMA((3 * (NDEV - 1),)), pltpu.SemaphoreType.DMA((3 * (NDEV - 1),))],
        compiler_params=_params(),
    )(c8, cctx8, w_ada, small)


def _scatter_copies(src_ref, dst_ref, send_sems, recv_sems):
    me = _idx(_my_pos())
    copies = [pltpu.make_async_copy(src_ref.at[me], dst_ref.at[0], send_sems.at[0])]
    for k in range(1, NDEV):
        peer = _peer(k)
        copies.append(pltpu.make_async_remote_copy(
            src_ref=src_ref.at[_idx(peer)], dst_ref=dst_ref.at[k], send_sem=send_sems.at[k],
            recv_sem=recv_sems.at[k], device_id=peer, device_id_type=MESH))
    return copies


def _gather_copies(src_ref, dst_ref, send_sems, recv_sems):
    me = _idx(_my_pos())
    sends = [pltpu.make_async_copy(src_ref, dst_ref.at[me], send_sems.at[0])]
    arrivals = []
    for k in range(1, NDEV):
        peer = _peer(k)
        sends.append(pltpu.make_async_remote_copy(
            src_ref=src_ref, dst_ref=dst_ref.at[me], send_sem=send_sems.at[k],
            recv_sem=recv_sems.at[k], device_id=peer, device_id_type=MESH))
        arrivals.append(pltpu.make_async_remote_copy(
            src_ref=src_ref, dst_ref=dst_ref.at[_idx(peer)], send_sem=send_sems.at[k],
            recv_sem=recv_sems.at[k], device_id=peer, device_id_type=MESH))
    return sends, arrivals


def _exchange_wait(sends, arrivals):
    sends[0].wait()
    for cp in arrivals:
        cp.wait_recv()
    for cp in sends[1:]:
        cp.wait_send()


def _chip_order(k, c):
    return (6, 4 - 2 * c, 2 + 2 * c, 0)[k]


def _scatter_order(s, c):
    k = s >> 1
    mine = jnp.where(k == 0, 6, jnp.where(k == 1, 4 - 2 * c, jnp.where(k == 2, 2 + 2 * c, 0)))
    theirs = jnp.where(k == 0, 6, jnp.where(k == 1, 2 + 2 * c, jnp.where(k == 2, 4 - 2 * c, 0))) ^ 1
    return jnp.where((s & 1) == 0, theirs, mine)


def _peer_at(dist):
    x, y, c = _my_pos()
    return (x ^ ((dist >> 2) & 1), y ^ ((dist >> 1) & 1), c ^ (dist & 1))


def _reduce_small(summed, w_ada, dsilu_cctx):
    d, cols = w_ada.shape
    assert cols % 128 == 0
    cb = cols // 128

    def body(sum_ref, w_ref, ds_ref, cctx_ref, cpart, call, send_sems, recv_sems):
        me = _idx(_my_pos())
        part = jnp.zeros((8, d), F32)
        for q in range(cb):
            dm = jnp.broadcast_to(sum_ref[pl.ds((NDEV + me) * cb + q, 1), :], (8, 128))
            part = part + lax.dot_general(dm, w_ref[:, q * 128:(q + 1) * 128],
                                          (((1,), (1,)), ((), ())), precision=HIGHEST,
                                          preferred_element_type=F32)
        cpart[...] = part
        _exchange_vmem(cpart, call, send_sems, recv_sems, 0)
        tot = call[0]
        for j in range(1, NDEV):
            tot = tot + call[j]
        cctx_ref[...] = tot * ds_ref[...]

    return _call(
        body, name="reduce_small",
        out_shape=jax.ShapeDtypeStruct((8, d), F32),
        in_specs=[VMEM] * 3, out_specs=VMEM,
        scratch_shapes=[pltpu.VMEM((8, d), F32), pltpu.VMEM((NDEV, 8, d), F32),
                        pltpu.SemaphoreType.DMA((NDEV - 1,)), pltpu.SemaphoreType.DMA((NDEV - 1,))],
        compiler_params=_params(),
    )(summed, w_ada, dsilu_cctx)


def _normalize(src, mv, la, row0, tm, name, prev=None):
    rows, d = src.shape
    blk0 = row0 // tm

    def body(*refs):
        x_ref, mv_ref, h_ref = refs[0], refs[1], refs[-1]
        xf = x_ref[...]
        r = lax.rsqrt(jnp.mean(xf * xf, axis=-1, keepdims=True) + EPS)
        h = xf * r * (mv_ref[0:1, :] * (1.0 + mv_ref[1:2, :])) + mv_ref[2:3, :]
        h_ref[...] = h.astype(BF16)

    in_specs = [pl.BlockSpec((tm, d), lambda i: (i, 0)), pl.BlockSpec((8, d), lambda i: (0, 0))]
    args = [src, mv]
    aliases = {}
    if prev is not None:
        in_specs += [ANY]
        args += [prev]
        aliases = {2: 0}
    return _call(
        body, name=name,
        grid=(rows // tm,),
        out_shape=jax.ShapeDtypeStruct((la, d), BF16),
        in_specs=in_specs,
        out_specs=pl.BlockSpec((tm, d), lambda i: (blk0 + i, 0)),
        input_output_aliases=aliases,
        compiler_params=_params(("arbitrary",)),
    )(*args)


def _gather_order(step):
    return (step & 1) | (((step >> 2) & 1) << 1) | (((step >> 1) & 1) << 2)


def _in_projection(h, w_shard, wo_shard, tm):
    la, d = h.shape
    bw = w_shard.shape[1]
    ni = la // tm
    where = jnp.reshape(_idx(_my_pos()), (1,)).astype(jnp.int32)

    def body(me_ref, h_ref, w_ref, wo_ref, p_ref, all_ref, wo_all, wbuf, send_sems, recv_sems, local_sems,
             wo_send, wo_recv):
        s, i = pl.program_id(0), pl.program_id(1)
        x, y, c = _my_pos()
        wo_sends, wo_arrivals = _gather_copies(wo_ref, wo_all, wo_send, wo_recv)

        @pl.when((s == NDEV // 2) & (i == 0))
        def _():
            for cp in wo_sends:
                cp.start()

        me, sibling = (x, y, c), (x, y, 1 - c)
        chips = [(1 - x, y), (x, 1 - y), (1 - x, 1 - y)]

        def copy(k, block, to, from_shard=False):
            return pltpu.make_async_remote_copy(
                src_ref=w_ref if from_shard else all_ref.at[_idx(block)], dst_ref=all_ref.at[_idx(block)],
                send_sem=send_sems.at[k], recv_sem=recv_sems.at[k], device_id=to, device_id_type=MESH)

        def load(block, slot):
            return pltpu.make_async_copy(all_ref.at[_idx(block)], wbuf.at[slot], local_sems.at[1])

        keep = pltpu.make_async_copy(w_ref, all_ref.at[_idx(me)], local_sems.at[0])
        first = [copy(0, me, sibling, True)] + [copy(1 + j, me, (*chip, c), True) for j, chip in enumerate(chips)]
        passed = [copy(4 + j, (*chip, c), sibling) for j, chip in enumerate(chips)]
        steps = [(copy(0, sibling, me), None, sibling)]
        for j, chip in enumerate(chips):
            steps.append((copy(1 + j, (*chip, c), me), passed[j], (*chip, c)))
            steps.append((copy(4 + j, (*chip, 1 - c), me), None, (*chip, 1 - c)))

        @pl.when((s == 0) & (i == 0))
        def _():
            keep.start()
            mine = pltpu.make_async_copy(w_ref, wbuf.at[0], local_sems.at[1])
            mine.start()
            for cp in first:
                cp.start()
            mine.wait()

        for n, (arrival, forward, block) in enumerate(steps, start=1):
            @pl.when((s == n - 1) & (i == ni - 1))
            def _(arrival=arrival, forward=forward, block=block, n=n):
                arrival.wait_recv()
                if forward is not None:
                    forward.start()
                load(block, n % 2).start()

        @pl.when((s > 0) & (i == 0))
        def _():
            load(me, s % 2).wait()

        p_ref[...] = _dot(h_ref[...], wbuf[s % 2]).astype(BF16)

        @pl.when((s == NDEV - 1) & (i == ni - 1))
        def _():
            for cp in first + passed:
                cp.wait_send()
            keep.wait()
            _exchange_wait(wo_sends, wo_arrivals)

    return _call(
        body, name="in_projection",
        grid_spec=pltpu.PrefetchScalarGridSpec(
            num_scalar_prefetch=1, grid=(NDEV, ni),
            in_specs=[pl.BlockSpec((tm, d), lambda s, i, me_ref: (i, 0)), ANY, ANY],
            out_specs=(pl.BlockSpec((tm, bw), lambda s, i, me_ref: (i, me_ref[0] ^ _gather_order(s))), ANY, ANY),
            scratch_shapes=[pltpu.VMEM((2, d, bw), BF16), pltpu.SemaphoreType.DMA((7,)),
                            pltpu.SemaphoreType.DMA((7,)), pltpu.SemaphoreType.DMA((2,)),
                            pltpu.SemaphoreType.DMA((NDEV,)), pltpu.SemaphoreType.DMA((NDEV,))]),
        out_shape=(jax.ShapeDtypeStruct((la, NDEV * bw), BF16), jax.ShapeDtypeStruct((NDEV, d, bw), BF16),
                   jax.ShapeDtypeStruct((NDEV,) + wo_shard.shape, wo_shard.dtype)),
        compiler_params=_params(("arbitrary", "arbitrary")),
    )(where, h, w_shard, wo_shard)


def _conv_input(p, wcb, taps_m, l, t):
    la = p.shape[0]
    w = wcb.shape[1]
    nt = l // t

    def body(v_ref, wcb_ref, tm_ref, xb_ref):
        taps = _dot(tm_ref[...].reshape(4 * t, t), v_ref[...])
        xb = wcb_ref[4:5, :] + wcb_ref[0:1, :] * taps[0:t]
        for j in range(1, 4):
            xb = xb + wcb_ref[j:j + 1, :] * taps[j * t:(j + 1) * t]
        xb_ref[...] = xb

    return _call(
        body, name="conv_input",
        grid=(nt + 1,),
        out_shape=jax.ShapeDtypeStruct((la, w), F32),
        in_specs=[pl.BlockSpec((t, w), lambda i: (i, 4)), pl.BlockSpec((8, w), lambda i: (0, 0)),
                  pl.BlockSpec((None, 4, t, t), lambda i: (i // nt, 0, 0, 0))],
        out_specs=pl.BlockSpec((t, w), lambda i: (i, 0)),
        compiler_params=_params(("arbitrary",)),
    )(p, wcb, taps_m)


def _lru_forward(xb, wg, lv, l, t):
    la, w = xb.shape
    gc = wg.shape[2]
    nt = l // t

    def body(xf_ref, xr_ref, wg_ref, lv_ref, hf_ref, hr_ref, a_s, b_s, carry):
        @pl.when(pl.program_id(0) == 0)
        def _():
            carry[...] = jnp.zeros_like(carry)

        for dr, (x_ref, h_ref) in enumerate(((xf_ref, hf_ref), (xr_ref, hr_ref))):
            x = x_ref[...]
            a, s, _, _, ti, _ = _lru_coef(x, wg_ref, dr, lv_ref[3 * dr:3 * dr + 1, :],
                                          lv_ref[3 * dr + 1:3 * dr + 2, :], lv_ref[3 * dr + 2:3 * dr + 3, :], gc)
            a_s[...] = a
            b_s[...] = (s * x) * (0.5 * ti + 0.5)
            carry[dr] = _scan_tile(a_s, b_s, h_ref, carry[dr], dr == 1)

    full = lambda shape: pl.BlockSpec(shape, lambda i: (0,) * len(shape))
    fmap = lambda i: (jnp.where(i == 0, nt, i - 1), 0)
    rmap = lambda i: (jnp.where(i == 0, nt, nt - i), 0)
    return _call(
        body, name="lru_forward",
        grid=(nt + 1,),
        out_shape=(jax.ShapeDtypeStruct((la, w), F32), jax.ShapeDtypeStruct((la, w), F32)),
        in_specs=[pl.BlockSpec((t, w), fmap), pl.BlockSpec((t, w), rmap), full(wg.shape), full(lv.shape)],
        out_specs=(pl.BlockSpec((t, w), fmap), pl.BlockSpec((t, w), rmap)),
        scratch_shapes=[pltpu.VMEM((t, w), F32), pltpu.VMEM((t, w), F32), pltpu.VMEM((2, 8, w), F32)],
        compiler_params=_params(("arbitrary",)),
    )(xb, xb, wg, lv)


def _mix_gates(p_refs, hf_ref, hr_ref, wca_ref, perm_ref, t, w):
    bl, cl, ul, gl, ql = [r[...].astype(F32) for r in p_refs]
    tt = cl * ul
    tt16 = tt.astype(BF16)
    beside = _dot(perm_ref[2:4].reshape(2 * t, t), tt16)
    before, after = beside[:t], beside[t:]
    z = wca_ref[0:1, :] * before + wca_ref[1:2, :] * tt + wca_ref[2:3, :] * after
    sig_g = _sigmoid(gl)
    sig_q = _sigmoid(ql)
    ylru = _dot(perm_ref[1], (hf_ref[...] + hr_ref[...]).astype(BF16))
    return bl, cl, ul, gl, ql, (before, tt, after), z, sig_g, sig_q, ylru


def _p_specs(t, w, nt):
    return [pl.BlockSpec((t, w), functools.partial(lambda i, s: (jnp.minimum(i, nt - 1), s), s=s))
            for s in (0, 1, 2, 3, 5)]


def _mix_forward(x, tgt, p, hf, hr, wo, ov, wca, perm, t):
    l, d = x.shape
    w = d // 2
    nt = l // t

    def body(x_ref, tg_ref, b_ref, c_ref, u_ref, g_ref, q_ref, hf_ref, hr_ref, wo_ref, ov_ref, wca_ref, perm_ref,
             dn_ref, ct_ref, do_ref, part_ref):
        i = pl.program_id(0)
        bl, _, _, gl, ql, _, z, sig_g, sig_q, ylru = _mix_gates(
            (b_ref, c_ref, u_ref, g_ref, q_ref), hf_ref, hr_ref, wca_ref, perm_ref, t, w)
        ya = bl * z * (gl * sig_g)
        yb = ylru * (ql * sig_q)
        ct_ref[:, 0:w] = ya.astype(BF16)
        ct_ref[:, w:] = yb.astype(BF16)
        out = _dot(ya.astype(BF16), wo_ref[0:w, :]) + _dot(yb.astype(BF16), wo_ref[w:, :])
        gate, fg = ov_ref[0:1, :], ov_ref[1:2, :]
        n = x_ref[...] + gate * out
        rr = lax.rsqrt(jnp.mean(n * n, axis=-1, keepdims=True) + EPS)
        nh = n * rr
        e = nh * fg - tg_ref[...]
        loss = 0.5 * jnp.sum(jnp.mean(e * e, axis=-1, keepdims=True), axis=0, keepdims=True)
        dy = e * (1.0 / d)
        dnh = dy * fg
        dn = rr * (dnh - nh * jnp.mean(dnh * nh, axis=-1, keepdims=True))
        dn_ref[...] = dn.astype(BF16)
        do_ref[...] = (dn * gate).astype(BF16)

        @pl.when(i == 0)
        def _():
            part_ref[...] = jnp.zeros_like(part_ref)

        part_ref[0:1, :] += jnp.sum(dy * nh, axis=0, keepdims=True)
        part_ref[1:2, :] += jnp.sum(dn * out, axis=0, keepdims=True)
        part_ref[2:3, :] += jnp.broadcast_to(loss, (1, d))

    tile = lambda cols: pl.BlockSpec((t, cols), lambda i: (i, 0))
    full = lambda shape: pl.BlockSpec(shape, lambda i: (0,) * len(shape))
    return _call(
        body, name="mix_forward",
        grid=(nt,),
        out_shape=(jax.ShapeDtypeStruct((l, d), BF16), jax.ShapeDtypeStruct((l, d), BF16),
                   jax.ShapeDtypeStruct((l, d), BF16), jax.ShapeDtypeStruct((8, d), F32)),
        in_specs=[tile(d), tile(d)] + _p_specs(t, w, nt) + [tile(w), tile(w),
                  pl.BlockSpec((d, d), lambda i: (0, 0), pipeline_mode=pl.Buffered(1)),
                  full(ov.shape), full(wca.shape), full(perm.shape)],
        out_specs=(tile(d), tile(d), tile(d), full((8, d))),
        compiler_params=_params(("arbitrary",)),
    )(x, tgt, p, p, p, p, p, hf, hr, wo, ov, wca, perm)


def _mix_backward(dout, p, hf, hr, wo, wca, perm, l, t):
    d = dout.shape[1]
    w = d // 2
    nt = l // t
    la = p.shape[0]

    def body(do_ref, b_ref, c_ref, u_ref, g_ref, q_ref, hf_ref, hr_ref, wo_ref, wca_ref, perm_ref,
             dp_ref, dh_ref, part_ref):
        i = pl.program_id(0)

        @pl.when(i == 0)
        def _():
            part_ref[...] = jnp.zeros_like(part_ref)

        @pl.when(i == nt)
        def _():
            dp_ref[...] = jnp.zeros_like(dp_ref)

        @pl.when(i < nt)
        def _():
            bl, cl, ul, gl, ql, taps, z, sig_g, sig_q, ylru = _mix_gates(
                (b_ref, c_ref, u_ref, g_ref, q_ref), hf_ref, hr_ref, wca_ref, perm_ref, t, w)
            do = do_ref[...]
            dya = _dot_nt(do, wo_ref[0:w, :])
            dyb = _dot_nt(do, wo_ref[w:, :])
            sg = gl * sig_g
            dz = dya * bl * sg
            dz16 = dz.astype(BF16)
            beside = _dot(perm_ref[2:4].reshape(2 * t, t), dz16)
            dt = wca_ref[0:1, :] * beside[t:] + wca_ref[1:2, :] * dz + wca_ref[2:3, :] * beside[:t]
            dp_ref[:, 0:w] = (dya * z * sg).astype(BF16)
            dp_ref[:, w:2 * w] = (dt * ul).astype(BF16)
            dp_ref[:, 2 * w:3 * w] = (dt * cl).astype(BF16)
            dp_ref[:, 3 * w:4 * w] = (dya * bl * z * (sig_g * (1.0 + gl * (1.0 - sig_g)))).astype(BF16)
            dp_ref[:, 4 * w:5 * w] = jnp.zeros((t, w), BF16)
            dp_ref[:, 5 * w:6 * w] = (dyb * ylru * (sig_q * (1.0 + ql * (1.0 - sig_q)))).astype(BF16)
            dh_ref[...] = _dot(perm_ref[0], (dyb * (ql * sig_q)).astype(BF16)).astype(BF16)
            for j in range(3):
                part_ref[j:j + 1, :] += jnp.sum(dz * taps[j], axis=0, keepdims=True)

    clamp = lambda cols: pl.BlockSpec((t, cols), lambda i: (jnp.minimum(i, nt - 1), 0))
    full = lambda shape: pl.BlockSpec(shape, lambda i: (0,) * len(shape))
    return _call(
        body, name="mix_backward",
        grid=(nt + 1,),
        out_shape=(jax.ShapeDtypeStruct((la, 6 * w), BF16), jax.ShapeDtypeStruct((l, w), BF16),
                   jax.ShapeDtypeStruct((8, w), F32)),
        in_specs=[clamp(d)] + _p_specs(t, w, nt) + [clamp(w), clamp(w),
                  pl.BlockSpec((d, d), lambda i: (0, 0), pipeline_mode=pl.Buffered(1)), full(wca.shape),
                  full(perm.shape)],
        out_specs=(pl.BlockSpec((t, 6 * w), lambda i: (i, 0)), clamp(w), full((8, w))),
        compiler_params=_params(("arbitrary",)),
    )(dout, p, p, p, p, p, hf, hr, wo, wca, perm)


def _lru_backward(direction, xb, dhs, hs, wg, lv, l, t, conv=None):
    la, w = hs.shape
    gc = wg.shape[2]
    ng = w // gc
    nt = l // t
    nblk8 = la // 8
    last = conv is not None
    assert last == (direction == 1)

    if direction == 0:
        tile = lambda i: jnp.where(i == nt, nt, nt - 1 - i)
        halo = lambda i: jnp.where(tile(i) == 0, nblk8 - 1, tile(i) * (t // 8) - 1)
    else:
        tile = lambda i: i
        halo = lambda i: jnp.minimum((i + 1) * (t // 8), nblk8 - 1)

    def body(*refs):
        x_ref, dh_ref, hs_ref, halo_ref, wg_ref, lv_ref = refs[:6]
        i = pl.program_id(0)
        is_ctx = i == nt
        if last:
            v_ref, wcb_ref, bm_ref, dxo_ref, _, gw_ref = refs[6:12]
            out_ref, dwg_ref, part_ref, sc_ref, a_s, dh_s, g_s, carry, send_sems, recv_sems = refs[12:]
            copies = _scatter_copies(gw_ref, sc_ref, send_sems, recv_sems)
        else:
            out_ref, dwg_ref, part_ref, a_s, dh_s, g_s, carry = refs[-7:]
            copies = []

        @pl.when(i == 0)
        def _():
            carry[...] = jnp.zeros_like(carry)
            dwg_ref[...] = jnp.zeros_like(dwg_ref)
            part_ref[...] = jnp.zeros_like(part_ref)
            for cp in copies:
                cp.start()

        if last:
            @pl.when(is_ctx)
            def _():
                _exchange_wait(copies, copies[1:])

        xb = x_ref[...]
        lam = lv_ref[3 * direction + 2:3 * direction + 3, :]
        a, s, rs, tr, ti, sp = _lru_coef(xb, wg_ref, direction, lv_ref[3 * direction:3 * direction + 1, :],
                                         lv_ref[3 * direction + 1:3 * direction + 2, :], lam, gc)
        hs_t = hs_ref[...]
        r8 = _rows((8, w))
        if direction == 0:
            edge = jnp.where(is_ctx, 0.0, halo_ref[7:8, :])
            first = jnp.where(r8 == 0, edge, pltpu.roll(hs_t[t - 8:, :], 1, 0))
            hprev = jnp.concatenate([first, hs_t[:t - 8, :]], axis=0)
        else:
            edge = jnp.where(is_ctx, 0.0, halo_ref[0:1, :])
            final = jnp.where(r8 == 7, edge, pltpu.roll(hs_t[:8, :], 7, 0))
            hprev = jnp.concatenate([hs_t[8:, :], final], axis=0)
        a_s[...] = a
        dh_s[...] = jnp.where(is_ctx, 0.0, dh_ref[...].astype(F32))
        carry[...] = _scan_tile_backward(a_s, dh_s, g_s, carry[...], direction == 0)

        g = g_s[...]
        r = 0.5 * tr + 0.5
        ig = 0.5 * ti + 0.5
        ix = ig * xb
        gs = g * s
        dla = (g * a) * (hprev - ix * (a * rs))
        dxb = gs * ig
        dzr = dla * (r * (1.0 - tr)) * (-LRU_C * sp)
        dzi = gs * ix * (1.0 - ti)
        part_ref[0:1, :] += jnp.sum(dzr, axis=0, keepdims=True)
        part_ref[1:2, :] += jnp.sum(dzi, axis=0, keepdims=True)
        part_ref[2:3, :] += jnp.sum(dla * r, axis=0, keepdims=True) * (LRU_C * _sigmoid(-lam))
        pieces = []
        for gi in range(ng):
            sl = slice(gi * gc, (gi + 1) * gc)
            dz = jnp.concatenate([dzr[:, sl], dzi[:, sl]], axis=-1).astype(BF16)
            pieces.append(_dot_nt(dz, wg_ref[direction, gi]))
            dwg_ref[gi] += _dot(xb[:, sl].T.astype(BF16), dz)
        dxb = dxb + (pieces[0] if ng == 1 else jnp.concatenate(pieces, axis=-1))
        if not last:
            out_ref[...] = dxb
        else:
            dxb = dxb + dxo_ref[...]
            v = v_ref[...].astype(F32)
            backs = _dot(bm_ref[...].reshape(4 * t, t), dxb.astype(BF16))
            dv = jnp.zeros((t, w), F32)
            for j in range(4):
                back = backs[j * t:(j + 1) * t]
                dv = dv + wcb_ref[j:j + 1, :] * back
                part_ref[4 + j:5 + j, :] += jnp.sum(back * v, axis=0, keepdims=True)
            out_ref[...] = dv.astype(BF16)
            part_ref[3:4, :] += jnp.sum(dxb, axis=0, keepdims=True)

    full = lambda shape: pl.BlockSpec(shape, lambda i: (0,) * len(shape))
    kind = lambda i: (jnp.where(i == nt, 1, 0), 0, 0, 0)
    in_specs = [pl.BlockSpec((t, w), lambda i: (tile(i), 0)),
                pl.BlockSpec((t, w), lambda i: (jnp.minimum(tile(i), nt - 1), 0)),
                pl.BlockSpec((t, w), lambda i: (tile(i), 0)),
                pl.BlockSpec((8, w), lambda i: (halo(i), 0)),
                full(wg.shape), full(lv.shape)]
    args = [xb, dhs, hs, hs, wg, lv]
    more_out, more_spec, more_scratch = (), (), []
    if last:
        p, wcb, back_m, dxb_other, dp, g_wout = conv
        in_specs += [pl.BlockSpec((t, w), lambda i: (tile(i), 4)), full(wcb.shape),
                     pl.BlockSpec((None, 4, t, t), kind), pl.BlockSpec((t, w), lambda i: (tile(i), 0)), ANY, ANY]
        args += [p, wcb, back_m, dxb_other, dp, g_wout]
        out0 = jax.ShapeDtypeStruct(dp.shape, dp.dtype)
        spec0 = pl.BlockSpec((t, w), lambda i: (tile(i), 4))
        more_out, more_spec = (jax.ShapeDtypeStruct(g_wout.shape, g_wout.dtype),), (ANY,)
        more_scratch = [pltpu.SemaphoreType.DMA((NDEV,)), pltpu.SemaphoreType.DMA((NDEV,))]
        aliases = {10: 0}
    else:
        out0 = jax.ShapeDtypeStruct((la, w), F32)
        spec0 = pl.BlockSpec((t, w), lambda i: (tile(i), 0))
        aliases = {}
    return _call(
        body, name="lru_backward_%d" % direction,
        grid=(nt + 1,),
        out_shape=(out0, jax.ShapeDtypeStruct((ng, gc, 2 * gc), F32), jax.ShapeDtypeStruct((8, w), F32)) + more_out,
        in_specs=in_specs,
        out_specs=(spec0, full((ng, gc, 2 * gc)), full((8, w))) + more_spec,
        scratch_shapes=[pltpu.VMEM((t, w), F32), pltpu.VMEM((t, w), F32), pltpu.VMEM((t, w), F32),
                        pltpu.VMEM((8, w), F32)] + more_scratch,
        input_output_aliases=aliases,
        compiler_params=_params(("arbitrary",)),
    )(*args)


def _weight_grad_t(a, b, nblk_m, nblk_n, tk, name):
    k, m = a.shape
    n = b.shape[1]
    bm, bn = m // nblk_m, n // nblk_n
    nk = k // tk

    def body(a_ref, b_ref, o_ref, acc):
        kk = pl.program_id(2)

        @pl.when(kk == 0)
        def _():
            acc[...] = jnp.zeros_like(acc)

        acc[...] += lax.dot_general(a_ref[...], b_ref[...], (((0,), (0,)), ((), ())), preferred_element_type=F32)

        @pl.when(kk == nk - 1)
        def _():
            o_ref[...] = acc[...].astype(BF16)

    return _call(
        body, name=name,
        grid=(nblk_m, nblk_n, nk),
        out_shape=jax.ShapeDtypeStruct((nblk_m * nblk_n, bm, bn), BF16),
        in_specs=[pl.BlockSpec((tk, bm), lambda i, j, kk: (kk, i)),
                  pl.BlockSpec((tk, bn), lambda i, j, kk: (kk, j))],
        out_specs=pl.BlockSpec((None, bm, bn), lambda i, j, kk: (i * nblk_n + j, 0, 0)),
        scratch_shapes=[pltpu.VMEM((bm, bn), F32)],
        compiler_params=_params(("arbitrary", "arbitrary", "arbitrary")),
    )(a, b)


def _weight_grad_scatter(at, b, lru_parts, packed, tk, name):
    k, m = at.shape
    n = b.shape[1]
    bn = n // NDEV
    nk = k // tk
    rl = lru_parts.shape[1]
    rp = packed.shape[0]
    where = jnp.stack([_idx(_my_pos()), lax.axis_index("c")]).astype(jnp.int32)
    tn = (((0,), (0,)), ((), ()))

    def body(w_ref, a_ref, b_ref, l_ref, p_ref, recv_ref, lru_ref, psum_ref, pall_ref, acc, sbuf, sib, lbuf, lsum,
             lall, pall, sib_send, sib_recv, chip_send, chip_recv, keep_sem, l_send, l_recv, g_send, g_recv,
             p_send, p_recv):
        s, kk = pl.program_id(0), pl.program_id(1)
        x, y, c = _my_pos()
        scattered = _scatter_copies(l_ref, lbuf, l_send, l_recv)
        gathered, arrivals = _gather_copies(lsum, lall, g_send, g_recv)
        packed_sends, packed_arrivals = _gather_copies(p_ref, pall, p_send, p_recv)

        @pl.when((s == 0) & (kk == 0))
        def _():
            for cp in scattered + packed_sends:
                cp.start()

        @pl.when((s == NDEV // 2) & (kk == 0))
        def _():
            _exchange_wait(scattered, scattered[1:])
            red = lbuf[0]
            for j in range(1, NDEV):
                red = red + lbuf[j]
            lsum[...] = red
            for cp in gathered:
                cp.start()

        @pl.when(kk == 0)
        def _():
            acc[...] = lax.dot_general(a_ref[...], b_ref[...], tn, preferred_element_type=F32)

        @pl.when(kk > 0)
        def _():
            acc[...] += lax.dot_general(a_ref[...], b_ref[...], tn, preferred_element_type=F32)

        def to_sibling(j):
            return pltpu.make_async_remote_copy(
                src_ref=sbuf.at[0], dst_ref=sib.at[j], send_sem=sib_send.at[j], recv_sem=sib_recv.at[j],
                device_id=(x, y, 1 - c), device_id_type=MESH)

        def to_chip(j):
            dist = _chip_order(j, c)
            return pltpu.make_async_remote_copy(
                src_ref=sbuf.at[1], dst_ref=recv_ref.at[dist // 2], send_sem=chip_send.at[j],
                recv_sem=chip_recv.at[dist // 2], device_id=_peer_at(dist), device_id_type=MESH)

        keep = pltpu.make_async_copy(sbuf.at[1], recv_ref.at[0], keep_sem)
        sends = []
        for j in range(4):
            sends += [to_sibling(j), to_chip(j) if j < 3 else keep]

        for st in range(NDEV):
            @pl.when((kk == nk - 1) & (s == st))
            def _(st=st):
                if st >= 2:
                    sends[st - 2].wait_send()
                part = acc[...]
                if st % 2 == 1:
                    to_sibling(st // 2).wait_recv()
                    part = part + sib[st // 2].astype(F32)
                sbuf[st % 2] = part.astype(BF16)
                sends[st].start()
                if st == NDEV - 1:
                    sends[st - 1].wait_send()
                    sends[st].wait()
                    for j in range(1, 4):
                        pltpu.make_async_remote_copy(
                            src_ref=sbuf.at[0], dst_ref=recv_ref.at[j], send_sem=chip_send.at[0],
                            recv_sem=chip_recv.at[j], device_id=_peer_at(2 * j), device_id_type=MESH).wait_recv()
                    _exchange_wait(gathered, arrivals)
                    lru_ref[...] = lall[...]
                    _exchange_wait(packed_sends, packed_arrivals)
                    total = pall[0]
                    for j in range(1, NDEV):
                        total = total + pall[j]
                    psum_ref[...] = total
                    pall_ref[...] = pall[...]

    blk = lambda s, w_ref: w_ref[0] ^ _scatter_order(s, w_ref[1])
    return _call(
        body, name=name,
        grid_spec=pltpu.PrefetchScalarGridSpec(
            num_scalar_prefetch=1, grid=(NDEV, nk),
            in_specs=[pl.BlockSpec((tk, m), lambda s, kk, w_ref: (kk, 0)),
                      pl.BlockSpec((tk, bn), lambda s, kk, w_ref: (kk, blk(s, w_ref))), ANY, ANY],
            out_specs=(ANY, pl.BlockSpec((NDEV, rl, 128), lambda s, kk, w_ref: (0, 0, 0)),
                       pl.BlockSpec((rp, 128), lambda s, kk, w_ref: (0, 0)),
                       pl.BlockSpec((NDEV, rp, 128), lambda s, kk, w_ref: (0, 0, 0))),
            scratch_shapes=[pltpu.VMEM((m, bn), F32), pltpu.VMEM((2, m, bn), BF16), pltpu.VMEM((4, m, bn), BF16),
                            pltpu.VMEM((NDEV, rl, 128), F32), pltpu.VMEM((rl, 128), F32),
                            pltpu.VMEM((NDEV, rl, 128), F32), pltpu.VMEM((NDEV, rp, 128), F32),
                            pltpu.SemaphoreType.DMA((4,)), pltpu.SemaphoreType.DMA((4,)),
                            pltpu.SemaphoreType.DMA((4,)), pltpu.SemaphoreType.DMA((4,)),
                            pltpu.SemaphoreType.DMA,
                            pltpu.SemaphoreType.DMA((NDEV,)), pltpu.SemaphoreType.DMA((NDEV,)),
                            pltpu.SemaphoreType.DMA((NDEV,)), pltpu.SemaphoreType.DMA((NDEV,)),
                            pltpu.SemaphoreType.DMA((NDEV,)), pltpu.SemaphoreType.DMA((NDEV,))]),
        out_shape=(jax.ShapeDtypeStruct((4, m, bn), BF16), jax.ShapeDtypeStruct((NDEV, rl, 128), F32),
                   jax.ShapeDtypeStruct((rp, 128), F32), jax.ShapeDtypeStruct((NDEV, rp, 128), F32)),
        compiler_params=_params(("arbitrary", "arbitrary")),
    )(where, at, b, lru_parts, packed)


def _input_backward(dp, w_all, src, mv, row0, tm, nbk, name, dn=None, cols=None):
    rows, d = src.shape
    nb, _, bw = w_all.shape
    first, last = (0, nb * bw - 1) if cols is None else cols
    k0 = first // (nbk * bw)
    nk = last // (nbk * bw) - k0 + 1
    ni = rows // tm
    blk0 = row0 // tm
    latent = dn is not None

    def body(*refs):
        dp_ref, w_ref, x_ref, mv_ref = refs[:4]
        outs = refs[4 + latent:]
        part_ref, acc = outs[latent], outs[latent + 1]
        i, k = pl.program_id(0), pl.program_id(1)

        def product():
            step = _dot_nt(dp_ref[:, 0:bw], w_ref[0])
            for q in range(1, nbk):
                step = step + _dot_nt(dp_ref[:, q * bw:(q + 1) * bw], w_ref[q])
            return step

        def finish(slot):
            xf = x_ref[...]
            r = lax.rsqrt(jnp.mean(xf * xf, axis=-1, keepdims=True) + EPS)
            xn = xf * r
            dhl = acc[slot]
            gain, sc = mv_ref[0:1, :], mv_ref[1:2, :]
            dhx = jnp.sum(dhl * xn, axis=0, keepdims=True)
            part_ref[0:1, :] += jnp.sum(dhl, axis=0, keepdims=True)
            part_ref[1:2, :] += dhx * gain
            part_ref[2:3, :] += dhx * (1.0 + sc)
            if latent:
                dxn = dhl * (gain * (1.0 + sc))
                outs[0][...] = (refs[4][...].astype(F32)
                                + r * (dxn - xn * jnp.mean(dxn * xn, axis=-1, keepdims=True)))

        @pl.when((i == 0) & (k == 0))
        def _():
            part_ref[...] = jnp.zeros_like(part_ref)
            acc[0] = product()

        @pl.when((i > 0) & (i < ni) & (k == 0))
        def _():
            acc[i % 2] = product()
            finish((i - 1) % 2)

        @pl.when((i == ni) & (k == 0))
        def _():
            finish((ni - 1) % 2)

        @pl.when((i < ni) & (k > 0))
        def _():
            acc[i % 2] += product()

    tile = pl.BlockSpec((tm, d), lambda i, k: (jnp.maximum(i - 1, 0), 0))
    vec = pl.BlockSpec((8, d), lambda i, k: (0, 0))
    kblock = lambda i, k: k0 + jnp.where(i == ni, nk - 1, k)
    return _call(
        body, name=name,
        grid=(ni + 1, nk),
        out_shape=((jax.ShapeDtypeStruct((rows, d), F32),) if latent else ()) + (jax.ShapeDtypeStruct((8, d), F32),),
        in_specs=[pl.BlockSpec((tm, nbk * bw), lambda i, k: (blk0 + jnp.minimum(i, ni - 1), kblock(i, k))),
                  pl.BlockSpec((nbk, d, bw), lambda i, k: (kblock(i, k), 0, 0)), tile, vec]
                 + ([tile] if latent else []),
        out_specs=((tile,) if latent else ()) + (vec,),
        scratch_shapes=[pltpu.VMEM((2, tm, d), F32)],
        compiler_params=_params(("arbitrary", "arbitrary")),
    )(*([dp, w_all, src, mv] + ([dn] if latent else [])))


def _adamw_scattered(parts, w, m, v, tr):
    r, c = w.shape
    nslot = parts.shape[0]

    def body(p_ref, w_ref, m_ref, v_ref, g_ref, d_ref, m2_ref, v2_ref):
        g = p_ref[0].astype(F32)
        for k in range(1, nslot):
            g = g + p_ref[k].astype(F32)
        g_ref[...] = g
        d_ref[...], m2_ref[...], v2_ref[...] = _adamw(w_ref[...], g, m_ref[...], v_ref[...])

    tile = pl.BlockSpec((tr, c), lambda i: (i, 0))
    return _call(
        body, name="adamw_scattered_%dx%d" % (r, c),
        grid=(r // tr,),
        out_shape=tuple(jax.ShapeDtypeStruct((r, c), F32) for _ in range(4)),
        in_specs=[pl.BlockSpec((nslot, tr, c), lambda i: (0, i, 0)), tile, tile, tile],
        out_specs=(tile,) * 4,
        compiler_params=_params(("arbitrary",)),
    )(parts, w, m, v)


def _adamw_ada(st, dmod, w, m, v, tr):
    r, c = w.shape

    def body(s_ref, dm_ref, w_ref, m_ref, v_ref, g_ref, d_ref, m2_ref, v2_ref):
        g = jnp.dot(s_ref[...], dm_ref[...], precision=HIGHEST, preferred_element_type=F32)
        g_ref[...] = g
        d_ref[...], m2_ref[...], v2_ref[...] = _adamw(w_ref[...], g, m_ref[...], v_ref[...])

    tile = pl.BlockSpec((tr, c), lambda i: (i, 0))
    return _call(
        body, name="adamw_ada",
        grid=(r // tr,),
        out_shape=tuple(jax.ShapeDtypeStruct((r, c), F32) for _ in range(4)),
        in_specs=[pl.BlockSpec((tr, 16), lambda i: (i, 0)), pl.BlockSpec((16, c), lambda i: (0, 0)),
                  tile, tile, tile],
        out_specs=(tile,) * 4,
        compiler_params=_params(("arbitrary",)),
    )(st, dmod, w, m, v)


def _adamw_small(gs, ws, ms, vs):
    n = len(ws)

    def body(*refs):
        for j in range(n):
            g_ref, w_ref, m_ref, v_ref = refs[j], refs[n + j], refs[2 * n + j], refs[3 * n + j]
            d_ref, m2_ref, v2_ref = refs[4 * n + j], refs[5 * n + j], refs[6 * n + j]
            d_ref[...], m2_ref[...], v2_ref[...] = _adamw(w_ref[...], g_ref[...], m_ref[...], v_ref[...])

    shapes = tuple(jax.ShapeDtypeStruct(a.shape, F32) for a in ws)
    out = _call(
        body, name="adamw_small",
        out_shape=shapes * 3,
        in_specs=[VMEM] * (4 * n), out_specs=(VMEM,) * (3 * n),
        compiler_params=_params(),
    )(*gs, *ws, *ms, *vs)
    return list(out[:n]), list(out[n:2 * n]), list(out[2 * n:])


def _blockdiag_groups(wh, gc):
    h, dh, _ = wh.shape
    g = gc // dh
    w4 = wh.reshape(h // g, g, dh, dh)
    bd = jnp.einsum("ngij,gh->ngihj", w4, jnp.eye(g, dtype=wh.dtype))
    return bd.reshape(h // g, gc, gc)


def _blockdiag_extract(bd, dh):
    ng, gc, _ = bd.shape
    g = gc // dh
    x = bd.reshape(ng, g, dh, g, dh)
    return jnp.einsum("ngihj,gh->ngij", x, jnp.eye(g, dtype=bd.dtype)).reshape(ng * g, dh, dh)


def _largest_tile(n, cap):
    return max(q for q in range(128, min(n, cap) + 1, 128) if n % q == 0)


def _rows8(*vecs):
    rows = [jnp.reshape(v, (1, -1)).astype(F32) for v in vecs]
    n = rows[0].shape[1]
    return jnp.concatenate(rows + [jnp.zeros((8 - len(rows), n), F32)], axis=0)


def _pack(pieces):
    flat = jnp.concatenate([jnp.reshape(a, (-1,)).astype(F32) for a in pieces])
    total = -(-flat.shape[0] // 1024) * 1024
    return jnp.pad(flat, (0, total - flat.shape[0])).reshape(total // 128, 128)


def _unpack(packed, shapes):
    flat = packed.reshape(-1)
    out, off = [], 0
    for s in shapes:
        n = 1
        for q in s:
            n *= q
        out.append(flat[off:off + n].reshape(s))
        off += n
    return out


def kernel(x, c, ctx, c_ctx, norm_g, w_ada, b_ada, w_in, w_conv_a, w_conv_b, b_conv_b, lru_wa, lru_ba, lru_wx, lru_bx, lru_lambda, w_out, final_g, loss_target, m_c_ctx, m_norm_g, m_w_ada, m_b_ada, m_w_in, m_w_conv_a, m_w_conv_b, m_b_conv_b, m_lru_wa, m_lru_ba, m_lru_wx, m_lru_bx, m_lru_lambda, m_w_out, m_final_g, v_c_ctx, v_norm_g, v_w_ada, v_b_ada, v_w_in, v_w_conv_a, v_w_conv_b, v_b_conv_b, v_lru_wa, v_lru_ba, v_lru_wx, v_lru_bx, v_lru_lambda, v_w_out, v_final_g):
    _, l, d = x.shape
    lc = ctx.shape[1]
    w = d // 2
    t = lc
    assert l % t == 0 and t % GRID_W == 0 and t % 128 == 0
    dh = w // N_HEADS
    gc = min(w, MXU_WIDTH)
    cols = w_ada.shape[2]
    wo_rows = w_out.shape[1]
    me = _idx(_my_pos())
    x2, ctx2, tgt2 = x[0], ctx[0], loss_target[0]
    w_ada2, w_in2, w_out2 = w_ada[0], w_in[0], w_out[0]

    small_mine = jnp.concatenate([a.reshape(-1) for a in (w_conv_a, w_conv_b, lru_ba, lru_bx, lru_lambda)]
                                 + [jnp.zeros((3 * (w // NDEV),), F32)]).reshape(16, w // NDEV)
    mod_all, s_mat, small_all = _mod_forward(
        jnp.broadcast_to(c, (8, d)), jnp.broadcast_to(c_ctx[None], (8, d)), w_ada2, small_mine)
    mod = jnp.transpose(mod_all, (1, 0, 2)).reshape(16, NDEV * cols) + b_ada
    mod_lat = lax.dynamic_slice_in_dim(mod, me, 1, axis=0)
    sh_l, sc_l, gt_l = jnp.split(mod_lat, 3, axis=-1)
    sh_c, sc_c, _ = jnp.split(mod[8:9], 3, axis=-1)
    small = jnp.transpose(small_all, (1, 0, 2)).reshape(16, w)
    wca = _rows8(*[small[j] for j in range(0, 3)])
    wcb = _rows8(*[small[j] for j in range(3, 7)], b_conv_b)
    lv = _rows8(0.5 * small[7], 0.5 * small[9], small[11], 0.5 * small[8], 0.5 * small[10], small[12])
    wg = jnp.stack([
        jnp.concatenate([_blockdiag_groups(lru_wa[0, dr], gc), _blockdiag_groups(lru_wx[0, dr], gc)], axis=-1)
        for dr in range(2)])
    wg = (0.5 * wg).astype(BF16)

    la = l + lc
    tm = 2 * t if l % (2 * t) == 0 else t
    tk = 3 * t if la % (3 * t) == 0 else t
    h = _normalize(x2, _rows8(norm_g, sc_l, sh_l), la, 0, tm, "normalize")
    h = _normalize(ctx2, _rows8(norm_g, sc_c, sh_c), la, l, t, "normalize_ctx", prev=h)
    p, w_all, wo_all = _in_projection(h, w_in2.astype(BF16), w_out2.astype(BF16), la // 4 if la % 64 == 0 else tk)
    taps_m, back_m, perm = _scan_matrices(t)
    xb = _conv_input(p, wcb, taps_m, l, t)
    hf, hr = _lru_forward(xb, wg, lv, l, t)
    wo = wo_all.reshape(d, d)
    dn, cat, dout, part_mix = _mix_forward(x2, tgt2, p, hf, hr, wo, _rows8(gt_l, final_g), wca, perm, t)
    g_wout = _weight_grad_t(cat, dout, 2, 1, _largest_tile(l, 2048), "grad_w_out")
    dp, dhs, part_ca = _mix_backward(dout, p, hf, hr, wo, wca, perm, l, t)
    dxb0, dwg0, part_l0 = _lru_backward(0, xb, dhs, hf, wg, lv, l, t)
    dp, dwg1, part_l1, sc_wout = _lru_backward(
        1, xb, dhs, hr, wg, lv, l, t, conv=(p, wcb, back_m, dxb0, dp, g_wout.reshape(NDEV, wo_rows, d)))
    dwa = jnp.stack([_blockdiag_extract(dwg0[:, :, :gc], dh), _blockdiag_extract(dwg1[:, :, :gc], dh)])
    dwx = jnp.stack([_blockdiag_extract(dwg0[:, :, gc:], dh), _blockdiag_extract(dwg1[:, :, gc:], dh)])
    lru_part = (0.5 * jnp.stack([dwa, dwx])).reshape(NDEV, -1, 128)
    grad_x, part_lat = _input_backward(dp, w_all, x2, _rows8(norm_g, sc_l), 0, tm, 2, "input_backward", dn=dn)
    (part_ctx,) = _input_backward(dp, w_all, ctx2, _rows8(norm_g, sc_c), l, t, 2, "input_backward_ctx",
                                  cols=(4 * w, 5 * w - 1))
    part_in = jnp.concatenate([part_lat[0:2], part_ctx[0:2], (part_lat[2] + part_ctx[2])[None]], axis=0)

    zeros_d = jnp.zeros((d,), F32)
    pieces = [
        jnp.concatenate([part_in[0], part_in[1], part_mix[1]]),
        jnp.concatenate([part_in[2], part_in[3], zeros_d]),
        part_in[4], part_mix[0], part_ca[0:3], part_l1[4:8], part_l1[3],
        0.5 * jnp.stack([part_l0[0], part_l1[0]]), 0.5 * jnp.stack([part_l0[1], part_l1[1]]),
        jnp.stack([part_l0[2], part_l1[2]]), part_mix[2, 0:1],
    ]
    shapes = [(3 * d,), (3 * d,), (d,), (d,), (3, w), (4, w), (w,), (2, w), (2, w), (2, w), (1,)]
    sig_cc = jax.nn.sigmoid(c_ctx)
    dsilu_cc = jnp.broadcast_to((sig_cc * (1.0 + c_ctx * (1.0 - sig_cc)))[None], (8, d))
    sc_win, lru_sum, psum, pall = _weight_grad_scatter(h, dp, lru_part, _pack(pieces), tk, "grad_w_in")
    g_cctx8 = _reduce_small(psum, w_ada2, dsilu_cc)
    (g_modl, g_modc, g_norm, g_final, g_ca, g_cb, g_bcb, g_ba, g_bx, g_lam, loss1) = _unpack(psum, shapes)
    loss = loss1[0]
    g_cctx = g_cctx8[0]
    g_bada = (g_modl + g_modc)[None]
    g_lru = lru_sum.reshape(2, 2, N_HEADS, dh, dh)
    g_wa, g_wx = g_lru[0][None], g_lru[1][None]
    wsl = w // NDEV
    mine = lambda a: lax.dynamic_slice_in_dim(a, me * wsl, wsl, axis=-1)
    g_ca_m, g_cb_m, g_ba_m, g_bx_m, g_lam_m = (mine(g_ca)[None], mine(g_cb)[None], mine(g_ba)[None],
                                               mine(g_bx)[None], mine(g_lam)[None])
    g_norm, g_bcb = g_norm[None], g_bcb[None]

    per_dev = pall[:, :3 * d // 128].reshape(NDEV, NDEV, cols)
    dmod_lat = lax.dynamic_slice_in_dim(per_dev, me, 1, axis=1)[:, 0]
    dmod_ctx = lax.dynamic_slice_in_dim(g_modc.reshape(NDEV, cols), me, 1, axis=0)
    dmod16 = jnp.concatenate([dmod_lat, dmod_ctx, jnp.zeros((7, cols), F32)], axis=0)
    tr_ada = 256 if d % 256 == 0 else d
    g_wada, d_wada, m_wada, v_wada = _adamw_ada(s_mat.T, dmod16, w_ada2, m_w_ada[0], v_w_ada[0], tr_ada)
    g_win2, d_win, m_win, v_win = _adamw_scattered(sc_win, w_in2, m_w_in[0], v_w_in[0], tr_ada)
    tr_out = 64 if wo_rows % 64 == 0 else wo_rows
    g_wout2, d_wout, m_wout, v_wout = _adamw_scattered(sc_wout, w_out2, m_w_out[0], v_w_out[0], tr_out)

    small_w = [c_ctx, norm_g, b_ada, w_conv_a, w_conv_b, b_conv_b, lru_wa, lru_ba, lru_wx, lru_bx, lru_lambda, final_g]
    small_m = [m_c_ctx, m_norm_g, m_b_ada, m_w_conv_a, m_w_conv_b, m_b_conv_b, m_lru_wa, m_lru_ba, m_lru_wx,
               m_lru_bx, m_lru_lambda, m_final_g]
    small_v = [v_c_ctx, v_norm_g, v_b_ada, v_w_conv_a, v_w_conv_b, v_b_conv_b, v_lru_wa, v_lru_ba, v_lru_wx,
               v_lru_bx, v_lru_lambda, v_final_g]
    small_g = [g_cctx, g_norm, g_bada, g_ca_m, g_cb_m, g_bcb, g_wa, g_ba_m, g_wx, g_bx_m, g_lam_m, g_final]
    small_g = [jnp.reshape(a, b.shape) for a, b in zip(small_g, small_w)]
    d_s, m_s, v_s = _adamw_small(small_g, small_w, small_m, small_v)

    def weights(small_list, ada, win, wout):
        (cctx_, norm_, bada_, ca_, cb_, bcb_, wa_, ba_, wx_, bx_, lam_, final_) = small_list
        return [cctx_, norm_, ada[None], bada_, win[None], ca_, cb_, bcb_, wa_, ba_, wx_, bx_, lam_, wout[None], final_]

    return (loss, grad_x[None],
            *weights(small_g, g_wada, g_win2, g_wout2), *weights(d_s, d_wada, d_win, d_wout),
            *weights(m_s, m_wada, m_win, m_wout), *weights(v_s, v_wada, v_win, v_wout))
```

```python
import functools

import jax
import jax.numpy as jnp
import numpy as np
from jax import lax
from jax.experimental import pallas as pl
from jax.experimental.pallas import tpu as pltpu

F32 = jnp.float32
BF16 = jnp.bfloat16
MESH = pl.DeviceIdType.MESH
NDEV = 8
GRID_W = 64
N_HEADS = 16
LRU_C = 8.0
EPS = 1e-6
MXU_WIDTH = 256
VMEM_LIMIT = 60 * 1024 * 1024

ADAM_LR = 0.001
ADAM_B1 = 0.9
ADAM_B2 = 0.999
ADAM_EPS = 1e-08
ADAM_WD = 0.01
ADAM_STEP = 10
ADAM_C1 = 1.0 - ADAM_B1 ** ADAM_STEP
ADAM_C2 = 1.0 - ADAM_B2 ** ADAM_STEP

HIGHEST = lax.Precision.HIGHEST
ANY = pl.BlockSpec(memory_space=pl.ANY)
VMEM = pl.BlockSpec(memory_space=pltpu.VMEM)


def _call(body, **kw):
    return pl.pallas_call(body, **kw)


def _params(sem=None, vmem=VMEM_LIMIT):
    return pltpu.CompilerParams(dimension_semantics=sem, vmem_limit_bytes=vmem)


def _my_pos():
    return lax.axis_index("x"), lax.axis_index("y"), lax.axis_index("c")


def _idx(pos):
    return 4 * pos[0] + 2 * pos[1] + pos[2]


def _peer(k):
    x, y, c = _my_pos()
    return ((1 - x) if (k >> 2) & 1 else x, (1 - y) if (k >> 1) & 1 else y, (1 - c) if k & 1 else c)


def _exchange_pushes(src_ref, dst_ref, send_sems, recv_sems, base):
    me = _idx(_my_pos())
    sends = [pltpu.make_async_remote_copy(
        src_ref=src_ref, dst_ref=dst_ref.at[me], send_sem=send_sems.at[base + k - 1],
        recv_sem=recv_sems.at[base + k - 1], device_id=_peer(k), device_id_type=MESH) for k in range(1, NDEV)]
    return sends, (src_ref, dst_ref, send_sems, recv_sems, base)


def _exchange_start(src_ref, dst_ref, send_sems, recv_sems, base):
    started = _exchange_pushes(src_ref, dst_ref, send_sems, recv_sems, base)
    for cp in started[0]:
        cp.start()
    dst_ref[_idx(_my_pos())] = src_ref[...]
    return started


def _exchange_finish(started):
    sends, (src_ref, dst_ref, send_sems, recv_sems, base) = started
    for k in range(1, NDEV):
        peer = _peer(k)
        pltpu.make_async_remote_copy(
            src_ref=src_ref, dst_ref=dst_ref.at[_idx(peer)], send_sem=send_sems.at[base + k - 1],
            recv_sem=recv_sems.at[base + k - 1], device_id=peer, device_id_type=MESH).wait_recv()
    for cp in sends:
        cp.wait_send()


def _exchange_vmem(src_ref, dst_ref, send_sems, recv_sems, base):
    _exchange_finish(_exchange_start(src_ref, dst_ref, send_sems, recv_sems, base))


def _sigmoid(z):
    return 0.5 * jnp.tanh(0.5 * z) + 0.5


def _softplus(x):
    return jnp.maximum(x, 0.0) + jnp.log1p(jnp.exp(-jnp.abs(x)))


def _one_minus_sq(a, la):
    series = (-2.0 * la) * (1.0 + la)
    return jnp.where(la > -0.0015, series, 1.0 - a * a)


def _dot(a, b):
    return jnp.dot(a, b, preferred_element_type=F32)


def _dot_nt(a, b):
    return lax.dot_general(a, b, (((1,), (1,)), ((), ())), preferred_element_type=F32)


def _rows(shape):
    return lax.broadcasted_iota(jnp.int32, shape, 0)


def _scan_matrices(t):
    seg = t // 8
    r = np.arange(t)
    perm = (np.arange(t)[None, :] == ((r % 8) * seg + r // 8)[:, None]).astype(np.float32)
    rows, cols = r[:, None], r[None, :]
    taps, back = [], []
    for rowlen in (GRID_W, t):
        pos = rows % rowlen
        shift = {-2: (cols == rows - 2) & (pos >= 2), -1: (cols == rows - 1) & (pos >= 1),
                 0: cols == rows, 1: (cols == rows + 1) & (pos + 1 < rowlen),
                 2: (cols == rows + 2) & (pos + 2 < rowlen)}
        if rowlen == GRID_W:
            beside = [shift[-1].astype(np.float32), shift[1].astype(np.float32)]
        taps.append(np.stack([perm @ shift[k].astype(np.float32) for k in (-2, -1, 0, 1)]))
        back.append(np.stack([shift[k].astype(np.float32) @ perm.T for k in (2, 1, 0, -1)]))
    as_bf16 = lambda a: jnp.asarray(a, dtype=BF16)
    return as_bf16(np.stack(taps)), as_bf16(np.stack(back)), as_bf16(np.stack([perm, perm.T] + beside))


def _chunk_scan(a, b, reverse):
    row = _rows(a.shape)
    for s in (1, 2, 4):
        if reverse:
            m = row < 8 - s
            sh = 8 - s
        else:
            m = row >= s
            sh = s
        a_s = jnp.where(m, pltpu.roll(a, sh, 0), 1.0)
        b_s = jnp.where(m, pltpu.roll(b, sh, 0), 0.0)
        b = b + a * b_s
        a = a * a_s
    return a, b


def _chain_segments(ptot, hend, carry, reverse):
    ca, cb = _chunk_scan(ptot, hend, reverse)
    incl = ca * carry + cb
    r8 = _rows(incl.shape)
    if reverse:
        start = jnp.where(r8 < 7, pltpu.roll(incl, 7, 0), carry)
        last = incl[0:1, :]
    else:
        start = jnp.where(r8 >= 1, pltpu.roll(incl, 1, 0), carry)
        last = incl[7:8, :]
    return start, jnp.broadcast_to(last, incl.shape)


def _blocks(nblock, reverse):
    order = range(nblock - 1, -1, -1) if reverse else range(nblock)
    return [slice(8 * k, 8 * k + 8) for k in order]


def _scan_tile(a_ref, b_ref, out_ref, carry, reverse):
    t, w = a_ref.shape
    seg = t // 8

    hend, ptot = jnp.zeros((8, w), F32), jnp.ones((8, w), F32)
    for rows in _blocks(seg, reverse):
        a = a_ref[rows, :]
        hend, ptot = a * hend + b_ref[rows, :], a * ptot
    h, new_carry = _chain_segments(ptot, hend, carry, reverse)
    for rows in _blocks(seg, reverse):
        h = a_ref[rows, :] * h + b_ref[rows, :]
        out_ref[rows, :] = h
    return new_carry


def _scan_tile_backward(a_ref, dh_ref, g_ref, carry, reverse):
    t, w = a_ref.shape
    seg = t // 8

    uend, ptot = jnp.zeros((8, w), F32), jnp.ones((8, w), F32)
    for rows in _blocks(seg, reverse):
        a = a_ref[rows, :]
        uend, ptot = a * (dh_ref[rows, :] + uend), a * ptot
    u, new_carry = _chain_segments(ptot, uend, carry, reverse)
    for rows in _blocks(seg, reverse):
        g = dh_ref[rows, :] + u
        g_ref[rows, :] = g
        u = a_ref[rows, :] * g
    return new_carry


def _lru_coef(xb, wg_ref, d, ba, bx, lam, gc):
    w = xb.shape[1]
    xb16 = xb.astype(BF16)
    zr, zi = [], []
    for g in range(w // gc):
        z = _dot(xb16[:, g * gc:(g + 1) * gc], wg_ref[d, g])
        zr.append(z[:, :gc])
        zi.append(z[:, gc:])
    zr = zr[0] if len(zr) == 1 else jnp.concatenate(zr, axis=-1)
    zi = zi[0] if len(zi) == 1 else jnp.concatenate(zi, axis=-1)
    tr = jnp.tanh(zr + ba)
    ti = jnp.tanh(zi + bx)
    sp = _softplus(-lam)
    half = -0.5 * LRU_C * sp
    la = tr * half + half
    a = jnp.exp(la)
    q = _one_minus_sq(a, la)
    rs = lax.rsqrt(jnp.maximum(q, 1e-30))
    return a, q * rs, rs, tr, ti, sp


def _adamw(w, g, m, v):
    m2 = ADAM_B1 * m + (1.0 - ADAM_B1) * g
    v2 = ADAM_B2 * v + (1.0 - ADAM_B2) * (g * g)
    m_hat = m2 / ADAM_C1
    v_hat = v2 / ADAM_C2
    delta = -ADAM_LR * (m_hat / (jnp.sqrt(v_hat) + ADAM_EPS) + ADAM_WD * w)
    return delta, m2, v2


def _mod_forward(c8, cctx8, w_ada, small):
    d = c8.shape[1]
    cols = w_ada.shape[1]

    def body(c_ref, cctx_ref, w_ref, sm_ref, mod_ref, s_ref, sm_all, cbuf, mod_my, send_sems, recv_sems):
        _exchange_vmem(sm_ref, sm_all, send_sems, recv_sems, 2 * (NDEV - 1))
        _exchange_vmem(c_ref, cbuf, send_sems, recv_sems, 0)
        row = _rows((8, d))
        c_all = jnp.zeros((8, d), F32)
        for b in range(NDEV):
            c_all = jnp.where(row == b, cbuf[b], c_all)
        cc = cctx_ref[...]
        s_top = c_all * _sigmoid(c_all)
        s_bot = jnp.where(row == 0, cc * _sigmoid(cc), 0.0)
        s = jnp.concatenate([s_top, s_bot], axis=0)
        s_ref[...] = s
        mod_my[...] = jnp.dot(s, w_ref[...], precision=HIGHEST, preferred_element_type=F32)
        _exchange_vmem(mod_my, mod_ref, send_sems, recv_sems, NDEV - 1)

    return _call(
        body, name="mod_forward",
        out_shape=(jax.ShapeDtypeStruct((NDEV, 16, cols), F32), jax.ShapeDtypeStruct((16, d), F32),
                   jax.ShapeDtypeStruct((NDEV,) + small.shape, F32)),
        in_specs=[VMEM] * 4, out_specs=(VMEM,) * 3,
        scratch_shapes=[pltpu.VMEM((NDEV, 8, d), F32), pltpu.VMEM((16, cols), F32),
                        pltpu.SemaphoreType.DMA((3 * (NDEV - 1),)), pltpu.SemaphoreType.DMA((3 * (NDEV - 1),))],
        compiler_params=_params(),
    )(c8, cctx8, w_ada, small)


def _scatter_copies(src_ref, dst_ref, send_sems, recv_sems):
    me = _idx(_my_pos())
    copies = [pltpu.make_async_copy(src_ref.at[me], dst_ref.at[0], send_sems.at[0])]
    for k in range(1, NDEV):
        peer = _peer(k)
        copies.append(pltpu.make_async_remote_copy(
            src_ref=src_ref.at[_idx(peer)], dst_ref=dst_ref.at[k], send_sem=send_sems.at[k],
            recv_sem=recv_sems.at[k], device_id=peer, device_id_type=MESH))
    return copies


def _gather_copies(src_ref, dst_ref, send_sems, recv_sems):
    me = _idx(_my_pos())
    sends = [pltpu.make_async_copy(src_ref, dst_ref.at[me], send_sems.at[0])]
    arrivals = []
    for k in range(1, NDEV):
        peer = _peer(k)
        sends.append(pltpu.make_async_remote_copy(
            src_ref=src_ref, dst_ref=dst_ref.at[me], send_sem=send_sems.at[k],
            recv_sem=recv_sems.at[k], device_id=peer, device_id_type=MESH))
        arrivals.append(pltpu.make_async_remote_copy(
            src_ref=src_ref, dst_ref=dst_ref.at[_idx(peer)], send_sem=send_sems.at[k],
            recv_sem=recv_sems.at[k], device_id=peer, device_id_type=MESH))
    return sends, arrivals


def _exchange_wait(sends, arrivals):
    sends[0].wait()
    for cp in arrivals:
        cp.wait_recv()
    for cp in sends[1:]:
        cp.wait_send()


def _chip_order(k, c):
    return (6, 4 - 2 * c, 2 + 2 * c, 0)[k]


def _scatter_order(s, c):
    k = s >> 1
    mine = jnp.where(k == 0, 6, jnp.where(k == 1, 4 - 2 * c, jnp.where(k == 2, 2 + 2 * c, 0)))
    theirs = jnp.where(k == 0, 6, jnp.where(k == 1, 2 + 2 * c, jnp.where(k == 2, 4 - 2 * c, 0))) ^ 1
    return jnp.where((s & 1) == 0, theirs, mine)


def _peer_at(dist):
    x, y, c = _my_pos()
    return (x ^ ((dist >> 2) & 1), y ^ ((dist >> 1) & 1), c ^ (dist & 1))


def _normalize(src, mv, la, row0, tm, name, prev=None):
    rows, d = src.shape
    blk0 = row0 // tm

    def body(*refs):
        x_ref, mv_ref, h_ref = refs[0], refs[1], refs[-1]
        xf = x_ref[...]
        r = lax.rsqrt(jnp.mean(xf * xf, axis=-1, keepdims=True) + EPS)
        h = xf * r * (mv_ref[0:1, :] * (1.0 + mv_ref[1:2, :])) + mv_ref[2:3, :]
        h_ref[...] = h.astype(BF16)

    in_specs = [pl.BlockSpec((tm, d), lambda i: (i, 0)), pl.BlockSpec((8, d), lambda i: (0, 0))]
    args = [src, mv]
    aliases = {}
    if prev is not None:
        in_specs += [ANY]
        args += [prev]
        aliases = {2: 0}
    return _call(
        body, name=name,
        grid=(rows // tm,),
        out_shape=jax.ShapeDtypeStruct((la, d), BF16),
        in_specs=in_specs,
        out_specs=pl.BlockSpec((tm, d), lambda i: (blk0 + i, 0)),
        input_output_aliases=aliases,
        compiler_params=_params(("arbitrary",)),
    )(*args)


def _gather_order(step):
    return (step & 1) | (((step >> 2) & 1) << 1) | (((step >> 1) & 1) << 2)


def _in_projection(h, w_shard, wo_shard, tm):
    la, d = h.shape
    bw = w_shard.shape[1]
    ni = la // tm
    where = jnp.reshape(_idx(_my_pos()), (1,)).astype(jnp.int32)

    def body(me_ref, h_ref, w_ref, wo_ref, p_ref, all_ref, wo_all, wbuf, send_sems, recv_sems, local_sems,
             wo_send, wo_recv):
        s, i = pl.program_id(0), pl.program_id(1)
        x, y, c = _my_pos()
        wo_sends, wo_arrivals = _gather_copies(wo_ref, wo_all, wo_send, wo_recv)

        @pl.when((s == NDEV // 2) & (i == 0))
        def _():
            for cp in wo_sends:
                cp.start()

        me, sibling = (x, y, c), (x, y, 1 - c)
        chips = [(1 - x, y), (x, 1 - y), (1 - x, 1 - y)]

        def copy(k, block, to, from_shard=False):
            return pltpu.make_async_remote_copy(
                src_ref=w_ref if from_shard else all_ref.at[_idx(block)], dst_ref=all_ref.at[_idx(block)],
                send_sem=send_sems.at[k], recv_sem=recv_sems.at[k], device_id=to, device_id_type=MESH)

        def load(block, slot):
            return pltpu.make_async_copy(all_ref.at[_idx(block)], wbuf.at[slot], local_sems.at[1])

        keep = pltpu.make_async_copy(w_ref, all_ref.at[_idx(me)], local_sems.at[0])
        first = [copy(0, me, sibling, True)] + [copy(1 + j, me, (*chip, c), True) for j, chip in enumerate(chips)]
        passed = [copy(4 + j, (*chip, c), sibling) for j, chip in enumerate(chips)]
        steps = [(copy(0, sibling, me), None, sibling)]
        for j, chip in enumerate(chips):
            steps.append((copy(1 + j, (*chip, c), me), passed[j], (*chip, c)))
            steps.append((copy(4 + j, (*chip, 1 - c), me), None, (*chip, 1 - c)))

        @pl.when((s == 0) & (i == 0))
        def _():
            keep.start()
            mine = pltpu.make_async_copy(w_ref, wbuf.at[0], local_sems.at[1])
            mine.start()
            for cp in first:
                cp.start()
            mine.wait()

        for n, (arrival, forward, block) in enumerate(steps, start=1):
            @pl.when((s == n - 1) & (i == ni - 1))
            def _(arrival=arrival, forward=forward, block=block, n=n):
                arrival.wait_recv()
                if forward is not None:
                    forward.start()
                load(block, n % 2).start()

        @pl.when((s > 0) & (i == 0))
        def _():
            load(me, s % 2).wait()

        p_ref[...] = _dot(h_ref[...], wbuf[s % 2]).astype(BF16)

        @pl.when((s == NDEV - 1) & (i == ni - 1))
        def _():
            for cp in first + passed:
                cp.wait_send()
            keep.wait()
            _exchange_wait(wo_sends, wo_arrivals)

    return _call(
        body, name="in_projection",
        grid_spec=pltpu.PrefetchScalarGridSpec(
            num_scalar_prefetch=1, grid=(NDEV, ni),
            in_specs=[pl.BlockSpec((tm, d), lambda s, i, me_ref: (i, 0)), ANY, ANY],
            out_specs=(pl.BlockSpec((tm, bw), lambda s, i, me_ref: (i, me_ref[0] ^ _gather_order(s))), ANY, ANY),
            scratch_shapes=[pltpu.VMEM((2, d, bw), BF16), pltpu.SemaphoreType.DMA((7,)),
                            pltpu.SemaphoreType.DMA((7,)), pltpu.SemaphoreType.DMA((2,)),
                            pltpu.SemaphoreType.DMA((NDEV,)), pltpu.SemaphoreType.DMA((NDEV,))]),
        out_shape=(jax.ShapeDtypeStruct((la, NDEV * bw), BF16), jax.ShapeDtypeStruct((NDEV, d, bw), BF16),
                   jax.ShapeDtypeStruct((NDEV,) + wo_shard.shape, wo_shard.dtype)),
        compiler_params=_params(("arbitrary", "arbitrary")),
    )(where, h, w_shard, wo_shard)


def _conv_input(p, wcb, taps_m, l, t):
    la = p.shape[0]
    w = wcb.shape[1]
    nt = l // t

    def body(v_ref, wcb_ref, tm_ref, xb_ref):
        taps = _dot(tm_ref[...].reshape(4 * t, t), v_ref[...])
        xb = wcb_ref[4:5, :] + wcb_ref[0:1, :] * taps[0:t]
        for j in range(1, 4):
            xb = xb + wcb_ref[j:j + 1, :] * taps[j * t:(j + 1) * t]
        xb_ref[...] = xb

    return _call(
        body, name="conv_input",
        grid=(nt + 1,),
        out_shape=jax.ShapeDtypeStruct((la, w), F32),
        in_specs=[pl.BlockSpec((t, w), lambda i: (i, 4)), pl.BlockSpec((8, w), lambda i: (0, 0)),
                  pl.BlockSpec((None, 4, t, t), lambda i: (i // nt, 0, 0, 0))],
        out_specs=pl.BlockSpec((t, w), lambda i: (i, 0)),
        compiler_params=_params(("arbitrary",)),
    )(p, wcb, taps_m)


def _lru_forward(xb, wg, lv, l, t):
    la, w = xb.shape
    gc = wg.shape[2]
    nt = l // t

    def body(xf_ref, xr_ref, wg_ref, lv_ref, hf_ref, hr_ref, a_s, b_s, carry):
        @pl.when(pl.program_id(0) == 0)
        def _():
            carry[...] = jnp.zeros_like(carry)

        for dr, (x_ref, h_ref) in enumerate(((xf_ref, hf_ref), (xr_ref, hr_ref))):
            x = x_ref[...]
            a, s, _, _, ti, _ = _lru_coef(x, wg_ref, dr, lv_ref[3 * dr:3 * dr + 1, :],
                                          lv_ref[3 * dr + 1:3 * dr + 2, :], lv_ref[3 * dr + 2:3 * dr + 3, :], gc)
            a_s[...] = a
            b_s[...] = (s * x) * (0.5 * ti + 0.5)
            carry[dr] = _scan_tile(a_s, b_s, h_ref, carry[dr], dr == 1)

    full = lambda shape: pl.BlockSpec(shape, lambda i: (0,) * len(shape))
    fmap = lambda i: (jnp.where(i == 0, nt, i - 1), 0)
    rmap = lambda i: (jnp.where(i == 0, nt, nt - i), 0)
    return _call(
        body, name="lru_forward",
        grid=(nt + 1,),
        out_shape=(jax.ShapeDtypeStruct((la, w), F32), jax.ShapeDtypeStruct((la, w), F32)),
        in_specs=[pl.BlockSpec((t, w), fmap), pl.BlockSpec((t, w), rmap), full(wg.shape), full(lv.shape)],
        out_specs=(pl.BlockSpec((t, w), fmap), pl.BlockSpec((t, w), rmap)),
        scratch_shapes=[pltpu.VMEM((t, w), F32), pltpu.VMEM((t, w), F32), pltpu.VMEM((2, 8, w), F32)],
        compiler_params=_params(("arbitrary",)),
    )(xb, xb, wg, lv)


def _mix_gates(p_refs, hf_ref, hr_ref, wca_ref, perm_ref, t, w):
    bl, cl, ul, gl, ql = [r[...].astype(F32) for r in p_refs]
    tt = cl * ul
    tt16 = tt.astype(BF16)
    beside = _dot(perm_ref[2:4].reshape(2 * t, t), tt16)
    before, after = beside[:t], beside[t:]
    z = wca_ref[0:1, :] * before + wca_ref[1:2, :] * tt + wca_ref[2:3, :] * after
    sig_g = _sigmoid(gl)
    sig_q = _sigmoid(ql)
    ylru = _dot(perm_ref[1], (hf_ref[...] + hr_ref[...]).astype(BF16))
    return bl, cl, ul, gl, ql, (before, tt, after), z, sig_g, sig_q, ylru


def _p_specs(t, w, nt):
    return [pl.BlockSpec((t, w), functools.partial(lambda i, s: (jnp.minimum(i, nt - 1), s), s=s))
            for s in (0, 1, 2, 3, 5)]


def _mix_forward(x, tgt, p, hf, hr, wo, ov, wca, perm, t):
    l, d = x.shape
    w = d // 2
    nt = l // t

    def body(x_ref, tg_ref, b_ref, c_ref, u_ref, g_ref, q_ref, hf_ref, hr_ref, wo_ref, ov_ref, wca_ref, perm_ref,
             dn_ref, ct_ref, do_ref, part_ref):
        i = pl.program_id(0)
        bl, _, _, gl, ql, _, z, sig_g, sig_q, ylru = _mix_gates(
            (b_ref, c_ref, u_ref, g_ref, q_ref), hf_ref, hr_ref, wca_ref, perm_ref, t, w)
        ya = bl * z * (gl * sig_g)
        yb = ylru * (ql * sig_q)
        ct_ref[:, 0:w] = ya.astype(BF16)
        ct_ref[:, w:] = yb.astype(BF16)
        out = _dot(ya.astype(BF16), wo_ref[0:w, :]) + _dot(yb.astype(BF16), wo_ref[w:, :])
        gate, fg = ov_ref[0:1, :], ov_ref[1:2, :]
        n = x_ref[...] + gate * out
        rr = lax.rsqrt(jnp.mean(n * n, axis=-1, keepdims=True) + EPS)
        nh = n * rr
        e = nh * fg - tg_ref[...]
        loss = 0.5 * jnp.sum(jnp.mean(e * e, axis=-1, keepdims=True), axis=0, keepdims=True)
        dy = e * (1.0 / d)
        dnh = dy * fg
        dn = rr * (dnh - nh * jnp.mean(dnh * nh, axis=-1, keepdims=True))
        dn_ref[...] = dn.astype(BF16)
        do_ref[...] = (dn * gate).astype(BF16)

        @pl.when(i == 0)
        def _():
            part_ref[...] = jnp.zeros_like(part_ref)

        part_ref[0:1, :] += jnp.sum(dy * nh, axis=0, keepdims=True)
        part_ref[1:2, :] += jnp.sum(dn * out, axis=0, keepdims=True)
        part_ref[2:3, :] += jnp.broadcast_to(loss, (1, d))

    tile = lambda cols: pl.BlockSpec((t, cols), lambda i: (i, 0))
    full = lambda shape: pl.BlockSpec(shape, lambda i: (0,) * len(shape))
    return _call(
        body, name="mix_forward",
        grid=(nt,),
        out_shape=(jax.ShapeDtypeStruct((l, d), BF16), jax.ShapeDtypeStruct((l, d), BF16),
                   jax.ShapeDtypeStruct((l, d), BF16), jax.ShapeDtypeStruct((8, d), F32)),
        in_specs=[tile(d), tile(d)] + _p_specs(t, w, nt) + [tile(w), tile(w),
                  pl.BlockSpec((d, d), lambda i: (0, 0), pipeline_mode=pl.Buffered(1)),
                  full(ov.shape), full(wca.shape), full(perm.shape)],
        out_specs=(tile(d), tile(d), tile(d), full((8, d))),
        compiler_params=_params(("arbitrary",)),
    )(x, tgt, p, p, p, p, p, hf, hr, wo, ov, wca, perm)


def _mix_backward(dout, p, hf, hr, wo, wca, perm, l, t):
    d = dout.shape[1]
    w = d // 2
    nt = l // t
    la = p.shape[0]

    def body(do_ref, b_ref, c_ref, u_ref, g_ref, q_ref, hf_ref, hr_ref, wo_ref, wca_ref, perm_ref,
             dp_ref, dh_ref, part_ref):
        i = pl.program_id(0)

        @pl.when(i == 0)
        def _():
            part_ref[...] = jnp.zeros_like(part_ref)

        @pl.when(i == nt)
        def _():
            dp_ref[...] = jnp.zeros_like(dp_ref)

        @pl.when(i < nt)
        def _():
            bl, cl, ul, gl, ql, taps, z, sig_g, sig_q, ylru = _mix_gates(
                (b_ref, c_ref, u_ref, g_ref, q_ref), hf_ref, hr_ref, wca_ref, perm_ref, t, w)
            do = do_ref[...]
            dya = _dot_nt(do, wo_ref[0:w, :])
            dyb = _dot_nt(do, wo_ref[w:, :])
            sg = gl * sig_g
            dz = dya * bl * sg
            dz16 = dz.astype(BF16)
            beside = _dot(perm_ref[2:4].reshape(2 * t, t), dz16)
            dt = wca_ref[0:1, :] * beside[t:] + wca_ref[1:2, :] * dz + wca_ref[2:3, :] * beside[:t]
            dp_ref[:, 0:w] = (dya * z * sg).astype(BF16)
            dp_ref[:, w:2 * w] = (dt * ul).astype(BF16)
            dp_ref[:, 2 * w:3 * w] = (dt * cl).astype(BF16)
            dp_ref[:, 3 * w:4 * w] = (dya * bl * z * (sig_g * (1.0 + gl * (1.0 - sig_g)))).astype(BF16)
            dp_ref[:, 4 * w:5 * w] = jnp.zeros((t, w), BF16)
            dp_ref[:, 5 * w:6 * w] = (dyb * ylru * (sig_q * (1.0 + ql * (1.0 - sig_q)))).astype(BF16)
            dh_ref[...] = _dot(perm_ref[0], (dyb * (ql * sig_q)).astype(BF16)).astype(BF16)
            for j in range(3):
                part_ref[j:j + 1, :] += jnp.sum(dz * taps[j], axis=0, keepdims=True)

    clamp = lambda cols: pl.BlockSpec((t, cols), lambda i: (jnp.minimum(i, nt - 1), 0))
    full = lambda shape: pl.BlockSpec(shape, lambda i: (0,) * len(shape))
    return _call(
        body, name="mix_backward",
        grid=(nt + 1,),
        out_shape=(jax.ShapeDtypeStruct((la, 6 * w), BF16), jax.ShapeDtypeStruct((l, w), BF16),
                   jax.ShapeDtypeStruct((8, w), F32)),
        in_specs=[clamp(d)] + _p_specs(t, w, nt) + [clamp(w), clamp(w),
                  pl.BlockSpec((d, d), lambda i: (0, 0), pipeline_mode=pl.Buffered(1)), full(wca.shape),
                  full(perm.shape)],
        out_specs=(pl.BlockSpec((t, 6 * w), lambda i: (i, 0)), clamp(w), full((8, w))),
        compiler_params=_params(("arbitrary",)),
    )(dout, p, p, p, p, p, hf, hr, wo, wca, perm)


def _lru_backward(direction, xb, dhs, hs, wg, lv, l, t, conv=None):
    la, w = hs.shape
    gc = wg.shape[2]
    ng = w // gc
    nt = l // t
    nblk8 = la // 8
    last = conv is not None
    assert last == (direction == 1)

    if direction == 0:
        tile = lambda i: jnp.where(i == nt, nt, nt - 1 - i)
        halo = lambda i: jnp.where(tile(i) == 0, nblk8 - 1, tile(i) * (t // 8) - 1)
    else:
        tile = lambda i: i
        halo = lambda i: jnp.minimum((i + 1) * (t // 8), nblk8 - 1)

    def body(*refs):
        x_ref, dh_ref, hs_ref, halo_ref, wg_ref, lv_ref = refs[:6]
        i = pl.program_id(0)
        is_ctx = i == nt
        if last:
            v_ref, wcb_ref, bm_ref, dxo_ref, _, gw_ref = refs[6:12]
            out_ref, dwg_ref, part_ref, sc_ref, a_s, dh_s, g_s, carry, send_sems, recv_sems = refs[12:]
            copies = _scatter_copies(gw_ref, sc_ref, send_sems, recv_sems)
        else:
            out_ref, dwg_ref, part_ref, a_s, dh_s, g_s, carry = refs[-7:]
            copies = []

        @pl.when(i == 0)
        def _():
            carry[...] = jnp.zeros_like(carry)
            dwg_ref[...] = jnp.zeros_like(dwg_ref)
            part_ref[...] = jnp.zeros_like(part_ref)
            for cp in copies:
                cp.start()

        if last:
            @pl.when(is_ctx)
            def _():
                _exchange_wait(copies, copies[1:])

        xb = x_ref[...]
        lam = lv_ref[3 * direction + 2:3 * direction + 3, :]
        a, s, rs, tr, ti, sp = _lru_coef(xb, wg_ref, direction, lv_ref[3 * direction:3 * direction + 1, :],
                                         lv_ref[3 * direction + 1:3 * direction + 2, :], lam, gc)
        hs_t = hs_ref[...]
        r8 = _rows((8, w))
        if direction == 0:
            edge = jnp.where(is_ctx, 0.0, halo_ref[7:8, :])
            first = jnp.where(r8 == 0, edge, pltpu.roll(hs_t[t - 8:, :], 1, 0))
            hprev = jnp.concatenate([first, hs_t[:t - 8, :]], axis=0)
        else:
            edge = jnp.where(is_ctx, 0.0, halo_ref[0:1, :])
            final = jnp.where(r8 == 7, edge, pltpu.roll(hs_t[:8, :], 7, 0))
            hprev = jnp.concatenate([hs_t[8:, :], final], axis=0)
        a_s[...] = a
        dh_s[...] = jnp.where(is_ctx, 0.0, dh_ref[...].astype(F32))
        carry[...] = _scan_tile_backward(a_s, dh_s, g_s, carry[...], direction == 0)

        g = g_s[...]
        r = 0.5 * tr + 0.5
        ig = 0.5 * ti + 0.5
        ix = ig * xb
        gs = g * s
        dla = (g * a) * (hprev - ix * (a * rs))
        dxb = gs * ig
        dzr = dla * (r * (1.0 - tr)) * (-LRU_C * sp)
        dzi = gs * ix * (1.0 - ti)
        part_ref[0:1, :] += jnp.sum(dzr, axis=0, keepdims=True)
        part_ref[1:2, :] += jnp.sum(dzi, axis=0, keepdims=True)
        part_ref[2:3, :] += jnp.sum(dla * r, axis=0, keepdims=True) * (LRU_C * _sigmoid(-lam))
        pieces = []
        for gi in range(ng):
            sl = slice(gi * gc, (gi + 1) * gc)
            dz = jnp.concatenate([dzr[:, sl], dzi[:, sl]], axis=-1).astype(BF16)
            pieces.append(_dot_nt(dz, wg_ref[direction, gi]))
            dwg_ref[gi] += _dot(xb[:, sl].T.astype(BF16), dz)
        dxb = dxb + (pieces[0] if ng == 1 else jnp.concatenate(pieces, axis=-1))
        if not last:
            out_ref[...] = dxb
        else:
            dxb = dxb + dxo_ref[...]
            v = v_ref[...].astype(F32)
            backs = _dot(bm_ref[...].reshape(4 * t, t), dxb.astype(BF16))
            dv = jnp.zeros((t, w), F32)
            for j in range(4):
                back = backs[j * t:(j + 1) * t]
                dv = dv + wcb_ref[j:j + 1, :] * back
                part_ref[4 + j:5 + j, :] += jnp.sum(back * v, axis=0, keepdims=True)
            out_ref[...] = dv.astype(BF16)
            part_ref[3:4, :] += jnp.sum(dxb, axis=0, keepdims=True)

    full = lambda shape: pl.BlockSpec(shape, lambda i: (0,) * len(shape))
    kind = lambda i: (jnp.where(i == nt, 1, 0), 0, 0, 0)
    in_specs = [pl.BlockSpec((t, w), lambda i: (tile(i), 0)),
                pl.BlockSpec((t, w), lambda i: (jnp.minimum(tile(i), nt - 1), 0)),
                pl.BlockSpec((t, w), lambda i: (tile(i), 0)),
                pl.BlockSpec((8, w), lambda i: (halo(i), 0)),
                full(wg.shape), full(lv.shape)]
    args = [xb, dhs, hs, hs, wg, lv]
    more_out, more_spec, more_scratch = (), (), []
    if last:
        p, wcb, back_m, dxb_other, dp, g_wout = conv
        in_specs += [pl.BlockSpec((t, w), lambda i: (tile(i), 4)), full(wcb.shape),
                     pl.BlockSpec((None, 4, t, t), kind), pl.BlockSpec((t, w), lambda i: (tile(i), 0)), ANY, ANY]
        args += [p, wcb, back_m, dxb_other, dp, g_wout]
        out0 = jax.ShapeDtypeStruct(dp.shape, dp.dtype)
        spec0 = pl.BlockSpec((t, w), lambda i: (tile(i), 4))
        more_out, more_spec = (jax.ShapeDtypeStruct(g_wout.shape, g_wout.dtype),), (ANY,)
        more_scratch = [pltpu.SemaphoreType.DMA((NDEV,)), pltpu.SemaphoreType.DMA((NDEV,))]
        aliases = {10: 0}
    else:
        out0 = jax.ShapeDtypeStruct((la, w), F32)
        spec0 = pl.BlockSpec((t, w), lambda i: (tile(i), 0))
        aliases = {}
    return _call(
        body, name="lru_backward_%d" % direction,
        grid=(nt + 1,),
        out_shape=(out0, jax.ShapeDtypeStruct((ng, gc, 2 * gc), F32), jax.ShapeDtypeStruct((8, w), F32)) + more_out,
        in_specs=in_specs,
        out_specs=(spec0, full((ng, gc, 2 * gc)), full((8, w))) + more_spec,
        scratch_shapes=[pltpu.VMEM((t, w), F32), pltpu.VMEM((t, w), F32), pltpu.VMEM((t, w), F32),
                        pltpu.VMEM((8, w), F32)] + more_scratch,
        input_output_aliases=aliases,
        compiler_params=_params(("arbitrary",)),
    )(*args)


def _weight_grad_t(a, b, nblk_m, nblk_n, tk, name):
    k, m = a.shape
    n = b.shape[1]
    bm, bn = m // nblk_m, n // nblk_n
    nk = k // tk

    def body(a_ref, b_ref, o_ref, acc):
        kk = pl.program_id(2)

        @pl.when(kk == 0)
        def _():
            acc[...] = jnp.zeros_like(acc)

        acc[...] += lax.dot_general(a_ref[...], b_ref[...], (((0,), (0,)), ((), ())), preferred_element_type=F32)

        @pl.when(kk == nk - 1)
        def _():
            o_ref[...] = acc[...].astype(BF16)

    return _call(
        body, name=name,
        grid=(nblk_m, nblk_n, nk),
        out_shape=jax.ShapeDtypeStruct((nblk_m * nblk_n, bm, bn), BF16),
        in_specs=[pl.BlockSpec((tk, bm), lambda i, j, kk: (kk, i)),
                  pl.BlockSpec((tk, bn), lambda i, j, kk: (kk, j))],
        out_specs=pl.BlockSpec((None, bm, bn), lambda i, j, kk: (i * nblk_n + j, 0, 0)),
        scratch_shapes=[pltpu.VMEM((bm, bn), F32)],
        compiler_params=_params(("arbitrary", "arbitrary", "arbitrary")),
    )(a, b)


def _weight_grad_scatter(at, b, lru_parts, packed, w_ada, dsilu_cctx, tk, name):
    k, m = at.shape
    n = b.shape[1]
    bn = n // NDEV
    nk = k // tk
    rl = lru_parts.shape[1]
    rp = packed.shape[0]
    d, cols = w_ada.shape
    assert cols % 128 == 0
    cb = cols // 128
    where = jnp.stack([_idx(_my_pos()), lax.axis_index("c")]).astype(jnp.int32)
    tn = (((0,), (0,)), ((), ()))

    def body(w_ref, a_ref, b_ref, l_ref, p_ref, wa_ref, ds_ref, recv_ref, lru_ref, psum_ref, pall_ref, cctx_ref,
             acc, sbuf, sib, lbuf, lsum, lall, pall, psum, wada, cpart, call, sib_send, sib_recv, chip_send,
             chip_recv, keep_sem, l_send, l_recv, g_send, g_recv, p_send, p_recv, wa_sem, c_send, c_recv):
        s, kk = pl.program_id(0), pl.program_id(1)
        x, y, c = _my_pos()
        scattered = _scatter_copies(l_ref, lbuf, l_send, l_recv)
        gathered, arrivals = _gather_copies(lsum, lall, g_send, g_recv)
        packed_sends, packed_arrivals = _gather_copies(p_ref, pall, p_send, p_recv)
        fetch_w_ada = pltpu.make_async_copy(wa_ref, wada, wa_sem)

        @pl.when((s == 0) & (kk == 0))
        def _():
            for cp in scattered + packed_sends + [fetch_w_ada]:
                cp.start()

        @pl.when((s == NDEV // 2) & (kk == 0))
        def _():
            _exchange_wait(scattered, scattered[1:])
            red = lbuf[0]
            for j in range(1, NDEV):
                red = red + lbuf[j]
            lsum[...] = red
            for cp in gathered:
                cp.start()
            _exchange_wait(packed_sends, packed_arrivals)
            total = pall[0]
            for j in range(1, NDEV):
                total = total + pall[j]
            psum[...] = total
            fetch_w_ada.wait()
            me = _idx(_my_pos())
            part = jnp.zeros((8, d), F32)
            for q in range(cb):
                dm = jnp.broadcast_to(psum[pl.ds((NDEV + me) * cb + q, 1), :], (8, 128))
                part = part + lax.dot_general(dm, wada[:, q * 128:(q + 1) * 128],
                                              (((1,), (1,)), ((), ())), precision=HIGHEST,
                                              preferred_element_type=F32)
            cpart[...] = part
            _exchange_start(cpart, call, c_send, c_recv, 0)

        @pl.when(kk == 0)
        def _():
            acc[...] = lax.dot_general(a_ref[...], b_ref[...], tn, preferred_element_type=F32)

        @pl.when(kk > 0)
        def _():
            acc[...] += lax.dot_general(a_ref[...], b_ref[...], tn, preferred_element_type=F32)

        def to_sibling(j):
            return pltpu.make_async_remote_copy(
                src_ref=sbuf.at[0], dst_ref=sib.at[j], send_sem=sib_send.at[j], recv_sem=sib_recv.at[j],
                device_id=(x, y, 1 - c), device_id_type=MESH)

        def to_chip(j):
            dist = _chip_order(j, c)
            return pltpu.make_async_remote_copy(
                src_ref=sbuf.at[1], dst_ref=recv_ref.at[dist // 2], send_sem=chip_send.at[j],
                recv_sem=chip_recv.at[dist // 2], device_id=_peer_at(dist), device_id_type=MESH)

        keep = pltpu.make_async_copy(sbuf.at[1], recv_ref.at[0], keep_sem)
        sends = []
        for j in range(4):
            sends += [to_sibling(j), to_chip(j) if j < 3 else keep]

        for st in range(NDEV):
            @pl.when((kk == nk - 1) & (s == st))
            def _(st=st):
                if st >= 2:
                    sends[st - 2].wait_send()
                part = acc[...]
                if st % 2 == 1:
                    to_sibling(st // 2).wait_recv()
                    part = part + sib[st // 2].astype(F32)
                sbuf[st % 2] = part.astype(BF16)
                sends[st].start()
                if st == NDEV - 1:
                    sends[st - 1].wait_send()
                    sends[st].wait()
                    for j in range(1, 4):
                        pltpu.make_async_remote_copy(
                            src_ref=sbuf.at[0], dst_ref=recv_ref.at[j], send_sem=chip_send.at[0],
                            recv_sem=chip_recv.at[j], device_id=_peer_at(2 * j), device_id_type=MESH).wait_recv()
                    _exchange_wait(gathered, arrivals)
                    lru_ref[...] = lall[...]
                    psum_ref[...] = psum[...]
                    pall_ref[...] = pall[...]
                    _exchange_finish(_exchange_pushes(cpart, call, c_send, c_recv, 0))
                    tot = call[0]
                    for j in range(1, NDEV):
                        tot = tot + call[j]
                    cctx_ref[...] = tot * ds_ref[...]

    blk = lambda s, w_ref: w_ref[0] ^ _scatter_order(s, w_ref[1])
    return _call(
        body, name=name,
        grid_spec=pltpu.PrefetchScalarGridSpec(
            num_scalar_prefetch=1, grid=(NDEV, nk),
            in_specs=[pl.BlockSpec((tk, m), lambda s, kk, w_ref: (kk, 0)),
                      pl.BlockSpec((tk, bn), lambda s, kk, w_ref: (kk, blk(s, w_ref))), ANY, ANY, ANY,
                      pl.BlockSpec((8, d), lambda s, kk, w_ref: (0, 0))],
            out_specs=(ANY, pl.BlockSpec((NDEV, rl, 128), lambda s, kk, w_ref: (0, 0, 0)),
                       pl.BlockSpec((rp, 128), lambda s, kk, w_ref: (0, 0)),
                       pl.BlockSpec((NDEV, rp, 128), lambda s, kk, w_ref: (0, 0, 0)),
                       pl.BlockSpec((8, d), lambda s, kk, w_ref: (0, 0))),
            scratch_shapes=[pltpu.VMEM((m, bn), F32), pltpu.VMEM((2, m, bn), BF16), pltpu.VMEM((4, m, bn), BF16),
                            pltpu.VMEM((NDEV, rl, 128), F32), pltpu.VMEM((rl, 128), F32),
                            pltpu.VMEM((NDEV, rl, 128), F32), pltpu.VMEM((NDEV, rp, 128), F32),
                            pltpu.VMEM((rp, 128), F32), pltpu.VMEM((d, cols), F32),
                            pltpu.VMEM((8, d), F32), pltpu.VMEM((NDEV, 8, d), F32),
                            pltpu.SemaphoreType.DMA((4,)), pltpu.SemaphoreType.DMA((4,)),
                            pltpu.SemaphoreType.DMA((4,)), pltpu.SemaphoreType.DMA((4,)),
                            pltpu.SemaphoreType.DMA,
                            pltpu.SemaphoreType.DMA((NDEV,)), pltpu.SemaphoreType.DMA((NDEV,)),
                            pltpu.SemaphoreType.DMA((NDEV,)), pltpu.SemaphoreType.DMA((NDEV,)),
                            pltpu.SemaphoreType.DMA((NDEV,)), pltpu.SemaphoreType.DMA((NDEV,)),
                            pltpu.SemaphoreType.DMA,
                            pltpu.SemaphoreType.DMA((NDEV - 1,)), pltpu.SemaphoreType.DMA((NDEV - 1,))]),
        out_shape=(jax.ShapeDtypeStruct((4, m, bn), BF16), jax.ShapeDtypeStruct((NDEV, rl, 128), F32),
                   jax.ShapeDtypeStruct((rp, 128), F32), jax.ShapeDtypeStruct((NDEV, rp, 128), F32),
                   jax.ShapeDtypeStruct((8, d), F32)),
        compiler_params=_params(("arbitrary", "arbitrary")),
    )(where, at, b, lru_parts, packed, w_ada, dsilu_cctx)


def _input_backward(dp, w_all, src, mv, row0, tm, nbk, name, dn=None, cols=None):
    rows, d = src.shape
    nb, _, bw = w_all.shape
    first, last = (0, nb * bw - 1) if cols is None else cols
    k0 = first // (nbk * bw)
    nk = last // (nbk * bw) - k0 + 1
    ni = rows // tm
    blk0 = row0 // tm
    latent = dn is not None

    def body(*refs):
        dp_ref, w_ref, x_ref, mv_ref = refs[:4]
        outs = refs[4 + latent:]
        part_ref, acc = outs[latent], outs[latent + 1]
        i, k = pl.program_id(0), pl.program_id(1)

        def product():
            step = _dot_nt(dp_ref[:, 0:bw], w_ref[0])
            for q in range(1, nbk):
                step = step + _dot_nt(dp_ref[:, q * bw:(q + 1) * bw], w_ref[q])
            return step

        def finish(slot):
            xf = x_ref[...]
            r = lax.rsqrt(jnp.mean(xf * xf, axis=-1, keepdims=True) + EPS)
            xn = xf * r
            dhl = acc[slot]
            gain, sc = mv_ref[0:1, :], mv_ref[1:2, :]
            dhx = jnp.sum(dhl * xn, axis=0, keepdims=True)
            part_ref[0:1, :] += jnp.sum(dhl, axis=0, keepdims=True)
            part_ref[1:2, :] += dhx * gain
            part_ref[2:3, :] += dhx * (1.0 + sc)
            if latent:
                dxn = dhl * (gain * (1.0 + sc))
                outs[0][...] = (refs[4][...].astype(F32)
                                + r * (dxn - xn * jnp.mean(dxn * xn, axis=-1, keepdims=True)))

        @pl.when((i == 0) & (k == 0))
        def _():
            part_ref[...] = jnp.zeros_like(part_ref)
            acc[0] = product()

        @pl.when((i > 0) & (i < ni) & (k == 0))
        def _():
            acc[i % 2] = product()
            finish((i - 1) % 2)

        @pl.when((i == ni) & (k == 0))
        def _():
            finish((ni - 1) % 2)

        @pl.when((i < ni) & (k > 0))
        def _():
            acc[i % 2] += product()

    tile = pl.BlockSpec((tm, d), lambda i, k: (jnp.maximum(i - 1, 0), 0))
    vec = pl.BlockSpec((8, d), lambda i, k: (0, 0))
    kblock = lambda i, k: k0 + jnp.where(i == ni, nk - 1, k)
    return _call(
        body, name=name,
        grid=(ni + 1, nk),
        out_shape=((jax.ShapeDtypeStruct((rows, d), F32),) if latent else ()) + (jax.ShapeDtypeStruct((8, d), F32),),
        in_specs=[pl.BlockSpec((tm, nbk * bw), lambda i, k: (blk0 + jnp.minimum(i, ni - 1), kblock(i, k))),
                  pl.BlockSpec((nbk, d, bw), lambda i, k: (kblock(i, k), 0, 0)), tile, vec]
                 + ([tile] if latent else []),
        out_specs=((tile,) if latent else ()) + (vec,),
        scratch_shapes=[pltpu.VMEM((2, tm, d), F32)],
        compiler_params=_params(("arbitrary", "arbitrary")),
    )(*([dp, w_all, src, mv] + ([dn] if latent else [])))


def _adamw_scattered(parts, w, m, v, tr):
    r, c = w.shape
    nslot = parts.shape[0]

    def body(p_ref, w_ref, m_ref, v_ref, g_ref, d_ref, m2_ref, v2_ref):
        g = p_ref[0].astype(F32)
        for k in range(1, nslot):
            g = g + p_ref[k].astype(F32)
        g_ref[...] = g
        d_ref[...], m2_ref[...], v2_ref[...] = _adamw(w_ref[...], g, m_ref[...], v_ref[...])

    tile = pl.BlockSpec((tr, c), lambda i: (i, 0))
    return _call(
        body, name="adamw_scattered_%dx%d" % (r, c),
        grid=(r // tr,),
        out_shape=tuple(jax.ShapeDtypeStruct((r, c), F32) for _ in range(4)),
        in_specs=[pl.BlockSpec((nslot, tr, c), lambda i: (0, i, 0)), tile, tile, tile],
        out_specs=(tile,) * 4,
        compiler_params=_params(("arbitrary",)),
    )(parts, w, m, v)


def _adamw_ada(st, dmod, w, m, v, tr):
    r, c = w.shape

    def body(s_ref, dm_ref, w_ref, m_ref, v_ref, g_ref, d_ref, m2_ref, v2_ref):
        g = jnp.dot(s_ref[...], dm_ref[...], precision=HIGHEST, preferred_element_type=F32)
        g_ref[...] = g
        d_ref[...], m2_ref[...], v2_ref[...] = _adamw(w_ref[...], g, m_ref[...], v_ref[...])

    tile = pl.BlockSpec((tr, c), lambda i: (i, 0))
    return _call(
        body, name="adamw_ada",
        grid=(r // tr,),
        out_shape=tuple(jax.ShapeDtypeStruct((r, c), F32) for _ in range(4)),
        in_specs=[pl.BlockSpec((tr, 16), lambda i: (i, 0)), pl.BlockSpec((16, c), lambda i: (0, 0)),
                  tile, tile, tile],
        out_specs=(tile,) * 4,
        compiler_params=_params(("arbitrary",)),
    )(st, dmod, w, m, v)


def _adamw_small(gs, ws, ms, vs):
    n = len(ws)

    def body(*refs):
        for j in range(n):
            g_ref, w_ref, m_ref, v_ref = refs[j], refs[n + j], refs[2 * n + j], refs[3 * n + j]
            d_ref, m2_ref, v2_ref = refs[4 * n + j], refs[5 * n + j], refs[6 * n + j]
            d_ref[...], m2_ref[...], v2_ref[...] = _adamw(w_ref[...], g_ref[...], m_ref[...], v_ref[...])

    shapes = tuple(jax.ShapeDtypeStruct(a.shape, F32) for a in ws)
    out = _call(
        body, name="adamw_small",
        out_shape=shapes * 3,
        in_specs=[VMEM] * (4 * n), out_specs=(VMEM,) * (3 * n),
        compiler_params=_params(),
    )(*gs, *ws, *ms, *vs)
    return list(out[:n]), list(out[n:2 * n]), list(out[2 * n:])


def _blockdiag_groups(wh, gc):
    h, dh, _ = wh.shape
    g = gc // dh
    w4 = wh.reshape(h // g, g, dh, dh)
    bd = jnp.einsum("ngij,gh->ngihj", w4, jnp.eye(g, dtype=wh.dtype))
    return bd.reshape(h // g, gc, gc)


def _blockdiag_extract(bd, dh):
    ng, gc, _ = bd.shape
    g = gc // dh
    x = bd.reshape(ng, g, dh, g, dh)
    return jnp.einsum("ngihj,gh->ngij", x, jnp.eye(g, dtype=bd.dtype)).reshape(ng * g, dh, dh)


def _largest_tile(n, cap):
    return max(q for q in range(128, min(n, cap) + 1, 128) if n % q == 0)


def _rows8(*vecs):
    rows = [jnp.reshape(v, (1, -1)).astype(F32) for v in vecs]
    n = rows[0].shape[1]
    return jnp.concatenate(rows + [jnp.zeros((8 - len(rows), n), F32)], axis=0)


def _pack(pieces):
    flat = jnp.concatenate([jnp.reshape(a, (-1,)).astype(F32) for a in pieces])
    total = -(-flat.shape[0] // 1024) * 1024
    return jnp.pad(flat, (0, total - flat.shape[0])).reshape(total // 128, 128)


def _unpack(packed, shapes):
    flat = packed.reshape(-1)
    out, off = [], 0
    for s in shapes:
        n = 1
        for q in s:
            n *= q
        out.append(flat[off:off + n].reshape(s))
        off += n
    return out


def kernel(x, c, ctx, c_ctx, norm_g, w_ada, b_ada, w_in, w_conv_a, w_conv_b, b_conv_b, lru_wa, lru_ba, lru_wx, lru_bx, lru_lambda, w_out, final_g, loss_target, m_c_ctx, m_norm_g, m_w_ada, m_b_ada, m_w_in, m_w_conv_a, m_w_conv_b, m_b_conv_b, m_lru_wa, m_lru_ba, m_lru_wx, m_lru_bx, m_lru_lambda, m_w_out, m_final_g, v_c_ctx, v_norm_g, v_w_ada, v_b_ada, v_w_in, v_w_conv_a, v_w_conv_b, v_b_conv_b, v_lru_wa, v_lru_ba, v_lru_wx, v_lru_bx, v_lru_lambda, v_w_out, v_final_g):
    _, l, d = x.shape
    lc = ctx.shape[1]
    w = d // 2
    t = lc
    assert l % t == 0 and t % GRID_W == 0 and t % 128 == 0
    dh = w // N_HEADS
    gc = min(w, MXU_WIDTH)
    cols = w_ada.shape[2]
    wo_rows = w_out.shape[1]
    me = _idx(_my_pos())
    x2, ctx2, tgt2 = x[0], ctx[0], loss_target[0]
    w_ada2, w_in2, w_out2 = w_ada[0], w_in[0], w_out[0]

    small_mine = jnp.concatenate([a.reshape(-1) for a in (w_conv_a, w_conv_b, lru_ba, lru_bx, lru_lambda)]
                                 + [jnp.zeros((3 * (w // NDEV),), F32)]).reshape(16, w // NDEV)
    mod_all, s_mat, small_all = _mod_forward(
        jnp.broadcast_to(c, (8, d)), jnp.broadcast_to(c_ctx[None], (8, d)), w_ada2, small_mine)
    mod = jnp.transpose(mod_all, (1, 0, 2)).reshape(16, NDEV * cols) + b_ada
    mod_lat = lax.dynamic_slice_in_dim(mod, me, 1, axis=0)
    sh_l, sc_l, gt_l = jnp.split(mod_lat, 3, axis=-1)
    sh_c, sc_c, _ = jnp.split(mod[8:9], 3, axis=-1)
    small = jnp.transpose(small_all, (1, 0, 2)).reshape(16, w)
    wca = _rows8(*[small[j] for j in range(0, 3)])
    wcb = _rows8(*[small[j] for j in range(3, 7)], b_conv_b)
    lv = _rows8(0.5 * small[7], 0.5 * small[9], small[11], 0.5 * small[8], 0.5 * small[10], small[12])
    wg = jnp.stack([
        jnp.concatenate([_blockdiag_groups(lru_wa[0, dr], gc), _blockdiag_groups(lru_wx[0, dr], gc)], axis=-1)
        for dr in range(2)])
    wg = (0.5 * wg).astype(BF16)

    la = l + lc
    tm = 2 * t if l % (2 * t) == 0 else t
    tk = 3 * t if la % (3 * t) == 0 else t
    h = _normalize(x2, _rows8(norm_g, sc_l, sh_l), la, 0, tm, "normalize")
    h = _normalize(ctx2, _rows8(norm_g, sc_c, sh_c), la, l, t, "normalize_ctx", prev=h)
    p, w_all, wo_all = _in_projection(h, w_in2.astype(BF16), w_out2.astype(BF16), la // 4 if la % 64 == 0 else tk)
    taps_m, back_m, perm = _scan_matrices(t)
    xb = _conv_input(p, wcb, taps_m, l, t)
    hf, hr = _lru_forward(xb, wg, lv, l, t)
    wo = wo_all.reshape(d, d)
    dn, cat, dout, part_mix = _mix_forward(x2, tgt2, p, hf, hr, wo, _rows8(gt_l, final_g), wca, perm, t)
    g_wout = _weight_grad_t(cat, dout, 2, 1, _largest_tile(l, 2048), "grad_w_out")
    dp, dhs, part_ca = _mix_backward(dout, p, hf, hr, wo, wca, perm, l, t)
    dxb0, dwg0, part_l0 = _lru_backward(0, xb, dhs, hf, wg, lv, l, t)
    dp, dwg1, part_l1, sc_wout = _lru_backward(
        1, xb, dhs, hr, wg, lv, l, t, conv=(p, wcb, back_m, dxb0, dp, g_wout.reshape(NDEV, wo_rows, d)))
    dwa = jnp.stack([_blockdiag_extract(dwg0[:, :, :gc], dh), _blockdiag_extract(dwg1[:, :, :gc], dh)])
    dwx = jnp.stack([_blockdiag_extract(dwg0[:, :, gc:], dh), _blockdiag_extract(dwg1[:, :, gc:], dh)])
    lru_part = (0.5 * jnp.stack([dwa, dwx])).reshape(NDEV, -1, 128)
    grad_x, part_lat = _input_backward(dp, w_all, x2, _rows8(norm_g, sc_l), 0, tm, 2, "input_backward", dn=dn)
    (part_ctx,) = _input_backward(dp, w_all, ctx2, _rows8(norm_g, sc_c), l, t, 2, "input_backward_ctx",
                                  cols=(4 * w, 5 * w - 1))
    part_in = jnp.concatenate([part_lat[0:2], part_ctx[0:2], (part_lat[2] + part_ctx[2])[None]], axis=0)

    zeros_d = jnp.zeros((d,), F32)
    pieces = [
        jnp.concatenate([part_in[0], part_in[1], part_mix[1]]),
        jnp.concatenate([part_in[2], part_in[3], zeros_d]),
        part_in[4], part_mix[0], part_ca[0:3], part_l1[4:8], part_l1[3],
        0.5 * jnp.stack([part_l0[0], part_l1[0]]), 0.5 * jnp.stack([part_l0[1], part_l1[1]]),
        jnp.stack([part_l0[2], part_l1[2]]), part_mix[2, 0:1],
    ]
    shapes = [(3 * d,), (3 * d,), (d,), (d,), (3, w), (4, w), (w,), (2, w), (2, w), (2, w), (1,)]
    sig_cc = jax.nn.sigmoid(c_ctx)
    dsilu_cc = jnp.broadcast_to((sig_cc * (1.0 + c_ctx * (1.0 - sig_cc)))[None], (8, d))
    sc_win, lru_sum, psum, pall, g_cctx8 = _weight_grad_scatter(
        h, dp, lru_part, _pack(pieces), w_ada2, dsilu_cc, tk, "grad_w_in")
    (g_modl, g_modc, g_norm, g_final, g_ca, g_cb, g_bcb, g_ba, g_bx, g_lam, loss1) = _unpack(psum, shapes)
    loss = loss1[0]
    g_cctx = g_cctx8[0]
    g_bada = (g_modl + g_modc)[None]
    g_lru = lru_sum.reshape(2, 2, N_HEADS, dh, dh)
    g_wa, g_wx = g_lru[0][None], g_lru[1][None]
    wsl = w // NDEV
    mine = lambda a: lax.dynamic_slice_in_dim(a, me * wsl, wsl, axis=-1)
    g_ca_m, g_cb_m, g_ba_m, g_bx_m, g_lam_m = (mine(g_ca)[None], mine(g_cb)[None], mine(g_ba)[None],
                                               mine(g_bx)[None], mine(g_lam)[None])
    g_norm, g_bcb = g_norm[None], g_bcb[None]

    per_dev = pall[:, :3 * d // 128].reshape(NDEV, NDEV, cols)
    dmod_lat = lax.dynamic_slice_in_dim(per_dev, me, 1, axis=1)[:, 0]
    dmod_ctx = lax.dynamic_slice_in_dim(g_modc.reshape(NDEV, cols), me, 1, axis=0)
    dmod16 = jnp.concatenate([dmod_lat, dmod_ctx, jnp.zeros((7, cols), F32)], axis=0)
    tr_ada = 256 if d % 256 == 0 else d
    g_wada, d_wada, m_wada, v_wada = _adamw_ada(s_mat.T, dmod16, w_ada2, m_w_ada[0], v_w_ada[0], tr_ada)
    g_win2, d_win, m_win, v_win = _adamw_scattered(sc_win, w_in2, m_w_in[0], v_w_in[0], tr_ada)
    tr_out = 64 if wo_rows % 64 == 0 else wo_rows
    g_wout2, d_wout, m_wout, v_wout = _adamw_scattered(sc_wout, w_out2, m_w_out[0], v_w_out[0], tr_out)

    small_w = [c_ctx, norm_g, b_ada, w_conv_a, w_conv_b, b_conv_b, lru_wa, lru_ba, lru_wx, lru_bx, lru_lambda, final_g]
    small_m = [m_c_ctx, m_norm_g, m_b_ada, m_w_conv_a, m_w_conv_b, m_b_conv_b, m_lru_wa, m_lru_ba, m_lru_wx,
               m_lru_bx, m_lru_lambda, m_final_g]
    small_v = [v_c_ctx, v_norm_g, v_b_ada, v_w_conv_a, v_w_conv_b, v_b_conv_b, v_lru_wa, v_lru_ba, v_lru_wx,
               v_lru_bx, v_lru_lambda, v_final_g]
    small_g = [g_cctx, g_norm, g_bada, g_ca_m, g_cb_m, g_bcb, g_wa, g_ba_m, g_wx, g_bx_m, g_lam_m, g_final]
    small_g = [jnp.reshape(a, b.shape) for a, b in zip(small_g, small_w)]
    d_s, m_s, v_s = _adamw_small(small_g, small_w, small_m, small_v)

    def weights(small_list, ada, win, wout):
        (cctx_, norm_, bada_, ca_, cb_, bcb_, wa_, ba_, wx_, bx_, lam_, final_) = small_list
        return [cctx_, norm_, ada[None], bada_, win[None], ca_, cb_, bcb_, wa_, ba_, wx_, bx_, lam_, wout[None], final_]

    return (loss, grad_x[None],
            *weights(small_g, g_wada, g_win2, g_wout2), *weights(d_s, d_wada, d_win, d_wout),
            *weights(m_s, m_wada, m_win, m_wout), *weights(v_s, v_wada, v_win, v_wout))
```

```python
import functools

import jax
import jax.numpy as jnp
import numpy as np
from jax import lax
from jax.experimental import pallas as pl
from jax.experimental.pallas import tpu as pltpu

F32 = jnp.float32
BF16 = jnp.bfloat16
MESH = pl.DeviceIdType.MESH
NDEV = 8
GRID_W = 64
N_HEADS = 16
LRU_C = 8.0
EPS = 1e-6
MXU_WIDTH = 256
VMEM_LIMIT = 60 * 1024 * 1024

ADAM_LR = 0.001
ADAM_B1 = 0.9
ADAM_B2 = 0.999
ADAM_EPS = 1e-08
ADAM_WD = 0.01
ADAM_STEP = 10
ADAM_C1 = 1.0 - ADAM_B1 ** ADAM_STEP
ADAM_C2 = 1.0 - ADAM_B2 ** ADAM_STEP

HIGHEST = lax.Precision.HIGHEST
ANY = pl.BlockSpec(memory_space=pl.ANY)
VMEM = pl.BlockSpec(memory_space=pltpu.VMEM)


def _call(body, **kw):
    return pl.pallas_call(body, **kw)


def _params(sem=None, vmem=VMEM_LIMIT):
    return pltpu.CompilerParams(dimension_semantics=sem, vmem_limit_bytes=vmem)


def _my_pos():
    return lax.axis_index("x"), lax.axis_index("y"), lax.axis_index("c")


def _idx(pos):
    return 4 * pos[0] + 2 * pos[1] + pos[2]


def _peer(k):
    x, y, c = _my_pos()
    return ((1 - x) if (k >> 2) & 1 else x, (1 - y) if (k >> 1) & 1 else y, (1 - c) if k & 1 else c)


def _exchange_pushes(src_ref, dst_ref, send_sems, recv_sems, base):
    me = _idx(_my_pos())
    sends = [pltpu.make_async_remote_copy(
        src_ref=src_ref, dst_ref=dst_ref.at[me], send_sem=send_sems.at[base + k - 1],
        recv_sem=recv_sems.at[base + k - 1], device_id=_peer(k), device_id_type=MESH) for k in range(1, NDEV)]
    return sends, (src_ref, dst_ref, send_sems, recv_sems, base)


def _exchange_start(src_ref, dst_ref, send_sems, recv_sems, base):
    started = _exchange_pushes(src_ref, dst_ref, send_sems, recv_sems, base)
    for cp in started[0]:
        cp.start()
    dst_ref[_idx(_my_pos())] = src_ref[...]
    return started


def _exchange_finish(started):
    sends, (src_ref, dst_ref, send_sems, recv_sems, base) = started
    for k in range(1, NDEV):
        peer = _peer(k)
        pltpu.make_async_remote_copy(
            src_ref=src_ref, dst_ref=dst_ref.at[_idx(peer)], send_sem=send_sems.at[base + k - 1],
            recv_sem=recv_sems.at[base + k - 1], device_id=peer, device_id_type=MESH).wait_recv()
    for cp in sends:
        cp.wait_send()


def _exchange_vmem(src_ref, dst_ref, send_sems, recv_sems, base):
    _exchange_finish(_exchange_start(src_ref, dst_ref, send_sems, recv_sems, base))


def _sigmoid(z):
    return 0.5 * jnp.tanh(0.5 * z) + 0.5


def _softplus(x):
    return jnp.maximum(x, 0.0) + jnp.log1p(jnp.exp(-jnp.abs(x)))


def _one_minus_sq(a, la):
    series = (-2.0 * la) * (1.0 + la)
    return jnp.where(la > -0.0015, series, 1.0 - a * a)


def _dot(a, b):
    return jnp.dot(a, b, preferred_element_type=F32)


def _dot_nt(a, b):
    return lax.dot_general(a, b, (((1,), (1,)), ((), ())), preferred_element_type=F32)


def _rows(shape):
    return lax.broadcasted_iota(jnp.int32, shape, 0)


def _scan_matrices(t):
    seg = t // 8
    r = np.arange(t)
    perm = (np.arange(t)[None, :] == ((r % 8) * seg + r // 8)[:, None]).astype(np.float32)
    rows, cols = r[:, None], r[None, :]
    taps, back = [], []
    for rowlen in (GRID_W, t):
        pos = rows % rowlen
        shift = {-2: (cols == rows - 2) & (pos >= 2), -1: (cols == rows - 1) & (pos >= 1),
                 0: cols == rows, 1: (cols == rows + 1) & (pos + 1 < rowlen),
                 2: (cols == rows + 2) & (pos + 2 < rowlen)}
        if rowlen == GRID_W:
            beside = [shift[-1].astype(np.float32), shift[1].astype(np.float32)]
        taps.append(np.stack([perm @ shift[k].astype(np.float32) for k in (-2, -1, 0, 1)]))
        back.append(np.stack([shift[k].astype(np.float32) @ perm.T for k in (2, 1, 0, -1)]))
    as_bf16 = lambda a: jnp.asarray(a, dtype=BF16)
    return as_bf16(np.stack(taps)), as_bf16(np.stack(back)), as_bf16(np.stack([perm, perm.T] + beside))


def _chunk_scan(a, b, reverse):
    row = _rows(a.shape)
    for s in (1, 2, 4):
        if reverse:
            m = row < 8 - s
            sh = 8 - s
        else:
            m = row >= s
            sh = s
        a_s = jnp.where(m, pltpu.roll(a, sh, 0), 1.0)
        b_s = jnp.where(m, pltpu.roll(b, sh, 0), 0.0)
        b = b + a * b_s
        a = a * a_s
    return a, b


def _chain_segments(ptot, hend, carry, reverse):
    ca, cb = _chunk_scan(ptot, hend, reverse)
    incl = ca * carry + cb
    r8 = _rows(incl.shape)
    if reverse:
        start = jnp.where(r8 < 7, pltpu.roll(incl, 7, 0), carry)
        last = incl[0:1, :]
    else:
        start = jnp.where(r8 >= 1, pltpu.roll(incl, 1, 0), carry)
        last = incl[7:8, :]
    return start, jnp.broadcast_to(last, incl.shape)


def _blocks(nblock, reverse):
    order = range(nblock - 1, -1, -1) if reverse else range(nblock)
    return [slice(8 * k, 8 * k + 8) for k in order]


def _scan_tile(a_ref, b_ref, out_ref, carry, reverse):
    t, w = a_ref.shape
    seg = t // 8

    hend, ptot = jnp.zeros((8, w), F32), jnp.ones((8, w), F32)
    for rows in _blocks(seg, reverse):
        a = a_ref[rows, :]
        hend, ptot = a * hend + b_ref[rows, :], a * ptot
    h, new_carry = _chain_segments(ptot, hend, carry, reverse)
    for rows in _blocks(seg, reverse):
        h = a_ref[rows, :] * h + b_ref[rows, :]
        out_ref[rows, :] = h
    return new_carry


def _scan_tile_backward(a_ref, dh_ref, g_ref, carry, reverse):
    t, w = a_ref.shape
    seg = t // 8

    uend, ptot = jnp.zeros((8, w), F32), jnp.ones((8, w), F32)
    for rows in _blocks(seg, reverse):
        a = a_ref[rows, :]
        uend, ptot = a * (dh_ref[rows, :] + uend), a * ptot
    u, new_carry = _chain_segments(ptot, uend, carry, reverse)
    for rows in _blocks(seg, reverse):
        g = dh_ref[rows, :] + u
        g_ref[rows, :] = g
        u = a_ref[rows, :] * g
    return new_carry


def _lru_coef(xb, wg_ref, d, ba, bx, lam, gc):
    w = xb.shape[1]
    xb16 = xb.astype(BF16)
    zr, zi = [], []
    for g in range(w // gc):
        z = _dot(xb16[:, g * gc:(g + 1) * gc], wg_ref[d, g])
        zr.append(z[:, :gc])
        zi.append(z[:, gc:])
    zr = zr[0] if len(zr) == 1 else jnp.concatenate(zr, axis=-1)
    zi = zi[0] if len(zi) == 1 else jnp.concatenate(zi, axis=-1)
    tr = jnp.tanh(zr + ba)
    ti = jnp.tanh(zi + bx)
    sp = _softplus(-lam)
    half = -0.5 * LRU_C * sp
    la = tr * half + half
    a = jnp.exp(la)
    q = _one_minus_sq(a, la)
    rs = lax.rsqrt(jnp.maximum(q, 1e-30))
    return a, q * rs, rs, tr, ti, sp


def _adamw(w, g, m, v):
    m2 = ADAM_B1 * m + (1.0 - ADAM_B1) * g
    v2 = ADAM_B2 * v + (1.0 - ADAM_B2) * (g * g)
    m_hat = m2 / ADAM_C1
    v_hat = v2 / ADAM_C2
    delta = -ADAM_LR * (m_hat / (jnp.sqrt(v_hat) + ADAM_EPS) + ADAM_WD * w)
    return delta, m2, v2


def _mod_forward(c8, cctx8, w_ada, small):
    d = c8.shape[1]
    cols = w_ada.shape[1]

    def body(c_ref, cctx_ref, w_ref, sm_ref, mod_ref, s_ref, sm_all, cbuf, mod_my, send_sems, recv_sems):
        _exchange_vmem(sm_ref, sm_all, send_sems, recv_sems, 2 * (NDEV - 1))
        _exchange_vmem(c_ref, cbuf, send_sems, recv_sems, 0)
        row = _rows((8, d))
        c_all = jnp.zeros((8, d), F32)
        for b in range(NDEV):
            c_all = jnp.where(row == b, cbuf[b], c_all)
        cc = cctx_ref[...]
        s_top = c_all * _sigmoid(c_all)
        s_bot = jnp.where(row == 0, cc * _sigmoid(cc), 0.0)
        s = jnp.concatenate([s_top, s_bot], axis=0)
        s_ref[...] = s
        mod_my[...] = jnp.dot(s, w_ref[...], precision=HIGHEST, preferred_element_type=F32)
        _exchange_vmem(mod_my, mod_ref, send_sems, recv_sems, NDEV - 1)

    return _call(
        body, name="mod_forward",
        out_shape=(jax.ShapeDtypeStruct((NDEV, 16, cols), F32), jax.ShapeDtypeStruct((16, d), F32),
                   jax.ShapeDtypeStruct((NDEV,) + small.shape, F32)),
        in_specs=[VMEM] * 4, out_specs=(VMEM,) * 3,
        scratch_shapes=[pltpu.VMEM((NDEV, 8, d), F32), pltpu.VMEM((16, cols), F32),
                        pltpu.SemaphoreType.DMA((3 * (NDEV - 1),)), pltpu.SemaphoreType.DMA((3 * (NDEV - 1),))],
        compiler_params=_params(),
    )(c8, cctx8, w_ada, small)


def _scatter_copies(src_ref, dst_ref, send_sems, recv_sems):
    me = _idx(_my_pos())
    copies = [pltpu.make_async_copy(src_ref.at[me], dst_ref.at[0], send_sems.at[0])]
    for k in range(1, NDEV):
        peer = _peer(k)
        copies.append(pltpu.make_async_remote_copy(
            src_ref=src_ref.at[_idx(peer)], dst_ref=dst_ref.at[k], send_sem=send_sems.at[k],
            recv_sem=recv_sems.at[k], device_id=peer, device_id_type=MESH))
    return copies


def _gather_copies(src_ref, dst_ref, send_sems, recv_sems):
    me = _idx(_my_pos())
    sends = [pltpu.make_async_copy(src_ref, dst_ref.at[me], send_sems.at[0])]
    arrivals = []
    for k in range(1, NDEV):
        peer = _peer(k)
        sends.append(pltpu.make_async_remote_copy(
            src_ref=src_ref, dst_ref=dst_ref.at[me], send_sem=send_sems.at[k],
            recv_sem=recv_sems.at[k], device_id=peer, device_id_type=MESH))
        arrivals.append(pltpu.make_async_remote_copy(
            src_ref=src_ref, dst_ref=dst_ref.at[_idx(peer)], send_sem=send_sems.at[k],
            recv_sem=recv_sems.at[k], device_id=peer, device_id_type=MESH))
    return sends, arrivals


def _exchange_wait(sends, arrivals):
    sends[0].wait()
    for cp in arrivals:
        cp.wait_recv()
    for cp in sends[1:]:
        cp.wait_send()


def _chip_order(k, c):
    return (6, 4 - 2 * c, 2 + 2 * c, 0)[k]


def _scatter_order(s, c):
    k = s >> 1
    mine = jnp.where(k == 0, 6, jnp.where(k == 1, 4 - 2 * c, jnp.where(k == 2, 2 + 2 * c, 0)))
    theirs = jnp.where(k == 0, 6, jnp.where(k == 1, 2 + 2 * c, jnp.where(k == 2, 4 - 2 * c, 0))) ^ 1
    return jnp.where((s & 1) == 0, theirs, mine)


def _peer_at(dist):
    x, y, c = _my_pos()
    return (x ^ ((dist >> 2) & 1), y ^ ((dist >> 1) & 1), c ^ (dist & 1))


def _normalize(src, mv, la, row0, tm, name, prev=None):
    rows, d = src.shape
    blk0 = row0 // tm

    def body(*refs):
        x_ref, mv_ref, h_ref = refs[0], refs[1], refs[-1]
        xf = x_ref[...]
        r = lax.rsqrt(jnp.mean(xf * xf, axis=-1, keepdims=True) + EPS)
        h = xf * r * (mv_ref[0:1, :] * (1.0 + mv_ref[1:2, :])) + mv_ref[2:3, :]
        h_ref[...] = h.astype(BF16)

    in_specs = [pl.BlockSpec((tm, d), lambda i: (i, 0)), pl.BlockSpec((8, d), lambda i: (0, 0))]
    args = [src, mv]
    aliases = {}
    if prev is not None:
        in_specs += [ANY]
        args += [prev]
        aliases = {2: 0}
    return _call(
        body, name=name,
        grid=(rows // tm,),
        out_shape=jax.ShapeDtypeStruct((la, d), BF16),
        in_specs=in_specs,
        out_specs=pl.BlockSpec((tm, d), lambda i: (blk0 + i, 0)),
        input_output_aliases=aliases,
        compiler_params=_params(("arbitrary",)),
    )(*args)


def _gather_order(step):
    return (step & 1) | (((step >> 2) & 1) << 1) | (((step >> 1) & 1) << 2)


def _in_projection(h, w_shard, wo_shard, tm):
    la, d = h.shape
    bw = w_shard.shape[1]
    ni = la // tm
    where = jnp.reshape(_idx(_my_pos()), (1,)).astype(jnp.int32)

    def body(me_ref, h_ref, w_ref, wo_ref, p_ref, all_ref, wo_all, wbuf, send_sems, recv_sems, local_sems,
             wo_send, wo_recv):
        s, i = pl.program_id(0), pl.program_id(1)
        x, y, c = _my_pos()
        wo_sends, wo_arrivals = _gather_copies(wo_ref, wo_all, wo_send, wo_recv)

        @pl.when((s == NDEV // 2) & (i == 0))
        def _():
            for cp in wo_sends:
                cp.start()

        me, sibling = (x, y, c), (x, y, 1 - c)
        chips = [(1 - x, y), (x, 1 - y), (1 - x, 1 - y)]

        def copy(k, block, to, from_shard=False):
            return pltpu.make_async_remote_copy(
                src_ref=w_ref if from_shard else all_ref.at[_idx(block)], dst_ref=all_ref.at[_idx(block)],
                send_sem=send_sems.at[k], recv_sem=recv_sems.at[k], device_id=to, device_id_type=MESH)

        def load(block, slot):
            return pltpu.make_async_copy(all_ref.at[_idx(block)], wbuf.at[slot], local_sems.at[1])

        keep = pltpu.make_async_copy(w_ref, all_ref.at[_idx(me)], local_sems.at[0])
        first = [copy(0, me, sibling, True)] + [copy(1 + j, me, (*chip, c), True) for j, chip in enumerate(chips)]
        passed = [copy(4 + j, (*chip, c), sibling) for j, chip in enumerate(chips)]
        steps = [(copy(0, sibling, me), None, sibling)]
        for j, chip in enumerate(chips):
            steps.append((copy(1 + j, (*chip, c), me), passed[j], (*chip, c)))
            steps.append((copy(4 + j, (*chip, 1 - c), me), None, (*chip, 1 - c)))

        @pl.when((s == 0) & (i == 0))
        def _():
            keep.start()
            mine = pltpu.make_async_copy(w_ref, wbuf.at[0], local_sems.at[1])
            mine.start()
            for cp in first:
                cp.start()
            mine.wait()

        for n, (arrival, forward, block) in enumerate(steps, start=1):
            @pl.when((s == n - 1) & (i == ni - 1))
            def _(arrival=arrival, forward=forward, block=block, n=n):
                arrival.wait_recv()
                if forward is not None:
                    forward.start()
                load(block, n % 2).start()

        @pl.when((s > 0) & (i == 0))
        def _():
            load(me, s % 2).wait()

        p_ref[...] = _dot(h_ref[...], wbuf[s % 2]).astype(BF16)

        @pl.when((s == NDEV - 1) & (i == ni - 1))
        def _():
            for cp in first + passed:
                cp.wait_send()
            keep.wait()
            _exchange_wait(wo_sends, wo_arrivals)

    return _call(
        body, name="in_projection",
        grid_spec=pltpu.PrefetchScalarGridSpec(
            num_scalar_prefetch=1, grid=(NDEV, ni),
            in_specs=[pl.BlockSpec((tm, d), lambda s, i, me_ref: (i, 0)), ANY, ANY],
            out_specs=(pl.BlockSpec((tm, bw), lambda s, i, me_ref: (i, me_ref[0] ^ _gather_order(s))), ANY, ANY),
            scratch_shapes=[pltpu.VMEM((2, d, bw), BF16), pltpu.SemaphoreType.DMA((7,)),
                            pltpu.SemaphoreType.DMA((7,)), pltpu.SemaphoreType.DMA((2,)),
                            pltpu.SemaphoreType.DMA((NDEV,)), pltpu.SemaphoreType.DMA((NDEV,))]),
        out_shape=(jax.ShapeDtypeStruct((la, NDEV * bw), BF16), jax.ShapeDtypeStruct((NDEV, d, bw), BF16),
                   jax.ShapeDtypeStruct((NDEV,) + wo_shard.shape, wo_shard.dtype)),
        compiler_params=_params(("arbitrary", "arbitrary")),
    )(where, h, w_shard, wo_shard)


def _conv_input(p, wcb, taps_m, l, t):
    la = p.shape[0]
    w = wcb.shape[1]
    nt = l // t

    def body(v_ref, wcb_ref, tm_ref, xb_ref):
        taps = _dot(tm_ref[...].reshape(4 * t, t), v_ref[...])
        xb = wcb_ref[4:5, :] + wcb_ref[0:1, :] * taps[0:t]
        for j in range(1, 4):
            xb = xb + wcb_ref[j:j + 1, :] * taps[j * t:(j + 1) * t]
        xb_ref[...] = xb

    return _call(
        body, name="conv_input",
        grid=(nt + 1,),
        out_shape=jax.ShapeDtypeStruct((la, w), F32),
        in_specs=[pl.BlockSpec((t, w), lambda i: (i, 4)), pl.BlockSpec((8, w), lambda i: (0, 0)),
                  pl.BlockSpec((None, 4, t, t), lambda i: (i // nt, 0, 0, 0))],
        out_specs=pl.BlockSpec((t, w), lambda i: (i, 0)),
        compiler_params=_params(("arbitrary",)),
    )(p, wcb, taps_m)


def _lru_forward(xb, wg, lv, l, t):
    la, w = xb.shape
    gc = wg.shape[2]
    nt = l // t

    def body(xf_ref, xr_ref, wg_ref, lv_ref, hf_ref, hr_ref, a_s, b_s, carry):
        @pl.when(pl.program_id(0) == 0)
        def _():
            carry[...] = jnp.zeros_like(carry)

        for dr, (x_ref, h_ref) in enumerate(((xf_ref, hf_ref), (xr_ref, hr_ref))):
            x = x_ref[...]
            a, s, _, _, ti, _ = _lru_coef(x, wg_ref, dr, lv_ref[3 * dr:3 * dr + 1, :],
                                          lv_ref[3 * dr + 1:3 * dr + 2, :], lv_ref[3 * dr + 2:3 * dr + 3, :], gc)
            a_s[...] = a
            b_s[...] = (s * x) * (0.5 * ti + 0.5)
            carry[dr] = _scan_tile(a_s, b_s, h_ref, carry[dr], dr == 1)

    full = lambda shape: pl.BlockSpec(shape, lambda i: (0,) * len(shape))
    fmap = lambda i: (jnp.where(i == 0, nt, i - 1), 0)
    rmap = lambda i: (jnp.where(i == 0, nt, nt - i), 0)
    return _call(
        body, name="lru_forward",
        grid=(nt + 1,),
        out_shape=(jax.ShapeDtypeStruct((la, w), F32), jax.ShapeDtypeStruct((la, w), F32)),
        in_specs=[pl.BlockSpec((t, w), fmap), pl.BlockSpec((t, w), rmap), full(wg.shape), full(lv.shape)],
        out_specs=(pl.BlockSpec((t, w), fmap), pl.BlockSpec((t, w), rmap)),
        scratch_shapes=[pltpu.VMEM((t, w), F32), pltpu.VMEM((t, w), F32), pltpu.VMEM((2, 8, w), F32)],
        compiler_params=_params(("arbitrary",)),
    )(xb, xb, wg, lv)


def _mix_gates(p_refs, hf_ref, hr_ref, wca_ref, perm_ref, t, w):
    bl, cl, ul, gl, ql = [r[...].astype(F32) for r in p_refs]
    tt = cl * ul
    tt16 = tt.astype(BF16)
    beside = _dot(perm_ref[2:4].reshape(2 * t, t), tt16)
    before, after = beside[:t], beside[t:]
    z = wca_ref[0:1, :] * before + wca_ref[1:2, :] * tt + wca_ref[2:3, :] * after
    sig_g = _sigmoid(gl)
    sig_q = _sigmoid(ql)
    ylru = _dot(perm_ref[1], (hf_ref[...] + hr_ref[...]).astype(BF16))
    return bl, cl, ul, gl, ql, (before, tt, after), z, sig_g, sig_q, ylru


def _p_specs(t, w, nt):
    return [pl.BlockSpec((t, w), functools.partial(lambda i, s: (jnp.minimum(i, nt - 1), s), s=s))
            for s in (0, 1, 2, 3, 5)]


def _mix_forward(x, tgt, p, hf, hr, wo, ov, wca, perm, t):
    l, d = x.shape
    w = d // 2
    nt = l // t

    def body(x_ref, tg_ref, b_ref, c_ref, u_ref, g_ref, q_ref, hf_ref, hr_ref, wo_ref, ov_ref, wca_ref, perm_ref,
             dn_ref, ct_ref, do_ref, part_ref):
        i = pl.program_id(0)
        bl, _, _, gl, ql, _, z, sig_g, sig_q, ylru = _mix_gates(
            (b_ref, c_ref, u_ref, g_ref, q_ref), hf_ref, hr_ref, wca_ref, perm_ref, t, w)
        ya = bl * z * (gl * sig_g)
        yb = ylru * (ql * sig_q)
        ct_ref[:, 0:w] = ya.astype(BF16)
        ct_ref[:, w:] = yb.astype(BF16)
        out = _dot(ya.astype(BF16), wo_ref[0:w, :]) + _dot(yb.astype(BF16), wo_ref[w:, :])
        gate, fg = ov_ref[0:1, :], ov_ref[1:2, :]
        n = x_ref[...] + gate * out
        rr = lax.rsqrt(jnp.mean(n * n, axis=-1, keepdims=True) + EPS)
        nh = n * rr
        e = nh * fg - tg_ref[...]
        loss = 0.5 * jnp.sum(jnp.mean(e * e, axis=-1, keepdims=True), axis=0, keepdims=True)
        dy = e * (1.0 / d)
        dnh = dy * fg
        dn = rr * (dnh - nh * jnp.mean(dnh * nh, axis=-1, keepdims=True))
        dn_ref[...] = dn.astype(BF16)
        do_ref[...] = (dn * gate).astype(BF16)

        @pl.when(i == 0)
        def _():
            part_ref[...] = jnp.zeros_like(part_ref)

        part_ref[0:1, :] += jnp.sum(dy * nh, axis=0, keepdims=True)
        part_ref[1:2, :] += jnp.sum(dn * out, axis=0, keepdims=True)
        part_ref[2:3, :] += jnp.broadcast_to(loss, (1, d))

    tile = lambda cols: pl.BlockSpec((t, cols), lambda i: (i, 0))
    full = lambda shape: pl.BlockSpec(shape, lambda i: (0,) * len(shape))
    return _call(
        body, name="mix_forward",
        grid=(nt,),
        out_shape=(jax.ShapeDtypeStruct((l, d), BF16), jax.ShapeDtypeStruct((l, d), BF16),
                   jax.ShapeDtypeStruct((l, d), BF16), jax.ShapeDtypeStruct((8, d), F32)),
        in_specs=[tile(d), tile(d)] + _p_specs(t, w, nt) + [tile(w), tile(w),
                  pl.BlockSpec((d, d), lambda i: (0, 0), pipeline_mode=pl.Buffered(1)),
                  full(ov.shape), full(wca.shape), full(perm.shape)],
        out_specs=(tile(d), tile(d), tile(d), full((8, d))),
        compiler_params=_params(("arbitrary",)),
    )(x, tgt, p, p, p, p, p, hf, hr, wo, ov, wca, perm)


def _mix_backward(dout, p, hf, hr, wo, wca, perm, l, t):
    d = dout.shape[1]
    w = d // 2
    nt = l // t
    la = p.shape[0]

    def body(do_ref, b_ref, c_ref, u_ref, g_ref, q_ref, hf_ref, hr_ref, wo_ref, wca_ref, perm_ref,
             dp_ref, dh_ref, part_ref):
        i = pl.program_id(0)

        @pl.when(i == 0)
        def _():
            part_ref[...] = jnp.zeros_like(part_ref)

        @pl.when(i == nt)
        def _():
            dp_ref[...] = jnp.zeros_like(dp_ref)

        @pl.when(i < nt)
        def _():
            bl, cl, ul, gl, ql, taps, z, sig_g, sig_q, ylru = _mix_gates(
                (b_ref, c_ref, u_ref, g_ref, q_ref), hf_ref, hr_ref, wca_ref, perm_ref, t, w)
            do = do_ref[...]
            dya = _dot_nt(do, wo_ref[0:w, :])
            dyb = _dot_nt(do, wo_ref[w:, :])
            sg = gl * sig_g
            dz = dya * bl * sg
            dz16 = dz.astype(BF16)
            beside = _dot(perm_ref[2:4].reshape(2 * t, t), dz16)
            dt = wca_ref[0:1, :] * beside[t:] + wca_ref[1:2, :] * dz + wca_ref[2:3, :] * beside[:t]
            dp_ref[:, 0:w] = (dya * z * sg).astype(BF16)
            dp_ref[:, w:2 * w] = (dt * ul).astype(BF16)
            dp_ref[:, 2 * w:3 * w] = (dt * cl).astype(BF16)
            dp_ref[:, 3 * w:4 * w] = (dya * bl * z * (sig_g * (1.0 + gl * (1.0 - sig_g)))).astype(BF16)
            dp_ref[:, 4 * w:5 * w] = jnp.zeros((t, w), BF16)
            dp_ref[:, 5 * w:6 * w] = (dyb * ylru * (sig_q * (1.0 + ql * (1.0 - sig_q)))).astype(BF16)
            dh_ref[...] = _dot(perm_ref[0], (dyb * (ql * sig_q)).astype(BF16)).astype(BF16)
            for j in range(3):
                part_ref[j:j + 1, :] += jnp.sum(dz * taps[j], axis=0, keepdims=True)

    clamp = lambda cols: pl.BlockSpec((t, cols), lambda i: (jnp.minimum(i, nt - 1), 0))
    full = lambda shape: pl.BlockSpec(shape, lambda i: (0,) * len(shape))
    return _call(
        body, name="mix_backward",
        grid=(nt + 1,),
        out_shape=(jax.ShapeDtypeStruct((la, 6 * w), BF16), jax.ShapeDtypeStruct((l, w), BF16),
                   jax.ShapeDtypeStruct((8, w), F32)),
        in_specs=[clamp(d)] + _p_specs(t, w, nt) + [clamp(w), clamp(w),
                  pl.BlockSpec((d, d), lambda i: (0, 0), pipeline_mode=pl.Buffered(1)), full(wca.shape),
                  full(perm.shape)],
        out_specs=(pl.BlockSpec((t, 6 * w), lambda i: (i, 0)), clamp(w), full((8, w))),
        compiler_params=_params(("arbitrary",)),
    )(dout, p, p, p, p, p, hf, hr, wo, wca, perm)


def _lru_backward(direction, xb, dhs, hs, wg, lv, l, t, conv=None):
    la, w = hs.shape
    gc = wg.shape[2]
    ng = w // gc
    nt = l // t
    nblk8 = la // 8
    last = conv is not None
    assert last == (direction == 1)

    if direction == 0:
        tile = lambda i: jnp.where(i == nt, nt, nt - 1 - i)
        halo = lambda i: jnp.where(tile(i) == 0, nblk8 - 1, tile(i) * (t // 8) - 1)
    else:
        tile = lambda i: i
        halo = lambda i: jnp.minimum((i + 1) * (t // 8), nblk8 - 1)

    def body(*refs):
        x_ref, dh_ref, hs_ref, halo_ref, wg_ref, lv_ref = refs[:6]
        i = pl.program_id(0)
        is_ctx = i == nt
        if last:
            v_ref, wcb_ref, bm_ref, dxo_ref, _, gw_ref = refs[6:12]
            out_ref, dwg_ref, part_ref, sc_ref, a_s, dh_s, g_s, carry, send_sems, recv_sems = refs[12:]
            copies = _scatter_copies(gw_ref, sc_ref, send_sems, recv_sems)
        else:
            out_ref, dwg_ref, part_ref, a_s, dh_s, g_s, carry = refs[-7:]
            copies = []

        @pl.when(i == 0)
        def _():
            carry[...] = jnp.zeros_like(carry)
            dwg_ref[...] = jnp.zeros_like(dwg_ref)
            part_ref[...] = jnp.zeros_like(part_ref)
            for cp in copies:
                cp.start()

        if last:
            @pl.when(is_ctx)
            def _():
                _exchange_wait(copies, copies[1:])

        xb = x_ref[...]
        lam = lv_ref[3 * direction + 2:3 * direction + 3, :]
        a, s, rs, tr, ti, sp = _lru_coef(xb, wg_ref, direction, lv_ref[3 * direction:3 * direction + 1, :],
                                         lv_ref[3 * direction + 1:3 * direction + 2, :], lam, gc)
        hs_t = hs_ref[...]
        r8 = _rows((8, w))
        if direction == 0:
            edge = jnp.where(is_ctx, 0.0, halo_ref[7:8, :])
            first = jnp.where(r8 == 0, edge, pltpu.roll(hs_t[t - 8:, :], 1, 0))
            hprev = jnp.concatenate([first, hs_t[:t - 8, :]], axis=0)
        else:
            edge = jnp.where(is_ctx, 0.0, halo_ref[0:1, :])
            final = jnp.where(r8 == 7, edge, pltpu.roll(hs_t[:8, :], 7, 0))
            hprev = jnp.concatenate([hs_t[8:, :], final], axis=0)
        a_s[...] = a
        dh_s[...] = jnp.where(is_ctx, 0.0, dh_ref[...].astype(F32))
        carry[...] = _scan_tile_backward(a_s, dh_s, g_s, carry[...], direction == 0)

        g = g_s[...]
        r = 0.5 * tr + 0.5
        ig = 0.5 * ti + 0.5
        ix = ig * xb
        gs = g * s
        dla = (g * a) * (hprev - ix * (a * rs))
        dxb = gs * ig
        dzr = dla * (r * (1.0 - tr)) * (-LRU_C * sp)
        dzi = gs * ix * (1.0 - ti)
        part_ref[0:1, :] += jnp.sum(dzr, axis=0, keepdims=True)
        part_ref[1:2, :] += jnp.sum(dzi, axis=0, keepdims=True)
        part_ref[2:3, :] += jnp.sum(dla * r, axis=0, keepdims=True) * (LRU_C * _sigmoid(-lam))
        pieces = []
        for gi in range(ng):
            sl = slice(gi * gc, (gi + 1) * gc)
            dz = jnp.concatenate([dzr[:, sl], dzi[:, sl]], axis=-1).astype(BF16)
            pieces.append(_dot_nt(dz, wg_ref[direction, gi]))
            dwg_ref[gi] += _dot(xb[:, sl].T.astype(BF16), dz)
        dxb = dxb + (pieces[0] if ng == 1 else jnp.concatenate(pieces, axis=-1))
        if not last:
            out_ref[...] = dxb
        else:
            dxb = dxb + dxo_ref[...]
            v = v_ref[...].astype(F32)
            backs = _dot(bm_ref[...].reshape(4 * t, t), dxb.astype(BF16))
            dv = jnp.zeros((t, w), F32)
            for j in range(4):
                back = backs[j * t:(j + 1) * t]
                dv = dv + wcb_ref[j:j + 1, :] * back
                part_ref[4 + j:5 + j, :] += jnp.sum(back * v, axis=0, keepdims=True)
            out_ref[...] = dv.astype(BF16)
            part_ref[3:4, :] += jnp.sum(dxb, axis=0, keepdims=True)

    full = lambda shape: pl.BlockSpec(shape, lambda i: (0,) * len(shape))
    kind = lambda i: (jnp.where(i == nt, 1, 0), 0, 0, 0)
    in_specs = [pl.BlockSpec((t, w), lambda i: (tile(i), 0)),
                pl.BlockSpec((t, w), lambda i: (jnp.minimum(tile(i), nt - 1), 0)),
                pl.BlockSpec((t, w), lambda i: (tile(i), 0)),
                pl.BlockSpec((8, w), lambda i: (halo(i), 0)),
                full(wg.shape), full(lv.shape)]
    args = [xb, dhs, hs, hs, wg, lv]
    more_out, more_spec, more_scratch = (), (), []
    if last:
        p, wcb, back_m, dxb_other, dp, g_wout = conv
        in_specs += [pl.BlockSpec((t, w), lambda i: (tile(i), 4)), full(wcb.shape),
                     pl.BlockSpec((None, 4, t, t), kind), pl.BlockSpec((t, w), lambda i: (tile(i), 0)), ANY, ANY]
        args += [p, wcb, back_m, dxb_other, dp, g_wout]
        out0 = jax.ShapeDtypeStruct(dp.shape, dp.dtype)
        spec0 = pl.BlockSpec((t, w), lambda i: (tile(i), 4))
        more_out, more_spec = (jax.ShapeDtypeStruct(g_wout.shape, g_wout.dtype),), (ANY,)
        more_scratch = [pltpu.SemaphoreType.DMA((NDEV,)), pltpu.SemaphoreType.DMA((NDEV,))]
        aliases = {10: 0}
    else:
        out0 = jax.ShapeDtypeStruct((la, w), F32)
        spec0 = pl.BlockSpec((t, w), lambda i: (tile(i), 0))
        aliases = {}
    return _call(
        body, name="lru_backward_%d" % direction,
        grid=(nt + 1,),
        out_shape=(out0, jax.ShapeDtypeStruct((ng, gc, 2 * gc), F32), jax.ShapeDtypeStruct((8, w), F32)) + more_out,
        in_specs=in_specs,
        out_specs=(spec0, full((ng, gc, 2 * gc)), full((8, w))) + more_spec,
        scratch_shapes=[pltpu.VMEM((t, w), F32), pltpu.VMEM((t, w), F32), pltpu.VMEM((t, w), F32),
                        pltpu.VMEM((8, w), F32)] + more_scratch,
        input_output_aliases=aliases,
        compiler_params=_params(("arbitrary",)),
    )(*args)


def _weight_grad_t(a, b, nblk_m, nblk_n, tk, name):
    k, m = a.shape
    n = b.shape[1]
    bm, bn = m // nblk_m, n // nblk_n
    nk = k // tk

    def body(a_ref, b_ref, o_ref, acc):
        kk = pl.program_id(2)

        @pl.when(kk == 0)
        def _():
            acc[...] = jnp.zeros_like(acc)

        acc[...] += lax.dot_general(a_ref[...], b_ref[...], (((0,), (0,)), ((), ())), preferred_element_type=F32)

        @pl.when(kk == nk - 1)
        def _():
            o_ref[...] = acc[...].astype(BF16)

    return _call(
        body, name=name,
        grid=(nblk_m, nblk_n, nk),
        out_shape=jax.ShapeDtypeStruct((nblk_m * nblk_n, bm, bn), BF16),
        in_specs=[pl.BlockSpec((tk, bm), lambda i, j, kk: (kk, i)),
                  pl.BlockSpec((tk, bn), lambda i, j, kk: (kk, j))],
        out_specs=pl.BlockSpec((None, bm, bn), lambda i, j, kk: (i * nblk_n + j, 0, 0)),
        scratch_shapes=[pltpu.VMEM((bm, bn), F32)],
        compiler_params=_params(("arbitrary", "arbitrary", "arbitrary")),
    )(a, b)


def _weight_grad_scatter(at, b, lru_parts, packed, w_ada, dsilu_cctx, tk, name):
    k, m = at.shape
    n = b.shape[1]
    bn = n // NDEV
    nk = k // tk
    rl = lru_parts.shape[1]
    rp = packed.shape[0]
    d, cols = w_ada.shape
    assert cols % 128 == 0
    cb = cols // 128
    where = jnp.stack([_idx(_my_pos()), lax.axis_index("c")]).astype(jnp.int32)
    tn = (((0,), (0,)), ((), ()))

    def body(w_ref, a_ref, b_ref, l_ref, p_ref, wa_ref, ds_ref, recv_ref, lru_ref, psum_ref, pall_ref, cctx_ref,
             acc, sbuf, sib, lbuf, lsum, lall, pall, psum, wada, cpart, call, sib_send, sib_recv, chip_send,
             chip_recv, keep_sem, l_send, l_recv, g_send, g_recv, p_send, p_recv, wa_sem, c_send, c_recv):
        s, kk = pl.program_id(0), pl.program_id(1)
        x, y, c = _my_pos()
        scattered = _scatter_copies(l_ref, lbuf, l_send, l_recv)
        gathered, arrivals = _gather_copies(lsum, lall, g_send, g_recv)
        packed_sends, packed_arrivals = _gather_copies(p_ref, pall, p_send, p_recv)
        fetch_w_ada = pltpu.make_async_copy(wa_ref, wada, wa_sem)

        @pl.when((s == 0) & (kk == 0))
        def _():
            for cp in scattered + packed_sends + [fetch_w_ada]:
                cp.start()

        @pl.when((s == NDEV // 2) & (kk == 0))
        def _():
            _exchange_wait(scattered, scattered[1:])
            red = lbuf[0]
            for j in range(1, NDEV):
                red = red + lbuf[j]
            lsum[...] = red
            for cp in gathered:
                cp.start()
            _exchange_wait(packed_sends, packed_arrivals)
            total = pall[0]
            for j in range(1, NDEV):
                total = total + pall[j]
            psum[...] = total
            fetch_w_ada.wait()
            me = _idx(_my_pos())
            tr = _largest_tile(d, 256)
            for r in range(d // tr):
                prod = jnp.zeros((tr, 128), F32)
                for q in range(cb):
                    prod = prod + (wada[r * tr:(r + 1) * tr, q * 128:(q + 1) * 128]
                                   * psum[pl.ds((NDEV + me) * cb + q, 1), :])
                cpart[:, r * tr:(r + 1) * tr] = lax.dot_general(
                    jnp.ones((8, 128), F32), prod, (((1,), (1,)), ((), ())), precision=HIGHEST,
                    preferred_element_type=F32)
            _exchange_start(cpart, call, c_send, c_recv, 0)

        @pl.when(kk == 0)
        def _():
            acc[...] = lax.dot_general(a_ref[...], b_ref[...], tn, preferred_element_type=F32)

        @pl.when(kk > 0)
        def _():
            acc[...] += lax.dot_general(a_ref[...], b_ref[...], tn, preferred_element_type=F32)

        def to_sibling(j):
            return pltpu.make_async_remote_copy(
                src_ref=sbuf.at[0], dst_ref=sib.at[j], send_sem=sib_send.at[j], recv_sem=sib_recv.at[j],
                device_id=(x, y, 1 - c), device_id_type=MESH)

        def to_chip(j):
            dist = _chip_order(j, c)
            return pltpu.make_async_remote_copy(
                src_ref=sbuf.at[1], dst_ref=recv_ref.at[dist // 2], send_sem=chip_send.at[j],
                recv_sem=chip_recv.at[dist // 2], device_id=_peer_at(dist), device_id_type=MESH)

        keep = pltpu.make_async_copy(sbuf.at[1], recv_ref.at[0], keep_sem)
        sends = []
        for j in range(4):
            sends += [to_sibling(j), to_chip(j) if j < 3 else keep]

        for st in range(NDEV):
            @pl.when((kk == nk - 1) & (s == st))
            def _(st=st):
                if st >= 2:
                    sends[st - 2].wait_send()
                part = acc[...]
                if st % 2 == 1:
                    to_sibling(st // 2).wait_recv()
                    part = part + sib[st // 2].astype(F32)
                sbuf[st % 2] = part.astype(BF16)
                sends[st].start()
                if st == NDEV - 1:
                    sends[st - 1].wait_send()
                    sends[st].wait()
                    for j in range(1, 4):
                        pltpu.make_async_remote_copy(
                            src_ref=sbuf.at[0], dst_ref=recv_ref.at[j], send_sem=chip_send.at[0],
                            recv_sem=chip_recv.at[j], device_id=_peer_at(2 * j), device_id_type=MESH).wait_recv()
                    _exchange_wait(gathered, arrivals)
                    lru_ref[...] = lall[...]
                    psum_ref[...] = psum[...]
                    pall_ref[...] = pall[...]
                    _exchange_finish(_exchange_pushes(cpart, call, c_send, c_recv, 0))
                    tot = call[0]
                    for j in range(1, NDEV):
                        tot = tot + call[j]
                    cctx_ref[...] = tot * ds_ref[...]

    blk = lambda s, w_ref: w_ref[0] ^ _scatter_order(s, w_ref[1])
    return _call(
        body, name=name,
        grid_spec=pltpu.PrefetchScalarGridSpec(
            num_scalar_prefetch=1, grid=(NDEV, nk),
            in_specs=[pl.BlockSpec((tk, m), lambda s, kk, w_ref: (kk, 0)),
                      pl.BlockSpec((tk, bn), lambda s, kk, w_ref: (kk, blk(s, w_ref))), ANY, ANY, ANY,
                      pl.BlockSpec((8, d), lambda s, kk, w_ref: (0, 0))],
            out_specs=(ANY, pl.BlockSpec((NDEV, rl, 128), lambda s, kk, w_ref: (0, 0, 0)),
                       pl.BlockSpec((rp, 128), lambda s, kk, w_ref: (0, 0)),
                       pl.BlockSpec((NDEV, rp, 128), lambda s, kk, w_ref: (0, 0, 0)),
                       pl.BlockSpec((8, d), lambda s, kk, w_ref: (0, 0))),
            scratch_shapes=[pltpu.VMEM((m, bn), F32), pltpu.VMEM((2, m, bn), BF16), pltpu.VMEM((4, m, bn), BF16),
                            pltpu.VMEM((NDEV, rl, 128), F32), pltpu.VMEM((rl, 128), F32),
                            pltpu.VMEM((NDEV, rl, 128), F32), pltpu.VMEM((NDEV, rp, 128), F32),
                            pltpu.VMEM((rp, 128), F32), pltpu.VMEM((d, cols), F32),
                            pltpu.VMEM((8, d), F32), pltpu.VMEM((NDEV, 8, d), F32),
                            pltpu.SemaphoreType.DMA((4,)), pltpu.SemaphoreType.DMA((4,)),
                            pltpu.SemaphoreType.DMA((4,)), pltpu.SemaphoreType.DMA((4,)),
                            pltpu.SemaphoreType.DMA,
                            pltpu.SemaphoreType.DMA((NDEV,)), pltpu.SemaphoreType.DMA((NDEV,)),
                            pltpu.SemaphoreType.DMA((NDEV,)), pltpu.SemaphoreType.DMA((NDEV,)),
                            pltpu.SemaphoreType.DMA((NDEV,)), pltpu.SemaphoreType.DMA((NDEV,)),
                            pltpu.SemaphoreType.DMA,
                            pltpu.SemaphoreType.DMA((NDEV - 1,)), pltpu.SemaphoreType.DMA((NDEV - 1,))]),
        out_shape=(jax.ShapeDtypeStruct((4, m, bn), BF16), jax.ShapeDtypeStruct((NDEV, rl, 128), F32),
                   jax.ShapeDtypeStruct((rp, 128), F32), jax.ShapeDtypeStruct((NDEV, rp, 128), F32),
                   jax.ShapeDtypeStruct((8, d), F32)),
        compiler_params=_params(("arbitrary", "arbitrary")),
    )(where, at, b, lru_parts, packed, w_ada, dsilu_cctx)


def _input_backward(dp, w_all, src, mv, row0, tm, nbk, name, dn=None, cols=None):
    rows, d = src.shape
    nb, _, bw = w_all.shape
    first, last = (0, nb * bw - 1) if cols is None else cols
    k0 = first // (nbk * bw)
    nk = last // (nbk * bw) - k0 + 1
    ni = rows // tm
    blk0 = row0 // tm
    latent = dn is not None

    def body(*refs):
        dp_ref, w_ref, x_ref, mv_ref = refs[:4]
        outs = refs[4 + latent:]
        part_ref, acc = outs[latent], outs[latent + 1]
        i, k = pl.program_id(0), pl.program_id(1)

        def product():
            step = _dot_nt(dp_ref[:, 0:bw], w_ref[0])
            for q in range(1, nbk):
                step = step + _dot_nt(dp_ref[:, q * bw:(q + 1) * bw], w_ref[q])
            return step

        def finish(slot):
            xf = x_ref[...]
            r = lax.rsqrt(jnp.mean(xf * xf, axis=-1, keepdims=True) + EPS)
            xn = xf * r
            dhl = acc[slot]
            gain, sc = mv_ref[0:1, :], mv_ref[1:2, :]
            dhx = jnp.sum(dhl * xn, axis=0, keepdims=True)
            part_ref[0:1, :] += jnp.sum(dhl, axis=0, keepdims=True)
            part_ref[1:2, :] += dhx * gain
            part_ref[2:3, :] += dhx * (1.0 + sc)
            if latent:
                dxn = dhl * (gain * (1.0 + sc))
                outs[0][...] = (refs[4][...].astype(F32)
                                + r * (dxn - xn * jnp.mean(dxn * xn, axis=-1, keepdims=True)))

        @pl.when((i == 0) & (k == 0))
        def _():
            part_ref[...] = jnp.zeros_like(part_ref)
            acc[0] = product()

        @pl.when((i > 0) & (i < ni) & (k == 0))
        def _():
            acc[i % 2] = product()
            finish((i - 1) % 2)

        @pl.when((i == ni) & (k == 0))
        def _():
            finish((ni - 1) % 2)

        @pl.when((i < ni) & (k > 0))
        def _():
            acc[i % 2] += product()

    tile = pl.BlockSpec((tm, d), lambda i, k: (jnp.maximum(i - 1, 0), 0))
    vec = pl.BlockSpec((8, d), lambda i, k: (0, 0))
    kblock = lambda i, k: k0 + jnp.where(i == ni, nk - 1, k)
    return _call(
        body, name=name,
        grid=(ni + 1, nk),
        out_shape=((jax.ShapeDtypeStruct((rows, d), F32),) if latent else ()) + (jax.ShapeDtypeStruct((8, d), F32),),
        in_specs=[pl.BlockSpec((tm, nbk * bw), lambda i, k: (blk0 + jnp.minimum(i, ni - 1), kblock(i, k))),
                  pl.BlockSpec((nbk, d, bw), lambda i, k: (kblock(i, k), 0, 0)), tile, vec]
                 + ([tile] if latent else []),
        out_specs=((tile,) if latent else ()) + (vec,),
        scratch_shapes=[pltpu.VMEM((2, tm, d), F32)],
        compiler_params=_params(("arbitrary", "arbitrary")),
    )(*([dp, w_all, src, mv] + ([dn] if latent else [])))


def _adamw_scattered(parts, w, m, v, tr):
    r, c = w.shape
    nslot = parts.shape[0]

    def body(p_ref, w_ref, m_ref, v_ref, g_ref, d_ref, m2_ref, v2_ref):
        g = p_ref[0].astype(F32)
        for k in range(1, nslot):
            g = g + p_ref[k].astype(F32)
        g_ref[...] = g
        d_ref[...], m2_ref[...], v2_ref[...] = _adamw(w_ref[...], g, m_ref[...], v_ref[...])

    tile = pl.BlockSpec((tr, c), lambda i: (i, 0))
    return _call(
        body, name="adamw_scattered_%dx%d" % (r, c),
        grid=(r // tr,),
        out_shape=tuple(jax.ShapeDtypeStruct((r, c), F32) for _ in range(4)),
        in_specs=[pl.BlockSpec((nslot, tr, c), lambda i: (0, i, 0)), tile, tile, tile],
        out_specs=(tile,) * 4,
        compiler_params=_params(("arbitrary",)),
    )(parts, w, m, v)


def _adamw_ada(st, dmod, w, m, v, tr):
    r, c = w.shape

    def body(s_ref, dm_ref, w_ref, m_ref, v_ref, g_ref, d_ref, m2_ref, v2_ref):
        g = jnp.dot(s_ref[...], dm_ref[...], precision=HIGHEST, preferred_element_type=F32)
        g_ref[...] = g
        d_ref[...], m2_ref[...], v2_ref[...] = _adamw(w_ref[...], g, m_ref[...], v_ref[...])

    tile = pl.BlockSpec((tr, c), lambda i: (i, 0))
    return _call(
        body, name="adamw_ada",
        grid=(r // tr,),
        out_shape=tuple(jax.ShapeDtypeStruct((r, c), F32) for _ in range(4)),
        in_specs=[pl.BlockSpec((tr, 16), lambda i: (i, 0)), pl.BlockSpec((16, c), lambda i: (0, 0)),
                  tile, tile, tile],
        out_specs=(tile,) * 4,
        compiler_params=_params(("arbitrary",)),
    )(st, dmod, w, m, v)


def _adamw_small(gs, ws, ms, vs):
    n = len(ws)

    def body(*refs):
        for j in range(n):
            g_ref, w_ref, m_ref, v_ref = refs[j], refs[n + j], refs[2 * n + j], refs[3 * n + j]
            d_ref, m2_ref, v2_ref = refs[4 * n + j], refs[5 * n + j], refs[6 * n + j]
            d_ref[...], m2_ref[...], v2_ref[...] = _adamw(w_ref[...], g_ref[...], m_ref[...], v_ref[...])

    shapes = tuple(jax.ShapeDtypeStruct(a.shape, F32) for a in ws)
    out = _call(
        body, name="adamw_small",
        out_shape=shapes * 3,
        in_specs=[VMEM] * (4 * n), out_specs=(VMEM,) * (3 * n),
        compiler_params=_params(),
    )(*gs, *ws, *ms, *vs)
    return list(out[:n]), list(out[n:2 * n]), list(out[2 * n:])


def _blockdiag_groups(wh, gc):
    h, dh, _ = wh.shape
    g = gc // dh
    w4 = wh.reshape(h // g, g, dh, dh)
    bd = jnp.einsum("ngij,gh->ngihj", w4, jnp.eye(g, dtype=wh.dtype))
    return bd.reshape(h // g, gc, gc)


def _blockdiag_extract(bd, dh):
    ng, gc, _ = bd.shape
    g = gc // dh
    x = bd.reshape(ng, g, dh, g, dh)
    return jnp.einsum("ngihj,gh->ngij", x, jnp.eye(g, dtype=bd.dtype)).reshape(ng * g, dh, dh)


def _largest_tile(n, cap):
    return max(q for q in range(128, min(n, cap) + 1, 128) if n % q == 0)


def _rows8(*vecs):
    rows = [jnp.reshape(v, (1, -1)).astype(F32) for v in vecs]
    n = rows[0].shape[1]
    return jnp.concatenate(rows + [jnp.zeros((8 - len(rows), n), F32)], axis=0)


def _pack(pieces):
    flat = jnp.concatenate([jnp.reshape(a, (-1,)).astype(F32) for a in pieces])
    total = -(-flat.shape[0] // 1024) * 1024
    return jnp.pad(flat, (0, total - flat.shape[0])).reshape(total // 128, 128)


def _unpack(packed, shapes):
    flat = packed.reshape(-1)
    out, off = [], 0
    for s in shapes:
        n = 1
        for q in s:
            n *= q
        out.append(flat[off:off + n].reshape(s))
        off += n
    return out


def kernel(x, c, ctx, c_ctx, norm_g, w_ada, b_ada, w_in, w_conv_a, w_conv_b, b_conv_b, lru_wa, lru_ba, lru_wx, lru_bx, lru_lambda, w_out, final_g, loss_target, m_c_ctx, m_norm_g, m_w_ada, m_b_ada, m_w_in, m_w_conv_a, m_w_conv_b, m_b_conv_b, m_lru_wa, m_lru_ba, m_lru_wx, m_lru_bx, m_lru_lambda, m_w_out, m_final_g, v_c_ctx, v_norm_g, v_w_ada, v_b_ada, v_w_in, v_w_conv_a, v_w_conv_b, v_b_conv_b, v_lru_wa, v_lru_ba, v_lru_wx, v_lru_bx, v_lru_lambda, v_w_out, v_final_g):
    _, l, d = x.shape
    lc = ctx.shape[1]
    w = d // 2
    t = lc
    assert l % t == 0 and t % GRID_W == 0 and t % 128 == 0
    dh = w // N_HEADS
    gc = min(w, MXU_WIDTH)
    cols = w_ada.shape[2]
    wo_rows = w_out.shape[1]
    me = _idx(_my_pos())
    x2, ctx2, tgt2 = x[0], ctx[0], loss_target[0]
    w_ada2, w_in2, w_out2 = w_ada[0], w_in[0], w_out[0]

    small_mine = jnp.concatenate([a.reshape(-1) for a in (w_conv_a, w_conv_b, lru_ba, lru_bx, lru_lambda)]
                                 + [jnp.zeros((3 * (w // NDEV),), F32)]).reshape(16, w // NDEV)
    mod_all, s_mat, small_all = _mod_forward(
        jnp.broadcast_to(c, (8, d)), jnp.broadcast_to(c_ctx[None], (8, d)), w_ada2, small_mine)
    mod = jnp.transpose(mod_all, (1, 0, 2)).reshape(16, NDEV * cols) + b_ada
    mod_lat = lax.dynamic_slice_in_dim(mod, me, 1, axis=0)
    sh_l, sc_l, gt_l = jnp.split(mod_lat, 3, axis=-1)
    sh_c, sc_c, _ = jnp.split(mod[8:9], 3, axis=-1)
    small = jnp.transpose(small_all, (1, 0, 2)).reshape(16, w)
    wca = _rows8(*[small[j] for j in range(0, 3)])
    wcb = _rows8(*[small[j] for j in range(3, 7)], b_conv_b)
    lv = _rows8(0.5 * small[7], 0.5 * small[9], small[11], 0.5 * small[8], 0.5 * small[10], small[12])
    wg = jnp.stack([
        jnp.concatenate([_blockdiag_groups(lru_wa[0, dr], gc), _blockdiag_groups(lru_wx[0, dr], gc)], axis=-1)
        for dr in range(2)])
    wg = (0.5 * wg).astype(BF16)

    la = l + lc
    tm = 2 * t if l % (2 * t) == 0 else t
    tk = 3 * t if la % (3 * t) == 0 else t
    h = _normalize(x2, _rows8(norm_g, sc_l, sh_l), la, 0, tm, "normalize")
    h = _normalize(ctx2, _rows8(norm_g, sc_c, sh_c), la, l, t, "normalize_ctx", prev=h)
    p, w_all, wo_all = _in_projection(h, w_in2.astype(BF16), w_out2.astype(BF16), la // 4 if la % 64 == 0 else tk)
    taps_m, back_m, perm = _scan_matrices(t)
    xb = _conv_input(p, wcb, taps_m, l, t)
    hf, hr = _lru_forward(xb, wg, lv, l, t)
    wo = wo_all.reshape(d, d)
    dn, cat, dout, part_mix = _mix_forward(x2, tgt2, p, hf, hr, wo, _rows8(gt_l, final_g), wca, perm, t)
    g_wout = _weight_grad_t(cat, dout, 2, 1, _largest_tile(l, 2048), "grad_w_out")
    dp, dhs, part_ca = _mix_backward(dout, p, hf, hr, wo, wca, perm, l, t)
    dxb0, dwg0, part_l0 = _lru_backward(0, xb, dhs, hf, wg, lv, l, t)
    dp, dwg1, part_l1, sc_wout = _lru_backward(
        1, xb, dhs, hr, wg, lv, l, t, conv=(p, wcb, back_m, dxb0, dp, g_wout.reshape(NDEV, wo_rows, d)))
    dwa = jnp.stack([_blockdiag_extract(dwg0[:, :, :gc], dh), _blockdiag_extract(dwg1[:, :, :gc], dh)])
    dwx = jnp.stack([_blockdiag_extract(dwg0[:, :, gc:], dh), _blockdiag_extract(dwg1[:, :, gc:], dh)])
    lru_part = (0.5 * jnp.stack([dwa, dwx])).reshape(NDEV, -1, 128)
    grad_x, part_lat = _input_backward(dp, w_all, x2, _rows8(norm_g, sc_l), 0, tm, 2, "input_backward", dn=dn)
    (part_ctx,) = _input_backward(dp, w_all, ctx2, _rows8(norm_g, sc_c), l, t, 2, "input_backward_ctx",
                                  cols=(4 * w, 5 * w - 1))
    part_in = jnp.concatenate([part_lat[0:2], part_ctx[0:2], (part_lat[2] + part_ctx[2])[None]], axis=0)

    zeros_d = jnp.zeros((d,), F32)
    pieces = [
        jnp.concatenate([part_in[0], part_in[1], part_mix[1]]),
        jnp.concatenate([part_in[2], part_in[3], zeros_d]),
        part_in[4], part_mix[0], part_ca[0:3], part_l1[4:8], part_l1[3],
        0.5 * jnp.stack([part_l0[0], part_l1[0]]), 0.5 * jnp.stack([part_l0[1], part_l1[1]]),
        jnp.stack([part_l0[2], part_l1[2]]), part_mix[2, 0:1],
    ]
    shapes = [(3 * d,), (3 * d,), (d,), (d,), (3, w), (4, w), (w,), (2, w), (2, w), (2, w), (1,)]
    sig_cc = jax.nn.sigmoid(c_ctx)
    dsilu_cc = jnp.broadcast_to((sig_cc * (1.0 + c_ctx * (1.0 - sig_cc)))[None], (8, d))
    sc_win, lru_sum, psum, pall, g_cctx8 = _weight_grad_scatter(
        h, dp, lru_part, _pack(pieces), w_ada2, dsilu_cc, tk, "grad_w_in")
    (g_modl, g_modc, g_norm, g_final, g_ca, g_cb, g_bcb, g_ba, g_bx, g_lam, loss1) = _unpack(psum, shapes)
    loss = loss1[0]
    g_cctx = g_cctx8[0]
    g_bada = (g_modl + g_modc)[None]
    g_lru = lru_sum.reshape(2, 2, N_HEADS, dh, dh)
    g_wa, g_wx = g_lru[0][None], g_lru[1][None]
    wsl = w // NDEV
    mine = lambda a: lax.dynamic_slice_in_dim(a, me * wsl, wsl, axis=-1)
    g_ca_m, g_cb_m, g_ba_m, g_bx_m, g_lam_m = (mine(g_ca)[None], mine(g_cb)[None], mine(g_ba)[None],
                                               mine(g_bx)[None], mine(g_lam)[None])
    g_norm, g_bcb = g_norm[None], g_bcb[None]

    per_dev = pall[:, :3 * d // 128].reshape(NDEV, NDEV, cols)
    dmod_lat = lax.dynamic_slice_in_dim(per_dev, me, 1, axis=1)[:, 0]
    dmod_ctx = lax.dynamic_slice_in_dim(g_modc.reshape(NDEV, cols), me, 1, axis=0)
    dmod16 = jnp.concatenate([dmod_lat, dmod_ctx, jnp.zeros((7, cols), F32)], axis=0)
    tr_ada = 256 if d % 256 == 0 else d
    g_wada, d_wada, m_wada, v_wada = _adamw_ada(s_mat.T, dmod16, w_ada2, m_w_ada[0], v_w_ada[0], tr_ada)
    g_win2, d_win, m_win, v_win = _adamw_scattered(sc_win, w_in2, m_w_in[0], v_w_in[0], tr_ada)
    tr_out = 64 if wo_rows % 64 == 0 else wo_rows
    g_wout2, d_wout, m_wout, v_wout = _adamw_scattered(sc_wout, w_out2, m_w_out[0], v_w_out[0], tr_out)

    small_w = [c_ctx, norm_g, b_ada, w_conv_a, w_conv_b, b_conv_b, lru_wa, lru_ba, lru_wx, lru_bx, lru_lambda, final_g]
    small_m = [m_c_ctx, m_norm_g, m_b_ada, m_w_conv_a, m_w_conv_b, m_b_conv_b, m_lru_wa, m_lru_ba, m_lru_wx,
               m_lru_bx, m_lru_lambda, m_final_g]
    small_v = [v_c_ctx, v_norm_g, v_b_ada, v_w_conv_a, v_w_conv_b, v_b_conv_b, v_lru_wa, v_lru_ba, v_lru_wx,
               v_lru_bx, v_lru_lambda, v_final_g]
    small_g = [g_cctx, g_norm, g_bada, g_ca_m, g_cb_m, g_bcb, g_wa, g_ba_m, g_wx, g_bx_m, g_lam_m, g_final]
    small_g = [jnp.reshape(a, b.shape) for a, b in zip(small_g, small_w)]
    d_s, m_s, v_s = _adamw_small(small_g, small_w, small_m, small_v)

    def weights(small_list, ada, win, wout):
        (cctx_, norm_, bada_, ca_, cb_, bcb_, wa_, ba_, wx_, bx_, lam_, final_) = small_list
        return [cctx_, norm_, ada[None], bada_, win[None], ca_, cb_, bcb_, wa_, ba_, wx_, bx_, lam_, wout[None], final_]

    return (loss, grad_x[None],
            *weights(small_g, g_wada, g_win2, g_wout2), *weights(d_s, d_wada, d_win, d_wout),
            *weights(m_s, m_wada, m_win, m_wout), *weights(v_s, v_wada, v_win, v_wout))
```

```python
import functools

import jax
import jax.numpy as jnp
import numpy as np
from jax import lax
from jax.experimental import pallas as pl
from jax.experimental.pallas import tpu as pltpu

F32 = jnp.float32
BF16 = jnp.bfloat16
MESH = pl.DeviceIdType.MESH
NDEV = 8
GRID_W = 64
N_HEADS = 16
LRU_C = 8.0
EPS = 1e-6
MXU_WIDTH = 256
VMEM_LIMIT = 60 * 1024 * 1024

ADAM_LR = 0.001
ADAM_B1 = 0.9
ADAM_B2 = 0.999
ADAM_EPS = 1e-08
ADAM_WD = 0.01
ADAM_STEP = 10
ADAM_C1 = 1.0 - ADAM_B1 ** ADAM_STEP
ADAM_C2 = 1.0 - ADAM_B2 ** ADAM_STEP

HIGHEST = lax.Precision.HIGHEST
ANY = pl.BlockSpec(memory_space=pl.ANY)
VMEM = pl.BlockSpec(memory_space=pltpu.VMEM)


def _call(body, **kw):
    return pl.pallas_call(body, **kw)


def _params(sem=None, vmem=VMEM_LIMIT):
    return pltpu.CompilerParams(dimension_semantics=sem, vmem_limit_bytes=vmem)


def _my_pos():
    return lax.axis_index("x"), lax.axis_index("y"), lax.axis_index("c")


def _idx(pos):
    return 4 * pos[0] + 2 * pos[1] + pos[2]


def _peer(k):
    x, y, c = _my_pos()
    return ((1 - x) if (k >> 2) & 1 else x, (1 - y) if (k >> 1) & 1 else y, (1 - c) if k & 1 else c)


def _exchange_pushes(src_ref, dst_ref, send_sems, recv_sems, base):
    me = _idx(_my_pos())
    sends = [pltpu.make_async_remote_copy(
        src_ref=src_ref, dst_ref=dst_ref.at[me], send_sem=send_sems.at[base + k - 1],
        recv_sem=recv_sems.at[base + k - 1], device_id=_peer(k), device_id_type=MESH) for k in range(1, NDEV)]
    return sends, (src_ref, dst_ref, send_sems, recv_sems, base)


def _exchange_start(src_ref, dst_ref, send_sems, recv_sems, base):
    started = _exchange_pushes(src_ref, dst_ref, send_sems, recv_sems, base)
    for cp in started[0]:
        cp.start()
    dst_ref[_idx(_my_pos())] = src_ref[...]
    return started


def _exchange_finish(started):
    sends, (src_ref, dst_ref, send_sems, recv_sems, base) = started
    for k in range(1, NDEV):
        peer = _peer(k)
        pltpu.make_async_remote_copy(
            src_ref=src_ref, dst_ref=dst_ref.at[_idx(peer)], send_sem=send_sems.at[base + k - 1],
            recv_sem=recv_sems.at[base + k - 1], device_id=peer, device_id_type=MESH).wait_recv()
    for cp in sends:
        cp.wait_send()


def _exchange_vmem(src_ref, dst_ref, send_sems, recv_sems, base):
    _exchange_finish(_exchange_start(src_ref, dst_ref, send_sems, recv_sems, base))


def _sigmoid(z):
    return 0.5 * jnp.tanh(0.5 * z) + 0.5


def _softplus(x):
    return jnp.maximum(x, 0.0) + jnp.log1p(jnp.exp(-jnp.abs(x)))


def _one_minus_sq(a, la):
    series = (-2.0 * la) * (1.0 + la)
    return jnp.where(la > -0.0015, series, 1.0 - a * a)


def _dot(a, b):
    return jnp.dot(a, b, preferred_element_type=F32)


def _dot_nt(a, b):
    return lax.dot_general(a, b, (((1,), (1,)), ((), ())), preferred_element_type=F32)


def _rows(shape):
    return lax.broadcasted_iota(jnp.int32, shape, 0)


def _scan_matrices(t):
    seg = t // 8
    r = np.arange(t)
    perm = (np.arange(t)[None, :] == ((r % 8) * seg + r // 8)[:, None]).astype(np.float32)
    rows, cols = r[:, None], r[None, :]
    taps, back = [], []
    for rowlen in (GRID_W, t):
        pos = rows % rowlen
        shift = {-2: (cols == rows - 2) & (pos >= 2), -1: (cols == rows - 1) & (pos >= 1),
                 0: cols == rows, 1: (cols == rows + 1) & (pos + 1 < rowlen),
                 2: (cols == rows + 2) & (pos + 2 < rowlen)}
        if rowlen == GRID_W:
            beside = [shift[-1].astype(np.float32), shift[1].astype(np.float32)]
        taps.append(np.stack([perm @ shift[k].astype(np.float32) for k in (-2, -1, 0, 1)]))
        back.append(np.stack([shift[k].astype(np.float32) @ perm.T for k in (2, 1, 0, -1)]))
    as_bf16 = lambda a: jnp.asarray(a, dtype=BF16)
    return as_bf16(np.stack(taps)), as_bf16(np.stack(back)), as_bf16(np.stack([perm, perm.T] + beside))


def _chunk_scan(a, b, reverse):
    row = _rows(a.shape)
    for s in (1, 2, 4):
        if reverse:
            m = row < 8 - s
            sh = 8 - s
        else:
            m = row >= s
            sh = s
        a_s = jnp.where(m, pltpu.roll(a, sh, 0), 1.0)
        b_s = jnp.where(m, pltpu.roll(b, sh, 0), 0.0)
        b = b + a * b_s
        a = a * a_s
    return a, b


def _chain_segments(ptot, hend, carry, reverse):
    ca, cb = _chunk_scan(ptot, hend, reverse)
    incl = ca * carry + cb
    r8 = _rows(incl.shape)
    if reverse:
        start = jnp.where(r8 < 7, pltpu.roll(incl, 7, 0), carry)
        last = incl[0:1, :]
    else:
        start = jnp.where(r8 >= 1, pltpu.roll(incl, 1, 0), carry)
        last = incl[7:8, :]
    return start, jnp.broadcast_to(last, incl.shape)


def _blocks(nblock, reverse):
    order = range(nblock - 1, -1, -1) if reverse else range(nblock)
    return [slice(8 * k, 8 * k + 8) for k in order]


def _scan_tile(a_ref, b_ref, out_ref, carry, reverse):
    t, w = a_ref.shape
    seg = t // 8

    hend, ptot = jnp.zeros((8, w), F32), jnp.ones((8, w), F32)
    for rows in _blocks(seg, reverse):
        a = a_ref[rows, :]
        hend, ptot = a * hend + b_ref[rows, :], a * ptot
    h, new_carry = _chain_segments(ptot, hend, carry, reverse)
    for rows in _blocks(seg, reverse):
        h = a_ref[rows, :] * h + b_ref[rows, :]
        out_ref[rows, :] = h
    return new_carry


def _scan_tile_backward(a_ref, dh_ref, g_ref, carry, reverse):
    t, w = a_ref.shape
    seg = t // 8

    uend, ptot = jnp.zeros((8, w), F32), jnp.ones((8, w), F32)
    for rows in _blocks(seg, reverse):
        a = a_ref[rows, :]
        uend, ptot = a * (dh_ref[rows, :] + uend), a * ptot
    u, new_carry = _chain_segments(ptot, uend, carry, reverse)
    for rows in _blocks(seg, reverse):
        g = dh_ref[rows, :] + u
        g_ref[rows, :] = g
        u = a_ref[rows, :] * g
    return new_carry


def _lru_coef(xb, wg_ref, d, ba, bx, lam, gc):
    w = xb.shape[1]
    xb16 = xb.astype(BF16)
    zr, zi = [], []
    for g in range(w // gc):
        z = _dot(xb16[:, g * gc:(g + 1) * gc], wg_ref[d, g])
        zr.append(z[:, :gc])
        zi.append(z[:, gc:])
    zr = zr[0] if len(zr) == 1 else jnp.concatenate(zr, axis=-1)
    zi = zi[0] if len(zi) == 1 else jnp.concatenate(zi, axis=-1)
    tr = jnp.tanh(zr + ba)
    ti = jnp.tanh(zi + bx)
    sp = _softplus(-lam)
    half = -0.5 * LRU_C * sp
    la = tr * half + half
    a = jnp.exp(la)
    q = _one_minus_sq(a, la)
    rs = lax.rsqrt(jnp.maximum(q, 1e-30))
    return a, q * rs, rs, tr, ti, sp


def _adamw(w, g, m, v):
    m2 = ADAM_B1 * m + (1.0 - ADAM_B1) * g
    v2 = ADAM_B2 * v + (1.0 - ADAM_B2) * (g * g)
    m_hat = m2 / ADAM_C1
    v_hat = v2 / ADAM_C2
    delta = -ADAM_LR * (m_hat / (jnp.sqrt(v_hat) + ADAM_EPS) + ADAM_WD * w)
    return delta, m2, v2


def _mod_forward(c8, cctx8, w_ada, small):
    d = c8.shape[1]
    cols = w_ada.shape[1]

    def body(c_ref, cctx_ref, w_ref, sm_ref, mod_ref, s_ref, sm_all, cbuf, mod_my, send_sems, recv_sems):
        _exchange_vmem(sm_ref, sm_all, send_sems, recv_sems, 2 * (NDEV - 1))
        _exchange_vmem(c_ref, cbuf, send_sems, recv_sems, 0)
        row = _rows((8, d))
        c_all = jnp.zeros((8, d), F32)
        for b in range(NDEV):
            c_all = jnp.where(row == b, cbuf[b], c_all)
        cc = cctx_ref[...]
        s_top = c_all * _sigmoid(c_all)
        s_bot = jnp.where(row == 0, cc * _sigmoid(cc), 0.0)
        s = jnp.concatenate([s_top, s_bot], axis=0)
        s_ref[...] = s
        mod_my[...] = jnp.dot(s, w_ref[...], precision=HIGHEST, preferred_element_type=F32)
        _exchange_vmem(mod_my, mod_ref, send_sems, recv_sems, NDEV - 1)

    return _call(
        body, name="mod_forward",
        out_shape=(jax.ShapeDtypeStruct((NDEV, 16, cols), F32), jax.ShapeDtypeStruct((16, d), F32),
                   jax.ShapeDtypeStruct((NDEV,) + small.shape, F32)),
        in_specs=[VMEM] * 4, out_specs=(VMEM,) * 3,
        scratch_shapes=[pltpu.VMEM((NDEV, 8, d), F32), pltpu.VMEM((16, cols), F32),
                        pltpu.SemaphoreType.DMA((3 * (NDEV - 1),)), pltpu.SemaphoreType.DMA((3 * (NDEV - 1),))],
        compiler_params=_params(),
    )(c8, cctx8, w_ada, small)


def _scatter_copies(src_ref, dst_ref, send_sems, recv_sems):
    me = _idx(_my_pos())
    copies = [pltpu.make_async_copy(src_ref.at[me], dst_ref.at[0], send_sems.at[0])]
    for k in range(1, NDEV):
        peer = _peer(k)
        copies.append(pltpu.make_async_remote_copy(
            src_ref=src_ref.at[_idx(peer)], dst_ref=dst_ref.at[k], send_sem=send_sems.at[k],
            recv_sem=recv_sems.at[k], device_id=peer, device_id_type=MESH))
    return copies


def _gather_copies(src_ref, dst_ref, send_sems, recv_sems):
    me = _idx(_my_pos())
    sends = [pltpu.make_async_copy(src_ref, dst_ref.at[me], send_sems.at[0])]
    arrivals = []
    for k in range(1, NDEV):
        peer = _peer(k)
        sends.append(pltpu.make_async_remote_copy(
            src_ref=src_ref, dst_ref=dst_ref.at[me], send_sem=send_sems.at[k],
            recv_sem=recv_sems.at[k], device_id=peer, device_id_type=MESH))
        arrivals.append(pltpu.make_async_remote_copy(
            src_ref=src_ref, dst_ref=dst_ref.at[_idx(peer)], send_sem=send_sems.at[k],
            recv_sem=recv_sems.at[k], device_id=peer, device_id_type=MESH))
    return sends, arrivals


def _exchange_wait(sends, arrivals):
    sends[0].wait()
    for cp in arrivals:
        cp.wait_recv()
    for cp in sends[1:]:
        cp.wait_send()


def _chip_order(k, c):
    return (6, 4 - 2 * c, 2 + 2 * c, 0)[k]


def _scatter_order(s, c):
    k = s >> 1
    mine = jnp.where(k == 0, 6, jnp.where(k == 1, 4 - 2 * c, jnp.where(k == 2, 2 + 2 * c, 0)))
    theirs = jnp.where(k == 0, 6, jnp.where(k == 1, 2 + 2 * c, jnp.where(k == 2, 4 - 2 * c, 0))) ^ 1
    return jnp.where((s & 1) == 0, theirs, mine)


def _peer_at(dist):
    x, y, c = _my_pos()
    return (x ^ ((dist >> 2) & 1), y ^ ((dist >> 1) & 1), c ^ (dist & 1))


def _normalize(src, mv, la, row0, tm, name, prev=None):
    rows, d = src.shape
    blk0 = row0 // tm

    def body(*refs):
        x_ref, mv_ref, h_ref = refs[0], refs[1], refs[-1]
        xf = x_ref[...]
        r = lax.rsqrt(jnp.mean(xf * xf, axis=-1, keepdims=True) + EPS)
        h = xf * r * (mv_ref[0:1, :] * (1.0 + mv_ref[1:2, :])) + mv_ref[2:3, :]
        h_ref[...] = h.astype(BF16)

    in_specs = [pl.BlockSpec((tm, d), lambda i: (i, 0)), pl.BlockSpec((8, d), lambda i: (0, 0))]
    args = [src, mv]
    aliases = {}
    if prev is not None:
        in_specs += [ANY]
        args += [prev]
        aliases = {2: 0}
    return _call(
        body, name=name,
        grid=(rows // tm,),
        out_shape=jax.ShapeDtypeStruct((la, d), BF16),
        in_specs=in_specs,
        out_specs=pl.BlockSpec((tm, d), lambda i: (blk0 + i, 0)),
        input_output_aliases=aliases,
        compiler_params=_params(("arbitrary",)),
    )(*args)


def _gather_order(step):
    return (step & 1) | (((step >> 2) & 1) << 1) | (((step >> 1) & 1) << 2)


def _in_projection(h, w_shard, wo_shard, tm):
    la, d = h.shape
    bw = w_shard.shape[1]
    ni = la // tm
    where = jnp.reshape(_idx(_my_pos()), (1,)).astype(jnp.int32)

    def body(me_ref, h_ref, w_ref, wo_ref, p_ref, all_ref, wo_all, wbuf, send_sems, recv_sems, local_sems,
             wo_send, wo_recv):
        s, i = pl.program_id(0), pl.program_id(1)
        x, y, c = _my_pos()
        wo_sends, wo_arrivals = _gather_copies(wo_ref, wo_all, wo_send, wo_recv)

        @pl.when((s == NDEV // 2) & (i == 0))
        def _():
            for cp in wo_sends:
                cp.start()

        me, sibling = (x, y, c), (x, y, 1 - c)
        chips = [(1 - x, y), (x, 1 - y), (1 - x, 1 - y)]

        def copy(k, block, to, from_shard=False):
            return pltpu.make_async_remote_copy(
                src_ref=w_ref if from_shard else all_ref.at[_idx(block)], dst_ref=all_ref.at[_idx(block)],
                send_sem=send_sems.at[k], recv_sem=recv_sems.at[k], device_id=to, device_id_type=MESH)

        def load(block, slot):
            return pltpu.make_async_copy(all_ref.at[_idx(block)], wbuf.at[slot], local_sems.at[1])

        keep = pltpu.make_async_copy(w_ref, all_ref.at[_idx(me)], local_sems.at[0])
        first = [copy(0, me, sibling, True)] + [copy(1 + j, me, (*chip, c), True) for j, chip in enumerate(chips)]
        passed = [copy(4 + j, (*chip, c), sibling) for j, chip in enumerate(chips)]
        steps = [(copy(0, sibling, me), None, sibling)]
        for j, chip in enumerate(chips):
            steps.append((copy(1 + j, (*chip, c), me), passed[j], (*chip, c)))
            steps.append((copy(4 + j, (*chip, 1 - c), me), None, (*chip, 1 - c)))

        @pl.when((s == 0) & (i == 0))
        def _():
            keep.start()
            mine = pltpu.make_async_copy(w_ref, wbuf.at[0], local_sems.at[1])
            mine.start()
            for cp in first:
                cp.start()
            mine.wait()

        for n, (arrival, forward, block) in enumerate(steps, start=1):
            @pl.when((s == n - 1) & (i == ni - 1))
            def _(arrival=arrival, forward=forward, block=block, n=n):
                arrival.wait_recv()
                if forward is not None:
                    forward.start()
                load(block, n % 2).start()

        @pl.when((s > 0) & (i == 0))
        def _():
            load(me, s % 2).wait()

        p_ref[...] = _dot(h_ref[...], wbuf[s % 2]).astype(BF16)

        @pl.when((s == NDEV - 1) & (i == ni - 1))
        def _():
            for cp in first + passed:
                cp.wait_send()
            keep.wait()
            _exchange_wait(wo_sends, wo_arrivals)

    return _call(
        body, name="in_projection",
        grid_spec=pltpu.PrefetchScalarGridSpec(
            num_scalar_prefetch=1, grid=(NDEV, ni),
            in_specs=[pl.BlockSpec((tm, d), lambda s, i, me_ref: (i, 0)), ANY, ANY],
            out_specs=(pl.BlockSpec((tm, bw), lambda s, i, me_ref: (i, me_ref[0] ^ _gather_order(s))), ANY, ANY),
            scratch_shapes=[pltpu.VMEM((2, d, bw), BF16), pltpu.SemaphoreType.DMA((7,)),
                            pltpu.SemaphoreType.DMA((7,)), pltpu.SemaphoreType.DMA((2,)),
                            pltpu.SemaphoreType.DMA((NDEV,)), pltpu.SemaphoreType.DMA((NDEV,))]),
        out_shape=(jax.ShapeDtypeStruct((la, NDEV * bw), BF16), jax.ShapeDtypeStruct((NDEV, d, bw), BF16),
                   jax.ShapeDtypeStruct((NDEV,) + wo_shard.shape, wo_shard.dtype)),
        compiler_params=_params(("arbitrary", "arbitrary")),
    )(where, h, w_shard, wo_shard)


def _conv_input(p, wcb, taps_m, l, t):
    la = p.shape[0]
    w = wcb.shape[1]
    nt = l // t

    def body(v_ref, wcb_ref, tm_ref, xb_ref):
        taps = _dot(tm_ref[...].reshape(4 * t, t), v_ref[...])
        xb = wcb_ref[4:5, :] + wcb_ref[0:1, :] * taps[0:t]
        for j in range(1, 4):
            xb = xb + wcb_ref[j:j + 1, :] * taps[j * t:(j + 1) * t]
        xb_ref[...] = xb

    return _call(
        body, name="conv_input",
        grid=(nt + 1,),
        out_shape=jax.ShapeDtypeStruct((la, w), F32),
        in_specs=[pl.BlockSpec((t, w), lambda i: (i, 4)), pl.BlockSpec((8, w), lambda i: (0, 0)),
                  pl.BlockSpec((None, 4, t, t), lambda i: (i // nt, 0, 0, 0))],
        out_specs=pl.BlockSpec((t, w), lambda i: (i, 0)),
        compiler_params=_params(("arbitrary",)),
    )(p, wcb, taps_m)


def _lru_forward(xb, wg, lv, l, t):
    la, w = xb.shape
    gc = wg.shape[2]
    nt = l // t

    def body(xf_ref, xr_ref, wg_ref, lv_ref, hf_ref, hr_ref, a_s, b_s, carry):
        @pl.when(pl.program_id(0) == 0)
        def _():
            carry[...] = jnp.zeros_like(carry)

        for dr, (x_ref, h_ref) in enumerate(((xf_ref, hf_ref), (xr_ref, hr_ref))):
            x = x_ref[...]
            a, s, _, _, ti, _ = _lru_coef(x, wg_ref, dr, lv_ref[3 * dr:3 * dr + 1, :],
                                          lv_ref[3 * dr + 1:3 * dr + 2, :], lv_ref[3 * dr + 2:3 * dr + 3, :], gc)
            a_s[...] = a
            b_s[...] = (s * x) * (0.5 * ti + 0.5)
            carry[dr] = _scan_tile(a_s, b_s, h_ref, carry[dr], dr == 1)

    full = lambda shape: pl.BlockSpec(shape, lambda i: (0,) * len(shape))
    fmap = lambda i: (jnp.where(i == 0, nt, i - 1), 0)
    rmap = lambda i: (jnp.where(i == 0, nt, nt - i), 0)
    return _call(
        body, name="lru_forward",
        grid=(nt + 1,),
        out_shape=(jax.ShapeDtypeStruct((la, w), F32), jax.ShapeDtypeStruct((la, w), F32)),
        in_specs=[pl.BlockSpec((t, w), fmap), pl.BlockSpec((t, w), rmap), full(wg.shape), full(lv.shape)],
        out_specs=(pl.BlockSpec((t, w), fmap), pl.BlockSpec((t, w), rmap)),
        scratch_shapes=[pltpu.VMEM((t, w), F32), pltpu.VMEM((t, w), F32), pltpu.VMEM((2, 8, w), F32)],
        compiler_params=_params(("arbitrary",)),
    )(xb, xb, wg, lv)


def _mix_gates(p_refs, hf_ref, hr_ref, wca_ref, perm_ref, t, w):
    bl, cl, ul, gl, ql = [r[...].astype(F32) for r in p_refs]
    tt = cl * ul
    tt16 = tt.astype(BF16)
    beside = _dot(perm_ref[2:4].reshape(2 * t, t), tt16)
    before, after = beside[:t], beside[t:]
    z = wca_ref[0:1, :] * before + wca_ref[1:2, :] * tt + wca_ref[2:3, :] * after
    sig_g = _sigmoid(gl)
    sig_q = _sigmoid(ql)
    ylru = _dot(perm_ref[1], (hf_ref[...] + hr_ref[...]).astype(BF16))
    return bl, cl, ul, gl, ql, (before, tt, after), z, sig_g, sig_q, ylru


def _p_specs(t, w, nt):
    return [pl.BlockSpec((t, w), functools.partial(lambda i, s: (jnp.minimum(i, nt - 1), s), s=s))
            for s in (0, 1, 2, 3, 5)]


def _mix_forward(x, tgt, p, hf, hr, wo, ov, wca, perm, t):
    l, d = x.shape
    w = d // 2
    nt = l // t

    def body(x_ref, tg_ref, b_ref, c_ref, u_ref, g_ref, q_ref, hf_ref, hr_ref, wo_ref, ov_ref, wca_ref, perm_ref,
             dn_ref, ct_ref, do_ref, part_ref):
        i = pl.program_id(0)
        bl, _, _, gl, ql, _, z, sig_g, sig_q, ylru = _mix_gates(
            (b_ref, c_ref, u_ref, g_ref, q_ref), hf_ref, hr_ref, wca_ref, perm_ref, t, w)
        ya = bl * z * (gl * sig_g)
        yb = ylru * (ql * sig_q)
        ct_ref[:, 0:w] = ya.astype(BF16)
        ct_ref[:, w:] = yb.astype(BF16)
        out = _dot(ya.astype(BF16), wo_ref[0:w, :]) + _dot(yb.astype(BF16), wo_ref[w:, :])
        gate, fg = ov_ref[0:1, :], ov_ref[1:2, :]
        n = x_ref[...] + gate * out
        rr = lax.rsqrt(jnp.mean(n * n, axis=-1, keepdims=True) + EPS)
        nh = n * rr
        e = nh * fg - tg_ref[...]
        loss = 0.5 * jnp.sum(jnp.mean(e * e, axis=-1, keepdims=True), axis=0, keepdims=True)
        dy = e * (1.0 / d)
        dnh = dy * fg
        dn = rr * (dnh - nh * jnp.mean(dnh * nh, axis=-1, keepdims=True))
        dn_ref[...] = dn.astype(BF16)
        do_ref[...] = (dn * gate).astype(BF16)

        @pl.when(i == 0)
        def _():
            part_ref[...] = jnp.zeros_like(part_ref)

        part_ref[0:1, :] += jnp.sum(dy * nh, axis=0, keepdims=True)
        part_ref[1:2, :] += jnp.sum(dn * out, axis=0, keepdims=True)
        part_ref[2:3, :] += jnp.broadcast_to(loss, (1, d))

    tile = lambda cols: pl.BlockSpec((t, cols), lambda i: (i, 0))
    full = lambda shape: pl.BlockSpec(shape, lambda i: (0,) * len(shape))
    return _call(
        body, name="mix_forward",
        grid=(nt,),
        out_shape=(jax.ShapeDtypeStruct((l, d), BF16), jax.ShapeDtypeStruct((l, d), BF16),
                   jax.ShapeDtypeStruct((l, d), BF16), jax.ShapeDtypeStruct((8, d), F32)),
        in_specs=[tile(d), tile(d)] + _p_specs(t, w, nt) + [tile(w), tile(w),
                  pl.BlockSpec((d, d), lambda i: (0, 0), pipeline_mode=pl.Buffered(1)),
                  full(ov.shape), full(wca.shape), full(perm.shape)],
        out_specs=(tile(d), tile(d), tile(d), full((8, d))),
        compiler_params=_params(("arbitrary",)),
    )(x, tgt, p, p, p, p, p, hf, hr, wo, ov, wca, perm)


def _mix_backward(dout, p, hf, hr, wo, wca, perm, l, t):
    d = dout.shape[1]
    w = d // 2
    nt = l // t
    la = p.shape[0]

    def body(do_ref, b_ref, c_ref, u_ref, g_ref, q_ref, hf_ref, hr_ref, wo_ref, wca_ref, perm_ref,
             dp_ref, dh_ref, part_ref):
        i = pl.program_id(0)

        @pl.when(i == 0)
        def _():
            part_ref[...] = jnp.zeros_like(part_ref)

        @pl.when(i == nt)
        def _():
            dp_ref[...] = jnp.zeros_like(dp_ref)

        @pl.when(i < nt)
        def _():
            bl, cl, ul, gl, ql, taps, z, sig_g, sig_q, ylru = _mix_gates(
                (b_ref, c_ref, u_ref, g_ref, q_ref), hf_ref, hr_ref, wca_ref, perm_ref, t, w)
            do = do_ref[...]
            dya = _dot_nt(do, wo_ref[0:w, :])
            dyb = _dot_nt(do, wo_ref[w:, :])
            sg = gl * sig_g
            dz = dya * bl * sg
            dz16 = dz.astype(BF16)
            beside = _dot(perm_ref[2:4].reshape(2 * t, t), dz16)
            dt = wca_ref[0:1, :] * beside[t:] + wca_ref[1:2, :] * dz + wca_ref[2:3, :] * beside[:t]
            dp_ref[:, 0:w] = (dya * z * sg).astype(BF16)
            dp_ref[:, w:2 * w] = (dt * ul).astype(BF16)
            dp_ref[:, 2 * w:3 * w] = (dt * cl).astype(BF16)
            dp_ref[:, 3 * w:4 * w] = (dya * bl * z * (sig_g * (1.0 + gl * (1.0 - sig_g)))).astype(BF16)
            dp_ref[:, 4 * w:5 * w] = jnp.zeros((t, w), BF16)
            dp_ref[:, 5 * w:6 * w] = (dyb * ylru * (sig_q * (1.0 + ql * (1.0 - sig_q)))).astype(BF16)
            dh_ref[...] = _dot(perm_ref[0], (dyb * (ql * sig_q)).astype(BF16)).astype(BF16)
            for j in range(3):
                part_ref[j:j + 1, :] += jnp.sum(dz * taps[j], axis=0, keepdims=True)

    clamp = lambda cols: pl.BlockSpec((t, cols), lambda i: (jnp.minimum(i, nt - 1), 0))
    full = lambda shape: pl.BlockSpec(shape, lambda i: (0,) * len(shape))
    return _call(
        body, name="mix_backward",
        grid=(nt + 1,),
        out_shape=(jax.ShapeDtypeStruct((la, 6 * w), BF16), jax.ShapeDtypeStruct((l, w), BF16),
                   jax.ShapeDtypeStruct((8, w), F32)),
        in_specs=[clamp(d)] + _p_specs(t, w, nt) + [clamp(w), clamp(w),
                  pl.BlockSpec((d, d), lambda i: (0, 0), pipeline_mode=pl.Buffered(1)), full(wca.shape),
                  full(perm.shape)],
        out_specs=(pl.BlockSpec((t, 6 * w), lambda i: (i, 0)), clamp(w), full((8, w))),
        compiler_params=_params(("arbitrary",)),
    )(dout, p, p, p, p, p, hf, hr, wo, wca, perm)


def _lru_backward(direction, xb, dhs, hs, wg, lv, l, t, conv=None):
    la, w = hs.shape
    gc = wg.shape[2]
    ng = w // gc
    nt = l // t
    nblk8 = la // 8
    last = conv is not None
    assert last == (direction == 1)

    if direction == 0:
        tile = lambda i: jnp.where(i == nt, nt, nt - 1 - i)
        halo = lambda i: jnp.where(tile(i) == 0, nblk8 - 1, tile(i) * (t // 8) - 1)
    else:
        tile = lambda i: i
        halo = lambda i: jnp.minimum((i + 1) * (t // 8), nblk8 - 1)

    def body(*refs):
        x_ref, dh_ref, hs_ref, halo_ref, wg_ref, lv_ref = refs[:6]
        i = pl.program_id(0)
        is_ctx = i == nt
        if last:
            v_ref, wcb_ref, bm_ref, dxo_ref, _, gw_ref = refs[6:12]
            out_ref, dwg_ref, part_ref, sc_ref, a_s, dh_s, g_s, carry, send_sems, recv_sems = refs[12:]
            copies = _scatter_copies(gw_ref, sc_ref, send_sems, recv_sems)
        else:
            out_ref, dwg_ref, part_ref, a_s, dh_s, g_s, carry = refs[-7:]
            copies = []

        @pl.when(i == 0)
        def _():
            carry[...] = jnp.zeros_like(carry)
            dwg_ref[...] = jnp.zeros_like(dwg_ref)
            part_ref[...] = jnp.zeros_like(part_ref)
            for cp in copies:
                cp.start()

        if last:
            @pl.when(is_ctx)
            def _():
                _exchange_wait(copies, copies[1:])

        xb = x_ref[...]
        lam = lv_ref[3 * direction + 2:3 * direction + 3, :]
        a, s, rs, tr, ti, sp = _lru_coef(xb, wg_ref, direction, lv_ref[3 * direction:3 * direction + 1, :],
                                         lv_ref[3 * direction + 1:3 * direction + 2, :], lam, gc)
        hs_t = hs_ref[...]
        r8 = _rows((8, w))
        if direction == 0:
            edge = jnp.where(is_ctx, 0.0, halo_ref[7:8, :])
            first = jnp.where(r8 == 0, edge, pltpu.roll(hs_t[t - 8:, :], 1, 0))
            hprev = jnp.concatenate([first, hs_t[:t - 8, :]], axis=0)
        else:
            edge = jnp.where(is_ctx, 0.0, halo_ref[0:1, :])
            final = jnp.where(r8 == 7, edge, pltpu.roll(hs_t[:8, :], 7, 0))
            hprev = jnp.concatenate([hs_t[8:, :], final], axis=0)
        a_s[...] = a
        dh_s[...] = jnp.where(is_ctx, 0.0, dh_ref[...].astype(F32))
        carry[...] = _scan_tile_backward(a_s, dh_s, g_s, carry[...], direction == 0)

        g = g_s[...]
        r = 0.5 * tr + 0.5
        ig = 0.5 * ti + 0.5
        ix = ig * xb
        gs = g * s
        dla = (g * a) * (hprev - ix * (a * rs))
        dxb = gs * ig
        dzr = dla * (r * (1.0 - tr)) * (-LRU_C * sp)
        dzi = gs * ix * (1.0 - ti)
        part_ref[0:1, :] += jnp.sum(dzr, axis=0, keepdims=True)
        part_ref[1:2, :] += jnp.sum(dzi, axis=0, keepdims=True)
        part_ref[2:3, :] += jnp.sum(dla * r, axis=0, keepdims=True) * (LRU_C * _sigmoid(-lam))
        pieces = []
        for gi in range(ng):
            sl = slice(gi * gc, (gi + 1) * gc)
            dz = jnp.concatenate([dzr[:, sl], dzi[:, sl]], axis=-1).astype(BF16)
            pieces.append(_dot_nt(dz, wg_ref[direction, gi]))
            dwg_ref[gi] += _dot(xb[:, sl].T.astype(BF16), dz)
        dxb = dxb + (pieces[0] if ng == 1 else jnp.concatenate(pieces, axis=-1))
        if not last:
            out_ref[...] = dxb
        else:
            dxb = dxb + dxo_ref[...]
            v = v_ref[...].astype(F32)
            backs = _dot(bm_ref[...].reshape(4 * t, t), dxb.astype(BF16))
            dv = jnp.zeros((t, w), F32)
            for j in range(4):
                back = backs[j * t:(j + 1) * t]
                dv = dv + wcb_ref[j:j + 1, :] * back
                part_ref[4 + j:5 + j, :] += jnp.sum(back * v, axis=0, keepdims=True)
            out_ref[...] = dv.astype(BF16)
            part_ref[3:4, :] += jnp.sum(dxb, axis=0, keepdims=True)

    full = lambda shape: pl.BlockSpec(shape, lambda i: (0,) * len(shape))
    kind = lambda i: (jnp.where(i == nt, 1, 0), 0, 0, 0)
    in_specs = [pl.BlockSpec((t, w), lambda i: (tile(i), 0)),
                pl.BlockSpec((t, w), lambda i: (jnp.minimum(tile(i), nt - 1), 0)),
                pl.BlockSpec((t, w), lambda i: (tile(i), 0)),
                pl.BlockSpec((8, w), lambda i: (halo(i), 0)),
                full(wg.shape), full(lv.shape)]
    args = [xb, dhs, hs, hs, wg, lv]
    more_out, more_spec, more_scratch = (), (), []
    if last:
        p, wcb, back_m, dxb_other, dp, g_wout = conv
        in_specs += [pl.BlockSpec((t, w), lambda i: (tile(i), 4)), full(wcb.shape),
                     pl.BlockSpec((None, 4, t, t), kind), pl.BlockSpec((t, w), lambda i: (tile(i), 0)), ANY, ANY]
        args += [p, wcb, back_m, dxb_other, dp, g_wout]
        out0 = jax.ShapeDtypeStruct(dp.shape, dp.dtype)
        spec0 = pl.BlockSpec((t, w), lambda i: (tile(i), 4))
        more_out, more_spec = (jax.ShapeDtypeStruct(g_wout.shape, g_wout.dtype),), (ANY,)
        more_scratch = [pltpu.SemaphoreType.DMA((NDEV,)), pltpu.SemaphoreType.DMA((NDEV,))]
        aliases = {10: 0}
    else:
        out0 = jax.ShapeDtypeStruct((la, w), F32)
        spec0 = pl.BlockSpec((t, w), lambda i: (tile(i), 0))
        aliases = {}
    return _call(
        body, name="lru_backward_%d" % direction,
        grid=(nt + 1,),
        out_shape=(out0, jax.ShapeDtypeStruct((ng, gc, 2 * gc), F32), jax.ShapeDtypeStruct((8, w), F32)) + more_out,
        in_specs=in_specs,
        out_specs=(spec0, full((ng, gc, 2 * gc)), full((8, w))) + more_spec,
        scratch_shapes=[pltpu.VMEM((t, w), F32), pltpu.VMEM((t, w), F32), pltpu.VMEM((t, w), F32),
                        pltpu.VMEM((8, w), F32)] + more_scratch,
        input_output_aliases=aliases,
        compiler_params=_params(("arbitrary",)),
    )(*args)


def _weight_grad_t(a, b, nblk_m, nblk_n, tk, name):
    k, m = a.shape
    n = b.shape[1]
    bm, bn = m // nblk_m, n // nblk_n
    nk = k // tk

    def body(a_ref, b_ref, o_ref, acc):
        kk = pl.program_id(2)

        @pl.when(kk == 0)
        def _():
            acc[...] = jnp.zeros_like(acc)

        acc[...] += lax.dot_general(a_ref[...], b_ref[...], (((0,), (0,)), ((), ())), preferred_element_type=F32)

        @pl.when(kk == nk - 1)
        def _():
            o_ref[...] = acc[...].astype(BF16)

    return _call(
        body, name=name,
        grid=(nblk_m, nblk_n, nk),
        out_shape=jax.ShapeDtypeStruct((nblk_m * nblk_n, bm, bn), BF16),
        in_specs=[pl.BlockSpec((tk, bm), lambda i, j, kk: (kk, i)),
                  pl.BlockSpec((tk, bn), lambda i, j, kk: (kk, j))],
        out_specs=pl.BlockSpec((None, bm, bn), lambda i, j, kk: (i * nblk_n + j, 0, 0)),
        scratch_shapes=[pltpu.VMEM((bm, bn), F32)],
        compiler_params=_params(("arbitrary", "arbitrary", "arbitrary")),
    )(a, b)


def _weight_grad_scatter(at, b, lru_parts, packed, w_ada, dsilu_cctx, tk, name):
    k, m = at.shape
    n = b.shape[1]
    bn = n // NDEV
    nk = k // tk
    rl = lru_parts.shape[1]
    rp = packed.shape[0]
    d, cols = w_ada.shape
    assert cols % 128 == 0
    cb = cols // 128
    where = jnp.stack([_idx(_my_pos()), lax.axis_index("c")]).astype(jnp.int32)
    tn = (((0,), (0,)), ((), ()))

    def body(w_ref, a_ref, b_ref, l_ref, p_ref, wa_ref, ds_ref, recv_ref, lru_ref, psum_ref, pall_ref, cctx_ref,
             acc, sbuf, sib, lbuf, lsum, lall, pall, psum, wada, cpart, call, sib_send, sib_recv, chip_send,
             chip_recv, keep_sem, l_send, l_recv, g_send, g_recv, p_send, p_recv, wa_sem, c_send, c_recv):
        s, kk = pl.program_id(0), pl.program_id(1)
        x, y, c = _my_pos()
        scattered = _scatter_copies(l_ref, lbuf, l_send, l_recv)
        gathered, arrivals = _gather_copies(lsum, lall, g_send, g_recv)
        packed_sends, packed_arrivals = _gather_copies(p_ref, pall, p_send, p_recv)
        fetch_w_ada = pltpu.make_async_copy(wa_ref, wada, wa_sem)

        @pl.when((s == 0) & (kk == 0))
        def _():
            for cp in scattered + packed_sends + [fetch_w_ada]:
                cp.start()

        @pl.when((s == NDEV // 2) & (kk == 0))
        def _():
            _exchange_wait(scattered, scattered[1:])
            red = lbuf[0]
            for j in range(1, NDEV):
                red = red + lbuf[j]
            lsum[...] = red
            for cp in gathered:
                cp.start()
            _exchange_wait(packed_sends, packed_arrivals)
            total = pall[0]
            for j in range(1, NDEV):
                total = total + pall[j]
            psum[...] = total
            fetch_w_ada.wait()
            me = _idx(_my_pos())
            part = jnp.zeros((8, d), F32)
            for q in range(cb):
                dm = jnp.broadcast_to(psum[pl.ds((NDEV + me) * cb + q, 1), :], (8, 128))
                part = part + lax.dot_general(dm, wada[:, q * 128:(q + 1) * 128],
                                              (((1,), (1,)), ((), ())), precision=HIGHEST,
                                              preferred_element_type=F32)
            cpart[...] = part
            _exchange_start(cpart, call, c_send, c_recv, 0)

        @pl.when(kk == 0)
        def _():
            acc[...] = lax.dot_general(a_ref[...], b_ref[...], tn, preferred_element_type=F32)

        @pl.when(kk > 0)
        def _():
            acc[...] += lax.dot_general(a_ref[...], b_ref[...], tn, preferred_element_type=F32)

        def to_sibling(j):
            return pltpu.make_async_remote_copy(
                src_ref=sbuf.at[0], dst_ref=sib.at[j], send_sem=sib_send.at[j], recv_sem=sib_recv.at[j],
                device_id=(x, y, 1 - c), device_id_type=MESH)

        def to_chip(j):
            dist = _chip_order(j, c)
            return pltpu.make_async_remote_copy(
                src_ref=sbuf.at[1], dst_ref=recv_ref.at[dist // 2], send_sem=chip_send.at[j],
                recv_sem=chip_recv.at[dist // 2], device_id=_peer_at(dist), device_id_type=MESH)

        keep = pltpu.make_async_copy(sbuf.at[1], recv_ref.at[0], keep_sem)
        sends = []
        for j in range(4):
            sends += [to_sibling(j), to_chip(j) if j < 3 else keep]

        for st in range(NDEV):
            @pl.when((kk == nk - 1) & (s == st))
            def _(st=st):
                if st >= 2:
                    sends[st - 2].wait_send()
                part = acc[...]
                if st % 2 == 1:
                    to_sibling(st // 2).wait_recv()
                    part = part + sib[st // 2].astype(F32)
                sbuf[st % 2] = part.astype(BF16)
                sends[st].start()
                if st == NDEV - 1:
                    sends[st - 1].wait_send()
                    sends[st].wait()
                    for j in range(1, 4):
                        pltpu.make_async_remote_copy(
                            src_ref=sbuf.at[0], dst_ref=recv_ref.at[j], send_sem=chip_send.at[0],
                            recv_sem=chip_recv.at[j], device_id=_peer_at(2 * j), device_id_type=MESH).wait_recv()
                    _exchange_wait(gathered, arrivals)
                    lru_ref[...] = lall[...]
                    psum_ref[...] = psum[...]
                    pall_ref[...] = pall[...]
                    _exchange_finish(_exchange_pushes(cpart, call, c_send, c_recv, 0))
                    tot = call[0]
                    for j in range(1, NDEV):
                        tot = tot + call[j]
                    cctx_ref[...] = tot * ds_ref[...]

    blk = lambda s, w_ref: w_ref[0] ^ _scatter_order(s, w_ref[1])
    return _call(
        body, name=name,
        grid_spec=pltpu.PrefetchScalarGridSpec(
            num_scalar_prefetch=1, grid=(NDEV, nk),
            in_specs=[pl.BlockSpec((tk, m), lambda s, kk, w_ref: (kk, 0)),
                      pl.BlockSpec((tk, bn), lambda s, kk, w_ref: (kk, blk(s, w_ref))), ANY, ANY, ANY,
                      pl.BlockSpec((8, d), lambda s, kk, w_ref: (0, 0))],
            out_specs=(ANY, pl.BlockSpec((NDEV, rl, 128), lambda s, kk, w_ref: (0, 0, 0)),
                       pl.BlockSpec((rp, 128), lambda s, kk, w_ref: (0, 0)),
                       pl.BlockSpec((NDEV, rp, 128), lambda s, kk, w_ref: (0, 0, 0)),
                       pl.BlockSpec((8, d), lambda s, kk, w_ref: (0, 0))),
            scratch_shapes=[pltpu.VMEM((m, bn), F32), pltpu.VMEM((2, m, bn), BF16), pltpu.VMEM((4, m, bn), BF16),
                            pltpu.VMEM((NDEV, rl, 128), F32), pltpu.VMEM((rl, 128), F32),
                            pltpu.VMEM((NDEV, rl, 128), F32), pltpu.VMEM((NDEV, rp, 128), F32),
                            pltpu.VMEM((rp, 128), F32), pltpu.VMEM((d, cols), F32),
                            pltpu.VMEM((8, d), F32), pltpu.VMEM((NDEV, 8, d), F32),
                            pltpu.SemaphoreType.DMA((4,)), pltpu.SemaphoreType.DMA((4,)),
                            pltpu.SemaphoreType.DMA((4,)), pltpu.SemaphoreType.DMA((4,)),
                            pltpu.SemaphoreType.DMA,
                            pltpu.SemaphoreType.DMA((NDEV,)), pltpu.SemaphoreType.DMA((NDEV,)),
                            pltpu.SemaphoreType.DMA((NDEV,)), pltpu.SemaphoreType.DMA((NDEV,)),
                            pltpu.SemaphoreType.DMA((NDEV,)), pltpu.SemaphoreType.DMA((NDEV,)),
                            pltpu.SemaphoreType.DMA,
                            pltpu.SemaphoreType.DMA((NDEV - 1,)), pltpu.SemaphoreType.DMA((NDEV - 1,))]),
        out_shape=(jax.ShapeDtypeStruct((4, m, bn), BF16), jax.ShapeDtypeStruct((NDEV, rl, 128), F32),
                   jax.ShapeDtypeStruct((rp, 128), F32), jax.ShapeDtypeStruct((NDEV, rp, 128), F32),
                   jax.ShapeDtypeStruct((8, d), F32)),
        compiler_params=_params(("arbitrary", "arbitrary")),
    )(where, at, b, lru_parts, packed, w_ada, dsilu_cctx)


def _input_backward(dp, w_all, src, mv, row0, tm, nbk, name, dn=None, cols=None):
    rows, d = src.shape
    nb, _, bw = w_all.shape
    first, last = (0, nb * bw - 1) if cols is None else cols
    k0 = first // (nbk * bw)
    nk = last // (nbk * bw) - k0 + 1
    ni = rows // tm
    blk0 = row0 // tm
    latent = dn is not None

    def body(*refs):
        dp_ref, w_ref, x_ref, mv_ref = refs[:4]
        outs = refs[4 + latent:]
        part_ref, acc = outs[latent], outs[latent + 1]
        i, k = pl.program_id(0), pl.program_id(1)

        def product():
            step = _dot_nt(dp_ref[:, 0:bw], w_ref[0])
            for q in range(1, nbk):
                step = step + _dot_nt(dp_ref[:, q * bw:(q + 1) * bw], w_ref[q])
            return step

        def finish(slot):
            xf = x_ref[...]
            r = lax.rsqrt(jnp.mean(xf * xf, axis=-1, keepdims=True) + EPS)
            xn = xf * r
            dhl = acc[slot]
            gain, sc = mv_ref[0:1, :], mv_ref[1:2, :]
            dhx = jnp.sum(dhl * xn, axis=0, keepdims=True)
            part_ref[0:1, :] += jnp.sum(dhl, axis=0, keepdims=True)
            part_ref[1:2, :] += dhx * gain
            part_ref[2:3, :] += dhx * (1.0 + sc)
            if latent:
                dxn = dhl * (gain * (1.0 + sc))
                outs[0][...] = (refs[4][...].astype(F32)
                                + r * (dxn - xn * jnp.mean(dxn * xn, axis=-1, keepdims=True)))

        @pl.when((i == 0) & (k == 0))
        def _():
            part_ref[...] = jnp.zeros_like(part_ref)
            acc[0] = product()

        @pl.when((i > 0) & (i < ni) & (k == 0))
        def _():
            acc[i % 2] = product()
            finish((i - 1) % 2)

        @pl.when((i == ni) & (k == 0))
        def _():
            finish((ni - 1) % 2)

        @pl.when((i < ni) & (k > 0))
        def _():
            acc[i % 2] += product()

    tile = pl.BlockSpec((tm, d), lambda i, k: (jnp.maximum(i - 1, 0), 0))
    vec = pl.BlockSpec((8, d), lambda i, k: (0, 0))
    kblock = lambda i, k: k0 + jnp.where(i == ni, nk - 1, k)
    return _call(
        body, name=name,
        grid=(ni + 1, nk),
        out_shape=((jax.ShapeDtypeStruct((rows, d), F32),) if latent else ()) + (jax.ShapeDtypeStruct((8, d), F32),),
        in_specs=[pl.BlockSpec((tm, nbk * bw), lambda i, k: (blk0 + jnp.minimum(i, ni - 1), kblock(i, k))),
                  pl.BlockSpec((nbk, d, bw), lambda i, k: (kblock(i, k), 0, 0)), tile, vec]
                 + ([tile] if latent else []),
        out_specs=((tile,) if latent else ()) + (vec,),
        scratch_shapes=[pltpu.VMEM((2, tm, d), F32)],
        compiler_params=_params(("arbitrary", "arbitrary")),
    )(*([dp, w_all, src, mv] + ([dn] if latent else [])))


def _adamw_scattered(parts, w, m, v, tr):
    r, c = w.shape
    nslot = parts.shape[0]

    def body(p_ref, w_ref, m_ref, v_ref, g_ref, d_ref, m2_ref, v2_ref):
        g = p_ref[0].astype(F32)
        for k in range(1, nslot):
            g = g + p_ref[k].astype(F32)
        g_ref[...] = g
        d_ref[...], m2_ref[...], v2_ref[...] = _adamw(w_ref[...], g, m_ref[...], v_ref[...])

    tile = pl.BlockSpec((tr, c), lambda i: (i, 0))
    return _call(
        body, name="adamw_scattered_%dx%d" % (r, c),
        grid=(r // tr,),
        out_shape=tuple(jax.ShapeDtypeStruct((r, c), F32) for _ in range(4)),
        in_specs=[pl.BlockSpec((nslot, tr, c), lambda i: (0, i, 0)), tile, tile, tile],
        out_specs=(tile,) * 4,
        compiler_params=_params(("arbitrary",)),
    )(parts, w, m, v)


def _adamw_ada(st, dmod, w, m, v, tr):
    r, c = w.shape

    def body(s_ref, dm_ref, w_ref, m_ref, v_ref, g_ref, d_ref, m2_ref, v2_ref):
        g = jnp.dot(s_ref[...], dm_ref[...], precision=HIGHEST, preferred_element_type=F32)
        g_ref[...] = g
        d_ref[...], m2_ref[...], v2_ref[...] = _adamw(w_ref[...], g, m_ref[...], v_ref[...])

    tile = pl.BlockSpec((tr, c), lambda i: (i, 0))
    return _call(
        body, name="adamw_ada",
        grid=(r // tr,),
        out_shape=tuple(jax.ShapeDtypeStruct((r, c), F32) for _ in range(4)),
        in_specs=[pl.BlockSpec((tr, 16), lambda i: (i, 0)), pl.BlockSpec((16, c), lambda i: (0, 0)),
                  tile, tile, tile],
        out_specs=(tile,) * 4,
        compiler_params=_params(("arbitrary",)),
    )(st, dmod, w, m, v)


def _adamw_small(gs, ws, ms, vs):
    n = len(ws)

    def body(*refs):
        for j in range(n):
            g_ref, w_ref, m_ref, v_ref = refs[j], refs[n + j], refs[2 * n + j], refs[3 * n + j]
            d_ref, m2_ref, v2_ref = refs[4 * n + j], refs[5 * n + j], refs[6 * n + j]
            d_ref[...], m2_ref[...], v2_ref[...] = _adamw(w_ref[...], g_ref[...], m_ref[...], v_ref[...])

    shapes = tuple(jax.ShapeDtypeStruct(a.shape, F32) for a in ws)
    out = _call(
        body, name="adamw_small",
        out_shape=shapes * 3,
        in_specs=[VMEM] * (4 * n), out_specs=(VMEM,) * (3 * n),
        compiler_params=_params(),
    )(*gs, *ws, *ms, *vs)
    return list(out[:n]), list(out[n:2 * n]), list(out[2 * n:])


def _blockdiag_groups(wh, gc):
    h, dh, _ = wh.shape
    g = gc // dh
    w4 = wh.reshape(h // g, g, dh, dh)
    bd = jnp.einsum("ngij,gh->ngihj", w4, jnp.eye(g, dtype=wh.dtype))
    return bd.reshape(h // g, gc, gc)


def _blockdiag_extract(bd, dh):
    ng, gc, _ = bd.shape
    g = gc // dh
    x = bd.reshape(ng, g, dh, g, dh)
    return jnp.einsum("ngihj,gh->ngij", x, jnp.eye(g, dtype=bd.dtype)).reshape(ng * g, dh, dh)


def _largest_tile(n, cap):
    return max(q for q in range(128, min(n, cap) + 1, 128) if n % q == 0)


def _rows8(*vecs):
    rows = [jnp.reshape(v, (1, -1)).astype(F32) for v in vecs]
    n = rows[0].shape[1]
    return jnp.concatenate(rows + [jnp.zeros((8 - len(rows), n), F32)], axis=0)


def _pack(pieces):
    flat = jnp.concatenate([jnp.reshape(a, (-1,)).astype(F32) for a in pieces])
    total = -(-flat.shape[0] // 1024) * 1024
    return jnp.pad(flat, (0, total - flat.shape[0])).reshape(total // 128, 128)


def _unpack(packed, shapes):
    flat = packed.reshape(-1)
    out, off = [], 0
    for s in shapes:
        n = 1
        for q in s:
            n *= q
        out.append(flat[off:off + n].reshape(s))
        off += n
    return out


def kernel(x, c, ctx, c_ctx, norm_g, w_ada, b_ada, w_in, w_conv_a, w_conv_b, b_conv_b, lru_wa, lru_ba, lru_wx, lru_bx, lru_lambda, w_out, final_g, loss_target, m_c_ctx, m_norm_g, m_w_ada, m_b_ada, m_w_in, m_w_conv_a, m_w_conv_b, m_b_conv_b, m_lru_wa, m_lru_ba, m_lru_wx, m_lru_bx, m_lru_lambda, m_w_out, m_final_g, v_c_ctx, v_norm_g, v_w_ada, v_b_ada, v_w_in, v_w_conv_a, v_w_conv_b, v_b_conv_b, v_lru_wa, v_lru_ba, v_lru_wx, v_lru_bx, v_lru_lambda, v_w_out, v_final_g):
    _, l, d = x.shape
    lc = ctx.shape[1]
    w = d // 2
    t = lc
    assert l % t == 0 and t % GRID_W == 0 and t % 128 == 0
    dh = w // N_HEADS
    gc = min(w, MXU_WIDTH)
    cols = w_ada.shape[2]
    wo_rows = w_out.shape[1]
    me = _idx(_my_pos())
    x2, ctx2, tgt2 = x[0], ctx[0], loss_target[0]
    w_ada2, w_in2, w_out2 = w_ada[0], w_in[0], w_out[0]

    small_mine = jnp.concatenate([a.reshape(-1) for a in (w_conv_a, w_conv_b, lru_ba, lru_bx, lru_lambda)]
                                 + [jnp.zeros((3 * (w // NDEV),), F32)]).reshape(16, w // NDEV)
    mod_all, s_mat, small_all = _mod_forward(
        jnp.broadcast_to(c, (8, d)), jnp.broadcast_to(c_ctx[None], (8, d)), w_ada2, small_mine)
    mod = jnp.transpose(mod_all, (1, 0, 2)).reshape(16, NDEV * cols) + b_ada
    mod_lat = lax.dynamic_slice_in_dim(mod, me, 1, axis=0)
    sh_l, sc_l, gt_l = jnp.split(mod_lat, 3, axis=-1)
    sh_c, sc_c, _ = jnp.split(mod[8:9], 3, axis=-1)
    small = jnp.transpose(small_all, (1, 0, 2)).reshape(16, w)
    wca = _rows8(*[small[j] for j in range(0, 3)])
    wcb = _rows8(*[small[j] for j in range(3, 7)], b_conv_b)
    lv = _rows8(0.5 * small[7], 0.5 * small[9], small[11], 0.5 * small[8], 0.5 * small[10], small[12])
    wg = jnp.stack([
        jnp.concatenate([_blockdiag_groups(lru_wa[0, dr], gc), _blockdiag_groups(lru_wx[0, dr], gc)], axis=-1)
        for dr in range(2)])
    wg = (0.5 * wg).astype(BF16)

    la = l + lc
    tm = 2 * t if l % (2 * t) == 0 else t
    tk = 3 * t if la % (3 * t) == 0 else t
    h = _normalize(x2, _rows8(norm_g, sc_l, sh_l), la, 0, tm, "normalize")
    h = _normalize(ctx2, _rows8(norm_g, sc_c, sh_c), la, l, t, "normalize_ctx", prev=h)
    p, w_all, wo_all = _in_projection(h, w_in2.astype(BF16), w_out2.astype(BF16), la // 4 if la % 64 == 0 else tk)
    taps_m, back_m, perm = _scan_matrices(t)
    xb = _conv_input(p, wcb, taps_m, l, t)
    hf, hr = _lru_forward(xb, wg, lv, l, t)
    wo = wo_all.reshape(d, d)
    dn, cat, dout, part_mix = _mix_forward(x2, tgt2, p, hf, hr, wo, _rows8(gt_l, final_g), wca, perm, t)
    g_wout = _weight_grad_t(cat, dout, 2, 1, _largest_tile(l, 2048), "grad_w_out")
    dp, dhs, part_ca = _mix_backward(dout, p, hf, hr, wo, wca, perm, l, t)
    dxb0, dwg0, part_l0 = _lru_backward(0, xb, dhs, hf, wg, lv, l, t)
    dp, dwg1, part_l1, sc_wout = _lru_backward(
        1, xb, dhs, hr, wg, lv, l, t, conv=(p, wcb, back_m, dxb0, dp, g_wout.reshape(NDEV, wo_rows, d)))
    dwa = jnp.stack([_blockdiag_extract(dwg0[:, :, :gc], dh), _blockdiag_extract(dwg1[:, :, :gc], dh)])
    dwx = jnp.stack([_blockdiag_extract(dwg0[:, :, gc:], dh), _blockdiag_extract(dwg1[:, :, gc:], dh)])
    lru_part = (0.5 * jnp.stack([dwa, dwx])).reshape(NDEV, -1, 128)
    grad_x, part_lat = _input_backward(dp, w_all, x2, _rows8(norm_g, sc_l), 0, tm, 2, "input_backward", dn=dn)
    (part_ctx,) = _input_backward(dp, w_all, ctx2, _rows8(norm_g, sc_c), l, t, 2, "input_backward_ctx",
                                  cols=(4 * w, 5 * w - 1))
    part_in = jnp.concatenate([part_lat[0:2], part_ctx[0:2], (part_lat[2] + part_ctx[2])[None]], axis=0)

    zeros_d = jnp.zeros((d,), F32)
    pieces = [
        jnp.concatenate([part_in[0], part_in[1], part_mix[1]]),
        jnp.concatenate([part_in[2], part_in[3], zeros_d]),
        part_in[4], part_mix[0], part_ca[0:3], part_l1[4:8], part_l1[3],
        0.5 * jnp.stack([part_l0[0], part_l1[0]]), 0.5 * jnp.stack([part_l0[1], part_l1[1]]),
        jnp.stack([part_l0[2], part_l1[2]]), part_mix[2, 0:1],
    ]
    shapes = [(3 * d,), (3 * d,), (d,), (d,), (3, w), (4, w), (w,), (2, w), (2, w), (2, w), (1,)]
    sig_cc = jax.nn.sigmoid(c_ctx)
    dsilu_cc = jnp.broadcast_to((sig_cc * (1.0 + c_ctx * (1.0 - sig_cc)))[None], (8, d))
    sc_win, lru_sum, psum, pall, g_cctx8 = _weight_grad_scatter(
        h, dp, lru_part, _pack(pieces), w_ada2, dsilu_cc, _largest_tile(la, 1536), "grad_w_in")
    (g_modl, g_modc, g_norm, g_final, g_ca, g_cb, g_bcb, g_ba, g_bx, g_lam, loss1) = _unpack(psum, shapes)
    loss = loss1[0]
    g_cctx = g_cctx8[0]
    g_bada = (g_modl + g_modc)[None]
    g_lru = lru_sum.reshape(2, 2, N_HEADS, dh, dh)
    g_wa, g_wx = g_lru[0][None], g_lru[1][None]
    wsl = w // NDEV
    mine = lambda a: lax.dynamic_slice_in_dim(a, me * wsl, wsl, axis=-1)
    g_ca_m, g_cb_m, g_ba_m, g_bx_m, g_lam_m = (mine(g_ca)[None], mine(g_cb)[None], mine(g_ba)[None],
                                               mine(g_bx)[None], mine(g_lam)[None])
    g_norm, g_bcb = g_norm[None], g_bcb[None]

    per_dev = pall[:, :3 * d // 128].reshape(NDEV, NDEV, cols)
    dmod_lat = lax.dynamic_slice_in_dim(per_dev, me, 1, axis=1)[:, 0]
    dmod_ctx = lax.dynamic_slice_in_dim(g_modc.reshape(NDEV, cols), me, 1, axis=0)
    dmod16 = jnp.concatenate([dmod_lat, dmod_ctx, jnp.zeros((7, cols), F32)], axis=0)
    tr_ada = 256 if d % 256 == 0 else d
    g_wada, d_wada, m_wada, v_wada = _adamw_ada(s_mat.T, dmod16, w_ada2, m_w_ada[0], v_w_ada[0], tr_ada)
    g_win2, d_win, m_win, v_win = _adamw_scattered(sc_win, w_in2, m_w_in[0], v_w_in[0], tr_ada)
    tr_out = 64 if wo_rows % 64 == 0 else wo_rows
    g_wout2, d_wout, m_wout, v_wout = _adamw_scattered(sc_wout, w_out2, m_w_out[0], v_w_out[0], tr_out)

    small_w = [c_ctx, norm_g, b_ada, w_conv_a, w_conv_b, b_conv_b, lru_wa, lru_ba, lru_wx, lru_bx, lru_lambda, final_g]
    small_m = [m_c_ctx, m_norm_g, m_b_ada, m_w_conv_a, m_w_conv_b, m_b_conv_b, m_lru_wa, m_lru_ba, m_lru_wx,
               m_lru_bx, m_lru_lambda, m_final_g]
    small_v = [v_c_ctx, v_norm_g, v_b_ada, v_w_conv_a, v_w_conv_b, v_b_conv_b, v_lru_wa, v_lru_ba, v_lru_wx,
               v_lru_bx, v_lru_lambda, v_final_g]
    small_g = [g_cctx, g_norm, g_bada, g_ca_m, g_cb_m, g_bcb, g_wa, g_ba_m, g_wx, g_bx_m, g_lam_m, g_final]
    small_g = [jnp.reshape(a, b.shape) for a, b in zip(small_g, small_w)]
    d_s, m_s, v_s = _adamw_small(small_g, small_w, small_m, small_v)

    def weights(small_list, ada, win, wout):
        (cctx_, norm_, bada_, ca_, cb_, bcb_, wa_, ba_, wx_, bx_, lam_, final_) = small_list
        return [cctx_, norm_, ada[None], bada_, win[None], ca_, cb_, bcb_, wa_, ba_, wx_, bx_, lam_, wout[None], final_]

    return (loss, grad_x[None],
            *weights(small_g, g_wada, g_win2, g_wout2), *weights(d_s, d_wada, d_win, d_wout),
            *weights(m_s, m_wada, m_win, m_wout), *weights(v_s, v_wada, v_win, v_wout))
```

```python
import functools

import jax
import jax.numpy as jnp
import numpy as np
from jax import lax
from jax.experimental import pallas as pl
from jax.experimental.pallas import tpu as pltpu

F32 = jnp.float32
BF16 = jnp.bfloat16
MESH = pl.DeviceIdType.MESH
NDEV = 8
GRID_W = 64
N_HEADS = 16
LRU_C = 8.0
EPS = 1e-6
MXU_WIDTH = 256
VMEM_LIMIT = 60 * 1024 * 1024

ADAM_LR = 0.001
ADAM_B1 = 0.9
ADAM_B2 = 0.999
ADAM_EPS = 1e-08
ADAM_WD = 0.01
ADAM_STEP = 10
ADAM_C1 = 1.0 - ADAM_B1 ** ADAM_STEP
ADAM_C2 = 1.0 - ADAM_B2 ** ADAM_STEP

HIGHEST = lax.Precision.HIGHEST
ANY = pl.BlockSpec(memory_space=pl.ANY)
VMEM = pl.BlockSpec(memory_space=pltpu.VMEM)


def _call(body, **kw):
    return pl.pallas_call(body, **kw)


def _params(sem=None, vmem=VMEM_LIMIT):
    return pltpu.CompilerParams(dimension_semantics=sem, vmem_limit_bytes=vmem)


def _my_pos():
    return lax.axis_index("x"), lax.axis_index("y"), lax.axis_index("c")


def _idx(pos):
    return 4 * pos[0] + 2 * pos[1] + pos[2]


def _peer(k):
    x, y, c = _my_pos()
    return ((1 - x) if (k >> 2) & 1 else x, (1 - y) if (k >> 1) & 1 else y, (1 - c) if k & 1 else c)


def _exchange_pushes(src_ref, dst_ref, send_sems, recv_sems, base):
    me = _idx(_my_pos())
    sends = [pltpu.make_async_remote_copy(
        src_ref=src_ref, dst_ref=dst_ref.at[me], send_sem=send_sems.at[base + k - 1],
        recv_sem=recv_sems.at[base + k - 1], device_id=_peer(k), device_id_type=MESH) for k in range(1, NDEV)]
    return sends, (src_ref, dst_ref, send_sems, recv_sems, base)


def _exchange_start(src_ref, dst_ref, send_sems, recv_sems, base):
    started = _exchange_pushes(src_ref, dst_ref, send_sems, recv_sems, base)
    for cp in started[0]:
        cp.start()
    dst_ref[_idx(_my_pos())] = src_ref[...]
    return started


def _exchange_finish(started):
    sends, (src_ref, dst_ref, send_sems, recv_sems, base) = started
    for k in range(1, NDEV):
        peer = _peer(k)
        pltpu.make_async_remote_copy(
            src_ref=src_ref, dst_ref=dst_ref.at[_idx(peer)], send_sem=send_sems.at[base + k - 1],
            recv_sem=recv_sems.at[base + k - 1], device_id=peer, device_id_type=MESH).wait_recv()
    for cp in sends:
        cp.wait_send()


def _exchange_vmem(src_ref, dst_ref, send_sems, recv_sems, base):
    _exchange_finish(_exchange_start(src_ref, dst_ref, send_sems, recv_sems, base))


def _sigmoid(z):
    return 0.5 * jnp.tanh(0.5 * z) + 0.5


def _softplus(x):
    return jnp.maximum(x, 0.0) + jnp.log1p(jnp.exp(-jnp.abs(x)))


def _one_minus_sq(a, la):
    series = (-2.0 * la) * (1.0 + la)
    return jnp.where(la > -0.0015, series, 1.0 - a * a)


def _dot(a, b):
    return jnp.dot(a, b, preferred_element_type=F32)


def _dot_nt(a, b):
    return lax.dot_general(a, b, (((1,), (1,)), ((), ())), preferred_element_type=F32)


def _rows(shape):
    return lax.broadcasted_iota(jnp.int32, shape, 0)


def _scan_matrices(t):
    seg = t // 8
    r = np.arange(t)
    perm = (np.arange(t)[None, :] == ((r % 8) * seg + r // 8)[:, None]).astype(np.float32)
    rows, cols = r[:, None], r[None, :]
    taps, back = [], []
    for rowlen in (GRID_W, t):
        pos = rows % rowlen
        shift = {-2: (cols == rows - 2) & (pos >= 2), -1: (cols == rows - 1) & (pos >= 1),
                 0: cols == rows, 1: (cols == rows + 1) & (pos + 1 < rowlen),
                 2: (cols == rows + 2) & (pos + 2 < rowlen)}
        if rowlen == GRID_W:
            beside = [shift[-1].astype(np.float32), shift[1].astype(np.float32)]
        taps.append(np.stack([perm @ shift[k].astype(np.float32) for k in (-2, -1, 0, 1)]))
        back.append(np.stack([shift[k].astype(np.float32) @ perm.T for k in (2, 1, 0, -1)]))
    as_bf16 = lambda a: jnp.asarray(a, dtype=BF16)
    return as_bf16(np.stack(taps)), as_bf16(np.stack(back)), as_bf16(np.stack([perm, perm.T] + beside))


def _chunk_scan(a, b, reverse):
    row = _rows(a.shape)
    for s in (1, 2, 4):
        if reverse:
            m = row < 8 - s
            sh = 8 - s
        else:
            m = row >= s
            sh = s
        a_s = jnp.where(m, pltpu.roll(a, sh, 0), 1.0)
        b_s = jnp.where(m, pltpu.roll(b, sh, 0), 0.0)
        b = b + a * b_s
        a = a * a_s
    return a, b


def _chain_segments(ptot, hend, carry, reverse):
    ca, cb = _chunk_scan(ptot, hend, reverse)
    incl = ca * carry + cb
    r8 = _rows(incl.shape)
    if reverse:
        start = jnp.where(r8 < 7, pltpu.roll(incl, 7, 0), carry)
        last = incl[0:1, :]
    else:
        start = jnp.where(r8 >= 1, pltpu.roll(incl, 1, 0), carry)
        last = incl[7:8, :]
    return start, jnp.broadcast_to(last, incl.shape)


def _blocks(nblock, reverse):
    order = range(nblock - 1, -1, -1) if reverse else range(nblock)
    return [slice(8 * k, 8 * k + 8) for k in order]


def _scan_tile(a_ref, b_ref, out_ref, carry, reverse):
    t, w = a_ref.shape
    seg = t // 8

    hend, ptot = jnp.zeros((8, w), F32), jnp.ones((8, w), F32)
    for rows in _blocks(seg, reverse):
        a = a_ref[rows, :]
        hend, ptot = a * hend + b_ref[rows, :], a * ptot
    h, new_carry = _chain_segments(ptot, hend, carry, reverse)
    for rows in _blocks(seg, reverse):
        h = a_ref[rows, :] * h + b_ref[rows, :]
        out_ref[rows, :] = h
    return new_carry


def _scan_tile_backward(a_ref, dh_ref, g_ref, carry, reverse):
    t, w = a_ref.shape
    seg = t // 8

    uend, ptot = jnp.zeros((8, w), F32), jnp.ones((8, w), F32)
    for rows in _blocks(seg, reverse):
        a = a_ref[rows, :]
        uend, ptot = a * (dh_ref[rows, :] + uend), a * ptot
    u, new_carry = _chain_segments(ptot, uend, carry, reverse)
    for rows in _blocks(seg, reverse):
        g = dh_ref[rows, :] + u
        g_ref[rows, :] = g
        u = a_ref[rows, :] * g
    return new_carry


def _lru_coef(xb, wg_ref, d, ba, bx, lam, gc):
    w = xb.shape[1]
    xb16 = xb.astype(BF16)
    zr, zi = [], []
    for g in range(w // gc):
        z = _dot(xb16[:, g * gc:(g + 1) * gc], wg_ref[d, g])
        zr.append(z[:, :gc])
        zi.append(z[:, gc:])
    zr = zr[0] if len(zr) == 1 else jnp.concatenate(zr, axis=-1)
    zi = zi[0] if len(zi) == 1 else jnp.concatenate(zi, axis=-1)
    tr = jnp.tanh(zr + ba)
    ti = jnp.tanh(zi + bx)
    sp = _softplus(-lam)
    half = -0.5 * LRU_C * sp
    la = tr * half + half
    a = jnp.exp(la)
    q = _one_minus_sq(a, la)
    rs = lax.rsqrt(jnp.maximum(q, 1e-30))
    return a, q * rs, rs, tr, ti, sp


def _adamw(w, g, m, v):
    m2 = ADAM_B1 * m + (1.0 - ADAM_B1) * g
    v2 = ADAM_B2 * v + (1.0 - ADAM_B2) * (g * g)
    m_hat = m2 / ADAM_C1
    v_hat = v2 / ADAM_C2
    delta = -ADAM_LR * (m_hat / (jnp.sqrt(v_hat) + ADAM_EPS) + ADAM_WD * w)
    return delta, m2, v2


def _mod_forward(c8, cctx8, w_ada, small):
    d = c8.shape[1]
    cols = w_ada.shape[1]

    def body(c_ref, cctx_ref, w_ref, sm_ref, mod_ref, s_ref, sm_all, cbuf, mod_my, send_sems, recv_sems):
        _exchange_vmem(sm_ref, sm_all, send_sems, recv_sems, 2 * (NDEV - 1))
        _exchange_vmem(c_ref, cbuf, send_sems, recv_sems, 0)
        row = _rows((8, d))
        c_all = jnp.zeros((8, d), F32)
        for b in range(NDEV):
            c_all = jnp.where(row == b, cbuf[b], c_all)
        cc = cctx_ref[...]
        s_top = c_all * _sigmoid(c_all)
        s_bot = jnp.where(row == 0, cc * _sigmoid(cc), 0.0)
        s = jnp.concatenate([s_top, s_bot], axis=0)
        s_ref[...] = s
        mod_my[...] = jnp.dot(s, w_ref[...], precision=HIGHEST, preferred_element_type=F32)
        _exchange_vmem(mod_my, mod_ref, send_sems, recv_sems, NDEV - 1)

    return _call(
        body, name="mod_forward",
        out_shape=(jax.ShapeDtypeStruct((NDEV, 16, cols), F32), jax.ShapeDtypeStruct((16, d), F32),
                   jax.ShapeDtypeStruct((NDEV,) + small.shape, F32)),
        in_specs=[VMEM] * 4, out_specs=(VMEM,) * 3,
        scratch_shapes=[pltpu.VMEM((NDEV, 8, d), F32), pltpu.VMEM((16, cols), F32),
                        pltpu.SemaphoreType.DMA((3 * (NDEV - 1),)), pltpu.SemaphoreType.DMA((3 * (NDEV - 1),))],
        compiler_params=_params(),
    )(c8, cctx8, w_ada, small)


def _scatter_copies(src_ref, dst_ref, send_sems, recv_sems):
    me = _idx(_my_pos())
    copies = [pltpu.make_async_copy(src_ref.at[me], dst_ref.at[0], send_sems.at[0])]
    for k in range(1, NDEV):
        peer = _peer(k)
        copies.append(pltpu.make_async_remote_copy(
            src_ref=src_ref.at[_idx(peer)], dst_ref=dst_ref.at[k], send_sem=send_sems.at[k],
            recv_sem=recv_sems.at[k], device_id=peer, device_id_type=MESH))
    return copies


def _gather_copies(src_ref, dst_ref, send_sems, recv_sems):
    me = _idx(_my_pos())
    sends = [pltpu.make_async_copy(src_ref, dst_ref.at[me], send_sems.at[0])]
    arrivals = []
    for k in range(1, NDEV):
        peer = _peer(k)
        sends.append(pltpu.make_async_remote_copy(
            src_ref=src_ref, dst_ref=dst_ref.at[me], send_sem=send_sems.at[k],
            recv_sem=recv_sems.at[k], device_id=peer, device_id_type=MESH))
        arrivals.append(pltpu.make_async_remote_copy(
            src_ref=src_ref, dst_ref=dst_ref.at[_idx(peer)], send_sem=send_sems.at[k],
            recv_sem=recv_sems.at[k], device_id=peer, device_id_type=MESH))
    return sends, arrivals


def _exchange_wait(sends, arrivals):
    sends[0].wait()
    for cp in arrivals:
        cp.wait_recv()
    for cp in sends[1:]:
        cp.wait_send()


def _chip_order(k, c):
    return (6, 4 - 2 * c, 2 + 2 * c, 0)[k]


def _scatter_order(s, c):
    k = s >> 1
    mine = jnp.where(k == 0, 6, jnp.where(k == 1, 4 - 2 * c, jnp.where(k == 2, 2 + 2 * c, 0)))
    theirs = jnp.where(k == 0, 6, jnp.where(k == 1, 2 + 2 * c, jnp.where(k == 2, 4 - 2 * c, 0))) ^ 1
    return jnp.where((s & 1) == 0, theirs, mine)


def _peer_at(dist):
    x, y, c = _my_pos()
    return (x ^ ((dist >> 2) & 1), y ^ ((dist >> 1) & 1), c ^ (dist & 1))


def _normalize(src, mv, la, row0, tm, name, prev=None):
    rows, d = src.shape
    blk0 = row0 // tm

    def body(*refs):
        x_ref, mv_ref, h_ref = refs[0], refs[1], refs[-1]
        xf = x_ref[...]
        r = lax.rsqrt(jnp.mean(xf * xf, axis=-1, keepdims=True) + EPS)
        h = xf * r * (mv_ref[0:1, :] * (1.0 + mv_ref[1:2, :])) + mv_ref[2:3, :]
        h_ref[...] = h.astype(BF16)

    in_specs = [pl.BlockSpec((tm, d), lambda i: (i, 0)), pl.BlockSpec((8, d), lambda i: (0, 0))]
    args = [src, mv]
    aliases = {}
    if prev is not None:
        in_specs += [ANY]
        args += [prev]
        aliases = {2: 0}
    return _call(
        body, name=name,
        grid=(rows // tm,),
        out_shape=jax.ShapeDtypeStruct((la, d), BF16),
        in_specs=in_specs,
        out_specs=pl.BlockSpec((tm, d), lambda i: (blk0 + i, 0)),
        input_output_aliases=aliases,
        compiler_params=_params(("arbitrary",)),
    )(*args)


def _gather_order(step):
    return (step & 1) | (((step >> 2) & 1) << 1) | (((step >> 1) & 1) << 2)


def _in_projection(h, w_shard, wo_shard, tm):
    la, d = h.shape
    bw = w_shard.shape[1]
    ni = la // tm
    where = jnp.reshape(_idx(_my_pos()), (1,)).astype(jnp.int32)

    def body(me_ref, h_ref, w_ref, wo_ref, p_ref, all_ref, wo_all, wbuf, send_sems, recv_sems, local_sems,
             wo_send, wo_recv):
        s, i = pl.program_id(0), pl.program_id(1)
        x, y, c = _my_pos()
        wo_sends, wo_arrivals = _gather_copies(wo_ref, wo_all, wo_send, wo_recv)

        @pl.when((s == NDEV // 2) & (i == 0))
        def _():
            for cp in wo_sends:
                cp.start()

        me, sibling = (x, y, c), (x, y, 1 - c)
        chips = [(1 - x, y), (x, 1 - y), (1 - x, 1 - y)]

        def copy(k, block, to, from_shard=False):
            return pltpu.make_async_remote_copy(
                src_ref=w_ref if from_shard else all_ref.at[_idx(block)], dst_ref=all_ref.at[_idx(block)],
                send_sem=send_sems.at[k], recv_sem=recv_sems.at[k], device_id=to, device_id_type=MESH)

        def load(block, slot):
            return pltpu.make_async_copy(all_ref.at[_idx(block)], wbuf.at[slot], local_sems.at[1])

        keep = pltpu.make_async_copy(w_ref, all_ref.at[_idx(me)], local_sems.at[0])
        first = [copy(0, me, sibling, True)] + [copy(1 + j, me, (*chip, c), True) for j, chip in enumerate(chips)]
        passed = [copy(4 + j, (*chip, c), sibling) for j, chip in enumerate(chips)]
        steps = [(copy(0, sibling, me), None, sibling)]
        for j, chip in enumerate(chips):
            steps.append((copy(1 + j, (*chip, c), me), passed[j], (*chip, c)))
            steps.append((copy(4 + j, (*chip, 1 - c), me), None, (*chip, 1 - c)))

        @pl.when((s == 0) & (i == 0))
        def _():
            keep.start()
            mine = pltpu.make_async_copy(w_ref, wbuf.at[0], local_sems.at[1])
            mine.start()
            for cp in first:
                cp.start()
            mine.wait()

        for n, (arrival, forward, block) in enumerate(steps, start=1):
            @pl.when((s == n - 1) & (i == ni - 1))
            def _(arrival=arrival, forward=forward, block=block, n=n):
                arrival.wait_recv()
                if forward is not None:
                    forward.start()
                load(block, n % 2).start(priority=1)

        @pl.when((s > 0) & (i == 0))
        def _():
            load(me, s % 2).wait()

        p_ref[...] = _dot(h_ref[...], wbuf[s % 2]).astype(BF16)

        @pl.when((s == NDEV - 1) & (i == ni - 1))
        def _():
            for cp in first + passed:
                cp.wait_send()
            keep.wait()
            _exchange_wait(wo_sends, wo_arrivals)

    return _call(
        body, name="in_projection",
        grid_spec=pltpu.PrefetchScalarGridSpec(
            num_scalar_prefetch=1, grid=(NDEV, ni),
            in_specs=[pl.BlockSpec((tm, d), lambda s, i, me_ref: (i, 0)), ANY, ANY],
            out_specs=(pl.BlockSpec((tm, bw), lambda s, i, me_ref: (i, me_ref[0] ^ _gather_order(s))), ANY, ANY),
            scratch_shapes=[pltpu.VMEM((2, d, bw), BF16), pltpu.SemaphoreType.DMA((7,)),
                            pltpu.SemaphoreType.DMA((7,)), pltpu.SemaphoreType.DMA((2,)),
                            pltpu.SemaphoreType.DMA((NDEV,)), pltpu.SemaphoreType.DMA((NDEV,))]),
        out_shape=(jax.ShapeDtypeStruct((la, NDEV * bw), BF16), jax.ShapeDtypeStruct((NDEV, d, bw), BF16),
                   jax.ShapeDtypeStruct((NDEV,) + wo_shard.shape, wo_shard.dtype)),
        compiler_params=_params(("arbitrary", "arbitrary")),
    )(where, h, w_shard, wo_shard)


def _conv_input(p, wcb, taps_m, l, t):
    la = p.shape[0]
    w = wcb.shape[1]
    nt = l // t

    def body(v_ref, wcb_ref, tm_ref, xb_ref):
        taps = _dot(tm_ref[...].reshape(4 * t, t), v_ref[...])
        xb = wcb_ref[4:5, :] + wcb_ref[0:1, :] * taps[0:t]
        for j in range(1, 4):
            xb = xb + wcb_ref[j:j + 1, :] * taps[j * t:(j + 1) * t]
        xb_ref[...] = xb

    return _call(
        body, name="conv_input",
        grid=(nt + 1,),
        out_shape=jax.ShapeDtypeStruct((la, w), F32),
        in_specs=[pl.BlockSpec((t, w), lambda i: (i, 4)), pl.BlockSpec((8, w), lambda i: (0, 0)),
                  pl.BlockSpec((None, 4, t, t), lambda i: (i // nt, 0, 0, 0))],
        out_specs=pl.BlockSpec((t, w), lambda i: (i, 0)),
        compiler_params=_params(("arbitrary",)),
    )(p, wcb, taps_m)


def _lru_forward(xb, wg, lv, l, t):
    la, w = xb.shape
    gc = wg.shape[2]
    nt = l // t

    def body(xf_ref, xr_ref, wg_ref, lv_ref, hf_ref, hr_ref, a_s, b_s, carry):
        @pl.when(pl.program_id(0) == 0)
        def _():
            carry[...] = jnp.zeros_like(carry)

        for dr, (x_ref, h_ref) in enumerate(((xf_ref, hf_ref), (xr_ref, hr_ref))):
            x = x_ref[...]
            a, s, _, _, ti, _ = _lru_coef(x, wg_ref, dr, lv_ref[3 * dr:3 * dr + 1, :],
                                          lv_ref[3 * dr + 1:3 * dr + 2, :], lv_ref[3 * dr + 2:3 * dr + 3, :], gc)
            a_s[...] = a
            b_s[...] = (s * x) * (0.5 * ti + 0.5)
            carry[dr] = _scan_tile(a_s, b_s, h_ref, carry[dr], dr == 1)

    full = lambda shape: pl.BlockSpec(shape, lambda i: (0,) * len(shape))
    fmap = lambda i: (jnp.where(i == 0, nt, i - 1), 0)
    rmap = lambda i: (jnp.where(i == 0, nt, nt - i), 0)
    return _call(
        body, name="lru_forward",
        grid=(nt + 1,),
        out_shape=(jax.ShapeDtypeStruct((la, w), F32), jax.ShapeDtypeStruct((la, w), F32)),
        in_specs=[pl.BlockSpec((t, w), fmap), pl.BlockSpec((t, w), rmap), full(wg.shape), full(lv.shape)],
        out_specs=(pl.BlockSpec((t, w), fmap), pl.BlockSpec((t, w), rmap)),
        scratch_shapes=[pltpu.VMEM((t, w), F32), pltpu.VMEM((t, w), F32), pltpu.VMEM((2, 8, w), F32)],
        compiler_params=_params(("arbitrary",)),
    )(xb, xb, wg, lv)


def _mix_gates(p_refs, hf_ref, hr_ref, wca_ref, perm_ref, t, w):
    bl, cl, ul, gl, ql = [r[...].astype(F32) for r in p_refs]
    tt = cl * ul
    tt16 = tt.astype(BF16)
    beside = _dot(perm_ref[2:4].reshape(2 * t, t), tt16)
    before, after = beside[:t], beside[t:]
    z = wca_ref[0:1, :] * before + wca_ref[1:2, :] * tt + wca_ref[2:3, :] * after
    sig_g = _sigmoid(gl)
    sig_q = _sigmoid(ql)
    ylru = _dot(perm_ref[1], (hf_ref[...] + hr_ref[...]).astype(BF16))
    return bl, cl, ul, gl, ql, (before, tt, after), z, sig_g, sig_q, ylru


def _p_specs(t, w, nt):
    return [pl.BlockSpec((t, w), functools.partial(lambda i, s: (jnp.minimum(i, nt - 1), s), s=s))
            for s in (0, 1, 2, 3, 5)]


def _mix_forward(x, tgt, p, hf, hr, wo, ov, wca, perm, t):
    l, d = x.shape
    w = d // 2
    nt = l // t

    def body(x_ref, tg_ref, b_ref, c_ref, u_ref, g_ref, q_ref, hf_ref, hr_ref, wo_ref, ov_ref, wca_ref, perm_ref,
             dn_ref, ct_ref, do_ref, part_ref):
        i = pl.program_id(0)
        bl, _, _, gl, ql, _, z, sig_g, sig_q, ylru = _mix_gates(
            (b_ref, c_ref, u_ref, g_ref, q_ref), hf_ref, hr_ref, wca_ref, perm_ref, t, w)
        ya = bl * z * (gl * sig_g)
        yb = ylru * (ql * sig_q)
        ct_ref[:, 0:w] = ya.astype(BF16)
        ct_ref[:, w:] = yb.astype(BF16)
        out = _dot(ya.astype(BF16), wo_ref[0:w, :]) + _dot(yb.astype(BF16), wo_ref[w:, :])
        gate, fg = ov_ref[0:1, :], ov_ref[1:2, :]
        n = x_ref[...] + gate * out
        rr = lax.rsqrt(jnp.mean(n * n, axis=-1, keepdims=True) + EPS)
        nh = n * rr
        e = nh * fg - tg_ref[...]
        loss = 0.5 * jnp.sum(jnp.mean(e * e, axis=-1, keepdims=True), axis=0, keepdims=True)
        dy = e * (1.0 / d)
        dnh = dy * fg
        dn = rr * (dnh - nh * jnp.mean(dnh * nh, axis=-1, keepdims=True))
        dn_ref[...] = dn.astype(BF16)
        do_ref[...] = (dn * gate).astype(BF16)

        @pl.when(i == 0)
        def _():
            part_ref[...] = jnp.zeros_like(part_ref)

        part_ref[0:1, :] += jnp.sum(dy * nh, axis=0, keepdims=True)
        part_ref[1:2, :] += jnp.sum(dn * out, axis=0, keepdims=True)
        part_ref[2:3, :] += jnp.broadcast_to(loss, (1, d))

    tile = lambda cols: pl.BlockSpec((t, cols), lambda i: (i, 0))
    full = lambda shape: pl.BlockSpec(shape, lambda i: (0,) * len(shape))
    return _call(
        body, name="mix_forward",
        grid=(nt,),
        out_shape=(jax.ShapeDtypeStruct((l, d), BF16), jax.ShapeDtypeStruct((l, d), BF16),
                   jax.ShapeDtypeStruct((l, d), BF16), jax.ShapeDtypeStruct((8, d), F32)),
        in_specs=[tile(d), tile(d)] + _p_specs(t, w, nt) + [tile(w), tile(w),
                  pl.BlockSpec((d, d), lambda i: (0, 0), pipeline_mode=pl.Buffered(1)),
                  full(ov.shape), full(wca.shape), full(perm.shape)],
        out_specs=(tile(d), tile(d), tile(d), full((8, d))),
        compiler_params=_params(("arbitrary",)),
    )(x, tgt, p, p, p, p, p, hf, hr, wo, ov, wca, perm)


def _mix_backward(dout, p, hf, hr, wo, wca, perm, l, t):
    d = dout.shape[1]
    w = d // 2
    nt = l // t
    la = p.shape[0]

    def body(do_ref, b_ref, c_ref, u_ref, g_ref, q_ref, hf_ref, hr_ref, wo_ref, wca_ref, perm_ref,
             dp_ref, dh_ref, part_ref):
        i = pl.program_id(0)

        @pl.when(i == 0)
        def _():
            part_ref[...] = jnp.zeros_like(part_ref)

        @pl.when(i == nt)
        def _():
            dp_ref[...] = jnp.zeros_like(dp_ref)

        @pl.when(i < nt)
        def _():
            bl, cl, ul, gl, ql, taps, z, sig_g, sig_q, ylru = _mix_gates(
                (b_ref, c_ref, u_ref, g_ref, q_ref), hf_ref, hr_ref, wca_ref, perm_ref, t, w)
            do = do_ref[...]
            dya = _dot_nt(do, wo_ref[0:w, :])
            dyb = _dot_nt(do, wo_ref[w:, :])
            sg = gl * sig_g
            dz = dya * bl * sg
            dz16 = dz.astype(BF16)
            beside = _dot(perm_ref[2:4].reshape(2 * t, t), dz16)
            dt = wca_ref[0:1, :] * beside[t:] + wca_ref[1:2, :] * dz + wca_ref[2:3, :] * beside[:t]
            dp_ref[:, 0:w] = (dya * z * sg).astype(BF16)
            dp_ref[:, w:2 * w] = (dt * ul).astype(BF16)
            dp_ref[:, 2 * w:3 * w] = (dt * cl).astype(BF16)
            dp_ref[:, 3 * w:4 * w] = (dya * bl * z * (sig_g * (1.0 + gl * (1.0 - sig_g)))).astype(BF16)
            dp_ref[:, 4 * w:5 * w] = jnp.zeros((t, w), BF16)
            dp_ref[:, 5 * w:6 * w] = (dyb * ylru * (sig_q * (1.0 + ql * (1.0 - sig_q)))).astype(BF16)
            dh_ref[...] = _dot(perm_ref[0], (dyb * (ql * sig_q)).astype(BF16)).astype(BF16)
            for j in range(3):
                part_ref[j:j + 1, :] += jnp.sum(dz * taps[j], axis=0, keepdims=True)

    clamp = lambda cols: pl.BlockSpec((t, cols), lambda i: (jnp.minimum(i, nt - 1), 0))
    full = lambda shape: pl.BlockSpec(shape, lambda i: (0,) * len(shape))
    return _call(
        body, name="mix_backward",
        grid=(nt + 1,),
        out_shape=(jax.ShapeDtypeStruct((la, 6 * w), BF16), jax.ShapeDtypeStruct((l, w), BF16),
                   jax.ShapeDtypeStruct((8, w), F32)),
        in_specs=[clamp(d)] + _p_specs(t, w, nt) + [clamp(w), clamp(w),
                  pl.BlockSpec((d, d), lambda i: (0, 0), pipeline_mode=pl.Buffered(1)), full(wca.shape),
                  full(perm.shape)],
        out_specs=(pl.BlockSpec((t, 6 * w), lambda i: (i, 0)), clamp(w), full((8, w))),
        compiler_params=_params(("arbitrary",)),
    )(dout, p, p, p, p, p, hf, hr, wo, wca, perm)


def _lru_backward(direction, xb, dhs, hs, wg, lv, l, t, conv=None):
    la, w = hs.shape
    gc = wg.shape[2]
    ng = w // gc
    nt = l // t
    nblk8 = la // 8
    last = conv is not None
    assert last == (direction == 1)

    if direction == 0:
        tile = lambda i: jnp.where(i == nt, nt, nt - 1 - i)
        halo = lambda i: jnp.where(tile(i) == 0, nblk8 - 1, tile(i) * (t // 8) - 1)
    else:
        tile = lambda i: i
        halo = lambda i: jnp.minimum((i + 1) * (t // 8), nblk8 - 1)

    def body(*refs):
        x_ref, dh_ref, hs_ref, halo_ref, wg_ref, lv_ref = refs[:6]
        i = pl.program_id(0)
        is_ctx = i == nt
        if last:
            v_ref, wcb_ref, bm_ref, dxo_ref, _, gw_ref = refs[6:12]
            out_ref, dwg_ref, part_ref, sc_ref, a_s, dh_s, g_s, carry, send_sems, recv_sems = refs[12:]
            copies = _scatter_copies(gw_ref, sc_ref, send_sems, recv_sems)
        else:
            out_ref, dwg_ref, part_ref, a_s, dh_s, g_s, carry = refs[-7:]
            copies = []

        @pl.when(i == 0)
        def _():
            carry[...] = jnp.zeros_like(carry)
            dwg_ref[...] = jnp.zeros_like(dwg_ref)
            part_ref[...] = jnp.zeros_like(part_ref)
            for cp in copies:
                cp.start()

        if last:
            @pl.when(is_ctx)
            def _():
                _exchange_wait(copies, copies[1:])

        xb = x_ref[...]
        lam = lv_ref[3 * direction + 2:3 * direction + 3, :]
        a, s, rs, tr, ti, sp = _lru_coef(xb, wg_ref, direction, lv_ref[3 * direction:3 * direction + 1, :],
                                         lv_ref[3 * direction + 1:3 * direction + 2, :], lam, gc)
        hs_t = hs_ref[...]
        r8 = _rows((8, w))
        if direction == 0:
            edge = jnp.where(is_ctx, 0.0, halo_ref[7:8, :])
            first = jnp.where(r8 == 0, edge, pltpu.roll(hs_t[t - 8:, :], 1, 0))
            hprev = jnp.concatenate([first, hs_t[:t - 8, :]], axis=0)
        else:
            edge = jnp.where(is_ctx, 0.0, halo_ref[0:1, :])
            final = jnp.where(r8 == 7, edge, pltpu.roll(hs_t[:8, :], 7, 0))
            hprev = jnp.concatenate([hs_t[8:, :], final], axis=0)
        a_s[...] = a
        dh_s[...] = jnp.where(is_ctx, 0.0, dh_ref[...].astype(F32))
        carry[...] = _scan_tile_backward(a_s, dh_s, g_s, carry[...], direction == 0)

        g = g_s[...]
        r = 0.5 * tr + 0.5
        ig = 0.5 * ti + 0.5
        ix = ig * xb
        gs = g * s
        dla = (g * a) * (hprev - ix * (a * rs))
        dxb = gs * ig
        dzr = dla * (r * (1.0 - tr)) * (-LRU_C * sp)
        dzi = gs * ix * (1.0 - ti)
        part_ref[0:1, :] += jnp.sum(dzr, axis=0, keepdims=True)
        part_ref[1:2, :] += jnp.sum(dzi, axis=0, keepdims=True)
        part_ref[2:3, :] += jnp.sum(dla * r, axis=0, keepdims=True) * (LRU_C * _sigmoid(-lam))
        pieces = []
        for gi in range(ng):
            sl = slice(gi * gc, (gi + 1) * gc)
            dz = jnp.concatenate([dzr[:, sl], dzi[:, sl]], axis=-1).astype(BF16)
            pieces.append(_dot_nt(dz, wg_ref[direction, gi]))
            dwg_ref[gi] += _dot(xb[:, sl].T.astype(BF16), dz)
        dxb = dxb + (pieces[0] if ng == 1 else jnp.concatenate(pieces, axis=-1))
        if not last:
            out_ref[...] = dxb
        else:
            dxb = dxb + dxo_ref[...]
            v = v_ref[...].astype(F32)
            backs = _dot(bm_ref[...].reshape(4 * t, t), dxb.astype(BF16))
            dv = jnp.zeros((t, w), F32)
            for j in range(4):
                back = backs[j * t:(j + 1) * t]
                dv = dv + wcb_ref[j:j + 1, :] * back
                part_ref[4 + j:5 + j, :] += jnp.sum(back * v, axis=0, keepdims=True)
            out_ref[...] = dv.astype(BF16)
            part_ref[3:4, :] += jnp.sum(dxb, axis=0, keepdims=True)

    full = lambda shape: pl.BlockSpec(shape, lambda i: (0,) * len(shape))
    kind = lambda i: (jnp.where(i == nt, 1, 0), 0, 0, 0)
    in_specs = [pl.BlockSpec((t, w), lambda i: (tile(i), 0)),
                pl.BlockSpec((t, w), lambda i: (jnp.minimum(tile(i), nt - 1), 0)),
                pl.BlockSpec((t, w), lambda i: (tile(i), 0)),
                pl.BlockSpec((8, w), lambda i: (halo(i), 0)),
                full(wg.shape), full(lv.shape)]
    args = [xb, dhs, hs, hs, wg, lv]
    more_out, more_spec, more_scratch = (), (), []
    if last:
        p, wcb, back_m, dxb_other, dp, g_wout = conv
        in_specs += [pl.BlockSpec((t, w), lambda i: (tile(i), 4)), full(wcb.shape),
                     pl.BlockSpec((None, 4, t, t), kind), pl.BlockSpec((t, w), lambda i: (tile(i), 0)), ANY, ANY]
        args += [p, wcb, back_m, dxb_other, dp, g_wout]
        out0 = jax.ShapeDtypeStruct(dp.shape, dp.dtype)
        spec0 = pl.BlockSpec((t, w), lambda i: (tile(i), 4))
        more_out, more_spec = (jax.ShapeDtypeStruct(g_wout.shape, g_wout.dtype),), (ANY,)
        more_scratch = [pltpu.SemaphoreType.DMA((NDEV,)), pltpu.SemaphoreType.DMA((NDEV,))]
        aliases = {10: 0}
    else:
        out0 = jax.ShapeDtypeStruct((la, w), F32)
        spec0 = pl.BlockSpec((t, w), lambda i: (tile(i), 0))
        aliases = {}
    return _call(
        body, name="lru_backward_%d" % direction,
        grid=(nt + 1,),
        out_shape=(out0, jax.ShapeDtypeStruct((ng, gc, 2 * gc), F32), jax.ShapeDtypeStruct((8, w), F32)) + more_out,
        in_specs=in_specs,
        out_specs=(spec0, full((ng, gc, 2 * gc)), full((8, w))) + more_spec,
        scratch_shapes=[pltpu.VMEM((t, w), F32), pltpu.VMEM((t, w), F32), pltpu.VMEM((t, w), F32),
                        pltpu.VMEM((8, w), F32)] + more_scratch,
        input_output_aliases=aliases,
        compiler_params=_params(("arbitrary",)),
    )(*args)


def _weight_grad_t(a, b, nblk_m, nblk_n, tk, name):
    k, m = a.shape
    n = b.shape[1]
    bm, bn = m // nblk_m, n // nblk_n
    nk = k // tk

    def body(a_ref, b_ref, o_ref, acc):
        kk = pl.program_id(2)

        @pl.when(kk == 0)
        def _():
            acc[...] = jnp.zeros_like(acc)

        acc[...] += lax.dot_general(a_ref[...], b_ref[...], (((0,), (0,)), ((), ())), preferred_element_type=F32)

        @pl.when(kk == nk - 1)
        def _():
            o_ref[...] = acc[...].astype(BF16)

    return _call(
        body, name=name,
        grid=(nblk_m, nblk_n, nk),
        out_shape=jax.ShapeDtypeStruct((nblk_m * nblk_n, bm, bn), BF16),
        in_specs=[pl.BlockSpec((tk, bm), lambda i, j, kk: (kk, i)),
                  pl.BlockSpec((tk, bn), lambda i, j, kk: (kk, j))],
        out_specs=pl.BlockSpec((None, bm, bn), lambda i, j, kk: (i * nblk_n + j, 0, 0)),
        scratch_shapes=[pltpu.VMEM((bm, bn), F32)],
        compiler_params=_params(("arbitrary", "arbitrary", "arbitrary")),
    )(a, b)


def _weight_grad_scatter(at, b, lru_parts, packed, w_ada, dsilu_cctx, tk, name):
    k, m = at.shape
    n = b.shape[1]
    bn = n // NDEV
    nk = k // tk
    rl = lru_parts.shape[1]
    rp = packed.shape[0]
    d, cols = w_ada.shape
    assert cols % 128 == 0
    cb = cols // 128
    where = jnp.stack([_idx(_my_pos()), lax.axis_index("c")]).astype(jnp.int32)
    tn = (((0,), (0,)), ((), ()))

    def body(w_ref, a_ref, b_ref, l_ref, p_ref, wa_ref, ds_ref, recv_ref, lru_ref, psum_ref, pall_ref, cctx_ref,
             acc, sbuf, sib, lbuf, lsum, lall, pall, psum, wada, cpart, call, sib_send, sib_recv, chip_send,
             chip_recv, keep_sem, l_send, l_recv, g_send, g_recv, p_send, p_recv, wa_sem, c_send, c_recv):
        s, kk = pl.program_id(0), pl.program_id(1)
        x, y, c = _my_pos()
        scattered = _scatter_copies(l_ref, lbuf, l_send, l_recv)
        gathered, arrivals = _gather_copies(lsum, lall, g_send, g_recv)
        packed_sends, packed_arrivals = _gather_copies(p_ref, pall, p_send, p_recv)
        fetch_w_ada = pltpu.make_async_copy(wa_ref, wada, wa_sem)

        @pl.when((s == 0) & (kk == 0))
        def _():
            for cp in scattered + packed_sends:
                cp.start()
            fetch_w_ada.start(priority=1)

        @pl.when((s == NDEV // 2) & (kk == 0))
        def _():
            _exchange_wait(scattered, scattered[1:])
            red = lbuf[0]
            for j in range(1, NDEV):
                red = red + lbuf[j]
            lsum[...] = red
            for cp in gathered:
                cp.start()
            _exchange_wait(packed_sends, packed_arrivals)
            total = pall[0]
            for j in range(1, NDEV):
                total = total + pall[j]
            psum[...] = total
            fetch_w_ada.wait()
            me = _idx(_my_pos())
            part = jnp.zeros((8, d), F32)
            for q in range(cb):
                dm = jnp.broadcast_to(psum[pl.ds((NDEV + me) * cb + q, 1), :], (8, 128))
                part = part + lax.dot_general(dm, wada[:, q * 128:(q + 1) * 128],
                                              (((1,), (1,)), ((), ())), precision=HIGHEST,
                                              preferred_element_type=F32)
            cpart[...] = part
            _exchange_start(cpart, call, c_send, c_recv, 0)

        @pl.when(kk == 0)
        def _():
            acc[...] = lax.dot_general(a_ref[...], b_ref[...], tn, preferred_element_type=F32)

        @pl.when(kk > 0)
        def _():
            acc[...] += lax.dot_general(a_ref[...], b_ref[...], tn, preferred_element_type=F32)

        def to_sibling(j):
            return pltpu.make_async_remote_copy(
                src_ref=sbuf.at[0], dst_ref=sib.at[j], send_sem=sib_send.at[j], recv_sem=sib_recv.at[j],
                device_id=(x, y, 1 - c), device_id_type=MESH)

        def to_chip(j):
            dist = _chip_order(j, c)
            return pltpu.make_async_remote_copy(
                src_ref=sbuf.at[1], dst_ref=recv_ref.at[dist // 2], send_sem=chip_send.at[j],
                recv_sem=chip_recv.at[dist // 2], device_id=_peer_at(dist), device_id_type=MESH)

        keep = pltpu.make_async_copy(sbuf.at[1], recv_ref.at[0], keep_sem)
        sends = []
        for j in range(4):
            sends += [to_sibling(j), to_chip(j) if j < 3 else keep]

        for st in range(NDEV):
            @pl.when((kk == nk - 1) & (s == st))
            def _(st=st):
                if st >= 2:
                    sends[st - 2].wait_send()
                part = acc[...]
                if st % 2 == 1:
                    to_sibling(st // 2).wait_recv()
                    part = part + sib[st // 2].astype(F32)
                sbuf[st % 2] = part.astype(BF16)
                sends[st].start()
                if st == NDEV - 1:
                    sends[st - 1].wait_send()
                    sends[st].wait()
                    for j in range(1, 4):
                        pltpu.make_async_remote_copy(
                            src_ref=sbuf.at[0], dst_ref=recv_ref.at[j], send_sem=chip_send.at[0],
                            recv_sem=chip_recv.at[j], device_id=_peer_at(2 * j), device_id_type=MESH).wait_recv()
                    _exchange_wait(gathered, arrivals)
                    lru_ref[...] = lall[...]
                    psum_ref[...] = psum[...]
                    pall_ref[...] = pall[...]
                    _exchange_finish(_exchange_pushes(cpart, call, c_send, c_recv, 0))
                    tot = call[0]
                    for j in range(1, NDEV):
                        tot = tot + call[j]
                    cctx_ref[...] = tot * ds_ref[...]

    blk = lambda s, w_ref: w_ref[0] ^ _scatter_order(s, w_ref[1])
    return _call(
        body, name=name,
        grid_spec=pltpu.PrefetchScalarGridSpec(
            num_scalar_prefetch=1, grid=(NDEV, nk),
            in_specs=[pl.BlockSpec((tk, m), lambda s, kk, w_ref: (kk, 0)),
                      pl.BlockSpec((tk, bn), lambda s, kk, w_ref: (kk, blk(s, w_ref))), ANY, ANY, ANY,
                      pl.BlockSpec((8, d), lambda s, kk, w_ref: (0, 0))],
            out_specs=(ANY, pl.BlockSpec((NDEV, rl, 128), lambda s, kk, w_ref: (0, 0, 0)),
                       pl.BlockSpec((rp, 128), lambda s, kk, w_ref: (0, 0)),
                       pl.BlockSpec((NDEV, rp, 128), lambda s, kk, w_ref: (0, 0, 0)),
                       pl.BlockSpec((8, d), lambda s, kk, w_ref: (0, 0))),
            scratch_shapes=[pltpu.VMEM((m, bn), F32), pltpu.VMEM((2, m, bn), BF16), pltpu.VMEM((4, m, bn), BF16),
                            pltpu.VMEM((NDEV, rl, 128), F32), pltpu.VMEM((rl, 128), F32),
                            pltpu.VMEM((NDEV, rl, 128), F32), pltpu.VMEM((NDEV, rp, 128), F32),
                            pltpu.VMEM((rp, 128), F32), pltpu.VMEM((d, cols), F32),
                            pltpu.VMEM((8, d), F32), pltpu.VMEM((NDEV, 8, d), F32),
                            pltpu.SemaphoreType.DMA((4,)), pltpu.SemaphoreType.DMA((4,)),
                            pltpu.SemaphoreType.DMA((4,)), pltpu.SemaphoreType.DMA((4,)),
                            pltpu.SemaphoreType.DMA,
                            pltpu.SemaphoreType.DMA((NDEV,)), pltpu.SemaphoreType.DMA((NDEV,)),
                            pltpu.SemaphoreType.DMA((NDEV,)), pltpu.SemaphoreType.DMA((NDEV,)),
                            pltpu.SemaphoreType.DMA((NDEV,)), pltpu.SemaphoreType.DMA((NDEV,)),
                            pltpu.SemaphoreType.DMA,
                            pltpu.SemaphoreType.DMA((NDEV - 1,)), pltpu.SemaphoreType.DMA((NDEV - 1,))]),
        out_shape=(jax.ShapeDtypeStruct((4, m, bn), BF16), jax.ShapeDtypeStruct((NDEV, rl, 128), F32),
                   jax.ShapeDtypeStruct((rp, 128), F32), jax.ShapeDtypeStruct((NDEV, rp, 128), F32),
                   jax.ShapeDtypeStruct((8, d), F32)),
        compiler_params=_params(("arbitrary", "arbitrary")),
    )(where, at, b, lru_parts, packed, w_ada, dsilu_cctx)


def _input_backward(dp, w_all, src, mv, row0, tm, nbk, name, dn=None, cols=None):
    rows, d = src.shape
    nb, _, bw = w_all.shape
    first, last = (0, nb * bw - 1) if cols is None else cols
    k0 = first // (nbk * bw)
    nk = last // (nbk * bw) - k0 + 1
    ni = rows // tm
    blk0 = row0 // tm
    latent = dn is not None

    def body(*refs):
        dp_ref, w_ref, x_ref, mv_ref = refs[:4]
        outs = refs[4 + latent:]
        part_ref, acc = outs[latent], outs[latent + 1]
        i, k = pl.program_id(0), pl.program_id(1)

        def product():
            step = _dot_nt(dp_ref[:, 0:bw], w_ref[0])
            for q in range(1, nbk):
                step = step + _dot_nt(dp_ref[:, q * bw:(q + 1) * bw], w_ref[q])
            return step

        def finish(slot):
            xf = x_ref[...]
            r = lax.rsqrt(jnp.mean(xf * xf, axis=-1, keepdims=True) + EPS)
            xn = xf * r
            dhl = acc[slot]
            gain, sc = mv_ref[0:1, :], mv_ref[1:2, :]
            dhx = jnp.sum(dhl * xn, axis=0, keepdims=True)
            part_ref[0:1, :] += jnp.sum(dhl, axis=0, keepdims=True)
            part_ref[1:2, :] += dhx * gain
            part_ref[2:3, :] += dhx * (1.0 + sc)
            if latent:
                dxn = dhl * (gain * (1.0 + sc))
                outs[0][...] = (refs[4][...].astype(F32)
                                + r * (dxn - xn * jnp.mean(dxn * xn, axis=-1, keepdims=True)))

        @pl.when((i == 0) & (k == 0))
        def _():
            part_ref[...] = jnp.zeros_like(part_ref)
            acc[0] = product()

        @pl.when((i > 0) & (i < ni) & (k == 0))
        def _():
            acc[i % 2] = product()
            finish((i - 1) % 2)

        @pl.when((i == ni) & (k == 0))
        def _():
            finish((ni - 1) % 2)

        @pl.when((i < ni) & (k > 0))
        def _():
            acc[i % 2] += product()

    tile = pl.BlockSpec((tm, d), lambda i, k: (jnp.maximum(i - 1, 0), 0))
    vec = pl.BlockSpec((8, d), lambda i, k: (0, 0))
    kblock = lambda i, k: k0 + jnp.where(i == ni, nk - 1, k)
    return _call(
        body, name=name,
        grid=(ni + 1, nk),
        out_shape=((jax.ShapeDtypeStruct((rows, d), F32),) if latent else ()) + (jax.ShapeDtypeStruct((8, d), F32),),
        in_specs=[pl.BlockSpec((tm, nbk * bw), lambda i, k: (blk0 + jnp.minimum(i, ni - 1), kblock(i, k))),
                  pl.BlockSpec((nbk, d, bw), lambda i, k: (kblock(i, k), 0, 0)), tile, vec]
                 + ([tile] if latent else []),
        out_specs=((tile,) if latent else ()) + (vec,),
        scratch_shapes=[pltpu.VMEM((2, tm, d), F32)],
        compiler_params=_params(("arbitrary", "arbitrary")),
    )(*([dp, w_all, src, mv] + ([dn] if latent else [])))


def _adamw_scattered(parts, w, m, v, tr):
    r, c = w.shape
    nslot = parts.shape[0]

    def body(p_ref, w_ref, m_ref, v_ref, g_ref, d_ref, m2_ref, v2_ref):
        g = p_ref[0].astype(F32)
        for k in range(1, nslot):
            g = g + p_ref[k].astype(F32)
        g_ref[...] = g
        d_ref[...], m2_ref[...], v2_ref[...] = _adamw(w_ref[...], g, m_ref[...], v_ref[...])

    tile = pl.BlockSpec((tr, c), lambda i: (i, 0))
    return _call(
        body, name="adamw_scattered_%dx%d" % (r, c),
        grid=(r // tr,),
        out_shape=tuple(jax.ShapeDtypeStruct((r, c), F32) for _ in range(4)),
        in_specs=[pl.BlockSpec((nslot, tr, c), lambda i: (0, i, 0)), tile, tile, tile],
        out_specs=(tile,) * 4,
        compiler_params=_params(("arbitrary",)),
    )(parts, w, m, v)


def _adamw_ada(st, dmod, w, m, v, tr):
    r, c = w.shape

    def body(s_ref, dm_ref, w_ref, m_ref, v_ref, g_ref, d_ref, m2_ref, v2_ref):
        g = jnp.dot(s_ref[...], dm_ref[...], precision=HIGHEST, preferred_element_type=F32)
        g_ref[...] = g
        d_ref[...], m2_ref[...], v2_ref[...] = _adamw(w_ref[...], g, m_ref[...], v_ref[...])

    tile = pl.BlockSpec((tr, c), lambda i: (i, 0))
    return _call(
        body, name="adamw_ada",
        grid=(r // tr,),
        out_shape=tuple(jax.ShapeDtypeStruct((r, c), F32) for _ in range(4)),
        in_specs=[pl.BlockSpec((tr, 16), lambda i: (i, 0)), pl.BlockSpec((16, c), lambda i: (0, 0)),
                  tile, tile, tile],
        out_specs=(tile,) * 4,
        compiler_params=_params(("arbitrary",)),
    )(st, dmod, w, m, v)


def _adamw_small(gs, ws, ms, vs):
    n = len(ws)

    def body(*refs):
        for j in range(n):
            g_ref, w_ref, m_ref, v_ref = refs[j], refs[n + j], refs[2 * n + j], refs[3 * n + j]
            d_ref, m2_ref, v2_ref = refs[4 * n + j], refs[5 * n + j], refs[6 * n + j]
            d_ref[...], m2_ref[...], v2_ref[...] = _adamw(w_ref[...], g_ref[...], m_ref[...], v_ref[...])

    shapes = tuple(jax.ShapeDtypeStruct(a.shape, F32) for a in ws)
    out = _call(
        body, name="adamw_small",
        out_shape=shapes * 3,
        in_specs=[VMEM] * (4 * n), out_specs=(VMEM,) * (3 * n),
        compiler_params=_params(),
    )(*gs, *ws, *ms, *vs)
    return list(out[:n]), list(out[n:2 * n]), list(out[2 * n:])


def _blockdiag_groups(wh, gc):
    h, dh, _ = wh.shape
    g = gc // dh
    w4 = wh.reshape(h // g, g, dh, dh)
    bd = jnp.einsum("ngij,gh->ngihj", w4, jnp.eye(g, dtype=wh.dtype))
    return bd.reshape(h // g, gc, gc)


def _blockdiag_extract(bd, dh):
    ng, gc, _ = bd.shape
    g = gc // dh
    x = bd.reshape(ng, g, dh, g, dh)
    return jnp.einsum("ngihj,gh->ngij", x, jnp.eye(g, dtype=bd.dtype)).reshape(ng * g, dh, dh)


def _largest_tile(n, cap):
    return max(q for q in range(128, min(n, cap) + 1, 128) if n % q == 0)


def _rows8(*vecs):
    rows = [jnp.reshape(v, (1, -1)).astype(F32) for v in vecs]
    n = rows[0].shape[1]
    return jnp.concatenate(rows + [jnp.zeros((8 - len(rows), n), F32)], axis=0)


def _pack(pieces):
    flat = jnp.concatenate([jnp.reshape(a, (-1,)).astype(F32) for a in pieces])
    total = -(-flat.shape[0] // 1024) * 1024
    return jnp.pad(flat, (0, total - flat.shape[0])).reshape(total // 128, 128)


def _unpack(packed, shapes):
    flat = packed.reshape(-1)
    out, off = [], 0
    for s in shapes:
        n = 1
        for q in s:
            n *= q
        out.append(flat[off:off + n].reshape(s))
        off += n
    return out


def kernel(x, c, ctx, c_ctx, norm_g, w_ada, b_ada, w_in, w_conv_a, w_conv_b, b_conv_b, lru_wa, lru_ba, lru_wx, lru_bx, lru_lambda, w_out, final_g, loss_target, m_c_ctx, m_norm_g, m_w_ada, m_b_ada, m_w_in, m_w_conv_a, m_w_conv_b, m_b_conv_b, m_lru_wa, m_lru_ba, m_lru_wx, m_lru_bx, m_lru_lambda, m_w_out, m_final_g, v_c_ctx, v_norm_g, v_w_ada, v_b_ada, v_w_in, v_w_conv_a, v_w_conv_b, v_b_conv_b, v_lru_wa, v_lru_ba, v_lru_wx, v_lru_bx, v_lru_lambda, v_w_out, v_final_g):
    _, l, d = x.shape
    lc = ctx.shape[1]
    w = d // 2
    t = lc
    assert l % t == 0 and t % GRID_W == 0 and t % 128 == 0
    dh = w // N_HEADS
    gc = min(w, MXU_WIDTH)
    cols = w_ada.shape[2]
    wo_rows = w_out.shape[1]
    me = _idx(_my_pos())
    x2, ctx2, tgt2 = x[0], ctx[0], loss_target[0]
    w_ada2, w_in2, w_out2 = w_ada[0], w_in[0], w_out[0]

    small_mine = jnp.concatenate([a.reshape(-1) for a in (w_conv_a, w_conv_b, lru_ba, lru_bx, lru_lambda)]
                                 + [jnp.zeros((3 * (w // NDEV),), F32)]).reshape(16, w // NDEV)
    mod_all, s_mat, small_all = _mod_forward(
        jnp.broadcast_to(c, (8, d)), jnp.broadcast_to(c_ctx[None], (8, d)), w_ada2, small_mine)
    mod = jnp.transpose(mod_all, (1, 0, 2)).reshape(16, NDEV * cols) + b_ada
    mod_lat = lax.dynamic_slice_in_dim(mod, me, 1, axis=0)
    sh_l, sc_l, gt_l = jnp.split(mod_lat, 3, axis=-1)
    sh_c, sc_c, _ = jnp.split(mod[8:9], 3, axis=-1)
    small = jnp.transpose(small_all, (1, 0, 2)).reshape(16, w)
    wca = _rows8(*[small[j] for j in range(0, 3)])
    wcb = _rows8(*[small[j] for j in range(3, 7)], b_conv_b)
    lv = _rows8(0.5 * small[7], 0.5 * small[9], small[11], 0.5 * small[8], 0.5 * small[10], small[12])
    wg = jnp.stack([
        jnp.concatenate([_blockdiag_groups(lru_wa[0, dr], gc), _blockdiag_groups(lru_wx[0, dr], gc)], axis=-1)
        for dr in range(2)])
    wg = (0.5 * wg).astype(BF16)

    la = l + lc
    tm = 2 * t if l % (2 * t) == 0 else t
    tk = 3 * t if la % (3 * t) == 0 else t
    h = _normalize(x2, _rows8(norm_g, sc_l, sh_l), la, 0, tm, "normalize")
    h = _normalize(ctx2, _rows8(norm_g, sc_c, sh_c), la, l, t, "normalize_ctx", prev=h)
    p, w_all, wo_all = _in_projection(h, w_in2.astype(BF16), w_out2.astype(BF16), la // 4 if la % 64 == 0 else tk)
    taps_m, back_m, perm = _scan_matrices(t)
    xb = _conv_input(p, wcb, taps_m, l, t)
    hf, hr = _lru_forward(xb, wg, lv, l, t)
    wo = wo_all.reshape(d, d)
    dn, cat, dout, part_mix = _mix_forward(x2, tgt2, p, hf, hr, wo, _rows8(gt_l, final_g), wca, perm, t)
    g_wout = _weight_grad_t(cat, dout, 2, 1, _largest_tile(l, 2048), "grad_w_out")
    dp, dhs, part_ca = _mix_backward(dout, p, hf, hr, wo, wca, perm, l, t)
    dxb0, dwg0, part_l0 = _lru_backward(0, xb, dhs, hf, wg, lv, l, t)
    dp, dwg1, part_l1, sc_wout = _lru_backward(
        1, xb, dhs, hr, wg, lv, l, t, conv=(p, wcb, back_m, dxb0, dp, g_wout.reshape(NDEV, wo_rows, d)))
    dwa = jnp.stack([_blockdiag_extract(dwg0[:, :, :gc], dh), _blockdiag_extract(dwg1[:, :, :gc], dh)])
    dwx = jnp.stack([_blockdiag_extract(dwg0[:, :, gc:], dh), _blockdiag_extract(dwg1[:, :, gc:], dh)])
    lru_part = (0.5 * jnp.stack([dwa, dwx])).reshape(NDEV, -1, 128)
    grad_x, part_lat = _input_backward(dp, w_all, x2, _rows8(norm_g, sc_l), 0, tm, 2, "input_backward", dn=dn)
    (part_ctx,) = _input_backward(dp, w_all, ctx2, _rows8(norm_g, sc_c), l, t, 2, "input_backward_ctx",
                                  cols=(4 * w, 5 * w - 1))
    part_in = jnp.concatenate([part_lat[0:2], part_ctx[0:2], (part_lat[2] + part_ctx[2])[None]], axis=0)

    zeros_d = jnp.zeros((d,), F32)
    pieces = [
        jnp.concatenate([part_in[0], part_in[1], part_mix[1]]),
        jnp.concatenate([part_in[2], part_in[3], zeros_d]),
        part_in[4], part_mix[0], part_ca[0:3], part_l1[4:8], part_l1[3],
        0.5 * jnp.stack([part_l0[0], part_l1[0]]), 0.5 * jnp.stack([part_l0[1], part_l1[1]]),
        jnp.stack([part_l0[2], part_l1[2]]), part_mix[2, 0:1],
    ]
    shapes = [(3 * d,), (3 * d,), (d,), (d,), (3, w), (4, w), (w,), (2, w), (2, w), (2, w), (1,)]
    sig_cc = jax.nn.sigmoid(c_ctx)
    dsilu_cc = jnp.broadcast_to((sig_cc * (1.0 + c_ctx * (1.0 - sig_cc)))[None], (8, d))
    sc_win, lru_sum, psum, pall, g_cctx8 = _weight_grad_scatter(
        h, dp, lru_part, _pack(pieces), w_ada2, dsilu_cc, tk, "grad_w_in")
    (g_modl, g_modc, g_norm, g_final, g_ca, g_cb, g_bcb, g_ba, g_bx, g_lam, loss1) = _unpack(psum, shapes)
    loss = loss1[0]
    g_cctx = g_cctx8[0]
    g_bada = (g_modl + g_modc)[None]
    g_lru = lru_sum.reshape(2, 2, N_HEADS, dh, dh)
    g_wa, g_wx = g_lru[0][None], g_lru[1][None]
    wsl = w // NDEV
    mine = lambda a: lax.dynamic_slice_in_dim(a, me * wsl, wsl, axis=-1)
    g_ca_m, g_cb_m, g_ba_m, g_bx_m, g_lam_m = (mine(g_ca)[None], mine(g_cb)[None], mine(g_ba)[None],
                                               mine(g_bx)[None], mine(g_lam)[None])
    g_norm, g_bcb = g_norm[None], g_bcb[None]

    per_dev = pall[:, :3 * d // 128].reshape(NDEV, NDEV, cols)
    dmod_lat = lax.dynamic_slice_in_dim(per_dev, me, 1, axis=1)[:, 0]
    dmod_ctx = lax.dynamic_slice_in_dim(g_modc.reshape(NDEV, cols), me, 1, axis=0)
    dmod16 = jnp.concatenate([dmod_lat, dmod_ctx, jnp.zeros((7, cols), F32)], axis=0)
    tr_ada = 256 if d % 256 == 0 else d
    g_wada, d_wada, m_wada, v_wada = _adamw_ada(s_mat.T, dmod16, w_ada2, m_w_ada[0], v_w_ada[0], tr_ada)
    g_win2, d_win, m_win, v_win = _adamw_scattered(sc_win, w_in2, m_w_in[0], v_w_in[0], tr_ada)
    tr_out = 64 if wo_rows % 64 == 0 else wo_rows
    g_wout2, d_wout, m_wout, v_wout = _adamw_scattered(sc_wout, w_out2, m_w_out[0], v_w_out[0], tr_out)

    small_w = [c_ctx, norm_g, b_ada, w_conv_a, w_conv_b, b_conv_b, lru_wa, lru_ba, lru_wx, lru_bx, lru_lambda, final_g]
    small_m = [m_c_ctx, m_norm_g, m_b_ada, m_w_conv_a, m_w_conv_b, m_b_conv_b, m_lru_wa, m_lru_ba, m_lru_wx,
               m_lru_bx, m_lru_lambda, m_final_g]
    small_v = [v_c_ctx, v_norm_g, v_b_ada, v_w_conv_a, v_w_conv_b, v_b_conv_b, v_lru_wa, v_lru_ba, v_lru_wx,
               v_lru_bx, v_lru_lambda, v_final_g]
    small_g = [g_cctx, g_norm, g_bada, g_ca_m, g_cb_m, g_bcb, g_wa, g_ba_m, g_wx, g_bx_m, g_lam_m, g_final]
    small_g = [jnp.reshape(a, b.shape) for a, b in zip(small_g, small_w)]
    d_s, m_s, v_s = _adamw_small(small_g, small_w, small_m, small_v)

    def weights(small_list, ada, win, wout):
        (cctx_, norm_, bada_, ca_, cb_, bcb_, wa_, ba_, wx_, bx_, lam_, final_) = small_list
        return [cctx_, norm_, ada[None], bada_, win[None], ca_, cb_, bcb_, wa_, ba_, wx_, bx_, lam_, wout[None], final_]

    return (loss, grad_x[None],
            *weights(small_g, g_wada, g_win2, g_wout2), *weights(d_s, d_wada, d_win, d_wout),
            *weights(m_s, m_wada, m_win, m_wout), *weights(v_s, v_wada, v_win, v_wout))
```
